```python
import jax, jax.numpy as jnp
from jax import lax
import numpy as np

D_MODEL = 1024
BATCH = 16
SEQ = 4096
DEPTH = 1

CHUNK = 64
SUB_CHUNK = 16
Q_BLOCK = 128
A_WIDTH = D_MODEL // 2
A_HEAD_DIM = 128
A_HEADS = A_WIDTH // A_HEAD_DIM
B_WIDTH = D_MODEL // 2
B_HEAD_DIM = 64
B_HEADS = B_WIDTH // B_HEAD_DIM
N_BRANCH = 2
D_FF = -(-(8 * D_MODEL) // (3 * 256)) * 256
N_IN = 4 * A_WIDTH + 3 * B_WIDTH + B_HEADS + N_BRANCH * D_MODEL
ALPHA = (2.0 * DEPTH) ** 0.25
BETA = (8.0 * DEPTH) ** -0.25
LN_EPS = 1e-5
RMS_EPS = 1e-6
N_MOD = 6

kernel_name = "hybrid_hgrn2_fox_deepnorm_adaln_block"


def layer_norm(x, w, b):
    xf = x.astype(jnp.float32)
    mu = jnp.mean(xf, axis=-1, keepdims=True)
    var = jnp.mean(jnp.square(xf - mu), axis=-1, keepdims=True)
    return ((xf - mu) * lax.rsqrt(var + LN_EPS) * w + b).astype(x.dtype)


def gated_linear_recurrence(q, k, v, logf):
    bsz, seq, heads, dk = q.shape
    dv = v.shape[-1]
    n_chunks = seq // CHUNK
    ns = CHUNK // SUB_CHUNK

    def to_chunks(t):
        return t.reshape(bsz, n_chunks, CHUNK, heads, t.shape[-1]).transpose(1, 0, 3, 2, 4)

    qc, kc, vc, lc = to_chunks(q), to_chunks(k), to_chunks(v), to_chunks(logf)
    bc = jnp.cumsum(lc, axis=3)
    tri = jnp.tril(jnp.ones((SUB_CHUNK, SUB_CHUNK), dtype=bool))
    later = jnp.tril(jnp.ones((ns, ns), dtype=bool), -1)
    eye = jnp.eye(ns, dtype=jnp.float32)

    def step(state, inp):
        qt, kt, vt, bt = inp
        o_inter = jnp.einsum('bhtk,bhkv->bhtv', qt * jnp.exp(bt), state)
        qs = qt.reshape(bsz, heads, ns, SUB_CHUNK, dk)
        ks = kt.reshape(bsz, heads, ns, SUB_CHUNK, dk)
        bs = bt.reshape(bsz, heads, ns, SUB_CHUNK, dk)
        diff = bs[:, :, :, :, None, :] - bs[:, :, :, None, :, :]
        diff = jnp.where(tri[:, :, None], diff, -jnp.inf)
        a_diag = jnp.sum(qs[:, :, :, :, None, :] * ks[:, :, :, None, :, :] * jnp.exp(diff), axis=-1)
        b_ref = bs[:, :, :, -1, :]
        eq = bs[:, :, :, None, :, :] - b_ref[:, :, None, :, None, :]
        eq = jnp.where(later[:, :, None, None], eq, -jnp.inf)
        qd = qs[:, :, :, None] * jnp.exp(eq)
        kd = ks * jnp.exp(b_ref[:, :, :, None, :] - bs)
        a_off = jnp.einsum('bhijtk,bhjsk->bhitjs', qd, kd)
        a = a_off + eye[:, None, :, None] * a_diag[:, :, :, :, None, :]
        a = a.reshape(bsz, heads, CHUNK, CHUNK)
        o = o_inter + jnp.einsum('bhts,bhsv->bhtv', a, vt)
        b_last = bt[:, :, -1, :]
        new_state = (jnp.exp(b_last)[..., None] * state
                     + jnp.einsum('bhsk,bhsv->bhkv', kt * jnp.exp(b_last[:, :, None, :] - bt), vt))
        return new_state, o

    state0 = jnp.zeros((bsz, heads, dk, dv), jnp.float32)
    _, oc = lax.scan(step, state0, (qc, kc, vc, bc))
    return oc.transpose(1, 0, 3, 2, 4).reshape(bsz, seq, heads, dv)


def hgrn2_mixer(q, f_logit, i_in, g, lb, norm_w):
    bsz, seq, _ = q.shape
    dt = q.dtype

    def split(t):
        return t.reshape(bsz, seq, A_HEADS, A_HEAD_DIM).astype(jnp.float32)

    lbh = lb.astype(jnp.float32).reshape(A_HEADS, A_HEAD_DIM)
    f = lbh + (1.0 - lbh) * jax.nn.sigmoid(split(f_logit))
    o = gated_linear_recurrence(split(q), 1.0 - f, split(i_in), jnp.log(f))
    o = o * lax.rsqrt(jnp.mean(jnp.square(o), axis=-1, keepdims=True) + RMS_EPS)
    o = o * norm_w.astype(jnp.float32).reshape(A_HEADS, A_HEAD_DIM) * jax.nn.sigmoid(split(g))
    return o.reshape(bsz, seq, A_WIDTH).astype(dt)


def forgetting_attention(q, k, v, f_logit, f_bias):
    bsz, seq, _ = q.shape
    dt = q.dtype

    def heads(t):
        return t.reshape(bsz, seq, B_HEADS, B_HEAD_DIM).transpose(0, 2, 1, 3)

    qh, kh, vh = heads(q), heads(k), heads(v)
    logf = jax.nn.log_sigmoid((f_logit + f_bias).astype(jnp.float32))
    cum = jnp.cumsum(logf, axis=1).transpose(0, 2, 1)
    scale = B_HEAD_DIM ** -0.5
    outs = []
    for blk in range(seq // Q_BLOCK):
        lo, hi = blk * Q_BLOCK, (blk + 1) * Q_BLOCK
        s = jnp.einsum('bhqd,bhkd->bhqk', qh[:, :, lo:hi], kh[:, :, :hi]).astype(jnp.float32) * scale
        s = s + cum[:, :, lo:hi, None] - cum[:, :, None, :hi]
        causal = (lo + jnp.arange(Q_BLOCK))[:, None] >= jnp.arange(hi)[None, :]
        p = jax.nn.softmax(jnp.where(causal, s, -jnp.inf), axis=-1)
        outs.append(jnp.einsum('bhqk,bhkd->bhqd', p.astype(dt), vh[:, :, :hi]))
    o = jnp.concatenate(outs, axis=2)
    return o.transpose(0, 2, 1, 3).reshape(bsz, seq, B_WIDTH)


def _fwd_setup_inputs(seed: int = 0) -> dict:
    key = jax.random.key(seed)
    ks = jax.random.split(key, 20)
    L, D = DEPTH, D_MODEL
    nrm = lambda k, shape, s: jax.random.normal(k, shape, jnp.float32) * s
    col_scale = np.ones((N_IN,), np.float32)
    col_scale[2 * A_WIDTH:3 * A_WIDTH] = BETA
    col_scale[4 * A_WIDTH + 2 * B_WIDTH:4 * A_WIDTH + 3 * B_WIDTH] = BETA
    return {
        "x": nrm(ks[0], (BATCH, SEQ, D), 1.0),
        "c": nrm(ks[1], (BATCH, D), 1.0),
        "w_ada": nrm(ks[2], (L, D, N_MOD * D), 0.1 * D ** -0.5),
        "b_ada": nrm(ks[3], (L, N_MOD * D), 0.01),
        "w_in": nrm(ks[4], (L, D, N_IN), D ** -0.5) * jnp.asarray(col_scale),
        "fox_f_bias": nrm(ks[5], (L, B_HEADS), 0.1) + 2.0,
        "lb_logits": nrm(ks[6], (L + 1, A_WIDTH), 0.1),
        "hgrn_norm_w": 1.0 + nrm(ks[7], (L, A_WIDTH), 0.02),
        "w_branch_a": nrm(ks[8], (L, A_WIDTH, D), BETA * A_WIDTH ** -0.5),
        "w_branch_b": nrm(ks[9], (L, B_WIDTH, D), BETA * B_WIDTH ** -0.5),
        "w_out": nrm(ks[10], (L, D, D), BETA * D ** -0.5),
        "ln1_w": 1.0 + nrm(ks[11], (L, D), 0.02),
        "ln1_b": nrm(ks[12], (L, D), 0.01),
        "w_ffn_gate": nrm(ks[13], (L, D, D_FF), BETA * D ** -0.5),
        "w_ffn_up": nrm(ks[14], (L, D, D_FF), BETA * D ** -0.5),
        "w_ffn_down": nrm(ks[15], (L, D_FF, D), BETA * D_FF ** -0.5),
        "ln2_w": 1.0 + nrm(ks[16], (L, D), 0.02),
        "ln2_b": nrm(ks[17], (L, D), 0.01),
    }


def _fwd_reference(x, c, w_ada, b_ada, w_in, fox_f_bias, lb_logits, hgrn_norm_w, w_branch_a, w_branch_b,
              w_out, ln1_w, ln1_b, w_ffn_gate, w_ffn_up, w_ffn_down, ln2_w, ln2_b):
    splits = list(np.cumsum([A_WIDTH] * 4 + [B_WIDTH] * 3 + [B_HEADS])[:])
    lower_bounds = jnp.cumsum(jax.nn.softmax(lb_logits.astype(jnp.float32), axis=0), axis=0)
    c_act = jax.nn.silu(c)
    for l in range(DEPTH):
        mod = (c_act @ w_ada[l] + b_ada[l])[:, None, :]
        sh1, sc1, g1, sh2, sc2, g2 = jnp.split(mod, N_MOD, axis=-1)
        h = x * (1.0 + sc1) + sh1
        proj = h @ w_in[l]
        aq, af, ai, ag, bq, bk, bv, bf, gates = jnp.split(proj, splits, axis=-1)
        ya = hgrn2_mixer(aq, af, ai, ag, lower_bounds[l], hgrn_norm_w[l])
        yb = forgetting_attention(bq, bk, bv, bf, fox_f_bias[l])
        gate_a, gate_b = jnp.split(jax.nn.sigmoid(gates), N_BRANCH, axis=-1)
        merged = gate_a * (ya @ w_branch_a[l]) + gate_b * (yb @ w_branch_b[l])
        x = layer_norm(ALPHA * x + (1.0 + g1) * (merged @ w_out[l]), ln1_w[l], ln1_b[l])
        h = x * (1.0 + sc2) + sh2
        ffn = (jax.nn.silu(h @ w_ffn_gate[l]) * (h @ w_ffn_up[l])) @ w_ffn_down[l]
        x = layer_norm(ALPHA * x + (1.0 + g2) * ffn, ln2_w[l], ln2_b[l])
    return x


import jax as _jax
import jax.numpy as _jnp

TWIN_FORMAT = 'train_step'
FWD_PARAMS = ['x', 'c', 'w_ada', 'b_ada', 'w_in', 'fox_f_bias', 'lb_logits', 'hgrn_norm_w', 'w_branch_a', 'w_branch_b', 'w_out', 'ln1_w', 'ln1_b', 'w_ffn_gate', 'w_ffn_up', 'w_ffn_down', 'ln2_w', 'ln2_b']
TWIN_WEIGHTS = ['w_ada', 'b_ada', 'w_in', 'fox_f_bias', 'lb_logits', 'hgrn_norm_w', 'w_branch_a', 'w_branch_b', 'w_out', 'ln1_w', 'ln1_b', 'w_ffn_gate', 'w_ffn_up', 'w_ffn_down', 'ln2_w', 'ln2_b']
TWIN_DIFF_INPUT = 'x'
TWIN_INPUTS = ['x', 'c', 'w_ada', 'b_ada', 'w_in', 'fox_f_bias', 'lb_logits', 'hgrn_norm_w', 'w_branch_a', 'w_branch_b', 'w_out', 'ln1_w', 'ln1_b', 'w_ffn_gate', 'w_ffn_up', 'w_ffn_down', 'ln2_w', 'ln2_b', 'loss_target', 'm_w_ada', 'm_b_ada', 'm_w_in', 'm_fox_f_bias', 'm_lb_logits', 'm_hgrn_norm_w', 'm_w_branch_a', 'm_w_branch_b', 'm_w_out', 'm_ln1_w', 'm_ln1_b', 'm_w_ffn_gate', 'm_w_ffn_up', 'm_w_ffn_down', 'm_ln2_w', 'm_ln2_b', 'v_w_ada', 'v_b_ada', 'v_w_in', 'v_fox_f_bias', 'v_lb_logits', 'v_hgrn_norm_w', 'v_w_branch_a', 'v_w_branch_b', 'v_w_out', 'v_ln1_w', 'v_ln1_b', 'v_w_ffn_gate', 'v_w_ffn_up', 'v_w_ffn_down', 'v_ln2_w', 'v_ln2_b']
TWIN_OUTPUTS = ['loss', 'grad_x', 'grad_w_ada', 'grad_b_ada', 'grad_w_in', 'grad_fox_f_bias', 'grad_lb_logits', 'grad_hgrn_norm_w', 'grad_w_branch_a', 'grad_w_branch_b', 'grad_w_out', 'grad_ln1_w', 'grad_ln1_b', 'grad_w_ffn_gate', 'grad_w_ffn_up', 'grad_w_ffn_down', 'grad_ln2_w', 'grad_ln2_b', 'delta_w_ada', 'delta_b_ada', 'delta_w_in', 'delta_fox_f_bias', 'delta_lb_logits', 'delta_hgrn_norm_w', 'delta_w_branch_a', 'delta_w_branch_b', 'delta_w_out', 'delta_ln1_w', 'delta_ln1_b', 'delta_w_ffn_gate', 'delta_w_ffn_up', 'delta_w_ffn_down', 'delta_ln2_w', 'delta_ln2_b', 'new_m_w_ada', 'new_m_b_ada', 'new_m_w_in', 'new_m_fox_f_bias', 'new_m_lb_logits', 'new_m_hgrn_norm_w', 'new_m_w_branch_a', 'new_m_w_branch_b', 'new_m_w_out', 'new_m_ln1_w', 'new_m_ln1_b', 'new_m_w_ffn_gate', 'new_m_w_ffn_up', 'new_m_w_ffn_down', 'new_m_ln2_w', 'new_m_ln2_b', 'new_v_w_ada', 'new_v_b_ada', 'new_v_w_in', 'new_v_fox_f_bias', 'new_v_lb_logits', 'new_v_hgrn_norm_w', 'new_v_w_branch_a', 'new_v_w_branch_b', 'new_v_w_out', 'new_v_ln1_w', 'new_v_ln1_b', 'new_v_w_ffn_gate', 'new_v_w_ffn_up', 'new_v_w_ffn_down', 'new_v_ln2_w', 'new_v_ln2_b']
TWIN_LEAF_KINDS = {'loss': 'loss', 'grad_x': 'grad_x', 'grad_w_ada': 'grad_w', 'grad_b_ada': 'grad_w', 'grad_w_in': 'grad_w', 'grad_fox_f_bias': 'grad_w', 'grad_lb_logits': 'grad_w', 'grad_hgrn_norm_w': 'grad_w', 'grad_w_branch_a': 'grad_w', 'grad_w_branch_b': 'grad_w', 'grad_w_out': 'grad_w', 'grad_ln1_w': 'grad_w', 'grad_ln1_b': 'grad_w', 'grad_w_ffn_gate': 'grad_w', 'grad_w_ffn_up': 'grad_w', 'grad_w_ffn_down': 'grad_w', 'grad_ln2_w': 'grad_w', 'grad_ln2_b': 'grad_w', 'delta_w_ada': 'delta_w', 'delta_b_ada': 'delta_w', 'delta_w_in': 'delta_w', 'delta_fox_f_bias': 'delta_w', 'delta_lb_logits': 'delta_w', 'delta_hgrn_norm_w': 'delta_w', 'delta_w_branch_a': 'delta_w', 'delta_w_branch_b': 'delta_w', 'delta_w_out': 'delta_w', 'delta_ln1_w': 'delta_w', 'delta_ln1_b': 'delta_w', 'delta_w_ffn_gate': 'delta_w', 'delta_w_ffn_up': 'delta_w', 'delta_w_ffn_down': 'delta_w', 'delta_ln2_w': 'delta_w', 'delta_ln2_b': 'delta_w', 'new_m_w_ada': 'new_m', 'new_m_b_ada': 'new_m', 'new_m_w_in': 'new_m', 'new_m_fox_f_bias': 'new_m', 'new_m_lb_logits': 'new_m', 'new_m_hgrn_norm_w': 'new_m', 'new_m_w_branch_a': 'new_m', 'new_m_w_branch_b': 'new_m', 'new_m_w_out': 'new_m', 'new_m_ln1_w': 'new_m', 'new_m_ln1_b': 'new_m', 'new_m_w_ffn_gate': 'new_m', 'new_m_w_ffn_up': 'new_m', 'new_m_w_ffn_down': 'new_m', 'new_m_ln2_w': 'new_m', 'new_m_ln2_b': 'new_m', 'new_v_w_ada': 'new_v', 'new_v_b_ada': 'new_v', 'new_v_w_in': 'new_v', 'new_v_fox_f_bias': 'new_v', 'new_v_lb_logits': 'new_v', 'new_v_hgrn_norm_w': 'new_v', 'new_v_w_branch_a': 'new_v', 'new_v_w_branch_b': 'new_v', 'new_v_w_out': 'new_v', 'new_v_ln1_w': 'new_v', 'new_v_ln1_b': 'new_v', 'new_v_w_ffn_gate': 'new_v', 'new_v_w_ffn_up': 'new_v', 'new_v_w_ffn_down': 'new_v', 'new_v_ln2_w': 'new_v', 'new_v_ln2_b': 'new_v'}


def _forward(args):
    return _fwd_reference(*[args[k] for k in FWD_PARAMS])


def _output_shape():
    out = _jax.eval_shape(lambda: _forward(_fwd_setup_inputs(0)))
    return out.shape, out.dtype

N_MICROBATCH = 1
ADAM_LR = 0.001
ADAM_B1 = 0.9
ADAM_B2 = 0.999
ADAM_EPS = 1e-08
ADAM_WD = 0.01
ADAM_STEP = 10
PER_EXAMPLE_BATCH_AXIS = {'x': 0, 'c': 0, 'loss_target': 0}
SHARED_INPUTS = []
_WEIGHT_DTYPES = {'w_ada': _jnp.float32, 'b_ada': _jnp.float32, 'w_in': _jnp.float32, 'fox_f_bias': _jnp.float32, 'lb_logits': _jnp.float32, 'hgrn_norm_w': _jnp.float32, 'w_branch_a': _jnp.float32, 'w_branch_b': _jnp.float32, 'w_out': _jnp.float32, 'ln1_w': _jnp.float32, 'ln1_b': _jnp.float32, 'w_ffn_gate': _jnp.float32, 'w_ffn_up': _jnp.float32, 'w_ffn_down': _jnp.float32, 'ln2_w': _jnp.float32, 'ln2_b': _jnp.float32}
MOMENT_SCALE = {'w_ada': 2.501391e-02, 'b_ada': 4.111697e-02, 'w_in': 2.437875e-02, 'fox_f_bias': 4.663169e-02, 'lb_logits': 2.253113e-02, 'hgrn_norm_w': 3.632203e-02, 'w_branch_a': 3.694015e-02, 'w_branch_b': 1.686349e-02, 'w_out': 4.046985e-02, 'ln1_w': 2.030197e+00, 'ln1_b': 3.655270e-01, 'w_ffn_gate': 2.537581e-02, 'w_ffn_up': 2.479262e-02, 'w_ffn_down': 4.106259e-02, 'ln2_w': 6.400754e+01, 'ln2_b': 5.537704e-01}


def _to_microbatches(a, axis):
    t = _jnp.moveaxis(a, axis, 0)
    t = t.reshape((N_MICROBATCH, t.shape[0] // N_MICROBATCH) + t.shape[1:])
    return _jnp.moveaxis(t, 1, axis + 1)


def setup_inputs(seed: int = 0) -> dict:
    inp = _fwd_setup_inputs(seed)
    key = _jax.random.fold_in(_jax.random.key(seed), 7919)
    shape, _ = _output_shape()
    out = dict(inp)
    out["loss_target"] = _jax.random.normal(_jax.random.fold_in(key, 0), shape, _jnp.float32)
    for i, name in enumerate(TWIN_WEIGHTS):
        w = inp[name].astype(_jnp.float32)
        if MOMENT_SCALE is None:
            s = _jnp.sqrt(_jnp.mean(_jnp.square(w)) + 1e-30)
        else:
            s = MOMENT_SCALE[name]
        km, kv = _jax.random.split(_jax.random.fold_in(key, i + 1))
        out[name] = w
        out["m_" + name] = s * _jax.random.normal(km, w.shape, _jnp.float32)
        out["v_" + name] = (s * s) * _jax.random.uniform(kv, w.shape, _jnp.float32, 0.5, 1.5)
    if N_MICROBATCH > 1:
        for name, axis in PER_EXAMPLE_BATCH_AXIS.items():
            out[name] = _to_microbatches(out[name], axis)
    return {'x': out['x'], 'c': out['c'], 'w_ada': out['w_ada'], 'b_ada': out['b_ada'], 'w_in': out['w_in'], 'fox_f_bias': out['fox_f_bias'], 'lb_logits': out['lb_logits'], 'hgrn_norm_w': out['hgrn_norm_w'], 'w_branch_a': out['w_branch_a'], 'w_branch_b': out['w_branch_b'], 'w_out': out['w_out'], 'ln1_w': out['ln1_w'], 'ln1_b': out['ln1_b'], 'w_ffn_gate': out['w_ffn_gate'], 'w_ffn_up': out['w_ffn_up'], 'w_ffn_down': out['w_ffn_down'], 'ln2_w': out['ln2_w'], 'ln2_b': out['ln2_b'], 'loss_target': out['loss_target'], 'm_w_ada': out['m_w_ada'], 'm_b_ada': out['m_b_ada'], 'm_w_in': out['m_w_in'], 'm_fox_f_bias': out['m_fox_f_bias'], 'm_lb_logits': out['m_lb_logits'], 'm_hgrn_norm_w': out['m_hgrn_norm_w'], 'm_w_branch_a': out['m_w_branch_a'], 'm_w_branch_b': out['m_w_branch_b'], 'm_w_out': out['m_w_out'], 'm_ln1_w': out['m_ln1_w'], 'm_ln1_b': out['m_ln1_b'], 'm_w_ffn_gate': out['m_w_ffn_gate'], 'm_w_ffn_up': out['m_w_ffn_up'], 'm_w_ffn_down': out['m_w_ffn_down'], 'm_ln2_w': out['m_ln2_w'], 'm_ln2_b': out['m_ln2_b'], 'v_w_ada': out['v_w_ada'], 'v_b_ada': out['v_b_ada'], 'v_w_in': out['v_w_in'], 'v_fox_f_bias': out['v_fox_f_bias'], 'v_lb_logits': out['v_lb_logits'], 'v_hgrn_norm_w': out['v_hgrn_norm_w'], 'v_w_branch_a': out['v_w_branch_a'], 'v_w_branch_b': out['v_w_branch_b'], 'v_w_out': out['v_w_out'], 'v_ln1_w': out['v_ln1_w'], 'v_ln1_b': out['v_ln1_b'], 'v_w_ffn_gate': out['v_w_ffn_gate'], 'v_w_ffn_up': out['v_w_ffn_up'], 'v_w_ffn_down': out['v_w_ffn_down'], 'v_ln2_w': out['v_ln2_w'], 'v_ln2_b': out['v_ln2_b']}


def _loss(weights, diff, rest, loss_target):
    with _jax.named_scope("forward"):
        args = {**rest, TWIN_DIFF_INPUT: diff, **{k: w.astype(_WEIGHT_DTYPES[k]) for k, w in weights.items()}}
        y = _forward(args)
    with _jax.named_scope("loss_head"):
        err = _jnp.square(y.astype(_jnp.float32) - loss_target)
        return 0.5 * _jnp.sum(_jnp.mean(err, axis=-1)) if err.ndim else 0.5 * err


def _adamw(w, g, m, v):
    m = ADAM_B1 * m + (1.0 - ADAM_B1) * g
    v = ADAM_B2 * v + (1.0 - ADAM_B2) * _jnp.square(g)
    m_hat = m / (1.0 - ADAM_B1 ** ADAM_STEP)
    v_hat = v / (1.0 - ADAM_B2 ** ADAM_STEP)
    delta = -ADAM_LR * (m_hat / (_jnp.sqrt(v_hat) + ADAM_EPS) + ADAM_WD * w)
    return delta, m, v


def reference(x, c, w_ada, b_ada, w_in, fox_f_bias, lb_logits, hgrn_norm_w, w_branch_a, w_branch_b, w_out, ln1_w, ln1_b, w_ffn_gate, w_ffn_up, w_ffn_down, ln2_w, ln2_b, loss_target, m_w_ada, m_b_ada, m_w_in, m_fox_f_bias, m_lb_logits, m_hgrn_norm_w, m_w_branch_a, m_w_branch_b, m_w_out, m_ln1_w, m_ln1_b, m_w_ffn_gate, m_w_ffn_up, m_w_ffn_down, m_ln2_w, m_ln2_b, v_w_ada, v_b_ada, v_w_in, v_fox_f_bias, v_lb_logits, v_hgrn_norm_w, v_w_branch_a, v_w_branch_b, v_w_out, v_ln1_w, v_ln1_b, v_w_ffn_gate, v_w_ffn_up, v_w_ffn_down, v_ln2_w, v_ln2_b):
    given = dict(x=x, c=c, w_ada=w_ada, b_ada=b_ada, w_in=w_in, fox_f_bias=fox_f_bias, lb_logits=lb_logits, hgrn_norm_w=hgrn_norm_w, w_branch_a=w_branch_a, w_branch_b=w_branch_b, w_out=w_out, ln1_w=ln1_w, ln1_b=ln1_b, w_ffn_gate=w_ffn_gate, w_ffn_up=w_ffn_up, w_ffn_down=w_ffn_down, ln2_w=ln2_w, ln2_b=ln2_b, loss_target=loss_target, m_w_ada=m_w_ada, m_b_ada=m_b_ada, m_w_in=m_w_in, m_fox_f_bias=m_fox_f_bias, m_lb_logits=m_lb_logits, m_hgrn_norm_w=m_hgrn_norm_w, m_w_branch_a=m_w_branch_a, m_w_branch_b=m_w_branch_b, m_w_out=m_w_out, m_ln1_w=m_ln1_w, m_ln1_b=m_ln1_b, m_w_ffn_gate=m_w_ffn_gate, m_w_ffn_up=m_w_ffn_up, m_w_ffn_down=m_w_ffn_down, m_ln2_w=m_ln2_w, m_ln2_b=m_ln2_b, v_w_ada=v_w_ada, v_b_ada=v_b_ada, v_w_in=v_w_in, v_fox_f_bias=v_fox_f_bias, v_lb_logits=v_lb_logits, v_hgrn_norm_w=v_hgrn_norm_w, v_w_branch_a=v_w_branch_a, v_w_branch_b=v_w_branch_b, v_w_out=v_w_out, v_ln1_w=v_ln1_w, v_ln1_b=v_ln1_b, v_w_ffn_gate=v_w_ffn_gate, v_w_ffn_up=v_w_ffn_up, v_w_ffn_down=v_w_ffn_down, v_ln2_w=v_ln2_w, v_ln2_b=v_ln2_b)
    weights = {n: given[n] for n in TWIN_WEIGHTS}
    shared = {n: given[n] for n in SHARED_INPUTS}
    per_example = {n: given[n] for n in ['x', 'c']}
    grad_fn = _jax.value_and_grad(_loss, argnums=(0, 1))

    def one_microbatch(ex, loss_target):
        ex = dict(ex)
        diff = ex.pop(TWIN_DIFF_INPUT)
        return grad_fn(weights, diff, {**shared, **ex}, loss_target)

    if N_MICROBATCH == 1:
        loss, (grad_w, grad_x) = one_microbatch(per_example, given["loss_target"])
    else:
        def body(carry, xs):
            loss_sum, grad_sum = carry
            l_k, (gw_k, gx_k) = one_microbatch(xs[0], xs[1])
            with _jax.named_scope("update"):
                return (loss_sum + l_k, _jax.tree.map(_jnp.add, grad_sum, gw_k)), gx_k

        init = (_jnp.zeros((), _jnp.float32), _jax.tree.map(_jnp.zeros_like, weights))
        (loss, grad_w), grad_x = _jax.lax.scan(body, init, (per_example, given["loss_target"]))
    with _jax.named_scope("update"):
        delta_w, new_m, new_v = {}, {}, {}
        for n in TWIN_WEIGHTS:
            delta_w[n], new_m[n], new_v[n] = _adamw(weights[n], grad_w[n], given["m_" + n], given["v_" + n])
    return (loss, grad_x, *[grad_w[n] for n in TWIN_WEIGHTS], *[delta_w[n] for n in TWIN_WEIGHTS],
            *[new_m[n] for n in TWIN_WEIGHTS], *[new_v[n] for n in TWIN_WEIGHTS])
```

```python
import functools
import math

import jax
import jax.numpy as jnp
import numpy as np
from jax import lax
from jax.experimental import pallas as pl
from jax.experimental.pallas import tpu as pltpu

F32 = jnp.float32
BF16 = jnp.bfloat16
MESH = pl.DeviceIdType.MESH
HIGHEST = lax.Precision.HIGHEST

D = 1024
AW = 512
AH = 4
ADH = 128
BH = 8
BDH = 64
DFF = 2816
NIN = 5640
NP = 5760
N_CHIPS = 4
N_DEV = 8
SUB = 16
COL_GATES = 0
COL_BQ = 2048
COL_KV = 2560
COL_A = 3584
COL_BF = 5632
ALPHA = 2.0 ** 0.25
LN_EPS = 1e-5
RMS_EPS = 1e-6
NEG = -1e30
LR, B1, B2, EPS, WD, STEP = 0.001, 0.9, 0.999, 1e-08, 0.01, 10
PACK_ROWS = 4096
HALF = PACK_ROWS // 2
SMALL_W = 6144
O_LN1W, O_LN1B, O_LN2W, O_LN2B, O_NORM, O_LB0, O_LB1, O_FOX = 0, 1024, 2048, 3072, 4096, 4608, 5120, 5632


def _params(sem=None, vmem_mb=None):
    kw = {}
    if sem is not None:
        kw["dimension_semantics"] = sem
    if vmem_mb is not None:
        kw["vmem_limit_bytes"] = vmem_mb << 20
    return pltpu.CompilerParams(**kw)


def _dot(a, b):
    return jnp.dot(a.astype(BF16), b.astype(BF16), preferred_element_type=F32)


def _dot_nt(a, b):
    return lax.dot_general(a.astype(BF16), b.astype(BF16), (((1,), (1,)), ((), ())), preferred_element_type=F32)


def _dot_tn(a, b):
    return lax.dot_general(a.astype(BF16), b.astype(BF16), (((0,), (0,)), ((), ())), preferred_element_type=F32)


def _dot_f32(a, b):
    return jnp.dot(a, b, preferred_element_type=F32, precision=HIGHEST)


def _perm_segments():
    segs = [(3592, 5640), (2048, 2560)]
    for p in range(4):
        segs += [(2560 + 128 * p, 2688 + 128 * p), (3072 + 128 * p, 3200 + 128 * p)]
    for h in range(4):
        segs += [(128 * h + 512 * t, 128 * h + 512 * t + 128) for t in range(4)]
    segs += [(3584, 3592)]
    return segs


def _permute_cols(w):
    parts = [w[:, a:b] for a, b in _perm_segments()]
    parts.append(jnp.zeros((w.shape[0], NP - NIN), w.dtype))
    return jnp.concatenate(parts, axis=1)


def _unpermute_cols(g):
    pos, where = 0, {}
    for a, b in _perm_segments():
        where[a] = (pos, pos + b - a)
        pos += b - a
    parts = [g[:, where[a][0]:where[a][1]] for a in sorted(where)]
    return jnp.concatenate(parts, axis=1)


def _allgather8(v, name):
    rows, cols = v.shape

    def body(x_ref, out_ref, send_sems, recv_sems, local_sem):
        x, y, c = lax.axis_index("x"), lax.axis_index("y"), lax.axis_index("c")
        me, sibling = (x, y, c), (x, y, 1 - c)
        chips = [(1 - x, y), (x, 1 - y), (1 - x, 1 - y)]

        def slot(px, py, pc):
            return out_ref.at[4 * px + 2 * py + pc]

        def copy(k, block, to, src=None):
            return pltpu.make_async_remote_copy(
                src_ref=slot(*block) if src is None else src, dst_ref=slot(*block),
                send_sem=send_sems.at[k], recv_sem=recv_sems.at[k], device_id=to, device_id_type=MESH)

        mine = pltpu.make_async_copy(x_ref, slot(*me), local_sem)
        mine.start()
        first = [copy(0, me, sibling, src=x_ref)]
        first += [copy(1 + j, me, (*chip, c), src=x_ref) for j, chip in enumerate(chips)]
        for cp in first:
            cp.start()
        passed = [copy(4 + j, (*chip, c), sibling) for j, chip in enumerate(chips)]
        for j, chip in enumerate(chips):
            copy(1 + j, (*chip, c), me).wait_recv()
            passed[j].start()
        copy(0, sibling, me).wait_recv()
        for j, chip in enumerate(chips):
            copy(4 + j, (*chip, 1 - c), me).wait_recv()
        for cp in first + passed:
            cp.wait_send()
        mine.wait()

    return pl.pallas_call(
        body, name=name,
        out_shape=jax.ShapeDtypeStruct((N_DEV, rows, cols), v.dtype),
        in_specs=[pl.BlockSpec(memory_space=pltpu.VMEM)],
        out_specs=pl.BlockSpec(memory_space=pltpu.VMEM),
        scratch_shapes=[pltpu.SemaphoreType.DMA((7,)), pltpu.SemaphoreType.DMA((7,)), pltpu.SemaphoreType.DMA],
    )(v)


def _gather_weights(pack):
    def body(pack_ref, out_ref, send_sems, recv_sems, local_sem):
        x, y, c = lax.axis_index("x"), lax.axis_index("y"), lax.axis_index("c")
        sibling = (x, y, 1 - c)
        chips = [(1 - x, y), (x, 1 - y), (1 - x, 1 - y)]

        def blk(px, py, half):
            return out_ref.at[2 * px + py, pl.ds(half * HALF, HALF), :]

        def copy(k, block, to, src=None):
            return pltpu.make_async_remote_copy(
                src_ref=blk(*block) if src is None else src, dst_ref=blk(*block),
                send_sem=send_sems.at[k], recv_sem=recv_sems.at[k], device_id=to, device_id_type=MESH)

        mine = pltpu.make_async_copy(pack_ref, out_ref.at[2 * x + y], local_sem)
        mine.start()
        my_half = pack_ref.at[pl.ds(c * HALF, HALF), :]
        first = [copy(j, (x, y, c), (*chip, c), src=my_half) for j, chip in enumerate(chips)]
        for cp in first:
            cp.start()
        passed = [copy(3 + j, (*chip, c), sibling) for j, chip in enumerate(chips)]
        for j, chip in enumerate(chips):
            copy(j, (*chip, c), (x, y, c)).wait_recv()
            passed[j].start()
        for j, chip in enumerate(chips):
            copy(3 + j, (*chip, 1 - c), (x, y, c)).wait_recv()
        for cp in first + passed:
            cp.wait_send()
        mine.wait()

    return pl.pallas_call(
        body, name="gather_weights",
        out_shape=jax.ShapeDtypeStruct((N_CHIPS, PACK_ROWS, 1024), pack.dtype),
        in_specs=[pl.BlockSpec(memory_space=pl.ANY)],
        out_specs=pl.BlockSpec(memory_space=pl.ANY),
        scratch_shapes=[pltpu.SemaphoreType.DMA((6,)), pltpu.SemaphoreType.DMA((6,)), pltpu.SemaphoreType.DMA],
    )(pack)


def _swap_halves(gpack):
    def body(g_ref, out_ref, send_sem, recv_sem):
        x, y, c = lax.axis_index("x"), lax.axis_index("y"), lax.axis_index("c")
        cp = pltpu.make_async_remote_copy(
            src_ref=g_ref.at[:, pl.ds((1 - c) * HALF, HALF), :], dst_ref=out_ref,
            send_sem=send_sem, recv_sem=recv_sem, device_id=(x, y, 1 - c), device_id_type=MESH)
        cp.start()
        cp.wait()

    return pl.pallas_call(
        body, name="grad_swap_halves",
        out_shape=jax.ShapeDtypeStruct((N_CHIPS, HALF, 1024), gpack.dtype),
        in_specs=[pl.BlockSpec(memory_space=pl.ANY)],
        out_specs=pl.BlockSpec(memory_space=pl.ANY),
        scratch_shapes=[pltpu.SemaphoreType.DMA, pltpu.SemaphoreType.DMA],
    )(gpack)


def _scatter_chips(red):
    def body(r_ref, out_ref, send_sems, recv_sems):
        x, y, c = lax.axis_index("x"), lax.axis_index("y"), lax.axis_index("c")
        chips = [(1 - x, y), (x, 1 - y), (1 - x, 1 - y)]
        cps = [pltpu.make_async_remote_copy(
            src_ref=r_ref.at[2 * chip[0] + chip[1]], dst_ref=out_ref.at[j],
            send_sem=send_sems.at[j], recv_sem=recv_sems.at[j], device_id=(*chip, c), device_id_type=MESH)
            for j, chip in enumerate(chips)]
        for cp in cps:
            cp.start()
        for cp in cps:
            cp.wait()

    return pl.pallas_call(
        body, name="grad_scatter_chips",
        out_shape=jax.ShapeDtypeStruct((3, HALF, 1024), red.dtype),
        in_specs=[pl.BlockSpec(memory_space=pl.ANY)],
        out_specs=pl.BlockSpec(memory_space=pl.ANY),
        scratch_shapes=[pltpu.SemaphoreType.DMA((3,)), pltpu.SemaphoreType.DMA((3,))],
    )(red)


def _join_halves(half):
    def body(h_ref, out_ref, send_sem, recv_sem, local_sem):
        x, y, c = lax.axis_index("x"), lax.axis_index("y"), lax.axis_index("c")
        rows = out_ref.at[pl.ds(c * HALF, HALF), :]
        mine = pltpu.make_async_copy(h_ref, rows, local_sem)
        mine.start()
        cp = pltpu.make_async_remote_copy(
            src_ref=h_ref, dst_ref=rows, send_sem=send_sem, recv_sem=recv_sem,
            device_id=(x, y, 1 - c), device_id_type=MESH)
        cp.start()
        pltpu.make_async_remote_copy(
            src_ref=h_ref, dst_ref=out_ref.at[pl.ds((1 - c) * HALF, HALF), :], send_sem=send_sem,
            recv_sem=recv_sem, device_id=(x, y, 1 - c), device_id_type=MESH).wait_recv()
        cp.wait_send()
        mine.wait()

    return pl.pallas_call(
        body, name="grad_join_halves",
        out_shape=jax.ShapeDtypeStruct((PACK_ROWS, 1024), half.dtype),
        in_specs=[pl.BlockSpec(memory_space=pl.ANY)],
        out_specs=pl.BlockSpec(memory_space=pl.ANY),
        scratch_shapes=[pltpu.SemaphoreType.DMA, pltpu.SemaphoreType.DMA, pltpu.SemaphoreType.DMA],
    )(half)


def _add_my_half(gpack, other, c_idx):
    tr = 512
    nb = HALF // tr

    def body(c_ref, g_ref, o_ref, out_ref):
        out_ref[...] = g_ref[...] + o_ref[...]

    return pl.pallas_call(
        body, name="grad_add_halves",
        grid_spec=pltpu.PrefetchScalarGridSpec(
            num_scalar_prefetch=1, grid=(N_CHIPS, nb),
            in_specs=[pl.BlockSpec((1, tr, 1024), lambda k, i, c: (k, c[0] * nb + i, 0)),
                      pl.BlockSpec((1, tr, 1024), lambda k, i, c: (k, i, 0))],
            out_specs=pl.BlockSpec((1, tr, 1024), lambda k, i, c: (k, i, 0))),
        out_shape=jax.ShapeDtypeStruct((N_CHIPS, HALF, 1024), F32),
        compiler_params=_params(("parallel", "parallel")),
    )(c_idx, gpack, other)


def _add_chips(red, recv, chip_idx):
    tr = 512

    def body(k_ref, r_ref, v_ref, out_ref):
        out_ref[...] = ((r_ref[0] + v_ref[0]) + v_ref[1]) + v_ref[2]

    return pl.pallas_call(
        body, name="grad_add_chips",
        grid_spec=pltpu.PrefetchScalarGridSpec(
            num_scalar_prefetch=1, grid=(HALF // tr,),
            in_specs=[pl.BlockSpec((1, tr, 1024), lambda i, k: (k[0], i, 0)),
                      pl.BlockSpec((3, tr, 1024), lambda i, k: (0, i, 0))],
            out_specs=pl.BlockSpec((tr, 1024), lambda i, k: (i, 0))),
        out_shape=jax.ShapeDtypeStruct((HALF, 1024), F32),
        compiler_params=_params(("parallel",)),
    )(chip_idx, red, recv)


def _mod_shard(c_all, w_ada, b_ada):
    nb, cols = c_all.shape[0], w_ada.shape[1]

    def body(c_ref, w_ref, b_ref, o_ref):
        c = c_ref[...]
        o_ref[...] = _dot(c * jax.nn.sigmoid(c), w_ref[...]) + b_ref[...]

    return pl.pallas_call(
        body, name="mod_shard", out_shape=jax.ShapeDtypeStruct((nb, cols), F32),
        compiler_params=_params(vmem_mb=48),
    )(c_all, w_ada, b_ada)


def _proj(x2, mod8, w_p, seq):
    t = x2.shape[0]
    tm, tn = min(512, seq), 1152
    tpb = seq // tm

    def body(x_ref, mod_ref, w_ref, o_ref, h_scr):
        @pl.when(pl.program_id(1) == 0)
        def _():
            h_scr[...] = (x_ref[...] * (1.0 + mod_ref[0, 1:2, :]) + mod_ref[0, 0:1, :]).astype(BF16)
        o_ref[...] = jnp.dot(h_scr[...], w_ref[...], preferred_element_type=F32)

    return pl.pallas_call(
        body, name="proj", grid=(t // tm, NP // tn),
        in_specs=[pl.BlockSpec((tm, D), lambda i, j: (i, 0)),
                  pl.BlockSpec((1, 8, D), lambda i, j: (i // tpb, 0, 0)),
                  pl.BlockSpec((D, tn), lambda i, j: (0, j))],
        out_specs=pl.BlockSpec((tm, tn), lambda i, j: (i, j)),
        out_shape=jax.ShapeDtypeStruct((t, NP), F32),
        scratch_shapes=[pltpu.VMEM((tm, D), BF16)],
        compiler_params=_params(("parallel", "arbitrary"), 48),
    )(x2, mod8, w_p)


def _tn_matmul(a, b, name, seq, mod8=None, rows=None):
    t, ka = a.shape
    n = b.shape[1]
    tt = min(512, seq)
    tpb = seq // tt
    tn = n
    for cand in (1152, 1024, 1408, 512):
        if n % cand == 0:
            tn = cand
            break
    nt = t // tt

    def body(*refs):
        if mod8 is None:
            a_ref, b_ref, o_ref = refs
            av = a_ref[...]
        else:
            a_ref, m_ref, b_ref, o_ref = refs
            av = a_ref[...] * (1.0 + m_ref[0, rows[1]:rows[1] + 1, :]) + m_ref[0, rows[0]:rows[0] + 1, :]
        part = _dot_tn(av, b_ref[...])

        @pl.when(pl.program_id(1) == 0)
        def _():
            o_ref[...] = part

        @pl.when(pl.program_id(1) > 0)
        def _():
            o_ref[...] += part

    in_specs = [pl.BlockSpec((tt, ka), lambda j, k: (k, 0))]
    args = [a]
    if mod8 is not None:
        in_specs.append(pl.BlockSpec((1, 8, ka), lambda j, k: (k // tpb, 0, 0)))
        args.append(mod8)
    in_specs.append(pl.BlockSpec((tt, tn), lambda j, k: (k, j)))
    args.append(b)
    return pl.pallas_call(
        body, name=name, grid=(n // tn, nt), in_specs=in_specs,
        out_specs=pl.BlockSpec((ka, tn), lambda j, k: (0, j)),
        out_shape=jax.ShapeDtypeStruct((ka, n), F32),
        compiler_params=_params(("parallel", "arbitrary"), 48),
    )(*args)


def _dh_kernel(dproj, w_p, x2, dxp, mod8, seq):
    t = x2.shape[0]
    tm, tk = min(512, seq), 1152
    tpb = seq // tm
    nk = NP // tk
    nbatch = t // seq

    def body(dp_ref, w_ref, x_ref, dxp_ref, mod_ref, gx_ref, dm_ref, acc):
        i, k = pl.program_id(0), pl.program_id(1)

        @pl.when(k == 0)
        def _():
            acc[...] = jnp.zeros_like(acc)

        acc[...] += _dot_nt(dp_ref[...], w_ref[...])

        @pl.when(k == nk - 1)
        def _():
            dh = acc[...]
            gx_ref[...] = dxp_ref[...] + dh * (1.0 + mod_ref[0, 1:2, :])
            upd = jnp.concatenate(
                [jnp.sum(dh, axis=0, keepdims=True), jnp.sum(dh * x_ref[...], axis=0, keepdims=True),
                 jnp.zeros((6, D), F32)], axis=0)

            @pl.when(i % tpb == 0)
            def _():
                dm_ref[0] = upd

            @pl.when(i % tpb != 0)
            def _():
                dm_ref[0] += upd

    return pl.pallas_call(
        body, name="dh", grid=(t // tm, nk),
        in_specs=[pl.BlockSpec((tm, tk), lambda i, k: (i, k)),
                  pl.BlockSpec((D, tk), lambda i, k: (0, k)),
                  pl.BlockSpec((tm, D), lambda i, k: (i, 0)),
                  pl.BlockSpec((tm, D), lambda i, k: (i, 0)),
                  pl.BlockSpec((1, 8, D), lambda i, k: (i // tpb, 0, 0))],
        out_specs=[pl.BlockSpec((tm, D), lambda i, k: (i, 0)),
                   pl.BlockSpec((1, 8, D), lambda i, k: (i // tpb, 0, 0))],
        out_shape=[jax.ShapeDtypeStruct((t, D), F32), jax.ShapeDtypeStruct((nbatch, 8, D), F32)],
        scratch_shapes=[pltpu.VMEM((tm, D), F32)],
        compiler_params=_params(("arbitrary", "arbitrary"), 48),
    )(dproj, w_p, x2, dxp, mod8)


def _hgrn_gates(p, lbl):
    q, fl, v, g = p[:, 0:128], p[:, 128:256], p[:, 256:384], p[:, 384:512]
    lb = jax.nn.sigmoid(lbl[0:1, :] - lbl[1:2, :])
    sg = jax.nn.sigmoid(fl)
    f = lb + (1.0 - lb) * sg
    return q, v, g, lb, sg, f


def _group_tri(n, upper):
    r = lax.broadcasted_iota(jnp.int32, (n, n), 0)
    c = lax.broadcasted_iota(jnp.int32, (n, n), 1)
    same = (r // SUB) == (c // SUB)
    keep = (c >= r) if upper else (c <= r)
    return jnp.where(same & keep, 1.0, 0.0).astype(F32)


def _tri3():
    return (lax.broadcasted_iota(jnp.int32, (SUB, SUB, ADH), 0) >= lax.broadcasted_iota(jnp.int32, (SUB, SUB, ADH), 1))


def _hgrn_sub_fwd(qj, kj, vj, bj, st):
    bl = bj[SUB - 1:SUB, :]
    qt = qj * jnp.exp(bj)
    kt = kj * jnp.exp(bl - bj)
    e3 = jnp.exp(jnp.where(_tri3(), bj[:, None, :] - bj[None, :, :], -jnp.inf))
    a3 = jnp.sum(qj[:, None, :] * kj[None, :, :] * e3, axis=-1, keepdims=True)
    o = _dot_nt(qt, st) + jnp.sum(a3 * vj[None, :, :], axis=1)
    st_new = st * jnp.exp(bl) + _dot_tn(vj, kt)
    return o, st_new


def _hgrn_fwd(proj, lb_logits, norm_w, nbatch, seq):
    t = proj.shape[0]
    blk = min(256, seq)
    nb = seq // blk
    nsub = blk // SUB

    def body(p_ref, lbl_ref, nw_ref, y_ref, ck_ref, q_s, k_s, v_s, b_s, o_s, st_s):
        i = pl.program_id(2)

        @pl.when(i == 0)
        def _():
            st_s[...] = jnp.zeros_like(st_s)

        ck_ref[0] = st_s[...]
        q, v, g, lb, sg, f = _hgrn_gates(p_ref[...], lbl_ref[...])
        q_s[...] = q
        k_s[...] = 1.0 - f
        v_s[...] = v
        b_s[...] = _dot_f32(_group_tri(blk, False), jnp.log(f))

        def sub(j, st):
            r = pl.ds(pl.multiple_of(j * SUB, SUB), SUB)
            o, st = _hgrn_sub_fwd(q_s[r, :], k_s[r, :], v_s[r, :], b_s[r, :], st)
            o_s[r, :] = o
            return st

        st_s[...] = lax.fori_loop(0, nsub, sub, st_s[...])
        o = o_s[...]
        r = lax.rsqrt(jnp.mean(o * o, axis=-1, keepdims=True) + RMS_EPS)
        y_ref[...] = (o * r * nw_ref[...] * jax.nn.sigmoid(g)).astype(y_ref.dtype)

    return pl.pallas_call(
        body, name="hgrn_fwd", grid=(AH, nbatch, nb),
        in_specs=[pl.BlockSpec((blk, 512), lambda h, b, i: (b * nb + i, COL_A // 512 + h)),
                  pl.BlockSpec((2, 128), lambda h, b, i: (0, h)),
                  pl.BlockSpec((1, 128), lambda h, b, i: (0, h))],
        out_specs=[pl.BlockSpec((blk, 128), lambda h, b, i: (b * nb + i, h)),
                   pl.BlockSpec((1, 128, 128), lambda h, b, i: ((h * nbatch + b) * nb + i, 0, 0))],
        out_shape=[jax.ShapeDtypeStruct((t, AW), BF16), jax.ShapeDtypeStruct((AH * nbatch * nb, 128, 128), F32)],
        scratch_shapes=[pltpu.VMEM((blk, 128), F32)] * 5 + [pltpu.VMEM((128, 128), F32)],
        compiler_params=_params(("parallel", "parallel", "arbitrary")),
    )(proj, lb_logits, norm_w)


def _hgrn_bwd(proj, dya, ckpt, lb_logits, norm_w, dproj, nbatch, seq):
    t = proj.shape[0]
    blk = min(256, seq)
    nb = seq // blk
    nsub = blk // SUB

    def body(p_ref, dy_ref, ck_ref, lbl_ref, nw_ref, dp_in, dp_ref, sm_ref,
             q_s, k_s, v_s, b_s, o_s, do_s, dq_s, dk_s, dv_s, db_s, st_all, dst_s):
        del dp_in
        b_id, i = pl.program_id(1), pl.program_id(2)

        @pl.when(i == 0)
        def _():
            dst_s[...] = jnp.zeros_like(dst_s)

        q, v, g, lb, sg, f = _hgrn_gates(p_ref[...], lbl_ref[...])
        q_s[...] = q
        k_s[...] = 1.0 - f
        v_s[...] = v
        b_s[...] = _dot_f32(_group_tri(blk, False), jnp.log(f))

        def sub(j, st):
            r = pl.ds(pl.multiple_of(j * SUB, SUB), SUB)
            st_all[j] = st
            o, st = _hgrn_sub_fwd(q_s[r, :], k_s[r, :], v_s[r, :], b_s[r, :], st)
            o_s[r, :] = o
            return st

        lax.fori_loop(0, nsub, sub, ck_ref[0])

        o = o_s[...]
        dy = dy_ref[...]
        nw = nw_ref[...]
        sgo = jax.nn.sigmoid(g)
        r = lax.rsqrt(jnp.mean(o * o, axis=-1, keepdims=True) + RMS_EPS)
        on = o * r
        dg = dy * on * nw * sgo * (1.0 - sgo)
        dnw = jnp.sum(dy * on * sgo, axis=0, keepdims=True)
        dn = dy * nw * sgo
        do_s[...] = r * dn - o * (r * r * r) * jnp.mean(dn * o, axis=-1, keepdims=True)

        def bsub(jj, dstn):
            j = nsub - 1 - jj
            rr = pl.ds(pl.multiple_of(j * SUB, SUB), SUB)
            qj, kj, vj, bj, doj = q_s[rr, :], k_s[rr, :], v_s[rr, :], b_s[rr, :], do_s[rr, :]
            st = st_all[j]
            bl = bj[SUB - 1:SUB, :]
            eb = jnp.exp(bj)
            ebl = jnp.exp(bl - bj)
            e = jnp.exp(bl)
            qt = qj * eb
            kt = kj * ebl
            e3 = jnp.exp(jnp.where(_tri3(), bj[:, None, :] - bj[None, :, :], -jnp.inf))
            a3 = jnp.sum(qj[:, None, :] * kj[None, :, :] * e3, axis=-1, keepdims=True)
            da3 = jnp.sum(doj[:, None, :] * vj[None, :, :], axis=-1, keepdims=True)
            w3 = da3 * e3
            dqa = jnp.sum(w3 * kj[None, :, :], axis=1)
            dka = jnp.sum(w3 * qj[:, None, :], axis=0)
            dqt = _dot(doj, st)
            dkt = _dot(vj, dstn)
            dv_s[rr, :] = jnp.sum(a3 * doj[:, None, :], axis=0) + _dot_nt(kt, dstn)
            dq_s[rr, :] = dqa + dqt * eb
            dk_s[rr, :] = dka + dkt * ebl
            dbl = jnp.sum(dstn * st, axis=0, keepdims=True) * e + jnp.sum(dkt * kt, axis=0, keepdims=True)
            last = lax.broadcasted_iota(jnp.int32, (SUB, ADH), 0) == SUB - 1
            db_s[rr, :] = qj * dqa - kj * dka + dqt * qt - dkt * kt + jnp.where(last, dbl, 0.0)
            return dstn * e + _dot_tn(doj, qt)

        dst_s[...] = lax.fori_loop(0, nsub, bsub, dst_s[...])

        dlf = _dot_f32(_group_tri(blk, True), db_s[...])
        df = dlf / f - dk_s[...]
        dfl = df * (1.0 - lb) * sg * (1.0 - sg)
        dlb = jnp.sum(df * (1.0 - sg), axis=0, keepdims=True)
        dp_ref[:, 0:128] = dq_s[...].astype(dp_ref.dtype)
        dp_ref[:, 128:256] = dfl.astype(dp_ref.dtype)
        dp_ref[:, 256:384] = dv_s[...].astype(dp_ref.dtype)
        dp_ref[:, 384:512] = dg.astype(dp_ref.dtype)
        upd = jnp.concatenate([dlb, dnw, jnp.zeros((6, 128), F32)], axis=0)
        first = (b_id == 0) & (i == 0)

        @pl.when(first)
        def _():
            sm_ref[...] = upd

        @pl.when(jnp.logical_not(first))
        def _():
            sm_ref[...] += upd

    def rows(h, b, i):
        return b * nb + (nb - 1 - i)

    return pl.pallas_call(
        body, name="hgrn_bwd", grid=(AH, nbatch, nb),
        in_specs=[pl.BlockSpec((blk, 512), lambda h, b, i: (rows(h, b, i), COL_A // 512 + h)),
                  pl.BlockSpec((blk, 128), lambda h, b, i: (rows(h, b, i), h)),
                  pl.BlockSpec((1, 128, 128), lambda h, b, i: ((h * nbatch + b) * nb + (nb - 1 - i), 0, 0)),
                  pl.BlockSpec((2, 128), lambda h, b, i: (0, h)),
                  pl.BlockSpec((1, 128), lambda h, b, i: (0, h)),
                  pl.BlockSpec(memory_space=pl.ANY)],
        out_specs=[pl.BlockSpec((blk, 512), lambda h, b, i: (rows(h, b, i), COL_A // 512 + h)),
                   pl.BlockSpec((8, 128), lambda h, b, i: (0, h))],
        out_shape=[jax.ShapeDtypeStruct((t, NP), BF16), jax.ShapeDtypeStruct((8, AW), F32)],
        input_output_aliases={5: 0},
        scratch_shapes=[pltpu.VMEM((blk, 128), F32)] * 10 + [pltpu.VMEM((nsub, 128, 128), F32),
                                                            pltpu.VMEM((128, 128), F32)],
        compiler_params=_params(("parallel", "arbitrary", "arbitrary")),
    )(proj, dya, ckpt, lb_logits, norm_w, dproj)


def _tri(n, upper):
    r = lax.broadcasted_iota(jnp.int32, (n, n), 0)
    c = lax.broadcasted_iota(jnp.int32, (n, n), 1)
    return jnp.where((c >= r) if upper else (c <= r), 1.0, 0.0).astype(F32)


def _log_sigmoid(z):
    return jnp.minimum(z, 0.0) - jnp.log(1.0 + jnp.exp(-jnp.abs(z)))


def _fox_cum(proj, bias128, nbatch, seq):
    t = proj.shape[0]
    ts = min(512, seq)
    nb = seq // ts

    def body(p_ref, b_ref, o_ref, carry):
        @pl.when(pl.program_id(1) == 0)
        def _():
            carry[...] = jnp.zeros_like(carry)
        cum = _dot_f32(_tri(ts, False), _log_sigmoid(p_ref[...] + b_ref[...])) + carry[...]
        o_ref[...] = cum
        carry[...] = cum[ts - 1:ts, :]

    return pl.pallas_call(
        body, name="fox_cum", grid=(nbatch, nb),
        in_specs=[pl.BlockSpec((ts, 128), lambda b, i: (b * nb + i, COL_BF // 128)),
                  pl.BlockSpec((1, 128), lambda b, i: (0, 0))],
        out_specs=pl.BlockSpec((ts, 128), lambda b, i: (b * nb + i, 0)),
        out_shape=jax.ShapeDtypeStruct((t, 128), F32),
        scratch_shapes=[pltpu.VMEM((1, 128), F32)],
        compiler_params=_params(("parallel", "arbitrary")),
    )(proj, bias128)


def _fox_scores(q_ref, kv_ref, cq_ref, ck_ref, hh, hp, diag, tq, tk):
    qh = (q_ref[:, 64 * hh:64 * hh + 64] * (BDH ** -0.5)).astype(BF16)
    kh = kv_ref[:, 64 * hh:64 * hh + 64]
    s = _dot_nt(qh, kh) + cq_ref[0, :, 64 * hh:64 * hh + 1] - ck_ref[0, pl.ds(2 * hp + hh, 1), :]
    row = lax.broadcasted_iota(jnp.int32, (tq, tk), 0)
    col = lax.broadcasted_iota(jnp.int32, (tq, tk), 1)
    return jnp.where(jnp.logical_or(jnp.logical_not(diag), col <= row), s, NEG)


def _fox_fwd(proj, cumq, cumk, nbatch, seq):
    t = proj.shape[0]
    tq = tk = min(512, seq)
    nq = seq // tq

    def body(q_ref, kv_ref, cq_ref, ck_ref, o_ref, lse_ref, m_s, l_s, acc_s):
        hp, i, j = pl.program_id(1), pl.program_id(2), pl.program_id(3)

        @pl.when(j == 0)
        def _():
            m_s[...] = jnp.full_like(m_s, NEG)
            l_s[...] = jnp.zeros_like(l_s)
            acc_s[...] = jnp.zeros_like(acc_s)

        @pl.when(j <= i)
        def _():
            for hh in range(2):
                s = _fox_scores(q_ref, kv_ref, cq_ref, ck_ref, hh, hp, j == i, tq, tk)
                m_prev = m_s[hh]
                m_new = jnp.maximum(m_prev, jnp.max(s, axis=-1, keepdims=True))
                alpha = jnp.exp(m_prev - m_new)
                p = jnp.exp(s - m_new)
                l_s[hh] = l_s[hh] * alpha + jnp.sum(p, axis=-1, keepdims=True)
                acc_s[hh] = acc_s[hh] * alpha + _dot(p, kv_ref[:, 128 + 64 * hh:192 + 64 * hh])
                m_s[hh] = m_new

        @pl.when(j == i)
        def _():
            for hh in range(2):
                o_ref[:, 64 * hh:64 * hh + 64] = (acc_s[hh] / l_s[hh]).astype(o_ref.dtype)
                lse_ref[0, :, 64 * hh:64 * hh + 64] = jnp.broadcast_to(m_s[hh] + jnp.log(l_s[hh]), (tq, 64))

    return pl.pallas_call(
        body, name="fox_fwd", grid=(nbatch, 4, nq, nq),
        in_specs=[pl.BlockSpec((tq, 128), lambda b, p, i, j: (b * nq + i, COL_BQ // 128 + p)),
                  pl.BlockSpec((tk, 256), lambda b, p, i, j: (b * nq + jnp.minimum(j, i), COL_KV // 256 + p)),
                  pl.BlockSpec((1, tq, 128), lambda b, p, i, j: (p, b * nq + i, 0)),
                  pl.BlockSpec((1, 8, tk), lambda b, p, i, j: (b, 0, jnp.minimum(j, i)))],
        out_specs=[pl.BlockSpec((tq, 128), lambda b, p, i, j: (b * nq + i, p)),
                   pl.BlockSpec((1, tq, 128), lambda b, p, i, j: (p, b * nq + i, 0))],
        out_shape=[jax.ShapeDtypeStruct((t, 512), BF16), jax.ShapeDtypeStruct((4, t, 128), F32)],
        scratch_shapes=[pltpu.VMEM((2, tq, 1), F32), pltpu.VMEM((2, tq, 1), F32), pltpu.VMEM((2, tq, 64), F32)],
        compiler_params=_params(("parallel", "parallel", "parallel", "arbitrary"), 48),
    )(proj, proj, cumq, cumk)


def _fox_dq(proj, cumq, cumk, lse, yb, dyb, dproj, nbatch, seq):
    t = proj.shape[0]
    tq = tk = min(512, seq)
    nq = seq // tq

    def body(q_ref, kv_ref, cq_ref, ck_ref, lse_ref, o_ref, do_ref, dp_in, dq_ref, dr_ref, acc_s, rs_s):
        del dp_in
        hp, i, j = pl.program_id(1), pl.program_id(2), pl.program_id(3)

        @pl.when(j == 0)
        def _():
            acc_s[...] = jnp.zeros_like(acc_s)
            rs_s[...] = jnp.zeros_like(rs_s)

        @pl.when(j <= i)
        def _():
            for hh in range(2):
                s = _fox_scores(q_ref, kv_ref, cq_ref, ck_ref, hh, hp, j == i, tq, tk)
                p = jnp.exp(s - lse_ref[0, :, 64 * hh:64 * hh + 1])
                doh = do_ref[:, 64 * hh:64 * hh + 64]
                dd = jnp.sum(doh * o_ref[:, 64 * hh:64 * hh + 64].astype(F32), axis=-1, keepdims=True)
                dp = _dot_nt(doh, kv_ref[:, 128 + 64 * hh:192 + 64 * hh])
                ds = p * (dp - dd)
                acc_s[hh] += _dot(ds, kv_ref[:, 64 * hh:64 * hh + 64])
                rs_s[hh] += jnp.sum(ds, axis=-1, keepdims=True)

        @pl.when(j == i)
        def _():
            for hh in range(2):
                dq_ref[:, 64 * hh:64 * hh + 64] = (acc_s[hh] * (BDH ** -0.5)).astype(dq_ref.dtype)
                dr_ref[0, :, 64 * hh:64 * hh + 64] = jnp.broadcast_to(rs_s[hh], (tq, 64))

    return pl.pallas_call(
        body, name="fox_dq", grid=(nbatch, 4, nq, nq),
        in_specs=[pl.BlockSpec((tq, 128), lambda b, p, i, j: (b * nq + i, COL_BQ // 128 + p)),
                  pl.BlockSpec((tk, 256), lambda b, p, i, j: (b * nq + jnp.minimum(j, i), COL_KV // 256 + p)),
                  pl.BlockSpec((1, tq, 128), lambda b, p, i, j: (p, b * nq + i, 0)),
                  pl.BlockSpec((1, 8, tk), lambda b, p, i, j: (b, 0, jnp.minimum(j, i))),
                  pl.BlockSpec((1, tq, 128), lambda b, p, i, j: (p, b * nq + i, 0)),
                  pl.BlockSpec((tq, 128), lambda b, p, i, j: (b * nq + i, p)),
                  pl.BlockSpec((tq, 128), lambda b, p, i, j: (b * nq + i, p)),
                  pl.BlockSpec(memory_space=pl.ANY)],
        out_specs=[pl.BlockSpec((tq, 128), lambda b, p, i, j: (b * nq + i, COL_BQ // 128 + p)),
                   pl.BlockSpec((1, tq, 128), lambda b, p, i, j: (p, b * nq + i, 0))],
        out_shape=[jax.ShapeDtypeStruct((t, NP), BF16), jax.ShapeDtypeStruct((4, t, 128), F32)],
        input_output_aliases={7: 0},
        scratch_shapes=[pltpu.VMEM((2, tq, 64), F32), pltpu.VMEM((2, tq, 1), F32)],
        compiler_params=_params(("parallel", "parallel", "parallel", "arbitrary"), 48),
    )(proj, proj, cumq, cumk, lse, yb, dyb, dproj)


def _fox_dkv(proj, cumq, cumk, lse, yb, dyb, dproj, nbatch, seq):
    t = proj.shape[0]
    tq = tk = min(512, seq)
    nq = seq // tq

    def body(q_ref, kv_ref, cq_ref, ck_ref, lse_ref, o_ref, do_ref, dp_in, dkv_ref, dc_ref, dk_s, dv_s, dc_s):
        del dp_in
        hp, j, ii = pl.program_id(1), pl.program_id(2), pl.program_id(3)
        i = j + ii

        @pl.when(ii == 0)
        def _():
            dk_s[...] = jnp.zeros_like(dk_s)
            dv_s[...] = jnp.zeros_like(dv_s)
            dc_s[...] = jnp.zeros_like(dc_s)

        @pl.when(i < nq)
        def _():
            for hh in range(2):
                s = _fox_scores(q_ref, kv_ref, cq_ref, ck_ref, hh, hp, ii == 0, tq, tk)
                p = jnp.exp(s - lse_ref[0, :, 64 * hh:64 * hh + 1])
                doh = do_ref[:, 64 * hh:64 * hh + 64]
                dd = jnp.sum(doh * o_ref[:, 64 * hh:64 * hh + 64].astype(F32), axis=-1, keepdims=True)
                dp = _dot_nt(doh, kv_ref[:, 128 + 64 * hh:192 + 64 * hh])
                ds = p * (dp - dd)
                dv_s[hh] += _dot_tn(p, doh)
                dk_s[hh] += _dot_tn(ds, q_ref[:, 64 * hh:64 * hh + 64] * (BDH ** -0.5))
                dc_s[hh:hh + 1, :] += jnp.sum(ds, axis=0, keepdims=True)

        @pl.when(ii == nq - 1)
        def _():
            for hh in range(2):
                dkv_ref[:, 64 * hh:64 * hh + 64] = dk_s[hh].astype(dkv_ref.dtype)
                dkv_ref[:, 128 + 64 * hh:192 + 64 * hh] = dv_s[hh].astype(dkv_ref.dtype)
            dc_ref[0, 0] = dc_s[...]

    def qrow(b, p, j, ii):
        return b * nq + jnp.minimum(j + ii, nq - 1)

    return pl.pallas_call(
        body, name="fox_dkv", grid=(nbatch, 4, nq, nq),
        in_specs=[pl.BlockSpec((tq, 128), lambda b, p, j, ii: (qrow(b, p, j, ii), COL_BQ // 128 + p)),
                  pl.BlockSpec((tk, 256), lambda b, p, j, ii: (b * nq + j, COL_KV // 256 + p)),
                  pl.BlockSpec((1, tq, 128), lambda b, p, j, ii: (p, qrow(b, p, j, ii), 0)),
                  pl.BlockSpec((1, 8, tk), lambda b, p, j, ii: (b, 0, j)),
                  pl.BlockSpec((1, tq, 128), lambda b, p, j, ii: (p, qrow(b, p, j, ii), 0)),
                  pl.BlockSpec((tq, 128), lambda b, p, j, ii: (qrow(b, p, j, ii), p)),
                  pl.BlockSpec((tq, 128), lambda b, p, j, ii: (qrow(b, p, j, ii), p)),
                  pl.BlockSpec(memory_space=pl.ANY)],
        out_specs=[pl.BlockSpec((tk, 256), lambda b, p, j, ii: (b * nq + j, COL_KV // 256 + p)),
                   pl.BlockSpec((1, 1, 8, tk), lambda b, p, j, ii: (b, p, 0, j))],
        out_shape=[jax.ShapeDtypeStruct((t, NP), BF16), jax.ShapeDtypeStruct((nbatch, 4, 8, seq), F32)],
        input_output_aliases={7: 0},
        scratch_shapes=[pltpu.VMEM((2, tk, 64), F32), pltpu.VMEM((2, tk, 64), F32), pltpu.VMEM((8, tk), F32)],
        compiler_params=_params(("parallel", "parallel", "parallel", "arbitrary"), 48),
    )(proj, proj, cumq, cumk, lse, yb, dyb, dproj)


def _fox_dbf(proj, bias128, drs, dcs, dproj, nbatch, seq):
    t = proj.shape[0]
    ts = min(512, seq)
    nb = seq // ts

    def body(p_ref, b_ref, dr_ref, dc_ref, dp_in, dp_ref, sm_ref, carry):
        del dp_in
        b_id, i = pl.program_id(0), pl.program_id(1)

        @pl.when(i == 0)
        def _():
            carry[...] = jnp.zeros_like(carry)

        dcum = dr_ref[...] - dc_ref[...]
        rc = _dot_f32(_tri(ts, True), dcum) + carry[...]
        carry[...] = rc[0:1, :]
        z = p_ref[...] + b_ref[...]
        lane = lax.broadcasted_iota(jnp.int32, (ts, 128), 1)
        dz = jnp.where(lane < BH, rc * jax.nn.sigmoid(-z), 0.0)
        dp_ref[...] = dz.astype(dp_ref.dtype)
        upd = jnp.concatenate([jnp.sum(dz, axis=0, keepdims=True), jnp.zeros((7, 128), F32)], axis=0)
        first = (b_id == 0) & (i == 0)

        @pl.when(first)
        def _():
            sm_ref[...] = upd

        @pl.when(jnp.logical_not(first))
        def _():
            sm_ref[...] += upd

    def rows(b, i):
        return b * nb + (nb - 1 - i)

    return pl.pallas_call(
        body, name="fox_dbf", grid=(nbatch, nb),
        in_specs=[pl.BlockSpec((ts, 128), lambda b, i: (rows(b, i), COL_BF // 128)),
                  pl.BlockSpec((1, 128), lambda b, i: (0, 0)),
                  pl.BlockSpec((ts, 128), lambda b, i: (rows(b, i), 0)),
                  pl.BlockSpec((ts, 128), lambda b, i: (rows(b, i), 0)),
                  pl.BlockSpec(memory_space=pl.ANY)],
        out_specs=[pl.BlockSpec((ts, 128), lambda b, i: (rows(b, i), COL_BF // 128)),
                   pl.BlockSpec((8, 128), lambda b, i: (0, 0))],
        out_shape=[jax.ShapeDtypeStruct((t, NP), BF16), jax.ShapeDtypeStruct((8, 128), F32)],
        input_output_aliases={4: 0},
        scratch_shapes=[pltpu.VMEM((1, 128), F32)],
        compiler_params=_params(("arbitrary", "arbitrary")),
    )(proj, bias128, drs, dcs, dproj)


def _ln_stats(z):
    mu = jnp.mean(z, axis=-1, keepdims=True)
    zc = z - mu
    rstd = lax.rsqrt(jnp.mean(zc * zc, axis=-1, keepdims=True) + LN_EPS)
    return zc * rstd, rstd


def _ln_bwd(dy, xhat, rstd, w):
    dxh = dy * w
    return rstd * (dxh - jnp.mean(dxh, axis=-1, keepdims=True) - xhat * jnp.mean(dxh * xhat, axis=-1, keepdims=True))


def _merge_fwd(ya, yb, proj, x2, mod8, wba, wbb, wout, ln1w, ln1b, seq):
    t = x2.shape[0]
    tm = min(256, seq)
    tpb = seq // tm

    def body(ya_ref, yb_ref, g_ref, x_ref, mod_ref, wa_ref, wb_ref, wo_ref, lw_ref, lb_ref, mg_ref, u_ref, x1_ref):
        ga = jax.nn.sigmoid(g_ref[:, 0:D])
        gb = jax.nn.sigmoid(g_ref[:, D:2 * D])
        merged = (ga * jnp.dot(ya_ref[...], wa_ref[...], preferred_element_type=F32)
                  + gb * jnp.dot(yb_ref[...], wb_ref[...], preferred_element_type=F32))
        mg = merged.astype(BF16)
        mg_ref[...] = mg
        u = jnp.dot(mg, wo_ref[...], preferred_element_type=F32)
        u_ref[...] = u
        xhat, _ = _ln_stats(ALPHA * x_ref[...] + (1.0 + mod_ref[0, 2:3, :]) * u)
        x1_ref[...] = xhat * lw_ref[...] + lb_ref[...]

    tok = lambda w: pl.BlockSpec((tm, w), lambda i: (i, 0))
    full = lambda a: pl.BlockSpec(a.shape, lambda i: (0,) * a.ndim)
    return pl.pallas_call(
        body, name="merge_fwd", grid=(t // tm,),
        in_specs=[tok(512), tok(512), pl.BlockSpec((tm, 2048), lambda i: (i, COL_GATES // 2048)), tok(D),
                  pl.BlockSpec((1, 8, D), lambda i: (i // tpb, 0, 0)),
                  full(wba), full(wbb), full(wout), full(ln1w), full(ln1b)],
        out_specs=[tok(D), tok(D), tok(D)],
        out_shape=[jax.ShapeDtypeStruct((t, D), BF16), jax.ShapeDtypeStruct((t, D), F32),
                   jax.ShapeDtypeStruct((t, D), F32)],
        compiler_params=_params(("parallel",), 48),
    )(ya, yb, proj, x2, mod8, wba, wbb, wout, ln1w, ln1b)


def _merge_bwd(du, ya, yb, proj, wba, wbb, wout, seq):
    t = du.shape[0]
    tm = min(256, seq)

    def body(du_ref, ya_ref, yb_ref, g_ref, wa_ref, wb_ref, wo_ref, dp_ref, dpa_ref, dpb_ref, dya_ref, dyb_ref):
        ga = jax.nn.sigmoid(g_ref[:, 0:D])
        gb = jax.nn.sigmoid(g_ref[:, D:2 * D])
        dm = _dot_nt(du_ref[...], wo_ref[...])
        pa = jnp.dot(ya_ref[...], wa_ref[...], preferred_element_type=F32)
        pb = jnp.dot(yb_ref[...], wb_ref[...], preferred_element_type=F32)
        dpa = (dm * ga).astype(BF16)
        dpb = (dm * gb).astype(BF16)
        dpa_ref[...] = dpa
        dpb_ref[...] = dpb
        dp_ref[:, 0:D] = (dm * pa * ga * (1.0 - ga)).astype(BF16)
        dp_ref[:, D:2 * D] = (dm * pb * gb * (1.0 - gb)).astype(BF16)
        dya_ref[...] = _dot_nt(dpa, wa_ref[...])
        dyb_ref[...] = _dot_nt(dpb, wb_ref[...])

    tok = lambda w: pl.BlockSpec((tm, w), lambda i: (i, 0))
    full = lambda a: pl.BlockSpec(a.shape, lambda i: (0,) * a.ndim)
    return pl.pallas_call(
        body, name="merge_bwd", grid=(t // tm,),
        in_specs=[tok(D), tok(512), tok(512), pl.BlockSpec((tm, 2048), lambda i: (i, COL_GATES // 2048)),
                  full(wba), full(wbb), full(wout)],
        out_specs=[pl.BlockSpec((tm, 2048), lambda i: (i, COL_GATES // 2048)), tok(D), tok(D), tok(512), tok(512)],
        out_shape=[jax.ShapeDtypeStruct((t, NP), BF16), jax.ShapeDtypeStruct((t, D), BF16),
                   jax.ShapeDtypeStruct((t, D), BF16), jax.ShapeDtypeStruct((t, 512), F32),
                   jax.ShapeDtypeStruct((t, 512), F32)],
        compiler_params=_params(("parallel",), 48),
    )(du, ya, yb, proj, wba, wbb, wout)


def _ffn_fwd(x1, mod8, wg, wu, wd, target, ln2w, ln2b, seq):
    t = x1.shape[0]
    tm, tf = min(512, seq), 256
    tpb = seq // tm
    nf = DFF // tf
    nbatch = t // seq

    def body(x_ref, mod_ref, wg_ref, wu_ref, wd_ref, t_ref, lw_ref, lb_ref,
             a_ref, b_ref, dz_ref, st_ref, dm_ref, h_s, acc):
        i, j = pl.program_id(0), pl.program_id(1)

        @pl.when(j == 0)
        def _():
            h_s[...] = (x_ref[...] * (1.0 + mod_ref[0, 4:5, :]) + mod_ref[0, 3:4, :]).astype(BF16)
            acc[...] = jnp.zeros_like(acc)

        a = jnp.dot(h_s[...], wg_ref[...], preferred_element_type=F32)
        b = jnp.dot(h_s[...], wu_ref[...], preferred_element_type=F32)
        a_ref[...] = a.astype(BF16)
        b_ref[...] = b.astype(BF16)
        acc[...] += _dot(a * jax.nn.sigmoid(a) * b, wd_ref[...])

        @pl.when(j == nf - 1)
        def _():
            ffn = acc[...]
            xhat, rstd = _ln_stats(ALPHA * x_ref[...] + (1.0 + mod_ref[0, 5:6, :]) * ffn)
            diff = xhat * lw_ref[...] + lb_ref[...] - t_ref[...]
            loss = 0.5 * jnp.sum(jnp.sum(diff * diff, axis=-1, keepdims=True), axis=0, keepdims=True) / D
            dy = diff * (1.0 / D)
            dz = _ln_bwd(dy, xhat, rstd, lw_ref[...])
            dz_ref[...] = dz
            lane = lax.broadcasted_iota(jnp.int32, (1, D), 1)
            upd = jnp.concatenate(
                [jnp.sum(dy * xhat, axis=0, keepdims=True), jnp.sum(dy, axis=0, keepdims=True),
                 jnp.where(lane == 0, loss, 0.0), jnp.zeros((5, D), F32)], axis=0)
            dmu = jnp.concatenate(
                [jnp.zeros((5, D), F32), jnp.sum(dz * ffn, axis=0, keepdims=True), jnp.zeros((2, D), F32)], axis=0)

            @pl.when(i == 0)
            def _():
                st_ref[...] = upd

            @pl.when(i > 0)
            def _():
                st_ref[...] += upd

            @pl.when(i % tpb == 0)
            def _():
                dm_ref[0] = dmu

            @pl.when(i % tpb != 0)
            def _():
                dm_ref[0] += dmu

    row = lambda: pl.BlockSpec((tm, D), lambda i, j: (i, 0))
    vec = lambda: pl.BlockSpec((1, D), lambda i, j: (0, 0))
    return pl.pallas_call(
        body, name="ffn_fwd", grid=(t // tm, nf),
        in_specs=[row(), pl.BlockSpec((1, 8, D), lambda i, j: (i // tpb, 0, 0)),
                  pl.BlockSpec((D, tf), lambda i, j: (0, j)), pl.BlockSpec((D, tf), lambda i, j: (0, j)),
                  pl.BlockSpec((tf, D), lambda i, j: (j, 0)), row(), vec(), vec()],
        out_specs=[pl.BlockSpec((tm, tf), lambda i, j: (i, j)), pl.BlockSpec((tm, tf), lambda i, j: (i, j)),
                   row(), pl.BlockSpec((8, D), lambda i, j: (0, 0)),
                   pl.BlockSpec((1, 8, D), lambda i, j: (i // tpb, 0, 0))],
        out_shape=[jax.ShapeDtypeStruct((t, DFF), BF16), jax.ShapeDtypeStruct((t, DFF), BF16),
                   jax.ShapeDtypeStruct((t, D), F32), jax.ShapeDtypeStruct((8, D), F32),
                   jax.ShapeDtypeStruct((nbatch, 8, D), F32)],
        scratch_shapes=[pltpu.VMEM((tm, D), BF16), pltpu.VMEM((tm, D), F32)],
        compiler_params=_params(("arbitrary", "arbitrary"), 48),
    )(x1, mod8, wg, wu, wd, target, ln2w, ln2b)


def _ffn_bwd(dz2, a, b, wg, wu, wd, x1, x2, u, mod8, ln1w, seq):
    t = x1.shape[0]
    tm, tf = min(512, seq), 256
    tpb = seq // tm
    nf = DFF // tf
    nbatch = t // seq

    def body(dz_ref, a_ref, b_ref, wg_ref, wu_ref, wd_ref, x1_ref, x_ref, u_ref, mod_ref, lw_ref,
             da_ref, db_ref, hm_ref, df_ref, du_ref, dxp_ref, st_ref, dm_ref, acc):
        i, j = pl.program_id(0), pl.program_id(1)

        @pl.when(j == 0)
        def _():
            df_ref[...] = ((1.0 + mod_ref[0, 5:6, :]) * dz_ref[...]).astype(BF16)
            acc[...] = jnp.zeros_like(acc)

        dhm = _dot_nt(df_ref[...], wd_ref[...])
        av = a_ref[...].astype(F32)
        bv = b_ref[...].astype(F32)
        sg = jax.nn.sigmoid(av)
        sl = av * sg
        hm_ref[...] = (sl * bv).astype(BF16)
        da = (dhm * bv * (sg * (1.0 + av * (1.0 - sg)))).astype(BF16)
        db = (dhm * sl).astype(BF16)
        da_ref[...] = da
        db_ref[...] = db
        acc[...] += _dot_nt(da, wg_ref[...]) + _dot_nt(db, wu_ref[...])

        @pl.when(j == nf - 1)
        def _():
            dh2 = acc[...]
            x1v = x1_ref[...]
            uv = u_ref[...]
            dx1 = ALPHA * dz_ref[...] + dh2 * (1.0 + mod_ref[0, 4:5, :])
            xhat, rstd = _ln_stats(ALPHA * x_ref[...] + (1.0 + mod_ref[0, 2:3, :]) * uv)
            dz1 = _ln_bwd(dx1, xhat, rstd, lw_ref[...])
            du_ref[...] = ((1.0 + mod_ref[0, 2:3, :]) * dz1).astype(BF16)
            dxp_ref[...] = ALPHA * dz1
            upd = jnp.concatenate(
                [jnp.sum(dx1 * xhat, axis=0, keepdims=True), jnp.sum(dx1, axis=0, keepdims=True),
                 jnp.zeros((6, D), F32)], axis=0)
            dmu = jnp.concatenate(
                [jnp.zeros((2, D), F32), jnp.sum(dz1 * uv, axis=0, keepdims=True),
                 jnp.sum(dh2, axis=0, keepdims=True), jnp.sum(dh2 * x1v, axis=0, keepdims=True),
                 jnp.zeros((3, D), F32)], axis=0)

            @pl.when(i == 0)
            def _():
                st_ref[...] = upd

            @pl.when(i > 0)
            def _():
                st_ref[...] += upd

            @pl.when(i % tpb == 0)
            def _():
                dm_ref[0] = dmu

            @pl.when(i % tpb != 0)
            def _():
                dm_ref[0] += dmu

    row = lambda: pl.BlockSpec((tm, D), lambda i, j: (i, 0))
    ffb = lambda: pl.BlockSpec((tm, tf), lambda i, j: (i, j))
    return pl.pallas_call(
        body, name="ffn_bwd", grid=(t // tm, nf),
        in_specs=[row(), ffb(), ffb(),
                  pl.BlockSpec((D, tf), lambda i, j: (0, j)), pl.BlockSpec((D, tf), lambda i, j: (0, j)),
                  pl.BlockSpec((tf, D), lambda i, j: (j, 0)), row(), row(), row(),
                  pl.BlockSpec((1, 8, D), lambda i, j: (i // tpb, 0, 0)), pl.BlockSpec((1, D), lambda i, j: (0, 0))],
        out_specs=[ffb(), ffb(), ffb(), row(), row(), row(), pl.BlockSpec((8, D), lambda i, j: (0, 0)),
                   pl.BlockSpec((1, 8, D), lambda i, j: (i // tpb, 0, 0))],
        out_shape=[jax.ShapeDtypeStruct((t, DFF), BF16), jax.ShapeDtypeStruct((t, DFF), BF16),
                   jax.ShapeDtypeStruct((t, DFF), BF16), jax.ShapeDtypeStruct((t, D), BF16),
                   jax.ShapeDtypeStruct((t, D), BF16), jax.ShapeDtypeStruct((t, D), F32),
                   jax.ShapeDtypeStruct((8, D), F32), jax.ShapeDtypeStruct((nbatch, 8, D), F32)],
        scratch_shapes=[pltpu.VMEM((tm, D), F32)],
        compiler_params=_params(("arbitrary", "arbitrary"), 48),
    )(dz2, a, b, wg, wu, wd, x1, x2, u, mod8, ln1w)


def _adamw_math(w, g, m, v):
    m = B1 * m + (1.0 - B1) * g
    v = B2 * v + (1.0 - B2) * (g * g)
    m_hat = m / (1.0 - B1 ** STEP)
    v_hat = v / (1.0 - B2 ** STEP)
    return -LR * (m_hat / (jnp.sqrt(v_hat) + EPS) + WD * w), m, v


def _adamw(w, g, m, v, name):
    rows, cols = w.shape
    tr = rows
    for cand in (128, 64, 32, 16, 8):
        if rows % cand == 0:
            tr = cand
            break

    def body(w_ref, g_ref, m_ref, v_ref, d_ref, mo_ref, vo_ref):
        d, mn, vn = _adamw_math(w_ref[...], g_ref[...], m_ref[...], v_ref[...])
        d_ref[...] = d
        mo_ref[...] = mn
        vo_ref[...] = vn

    spec = pl.BlockSpec((tr, cols), lambda i: (i, 0))
    return pl.pallas_call(
        body, name=name, grid=(rows // tr,), in_specs=[spec] * 4, out_specs=[spec] * 3,
        out_shape=[jax.ShapeDtypeStruct((rows, cols), F32)] * 3,
        compiler_params=_params(("parallel",), 48),
    )(w, g, m, v)


def _grad_w_ada(c_all, dmod_cols):
    def body(c_ref, d_ref, o_ref):
        c = c_ref[...]
        o_ref[...] = lax.dot_general(c * jax.nn.sigmoid(c), d_ref[...], (((0,), (0,)), ((), ())),
                                     preferred_element_type=F32, precision=HIGHEST)

    return pl.pallas_call(
        body, name="grad_w_ada", out_shape=jax.ShapeDtypeStruct((D, dmod_cols.shape[1]), F32),
        compiler_params=_params(vmem_mb=48),
    )(c_all, dmod_cols)


def _small_update(gath, w8, m8, v8):
    def body(g_ref, w_ref, m_ref, v_ref, go_ref, d_ref, mo_ref, vo_ref):
        g0 = g_ref[0, 0:1, :] + g_ref[0, 1:2, :]
        g1 = g_ref[0, 2:3, :]
        for dev in range(1, N_DEV):
            g0 = g0 + (g_ref[dev, 0:1, :] + g_ref[dev, 1:2, :])
            g1 = g1 + g_ref[dev, 2:3, :]
        w = w_ref[...]
        lb = jax.nn.sigmoid(w[1:2, O_LB0:O_LB1] - w[1:2, O_LB1:O_FOX])
        fac = lb * (1.0 - lb)
        g1 = jnp.concatenate([g1[:, :O_LB0], g1[:, O_LB0:O_LB1] * fac, -g1[:, O_LB1:O_FOX] * fac, g1[:, O_FOX:]],
                             axis=1)
        g = jnp.concatenate([g0, g1, jnp.zeros((6, SMALL_W), F32)], axis=0)
        d, mn, vn = _adamw_math(w, g, m_ref[...], v_ref[...])
        go_ref[...] = g
        d_ref[...] = d
        mo_ref[...] = mn
        vo_ref[...] = vn

    return pl.pallas_call(
        body, name="small_update", out_shape=[jax.ShapeDtypeStruct((8, SMALL_W), F32)] * 4,
        compiler_params=_params(vmem_mb=48),
    )(gath, w8, m8, v8)


def _pack_small(b_ada, ln1w, ln1b, ln2w, ln2b, norm_w, lb_logits, fox):
    row1 = jnp.concatenate([ln1w, ln1b, ln2w, ln2b, norm_w, lb_logits[0:1], lb_logits[1:2], fox,
                            jnp.zeros((1, SMALL_W - O_FOX - BH), F32)], axis=1)
    return jnp.concatenate([b_ada, row1, jnp.zeros((6, SMALL_W), F32)], axis=0)


def _unpack_small(p):
    r = p[1:2]
    lb = jnp.concatenate([r[:, O_LB0:O_LB1], r[:, O_LB1:O_FOX]], axis=0)
    return dict(b_ada=p[0:1], ln1_w=r[:, O_LN1W:O_LN1B], ln1_b=r[:, O_LN1B:O_LN2W], ln2_w=r[:, O_LN2W:O_LN2B],
                ln2_b=r[:, O_LN2B:O_NORM], hgrn_norm_w=r[:, O_NORM:O_LB0], lb_logits=lb,
                fox_f_bias=r[:, O_FOX:O_FOX + BH])


_BIG = ("w_in", "w_branch_a", "w_branch_b", "w_out", "w_ffn_gate", "w_ffn_up", "w_ffn_down")
_BIG_SHARD_SHAPES = dict(w_in=(D, NIN // 4), w_branch_a=(AW, D // 4), w_branch_b=(AW, D // 4), w_out=(D // 4, D),
                         w_ffn_gate=(D, DFF // 4), w_ffn_up=(D, DFF // 4), w_ffn_down=(DFF // 4, D))
_ROW_SHARDED = ("w_out", "w_ffn_down")


def _pack_shard(parts, dtype):
    flat = jnp.concatenate([parts[n].astype(dtype).reshape(-1) for n in _BIG])
    flat = jnp.concatenate([flat, jnp.zeros((PACK_ROWS * 1024 - flat.shape[0],), dtype)])
    return flat.reshape(PACK_ROWS, 1024)


def _unpack_shard(pack):
    flat = pack.reshape(-1)
    out, pos = {}, 0
    for n in _BIG:
        shp = _BIG_SHARD_SHAPES[n]
        size = shp[0] * shp[1]
        out[n] = flat[pos:pos + size].reshape(shp)
        pos += size
    return out


def _full_from_shards(gathered):
    per = [_unpack_shard(gathered[k]) for k in range(N_CHIPS)]
    return {n: jnp.concatenate([p[n] for p in per], axis=0 if n in _ROW_SHARDED else 1) for n in _BIG}


def _shards_from_full(full, dtype):
    packs = []
    for k in range(N_CHIPS):
        parts = {}
        for n in _BIG:
            shp = _BIG_SHARD_SHAPES[n]
            if n in _ROW_SHARDED:
                parts[n] = full[n][k * shp[0]:(k + 1) * shp[0], :]
            else:
                parts[n] = full[n][:, k * shp[1]:(k + 1) * shp[1]]
        packs.append(_pack_shard(parts, dtype))
    return jnp.stack(packs)


def kernel(x, c, w_ada, b_ada, w_in, fox_f_bias, lb_logits, hgrn_norm_w, w_branch_a, w_branch_b, w_out, ln1_w, ln1_b, w_ffn_gate, w_ffn_up, w_ffn_down, ln2_w, ln2_b, loss_target, m_w_ada, m_b_ada, m_w_in, m_fox_f_bias, m_lb_logits, m_hgrn_norm_w, m_w_branch_a, m_w_branch_b, m_w_out, m_ln1_w, m_ln1_b, m_w_ffn_gate, m_w_ffn_up, m_w_ffn_down, m_ln2_w, m_ln2_b, v_w_ada, v_b_ada, v_w_in, v_fox_f_bias, v_lb_logits, v_hgrn_norm_w, v_w_branch_a, v_w_branch_b, v_w_out, v_ln1_w, v_ln1_b, v_w_ffn_gate, v_w_ffn_up, v_w_ffn_down, v_ln2_w, v_ln2_b):
    nbatch, seq, _ = x.shape
    t = nbatch * seq
    ax, ay, ac = lax.axis_index("x"), lax.axis_index("y"), lax.axis_index("c")
    chip = 2 * ax + ay
    dev = 2 * chip + ac
    chip_arr = jnp.reshape(chip, (1,)).astype(jnp.int32)
    core_arr = jnp.reshape(ac, (1,)).astype(jnp.int32)

    shard_w = dict(w_in=w_in[0], w_branch_a=w_branch_a[0], w_branch_b=w_branch_b[0], w_out=w_out[0],
                   w_ffn_gate=w_ffn_gate[0], w_ffn_up=w_ffn_up[0], w_ffn_down=w_ffn_down[0])
    shard_m = dict(w_in=m_w_in[0], w_branch_a=m_w_branch_a[0], w_branch_b=m_w_branch_b[0], w_out=m_w_out[0],
                   w_ffn_gate=m_w_ffn_gate[0], w_ffn_up=m_w_ffn_up[0], w_ffn_down=m_w_ffn_down[0])
    shard_v = dict(w_in=v_w_in[0], w_branch_a=v_w_branch_a[0], w_branch_b=v_w_branch_b[0], w_out=v_w_out[0],
                   w_ffn_gate=v_w_ffn_gate[0], w_ffn_up=v_w_ffn_up[0], w_ffn_down=v_w_ffn_down[0])

    full = _full_from_shards(_gather_weights(_pack_shard(shard_w, BF16)))
    w_p = _permute_cols(full["w_in"])
    wba, wbb, wout = full["w_branch_a"], full["w_branch_b"], full["w_out"]
    wg, wu, wd = full["w_ffn_gate"], full["w_ffn_up"], full["w_ffn_down"]

    c8 = jnp.concatenate([c, jnp.zeros((8 - nbatch, D), F32)], axis=0)
    c_all = _allgather8(c8, "gather_c")[:, :nbatch, :].reshape(N_DEV * nbatch, D)
    ncol = w_ada.shape[2]
    b_cols = lax.dynamic_slice_in_dim(b_ada, chip * ncol, ncol, axis=1)
    mod_g = _allgather8(_mod_shard(c_all, w_ada[0], b_cols), "gather_mod")
    mod_all = jnp.concatenate([mod_g[2 * k] for k in range(N_CHIPS)], axis=1)
    mod_mine = lax.dynamic_slice_in_dim(mod_all, dev * nbatch, nbatch, axis=0)
    mod8 = jnp.concatenate([mod_mine.reshape(nbatch, 6, D), jnp.zeros((nbatch, 2, D), F32)], axis=1)

    x2 = x.reshape(t, D)
    tgt2 = loss_target.reshape(t, D)
    bias128 = jnp.concatenate([fox_f_bias, jnp.zeros((1, 128 - BH), F32)], axis=1)

    proj = _proj(x2, mod8, w_p, seq)
    ya, ckpt = _hgrn_fwd(proj, lb_logits, hgrn_norm_w, nbatch, seq)
    cum = _fox_cum(proj, bias128, nbatch, seq)
    cum8 = cum[:, :BH]
    cumq = jnp.broadcast_to(cum8.reshape(t, 4, 2, 1), (t, 4, 2, 64)).transpose(1, 0, 2, 3).reshape(4, t, 128)
    cumk = cum8.reshape(nbatch, seq, BH).transpose(0, 2, 1)
    yb, lse = _fox_fwd(proj, cumq, cumk, nbatch, seq)
    merged, u, x1 = _merge_fwd(ya, yb, proj, x2, mod8, wba, wbb, wout, ln1_w, ln1_b, seq)
    a_pre, b_pre, dz2, st2, dm2 = _ffn_fwd(x1, mod8, wg, wu, wd, tgt2, ln2_w, ln2_b, seq)
    loss = lax.psum(st2[2, 0], ("x", "y", "c"))

    da, db, hmid, dffn, du, dxp, st1, dm1 = _ffn_bwd(dz2, a_pre, b_pre, wg, wu, wd, x1, x2, u, mod8, ln1_w, seq)
    g_full = {}
    g_full["w_ffn_down"] = _tn_matmul(hmid, dffn, "dw_ffn_down", seq)
    g_full["w_ffn_gate"] = _tn_matmul(x1, da, "dw_ffn_gate", seq, mod8, (3, 4))
    g_full["w_ffn_up"] = _tn_matmul(x1, db, "dw_ffn_up", seq, mod8, (3, 4))
    g_full["w_out"] = _tn_matmul(merged, du, "dw_out", seq)
    dproj, dpa, dpb, dya, dyb = _merge_bwd(du, ya, yb, proj, wba, wbb, wout, seq)
    g_full["w_branch_a"] = _tn_matmul(ya, dpa, "dw_branch_a", seq)
    g_full["w_branch_b"] = _tn_matmul(yb, dpb, "dw_branch_b", seq)
    dproj, drs = _fox_dq(proj, cumq, cumk, lse, yb, dyb, dproj, nbatch, seq)
    dproj, dcs = _fox_dkv(proj, cumq, cumk, lse, yb, dyb, dproj, nbatch, seq)
    drs_tok = drs[:, :, ::64].transpose(1, 0, 2).reshape(t, BH)
    drs128 = jnp.concatenate([drs_tok, jnp.zeros((t, 128 - BH), F32)], axis=1)
    dcs_tok = dcs[:, :, :2, :].reshape(nbatch, BH, seq).transpose(0, 2, 1).reshape(t, BH)
    dcs128 = jnp.concatenate([dcs_tok, jnp.zeros((t, 128 - BH), F32)], axis=1)
    dproj, sm_fox = _fox_dbf(proj, bias128, drs128, dcs128, dproj, nbatch, seq)
    dproj, sm_hgrn = _hgrn_bwd(proj, dya, ckpt, lb_logits, hgrn_norm_w, dproj, nbatch, seq)
    grad_x2, dm0 = _dh_kernel(dproj, w_p, x2, dxp, mod8, seq)
    g_full["w_in"] = _unpermute_cols(_tn_matmul(x2, dproj, "dw_in", seq, mod8, (0, 1)))

    gpack = _shards_from_full(g_full, F32)
    red_c = _add_my_half(gpack, _swap_halves(gpack), core_arr)
    red = _add_chips(red_c, _scatter_chips(red_c), chip_arr)
    g_shard = _unpack_shard(_join_halves(red))

    dmod = (dm0 + dm1 + dm2)[:, :6, :].reshape(nbatch, 6 * D)
    row2 = jnp.concatenate([st1[0:1], st1[1:2], st2[0:1], st2[1:2], sm_hgrn[1:2], sm_hgrn[0:1], sm_hgrn[0:1],
                            sm_fox[0:1, :BH], jnp.zeros((1, SMALL_W - O_FOX - BH), F32)], axis=1)
    spack = jnp.concatenate([dmod, row2, jnp.zeros((8 - nbatch - 1, SMALL_W), F32)], axis=0)
    gath = _allgather8(spack, "gather_small")
    w8 = _pack_small(b_ada, ln1_w, ln1_b, ln2_w, ln2_b, hgrn_norm_w, lb_logits, fox_f_bias)
    m8 = _pack_small(m_b_ada, m_ln1_w, m_ln1_b, m_ln2_w, m_ln2_b, m_hgrn_norm_w, m_lb_logits, m_fox_f_bias)
    v8 = _pack_small(v_b_ada, v_ln1_w, v_ln1_b, v_ln2_w, v_ln2_b, v_hgrn_norm_w, v_lb_logits, v_fox_f_bias)
    sg, sd, smn, svn = (_unpack_small(p) for p in _small_update(gath, w8, m8, v8))
    dmod_all = gath[:, :nbatch, :].reshape(N_DEV * nbatch, SMALL_W)
    g_ada = _grad_w_ada(c_all, lax.dynamic_slice_in_dim(dmod_all, chip * ncol, ncol, axis=1))

    grads = dict(sg)
    deltas = dict(sd)
    new_m = dict(smn)
    new_v = dict(svn)
    grads["w_ada"] = g_ada
    deltas["w_ada"], new_m["w_ada"], new_v["w_ada"] = _adamw(w_ada[0], g_ada, m_w_ada[0], v_w_ada[0], "adamw_w_ada")
    for n in _BIG:
        grads[n] = g_shard[n]
        deltas[n], new_m[n], new_v[n] = _adamw(shard_w[n], g_shard[n], shard_m[n], shard_v[n], "adamw_" + n)

    names = ["w_ada", "b_ada", "w_in", "fox_f_bias", "lb_logits", "hgrn_norm_w", "w_branch_a", "w_branch_b", "w_out",
             "ln1_w", "ln1_b", "w_ffn_gate", "w_ffn_up", "w_ffn_down", "ln2_w", "ln2_b"]
    shapes = dict(w_ada=w_ada.shape, b_ada=b_ada.shape, w_in=w_in.shape, fox_f_bias=fox_f_bias.shape,
                  lb_logits=lb_logits.shape, hgrn_norm_w=hgrn_norm_w.shape, w_branch_a=w_branch_a.shape,
                  w_branch_b=w_branch_b.shape, w_out=w_out.shape, ln1_w=ln1_w.shape, ln1_b=ln1_b.shape,
                  w_ffn_gate=w_ffn_gate.shape, w_ffn_up=w_ffn_up.shape, w_ffn_down=w_ffn_down.shape,
                  ln2_w=ln2_w.shape, ln2_b=ln2_b.shape)
    outs = [loss, grad_x2.reshape(x.shape)]
    for group in (grads, deltas, new_m, new_v):
        outs += [group[n].reshape(shapes[n]) for n in names]
    return tuple(outs)
```

```python
import functools
import math

import jax
import jax.numpy as jnp
import numpy as np
from jax import lax
from jax.experimental import pallas as pl
from jax.experimental.pallas import tpu as pltpu

F32 = jnp.float32
BF16 = jnp.bfloat16
MESH = pl.DeviceIdType.MESH
HIGHEST = lax.Precision.HIGHEST

D = 1024
AW = 512
AH = 4
ADH = 128
BH = 8
BDH = 64
DFF = 2816
NIN = 5640
NP = 5760
N_CHIPS = 4
N_DEV = 8
SUB = 16
COL_GATES = 0
COL_BQ = 2048
COL_KV = 2560
COL_A = 3584
COL_BF = 5632
ALPHA = 2.0 ** 0.25
LN_EPS = 1e-5
RMS_EPS = 1e-6
NEG = -1e30
LR, B1, B2, EPS, WD, STEP = 0.001, 0.9, 0.999, 1e-08, 0.01, 10
PACK_ROWS = 4096
HALF = PACK_ROWS // 2
SMALL_W = 6144
O_LN1W, O_LN1B, O_LN2W, O_LN2B, O_NORM, O_LB0, O_LB1, O_FOX = 0, 1024, 2048, 3072, 4096, 4608, 5120, 5632


def _params(sem=None, vmem_mb=None):
    kw = {}
    if sem is not None:
        kw["dimension_semantics"] = sem
    if vmem_mb is not None:
        kw["vmem_limit_bytes"] = vmem_mb << 20
    return pltpu.CompilerParams(**kw)


def _dot(a, b):
    return jnp.dot(a.astype(BF16), b.astype(BF16), preferred_element_type=F32)


def _dot_nt(a, b):
    return lax.dot_general(a.astype(BF16), b.astype(BF16), (((1,), (1,)), ((), ())), preferred_element_type=F32)


def _dot_tn(a, b):
    return lax.dot_general(a.astype(BF16), b.astype(BF16), (((0,), (0,)), ((), ())), preferred_element_type=F32)


def _dot_f32(a, b):
    return jnp.dot(a, b, preferred_element_type=F32, precision=HIGHEST)


def _perm_segments():
    segs = [(3592, 5640), (2048, 2560)]
    for p in range(4):
        segs += [(2560 + 128 * p, 2688 + 128 * p), (3072 + 128 * p, 3200 + 128 * p)]
    for h in range(4):
        segs += [(128 * h + 512 * t, 128 * h + 512 * t + 128) for t in range(4)]
    segs += [(3584, 3592)]
    return segs


def _permute_cols(w):
    parts = [w[:, a:b] for a, b in _perm_segments()]
    parts.append(jnp.zeros((w.shape[0], NP - NIN), w.dtype))
    return jnp.concatenate(parts, axis=1)


def _unpermute_cols(g):
    pos, where = 0, {}
    for a, b in _perm_segments():
        where[a] = (pos, pos + b - a)
        pos += b - a
    parts = [g[:, where[a][0]:where[a][1]] for a in sorted(where)]
    return jnp.concatenate(parts, axis=1)


def _allgather8(v, name):
    rows, cols = v.shape

    def body(x_ref, out_ref, send_sems, recv_sems, local_sem):
        x, y, c = lax.axis_index("x"), lax.axis_index("y"), lax.axis_index("c")
        me, sibling = (x, y, c), (x, y, 1 - c)
        chips = [(1 - x, y), (x, 1 - y), (1 - x, 1 - y)]

        def slot(px, py, pc):
            return out_ref.at[4 * px + 2 * py + pc]

        def copy(k, block, to, src=None):
            return pltpu.make_async_remote_copy(
                src_ref=slot(*block) if src is None else src, dst_ref=slot(*block),
                send_sem=send_sems.at[k], recv_sem=recv_sems.at[k], device_id=to, device_id_type=MESH)

        mine = pltpu.make_async_copy(x_ref, slot(*me), local_sem)
        mine.start()
        first = [copy(0, me, sibling, src=x_ref)]
        first += [copy(1 + j, me, (*chip, c), src=x_ref) for j, chip in enumerate(chips)]
        for cp in first:
            cp.start()
        passed = [copy(4 + j, (*chip, c), sibling) for j, chip in enumerate(chips)]
        for j, chip in enumerate(chips):
            copy(1 + j, (*chip, c), me).wait_recv()
            passed[j].start()
        copy(0, sibling, me).wait_recv()
        for j, chip in enumerate(chips):
            copy(4 + j, (*chip, 1 - c), me).wait_recv()
        for cp in first + passed:
            cp.wait_send()
        mine.wait()

    return pl.pallas_call(
        body, name=name,
        out_shape=jax.ShapeDtypeStruct((N_DEV, rows, cols), v.dtype),
        in_specs=[pl.BlockSpec(memory_space=pltpu.VMEM)],
        out_specs=pl.BlockSpec(memory_space=pltpu.VMEM),
        scratch_shapes=[pltpu.SemaphoreType.DMA((7,)), pltpu.SemaphoreType.DMA((7,)), pltpu.SemaphoreType.DMA],
    )(v)


def _gather_weights(pack):
    def body(pack_ref, out_ref, send_sems, recv_sems, local_sem):
        x, y, c = lax.axis_index("x"), lax.axis_index("y"), lax.axis_index("c")
        sibling = (x, y, 1 - c)
        chips = [(1 - x, y), (x, 1 - y), (1 - x, 1 - y)]

        def blk(px, py, half):
            return out_ref.at[2 * px + py, pl.ds(half * HALF, HALF), :]

        def copy(k, block, to, src=None):
            return pltpu.make_async_remote_copy(
                src_ref=blk(*block) if src is None else src, dst_ref=blk(*block),
                send_sem=send_sems.at[k], recv_sem=recv_sems.at[k], device_id=to, device_id_type=MESH)

        mine = pltpu.make_async_copy(pack_ref, out_ref.at[2 * x + y], local_sem)
        mine.start()
        my_half = pack_ref.at[pl.ds(c * HALF, HALF), :]
        first = [copy(j, (x, y, c), (*chip, c), src=my_half) for j, chip in enumerate(chips)]
        for cp in first:
            cp.start()
        passed = [copy(3 + j, (*chip, c), sibling) for j, chip in enumerate(chips)]
        for j, chip in enumerate(chips):
            copy(j, (*chip, c), (x, y, c)).wait_recv()
            passed[j].start()
        for j, chip in enumerate(chips):
            copy(3 + j, (*chip, 1 - c), (x, y, c)).wait_recv()
        for cp in first + passed:
            cp.wait_send()
        mine.wait()

    return pl.pallas_call(
        body, name="gather_weights",
        out_shape=jax.ShapeDtypeStruct((N_CHIPS, PACK_ROWS, 1024), pack.dtype),
        in_specs=[pl.BlockSpec(memory_space=pl.ANY)],
        out_specs=pl.BlockSpec(memory_space=pl.ANY),
        scratch_shapes=[pltpu.SemaphoreType.DMA((6,)), pltpu.SemaphoreType.DMA((6,)), pltpu.SemaphoreType.DMA],
    )(pack)


def _swap_halves(gpack):
    def body(g_ref, out_ref, send_sem, recv_sem):
        x, y, c = lax.axis_index("x"), lax.axis_index("y"), lax.axis_index("c")
        cp = pltpu.make_async_remote_copy(
            src_ref=g_ref.at[:, pl.ds((1 - c) * HALF, HALF), :], dst_ref=out_ref,
            send_sem=send_sem, recv_sem=recv_sem, device_id=(x, y, 1 - c), device_id_type=MESH)
        cp.start()
        cp.wait()

    return pl.pallas_call(
        body, name="grad_swap_halves",
        out_shape=jax.ShapeDtypeStruct((N_CHIPS, HALF, 1024), gpack.dtype),
        in_specs=[pl.BlockSpec(memory_space=pl.ANY)],
        out_specs=pl.BlockSpec(memory_space=pl.ANY),
        scratch_shapes=[pltpu.SemaphoreType.DMA, pltpu.SemaphoreType.DMA],
    )(gpack)


def _scatter_chips(red):
    def body(r_ref, out_ref, send_sems, recv_sems):
        x, y, c = lax.axis_index("x"), lax.axis_index("y"), lax.axis_index("c")
        chips = [(1 - x, y), (x, 1 - y), (1 - x, 1 - y)]
        cps = [pltpu.make_async_remote_copy(
            src_ref=r_ref.at[2 * chip[0] + chip[1]], dst_ref=out_ref.at[j],
            send_sem=send_sems.at[j], recv_sem=recv_sems.at[j], device_id=(*chip, c), device_id_type=MESH)
            for j, chip in enumerate(chips)]
        for cp in cps:
            cp.start()
        for cp in cps:
            cp.wait()

    return pl.pallas_call(
        body, name="grad_scatter_chips",
        out_shape=jax.ShapeDtypeStruct((3, HALF, 1024), red.dtype),
        in_specs=[pl.BlockSpec(memory_space=pl.ANY)],
        out_specs=pl.BlockSpec(memory_space=pl.ANY),
        scratch_shapes=[pltpu.SemaphoreType.DMA((3,)), pltpu.SemaphoreType.DMA((3,))],
    )(red)


def _join_halves(half):
    def body(h_ref, out_ref, send_sem, recv_sem, local_sem):
        x, y, c = lax.axis_index("x"), lax.axis_index("y"), lax.axis_index("c")
        rows = out_ref.at[pl.ds(c * HALF, HALF), :]
        mine = pltpu.make_async_copy(h_ref, rows, local_sem)
        mine.start()
        cp = pltpu.make_async_remote_copy(
            src_ref=h_ref, dst_ref=rows, send_sem=send_sem, recv_sem=recv_sem,
            device_id=(x, y, 1 - c), device_id_type=MESH)
        cp.start()
        pltpu.make_async_remote_copy(
            src_ref=h_ref, dst_ref=out_ref.at[pl.ds((1 - c) * HALF, HALF), :], send_sem=send_sem,
            recv_sem=recv_sem, device_id=(x, y, 1 - c), device_id_type=MESH).wait_recv()
        cp.wait_send()
        mine.wait()

    return pl.pallas_call(
        body, name="grad_join_halves",
        out_shape=jax.ShapeDtypeStruct((PACK_ROWS, 1024), half.dtype),
        in_specs=[pl.BlockSpec(memory_space=pl.ANY)],
        out_specs=pl.BlockSpec(memory_space=pl.ANY),
        scratch_shapes=[pltpu.SemaphoreType.DMA, pltpu.SemaphoreType.DMA, pltpu.SemaphoreType.DMA],
    )(half)


def _add_my_half(gpack, other, c_idx):
    tr = 512
    nb = HALF // tr

    def body(c_ref, g_ref, o_ref, out_ref, out16_ref):
        s = g_ref[...] + o_ref[...]
        out_ref[...] = s
        out16_ref[...] = s.astype(BF16)

    return pl.pallas_call(
        body, name="grad_add_halves",
        grid_spec=pltpu.PrefetchScalarGridSpec(
            num_scalar_prefetch=1, grid=(N_CHIPS, nb),
            in_specs=[pl.BlockSpec((1, tr, 1024), lambda k, i, c: (k, c[0] * nb + i, 0)),
                      pl.BlockSpec((1, tr, 1024), lambda k, i, c: (k, i, 0))],
            out_specs=[pl.BlockSpec((1, tr, 1024), lambda k, i, c: (k, i, 0)),
                       pl.BlockSpec((1, tr, 1024), lambda k, i, c: (k, i, 0))]),
        out_shape=[jax.ShapeDtypeStruct((N_CHIPS, HALF, 1024), F32), jax.ShapeDtypeStruct((N_CHIPS, HALF, 1024), BF16)],
        compiler_params=_params(("parallel", "parallel")),
    )(c_idx, gpack, other)


def _add_chips(red, recv, chip_idx):
    tr = 512

    def body(k_ref, r_ref, v_ref, out_ref):
        out_ref[...] = ((r_ref[0] + v_ref[0].astype(F32)) + v_ref[1].astype(F32)) + v_ref[2].astype(F32)

    return pl.pallas_call(
        body, name="grad_add_chips",
        grid_spec=pltpu.PrefetchScalarGridSpec(
            num_scalar_prefetch=1, grid=(HALF // tr,),
            in_specs=[pl.BlockSpec((1, tr, 1024), lambda i, k: (k[0], i, 0)),
                      pl.BlockSpec((3, tr, 1024), lambda i, k: (0, i, 0))],
            out_specs=pl.BlockSpec((tr, 1024), lambda i, k: (i, 0))),
        out_shape=jax.ShapeDtypeStruct((HALF, 1024), F32),
        compiler_params=_params(("parallel",)),
    )(chip_idx, red, recv)


def _mod_shard(c_all, w_ada, b_ada):
    nb, cols = c_all.shape[0], w_ada.shape[1]

    def body(c_ref, w_ref, b_ref, o_ref):
        c = c_ref[...]
        o_ref[...] = _dot(c * jax.nn.sigmoid(c), w_ref[...]) + b_ref[...]

    return pl.pallas_call(
        body, name="mod_shard", out_shape=jax.ShapeDtypeStruct((nb, cols), F32),
        compiler_params=_params(vmem_mb=48),
    )(c_all, w_ada, b_ada)


def _proj(x2, mod8, w_p, seq):
    t = x2.shape[0]
    tm, tn = min(512, seq), 1152
    tpb = seq // tm

    def body(x_ref, mod_ref, w_ref, o_ref, h_scr):
        @pl.when(pl.program_id(1) == 0)
        def _():
            h_scr[...] = (x_ref[...] * (1.0 + mod_ref[0, 1:2, :]) + mod_ref[0, 0:1, :]).astype(BF16)
        o_ref[...] = jnp.dot(h_scr[...], w_ref[...], preferred_element_type=F32)

    return pl.pallas_call(
        body, name="proj", grid=(t // tm, NP // tn),
        in_specs=[pl.BlockSpec((tm, D), lambda i, j: (i, 0)),
                  pl.BlockSpec((1, 8, D), lambda i, j: (i // tpb, 0, 0)),
                  pl.BlockSpec((D, tn), lambda i, j: (0, j))],
        out_specs=pl.BlockSpec((tm, tn), lambda i, j: (i, j)),
        out_shape=jax.ShapeDtypeStruct((t, NP), F32),
        scratch_shapes=[pltpu.VMEM((tm, D), BF16)],
        compiler_params=_params(("parallel", "arbitrary"), 48),
    )(x2, mod8, w_p)


def _tn_matmul(a, b, name, seq, mod8=None, rows=None):
    t, ka = a.shape
    n = b.shape[1]
    tt = min(512, seq)
    tpb = seq // tt
    tn = n
    for cand in (1152, 1024, 1408, 512):
        if n % cand == 0:
            tn = cand
            break
    nt = t // tt

    def body(*refs):
        if mod8 is None:
            a_ref, b_ref, o_ref = refs
            av = a_ref[...]
        else:
            a_ref, m_ref, b_ref, o_ref = refs
            av = a_ref[...] * (1.0 + m_ref[0, rows[1]:rows[1] + 1, :]) + m_ref[0, rows[0]:rows[0] + 1, :]
        part = _dot_tn(av, b_ref[...])

        @pl.when(pl.program_id(1) == 0)
        def _():
            o_ref[...] = part

        @pl.when(pl.program_id(1) > 0)
        def _():
            o_ref[...] += part

    in_specs = [pl.BlockSpec((tt, ka), lambda j, k: (k, 0))]
    args = [a]
    if mod8 is not None:
        in_specs.append(pl.BlockSpec((1, 8, ka), lambda j, k: (k // tpb, 0, 0)))
        args.append(mod8)
    in_specs.append(pl.BlockSpec((tt, tn), lambda j, k: (k, j)))
    args.append(b)
    return pl.pallas_call(
        body, name=name, grid=(n // tn, nt), in_specs=in_specs,
        out_specs=pl.BlockSpec((ka, tn), lambda j, k: (0, j)),
        out_shape=jax.ShapeDtypeStruct((ka, n), F32),
        compiler_params=_params(("parallel", "arbitrary"), 48),
    )(*args)


def _dh_kernel(dproj, w_p, x2, dxp, mod8, seq):
    t = x2.shape[0]
    tm, tk = min(512, seq), 1152
    tpb = seq // tm
    nk = NP // tk
    nbatch = t // seq

    def body(dp_ref, w_ref, x_ref, dxp_ref, mod_ref, gx_ref, dm_ref, acc):
        i, k = pl.program_id(0), pl.program_id(1)

        @pl.when(k == 0)
        def _():
            acc[...] = jnp.zeros_like(acc)

        acc[...] += _dot_nt(dp_ref[...], w_ref[...])

        @pl.when(k == nk - 1)
        def _():
            dh = acc[...]
            gx_ref[...] = dxp_ref[...] + dh * (1.0 + mod_ref[0, 1:2, :])
            upd = jnp.concatenate(
                [jnp.sum(dh, axis=0, keepdims=True), jnp.sum(dh * x_ref[...], axis=0, keepdims=True),
                 jnp.zeros((6, D), F32)], axis=0)

            @pl.when(i % tpb == 0)
            def _():
                dm_ref[0] = upd

            @pl.when(i % tpb != 0)
            def _():
                dm_ref[0] += upd

    return pl.pallas_call(
        body, name="dh", grid=(t // tm, nk),
        in_specs=[pl.BlockSpec((tm, tk), lambda i, k: (i, k)),
                  pl.BlockSpec((D, tk), lambda i, k: (0, k)),
                  pl.BlockSpec((tm, D), lambda i, k: (i, 0)),
                  pl.BlockSpec((tm, D), lambda i, k: (i, 0)),
                  pl.BlockSpec((1, 8, D), lambda i, k: (i // tpb, 0, 0))],
        out_specs=[pl.BlockSpec((tm, D), lambda i, k: (i, 0)),
                   pl.BlockSpec((1, 8, D), lambda i, k: (i // tpb, 0, 0))],
        out_shape=[jax.ShapeDtypeStruct((t, D), F32), jax.ShapeDtypeStruct((nbatch, 8, D), F32)],
        scratch_shapes=[pltpu.VMEM((tm, D), F32)],
        compiler_params=_params(("arbitrary", "arbitrary"), 48),
    )(dproj, w_p, x2, dxp, mod8)


def _hgrn_gates(p, lbl):
    q, fl, v, g = p[:, 0:128], p[:, 128:256], p[:, 256:384], p[:, 384:512]
    lb = jax.nn.sigmoid(lbl[0:1, :] - lbl[1:2, :])
    sg = jax.nn.sigmoid(fl)
    f = lb + (1.0 - lb) * sg
    return q, v, g, lb, sg, f


def _group_tri(n, upper):
    r = lax.broadcasted_iota(jnp.int32, (n, n), 0)
    c = lax.broadcasted_iota(jnp.int32, (n, n), 1)
    same = (r // SUB) == (c // SUB)
    keep = (c >= r) if upper else (c <= r)
    return jnp.where(same & keep, 1.0, 0.0).astype(F32)


def _tri3():
    return (lax.broadcasted_iota(jnp.int32, (SUB, SUB, ADH), 0) >= lax.broadcasted_iota(jnp.int32, (SUB, SUB, ADH), 1))


def _hgrn_sub_fwd(qj, kj, vj, bj, st):
    bl = bj[SUB - 1:SUB, :]
    qt = qj * jnp.exp(bj)
    kt = kj * jnp.exp(bl - bj)
    e3 = jnp.exp(jnp.where(_tri3(), bj[:, None, :] - bj[None, :, :], -jnp.inf))
    a3 = jnp.sum(qj[:, None, :] * kj[None, :, :] * e3, axis=-1, keepdims=True)
    o = _dot_nt(qt, st) + jnp.sum(a3 * vj[None, :, :], axis=1)
    st_new = st * jnp.exp(bl) + _dot_tn(vj, kt)
    return o, st_new


def _hgrn_fwd(proj, lb_logits, norm_w, nbatch, seq):
    t = proj.shape[0]
    blk = min(256, seq)
    nb = seq // blk
    nsub = blk // SUB

    def body(p_ref, lbl_ref, nw_ref, y_ref, ck_ref, q_s, k_s, v_s, b_s, o_s, st_s):
        i = pl.program_id(2)

        @pl.when(i == 0)
        def _():
            st_s[...] = jnp.zeros_like(st_s)

        ck_ref[0] = st_s[...]
        q, v, g, lb, sg, f = _hgrn_gates(p_ref[...], lbl_ref[...])
        q_s[...] = q
        k_s[...] = 1.0 - f
        v_s[...] = v
        b_s[...] = _dot_f32(_group_tri(blk, False), jnp.log(f))

        def sub(j, st):
            r = pl.ds(pl.multiple_of(j * SUB, SUB), SUB)
            o, st = _hgrn_sub_fwd(q_s[r, :], k_s[r, :], v_s[r, :], b_s[r, :], st)
            o_s[r, :] = o
            return st

        st_s[...] = lax.fori_loop(0, nsub, sub, st_s[...])
        o = o_s[...]
        r = lax.rsqrt(jnp.mean(o * o, axis=-1, keepdims=True) + RMS_EPS)
        y_ref[...] = (o * r * nw_ref[...] * jax.nn.sigmoid(g)).astype(y_ref.dtype)

    return pl.pallas_call(
        body, name="hgrn_fwd", grid=(AH, nbatch, nb),
        in_specs=[pl.BlockSpec((blk, 512), lambda h, b, i: (b * nb + i, COL_A // 512 + h)),
                  pl.BlockSpec((2, 128), lambda h, b, i: (0, h)),
                  pl.BlockSpec((1, 128), lambda h, b, i: (0, h))],
        out_specs=[pl.BlockSpec((blk, 128), lambda h, b, i: (b * nb + i, h)),
                   pl.BlockSpec((1, 128, 128), lambda h, b, i: ((h * nbatch + b) * nb + i, 0, 0))],
        out_shape=[jax.ShapeDtypeStruct((t, AW), BF16), jax.ShapeDtypeStruct((AH * nbatch * nb, 128, 128), F32)],
        scratch_shapes=[pltpu.VMEM((blk, 128), F32)] * 5 + [pltpu.VMEM((128, 128), F32)],
        compiler_params=_params(("parallel", "parallel", "arbitrary")),
    )(proj, lb_logits, norm_w)


def _hgrn_bwd(proj, dya, ckpt, lb_logits, norm_w, dproj, nbatch, seq):
    t = proj.shape[0]
    blk = min(256, seq)
    nb = seq // blk
    nsub = blk // SUB

    def body(p_ref, dy_ref, ck_ref, lbl_ref, nw_ref, dp_in, dp_ref, sm_ref,
             q_s, k_s, v_s, b_s, o_s, do_s, dq_s, dk_s, dv_s, db_s, st_all, dst_s):
        del dp_in
        b_id, i = pl.program_id(1), pl.program_id(2)

        @pl.when(i == 0)
        def _():
            dst_s[...] = jnp.zeros_like(dst_s)

        q, v, g, lb, sg, f = _hgrn_gates(p_ref[...], lbl_ref[...])
        q_s[...] = q
        k_s[...] = 1.0 - f
        v_s[...] = v
        b_s[...] = _dot_f32(_group_tri(blk, False), jnp.log(f))

        def sub(j, st):
            r = pl.ds(pl.multiple_of(j * SUB, SUB), SUB)
            st_all[j] = st
            o, st = _hgrn_sub_fwd(q_s[r, :], k_s[r, :], v_s[r, :], b_s[r, :], st)
            o_s[r, :] = o
            return st

        lax.fori_loop(0, nsub, sub, ck_ref[0])

        o = o_s[...]
        dy = dy_ref[...]
        nw = nw_ref[...]
        sgo = jax.nn.sigmoid(g)
        r = lax.rsqrt(jnp.mean(o * o, axis=-1, keepdims=True) + RMS_EPS)
        on = o * r
        dg = dy * on * nw * sgo * (1.0 - sgo)
        dnw = jnp.sum(dy * on * sgo, axis=0, keepdims=True)
        dn = dy * nw * sgo
        do_s[...] = r * dn - o * (r * r * r) * jnp.mean(dn * o, axis=-1, keepdims=True)

        def bsub(jj, dstn):
            j = nsub - 1 - jj
            rr = pl.ds(pl.multiple_of(j * SUB, SUB), SUB)
            qj, kj, vj, bj, doj = q_s[rr, :], k_s[rr, :], v_s[rr, :], b_s[rr, :], do_s[rr, :]
            st = st_all[j]
            bl = bj[SUB - 1:SUB, :]
            eb = jnp.exp(bj)
            ebl = jnp.exp(bl - bj)
            e = jnp.exp(bl)
            qt = qj * eb
            kt = kj * ebl
            e3 = jnp.exp(jnp.where(_tri3(), bj[:, None, :] - bj[None, :, :], -jnp.inf))
            a3 = jnp.sum(qj[:, None, :] * kj[None, :, :] * e3, axis=-1, keepdims=True)
            da3 = jnp.sum(doj[:, None, :] * vj[None, :, :], axis=-1, keepdims=True)
            w3 = da3 * e3
            dqa = jnp.sum(w3 * kj[None, :, :], axis=1)
            dka = jnp.sum(w3 * qj[:, None, :], axis=0)
            dqt = _dot(doj, st)
            dkt = _dot(vj, dstn)
            dv_s[rr, :] = jnp.sum(a3 * doj[:, None, :], axis=0) + _dot_nt(kt, dstn)
            dq_s[rr, :] = dqa + dqt * eb
            dk_s[rr, :] = dka + dkt * ebl
            dbl = jnp.sum(dstn * st, axis=0, keepdims=True) * e + jnp.sum(dkt * kt, axis=0, keepdims=True)
            last = lax.broadcasted_iota(jnp.int32, (SUB, ADH), 0) == SUB - 1
            db_s[rr, :] = qj * dqa - kj * dka + dqt * qt - dkt * kt + jnp.where(last, dbl, 0.0)
            return dstn * e + _dot_tn(doj, qt)

        dst_s[...] = lax.fori_loop(0, nsub, bsub, dst_s[...])

        dlf = _dot_f32(_group_tri(blk, True), db_s[...])
        df = dlf / f - dk_s[...]
        dfl = df * (1.0 - lb) * sg * (1.0 - sg)
        dlb = jnp.sum(df * (1.0 - sg), axis=0, keepdims=True)
        dp_ref[:, 0:128] = dq_s[...].astype(dp_ref.dtype)
        dp_ref[:, 128:256] = dfl.astype(dp_ref.dtype)
        dp_ref[:, 256:384] = dv_s[...].astype(dp_ref.dtype)
        dp_ref[:, 384:512] = dg.astype(dp_ref.dtype)
        upd = jnp.concatenate([dlb, dnw, jnp.zeros((6, 128), F32)], axis=0)
        first = (b_id == 0) & (i == 0)

        @pl.when(first)
        def _():
            sm_ref[...] = upd

        @pl.when(jnp.logical_not(first))
        def _():
            sm_ref[...] += upd

    def rows(h, b, i):
        return b * nb + (nb - 1 - i)

    return pl.pallas_call(
        body, name="hgrn_bwd", grid=(AH, nbatch, nb),
        in_specs=[pl.BlockSpec((blk, 512), lambda h, b, i: (rows(h, b, i), COL_A // 512 + h)),
                  pl.BlockSpec((blk, 128), lambda h, b, i: (rows(h, b, i), h)),
                  pl.BlockSpec((1, 128, 128), lambda h, b, i: ((h * nbatch + b) * nb + (nb - 1 - i), 0, 0)),
                  pl.BlockSpec((2, 128), lambda h, b, i: (0, h)),
                  pl.BlockSpec((1, 128), lambda h, b, i: (0, h)),
                  pl.BlockSpec(memory_space=pl.ANY)],
        out_specs=[pl.BlockSpec((blk, 512), lambda h, b, i: (rows(h, b, i), COL_A // 512 + h)),
                   pl.BlockSpec((8, 128), lambda h, b, i: (0, h))],
        out_shape=[jax.ShapeDtypeStruct((t, NP), BF16), jax.ShapeDtypeStruct((8, AW), F32)],
        input_output_aliases={5: 0},
        scratch_shapes=[pltpu.VMEM((blk, 128), F32)] * 10 + [pltpu.VMEM((nsub, 128, 128), F32),
                                                            pltpu.VMEM((128, 128), F32)],
        compiler_params=_params(("parallel", "arbitrary", "arbitrary")),
    )(proj, dya, ckpt, lb_logits, norm_w, dproj)


def _tri(n, upper):
    r = lax.broadcasted_iota(jnp.int32, (n, n), 0)
    c = lax.broadcasted_iota(jnp.int32, (n, n), 1)
    return jnp.where((c >= r) if upper else (c <= r), 1.0, 0.0).astype(F32)


def _log_sigmoid(z):
    return jnp.minimum(z, 0.0) - jnp.log(1.0 + jnp.exp(-jnp.abs(z)))


def _fox_cum(proj, bias128, nbatch, seq):
    t = proj.shape[0]
    ts = min(512, seq)
    nb = seq // ts

    def body(p_ref, b_ref, r_ref, c_ref, carry):
        @pl.when(pl.program_id(1) == 0)
        def _():
            carry[...] = jnp.zeros_like(carry)
        cum = _dot_f32(_tri(ts, False), _log_sigmoid(p_ref[...] + b_ref[...])) + carry[...]
        carry[...] = cum[ts - 1:ts, :]
        r_ref[0] = cum.T[0:BH, :]
        lane = lax.broadcasted_iota(jnp.int32, (ts, 128), 1)
        for p in range(4):
            c_ref[p] = jnp.where(lane < 64, cum[:, 2 * p:2 * p + 1], cum[:, 2 * p + 1:2 * p + 2])

    return pl.pallas_call(
        body, name="fox_cum", grid=(nbatch, nb),
        in_specs=[pl.BlockSpec((ts, 128), lambda b, i: (b * nb + i, COL_BF // 128)),
                  pl.BlockSpec((1, 128), lambda b, i: (0, 0))],
        out_specs=[pl.BlockSpec((1, BH, ts), lambda b, i: (b, 0, i)),
                   pl.BlockSpec((4, ts, 128), lambda b, i: (0, b * nb + i, 0))],
        out_shape=[jax.ShapeDtypeStruct((nbatch, BH, seq), F32), jax.ShapeDtypeStruct((4, t, 128), F32)],
        scratch_shapes=[pltpu.VMEM((1, 128), F32)],
        compiler_params=_params(("parallel", "arbitrary")),
    )(proj, bias128)


def _fox_scores_t(q_ref, kv_ref, cr_ref, cc_ref, hh, hp, masked, tq, tk):
    kh = kv_ref[:, 64 * hh:64 * hh + 64].astype(BF16)
    qh = (q_ref[:, 64 * hh:64 * hh + 64] * (BDH ** -0.5)).astype(BF16)
    s = _dot_nt(kh, qh) + cr_ref[0, pl.ds(2 * hp + hh, 1), :] - cc_ref[0, :, 64 * hh:64 * hh + 1]
    if masked:
        key = lax.broadcasted_iota(jnp.int32, (tk, tq), 0)
        qry = lax.broadcasted_iota(jnp.int32, (tk, tq), 1)
        s = jnp.where(key <= qry, s, NEG)
    return s, kh, qh


def _with_ones_lane(x128, hh):
    lane = lax.broadcasted_iota(jnp.int32, x128.shape, 1)
    one = jnp.ones_like(x128)
    zero = jnp.zeros_like(x128)
    if hh == 0:
        return jnp.where(lane < 64, x128, jnp.where(lane == 64, one, zero))
    return jnp.where(lane >= 64, x128, jnp.where(lane == 0, one, zero))


def _fox_fwd(proj, cum_rows, cum_cols, nbatch, seq):
    t = proj.shape[0]
    tq = tk = min(512, seq)
    nq = seq // tq

    def body(q_ref, kv_ref, cr_ref, cc_ref, o_ref, lse_ref, m_s, acc_s):
        hp, i, j = pl.program_id(1), pl.program_id(2), pl.program_id(3)

        @pl.when(j == 0)
        def _():
            m_s[...] = jnp.full_like(m_s, NEG)
            acc_s[...] = jnp.zeros_like(acc_s)

        def step(masked):
            for hh in range(2):
                s, _, _ = _fox_scores_t(q_ref, kv_ref, cr_ref, cc_ref, hh, hp, masked, tq, tk)
                m_prev = m_s[hh:hh + 1, :]
                m_new = jnp.maximum(m_prev, jnp.max(s, axis=0, keepdims=True))
                alpha = jnp.exp(m_prev - m_new)
                p = jnp.exp(s - m_new).astype(BF16)
                v_aug = _with_ones_lane(kv_ref[:, 128:256].astype(BF16), hh)
                acc_s[hh] = acc_s[hh] * alpha + _dot_tn(v_aug, p)
                m_s[hh:hh + 1, :] = m_new

        @pl.when(j < i)
        def _():
            step(False)

        @pl.when(j == i)
        def _():
            step(True)
            a0, a1 = acc_s[0], acc_s[1]
            l0, l1 = a0[64:65, :], a1[0:1, :]
            o_t = jnp.concatenate([a0[0:64, :] / l0, a1[64:128, :] / l1], axis=0)
            o_ref[...] = o_t.T.astype(o_ref.dtype)
            lse_ref[0, 0] = jnp.concatenate(
                [m_s[0:1, :] + jnp.log(l0), m_s[1:2, :] + jnp.log(l1), jnp.zeros((6, tq), F32)], axis=0)

    return pl.pallas_call(
        body, name="fox_fwd", grid=(nbatch, 4, nq, nq),
        in_specs=[pl.BlockSpec((tq, 128), lambda b, p, i, j: (b * nq + i, COL_BQ // 128 + p)),
                  pl.BlockSpec((tk, 256), lambda b, p, i, j: (b * nq + jnp.minimum(j, i), COL_KV // 256 + p)),
                  pl.BlockSpec((1, BH, tq), lambda b, p, i, j: (b, 0, i)),
                  pl.BlockSpec((1, tk, 128), lambda b, p, i, j: (p, b * nq + jnp.minimum(j, i), 0))],
        out_specs=[pl.BlockSpec((tq, 128), lambda b, p, i, j: (b * nq + i, p)),
                   pl.BlockSpec((1, 1, 8, tq), lambda b, p, i, j: (b, p, 0, i))],
        out_shape=[jax.ShapeDtypeStruct((t, 512), BF16), jax.ShapeDtypeStruct((nbatch, 4, 8, seq), F32)],
        scratch_shapes=[pltpu.VMEM((8, tq), F32), pltpu.VMEM((2, 128, tq), F32)],
        compiler_params=_params(("parallel", "parallel", "parallel", "arbitrary"), 48),
    )(proj, proj, cum_rows, cum_cols)


def _fox_bwd(proj, cum_rows, cum_cols, lse, yb, dyb, dproj, nbatch, seq):
    t = proj.shape[0]
    tq = tk = min(512, seq)
    nq = seq // tq
    scale = BDH ** -0.5

    def body(q_ref, kv_ref, cr_ref, cc_ref, lse_ref, o_ref, do_ref, dp_in,
             dkv_ref, dq_ref, drs_ref, dcs_ref, dkv_s, dqa_s, dcs_s):
        del dp_in
        hp, j, ii = pl.program_id(1), pl.program_id(2), pl.program_id(3)
        i = j + ii

        @pl.when(ii == 0)
        def _():
            dkv_s[...] = jnp.zeros_like(dkv_s)
            dcs_s[...] = jnp.zeros_like(dcs_s)

        @pl.when((j == 0) & (ii == 0))
        def _():
            dqa_s[...] = jnp.zeros_like(dqa_s)

        def step(masked):
            lane = lax.broadcasted_iota(jnp.int32, (tq, 128), 1)
            for hh in range(2):
                s, kh, qh = _fox_scores_t(q_ref, kv_ref, cr_ref, cc_ref, hh, hp, masked, tq, tk)
                p = jnp.exp(s - lse_ref[0, 0, hh:hh + 1, :])
                doh = do_ref[:, 64 * hh:64 * hh + 64]
                dd = lax.dot_general(jnp.ones((8, 64), F32), doh * o_ref[:, 64 * hh:64 * hh + 64].astype(F32),
                                     (((1,), (1,)), ((), ())), preferred_element_type=F32, precision=HIGHEST)[0:1, :]
                doh = doh.astype(BF16)
                dp = _dot_nt(kv_ref[:, 128 + 64 * hh:192 + 64 * hh], doh)
                ds = (p * (dp - dd)).astype(BF16)
                dkv_s[:, 128 + 64 * hh:192 + 64 * hh] += _dot(p, doh)
                dkv_s[:, 64 * hh:64 * hh + 64] += _dot(ds, qh)
                k_aug = _with_ones_lane(kv_ref[:, 0:128].astype(BF16), hh)
                dqa_s[i, hh] += _dot_tn(k_aug, ds)
                sel = jnp.where(lane == 2 * hp + hh, 1.0, 0.0).astype(BF16)
                dcs_s[...] += _dot(ds, sel)

        @pl.when((ii == 0) & (i < nq))
        def _():
            step(True)

        @pl.when((ii > 0) & (i < nq))
        def _():
            step(False)

        @pl.when(ii == nq - 1)
        def _():
            dkv_ref[...] = dkv_s[...].astype(dkv_ref.dtype)
            dcs_ref[0] = dcs_s[...]

        @pl.when((j == nq - 1) & (ii == nq - 1))
        def _():
            lane = lax.broadcasted_iota(jnp.int32, (tq, 128), 1)
            for qi in range(nq):
                a0 = dqa_s[qi, 0].T
                a1 = dqa_s[qi, 1].T
                rows = pl.ds(qi * tq, tq)
                dq_ref[rows, :] = (jnp.where(lane < 64, a0, a1) * scale).astype(dq_ref.dtype)
                drs_ref[0, rows, :] = jnp.where(lane == 2 * hp, a0[:, 64:65], jnp.where(lane == 2 * hp + 1, a1[:, 0:1], 0.0))

    def qrow(b, p, j, ii):
        return b * nq + jnp.minimum(j + ii, nq - 1)

    def qblk(b, p, j, ii):
        return jnp.minimum(j + ii, nq - 1)

    return pl.pallas_call(
        body, name="fox_bwd", grid=(nbatch, 4, nq, nq),
        in_specs=[pl.BlockSpec((tq, 128), lambda b, p, j, ii: (qrow(b, p, j, ii), COL_BQ // 128 + p)),
                  pl.BlockSpec((tk, 256), lambda b, p, j, ii: (b * nq + j, COL_KV // 256 + p)),
                  pl.BlockSpec((1, BH, tq), lambda b, p, j, ii: (b, 0, qblk(b, p, j, ii))),
                  pl.BlockSpec((1, tk, 128), lambda b, p, j, ii: (p, b * nq + j, 0)),
                  pl.BlockSpec((1, 1, 8, tq), lambda b, p, j, ii: (b, p, 0, qblk(b, p, j, ii))),
                  pl.BlockSpec((tq, 128), lambda b, p, j, ii: (qrow(b, p, j, ii), p)),
                  pl.BlockSpec((tq, 128), lambda b, p, j, ii: (qrow(b, p, j, ii), p)),
                  pl.BlockSpec(memory_space=pl.ANY)],
        out_specs=[pl.BlockSpec((tk, 256), lambda b, p, j, ii: (b * nq + j, COL_KV // 256 + p)),
                   pl.BlockSpec((seq, 128), lambda b, p, j, ii: (b, p)),
                   pl.BlockSpec((1, seq, 128), lambda b, p, j, ii: (p, b, 0)),
                   pl.BlockSpec((1, tk, 128), lambda b, p, j, ii: (p, b * nq + j, 0))],
        out_shape=[jax.ShapeDtypeStruct((t, NP), BF16), jax.ShapeDtypeStruct((t, 512), BF16),
                   jax.ShapeDtypeStruct((4, t, 128), F32), jax.ShapeDtypeStruct((4, t, 128), F32)],
        input_output_aliases={7: 0},
        scratch_shapes=[pltpu.VMEM((tk, 256), F32), pltpu.VMEM((nq, 2, 128, tq), F32), pltpu.VMEM((tk, 128), F32)],
        compiler_params=_params(("parallel", "parallel", "arbitrary", "arbitrary"), 56),
    )(proj, proj, cum_rows, cum_cols, lse, yb, dyb, dproj)


def _place_cols(dproj, src, col):
    t, w = src.shape
    tm = 1024 if t % 1024 == 0 else t

    def body(s_ref, dp_in, o_ref):
        del dp_in
        o_ref[...] = s_ref[...]

    return pl.pallas_call(
        body, name="place_cols", grid=(t // tm,),
        in_specs=[pl.BlockSpec((tm, w), lambda i: (i, 0)), pl.BlockSpec(memory_space=pl.ANY)],
        out_specs=pl.BlockSpec((tm, w), lambda i: (i, col // w)),
        out_shape=jax.ShapeDtypeStruct(dproj.shape, dproj.dtype),
        input_output_aliases={1: 0},
        compiler_params=_params(("parallel",)),
    )(src, dproj)


def _fox_dbf(proj, bias128, drs, dcs, dproj, nbatch, seq):
    t = proj.shape[0]
    ts = min(512, seq)
    nb = seq // ts

    def body(p_ref, b_ref, dr_ref, dc_ref, dp_in, dp_ref, sm_ref, carry):
        del dp_in
        b_id, i = pl.program_id(0), pl.program_id(1)

        @pl.when(i == 0)
        def _():
            carry[...] = jnp.zeros_like(carry)

        dcum = (dr_ref[0] - dc_ref[0]) + (dr_ref[1] - dc_ref[1]) + (dr_ref[2] - dc_ref[2]) + (dr_ref[3] - dc_ref[3])
        rc = _dot_f32(_tri(ts, True), dcum) + carry[...]
        carry[...] = rc[0:1, :]
        z = p_ref[...] + b_ref[...]
        lane = lax.broadcasted_iota(jnp.int32, (ts, 128), 1)
        dz = jnp.where(lane < BH, rc * jax.nn.sigmoid(-z), 0.0)
        dp_ref[...] = dz.astype(dp_ref.dtype)
        upd = jnp.concatenate([jnp.sum(dz, axis=0, keepdims=True), jnp.zeros((7, 128), F32)], axis=0)
        first = (b_id == 0) & (i == 0)

        @pl.when(first)
        def _():
            sm_ref[...] = upd

        @pl.when(jnp.logical_not(first))
        def _():
            sm_ref[...] += upd

    def rows(b, i):
        return b * nb + (nb - 1 - i)

    return pl.pallas_call(
        body, name="fox_dbf", grid=(nbatch, nb),
        in_specs=[pl.BlockSpec((ts, 128), lambda b, i: (rows(b, i), COL_BF // 128)),
                  pl.BlockSpec((1, 128), lambda b, i: (0, 0)),
                  pl.BlockSpec((4, ts, 128), lambda b, i: (0, rows(b, i), 0)),
                  pl.BlockSpec((4, ts, 128), lambda b, i: (0, rows(b, i), 0)),
                  pl.BlockSpec(memory_space=pl.ANY)],
        out_specs=[pl.BlockSpec((ts, 128), lambda b, i: (rows(b, i), COL_BF // 128)),
                   pl.BlockSpec((8, 128), lambda b, i: (0, 0))],
        out_shape=[jax.ShapeDtypeStruct((t, NP), BF16), jax.ShapeDtypeStruct((8, 128), F32)],
        input_output_aliases={4: 0},
        scratch_shapes=[pltpu.VMEM((1, 128), F32)],
        compiler_params=_params(("arbitrary", "arbitrary")),
    )(proj, bias128, drs, dcs, dproj)


def _ln_stats(z):
    mu = jnp.mean(z, axis=-1, keepdims=True)
    zc = z - mu
    rstd = lax.rsqrt(jnp.mean(zc * zc, axis=-1, keepdims=True) + LN_EPS)
    return zc * rstd, rstd


def _ln_bwd(dy, xhat, rstd, w):
    dxh = dy * w
    return rstd * (dxh - jnp.mean(dxh, axis=-1, keepdims=True) - xhat * jnp.mean(dxh * xhat, axis=-1, keepdims=True))


def _merge_fwd(ya, yb, proj, x2, mod8, wba, wbb, wout, ln1w, ln1b, seq):
    t = x2.shape[0]
    tm = min(256, seq)
    tpb = seq // tm

    def body(ya_ref, yb_ref, g_ref, x_ref, mod_ref, wa_ref, wb_ref, wo_ref, lw_ref, lb_ref, mg_ref, u_ref, x1_ref):
        ga = jax.nn.sigmoid(g_ref[:, 0:D])
        gb = jax.nn.sigmoid(g_ref[:, D:2 * D])
        merged = (ga * jnp.dot(ya_ref[...], wa_ref[...], preferred_element_type=F32)
                  + gb * jnp.dot(yb_ref[...], wb_ref[...], preferred_element_type=F32))
        mg = merged.astype(BF16)
        mg_ref[...] = mg
        u = jnp.dot(mg, wo_ref[...], preferred_element_type=F32)
        u_ref[...] = u
        xhat, _ = _ln_stats(ALPHA * x_ref[...] + (1.0 + mod_ref[0, 2:3, :]) * u)
        x1_ref[...] = xhat * lw_ref[...] + lb_ref[...]

    tok = lambda w: pl.BlockSpec((tm, w), lambda i: (i, 0))
    full = lambda a: pl.BlockSpec(a.shape, lambda i: (0,) * a.ndim)
    return pl.pallas_call(
        body, name="merge_fwd", grid=(t // tm,),
        in_specs=[tok(512), tok(512), pl.BlockSpec((tm, 2048), lambda i: (i, COL_GATES // 2048)), tok(D),
                  pl.BlockSpec((1, 8, D), lambda i: (i // tpb, 0, 0)),
                  full(wba), full(wbb), full(wout), full(ln1w), full(ln1b)],
        out_specs=[tok(D), tok(D), tok(D)],
        out_shape=[jax.ShapeDtypeStruct((t, D), BF16), jax.ShapeDtypeStruct((t, D), F32),
                   jax.ShapeDtypeStruct((t, D), F32)],
        compiler_params=_params(("parallel",), 48),
    )(ya, yb, proj, x2, mod8, wba, wbb, wout, ln1w, ln1b)


def _merge_bwd(du, ya, yb, proj, wba, wbb, wout, seq):
    t = du.shape[0]
    tm = min(256, seq)

    def body(du_ref, ya_ref, yb_ref, g_ref, wa_ref, wb_ref, wo_ref, dp_ref, dpa_ref, dpb_ref, dya_ref, dyb_ref):
        ga = jax.nn.sigmoid(g_ref[:, 0:D])
        gb = jax.nn.sigmoid(g_ref[:, D:2 * D])
        dm = _dot_nt(du_ref[...], wo_ref[...])
        pa = jnp.dot(ya_ref[...], wa_ref[...], preferred_element_type=F32)
        pb = jnp.dot(yb_ref[...], wb_ref[...], preferred_element_type=F32)
        dpa = (dm * ga).astype(BF16)
        dpb = (dm * gb).astype(BF16)
        dpa_ref[...] = dpa
        dpb_ref[...] = dpb
        dp_ref[:, 0:D] = (dm * pa * ga * (1.0 - ga)).astype(BF16)
        dp_ref[:, D:2 * D] = (dm * pb * gb * (1.0 - gb)).astype(BF16)
        dya_ref[...] = _dot_nt(dpa, wa_ref[...])
        dyb_ref[...] = _dot_nt(dpb, wb_ref[...])

    tok = lambda w: pl.BlockSpec((tm, w), lambda i: (i, 0))
    full = lambda a: pl.BlockSpec(a.shape, lambda i: (0,) * a.ndim)
    return pl.pallas_call(
        body, name="merge_bwd", grid=(t // tm,),
        in_specs=[tok(D), tok(512), tok(512), pl.BlockSpec((tm, 2048), lambda i: (i, COL_GATES // 2048)),
                  full(wba), full(wbb), full(wout)],
        out_specs=[pl.BlockSpec((tm, 2048), lambda i: (i, COL_GATES // 2048)), tok(D), tok(D), tok(512), tok(512)],
        out_shape=[jax.ShapeDtypeStruct((t, NP), BF16), jax.ShapeDtypeStruct((t, D), BF16),
                   jax.ShapeDtypeStruct((t, D), BF16), jax.ShapeDtypeStruct((t, 512), F32),
                   jax.ShapeDtypeStruct((t, 512), F32)],
        compiler_params=_params(("parallel",), 48),
    )(du, ya, yb, proj, wba, wbb, wout)


def _ffn_fwd(x1, mod8, wg, wu, wd, target, ln2w, ln2b, seq):
    t = x1.shape[0]
    tm, tf = min(512, seq), 256
    tpb = seq // tm
    nf = DFF // tf
    nbatch = t // seq

    def body(x_ref, mod_ref, wg_ref, wu_ref, wd_ref, t_ref, lw_ref, lb_ref,
             a_ref, b_ref, dz_ref, st_ref, dm_ref, h_s, acc):
        i, j = pl.program_id(0), pl.program_id(1)

        @pl.when(j == 0)
        def _():
            h_s[...] = (x_ref[...] * (1.0 + mod_ref[0, 4:5, :]) + mod_ref[0, 3:4, :]).astype(BF16)
            acc[...] = jnp.zeros_like(acc)

        a = jnp.dot(h_s[...], wg_ref[...], preferred_element_type=F32)
        b = jnp.dot(h_s[...], wu_ref[...], preferred_element_type=F32)
        a_ref[...] = a.astype(BF16)
        b_ref[...] = b.astype(BF16)
        acc[...] += _dot(a * jax.nn.sigmoid(a) * b, wd_ref[...])

        @pl.when(j == nf - 1)
        def _():
            ffn = acc[...]
            xhat, rstd = _ln_stats(ALPHA * x_ref[...] + (1.0 + mod_ref[0, 5:6, :]) * ffn)
            diff = xhat * lw_ref[...] + lb_ref[...] - t_ref[...]
            loss = 0.5 * jnp.sum(jnp.sum(diff * diff, axis=-1, keepdims=True), axis=0, keepdims=True) / D
            dy = diff * (1.0 / D)
            dz = _ln_bwd(dy, xhat, rstd, lw_ref[...])
            dz_ref[...] = dz
            lane = lax.broadcasted_iota(jnp.int32, (1, D), 1)
            upd = jnp.concatenate(
                [jnp.sum(dy * xhat, axis=0, keepdims=True), jnp.sum(dy, axis=0, keepdims=True),
                 jnp.where(lane == 0, loss, 0.0), jnp.zeros((5, D), F32)], axis=0)
            dmu = jnp.concatenate(
                [jnp.zeros((5, D), F32), jnp.sum(dz * ffn, axis=0, keepdims=True), jnp.zeros((2, D), F32)], axis=0)

            @pl.when(i == 0)
            def _():
                st_ref[...] = upd

            @pl.when(i > 0)
            def _():
                st_ref[...] += upd

            @pl.when(i % tpb == 0)
            def _():
                dm_ref[0] = dmu

            @pl.when(i % tpb != 0)
            def _():
                dm_ref[0] += dmu

    row = lambda: pl.BlockSpec((tm, D), lambda i, j: (i, 0))
    vec = lambda: pl.BlockSpec((1, D), lambda i, j: (0, 0))
    return pl.pallas_call(
        body, name="ffn_fwd", grid=(t // tm, nf),
        in_specs=[row(), pl.BlockSpec((1, 8, D), lambda i, j: (i // tpb, 0, 0)),
                  pl.BlockSpec((D, tf), lambda i, j: (0, j)), pl.BlockSpec((D, tf), lambda i, j: (0, j)),
                  pl.BlockSpec((tf, D), lambda i, j: (j, 0)), row(), vec(), vec()],
        out_specs=[pl.BlockSpec((tm, tf), lambda i, j: (i, j)), pl.BlockSpec((tm, tf), lambda i, j: (i, j)),
                   row(), pl.BlockSpec((8, D), lambda i, j: (0, 0)),
                   pl.BlockSpec((1, 8, D), lambda i, j: (i // tpb, 0, 0))],
        out_shape=[jax.ShapeDtypeStruct((t, DFF), BF16), jax.ShapeDtypeStruct((t, DFF), BF16),
                   jax.ShapeDtypeStruct((t, D), F32), jax.ShapeDtypeStruct((8, D), F32),
                   jax.ShapeDtypeStruct((nbatch, 8, D), F32)],
        scratch_shapes=[pltpu.VMEM((tm, D), BF16), pltpu.VMEM((tm, D), F32)],
        compiler_params=_params(("arbitrary", "arbitrary"), 48),
    )(x1, mod8, wg, wu, wd, target, ln2w, ln2b)


def _ffn_bwd(dz2, a, b, wg, wu, wd, x1, x2, u, mod8, ln1w, seq):
    t = x1.shape[0]
    tm, tf = min(512, seq), 256
    tpb = seq // tm
    nf = DFF // tf
    nbatch = t // seq

    def body(dz_ref, a_ref, b_ref, wg_ref, wu_ref, wd_ref, x1_ref, x_ref, u_ref, mod_ref, lw_ref,
             da_ref, db_ref, hm_ref, df_ref, du_ref, dxp_ref, st_ref, dm_ref, acc):
        i, j = pl.program_id(0), pl.program_id(1)

        @pl.when(j == 0)
        def _():
            df_ref[...] = ((1.0 + mod_ref[0, 5:6, :]) * dz_ref[...]).astype(BF16)
            acc[...] = jnp.zeros_like(acc)

        dhm = _dot_nt(df_ref[...], wd_ref[...])
        av = a_ref[...].astype(F32)
        bv = b_ref[...].astype(F32)
        sg = jax.nn.sigmoid(av)
        sl = av * sg
        hm_ref[...] = (sl * bv).astype(BF16)
        da = (dhm * bv * (sg * (1.0 + av * (1.0 - sg)))).astype(BF16)
        db = (dhm * sl).astype(BF16)
        da_ref[...] = da
        db_ref[...] = db
        acc[...] += _dot_nt(da, wg_ref[...]) + _dot_nt(db, wu_ref[...])

        @pl.when(j == nf - 1)
        def _():
            dh2 = acc[...]
            x1v = x1_ref[...]
            uv = u_ref[...]
            dx1 = ALPHA * dz_ref[...] + dh2 * (1.0 + mod_ref[0, 4:5, :])
            xhat, rstd = _ln_stats(ALPHA * x_ref[...] + (1.0 + mod_ref[0, 2:3, :]) * uv)
            dz1 = _ln_bwd(dx1, xhat, rstd, lw_ref[...])
            du_ref[...] = ((1.0 + mod_ref[0, 2:3, :]) * dz1).astype(BF16)
            dxp_ref[...] = ALPHA * dz1
            upd = jnp.concatenate(
                [jnp.sum(dx1 * xhat, axis=0, keepdims=True), jnp.sum(dx1, axis=0, keepdims=True),
                 jnp.zeros((6, D), F32)], axis=0)
            dmu = jnp.concatenate(
                [jnp.zeros((2, D), F32), jnp.sum(dz1 * uv, axis=0, keepdims=True),
                 jnp.sum(dh2, axis=0, keepdims=True), jnp.sum(dh2 * x1v, axis=0, keepdims=True),
                 jnp.zeros((3, D), F32)], axis=0)

            @pl.when(i == 0)
            def _():
                st_ref[...] = upd

            @pl.when(i > 0)
            def _():
                st_ref[...] += upd

            @pl.when(i % tpb == 0)
            def _():
                dm_ref[0] = dmu

            @pl.when(i % tpb != 0)
            def _():
                dm_ref[0] += dmu

    row = lambda: pl.BlockSpec((tm, D), lambda i, j: (i, 0))
    ffb = lambda: pl.BlockSpec((tm, tf), lambda i, j: (i, j))
    return pl.pallas_call(
        body, name="ffn_bwd", grid=(t // tm, nf),
        in_specs=[row(), ffb(), ffb(),
                  pl.BlockSpec((D, tf), lambda i, j: (0, j)), pl.BlockSpec((D, tf), lambda i, j: (0, j)),
                  pl.BlockSpec((tf, D), lambda i, j: (j, 0)), row(), row(), row(),
                  pl.BlockSpec((1, 8, D), lambda i, j: (i // tpb, 0, 0)), pl.BlockSpec((1, D), lambda i, j: (0, 0))],
        out_specs=[ffb(), ffb(), ffb(), row(), row(), row(), pl.BlockSpec((8, D), lambda i, j: (0, 0)),
                   pl.BlockSpec((1, 8, D), lambda i, j: (i // tpb, 0, 0))],
        out_shape=[jax.ShapeDtypeStruct((t, DFF), BF16), jax.ShapeDtypeStruct((t, DFF), BF16),
                   jax.ShapeDtypeStruct((t, DFF), BF16), jax.ShapeDtypeStruct((t, D), BF16),
                   jax.ShapeDtypeStruct((t, D), BF16), jax.ShapeDtypeStruct((t, D), F32),
                   jax.ShapeDtypeStruct((8, D), F32), jax.ShapeDtypeStruct((nbatch, 8, D), F32)],
        scratch_shapes=[pltpu.VMEM((tm, D), F32)],
        compiler_params=_params(("arbitrary", "arbitrary"), 48),
    )(dz2, a, b, wg, wu, wd, x1, x2, u, mod8, ln1w)


def _adamw_math(w, g, m, v):
    m = B1 * m + (1.0 - B1) * g
    v = B2 * v + (1.0 - B2) * (g * g)
    m_hat = m / (1.0 - B1 ** STEP)
    v_hat = v / (1.0 - B2 ** STEP)
    return -LR * (m_hat / (jnp.sqrt(v_hat) + EPS) + WD * w), m, v


def _adamw(w, g, m, v, name):
    rows, cols = w.shape
    tr = rows
    for cand in (128, 64, 32, 16, 8):
        if rows % cand == 0:
            tr = cand
            break

    def body(w_ref, g_ref, m_ref, v_ref, d_ref, mo_ref, vo_ref):
        d, mn, vn = _adamw_math(w_ref[...], g_ref[...], m_ref[...], v_ref[...])
        d_ref[...] = d
        mo_ref[...] = mn
        vo_ref[...] = vn

    spec = pl.BlockSpec((tr, cols), lambda i: (i, 0))
    return pl.pallas_call(
        body, name=name, grid=(rows // tr,), in_specs=[spec] * 4, out_specs=[spec] * 3,
        out_shape=[jax.ShapeDtypeStruct((rows, cols), F32)] * 3,
        compiler_params=_params(("parallel",), 48),
    )(w, g, m, v)


def _grad_w_ada(c_all, dmod_cols):
    def body(c_ref, d_ref, o_ref):
        c = c_ref[...]
        o_ref[...] = lax.dot_general(c * jax.nn.sigmoid(c), d_ref[...], (((0,), (0,)), ((), ())),
                                     preferred_element_type=F32, precision=HIGHEST)

    return pl.pallas_call(
        body, name="grad_w_ada", out_shape=jax.ShapeDtypeStruct((D, dmod_cols.shape[1]), F32),
        compiler_params=_params(vmem_mb=48),
    )(c_all, dmod_cols)


def _small_update(gath, w8, m8, v8):
    def body(g_ref, w_ref, m_ref, v_ref, go_ref, d_ref, mo_ref, vo_ref):
        g0 = g_ref[0, 0:1, :] + g_ref[0, 1:2, :]
        g1 = g_ref[0, 2:3, :]
        for dev in range(1, N_DEV):
            g0 = g0 + (g_ref[dev, 0:1, :] + g_ref[dev, 1:2, :])
            g1 = g1 + g_ref[dev, 2:3, :]
        w = w_ref[...]
        lb = jax.nn.sigmoid(w[1:2, O_LB0:O_LB1] - w[1:2, O_LB1:O_FOX])
        fac = lb * (1.0 - lb)
        g1 = jnp.concatenate([g1[:, :O_LB0], g1[:, O_LB0:O_LB1] * fac, -g1[:, O_LB1:O_FOX] * fac, g1[:, O_FOX:]],
                             axis=1)
        g = jnp.concatenate([g0, g1, jnp.zeros((6, SMALL_W), F32)], axis=0)
        d, mn, vn = _adamw_math(w, g, m_ref[...], v_ref[...])
        go_ref[...] = g
        d_ref[...] = d
        mo_ref[...] = mn
        vo_ref[...] = vn

    return pl.pallas_call(
        body, name="small_update", out_shape=[jax.ShapeDtypeStruct((8, SMALL_W), F32)] * 4,
        compiler_params=_params(vmem_mb=48),
    )(gath, w8, m8, v8)


def _pack_small(b_ada, ln1w, ln1b, ln2w, ln2b, norm_w, lb_logits, fox):
    row1 = jnp.concatenate([ln1w, ln1b, ln2w, ln2b, norm_w, lb_logits[0:1], lb_logits[1:2], fox,
                            jnp.zeros((1, SMALL_W - O_FOX - BH), F32)], axis=1)
    return jnp.concatenate([b_ada, row1, jnp.zeros((6, SMALL_W), F32)], axis=0)


def _unpack_small(p):
    r = p[1:2]
    lb = jnp.concatenate([r[:, O_LB0:O_LB1], r[:, O_LB1:O_FOX]], axis=0)
    return dict(b_ada=p[0:1], ln1_w=r[:, O_LN1W:O_LN1B], ln1_b=r[:, O_LN1B:O_LN2W], ln2_w=r[:, O_LN2W:O_LN2B],
                ln2_b=r[:, O_LN2B:O_NORM], hgrn_norm_w=r[:, O_NORM:O_LB0], lb_logits=lb,
                fox_f_bias=r[:, O_FOX:O_FOX + BH])


_BIG = ("w_in", "w_branch_a", "w_branch_b", "w_out", "w_ffn_gate", "w_ffn_up", "w_ffn_down")
_BIG_SHARD_SHAPES = dict(w_in=(D, NIN // 4), w_branch_a=(AW, D // 4), w_branch_b=(AW, D // 4), w_out=(D // 4, D),
                         w_ffn_gate=(D, DFF // 4), w_ffn_up=(D, DFF // 4), w_ffn_down=(DFF // 4, D))
_ROW_SHARDED = ("w_out", "w_ffn_down")


def _pack_shard(parts, dtype):
    flat = jnp.concatenate([parts[n].astype(dtype).reshape(-1) for n in _BIG])
    flat = jnp.concatenate([flat, jnp.zeros((PACK_ROWS * 1024 - flat.shape[0],), dtype)])
    return flat.reshape(PACK_ROWS, 1024)


def _unpack_shard(pack):
    flat = pack.reshape(-1)
    out, pos = {}, 0
    for n in _BIG:
        shp = _BIG_SHARD_SHAPES[n]
        size = shp[0] * shp[1]
        out[n] = flat[pos:pos + size].reshape(shp)
        pos += size
    return out


def _full_from_shards(gathered):
    per = [_unpack_shard(gathered[k]) for k in range(N_CHIPS)]
    return {n: jnp.concatenate([p[n] for p in per], axis=0 if n in _ROW_SHARDED else 1) for n in _BIG}


def _shards_from_full(full, dtype):
    packs = []
    for k in range(N_CHIPS):
        parts = {}
        for n in _BIG:
            shp = _BIG_SHARD_SHAPES[n]
            if n in _ROW_SHARDED:
                parts[n] = full[n][k * shp[0]:(k + 1) * shp[0], :]
            else:
                parts[n] = full[n][:, k * shp[1]:(k + 1) * shp[1]]
        packs.append(_pack_shard(parts, dtype))
    return jnp.stack(packs)


def kernel(x, c, w_ada, b_ada, w_in, fox_f_bias, lb_logits, hgrn_norm_w, w_branch_a, w_branch_b, w_out, ln1_w, ln1_b, w_ffn_gate, w_ffn_up, w_ffn_down, ln2_w, ln2_b, loss_target, m_w_ada, m_b_ada, m_w_in, m_fox_f_bias, m_lb_logits, m_hgrn_norm_w, m_w_branch_a, m_w_branch_b, m_w_out, m_ln1_w, m_ln1_b, m_w_ffn_gate, m_w_ffn_up, m_w_ffn_down, m_ln2_w, m_ln2_b, v_w_ada, v_b_ada, v_w_in, v_fox_f_bias, v_lb_logits, v_hgrn_norm_w, v_w_branch_a, v_w_branch_b, v_w_out, v_ln1_w, v_ln1_b, v_w_ffn_gate, v_w_ffn_up, v_w_ffn_down, v_ln2_w, v_ln2_b):
    nbatch, seq, _ = x.shape
    t = nbatch * seq
    ax, ay, ac = lax.axis_index("x"), lax.axis_index("y"), lax.axis_index("c")
    chip = 2 * ax + ay
    dev = 2 * chip + ac
    chip_arr = jnp.reshape(chip, (1,)).astype(jnp.int32)
    core_arr = jnp.reshape(ac, (1,)).astype(jnp.int32)

    shard_w = dict(w_in=w_in[0], w_branch_a=w_branch_a[0], w_branch_b=w_branch_b[0], w_out=w_out[0],
                   w_ffn_gate=w_ffn_gate[0], w_ffn_up=w_ffn_up[0], w_ffn_down=w_ffn_down[0])
    shard_m = dict(w_in=m_w_in[0], w_branch_a=m_w_branch_a[0], w_branch_b=m_w_branch_b[0], w_out=m_w_out[0],
                   w_ffn_gate=m_w_ffn_gate[0], w_ffn_up=m_w_ffn_up[0], w_ffn_down=m_w_ffn_down[0])
    shard_v = dict(w_in=v_w_in[0], w_branch_a=v_w_branch_a[0], w_branch_b=v_w_branch_b[0], w_out=v_w_out[0],
                   w_ffn_gate=v_w_ffn_gate[0], w_ffn_up=v_w_ffn_up[0], w_ffn_down=v_w_ffn_down[0])

    full = _full_from_shards(_gather_weights(_pack_shard(shard_w, BF16)))
    w_p = _permute_cols(full["w_in"])
    wba, wbb, wout = full["w_branch_a"], full["w_branch_b"], full["w_out"]
    wg, wu, wd = full["w_ffn_gate"], full["w_ffn_up"], full["w_ffn_down"]

    c8 = jnp.concatenate([c, jnp.zeros((8 - nbatch, D), F32)], axis=0)
    c_all = _allgather8(c8, "gather_c")[:, :nbatch, :].reshape(N_DEV * nbatch, D)
    ncol = w_ada.shape[2]
    b_cols = lax.dynamic_slice_in_dim(b_ada, chip * ncol, ncol, axis=1)
    mod_g = _allgather8(_mod_shard(c_all, w_ada[0], b_cols), "gather_mod")
    mod_all = jnp.concatenate([mod_g[2 * k] for k in range(N_CHIPS)], axis=1)
    mod_mine = lax.dynamic_slice_in_dim(mod_all, dev * nbatch, nbatch, axis=0)
    mod8 = jnp.concatenate([mod_mine.reshape(nbatch, 6, D), jnp.zeros((nbatch, 2, D), F32)], axis=1)

    x2 = x.reshape(t, D)
    tgt2 = loss_target.reshape(t, D)
    bias128 = jnp.concatenate([fox_f_bias, jnp.zeros((1, 128 - BH), F32)], axis=1)

    proj = _proj(x2, mod8, w_p, seq)
    ya, ckpt = _hgrn_fwd(proj, lb_logits, hgrn_norm_w, nbatch, seq)
    cum_rows, cum_cols = _fox_cum(proj, bias128, nbatch, seq)
    yb, lse = _fox_fwd(proj, cum_rows, cum_cols, nbatch, seq)
    merged, u, x1 = _merge_fwd(ya, yb, proj, x2, mod8, wba, wbb, wout, ln1_w, ln1_b, seq)
    a_pre, b_pre, dz2, st2, dm2 = _ffn_fwd(x1, mod8, wg, wu, wd, tgt2, ln2_w, ln2_b, seq)
    loss = lax.psum(st2[2, 0], ("x", "y", "c"))

    da, db, hmid, dffn, du, dxp, st1, dm1 = _ffn_bwd(dz2, a_pre, b_pre, wg, wu, wd, x1, x2, u, mod8, ln1_w, seq)
    g_full = {}
    g_full["w_ffn_down"] = _tn_matmul(hmid, dffn, "dw_ffn_down", seq)
    g_full["w_ffn_gate"] = _tn_matmul(x1, da, "dw_ffn_gate", seq, mod8, (3, 4))
    g_full["w_ffn_up"] = _tn_matmul(x1, db, "dw_ffn_up", seq, mod8, (3, 4))
    g_full["w_out"] = _tn_matmul(merged, du, "dw_out", seq)
    dproj, dpa, dpb, dya, dyb = _merge_bwd(du, ya, yb, proj, wba, wbb, wout, seq)
    g_full["w_branch_a"] = _tn_matmul(ya, dpa, "dw_branch_a", seq)
    g_full["w_branch_b"] = _tn_matmul(yb, dpb, "dw_branch_b", seq)
    dproj, dq, drs, dcs = _fox_bwd(proj, cum_rows, cum_cols, lse, yb, dyb, dproj, nbatch, seq)
    dproj = _place_cols(dproj, dq, COL_BQ)
    dproj, sm_fox = _fox_dbf(proj, bias128, drs, dcs, dproj, nbatch, seq)
    dproj, sm_hgrn = _hgrn_bwd(proj, dya, ckpt, lb_logits, hgrn_norm_w, dproj, nbatch, seq)
    grad_x2, dm0 = _dh_kernel(dproj, w_p, x2, dxp, mod8, seq)
    g_full["w_in"] = _unpermute_cols(_tn_matmul(x2, dproj, "dw_in", seq, mod8, (0, 1)))

    gpack = _shards_from_full(g_full, F32)
    red_c, red_c16 = _add_my_half(gpack, _swap_halves(gpack), core_arr)
    red = _add_chips(red_c, _scatter_chips(red_c16), chip_arr)
    g_shard = _unpack_shard(_join_halves(red))

    dmod = (dm0 + dm1 + dm2)[:, :6, :].reshape(nbatch, 6 * D)
    row2 = jnp.concatenate([st1[0:1], st1[1:2], st2[0:1], st2[1:2], sm_hgrn[1:2], sm_hgrn[0:1], sm_hgrn[0:1],
                            sm_fox[0:1, :BH], jnp.zeros((1, SMALL_W - O_FOX - BH), F32)], axis=1)
    spack = jnp.concatenate([dmod, row2, jnp.zeros((8 - nbatch - 1, SMALL_W), F32)], axis=0)
    gath = _allgather8(spack, "gather_small")
    w8 = _pack_small(b_ada, ln1_w, ln1_b, ln2_w, ln2_b, hgrn_norm_w, lb_logits, fox_f_bias)
    m8 = _pack_small(m_b_ada, m_ln1_w, m_ln1_b, m_ln2_w, m_ln2_b, m_hgrn_norm_w, m_lb_logits, m_fox_f_bias)
    v8 = _pack_small(v_b_ada, v_ln1_w, v_ln1_b, v_ln2_w, v_ln2_b, v_hgrn_norm_w, v_lb_logits, v_fox_f_bias)
    sg, sd, smn, svn = (_unpack_small(p) for p in _small_update(gath, w8, m8, v8))
    dmod_all = gath[:, :nbatch, :].reshape(N_DEV * nbatch, SMALL_W)
    g_ada = _grad_w_ada(c_all, lax.dynamic_slice_in_dim(dmod_all, chip * ncol, ncol, axis=1))

    grads = dict(sg)
    deltas = dict(sd)
    new_m = dict(smn)
    new_v = dict(svn)
    grads["w_ada"] = g_ada
    deltas["w_ada"], new_m["w_ada"], new_v["w_ada"] = _adamw(w_ada[0], g_ada, m_w_ada[0], v_w_ada[0], "adamw_w_ada")
    for n in _BIG:
        grads[n] = g_shard[n]
        deltas[n], new_m[n], new_v[n] = _adamw(shard_w[n], g_shard[n], shard_m[n], shard_v[n], "adamw_" + n)

    names = ["w_ada", "b_ada", "w_in", "fox_f_bias", "lb_logits", "hgrn_norm_w", "w_branch_a", "w_branch_b", "w_out",
             "ln1_w", "ln1_b", "w_ffn_gate", "w_ffn_up", "w_ffn_down", "ln2_w", "ln2_b"]
    shapes = dict(w_ada=w_ada.shape, b_ada=b_ada.shape, w_in=w_in.shape, fox_f_bias=fox_f_bias.shape,
                  lb_logits=lb_logits.shape, hgrn_norm_w=hgrn_norm_w.shape, w_branch_a=w_branch_a.shape,
                  w_branch_b=w_branch_b.shape, w_out=w_out.shape, ln1_w=ln1_w.shape, ln1_b=ln1_b.shape,
                  w_ffn_gate=w_ffn_gate.shape, w_ffn_up=w_ffn_up.shape, w_ffn_down=w_ffn_down.shape,
                  ln2_w=ln2_w.shape, ln2_b=ln2_b.shape)
    outs = [loss, grad_x2.reshape(x.shape)]
    for group in (grads, deltas, new_m, new_v):
        outs += [group[n].reshape(shapes[n]) for n in names]
    return tuple(outs)
```

```python
import functools
import math

import jax
import jax.numpy as jnp
import numpy as np
from jax import lax
from jax.experimental import pallas as pl
from jax.experimental.pallas import tpu as pltpu

F32 = jnp.float32
BF16 = jnp.bfloat16
MESH = pl.DeviceIdType.MESH
HIGHEST = lax.Precision.HIGHEST

D = 1024
AW = 512
AH = 4
ADH = 128
BH = 8
BDH = 64
DFF = 2816
NIN = 5640
NP = 5760
N_CHIPS = 4
N_DEV = 8
HGRN_BLOCK = 256
COL_GATES = 0
COL_BQ = 2048
COL_KV = 2560
COL_A = 3584
COL_BF = 5632
ALPHA = 2.0 ** 0.25
LN_EPS = 1e-5
RMS_EPS = 1e-6
NEG = -1e30
LR, B1, B2, EPS, WD, STEP = 0.001, 0.9, 0.999, 1e-08, 0.01, 10
PACK_ROWS = 4096
HALF = PACK_ROWS // 2
SMALL_W = 6144
O_LN1W, O_LN1B, O_LN2W, O_LN2B, O_NORM, O_LB0, O_LB1, O_FOX = 0, 1024, 2048, 3072, 4096, 4608, 5120, 5632


def _params(sem=None, vmem_mb=None):
    kw = {}
    if sem is not None:
        kw["dimension_semantics"] = sem
    if vmem_mb is not None:
        kw["vmem_limit_bytes"] = vmem_mb << 20
    return pltpu.CompilerParams(**kw)


def _dot(a, b):
    return jnp.dot(a.astype(BF16), b.astype(BF16), preferred_element_type=F32)


def _dot_nt(a, b):
    return lax.dot_general(a.astype(BF16), b.astype(BF16), (((1,), (1,)), ((), ())), preferred_element_type=F32)


def _dot_tn(a, b):
    return lax.dot_general(a.astype(BF16), b.astype(BF16), (((0,), (0,)), ((), ())), preferred_element_type=F32)


def _dot_f32(a, b):
    return jnp.dot(a, b, preferred_element_type=F32, precision=HIGHEST)


def _perm_segments():
    segs = [(3592, 5640), (2048, 2560)]
    for p in range(4):
        segs += [(2560 + 128 * p, 2688 + 128 * p), (3072 + 128 * p, 3200 + 128 * p)]
    for h in range(4):
        segs += [(128 * h + 512 * t, 128 * h + 512 * t + 128) for t in range(4)]
    segs += [(3584, 3592)]
    return segs


def _permute_cols(w):
    parts = [w[:, a:b] for a, b in _perm_segments()]
    parts.append(jnp.zeros((w.shape[0], NP - NIN), w.dtype))
    return jnp.concatenate(parts, axis=1)


def _unpermute_cols(g):
    pos, where = 0, {}
    for a, b in _perm_segments():
        where[a] = (pos, pos + b - a)
        pos += b - a
    parts = [g[:, where[a][0]:where[a][1]] for a in sorted(where)]
    return jnp.concatenate(parts, axis=1)


def _allgather8(v, name):
    rows, cols = v.shape

    def body(x_ref, out_ref, send_sems, recv_sems, local_sem):
        x, y, c = lax.axis_index("x"), lax.axis_index("y"), lax.axis_index("c")
        me, sibling = (x, y, c), (x, y, 1 - c)
        chips = [(1 - x, y), (x, 1 - y), (1 - x, 1 - y)]

        def slot(px, py, pc):
            return out_ref.at[4 * px + 2 * py + pc]

        def copy(k, block, to, src=None):
            return pltpu.make_async_remote_copy(
                src_ref=slot(*block) if src is None else src, dst_ref=slot(*block),
                send_sem=send_sems.at[k], recv_sem=recv_sems.at[k], device_id=to, device_id_type=MESH)

        mine = pltpu.make_async_copy(x_ref, slot(*me), local_sem)
        mine.start()
        first = [copy(0, me, sibling, src=x_ref)]
        first += [copy(1 + j, me, (*chip, c), src=x_ref) for j, chip in enumerate(chips)]
        for cp in first:
            cp.start()
        passed = [copy(4 + j, (*chip, c), sibling) for j, chip in enumerate(chips)]
        for j, chip in enumerate(chips):
            copy(1 + j, (*chip, c), me).wait_recv()
            passed[j].start()
        copy(0, sibling, me).wait_recv()
        for j, chip in enumerate(chips):
            copy(4 + j, (*chip, 1 - c), me).wait_recv()
        for cp in first + passed:
            cp.wait_send()
        mine.wait()

    return pl.pallas_call(
        body, name=name,
        out_shape=jax.ShapeDtypeStruct((N_DEV, rows, cols), v.dtype),
        in_specs=[pl.BlockSpec(memory_space=pltpu.VMEM)],
        out_specs=pl.BlockSpec(memory_space=pltpu.VMEM),
        scratch_shapes=[pltpu.SemaphoreType.DMA((7,)), pltpu.SemaphoreType.DMA((7,)), pltpu.SemaphoreType.DMA],
    )(v)


def _gather_weights(pack):
    def body(pack_ref, out_ref, send_sems, recv_sems):
        x, y, c = lax.axis_index("x"), lax.axis_index("y"), lax.axis_index("c")
        sibling = (x, y, 1 - c)
        chips = [(1 - x, y), (x, 1 - y), (1 - x, 1 - y)]

        def blk(px, py, half):
            return out_ref.at[2 * px + py, pl.ds(half * HALF, HALF), :]

        def copy(k, block, to, src=None):
            return pltpu.make_async_remote_copy(
                src_ref=blk(*block) if src is None else src, dst_ref=blk(*block),
                send_sem=send_sems.at[k], recv_sem=recv_sems.at[k], device_id=to, device_id_type=MESH)

        my_half = pack_ref.at[pl.ds(c * HALF, HALF), :]
        first = [copy(j, (x, y, c), (*chip, c), src=my_half) for j, chip in enumerate(chips)]
        for cp in first:
            cp.start()
        passed = [copy(3 + j, (*chip, c), sibling) for j, chip in enumerate(chips)]
        for j, chip in enumerate(chips):
            copy(j, (*chip, c), (x, y, c)).wait_recv()
            passed[j].start()
        for j, chip in enumerate(chips):
            copy(3 + j, (*chip, 1 - c), (x, y, c)).wait_recv()
        for cp in first + passed:
            cp.wait_send()

    return pl.pallas_call(
        body, name="gather_weights",
        out_shape=jax.ShapeDtypeStruct((N_CHIPS, PACK_ROWS, 1024), pack.dtype),
        in_specs=[pl.BlockSpec(memory_space=pl.ANY)],
        out_specs=pl.BlockSpec(memory_space=pl.ANY),
        scratch_shapes=[pltpu.SemaphoreType.DMA((6,)), pltpu.SemaphoreType.DMA((6,))],
    )(pack)


def _swap_halves(gpack):
    def body(g_ref, out_ref, send_sem, recv_sem):
        x, y, c = lax.axis_index("x"), lax.axis_index("y"), lax.axis_index("c")
        cp = pltpu.make_async_remote_copy(
            src_ref=g_ref.at[:, pl.ds((1 - c) * HALF, HALF), :], dst_ref=out_ref,
            send_sem=send_sem, recv_sem=recv_sem, device_id=(x, y, 1 - c), device_id_type=MESH)
        cp.start()
        cp.wait()

    return pl.pallas_call(
        body, name="grad_swap_halves",
        out_shape=jax.ShapeDtypeStruct((N_CHIPS, HALF, 1024), gpack.dtype),
        in_specs=[pl.BlockSpec(memory_space=pl.ANY)],
        out_specs=pl.BlockSpec(memory_space=pl.ANY),
        scratch_shapes=[pltpu.SemaphoreType.DMA, pltpu.SemaphoreType.DMA],
    )(gpack)


def _scatter_chips(red):
    def body(r_ref, out_ref, send_sems, recv_sems):
        x, y, c = lax.axis_index("x"), lax.axis_index("y"), lax.axis_index("c")
        chips = [(1 - x, y), (x, 1 - y), (1 - x, 1 - y)]
        cps = [pltpu.make_async_remote_copy(
            src_ref=r_ref.at[2 * chip[0] + chip[1]], dst_ref=out_ref.at[j],
            send_sem=send_sems.at[j], recv_sem=recv_sems.at[j], device_id=(*chip, c), device_id_type=MESH)
            for j, chip in enumerate(chips)]
        for cp in cps:
            cp.start()
        for cp in cps:
            cp.wait()

    return pl.pallas_call(
        body, name="grad_scatter_chips",
        out_shape=jax.ShapeDtypeStruct((3, HALF, 1024), red.dtype),
        in_specs=[pl.BlockSpec(memory_space=pl.ANY)],
        out_specs=pl.BlockSpec(memory_space=pl.ANY),
        scratch_shapes=[pltpu.SemaphoreType.DMA((3,)), pltpu.SemaphoreType.DMA((3,))],
    )(red)


def _join_halves(half):
    def body(h_ref, out_ref, send_sem, recv_sem):
        x, y, c = lax.axis_index("x"), lax.axis_index("y"), lax.axis_index("c")
        cp = pltpu.make_async_remote_copy(
            src_ref=h_ref, dst_ref=out_ref, send_sem=send_sem, recv_sem=recv_sem,
            device_id=(x, y, 1 - c), device_id_type=MESH)
        cp.start()
        cp.wait()

    return pl.pallas_call(
        body, name="grad_join_halves",
        out_shape=jax.ShapeDtypeStruct((HALF, 1024), half.dtype),
        in_specs=[pl.BlockSpec(memory_space=pl.ANY)],
        out_specs=pl.BlockSpec(memory_space=pl.ANY),
        scratch_shapes=[pltpu.SemaphoreType.DMA, pltpu.SemaphoreType.DMA],
    )(half)


def _add_my_half(gpack, other, c_idx):
    tr = 512
    nb = HALF // tr

    def body(c_ref, g_ref, o_ref, out_ref, out16_ref):
        s = g_ref[...] + o_ref[...]
        out_ref[...] = s
        out16_ref[...] = s.astype(BF16)

    return pl.pallas_call(
        body, name="grad_add_halves",
        grid_spec=pltpu.PrefetchScalarGridSpec(
            num_scalar_prefetch=1, grid=(N_CHIPS, nb),
            in_specs=[pl.BlockSpec((1, tr, 1024), lambda k, i, c: (k, c[0] * nb + i, 0)),
                      pl.BlockSpec((1, tr, 1024), lambda k, i, c: (k, i, 0))],
            out_specs=[pl.BlockSpec((1, tr, 1024), lambda k, i, c: (k, i, 0)),
                       pl.BlockSpec((1, tr, 1024), lambda k, i, c: (k, i, 0))]),
        out_shape=[jax.ShapeDtypeStruct((N_CHIPS, HALF, 1024), F32), jax.ShapeDtypeStruct((N_CHIPS, HALF, 1024), BF16)],
        compiler_params=_params(("parallel", "parallel")),
    )(c_idx, gpack, other)


def _add_chips(red, recv, chip_idx):
    tr = 512

    def body(k_ref, r_ref, v_ref, out_ref):
        out_ref[...] = ((r_ref[0] + v_ref[0].astype(F32)) + v_ref[1].astype(F32)) + v_ref[2].astype(F32)

    return pl.pallas_call(
        body, name="grad_add_chips",
        grid_spec=pltpu.PrefetchScalarGridSpec(
            num_scalar_prefetch=1, grid=(HALF // tr,),
            in_specs=[pl.BlockSpec((1, tr, 1024), lambda i, k: (k[0], i, 0)),
                      pl.BlockSpec((3, tr, 1024), lambda i, k: (0, i, 0))],
            out_specs=pl.BlockSpec((tr, 1024), lambda i, k: (i, 0))),
        out_shape=jax.ShapeDtypeStruct((HALF, 1024), F32),
        compiler_params=_params(("parallel",)),
    )(chip_idx, red, recv)


def _mod_shard(c_all, w_ada, b_ada):
    nb, cols = c_all.shape[0], w_ada.shape[1]

    def body(c_ref, w_ref, b_ref, o_ref):
        c = c_ref[...]
        o_ref[...] = _dot(c * jax.nn.sigmoid(c), w_ref[...]) + b_ref[...]

    return pl.pallas_call(
        body, name="mod_shard", out_shape=jax.ShapeDtypeStruct((nb, cols), F32),
        compiler_params=_params(vmem_mb=48),
    )(c_all, w_ada, b_ada)


def _proj(x2, mod8, w_p, seq):
    t = x2.shape[0]
    tm, tn = min(512, seq), 1152
    tpb = seq // tm

    def body(x_ref, mod_ref, w_ref, o_ref, h_scr):
        @pl.when(pl.program_id(1) == 0)
        def _():
            h_scr[...] = (x_ref[...] * (1.0 + mod_ref[0, 1:2, :]) + mod_ref[0, 0:1, :]).astype(BF16)
        o_ref[...] = jnp.dot(h_scr[...], w_ref[...], preferred_element_type=F32)

    return pl.pallas_call(
        body, name="proj", grid=(t // tm, NP // tn),
        in_specs=[pl.BlockSpec((tm, D), lambda i, j: (i, 0)),
                  pl.BlockSpec((1, 8, D), lambda i, j: (i // tpb, 0, 0)),
                  pl.BlockSpec((D, tn), lambda i, j: (0, j))],
        out_specs=pl.BlockSpec((tm, tn), lambda i, j: (i, j)),
        out_shape=jax.ShapeDtypeStruct((t, NP), F32),
        scratch_shapes=[pltpu.VMEM((tm, D), BF16)],
        compiler_params=_params(("parallel", "arbitrary"), 48),
    )(x2, mod8, w_p)


def _tn_matmul(a, b, name, seq, mod8=None, rows=None):
    t, ka = a.shape
    n = b.shape[1]
    tt = min(512, seq)
    tpb = seq // tt
    tn = n
    for cand in (1152, 1024, 1408, 512):
        if n % cand == 0:
            tn = cand
            break
    nt = t // tt

    def body(*refs):
        if mod8 is None:
            a_ref, b_ref, o_ref = refs
            av = a_ref[...]
        else:
            a_ref, m_ref, b_ref, o_ref = refs
            av = a_ref[...] * (1.0 + m_ref[0, rows[1]:rows[1] + 1, :]) + m_ref[0, rows[0]:rows[0] + 1, :]
        part = _dot_tn(av, b_ref[...])

        @pl.when(pl.program_id(1) == 0)
        def _():
            o_ref[...] = part

        @pl.when(pl.program_id(1) > 0)
        def _():
            o_ref[...] += part

    in_specs = [pl.BlockSpec((tt, ka), lambda j, k: (k, 0))]
    args = [a]
    if mod8 is not None:
        in_specs.append(pl.BlockSpec((1, 8, ka), lambda j, k: (k // tpb, 0, 0)))
        args.append(mod8)
    in_specs.append(pl.BlockSpec((tt, tn), lambda j, k: (k, j)))
    args.append(b)
    return pl.pallas_call(
        body, name=name, grid=(n // tn, nt), in_specs=in_specs,
        out_specs=pl.BlockSpec((ka, tn), lambda j, k: (0, j)),
        out_shape=jax.ShapeDtypeStruct((ka, n), F32),
        compiler_params=_params(("parallel", "arbitrary"), 48),
    )(*args)


def _dh_kernel(dproj, w_p, x2, dxp, mod8, seq):
    t = x2.shape[0]
    tm, tk = min(512, seq), 1152
    tpb = seq // tm
    nk = NP // tk
    nbatch = t // seq

    def body(dp_ref, w_ref, x_ref, dxp_ref, mod_ref, gx_ref, dm_ref, acc):
        i, k = pl.program_id(0), pl.program_id(1)

        @pl.when(k == 0)
        def _():
            acc[...] = jnp.zeros_like(acc)

        acc[...] += _dot_nt(dp_ref[...], w_ref[...])

        @pl.when(k == nk - 1)
        def _():
            dh = acc[...]
            gx_ref[...] = dxp_ref[...] + dh * (1.0 + mod_ref[0, 1:2, :])
            upd = jnp.concatenate(
                [jnp.sum(dh, axis=0, keepdims=True), jnp.sum(dh * x_ref[...], axis=0, keepdims=True),
                 jnp.zeros((6, D), F32)], axis=0)

            @pl.when(i % tpb == 0)
            def _():
                dm_ref[0] = upd

            @pl.when(i % tpb != 0)
            def _():
                dm_ref[0] += upd

    return pl.pallas_call(
        body, name="dh", grid=(t // tm, nk),
        in_specs=[pl.BlockSpec((tm, tk), lambda i, k: (i, k)),
                  pl.BlockSpec((D, tk), lambda i, k: (0, k)),
                  pl.BlockSpec((tm, D), lambda i, k: (i, 0)),
                  pl.BlockSpec((tm, D), lambda i, k: (i, 0)),
                  pl.BlockSpec((1, 8, D), lambda i, k: (i // tpb, 0, 0))],
        out_specs=[pl.BlockSpec((tm, D), lambda i, k: (i, 0)),
                   pl.BlockSpec((1, 8, D), lambda i, k: (i // tpb, 0, 0))],
        out_shape=[jax.ShapeDtypeStruct((t, D), F32), jax.ShapeDtypeStruct((nbatch, 8, D), F32)],
        scratch_shapes=[pltpu.VMEM((tm, D), F32)],
        compiler_params=_params(("arbitrary", "arbitrary"), 48),
    )(dproj, w_p, x2, dxp, mod8)


def _tri(n, upper):
    r = lax.broadcasted_iota(jnp.int32, (n, n), 0)
    c = lax.broadcasted_iota(jnp.int32, (n, n), 1)
    return jnp.where((c >= r) if upper else (c <= r), 1.0, 0.0).astype(F32)


@jax.custom_vjp
def _mm_nn(a, b):
    return _dot(a, b)


_mm_nn.defvjp(lambda a, b: (_dot(a, b), (a, b)),
              lambda res, g: (_dot_nt(g, res[1]), _dot_tn(res[0], g)))


@jax.custom_vjp
def _mm_nt(a, b):
    return _dot_nt(a, b)


_mm_nt.defvjp(lambda a, b: (_dot_nt(a, b), (a, b)),
              lambda res, g: (_dot(g, res[1]), _dot_tn(g, res[0])))


@jax.custom_vjp
def _mm_tn(a, b):
    return _dot_tn(a, b)


_mm_tn.defvjp(lambda a, b: (_dot_tn(a, b), (a, b)),
              lambda res, g: (_dot_nt(res[1], g), _dot(res[0], g)))


@jax.custom_vjp
def _cumsum_rows(x):
    return _dot_f32(_tri(x.shape[0], False), x)


_cumsum_rows.defvjp(lambda x: (_cumsum_rows(x), None),
                    lambda _, g: (_dot_f32(_tri(g.shape[0], True), g),))


@functools.partial(jax.custom_vjp, nondiff_argnums=(1,))
def _shift_rows(x, k):
    return pltpu.roll(x, k % x.shape[0], 0)


_shift_rows.defvjp(lambda x, k: (_shift_rows(x, k), None),
                   lambda k, _, g: (pltpu.roll(g, (-k) % g.shape[0], 0),))


def _group_ref(bc, m):
    n = bc.shape[0] // (2 * m)
    b3 = bc.reshape(n, 2 * m, ADH)
    row = lax.broadcasted_iota(jnp.int32, b3.shape, 1)
    ref = jnp.sum(jnp.where(row == m - 1, b3, 0.0), axis=1, keepdims=True)
    return jnp.broadcast_to(ref, b3.shape).reshape(bc.shape)


def _hgrn_block(q, fl, v, g, st, lb, nw):
    n = q.shape[0]
    f = lb + (1.0 - lb) * jax.nn.sigmoid(fl)
    kk = 1.0 - f
    lf = jnp.log(f)
    bc = _cumsum_rows(lf)
    row = lax.broadcasted_iota(jnp.int32, (n, ADH), 0)
    same = jnp.bitwise_xor(lax.broadcasted_iota(jnp.int32, (n, n), 0), lax.broadcasted_iota(jnp.int32, (n, n), 1))
    a = jnp.zeros((n, n), F32)
    m = 1
    while m < n:
        r = jnp.bitwise_and(row, 2 * m - 1)
        up, lo = r >= m, r < m
        if m == 1:
            aq, ak = lf, jnp.zeros_like(lf)
        elif m == 2:
            aq = jnp.where(r == 3, lf + _shift_rows(lf, 1), lf)
            ak = jnp.where(r == 0, _shift_rows(lf, -1), 0.0)
        else:
            ref = _group_ref(bc, m)
            aq, ak = bc - ref, ref - bc
        qt = jnp.where(up, q * jnp.exp(jnp.where(up, aq, 0.0)), 0.0)
        kt = jnp.where(lo, kk * jnp.exp(jnp.where(lo, ak, 0.0)), 0.0)
        a = a + jnp.where(same < 2 * m, _mm_nt(qt, kt), 0.0)
        m *= 2
    last = row == n - 1
    bl = jnp.sum(jnp.where(last, bc, 0.0), axis=0, keepdims=True)
    o = _mm_nn(a, v) + _mm_nt(q * jnp.exp(bc), st) + jnp.sum(q * kk, axis=-1, keepdims=True) * v
    st_new = st * jnp.exp(bl) + _mm_tn(v, kk * jnp.exp(bl - bc))
    rms = lax.rsqrt(jnp.mean(o * o, axis=-1, keepdims=True) + RMS_EPS)
    return o * rms * nw * jax.nn.sigmoid(g), st_new


def _hgrn_fwd(proj, lb_logits, norm_w, nbatch, seq):
    t = proj.shape[0]
    blk = min(HGRN_BLOCK, seq)
    nb = seq // blk

    def body(p_ref, lbl_ref, nw_ref, y_ref, ck_ref, st_s):
        @pl.when(pl.program_id(2) == 0)
        def _():
            st_s[...] = jnp.zeros_like(st_s)

        st = st_s[...]
        ck_ref[0] = st
        lb = jax.nn.sigmoid(lbl_ref[0:1, :] - lbl_ref[1:2, :])
        y, st_new = _hgrn_block(p_ref[:, 0:128], p_ref[:, 128:256], p_ref[:, 256:384], p_ref[:, 384:512],
                                st, lb, nw_ref[...])
        st_s[...] = st_new
        y_ref[...] = y.astype(y_ref.dtype)

    return pl.pallas_call(
        body, name="hgrn_fwd", grid=(AH, nbatch, nb),
        in_specs=[pl.BlockSpec((blk, 512), lambda h, b, i: (b * nb + i, COL_A // 512 + h)),
                  pl.BlockSpec((2, 128), lambda h, b, i: (0, h)),
                  pl.BlockSpec((1, 128), lambda h, b, i: (0, h))],
        out_specs=[pl.BlockSpec((blk, 128), lambda h, b, i: (b * nb + i, h)),
                   pl.BlockSpec((1, 128, 128), lambda h, b, i: ((h * nbatch + b) * nb + i, 0, 0))],
        out_shape=[jax.ShapeDtypeStruct((t, AW), BF16), jax.ShapeDtypeStruct((AH * nbatch * nb, 128, 128), F32)],
        scratch_shapes=[pltpu.VMEM((128, 128), F32)],
        compiler_params=_params(("parallel", "parallel", "arbitrary"), 48),
    )(proj, lb_logits, norm_w)


def _hgrn_bwd(proj, dya, ckpt, lb_logits, norm_w, dproj, nbatch, seq):
    t = proj.shape[0]
    blk = min(HGRN_BLOCK, seq)
    nb = seq // blk

    def body(p_ref, dy_ref, ck_ref, lbl_ref, nw_ref, dp_in, dp_ref, sm_ref, dst_s):
        del dp_in
        b_id, i = pl.program_id(1), pl.program_id(2)

        @pl.when(i == 0)
        def _():
            dst_s[...] = jnp.zeros_like(dst_s)

        lb = jax.nn.sigmoid(lbl_ref[0:1, :] - lbl_ref[1:2, :])
        _, pullback = jax.vjp(_hgrn_block, p_ref[:, 0:128], p_ref[:, 128:256], p_ref[:, 256:384],
                              p_ref[:, 384:512], ck_ref[0], lb, nw_ref[...])
        dq, dfl, dv, dg, dst, dlb, dnw = pullback((dy_ref[...], dst_s[...]))
        dst_s[...] = dst
        dp_ref[:, 0:128] = dq.astype(dp_ref.dtype)
        dp_ref[:, 128:256] = dfl.astype(dp_ref.dtype)
        dp_ref[:, 256:384] = dv.astype(dp_ref.dtype)
        dp_ref[:, 384:512] = dg.astype(dp_ref.dtype)
        upd = jnp.concatenate([dlb, dnw, jnp.zeros((6, 128), F32)], axis=0)
        first = (b_id == 0) & (i == 0)

        @pl.when(first)
        def _():
            sm_ref[...] = upd

        @pl.when(jnp.logical_not(first))
        def _():
            sm_ref[...] += upd

    def rows(h, b, i):
        return b * nb + (nb - 1 - i)

    return pl.pallas_call(
        body, name="hgrn_bwd", grid=(AH, nbatch, nb),
        in_specs=[pl.BlockSpec((blk, 512), lambda h, b, i: (rows(h, b, i), COL_A // 512 + h)),
                  pl.BlockSpec((blk, 128), lambda h, b, i: (rows(h, b, i), h)),
                  pl.BlockSpec((1, 128, 128), lambda h, b, i: ((h * nbatch + b) * nb + (nb - 1 - i), 0, 0)),
                  pl.BlockSpec((2, 128), lambda h, b, i: (0, h)),
                  pl.BlockSpec((1, 128), lambda h, b, i: (0, h)),
                  pl.BlockSpec(memory_space=pl.ANY)],
        out_specs=[pl.BlockSpec((blk, 512), lambda h, b, i: (rows(h, b, i), COL_A // 512 + h)),
                   pl.BlockSpec((8, 128), lambda h, b, i: (0, h))],
        out_shape=[jax.ShapeDtypeStruct((t, NP), BF16), jax.ShapeDtypeStruct((8, AW), F32)],
        input_output_aliases={5: 0},
        scratch_shapes=[pltpu.VMEM((128, 128), F32)],
        compiler_params=_params(("parallel", "arbitrary", "arbitrary"), 48),
    )(proj, dya, ckpt, lb_logits, norm_w, dproj)


def _log_sigmoid(z):
    return jnp.minimum(z, 0.0) - jnp.log(1.0 + jnp.exp(-jnp.abs(z)))


def _fox_cum(proj, bias128, nbatch, seq):
    t = proj.shape[0]
    ts = min(512, seq)
    nb = seq // ts

    def body(p_ref, b_ref, r_ref, c_ref, carry):
        @pl.when(pl.program_id(1) == 0)
        def _():
            carry[...] = jnp.zeros_like(carry)
        cum = _dot_f32(_tri(ts, False), _log_sigmoid(p_ref[...] + b_ref[...])) + carry[...]
        carry[...] = cum[ts - 1:ts, :]
        r_ref[0] = cum.T[0:BH, :]
        lane = lax.broadcasted_iota(jnp.int32, (ts, 128), 1)
        for p in range(4):
            c_ref[p] = jnp.where(lane < 64, cum[:, 2 * p:2 * p + 1], cum[:, 2 * p + 1:2 * p + 2])

    return pl.pallas_call(
        body, name="fox_cum", grid=(nbatch, nb),
        in_specs=[pl.BlockSpec((ts, 128), lambda b, i: (b * nb + i, COL_BF // 128)),
                  pl.BlockSpec((1, 128), lambda b, i: (0, 0))],
        out_specs=[pl.BlockSpec((1, BH, ts), lambda b, i: (b, 0, i)),
                   pl.BlockSpec((4, ts, 128), lambda b, i: (0, b * nb + i, 0))],
        out_shape=[jax.ShapeDtypeStruct((nbatch, BH, seq), F32), jax.ShapeDtypeStruct((4, t, 128), F32)],
        scratch_shapes=[pltpu.VMEM((1, 128), F32)],
        compiler_params=_params(("parallel", "arbitrary")),
    )(proj, bias128)


def _fox_scores_t(q_ref, kv_ref, cr_ref, cc_ref, hh, hp, masked, tq, tk):
    kh = kv_ref[:, 64 * hh:64 * hh + 64].astype(BF16)
    qh = (q_ref[:, 64 * hh:64 * hh + 64] * (BDH ** -0.5)).astype(BF16)
    s = _dot_nt(kh, qh) + cr_ref[0, pl.ds(2 * hp + hh, 1), :] - cc_ref[0, :, 64 * hh:64 * hh + 1]
    if masked:
        key = lax.broadcasted_iota(jnp.int32, (tk, tq), 0)
        qry = lax.broadcasted_iota(jnp.int32, (tk, tq), 1)
        s = jnp.where(key <= qry, s, NEG)
    return s, kh, qh


def _with_ones_lane(x128, hh):
    lane = lax.broadcasted_iota(jnp.int32, x128.shape, 1)
    one = jnp.ones_like(x128)
    zero = jnp.zeros_like(x128)
    if hh == 0:
        return jnp.where(lane < 64, x128, jnp.where(lane == 64, one, zero))
    return jnp.where(lane >= 64, x128, jnp.where(lane == 0, one, zero))


def _fox_fwd(proj, cum_rows, cum_cols, nbatch, seq):
    t = proj.shape[0]
    tq = tk = min(512, seq)
    nq = seq // tq

    def body(q_ref, kv_ref, cr_ref, cc_ref, o_ref, lse_ref, m_s, acc_s):
        hp, i, j = pl.program_id(1), pl.program_id(2), pl.program_id(3)

        @pl.when(j == 0)
        def _():
            m_s[...] = jnp.full_like(m_s, NEG)
            acc_s[...] = jnp.zeros_like(acc_s)

        def step(masked):
            for hh in range(2):
                s, _, _ = _fox_scores_t(q_ref, kv_ref, cr_ref, cc_ref, hh, hp, masked, tq, tk)
                m_prev = m_s[hh:hh + 1, :]
                m_new = jnp.maximum(m_prev, jnp.max(s, axis=0, keepdims=True))
                alpha = jnp.exp(m_prev - m_new)
                p = jnp.exp(s - m_new).astype(BF16)
                v_aug = _with_ones_lane(kv_ref[:, 128:256].astype(BF16), hh)
                acc_s[hh] = acc_s[hh] * alpha + _dot_tn(v_aug, p)
                m_s[hh:hh + 1, :] = m_new

        @pl.when(j < i)
        def _():
            step(False)

        @pl.when(j == i)
        def _():
            step(True)
            a0, a1 = acc_s[0], acc_s[1]
            l0, l1 = a0[64:65, :], a1[0:1, :]
            o_t = jnp.concatenate([a0[0:64, :] / l0, a1[64:128, :] / l1], axis=0)
            o_ref[...] = o_t.T.astype(o_ref.dtype)
            lse_ref[0, 0] = jnp.concatenate(
                [m_s[0:1, :] + jnp.log(l0), m_s[1:2, :] + jnp.log(l1), jnp.zeros((6, tq), F32)], axis=0)

    return pl.pallas_call(
        body, name="fox_fwd", grid=(nbatch, 4, nq, nq),
        in_specs=[pl.BlockSpec((tq, 128), lambda b, p, i, j: (b * nq + i, COL_BQ // 128 + p)),
                  pl.BlockSpec((tk, 256), lambda b, p, i, j: (b * nq + jnp.minimum(j, i), COL_KV // 256 + p)),
                  pl.BlockSpec((1, BH, tq), lambda b, p, i, j: (b, 0, i)),
                  pl.BlockSpec((1, tk, 128), lambda b, p, i, j: (p, b * nq + jnp.minimum(j, i), 0))],
        out_specs=[pl.BlockSpec((tq, 128), lambda b, p, i, j: (b * nq + i, p)),
                   pl.BlockSpec((1, 1, 8, tq), lambda b, p, i, j: (b, p, 0, i))],
        out_shape=[jax.ShapeDtypeStruct((t, 512), BF16), jax.ShapeDtypeStruct((nbatch, 4, 8, seq), F32)],
        scratch_shapes=[pltpu.VMEM((8, tq), F32), pltpu.VMEM((2, 128, tq), F32)],
        compiler_params=_params(("parallel", "parallel", "parallel", "arbitrary"), 48),
    )(proj, proj, cum_rows, cum_cols)


def _fox_bwd(proj, cum_rows, cum_cols, lse, yb, dyb, dproj, nbatch, seq):
    t = proj.shape[0]
    tq = tk = min(512, seq)
    nq = seq // tq
    scale = BDH ** -0.5

    def body(q_ref, kv_ref, cr_ref, cc_ref, lse_ref, o_ref, do_ref, dp_in,
             dkv_ref, dq_ref, drs_ref, dcs_ref, dkv_s, dqa_s, dcs_s):
        del dp_in
        hp, j, ii = pl.program_id(1), pl.program_id(2), pl.program_id(3)
        i = j + ii

        @pl.when(ii == 0)
        def _():
            dkv_s[...] = jnp.zeros_like(dkv_s)
            dcs_s[...] = jnp.zeros_like(dcs_s)

        @pl.when((j == 0) & (ii == 0))
        def _():
            dqa_s[...] = jnp.zeros_like(dqa_s)

        def step(masked):
            lane = lax.broadcasted_iota(jnp.int32, (tq, 128), 1)
            for hh in range(2):
                s, kh, qh = _fox_scores_t(q_ref, kv_ref, cr_ref, cc_ref, hh, hp, masked, tq, tk)
                p = jnp.exp(s - lse_ref[0, 0, hh:hh + 1, :])
                doh = do_ref[:, 64 * hh:64 * hh + 64]
                dd = lax.dot_general(jnp.ones((8, 64), F32), doh * o_ref[:, 64 * hh:64 * hh + 64].astype(F32),
                                     (((1,), (1,)), ((), ())), preferred_element_type=F32, precision=HIGHEST)[0:1, :]
                doh = doh.astype(BF16)
                dp = _dot_nt(kv_ref[:, 128 + 64 * hh:192 + 64 * hh], doh)
                ds = (p * (dp - dd)).astype(BF16)
                dkv_s[:, 128 + 64 * hh:192 + 64 * hh] += _dot(p, doh)
                dkv_s[:, 64 * hh:64 * hh + 64] += _dot(ds, qh)
                k_aug = _with_ones_lane(kv_ref[:, 0:128].astype(BF16), hh)
                dqa_s[i, hh] += _dot_tn(k_aug, ds)
                sel = jnp.where(lane == 2 * hp + hh, 1.0, 0.0).astype(BF16)
                dcs_s[...] += _dot(ds, sel)

        @pl.when((ii == 0) & (i < nq))
        def _():
            step(True)

        @pl.when((ii > 0) & (i < nq))
        def _():
            step(False)

        @pl.when(ii == nq - 1)
        def _():
            dkv_ref[...] = dkv_s[...].astype(dkv_ref.dtype)
            dcs_ref[0] = dcs_s[...]

        @pl.when((j == nq - 1) & (ii == nq - 1))
        def _():
            lane = lax.broadcasted_iota(jnp.int32, (tq, 128), 1)
            for qi in range(nq):
                a0 = dqa_s[qi, 0].T
                a1 = dqa_s[qi, 1].T
                rows = pl.ds(qi * tq, tq)
                dq_ref[rows, :] = (jnp.where(lane < 64, a0, a1) * scale).astype(dq_ref.dtype)
                drs_ref[0, rows, :] = jnp.where(lane == 2 * hp, a0[:, 64:65], jnp.where(lane == 2 * hp + 1, a1[:, 0:1], 0.0))

    def qrow(b, p, j, ii):
        return b * nq + jnp.minimum(j + ii, nq - 1)

    def qblk(b, p, j, ii):
        return jnp.minimum(j + ii, nq - 1)

    return pl.pallas_call(
        body, name="fox_bwd", grid=(nbatch, 4, nq, nq),
        in_specs=[pl.BlockSpec((tq, 128), lambda b, p, j, ii: (qrow(b, p, j, ii), COL_BQ // 128 + p)),
                  pl.BlockSpec((tk, 256), lambda b, p, j, ii: (b * nq + j, COL_KV // 256 + p)),
                  pl.BlockSpec((1, BH, tq), lambda b, p, j, ii: (b, 0, qblk(b, p, j, ii))),
                  pl.BlockSpec((1, tk, 128), lambda b, p, j, ii: (p, b * nq + j, 0)),
                  pl.BlockSpec((1, 1, 8, tq), lambda b, p, j, ii: (b, p, 0, qblk(b, p, j, ii))),
                  pl.BlockSpec((tq, 128), lambda b, p, j, ii: (qrow(b, p, j, ii), p)),
                  pl.BlockSpec((tq, 128), lambda b, p, j, ii: (qrow(b, p, j, ii), p)),
                  pl.BlockSpec(memory_space=pl.ANY)],
        out_specs=[pl.BlockSpec((tk, 256), lambda b, p, j, ii: (b * nq + j, COL_KV // 256 + p)),
                   pl.BlockSpec((seq, 128), lambda b, p, j, ii: (b, p)),
                   pl.BlockSpec((1, seq, 128), lambda b, p, j, ii: (p, b, 0)),
                   pl.BlockSpec((1, tk, 128), lambda b, p, j, ii: (p, b * nq + j, 0))],
        out_shape=[jax.ShapeDtypeStruct((t, NP), BF16), jax.ShapeDtypeStruct((t, 512), BF16),
                   jax.ShapeDtypeStruct((4, t, 128), F32), jax.ShapeDtypeStruct((4, t, 128), F32)],
        input_output_aliases={7: 0},
        scratch_shapes=[pltpu.VMEM((tk, 256), F32), pltpu.VMEM((nq, 2, 128, tq), F32), pltpu.VMEM((tk, 128), F32)],
        compiler_params=_params(("parallel", "parallel", "arbitrary", "arbitrary"), 56),
    )(proj, proj, cum_rows, cum_cols, lse, yb, dyb, dproj)


def _place_cols(dproj, src, col):
    t, w = src.shape
    tm = 1024 if t % 1024 == 0 else t

    def body(s_ref, dp_in, o_ref):
        del dp_in
        o_ref[...] = s_ref[...]

    return pl.pallas_call(
        body, name="place_cols", grid=(t // tm,),
        in_specs=[pl.BlockSpec((tm, w), lambda i: (i, 0)), pl.BlockSpec(memory_space=pl.ANY)],
        out_specs=pl.BlockSpec((tm, w), lambda i: (i, col // w)),
        out_shape=jax.ShapeDtypeStruct(dproj.shape, dproj.dtype),
        input_output_aliases={1: 0},
        compiler_params=_params(("parallel",)),
    )(src, dproj)


def _fox_dbf(proj, bias128, drs, dcs, dproj, nbatch, seq):
    t = proj.shape[0]
    ts = min(512, seq)
    nb = seq // ts

    def body(p_ref, b_ref, dr_ref, dc_ref, dp_in, dp_ref, sm_ref, carry):
        del dp_in
        b_id, i = pl.program_id(0), pl.program_id(1)

        @pl.when(i == 0)
        def _():
            carry[...] = jnp.zeros_like(carry)

        dcum = (dr_ref[0] - dc_ref[0]) + (dr_ref[1] - dc_ref[1]) + (dr_ref[2] - dc_ref[2]) + (dr_ref[3] - dc_ref[3])
        rc = _dot_f32(_tri(ts, True), dcum) + carry[...]
        carry[...] = rc[0:1, :]
        z = p_ref[...] + b_ref[...]
        lane = lax.broadcasted_iota(jnp.int32, (ts, 128), 1)
        dz = jnp.where(lane < BH, rc * jax.nn.sigmoid(-z), 0.0)
        dp_ref[...] = dz.astype(dp_ref.dtype)
        upd = jnp.concatenate([jnp.sum(dz, axis=0, keepdims=True), jnp.zeros((7, 128), F32)], axis=0)
        first = (b_id == 0) & (i == 0)

        @pl.when(first)
        def _():
            sm_ref[...] = upd

        @pl.when(jnp.logical_not(first))
        def _():
            sm_ref[...] += upd

    def rows(b, i):
        return b * nb + (nb - 1 - i)

    return pl.pallas_call(
        body, name="fox_dbf", grid=(nbatch, nb),
        in_specs=[pl.BlockSpec((ts, 128), lambda b, i: (rows(b, i), COL_BF // 128)),
                  pl.BlockSpec((1, 128), lambda b, i: (0, 0)),
                  pl.BlockSpec((4, ts, 128), lambda b, i: (0, rows(b, i), 0)),
                  pl.BlockSpec((4, ts, 128), lambda b, i: (0, rows(b, i), 0)),
                  pl.BlockSpec(memory_space=pl.ANY)],
        out_specs=[pl.BlockSpec((ts, 128), lambda b, i: (rows(b, i), COL_BF // 128)),
                   pl.BlockSpec((8, 128), lambda b, i: (0, 0))],
        out_shape=[jax.ShapeDtypeStruct((t, NP), BF16), jax.ShapeDtypeStruct((8, 128), F32)],
        input_output_aliases={4: 0},
        scratch_shapes=[pltpu.VMEM((1, 128), F32)],
        compiler_params=_params(("arbitrary", "arbitrary")),
    )(proj, bias128, drs, dcs, dproj)


def _ln_stats(z):
    mu = jnp.mean(z, axis=-1, keepdims=True)
    zc = z - mu
    rstd = lax.rsqrt(jnp.mean(zc * zc, axis=-1, keepdims=True) + LN_EPS)
    return zc * rstd, rstd


def _ln_bwd(dy, xhat, rstd, w):
    dxh = dy * w
    return rstd * (dxh - jnp.mean(dxh, axis=-1, keepdims=True) - xhat * jnp.mean(dxh * xhat, axis=-1, keepdims=True))


def _merge_fwd(ya, yb, proj, x2, mod8, wba, wbb, wout, ln1w, ln1b, seq):
    t = x2.shape[0]
    tm = min(256, seq)
    tpb = seq // tm

    def body(ya_ref, yb_ref, g_ref, x_ref, mod_ref, wa_ref, wb_ref, wo_ref, lw_ref, lb_ref, mg_ref, u_ref, x1_ref):
        ga = jax.nn.sigmoid(g_ref[:, 0:D])
        gb = jax.nn.sigmoid(g_ref[:, D:2 * D])
        merged = (ga * jnp.dot(ya_ref[...], wa_ref[...], preferred_element_type=F32)
                  + gb * jnp.dot(yb_ref[...], wb_ref[...], preferred_element_type=F32))
        mg = merged.astype(BF16)
        mg_ref[...] = mg
        u = jnp.dot(mg, wo_ref[...], preferred_element_type=F32)
        u_ref[...] = u
        xhat, _ = _ln_stats(ALPHA * x_ref[...] + (1.0 + mod_ref[0, 2:3, :]) * u)
        x1_ref[...] = xhat * lw_ref[...] + lb_ref[...]

    tok = lambda w: pl.BlockSpec((tm, w), lambda i: (i, 0))
    full = lambda a: pl.BlockSpec(a.shape, lambda i: (0,) * a.ndim)
    return pl.pallas_call(
        body, name="merge_fwd", grid=(t // tm,),
        in_specs=[tok(512), tok(512), pl.BlockSpec((tm, 2048), lambda i: (i, COL_GATES // 2048)), tok(D),
                  pl.BlockSpec((1, 8, D), lambda i: (i // tpb, 0, 0)),
                  full(wba), full(wbb), full(wout), full(ln1w), full(ln1b)],
        out_specs=[tok(D), tok(D), tok(D)],
        out_shape=[jax.ShapeDtypeStruct((t, D), BF16), jax.ShapeDtypeStruct((t, D), F32),
                   jax.ShapeDtypeStruct((t, D), F32)],
        compiler_params=_params(("parallel",), 48),
    )(ya, yb, proj, x2, mod8, wba, wbb, wout, ln1w, ln1b)


def _merge_bwd(du, ya, yb, proj, wba, wbb, wout, seq):
    t = du.shape[0]
    tm = min(256, seq)

    def body(du_ref, ya_ref, yb_ref, g_ref, wa_ref, wb_ref, wo_ref, dp_ref, dpa_ref, dpb_ref, dya_ref, dyb_ref):
        ga = jax.nn.sigmoid(g_ref[:, 0:D])
        gb = jax.nn.sigmoid(g_ref[:, D:2 * D])
        dm = _dot_nt(du_ref[...], wo_ref[...])
        pa = jnp.dot(ya_ref[...], wa_ref[...], preferred_element_type=F32)
        pb = jnp.dot(yb_ref[...], wb_ref[...], preferred_element_type=F32)
        dpa = (dm * ga).astype(BF16)
        dpb = (dm * gb).astype(BF16)
        dpa_ref[...] = dpa
        dpb_ref[...] = dpb
        dp_ref[:, 0:D] = (dm * pa * ga * (1.0 - ga)).astype(BF16)
        dp_ref[:, D:2 * D] = (dm * pb * gb * (1.0 - gb)).astype(BF16)
        dya_ref[...] = _dot_nt(dpa, wa_ref[...])
        dyb_ref[...] = _dot_nt(dpb, wb_ref[...])

    tok = lambda w: pl.BlockSpec((tm, w), lambda i: (i, 0))
    full = lambda a: pl.BlockSpec(a.shape, lambda i: (0,) * a.ndim)
    return pl.pallas_call(
        body, name="merge_bwd", grid=(t // tm,),
        in_specs=[tok(D), tok(512), tok(512), pl.BlockSpec((tm, 2048), lambda i: (i, COL_GATES // 2048)),
                  full(wba), full(wbb), full(wout)],
        out_specs=[pl.BlockSpec((tm, 2048), lambda i: (i, COL_GATES // 2048)), tok(D), tok(D), tok(512), tok(512)],
        out_shape=[jax.ShapeDtypeStruct((t, NP), BF16), jax.ShapeDtypeStruct((t, D), BF16),
                   jax.ShapeDtypeStruct((t, D), BF16), jax.ShapeDtypeStruct((t, 512), F32),
                   jax.ShapeDtypeStruct((t, 512), F32)],
        compiler_params=_params(("parallel",), 48),
    )(du, ya, yb, proj, wba, wbb, wout)


def _ffn_fwd(x1, mod8, wg, wu, wd, target, ln2w, ln2b, seq):
    t = x1.shape[0]
    tm, tf = min(512, seq), 256
    tpb = seq // tm
    nf = DFF // tf
    nbatch = t // seq

    def body(x_ref, mod_ref, wg_ref, wu_ref, wd_ref, t_ref, lw_ref, lb_ref,
             a_ref, b_ref, dz_ref, st_ref, dm_ref, h_s, acc):
        i, j = pl.program_id(0), pl.program_id(1)

        @pl.when(j == 0)
        def _():
            h_s[...] = (x_ref[...] * (1.0 + mod_ref[0, 4:5, :]) + mod_ref[0, 3:4, :]).astype(BF16)
            acc[...] = jnp.zeros_like(acc)

        a = jnp.dot(h_s[...], wg_ref[...], preferred_element_type=F32)
        b = jnp.dot(h_s[...], wu_ref[...], preferred_element_type=F32)
        a_ref[...] = a.astype(BF16)
        b_ref[...] = b.astype(BF16)
        acc[...] += _dot(a * jax.nn.sigmoid(a) * b, wd_ref[...])

        @pl.when(j == nf - 1)
        def _():
            ffn = acc[...]
            xhat, rstd = _ln_stats(ALPHA * x_ref[...] + (1.0 + mod_ref[0, 5:6, :]) * ffn)
            diff = xhat * lw_ref[...] + lb_ref[...] - t_ref[...]
            loss = 0.5 * jnp.sum(jnp.sum(diff * diff, axis=-1, keepdims=True), axis=0, keepdims=True) / D
            dy = diff * (1.0 / D)
            dz = _ln_bwd(dy, xhat, rstd, lw_ref[...])
            dz_ref[...] = dz
            lane = lax.broadcasted_iota(jnp.int32, (1, D), 1)
            upd = jnp.concatenate(
                [jnp.sum(dy * xhat, axis=0, keepdims=True), jnp.sum(dy, axis=0, keepdims=True),
                 jnp.where(lane == 0, loss, 0.0), jnp.zeros((5, D), F32)], axis=0)
            dmu = jnp.concatenate(
                [jnp.zeros((5, D), F32), jnp.sum(dz * ffn, axis=0, keepdims=True), jnp.zeros((2, D), F32)], axis=0)

            @pl.when(i == 0)
            def _():
                st_ref[...] = upd

            @pl.when(i > 0)
            def _():
                st_ref[...] += upd

            @pl.when(i % tpb == 0)
            def _():
                dm_ref[0] = dmu

            @pl.when(i % tpb != 0)
            def _():
                dm_ref[0] += dmu

    row = lambda: pl.BlockSpec((tm, D), lambda i, j: (i, 0))
    vec = lambda: pl.BlockSpec((1, D), lambda i, j: (0, 0))
    return pl.pallas_call(
        body, name="ffn_fwd", grid=(t // tm, nf),
        in_specs=[row(), pl.BlockSpec((1, 8, D), lambda i, j: (i // tpb, 0, 0)),
                  pl.BlockSpec((D, tf), lambda i, j: (0, j)), pl.BlockSpec((D, tf), lambda i, j: (0, j)),
                  pl.BlockSpec((tf, D), lambda i, j: (j, 0)), row(), vec(), vec()],
        out_specs=[pl.BlockSpec((tm, tf), lambda i, j: (i, j)), pl.BlockSpec((tm, tf), lambda i, j: (i, j)),
                   row(), pl.BlockSpec((8, D), lambda i, j: (0, 0)),
                   pl.BlockSpec((1, 8, D), lambda i, j: (i // tpb, 0, 0))],
        out_shape=[jax.ShapeDtypeStruct((t, DFF), BF16), jax.ShapeDtypeStruct((t, DFF), BF16),
                   jax.ShapeDtypeStruct((t, D), F32), jax.ShapeDtypeStruct((8, D), F32),
                   jax.ShapeDtypeStruct((nbatch, 8, D), F32)],
        scratch_shapes=[pltpu.VMEM((tm, D), BF16), pltpu.VMEM((tm, D), F32)],
        compiler_params=_params(("arbitrary", "arbitrary"), 48),
    )(x1, mod8, wg, wu, wd, target, ln2w, ln2b)


def _ffn_bwd(dz2, a, b, wg, wu, wd, x1, x2, u, mod8, ln1w, seq):
    t = x1.shape[0]
    tm, tf = min(512, seq), 256
    tpb = seq // tm
    nf = DFF // tf
    nbatch = t // seq

    def body(dz_ref, a_ref, b_ref, wg_ref, wu_ref, wd_ref, x1_ref, x_ref, u_ref, mod_ref, lw_ref,
             da_ref, db_ref, hm_ref, df_ref, du_ref, dxp_ref, st_ref, dm_ref, acc):
        i, j = pl.program_id(0), pl.program_id(1)

        @pl.when(j == 0)
        def _():
            df_ref[...] = ((1.0 + mod_ref[0, 5:6, :]) * dz_ref[...]).astype(BF16)
            acc[...] = jnp.zeros_like(acc)

        dhm = _dot_nt(df_ref[...], wd_ref[...])
        av = a_ref[...].astype(F32)
        bv = b_ref[...].astype(F32)
        sg = jax.nn.sigmoid(av)
        sl = av * sg
        hm_ref[...] = (sl * bv).astype(BF16)
        da = (dhm * bv * (sg * (1.0 + av * (1.0 - sg)))).astype(BF16)
        db = (dhm * sl).astype(BF16)
        da_ref[...] = da
        db_ref[...] = db
        acc[...] += _dot_nt(da, wg_ref[...]) + _dot_nt(db, wu_ref[...])

        @pl.when(j == nf - 1)
        def _():
            dh2 = acc[...]
            x1v = x1_ref[...]
            uv = u_ref[...]
            dx1 = ALPHA * dz_ref[...] + dh2 * (1.0 + mod_ref[0, 4:5, :])
            xhat, rstd = _ln_stats(ALPHA * x_ref[...] + (1.0 + mod_ref[0, 2:3, :]) * uv)
            dz1 = _ln_bwd(dx1, xhat, rstd, lw_ref[...])
            du_ref[...] = ((1.0 + mod_ref[0, 2:3, :]) * dz1).astype(BF16)
            dxp_ref[...] = ALPHA * dz1
            upd = jnp.concatenate(
                [jnp.sum(dx1 * xhat, axis=0, keepdims=True), jnp.sum(dx1, axis=0, keepdims=True),
                 jnp.zeros((6, D), F32)], axis=0)
            dmu = jnp.concatenate(
                [jnp.zeros((2, D), F32), jnp.sum(dz1 * uv, axis=0, keepdims=True),
                 jnp.sum(dh2, axis=0, keepdims=True), jnp.sum(dh2 * x1v, axis=0, keepdims=True),
                 jnp.zeros((3, D), F32)], axis=0)

            @pl.when(i == 0)
            def _():
                st_ref[...] = upd

            @pl.when(i > 0)
            def _():
                st_ref[...] += upd

            @pl.when(i % tpb == 0)
            def _():
                dm_ref[0] = dmu

            @pl.when(i % tpb != 0)
            def _():
                dm_ref[0] += dmu

    row = lambda: pl.BlockSpec((tm, D), lambda i, j: (i, 0))
    ffb = lambda: pl.BlockSpec((tm, tf), lambda i, j: (i, j))
    return pl.pallas_call(
        body, name="ffn_bwd", grid=(t // tm, nf),
        in_specs=[row(), ffb(), ffb(),
                  pl.BlockSpec((D, tf), lambda i, j: (0, j)), pl.BlockSpec((D, tf), lambda i, j: (0, j)),
                  pl.BlockSpec((tf, D), lambda i, j: (j, 0)), row(), row(), row(),
                  pl.BlockSpec((1, 8, D), lambda i, j: (i // tpb, 0, 0)), pl.BlockSpec((1, D), lambda i, j: (0, 0))],
        out_specs=[ffb(), ffb(), ffb(), row(), row(), row(), pl.BlockSpec((8, D), lambda i, j: (0, 0)),
                   pl.BlockSpec((1, 8, D), lambda i, j: (i // tpb, 0, 0))],
        out_shape=[jax.ShapeDtypeStruct((t, DFF), BF16), jax.ShapeDtypeStruct((t, DFF), BF16),
                   jax.ShapeDtypeStruct((t, DFF), BF16), jax.ShapeDtypeStruct((t, D), BF16),
                   jax.ShapeDtypeStruct((t, D), BF16), jax.ShapeDtypeStruct((t, D), F32),
                   jax.ShapeDtypeStruct((8, D), F32), jax.ShapeDtypeStruct((nbatch, 8, D), F32)],
        scratch_shapes=[pltpu.VMEM((tm, D), F32)],
        compiler_params=_params(("arbitrary", "arbitrary"), 48),
    )(dz2, a, b, wg, wu, wd, x1, x2, u, mod8, ln1w)


def _adamw_math(w, g, m, v):
    m = B1 * m + (1.0 - B1) * g
    v = B2 * v + (1.0 - B2) * (g * g)
    m_hat = m / (1.0 - B1 ** STEP)
    v_hat = v / (1.0 - B2 ** STEP)
    return -LR * (m_hat / (jnp.sqrt(v_hat) + EPS) + WD * w), m, v


def _adamw(w, g, m, v, name):
    rows, cols = w.shape
    tr = rows
    for cand in (128, 64, 32, 16, 8):
        if rows % cand == 0:
            tr = cand
            break

    def body(w_ref, g_ref, m_ref, v_ref, d_ref, mo_ref, vo_ref):
        d, mn, vn = _adamw_math(w_ref[...], g_ref[...], m_ref[...], v_ref[...])
        d_ref[...] = d
        mo_ref[...] = mn
        vo_ref[...] = vn

    spec = pl.BlockSpec((tr, cols), lambda i: (i, 0))
    return pl.pallas_call(
        body, name=name, grid=(rows // tr,), in_specs=[spec] * 4, out_specs=[spec] * 3,
        out_shape=[jax.ShapeDtypeStruct((rows, cols), F32)] * 3,
        compiler_params=_params(("parallel",), 48),
    )(w, g, m, v)


def _grad_w_ada(c_all, dmod_cols):
    def body(c_ref, d_ref, o_ref):
        c = c_ref[...]
        o_ref[...] = lax.dot_general(c * jax.nn.sigmoid(c), d_ref[...], (((0,), (0,)), ((), ())),
                                     preferred_element_type=F32, precision=HIGHEST)

    return pl.pallas_call(
        body, name="grad_w_ada", out_shape=jax.ShapeDtypeStruct((D, dmod_cols.shape[1]), F32),
        compiler_params=_params(vmem_mb=48),
    )(c_all, dmod_cols)


def _small_update(gath, w8, m8, v8):
    def body(g_ref, w_ref, m_ref, v_ref, go_ref, d_ref, mo_ref, vo_ref):
        g0 = g_ref[0, 0:1, :] + g_ref[0, 1:2, :]
        g1 = g_ref[0, 2:3, :]
        for dev in range(1, N_DEV):
            g0 = g0 + (g_ref[dev, 0:1, :] + g_ref[dev, 1:2, :])
            g1 = g1 + g_ref[dev, 2:3, :]
        w = w_ref[...]
        lb = jax.nn.sigmoid(w[1:2, O_LB0:O_LB1] - w[1:2, O_LB1:O_FOX])
        fac = lb * (1.0 - lb)
        g1 = jnp.concatenate([g1[:, :O_LB0], g1[:, O_LB0:O_LB1] * fac, -g1[:, O_LB1:O_FOX] * fac, g1[:, O_FOX:]],
                             axis=1)
        g = jnp.concatenate([g0, g1, jnp.zeros((6, SMALL_W), F32)], axis=0)
        d, mn, vn = _adamw_math(w, g, m_ref[...], v_ref[...])
        go_ref[...] = g
        d_ref[...] = d
        mo_ref[...] = mn
        vo_ref[...] = vn

    return pl.pallas_call(
        body, name="small_update", out_shape=[jax.ShapeDtypeStruct((8, SMALL_W), F32)] * 4,
        compiler_params=_params(vmem_mb=48),
    )(gath, w8, m8, v8)


def _pack_small(b_ada, ln1w, ln1b, ln2w, ln2b, norm_w, lb_logits, fox):
    row1 = jnp.concatenate([ln1w, ln1b, ln2w, ln2b, norm_w, lb_logits[0:1], lb_logits[1:2], fox,
                            jnp.zeros((1, SMALL_W - O_FOX - BH), F32)], axis=1)
    return jnp.concatenate([b_ada, row1, jnp.zeros((6, SMALL_W), F32)], axis=0)


def _unpack_small(p):
    r = p[1:2]
    lb = jnp.concatenate([r[:, O_LB0:O_LB1], r[:, O_LB1:O_FOX]], axis=0)
    return dict(b_ada=p[0:1], ln1_w=r[:, O_LN1W:O_LN1B], ln1_b=r[:, O_LN1B:O_LN2W], ln2_w=r[:, O_LN2W:O_LN2B],
                ln2_b=r[:, O_LN2B:O_NORM], hgrn_norm_w=r[:, O_NORM:O_LB0], lb_logits=lb,
                fox_f_bias=r[:, O_FOX:O_FOX + BH])


_BIG = ("w_in", "w_branch_a", "w_branch_b", "w_out", "w_ffn_gate", "w_ffn_up", "w_ffn_down")
_BIG_SHARD_SHAPES = dict(w_in=(D, NIN // 4), w_branch_a=(AW, D // 4), w_branch_b=(AW, D // 4), w_out=(D // 4, D),
                         w_ffn_gate=(D, DFF // 4), w_ffn_up=(D, DFF // 4), w_ffn_down=(DFF // 4, D))
_ROW_SHARDED = ("w_out", "w_ffn_down")


def _pack_shard(parts, dtype):
    flat = jnp.concatenate([parts[n].astype(dtype).reshape(-1) for n in _BIG])
    flat = jnp.concatenate([flat, jnp.zeros((PACK_ROWS * 1024 - flat.shape[0],), dtype)])
    return flat.reshape(PACK_ROWS, 1024)


def _unpack_shard(pack):
    flat = pack.reshape(-1)
    out, pos = {}, 0
    for n in _BIG:
        shp = _BIG_SHARD_SHAPES[n]
        size = shp[0] * shp[1]
        out[n] = flat[pos:pos + size].reshape(shp)
        pos += size
    return out


def _full_from_shards(gathered):
    per = [_unpack_shard(gathered[k]) for k in range(N_CHIPS)]
    return {n: jnp.concatenate([p[n] for p in per], axis=0 if n in _ROW_SHARDED else 1) for n in _BIG}


def _shards_from_full(full, dtype):
    packs = []
    for k in range(N_CHIPS):
        parts = {}
        for n in _BIG:
            shp = _BIG_SHARD_SHAPES[n]
            if n in _ROW_SHARDED:
                parts[n] = full[n][k * shp[0]:(k + 1) * shp[0], :]
            else:
                parts[n] = full[n][:, k * shp[1]:(k + 1) * shp[1]]
        packs.append(_pack_shard(parts, dtype))
    return jnp.stack(packs)


def kernel(x, c, w_ada, b_ada, w_in, fox_f_bias, lb_logits, hgrn_norm_w, w_branch_a, w_branch_b, w_out, ln1_w, ln1_b, w_ffn_gate, w_ffn_up, w_ffn_down, ln2_w, ln2_b, loss_target, m_w_ada, m_b_ada, m_w_in, m_fox_f_bias, m_lb_logits, m_hgrn_norm_w, m_w_branch_a, m_w_branch_b, m_w_out, m_ln1_w, m_ln1_b, m_w_ffn_gate, m_w_ffn_up, m_w_ffn_down, m_ln2_w, m_ln2_b, v_w_ada, v_b_ada, v_w_in, v_fox_f_bias, v_lb_logits, v_hgrn_norm_w, v_w_branch_a, v_w_branch_b, v_w_out, v_ln1_w, v_ln1_b, v_w_ffn_gate, v_w_ffn_up, v_w_ffn_down, v_ln2_w, v_ln2_b):
    nbatch, seq, _ = x.shape
    t = nbatch * seq
    ax, ay, ac = lax.axis_index("x"), lax.axis_index("y"), lax.axis_index("c")
    chip = 2 * ax + ay
    dev = 2 * chip + ac
    chip_arr = jnp.reshape(chip, (1,)).astype(jnp.int32)
    core_arr = jnp.reshape(ac, (1,)).astype(jnp.int32)

    shard_w = dict(w_in=w_in[0], w_branch_a=w_branch_a[0], w_branch_b=w_branch_b[0], w_out=w_out[0],
                   w_ffn_gate=w_ffn_gate[0], w_ffn_up=w_ffn_up[0], w_ffn_down=w_ffn_down[0])
    shard_m = dict(w_in=m_w_in[0], w_branch_a=m_w_branch_a[0], w_branch_b=m_w_branch_b[0], w_out=m_w_out[0],
                   w_ffn_gate=m_w_ffn_gate[0], w_ffn_up=m_w_ffn_up[0], w_ffn_down=m_w_ffn_down[0])
    shard_v = dict(w_in=v_w_in[0], w_branch_a=v_w_branch_a[0], w_branch_b=v_w_branch_b[0], w_out=v_w_out[0],
                   w_ffn_gate=v_w_ffn_gate[0], w_ffn_up=v_w_ffn_up[0], w_ffn_down=v_w_ffn_down[0])

    my_pack = _pack_shard(shard_w, BF16)
    full = _full_from_shards(lax.dynamic_update_slice(_gather_weights(my_pack), my_pack[None], (chip, 0, 0)))
    w_p = _permute_cols(full["w_in"])
    wba, wbb, wout = full["w_branch_a"], full["w_branch_b"], full["w_out"]
    wg, wu, wd = full["w_ffn_gate"], full["w_ffn_up"], full["w_ffn_down"]

    c8 = jnp.concatenate([c, jnp.zeros((8 - nbatch, D), F32)], axis=0)
    c_all = _allgather8(c8, "gather_c")[:, :nbatch, :].reshape(N_DEV * nbatch, D)
    ncol = w_ada.shape[2]
    b_cols = lax.dynamic_slice_in_dim(b_ada, chip * ncol, ncol, axis=1)
    mod_g = _allgather8(_mod_shard(c_all, w_ada[0], b_cols), "gather_mod")
    mod_all = jnp.concatenate([mod_g[2 * k] for k in range(N_CHIPS)], axis=1)
    mod_mine = lax.dynamic_slice_in_dim(mod_all, dev * nbatch, nbatch, axis=0)
    mod8 = jnp.concatenate([mod_mine.reshape(nbatch, 6, D), jnp.zeros((nbatch, 2, D), F32)], axis=1)

    x2 = x.reshape(t, D)
    tgt2 = loss_target.reshape(t, D)
    bias128 = jnp.concatenate([fox_f_bias, jnp.zeros((1, 128 - BH), F32)], axis=1)

    proj = _proj(x2, mod8, w_p, seq)
    ya, ckpt = _hgrn_fwd(proj, lb_logits, hgrn_norm_w, nbatch, seq)
    cum_rows, cum_cols = _fox_cum(proj, bias128, nbatch, seq)
    yb, lse = _fox_fwd(proj, cum_rows, cum_cols, nbatch, seq)
    merged, u, x1 = _merge_fwd(ya, yb, proj, x2, mod8, wba, wbb, wout, ln1_w, ln1_b, seq)
    a_pre, b_pre, dz2, st2, dm2 = _ffn_fwd(x1, mod8, wg, wu, wd, tgt2, ln2_w, ln2_b, seq)
    loss = lax.psum(st2[2, 0], ("x", "y", "c"))

    da, db, hmid, dffn, du, dxp, st1, dm1 = _ffn_bwd(dz2, a_pre, b_pre, wg, wu, wd, x1, x2, u, mod8, ln1_w, seq)
    g_full = {}
    g_full["w_ffn_down"] = _tn_matmul(hmid, dffn, "dw_ffn_down", seq)
    g_full["w_ffn_gate"] = _tn_matmul(x1, da, "dw_ffn_gate", seq, mod8, (3, 4))
    g_full["w_ffn_up"] = _tn_matmul(x1, db, "dw_ffn_up", seq, mod8, (3, 4))
    g_full["w_out"] = _tn_matmul(merged, du, "dw_out", seq)
    dproj, dpa, dpb, dya, dyb = _merge_bwd(du, ya, yb, proj, wba, wbb, wout, seq)
    g_full["w_branch_a"] = _tn_matmul(ya, dpa, "dw_branch_a", seq)
    g_full["w_branch_b"] = _tn_matmul(yb, dpb, "dw_branch_b", seq)
    dproj, dq, drs, dcs = _fox_bwd(proj, cum_rows, cum_cols, lse, yb, dyb, dproj, nbatch, seq)
    dproj = _place_cols(dproj, dq, COL_BQ)
    dproj, sm_fox = _fox_dbf(proj, bias128, drs, dcs, dproj, nbatch, seq)
    dproj, sm_hgrn = _hgrn_bwd(proj, dya, ckpt, lb_logits, hgrn_norm_w, dproj, nbatch, seq)
    grad_x2, dm0 = _dh_kernel(dproj, w_p, x2, dxp, mod8, seq)
    g_full["w_in"] = _unpermute_cols(_tn_matmul(x2, dproj, "dw_in", seq, mod8, (0, 1)))

    gpack = _shards_from_full(g_full, F32)
    red_c, red_c16 = _add_my_half(gpack, _swap_halves(gpack), core_arr)
    red = _add_chips(red_c, _scatter_chips(red_c16), chip_arr)
    both = jnp.zeros((PACK_ROWS, 1024), F32)
    both = lax.dynamic_update_slice(both, red, (ac * HALF, 0))
    both = lax.dynamic_update_slice(both, _join_halves(red), ((1 - ac) * HALF, 0))
    g_shard = _unpack_shard(both)

    dmod = (dm0 + dm1 + dm2)[:, :6, :].reshape(nbatch, 6 * D)
    row2 = jnp.concatenate([st1[0:1], st1[1:2], st2[0:1], st2[1:2], sm_hgrn[1:2], sm_hgrn[0:1], sm_hgrn[0:1],
                            sm_fox[0:1, :BH], jnp.zeros((1, SMALL_W - O_FOX - BH), F32)], axis=1)
    spack = jnp.concatenate([dmod, row2, jnp.zeros((8 - nbatch - 1, SMALL_W), F32)], axis=0)
    gath = _allgather8(spack, "gather_small")
    w8 = _pack_small(b_ada, ln1_w, ln1_b, ln2_w, ln2_b, hgrn_norm_w, lb_logits, fox_f_bias)
    m8 = _pack_small(m_b_ada, m_ln1_w, m_ln1_b, m_ln2_w, m_ln2_b, m_hgrn_norm_w, m_lb_logits, m_fox_f_bias)
    v8 = _pack_small(v_b_ada, v_ln1_w, v_ln1_b, v_ln2_w, v_ln2_b, v_hgrn_norm_w, v_lb_logits, v_fox_f_bias)
    sg, sd, smn, svn = (_unpack_small(p) for p in _small_update(gath, w8, m8, v8))
    dmod_all = gath[:, :nbatch, :].reshape(N_DEV * nbatch, SMALL_W)
    g_ada = _grad_w_ada(c_all, lax.dynamic_slice_in_dim(dmod_all, chip * ncol, ncol, axis=1))

    grads = dict(sg)
    deltas = dict(sd)
    new_m = dict(smn)
    new_v = dict(svn)
    grads["w_ada"] = g_ada
    deltas["w_ada"], new_m["w_ada"], new_v["w_ada"] = _adamw(w_ada[0], g_ada, m_w_ada[0], v_w_ada[0], "adamw_w_ada")
    for n in _BIG:
        grads[n] = g_shard[n]
        deltas[n], new_m[n], new_v[n] = _adamw(shard_w[n], g_shard[n], shard_m[n], shard_v[n], "adamw_" + n)

    names = ["w_ada", "b_ada", "w_in", "fox_f_bias", "lb_logits", "hgrn_norm_w", "w_branch_a", "w_branch_b", "w_out",
             "ln1_w", "ln1_b", "w_ffn_gate", "w_ffn_up", "w_ffn_down", "ln2_w", "ln2_b"]
    shapes = dict(w_ada=w_ada.shape, b_ada=b_ada.shape, w_in=w_in.shape, fox_f_bias=fox_f_bias.shape,
                  lb_logits=lb_logits.shape, hgrn_norm_w=hgrn_norm_w.shape, w_branch_a=w_branch_a.shape,
                  w_branch_b=w_branch_b.shape, w_out=w_out.shape, ln1_w=ln1_w.shape, ln1_b=ln1_b.shape,
                  w_ffn_gate=w_ffn_gate.shape, w_ffn_up=w_ffn_up.shape, w_ffn_down=w_ffn_down.shape,
                  ln2_w=ln2_w.shape, ln2_b=ln2_b.shape)
    outs = [loss, grad_x2.reshape(x.shape)]
    for group in (grads, deltas, new_m, new_v):
        outs += [group[n].reshape(shapes[n]) for n in names]
    return tuple(outs)
```

```python
import functools
import math

import jax
import jax.numpy as jnp
import numpy as np
from jax import lax
from jax.experimental import pallas as pl
from jax.experimental.pallas import tpu as pltpu

F32 = jnp.float32
BF16 = jnp.bfloat16
MESH = pl.DeviceIdType.MESH
HIGHEST = lax.Precision.HIGHEST

D = 1024
AW = 512
AH = 4
ADH = 128
BH = 8
BDH = 64
DFF = 2816
NIN = 5640
NP = 5760
N_CHIPS = 4
N_DEV = 8
HGRN_BLOCK = 256
COL_GATES = 0
COL_BQ = 2048
COL_KV = 2560
COL_A = 3584
COL_BF = 5632
ALPHA = 2.0 ** 0.25
LN_EPS = 1e-5
RMS_EPS = 1e-6
NEG = -1e30
LR, B1, B2, EPS, WD, STEP = 0.001, 0.9, 0.999, 1e-08, 0.01, 10
SMALL_W = 6144
O_LN1W, O_LN1B, O_LN2W, O_LN2B, O_NORM, O_LB0, O_LB1, O_FOX = 0, 1024, 2048, 3072, 4096, 4608, 5120, 5632


def _params(sem=None, vmem_mb=None):
    kw = {}
    if sem is not None:
        kw["dimension_semantics"] = sem
    if vmem_mb is not None:
        kw["vmem_limit_bytes"] = vmem_mb << 20
    return pltpu.CompilerParams(**kw)


def _dot(a, b):
    return jnp.dot(a.astype(BF16), b.astype(BF16), preferred_element_type=F32)


def _dot_nt(a, b):
    return lax.dot_general(a.astype(BF16), b.astype(BF16), (((1,), (1,)), ((), ())), preferred_element_type=F32)


def _dot_tn(a, b):
    return lax.dot_general(a.astype(BF16), b.astype(BF16), (((0,), (0,)), ((), ())), preferred_element_type=F32)


def _dot_f32(a, b):
    return jnp.dot(a, b, preferred_element_type=F32, precision=HIGHEST)


def _perm_segments():
    segs = [(3592, 5640), (2048, 2560)]
    for p in range(4):
        segs += [(2560 + 128 * p, 2688 + 128 * p), (3072 + 128 * p, 3200 + 128 * p)]
    for h in range(4):
        segs += [(128 * h + 512 * t, 128 * h + 512 * t + 128) for t in range(4)]
    segs += [(3584, 3592)]
    return segs


def _permute_cols(w):
    parts = [w[:, a:b] for a, b in _perm_segments()]
    parts.append(jnp.zeros((w.shape[0], NP - NIN), w.dtype))
    return jnp.concatenate(parts, axis=1)


def _unpermute_cols(g):
    pos, where = 0, {}
    for a, b in _perm_segments():
        where[a] = (pos, pos + b - a)
        pos += b - a
    parts = [g[:, where[a][0]:where[a][1]] for a in sorted(where)]
    return jnp.concatenate(parts, axis=1)


def _allgather8(v, name):
    rows, cols = v.shape

    def body(x_ref, out_ref, send_sems, recv_sems, local_sem):
        x, y, c = lax.axis_index("x"), lax.axis_index("y"), lax.axis_index("c")
        me, sibling = (x, y, c), (x, y, 1 - c)
        chips = [(1 - x, y), (x, 1 - y), (1 - x, 1 - y)]

        def slot(px, py, pc):
            return out_ref.at[4 * px + 2 * py + pc]

        def copy(k, block, to, src=None):
            return pltpu.make_async_remote_copy(
                src_ref=slot(*block) if src is None else src, dst_ref=slot(*block),
                send_sem=send_sems.at[k], recv_sem=recv_sems.at[k], device_id=to, device_id_type=MESH)

        mine = pltpu.make_async_copy(x_ref, slot(*me), local_sem)
        mine.start()
        first = [copy(0, me, sibling, src=x_ref)]
        first += [copy(1 + j, me, (*chip, c), src=x_ref) for j, chip in enumerate(chips)]
        for cp in first:
            cp.start()
        passed = [copy(4 + j, (*chip, c), sibling) for j, chip in enumerate(chips)]
        for j, chip in enumerate(chips):
            copy(1 + j, (*chip, c), me).wait_recv()
            passed[j].start()
        copy(0, sibling, me).wait_recv()
        for j, chip in enumerate(chips):
            copy(4 + j, (*chip, 1 - c), me).wait_recv()
        for cp in first + passed:
            cp.wait_send()
        mine.wait()

    return pl.pallas_call(
        body, name=name,
        out_shape=jax.ShapeDtypeStruct((N_DEV, rows, cols), v.dtype),
        in_specs=[pl.BlockSpec(memory_space=pltpu.VMEM)],
        out_specs=pl.BlockSpec(memory_space=pltpu.VMEM),
        scratch_shapes=[pltpu.SemaphoreType.DMA((7,)), pltpu.SemaphoreType.DMA((7,)), pltpu.SemaphoreType.DMA],
    )(v)


def _hbm_specs(n):
    return [pl.BlockSpec(memory_space=pl.ANY)] * n


def _gather_weights(shards):
    n = len(shards)

    def body(*refs):
        ins, outs, (send_sems, recv_sems) = refs[:n], refs[n:2 * n], refs[2 * n:]
        x, y, c = lax.axis_index("x"), lax.axis_index("y"), lax.axis_index("c")
        sibling = (x, y, 1 - c)
        chips = [(1 - x, y), (x, 1 - y), (1 - x, 1 - y)]

        def blk(w, px, py, half):
            hr = ins[w].shape[0] // 2
            return outs[w].at[2 * px + py, pl.ds(half * hr, hr), :]

        def copy(w, k, block, to, src=None):
            return pltpu.make_async_remote_copy(
                src_ref=blk(w, *block) if src is None else src, dst_ref=blk(w, *block),
                send_sem=send_sems.at[6 * w + k], recv_sem=recv_sems.at[6 * w + k], device_id=to, device_id_type=MESH)

        first = []
        for w in range(n):
            hr = ins[w].shape[0] // 2
            my_half = ins[w].at[pl.ds(c * hr, hr), :]
            first += [copy(w, j, (x, y, c), (*chip, c), src=my_half) for j, chip in enumerate(chips)]
        for cp in first:
            cp.start()
        passed = []
        for j, chip in enumerate(chips):
            for w in range(n):
                copy(w, j, (*chip, c), (x, y, c)).wait_recv()
                passed.append(copy(w, 3 + j, (*chip, c), sibling))
                passed[-1].start()
        for j, chip in enumerate(chips):
            for w in range(n):
                copy(w, 3 + j, (*chip, 1 - c), (x, y, c)).wait_recv()
        for cp in first + passed:
            cp.wait_send()

    return pl.pallas_call(
        body, name="gather_weights",
        out_shape=[jax.ShapeDtypeStruct((N_CHIPS,) + s.shape, s.dtype) for s in shards],
        in_specs=_hbm_specs(n), out_specs=_hbm_specs(n),
        scratch_shapes=[pltpu.SemaphoreType.DMA((6 * n,)), pltpu.SemaphoreType.DMA((6 * n,))],
    )(*shards)


def _swap_halves(grads):
    n = len(grads)

    def body(*refs):
        ins, outs, (send_sems, recv_sems) = refs[:n], refs[n:2 * n], refs[2 * n:]
        x, y, c = lax.axis_index("x"), lax.axis_index("y"), lax.axis_index("c")
        cps = []
        for w in range(n):
            hr = ins[w].shape[1] // 2
            cps.append(pltpu.make_async_remote_copy(
                src_ref=ins[w].at[:, pl.ds((1 - c) * hr, hr), :], dst_ref=outs[w],
                send_sem=send_sems.at[w], recv_sem=recv_sems.at[w], device_id=(x, y, 1 - c), device_id_type=MESH))
        for cp in cps:
            cp.start()
        for cp in cps:
            cp.wait()

    return pl.pallas_call(
        body, name="grad_swap_halves",
        out_shape=[jax.ShapeDtypeStruct((N_CHIPS, g.shape[1] // 2, g.shape[2]), g.dtype) for g in grads],
        in_specs=_hbm_specs(n), out_specs=_hbm_specs(n),
        scratch_shapes=[pltpu.SemaphoreType.DMA((n,)), pltpu.SemaphoreType.DMA((n,))],
    )(*grads)


def _scatter_chips(reds):
    n = len(reds)

    def body(*refs):
        ins, outs, (send_sems, recv_sems) = refs[:n], refs[n:2 * n], refs[2 * n:]
        x, y, c = lax.axis_index("x"), lax.axis_index("y"), lax.axis_index("c")
        chips = [(1 - x, y), (x, 1 - y), (1 - x, 1 - y)]
        cps = [pltpu.make_async_remote_copy(
            src_ref=ins[w].at[2 * chip[0] + chip[1]], dst_ref=outs[w].at[j],
            send_sem=send_sems.at[3 * w + j], recv_sem=recv_sems.at[3 * w + j],
            device_id=(*chip, c), device_id_type=MESH)
            for j, chip in enumerate(chips) for w in range(n)]
        for cp in cps:
            cp.start()
        for cp in cps:
            cp.wait()

    return pl.pallas_call(
        body, name="grad_scatter_chips",
        out_shape=[jax.ShapeDtypeStruct((3,) + r.shape[1:], r.dtype) for r in reds],
        in_specs=_hbm_specs(n), out_specs=_hbm_specs(n),
        scratch_shapes=[pltpu.SemaphoreType.DMA((3 * n,)), pltpu.SemaphoreType.DMA((3 * n,))],
    )(*reds)


def _join_halves(halves):
    n = len(halves)

    def body(*refs):
        ins, outs, (send_sems, recv_sems) = refs[:n], refs[n:2 * n], refs[2 * n:]
        x, y, c = lax.axis_index("x"), lax.axis_index("y"), lax.axis_index("c")
        cps = [pltpu.make_async_remote_copy(
            src_ref=ins[w], dst_ref=outs[w], send_sem=send_sems.at[w], recv_sem=recv_sems.at[w],
            device_id=(x, y, 1 - c), device_id_type=MESH) for w in range(n)]
        for cp in cps:
            cp.start()
        for cp in cps:
            cp.wait()

    return pl.pallas_call(
        body, name="grad_join_halves",
        out_shape=[jax.ShapeDtypeStruct(h.shape, h.dtype) for h in halves],
        in_specs=_hbm_specs(n), out_specs=_hbm_specs(n),
        scratch_shapes=[pltpu.SemaphoreType.DMA((n,)), pltpu.SemaphoreType.DMA((n,))],
    )(*halves)


def _row_tile(rows):
    for cand in (256, 176, 128, 64, 32, 16):
        if rows % cand == 0:
            return cand
    raise ValueError(rows)


def _add_my_half(g, other, c_idx, name):
    _, k, n = g.shape
    hr = k // 2
    tr = _row_tile(hr)
    nb = hr // tr

    def body(c_ref, g_ref, o_ref, out_ref, out16_ref):
        s = g_ref[...] + o_ref[...]
        out_ref[...] = s
        out16_ref[...] = s.astype(BF16)

    return pl.pallas_call(
        body, name=name,
        grid_spec=pltpu.PrefetchScalarGridSpec(
            num_scalar_prefetch=1, grid=(N_CHIPS, nb),
            in_specs=[pl.BlockSpec((1, tr, n), lambda j, i, c: (j, c[0] * nb + i, 0)),
                      pl.BlockSpec((1, tr, n), lambda j, i, c: (j, i, 0))],
            out_specs=[pl.BlockSpec((1, tr, n), lambda j, i, c: (j, i, 0)),
                       pl.BlockSpec((1, tr, n), lambda j, i, c: (j, i, 0))]),
        out_shape=[jax.ShapeDtypeStruct((N_CHIPS, hr, n), F32), jax.ShapeDtypeStruct((N_CHIPS, hr, n), BF16)],
        compiler_params=_params(("parallel", "parallel")),
    )(c_idx, g, other)


def _add_chips(red, recv, chip_idx, name):
    _, hr, n = red.shape
    tr = _row_tile(hr)

    def body(k_ref, r_ref, v_ref, out_ref):
        out_ref[...] = ((r_ref[0] + v_ref[0].astype(F32)) + v_ref[1].astype(F32)) + v_ref[2].astype(F32)

    return pl.pallas_call(
        body, name=name,
        grid_spec=pltpu.PrefetchScalarGridSpec(
            num_scalar_prefetch=1, grid=(hr // tr,),
            in_specs=[pl.BlockSpec((1, tr, n), lambda i, k: (k[0], i, 0)),
                      pl.BlockSpec((3, tr, n), lambda i, k: (0, i, 0))],
            out_specs=pl.BlockSpec((tr, n), lambda i, k: (i, 0))),
        out_shape=jax.ShapeDtypeStruct((hr, n), F32),
        compiler_params=_params(("parallel",)),
    )(chip_idx, red, recv)


def _mod_shard(c_all, w_ada, b_ada):
    nb, cols = c_all.shape[0], w_ada.shape[1]

    def body(c_ref, w_ref, b_ref, o_ref):
        c = c_ref[...]
        o_ref[...] = _dot(c * jax.nn.sigmoid(c), w_ref[...]) + b_ref[...]

    return pl.pallas_call(
        body, name="mod_shard", out_shape=jax.ShapeDtypeStruct((nb, cols), F32),
        compiler_params=_params(vmem_mb=48),
    )(c_all, w_ada, b_ada)


def _proj(x2, mod8, w_p, seq):
    t = x2.shape[0]
    tm, tn = min(512, seq), 1152
    tpb = seq // tm

    def body(x_ref, mod_ref, w_ref, o_ref, h_scr):
        @pl.when(pl.program_id(1) == 0)
        def _():
            h_scr[...] = (x_ref[...] * (1.0 + mod_ref[0, 1:2, :]) + mod_ref[0, 0:1, :]).astype(BF16)
        o_ref[...] = jnp.dot(h_scr[...], w_ref[...], preferred_element_type=F32)

    return pl.pallas_call(
        body, name="proj", grid=(t // tm, NP // tn),
        in_specs=[pl.BlockSpec((tm, D), lambda i, j: (i, 0)),
                  pl.BlockSpec((1, 8, D), lambda i, j: (i // tpb, 0, 0)),
                  pl.BlockSpec((D, tn), lambda i, j: (0, j))],
        out_specs=pl.BlockSpec((tm, tn), lambda i, j: (i, j)),
        out_shape=jax.ShapeDtypeStruct((t, NP), F32),
        scratch_shapes=[pltpu.VMEM((tm, D), BF16)],
        compiler_params=_params(("parallel", "arbitrary"), 48),
    )(x2, mod8, w_p)


def _tn_matmul(a, b, name, seq, mod8=None, rows=None, split=None):
    a_st, b_st = a.ndim == 3, b.ndim == 3
    t, ka = a.shape[-2:]
    n = b.shape[-1]
    tt = min(1024, seq)
    tpb = seq // tt
    nt = t // tt
    if a_st or b_st:
        steps, tn = (a.shape[0] if a_st else b.shape[0]), n
    else:
        tn = split
        if tn is None:
            tn = next(cand for cand in (1152, 1024, 1408, 512, n) if n % cand == 0)
        steps = n // tn
    stacked_out = a_st or b_st or split is not None

    def body(*refs):
        if mod8 is None:
            a_ref, b_ref, o_ref = refs
            av = a_ref[0] if a_st else a_ref[...]
        else:
            a_ref, m_ref, b_ref, o_ref = refs
            av = a_ref[...] * (1.0 + m_ref[0, rows[1]:rows[1] + 1, :]) + m_ref[0, rows[0]:rows[0] + 1, :]
        part = _dot_tn(av, b_ref[0] if b_st else b_ref[...])
        if stacked_out:
            part = part[None]

        @pl.when(pl.program_id(1) == 0)
        def _():
            o_ref[...] = part

        @pl.when(pl.program_id(1) > 0)
        def _():
            o_ref[...] += part

    if a_st:
        in_specs = [pl.BlockSpec((1, tt, ka), lambda j, k: (j, k, 0))]
    else:
        in_specs = [pl.BlockSpec((tt, ka), lambda j, k: (k, 0))]
    args = [a]
    if mod8 is not None:
        in_specs.append(pl.BlockSpec((1, 8, ka), lambda j, k: (k // tpb, 0, 0)))
        args.append(mod8)
    if b_st:
        in_specs.append(pl.BlockSpec((1, tt, n), lambda j, k: (j, k, 0)))
    else:
        in_specs.append(pl.BlockSpec((tt, tn), lambda j, k: (k, 0 if a_st else j)))
    args.append(b)
    if stacked_out:
        out_spec = pl.BlockSpec((1, ka, tn), lambda j, k: (j, 0, 0))
        out_shape = jax.ShapeDtypeStruct((steps, ka, tn), F32)
    else:
        out_spec = pl.BlockSpec((ka, tn), lambda j, k: (0, j))
        out_shape = jax.ShapeDtypeStruct((ka, n), F32)
    return pl.pallas_call(
        body, name=name, grid=(steps, nt), in_specs=in_specs, out_specs=out_spec, out_shape=out_shape,
        compiler_params=_params(("parallel", "arbitrary"), 56),
    )(*args)


def _dh_kernel(dproj, w_p, x2, dxp, mod8, seq):
    t = x2.shape[0]
    tm, tk = min(512, seq), 1152
    tpb = seq // tm
    nk = NP // tk
    nbatch = t // seq

    def body(dp_ref, w_ref, x_ref, dxp_ref, mod_ref, gx_ref, dm_ref, acc):
        i, k = pl.program_id(0), pl.program_id(1)

        @pl.when(k == 0)
        def _():
            acc[...] = jnp.zeros_like(acc)

        acc[...] += _dot_nt(dp_ref[...], w_ref[...])

        @pl.when(k == nk - 1)
        def _():
            dh = acc[...]
            gx_ref[...] = dxp_ref[...] + dh * (1.0 + mod_ref[0, 1:2, :])
            upd = jnp.concatenate(
                [jnp.sum(dh, axis=0, keepdims=True), jnp.sum(dh * x_ref[...], axis=0, keepdims=True),
                 jnp.zeros((6, D), F32)], axis=0)

            @pl.when(i % tpb == 0)
            def _():
                dm_ref[0] = upd

            @pl.when(i % tpb != 0)
            def _():
                dm_ref[0] += upd

    return pl.pallas_call(
        body, name="dh", grid=(t // tm, nk),
        in_specs=[pl.BlockSpec((tm, tk), lambda i, k: (i, k)),
                  pl.BlockSpec((D, tk), lambda i, k: (0, k)),
                  pl.BlockSpec((tm, D), lambda i, k: (i, 0)),
                  pl.BlockSpec((tm, D), lambda i, k: (i, 0)),
                  pl.BlockSpec((1, 8, D), lambda i, k: (i // tpb, 0, 0))],
        out_specs=[pl.BlockSpec((tm, D), lambda i, k: (i, 0)),
                   pl.BlockSpec((1, 8, D), lambda i, k: (i // tpb, 0, 0))],
        out_shape=[jax.ShapeDtypeStruct((t, D), F32), jax.ShapeDtypeStruct((nbatch, 8, D), F32)],
        scratch_shapes=[pltpu.VMEM((tm, D), F32)],
        compiler_params=_params(("arbitrary", "arbitrary"), 48),
    )(dproj, w_p, x2, dxp, mod8)


def _tri(n, upper):
    r = lax.broadcasted_iota(jnp.int32, (n, n), 0)
    c = lax.broadcasted_iota(jnp.int32, (n, n), 1)
    return jnp.where((c >= r) if upper else (c <= r), 1.0, 0.0).astype(F32)


@jax.custom_vjp
def _mm_nn(a, b):
    return _dot(a, b)


_mm_nn.defvjp(lambda a, b: (_dot(a, b), (a, b)),
              lambda res, g: (_dot_nt(g, res[1]), _dot_tn(res[0], g)))


@jax.custom_vjp
def _mm_nt(a, b):
    return _dot_nt(a, b)


_mm_nt.defvjp(lambda a, b: (_dot_nt(a, b), (a, b)),
              lambda res, g: (_dot(g, res[1]), _dot_tn(g, res[0])))


@jax.custom_vjp
def _mm_tn(a, b):
    return _dot_tn(a, b)


_mm_tn.defvjp(lambda a, b: (_dot_tn(a, b), (a, b)),
              lambda res, g: (_dot_nt(res[1], g), _dot(res[0], g)))


@jax.custom_vjp
def _cumsum_rows(x):
    return _dot_f32(_tri(x.shape[0], False), x)


_cumsum_rows.defvjp(lambda x: (_cumsum_rows(x), None),
                    lambda _, g: (_dot_f32(_tri(g.shape[0], True), g),))


@functools.partial(jax.custom_vjp, nondiff_argnums=(1,))
def _shift_rows(x, k):
    return pltpu.roll(x, k % x.shape[0], 0)


_shift_rows.defvjp(lambda x, k: (_shift_rows(x, k), None),
                   lambda k, _, g: (pltpu.roll(g, (-k) % g.shape[0], 0),))


def _group_ref(bc, m):
    n = bc.shape[0] // (2 * m)
    b3 = bc.reshape(n, 2 * m, ADH)
    row = lax.broadcasted_iota(jnp.int32, b3.shape, 1)
    ref = jnp.sum(jnp.where(row == m - 1, b3, 0.0), axis=1, keepdims=True)
    return jnp.broadcast_to(ref, b3.shape).reshape(bc.shape)


def _hgrn_block(q, fl, v, g, st, lb, nw):
    n = q.shape[0]
    f = lb + (1.0 - lb) * jax.nn.sigmoid(fl)
    kk = 1.0 - f
    lf = jnp.log(f)
    bc = _cumsum_rows(lf)
    row = lax.broadcasted_iota(jnp.int32, (n, ADH), 0)
    same = jnp.bitwise_xor(lax.broadcasted_iota(jnp.int32, (n, n), 0), lax.broadcasted_iota(jnp.int32, (n, n), 1))
    a = jnp.zeros((n, n), F32)
    m = 1
    while m < n:
        r = jnp.bitwise_and(row, 2 * m - 1)
        up, lo = r >= m, r < m
        if m == 1:
            aq, ak = lf, jnp.zeros_like(lf)
        elif m == 2:
            aq = jnp.where(r == 3, lf + _shift_rows(lf, 1), lf)
            ak = jnp.where(r == 0, _shift_rows(lf, -1), 0.0)
        else:
            ref = _group_ref(bc, m)
            aq, ak = bc - ref, ref - bc
        qt = jnp.where(up, q * jnp.exp(jnp.where(up, aq, 0.0)), 0.0)
        kt = jnp.where(lo, kk * jnp.exp(jnp.where(lo, ak, 0.0)), 0.0)
        a = a + jnp.where(same < 2 * m, _mm_nt(qt, kt), 0.0)
        m *= 2
    last = row == n - 1
    bl = jnp.sum(jnp.where(last, bc, 0.0), axis=0, keepdims=True)
    o = _mm_nn(a, v) + _mm_nt(q * jnp.exp(bc), st) + jnp.sum(q * kk, axis=-1, keepdims=True) * v
    st_new = st * jnp.exp(bl) + _mm_tn(v, kk * jnp.exp(bl - bc))
    rms = lax.rsqrt(jnp.mean(o * o, axis=-1, keepdims=True) + RMS_EPS)
    return o * rms * nw * jax.nn.sigmoid(g), st_new


def _hgrn_fwd(proj, lb_logits, norm_w, nbatch, seq):
    t = proj.shape[0]
    blk = min(HGRN_BLOCK, seq)
    nb = seq // blk

    def body(p_ref, lbl_ref, nw_ref, y_ref, ck_ref, st_s):
        @pl.when(pl.program_id(2) == 0)
        def _():
            st_s[...] = jnp.zeros_like(st_s)

        st = st_s[...]
        ck_ref[0] = st
        lb = jax.nn.sigmoid(lbl_ref[0:1, :] - lbl_ref[1:2, :])
        y, st_new = _hgrn_block(p_ref[:, 0:128], p_ref[:, 128:256], p_ref[:, 256:384], p_ref[:, 384:512],
                                st, lb, nw_ref[...])
        st_s[...] = st_new
        y_ref[...] = y.astype(y_ref.dtype)

    return pl.pallas_call(
        body, name="hgrn_fwd", grid=(AH, nbatch, nb),
        in_specs=[pl.BlockSpec((blk, 512), lambda h, b, i: (b * nb + i, COL_A // 512 + h)),
                  pl.BlockSpec((2, 128), lambda h, b, i: (0, h)),
                  pl.BlockSpec((1, 128), lambda h, b, i: (0, h))],
        out_specs=[pl.BlockSpec((blk, 128), lambda h, b, i: (b * nb + i, h)),
                   pl.BlockSpec((1, 128, 128), lambda h, b, i: ((h * nbatch + b) * nb + i, 0, 0))],
        out_shape=[jax.ShapeDtypeStruct((t, AW), BF16), jax.ShapeDtypeStruct((AH * nbatch * nb, 128, 128), F32)],
        scratch_shapes=[pltpu.VMEM((128, 128), F32)],
        compiler_params=_params(("parallel", "parallel", "arbitrary"), 48),
    )(proj, lb_logits, norm_w)


def _hgrn_bwd(proj, dya, ckpt, lb_logits, norm_w, dproj, nbatch, seq):
    t = proj.shape[0]
    blk = min(HGRN_BLOCK, seq)
    nb = seq // blk

    def body(p_ref, dy_ref, ck_ref, lbl_ref, nw_ref, dp_in, dp_ref, sm_ref, dst_s):
        del dp_in
        b_id, i = pl.program_id(1), pl.program_id(2)

        @pl.when(i == 0)
        def _():
            dst_s[...] = jnp.zeros_like(dst_s)

        lb = jax.nn.sigmoid(lbl_ref[0:1, :] - lbl_ref[1:2, :])
        _, pullback = jax.vjp(_hgrn_block, p_ref[:, 0:128], p_ref[:, 128:256], p_ref[:, 256:384],
                              p_ref[:, 384:512], ck_ref[0], lb, nw_ref[...])
        dq, dfl, dv, dg, dst, dlb, dnw = pullback((dy_ref[...], dst_s[...]))
        dst_s[...] = dst
        dp_ref[:, 0:128] = dq.astype(dp_ref.dtype)
        dp_ref[:, 128:256] = dfl.astype(dp_ref.dtype)
        dp_ref[:, 256:384] = dv.astype(dp_ref.dtype)
        dp_ref[:, 384:512] = dg.astype(dp_ref.dtype)
        upd = jnp.concatenate([dlb, dnw, jnp.zeros((6, 128), F32)], axis=0)
        first = (b_id == 0) & (i == 0)

        @pl.when(first)
        def _():
            sm_ref[...] = upd

        @pl.when(jnp.logical_not(first))
        def _():
            sm_ref[...] += upd

    def rows(h, b, i):
        return b * nb + (nb - 1 - i)

    return pl.pallas_call(
        body, name="hgrn_bwd", grid=(AH, nbatch, nb),
        in_specs=[pl.BlockSpec((blk, 512), lambda h, b, i: (rows(h, b, i), COL_A // 512 + h)),
                  pl.BlockSpec((blk, 128), lambda h, b, i: (rows(h, b, i), h)),
                  pl.BlockSpec((1, 128, 128), lambda h, b, i: ((h * nbatch + b) * nb + (nb - 1 - i), 0, 0)),
                  pl.BlockSpec((2, 128), lambda h, b, i: (0, h)),
                  pl.BlockSpec((1, 128), lambda h, b, i: (0, h)),
                  pl.BlockSpec(memory_space=pl.ANY)],
        out_specs=[pl.BlockSpec((blk, 512), lambda h, b, i: (rows(h, b, i), COL_A // 512 + h)),
                   pl.BlockSpec((8, 128), lambda h, b, i: (0, h))],
        out_shape=[jax.ShapeDtypeStruct((t, NP), BF16), jax.ShapeDtypeStruct((8, AW), F32)],
        input_output_aliases={5: 0},
        scratch_shapes=[pltpu.VMEM((128, 128), F32)],
        compiler_params=_params(("parallel", "arbitrary", "arbitrary"), 48),
    )(proj, dya, ckpt, lb_logits, norm_w, dproj)


def _log_sigmoid(z):
    return jnp.minimum(z, 0.0) - jnp.log(1.0 + jnp.exp(-jnp.abs(z)))


def _fox_cum(proj, bias128, nbatch, seq):
    t = proj.shape[0]
    ts = min(512, seq)
    nb = seq // ts

    def body(p_ref, b_ref, r_ref, c_ref, carry):
        @pl.when(pl.program_id(1) == 0)
        def _():
            carry[...] = jnp.zeros_like(carry)
        cum = _dot_f32(_tri(ts, False), _log_sigmoid(p_ref[...] + b_ref[...])) + carry[...]
        carry[...] = cum[ts - 1:ts, :]
        r_ref[0] = cum.T[0:BH, :]
        lane = lax.broadcasted_iota(jnp.int32, (ts, 128), 1)
        for p in range(4):
            c_ref[p] = jnp.where(lane < 64, cum[:, 2 * p:2 * p + 1], cum[:, 2 * p + 1:2 * p + 2])

    return pl.pallas_call(
        body, name="fox_cum", grid=(nbatch, nb),
        in_specs=[pl.BlockSpec((ts, 128), lambda b, i: (b * nb + i, COL_BF // 128)),
                  pl.BlockSpec((1, 128), lambda b, i: (0, 0))],
        out_specs=[pl.BlockSpec((1, BH, ts), lambda b, i: (b, 0, i)),
                   pl.BlockSpec((4, ts, 128), lambda b, i: (0, b * nb + i, 0))],
        out_shape=[jax.ShapeDtypeStruct((nbatch, BH, seq), F32), jax.ShapeDtypeStruct((4, t, 128), F32)],
        scratch_shapes=[pltpu.VMEM((1, 128), F32)],
        compiler_params=_params(("parallel", "arbitrary")),
    )(proj, bias128)


def _fox_scores_t(q_ref, kv_ref, cr_ref, cc_ref, hh, hp, masked, tq, tk):
    kh = kv_ref[:, 64 * hh:64 * hh + 64].astype(BF16)
    qh = (q_ref[:, 64 * hh:64 * hh + 64] * (BDH ** -0.5)).astype(BF16)
    s = _dot_nt(kh, qh) + cr_ref[0, pl.ds(2 * hp + hh, 1), :] - cc_ref[0, :, 64 * hh:64 * hh + 1]
    if masked:
        key = lax.broadcasted_iota(jnp.int32, (tk, tq), 0)
        qry = lax.broadcasted_iota(jnp.int32, (tk, tq), 1)
        s = jnp.where(key <= qry, s, NEG)
    return s, kh, qh


def _with_ones_lane(x128, hh):
    lane = lax.broadcasted_iota(jnp.int32, x128.shape, 1)
    one = jnp.ones_like(x128)
    zero = jnp.zeros_like(x128)
    if hh == 0:
        return jnp.where(lane < 64, x128, jnp.where(lane == 64, one, zero))
    return jnp.where(lane >= 64, x128, jnp.where(lane == 0, one, zero))


def _fox_fwd(proj, cum_rows, cum_cols, nbatch, seq):
    t = proj.shape[0]
    tq = tk = min(512, seq)
    nq = seq // tq

    def body(q_ref, kv_ref, cr_ref, cc_ref, o_ref, lse_ref, m_s, acc_s):
        hp, i, j = pl.program_id(1), pl.program_id(2), pl.program_id(3)

        @pl.when(j == 0)
        def _():
            m_s[...] = jnp.full_like(m_s, NEG)
            acc_s[...] = jnp.zeros_like(acc_s)

        def step(masked):
            for hh in range(2):
                s, _, _ = _fox_scores_t(q_ref, kv_ref, cr_ref, cc_ref, hh, hp, masked, tq, tk)
                m_prev = m_s[hh:hh + 1, :]
                m_new = jnp.maximum(m_prev, jnp.max(s, axis=0, keepdims=True))
                alpha = jnp.exp(m_prev - m_new)
                p = jnp.exp(s - m_new).astype(BF16)
                v_aug = _with_ones_lane(kv_ref[:, 128:256].astype(BF16), hh)
                acc_s[hh] = acc_s[hh] * alpha + _dot_tn(v_aug, p)
                m_s[hh:hh + 1, :] = m_new

        @pl.when(j < i)
        def _():
            step(False)

        @pl.when(j == i)
        def _():
            step(True)
            a0, a1 = acc_s[0], acc_s[1]
            l0, l1 = a0[64:65, :], a1[0:1, :]
            o_t = jnp.concatenate([a0[0:64, :] / l0, a1[64:128, :] / l1], axis=0)
            o_ref[...] = o_t.T.astype(o_ref.dtype)
            lse_ref[0, 0] = jnp.concatenate(
                [m_s[0:1, :] + jnp.log(l0), m_s[1:2, :] + jnp.log(l1), jnp.zeros((6, tq), F32)], axis=0)

    return pl.pallas_call(
        body, name="fox_fwd", grid=(nbatch, 4, nq, nq),
        in_specs=[pl.BlockSpec((tq, 128), lambda b, p, i, j: (b * nq + i, COL_BQ // 128 + p)),
                  pl.BlockSpec((tk, 256), lambda b, p, i, j: (b * nq + jnp.minimum(j, i), COL_KV // 256 + p)),
                  pl.BlockSpec((1, BH, tq), lambda b, p, i, j: (b, 0, i)),
                  pl.BlockSpec((1, tk, 128), lambda b, p, i, j: (p, b * nq + jnp.minimum(j, i), 0))],
        out_specs=[pl.BlockSpec((tq, 128), lambda b, p, i, j: (b * nq + i, p)),
                   pl.BlockSpec((1, 1, 8, tq), lambda b, p, i, j: (b, p, 0, i))],
        out_shape=[jax.ShapeDtypeStruct((t, 512), BF16), jax.ShapeDtypeStruct((nbatch, 4, 8, seq), F32)],
        scratch_shapes=[pltpu.VMEM((8, tq), F32), pltpu.VMEM((2, 128, tq), F32)],
        compiler_params=_params(("parallel", "parallel", "parallel", "arbitrary"), 48),
    )(proj, proj, cum_rows, cum_cols)


def _fox_bwd(proj, cum_rows, cum_cols, lse, yb, dyb, dproj, nbatch, seq):
    t = proj.shape[0]
    tq = tk = min(512, seq)
    nq = seq // tq
    scale = BDH ** -0.5

    def body(q_ref, kv_ref, cr_ref, cc_ref, lse_ref, o_ref, do_ref, dp_in,
             dkv_ref, dq_ref, drs_ref, dcs_ref, dkv_s, dqa_s, dcs_s):
        del dp_in
        hp, j, ii = pl.program_id(1), pl.program_id(2), pl.program_id(3)
        i = j + ii

        @pl.when(ii == 0)
        def _():
            dkv_s[...] = jnp.zeros_like(dkv_s)
            dcs_s[...] = jnp.zeros_like(dcs_s)

        @pl.when((j == 0) & (ii == 0))
        def _():
            dqa_s[...] = jnp.zeros_like(dqa_s)

        def step(masked):
            lane = lax.broadcasted_iota(jnp.int32, (tq, 128), 1)
            for hh in range(2):
                s, kh, qh = _fox_scores_t(q_ref, kv_ref, cr_ref, cc_ref, hh, hp, masked, tq, tk)
                p = jnp.exp(s - lse_ref[0, 0, hh:hh + 1, :])
                doh = do_ref[:, 64 * hh:64 * hh + 64]
                dd = lax.dot_general(jnp.ones((8, 64), F32), doh * o_ref[:, 64 * hh:64 * hh + 64].astype(F32),
                                     (((1,), (1,)), ((), ())), preferred_element_type=F32, precision=HIGHEST)[0:1, :]
                doh = doh.astype(BF16)
                dp = _dot_nt(kv_ref[:, 128 + 64 * hh:192 + 64 * hh], doh)
                ds = (p * (dp - dd)).astype(BF16)
                dkv_s[:, 128 + 64 * hh:192 + 64 * hh] += _dot(p, doh)
                dkv_s[:, 64 * hh:64 * hh + 64] += _dot(ds, qh)
                k_aug = _with_ones_lane(kv_ref[:, 0:128].astype(BF16), hh)
                dqa_s[i, hh] += _dot_tn(k_aug, ds)
                sel = jnp.where(lane == 2 * hp + hh, 1.0, 0.0).astype(BF16)
                dcs_s[...] += _dot(ds, sel)

        @pl.when((ii == 0) & (i < nq))
        def _():
            step(True)

        @pl.when((ii > 0) & (i < nq))
        def _():
            step(False)

        @pl.when(ii == nq - 1)
        def _():
            dkv_ref[...] = dkv_s[...].astype(dkv_ref.dtype)
            dcs_ref[0] = dcs_s[...]

        @pl.when((j == nq - 1) & (ii == nq - 1))
        def _():
            lane = lax.broadcasted_iota(jnp.int32, (tq, 128), 1)
            for qi in range(nq):
                a0 = dqa_s[qi, 0].T
                a1 = dqa_s[qi, 1].T
                rows = pl.ds(qi * tq, tq)
                dq_ref[rows, :] = (jnp.where(lane < 64, a0, a1) * scale).astype(dq_ref.dtype)
                drs_ref[0, rows, :] = jnp.where(lane == 2 * hp, a0[:, 64:65], jnp.where(lane == 2 * hp + 1, a1[:, 0:1], 0.0))

    def qrow(b, p, j, ii):
        return b * nq + jnp.minimum(j + ii, nq - 1)

    def qblk(b, p, j, ii):
        return jnp.minimum(j + ii, nq - 1)

    return pl.pallas_call(
        body, name="fox_bwd", grid=(nbatch, 4, nq, nq),
        in_specs=[pl.BlockSpec((tq, 128), lambda b, p, j, ii: (qrow(b, p, j, ii), COL_BQ // 128 + p)),
                  pl.BlockSpec((tk, 256), lambda b, p, j, ii: (b * nq + j, COL_KV // 256 + p)),
                  pl.BlockSpec((1, BH, tq), lambda b, p, j, ii: (b, 0, qblk(b, p, j, ii))),
                  pl.BlockSpec((1, tk, 128), lambda b, p, j, ii: (p, b * nq + j, 0)),
                  pl.BlockSpec((1, 1, 8, tq), lambda b, p, j, ii: (b, p, 0, qblk(b, p, j, ii))),
                  pl.BlockSpec((tq, 128), lambda b, p, j, ii: (qrow(b, p, j, ii), p)),
                  pl.BlockSpec((tq, 128), lambda b, p, j, ii: (qrow(b, p, j, ii), p)),
                  pl.BlockSpec(memory_space=pl.ANY)],
        out_specs=[pl.BlockSpec((tk, 256), lambda b, p, j, ii: (b * nq + j, COL_KV // 256 + p)),
                   pl.BlockSpec((seq, 128), lambda b, p, j, ii: (b, p)),
                   pl.BlockSpec((1, seq, 128), lambda b, p, j, ii: (p, b, 0)),
                   pl.BlockSpec((1, tk, 128), lambda b, p, j, ii: (p, b * nq + j, 0))],
        out_shape=[jax.ShapeDtypeStruct((t, NP), BF16), jax.ShapeDtypeStruct((t, 512), BF16),
                   jax.ShapeDtypeStruct((4, t, 128), F32), jax.ShapeDtypeStruct((4, t, 128), F32)],
        input_output_aliases={7: 0},
        scratch_shapes=[pltpu.VMEM((tk, 256), F32), pltpu.VMEM((nq, 2, 128, tq), F32), pltpu.VMEM((tk, 128), F32)],
        compiler_params=_params(("parallel", "parallel", "arbitrary", "arbitrary"), 56),
    )(proj, proj, cum_rows, cum_cols, lse, yb, dyb, dproj)


def _place_cols(dproj, src, col):
    t, w = src.shape
    tm = 1024 if t % 1024 == 0 else t

    def body(s_ref, dp_in, o_ref):
        del dp_in
        o_ref[...] = s_ref[...]

    return pl.pallas_call(
        body, name="place_cols", grid=(t // tm,),
        in_specs=[pl.BlockSpec((tm, w), lambda i: (i, 0)), pl.BlockSpec(memory_space=pl.ANY)],
        out_specs=pl.BlockSpec((tm, w), lambda i: (i, col // w)),
        out_shape=jax.ShapeDtypeStruct(dproj.shape, dproj.dtype),
        input_output_aliases={1: 0},
        compiler_params=_params(("parallel",)),
    )(src, dproj)


def _fox_dbf(proj, bias128, drs, dcs, dproj, nbatch, seq):
    t = proj.shape[0]
    ts = min(512, seq)
    nb = seq // ts

    def body(p_ref, b_ref, dr_ref, dc_ref, dp_in, dp_ref, sm_ref, carry):
        del dp_in
        b_id, i = pl.program_id(0), pl.program_id(1)

        @pl.when(i == 0)
        def _():
            carry[...] = jnp.zeros_like(carry)

        dcum = (dr_ref[0] - dc_ref[0]) + (dr_ref[1] - dc_ref[1]) + (dr_ref[2] - dc_ref[2]) + (dr_ref[3] - dc_ref[3])
        rc = _dot_f32(_tri(ts, True), dcum) + carry[...]
        carry[...] = rc[0:1, :]
        z = p_ref[...] + b_ref[...]
        lane = lax.broadcasted_iota(jnp.int32, (ts, 128), 1)
        dz = jnp.where(lane < BH, rc * jax.nn.sigmoid(-z), 0.0)
        dp_ref[...] = dz.astype(dp_ref.dtype)
        upd = jnp.concatenate([jnp.sum(dz, axis=0, keepdims=True), jnp.zeros((7, 128), F32)], axis=0)
        first = (b_id == 0) & (i == 0)

        @pl.when(first)
        def _():
            sm_ref[...] = upd

        @pl.when(jnp.logical_not(first))
        def _():
            sm_ref[...] += upd

    def rows(b, i):
        return b * nb + (nb - 1 - i)

    return pl.pallas_call(
        body, name="fox_dbf", grid=(nbatch, nb),
        in_specs=[pl.BlockSpec((ts, 128), lambda b, i: (rows(b, i), COL_BF // 128)),
                  pl.BlockSpec((1, 128), lambda b, i: (0, 0)),
                  pl.BlockSpec((4, ts, 128), lambda b, i: (0, rows(b, i), 0)),
                  pl.BlockSpec((4, ts, 128), lambda b, i: (0, rows(b, i), 0)),
                  pl.BlockSpec(memory_space=pl.ANY)],
        out_specs=[pl.BlockSpec((ts, 128), lambda b, i: (rows(b, i), COL_BF // 128)),
                   pl.BlockSpec((8, 128), lambda b, i: (0, 0))],
        out_shape=[jax.ShapeDtypeStruct((t, NP), BF16), jax.ShapeDtypeStruct((8, 128), F32)],
        input_output_aliases={4: 0},
        scratch_shapes=[pltpu.VMEM((1, 128), F32)],
        compiler_params=_params(("arbitrary", "arbitrary")),
    )(proj, bias128, drs, dcs, dproj)


def _ln_stats(z):
    mu = jnp.mean(z, axis=-1, keepdims=True)
    zc = z - mu
    rstd = lax.rsqrt(jnp.mean(zc * zc, axis=-1, keepdims=True) + LN_EPS)
    return zc * rstd, rstd


def _ln_bwd(dy, xhat, rstd, w):
    dxh = dy * w
    return rstd * (dxh - jnp.mean(dxh, axis=-1, keepdims=True) - xhat * jnp.mean(dxh * xhat, axis=-1, keepdims=True))


def _merge_fwd(ya, yb, proj, x2, mod8, wba, wbb, wout, ln1w, ln1b, seq):
    t = x2.shape[0]
    tm = min(256, seq)
    tpb = seq // tm

    def body(ya_ref, yb_ref, g_ref, x_ref, mod_ref, wa_ref, wb_ref, wo_ref, lw_ref, lb_ref, mg_ref, u_ref, x1_ref):
        ga = jax.nn.sigmoid(g_ref[:, 0:D])
        gb = jax.nn.sigmoid(g_ref[:, D:2 * D])
        merged = (ga * jnp.dot(ya_ref[...], wa_ref[...], preferred_element_type=F32)
                  + gb * jnp.dot(yb_ref[...], wb_ref[...], preferred_element_type=F32))
        mg = merged.astype(BF16)
        mg_ref[...] = mg
        u = jnp.dot(mg, wo_ref[...], preferred_element_type=F32)
        u_ref[...] = u
        xhat, _ = _ln_stats(ALPHA * x_ref[...] + (1.0 + mod_ref[0, 2:3, :]) * u)
        x1_ref[...] = xhat * lw_ref[...] + lb_ref[...]

    tok = lambda w: pl.BlockSpec((tm, w), lambda i: (i, 0))
    full = lambda a: pl.BlockSpec(a.shape, lambda i: (0,) * a.ndim)
    return pl.pallas_call(
        body, name="merge_fwd", grid=(t // tm,),
        in_specs=[tok(512), tok(512), pl.BlockSpec((tm, 2048), lambda i: (i, COL_GATES // 2048)), tok(D),
                  pl.BlockSpec((1, 8, D), lambda i: (i // tpb, 0, 0)),
                  full(wba), full(wbb), full(wout), full(ln1w), full(ln1b)],
        out_specs=[tok(D), tok(D), tok(D)],
        out_shape=[jax.ShapeDtypeStruct((t, D), BF16), jax.ShapeDtypeStruct((t, D), F32),
                   jax.ShapeDtypeStruct((t, D), F32)],
        compiler_params=_params(("parallel",), 48),
    )(ya, yb, proj, x2, mod8, wba, wbb, wout, ln1w, ln1b)


def _merge_bwd(du, ya, yb, proj, wba, wbb, wout, seq):
    t = du.shape[0]
    tm = min(256, seq)

    def body(du_ref, ya_ref, yb_ref, g_ref, wa_ref, wb_ref, wo_ref, dp_ref, dpa_ref, dpb_ref, dya_ref, dyb_ref):
        ga = jax.nn.sigmoid(g_ref[:, 0:D])
        gb = jax.nn.sigmoid(g_ref[:, D:2 * D])
        dm = _dot_nt(du_ref[...], wo_ref[...])
        pa = jnp.dot(ya_ref[...], wa_ref[...], preferred_element_type=F32)
        pb = jnp.dot(yb_ref[...], wb_ref[...], preferred_element_type=F32)
        dpa = (dm * ga).astype(BF16)
        dpb = (dm * gb).astype(BF16)
        dpa_ref[...] = dpa
        dpb_ref[...] = dpb
        dp_ref[:, 0:D] = (dm * pa * ga * (1.0 - ga)).astype(BF16)
        dp_ref[:, D:2 * D] = (dm * pb * gb * (1.0 - gb)).astype(BF16)
        dya_ref[...] = _dot_nt(dpa, wa_ref[...])
        dyb_ref[...] = _dot_nt(dpb, wb_ref[...])

    tok = lambda w: pl.BlockSpec((tm, w), lambda i: (i, 0))
    full = lambda a: pl.BlockSpec(a.shape, lambda i: (0,) * a.ndim)
    return pl.pallas_call(
        body, name="merge_bwd", grid=(t // tm,),
        in_specs=[tok(D), tok(512), tok(512), pl.BlockSpec((tm, 2048), lambda i: (i, COL_GATES // 2048)),
                  full(wba), full(wbb), full(wout)],
        out_specs=[pl.BlockSpec((tm, 2048), lambda i: (i, COL_GATES // 2048)), tok(D), tok(D), tok(512), tok(512)],
        out_shape=[jax.ShapeDtypeStruct((t, NP), BF16), jax.ShapeDtypeStruct((t, D), BF16),
                   jax.ShapeDtypeStruct((t, D), BF16), jax.ShapeDtypeStruct((t, 512), F32),
                   jax.ShapeDtypeStruct((t, 512), F32)],
        compiler_params=_params(("parallel",), 48),
    )(du, ya, yb, proj, wba, wbb, wout)


def _ffn_fwd(x1, mod8, wg, wu, wd, target, ln2w, ln2b, seq):
    t = x1.shape[0]
    tm = min(512, seq)
    nf, _, tf = wg.shape
    tpb = seq // tm
    nbatch = t // seq

    def body(x_ref, mod_ref, wg_ref, wu_ref, wd_ref, t_ref, lw_ref, lb_ref,
             a_ref, b_ref, dz_ref, st_ref, dm_ref, h_s, acc):
        i, j = pl.program_id(0), pl.program_id(1)

        @pl.when(j == 0)
        def _():
            h_s[...] = (x_ref[...] * (1.0 + mod_ref[0, 4:5, :]) + mod_ref[0, 3:4, :]).astype(BF16)
            acc[...] = jnp.zeros_like(acc)

        a = jnp.dot(h_s[...], wg_ref[0], preferred_element_type=F32)
        b = jnp.dot(h_s[...], wu_ref[0], preferred_element_type=F32)
        a_ref[0] = a.astype(BF16)
        b_ref[0] = b.astype(BF16)
        acc[...] += _dot(a * jax.nn.sigmoid(a) * b, wd_ref[0])

        @pl.when(j == nf - 1)
        def _():
            ffn = acc[...]
            xhat, rstd = _ln_stats(ALPHA * x_ref[...] + (1.0 + mod_ref[0, 5:6, :]) * ffn)
            diff = xhat * lw_ref[...] + lb_ref[...] - t_ref[...]
            loss = 0.5 * jnp.sum(jnp.sum(diff * diff, axis=-1, keepdims=True), axis=0, keepdims=True) / D
            dy = diff * (1.0 / D)
            dz = _ln_bwd(dy, xhat, rstd, lw_ref[...])
            dz_ref[...] = dz
            lane = lax.broadcasted_iota(jnp.int32, (1, D), 1)
            upd = jnp.concatenate(
                [jnp.sum(dy * xhat, axis=0, keepdims=True), jnp.sum(dy, axis=0, keepdims=True),
                 jnp.where(lane == 0, loss, 0.0), jnp.zeros((5, D), F32)], axis=0)
            dmu = jnp.concatenate(
                [jnp.zeros((5, D), F32), jnp.sum(dz * ffn, axis=0, keepdims=True), jnp.zeros((2, D), F32)], axis=0)

            @pl.when(i == 0)
            def _():
                st_ref[...] = upd

            @pl.when(i > 0)
            def _():
                st_ref[...] += upd

            @pl.when(i % tpb == 0)
            def _():
                dm_ref[0] = dmu

            @pl.when(i % tpb != 0)
            def _():
                dm_ref[0] += dmu

    row = lambda: pl.BlockSpec((tm, D), lambda i, j: (i, 0))
    vec = lambda: pl.BlockSpec((1, D), lambda i, j: (0, 0))
    return pl.pallas_call(
        body, name="ffn_fwd", grid=(t // tm, nf),
        in_specs=[row(), pl.BlockSpec((1, 8, D), lambda i, j: (i // tpb, 0, 0)),
                  pl.BlockSpec((1, D, tf), lambda i, j: (j, 0, 0)), pl.BlockSpec((1, D, tf), lambda i, j: (j, 0, 0)),
                  pl.BlockSpec((1, tf, D), lambda i, j: (j, 0, 0)), row(), vec(), vec()],
        out_specs=[pl.BlockSpec((1, tm, tf), lambda i, j: (j, i, 0)), pl.BlockSpec((1, tm, tf), lambda i, j: (j, i, 0)),
                   row(), pl.BlockSpec((8, D), lambda i, j: (0, 0)),
                   pl.BlockSpec((1, 8, D), lambda i, j: (i // tpb, 0, 0))],
        out_shape=[jax.ShapeDtypeStruct((nf, t, tf), BF16), jax.ShapeDtypeStruct((nf, t, tf), BF16),
                   jax.ShapeDtypeStruct((t, D), F32), jax.ShapeDtypeStruct((8, D), F32),
                   jax.ShapeDtypeStruct((nbatch, 8, D), F32)],
        scratch_shapes=[pltpu.VMEM((tm, D), BF16), pltpu.VMEM((tm, D), F32)],
        compiler_params=_params(("arbitrary", "arbitrary"), 48),
    )(x1, mod8, wg, wu, wd, target, ln2w, ln2b)


def _ffn_bwd(dz2, a, b, wg, wu, wd, x1, x2, u, mod8, ln1w, seq):
    t = x1.shape[0]
    tm = min(512, seq)
    nf, _, tf = wg.shape
    tpb = seq // tm
    nbatch = t // seq

    def body(dz_ref, a_ref, b_ref, wg_ref, wu_ref, wd_ref, x1_ref, x_ref, u_ref, mod_ref, lw_ref,
             da_ref, db_ref, hm_ref, df_ref, du_ref, dxp_ref, st_ref, dm_ref, acc):
        i, j = pl.program_id(0), pl.program_id(1)

        @pl.when(j == 0)
        def _():
            df_ref[...] = ((1.0 + mod_ref[0, 5:6, :]) * dz_ref[...]).astype(BF16)
            acc[...] = jnp.zeros_like(acc)

        dhm = _dot_nt(df_ref[...], wd_ref[0])
        av = a_ref[0].astype(F32)
        bv = b_ref[0].astype(F32)
        sg = jax.nn.sigmoid(av)
        sl = av * sg
        hm_ref[0] = (sl * bv).astype(BF16)
        da = (dhm * bv * (sg * (1.0 + av * (1.0 - sg)))).astype(BF16)
        db = (dhm * sl).astype(BF16)
        da_ref[0] = da
        db_ref[0] = db
        acc[...] += _dot_nt(da, wg_ref[0]) + _dot_nt(db, wu_ref[0])

        @pl.when(j == nf - 1)
        def _():
            dh2 = acc[...]
            x1v = x1_ref[...]
            uv = u_ref[...]
            dx1 = ALPHA * dz_ref[...] + dh2 * (1.0 + mod_ref[0, 4:5, :])
            xhat, rstd = _ln_stats(ALPHA * x_ref[...] + (1.0 + mod_ref[0, 2:3, :]) * uv)
            dz1 = _ln_bwd(dx1, xhat, rstd, lw_ref[...])
            du_ref[...] = ((1.0 + mod_ref[0, 2:3, :]) * dz1).astype(BF16)
            dxp_ref[...] = ALPHA * dz1
            upd = jnp.concatenate(
                [jnp.sum(dx1 * xhat, axis=0, keepdims=True), jnp.sum(dx1, axis=0, keepdims=True),
                 jnp.zeros((6, D), F32)], axis=0)
            dmu = jnp.concatenate(
                [jnp.zeros((2, D), F32), jnp.sum(dz1 * uv, axis=0, keepdims=True),
                 jnp.sum(dh2, axis=0, keepdims=True), jnp.sum(dh2 * x1v, axis=0, keepdims=True),
                 jnp.zeros((3, D), F32)], axis=0)

            @pl.when(i == 0)
            def _():
                st_ref[...] = upd

            @pl.when(i > 0)
            def _():
                st_ref[...] += upd

            @pl.when(i % tpb == 0)
            def _():
                dm_ref[0] = dmu

            @pl.when(i % tpb != 0)
            def _():
                dm_ref[0] += dmu

    row = lambda: pl.BlockSpec((tm, D), lambda i, j: (i, 0))
    ffb = lambda: pl.BlockSpec((1, tm, tf), lambda i, j: (j, i, 0))
    return pl.pallas_call(
        body, name="ffn_bwd", grid=(t // tm, nf),
        in_specs=[row(), ffb(), ffb(),
                  pl.BlockSpec((1, D, tf), lambda i, j: (j, 0, 0)), pl.BlockSpec((1, D, tf), lambda i, j: (j, 0, 0)),
                  pl.BlockSpec((1, tf, D), lambda i, j: (j, 0, 0)), row(), row(), row(),
                  pl.BlockSpec((1, 8, D), lambda i, j: (i // tpb, 0, 0)), pl.BlockSpec((1, D), lambda i, j: (0, 0))],
        out_specs=[ffb(), ffb(), ffb(), row(), row(), row(), pl.BlockSpec((8, D), lambda i, j: (0, 0)),
                   pl.BlockSpec((1, 8, D), lambda i, j: (i // tpb, 0, 0))],
        out_shape=[jax.ShapeDtypeStruct((nf, t, tf), BF16), jax.ShapeDtypeStruct((nf, t, tf), BF16),
                   jax.ShapeDtypeStruct((nf, t, tf), BF16), jax.ShapeDtypeStruct((t, D), BF16),
                   jax.ShapeDtypeStruct((t, D), BF16), jax.ShapeDtypeStruct((t, D), F32),
                   jax.ShapeDtypeStruct((8, D), F32), jax.ShapeDtypeStruct((nbatch, 8, D), F32)],
        scratch_shapes=[pltpu.VMEM((tm, D), F32)],
        compiler_params=_params(("arbitrary", "arbitrary"), 48),
    )(dz2, a, b, wg, wu, wd, x1, x2, u, mod8, ln1w)


def _adamw_math(w, g, m, v):
    m = B1 * m + (1.0 - B1) * g
    v = B2 * v + (1.0 - B2) * (g * g)
    m_hat = m / (1.0 - B1 ** STEP)
    v_hat = v / (1.0 - B2 ** STEP)
    return -LR * (m_hat / (jnp.sqrt(v_hat) + EPS) + WD * w), m, v


def _adamw(w, g, m, v, name):
    rows, cols = w.shape
    tr = rows
    for cand in (128, 64, 32, 16, 8):
        if rows % cand == 0:
            tr = cand
            break

    def body(w_ref, g_ref, m_ref, v_ref, d_ref, mo_ref, vo_ref):
        d, mn, vn = _adamw_math(w_ref[...], g_ref[...], m_ref[...], v_ref[...])
        d_ref[...] = d
        mo_ref[...] = mn
        vo_ref[...] = vn

    spec = pl.BlockSpec((tr, cols), lambda i: (i, 0))
    return pl.pallas_call(
        body, name=name, grid=(rows // tr,), in_specs=[spec] * 4, out_specs=[spec] * 3,
        out_shape=[jax.ShapeDtypeStruct((rows, cols), F32)] * 3,
        compiler_params=_params(("parallel",), 48),
    )(w, g, m, v)


def _adamw_halves(w, g_mine, g_sib, m, v, c_idx, name):
    rows, cols = w.shape
    hr = rows // 2
    tr = next(cand for cand in (128, 88, 64, 32, 16, 8) if hr % cand == 0)
    tph = hr // tr

    def body(c_ref, w_ref, gm_ref, gs_ref, m_ref, v_ref, g_ref, d_ref, mo_ref, vo_ref):
        g = jnp.where(pl.program_id(0) == c_ref[0], gm_ref[...], gs_ref[...])
        d, mn, vn = _adamw_math(w_ref[...], g, m_ref[...], v_ref[...])
        g_ref[...] = g
        d_ref[...] = d
        mo_ref[...] = mn
        vo_ref[...] = vn

    full = pl.BlockSpec((tr, cols), lambda h, i, c: (h * tph + i, 0))
    half = pl.BlockSpec((tr, cols), lambda h, i, c: (i, 0))
    return pl.pallas_call(
        body, name=name,
        grid_spec=pltpu.PrefetchScalarGridSpec(
            num_scalar_prefetch=1, grid=(2, tph), in_specs=[full, half, half, full, full], out_specs=[full] * 4),
        out_shape=[jax.ShapeDtypeStruct((rows, cols), F32)] * 4,
        compiler_params=_params(("parallel", "parallel"), 48),
    )(c_idx, w, g_mine, g_sib, m, v)


def _grad_w_ada(c_all, dmod_cols):
    def body(c_ref, d_ref, o_ref):
        c = c_ref[...]
        o_ref[...] = lax.dot_general(c * jax.nn.sigmoid(c), d_ref[...], (((0,), (0,)), ((), ())),
                                     preferred_element_type=F32, precision=HIGHEST)

    return pl.pallas_call(
        body, name="grad_w_ada", out_shape=jax.ShapeDtypeStruct((D, dmod_cols.shape[1]), F32),
        compiler_params=_params(vmem_mb=48),
    )(c_all, dmod_cols)


def _small_update(gath, w8, m8, v8):
    def body(g_ref, w_ref, m_ref, v_ref, go_ref, d_ref, mo_ref, vo_ref):
        g0 = g_ref[0, 0:1, :] + g_ref[0, 1:2, :]
        g1 = g_ref[0, 2:3, :]
        for dev in range(1, N_DEV):
            g0 = g0 + (g_ref[dev, 0:1, :] + g_ref[dev, 1:2, :])
            g1 = g1 + g_ref[dev, 2:3, :]
        w = w_ref[...]
        lb = jax.nn.sigmoid(w[1:2, O_LB0:O_LB1] - w[1:2, O_LB1:O_FOX])
        fac = lb * (1.0 - lb)
        g1 = jnp.concatenate([g1[:, :O_LB0], g1[:, O_LB0:O_LB1] * fac, -g1[:, O_LB1:O_FOX] * fac, g1[:, O_FOX:]],
                             axis=1)
        g = jnp.concatenate([g0, g1, jnp.zeros((6, SMALL_W), F32)], axis=0)
        d, mn, vn = _adamw_math(w, g, m_ref[...], v_ref[...])
        go_ref[...] = g
        d_ref[...] = d
        mo_ref[...] = mn
        vo_ref[...] = vn

    return pl.pallas_call(
        body, name="small_update", out_shape=[jax.ShapeDtypeStruct((8, SMALL_W), F32)] * 4,
        compiler_params=_params(vmem_mb=48),
    )(gath, w8, m8, v8)


def _pack_small(b_ada, ln1w, ln1b, ln2w, ln2b, norm_w, lb_logits, fox):
    row1 = jnp.concatenate([ln1w, ln1b, ln2w, ln2b, norm_w, lb_logits[0:1], lb_logits[1:2], fox,
                            jnp.zeros((1, SMALL_W - O_FOX - BH), F32)], axis=1)
    return jnp.concatenate([b_ada, row1, jnp.zeros((6, SMALL_W), F32)], axis=0)


def _unpack_small(p):
    r = p[1:2]
    lb = jnp.concatenate([r[:, O_LB0:O_LB1], r[:, O_LB1:O_FOX]], axis=0)
    return dict(b_ada=p[0:1], ln1_w=r[:, O_LN1W:O_LN1B], ln1_b=r[:, O_LN1B:O_LN2W], ln2_w=r[:, O_LN2W:O_LN2B],
                ln2_b=r[:, O_LN2B:O_NORM], hgrn_norm_w=r[:, O_NORM:O_LB0], lb_logits=lb,
                fox_f_bias=r[:, O_FOX:O_FOX + BH])


_BIG = ("w_in", "w_branch_a", "w_branch_b", "w_out", "w_ffn_gate", "w_ffn_up", "w_ffn_down")


def _cols_of_chips(stacked):
    return jnp.concatenate([stacked[k] for k in range(N_CHIPS)], axis=1)


def kernel(x, c, w_ada, b_ada, w_in, fox_f_bias, lb_logits, hgrn_norm_w, w_branch_a, w_branch_b, w_out, ln1_w, ln1_b, w_ffn_gate, w_ffn_up, w_ffn_down, ln2_w, ln2_b, loss_target, m_w_ada, m_b_ada, m_w_in, m_fox_f_bias, m_lb_logits, m_hgrn_norm_w, m_w_branch_a, m_w_branch_b, m_w_out, m_ln1_w, m_ln1_b, m_w_ffn_gate, m_w_ffn_up, m_w_ffn_down, m_ln2_w, m_ln2_b, v_w_ada, v_b_ada, v_w_in, v_fox_f_bias, v_lb_logits, v_hgrn_norm_w, v_w_branch_a, v_w_branch_b, v_w_out, v_ln1_w, v_ln1_b, v_w_ffn_gate, v_w_ffn_up, v_w_ffn_down, v_ln2_w, v_ln2_b):
    nbatch, seq, _ = x.shape
    t = nbatch * seq
    ax, ay, ac = lax.axis_index("x"), lax.axis_index("y"), lax.axis_index("c")
    chip = 2 * ax + ay
    dev = 2 * chip + ac
    chip_arr = jnp.reshape(chip, (1,)).astype(jnp.int32)
    core_arr = jnp.reshape(ac, (1,)).astype(jnp.int32)

    shard_w = dict(w_in=w_in[0], w_branch_a=w_branch_a[0], w_branch_b=w_branch_b[0], w_out=w_out[0],
                   w_ffn_gate=w_ffn_gate[0], w_ffn_up=w_ffn_up[0], w_ffn_down=w_ffn_down[0])
    shard_m = dict(w_in=m_w_in[0], w_branch_a=m_w_branch_a[0], w_branch_b=m_w_branch_b[0], w_out=m_w_out[0],
                   w_ffn_gate=m_w_ffn_gate[0], w_ffn_up=m_w_ffn_up[0], w_ffn_down=m_w_ffn_down[0])
    shard_v = dict(w_in=v_w_in[0], w_branch_a=v_w_branch_a[0], w_branch_b=v_w_branch_b[0], w_out=v_w_out[0],
                   w_ffn_gate=v_w_ffn_gate[0], w_ffn_up=v_w_ffn_up[0], w_ffn_down=v_w_ffn_down[0])

    shard16 = [shard_w[n].astype(BF16) for n in _BIG]
    full = {n: lax.dynamic_update_slice(g, s[None], (chip, 0, 0))
            for n, g, s in zip(_BIG, _gather_weights(shard16), shard16)}
    w_p = _permute_cols(_cols_of_chips(full["w_in"]))
    wba, wbb = _cols_of_chips(full["w_branch_a"]), _cols_of_chips(full["w_branch_b"])
    wout = full["w_out"].reshape(D, D)
    wg, wu, wd = full["w_ffn_gate"], full["w_ffn_up"], full["w_ffn_down"]

    c8 = jnp.concatenate([c, jnp.zeros((8 - nbatch, D), F32)], axis=0)
    c_all = _allgather8(c8, "gather_c")[:, :nbatch, :].reshape(N_DEV * nbatch, D)
    ncol = w_ada.shape[2]
    b_cols = lax.dynamic_slice_in_dim(b_ada, chip * ncol, ncol, axis=1)
    mod_g = _allgather8(_mod_shard(c_all, w_ada[0], b_cols), "gather_mod")
    mod_all = jnp.concatenate([mod_g[2 * k] for k in range(N_CHIPS)], axis=1)
    mod_mine = lax.dynamic_slice_in_dim(mod_all, dev * nbatch, nbatch, axis=0)
    mod8 = jnp.concatenate([mod_mine.reshape(nbatch, 6, D), jnp.zeros((nbatch, 2, D), F32)], axis=1)

    x2 = x.reshape(t, D)
    tgt2 = loss_target.reshape(t, D)
    bias128 = jnp.concatenate([fox_f_bias, jnp.zeros((1, 128 - BH), F32)], axis=1)

    proj = _proj(x2, mod8, w_p, seq)
    ya, ckpt = _hgrn_fwd(proj, lb_logits, hgrn_norm_w, nbatch, seq)
    cum_rows, cum_cols = _fox_cum(proj, bias128, nbatch, seq)
    yb, lse = _fox_fwd(proj, cum_rows, cum_cols, nbatch, seq)
    merged, u, x1 = _merge_fwd(ya, yb, proj, x2, mod8, wba, wbb, wout, ln1_w, ln1_b, seq)
    a_pre, b_pre, dz2, st2, dm2 = _ffn_fwd(x1, mod8, wg, wu, wd, tgt2, ln2_w, ln2_b, seq)
    loss = lax.psum(st2[2, 0], ("x", "y", "c"))

    da, db, hmid, dffn, du, dxp, st1, dm1 = _ffn_bwd(dz2, a_pre, b_pre, wg, wu, wd, x1, x2, u, mod8, ln1_w, seq)
    g_st = {}
    g_st["w_ffn_down"] = _tn_matmul(hmid, dffn, "dw_ffn_down", seq)
    g_st["w_ffn_gate"] = _tn_matmul(x1, da, "dw_ffn_gate", seq, mod8, (3, 4))
    g_st["w_ffn_up"] = _tn_matmul(x1, db, "dw_ffn_up", seq, mod8, (3, 4))
    g_st["w_out"] = _tn_matmul(merged, du, "dw_out", seq).reshape(N_CHIPS, D // N_CHIPS, D)
    dproj, dpa, dpb, dya, dyb = _merge_bwd(du, ya, yb, proj, wba, wbb, wout, seq)
    g_st["w_branch_a"] = _tn_matmul(ya, dpa, "dw_branch_a", seq, split=D // N_CHIPS)
    g_st["w_branch_b"] = _tn_matmul(yb, dpb, "dw_branch_b", seq, split=D // N_CHIPS)
    dproj, dq, drs, dcs = _fox_bwd(proj, cum_rows, cum_cols, lse, yb, dyb, dproj, nbatch, seq)
    dproj = _place_cols(dproj, dq, COL_BQ)
    dproj, sm_fox = _fox_dbf(proj, bias128, drs, dcs, dproj, nbatch, seq)
    dproj, sm_hgrn = _hgrn_bwd(proj, dya, ckpt, lb_logits, hgrn_norm_w, dproj, nbatch, seq)
    grad_x2, dm0 = _dh_kernel(dproj, w_p, x2, dxp, mod8, seq)
    dw_in = _unpermute_cols(_tn_matmul(x2, dproj, "dw_in", seq, mod8, (0, 1)))
    ncin = NIN // N_CHIPS
    g_st["w_in"] = jnp.stack([dw_in[:, k * ncin:(k + 1) * ncin] for k in range(N_CHIPS)])

    g_list = [g_st[n] for n in _BIG]
    halves = [_add_my_half(g, o, core_arr, "grad_add_halves_" + n)
              for n, g, o in zip(_BIG, g_list, _swap_halves(g_list))]
    recv = _scatter_chips([h16 for _, h16 in halves])
    g_mine = [_add_chips(h32, r, chip_arr, "grad_add_chips_" + n) for n, (h32, _), r in zip(_BIG, halves, recv)]
    g_sib = _join_halves(g_mine)

    dmod = (dm0 + dm1 + dm2)[:, :6, :].reshape(nbatch, 6 * D)
    row2 = jnp.concatenate([st1[0:1], st1[1:2], st2[0:1], st2[1:2], sm_hgrn[1:2], sm_hgrn[0:1], sm_hgrn[0:1],
                            sm_fox[0:1, :BH], jnp.zeros((1, SMALL_W - O_FOX - BH), F32)], axis=1)
    spack = jnp.concatenate([dmod, row2, jnp.zeros((8 - nbatch - 1, SMALL_W), F32)], axis=0)
    gath = _allgather8(spack, "gather_small")
    w8 = _pack_small(b_ada, ln1_w, ln1_b, ln2_w, ln2_b, hgrn_norm_w, lb_logits, fox_f_bias)
    m8 = _pack_small(m_b_ada, m_ln1_w, m_ln1_b, m_ln2_w, m_ln2_b, m_hgrn_norm_w, m_lb_logits, m_fox_f_bias)
    v8 = _pack_small(v_b_ada, v_ln1_w, v_ln1_b, v_ln2_w, v_ln2_b, v_hgrn_norm_w, v_lb_logits, v_fox_f_bias)
    sg, sd, smn, svn = (_unpack_small(p) for p in _small_update(gath, w8, m8, v8))
    dmod_all = gath[:, :nbatch, :].reshape(N_DEV * nbatch, SMALL_W)
    g_ada = _grad_w_ada(c_all, lax.dynamic_slice_in_dim(dmod_all, chip * ncol, ncol, axis=1))

    grads = dict(sg)
    deltas = dict(sd)
    new_m = dict(smn)
    new_v = dict(svn)
    grads["w_ada"] = g_ada
    deltas["w_ada"], new_m["w_ada"], new_v["w_ada"] = _adamw(w_ada[0], g_ada, m_w_ada[0], v_w_ada[0], "adamw_w_ada")
    for n, gm, gs in zip(_BIG, g_mine, g_sib):
        grads[n], deltas[n], new_m[n], new_v[n] = _adamw_halves(shard_w[n], gm, gs, shard_m[n], shard_v[n], core_arr,
                                                                "adamw_" + n)

    names = ["w_ada", "b_ada", "w_in", "fox_f_bias", "lb_logits", "hgrn_norm_w", "w_branch_a", "w_branch_b", "w_out",
             "ln1_w", "ln1_b", "w_ffn_gate", "w_ffn_up", "w_ffn_down", "ln2_w", "ln2_b"]
    shapes = dict(w_ada=w_ada.shape, b_ada=b_ada.shape, w_in=w_in.shape, fox_f_bias=fox_f_bias.shape,
                  lb_logits=lb_logits.shape, hgrn_norm_w=hgrn_norm_w.shape, w_branch_a=w_branch_a.shape,
                  w_branch_b=w_branch_b.shape, w_out=w_out.shape, ln1_w=ln1_w.shape, ln1_b=ln1_b.shape,
                  w_ffn_gate=w_ffn_gate.shape, w_ffn_up=w_ffn_up.shape, w_ffn_down=w_ffn_down.shape,
                  ln2_w=ln2_w.shape, ln2_b=ln2_b.shape)
    outs = [loss, grad_x2.reshape(x.shape)]
    for group in (grads, deltas, new_m, new_v):
        outs += [group[n].reshape(shapes[n]) for n in names]
    return tuple(outs)
```

```python
import functools
import math

import jax
import jax.numpy as jnp
import numpy as np
from jax import lax
from jax.experimental import pallas as pl
from jax.experimental.pallas import tpu as pltpu

F32 = jnp.float32
BF16 = jnp.bfloat16
MESH = pl.DeviceIdType.MESH
HIGHEST = lax.Precision.HIGHEST

D = 1024
AW = 512
AH = 4
ADH = 128
BH = 8
BDH = 64
DFF = 2816
NIN = 5640
NP = 5760
N_CHIPS = 4
N_DEV = 8
HGRN_BLOCK = 256
COL_GATES = 0
COL_BQ = 2048
COL_KV = 2560
COL_A = 3584
COL_BF = 5632
ALPHA = 2.0 ** 0.25
LN_EPS = 1e-5
RMS_EPS = 1e-6
NEG = -1e30
LOG2E = 1.4426950408889634
LR, B1, B2, EPS, WD, STEP = 0.001, 0.9, 0.999, 1e-08, 0.01, 10
SMALL_W = 6144
O_LN1W, O_LN1B, O_LN2W, O_LN2B, O_NORM, O_LB0, O_LB1, O_FOX = 0, 1024, 2048, 3072, 4096, 4608, 5120, 5632


def _params(sem=None, vmem_mb=None):
    kw = {}
    if sem is not None:
        kw["dimension_semantics"] = sem
    if vmem_mb is not None:
        kw["vmem_limit_bytes"] = vmem_mb << 20
    return pltpu.CompilerParams(**kw)


def _dot(a, b):
    return jnp.dot(a.astype(BF16), b.astype(BF16), preferred_element_type=F32)


def _dot_nt(a, b):
    return lax.dot_general(a.astype(BF16), b.astype(BF16), (((1,), (1,)), ((), ())), preferred_element_type=F32)


def _dot_tn(a, b):
    return lax.dot_general(a.astype(BF16), b.astype(BF16), (((0,), (0,)), ((), ())), preferred_element_type=F32)


def _dot_f32(a, b):
    return jnp.dot(a, b, preferred_element_type=F32, precision=HIGHEST)


def _perm_segments():
    segs = [(3592, 5640), (2048, 2560)]
    for p in range(4):
        segs += [(2560 + 128 * p, 2688 + 128 * p), (3072 + 128 * p, 3200 + 128 * p)]
    for h in range(4):
        segs += [(128 * h + 512 * t, 128 * h + 512 * t + 128) for t in range(4)]
    segs += [(3584, 3592)]
    return segs


def _permute_cols(w):
    parts = [w[:, a:b] for a, b in _perm_segments()]
    parts.append(jnp.zeros((w.shape[0], NP - NIN), w.dtype))
    return jnp.concatenate(parts, axis=1)


def _unpermute_cols(g):
    pos, where = 0, {}
    for a, b in _perm_segments():
        where[a] = (pos, pos + b - a)
        pos += b - a
    parts = [g[:, where[a][0]:where[a][1]] for a in sorted(where)]
    return jnp.concatenate(parts, axis=1)


def _allgather8(v, name):
    rows, cols = v.shape

    def body(x_ref, out_ref, send_sems, recv_sems, local_sem):
        x, y, c = lax.axis_index("x"), lax.axis_index("y"), lax.axis_index("c")
        me, sibling = (x, y, c), (x, y, 1 - c)
        chips = [(1 - x, y), (x, 1 - y), (1 - x, 1 - y)]

        def slot(px, py, pc):
            return out_ref.at[4 * px + 2 * py + pc]

        def copy(k, block, to, src=None):
            return pltpu.make_async_remote_copy(
                src_ref=slot(*block) if src is None else src, dst_ref=slot(*block),
                send_sem=send_sems.at[k], recv_sem=recv_sems.at[k], device_id=to, device_id_type=MESH)

        mine = pltpu.make_async_copy(x_ref, slot(*me), local_sem)
        mine.start()
        first = [copy(0, me, sibling, src=x_ref)]
        first += [copy(1 + j, me, (*chip, c), src=x_ref) for j, chip in enumerate(chips)]
        for cp in first:
            cp.start()
        passed = [copy(4 + j, (*chip, c), sibling) for j, chip in enumerate(chips)]
        for j, chip in enumerate(chips):
            copy(1 + j, (*chip, c), me).wait_recv()
            passed[j].start()
        copy(0, sibling, me).wait_recv()
        for j, chip in enumerate(chips):
            copy(4 + j, (*chip, 1 - c), me).wait_recv()
        for cp in first + passed:
            cp.wait_send()
        mine.wait()

    return pl.pallas_call(
        body, name=name,
        out_shape=jax.ShapeDtypeStruct((N_DEV, rows, cols), v.dtype),
        in_specs=[pl.BlockSpec(memory_space=pltpu.VMEM)],
        out_specs=pl.BlockSpec(memory_space=pltpu.VMEM),
        scratch_shapes=[pltpu.SemaphoreType.DMA((7,)), pltpu.SemaphoreType.DMA((7,)), pltpu.SemaphoreType.DMA],
    )(v)


def _hbm_specs(n):
    return [pl.BlockSpec(memory_space=pl.ANY)] * n


def _gather_weights(shards):
    n = len(shards)

    def body(*refs):
        ins, outs, (send_sems, recv_sems) = refs[:n], refs[n:2 * n], refs[2 * n:]
        x, y, c = lax.axis_index("x"), lax.axis_index("y"), lax.axis_index("c")
        sibling = (x, y, 1 - c)
        chips = [(1 - x, y), (x, 1 - y), (1 - x, 1 - y)]

        def blk(w, px, py, half):
            hr = ins[w].shape[0] // 2
            return outs[w].at[2 * px + py, pl.ds(half * hr, hr), :]

        def copy(w, k, block, to, src=None):
            return pltpu.make_async_remote_copy(
                src_ref=blk(w, *block) if src is None else src, dst_ref=blk(w, *block),
                send_sem=send_sems.at[6 * w + k], recv_sem=recv_sems.at[6 * w + k], device_id=to, device_id_type=MESH)

        first = []
        for w in range(n):
            hr = ins[w].shape[0] // 2
            my_half = ins[w].at[pl.ds(c * hr, hr), :]
            first += [copy(w, j, (x, y, c), (*chip, c), src=my_half) for j, chip in enumerate(chips)]
        for cp in first:
            cp.start()
        passed = []
        for j, chip in enumerate(chips):
            for w in range(n):
                copy(w, j, (*chip, c), (x, y, c)).wait_recv()
                passed.append(copy(w, 3 + j, (*chip, c), sibling))
                passed[-1].start()
        for j, chip in enumerate(chips):
            for w in range(n):
                copy(w, 3 + j, (*chip, 1 - c), (x, y, c)).wait_recv()
        for cp in first + passed:
            cp.wait_send()

    return pl.pallas_call(
        body, name="gather_weights",
        out_shape=[jax.ShapeDtypeStruct((N_CHIPS,) + s.shape, s.dtype) for s in shards],
        in_specs=_hbm_specs(n), out_specs=_hbm_specs(n),
        scratch_shapes=[pltpu.SemaphoreType.DMA((6 * n,)), pltpu.SemaphoreType.DMA((6 * n,))],
    )(*shards)


def _swap_halves(grads):
    n = len(grads)

    def body(*refs):
        ins, outs, (send_sems, recv_sems) = refs[:n], refs[n:2 * n], refs[2 * n:]
        x, y, c = lax.axis_index("x"), lax.axis_index("y"), lax.axis_index("c")
        cps = []
        for w in range(n):
            hr = ins[w].shape[1] // 2
            cps.append(pltpu.make_async_remote_copy(
                src_ref=ins[w].at[:, pl.ds((1 - c) * hr, hr), :], dst_ref=outs[w],
                send_sem=send_sems.at[w], recv_sem=recv_sems.at[w], device_id=(x, y, 1 - c), device_id_type=MESH))
        for cp in cps:
            cp.start()
        for cp in cps:
            cp.wait()

    return pl.pallas_call(
        body, name="grad_swap_halves",
        out_shape=[jax.ShapeDtypeStruct((N_CHIPS, g.shape[1] // 2, g.shape[2]), g.dtype) for g in grads],
        in_specs=_hbm_specs(n), out_specs=_hbm_specs(n),
        scratch_shapes=[pltpu.SemaphoreType.DMA((n,)), pltpu.SemaphoreType.DMA((n,))],
    )(*grads)


def _scatter_chips(reds):
    n = len(reds)

    def body(*refs):
        ins, outs, (send_sems, recv_sems) = refs[:n], refs[n:2 * n], refs[2 * n:]
        x, y, c = lax.axis_index("x"), lax.axis_index("y"), lax.axis_index("c")
        chips = [(1 - x, y), (x, 1 - y), (1 - x, 1 - y)]
        cps = [pltpu.make_async_remote_copy(
            src_ref=ins[w].at[2 * chip[0] + chip[1]], dst_ref=outs[w].at[j],
            send_sem=send_sems.at[3 * w + j], recv_sem=recv_sems.at[3 * w + j],
            device_id=(*chip, c), device_id_type=MESH)
            for j, chip in enumerate(chips) for w in range(n)]
        for cp in cps:
            cp.start()
        for cp in cps:
            cp.wait()

    return pl.pallas_call(
        body, name="grad_scatter_chips",
        out_shape=[jax.ShapeDtypeStruct((3,) + r.shape[1:], r.dtype) for r in reds],
        in_specs=_hbm_specs(n), out_specs=_hbm_specs(n),
        scratch_shapes=[pltpu.SemaphoreType.DMA((3 * n,)), pltpu.SemaphoreType.DMA((3 * n,))],
    )(*reds)


def _join_halves(halves):
    n = len(halves)

    def body(*refs):
        ins, outs, (send_sems, recv_sems) = refs[:n], refs[n:2 * n], refs[2 * n:]
        x, y, c = lax.axis_index("x"), lax.axis_index("y"), lax.axis_index("c")
        cps = [pltpu.make_async_remote_copy(
            src_ref=ins[w], dst_ref=outs[w], send_sem=send_sems.at[w], recv_sem=recv_sems.at[w],
            device_id=(x, y, 1 - c), device_id_type=MESH) for w in range(n)]
        for cp in cps:
            cp.start()
        for cp in cps:
            cp.wait()

    return pl.pallas_call(
        body, name="grad_join_halves",
        out_shape=[jax.ShapeDtypeStruct(h.shape, h.dtype) for h in halves],
        in_specs=_hbm_specs(n), out_specs=_hbm_specs(n),
        scratch_shapes=[pltpu.SemaphoreType.DMA((n,)), pltpu.SemaphoreType.DMA((n,))],
    )(*halves)


def _row_tile(rows):
    for cand in (256, 176, 128, 64, 32, 16):
        if rows % cand == 0:
            return cand
    raise ValueError(rows)


def _add_my_half(g, other, c_idx, name):
    _, k, n = g.shape
    hr = k // 2
    tr = _row_tile(hr)
    nb = hr // tr

    def body(c_ref, g_ref, o_ref, out_ref, out16_ref):
        s = g_ref[...] + o_ref[...]
        out_ref[...] = s
        out16_ref[...] = s.astype(BF16)

    return pl.pallas_call(
        body, name=name,
        grid_spec=pltpu.PrefetchScalarGridSpec(
            num_scalar_prefetch=1, grid=(N_CHIPS, nb),
            in_specs=[pl.BlockSpec((1, tr, n), lambda j, i, c: (j, c[0] * nb + i, 0)),
                      pl.BlockSpec((1, tr, n), lambda j, i, c: (j, i, 0))],
            out_specs=[pl.BlockSpec((1, tr, n), lambda j, i, c: (j, i, 0)),
                       pl.BlockSpec((1, tr, n), lambda j, i, c: (j, i, 0))]),
        out_shape=[jax.ShapeDtypeStruct((N_CHIPS, hr, n), F32), jax.ShapeDtypeStruct((N_CHIPS, hr, n), BF16)],
        compiler_params=_params(("parallel", "parallel")),
    )(c_idx, g, other)


def _add_chips(red, recv, chip_idx, name):
    _, hr, n = red.shape
    tr = _row_tile(hr)

    def body(k_ref, r_ref, v_ref, out_ref):
        out_ref[...] = ((r_ref[0] + v_ref[0].astype(F32)) + v_ref[1].astype(F32)) + v_ref[2].astype(F32)

    return pl.pallas_call(
        body, name=name,
        grid_spec=pltpu.PrefetchScalarGridSpec(
            num_scalar_prefetch=1, grid=(hr // tr,),
            in_specs=[pl.BlockSpec((1, tr, n), lambda i, k: (k[0], i, 0)),
                      pl.BlockSpec((3, tr, n), lambda i, k: (0, i, 0))],
            out_specs=pl.BlockSpec((tr, n), lambda i, k: (i, 0))),
        out_shape=jax.ShapeDtypeStruct((hr, n), F32),
        compiler_params=_params(("parallel",)),
    )(chip_idx, red, recv)


def _mod_shard(c_all, w_ada, b_ada):
    nb, cols = c_all.shape[0], w_ada.shape[1]

    def body(c_ref, w_ref, b_ref, o_ref):
        c = c_ref[...]
        o_ref[...] = _dot(c * jax.nn.sigmoid(c), w_ref[...]) + b_ref[...]

    return pl.pallas_call(
        body, name="mod_shard", out_shape=jax.ShapeDtypeStruct((nb, cols), F32),
        compiler_params=_params(vmem_mb=48),
    )(c_all, w_ada, b_ada)


def _proj(x2, mod8, w, seq, out_dtype, name):
    t = x2.shape[0]
    n = w.shape[1]
    tm, tn = min(512, seq), min(1152, n)
    tpb = seq // tm

    def body(x_ref, mod_ref, w_ref, o_ref, h_scr):
        @pl.when(pl.program_id(1) == 0)
        def _():
            h_scr[...] = (x_ref[...] * (1.0 + mod_ref[0, 1:2, :]) + mod_ref[0, 0:1, :]).astype(BF16)
        o_ref[...] = jnp.dot(h_scr[...], w_ref[...], preferred_element_type=F32).astype(o_ref.dtype)

    return pl.pallas_call(
        body, name=name, grid=(t // tm, n // tn),
        in_specs=[pl.BlockSpec((tm, D), lambda i, j: (i, 0)),
                  pl.BlockSpec((1, 8, D), lambda i, j: (i // tpb, 0, 0)),
                  pl.BlockSpec((D, tn), lambda i, j: (0, j))],
        out_specs=pl.BlockSpec((tm, tn), lambda i, j: (i, j)),
        out_shape=jax.ShapeDtypeStruct((t, n), out_dtype),
        scratch_shapes=[pltpu.VMEM((tm, D), BF16)],
        compiler_params=_params(("parallel", "arbitrary"), 48),
    )(x2, mod8, w)


def _tn_matmul(a, b, name, seq, mod8=None, rows=None, split=None):
    a_st, b_st = a.ndim == 3, b.ndim == 3
    t, ka = a.shape[-2:]
    n = b.shape[-1]
    tt = min(1024, seq)
    tpb = seq // tt
    nt = t // tt
    if a_st or b_st:
        steps, tn = (a.shape[0] if a_st else b.shape[0]), n
    else:
        tn = split
        if tn is None:
            tn = next(cand for cand in (1152, 1024, 1408, 512, n) if n % cand == 0)
        steps = n // tn
    stacked_out = a_st or b_st or split is not None

    def body(*refs):
        if mod8 is None:
            a_ref, b_ref, o_ref = refs
            av = a_ref[0] if a_st else a_ref[...]
        else:
            a_ref, m_ref, b_ref, o_ref = refs
            av = a_ref[...] * (1.0 + m_ref[0, rows[1]:rows[1] + 1, :]) + m_ref[0, rows[0]:rows[0] + 1, :]
        part = _dot_tn(av, b_ref[0] if b_st else b_ref[...])
        if stacked_out:
            part = part[None]

        @pl.when(pl.program_id(1) == 0)
        def _():
            o_ref[...] = part

        @pl.when(pl.program_id(1) > 0)
        def _():
            o_ref[...] += part

    if a_st:
        in_specs = [pl.BlockSpec((1, tt, ka), lambda j, k: (j, k, 0))]
    else:
        in_specs = [pl.BlockSpec((tt, ka), lambda j, k: (k, 0))]
    args = [a]
    if mod8 is not None:
        in_specs.append(pl.BlockSpec((1, 8, ka), lambda j, k: (k // tpb, 0, 0)))
        args.append(mod8)
    if b_st:
        in_specs.append(pl.BlockSpec((1, tt, n), lambda j, k: (j, k, 0)))
    else:
        in_specs.append(pl.BlockSpec((tt, tn), lambda j, k: (k, 0 if a_st else j)))
    args.append(b)
    if stacked_out:
        out_spec = pl.BlockSpec((1, ka, tn), lambda j, k: (j, 0, 0))
        out_shape = jax.ShapeDtypeStruct((steps, ka, tn), F32)
    else:
        out_spec = pl.BlockSpec((ka, tn), lambda j, k: (0, j))
        out_shape = jax.ShapeDtypeStruct((ka, n), F32)
    return pl.pallas_call(
        body, name=name, grid=(steps, nt), in_specs=in_specs, out_specs=out_spec, out_shape=out_shape,
        compiler_params=_params(("parallel", "arbitrary"), 56),
    )(*args)


def _dh_kernel(dproj, w_p, x2, dxp, mod8, seq):
    t = x2.shape[0]
    tm, tk = min(512, seq), 1152
    tpb = seq // tm
    nk = NP // tk
    nbatch = t // seq

    def body(dp_ref, w_ref, x_ref, dxp_ref, mod_ref, gx_ref, dm_ref, acc):
        i, k = pl.program_id(0), pl.program_id(1)

        @pl.when(k == 0)
        def _():
            acc[...] = jnp.zeros_like(acc)

        acc[...] += _dot_nt(dp_ref[...], w_ref[...])

        @pl.when(k == nk - 1)
        def _():
            dh = acc[...]
            gx_ref[...] = dxp_ref[...] + dh * (1.0 + mod_ref[0, 1:2, :])
            upd = jnp.concatenate(
                [jnp.sum(dh, axis=0, keepdims=True), jnp.sum(dh * x_ref[...], axis=0, keepdims=True),
                 jnp.zeros((6, D), F32)], axis=0)

            @pl.when(i % tpb == 0)
            def _():
                dm_ref[0] = upd

            @pl.when(i % tpb != 0)
            def _():
                dm_ref[0] += upd

    return pl.pallas_call(
        body, name="dh", grid=(t // tm, nk),
        in_specs=[pl.BlockSpec((tm, tk), lambda i, k: (i, k)),
                  pl.BlockSpec((D, tk), lambda i, k: (0, k)),
                  pl.BlockSpec((tm, D), lambda i, k: (i, 0)),
                  pl.BlockSpec((tm, D), lambda i, k: (i, 0)),
                  pl.BlockSpec((1, 8, D), lambda i, k: (i // tpb, 0, 0))],
        out_specs=[pl.BlockSpec((tm, D), lambda i, k: (i, 0)),
                   pl.BlockSpec((1, 8, D), lambda i, k: (i // tpb, 0, 0))],
        out_shape=[jax.ShapeDtypeStruct((t, D), F32), jax.ShapeDtypeStruct((nbatch, 8, D), F32)],
        scratch_shapes=[pltpu.VMEM((tm, D), F32)],
        compiler_params=_params(("arbitrary", "arbitrary"), 48),
    )(dproj, w_p, x2, dxp, mod8)


def _tri(n, upper):
    r = lax.broadcasted_iota(jnp.int32, (n, n), 0)
    c = lax.broadcasted_iota(jnp.int32, (n, n), 1)
    return jnp.where((c >= r) if upper else (c <= r), 1.0, 0.0).astype(F32)


@jax.custom_vjp
def _mm_nn(a, b):
    return _dot(a, b)


_mm_nn.defvjp(lambda a, b: (_dot(a, b), (a, b)),
              lambda res, g: (_dot_nt(g, res[1]), _dot_tn(res[0], g)))


@jax.custom_vjp
def _mm_nt(a, b):
    return _dot_nt(a, b)


_mm_nt.defvjp(lambda a, b: (_dot_nt(a, b), (a, b)),
              lambda res, g: (_dot(g, res[1]), _dot_tn(g, res[0])))


@jax.custom_vjp
def _mm_tn(a, b):
    return _dot_tn(a, b)


_mm_tn.defvjp(lambda a, b: (_dot_tn(a, b), (a, b)),
              lambda res, g: (_dot_nt(res[1], g), _dot(res[0], g)))


@jax.custom_vjp
def _cumsum_rows(x):
    return _dot_f32(_tri(x.shape[0], False), x)


_cumsum_rows.defvjp(lambda x: (_cumsum_rows(x), None),
                    lambda _, g: (_dot_f32(_tri(g.shape[0], True), g),))


@functools.partial(jax.custom_vjp, nondiff_argnums=(1,))
def _shift_rows(x, k):
    return pltpu.roll(x, k % x.shape[0], 0)


_shift_rows.defvjp(lambda x, k: (_shift_rows(x, k), None),
                   lambda k, _, g: (pltpu.roll(g, (-k) % g.shape[0], 0),))


def _group_ref(bc, m):
    n = bc.shape[0] // (2 * m)
    b3 = bc.reshape(n, 2 * m, ADH)
    row = lax.broadcasted_iota(jnp.int32, b3.shape, 1)
    ref = jnp.sum(jnp.where(row == m - 1, b3, 0.0), axis=1, keepdims=True)
    return jnp.broadcast_to(ref, b3.shape).reshape(bc.shape)


def _hgrn_block(q, fl, v, g, st, lb, nw):
    n = q.shape[0]
    f = lb + (1.0 - lb) * jax.nn.sigmoid(fl)
    kk = 1.0 - f
    lf = jnp.log(f)
    bc = _cumsum_rows(lf)
    row = lax.broadcasted_iota(jnp.int32, (n, ADH), 0)
    same = jnp.bitwise_xor(lax.broadcasted_iota(jnp.int32, (n, n), 0), lax.broadcasted_iota(jnp.int32, (n, n), 1))
    a = jnp.zeros((n, n), F32)
    m = 1
    while m < n:
        r = jnp.bitwise_and(row, 2 * m - 1)
        up, lo = r >= m, r < m
        if m == 1:
            aq, ak = lf, jnp.zeros_like(lf)
        elif m == 2:
            aq = jnp.where(r == 3, lf + _shift_rows(lf, 1), lf)
            ak = jnp.where(r == 0, _shift_rows(lf, -1), 0.0)
        else:
            ref = _group_ref(bc, m)
            aq, ak = bc - ref, ref - bc
        qt = jnp.where(up, q * jnp.exp(jnp.where(up, aq, 0.0)), 0.0)
        kt = jnp.where(lo, kk * jnp.exp(jnp.where(lo, ak, 0.0)), 0.0)
        a = a + jnp.where(same < 2 * m, _mm_nt(qt, kt), 0.0)
        m *= 2
    last = row == n - 1
    bl = jnp.sum(jnp.where(last, bc, 0.0), axis=0, keepdims=True)
    o = _mm_nn(a, v) + _mm_nt(q * jnp.exp(bc), st) + jnp.sum(q * kk, axis=-1, keepdims=True) * v
    st_new = st * jnp.exp(bl) + _mm_tn(v, kk * jnp.exp(bl - bc))
    rms = lax.rsqrt(jnp.mean(o * o, axis=-1, keepdims=True) + RMS_EPS)
    return o * rms * nw * jax.nn.sigmoid(g), st_new


def _hgrn_fwd(proj, lb_logits, norm_w, nbatch, seq):
    t = proj.shape[0]
    blk = min(HGRN_BLOCK, seq)
    nb = seq // blk

    def body(p_ref, lbl_ref, nw_ref, y_ref, ck_ref, st_s):
        @pl.when(pl.program_id(2) == 0)
        def _():
            st_s[...] = jnp.zeros_like(st_s)

        st = st_s[...]
        ck_ref[0] = st
        lb = jax.nn.sigmoid(lbl_ref[0:1, :] - lbl_ref[1:2, :])
        p = p_ref[...].astype(F32)
        y, st_new = _hgrn_block(p[:, 0:128], p[:, 128:256], p[:, 256:384], p[:, 384:512], st, lb, nw_ref[...])
        st_s[...] = st_new
        y_ref[...] = y.astype(y_ref.dtype)

    return pl.pallas_call(
        body, name="hgrn_fwd", grid=(AH, nbatch, nb),
        in_specs=[pl.BlockSpec((blk, 512), lambda h, b, i: (b * nb + i, COL_A // 512 + h)),
                  pl.BlockSpec((2, 128), lambda h, b, i: (0, h)),
                  pl.BlockSpec((1, 128), lambda h, b, i: (0, h))],
        out_specs=[pl.BlockSpec((blk, 128), lambda h, b, i: (b * nb + i, h)),
                   pl.BlockSpec((1, 128, 128), lambda h, b, i: ((h * nbatch + b) * nb + i, 0, 0))],
        out_shape=[jax.ShapeDtypeStruct((t, AW), BF16), jax.ShapeDtypeStruct((AH * nbatch * nb, 128, 128), F32)],
        scratch_shapes=[pltpu.VMEM((128, 128), F32)],
        compiler_params=_params(("parallel", "parallel", "arbitrary"), 48),
    )(proj, lb_logits, norm_w)


def _hgrn_bwd(proj, dya, ckpt, lb_logits, norm_w, dproj, nbatch, seq):
    t = proj.shape[0]
    blk = min(HGRN_BLOCK, seq)
    nb = seq // blk

    def body(p_ref, dy_ref, ck_ref, lbl_ref, nw_ref, dp_in, dp_ref, sm_ref, dst_s):
        del dp_in
        b_id, i = pl.program_id(1), pl.program_id(2)

        @pl.when(i == 0)
        def _():
            dst_s[...] = jnp.zeros_like(dst_s)

        lb = jax.nn.sigmoid(lbl_ref[0:1, :] - lbl_ref[1:2, :])
        p = p_ref[...].astype(F32)
        _, pullback = jax.vjp(_hgrn_block, p[:, 0:128], p[:, 128:256], p[:, 256:384], p[:, 384:512],
                              ck_ref[0], lb, nw_ref[...])
        dq, dfl, dv, dg, dst, dlb, dnw = pullback((dy_ref[...], dst_s[...]))
        dst_s[...] = dst
        dp_ref[:, 0:128] = dq.astype(dp_ref.dtype)
        dp_ref[:, 128:256] = dfl.astype(dp_ref.dtype)
        dp_ref[:, 256:384] = dv.astype(dp_ref.dtype)
        dp_ref[:, 384:512] = dg.astype(dp_ref.dtype)
        upd = jnp.concatenate([dlb, dnw, jnp.zeros((6, 128), F32)], axis=0)
        first = (b_id == 0) & (i == 0)

        @pl.when(first)
        def _():
            sm_ref[...] = upd

        @pl.when(jnp.logical_not(first))
        def _():
            sm_ref[...] += upd

    def rows(h, b, i):
        return b * nb + (nb - 1 - i)

    return pl.pallas_call(
        body, name="hgrn_bwd", grid=(AH, nbatch, nb),
        in_specs=[pl.BlockSpec((blk, 512), lambda h, b, i: (rows(h, b, i), COL_A // 512 + h)),
                  pl.BlockSpec((blk, 128), lambda h, b, i: (rows(h, b, i), h)),
                  pl.BlockSpec((1, 128, 128), lambda h, b, i: ((h * nbatch + b) * nb + (nb - 1 - i), 0, 0)),
                  pl.BlockSpec((2, 128), lambda h, b, i: (0, h)),
                  pl.BlockSpec((1, 128), lambda h, b, i: (0, h)),
                  pl.BlockSpec(memory_space=pl.ANY)],
        out_specs=[pl.BlockSpec((blk, 512), lambda h, b, i: (rows(h, b, i), COL_A // 512 + h)),
                   pl.BlockSpec((8, 128), lambda h, b, i: (0, h))],
        out_shape=[jax.ShapeDtypeStruct((t, NP), BF16), jax.ShapeDtypeStruct((8, AW), F32)],
        input_output_aliases={5: 0},
        scratch_shapes=[pltpu.VMEM((128, 128), F32)],
        compiler_params=_params(("parallel", "arbitrary", "arbitrary"), 48),
    )(proj, dya, ckpt, lb_logits, norm_w, dproj)


def _log_sigmoid(z):
    return jnp.minimum(z, 0.0) - jnp.log(1.0 + jnp.exp(-jnp.abs(z)))


def _fox_cum(proj, bias128, nbatch, seq):
    t = proj.shape[0]
    ts = min(512, seq)
    nb = seq // ts

    def body(p_ref, b_ref, c_ref, carry):
        @pl.when(pl.program_id(1) == 0)
        def _():
            carry[...] = jnp.zeros_like(carry)
        cum = _dot_f32(_tri(ts, False), _log_sigmoid(p_ref[...] + b_ref[...])) + carry[...]
        carry[...] = cum[ts - 1:ts, :]
        cum2 = cum * LOG2E
        lane = lax.broadcasted_iota(jnp.int32, (ts, 128), 1)
        for p in range(4):
            c_ref[p] = jnp.where(lane < 64, cum2[:, 2 * p:2 * p + 1], cum2[:, 2 * p + 1:2 * p + 2])

    return pl.pallas_call(
        body, name="fox_cum", grid=(nbatch, nb),
        in_specs=[pl.BlockSpec((ts, 128), lambda b, i: (b * nb + i, 0)),
                  pl.BlockSpec((1, 128), lambda b, i: (0, 0))],
        out_specs=pl.BlockSpec((4, ts, 128), lambda b, i: (0, b * nb + i, 0)),
        out_shape=jax.ShapeDtypeStruct((4, t, 128), F32),
        scratch_shapes=[pltpu.VMEM((1, 128), F32)],
        compiler_params=_params(("parallel", "arbitrary")),
    )(proj, bias128)


def _fox_scores_t(q_ref, kv_ref, cc_ref, hh, masked, tq, tk):
    kh = kv_ref[:, 64 * hh:64 * hh + 64].astype(BF16)
    qh = (q_ref[:, 64 * hh:64 * hh + 64] * (LOG2E * BDH ** -0.5)).astype(BF16)
    s = _dot_nt(kh, qh) - cc_ref[0, :, 64 * hh:64 * hh + 1]
    if masked:
        key = lax.broadcasted_iota(jnp.int32, (tk, tq), 0)
        qry = lax.broadcasted_iota(jnp.int32, (tk, tq), 1)
        s = jnp.where(key <= qry, s, NEG)
    return s, kh


def _causal_pairs(nq, key_major):
    if key_major:
        pairs = [(i, j) for j in range(nq) for i in range(j, nq)]
    else:
        pairs = [(i, j) for i in range(nq) for j in range(i + 1)]
    return (jnp.asarray([p[0] for p in pairs], jnp.int32), jnp.asarray([p[1] for p in pairs], jnp.int32))


def _with_ones_lane(x128, hh):
    lane = lax.broadcasted_iota(jnp.int32, x128.shape, 1)
    one = jnp.ones_like(x128)
    zero = jnp.zeros_like(x128)
    if hh == 0:
        return jnp.where(lane < 64, x128, jnp.where(lane == 64, one, zero))
    return jnp.where(lane >= 64, x128, jnp.where(lane == 0, one, zero))


def _fox_fwd(proj, cum_cols, nbatch, seq):
    t = proj.shape[0]
    tq = tk = min(512, seq)
    nq = seq // tq
    qi, kj = _causal_pairs(nq, key_major=False)

    def body(qi_ref, kj_ref, q_ref, kv_ref, cc_ref, o_ref, lse_ref, m_s, acc_s):
        s_id = pl.program_id(2)
        i, j = qi_ref[s_id], kj_ref[s_id]

        @pl.when(j == 0)
        def _():
            m_s[...] = jnp.full_like(m_s, NEG)
            acc_s[...] = jnp.zeros_like(acc_s)

        def step(masked):
            for hh in range(2):
                s, _ = _fox_scores_t(q_ref, kv_ref, cc_ref, hh, masked, tq, tk)
                m_prev = m_s[hh:hh + 1, :]
                m_new = jnp.maximum(m_prev, jnp.max(s, axis=0, keepdims=True))
                alpha = jnp.exp2(m_prev - m_new)
                p = jnp.exp2(s - m_new).astype(BF16)
                v_aug = _with_ones_lane(kv_ref[:, 128:256].astype(BF16), hh)
                acc_s[hh] = acc_s[hh] * alpha + _dot_tn(v_aug, p)
                m_s[hh:hh + 1, :] = m_new

        @pl.when(j < i)
        def _():
            step(False)

        @pl.when(j == i)
        def _():
            step(True)
            a0, a1 = acc_s[0], acc_s[1]
            l0, l1 = a0[64:65, :], a1[0:1, :]
            o_t = jnp.concatenate([a0[0:64, :] / l0, a1[64:128, :] / l1], axis=0)
            o_ref[...] = o_t.T.astype(o_ref.dtype)
            lse_ref[0, 0] = jnp.concatenate(
                [m_s[0:1, :] + jnp.log2(l0), m_s[1:2, :] + jnp.log2(l1), jnp.zeros((6, tq), F32)], axis=0)

    return pl.pallas_call(
        body, name="fox_fwd",
        grid_spec=pltpu.PrefetchScalarGridSpec(
            num_scalar_prefetch=2, grid=(nbatch, 4, qi.shape[0]),
            in_specs=[pl.BlockSpec((tq, 128), lambda b, p, s, qi, kj: (b * nq + qi[s], COL_BQ // 128 + p)),
                      pl.BlockSpec((tk, 256), lambda b, p, s, qi, kj: (b * nq + kj[s], COL_KV // 256 + p)),
                      pl.BlockSpec((1, tk, 128), lambda b, p, s, qi, kj: (p, b * nq + kj[s], 0))],
            out_specs=[pl.BlockSpec((tq, 128), lambda b, p, s, qi, kj: (b * nq + qi[s], p)),
                       pl.BlockSpec((1, 1, 8, tq), lambda b, p, s, qi, kj: (b, p, 0, qi[s]))],
            scratch_shapes=[pltpu.VMEM((8, tq), F32), pltpu.VMEM((2, 128, tq), F32)]),
        out_shape=[jax.ShapeDtypeStruct((t, 512), BF16), jax.ShapeDtypeStruct((nbatch, 4, 8, seq), F32)],
        compiler_params=_params(("parallel", "parallel", "arbitrary"), 48),
    )(qi, kj, proj, proj, cum_cols)


def _fox_bwd(proj, cum_cols, lse, yb, dyb, dproj, nbatch, seq):
    t = proj.shape[0]
    tq = tk = min(512, seq)
    nq = seq // tq
    scale = BDH ** -0.5
    qi, kj = _causal_pairs(nq, key_major=True)
    nsteps = qi.shape[0]

    def body(qi_ref, kj_ref, q_ref, kv_ref, cc_ref, lse_ref, o_ref, do_ref, dp_in,
             dkv_ref, dq_ref, drs_ref, dcs_ref, dkv_s, dqa_s, dcs_s):
        del dp_in
        hp, s_id = pl.program_id(1), pl.program_id(2)
        i, j = qi_ref[s_id], kj_ref[s_id]

        @pl.when(i == j)
        def _():
            dkv_s[...] = jnp.zeros_like(dkv_s)
            dcs_s[...] = jnp.zeros_like(dcs_s)

        @pl.when(s_id == 0)
        def _():
            dqa_s[...] = jnp.zeros_like(dqa_s)

        def step(masked):
            lane = lax.broadcasted_iota(jnp.int32, (tq, 128), 1)
            for hh in range(2):
                s, _ = _fox_scores_t(q_ref, kv_ref, cc_ref, hh, masked, tq, tk)
                p = jnp.exp2(s - lse_ref[0, 0, hh:hh + 1, :])
                doh = do_ref[:, 64 * hh:64 * hh + 64]
                dd = lax.dot_general(jnp.ones((8, 64), F32), doh * o_ref[:, 64 * hh:64 * hh + 64].astype(F32),
                                     (((1,), (1,)), ((), ())), preferred_element_type=F32, precision=HIGHEST)[0:1, :]
                doh = doh.astype(BF16)
                dp = _dot_nt(kv_ref[:, 128 + 64 * hh:192 + 64 * hh], doh)
                ds = (p * (dp - dd)).astype(BF16)
                dkv_s[:, 128 + 64 * hh:192 + 64 * hh] += _dot(p, doh)
                dkv_s[:, 64 * hh:64 * hh + 64] += _dot(ds, q_ref[:, 64 * hh:64 * hh + 64] * scale)
                k_aug = _with_ones_lane(kv_ref[:, 0:128].astype(BF16), hh)
                dqa_s[i, hh] += _dot_tn(k_aug, ds)
                sel = jnp.where(lane == 2 * hp + hh, 1.0, 0.0).astype(BF16)
                dcs_s[...] += _dot(ds, sel)

        @pl.when(i == j)
        def _():
            step(True)

        @pl.when(i > j)
        def _():
            step(False)

        @pl.when(i == nq - 1)
        def _():
            dkv_ref[...] = dkv_s[...].astype(dkv_ref.dtype)
            dcs_ref[0] = dcs_s[...]

        @pl.when(s_id == nsteps - 1)
        def _():
            lane = lax.broadcasted_iota(jnp.int32, (tq, 128), 1)
            for blk in range(nq):
                a0 = dqa_s[blk, 0].T
                a1 = dqa_s[blk, 1].T
                rows = pl.ds(blk * tq, tq)
                dq_ref[rows, :] = (jnp.where(lane < 64, a0, a1) * scale).astype(dq_ref.dtype)
                drs_ref[0, rows, :] = jnp.where(lane == 2 * hp, a0[:, 64:65], jnp.where(lane == 2 * hp + 1, a1[:, 0:1], 0.0))

    return pl.pallas_call(
        body, name="fox_bwd",
        grid_spec=pltpu.PrefetchScalarGridSpec(
            num_scalar_prefetch=2, grid=(nbatch, 4, nsteps),
            in_specs=[pl.BlockSpec((tq, 128), lambda b, p, s, qi, kj: (b * nq + qi[s], COL_BQ // 128 + p)),
                      pl.BlockSpec((tk, 256), lambda b, p, s, qi, kj: (b * nq + kj[s], COL_KV // 256 + p)),
                      pl.BlockSpec((1, tk, 128), lambda b, p, s, qi, kj: (p, b * nq + kj[s], 0)),
                      pl.BlockSpec((1, 1, 8, tq), lambda b, p, s, qi, kj: (b, p, 0, qi[s])),
                      pl.BlockSpec((tq, 128), lambda b, p, s, qi, kj: (b * nq + qi[s], p)),
                      pl.BlockSpec((tq, 128), lambda b, p, s, qi, kj: (b * nq + qi[s], p)),
                      pl.BlockSpec(memory_space=pl.ANY)],
            out_specs=[pl.BlockSpec((tk, 256), lambda b, p, s, qi, kj: (b * nq + kj[s], COL_KV // 256 + p)),
                       pl.BlockSpec((seq, 128), lambda b, p, s, qi, kj: (b, p)),
                       pl.BlockSpec((1, seq, 128), lambda b, p, s, qi, kj: (p, b, 0)),
                       pl.BlockSpec((1, tk, 128), lambda b, p, s, qi, kj: (p, b * nq + kj[s], 0))],
            scratch_shapes=[pltpu.VMEM((tk, 256), F32), pltpu.VMEM((nq, 2, 128, tq), F32),
                            pltpu.VMEM((tk, 128), F32)]),
        out_shape=[jax.ShapeDtypeStruct((t, NP), BF16), jax.ShapeDtypeStruct((t, 512), BF16),
                   jax.ShapeDtypeStruct((4, t, 128), F32), jax.ShapeDtypeStruct((4, t, 128), F32)],
        input_output_aliases={8: 0},
        compiler_params=_params(("parallel", "parallel", "arbitrary"), 56),
    )(qi, kj, proj, proj, cum_cols, lse, yb, dyb, dproj)


def _place_cols(dproj, src, col):
    t, w = src.shape
    tm = 1024 if t % 1024 == 0 else t

    def body(s_ref, dp_in, o_ref):
        del dp_in
        o_ref[...] = s_ref[...]

    return pl.pallas_call(
        body, name="place_cols", grid=(t // tm,),
        in_specs=[pl.BlockSpec((tm, w), lambda i: (i, 0)), pl.BlockSpec(memory_space=pl.ANY)],
        out_specs=pl.BlockSpec((tm, w), lambda i: (i, col // w)),
        out_shape=jax.ShapeDtypeStruct(dproj.shape, dproj.dtype),
        input_output_aliases={1: 0},
        compiler_params=_params(("parallel",)),
    )(src, dproj)


def _fox_dbf(proj, bias128, drs, dcs, dproj, nbatch, seq):
    t = proj.shape[0]
    ts = min(512, seq)
    nb = seq // ts

    def body(p_ref, b_ref, dr_ref, dc_ref, dp_in, dp_ref, sm_ref, carry):
        del dp_in
        b_id, i = pl.program_id(0), pl.program_id(1)

        @pl.when(i == 0)
        def _():
            carry[...] = jnp.zeros_like(carry)

        dcum = (dr_ref[0] - dc_ref[0]) + (dr_ref[1] - dc_ref[1]) + (dr_ref[2] - dc_ref[2]) + (dr_ref[3] - dc_ref[3])
        rc = _dot_f32(_tri(ts, True), dcum) + carry[...]
        carry[...] = rc[0:1, :]
        z = p_ref[...] + b_ref[...]
        lane = lax.broadcasted_iota(jnp.int32, (ts, 128), 1)
        dz = jnp.where(lane < BH, rc * jax.nn.sigmoid(-z), 0.0)
        dp_ref[...] = dz.astype(dp_ref.dtype)
        upd = jnp.concatenate([jnp.sum(dz, axis=0, keepdims=True), jnp.zeros((7, 128), F32)], axis=0)
        first = (b_id == 0) & (i == 0)

        @pl.when(first)
        def _():
            sm_ref[...] = upd

        @pl.when(jnp.logical_not(first))
        def _():
            sm_ref[...] += upd

    def rows(b, i):
        return b * nb + (nb - 1 - i)

    return pl.pallas_call(
        body, name="fox_dbf", grid=(nbatch, nb),
        in_specs=[pl.BlockSpec((ts, 128), lambda b, i: (rows(b, i), 0)),
                  pl.BlockSpec((1, 128), lambda b, i: (0, 0)),
                  pl.BlockSpec((4, ts, 128), lambda b, i: (0, rows(b, i), 0)),
                  pl.BlockSpec((4, ts, 128), lambda b, i: (0, rows(b, i), 0)),
                  pl.BlockSpec(memory_space=pl.ANY)],
        out_specs=[pl.BlockSpec((ts, 128), lambda b, i: (rows(b, i), COL_BF // 128)),
                   pl.BlockSpec((8, 128), lambda b, i: (0, 0))],
        out_shape=[jax.ShapeDtypeStruct((t, NP), BF16), jax.ShapeDtypeStruct((8, 128), F32)],
        input_output_aliases={4: 0},
        scratch_shapes=[pltpu.VMEM((1, 128), F32)],
        compiler_params=_params(("arbitrary", "arbitrary")),
    )(proj, bias128, drs, dcs, dproj)


def _ln_stats(z):
    mu = jnp.mean(z, axis=-1, keepdims=True)
    zc = z - mu
    rstd = lax.rsqrt(jnp.mean(zc * zc, axis=-1, keepdims=True) + LN_EPS)
    return zc * rstd, rstd


def _ln_bwd(dy, xhat, rstd, w):
    dxh = dy * w
    return rstd * (dxh - jnp.mean(dxh, axis=-1, keepdims=True) - xhat * jnp.mean(dxh * xhat, axis=-1, keepdims=True))


def _merge_fwd(ya, yb, proj, x2, mod8, wba, wbb, wout, ln1w, ln1b, seq):
    t = x2.shape[0]
    tm = min(256, seq)
    tpb = seq // tm

    def body(ya_ref, yb_ref, g_ref, x_ref, mod_ref, wa_ref, wb_ref, wo_ref, lw_ref, lb_ref, mg_ref, u_ref, x1_ref):
        ga = jax.nn.sigmoid(g_ref[:, 0:D].astype(F32))
        gb = jax.nn.sigmoid(g_ref[:, D:2 * D].astype(F32))
        merged = (ga * jnp.dot(ya_ref[...], wa_ref[...], preferred_element_type=F32)
                  + gb * jnp.dot(yb_ref[...], wb_ref[...], preferred_element_type=F32))
        mg = merged.astype(BF16)
        mg_ref[...] = mg
        u = jnp.dot(mg, wo_ref[...], preferred_element_type=F32)
        u_ref[...] = u
        xhat, _ = _ln_stats(ALPHA * x_ref[...] + (1.0 + mod_ref[0, 2:3, :]) * u)
        x1_ref[...] = xhat * lw_ref[...] + lb_ref[...]

    tok = lambda w: pl.BlockSpec((tm, w), lambda i: (i, 0))
    full = lambda a: pl.BlockSpec(a.shape, lambda i: (0,) * a.ndim)
    return pl.pallas_call(
        body, name="merge_fwd", grid=(t // tm,),
        in_specs=[tok(512), tok(512), pl.BlockSpec((tm, 2048), lambda i: (i, COL_GATES // 2048)), tok(D),
                  pl.BlockSpec((1, 8, D), lambda i: (i // tpb, 0, 0)),
                  full(wba), full(wbb), full(wout), full(ln1w), full(ln1b)],
        out_specs=[tok(D), tok(D), tok(D)],
        out_shape=[jax.ShapeDtypeStruct((t, D), BF16), jax.ShapeDtypeStruct((t, D), F32),
                   jax.ShapeDtypeStruct((t, D), F32)],
        compiler_params=_params(("parallel",), 48),
    )(ya, yb, proj, x2, mod8, wba, wbb, wout, ln1w, ln1b)


def _merge_bwd(du, ya, yb, proj, wba, wbb, wout, seq):
    t = du.shape[0]
    tm = min(256, seq)

    def body(du_ref, ya_ref, yb_ref, g_ref, wa_ref, wb_ref, wo_ref, dp_ref, dpa_ref, dpb_ref, dya_ref, dyb_ref):
        ga = jax.nn.sigmoid(g_ref[:, 0:D].astype(F32))
        gb = jax.nn.sigmoid(g_ref[:, D:2 * D].astype(F32))
        dm = _dot_nt(du_ref[...], wo_ref[...])
        pa = jnp.dot(ya_ref[...], wa_ref[...], preferred_element_type=F32)
        pb = jnp.dot(yb_ref[...], wb_ref[...], preferred_element_type=F32)
        dpa = (dm * ga).astype(BF16)
        dpb = (dm * gb).astype(BF16)
        dpa_ref[...] = dpa
        dpb_ref[...] = dpb
        dp_ref[:, 0:D] = (dm * pa * ga * (1.0 - ga)).astype(BF16)
        dp_ref[:, D:2 * D] = (dm * pb * gb * (1.0 - gb)).astype(BF16)
        dya_ref[...] = _dot_nt(dpa, wa_ref[...])
        dyb_ref[...] = _dot_nt(dpb, wb_ref[...])

    tok = lambda w: pl.BlockSpec((tm, w), lambda i: (i, 0))
    full = lambda a: pl.BlockSpec(a.shape, lambda i: (0,) * a.ndim)
    return pl.pallas_call(
        body, name="merge_bwd", grid=(t // tm,),
        in_specs=[tok(D), tok(512), tok(512), pl.BlockSpec((tm, 2048), lambda i: (i, COL_GATES // 2048)),
                  full(wba), full(wbb), full(wout)],
        out_specs=[pl.BlockSpec((tm, 2048), lambda i: (i, COL_GATES // 2048)), tok(D), tok(D), tok(512), tok(512)],
        out_shape=[jax.ShapeDtypeStruct((t, NP), BF16), jax.ShapeDtypeStruct((t, D), BF16),
                   jax.ShapeDtypeStruct((t, D), BF16), jax.ShapeDtypeStruct((t, 512), F32),
                   jax.ShapeDtypeStruct((t, 512), F32)],
        compiler_params=_params(("parallel",), 48),
    )(du, ya, yb, proj, wba, wbb, wout)


def _ffn_fwd(x1, mod8, wg, wu, wd, target, ln2w, ln2b, seq):
    t = x1.shape[0]
    tm = min(512, seq)
    nf, _, tf = wg.shape
    tpb = seq // tm
    nbatch = t // seq

    def body(x_ref, mod_ref, wg_ref, wu_ref, wd_ref, t_ref, lw_ref, lb_ref,
             a_ref, b_ref, dz_ref, st_ref, dm_ref, h_s, acc):
        i, j = pl.program_id(0), pl.program_id(1)

        @pl.when(j == 0)
        def _():
            h_s[...] = (x_ref[...] * (1.0 + mod_ref[0, 4:5, :]) + mod_ref[0, 3:4, :]).astype(BF16)
            acc[...] = jnp.zeros_like(acc)

        a = jnp.dot(h_s[...], wg_ref[0], preferred_element_type=F32)
        b = jnp.dot(h_s[...], wu_ref[0], preferred_element_type=F32)
        a_ref[0] = a.astype(BF16)
        b_ref[0] = b.astype(BF16)
        acc[...] += _dot(a * jax.nn.sigmoid(a) * b, wd_ref[0])

        @pl.when(j == nf - 1)
        def _():
            ffn = acc[...]
            xhat, rstd = _ln_stats(ALPHA * x_ref[...] + (1.0 + mod_ref[0, 5:6, :]) * ffn)
            diff = xhat * lw_ref[...] + lb_ref[...] - t_ref[...]
            loss = 0.5 * jnp.sum(jnp.sum(diff * diff, axis=-1, keepdims=True), axis=0, keepdims=True) / D
            dy = diff * (1.0 / D)
            dz = _ln_bwd(dy, xhat, rstd, lw_ref[...])
            dz_ref[...] = dz
            lane = lax.broadcasted_iota(jnp.int32, (1, D), 1)
            upd = jnp.concatenate(
                [jnp.sum(dy * xhat, axis=0, keepdims=True), jnp.sum(dy, axis=0, keepdims=True),
                 jnp.where(lane == 0, loss, 0.0), jnp.zeros((5, D), F32)], axis=0)
            dmu = jnp.concatenate(
                [jnp.zeros((5, D), F32), jnp.sum(dz * ffn, axis=0, keepdims=True), jnp.zeros((2, D), F32)], axis=0)

            @pl.when(i == 0)
            def _():
                st_ref[...] = upd

            @pl.when(i > 0)
            def _():
                st_ref[...] += upd

            @pl.when(i % tpb == 0)
            def _():
                dm_ref[0] = dmu

            @pl.when(i % tpb != 0)
            def _():
                dm_ref[0] += dmu

    row = lambda: pl.BlockSpec((tm, D), lambda i, j: (i, 0))
    vec = lambda: pl.BlockSpec((1, D), lambda i, j: (0, 0))
    return pl.pallas_call(
        body, name="ffn_fwd", grid=(t // tm, nf),
        in_specs=[row(), pl.BlockSpec((1, 8, D), lambda i, j: (i // tpb, 0, 0)),
                  pl.BlockSpec((1, D, tf), lambda i, j: (j, 0, 0)), pl.BlockSpec((1, D, tf), lambda i, j: (j, 0, 0)),
                  pl.BlockSpec((1, tf, D), lambda i, j: (j, 0, 0)), row(), vec(), vec()],
        out_specs=[pl.BlockSpec((1, tm, tf), lambda i, j: (j, i, 0)), pl.BlockSpec((1, tm, tf), lambda i, j: (j, i, 0)),
                   row(), pl.BlockSpec((8, D), lambda i, j: (0, 0)),
                   pl.BlockSpec((1, 8, D), lambda i, j: (i // tpb, 0, 0))],
        out_shape=[jax.ShapeDtypeStruct((nf, t, tf), BF16), jax.ShapeDtypeStruct((nf, t, tf), BF16),
                   jax.ShapeDtypeStruct((t, D), F32), jax.ShapeDtypeStruct((8, D), F32),
                   jax.ShapeDtypeStruct((nbatch, 8, D), F32)],
        scratch_shapes=[pltpu.VMEM((tm, D), BF16), pltpu.VMEM((tm, D), F32)],
        compiler_params=_params(("arbitrary", "arbitrary"), 48),
    )(x1, mod8, wg, wu, wd, target, ln2w, ln2b)


def _ffn_bwd(dz2, a, b, wg, wu, wd, x1, x2, u, mod8, ln1w, seq):
    t = x1.shape[0]
    tm = min(512, seq)
    nf, tf, _ = wg.shape
    tpb = seq // tm
    nbatch = t // seq

    def body(dz_ref, a_ref, b_ref, wg_ref, wu_ref, wd_ref, x1_ref, x_ref, u_ref, mod_ref, lw_ref,
             da_ref, db_ref, hm_ref, df_ref, du_ref, dxp_ref, st_ref, dm_ref, acc):
        i, j = pl.program_id(0), pl.program_id(1)

        @pl.when(j == 0)
        def _():
            df_ref[...] = ((1.0 + mod_ref[0, 5:6, :]) * dz_ref[...]).astype(BF16)
            acc[...] = jnp.zeros_like(acc)

        dhm = _dot(df_ref[...], wd_ref[0])
        av = a_ref[0].astype(F32)
        bv = b_ref[0].astype(F32)
        sg = jax.nn.sigmoid(av)
        sl = av * sg
        hm_ref[0] = (sl * bv).astype(BF16)
        da = (dhm * bv * (sg * (1.0 + av * (1.0 - sg)))).astype(BF16)
        db = (dhm * sl).astype(BF16)
        da_ref[0] = da
        db_ref[0] = db
        acc[...] += _dot(da, wg_ref[0]) + _dot(db, wu_ref[0])

        @pl.when(j == nf - 1)
        def _():
            dh2 = acc[...]
            x1v = x1_ref[...]
            uv = u_ref[...]
            dx1 = ALPHA * dz_ref[...] + dh2 * (1.0 + mod_ref[0, 4:5, :])
            xhat, rstd = _ln_stats(ALPHA * x_ref[...] + (1.0 + mod_ref[0, 2:3, :]) * uv)
            dz1 = _ln_bwd(dx1, xhat, rstd, lw_ref[...])
            du_ref[...] = ((1.0 + mod_ref[0, 2:3, :]) * dz1).astype(BF16)
            dxp_ref[...] = ALPHA * dz1
            upd = jnp.concatenate(
                [jnp.sum(dx1 * xhat, axis=0, keepdims=True), jnp.sum(dx1, axis=0, keepdims=True),
                 jnp.zeros((6, D), F32)], axis=0)
            dmu = jnp.concatenate(
                [jnp.zeros((2, D), F32), jnp.sum(dz1 * uv, axis=0, keepdims=True),
                 jnp.sum(dh2, axis=0, keepdims=True), jnp.sum(dh2 * x1v, axis=0, keepdims=True),
                 jnp.zeros((3, D), F32)], axis=0)

            @pl.when(i == 0)
            def _():
                st_ref[...] = upd

            @pl.when(i > 0)
            def _():
                st_ref[...] += upd

            @pl.when(i % tpb == 0)
            def _():
                dm_ref[0] = dmu

            @pl.when(i % tpb != 0)
            def _():
                dm_ref[0] += dmu

    row = lambda: pl.BlockSpec((tm, D), lambda i, j: (i, 0))
    ffb = lambda: pl.BlockSpec((1, tm, tf), lambda i, j: (j, i, 0))
    return pl.pallas_call(
        body, name="ffn_bwd", grid=(t // tm, nf),
        in_specs=[row(), ffb(), ffb(),
                  pl.BlockSpec((1, tf, D), lambda i, j: (j, 0, 0)), pl.BlockSpec((1, tf, D), lambda i, j: (j, 0, 0)),
                  pl.BlockSpec((1, D, tf), lambda i, j: (j, 0, 0)), row(), row(), row(),
                  pl.BlockSpec((1, 8, D), lambda i, j: (i // tpb, 0, 0)), pl.BlockSpec((1, D), lambda i, j: (0, 0))],
        out_specs=[ffb(), ffb(), ffb(), row(), row(), row(), pl.BlockSpec((8, D), lambda i, j: (0, 0)),
                   pl.BlockSpec((1, 8, D), lambda i, j: (i // tpb, 0, 0))],
        out_shape=[jax.ShapeDtypeStruct((nf, t, tf), BF16), jax.ShapeDtypeStruct((nf, t, tf), BF16),
                   jax.ShapeDtypeStruct((nf, t, tf), BF16), jax.ShapeDtypeStruct((t, D), BF16),
                   jax.ShapeDtypeStruct((t, D), BF16), jax.ShapeDtypeStruct((t, D), F32),
                   jax.ShapeDtypeStruct((8, D), F32), jax.ShapeDtypeStruct((nbatch, 8, D), F32)],
        scratch_shapes=[pltpu.VMEM((tm, D), F32)],
        compiler_params=_params(("arbitrary", "arbitrary"), 48),
    )(dz2, a, b, wg, wu, wd, x1, x2, u, mod8, ln1w)


def _adamw_math(w, g, m, v):
    m = B1 * m + (1.0 - B1) * g
    v = B2 * v + (1.0 - B2) * (g * g)
    m_hat = m / (1.0 - B1 ** STEP)
    v_hat = v / (1.0 - B2 ** STEP)
    return -LR * (m_hat / (jnp.sqrt(v_hat) + EPS) + WD * w), m, v


def _adamw(w, g, m, v, name):
    rows, cols = w.shape
    tr = rows
    for cand in (128, 64, 32, 16, 8):
        if rows % cand == 0:
            tr = cand
            break

    def body(w_ref, g_ref, m_ref, v_ref, d_ref, mo_ref, vo_ref):
        d, mn, vn = _adamw_math(w_ref[...], g_ref[...], m_ref[...], v_ref[...])
        d_ref[...] = d
        mo_ref[...] = mn
        vo_ref[...] = vn

    spec = pl.BlockSpec((tr, cols), lambda i: (i, 0))
    return pl.pallas_call(
        body, name=name, grid=(rows // tr,), in_specs=[spec] * 4, out_specs=[spec] * 3,
        out_shape=[jax.ShapeDtypeStruct((rows, cols), F32)] * 3,
        compiler_params=_params(("parallel",), 48),
    )(w, g, m, v)


def _adamw_halves(w, g_mine, g_sib, m, v, c_idx, name):
    rows, cols = w.shape
    hr = rows // 2
    tr = next(cand for cand in (128, 88, 64, 32, 16, 8) if hr % cand == 0)
    tph = hr // tr

    def body(c_ref, w_ref, gm_ref, gs_ref, m_ref, v_ref, g_ref, d_ref, mo_ref, vo_ref):
        g = jnp.where(pl.program_id(0) == c_ref[0], gm_ref[...], gs_ref[...])
        d, mn, vn = _adamw_math(w_ref[...], g, m_ref[...], v_ref[...])
        g_ref[...] = g
        d_ref[...] = d
        mo_ref[...] = mn
        vo_ref[...] = vn

    full = pl.BlockSpec((tr, cols), lambda h, i, c: (h * tph + i, 0))
    half = pl.BlockSpec((tr, cols), lambda h, i, c: (i, 0))
    return pl.pallas_call(
        body, name=name,
        grid_spec=pltpu.PrefetchScalarGridSpec(
            num_scalar_prefetch=1, grid=(2, tph), in_specs=[full, half, half, full, full], out_specs=[full] * 4),
        out_shape=[jax.ShapeDtypeStruct((rows, cols), F32)] * 4,
        compiler_params=_params(("parallel", "parallel"), 48),
    )(c_idx, w, g_mine, g_sib, m, v)


def _grad_w_ada(c_all, dmod_cols):
    def body(c_ref, d_ref, o_ref):
        c = c_ref[...]
        o_ref[...] = lax.dot_general(c * jax.nn.sigmoid(c), d_ref[...], (((0,), (0,)), ((), ())),
                                     preferred_element_type=F32, precision=HIGHEST)

    return pl.pallas_call(
        body, name="grad_w_ada", out_shape=jax.ShapeDtypeStruct((D, dmod_cols.shape[1]), F32),
        compiler_params=_params(vmem_mb=48),
    )(c_all, dmod_cols)


def _small_update(gath, w8, m8, v8):
    def body(g_ref, w_ref, m_ref, v_ref, go_ref, d_ref, mo_ref, vo_ref):
        g0 = g_ref[0, 0:1, :] + g_ref[0, 1:2, :]
        g1 = g_ref[0, 2:3, :]
        for dev in range(1, N_DEV):
            g0 = g0 + (g_ref[dev, 0:1, :] + g_ref[dev, 1:2, :])
            g1 = g1 + g_ref[dev, 2:3, :]
        w = w_ref[...]
        lb = jax.nn.sigmoid(w[1:2, O_LB0:O_LB1] - w[1:2, O_LB1:O_FOX])
        fac = lb * (1.0 - lb)
        g1 = jnp.concatenate([g1[:, :O_LB0], g1[:, O_LB0:O_LB1] * fac, -g1[:, O_LB1:O_FOX] * fac, g1[:, O_FOX:]],
                             axis=1)
        g = jnp.concatenate([g0, g1, jnp.zeros((6, SMALL_W), F32)], axis=0)
        d, mn, vn = _adamw_math(w, g, m_ref[...], v_ref[...])
        go_ref[...] = g
        d_ref[...] = d
        mo_ref[...] = mn
        vo_ref[...] = vn

    return pl.pallas_call(
        body, name="small_update", out_shape=[jax.ShapeDtypeStruct((8, SMALL_W), F32)] * 4,
        compiler_params=_params(vmem_mb=48),
    )(gath, w8, m8, v8)


def _pack_small(b_ada, ln1w, ln1b, ln2w, ln2b, norm_w, lb_logits, fox):
    row1 = jnp.concatenate([ln1w, ln1b, ln2w, ln2b, norm_w, lb_logits[0:1], lb_logits[1:2], fox,
                            jnp.zeros((1, SMALL_W - O_FOX - BH), F32)], axis=1)
    return jnp.concatenate([b_ada, row1, jnp.zeros((6, SMALL_W), F32)], axis=0)


def _unpack_small(p):
    r = p[1:2]
    lb = jnp.concatenate([r[:, O_LB0:O_LB1], r[:, O_LB1:O_FOX]], axis=0)
    return dict(b_ada=p[0:1], ln1_w=r[:, O_LN1W:O_LN1B], ln1_b=r[:, O_LN1B:O_LN2W], ln2_w=r[:, O_LN2W:O_LN2B],
                ln2_b=r[:, O_LN2B:O_NORM], hgrn_norm_w=r[:, O_NORM:O_LB0], lb_logits=lb,
                fox_f_bias=r[:, O_FOX:O_FOX + BH])


_BIG = ("w_in", "w_branch_a", "w_branch_b", "w_out", "w_ffn_gate", "w_ffn_up", "w_ffn_down")


def _cols_of_chips(stacked):
    return jnp.concatenate([stacked[k] for k in range(N_CHIPS)], axis=1)


def kernel(x, c, w_ada, b_ada, w_in, fox_f_bias, lb_logits, hgrn_norm_w, w_branch_a, w_branch_b, w_out, ln1_w, ln1_b, w_ffn_gate, w_ffn_up, w_ffn_down, ln2_w, ln2_b, loss_target, m_w_ada, m_b_ada, m_w_in, m_fox_f_bias, m_lb_logits, m_hgrn_norm_w, m_w_branch_a, m_w_branch_b, m_w_out, m_ln1_w, m_ln1_b, m_w_ffn_gate, m_w_ffn_up, m_w_ffn_down, m_ln2_w, m_ln2_b, v_w_ada, v_b_ada, v_w_in, v_fox_f_bias, v_lb_logits, v_hgrn_norm_w, v_w_branch_a, v_w_branch_b, v_w_out, v_ln1_w, v_ln1_b, v_w_ffn_gate, v_w_ffn_up, v_w_ffn_down, v_ln2_w, v_ln2_b):
    nbatch, seq, _ = x.shape
    t = nbatch * seq
    ax, ay, ac = lax.axis_index("x"), lax.axis_index("y"), lax.axis_index("c")
    chip = 2 * ax + ay
    dev = 2 * chip + ac
    chip_arr = jnp.reshape(chip, (1,)).astype(jnp.int32)
    core_arr = jnp.reshape(ac, (1,)).astype(jnp.int32)

    shard_w = dict(w_in=w_in[0], w_branch_a=w_branch_a[0], w_branch_b=w_branch_b[0], w_out=w_out[0],
                   w_ffn_gate=w_ffn_gate[0], w_ffn_up=w_ffn_up[0], w_ffn_down=w_ffn_down[0])
    shard_m = dict(w_in=m_w_in[0], w_branch_a=m_w_branch_a[0], w_branch_b=m_w_branch_b[0], w_out=m_w_out[0],
                   w_ffn_gate=m_w_ffn_gate[0], w_ffn_up=m_w_ffn_up[0], w_ffn_down=m_w_ffn_down[0])
    shard_v = dict(w_in=v_w_in[0], w_branch_a=v_w_branch_a[0], w_branch_b=v_w_branch_b[0], w_out=v_w_out[0],
                   w_ffn_gate=v_w_ffn_gate[0], w_ffn_up=v_w_ffn_up[0], w_ffn_down=v_w_ffn_down[0])

    shard16 = [shard_w[n].astype(BF16) for n in _BIG]
    full = {n: lax.dynamic_update_slice(g, s[None], (chip, 0, 0))
            for n, g, s in zip(_BIG, _gather_weights(shard16), shard16)}
    w_p = _permute_cols(_cols_of_chips(full["w_in"]))
    wba, wbb = _cols_of_chips(full["w_branch_a"]), _cols_of_chips(full["w_branch_b"])
    wout = full["w_out"].reshape(D, D)
    wg, wu, wd = full["w_ffn_gate"], full["w_ffn_up"], full["w_ffn_down"]

    c8 = jnp.concatenate([c, jnp.zeros((8 - nbatch, D), F32)], axis=0)
    c_all = _allgather8(c8, "gather_c")[:, :nbatch, :].reshape(N_DEV * nbatch, D)
    ncol = w_ada.shape[2]
    b_cols = lax.dynamic_slice_in_dim(b_ada, chip * ncol, ncol, axis=1)
    mod_g = _allgather8(_mod_shard(c_all, w_ada[0], b_cols), "gather_mod")
    mod_all = jnp.concatenate([mod_g[2 * k] for k in range(N_CHIPS)], axis=1)
    mod_mine = lax.dynamic_slice_in_dim(mod_all, dev * nbatch, nbatch, axis=0)
    mod8 = jnp.concatenate([mod_mine.reshape(nbatch, 6, D), jnp.zeros((nbatch, 2, D), F32)], axis=1)

    x2 = x.reshape(t, D)
    tgt2 = loss_target.reshape(t, D)
    bias128 = jnp.concatenate([fox_f_bias, jnp.zeros((1, 128 - BH), F32)], axis=1)

    proj = _proj(x2, mod8, w_p, seq, BF16, "proj")
    projf = _proj(x2, mod8, w_p[:, COL_BF:], seq, F32, "proj_forget")
    ya, ckpt = _hgrn_fwd(proj, lb_logits, hgrn_norm_w, nbatch, seq)
    cum_cols = _fox_cum(projf, bias128, nbatch, seq)
    yb, lse = _fox_fwd(proj, cum_cols, nbatch, seq)
    merged, u, x1 = _merge_fwd(ya, yb, proj, x2, mod8, wba, wbb, wout, ln1_w, ln1_b, seq)
    a_pre, b_pre, dz2, st2, dm2 = _ffn_fwd(x1, mod8, wg, wu, wd, tgt2, ln2_w, ln2_b, seq)
    loss = lax.psum(st2[2, 0], ("x", "y", "c"))

    da, db, hmid, dffn, du, dxp, st1, dm1 = _ffn_bwd(
        dz2, a_pre, b_pre, jnp.swapaxes(wg, 1, 2), jnp.swapaxes(wu, 1, 2), jnp.swapaxes(wd, 1, 2),
        x1, x2, u, mod8, ln1_w, seq)
    g_st = {}
    g_st["w_ffn_down"] = _tn_matmul(hmid, dffn, "dw_ffn_down", seq)
    g_st["w_ffn_gate"] = _tn_matmul(x1, da, "dw_ffn_gate", seq, mod8, (3, 4))
    g_st["w_ffn_up"] = _tn_matmul(x1, db, "dw_ffn_up", seq, mod8, (3, 4))
    g_st["w_out"] = _tn_matmul(merged, du, "dw_out", seq).reshape(N_CHIPS, D // N_CHIPS, D)
    dproj, dpa, dpb, dya, dyb = _merge_bwd(du, ya, yb, proj, wba, wbb, wout, seq)
    g_st["w_branch_a"] = _tn_matmul(ya, dpa, "dw_branch_a", seq, split=D // N_CHIPS)
    g_st["w_branch_b"] = _tn_matmul(yb, dpb, "dw_branch_b", seq, split=D // N_CHIPS)
    dproj, dq, drs, dcs = _fox_bwd(proj, cum_cols, lse, yb, dyb, dproj, nbatch, seq)
    dproj = _place_cols(dproj, dq, COL_BQ)
    dproj, sm_fox = _fox_dbf(projf, bias128, drs, dcs, dproj, nbatch, seq)
    dproj, sm_hgrn = _hgrn_bwd(proj, dya, ckpt, lb_logits, hgrn_norm_w, dproj, nbatch, seq)
    grad_x2, dm0 = _dh_kernel(dproj, w_p, x2, dxp, mod8, seq)
    dw_in = _unpermute_cols(_tn_matmul(x2, dproj, "dw_in", seq, mod8, (0, 1)))
    ncin = NIN // N_CHIPS
    g_st["w_in"] = jnp.stack([dw_in[:, k * ncin:(k + 1) * ncin] for k in range(N_CHIPS)])

    g_list = [g_st[n] for n in _BIG]
    halves = [_add_my_half(g, o, core_arr, "grad_add_halves_" + n)
              for n, g, o in zip(_BIG, g_list, _swap_halves(g_list))]
    recv = _scatter_chips([h16 for _, h16 in halves])
    g_mine = [_add_chips(h32, r, chip_arr, "grad_add_chips_" + n) for n, (h32, _), r in zip(_BIG, halves, recv)]
    g_sib = _join_halves(g_mine)

    dmod = (dm0 + dm1 + dm2)[:, :6, :].reshape(nbatch, 6 * D)
    row2 = jnp.concatenate([st1[0:1], st1[1:2], st2[0:1], st2[1:2], sm_hgrn[1:2], sm_hgrn[0:1], sm_hgrn[0:1],
                            sm_fox[0:1, :BH], jnp.zeros((1, SMALL_W - O_FOX - BH), F32)], axis=1)
    spack = jnp.concatenate([dmod, row2, jnp.zeros((8 - nbatch - 1, SMALL_W), F32)], axis=0)
    gath = _allgather8(spack, "gather_small")
    w8 = _pack_small(b_ada, ln1_w, ln1_b, ln2_w, ln2_b, hgrn_norm_w, lb_logits, fox_f_bias)
    m8 = _pack_small(m_b_ada, m_ln1_w, m_ln1_b, m_ln2_w, m_ln2_b, m_hgrn_norm_w, m_lb_logits, m_fox_f_bias)
    v8 = _pack_small(v_b_ada, v_ln1_w, v_ln1_b, v_ln2_w, v_ln2_b, v_hgrn_norm_w, v_lb_logits, v_fox_f_bias)
    sg, sd, smn, svn = (_unpack_small(p) for p in _small_update(gath, w8, m8, v8))
    dmod_all = gath[:, :nbatch, :].reshape(N_DEV * nbatch, SMALL_W)
    g_ada = _grad_w_ada(c_all, lax.dynamic_slice_in_dim(dmod_all, chip * ncol, ncol, axis=1))

    grads = dict(sg)
    deltas = dict(sd)
    new_m = dict(smn)
    new_v = dict(svn)
    grads["w_ada"] = g_ada
    deltas["w_ada"], new_m["w_ada"], new_v["w_ada"] = _adamw(w_ada[0], g_ada, m_w_ada[0], v_w_ada[0], "adamw_w_ada")
    for n, gm, gs in zip(_BIG, g_mine, g_sib):
        grads[n], deltas[n], new_m[n], new_v[n] = _adamw_halves(shard_w[n], gm, gs, shard_m[n], shard_v[n], core_arr,
                                                                "adamw_" + n)

    names = ["w_ada", "b_ada", "w_in", "fox_f_bias", "lb_logits", "hgrn_norm_w", "w_branch_a", "w_branch_b", "w_out",
             "ln1_w", "ln1_b", "w_ffn_gate", "w_ffn_up", "w_ffn_down", "ln2_w", "ln2_b"]
    shapes = dict(w_ada=w_ada.shape, b_ada=b_ada.shape, w_in=w_in.shape, fox_f_bias=fox_f_bias.shape,
                  lb_logits=lb_logits.shape, hgrn_norm_w=hgrn_norm_w.shape, w_branch_a=w_branch_a.shape,
                  w_branch_b=w_branch_b.shape, w_out=w_out.shape, ln1_w=ln1_w.shape, ln1_b=ln1_b.shape,
                  w_ffn_gate=w_ffn_gate.shape, w_ffn_up=w_ffn_up.shape, w_ffn_down=w_ffn_down.shape,
                  ln2_w=ln2_w.shape, ln2_b=ln2_b.shape)
    outs = [loss, grad_x2.reshape(x.shape)]
    for group in (grads, deltas, new_m, new_v):
        outs += [group[n].reshape(shapes[n]) for n in names]
    return tuple(outs)
```

```python
import functools
import math

import jax
import jax.numpy as jnp
import numpy as np
from jax import lax
from jax.experimental import pallas as pl
from jax.experimental.pallas import tpu as pltpu

F32 = jnp.float32
BF16 = jnp.bfloat16
MESH = pl.DeviceIdType.MESH
HIGHEST = lax.Precision.HIGHEST

D = 1024
AW = 512
AH = 4
ADH = 128
BH = 8
BDH = 64
DFF = 2816
NIN = 5640
NP = 5760
N_CHIPS = 4
N_DEV = 8
HGRN_BLOCK = 256
COL_GATES = 0
COL_BQ = 2048
COL_KV = 2560
COL_A = 3584
COL_BF = 5632
ALPHA = 2.0 ** 0.25
LN_EPS = 1e-5
RMS_EPS = 1e-6
NEG = -1e30
LOG2E = 1.4426950408889634
LR, B1, B2, EPS, WD, STEP = 0.001, 0.9, 0.999, 1e-08, 0.01, 10
SMALL_W = 6144
O_LN1W, O_LN1B, O_LN2W, O_LN2B, O_NORM, O_LB0, O_LB1, O_FOX = 0, 1024, 2048, 3072, 4096, 4608, 5120, 5632


def _params(sem=None, vmem_mb=None):
    kw = {}
    if sem is not None:
        kw["dimension_semantics"] = sem
    if vmem_mb is not None:
        kw["vmem_limit_bytes"] = vmem_mb << 20
    return pltpu.CompilerParams(**kw)


def _dot(a, b):
    return jnp.dot(a.astype(BF16), b.astype(BF16), preferred_element_type=F32)


def _dot_nt(a, b):
    return lax.dot_general(a.astype(BF16), b.astype(BF16), (((1,), (1,)), ((), ())), preferred_element_type=F32)


def _dot_tn(a, b):
    return lax.dot_general(a.astype(BF16), b.astype(BF16), (((0,), (0,)), ((), ())), preferred_element_type=F32)


def _dot_f32(a, b):
    return jnp.dot(a, b, preferred_element_type=F32, precision=HIGHEST)


def _perm_segments():
    segs = [(3592, 5640), (2048, 2560)]
    for p in range(4):
        segs += [(2560 + 128 * p, 2688 + 128 * p), (3072 + 128 * p, 3200 + 128 * p)]
    for h in range(4):
        segs += [(128 * h + 512 * t, 128 * h + 512 * t + 128) for t in range(4)]
    segs += [(3584, 3592)]
    return segs


def _permute_cols(w):
    parts = [w[:, a:b] for a, b in _perm_segments()]
    parts.append(jnp.zeros((w.shape[0], NP - NIN), w.dtype))
    return jnp.concatenate(parts, axis=1)


def _unpermute_cols(g):
    pos, where = 0, {}
    for a, b in _perm_segments():
        where[a] = (pos, pos + b - a)
        pos += b - a
    parts = [g[:, where[a][0]:where[a][1]] for a in sorted(where)]
    return jnp.concatenate(parts, axis=1)


def _allgather8(v, name):
    rows, cols = v.shape

    def body(x_ref, out_ref, send_sems, recv_sems, local_sem):
        x, y, c = lax.axis_index("x"), lax.axis_index("y"), lax.axis_index("c")
        me, sibling = (x, y, c), (x, y, 1 - c)
        chips = [(1 - x, y), (x, 1 - y), (1 - x, 1 - y)]

        def slot(px, py, pc):
            return out_ref.at[4 * px + 2 * py + pc]

        def copy(k, block, to, src=None):
            return pltpu.make_async_remote_copy(
                src_ref=slot(*block) if src is None else src, dst_ref=slot(*block),
                send_sem=send_sems.at[k], recv_sem=recv_sems.at[k], device_id=to, device_id_type=MESH)

        mine = pltpu.make_async_copy(x_ref, slot(*me), local_sem)
        mine.start()
        first = [copy(0, me, sibling, src=x_ref)]
        first += [copy(1 + j, me, (*chip, c), src=x_ref) for j, chip in enumerate(chips)]
        for cp in first:
            cp.start()
        passed = [copy(4 + j, (*chip, c), sibling) for j, chip in enumerate(chips)]
        for j, chip in enumerate(chips):
            copy(1 + j, (*chip, c), me).wait_recv()
            passed[j].start()
        copy(0, sibling, me).wait_recv()
        for j, chip in enumerate(chips):
            copy(4 + j, (*chip, 1 - c), me).wait_recv()
        for cp in first + passed:
            cp.wait_send()
        mine.wait()

    return pl.pallas_call(
        body, name=name,
        out_shape=jax.ShapeDtypeStruct((N_DEV, rows, cols), v.dtype),
        in_specs=[pl.BlockSpec(memory_space=pltpu.VMEM)],
        out_specs=pl.BlockSpec(memory_space=pltpu.VMEM),
        scratch_shapes=[pltpu.SemaphoreType.DMA((7,)), pltpu.SemaphoreType.DMA((7,)), pltpu.SemaphoreType.DMA],
    )(v)


def _hbm_specs(n):
    return [pl.BlockSpec(memory_space=pl.ANY)] * n


def _gather_weights(shards):
    n = len(shards)

    def body(*refs):
        ins, outs, (send_sems, recv_sems) = refs[:n], refs[n:2 * n], refs[2 * n:]
        x, y, c = lax.axis_index("x"), lax.axis_index("y"), lax.axis_index("c")
        sibling = (x, y, 1 - c)
        chips = [(1 - x, y), (x, 1 - y), (1 - x, 1 - y)]

        def blk(w, px, py, half):
            hr = ins[w].shape[0] // 2
            return outs[w].at[2 * px + py, pl.ds(half * hr, hr), :]

        def copy(w, k, block, to, src=None):
            return pltpu.make_async_remote_copy(
                src_ref=blk(w, *block) if src is None else src, dst_ref=blk(w, *block),
                send_sem=send_sems.at[6 * w + k], recv_sem=recv_sems.at[6 * w + k], device_id=to, device_id_type=MESH)

        first = []
        for w in range(n):
            hr = ins[w].shape[0] // 2
            my_half = ins[w].at[pl.ds(c * hr, hr), :]
            first += [copy(w, j, (x, y, c), (*chip, c), src=my_half) for j, chip in enumerate(chips)]
        for cp in first:
            cp.start()
        passed = []
        for j, chip in enumerate(chips):
            for w in range(n):
                copy(w, j, (*chip, c), (x, y, c)).wait_recv()
                passed.append(copy(w, 3 + j, (*chip, c), sibling))
                passed[-1].start()
        for j, chip in enumerate(chips):
            for w in range(n):
                copy(w, 3 + j, (*chip, 1 - c), (x, y, c)).wait_recv()
        for cp in first + passed:
            cp.wait_send()

    return pl.pallas_call(
        body, name="gather_weights",
        out_shape=[jax.ShapeDtypeStruct((N_CHIPS,) + s.shape, s.dtype) for s in shards],
        in_specs=_hbm_specs(n), out_specs=_hbm_specs(n),
        scratch_shapes=[pltpu.SemaphoreType.DMA((6 * n,)), pltpu.SemaphoreType.DMA((6 * n,))],
    )(*shards)


def _swap_halves(grads):
    n = len(grads)

    def body(*refs):
        ins, outs, (send_sems, recv_sems) = refs[:n], refs[n:2 * n], refs[2 * n:]
        x, y, c = lax.axis_index("x"), lax.axis_index("y"), lax.axis_index("c")
        cps = []
        for w in range(n):
            hr = ins[w].shape[1] // 2
            cps.append(pltpu.make_async_remote_copy(
                src_ref=ins[w].at[:, pl.ds((1 - c) * hr, hr), :], dst_ref=outs[w],
                send_sem=send_sems.at[w], recv_sem=recv_sems.at[w], device_id=(x, y, 1 - c), device_id_type=MESH))
        for cp in cps:
            cp.start()
        for cp in cps:
            cp.wait()

    return pl.pallas_call(
        body, name="grad_swap_halves",
        out_shape=[jax.ShapeDtypeStruct((N_CHIPS, g.shape[1] // 2, g.shape[2]), g.dtype) for g in grads],
        in_specs=_hbm_specs(n), out_specs=_hbm_specs(n),
        scratch_shapes=[pltpu.SemaphoreType.DMA((n,)), pltpu.SemaphoreType.DMA((n,))],
    )(*grads)


def _scatter_chips(reds):
    n = len(reds)

    def body(*refs):
        ins, outs, (send_sems, recv_sems) = refs[:n], refs[n:2 * n], refs[2 * n:]
        x, y, c = lax.axis_index("x"), lax.axis_index("y"), lax.axis_index("c")
        chips = [(1 - x, y), (x, 1 - y), (1 - x, 1 - y)]
        cps = [pltpu.make_async_remote_copy(
            src_ref=ins[w].at[2 * chip[0] + chip[1]], dst_ref=outs[w].at[j],
            send_sem=send_sems.at[3 * w + j], recv_sem=recv_sems.at[3 * w + j],
            device_id=(*chip, c), device_id_type=MESH)
            for j, chip in enumerate(chips) for w in range(n)]
        for cp in cps:
            cp.start()
        for cp in cps:
            cp.wait()

    return pl.pallas_call(
        body, name="grad_scatter_chips",
        out_shape=[jax.ShapeDtypeStruct((3,) + r.shape[1:], r.dtype) for r in reds],
        in_specs=_hbm_specs(n), out_specs=_hbm_specs(n),
        scratch_shapes=[pltpu.SemaphoreType.DMA((3 * n,)), pltpu.SemaphoreType.DMA((3 * n,))],
    )(*reds)


def _join_halves(halves):
    n = len(halves)

    def body(*refs):
        ins, outs, (send_sems, recv_sems) = refs[:n], refs[n:2 * n], refs[2 * n:]
        x, y, c = lax.axis_index("x"), lax.axis_index("y"), lax.axis_index("c")
        cps = [pltpu.make_async_remote_copy(
            src_ref=ins[w], dst_ref=outs[w], send_sem=send_sems.at[w], recv_sem=recv_sems.at[w],
            device_id=(x, y, 1 - c), device_id_type=MESH) for w in range(n)]
        for cp in cps:
            cp.start()
        for cp in cps:
            cp.wait()

    return pl.pallas_call(
        body, name="grad_join_halves",
        out_shape=[jax.ShapeDtypeStruct(h.shape, h.dtype) for h in halves],
        in_specs=_hbm_specs(n), out_specs=_hbm_specs(n),
        scratch_shapes=[pltpu.SemaphoreType.DMA((n,)), pltpu.SemaphoreType.DMA((n,))],
    )(*halves)


def _in_hbm(v):
    return pltpu.with_memory_space_constraint(v, pltpu.HBM)


_SPLIT_COPY = pltpu.CompilerParams(has_side_effects=pltpu.SideEffectType.DATAFLOW_SIDE_EFFECTING)


def _chip_copies(srcs, lands, send_sems, recv_sems):
    x, y, c = lax.axis_index("x"), lax.axis_index("y"), lax.axis_index("c")
    cps = []
    for w, (src, land) in enumerate(zip(srcs, lands)):
        hr = src.shape[0] // 2
        for j, chip in enumerate([(1 - x, y), (x, 1 - y), (1 - x, 1 - y)]):
            cps.append(pltpu.make_async_remote_copy(
                src_ref=src.at[pl.ds(c * hr, hr), :], dst_ref=land.at[2 * x + y, pl.ds(c * hr, hr), :],
                send_sem=send_sems.at[3 * w + j], recv_sem=recv_sems.at[3 * w + j],
                device_id=(*chip, c), device_id_type=MESH))
    return cps


def _gather_start(shards):
    n = len(shards)
    lands = [lax.empty((N_CHIPS,) + s.shape, s.dtype) for s in shards]

    def body(*refs):
        srcs, lnd, send_sems, recv_sems, token = refs[:n], refs[n:2 * n], refs[2 * n], refs[2 * n + 1], refs[-1]
        for cp in _chip_copies(srcs, lnd, send_sems, recv_sems):
            cp.start()
        token[...] = jnp.zeros_like(token)

    hbm = pl.BlockSpec(memory_space=pltpu.HBM)
    sem = pl.BlockSpec(memory_space=pltpu.SEMAPHORE)
    outs = pl.pallas_call(
        body, name="gather_late_start",
        out_shape=(pltpu.SemaphoreType.DMA((3 * n,)), pltpu.SemaphoreType.DMA((3 * n,)),
                   *[pltpu.HBM(v.shape, v.dtype) for v in shards + lands], jax.ShapeDtypeStruct((8, 128), F32)),
        in_specs=[hbm] * (2 * n),
        out_specs=(sem, sem, *([hbm] * (2 * n)), pl.BlockSpec(memory_space=pltpu.VMEM)),
        input_output_aliases={i: 2 + i for i in range(2 * n)},
        compiler_params=_SPLIT_COPY,
    )(*[_in_hbm(v) for v in shards + lands])
    return outs[0], outs[1], list(outs[2:2 + n]), list(outs[2 + n:2 + 2 * n]), outs[-1]


def _gather_wait(send_sems, recv_sems, shards, lands, after):
    n = len(shards)

    def body(*refs):
        srcs, lnd, send_sems, recv_sems = refs[:n], refs[n:2 * n], refs[2 * n], refs[2 * n + 1]
        for cp in _chip_copies(srcs, lnd, send_sems, recv_sems):
            cp.wait_send()
            cp.wait_recv()

    hbm = pl.BlockSpec(memory_space=pltpu.HBM)
    sem = pl.BlockSpec(memory_space=pltpu.SEMAPHORE)
    outs = pl.pallas_call(
        body, name="gather_late_wait",
        out_shape=tuple(pltpu.HBM(v.shape, v.dtype) for v in shards + lands),
        in_specs=[hbm] * (2 * n) + [sem, sem, pl.BlockSpec(memory_space=pl.ANY)],
        out_specs=tuple([hbm] * (2 * n)),
        input_output_aliases={i: i for i in range(2 * n)},
        compiler_params=_SPLIT_COPY,
    )(*shards, *lands, send_sems, recv_sems, after)
    return list(outs[n:])


def _pass_to_sibling(lands):
    n = len(lands)

    def body(*refs):
        ins, outs, (send_sems, recv_sems) = refs[:n], refs[n:2 * n], refs[2 * n:]
        x, y, c = lax.axis_index("x"), lax.axis_index("y"), lax.axis_index("c")
        cps = []
        for w in range(n):
            hr = ins[w].shape[1] // 2
            for j, chip in enumerate([(1 - x, y), (x, 1 - y), (1 - x, 1 - y)]):
                k = 2 * chip[0] + chip[1]
                cps.append(pltpu.make_async_remote_copy(
                    src_ref=ins[w].at[k, pl.ds(c * hr, hr), :], dst_ref=outs[w].at[k, pl.ds(c * hr, hr), :],
                    send_sem=send_sems.at[3 * w + j], recv_sem=recv_sems.at[3 * w + j],
                    device_id=(x, y, 1 - c), device_id_type=MESH))
        for cp in cps:
            cp.start()
        for cp in cps:
            cp.wait()

    return pl.pallas_call(
        body, name="gather_late_pass",
        out_shape=[jax.ShapeDtypeStruct(v.shape, v.dtype) for v in lands],
        in_specs=_hbm_specs(n), out_specs=_hbm_specs(n),
        input_output_aliases={i: i for i in range(n)},
        scratch_shapes=[pltpu.SemaphoreType.DMA((3 * n,)), pltpu.SemaphoreType.DMA((3 * n,))],
    )(*lands)


def _row_tile(rows):
    for cand in (256, 176, 128, 64, 32, 16):
        if rows % cand == 0:
            return cand
    raise ValueError(rows)


def _add_my_half(g, other, c_idx, name):
    _, k, n = g.shape
    hr = k // 2
    tr = _row_tile(hr)
    nb = hr // tr

    def body(c_ref, g_ref, o_ref, out_ref, out16_ref):
        s = g_ref[...] + o_ref[...]
        out_ref[...] = s
        out16_ref[...] = s.astype(BF16)

    return pl.pallas_call(
        body, name=name,
        grid_spec=pltpu.PrefetchScalarGridSpec(
            num_scalar_prefetch=1, grid=(N_CHIPS, nb),
            in_specs=[pl.BlockSpec((1, tr, n), lambda j, i, c: (j, c[0] * nb + i, 0)),
                      pl.BlockSpec((1, tr, n), lambda j, i, c: (j, i, 0))],
            out_specs=[pl.BlockSpec((1, tr, n), lambda j, i, c: (j, i, 0)),
                       pl.BlockSpec((1, tr, n), lambda j, i, c: (j, i, 0))]),
        out_shape=[jax.ShapeDtypeStruct((N_CHIPS, hr, n), F32), jax.ShapeDtypeStruct((N_CHIPS, hr, n), BF16)],
        compiler_params=_params(("parallel", "parallel")),
    )(c_idx, g, other)


def _add_chips(red, recv, chip_idx, name):
    _, hr, n = red.shape
    tr = _row_tile(hr)

    def body(k_ref, r_ref, v_ref, out_ref):
        out_ref[...] = ((r_ref[0] + v_ref[0].astype(F32)) + v_ref[1].astype(F32)) + v_ref[2].astype(F32)

    return pl.pallas_call(
        body, name=name,
        grid_spec=pltpu.PrefetchScalarGridSpec(
            num_scalar_prefetch=1, grid=(hr // tr,),
            in_specs=[pl.BlockSpec((1, tr, n), lambda i, k: (k[0], i, 0)),
                      pl.BlockSpec((3, tr, n), lambda i, k: (0, i, 0))],
            out_specs=pl.BlockSpec((tr, n), lambda i, k: (i, 0))),
        out_shape=jax.ShapeDtypeStruct((hr, n), F32),
        compiler_params=_params(("parallel",)),
    )(chip_idx, red, recv)


def _mod_shard(c_all, w_ada, b_ada):
    nb, cols = c_all.shape[0], w_ada.shape[1]

    def body(c_ref, w_ref, b_ref, o_ref):
        c = c_ref[...]
        o_ref[...] = _dot(c * jax.nn.sigmoid(c), w_ref[...]) + b_ref[...]

    return pl.pallas_call(
        body, name="mod_shard", out_shape=jax.ShapeDtypeStruct((nb, cols), F32),
        compiler_params=_params(vmem_mb=48),
    )(c_all, w_ada, b_ada)


def _proj(x2, mod8, w, seq, out_dtype, name):
    t = x2.shape[0]
    n = w.shape[1]
    tm, tn = min(512, seq), min(1152, n)
    tpb = seq // tm

    def body(x_ref, mod_ref, w_ref, o_ref, h_scr):
        @pl.when(pl.program_id(1) == 0)
        def _():
            h_scr[...] = (x_ref[...] * (1.0 + mod_ref[0, 1:2, :]) + mod_ref[0, 0:1, :]).astype(BF16)
        o_ref[...] = jnp.dot(h_scr[...], w_ref[...], preferred_element_type=F32).astype(o_ref.dtype)

    return pl.pallas_call(
        body, name=name, grid=(t // tm, n // tn),
        in_specs=[pl.BlockSpec((tm, D), lambda i, j: (i, 0)),
                  pl.BlockSpec((1, 8, D), lambda i, j: (i // tpb, 0, 0)),
                  pl.BlockSpec((D, tn), lambda i, j: (0, j))],
        out_specs=pl.BlockSpec((tm, tn), lambda i, j: (i, j)),
        out_shape=jax.ShapeDtypeStruct((t, n), out_dtype),
        scratch_shapes=[pltpu.VMEM((tm, D), BF16)],
        compiler_params=_params(("parallel", "arbitrary"), 48),
    )(x2, mod8, w)


def _tn_matmul(a, b, name, seq, mod8=None, rows=None, split=None):
    a_st, b_st = a.ndim == 3, b.ndim == 3
    t, ka = a.shape[-2:]
    n = b.shape[-1]
    tt = min(1024, seq)
    tpb = seq // tt
    nt = t // tt
    if a_st or b_st:
        steps, tn = (a.shape[0] if a_st else b.shape[0]), n
    else:
        tn = split
        if tn is None:
            tn = next(cand for cand in (1152, 1024, 1408, 512, n) if n % cand == 0)
        steps = n // tn
    stacked_out = a_st or b_st or split is not None

    def body(*refs):
        if mod8 is None:
            a_ref, b_ref, o_ref = refs
            av = a_ref[0] if a_st else a_ref[...]
        else:
            a_ref, m_ref, b_ref, o_ref = refs
            av = a_ref[...] * (1.0 + m_ref[0, rows[1]:rows[1] + 1, :]) + m_ref[0, rows[0]:rows[0] + 1, :]
        part = _dot_tn(av, b_ref[0] if b_st else b_ref[...])
        if stacked_out:
            part = part[None]

        @pl.when(pl.program_id(1) == 0)
        def _():
            o_ref[...] = part

        @pl.when(pl.program_id(1) > 0)
        def _():
            o_ref[...] += part

    if a_st:
        in_specs = [pl.BlockSpec((1, tt, ka), lambda j, k: (j, k, 0))]
    else:
        in_specs = [pl.BlockSpec((tt, ka), lambda j, k: (k, 0))]
    args = [a]
    if mod8 is not None:
        in_specs.append(pl.BlockSpec((1, 8, ka), lambda j, k: (k // tpb, 0, 0)))
        args.append(mod8)
    if b_st:
        in_specs.append(pl.BlockSpec((1, tt, n), lambda j, k: (j, k, 0)))
    else:
        in_specs.append(pl.BlockSpec((tt, tn), lambda j, k: (k, 0 if a_st else j)))
    args.append(b)
    if stacked_out:
        out_spec = pl.BlockSpec((1, ka, tn), lambda j, k: (j, 0, 0))
        out_shape = jax.ShapeDtypeStruct((steps, ka, tn), F32)
    else:
        out_spec = pl.BlockSpec((ka, tn), lambda j, k: (0, j))
        out_shape = jax.ShapeDtypeStruct((ka, n), F32)
    return pl.pallas_call(
        body, name=name, grid=(steps, nt), in_specs=in_specs, out_specs=out_spec, out_shape=out_shape,
        compiler_params=_params(("parallel", "arbitrary"), 56),
    )(*args)


def _dh_kernel(dproj, w_p, x2, dxp, mod8, seq):
    t = x2.shape[0]
    tm, tk = min(512, seq), 1152
    tpb = seq // tm
    nk = NP // tk
    nbatch = t // seq

    def body(dp_ref, w_ref, x_ref, dxp_ref, mod_ref, gx_ref, dm_ref, acc):
        i, k = pl.program_id(0), pl.program_id(1)

        @pl.when(k == 0)
        def _():
            acc[...] = jnp.zeros_like(acc)

        acc[...] += _dot_nt(dp_ref[...], w_ref[...])

        @pl.when(k == nk - 1)
        def _():
            dh = acc[...]
            gx_ref[...] = dxp_ref[...] + dh * (1.0 + mod_ref[0, 1:2, :])
            upd = jnp.concatenate(
                [jnp.sum(dh, axis=0, keepdims=True), jnp.sum(dh * x_ref[...], axis=0, keepdims=True),
                 jnp.zeros((6, D), F32)], axis=0)

            @pl.when(i % tpb == 0)
            def _():
                dm_ref[0] = upd

            @pl.when(i % tpb != 0)
            def _():
                dm_ref[0] += upd

    return pl.pallas_call(
        body, name="dh", grid=(t // tm, nk),
        in_specs=[pl.BlockSpec((tm, tk), lambda i, k: (i, k)),
                  pl.BlockSpec((D, tk), lambda i, k: (0, k)),
                  pl.BlockSpec((tm, D), lambda i, k: (i, 0)),
                  pl.BlockSpec((tm, D), lambda i, k: (i, 0)),
                  pl.BlockSpec((1, 8, D), lambda i, k: (i // tpb, 0, 0))],
        out_specs=[pl.BlockSpec((tm, D), lambda i, k: (i, 0)),
                   pl.BlockSpec((1, 8, D), lambda i, k: (i // tpb, 0, 0))],
        out_shape=[jax.ShapeDtypeStruct((t, D), F32), jax.ShapeDtypeStruct((nbatch, 8, D), F32)],
        scratch_shapes=[pltpu.VMEM((tm, D), F32)],
        compiler_params=_params(("arbitrary", "arbitrary"), 48),
    )(dproj, w_p, x2, dxp, mod8)


def _tri(n, upper):
    r = lax.broadcasted_iota(jnp.int32, (n, n), 0)
    c = lax.broadcasted_iota(jnp.int32, (n, n), 1)
    return jnp.where((c >= r) if upper else (c <= r), 1.0, 0.0).astype(F32)


@jax.custom_vjp
def _mm_nn(a, b):
    return _dot(a, b)


_mm_nn.defvjp(lambda a, b: (_dot(a, b), (a, b)),
              lambda res, g: (_dot_nt(g, res[1]), _dot_tn(res[0], g)))


@jax.custom_vjp
def _mm_nt(a, b):
    return _dot_nt(a, b)


_mm_nt.defvjp(lambda a, b: (_dot_nt(a, b), (a, b)),
              lambda res, g: (_dot(g, res[1]), _dot_tn(g, res[0])))


@jax.custom_vjp
def _mm_tn(a, b):
    return _dot_tn(a, b)


_mm_tn.defvjp(lambda a, b: (_dot_tn(a, b), (a, b)),
              lambda res, g: (_dot_nt(res[1], g), _dot(res[0], g)))


@jax.custom_vjp
def _cumsum_rows(x):
    return _dot_f32(_tri(x.shape[0], False), x)


_cumsum_rows.defvjp(lambda x: (_cumsum_rows(x), None),
                    lambda _, g: (_dot_f32(_tri(g.shape[0], True), g),))


@functools.partial(jax.custom_vjp, nondiff_argnums=(1,))
def _shift_rows(x, k):
    return pltpu.roll(x, k % x.shape[0], 0)


_shift_rows.defvjp(lambda x, k: (_shift_rows(x, k), None),
                   lambda k, _, g: (pltpu.roll(g, (-k) % g.shape[0], 0),))


def _group_ref(bc, m):
    n = bc.shape[0] // (2 * m)
    b3 = bc.reshape(n, 2 * m, ADH)
    row = lax.broadcasted_iota(jnp.int32, b3.shape, 1)
    ref = jnp.sum(jnp.where(row == m - 1, b3, 0.0), axis=1, keepdims=True)
    return jnp.broadcast_to(ref, b3.shape).reshape(bc.shape)


def _hgrn_block(q, fl, v, g, st, lb, nw):
    n = q.shape[0]
    f = lb + (1.0 - lb) * jax.nn.sigmoid(fl)
    kk = 1.0 - f
    lf = jnp.log(f)
    bc = _cumsum_rows(lf)
    row = lax.broadcasted_iota(jnp.int32, (n, ADH), 0)
    same = jnp.bitwise_xor(lax.broadcasted_iota(jnp.int32, (n, n), 0), lax.broadcasted_iota(jnp.int32, (n, n), 1))
    a = jnp.zeros((n, n), F32)
    m = 1
    while m < n:
        r = jnp.bitwise_and(row, 2 * m - 1)
        up, lo = r >= m, r < m
        if m == 1:
            aq, ak = lf, jnp.zeros_like(lf)
        elif m == 2:
            aq = jnp.where(r == 3, lf + _shift_rows(lf, 1), lf)
            ak = jnp.where(r == 0, _shift_rows(lf, -1), 0.0)
        else:
            ref = _group_ref(bc, m)
            aq, ak = bc - ref, ref - bc
        qt = jnp.where(up, q * jnp.exp(jnp.where(up, aq, 0.0)), 0.0)
        kt = jnp.where(lo, kk * jnp.exp(jnp.where(lo, ak, 0.0)), 0.0)
        a = a + jnp.where(same < 2 * m, _mm_nt(qt, kt), 0.0)
        m *= 2
    last = row == n - 1
    bl = jnp.sum(jnp.where(last, bc, 0.0), axis=0, keepdims=True)
    o = _mm_nn(a, v) + _mm_nt(q * jnp.exp(bc), st) + jnp.sum(q * kk, axis=-1, keepdims=True) * v
    st_new = st * jnp.exp(bl) + _mm_tn(v, kk * jnp.exp(bl - bc))
    rms = lax.rsqrt(jnp.mean(o * o, axis=-1, keepdims=True) + RMS_EPS)
    return o * rms * nw * jax.nn.sigmoid(g), st_new


def _hgrn_fwd(proj, lb_logits, norm_w, nbatch, seq):
    t = proj.shape[0]
    blk = min(HGRN_BLOCK, seq)
    nb = seq // blk

    def body(p_ref, lbl_ref, nw_ref, y_ref, ck_ref, st_s):
        @pl.when(pl.program_id(2) == 0)
        def _():
            st_s[...] = jnp.zeros_like(st_s)

        st = st_s[...]
        ck_ref[0] = st
        lb = jax.nn.sigmoid(lbl_ref[0:1, :] - lbl_ref[1:2, :])
        p = p_ref[...].astype(F32)
        y, st_new = _hgrn_block(p[:, 0:128], p[:, 128:256], p[:, 256:384], p[:, 384:512], st, lb, nw_ref[...])
        st_s[...] = st_new
        y_ref[...] = y.astype(y_ref.dtype)

    return pl.pallas_call(
        body, name="hgrn_fwd", grid=(AH, nbatch, nb),
        in_specs=[pl.BlockSpec((blk, 512), lambda h, b, i: (b * nb + i, COL_A // 512 + h)),
                  pl.BlockSpec((2, 128), lambda h, b, i: (0, h)),
                  pl.BlockSpec((1, 128), lambda h, b, i: (0, h))],
        out_specs=[pl.BlockSpec((blk, 128), lambda h, b, i: (b * nb + i, h)),
                   pl.BlockSpec((1, 128, 128), lambda h, b, i: ((h * nbatch + b) * nb + i, 0, 0))],
        out_shape=[jax.ShapeDtypeStruct((t, AW), BF16), jax.ShapeDtypeStruct((AH * nbatch * nb, 128, 128), F32)],
        scratch_shapes=[pltpu.VMEM((128, 128), F32)],
        compiler_params=_params(("parallel", "parallel", "arbitrary"), 48),
    )(proj, lb_logits, norm_w)


def _hgrn_bwd(proj, dya, ckpt, lb_logits, norm_w, dproj, nbatch, seq):
    t = proj.shape[0]
    blk = min(HGRN_BLOCK, seq)
    nb = seq // blk

    def body(p_ref, dy_ref, ck_ref, lbl_ref, nw_ref, dp_in, dp_ref, sm_ref, dst_s):
        del dp_in
        b_id, i = pl.program_id(1), pl.program_id(2)

        @pl.when(i == 0)
        def _():
            dst_s[...] = jnp.zeros_like(dst_s)

        lb = jax.nn.sigmoid(lbl_ref[0:1, :] - lbl_ref[1:2, :])
        p = p_ref[...].astype(F32)
        _, pullback = jax.vjp(_hgrn_block, p[:, 0:128], p[:, 128:256], p[:, 256:384], p[:, 384:512],
                              ck_ref[0], lb, nw_ref[...])
        dq, dfl, dv, dg, dst, dlb, dnw = pullback((dy_ref[...], dst_s[...]))
        dst_s[...] = dst
        dp_ref[:, 0:128] = dq.astype(dp_ref.dtype)
        dp_ref[:, 128:256] = dfl.astype(dp_ref.dtype)
        dp_ref[:, 256:384] = dv.astype(dp_ref.dtype)
        dp_ref[:, 384:512] = dg.astype(dp_ref.dtype)
        upd = jnp.concatenate([dlb, dnw, jnp.zeros((6, 128), F32)], axis=0)
        first = (b_id == 0) & (i == 0)

        @pl.when(first)
        def _():
            sm_ref[...] = upd

        @pl.when(jnp.logical_not(first))
        def _():
            sm_ref[...] += upd

    def rows(h, b, i):
        return b * nb + (nb - 1 - i)

    return pl.pallas_call(
        body, name="hgrn_bwd", grid=(AH, nbatch, nb),
        in_specs=[pl.BlockSpec((blk, 512), lambda h, b, i: (rows(h, b, i), COL_A // 512 + h)),
                  pl.BlockSpec((blk, 128), lambda h, b, i: (rows(h, b, i), h)),
                  pl.BlockSpec((1, 128, 128), lambda h, b, i: ((h * nbatch + b) * nb + (nb - 1 - i), 0, 0)),
                  pl.BlockSpec((2, 128), lambda h, b, i: (0, h)),
                  pl.BlockSpec((1, 128), lambda h, b, i: (0, h)),
                  pl.BlockSpec(memory_space=pl.ANY)],
        out_specs=[pl.BlockSpec((blk, 512), lambda h, b, i: (rows(h, b, i), COL_A // 512 + h)),
                   pl.BlockSpec((8, 128), lambda h, b, i: (0, h))],
        out_shape=[jax.ShapeDtypeStruct((t, NP), BF16), jax.ShapeDtypeStruct((8, AW), F32)],
        input_output_aliases={5: 0},
        scratch_shapes=[pltpu.VMEM((128, 128), F32)],
        compiler_params=_params(("parallel", "arbitrary", "arbitrary"), 48),
    )(proj, dya, ckpt, lb_logits, norm_w, dproj)


def _log_sigmoid(z):
    return jnp.minimum(z, 0.0) - jnp.log(1.0 + jnp.exp(-jnp.abs(z)))


def _fox_cum(proj, bias128, nbatch, seq):
    t = proj.shape[0]
    ts = min(512, seq)
    nb = seq // ts

    def body(p_ref, b_ref, c_ref, carry):
        @pl.when(pl.program_id(1) == 0)
        def _():
            carry[...] = jnp.zeros_like(carry)
        cum = _dot_f32(_tri(ts, False), _log_sigmoid(p_ref[...] + b_ref[...])) + carry[...]
        carry[...] = cum[ts - 1:ts, :]
        cum2 = cum * LOG2E
        lane = lax.broadcasted_iota(jnp.int32, (ts, 128), 1)
        for p in range(4):
            c_ref[p] = jnp.where(lane < 64, cum2[:, 2 * p:2 * p + 1], cum2[:, 2 * p + 1:2 * p + 2])

    return pl.pallas_call(
        body, name="fox_cum", grid=(nbatch, nb),
        in_specs=[pl.BlockSpec((ts, 128), lambda b, i: (b * nb + i, 0)),
                  pl.BlockSpec((1, 128), lambda b, i: (0, 0))],
        out_specs=pl.BlockSpec((4, ts, 128), lambda b, i: (0, b * nb + i, 0)),
        out_shape=jax.ShapeDtypeStruct((4, t, 128), F32),
        scratch_shapes=[pltpu.VMEM((1, 128), F32)],
        compiler_params=_params(("parallel", "arbitrary")),
    )(proj, bias128)


def _fox_scores_t(q_ref, kv_ref, cc_ref, hh, masked, tq, tk):
    kh = kv_ref[:, 64 * hh:64 * hh + 64].astype(BF16)
    qh = (q_ref[:, 64 * hh:64 * hh + 64] * (LOG2E * BDH ** -0.5)).astype(BF16)
    s = _dot_nt(kh, qh) - cc_ref[0, :, 64 * hh:64 * hh + 1]
    if masked:
        key = lax.broadcasted_iota(jnp.int32, (tk, tq), 0)
        qry = lax.broadcasted_iota(jnp.int32, (tk, tq), 1)
        s = jnp.where(key <= qry, s, NEG)
    return s, kh


def _causal_pairs(nq, key_major):
    if key_major:
        pairs = [(i, j) for j in range(nq) for i in range(j, nq)]
    else:
        pairs = [(i, j) for i in range(nq) for j in range(i + 1)]
    return (jnp.asarray([p[0] for p in pairs], jnp.int32), jnp.asarray([p[1] for p in pairs], jnp.int32))


def _with_ones_lane(x128, hh):
    lane = lax.broadcasted_iota(jnp.int32, x128.shape, 1)
    one = jnp.ones_like(x128)
    zero = jnp.zeros_like(x128)
    if hh == 0:
        return jnp.where(lane < 64, x128, jnp.where(lane == 64, one, zero))
    return jnp.where(lane >= 64, x128, jnp.where(lane == 0, one, zero))


def _fox_fwd(proj, cum_cols, nbatch, seq):
    t = proj.shape[0]
    tq = tk = min(512, seq)
    nq = seq // tq
    qi, kj = _causal_pairs(nq, key_major=False)

    def body(qi_ref, kj_ref, q_ref, kv_ref, cc_ref, o_ref, lse_ref, m_s, acc_s):
        s_id = pl.program_id(2)
        i, j = qi_ref[s_id], kj_ref[s_id]

        @pl.when(j == 0)
        def _():
            m_s[...] = jnp.full_like(m_s, NEG)
            acc_s[...] = jnp.zeros_like(acc_s)

        def step(masked):
            for hh in range(2):
                s, _ = _fox_scores_t(q_ref, kv_ref, cc_ref, hh, masked, tq, tk)
                m_prev = m_s[hh:hh + 1, :]
                m_new = jnp.maximum(m_prev, jnp.max(s, axis=0, keepdims=True))
                alpha = jnp.exp2(m_prev - m_new)
                p = jnp.exp2(s - m_new).astype(BF16)
                v_aug = _with_ones_lane(kv_ref[:, 128:256].astype(BF16), hh)
                acc_s[hh] = acc_s[hh] * alpha + _dot_tn(v_aug, p)
                m_s[hh:hh + 1, :] = m_new

        @pl.when(j < i)
        def _():
            step(False)

        @pl.when(j == i)
        def _():
            step(True)
            a0, a1 = acc_s[0], acc_s[1]
            l0, l1 = a0[64:65, :], a1[0:1, :]
            o_t = jnp.concatenate([a0[0:64, :] / l0, a1[64:128, :] / l1], axis=0)
            o_ref[...] = o_t.T.astype(o_ref.dtype)
            lse_ref[0, 0] = jnp.concatenate(
                [m_s[0:1, :] + jnp.log2(l0), m_s[1:2, :] + jnp.log2(l1), jnp.zeros((6, tq), F32)], axis=0)

    return pl.pallas_call(
        body, name="fox_fwd",
        grid_spec=pltpu.PrefetchScalarGridSpec(
            num_scalar_prefetch=2, grid=(nbatch, 4, qi.shape[0]),
            in_specs=[pl.BlockSpec((tq, 128), lambda b, p, s, qi, kj: (b * nq + qi[s], COL_BQ // 128 + p)),
                      pl.BlockSpec((tk, 256), lambda b, p, s, qi, kj: (b * nq + kj[s], COL_KV // 256 + p)),
                      pl.BlockSpec((1, tk, 128), lambda b, p, s, qi, kj: (p, b * nq + kj[s], 0))],
            out_specs=[pl.BlockSpec((tq, 128), lambda b, p, s, qi, kj: (b * nq + qi[s], p)),
                       pl.BlockSpec((1, 1, 8, tq), lambda b, p, s, qi, kj: (b, p, 0, qi[s]))],
            scratch_shapes=[pltpu.VMEM((8, tq), F32), pltpu.VMEM((2, 128, tq), F32)]),
        out_shape=[jax.ShapeDtypeStruct((t, 512), BF16), jax.ShapeDtypeStruct((nbatch, 4, 8, seq), F32)],
        compiler_params=_params(("parallel", "parallel", "arbitrary"), 48),
    )(qi, kj, proj, proj, cum_cols)


def _fox_bwd(proj, cum_cols, lse, yb, dyb, dproj, nbatch, seq):
    t = proj.shape[0]
    tq = tk = min(512, seq)
    nq = seq // tq
    scale = BDH ** -0.5
    qi, kj = _causal_pairs(nq, key_major=True)
    nsteps = qi.shape[0]

    def body(qi_ref, kj_ref, q_ref, kv_ref, cc_ref, lse_ref, o_ref, do_ref, dp_in,
             dkv_ref, dq_ref, drs_ref, dcs_ref, dkv_s, dqa_s, dcs_s):
        del dp_in
        hp, s_id = pl.program_id(1), pl.program_id(2)
        i, j = qi_ref[s_id], kj_ref[s_id]

        @pl.when(i == j)
        def _():
            dkv_s[...] = jnp.zeros_like(dkv_s)
            dcs_s[...] = jnp.zeros_like(dcs_s)

        @pl.when(s_id == 0)
        def _():
            dqa_s[...] = jnp.zeros_like(dqa_s)

        def step(masked):
            lane = lax.broadcasted_iota(jnp.int32, (tq, 128), 1)
            for hh in range(2):
                s, _ = _fox_scores_t(q_ref, kv_ref, cc_ref, hh, masked, tq, tk)
                p = jnp.exp2(s - lse_ref[0, 0, hh:hh + 1, :])
                doh = do_ref[:, 64 * hh:64 * hh + 64]
                dd = lax.dot_general(jnp.ones((8, 64), F32), doh * o_ref[:, 64 * hh:64 * hh + 64].astype(F32),
                                     (((1,), (1,)), ((), ())), preferred_element_type=F32, precision=HIGHEST)[0:1, :]
                doh = doh.astype(BF16)
                dp = _dot_nt(kv_ref[:, 128 + 64 * hh:192 + 64 * hh], doh)
                ds = (p * (dp - dd)).astype(BF16)
                dkv_s[:, 128 + 64 * hh:192 + 64 * hh] += _dot(p, doh)
                dkv_s[:, 64 * hh:64 * hh + 64] += _dot(ds, q_ref[:, 64 * hh:64 * hh + 64] * scale)
                k_aug = _with_ones_lane(kv_ref[:, 0:128].astype(BF16), hh)
                dqa_s[i, hh] += _dot_tn(k_aug, ds)
                sel = jnp.where(lane == 2 * hp + hh, 1.0, 0.0).astype(BF16)
                dcs_s[...] += _dot(ds, sel)

        @pl.when(i == j)
        def _():
            step(True)

        @pl.when(i > j)
        def _():
            step(False)

        @pl.when(i == nq - 1)
        def _():
            dkv_ref[...] = dkv_s[...].astype(dkv_ref.dtype)
            dcs_ref[0] = dcs_s[...]

        @pl.when(s_id == nsteps - 1)
        def _():
            lane = lax.broadcasted_iota(jnp.int32, (tq, 128), 1)
            for blk in range(nq):
                a0 = dqa_s[blk, 0].T
                a1 = dqa_s[blk, 1].T
                rows = pl.ds(blk * tq, tq)
                dq_ref[rows, :] = (jnp.where(lane < 64, a0, a1) * scale).astype(dq_ref.dtype)
                drs_ref[0, rows, :] = jnp.where(lane == 2 * hp, a0[:, 64:65], jnp.where(lane == 2 * hp + 1, a1[:, 0:1], 0.0))

    return pl.pallas_call(
        body, name="fox_bwd",
        grid_spec=pltpu.PrefetchScalarGridSpec(
            num_scalar_prefetch=2, grid=(nbatch, 4, nsteps),
            in_specs=[pl.BlockSpec((tq, 128), lambda b, p, s, qi, kj: (b * nq + qi[s], COL_BQ // 128 + p)),
                      pl.BlockSpec((tk, 256), lambda b, p, s, qi, kj: (b * nq + kj[s], COL_KV // 256 + p)),
                      pl.BlockSpec((1, tk, 128), lambda b, p, s, qi, kj: (p, b * nq + kj[s], 0)),
                      pl.BlockSpec((1, 1, 8, tq), lambda b, p, s, qi, kj: (b, p, 0, qi[s])),
                      pl.BlockSpec((tq, 128), lambda b, p, s, qi, kj: (b * nq + qi[s], p)),
                      pl.BlockSpec((tq, 128), lambda b, p, s, qi, kj: (b * nq + qi[s], p)),
                      pl.BlockSpec(memory_space=pl.ANY)],
            out_specs=[pl.BlockSpec((tk, 256), lambda b, p, s, qi, kj: (b * nq + kj[s], COL_KV // 256 + p)),
                       pl.BlockSpec((seq, 128), lambda b, p, s, qi, kj: (b, p)),
                       pl.BlockSpec((1, seq, 128), lambda b, p, s, qi, kj: (p, b, 0)),
                       pl.BlockSpec((1, tk, 128), lambda b, p, s, qi, kj: (p, b * nq + kj[s], 0))],
            scratch_shapes=[pltpu.VMEM((tk, 256), F32), pltpu.VMEM((nq, 2, 128, tq), F32),
                            pltpu.VMEM((tk, 128), F32)]),
        out_shape=[jax.ShapeDtypeStruct((t, NP), BF16), jax.ShapeDtypeStruct((t, 512), BF16),
                   jax.ShapeDtypeStruct((4, t, 128), F32), jax.ShapeDtypeStruct((4, t, 128), F32)],
        input_output_aliases={8: 0},
        compiler_params=_params(("parallel", "parallel", "arbitrary"), 56),
    )(qi, kj, proj, proj, cum_cols, lse, yb, dyb, dproj)


def _place_cols(dproj, src, col):
    t, w = src.shape
    tm = 1024 if t % 1024 == 0 else t

    def body(s_ref, dp_in, o_ref):
        del dp_in
        o_ref[...] = s_ref[...]

    return pl.pallas_call(
        body, name="place_cols", grid=(t // tm,),
        in_specs=[pl.BlockSpec((tm, w), lambda i: (i, 0)), pl.BlockSpec(memory_space=pl.ANY)],
        out_specs=pl.BlockSpec((tm, w), lambda i: (i, col // w)),
        out_shape=jax.ShapeDtypeStruct(dproj.shape, dproj.dtype),
        input_output_aliases={1: 0},
        compiler_params=_params(("parallel",)),
    )(src, dproj)


def _fox_dbf(proj, bias128, drs, dcs, dproj, nbatch, seq):
    t = proj.shape[0]
    ts = min(512, seq)
    nb = seq // ts

    def body(p_ref, b_ref, dr_ref, dc_ref, dp_in, dp_ref, sm_ref, carry):
        del dp_in
        b_id, i = pl.program_id(0), pl.program_id(1)

        @pl.when(i == 0)
        def _():
            carry[...] = jnp.zeros_like(carry)

        dcum = (dr_ref[0] - dc_ref[0]) + (dr_ref[1] - dc_ref[1]) + (dr_ref[2] - dc_ref[2]) + (dr_ref[3] - dc_ref[3])
        rc = _dot_f32(_tri(ts, True), dcum) + carry[...]
        carry[...] = rc[0:1, :]
        z = p_ref[...] + b_ref[...]
        lane = lax.broadcasted_iota(jnp.int32, (ts, 128), 1)
        dz = jnp.where(lane < BH, rc * jax.nn.sigmoid(-z), 0.0)
        dp_ref[...] = dz.astype(dp_ref.dtype)
        upd = jnp.concatenate([jnp.sum(dz, axis=0, keepdims=True), jnp.zeros((7, 128), F32)], axis=0)
        first = (b_id == 0) & (i == 0)

        @pl.when(first)
        def _():
            sm_ref[...] = upd

        @pl.when(jnp.logical_not(first))
        def _():
            sm_ref[...] += upd

    def rows(b, i):
        return b * nb + (nb - 1 - i)

    return pl.pallas_call(
        body, name="fox_dbf", grid=(nbatch, nb),
        in_specs=[pl.BlockSpec((ts, 128), lambda b, i: (rows(b, i), 0)),
                  pl.BlockSpec((1, 128), lambda b, i: (0, 0)),
                  pl.BlockSpec((4, ts, 128), lambda b, i: (0, rows(b, i), 0)),
                  pl.BlockSpec((4, ts, 128), lambda b, i: (0, rows(b, i), 0)),
                  pl.BlockSpec(memory_space=pl.ANY)],
        out_specs=[pl.BlockSpec((ts, 128), lambda b, i: (rows(b, i), COL_BF // 128)),
                   pl.BlockSpec((8, 128), lambda b, i: (0, 0))],
        out_shape=[jax.ShapeDtypeStruct((t, NP), BF16), jax.ShapeDtypeStruct((8, 128), F32)],
        input_output_aliases={4: 0},
        scratch_shapes=[pltpu.VMEM((1, 128), F32)],
        compiler_params=_params(("arbitrary", "arbitrary")),
    )(proj, bias128, drs, dcs, dproj)


def _ln_stats(z):
    mu = jnp.mean(z, axis=-1, keepdims=True)
    zc = z - mu
    rstd = lax.rsqrt(jnp.mean(zc * zc, axis=-1, keepdims=True) + LN_EPS)
    return zc * rstd, rstd


def _ln_bwd(dy, xhat, rstd, w):
    dxh = dy * w
    return rstd * (dxh - jnp.mean(dxh, axis=-1, keepdims=True) - xhat * jnp.mean(dxh * xhat, axis=-1, keepdims=True))


def _merge_fwd(ya, yb, proj, x2, mod8, wba, wbb, wout, ln1w, ln1b, seq):
    t = x2.shape[0]
    tm = min(256, seq)
    tpb = seq // tm

    def body(ya_ref, yb_ref, g_ref, x_ref, mod_ref, wa_ref, wb_ref, wo_ref, lw_ref, lb_ref, mg_ref, u_ref, x1_ref):
        ga = jax.nn.sigmoid(g_ref[:, 0:D].astype(F32))
        gb = jax.nn.sigmoid(g_ref[:, D:2 * D].astype(F32))
        merged = (ga * jnp.dot(ya_ref[...], wa_ref[...], preferred_element_type=F32)
                  + gb * jnp.dot(yb_ref[...], wb_ref[...], preferred_element_type=F32))
        mg = merged.astype(BF16)
        mg_ref[...] = mg
        u = jnp.dot(mg, wo_ref[...], preferred_element_type=F32)
        u_ref[...] = u
        xhat, _ = _ln_stats(ALPHA * x_ref[...] + (1.0 + mod_ref[0, 2:3, :]) * u)
        x1_ref[...] = xhat * lw_ref[...] + lb_ref[...]

    tok = lambda w: pl.BlockSpec((tm, w), lambda i: (i, 0))
    full = lambda a: pl.BlockSpec(a.shape, lambda i: (0,) * a.ndim)
    return pl.pallas_call(
        body, name="merge_fwd", grid=(t // tm,),
        in_specs=[tok(512), tok(512), pl.BlockSpec((tm, 2048), lambda i: (i, COL_GATES // 2048)), tok(D),
                  pl.BlockSpec((1, 8, D), lambda i: (i // tpb, 0, 0)),
                  full(wba), full(wbb), full(wout), full(ln1w), full(ln1b)],
        out_specs=[tok(D), tok(D), tok(D)],
        out_shape=[jax.ShapeDtypeStruct((t, D), BF16), jax.ShapeDtypeStruct((t, D), F32),
                   jax.ShapeDtypeStruct((t, D), F32)],
        compiler_params=_params(("parallel",), 48),
    )(ya, yb, proj, x2, mod8, wba, wbb, wout, ln1w, ln1b)


def _merge_bwd(du, ya, yb, proj, wba, wbb, wout, seq):
    t = du.shape[0]
    tm = min(256, seq)

    def body(du_ref, ya_ref, yb_ref, g_ref, wa_ref, wb_ref, wo_ref, dp_ref, dpa_ref, dpb_ref, dya_ref, dyb_ref):
        ga = jax.nn.sigmoid(g_ref[:, 0:D].astype(F32))
        gb = jax.nn.sigmoid(g_ref[:, D:2 * D].astype(F32))
        dm = _dot_nt(du_ref[...], wo_ref[...])
        pa = jnp.dot(ya_ref[...], wa_ref[...], preferred_element_type=F32)
        pb = jnp.dot(yb_ref[...], wb_ref[...], preferred_element_type=F32)
        dpa = (dm * ga).astype(BF16)
        dpb = (dm * gb).astype(BF16)
        dpa_ref[...] = dpa
        dpb_ref[...] = dpb
        dp_ref[:, 0:D] = (dm * pa * ga * (1.0 - ga)).astype(BF16)
        dp_ref[:, D:2 * D] = (dm * pb * gb * (1.0 - gb)).astype(BF16)
        dya_ref[...] = _dot_nt(dpa, wa_ref[...])
        dyb_ref[...] = _dot_nt(dpb, wb_ref[...])

    tok = lambda w: pl.BlockSpec((tm, w), lambda i: (i, 0))
    full = lambda a: pl.BlockSpec(a.shape, lambda i: (0,) * a.ndim)
    return pl.pallas_call(
        body, name="merge_bwd", grid=(t // tm,),
        in_specs=[tok(D), tok(512), tok(512), pl.BlockSpec((tm, 2048), lambda i: (i, COL_GATES // 2048)),
                  full(wba), full(wbb), full(wout)],
        out_specs=[pl.BlockSpec((tm, 2048), lambda i: (i, COL_GATES // 2048)), tok(D), tok(D), tok(512), tok(512)],
        out_shape=[jax.ShapeDtypeStruct((t, NP), BF16), jax.ShapeDtypeStruct((t, D), BF16),
                   jax.ShapeDtypeStruct((t, D), BF16), jax.ShapeDtypeStruct((t, 512), F32),
                   jax.ShapeDtypeStruct((t, 512), F32)],
        compiler_params=_params(("parallel",), 48),
    )(du, ya, yb, proj, wba, wbb, wout)


def _ffn_fwd(x1, mod8, wg, wu, wd, target, ln2w, ln2b, seq):
    t = x1.shape[0]
    tm = min(512, seq)
    nf, _, tf = wg.shape
    tpb = seq // tm
    nbatch = t // seq

    def body(x_ref, mod_ref, wg_ref, wu_ref, wd_ref, t_ref, lw_ref, lb_ref,
             a_ref, b_ref, dz_ref, st_ref, dm_ref, h_s, acc):
        i, j = pl.program_id(0), pl.program_id(1)

        @pl.when(j == 0)
        def _():
            h_s[...] = (x_ref[...] * (1.0 + mod_ref[0, 4:5, :]) + mod_ref[0, 3:4, :]).astype(BF16)
            acc[...] = jnp.zeros_like(acc)

        a = jnp.dot(h_s[...], wg_ref[0], preferred_element_type=F32)
        b = jnp.dot(h_s[...], wu_ref[0], preferred_element_type=F32)
        a_ref[0] = a.astype(BF16)
        b_ref[0] = b.astype(BF16)
        acc[...] += _dot(a * jax.nn.sigmoid(a) * b, wd_ref[0])

        @pl.when(j == nf - 1)
        def _():
            ffn = acc[...]
            xhat, rstd = _ln_stats(ALPHA * x_ref[...] + (1.0 + mod_ref[0, 5:6, :]) * ffn)
            diff = xhat * lw_ref[...] + lb_ref[...] - t_ref[...]
            loss = 0.5 * jnp.sum(jnp.sum(diff * diff, axis=-1, keepdims=True), axis=0, keepdims=True) / D
            dy = diff * (1.0 / D)
            dz = _ln_bwd(dy, xhat, rstd, lw_ref[...])
            dz_ref[...] = dz
            lane = lax.broadcasted_iota(jnp.int32, (1, D), 1)
            upd = jnp.concatenate(
                [jnp.sum(dy * xhat, axis=0, keepdims=True), jnp.sum(dy, axis=0, keepdims=True),
                 jnp.where(lane == 0, loss, 0.0), jnp.zeros((5, D), F32)], axis=0)
            dmu = jnp.concatenate(
                [jnp.zeros((5, D), F32), jnp.sum(dz * ffn, axis=0, keepdims=True), jnp.zeros((2, D), F32)], axis=0)

            @pl.when(i == 0)
            def _():
                st_ref[...] = upd

            @pl.when(i > 0)
            def _():
                st_ref[...] += upd

            @pl.when(i % tpb == 0)
            def _():
                dm_ref[0] = dmu

            @pl.when(i % tpb != 0)
            def _():
                dm_ref[0] += dmu

    row = lambda: pl.BlockSpec((tm, D), lambda i, j: (i, 0))
    vec = lambda: pl.BlockSpec((1, D), lambda i, j: (0, 0))
    return pl.pallas_call(
        body, name="ffn_fwd", grid=(t // tm, nf),
        in_specs=[row(), pl.BlockSpec((1, 8, D), lambda i, j: (i // tpb, 0, 0)),
                  pl.BlockSpec((1, D, tf), lambda i, j: (j, 0, 0)), pl.BlockSpec((1, D, tf), lambda i, j: (j, 0, 0)),
                  pl.BlockSpec((1, tf, D), lambda i, j: (j, 0, 0)), row(), vec(), vec()],
        out_specs=[pl.BlockSpec((1, tm, tf), lambda i, j: (j, i, 0)), pl.BlockSpec((1, tm, tf), lambda i, j: (j, i, 0)),
                   row(), pl.BlockSpec((8, D), lambda i, j: (0, 0)),
                   pl.BlockSpec((1, 8, D), lambda i, j: (i // tpb, 0, 0))],
        out_shape=[jax.ShapeDtypeStruct((nf, t, tf), BF16), jax.ShapeDtypeStruct((nf, t, tf), BF16),
                   jax.ShapeDtypeStruct((t, D), F32), jax.ShapeDtypeStruct((8, D), F32),
                   jax.ShapeDtypeStruct((nbatch, 8, D), F32)],
        scratch_shapes=[pltpu.VMEM((tm, D), BF16), pltpu.VMEM((tm, D), F32)],
        compiler_params=_params(("arbitrary", "arbitrary"), 48),
    )(x1, mod8, wg, wu, wd, target, ln2w, ln2b)


def _ffn_bwd(dz2, a, b, wg, wu, wd, x1, x2, u, mod8, ln1w, seq):
    t = x1.shape[0]
    tm = min(512, seq)
    nf, tf, _ = wg.shape
    tpb = seq // tm
    nbatch = t // seq

    def body(dz_ref, a_ref, b_ref, wg_ref, wu_ref, wd_ref, x1_ref, x_ref, u_ref, mod_ref, lw_ref,
             da_ref, db_ref, hm_ref, df_ref, du_ref, dxp_ref, st_ref, dm_ref, acc):
        i, j = pl.program_id(0), pl.program_id(1)

        @pl.when(j == 0)
        def _():
            df_ref[...] = ((1.0 + mod_ref[0, 5:6, :]) * dz_ref[...]).astype(BF16)
            acc[...] = jnp.zeros_like(acc)

        dhm = _dot(df_ref[...], wd_ref[0])
        av = a_ref[0].astype(F32)
        bv = b_ref[0].astype(F32)
        sg = jax.nn.sigmoid(av)
        sl = av * sg
        hm_ref[0] = (sl * bv).astype(BF16)
        da = (dhm * bv * (sg * (1.0 + av * (1.0 - sg)))).astype(BF16)
        db = (dhm * sl).astype(BF16)
        da_ref[0] = da
        db_ref[0] = db
        acc[...] += _dot(da, wg_ref[0]) + _dot(db, wu_ref[0])

        @pl.when(j == nf - 1)
        def _():
            dh2 = acc[...]
            x1v = x1_ref[...]
            uv = u_ref[...]
            dx1 = ALPHA * dz_ref[...] + dh2 * (1.0 + mod_ref[0, 4:5, :])
            xhat, rstd = _ln_stats(ALPHA * x_ref[...] + (1.0 + mod_ref[0, 2:3, :]) * uv)
            dz1 = _ln_bwd(dx1, xhat, rstd, lw_ref[...])
            du_ref[...] = ((1.0 + mod_ref[0, 2:3, :]) * dz1).astype(BF16)
            dxp_ref[...] = ALPHA * dz1
            upd = jnp.concatenate(
                [jnp.sum(dx1 * xhat, axis=0, keepdims=True), jnp.sum(dx1, axis=0, keepdims=True),
                 jnp.zeros((6, D), F32)], axis=0)
            dmu = jnp.concatenate(
                [jnp.zeros((2, D), F32), jnp.sum(dz1 * uv, axis=0, keepdims=True),
                 jnp.sum(dh2, axis=0, keepdims=True), jnp.sum(dh2 * x1v, axis=0, keepdims=True),
                 jnp.zeros((3, D), F32)], axis=0)

            @pl.when(i == 0)
            def _():
                st_ref[...] = upd

            @pl.when(i > 0)
            def _():
                st_ref[...] += upd

            @pl.when(i % tpb == 0)
            def _():
                dm_ref[0] = dmu

            @pl.when(i % tpb != 0)
            def _():
                dm_ref[0] += dmu

    row = lambda: pl.BlockSpec((tm, D), lambda i, j: (i, 0))
    ffb = lambda: pl.BlockSpec((1, tm, tf), lambda i, j: (j, i, 0))
    return pl.pallas_call(
        body, name="ffn_bwd", grid=(t // tm, nf),
        in_specs=[row(), ffb(), ffb(),
                  pl.BlockSpec((1, tf, D), lambda i, j: (j, 0, 0)), pl.BlockSpec((1, tf, D), lambda i, j: (j, 0, 0)),
                  pl.BlockSpec((1, D, tf), lambda i, j: (j, 0, 0)), row(), row(), row(),
                  pl.BlockSpec((1, 8, D), lambda i, j: (i // tpb, 0, 0)), pl.BlockSpec((1, D), lambda i, j: (0, 0))],
        out_specs=[ffb(), ffb(), ffb(), row(), row(), row(), pl.BlockSpec((8, D), lambda i, j: (0, 0)),
                   pl.BlockSpec((1, 8, D), lambda i, j: (i // tpb, 0, 0))],
        out_shape=[jax.ShapeDtypeStruct((nf, t, tf), BF16), jax.ShapeDtypeStruct((nf, t, tf), BF16),
                   jax.ShapeDtypeStruct((nf, t, tf), BF16), jax.ShapeDtypeStruct((t, D), BF16),
                   jax.ShapeDtypeStruct((t, D), BF16), jax.ShapeDtypeStruct((t, D), F32),
                   jax.ShapeDtypeStruct((8, D), F32), jax.ShapeDtypeStruct((nbatch, 8, D), F32)],
        scratch_shapes=[pltpu.VMEM((tm, D), F32)],
        compiler_params=_params(("arbitrary", "arbitrary"), 48),
    )(dz2, a, b, wg, wu, wd, x1, x2, u, mod8, ln1w)


def _adamw_math(w, g, m, v):
    m = B1 * m + (1.0 - B1) * g
    v = B2 * v + (1.0 - B2) * (g * g)
    m_hat = m / (1.0 - B1 ** STEP)
    v_hat = v / (1.0 - B2 ** STEP)
    return -LR * (m_hat / (jnp.sqrt(v_hat) + EPS) + WD * w), m, v


def _adamw(w, g, m, v, name):
    rows, cols = w.shape
    tr = rows
    for cand in (128, 64, 32, 16, 8):
        if rows % cand == 0:
            tr = cand
            break

    def body(w_ref, g_ref, m_ref, v_ref, d_ref, mo_ref, vo_ref):
        d, mn, vn = _adamw_math(w_ref[...], g_ref[...], m_ref[...], v_ref[...])
        d_ref[...] = d
        mo_ref[...] = mn
        vo_ref[...] = vn

    spec = pl.BlockSpec((tr, cols), lambda i: (i, 0))
    return pl.pallas_call(
        body, name=name, grid=(rows // tr,), in_specs=[spec] * 4, out_specs=[spec] * 3,
        out_shape=[jax.ShapeDtypeStruct((rows, cols), F32)] * 3,
        compiler_params=_params(("parallel",), 48),
    )(w, g, m, v)


def _adamw_halves(w, g_mine, g_sib, m, v, c_idx, name):
    rows, cols = w.shape
    hr = rows // 2
    tr = next(cand for cand in (128, 88, 64, 32, 16, 8) if hr % cand == 0)
    tph = hr // tr

    def body(c_ref, w_ref, gm_ref, gs_ref, m_ref, v_ref, g_ref, d_ref, mo_ref, vo_ref):
        g = jnp.where(pl.program_id(0) == c_ref[0], gm_ref[...], gs_ref[...])
        d, mn, vn = _adamw_math(w_ref[...], g, m_ref[...], v_ref[...])
        g_ref[...] = g
        d_ref[...] = d
        mo_ref[...] = mn
        vo_ref[...] = vn

    full = pl.BlockSpec((tr, cols), lambda h, i, c: (h * tph + i, 0))
    half = pl.BlockSpec((tr, cols), lambda h, i, c: (i, 0))
    return pl.pallas_call(
        body, name=name,
        grid_spec=pltpu.PrefetchScalarGridSpec(
            num_scalar_prefetch=1, grid=(2, tph), in_specs=[full, half, half, full, full], out_specs=[full] * 4),
        out_shape=[jax.ShapeDtypeStruct((rows, cols), F32)] * 4,
        compiler_params=_params(("parallel", "parallel"), 48),
    )(c_idx, w, g_mine, g_sib, m, v)


def _grad_w_ada(c_all, dmod_cols):
    def body(c_ref, d_ref, o_ref):
        c = c_ref[...]
        o_ref[...] = lax.dot_general(c * jax.nn.sigmoid(c), d_ref[...], (((0,), (0,)), ((), ())),
                                     preferred_element_type=F32, precision=HIGHEST)

    return pl.pallas_call(
        body, name="grad_w_ada", out_shape=jax.ShapeDtypeStruct((D, dmod_cols.shape[1]), F32),
        compiler_params=_params(vmem_mb=48),
    )(c_all, dmod_cols)


def _small_update(gath, w8, m8, v8):
    def body(g_ref, w_ref, m_ref, v_ref, go_ref, d_ref, mo_ref, vo_ref):
        g0 = g_ref[0, 0:1, :] + g_ref[0, 1:2, :]
        g1 = g_ref[0, 2:3, :]
        for dev in range(1, N_DEV):
            g0 = g0 + (g_ref[dev, 0:1, :] + g_ref[dev, 1:2, :])
            g1 = g1 + g_ref[dev, 2:3, :]
        w = w_ref[...]
        lb = jax.nn.sigmoid(w[1:2, O_LB0:O_LB1] - w[1:2, O_LB1:O_FOX])
        fac = lb * (1.0 - lb)
        g1 = jnp.concatenate([g1[:, :O_LB0], g1[:, O_LB0:O_LB1] * fac, -g1[:, O_LB1:O_FOX] * fac, g1[:, O_FOX:]],
                             axis=1)
        g = jnp.concatenate([g0, g1, jnp.zeros((6, SMALL_W), F32)], axis=0)
        d, mn, vn = _adamw_math(w, g, m_ref[...], v_ref[...])
        go_ref[...] = g
        d_ref[...] = d
        mo_ref[...] = mn
        vo_ref[...] = vn

    return pl.pallas_call(
        body, name="small_update", out_shape=[jax.ShapeDtypeStruct((8, SMALL_W), F32)] * 4,
        compiler_params=_params(vmem_mb=48),
    )(gath, w8, m8, v8)


def _pack_small(b_ada, ln1w, ln1b, ln2w, ln2b, norm_w, lb_logits, fox):
    row1 = jnp.concatenate([ln1w, ln1b, ln2w, ln2b, norm_w, lb_logits[0:1], lb_logits[1:2], fox,
                            jnp.zeros((1, SMALL_W - O_FOX - BH), F32)], axis=1)
    return jnp.concatenate([b_ada, row1, jnp.zeros((6, SMALL_W), F32)], axis=0)


def _unpack_small(p):
    r = p[1:2]
    lb = jnp.concatenate([r[:, O_LB0:O_LB1], r[:, O_LB1:O_FOX]], axis=0)
    return dict(b_ada=p[0:1], ln1_w=r[:, O_LN1W:O_LN1B], ln1_b=r[:, O_LN1B:O_LN2W], ln2_w=r[:, O_LN2W:O_LN2B],
                ln2_b=r[:, O_LN2B:O_NORM], hgrn_norm_w=r[:, O_NORM:O_LB0], lb_logits=lb,
                fox_f_bias=r[:, O_FOX:O_FOX + BH])


_BIG = ("w_in", "w_branch_a", "w_branch_b", "w_out", "w_ffn_gate", "w_ffn_up", "w_ffn_down")


def _cols_of_chips(stacked):
    return jnp.concatenate([stacked[k] for k in range(N_CHIPS)], axis=1)


def kernel(x, c, w_ada, b_ada, w_in, fox_f_bias, lb_logits, hgrn_norm_w, w_branch_a, w_branch_b, w_out, ln1_w, ln1_b, w_ffn_gate, w_ffn_up, w_ffn_down, ln2_w, ln2_b, loss_target, m_w_ada, m_b_ada, m_w_in, m_fox_f_bias, m_lb_logits, m_hgrn_norm_w, m_w_branch_a, m_w_branch_b, m_w_out, m_ln1_w, m_ln1_b, m_w_ffn_gate, m_w_ffn_up, m_w_ffn_down, m_ln2_w, m_ln2_b, v_w_ada, v_b_ada, v_w_in, v_fox_f_bias, v_lb_logits, v_hgrn_norm_w, v_w_branch_a, v_w_branch_b, v_w_out, v_ln1_w, v_ln1_b, v_w_ffn_gate, v_w_ffn_up, v_w_ffn_down, v_ln2_w, v_ln2_b):
    nbatch, seq, _ = x.shape
    t = nbatch * seq
    ax, ay, ac = lax.axis_index("x"), lax.axis_index("y"), lax.axis_index("c")
    chip = 2 * ax + ay
    dev = 2 * chip + ac
    chip_arr = jnp.reshape(chip, (1,)).astype(jnp.int32)
    core_arr = jnp.reshape(ac, (1,)).astype(jnp.int32)

    shard_w = dict(w_in=w_in[0], w_branch_a=w_branch_a[0], w_branch_b=w_branch_b[0], w_out=w_out[0],
                   w_ffn_gate=w_ffn_gate[0], w_ffn_up=w_ffn_up[0], w_ffn_down=w_ffn_down[0])
    shard_m = dict(w_in=m_w_in[0], w_branch_a=m_w_branch_a[0], w_branch_b=m_w_branch_b[0], w_out=m_w_out[0],
                   w_ffn_gate=m_w_ffn_gate[0], w_ffn_up=m_w_ffn_up[0], w_ffn_down=m_w_ffn_down[0])
    shard_v = dict(w_in=v_w_in[0], w_branch_a=v_w_branch_a[0], w_branch_b=v_w_branch_b[0], w_out=v_w_out[0],
                   w_ffn_gate=v_w_ffn_gate[0], w_ffn_up=v_w_ffn_up[0], w_ffn_down=v_w_ffn_down[0])

    shard16 = {n: shard_w[n].astype(BF16) for n in _BIG}

    def with_mine(gathered, n):
        return lax.dynamic_update_slice(gathered, shard16[n][None], (chip, 0, 0))

    w_p = _permute_cols(_cols_of_chips(with_mine(_gather_weights([shard16["w_in"]])[0], "w_in")))
    late = _BIG[1:]
    late_send, late_recv, late_src, late_land, late_token = _gather_start([shard16[n] for n in late])

    c8 = jnp.concatenate([c, jnp.zeros((8 - nbatch, D), F32)], axis=0)
    c_all = _allgather8(c8, "gather_c")[:, :nbatch, :].reshape(N_DEV * nbatch, D)
    ncol = w_ada.shape[2]
    b_cols = lax.dynamic_slice_in_dim(b_ada, chip * ncol, ncol, axis=1)
    mod_g = _allgather8(_mod_shard(c_all, w_ada[0], b_cols), "gather_mod")
    mod_all = jnp.concatenate([mod_g[2 * k] for k in range(N_CHIPS)], axis=1)
    mod_mine = lax.dynamic_slice_in_dim(mod_all, dev * nbatch, nbatch, axis=0)
    mod8 = jnp.concatenate([mod_mine.reshape(nbatch, 6, D), jnp.zeros((nbatch, 2, D), F32)], axis=1)
    mod8 = mod8 + late_token[0, 0]

    x2 = x.reshape(t, D)
    tgt2 = loss_target.reshape(t, D)
    bias128 = jnp.concatenate([fox_f_bias, jnp.zeros((1, 128 - BH), F32)], axis=1)

    proj = _proj(x2, mod8, w_p, seq, BF16, "proj")
    projf = _proj(x2, mod8, w_p[:, COL_BF:], seq, F32, "proj_forget")
    ya, ckpt = _hgrn_fwd(proj, lb_logits, hgrn_norm_w, nbatch, seq)
    cum_cols = _fox_cum(projf, bias128, nbatch, seq)
    yb, lse = _fox_fwd(proj, cum_cols, nbatch, seq)
    late_land = _pass_to_sibling(_gather_wait(late_send, late_recv, late_src, late_land, yb))
    full = {n: with_mine(g, n) for n, g in zip(late, late_land)}
    wba, wbb = _cols_of_chips(full["w_branch_a"]), _cols_of_chips(full["w_branch_b"])
    wout = full["w_out"].reshape(D, D)
    wg, wu, wd = full["w_ffn_gate"], full["w_ffn_up"], full["w_ffn_down"]
    merged, u, x1 = _merge_fwd(ya, yb, proj, x2, mod8, wba, wbb, wout, ln1_w, ln1_b, seq)
    a_pre, b_pre, dz2, st2, dm2 = _ffn_fwd(x1, mod8, wg, wu, wd, tgt2, ln2_w, ln2_b, seq)
    loss = lax.psum(st2[2, 0], ("x", "y", "c"))

    da, db, hmid, dffn, du, dxp, st1, dm1 = _ffn_bwd(
        dz2, a_pre, b_pre, jnp.swapaxes(wg, 1, 2), jnp.swapaxes(wu, 1, 2), jnp.swapaxes(wd, 1, 2),
        x1, x2, u, mod8, ln1_w, seq)
    g_st = {}
    g_st["w_ffn_down"] = _tn_matmul(hmid, dffn, "dw_ffn_down", seq)
    g_st["w_ffn_gate"] = _tn_matmul(x1, da, "dw_ffn_gate", seq, mod8, (3, 4))
    g_st["w_ffn_up"] = _tn_matmul(x1, db, "dw_ffn_up", seq, mod8, (3, 4))
    g_st["w_out"] = _tn_matmul(merged, du, "dw_out", seq).reshape(N_CHIPS, D // N_CHIPS, D)
    dproj, dpa, dpb, dya, dyb = _merge_bwd(du, ya, yb, proj, wba, wbb, wout, seq)
    g_st["w_branch_a"] = _tn_matmul(ya, dpa, "dw_branch_a", seq, split=D // N_CHIPS)
    g_st["w_branch_b"] = _tn_matmul(yb, dpb, "dw_branch_b", seq, split=D // N_CHIPS)
    dproj, dq, drs, dcs = _fox_bwd(proj, cum_cols, lse, yb, dyb, dproj, nbatch, seq)
    dproj = _place_cols(dproj, dq, COL_BQ)
    dproj, sm_fox = _fox_dbf(projf, bias128, drs, dcs, dproj, nbatch, seq)
    dproj, sm_hgrn = _hgrn_bwd(proj, dya, ckpt, lb_logits, hgrn_norm_w, dproj, nbatch, seq)
    grad_x2, dm0 = _dh_kernel(dproj, w_p, x2, dxp, mod8, seq)
    dw_in = _unpermute_cols(_tn_matmul(x2, dproj, "dw_in", seq, mod8, (0, 1)))
    ncin = NIN // N_CHIPS
    g_st["w_in"] = jnp.stack([dw_in[:, k * ncin:(k + 1) * ncin] for k in range(N_CHIPS)])

    g_list = [g_st[n] for n in _BIG]
    halves = [_add_my_half(g, o, core_arr, "grad_add_halves_" + n)
              for n, g, o in zip(_BIG, g_list, _swap_halves(g_list))]
    recv = _scatter_chips([h16 for _, h16 in halves])
    g_mine = [_add_chips(h32, r, chip_arr, "grad_add_chips_" + n) for n, (h32, _), r in zip(_BIG, halves, recv)]
    g_sib = _join_halves(g_mine)

    dmod = (dm0 + dm1 + dm2)[:, :6, :].reshape(nbatch, 6 * D)
    row2 = jnp.concatenate([st1[0:1], st1[1:2], st2[0:1], st2[1:2], sm_hgrn[1:2], sm_hgrn[0:1], sm_hgrn[0:1],
                            sm_fox[0:1, :BH], jnp.zeros((1, SMALL_W - O_FOX - BH), F32)], axis=1)
    spack = jnp.concatenate([dmod, row2, jnp.zeros((8 - nbatch - 1, SMALL_W), F32)], axis=0)
    gath = _allgather8(spack, "gather_small")
    w8 = _pack_small(b_ada, ln1_w, ln1_b, ln2_w, ln2_b, hgrn_norm_w, lb_logits, fox_f_bias)
    m8 = _pack_small(m_b_ada, m_ln1_w, m_ln1_b, m_ln2_w, m_ln2_b, m_hgrn_norm_w, m_lb_logits, m_fox_f_bias)
    v8 = _pack_small(v_b_ada, v_ln1_w, v_ln1_b, v_ln2_w, v_ln2_b, v_hgrn_norm_w, v_lb_logits, v_fox_f_bias)
    sg, sd, smn, svn = (_unpack_small(p) for p in _small_update(gath, w8, m8, v8))
    dmod_all = gath[:, :nbatch, :].reshape(N_DEV * nbatch, SMALL_W)
    g_ada = _grad_w_ada(c_all, lax.dynamic_slice_in_dim(dmod_all, chip * ncol, ncol, axis=1))

    grads = dict(sg)
    deltas = dict(sd)
    new_m = dict(smn)
    new_v = dict(svn)
    grads["w_ada"] = g_ada
    deltas["w_ada"], new_m["w_ada"], new_v["w_ada"] = _adamw(w_ada[0], g_ada, m_w_ada[0], v_w_ada[0], "adamw_w_ada")
    for n, gm, gs in zip(_BIG, g_mine, g_sib):
        grads[n], deltas[n], new_m[n], new_v[n] = _adamw_halves(shard_w[n], gm, gs, shard_m[n], shard_v[n], core_arr,
                                                                "adamw_" + n)

    names = ["w_ada", "b_ada", "w_in", "fox_f_bias", "lb_logits", "hgrn_norm_w", "w_branch_a", "w_branch_b", "w_out",
             "ln1_w", "ln1_b", "w_ffn_gate", "w_ffn_up", "w_ffn_down", "ln2_w", "ln2_b"]
    shapes = dict(w_ada=w_ada.shape, b_ada=b_ada.shape, w_in=w_in.shape, fox_f_bias=fox_f_bias.shape,
                  lb_logits=lb_logits.shape, hgrn_norm_w=hgrn_norm_w.shape, w_branch_a=w_branch_a.shape,
                  w_branch_b=w_branch_b.shape, w_out=w_out.shape, ln1_w=ln1_w.shape, ln1_b=ln1_b.shape,
                  w_ffn_gate=w_ffn_gate.shape, w_ffn_up=w_ffn_up.shape, w_ffn_down=w_ffn_down.shape,
                  ln2_w=ln2_w.shape, ln2_b=ln2_b.shape)
    outs = [loss, grad_x2.reshape(x.shape)]
    for group in (grads, deltas, new_m, new_v):
        outs += [group[n].reshape(shapes[n]) for n in names]
    return tuple(outs)
```

```python
import functools
import math

import jax
import jax.numpy as jnp
import numpy as np
from jax import lax
from jax.experimental import pallas as pl
from jax.experimental.pallas import tpu as pltpu

F32 = jnp.float32
BF16 = jnp.bfloat16
MESH = pl.DeviceIdType.MESH
HIGHEST = lax.Precision.HIGHEST

D = 1024
AW = 512
AH = 4
ADH = 128
BH = 8
BDH = 64
DFF = 2816
NIN = 5640
NP = 5760
N_CHIPS = 4
N_DEV = 8
HGRN_BLOCK = 256
COL_GATES = 0
COL_BQ = 2048
COL_KV = 2560
COL_A = 3584
COL_BF = 5632
ALPHA = 2.0 ** 0.25
LN_EPS = 1e-5
RMS_EPS = 1e-6
NEG = -1e30
LOG2E = 1.4426950408889634
LR, B1, B2, EPS, WD, STEP = 0.001, 0.9, 0.999, 1e-08, 0.01, 10
SMALL_W = 6144
O_LN1W, O_LN1B, O_LN2W, O_LN2B, O_NORM, O_LB0, O_LB1, O_FOX = 0, 1024, 2048, 3072, 4096, 4608, 5120, 5632


def _params(sem=None, vmem_mb=None):
    kw = {}
    if sem is not None:
        kw["dimension_semantics"] = sem
    if vmem_mb is not None:
        kw["vmem_limit_bytes"] = vmem_mb << 20
    return pltpu.CompilerParams(**kw)


def _dot(a, b):
    return jnp.dot(a.astype(BF16), b.astype(BF16), preferred_element_type=F32)


def _dot_nt(a, b):
    return lax.dot_general(a.astype(BF16), b.astype(BF16), (((1,), (1,)), ((), ())), preferred_element_type=F32)


def _dot_tn(a, b):
    return lax.dot_general(a.astype(BF16), b.astype(BF16), (((0,), (0,)), ((), ())), preferred_element_type=F32)


def _dot_f32(a, b):
    return jnp.dot(a, b, preferred_element_type=F32, precision=HIGHEST)


def _perm_segments():
    segs = [(3592, 5640), (2048, 2560)]
    for p in range(4):
        segs += [(2560 + 128 * p, 2688 + 128 * p), (3072 + 128 * p, 3200 + 128 * p)]
    for h in range(4):
        segs += [(128 * h + 512 * t, 128 * h + 512 * t + 128) for t in range(4)]
    segs += [(3584, 3592)]
    return segs


def _permute_cols(w):
    parts = [w[:, a:b] for a, b in _perm_segments()]
    parts.append(jnp.zeros((w.shape[0], NP - NIN), w.dtype))
    return jnp.concatenate(parts, axis=1)


def _unpermute_cols(g):
    pos, where = 0, {}
    for a, b in _perm_segments():
        where[a] = (pos, pos + b - a)
        pos += b - a
    parts = [g[:, where[a][0]:where[a][1]] for a in sorted(where)]
    return jnp.concatenate(parts, axis=1)


def _allgather8(v, name):
    rows, cols = v.shape

    def body(x_ref, out_ref, send_sems, recv_sems, local_sem):
        x, y, c = lax.axis_index("x"), lax.axis_index("y"), lax.axis_index("c")
        me, sibling = (x, y, c), (x, y, 1 - c)
        chips = [(1 - x, y), (x, 1 - y), (1 - x, 1 - y)]

        def slot(px, py, pc):
            return out_ref.at[4 * px + 2 * py + pc]

        def copy(k, block, to, src=None):
            return pltpu.make_async_remote_copy(
                src_ref=slot(*block) if src is None else src, dst_ref=slot(*block),
                send_sem=send_sems.at[k], recv_sem=recv_sems.at[k], device_id=to, device_id_type=MESH)

        mine = pltpu.make_async_copy(x_ref, slot(*me), local_sem)
        mine.start()
        first = [copy(0, me, sibling, src=x_ref)]
        first += [copy(1 + j, me, (*chip, c), src=x_ref) for j, chip in enumerate(chips)]
        for cp in first:
            cp.start()
        passed = [copy(4 + j, (*chip, c), sibling) for j, chip in enumerate(chips)]
        for j, chip in enumerate(chips):
            copy(1 + j, (*chip, c), me).wait_recv()
            passed[j].start()
        copy(0, sibling, me).wait_recv()
        for j, chip in enumerate(chips):
            copy(4 + j, (*chip, 1 - c), me).wait_recv()
        for cp in first + passed:
            cp.wait_send()
        mine.wait()

    return pl.pallas_call(
        body, name=name,
        out_shape=jax.ShapeDtypeStruct((N_DEV, rows, cols), v.dtype),
        in_specs=[pl.BlockSpec(memory_space=pltpu.VMEM)],
        out_specs=pl.BlockSpec(memory_space=pltpu.VMEM),
        scratch_shapes=[pltpu.SemaphoreType.DMA((7,)), pltpu.SemaphoreType.DMA((7,)), pltpu.SemaphoreType.DMA],
    )(v)


def _hbm_specs(n):
    return [pl.BlockSpec(memory_space=pl.ANY)] * n


def _gather_weights(shards):
    n = len(shards)

    def body(*refs):
        ins, outs, (send_sems, recv_sems) = refs[:n], refs[n:2 * n], refs[2 * n:]
        x, y, c = lax.axis_index("x"), lax.axis_index("y"), lax.axis_index("c")
        sibling = (x, y, 1 - c)
        chips = [(1 - x, y), (x, 1 - y), (1 - x, 1 - y)]

        def blk(w, px, py, half):
            hr = ins[w].shape[0] // 2
            return outs[w].at[2 * px + py, pl.ds(half * hr, hr), :]

        def copy(w, k, block, to, src=None):
            return pltpu.make_async_remote_copy(
                src_ref=blk(w, *block) if src is None else src, dst_ref=blk(w, *block),
                send_sem=send_sems.at[6 * w + k], recv_sem=recv_sems.at[6 * w + k], device_id=to, device_id_type=MESH)

        first = []
        for w in range(n):
            hr = ins[w].shape[0] // 2
            my_half = ins[w].at[pl.ds(c * hr, hr), :]
            first += [copy(w, j, (x, y, c), (*chip, c), src=my_half) for j, chip in enumerate(chips)]
        for cp in first:
            cp.start()
        passed = []
        for j, chip in enumerate(chips):
            for w in range(n):
                copy(w, j, (*chip, c), (x, y, c)).wait_recv()
                passed.append(copy(w, 3 + j, (*chip, c), sibling))
                passed[-1].start()
        for j, chip in enumerate(chips):
            for w in range(n):
                copy(w, 3 + j, (*chip, 1 - c), (x, y, c)).wait_recv()
        for cp in first + passed:
            cp.wait_send()

    return pl.pallas_call(
        body, name="gather_weights",
        out_shape=[jax.ShapeDtypeStruct((N_CHIPS,) + s.shape, s.dtype) for s in shards],
        in_specs=_hbm_specs(n), out_specs=_hbm_specs(n),
        scratch_shapes=[pltpu.SemaphoreType.DMA((6 * n,)), pltpu.SemaphoreType.DMA((6 * n,))],
    )(*shards)


def _swap_halves(grads, name):
    n = len(grads)

    def body(*refs):
        ins, outs, (send_sems, recv_sems) = refs[:n], refs[n:2 * n], refs[2 * n:]
        x, y, c = lax.axis_index("x"), lax.axis_index("y"), lax.axis_index("c")
        cps = []
        for w in range(n):
            hr = ins[w].shape[1] // 2
            cps.append(pltpu.make_async_remote_copy(
                src_ref=ins[w].at[:, pl.ds((1 - c) * hr, hr), :], dst_ref=outs[w],
                send_sem=send_sems.at[w], recv_sem=recv_sems.at[w], device_id=(x, y, 1 - c), device_id_type=MESH))
        for cp in cps:
            cp.start()
        for cp in cps:
            cp.wait()

    return pl.pallas_call(
        body, name=name,
        out_shape=[jax.ShapeDtypeStruct((N_CHIPS, g.shape[1] // 2, g.shape[2]), g.dtype) for g in grads],
        in_specs=_hbm_specs(n), out_specs=_hbm_specs(n),
        scratch_shapes=[pltpu.SemaphoreType.DMA((n,)), pltpu.SemaphoreType.DMA((n,))],
    )(*grads)


def _scatter_chips(reds, name):
    n = len(reds)

    def body(*refs):
        ins, outs, (send_sems, recv_sems) = refs[:n], refs[n:2 * n], refs[2 * n:]
        x, y, c = lax.axis_index("x"), lax.axis_index("y"), lax.axis_index("c")
        chips = [(1 - x, y), (x, 1 - y), (1 - x, 1 - y)]
        cps = [pltpu.make_async_remote_copy(
            src_ref=ins[w].at[2 * chip[0] + chip[1]], dst_ref=outs[w].at[j],
            send_sem=send_sems.at[3 * w + j], recv_sem=recv_sems.at[3 * w + j],
            device_id=(*chip, c), device_id_type=MESH)
            for j, chip in enumerate(chips) for w in range(n)]
        for cp in cps:
            cp.start()
        for cp in cps:
            cp.wait()

    return pl.pallas_call(
        body, name=name,
        out_shape=[jax.ShapeDtypeStruct((3,) + r.shape[1:], r.dtype) for r in reds],
        in_specs=_hbm_specs(n), out_specs=_hbm_specs(n),
        scratch_shapes=[pltpu.SemaphoreType.DMA((3 * n,)), pltpu.SemaphoreType.DMA((3 * n,))],
    )(*reds)


def _join_halves(halves):
    n = len(halves)

    def body(*refs):
        ins, outs, (send_sems, recv_sems) = refs[:n], refs[n:2 * n], refs[2 * n:]
        x, y, c = lax.axis_index("x"), lax.axis_index("y"), lax.axis_index("c")
        cps = [pltpu.make_async_remote_copy(
            src_ref=ins[w], dst_ref=outs[w], send_sem=send_sems.at[w], recv_sem=recv_sems.at[w],
            device_id=(x, y, 1 - c), device_id_type=MESH) for w in range(n)]
        for cp in cps:
            cp.start()
        for cp in cps:
            cp.wait()

    return pl.pallas_call(
        body, name="grad_join_halves",
        out_shape=[jax.ShapeDtypeStruct(h.shape, h.dtype) for h in halves],
        in_specs=_hbm_specs(n), out_specs=_hbm_specs(n),
        scratch_shapes=[pltpu.SemaphoreType.DMA((n,)), pltpu.SemaphoreType.DMA((n,))],
    )(*halves)


def _in_hbm(v):
    return pltpu.with_memory_space_constraint(v, pltpu.HBM)


_SPLIT_COPY = pltpu.CompilerParams(has_side_effects=pltpu.SideEffectType.DATAFLOW_SIDE_EFFECTING)


def _gather_copies(srcs, lands, send_sems, recv_sems):
    x, y, c = lax.axis_index("x"), lax.axis_index("y"), lax.axis_index("c")
    cps = []
    for w, (src, land) in enumerate(zip(srcs, lands)):
        hr = src.shape[0] // 2
        for j, chip in enumerate([(1 - x, y), (x, 1 - y), (1 - x, 1 - y)]):
            cps.append(pltpu.make_async_remote_copy(
                src_ref=src.at[pl.ds(c * hr, hr), :], dst_ref=land.at[2 * x + y, pl.ds(c * hr, hr), :],
                send_sem=send_sems.at[3 * w + j], recv_sem=recv_sems.at[3 * w + j],
                device_id=(*chip, c), device_id_type=MESH))
    return cps


def _scatter_copies(srcs, lands, send_sems, recv_sems):
    x, y, c = lax.axis_index("x"), lax.axis_index("y"), lax.axis_index("c")
    cps = []
    for w, (src, land) in enumerate(zip(srcs, lands)):
        for j, chip in enumerate([(1 - x, y), (x, 1 - y), (1 - x, 1 - y)]):
            cps.append(pltpu.make_async_remote_copy(
                src_ref=src.at[2 * chip[0] + chip[1]], dst_ref=land.at[j],
                send_sem=send_sems.at[3 * w + j], recv_sem=recv_sems.at[3 * w + j],
                device_id=(*chip, c), device_id_type=MESH))
    return cps


def _split_start(copies, srcs, lands, name):
    n = len(srcs)

    def body(*refs):
        src, lnd, send_sems, recv_sems, token = refs[:n], refs[n:2 * n], refs[2 * n], refs[2 * n + 1], refs[-1]
        for cp in copies(src, lnd, send_sems, recv_sems):
            cp.start()
        token[...] = jnp.zeros_like(token)

    hbm = pl.BlockSpec(memory_space=pltpu.HBM)
    sem = pl.BlockSpec(memory_space=pltpu.SEMAPHORE)
    outs = pl.pallas_call(
        body, name=name,
        out_shape=(pltpu.SemaphoreType.DMA((3 * n,)), pltpu.SemaphoreType.DMA((3 * n,)),
                   *[pltpu.HBM(v.shape, v.dtype) for v in srcs + lands], jax.ShapeDtypeStruct((8, 128), F32)),
        in_specs=[hbm] * (2 * n),
        out_specs=(sem, sem, *([hbm] * (2 * n)), pl.BlockSpec(memory_space=pltpu.VMEM)),
        input_output_aliases={i: 2 + i for i in range(2 * n)},
        compiler_params=_SPLIT_COPY,
    )(*[_in_hbm(v) for v in srcs + lands])
    return outs[0], outs[1], list(outs[2:2 + n]), list(outs[2 + n:2 + 2 * n]), outs[-1]


def _split_wait(copies, send_sems, recv_sems, srcs, lands, after, name):
    n = len(srcs)

    def body(*refs):
        src, lnd, send_sems, recv_sems = refs[:n], refs[n:2 * n], refs[2 * n], refs[2 * n + 1]
        for cp in copies(src, lnd, send_sems, recv_sems):
            cp.wait_send()
            cp.wait_recv()

    hbm = pl.BlockSpec(memory_space=pltpu.HBM)
    sem = pl.BlockSpec(memory_space=pltpu.SEMAPHORE)
    outs = pl.pallas_call(
        body, name=name,
        out_shape=tuple(pltpu.HBM(v.shape, v.dtype) for v in srcs + lands),
        in_specs=[hbm] * (2 * n) + [sem, sem, pl.BlockSpec(memory_space=pl.ANY)],
        out_specs=tuple([hbm] * (2 * n)),
        input_output_aliases={i: i for i in range(2 * n)},
        compiler_params=_SPLIT_COPY,
    )(*srcs, *lands, send_sems, recv_sems, after)
    return list(outs[n:])


def _pass_to_sibling(lands):
    n = len(lands)

    def body(*refs):
        ins, outs, (send_sems, recv_sems) = refs[:n], refs[n:2 * n], refs[2 * n:]
        x, y, c = lax.axis_index("x"), lax.axis_index("y"), lax.axis_index("c")
        cps = []
        for w in range(n):
            hr = ins[w].shape[1] // 2
            for j, chip in enumerate([(1 - x, y), (x, 1 - y), (1 - x, 1 - y)]):
                k = 2 * chip[0] + chip[1]
                cps.append(pltpu.make_async_remote_copy(
                    src_ref=ins[w].at[k, pl.ds(c * hr, hr), :], dst_ref=outs[w].at[k, pl.ds(c * hr, hr), :],
                    send_sem=send_sems.at[3 * w + j], recv_sem=recv_sems.at[3 * w + j],
                    device_id=(x, y, 1 - c), device_id_type=MESH))
        for cp in cps:
            cp.start()
        for cp in cps:
            cp.wait()

    return pl.pallas_call(
        body, name="gather_late_pass",
        out_shape=[jax.ShapeDtypeStruct(v.shape, v.dtype) for v in lands],
        in_specs=_hbm_specs(n), out_specs=_hbm_specs(n),
        input_output_aliases={i: i for i in range(n)},
        scratch_shapes=[pltpu.SemaphoreType.DMA((3 * n,)), pltpu.SemaphoreType.DMA((3 * n,))],
    )(*lands)


def _row_tile(rows):
    for cand in (256, 176, 128, 64, 32, 16):
        if rows % cand == 0:
            return cand
    raise ValueError(rows)


def _add_my_half(g, other, c_idx, name):
    _, k, n = g.shape
    hr = k // 2
    tr = _row_tile(hr)
    nb = hr // tr

    def body(c_ref, g_ref, o_ref, out_ref, out16_ref):
        s = g_ref[...] + o_ref[...]
        out_ref[...] = s
        out16_ref[...] = s.astype(BF16)

    return pl.pallas_call(
        body, name=name,
        grid_spec=pltpu.PrefetchScalarGridSpec(
            num_scalar_prefetch=1, grid=(N_CHIPS, nb),
            in_specs=[pl.BlockSpec((1, tr, n), lambda j, i, c: (j, c[0] * nb + i, 0)),
                      pl.BlockSpec((1, tr, n), lambda j, i, c: (j, i, 0))],
            out_specs=[pl.BlockSpec((1, tr, n), lambda j, i, c: (j, i, 0)),
                       pl.BlockSpec((1, tr, n), lambda j, i, c: (j, i, 0))]),
        out_shape=[jax.ShapeDtypeStruct((N_CHIPS, hr, n), F32), jax.ShapeDtypeStruct((N_CHIPS, hr, n), BF16)],
        compiler_params=_params(("parallel", "parallel")),
    )(c_idx, g, other)


def _add_chips(red, recv, chip_idx, name):
    _, hr, n = red.shape
    tr = _row_tile(hr)

    def body(k_ref, r_ref, v_ref, out_ref):
        out_ref[...] = ((r_ref[0] + v_ref[0].astype(F32)) + v_ref[1].astype(F32)) + v_ref[2].astype(F32)

    return pl.pallas_call(
        body, name=name,
        grid_spec=pltpu.PrefetchScalarGridSpec(
            num_scalar_prefetch=1, grid=(hr // tr,),
            in_specs=[pl.BlockSpec((1, tr, n), lambda i, k: (k[0], i, 0)),
                      pl.BlockSpec((3, tr, n), lambda i, k: (0, i, 0))],
            out_specs=pl.BlockSpec((tr, n), lambda i, k: (i, 0))),
        out_shape=jax.ShapeDtypeStruct((hr, n), F32),
        compiler_params=_params(("parallel",)),
    )(chip_idx, red, recv)


def _mod_shard(c_all, w_ada, b_ada):
    nb, cols = c_all.shape[0], w_ada.shape[1]

    def body(c_ref, w_ref, b_ref, o_ref):
        c = c_ref[...]
        o_ref[...] = _dot(c * jax.nn.sigmoid(c), w_ref[...]) + b_ref[...]

    return pl.pallas_call(
        body, name="mod_shard", out_shape=jax.ShapeDtypeStruct((nb, cols), F32),
        compiler_params=_params(vmem_mb=48),
    )(c_all, w_ada, b_ada)


def _proj(x2, mod8, w, seq, out_dtype, name):
    t = x2.shape[0]
    n = w.shape[1]
    tm, tn = min(512, seq), min(1152, n)
    tpb = seq // tm

    def body(x_ref, mod_ref, w_ref, o_ref, h_scr):
        @pl.when(pl.program_id(1) == 0)
        def _():
            h_scr[...] = (x_ref[...] * (1.0 + mod_ref[0, 1:2, :]) + mod_ref[0, 0:1, :]).astype(BF16)
        o_ref[...] = jnp.dot(h_scr[...], w_ref[...], preferred_element_type=F32).astype(o_ref.dtype)

    return pl.pallas_call(
        body, name=name, grid=(t // tm, n // tn),
        in_specs=[pl.BlockSpec((tm, D), lambda i, j: (i, 0)),
                  pl.BlockSpec((1, 8, D), lambda i, j: (i // tpb, 0, 0)),
                  pl.BlockSpec((D, tn), lambda i, j: (0, j))],
        out_specs=pl.BlockSpec((tm, tn), lambda i, j: (i, j)),
        out_shape=jax.ShapeDtypeStruct((t, n), out_dtype),
        scratch_shapes=[pltpu.VMEM((tm, D), BF16)],
        compiler_params=_params(("parallel", "arbitrary"), 48),
    )(x2, mod8, w)


def _tn_matmul(a, b, name, seq, mod8=None, rows=None, split=None):
    a_st, b_st = a.ndim == 3, b.ndim == 3
    t, ka = a.shape[-2:]
    n = b.shape[-1]
    tt = min(1024, seq)
    tpb = seq // tt
    nt = t // tt
    if a_st or b_st:
        steps, tn = (a.shape[0] if a_st else b.shape[0]), n
    else:
        tn = split
        if tn is None:
            tn = next(cand for cand in (1152, 1024, 1408, 512, n) if n % cand == 0)
        steps = n // tn
    stacked_out = a_st or b_st or split is not None

    def body(*refs):
        if mod8 is None:
            a_ref, b_ref, o_ref = refs
            av = a_ref[0] if a_st else a_ref[...]
        else:
            a_ref, m_ref, b_ref, o_ref = refs
            av = a_ref[...] * (1.0 + m_ref[0, rows[1]:rows[1] + 1, :]) + m_ref[0, rows[0]:rows[0] + 1, :]
        part = _dot_tn(av, b_ref[0] if b_st else b_ref[...])
        if stacked_out:
            part = part[None]

        @pl.when(pl.program_id(1) == 0)
        def _():
            o_ref[...] = part

        @pl.when(pl.program_id(1) > 0)
        def _():
            o_ref[...] += part

    if a_st:
        in_specs = [pl.BlockSpec((1, tt, ka), lambda j, k: (j, k, 0))]
    else:
        in_specs = [pl.BlockSpec((tt, ka), lambda j, k: (k, 0))]
    args = [a]
    if mod8 is not None:
        in_specs.append(pl.BlockSpec((1, 8, ka), lambda j, k: (k // tpb, 0, 0)))
        args.append(mod8)
    if b_st:
        in_specs.append(pl.BlockSpec((1, tt, n), lambda j, k: (j, k, 0)))
    else:
        in_specs.append(pl.BlockSpec((tt, tn), lambda j, k: (k, 0 if a_st else j)))
    args.append(b)
    if stacked_out:
        out_spec = pl.BlockSpec((1, ka, tn), lambda j, k: (j, 0, 0))
        out_shape = jax.ShapeDtypeStruct((steps, ka, tn), F32)
    else:
        out_spec = pl.BlockSpec((ka, tn), lambda j, k: (0, j))
        out_shape = jax.ShapeDtypeStruct((ka, n), F32)
    return pl.pallas_call(
        body, name=name, grid=(steps, nt), in_specs=in_specs, out_specs=out_spec, out_shape=out_shape,
        compiler_params=_params(("parallel", "arbitrary"), 56),
    )(*args)


def _dh_kernel(dproj, w_p, x2, dxp, mod8, seq):
    t = x2.shape[0]
    tm, tk = min(512, seq), 1152
    tpb = seq // tm
    nk = NP // tk
    nbatch = t // seq

    def body(dp_ref, w_ref, x_ref, dxp_ref, mod_ref, gx_ref, dm_ref, acc):
        i, k = pl.program_id(0), pl.program_id(1)

        @pl.when(k == 0)
        def _():
            acc[...] = jnp.zeros_like(acc)

        acc[...] += _dot_nt(dp_ref[...], w_ref[...])

        @pl.when(k == nk - 1)
        def _():
            dh = acc[...]
            gx_ref[...] = dxp_ref[...] + dh * (1.0 + mod_ref[0, 1:2, :])
            upd = jnp.concatenate(
                [jnp.sum(dh, axis=0, keepdims=True), jnp.sum(dh * x_ref[...], axis=0, keepdims=True),
                 jnp.zeros((6, D), F32)], axis=0)

            @pl.when(i % tpb == 0)
            def _():
                dm_ref[0] = upd

            @pl.when(i % tpb != 0)
            def _():
                dm_ref[0] += upd

    return pl.pallas_call(
        body, name="dh", grid=(t // tm, nk),
        in_specs=[pl.BlockSpec((tm, tk), lambda i, k: (i, k)),
                  pl.BlockSpec((D, tk), lambda i, k: (0, k)),
                  pl.BlockSpec((tm, D), lambda i, k: (i, 0)),
                  pl.BlockSpec((tm, D), lambda i, k: (i, 0)),
                  pl.BlockSpec((1, 8, D), lambda i, k: (i // tpb, 0, 0))],
        out_specs=[pl.BlockSpec((tm, D), lambda i, k: (i, 0)),
                   pl.BlockSpec((1, 8, D), lambda i, k: (i // tpb, 0, 0))],
        out_shape=[jax.ShapeDtypeStruct((t, D), F32), jax.ShapeDtypeStruct((nbatch, 8, D), F32)],
        scratch_shapes=[pltpu.VMEM((tm, D), F32)],
        compiler_params=_params(("arbitrary", "arbitrary"), 48),
    )(dproj, w_p, x2, dxp, mod8)


def _tri(n, upper):
    r = lax.broadcasted_iota(jnp.int32, (n, n), 0)
    c = lax.broadcasted_iota(jnp.int32, (n, n), 1)
    return jnp.where((c >= r) if upper else (c <= r), 1.0, 0.0).astype(F32)


@jax.custom_vjp
def _mm_nn(a, b):
    return _dot(a, b)


_mm_nn.defvjp(lambda a, b: (_dot(a, b), (a, b)),
              lambda res, g: (_dot_nt(g, res[1]), _dot_tn(res[0], g)))


@jax.custom_vjp
def _mm_nt(a, b):
    return _dot_nt(a, b)


_mm_nt.defvjp(lambda a, b: (_dot_nt(a, b), (a, b)),
              lambda res, g: (_dot(g, res[1]), _dot_tn(g, res[0])))


@jax.custom_vjp
def _mm_tn(a, b):
    return _dot_tn(a, b)


_mm_tn.defvjp(lambda a, b: (_dot_tn(a, b), (a, b)),
              lambda res, g: (_dot_nt(res[1], g), _dot(res[0], g)))


@jax.custom_vjp
def _cumsum_rows(x):
    return _dot_f32(_tri(x.shape[0], False), x)


_cumsum_rows.defvjp(lambda x: (_cumsum_rows(x), None),
                    lambda _, g: (_dot_f32(_tri(g.shape[0], True), g),))


@functools.partial(jax.custom_vjp, nondiff_argnums=(1,))
def _shift_rows(x, k):
    return pltpu.roll(x, k % x.shape[0], 0)


_shift_rows.defvjp(lambda x, k: (_shift_rows(x, k), None),
                   lambda k, _, g: (pltpu.roll(g, (-k) % g.shape[0], 0),))


def _group_ref(bc, m):
    n = bc.shape[0] // (2 * m)
    b3 = bc.reshape(n, 2 * m, ADH)
    row = lax.broadcasted_iota(jnp.int32, b3.shape, 1)
    ref = jnp.sum(jnp.where(row == m - 1, b3, 0.0), axis=1, keepdims=True)
    return jnp.broadcast_to(ref, b3.shape).reshape(bc.shape)


def _hgrn_block(q, fl, v, g, st, lb, nw):
    n = q.shape[0]
    f = lb + (1.0 - lb) * jax.nn.sigmoid(fl)
    kk = 1.0 - f
    lf = jnp.log(f)
    bc = _cumsum_rows(lf)
    row = lax.broadcasted_iota(jnp.int32, (n, ADH), 0)
    same = jnp.bitwise_xor(lax.broadcasted_iota(jnp.int32, (n, n), 0), lax.broadcasted_iota(jnp.int32, (n, n), 1))
    a = jnp.zeros((n, n), F32)
    m = 1
    while m < n:
        r = jnp.bitwise_and(row, 2 * m - 1)
        up, lo = r >= m, r < m
        if m == 1:
            aq, ak = lf, jnp.zeros_like(lf)
        elif m == 2:
            aq = jnp.where(r == 3, lf + _shift_rows(lf, 1), lf)
            ak = jnp.where(r == 0, _shift_rows(lf, -1), 0.0)
        else:
            ref = _group_ref(bc, m)
            aq, ak = bc - ref, ref - bc
        qt = jnp.where(up, q * jnp.exp(jnp.where(up, aq, 0.0)), 0.0)
        kt = jnp.where(lo, kk * jnp.exp(jnp.where(lo, ak, 0.0)), 0.0)
        a = a + jnp.where(same < 2 * m, _mm_nt(qt, kt), 0.0)
        m *= 2
    last = row == n - 1
    bl = jnp.sum(jnp.where(last, bc, 0.0), axis=0, keepdims=True)
    o = _mm_nn(a, v) + _mm_nt(q * jnp.exp(bc), st) + jnp.sum(q * kk, axis=-1, keepdims=True) * v
    st_new = st * jnp.exp(bl) + _mm_tn(v, kk * jnp.exp(bl - bc))
    rms = lax.rsqrt(jnp.mean(o * o, axis=-1, keepdims=True) + RMS_EPS)
    return o * rms * nw * jax.nn.sigmoid(g), st_new


def _hgrn_fwd(proj, lb_logits, norm_w, nbatch, seq):
    t = proj.shape[0]
    blk = min(HGRN_BLOCK, seq)
    nb = seq // blk

    def body(p_ref, lbl_ref, nw_ref, y_ref, ck_ref, st_s):
        @pl.when(pl.program_id(2) == 0)
        def _():
            st_s[...] = jnp.zeros_like(st_s)

        st = st_s[...]
        ck_ref[0] = st
        lb = jax.nn.sigmoid(lbl_ref[0:1, :] - lbl_ref[1:2, :])
        p = p_ref[...].astype(F32)
        y, st_new = _hgrn_block(p[:, 0:128], p[:, 128:256], p[:, 256:384], p[:, 384:512], st, lb, nw_ref[...])
        st_s[...] = st_new
        y_ref[...] = y.astype(y_ref.dtype)

    return pl.pallas_call(
        body, name="hgrn_fwd", grid=(AH, nbatch, nb),
        in_specs=[pl.BlockSpec((blk, 512), lambda h, b, i: (b * nb + i, COL_A // 512 + h)),
                  pl.BlockSpec((2, 128), lambda h, b, i: (0, h)),
                  pl.BlockSpec((1, 128), lambda h, b, i: (0, h))],
        out_specs=[pl.BlockSpec((blk, 128), lambda h, b, i: (b * nb + i, h)),
                   pl.BlockSpec((1, 128, 128), lambda h, b, i: ((h * nbatch + b) * nb + i, 0, 0))],
        out_shape=[jax.ShapeDtypeStruct((t, AW), BF16), jax.ShapeDtypeStruct((AH * nbatch * nb, 128, 128), F32)],
        scratch_shapes=[pltpu.VMEM((128, 128), F32)],
        compiler_params=_params(("parallel", "parallel", "arbitrary"), 48),
    )(proj, lb_logits, norm_w)


def _hgrn_bwd(proj, dya, ckpt, lb_logits, norm_w, dproj, nbatch, seq):
    t = proj.shape[0]
    blk = min(HGRN_BLOCK, seq)
    nb = seq // blk

    def body(p_ref, dy_ref, ck_ref, lbl_ref, nw_ref, dp_in, dp_ref, sm_ref, dst_s):
        del dp_in
        b_id, i = pl.program_id(1), pl.program_id(2)

        @pl.when(i == 0)
        def _():
            dst_s[...] = jnp.zeros_like(dst_s)

        lb = jax.nn.sigmoid(lbl_ref[0:1, :] - lbl_ref[1:2, :])
        p = p_ref[...].astype(F32)
        _, pullback = jax.vjp(_hgrn_block, p[:, 0:128], p[:, 128:256], p[:, 256:384], p[:, 384:512],
                              ck_ref[0], lb, nw_ref[...])
        dq, dfl, dv, dg, dst, dlb, dnw = pullback((dy_ref[...], dst_s[...]))
        dst_s[...] = dst
        dp_ref[:, 0:128] = dq.astype(dp_ref.dtype)
        dp_ref[:, 128:256] = dfl.astype(dp_ref.dtype)
        dp_ref[:, 256:384] = dv.astype(dp_ref.dtype)
        dp_ref[:, 384:512] = dg.astype(dp_ref.dtype)
        upd = jnp.concatenate([dlb, dnw, jnp.zeros((6, 128), F32)], axis=0)
        first = (b_id == 0) & (i == 0)

        @pl.when(first)
        def _():
            sm_ref[...] = upd

        @pl.when(jnp.logical_not(first))
        def _():
            sm_ref[...] += upd

    def rows(h, b, i):
        return b * nb + (nb - 1 - i)

    return pl.pallas_call(
        body, name="hgrn_bwd", grid=(AH, nbatch, nb),
        in_specs=[pl.BlockSpec((blk, 512), lambda h, b, i: (rows(h, b, i), COL_A // 512 + h)),
                  pl.BlockSpec((blk, 128), lambda h, b, i: (rows(h, b, i), h)),
                  pl.BlockSpec((1, 128, 128), lambda h, b, i: ((h * nbatch + b) * nb + (nb - 1 - i), 0, 0)),
                  pl.BlockSpec((2, 128), lambda h, b, i: (0, h)),
                  pl.BlockSpec((1, 128), lambda h, b, i: (0, h)),
                  pl.BlockSpec(memory_space=pl.ANY)],
        out_specs=[pl.BlockSpec((blk, 512), lambda h, b, i: (rows(h, b, i), COL_A // 512 + h)),
                   pl.BlockSpec((8, 128), lambda h, b, i: (0, h))],
        out_shape=[jax.ShapeDtypeStruct((t, NP), BF16), jax.ShapeDtypeStruct((8, AW), F32)],
        input_output_aliases={5: 0},
        scratch_shapes=[pltpu.VMEM((128, 128), F32)],
        compiler_params=_params(("parallel", "arbitrary", "arbitrary"), 48),
    )(proj, dya, ckpt, lb_logits, norm_w, dproj)


def _log_sigmoid(z):
    return jnp.minimum(z, 0.0) - jnp.log(1.0 + jnp.exp(-jnp.abs(z)))


def _fox_cum(proj, bias128, nbatch, seq):
    t = proj.shape[0]
    ts = min(512, seq)
    nb = seq // ts

    def body(p_ref, b_ref, c_ref, carry):
        @pl.when(pl.program_id(1) == 0)
        def _():
            carry[...] = jnp.zeros_like(carry)
        cum = _dot_f32(_tri(ts, False), _log_sigmoid(p_ref[...] + b_ref[...])) + carry[...]
        carry[...] = cum[ts - 1:ts, :]
        cum2 = cum * LOG2E
        lane = lax.broadcasted_iota(jnp.int32, (ts, 128), 1)
        for p in range(4):
            c_ref[p] = jnp.where(lane < 64, cum2[:, 2 * p:2 * p + 1], cum2[:, 2 * p + 1:2 * p + 2])

    return pl.pallas_call(
        body, name="fox_cum", grid=(nbatch, nb),
        in_specs=[pl.BlockSpec((ts, 128), lambda b, i: (b * nb + i, 0)),
                  pl.BlockSpec((1, 128), lambda b, i: (0, 0))],
        out_specs=pl.BlockSpec((4, ts, 128), lambda b, i: (0, b * nb + i, 0)),
        out_shape=jax.ShapeDtypeStruct((4, t, 128), F32),
        scratch_shapes=[pltpu.VMEM((1, 128), F32)],
        compiler_params=_params(("parallel", "arbitrary")),
    )(proj, bias128)


def _fox_scores_t(q_ref, kv_ref, cc_ref, hh, masked, tq, tk):
    kh = kv_ref[:, 64 * hh:64 * hh + 64].astype(BF16)
    qh = (q_ref[:, 64 * hh:64 * hh + 64] * (LOG2E * BDH ** -0.5)).astype(BF16)
    s = _dot_nt(kh, qh) - cc_ref[0, :, 64 * hh:64 * hh + 1]
    if masked:
        key = lax.broadcasted_iota(jnp.int32, (tk, tq), 0)
        qry = lax.broadcasted_iota(jnp.int32, (tk, tq), 1)
        s = jnp.where(key <= qry, s, NEG)
    return s, kh


def _causal_pairs(nq, key_major):
    if key_major:
        pairs = [(i, j) for j in range(nq) for i in range(j, nq)]
    else:
        pairs = [(i, j) for i in range(nq) for j in range(i + 1)]
    return (jnp.asarray([p[0] for p in pairs], jnp.int32), jnp.asarray([p[1] for p in pairs], jnp.int32))


def _with_ones_lane(x128, hh):
    lane = lax.broadcasted_iota(jnp.int32, x128.shape, 1)
    one = jnp.ones_like(x128)
    zero = jnp.zeros_like(x128)
    if hh == 0:
        return jnp.where(lane < 64, x128, jnp.where(lane == 64, one, zero))
    return jnp.where(lane >= 64, x128, jnp.where(lane == 0, one, zero))


def _fox_fwd(proj, cum_cols, nbatch, seq):
    t = proj.shape[0]
    tq = tk = min(512, seq)
    nq = seq // tq
    qi, kj = _causal_pairs(nq, key_major=False)

    def body(qi_ref, kj_ref, q_ref, kv_ref, cc_ref, o_ref, lse_ref, m_s, acc_s):
        s_id = pl.program_id(2)
        i, j = qi_ref[s_id], kj_ref[s_id]

        @pl.when(j == 0)
        def _():
            m_s[...] = jnp.full_like(m_s, NEG)
            acc_s[...] = jnp.zeros_like(acc_s)

        def step(masked):
            for hh in range(2):
                s, _ = _fox_scores_t(q_ref, kv_ref, cc_ref, hh, masked, tq, tk)
                m_prev = m_s[hh:hh + 1, :]
                m_new = jnp.maximum(m_prev, jnp.max(s, axis=0, keepdims=True))
                alpha = jnp.exp2(m_prev - m_new)
                p = jnp.exp2(s - m_new).astype(BF16)
                v_aug = _with_ones_lane(kv_ref[:, 128:256].astype(BF16), hh)
                acc_s[hh] = acc_s[hh] * alpha + _dot_tn(v_aug, p)
                m_s[hh:hh + 1, :] = m_new

        @pl.when(j < i)
        def _():
            step(False)

        @pl.when(j == i)
        def _():
            step(True)
            a0, a1 = acc_s[0], acc_s[1]
            l0, l1 = a0[64:65, :], a1[0:1, :]
            o_t = jnp.concatenate([a0[0:64, :] / l0, a1[64:128, :] / l1], axis=0)
            o_ref[...] = o_t.T.astype(o_ref.dtype)
            lse_ref[0, 0] = jnp.concatenate(
                [m_s[0:1, :] + jnp.log2(l0), m_s[1:2, :] + jnp.log2(l1), jnp.zeros((6, tq), F32)], axis=0)

    return pl.pallas_call(
        body, name="fox_fwd",
        grid_spec=pltpu.PrefetchScalarGridSpec(
            num_scalar_prefetch=2, grid=(nbatch, 4, qi.shape[0]),
            in_specs=[pl.BlockSpec((tq, 128), lambda b, p, s, qi, kj: (b * nq + qi[s], COL_BQ // 128 + p)),
                      pl.BlockSpec((tk, 256), lambda b, p, s, qi, kj: (b * nq + kj[s], COL_KV // 256 + p)),
                      pl.BlockSpec((1, tk, 128), lambda b, p, s, qi, kj: (p, b * nq + kj[s], 0))],
            out_specs=[pl.BlockSpec((tq, 128), lambda b, p, s, qi, kj: (b * nq + qi[s], p)),
                       pl.BlockSpec((1, 1, 8, tq), lambda b, p, s, qi, kj: (b, p, 0, qi[s]))],
            scratch_shapes=[pltpu.VMEM((8, tq), F32), pltpu.VMEM((2, 128, tq), F32)]),
        out_shape=[jax.ShapeDtypeStruct((t, 512), BF16), jax.ShapeDtypeStruct((nbatch, 4, 8, seq), F32)],
        compiler_params=_params(("parallel", "parallel", "arbitrary"), 48),
    )(qi, kj, proj, proj, cum_cols)


def _fox_bwd(proj, cum_cols, lse, yb, dyb, dproj, nbatch, seq):
    t = proj.shape[0]
    tq = tk = min(512, seq)
    nq = seq // tq
    scale = BDH ** -0.5
    qi, kj = _causal_pairs(nq, key_major=True)
    nsteps = qi.shape[0]

    def body(qi_ref, kj_ref, q_ref, kv_ref, cc_ref, lse_ref, o_ref, do_ref, dp_in,
             dkv_ref, dq_ref, drs_ref, dcs_ref, dkv_s, dqa_s, dcs_s):
        del dp_in
        hp, s_id = pl.program_id(1), pl.program_id(2)
        i, j = qi_ref[s_id], kj_ref[s_id]

        @pl.when(i == j)
        def _():
            dkv_s[...] = jnp.zeros_like(dkv_s)
            dcs_s[...] = jnp.zeros_like(dcs_s)

        @pl.when(s_id == 0)
        def _():
            dqa_s[...] = jnp.zeros_like(dqa_s)

        def step(masked):
            lane = lax.broadcasted_iota(jnp.int32, (tq, 128), 1)
            for hh in range(2):
                s, _ = _fox_scores_t(q_ref, kv_ref, cc_ref, hh, masked, tq, tk)
                p = jnp.exp2(s - lse_ref[0, 0, hh:hh + 1, :])
                doh = do_ref[:, 64 * hh:64 * hh + 64]
                dd = lax.dot_general(jnp.ones((8, 64), F32), doh * o_ref[:, 64 * hh:64 * hh + 64].astype(F32),
                                     (((1,), (1,)), ((), ())), preferred_element_type=F32, precision=HIGHEST)[0:1, :]
                doh = doh.astype(BF16)
                dp = _dot_nt(kv_ref[:, 128 + 64 * hh:192 + 64 * hh], doh)
                ds = (p * (dp - dd)).astype(BF16)
                dkv_s[:, 128 + 64 * hh:192 + 64 * hh] += _dot(p, doh)
                dkv_s[:, 64 * hh:64 * hh + 64] += _dot(ds, q_ref[:, 64 * hh:64 * hh + 64] * scale)
                k_aug = _with_ones_lane(kv_ref[:, 0:128].astype(BF16), hh)
                dqa_s[i, hh] += _dot_tn(k_aug, ds)
                sel = jnp.where(lane == 2 * hp + hh, 1.0, 0.0).astype(BF16)
                dcs_s[...] += _dot(ds, sel)

        @pl.when(i == j)
        def _():
            step(True)

        @pl.when(i > j)
        def _():
            step(False)

        @pl.when(i == nq - 1)
        def _():
            dkv_ref[...] = dkv_s[...].astype(dkv_ref.dtype)
            dcs_ref[0] = dcs_s[...]

        @pl.when(s_id == nsteps - 1)
        def _():
            lane = lax.broadcasted_iota(jnp.int32, (tq, 128), 1)
            for blk in range(nq):
                a0 = dqa_s[blk, 0].T
                a1 = dqa_s[blk, 1].T
                rows = pl.ds(blk * tq, tq)
                dq_ref[rows, :] = (jnp.where(lane < 64, a0, a1) * scale).astype(dq_ref.dtype)
                drs_ref[0, rows, :] = jnp.where(lane == 2 * hp, a0[:, 64:65], jnp.where(lane == 2 * hp + 1, a1[:, 0:1], 0.0))

    return pl.pallas_call(
        body, name="fox_bwd",
        grid_spec=pltpu.PrefetchScalarGridSpec(
            num_scalar_prefetch=2, grid=(nbatch, 4, nsteps),
            in_specs=[pl.BlockSpec((tq, 128), lambda b, p, s, qi, kj: (b * nq + qi[s], COL_BQ // 128 + p)),
                      pl.BlockSpec((tk, 256), lambda b, p, s, qi, kj: (b * nq + kj[s], COL_KV // 256 + p)),
                      pl.BlockSpec((1, tk, 128), lambda b, p, s, qi, kj: (p, b * nq + kj[s], 0)),
                      pl.BlockSpec((1, 1, 8, tq), lambda b, p, s, qi, kj: (b, p, 0, qi[s])),
                      pl.BlockSpec((tq, 128), lambda b, p, s, qi, kj: (b * nq + qi[s], p)),
                      pl.BlockSpec((tq, 128), lambda b, p, s, qi, kj: (b * nq + qi[s], p)),
                      pl.BlockSpec(memory_space=pl.ANY)],
            out_specs=[pl.BlockSpec((tk, 256), lambda b, p, s, qi, kj: (b * nq + kj[s], COL_KV // 256 + p)),
                       pl.BlockSpec((seq, 128), lambda b, p, s, qi, kj: (b, p)),
                       pl.BlockSpec((1, seq, 128), lambda b, p, s, qi, kj: (p, b, 0)),
                       pl.BlockSpec((1, tk, 128), lambda b, p, s, qi, kj: (p, b * nq + kj[s], 0))],
            scratch_shapes=[pltpu.VMEM((tk, 256), F32), pltpu.VMEM((nq, 2, 128, tq), F32),
                            pltpu.VMEM((tk, 128), F32)]),
        out_shape=[jax.ShapeDtypeStruct((t, NP), BF16), jax.ShapeDtypeStruct((t, 512), BF16),
                   jax.ShapeDtypeStruct((4, t, 128), F32), jax.ShapeDtypeStruct((4, t, 128), F32)],
        input_output_aliases={8: 0},
        compiler_params=_params(("parallel", "parallel", "arbitrary"), 56),
    )(qi, kj, proj, proj, cum_cols, lse, yb, dyb, dproj)


def _place_cols(dproj, src, col):
    t, w = src.shape
    tm = 1024 if t % 1024 == 0 else t

    def body(s_ref, dp_in, o_ref):
        del dp_in
        o_ref[...] = s_ref[...]

    return pl.pallas_call(
        body, name="place_cols", grid=(t // tm,),
        in_specs=[pl.BlockSpec((tm, w), lambda i: (i, 0)), pl.BlockSpec(memory_space=pl.ANY)],
        out_specs=pl.BlockSpec((tm, w), lambda i: (i, col // w)),
        out_shape=jax.ShapeDtypeStruct(dproj.shape, dproj.dtype),
        input_output_aliases={1: 0},
        compiler_params=_params(("parallel",)),
    )(src, dproj)


def _fox_dbf(proj, bias128, drs, dcs, dproj, nbatch, seq):
    t = proj.shape[0]
    ts = min(512, seq)
    nb = seq // ts

    def body(p_ref, b_ref, dr_ref, dc_ref, dp_in, dp_ref, sm_ref, carry):
        del dp_in
        b_id, i = pl.program_id(0), pl.program_id(1)

        @pl.when(i == 0)
        def _():
            carry[...] = jnp.zeros_like(carry)

        dcum = (dr_ref[0] - dc_ref[0]) + (dr_ref[1] - dc_ref[1]) + (dr_ref[2] - dc_ref[2]) + (dr_ref[3] - dc_ref[3])
        rc = _dot_f32(_tri(ts, True), dcum) + carry[...]
        carry[...] = rc[0:1, :]
        z = p_ref[...] + b_ref[...]
        lane = lax.broadcasted_iota(jnp.int32, (ts, 128), 1)
        dz = jnp.where(lane < BH, rc * jax.nn.sigmoid(-z), 0.0)
        dp_ref[...] = dz.astype(dp_ref.dtype)
        upd = jnp.concatenate([jnp.sum(dz, axis=0, keepdims=True), jnp.zeros((7, 128), F32)], axis=0)
        first = (b_id == 0) & (i == 0)

        @pl.when(first)
        def _():
            sm_ref[...] = upd

        @pl.when(jnp.logical_not(first))
        def _():
            sm_ref[...] += upd

    def rows(b, i):
        return b * nb + (nb - 1 - i)

    return pl.pallas_call(
        body, name="fox_dbf", grid=(nbatch, nb),
        in_specs=[pl.BlockSpec((ts, 128), lambda b, i: (rows(b, i), 0)),
                  pl.BlockSpec((1, 128), lambda b, i: (0, 0)),
                  pl.BlockSpec((4, ts, 128), lambda b, i: (0, rows(b, i), 0)),
                  pl.BlockSpec((4, ts, 128), lambda b, i: (0, rows(b, i), 0)),
                  pl.BlockSpec(memory_space=pl.ANY)],
        out_specs=[pl.BlockSpec((ts, 128), lambda b, i: (rows(b, i), COL_BF // 128)),
                   pl.BlockSpec((8, 128), lambda b, i: (0, 0))],
        out_shape=[jax.ShapeDtypeStruct((t, NP), BF16), jax.ShapeDtypeStruct((8, 128), F32)],
        input_output_aliases={4: 0},
        scratch_shapes=[pltpu.VMEM((1, 128), F32)],
        compiler_params=_params(("arbitrary", "arbitrary")),
    )(proj, bias128, drs, dcs, dproj)


def _ln_stats(z):
    mu = jnp.mean(z, axis=-1, keepdims=True)
    zc = z - mu
    rstd = lax.rsqrt(jnp.mean(zc * zc, axis=-1, keepdims=True) + LN_EPS)
    return zc * rstd, rstd


def _ln_bwd(dy, xhat, rstd, w):
    dxh = dy * w
    return rstd * (dxh - jnp.mean(dxh, axis=-1, keepdims=True) - xhat * jnp.mean(dxh * xhat, axis=-1, keepdims=True))


def _merge_fwd(ya, yb, proj, x2, mod8, wba, wbb, wout, ln1w, ln1b, seq):
    t = x2.shape[0]
    tm = min(256, seq)
    tpb = seq // tm

    def body(ya_ref, yb_ref, g_ref, x_ref, mod_ref, wa_ref, wb_ref, wo_ref, lw_ref, lb_ref, mg_ref, u_ref, x1_ref):
        ga = jax.nn.sigmoid(g_ref[:, 0:D].astype(F32))
        gb = jax.nn.sigmoid(g_ref[:, D:2 * D].astype(F32))
        merged = (ga * jnp.dot(ya_ref[...], wa_ref[...], preferred_element_type=F32)
                  + gb * jnp.dot(yb_ref[...], wb_ref[...], preferred_element_type=F32))
        mg = merged.astype(BF16)
        mg_ref[...] = mg
        u = jnp.dot(mg, wo_ref[...], preferred_element_type=F32)
        u_ref[...] = u
        xhat, _ = _ln_stats(ALPHA * x_ref[...] + (1.0 + mod_ref[0, 2:3, :]) * u)
        x1_ref[...] = xhat * lw_ref[...] + lb_ref[...]

    tok = lambda w: pl.BlockSpec((tm, w), lambda i: (i, 0))
    full = lambda a: pl.BlockSpec(a.shape, lambda i: (0,) * a.ndim)
    return pl.pallas_call(
        body, name="merge_fwd", grid=(t // tm,),
        in_specs=[tok(512), tok(512), pl.BlockSpec((tm, 2048), lambda i: (i, COL_GATES // 2048)), tok(D),
                  pl.BlockSpec((1, 8, D), lambda i: (i // tpb, 0, 0)),
                  full(wba), full(wbb), full(wout), full(ln1w), full(ln1b)],
        out_specs=[tok(D), tok(D), tok(D)],
        out_shape=[jax.ShapeDtypeStruct((t, D), BF16), jax.ShapeDtypeStruct((t, D), F32),
                   jax.ShapeDtypeStruct((t, D), F32)],
        compiler_params=_params(("parallel",), 48),
    )(ya, yb, proj, x2, mod8, wba, wbb, wout, ln1w, ln1b)


def _merge_bwd(du, ya, yb, proj, wba, wbb, wout, token, seq):
    t = du.shape[0]
    tm = min(256, seq)

    def body(du_ref, ya_ref, yb_ref, g_ref, wa_ref, wb_ref, wo_ref, token_ref,
             dp_ref, dpa_ref, dpb_ref, dya_ref, dyb_ref):
        del token_ref
        ga = jax.nn.sigmoid(g_ref[:, 0:D].astype(F32))
        gb = jax.nn.sigmoid(g_ref[:, D:2 * D].astype(F32))
        dm = _dot_nt(du_ref[...], wo_ref[...])
        pa = jnp.dot(ya_ref[...], wa_ref[...], preferred_element_type=F32)
        pb = jnp.dot(yb_ref[...], wb_ref[...], preferred_element_type=F32)
        dpa = (dm * ga).astype(BF16)
        dpb = (dm * gb).astype(BF16)
        dpa_ref[...] = dpa
        dpb_ref[...] = dpb
        dp_ref[:, 0:D] = (dm * pa * ga * (1.0 - ga)).astype(BF16)
        dp_ref[:, D:2 * D] = (dm * pb * gb * (1.0 - gb)).astype(BF16)
        dya_ref[...] = _dot_nt(dpa, wa_ref[...])
        dyb_ref[...] = _dot_nt(dpb, wb_ref[...])

    tok = lambda w: pl.BlockSpec((tm, w), lambda i: (i, 0))
    full = lambda a: pl.BlockSpec(a.shape, lambda i: (0,) * a.ndim)
    return pl.pallas_call(
        body, name="merge_bwd", grid=(t // tm,),
        in_specs=[tok(D), tok(512), tok(512), pl.BlockSpec((tm, 2048), lambda i: (i, COL_GATES // 2048)),
                  full(wba), full(wbb), full(wout), full(token)],
        out_specs=[pl.BlockSpec((tm, 2048), lambda i: (i, COL_GATES // 2048)), tok(D), tok(D), tok(512), tok(512)],
        out_shape=[jax.ShapeDtypeStruct((t, NP), BF16), jax.ShapeDtypeStruct((t, D), BF16),
                   jax.ShapeDtypeStruct((t, D), BF16), jax.ShapeDtypeStruct((t, 512), F32),
                   jax.ShapeDtypeStruct((t, 512), F32)],
        compiler_params=_params(("parallel",), 48),
    )(du, ya, yb, proj, wba, wbb, wout, token)


def _ffn_fwd(x1, mod8, wg, wu, wd, target, ln2w, ln2b, seq):
    t = x1.shape[0]
    tm = min(512, seq)
    nf, _, tf = wg.shape
    tpb = seq // tm
    nbatch = t // seq

    def body(x_ref, mod_ref, wg_ref, wu_ref, wd_ref, t_ref, lw_ref, lb_ref,
             a_ref, b_ref, dz_ref, st_ref, dm_ref, h_s, acc):
        i, j = pl.program_id(0), pl.program_id(1)

        @pl.when(j == 0)
        def _():
            h_s[...] = (x_ref[...] * (1.0 + mod_ref[0, 4:5, :]) + mod_ref[0, 3:4, :]).astype(BF16)
            acc[...] = jnp.zeros_like(acc)

        a = jnp.dot(h_s[...], wg_ref[0], preferred_element_type=F32)
        b = jnp.dot(h_s[...], wu_ref[0], preferred_element_type=F32)
        a_ref[0] = a.astype(BF16)
        b_ref[0] = b.astype(BF16)
        acc[...] += _dot(a * jax.nn.sigmoid(a) * b, wd_ref[0])

        @pl.when(j == nf - 1)
        def _():
            ffn = acc[...]
            xhat, rstd = _ln_stats(ALPHA * x_ref[...] + (1.0 + mod_ref[0, 5:6, :]) * ffn)
            diff = xhat * lw_ref[...] + lb_ref[...] - t_ref[...]
            loss = 0.5 * jnp.sum(jnp.sum(diff * diff, axis=-1, keepdims=True), axis=0, keepdims=True) / D
            dy = diff * (1.0 / D)
            dz = _ln_bwd(dy, xhat, rstd, lw_ref[...])
            dz_ref[...] = dz
            lane = lax.broadcasted_iota(jnp.int32, (1, D), 1)
            upd = jnp.concatenate(
                [jnp.sum(dy * xhat, axis=0, keepdims=True), jnp.sum(dy, axis=0, keepdims=True),
                 jnp.where(lane == 0, loss, 0.0), jnp.zeros((5, D), F32)], axis=0)
            dmu = jnp.concatenate(
                [jnp.zeros((5, D), F32), jnp.sum(dz * ffn, axis=0, keepdims=True), jnp.zeros((2, D), F32)], axis=0)

            @pl.when(i == 0)
            def _():
                st_ref[...] = upd

            @pl.when(i > 0)
            def _():
                st_ref[...] += upd

            @pl.when(i % tpb == 0)
            def _():
                dm_ref[0] = dmu

            @pl.when(i % tpb != 0)
            def _():
                dm_ref[0] += dmu

    row = lambda: pl.BlockSpec((tm, D), lambda i, j: (i, 0))
    vec = lambda: pl.BlockSpec((1, D), lambda i, j: (0, 0))
    return pl.pallas_call(
        body, name="ffn_fwd", grid=(t // tm, nf),
        in_specs=[row(), pl.BlockSpec((1, 8, D), lambda i, j: (i // tpb, 0, 0)),
                  pl.BlockSpec((1, D, tf), lambda i, j: (j, 0, 0)), pl.BlockSpec((1, D, tf), lambda i, j: (j, 0, 0)),
                  pl.BlockSpec((1, tf, D), lambda i, j: (j, 0, 0)), row(), vec(), vec()],
        out_specs=[pl.BlockSpec((1, tm, tf), lambda i, j: (j, i, 0)), pl.BlockSpec((1, tm, tf), lambda i, j: (j, i, 0)),
                   row(), pl.BlockSpec((8, D), lambda i, j: (0, 0)),
                   pl.BlockSpec((1, 8, D), lambda i, j: (i // tpb, 0, 0))],
        out_shape=[jax.ShapeDtypeStruct((nf, t, tf), BF16), jax.ShapeDtypeStruct((nf, t, tf), BF16),
                   jax.ShapeDtypeStruct((t, D), F32), jax.ShapeDtypeStruct((8, D), F32),
                   jax.ShapeDtypeStruct((nbatch, 8, D), F32)],
        scratch_shapes=[pltpu.VMEM((tm, D), BF16), pltpu.VMEM((tm, D), F32)],
        compiler_params=_params(("arbitrary", "arbitrary"), 48),
    )(x1, mod8, wg, wu, wd, target, ln2w, ln2b)


def _ffn_bwd(dz2, a, b, wg, wu, wd, x1, x2, u, mod8, ln1w, seq):
    t = x1.shape[0]
    tm = min(512, seq)
    nf, tf, _ = wg.shape
    tpb = seq // tm
    nbatch = t // seq

    def body(dz_ref, a_ref, b_ref, wg_ref, wu_ref, wd_ref, x1_ref, x_ref, u_ref, mod_ref, lw_ref,
             da_ref, db_ref, hm_ref, df_ref, du_ref, dxp_ref, st_ref, dm_ref, acc):
        i, j = pl.program_id(0), pl.program_id(1)

        @pl.when(j == 0)
        def _():
            df_ref[...] = ((1.0 + mod_ref[0, 5:6, :]) * dz_ref[...]).astype(BF16)
            acc[...] = jnp.zeros_like(acc)

        dhm = _dot(df_ref[...], wd_ref[0])
        av = a_ref[0].astype(F32)
        bv = b_ref[0].astype(F32)
        sg = jax.nn.sigmoid(av)
        sl = av * sg
        hm_ref[0] = (sl * bv).astype(BF16)
        da = (dhm * bv * (sg * (1.0 + av * (1.0 - sg)))).astype(BF16)
        db = (dhm * sl).astype(BF16)
        da_ref[0] = da
        db_ref[0] = db
        acc[...] += _dot(da, wg_ref[0]) + _dot(db, wu_ref[0])

        @pl.when(j == nf - 1)
        def _():
            dh2 = acc[...]
            x1v = x1_ref[...]
            uv = u_ref[...]
            dx1 = ALPHA * dz_ref[...] + dh2 * (1.0 + mod_ref[0, 4:5, :])
            xhat, rstd = _ln_stats(ALPHA * x_ref[...] + (1.0 + mod_ref[0, 2:3, :]) * uv)
            dz1 = _ln_bwd(dx1, xhat, rstd, lw_ref[...])
            du_ref[...] = ((1.0 + mod_ref[0, 2:3, :]) * dz1).astype(BF16)
            dxp_ref[...] = ALPHA * dz1
            upd = jnp.concatenate(
                [jnp.sum(dx1 * xhat, axis=0, keepdims=True), jnp.sum(dx1, axis=0, keepdims=True),
                 jnp.zeros((6, D), F32)], axis=0)
            dmu = jnp.concatenate(
                [jnp.zeros((2, D), F32), jnp.sum(dz1 * uv, axis=0, keepdims=True),
                 jnp.sum(dh2, axis=0, keepdims=True), jnp.sum(dh2 * x1v, axis=0, keepdims=True),
                 jnp.zeros((3, D), F32)], axis=0)

            @pl.when(i == 0)
            def _():
                st_ref[...] = upd

            @pl.when(i > 0)
            def _():
                st_ref[...] += upd

            @pl.when(i % tpb == 0)
            def _():
                dm_ref[0] = dmu

            @pl.when(i % tpb != 0)
            def _():
                dm_ref[0] += dmu

    row = lambda: pl.BlockSpec((tm, D), lambda i, j: (i, 0))
    ffb = lambda: pl.BlockSpec((1, tm, tf), lambda i, j: (j, i, 0))
    return pl.pallas_call(
        body, name="ffn_bwd", grid=(t // tm, nf),
        in_specs=[row(), ffb(), ffb(),
                  pl.BlockSpec((1, tf, D), lambda i, j: (j, 0, 0)), pl.BlockSpec((1, tf, D), lambda i, j: (j, 0, 0)),
                  pl.BlockSpec((1, D, tf), lambda i, j: (j, 0, 0)), row(), row(), row(),
                  pl.BlockSpec((1, 8, D), lambda i, j: (i // tpb, 0, 0)), pl.BlockSpec((1, D), lambda i, j: (0, 0))],
        out_specs=[ffb(), ffb(), ffb(), row(), row(), row(), pl.BlockSpec((8, D), lambda i, j: (0, 0)),
                   pl.BlockSpec((1, 8, D), lambda i, j: (i // tpb, 0, 0))],
        out_shape=[jax.ShapeDtypeStruct((nf, t, tf), BF16), jax.ShapeDtypeStruct((nf, t, tf), BF16),
                   jax.ShapeDtypeStruct((nf, t, tf), BF16), jax.ShapeDtypeStruct((t, D), BF16),
                   jax.ShapeDtypeStruct((t, D), BF16), jax.ShapeDtypeStruct((t, D), F32),
                   jax.ShapeDtypeStruct((8, D), F32), jax.ShapeDtypeStruct((nbatch, 8, D), F32)],
        scratch_shapes=[pltpu.VMEM((tm, D), F32)],
        compiler_params=_params(("arbitrary", "arbitrary"), 48),
    )(dz2, a, b, wg, wu, wd, x1, x2, u, mod8, ln1w)


def _adamw_math(w, g, m, v):
    m = B1 * m + (1.0 - B1) * g
    v = B2 * v + (1.0 - B2) * (g * g)
    m_hat = m / (1.0 - B1 ** STEP)
    v_hat = v / (1.0 - B2 ** STEP)
    return -LR * (m_hat / (jnp.sqrt(v_hat) + EPS) + WD * w), m, v


def _adamw(w, g, m, v, name):
    rows, cols = w.shape
    tr = rows
    for cand in (128, 64, 32, 16, 8):
        if rows % cand == 0:
            tr = cand
            break

    def body(w_ref, g_ref, m_ref, v_ref, d_ref, mo_ref, vo_ref):
        d, mn, vn = _adamw_math(w_ref[...], g_ref[...], m_ref[...], v_ref[...])
        d_ref[...] = d
        mo_ref[...] = mn
        vo_ref[...] = vn

    spec = pl.BlockSpec((tr, cols), lambda i: (i, 0))
    return pl.pallas_call(
        body, name=name, grid=(rows // tr,), in_specs=[spec] * 4, out_specs=[spec] * 3,
        out_shape=[jax.ShapeDtypeStruct((rows, cols), F32)] * 3,
        compiler_params=_params(("parallel",), 48),
    )(w, g, m, v)


def _adamw_halves(w, g_mine, g_sib, m, v, c_idx, name):
    rows, cols = w.shape
    hr = rows // 2
    tr = next(cand for cand in (128, 88, 64, 32, 16, 8) if hr % cand == 0)
    tph = hr // tr

    def body(c_ref, w_ref, gm_ref, gs_ref, m_ref, v_ref, g_ref, d_ref, mo_ref, vo_ref):
        g = jnp.where(pl.program_id(0) == c_ref[0], gm_ref[...], gs_ref[...])
        d, mn, vn = _adamw_math(w_ref[...], g, m_ref[...], v_ref[...])
        g_ref[...] = g
        d_ref[...] = d
        mo_ref[...] = mn
        vo_ref[...] = vn

    full = pl.BlockSpec((tr, cols), lambda h, i, c: (h * tph + i, 0))
    half = pl.BlockSpec((tr, cols), lambda h, i, c: (i, 0))
    return pl.pallas_call(
        body, name=name,
        grid_spec=pltpu.PrefetchScalarGridSpec(
            num_scalar_prefetch=1, grid=(2, tph), in_specs=[full, half, half, full, full], out_specs=[full] * 4),
        out_shape=[jax.ShapeDtypeStruct((rows, cols), F32)] * 4,
        compiler_params=_params(("parallel", "parallel"), 48),
    )(c_idx, w, g_mine, g_sib, m, v)


def _grad_w_ada(c_all, dmod_cols):
    def body(c_ref, d_ref, o_ref):
        c = c_ref[...]
        o_ref[...] = lax.dot_general(c * jax.nn.sigmoid(c), d_ref[...], (((0,), (0,)), ((), ())),
                                     preferred_element_type=F32, precision=HIGHEST)

    return pl.pallas_call(
        body, name="grad_w_ada", out_shape=jax.ShapeDtypeStruct((D, dmod_cols.shape[1]), F32),
        compiler_params=_params(vmem_mb=48),
    )(c_all, dmod_cols)


def _small_update(gath, w8, m8, v8):
    def body(g_ref, w_ref, m_ref, v_ref, go_ref, d_ref, mo_ref, vo_ref):
        g0 = g_ref[0, 0:1, :] + g_ref[0, 1:2, :]
        g1 = g_ref[0, 2:3, :]
        for dev in range(1, N_DEV):
            g0 = g0 + (g_ref[dev, 0:1, :] + g_ref[dev, 1:2, :])
            g1 = g1 + g_ref[dev, 2:3, :]
        w = w_ref[...]
        lb = jax.nn.sigmoid(w[1:2, O_LB0:O_LB1] - w[1:2, O_LB1:O_FOX])
        fac = lb * (1.0 - lb)
        g1 = jnp.concatenate([g1[:, :O_LB0], g1[:, O_LB0:O_LB1] * fac, -g1[:, O_LB1:O_FOX] * fac, g1[:, O_FOX:]],
                             axis=1)
        g = jnp.concatenate([g0, g1, jnp.zeros((6, SMALL_W), F32)], axis=0)
        d, mn, vn = _adamw_math(w, g, m_ref[...], v_ref[...])
        go_ref[...] = g
        d_ref[...] = d
        mo_ref[...] = mn
        vo_ref[...] = vn

    return pl.pallas_call(
        body, name="small_update", out_shape=[jax.ShapeDtypeStruct((8, SMALL_W), F32)] * 4,
        compiler_params=_params(vmem_mb=48),
    )(gath, w8, m8, v8)


def _pack_small(b_ada, ln1w, ln1b, ln2w, ln2b, norm_w, lb_logits, fox):
    row1 = jnp.concatenate([ln1w, ln1b, ln2w, ln2b, norm_w, lb_logits[0:1], lb_logits[1:2], fox,
                            jnp.zeros((1, SMALL_W - O_FOX - BH), F32)], axis=1)
    return jnp.concatenate([b_ada, row1, jnp.zeros((6, SMALL_W), F32)], axis=0)


def _unpack_small(p):
    r = p[1:2]
    lb = jnp.concatenate([r[:, O_LB0:O_LB1], r[:, O_LB1:O_FOX]], axis=0)
    return dict(b_ada=p[0:1], ln1_w=r[:, O_LN1W:O_LN1B], ln1_b=r[:, O_LN1B:O_LN2W], ln2_w=r[:, O_LN2W:O_LN2B],
                ln2_b=r[:, O_LN2B:O_NORM], hgrn_norm_w=r[:, O_NORM:O_LB0], lb_logits=lb,
                fox_f_bias=r[:, O_FOX:O_FOX + BH])


_BIG = ("w_in", "w_branch_a", "w_branch_b", "w_out", "w_ffn_gate", "w_ffn_up", "w_ffn_down")


def _cols_of_chips(stacked):
    return jnp.concatenate([stacked[k] for k in range(N_CHIPS)], axis=1)


def kernel(x, c, w_ada, b_ada, w_in, fox_f_bias, lb_logits, hgrn_norm_w, w_branch_a, w_branch_b, w_out, ln1_w, ln1_b, w_ffn_gate, w_ffn_up, w_ffn_down, ln2_w, ln2_b, loss_target, m_w_ada, m_b_ada, m_w_in, m_fox_f_bias, m_lb_logits, m_hgrn_norm_w, m_w_branch_a, m_w_branch_b, m_w_out, m_ln1_w, m_ln1_b, m_w_ffn_gate, m_w_ffn_up, m_w_ffn_down, m_ln2_w, m_ln2_b, v_w_ada, v_b_ada, v_w_in, v_fox_f_bias, v_lb_logits, v_hgrn_norm_w, v_w_branch_a, v_w_branch_b, v_w_out, v_ln1_w, v_ln1_b, v_w_ffn_gate, v_w_ffn_up, v_w_ffn_down, v_ln2_w, v_ln2_b):
    nbatch, seq, _ = x.shape
    t = nbatch * seq
    ax, ay, ac = lax.axis_index("x"), lax.axis_index("y"), lax.axis_index("c")
    chip = 2 * ax + ay
    dev = 2 * chip + ac
    chip_arr = jnp.reshape(chip, (1,)).astype(jnp.int32)
    core_arr = jnp.reshape(ac, (1,)).astype(jnp.int32)

    shard_w = dict(w_in=w_in[0], w_branch_a=w_branch_a[0], w_branch_b=w_branch_b[0], w_out=w_out[0],
                   w_ffn_gate=w_ffn_gate[0], w_ffn_up=w_ffn_up[0], w_ffn_down=w_ffn_down[0])
    shard_m = dict(w_in=m_w_in[0], w_branch_a=m_w_branch_a[0], w_branch_b=m_w_branch_b[0], w_out=m_w_out[0],
                   w_ffn_gate=m_w_ffn_gate[0], w_ffn_up=m_w_ffn_up[0], w_ffn_down=m_w_ffn_down[0])
    shard_v = dict(w_in=v_w_in[0], w_branch_a=v_w_branch_a[0], w_branch_b=v_w_branch_b[0], w_out=v_w_out[0],
                   w_ffn_gate=v_w_ffn_gate[0], w_ffn_up=v_w_ffn_up[0], w_ffn_down=v_w_ffn_down[0])

    shard16 = {n: shard_w[n].astype(BF16) for n in _BIG}

    def with_mine(gathered, n):
        return lax.dynamic_update_slice(gathered, shard16[n][None], (chip, 0, 0))

    w_p = _permute_cols(_cols_of_chips(with_mine(_gather_weights([shard16["w_in"]])[0], "w_in")))
    late = _BIG[1:]
    late_send, late_recv, late_src, late_land, late_token = _split_start(
        _gather_copies, [shard16[n] for n in late],
        [lax.empty((N_CHIPS,) + shard16[n].shape, BF16) for n in late], "gather_late_start")

    c8 = jnp.concatenate([c, jnp.zeros((8 - nbatch, D), F32)], axis=0)
    c_all = _allgather8(c8, "gather_c")[:, :nbatch, :].reshape(N_DEV * nbatch, D)
    ncol = w_ada.shape[2]
    b_cols = lax.dynamic_slice_in_dim(b_ada, chip * ncol, ncol, axis=1)
    mod_g = _allgather8(_mod_shard(c_all, w_ada[0], b_cols), "gather_mod")
    mod_all = jnp.concatenate([mod_g[2 * k] for k in range(N_CHIPS)], axis=1)
    mod_mine = lax.dynamic_slice_in_dim(mod_all, dev * nbatch, nbatch, axis=0)
    mod8 = jnp.concatenate([mod_mine.reshape(nbatch, 6, D), jnp.zeros((nbatch, 2, D), F32)], axis=1)
    mod8 = mod8 + late_token[0, 0]

    x2 = x.reshape(t, D)
    tgt2 = loss_target.reshape(t, D)
    bias128 = jnp.concatenate([fox_f_bias, jnp.zeros((1, 128 - BH), F32)], axis=1)

    proj = _proj(x2, mod8, w_p, seq, BF16, "proj")
    projf = _proj(x2, mod8, w_p[:, COL_BF:], seq, F32, "proj_forget")
    ya, ckpt = _hgrn_fwd(proj, lb_logits, hgrn_norm_w, nbatch, seq)
    cum_cols = _fox_cum(projf, bias128, nbatch, seq)
    yb, lse = _fox_fwd(proj, cum_cols, nbatch, seq)
    late_land = _pass_to_sibling(
        _split_wait(_gather_copies, late_send, late_recv, late_src, late_land, yb, "gather_late_wait"))
    full = {n: with_mine(g, n) for n, g in zip(late, late_land)}
    wba, wbb = _cols_of_chips(full["w_branch_a"]), _cols_of_chips(full["w_branch_b"])
    wout = full["w_out"].reshape(D, D)
    wg, wu, wd = full["w_ffn_gate"], full["w_ffn_up"], full["w_ffn_down"]
    merged, u, x1 = _merge_fwd(ya, yb, proj, x2, mod8, wba, wbb, wout, ln1_w, ln1_b, seq)
    a_pre, b_pre, dz2, st2, dm2 = _ffn_fwd(x1, mod8, wg, wu, wd, tgt2, ln2_w, ln2_b, seq)
    loss = lax.psum(st2[2, 0], ("x", "y", "c"))

    da, db, hmid, dffn, du, dxp, st1, dm1 = _ffn_bwd(
        dz2, a_pre, b_pre, jnp.swapaxes(wg, 1, 2), jnp.swapaxes(wu, 1, 2), jnp.swapaxes(wd, 1, 2),
        x1, x2, u, mod8, ln1_w, seq)
    g_st = {}
    g_st["w_ffn_down"] = _tn_matmul(hmid, dffn, "dw_ffn_down", seq)
    g_st["w_ffn_gate"] = _tn_matmul(x1, da, "dw_ffn_gate", seq, mod8, (3, 4))
    g_st["w_ffn_up"] = _tn_matmul(x1, db, "dw_ffn_up", seq, mod8, (3, 4))
    g_st["w_out"] = _tn_matmul(merged, du, "dw_out", seq).reshape(N_CHIPS, D // N_CHIPS, D)

    def sum_over_cores(names, tag):
        g_list = [g_st[n] for n in names]
        return [_add_my_half(g, o, core_arr, "grad_add_halves_" + n)
                for n, g, o in zip(names, g_list, _swap_halves(g_list, "grad_swap_halves_" + tag))]

    early = ("w_ffn_down", "w_ffn_gate", "w_ffn_up", "w_out")
    e_halves = sum_over_cores(early, "early")
    e_send, e_recv, e_src, e_land, e_token = _split_start(
        _scatter_copies, [h16 for _, h16 in e_halves],
        [lax.empty((3,) + h16.shape[1:], BF16) for _, h16 in e_halves], "grad_scatter_early_start")
    dproj, dpa, dpb, dya, dyb = _merge_bwd(du, ya, yb, proj, wba, wbb, wout, e_token, seq)
    g_st["w_branch_a"] = _tn_matmul(ya, dpa, "dw_branch_a", seq, split=D // N_CHIPS)
    g_st["w_branch_b"] = _tn_matmul(yb, dpb, "dw_branch_b", seq, split=D // N_CHIPS)
    dproj, dq, drs, dcs = _fox_bwd(proj, cum_cols, lse, yb, dyb, dproj, nbatch, seq)
    dproj = _place_cols(dproj, dq, COL_BQ)
    dproj, sm_fox = _fox_dbf(projf, bias128, drs, dcs, dproj, nbatch, seq)
    dproj, sm_hgrn = _hgrn_bwd(proj, dya, ckpt, lb_logits, hgrn_norm_w, dproj, nbatch, seq)
    grad_x2, dm0 = _dh_kernel(dproj, w_p, x2, dxp, mod8, seq)
    dw_in = _unpermute_cols(_tn_matmul(x2, dproj, "dw_in", seq, mod8, (0, 1)))
    ncin = NIN // N_CHIPS
    g_st["w_in"] = jnp.stack([dw_in[:, k * ncin:(k + 1) * ncin] for k in range(N_CHIPS)])

    e_recv = _split_wait(_scatter_copies, e_send, e_recv, e_src, e_land, dw_in, "grad_scatter_early_wait")
    rest = ("w_in", "w_branch_a", "w_branch_b")
    r_halves = sum_over_cores(rest, "rest")
    r_recv = _scatter_chips([h16 for _, h16 in r_halves], "grad_scatter_rest")
    mine = {n: _add_chips(h32, r, chip_arr, "grad_add_chips_" + n)
            for n, (h32, _), r in zip(early + rest, e_halves + r_halves, list(e_recv) + list(r_recv))}
    g_mine = [mine[n] for n in _BIG]
    g_sib = _join_halves(g_mine)

    dmod = (dm0 + dm1 + dm2)[:, :6, :].reshape(nbatch, 6 * D)
    row2 = jnp.concatenate([st1[0:1], st1[1:2], st2[0:1], st2[1:2], sm_hgrn[1:2], sm_hgrn[0:1], sm_hgrn[0:1],
                            sm_fox[0:1, :BH], jnp.zeros((1, SMALL_W - O_FOX - BH), F32)], axis=1)
    spack = jnp.concatenate([dmod, row2, jnp.zeros((8 - nbatch - 1, SMALL_W), F32)], axis=0)
    gath = _allgather8(spack, "gather_small")
    w8 = _pack_small(b_ada, ln1_w, ln1_b, ln2_w, ln2_b, hgrn_norm_w, lb_logits, fox_f_bias)
    m8 = _pack_small(m_b_ada, m_ln1_w, m_ln1_b, m_ln2_w, m_ln2_b, m_hgrn_norm_w, m_lb_logits, m_fox_f_bias)
    v8 = _pack_small(v_b_ada, v_ln1_w, v_ln1_b, v_ln2_w, v_ln2_b, v_hgrn_norm_w, v_lb_logits, v_fox_f_bias)
    sg, sd, smn, svn = (_unpack_small(p) for p in _small_update(gath, w8, m8, v8))
    dmod_all = gath[:, :nbatch, :].reshape(N_DEV * nbatch, SMALL_W)
    g_ada = _grad_w_ada(c_all, lax.dynamic_slice_in_dim(dmod_all, chip * ncol, ncol, axis=1))

    grads = dict(sg)
    deltas = dict(sd)
    new_m = dict(smn)
    new_v = dict(svn)
    grads["w_ada"] = g_ada
    deltas["w_ada"], new_m["w_ada"], new_v["w_ada"] = _adamw(w_ada[0], g_ada, m_w_ada[0], v_w_ada[0], "adamw_w_ada")
    for n, gm, gs in zip(_BIG, g_mine, g_sib):
        grads[n], deltas[n], new_m[n], new_v[n] = _adamw_halves(shard_w[n], gm, gs, shard_m[n], shard_v[n], core_arr,
                                                                "adamw_" + n)

    names = ["w_ada", "b_ada", "w_in", "fox_f_bias", "lb_logits", "hgrn_norm_w", "w_branch_a", "w_branch_b", "w_out",
             "ln1_w", "ln1_b", "w_ffn_gate", "w_ffn_up", "w_ffn_down", "ln2_w", "ln2_b"]
    shapes = dict(w_ada=w_ada.shape, b_ada=b_ada.shape, w_in=w_in.shape, fox_f_bias=fox_f_bias.shape,
                  lb_logits=lb_logits.shape, hgrn_norm_w=hgrn_norm_w.shape, w_branch_a=w_branch_a.shape,
                  w_branch_b=w_branch_b.shape, w_out=w_out.shape, ln1_w=ln1_w.shape, ln1_b=ln1_b.shape,
                  w_ffn_gate=w_ffn_gate.shape, w_ffn_up=w_ffn_up.shape, w_ffn_down=w_ffn_down.shape,
                  ln2_w=ln2_w.shape, ln2_b=ln2_b.shape)
    outs = [loss, grad_x2.reshape(x.shape)]
    for group in (grads, deltas, new_m, new_v):
        outs += [group[n].reshape(shapes[n]) for n in names]
    return tuple(outs)
```

```python
import functools
import math

import jax
import jax.numpy as jnp
import numpy as np
from jax import lax
from jax.experimental import pallas as pl
from jax.experimental.pallas import tpu as pltpu

F32 = jnp.float32
BF16 = jnp.bfloat16
MESH = pl.DeviceIdType.MESH
HIGHEST = lax.Precision.HIGHEST

D = 1024
AW = 512
AH = 4
ADH = 128
BH = 8
BDH = 64
DFF = 2816
NIN = 5640
NP = 5760
N_CHIPS = 4
N_DEV = 8
HGRN_BLOCK = 256
FFN_TOKENS = 512
COL_GATES = 0
COL_BQ = 2048
COL_KV = 2560
COL_A = 3584
COL_BF = 5632
ALPHA = 2.0 ** 0.25
LN_EPS = 1e-5
RMS_EPS = 1e-6
NEG = -1e30
LOG2E = 1.4426950408889634
LR, B1, B2, EPS, WD, STEP = 0.001, 0.9, 0.999, 1e-08, 0.01, 10
SMALL_W = 6144
O_LN1W, O_LN1B, O_LN2W, O_LN2B, O_NORM, O_LB0, O_LB1, O_FOX = 0, 1024, 2048, 3072, 4096, 4608, 5120, 5632


def _params(sem=None, vmem_mb=None):
    kw = {}
    if sem is not None:
        kw["dimension_semantics"] = sem
    if vmem_mb is not None:
        kw["vmem_limit_bytes"] = vmem_mb << 20
    return pltpu.CompilerParams(**kw)


def _dot(a, b):
    return jnp.dot(a.astype(BF16), b.astype(BF16), preferred_element_type=F32)


def _dot_nt(a, b):
    return lax.dot_general(a.astype(BF16), b.astype(BF16), (((1,), (1,)), ((), ())), preferred_element_type=F32)


def _dot_tn(a, b):
    return lax.dot_general(a.astype(BF16), b.astype(BF16), (((0,), (0,)), ((), ())), preferred_element_type=F32)


def _dot_f32(a, b):
    return jnp.dot(a, b, preferred_element_type=F32, precision=HIGHEST)


def _perm_segments():
    segs = [(3592, 5640), (2048, 2560)]
    for p in range(4):
        segs += [(2560 + 128 * p, 2688 + 128 * p), (3072 + 128 * p, 3200 + 128 * p)]
    for h in range(4):
        segs += [(128 * h + 512 * t, 128 * h + 512 * t + 128) for t in range(4)]
    segs += [(3584, 3592)]
    return segs


def _permute_cols(w):
    parts = [w[:, a:b] for a, b in _perm_segments()]
    parts.append(jnp.zeros((w.shape[0], NP - NIN), w.dtype))
    return jnp.concatenate(parts, axis=1)


def _unpermute_cols(g):
    pos, where = 0, {}
    for a, b in _perm_segments():
        where[a] = (pos, pos + b - a)
        pos += b - a
    parts = [g[:, where[a][0]:where[a][1]] for a in sorted(where)]
    return jnp.concatenate(parts, axis=1)


def _allgather8(v, name):
    rows, cols = v.shape

    def body(x_ref, out_ref, send_sems, recv_sems, local_sem):
        x, y, c = lax.axis_index("x"), lax.axis_index("y"), lax.axis_index("c")
        me, sibling = (x, y, c), (x, y, 1 - c)
        chips = [(1 - x, y), (x, 1 - y), (1 - x, 1 - y)]

        def slot(px, py, pc):
            return out_ref.at[4 * px + 2 * py + pc]

        def copy(k, block, to, src=None):
            return pltpu.make_async_remote_copy(
                src_ref=slot(*block) if src is None else src, dst_ref=slot(*block),
                send_sem=send_sems.at[k], recv_sem=recv_sems.at[k], device_id=to, device_id_type=MESH)

        mine = pltpu.make_async_copy(x_ref, slot(*me), local_sem)
        mine.start()
        first = [copy(0, me, sibling, src=x_ref)]
        first += [copy(1 + j, me, (*chip, c), src=x_ref) for j, chip in enumerate(chips)]
        for cp in first:
            cp.start()
        passed = [copy(4 + j, (*chip, c), sibling) for j, chip in enumerate(chips)]
        for j, chip in enumerate(chips):
            copy(1 + j, (*chip, c), me).wait_recv()
            passed[j].start()
        copy(0, sibling, me).wait_recv()
        for j, chip in enumerate(chips):
            copy(4 + j, (*chip, 1 - c), me).wait_recv()
        for cp in first + passed:
            cp.wait_send()
        mine.wait()

    return pl.pallas_call(
        body, name=name,
        out_shape=jax.ShapeDtypeStruct((N_DEV, rows, cols), v.dtype),
        in_specs=[pl.BlockSpec(memory_space=pltpu.VMEM)],
        out_specs=pl.BlockSpec(memory_space=pltpu.VMEM),
        scratch_shapes=[pltpu.SemaphoreType.DMA((7,)), pltpu.SemaphoreType.DMA((7,)), pltpu.SemaphoreType.DMA],
    )(v)


def _hbm_specs(n):
    return [pl.BlockSpec(memory_space=pl.ANY)] * n


def _gather_weights(shards):
    n = len(shards)

    def body(*refs):
        ins, outs, (send_sems, recv_sems) = refs[:n], refs[n:2 * n], refs[2 * n:]
        x, y, c = lax.axis_index("x"), lax.axis_index("y"), lax.axis_index("c")
        sibling = (x, y, 1 - c)
        chips = [(1 - x, y), (x, 1 - y), (1 - x, 1 - y)]

        def blk(w, px, py, half):
            hr = ins[w].shape[0] // 2
            return outs[w].at[2 * px + py, pl.ds(half * hr, hr), :]

        def copy(w, k, block, to, src=None):
            return pltpu.make_async_remote_copy(
                src_ref=blk(w, *block) if src is None else src, dst_ref=blk(w, *block),
                send_sem=send_sems.at[6 * w + k], recv_sem=recv_sems.at[6 * w + k], device_id=to, device_id_type=MESH)

        first = []
        for w in range(n):
            hr = ins[w].shape[0] // 2
            my_half = ins[w].at[pl.ds(c * hr, hr), :]
            first += [copy(w, j, (x, y, c), (*chip, c), src=my_half) for j, chip in enumerate(chips)]
        for cp in first:
            cp.start()
        passed = []
        for j, chip in enumerate(chips):
            for w in range(n):
                copy(w, j, (*chip, c), (x, y, c)).wait_recv()
                passed.append(copy(w, 3 + j, (*chip, c), sibling))
                passed[-1].start()
        for j, chip in enumerate(chips):
            for w in range(n):
                copy(w, 3 + j, (*chip, 1 - c), (x, y, c)).wait_recv()
        for cp in first + passed:
            cp.wait_send()

    return pl.pallas_call(
        body, name="gather_weights",
        out_shape=[jax.ShapeDtypeStruct((N_CHIPS,) + s.shape, s.dtype) for s in shards],
        in_specs=_hbm_specs(n), out_specs=_hbm_specs(n),
        scratch_shapes=[pltpu.SemaphoreType.DMA((6 * n,)), pltpu.SemaphoreType.DMA((6 * n,))],
    )(*shards)


def _swap_halves(grads, name):
    n = len(grads)

    def body(*refs):
        ins, outs, (send_sems, recv_sems) = refs[:n], refs[n:2 * n], refs[2 * n:]
        x, y, c = lax.axis_index("x"), lax.axis_index("y"), lax.axis_index("c")
        cps = []
        for w in range(n):
            hr = ins[w].shape[1] // 2
            cps.append(pltpu.make_async_remote_copy(
                src_ref=ins[w].at[:, pl.ds((1 - c) * hr, hr), :], dst_ref=outs[w],
                send_sem=send_sems.at[w], recv_sem=recv_sems.at[w], device_id=(x, y, 1 - c), device_id_type=MESH))
        for cp in cps:
            cp.start()
        for cp in cps:
            cp.wait()

    return pl.pallas_call(
        body, name=name,
        out_shape=[jax.ShapeDtypeStruct((N_CHIPS, g.shape[1] // 2, g.shape[2]), g.dtype) for g in grads],
        in_specs=_hbm_specs(n), out_specs=_hbm_specs(n),
        scratch_shapes=[pltpu.SemaphoreType.DMA((n,)), pltpu.SemaphoreType.DMA((n,))],
    )(*grads)


def _scatter_chips(reds, name):
    n = len(reds)

    def body(*refs):
        ins, outs, (send_sems, recv_sems) = refs[:n], refs[n:2 * n], refs[2 * n:]
        x, y, c = lax.axis_index("x"), lax.axis_index("y"), lax.axis_index("c")
        chips = [(1 - x, y), (x, 1 - y), (1 - x, 1 - y)]
        cps = [pltpu.make_async_remote_copy(
            src_ref=ins[w].at[2 * chip[0] + chip[1]], dst_ref=outs[w].at[j],
            send_sem=send_sems.at[3 * w + j], recv_sem=recv_sems.at[3 * w + j],
            device_id=(*chip, c), device_id_type=MESH)
            for j, chip in enumerate(chips) for w in range(n)]
        for cp in cps:
            cp.start()
        for cp in cps:
            cp.wait()

    return pl.pallas_call(
        body, name=name,
        out_shape=[jax.ShapeDtypeStruct((3,) + r.shape[1:], r.dtype) for r in reds],
        in_specs=_hbm_specs(n), out_specs=_hbm_specs(n),
        scratch_shapes=[pltpu.SemaphoreType.DMA((3 * n,)), pltpu.SemaphoreType.DMA((3 * n,))],
    )(*reds)


def _join_halves(halves):
    n = len(halves)

    def body(*refs):
        ins, outs, (send_sems, recv_sems) = refs[:n], refs[n:2 * n], refs[2 * n:]
        x, y, c = lax.axis_index("x"), lax.axis_index("y"), lax.axis_index("c")
        cps = [pltpu.make_async_remote_copy(
            src_ref=ins[w], dst_ref=outs[w], send_sem=send_sems.at[w], recv_sem=recv_sems.at[w],
            device_id=(x, y, 1 - c), device_id_type=MESH) for w in range(n)]
        for cp in cps:
            cp.start()
        for cp in cps:
            cp.wait()

    return pl.pallas_call(
        body, name="grad_join_halves",
        out_shape=[jax.ShapeDtypeStruct(h.shape, h.dtype) for h in halves],
        in_specs=_hbm_specs(n), out_specs=_hbm_specs(n),
        scratch_shapes=[pltpu.SemaphoreType.DMA((n,)), pltpu.SemaphoreType.DMA((n,))],
    )(*halves)


def _in_hbm(v):
    return pltpu.with_memory_space_constraint(v, pltpu.HBM)


_SPLIT_COPY = pltpu.CompilerParams(has_side_effects=pltpu.SideEffectType.DATAFLOW_SIDE_EFFECTING)


def _gather_copies(srcs, lands, send_sems, recv_sems):
    x, y, c = lax.axis_index("x"), lax.axis_index("y"), lax.axis_index("c")
    cps = []
    for w, (src, land) in enumerate(zip(srcs, lands)):
        hr = src.shape[0] // 2
        for j, chip in enumerate([(1 - x, y), (x, 1 - y), (1 - x, 1 - y)]):
            cps.append(pltpu.make_async_remote_copy(
                src_ref=src.at[pl.ds(c * hr, hr), :], dst_ref=land.at[2 * x + y, pl.ds(c * hr, hr), :],
                send_sem=send_sems.at[3 * w + j], recv_sem=recv_sems.at[3 * w + j],
                device_id=(*chip, c), device_id_type=MESH))
    return cps


def _scatter_copies(srcs, lands, send_sems, recv_sems):
    x, y, c = lax.axis_index("x"), lax.axis_index("y"), lax.axis_index("c")
    cps = []
    for w, (src, land) in enumerate(zip(srcs, lands)):
        for j, chip in enumerate([(1 - x, y), (x, 1 - y), (1 - x, 1 - y)]):
            cps.append(pltpu.make_async_remote_copy(
                src_ref=src.at[2 * chip[0] + chip[1]], dst_ref=land.at[j],
                send_sem=send_sems.at[3 * w + j], recv_sem=recv_sems.at[3 * w + j],
                device_id=(*chip, c), device_id_type=MESH))
    return cps


def _split_start(copies, srcs, lands, name):
    n = len(srcs)

    def body(*refs):
        src, lnd, send_sems, recv_sems, token = refs[:n], refs[n:2 * n], refs[2 * n], refs[2 * n + 1], refs[-1]
        for cp in copies(src, lnd, send_sems, recv_sems):
            cp.start()
        token[...] = jnp.zeros_like(token)

    hbm = pl.BlockSpec(memory_space=pltpu.HBM)
    sem = pl.BlockSpec(memory_space=pltpu.SEMAPHORE)
    outs = pl.pallas_call(
        body, name=name,
        out_shape=(pltpu.SemaphoreType.DMA((3 * n,)), pltpu.SemaphoreType.DMA((3 * n,)),
                   *[pltpu.HBM(v.shape, v.dtype) for v in srcs + lands], jax.ShapeDtypeStruct((8, 128), F32)),
        in_specs=[hbm] * (2 * n),
        out_specs=(sem, sem, *([hbm] * (2 * n)), pl.BlockSpec(memory_space=pltpu.VMEM)),
        input_output_aliases={i: 2 + i for i in range(2 * n)},
        compiler_params=_SPLIT_COPY,
    )(*[_in_hbm(v) for v in srcs + lands])
    return outs[0], outs[1], list(outs[2:2 + n]), list(outs[2 + n:2 + 2 * n]), outs[-1]


def _split_wait(copies, send_sems, recv_sems, srcs, lands, after, name):
    n = len(srcs)

    def body(*refs):
        src, lnd, send_sems, recv_sems = refs[:n], refs[n:2 * n], refs[2 * n], refs[2 * n + 1]
        for cp in copies(src, lnd, send_sems, recv_sems):
            cp.wait_send()
            cp.wait_recv()

    hbm = pl.BlockSpec(memory_space=pltpu.HBM)
    sem = pl.BlockSpec(memory_space=pltpu.SEMAPHORE)
    outs = pl.pallas_call(
        body, name=name,
        out_shape=tuple(pltpu.HBM(v.shape, v.dtype) for v in srcs + lands),
        in_specs=[hbm] * (2 * n) + [sem, sem, pl.BlockSpec(memory_space=pl.ANY)],
        out_specs=tuple([hbm] * (2 * n)),
        input_output_aliases={i: i for i in range(2 * n)},
        compiler_params=_SPLIT_COPY,
    )(*srcs, *lands, send_sems, recv_sems, after)
    return list(outs[n:])


def _pass_to_sibling(lands):
    n = len(lands)

    def body(*refs):
        ins, outs, (send_sems, recv_sems) = refs[:n], refs[n:2 * n], refs[2 * n:]
        x, y, c = lax.axis_index("x"), lax.axis_index("y"), lax.axis_index("c")
        cps = []
        for w in range(n):
            hr = ins[w].shape[1] // 2
            for j, chip in enumerate([(1 - x, y), (x, 1 - y), (1 - x, 1 - y)]):
                k = 2 * chip[0] + chip[1]
                cps.append(pltpu.make_async_remote_copy(
                    src_ref=ins[w].at[k, pl.ds(c * hr, hr), :], dst_ref=outs[w].at[k, pl.ds(c * hr, hr), :],
                    send_sem=send_sems.at[3 * w + j], recv_sem=recv_sems.at[3 * w + j],
                    device_id=(x, y, 1 - c), device_id_type=MESH))
        for cp in cps:
            cp.start()
        for cp in cps:
            cp.wait()

    return pl.pallas_call(
        body, name="gather_late_pass",
        out_shape=[jax.ShapeDtypeStruct(v.shape, v.dtype) for v in lands],
        in_specs=_hbm_specs(n), out_specs=_hbm_specs(n),
        input_output_aliases={i: i for i in range(n)},
        scratch_shapes=[pltpu.SemaphoreType.DMA((3 * n,)), pltpu.SemaphoreType.DMA((3 * n,))],
    )(*lands)


def _row_tile(rows):
    for cand in (256, 176, 128, 64, 32, 16):
        if rows % cand == 0:
            return cand
    raise ValueError(rows)


def _add_my_half(g, other, c_idx, name):
    _, k, n = g.shape
    hr = k // 2
    tr = _row_tile(hr)
    nb = hr // tr

    def body(c_ref, g_ref, o_ref, out_ref, out16_ref):
        s = g_ref[...] + o_ref[...]
        out_ref[...] = s
        out16_ref[...] = s.astype(BF16)

    return pl.pallas_call(
        body, name=name,
        grid_spec=pltpu.PrefetchScalarGridSpec(
            num_scalar_prefetch=1, grid=(N_CHIPS, nb),
            in_specs=[pl.BlockSpec((1, tr, n), lambda j, i, c: (j, c[0] * nb + i, 0)),
                      pl.BlockSpec((1, tr, n), lambda j, i, c: (j, i, 0))],
            out_specs=[pl.BlockSpec((1, tr, n), lambda j, i, c: (j, i, 0)),
                       pl.BlockSpec((1, tr, n), lambda j, i, c: (j, i, 0))]),
        out_shape=[jax.ShapeDtypeStruct((N_CHIPS, hr, n), F32), jax.ShapeDtypeStruct((N_CHIPS, hr, n), BF16)],
        compiler_params=_params(("parallel", "parallel")),
    )(c_idx, g, other)


def _add_chips(red, recv, chip_idx, name):
    _, hr, n = red.shape
    tr = _row_tile(hr)

    def body(k_ref, r_ref, v_ref, out_ref):
        out_ref[...] = ((r_ref[0] + v_ref[0].astype(F32)) + v_ref[1].astype(F32)) + v_ref[2].astype(F32)

    return pl.pallas_call(
        body, name=name,
        grid_spec=pltpu.PrefetchScalarGridSpec(
            num_scalar_prefetch=1, grid=(hr // tr,),
            in_specs=[pl.BlockSpec((1, tr, n), lambda i, k: (k[0], i, 0)),
                      pl.BlockSpec((3, tr, n), lambda i, k: (0, i, 0))],
            out_specs=pl.BlockSpec((tr, n), lambda i, k: (i, 0))),
        out_shape=jax.ShapeDtypeStruct((hr, n), F32),
        compiler_params=_params(("parallel",)),
    )(chip_idx, red, recv)


def _mod_shard(c_all, w_ada, b_ada):
    nb, cols = c_all.shape[0], w_ada.shape[1]

    def body(c_ref, w_ref, b_ref, o_ref):
        c = c_ref[...]
        o_ref[...] = _dot(c * jax.nn.sigmoid(c), w_ref[...]) + b_ref[...]

    return pl.pallas_call(
        body, name="mod_shard", out_shape=jax.ShapeDtypeStruct((nb, cols), F32),
        compiler_params=_params(vmem_mb=48),
    )(c_all, w_ada, b_ada)


def _proj(x2, mod8, w, seq, out_dtype, name):
    t = x2.shape[0]
    n = w.shape[1]
    tm, tn = min(2048, seq), min(1152, n)
    tpb = seq // tm

    def body(x_ref, mod_ref, w_ref, o_ref, h_ref):
        @pl.when(pl.program_id(1) == 0)
        def _():
            h_ref[...] = (x_ref[...] * (1.0 + mod_ref[0, 1:2, :]) + mod_ref[0, 0:1, :]).astype(BF16)
        o_ref[...] = jnp.dot(h_ref[...], w_ref[...], preferred_element_type=F32).astype(o_ref.dtype)

    return pl.pallas_call(
        body, name=name, grid=(t // tm, n // tn),
        in_specs=[pl.BlockSpec((tm, D), lambda i, j: (i, 0)),
                  pl.BlockSpec((1, 8, D), lambda i, j: (i // tpb, 0, 0)),
                  pl.BlockSpec((D, tn), lambda i, j: (0, j))],
        out_specs=[pl.BlockSpec((tm, tn), lambda i, j: (i, j)), pl.BlockSpec((tm, D), lambda i, j: (i, 0))],
        out_shape=[jax.ShapeDtypeStruct((t, n), out_dtype), jax.ShapeDtypeStruct((t, D), BF16)],
        compiler_params=_params(("parallel", "arbitrary"), 56),
    )(x2, mod8, w)


def _rows_matmul(a, w, name):
    t, k = a.shape
    n = w.shape[1]
    tm = 1024 if t % 1024 == 0 else t

    def body(a_ref, w_ref, o_ref):
        o_ref[...] = jnp.dot(a_ref[...], w_ref[...], preferred_element_type=F32)

    return pl.pallas_call(
        body, name=name, grid=(t // tm,),
        in_specs=[pl.BlockSpec((tm, k), lambda i: (i, 0)), pl.BlockSpec((k, n), lambda i: (0, 0))],
        out_specs=pl.BlockSpec((tm, n), lambda i: (i, 0)),
        out_shape=jax.ShapeDtypeStruct((t, n), F32),
        compiler_params=_params(("parallel",)),
    )(a, w)


def _tn_matmul(a, b, name, seq, split=None):
    a_st, b_st = a.ndim == 3, b.ndim == 3
    t, ka = a.shape[-2:]
    n = b.shape[-1]
    tt = min(1024, seq)
    nt = t // tt
    if a_st or b_st:
        steps, tn = (a.shape[0] if a_st else b.shape[0]), n
    else:
        tn = split
        if tn is None:
            tn = next(cand for cand in (1152, 1024, 1408, 512, n) if n % cand == 0)
        steps = n // tn
    stacked_out = a_st or b_st or split is not None

    def body(a_ref, b_ref, o_ref):
        part = _dot_tn(a_ref[0] if a_st else a_ref[...], b_ref[0] if b_st else b_ref[...])
        if stacked_out:
            part = part[None]

        @pl.when(pl.program_id(1) == 0)
        def _():
            o_ref[...] = part

        @pl.when(pl.program_id(1) > 0)
        def _():
            o_ref[...] += part

    if a_st:
        in_specs = [pl.BlockSpec((1, tt, ka), lambda j, k: (j, k, 0))]
    else:
        in_specs = [pl.BlockSpec((tt, ka), lambda j, k: (k, 0))]
    if b_st:
        in_specs.append(pl.BlockSpec((1, tt, n), lambda j, k: (j, k, 0)))
    else:
        in_specs.append(pl.BlockSpec((tt, tn), lambda j, k: (k, 0 if a_st else j)))
    if stacked_out:
        out_spec = pl.BlockSpec((1, ka, tn), lambda j, k: (j, 0, 0))
        out_shape = jax.ShapeDtypeStruct((steps, ka, tn), F32)
    else:
        out_spec = pl.BlockSpec((ka, tn), lambda j, k: (0, j))
        out_shape = jax.ShapeDtypeStruct((ka, n), F32)
    return pl.pallas_call(
        body, name=name, grid=(steps, nt), in_specs=in_specs, out_specs=out_spec, out_shape=out_shape,
        compiler_params=_params(("parallel", "arbitrary"), 56),
    )(a, b)


def _dh_kernel(dproj, w_p, x2, dxp, mod8, seq):
    t = x2.shape[0]
    tm, tk = min(1024, seq), 1152
    tpb = seq // tm
    nk = NP // tk
    nbatch = t // seq

    def body(dp_ref, w_ref, x_ref, dxp_ref, mod_ref, gx_ref, dm_ref, acc):
        i, k = pl.program_id(0), pl.program_id(1)

        @pl.when(k == 0)
        def _():
            acc[...] = jnp.zeros_like(acc)

        acc[...] += _dot_nt(dp_ref[...], w_ref[...])

        @pl.when(k == nk - 1)
        def _():
            dh = acc[...]
            gx_ref[...] = dxp_ref[...] + dh * (1.0 + mod_ref[0, 1:2, :])
            upd = jnp.concatenate(
                [jnp.sum(dh, axis=0, keepdims=True), jnp.sum(dh * x_ref[...], axis=0, keepdims=True),
                 jnp.zeros((6, D), F32)], axis=0)

            @pl.when(i % tpb == 0)
            def _():
                dm_ref[0] = upd

            @pl.when(i % tpb != 0)
            def _():
                dm_ref[0] += upd

    return pl.pallas_call(
        body, name="dh", grid=(t // tm, nk),
        in_specs=[pl.BlockSpec((tm, tk), lambda i, k: (i, k)),
                  pl.BlockSpec((D, tk), lambda i, k: (0, k)),
                  pl.BlockSpec((tm, D), lambda i, k: (i, 0)),
                  pl.BlockSpec((tm, D), lambda i, k: (i, 0)),
                  pl.BlockSpec((1, 8, D), lambda i, k: (i // tpb, 0, 0))],
        out_specs=[pl.BlockSpec((tm, D), lambda i, k: (i, 0)),
                   pl.BlockSpec((1, 8, D), lambda i, k: (i // tpb, 0, 0))],
        out_shape=[jax.ShapeDtypeStruct((t, D), F32), jax.ShapeDtypeStruct((nbatch, 8, D), F32)],
        scratch_shapes=[pltpu.VMEM((tm, D), F32)],
        compiler_params=_params(("arbitrary", "arbitrary"), 48),
    )(dproj, w_p, x2, dxp, mod8)


def _tri(n, upper):
    r = lax.broadcasted_iota(jnp.int32, (n, n), 0)
    c = lax.broadcasted_iota(jnp.int32, (n, n), 1)
    return jnp.where((c >= r) if upper else (c <= r), 1.0, 0.0).astype(F32)


@jax.custom_vjp
def _mm_nn(a, b):
    return _dot(a, b)


_mm_nn.defvjp(lambda a, b: (_dot(a, b), (a, b)),
              lambda res, g: (_dot_nt(g, res[1]), _dot_tn(res[0], g)))


@jax.custom_vjp
def _mm_nt(a, b):
    return _dot_nt(a, b)


_mm_nt.defvjp(lambda a, b: (_dot_nt(a, b), (a, b)),
              lambda res, g: (_dot(g, res[1]), _dot_tn(g, res[0])))


@jax.custom_vjp
def _mm_tn(a, b):
    return _dot_tn(a, b)


_mm_tn.defvjp(lambda a, b: (_dot_tn(a, b), (a, b)),
              lambda res, g: (_dot_nt(res[1], g), _dot(res[0], g)))


@jax.custom_vjp
def _cumsum_rows(x):
    return _dot_f32(_tri(x.shape[0], False), x)


_cumsum_rows.defvjp(lambda x: (_cumsum_rows(x), None),
                    lambda _, g: (_dot_f32(_tri(g.shape[0], True), g),))


@functools.partial(jax.custom_vjp, nondiff_argnums=(1,))
def _shift_rows(x, k):
    return pltpu.roll(x, k % x.shape[0], 0)


_shift_rows.defvjp(lambda x, k: (_shift_rows(x, k), None),
                   lambda k, _, g: (pltpu.roll(g, (-k) % g.shape[0], 0),))


def _group_ref(bc, m):
    n = bc.shape[0] // (2 * m)
    b3 = bc.reshape(n, 2 * m, ADH)
    row = lax.broadcasted_iota(jnp.int32, b3.shape, 1)
    ref = jnp.sum(jnp.where(row == m - 1, b3, 0.0), axis=1, keepdims=True)
    return jnp.broadcast_to(ref, b3.shape).reshape(bc.shape)


def _hgrn_block(q, fl, v, g, st, lb, nw):
    n = q.shape[0]
    f = lb + (1.0 - lb) * jax.nn.sigmoid(fl)
    kk = 1.0 - f
    lf = jnp.log(f)
    bc = _cumsum_rows(lf)
    row = lax.broadcasted_iota(jnp.int32, (n, ADH), 0)
    same = jnp.bitwise_xor(lax.broadcasted_iota(jnp.int32, (n, n), 0), lax.broadcasted_iota(jnp.int32, (n, n), 1))
    a = jnp.zeros((n, n), F32)
    m = 1
    while m < n:
        r = jnp.bitwise_and(row, 2 * m - 1)
        up, lo = r >= m, r < m
        if m == 1:
            aq, ak = lf, jnp.zeros_like(lf)
        elif m == 2:
            aq = jnp.where(r == 3, lf + _shift_rows(lf, 1), lf)
            ak = jnp.where(r == 0, _shift_rows(lf, -1), 0.0)
        else:
            ref = _group_ref(bc, m)
            aq, ak = bc - ref, ref - bc
        qt = jnp.where(up, q * jnp.exp(jnp.where(up, aq, 0.0)), 0.0)
        kt = jnp.where(lo, kk * jnp.exp(jnp.where(lo, ak, 0.0)), 0.0)
        a = a + jnp.where(same < 2 * m, _mm_nt(qt, kt), 0.0)
        m *= 2
    last = row == n - 1
    bl = jnp.sum(jnp.where(last, bc, 0.0), axis=0, keepdims=True)
    o = _mm_nn(a, v) + _mm_nt(q * jnp.exp(bc), st) + jnp.sum(q * kk, axis=-1, keepdims=True) * v
    st_new = st * jnp.exp(bl) + _mm_tn(v, kk * jnp.exp(bl - bc))
    rms = lax.rsqrt(jnp.mean(o * o, axis=-1, keepdims=True) + RMS_EPS)
    return o * rms * nw * jax.nn.sigmoid(g), st_new


def _hgrn_fwd(proj, lb_logits, norm_w, nbatch, seq):
    t = proj.shape[0]
    blk = min(HGRN_BLOCK, seq)
    nb = seq // blk

    def body(p_ref, lbl_ref, nw_ref, y_ref, ck_ref, st_s):
        @pl.when(pl.program_id(2) == 0)
        def _():
            st_s[...] = jnp.zeros_like(st_s)

        st = st_s[...]
        ck_ref[0] = st
        lb = jax.nn.sigmoid(lbl_ref[0:1, :] - lbl_ref[1:2, :])
        p = p_ref[...].astype(F32)
        y, st_new = _hgrn_block(p[:, 0:128], p[:, 128:256], p[:, 256:384], p[:, 384:512], st, lb, nw_ref[...])
        st_s[...] = st_new
        y_ref[...] = y.astype(y_ref.dtype)

    return pl.pallas_call(
        body, name="hgrn_fwd", grid=(AH, nbatch, nb),
        in_specs=[pl.BlockSpec((blk, 512), lambda h, b, i: (b * nb + i, COL_A // 512 + h)),
                  pl.BlockSpec((2, 128), lambda h, b, i: (0, h)),
                  pl.BlockSpec((1, 128), lambda h, b, i: (0, h))],
        out_specs=[pl.BlockSpec((blk, 128), lambda h, b, i: (b * nb + i, h)),
                   pl.BlockSpec((1, 128, 128), lambda h, b, i: ((h * nbatch + b) * nb + i, 0, 0))],
        out_shape=[jax.ShapeDtypeStruct((t, AW), BF16), jax.ShapeDtypeStruct((AH * nbatch * nb, 128, 128), F32)],
        scratch_shapes=[pltpu.VMEM((128, 128), F32)],
        compiler_params=_params(("parallel", "parallel", "arbitrary"), 48),
    )(proj, lb_logits, norm_w)


def _hgrn_bwd(proj, dya, ckpt, lb_logits, norm_w, dproj, nbatch, seq):
    t = proj.shape[0]
    blk = min(HGRN_BLOCK, seq)
    nb = seq // blk

    def body(p_ref, dy_ref, ck_ref, lbl_ref, nw_ref, dp_in, dp_ref, sm_ref, dst_s):
        del dp_in
        b_id, i = pl.program_id(1), pl.program_id(2)

        @pl.when(i == 0)
        def _():
            dst_s[...] = jnp.zeros_like(dst_s)

        lb = jax.nn.sigmoid(lbl_ref[0:1, :] - lbl_ref[1:2, :])
        p = p_ref[...].astype(F32)
        _, pullback = jax.vjp(_hgrn_block, p[:, 0:128], p[:, 128:256], p[:, 256:384], p[:, 384:512],
                              ck_ref[0], lb, nw_ref[...])
        dq, dfl, dv, dg, dst, dlb, dnw = pullback((dy_ref[...], dst_s[...]))
        dst_s[...] = dst
        dp_ref[:, 0:128] = dq.astype(dp_ref.dtype)
        dp_ref[:, 128:256] = dfl.astype(dp_ref.dtype)
        dp_ref[:, 256:384] = dv.astype(dp_ref.dtype)
        dp_ref[:, 384:512] = dg.astype(dp_ref.dtype)
        upd = jnp.concatenate([dlb, dnw, jnp.zeros((6, 128), F32)], axis=0)
        first = (b_id == 0) & (i == 0)

        @pl.when(first)
        def _():
            sm_ref[...] = upd

        @pl.when(jnp.logical_not(first))
        def _():
            sm_ref[...] += upd

    def rows(h, b, i):
        return b * nb + (nb - 1 - i)

    return pl.pallas_call(
        body, name="hgrn_bwd", grid=(AH, nbatch, nb),
        in_specs=[pl.BlockSpec((blk, 512), lambda h, b, i: (rows(h, b, i), COL_A // 512 + h)),
                  pl.BlockSpec((blk, 128), lambda h, b, i: (rows(h, b, i), h)),
                  pl.BlockSpec((1, 128, 128), lambda h, b, i: ((h * nbatch + b) * nb + (nb - 1 - i), 0, 0)),
                  pl.BlockSpec((2, 128), lambda h, b, i: (0, h)),
                  pl.BlockSpec((1, 128), lambda h, b, i: (0, h)),
                  pl.BlockSpec(memory_space=pl.ANY)],
        out_specs=[pl.BlockSpec((blk, 512), lambda h, b, i: (rows(h, b, i), COL_A // 512 + h)),
                   pl.BlockSpec((8, 128), lambda h, b, i: (0, h))],
        out_shape=[jax.ShapeDtypeStruct((t, NP), BF16), jax.ShapeDtypeStruct((8, AW), F32)],
        input_output_aliases={5: 0},
        scratch_shapes=[pltpu.VMEM((128, 128), F32)],
        compiler_params=_params(("parallel", "arbitrary", "arbitrary"), 48),
    )(proj, dya, ckpt, lb_logits, norm_w, dproj)


def _log_sigmoid(z):
    return jnp.minimum(z, 0.0) - jnp.log(1.0 + jnp.exp(-jnp.abs(z)))


def _fox_cum(proj, bias128, nbatch, seq):
    t = proj.shape[0]
    ts = min(512, seq)
    nb = seq // ts

    def body(p_ref, b_ref, c_ref, carry):
        @pl.when(pl.program_id(1) == 0)
        def _():
            carry[...] = jnp.zeros_like(carry)
        cum = _dot_f32(_tri(ts, False), _log_sigmoid(p_ref[...] + b_ref[...])) + carry[...]
        carry[...] = cum[ts - 1:ts, :]
        cum2 = cum * LOG2E
        lane = lax.broadcasted_iota(jnp.int32, (ts, 128), 1)
        for p in range(4):
            c_ref[p] = jnp.where(lane < 64, cum2[:, 2 * p:2 * p + 1], cum2[:, 2 * p + 1:2 * p + 2])

    return pl.pallas_call(
        body, name="fox_cum", grid=(nbatch, nb),
        in_specs=[pl.BlockSpec((ts, 128), lambda b, i: (b * nb + i, 0)),
                  pl.BlockSpec((1, 128), lambda b, i: (0, 0))],
        out_specs=pl.BlockSpec((4, ts, 128), lambda b, i: (0, b * nb + i, 0)),
        out_shape=jax.ShapeDtypeStruct((4, t, 128), F32),
        scratch_shapes=[pltpu.VMEM((1, 128), F32)],
        compiler_params=_params(("parallel", "arbitrary")),
    )(proj, bias128)


def _fox_scores_t(q_ref, kv_ref, cc_ref, hh, masked, tq, tk):
    kh = kv_ref[:, 64 * hh:64 * hh + 64].astype(BF16)
    qh = (q_ref[:, 64 * hh:64 * hh + 64] * (LOG2E * BDH ** -0.5)).astype(BF16)
    s = _dot_nt(kh, qh) - cc_ref[0, :, 64 * hh:64 * hh + 1]
    if masked:
        key = lax.broadcasted_iota(jnp.int32, (tk, tq), 0)
        qry = lax.broadcasted_iota(jnp.int32, (tk, tq), 1)
        s = jnp.where(key <= qry, s, NEG)
    return s, kh


def _causal_pairs(nq, key_major):
    if key_major:
        pairs = [(i, j) for j in range(nq) for i in range(j, nq)]
    else:
        pairs = [(i, j) for i in range(nq) for j in range(i + 1)]
    return (jnp.asarray([p[0] for p in pairs], jnp.int32), jnp.asarray([p[1] for p in pairs], jnp.int32))


def _with_ones_lane(x128, hh):
    lane = lax.broadcasted_iota(jnp.int32, x128.shape, 1)
    one = jnp.ones_like(x128)
    zero = jnp.zeros_like(x128)
    if hh == 0:
        return jnp.where(lane < 64, x128, jnp.where(lane == 64, one, zero))
    return jnp.where(lane >= 64, x128, jnp.where(lane == 0, one, zero))


def _fox_fwd(proj, cum_cols, nbatch, seq):
    t = proj.shape[0]
    tq = tk = min(512, seq)
    nq = seq // tq
    qi, kj = _causal_pairs(nq, key_major=False)

    def body(qi_ref, kj_ref, q_ref, kv_ref, cc_ref, o_ref, lse_ref, m_s, acc_s):
        s_id = pl.program_id(2)
        i, j = qi_ref[s_id], kj_ref[s_id]

        @pl.when(j == 0)
        def _():
            m_s[...] = jnp.full_like(m_s, NEG)
            acc_s[...] = jnp.zeros_like(acc_s)

        def step(masked):
            for hh in range(2):
                s, _ = _fox_scores_t(q_ref, kv_ref, cc_ref, hh, masked, tq, tk)
                m_prev = m_s[hh:hh + 1, :]
                m_new = jnp.maximum(m_prev, jnp.max(s, axis=0, keepdims=True))
                alpha = jnp.exp2(m_prev - m_new)
                p = jnp.exp2(s - m_new).astype(BF16)
                v_aug = _with_ones_lane(kv_ref[:, 128:256].astype(BF16), hh)
                acc_s[hh] = acc_s[hh] * alpha + _dot_tn(v_aug, p)
                m_s[hh:hh + 1, :] = m_new

        @pl.when(j < i)
        def _():
            step(False)

        @pl.when(j == i)
        def _():
            step(True)
            a0, a1 = acc_s[0], acc_s[1]
            l0, l1 = a0[64:65, :], a1[0:1, :]
            o_t = jnp.concatenate([a0[0:64, :] / l0, a1[64:128, :] / l1], axis=0)
            o_ref[...] = o_t.T.astype(o_ref.dtype)
            lse_ref[0, 0] = jnp.concatenate(
                [m_s[0:1, :] + jnp.log2(l0), m_s[1:2, :] + jnp.log2(l1), jnp.zeros((6, tq), F32)], axis=0)

    return pl.pallas_call(
        body, name="fox_fwd",
        grid_spec=pltpu.PrefetchScalarGridSpec(
            num_scalar_prefetch=2, grid=(nbatch, 4, qi.shape[0]),
            in_specs=[pl.BlockSpec((tq, 128), lambda b, p, s, qi, kj: (b * nq + qi[s], COL_BQ // 128 + p)),
                      pl.BlockSpec((tk, 256), lambda b, p, s, qi, kj: (b * nq + kj[s], COL_KV // 256 + p)),
                      pl.BlockSpec((1, tk, 128), lambda b, p, s, qi, kj: (p, b * nq + kj[s], 0))],
            out_specs=[pl.BlockSpec((tq, 128), lambda b, p, s, qi, kj: (b * nq + qi[s], p)),
                       pl.BlockSpec((1, 1, 8, tq), lambda b, p, s, qi, kj: (b, p, 0, qi[s]))],
            scratch_shapes=[pltpu.VMEM((8, tq), F32), pltpu.VMEM((2, 128, tq), F32)]),
        out_shape=[jax.ShapeDtypeStruct((t, 512), BF16), jax.ShapeDtypeStruct((nbatch, 4, 8, seq), F32)],
        compiler_params=_params(("parallel", "parallel", "arbitrary"), 48),
    )(qi, kj, proj, proj, cum_cols)


def _fox_bwd(proj, cum_cols, lse, yb, dyb, dproj, nbatch, seq):
    t = proj.shape[0]
    tq = tk = min(512, seq)
    nq = seq // tq
    scale = BDH ** -0.5
    qi, kj = _causal_pairs(nq, key_major=True)
    nsteps = qi.shape[0]

    def body(qi_ref, kj_ref, q_ref, kv_ref, cc_ref, lse_ref, o_ref, do_ref, dp_in,
             dkv_ref, dq_ref, drs_ref, dcs_ref, dkv_s, dqa_s, dcs_s):
        del dp_in
        hp, s_id = pl.program_id(1), pl.program_id(2)
        i, j = qi_ref[s_id], kj_ref[s_id]

        @pl.when(i == j)
        def _():
            dkv_s[...] = jnp.zeros_like(dkv_s)
            dcs_s[...] = jnp.zeros_like(dcs_s)

        @pl.when(s_id == 0)
        def _():
            dqa_s[...] = jnp.zeros_like(dqa_s)

        def step(masked):
            lane = lax.broadcasted_iota(jnp.int32, (tq, 128), 1)
            for hh in range(2):
                s, _ = _fox_scores_t(q_ref, kv_ref, cc_ref, hh, masked, tq, tk)
                p = jnp.exp2(s - lse_ref[0, 0, hh:hh + 1, :])
                doh = do_ref[:, 64 * hh:64 * hh + 64]
                dd = lax.dot_general(jnp.ones((8, 64), F32), doh * o_ref[:, 64 * hh:64 * hh + 64].astype(F32),
                                     (((1,), (1,)), ((), ())), preferred_element_type=F32, precision=HIGHEST)[0:1, :]
                doh = doh.astype(BF16)
                dp = _dot_nt(kv_ref[:, 128 + 64 * hh:192 + 64 * hh], doh)
                ds = (p * (dp - dd)).astype(BF16)
                dkv_s[:, 128 + 64 * hh:192 + 64 * hh] += _dot(p, doh)
                dkv_s[:, 64 * hh:64 * hh + 64] += _dot(ds, q_ref[:, 64 * hh:64 * hh + 64] * scale)
                k_aug = _with_ones_lane(kv_ref[:, 0:128].astype(BF16), hh)
                dqa_s[i, hh] += _dot_tn(k_aug, ds)
                sel = jnp.where(lane == 2 * hp + hh, 1.0, 0.0).astype(BF16)
                dcs_s[...] += _dot(ds, sel)

        @pl.when(i == j)
        def _():
            step(True)

        @pl.when(i > j)
        def _():
            step(False)

        @pl.when(i == nq - 1)
        def _():
            dkv_ref[...] = dkv_s[...].astype(dkv_ref.dtype)
            dcs_ref[0] = dcs_s[...]

        @pl.when(s_id == nsteps - 1)
        def _():
            lane = lax.broadcasted_iota(jnp.int32, (tq, 128), 1)
            for blk in range(nq):
                a0 = dqa_s[blk, 0].T
                a1 = dqa_s[blk, 1].T
                rows = pl.ds(blk * tq, tq)
                dq_ref[rows, :] = (jnp.where(lane < 64, a0, a1) * scale).astype(dq_ref.dtype)
                drs_ref[0, rows, :] = jnp.where(lane == 2 * hp, a0[:, 64:65], jnp.where(lane == 2 * hp + 1, a1[:, 0:1], 0.0))

    return pl.pallas_call(
        body, name="fox_bwd",
        grid_spec=pltpu.PrefetchScalarGridSpec(
            num_scalar_prefetch=2, grid=(nbatch, 4, nsteps),
            in_specs=[pl.BlockSpec((tq, 128), lambda b, p, s, qi, kj: (b * nq + qi[s], COL_BQ // 128 + p)),
                      pl.BlockSpec((tk, 256), lambda b, p, s, qi, kj: (b * nq + kj[s], COL_KV // 256 + p)),
                      pl.BlockSpec((1, tk, 128), lambda b, p, s, qi, kj: (p, b * nq + kj[s], 0)),
                      pl.BlockSpec((1, 1, 8, tq), lambda b, p, s, qi, kj: (b, p, 0, qi[s])),
                      pl.BlockSpec((tq, 128), lambda b, p, s, qi, kj: (b * nq + qi[s], p)),
                      pl.BlockSpec((tq, 128), lambda b, p, s, qi, kj: (b * nq + qi[s], p)),
                      pl.BlockSpec(memory_space=pl.ANY)],
            out_specs=[pl.BlockSpec((tk, 256), lambda b, p, s, qi, kj: (b * nq + kj[s], COL_KV // 256 + p)),
                       pl.BlockSpec((seq, 128), lambda b, p, s, qi, kj: (b, p)),
                       pl.BlockSpec((1, seq, 128), lambda b, p, s, qi, kj: (p, b, 0)),
                       pl.BlockSpec((1, tk, 128), lambda b, p, s, qi, kj: (p, b * nq + kj[s], 0))],
            scratch_shapes=[pltpu.VMEM((tk, 256), F32), pltpu.VMEM((nq, 2, 128, tq), F32),
                            pltpu.VMEM((tk, 128), F32)]),
        out_shape=[jax.ShapeDtypeStruct((t, NP), BF16), jax.ShapeDtypeStruct((t, 512), BF16),
                   jax.ShapeDtypeStruct((4, t, 128), F32), jax.ShapeDtypeStruct((4, t, 128), F32)],
        input_output_aliases={8: 0},
        compiler_params=_params(("parallel", "parallel", "arbitrary"), 56),
    )(qi, kj, proj, proj, cum_cols, lse, yb, dyb, dproj)


def _place_cols(dproj, src, col):
    t, w = src.shape
    tm = 1024 if t % 1024 == 0 else t

    def body(s_ref, dp_in, o_ref):
        del dp_in
        o_ref[...] = s_ref[...]

    return pl.pallas_call(
        body, name="place_cols", grid=(t // tm,),
        in_specs=[pl.BlockSpec((tm, w), lambda i: (i, 0)), pl.BlockSpec(memory_space=pl.ANY)],
        out_specs=pl.BlockSpec((tm, w), lambda i: (i, col // w)),
        out_shape=jax.ShapeDtypeStruct(dproj.shape, dproj.dtype),
        input_output_aliases={1: 0},
        compiler_params=_params(("parallel",)),
    )(src, dproj)


def _fox_dbf(proj, bias128, drs, dcs, dproj, nbatch, seq):
    t = proj.shape[0]
    ts = min(512, seq)
    nb = seq // ts

    def body(p_ref, b_ref, dr_ref, dc_ref, dp_in, dp_ref, sm_ref, carry):
        del dp_in
        b_id, i = pl.program_id(0), pl.program_id(1)

        @pl.when(i == 0)
        def _():
            carry[...] = jnp.zeros_like(carry)

        dcum = (dr_ref[0] - dc_ref[0]) + (dr_ref[1] - dc_ref[1]) + (dr_ref[2] - dc_ref[2]) + (dr_ref[3] - dc_ref[3])
        rc = _dot_f32(_tri(ts, True), dcum) + carry[...]
        carry[...] = rc[0:1, :]
        z = p_ref[...] + b_ref[...]
        lane = lax.broadcasted_iota(jnp.int32, (ts, 128), 1)
        dz = jnp.where(lane < BH, rc * jax.nn.sigmoid(-z), 0.0)
        dp_ref[...] = dz.astype(dp_ref.dtype)
        upd = jnp.concatenate([jnp.sum(dz, axis=0, keepdims=True), jnp.zeros((7, 128), F32)], axis=0)
        first = (b_id == 0) & (i == 0)

        @pl.when(first)
        def _():
            sm_ref[...] = upd

        @pl.when(jnp.logical_not(first))
        def _():
            sm_ref[...] += upd

    def rows(b, i):
        return b * nb + (nb - 1 - i)

    return pl.pallas_call(
        body, name="fox_dbf", grid=(nbatch, nb),
        in_specs=[pl.BlockSpec((ts, 128), lambda b, i: (rows(b, i), 0)),
                  pl.BlockSpec((1, 128), lambda b, i: (0, 0)),
                  pl.BlockSpec((4, ts, 128), lambda b, i: (0, rows(b, i), 0)),
                  pl.BlockSpec((4, ts, 128), lambda b, i: (0, rows(b, i), 0)),
                  pl.BlockSpec(memory_space=pl.ANY)],
        out_specs=[pl.BlockSpec((ts, 128), lambda b, i: (rows(b, i), COL_BF // 128)),
                   pl.BlockSpec((8, 128), lambda b, i: (0, 0))],
        out_shape=[jax.ShapeDtypeStruct((t, NP), BF16), jax.ShapeDtypeStruct((8, 128), F32)],
        input_output_aliases={4: 0},
        scratch_shapes=[pltpu.VMEM((1, 128), F32)],
        compiler_params=_params(("arbitrary", "arbitrary")),
    )(proj, bias128, drs, dcs, dproj)


def _ln_stats(z):
    mu = jnp.mean(z, axis=-1, keepdims=True)
    zc = z - mu
    rstd = lax.rsqrt(jnp.mean(zc * zc, axis=-1, keepdims=True) + LN_EPS)
    return zc * rstd, rstd


def _ln_bwd(dy, xhat, rstd, w):
    dxh = dy * w
    return rstd * (dxh - jnp.mean(dxh, axis=-1, keepdims=True) - xhat * jnp.mean(dxh * xhat, axis=-1, keepdims=True))


def _merge_fwd(ya, yb, proj, x2, mod8, wba, wbb, wout, ln1w, ln1b, seq):
    t = x2.shape[0]
    tm = min(256, seq)
    tpb = seq // tm

    def body(ya_ref, yb_ref, g_ref, x_ref, mod_ref, wa_ref, wb_ref, wo_ref, lw_ref, lb_ref, mg_ref, u_ref, x1_ref):
        ga = jax.nn.sigmoid(g_ref[:, 0:D].astype(F32))
        gb = jax.nn.sigmoid(g_ref[:, D:2 * D].astype(F32))
        merged = (ga * jnp.dot(ya_ref[...], wa_ref[...], preferred_element_type=F32)
                  + gb * jnp.dot(yb_ref[...], wb_ref[...], preferred_element_type=F32))
        mg = merged.astype(BF16)
        mg_ref[...] = mg
        u = jnp.dot(mg, wo_ref[...], preferred_element_type=F32)
        u_ref[...] = u
        xhat, _ = _ln_stats(ALPHA * x_ref[...] + (1.0 + mod_ref[0, 2:3, :]) * u)
        x1_ref[...] = xhat * lw_ref[...] + lb_ref[...]

    tok = lambda w: pl.BlockSpec((tm, w), lambda i: (i, 0))
    full = lambda a: pl.BlockSpec(a.shape, lambda i: (0,) * a.ndim)
    return pl.pallas_call(
        body, name="merge_fwd", grid=(t // tm,),
        in_specs=[tok(512), tok(512), pl.BlockSpec((tm, 2048), lambda i: (i, COL_GATES // 2048)), tok(D),
                  pl.BlockSpec((1, 8, D), lambda i: (i // tpb, 0, 0)),
                  full(wba), full(wbb), full(wout), full(ln1w), full(ln1b)],
        out_specs=[tok(D), tok(D), tok(D)],
        out_shape=[jax.ShapeDtypeStruct((t, D), BF16), jax.ShapeDtypeStruct((t, D), F32),
                   jax.ShapeDtypeStruct((t, D), F32)],
        compiler_params=_params(("parallel",), 48),
    )(ya, yb, proj, x2, mod8, wba, wbb, wout, ln1w, ln1b)


def _merge_bwd(du, ya, yb, proj, wba, wbb, wout, token, seq):
    t = du.shape[0]
    tm = min(256, seq)

    def body(du_ref, ya_ref, yb_ref, g_ref, wa_ref, wb_ref, wo_ref, token_ref,
             dp_ref, dpa_ref, dpb_ref, dya_ref, dyb_ref):
        del token_ref
        ga = jax.nn.sigmoid(g_ref[:, 0:D].astype(F32))
        gb = jax.nn.sigmoid(g_ref[:, D:2 * D].astype(F32))
        dm = _dot_nt(du_ref[...], wo_ref[...])
        pa = jnp.dot(ya_ref[...], wa_ref[...], preferred_element_type=F32)
        pb = jnp.dot(yb_ref[...], wb_ref[...], preferred_element_type=F32)
        dpa = (dm * ga).astype(BF16)
        dpb = (dm * gb).astype(BF16)
        dpa_ref[...] = dpa
        dpb_ref[...] = dpb
        dp_ref[:, 0:D] = (dm * pa * ga * (1.0 - ga)).astype(BF16)
        dp_ref[:, D:2 * D] = (dm * pb * gb * (1.0 - gb)).astype(BF16)
        dya_ref[...] = _dot_nt(dpa, wa_ref[...])
        dyb_ref[...] = _dot_nt(dpb, wb_ref[...])

    tok = lambda w: pl.BlockSpec((tm, w), lambda i: (i, 0))
    full = lambda a: pl.BlockSpec(a.shape, lambda i: (0,) * a.ndim)
    return pl.pallas_call(
        body, name="merge_bwd", grid=(t // tm,),
        in_specs=[tok(D), tok(512), tok(512), pl.BlockSpec((tm, 2048), lambda i: (i, COL_GATES // 2048)),
                  full(wba), full(wbb), full(wout), full(token)],
        out_specs=[pl.BlockSpec((tm, 2048), lambda i: (i, COL_GATES // 2048)), tok(D), tok(D), tok(512), tok(512)],
        out_shape=[jax.ShapeDtypeStruct((t, NP), BF16), jax.ShapeDtypeStruct((t, D), BF16),
                   jax.ShapeDtypeStruct((t, D), BF16), jax.ShapeDtypeStruct((t, 512), F32),
                   jax.ShapeDtypeStruct((t, 512), F32)],
        compiler_params=_params(("parallel",), 48),
    )(du, ya, yb, proj, wba, wbb, wout, token)


def _ffn_fwd(x1, mod8, wg, wu, wd, target, ln2w, ln2b, seq):
    t = x1.shape[0]
    tm = min(FFN_TOKENS, seq)
    nf, _, tf = wg.shape
    tpb = seq // tm
    nbatch = t // seq

    def body(x_ref, mod_ref, wg_ref, wu_ref, wd_ref, t_ref, lw_ref, lb_ref,
             a_ref, b_ref, h_s, dz_ref, st_ref, dm_ref, acc):
        i, j = pl.program_id(0), pl.program_id(1)

        @pl.when(j == 0)
        def _():
            h_s[...] = (x_ref[...] * (1.0 + mod_ref[0, 4:5, :]) + mod_ref[0, 3:4, :]).astype(BF16)
            acc[...] = jnp.zeros_like(acc)

        a = jnp.dot(h_s[...], wg_ref[0], preferred_element_type=F32)
        b = jnp.dot(h_s[...], wu_ref[0], preferred_element_type=F32)
        a_ref[0] = a.astype(BF16)
        b_ref[0] = b.astype(BF16)
        acc[...] += _dot(a * jax.nn.sigmoid(a) * b, wd_ref[0])

        @pl.when(j == nf - 1)
        def _():
            ffn = acc[...]
            xhat, rstd = _ln_stats(ALPHA * x_ref[...] + (1.0 + mod_ref[0, 5:6, :]) * ffn)
            diff = xhat * lw_ref[...] + lb_ref[...] - t_ref[...]
            loss = 0.5 * jnp.sum(jnp.sum(diff * diff, axis=-1, keepdims=True), axis=0, keepdims=True) / D
            dy = diff * (1.0 / D)
            dz = _ln_bwd(dy, xhat, rstd, lw_ref[...])
            dz_ref[...] = dz
            lane = lax.broadcasted_iota(jnp.int32, (1, D), 1)
            upd = jnp.concatenate(
                [jnp.sum(dy * xhat, axis=0, keepdims=True), jnp.sum(dy, axis=0, keepdims=True),
                 jnp.where(lane == 0, loss, 0.0), jnp.zeros((5, D), F32)], axis=0)
            dmu = jnp.concatenate(
                [jnp.zeros((5, D), F32), jnp.sum(dz * ffn, axis=0, keepdims=True), jnp.zeros((2, D), F32)], axis=0)

            @pl.when(i == 0)
            def _():
                st_ref[...] = upd

            @pl.when(i > 0)
            def _():
                st_ref[...] += upd

            @pl.when(i % tpb == 0)
            def _():
                dm_ref[0] = dmu

            @pl.when(i % tpb != 0)
            def _():
                dm_ref[0] += dmu

    row = lambda: pl.BlockSpec((tm, D), lambda i, j: (i, 0))
    vec = lambda: pl.BlockSpec((1, D), lambda i, j: (0, 0))
    return pl.pallas_call(
        body, name="ffn_fwd", grid=(t // tm, nf),
        in_specs=[row(), pl.BlockSpec((1, 8, D), lambda i, j: (i // tpb, 0, 0)),
                  pl.BlockSpec((1, D, tf), lambda i, j: (j, 0, 0)), pl.BlockSpec((1, D, tf), lambda i, j: (j, 0, 0)),
                  pl.BlockSpec((1, tf, D), lambda i, j: (j, 0, 0)), row(), vec(), vec()],
        out_specs=[pl.BlockSpec((1, tm, tf), lambda i, j: (j, i, 0)), pl.BlockSpec((1, tm, tf), lambda i, j: (j, i, 0)),
                   row(), row(), pl.BlockSpec((8, D), lambda i, j: (0, 0)),
                   pl.BlockSpec((1, 8, D), lambda i, j: (i // tpb, 0, 0))],
        out_shape=[jax.ShapeDtypeStruct((nf, t, tf), BF16), jax.ShapeDtypeStruct((nf, t, tf), BF16),
                   jax.ShapeDtypeStruct((t, D), BF16),
                   jax.ShapeDtypeStruct((t, D), F32), jax.ShapeDtypeStruct((8, D), F32),
                   jax.ShapeDtypeStruct((nbatch, 8, D), F32)],
        scratch_shapes=[pltpu.VMEM((tm, D), F32)],
        compiler_params=_params(("arbitrary", "arbitrary"), 60),
    )(x1, mod8, wg, wu, wd, target, ln2w, ln2b)


def _ffn_bwd(dz2, a, b, wg, wu, wd, x1, x2, u, mod8, ln1w, seq):
    t = x1.shape[0]
    tm = min(512, seq)
    nf, tf, _ = wg.shape
    tpb = seq // tm
    nbatch = t // seq

    def body(dz_ref, a_ref, b_ref, wg_ref, wu_ref, wd_ref, x1_ref, x_ref, u_ref, mod_ref, lw_ref,
             da_ref, db_ref, hm_ref, df_ref, du_ref, dxp_ref, st_ref, dm_ref, acc):
        i, j = pl.program_id(0), pl.program_id(1)

        @pl.when(j == 0)
        def _():
            df_ref[...] = ((1.0 + mod_ref[0, 5:6, :]) * dz_ref[...]).astype(BF16)
            acc[...] = jnp.zeros_like(acc)

        dhm = _dot(df_ref[...], wd_ref[0])
        av = a_ref[0].astype(F32)
        bv = b_ref[0].astype(F32)
        sg = jax.nn.sigmoid(av)
        sl = av * sg
        hm_ref[0] = (sl * bv).astype(BF16)
        da = (dhm * bv * (sg * (1.0 + av * (1.0 - sg)))).astype(BF16)
        db = (dhm * sl).astype(BF16)
        da_ref[0] = da
        db_ref[0] = db
        acc[...] += _dot(da, wg_ref[0]) + _dot(db, wu_ref[0])

        @pl.when(j == nf - 1)
        def _():
            dh2 = acc[...]
            x1v = x1_ref[...]
            uv = u_ref[...]
            dx1 = ALPHA * dz_ref[...] + dh2 * (1.0 + mod_ref[0, 4:5, :])
            xhat, rstd = _ln_stats(ALPHA * x_ref[...] + (1.0 + mod_ref[0, 2:3, :]) * uv)
            dz1 = _ln_bwd(dx1, xhat, rstd, lw_ref[...])
            du_ref[...] = ((1.0 + mod_ref[0, 2:3, :]) * dz1).astype(BF16)
            dxp_ref[...] = ALPHA * dz1
            upd = jnp.concatenate(
                [jnp.sum(dx1 * xhat, axis=0, keepdims=True), jnp.sum(dx1, axis=0, keepdims=True),
                 jnp.zeros((6, D), F32)], axis=0)
            dmu = jnp.concatenate(
                [jnp.zeros((2, D), F32), jnp.sum(dz1 * uv, axis=0, keepdims=True),
                 jnp.sum(dh2, axis=0, keepdims=True), jnp.sum(dh2 * x1v, axis=0, keepdims=True),
                 jnp.zeros((3, D), F32)], axis=0)

            @pl.when(i == 0)
            def _():
                st_ref[...] = upd

            @pl.when(i > 0)
            def _():
                st_ref[...] += upd

            @pl.when(i % tpb == 0)
            def _():
                dm_ref[0] = dmu

            @pl.when(i % tpb != 0)
            def _():
                dm_ref[0] += dmu

    row = lambda: pl.BlockSpec((tm, D), lambda i, j: (i, 0))
    ffb = lambda: pl.BlockSpec((1, tm, tf), lambda i, j: (j, i, 0))
    return pl.pallas_call(
        body, name="ffn_bwd", grid=(t // tm, nf),
        in_specs=[row(), ffb(), ffb(),
                  pl.BlockSpec((1, tf, D), lambda i, j: (j, 0, 0)), pl.BlockSpec((1, tf, D), lambda i, j: (j, 0, 0)),
                  pl.BlockSpec((1, D, tf), lambda i, j: (j, 0, 0)), row(), row(), row(),
                  pl.BlockSpec((1, 8, D), lambda i, j: (i // tpb, 0, 0)), pl.BlockSpec((1, D), lambda i, j: (0, 0))],
        out_specs=[ffb(), ffb(), ffb(), row(), row(), row(), pl.BlockSpec((8, D), lambda i, j: (0, 0)),
                   pl.BlockSpec((1, 8, D), lambda i, j: (i // tpb, 0, 0))],
        out_shape=[jax.ShapeDtypeStruct((nf, t, tf), BF16), jax.ShapeDtypeStruct((nf, t, tf), BF16),
                   jax.ShapeDtypeStruct((nf, t, tf), BF16), jax.ShapeDtypeStruct((t, D), BF16),
                   jax.ShapeDtypeStruct((t, D), BF16), jax.ShapeDtypeStruct((t, D), F32),
                   jax.ShapeDtypeStruct((8, D), F32), jax.ShapeDtypeStruct((nbatch, 8, D), F32)],
        scratch_shapes=[pltpu.VMEM((tm, D), F32)],
        compiler_params=_params(("arbitrary", "arbitrary"), 48),
    )(dz2, a, b, wg, wu, wd, x1, x2, u, mod8, ln1w)


def _adamw_math(w, g, m, v):
    m = B1 * m + (1.0 - B1) * g
    v = B2 * v + (1.0 - B2) * (g * g)
    m_hat = m / (1.0 - B1 ** STEP)
    v_hat = v / (1.0 - B2 ** STEP)
    return -LR * (m_hat / (jnp.sqrt(v_hat) + EPS) + WD * w), m, v


def _adamw(w, g, m, v, name):
    rows, cols = w.shape
    tr = rows
    for cand in (128, 64, 32, 16, 8):
        if rows % cand == 0:
            tr = cand
            break

    def body(w_ref, g_ref, m_ref, v_ref, d_ref, mo_ref, vo_ref):
        d, mn, vn = _adamw_math(w_ref[...], g_ref[...], m_ref[...], v_ref[...])
        d_ref[...] = d
        mo_ref[...] = mn
        vo_ref[...] = vn

    spec = pl.BlockSpec((tr, cols), lambda i: (i, 0))
    return pl.pallas_call(
        body, name=name, grid=(rows // tr,), in_specs=[spec] * 4, out_specs=[spec] * 3,
        out_shape=[jax.ShapeDtypeStruct((rows, cols), F32)] * 3,
        compiler_params=_params(("parallel",), 48),
    )(w, g, m, v)


def _adamw_halves(w, g_mine, g_sib, m, v, c_idx, name):
    rows, cols = w.shape
    hr = rows // 2
    tr = next(cand for cand in (128, 88, 64, 32, 16, 8) if hr % cand == 0)
    tph = hr // tr

    def body(c_ref, w_ref, gm_ref, gs_ref, m_ref, v_ref, g_ref, d_ref, mo_ref, vo_ref):
        g = jnp.where(pl.program_id(0) == c_ref[0], gm_ref[...], gs_ref[...])
        d, mn, vn = _adamw_math(w_ref[...], g, m_ref[...], v_ref[...])
        g_ref[...] = g
        d_ref[...] = d
        mo_ref[...] = mn
        vo_ref[...] = vn

    full = pl.BlockSpec((tr, cols), lambda h, i, c: (h * tph + i, 0))
    half = pl.BlockSpec((tr, cols), lambda h, i, c: (i, 0))
    return pl.pallas_call(
        body, name=name,
        grid_spec=pltpu.PrefetchScalarGridSpec(
            num_scalar_prefetch=1, grid=(2, tph), in_specs=[full, half, half, full, full], out_specs=[full] * 4),
        out_shape=[jax.ShapeDtypeStruct((rows, cols), F32)] * 4,
        compiler_params=_params(("parallel", "parallel"), 48),
    )(c_idx, w, g_mine, g_sib, m, v)


def _grad_w_ada(c_all, dmod_cols):
    def body(c_ref, d_ref, o_ref):
        c = c_ref[...]
        o_ref[...] = lax.dot_general(c * jax.nn.sigmoid(c), d_ref[...], (((0,), (0,)), ((), ())),
                                     preferred_element_type=F32, precision=HIGHEST)

    return pl.pallas_call(
        body, name="grad_w_ada", out_shape=jax.ShapeDtypeStruct((D, dmod_cols.shape[1]), F32),
        compiler_params=_params(vmem_mb=48),
    )(c_all, dmod_cols)


def _small_update(gath, w8, m8, v8):
    def body(g_ref, w_ref, m_ref, v_ref, go_ref, d_ref, mo_ref, vo_ref):
        g0 = g_ref[0, 0:1, :] + g_ref[0, 1:2, :]
        g1 = g_ref[0, 2:3, :]
        for dev in range(1, N_DEV):
            g0 = g0 + (g_ref[dev, 0:1, :] + g_ref[dev, 1:2, :])
            g1 = g1 + g_ref[dev, 2:3, :]
        w = w_ref[...]
        lb = jax.nn.sigmoid(w[1:2, O_LB0:O_LB1] - w[1:2, O_LB1:O_FOX])
        fac = lb * (1.0 - lb)
        g1 = jnp.concatenate([g1[:, :O_LB0], g1[:, O_LB0:O_LB1] * fac, -g1[:, O_LB1:O_FOX] * fac, g1[:, O_FOX:]],
                             axis=1)
        g = jnp.concatenate([g0, g1, jnp.zeros((6, SMALL_W), F32)], axis=0)
        d, mn, vn = _adamw_math(w, g, m_ref[...], v_ref[...])
        go_ref[...] = g
        d_ref[...] = d
        mo_ref[...] = mn
        vo_ref[...] = vn

    return pl.pallas_call(
        body, name="small_update", out_shape=[jax.ShapeDtypeStruct((8, SMALL_W), F32)] * 4,
        compiler_params=_params(vmem_mb=48),
    )(gath, w8, m8, v8)


def _pack_small(b_ada, ln1w, ln1b, ln2w, ln2b, norm_w, lb_logits, fox):
    row1 = jnp.concatenate([ln1w, ln1b, ln2w, ln2b, norm_w, lb_logits[0:1], lb_logits[1:2], fox,
                            jnp.zeros((1, SMALL_W - O_FOX - BH), F32)], axis=1)
    return jnp.concatenate([b_ada, row1, jnp.zeros((6, SMALL_W), F32)], axis=0)


def _unpack_small(p):
    r = p[1:2]
    lb = jnp.concatenate([r[:, O_LB0:O_LB1], r[:, O_LB1:O_FOX]], axis=0)
    return dict(b_ada=p[0:1], ln1_w=r[:, O_LN1W:O_LN1B], ln1_b=r[:, O_LN1B:O_LN2W], ln2_w=r[:, O_LN2W:O_LN2B],
                ln2_b=r[:, O_LN2B:O_NORM], hgrn_norm_w=r[:, O_NORM:O_LB0], lb_logits=lb,
                fox_f_bias=r[:, O_FOX:O_FOX + BH])


_BIG = ("w_in", "w_branch_a", "w_branch_b", "w_out", "w_ffn_gate", "w_ffn_up", "w_ffn_down")


def _cols_of_chips(stacked):
    return jnp.concatenate([stacked[k] for k in range(N_CHIPS)], axis=1)


def kernel(x, c, w_ada, b_ada, w_in, fox_f_bias, lb_logits, hgrn_norm_w, w_branch_a, w_branch_b, w_out, ln1_w, ln1_b, w_ffn_gate, w_ffn_up, w_ffn_down, ln2_w, ln2_b, loss_target, m_w_ada, m_b_ada, m_w_in, m_fox_f_bias, m_lb_logits, m_hgrn_norm_w, m_w_branch_a, m_w_branch_b, m_w_out, m_ln1_w, m_ln1_b, m_w_ffn_gate, m_w_ffn_up, m_w_ffn_down, m_ln2_w, m_ln2_b, v_w_ada, v_b_ada, v_w_in, v_fox_f_bias, v_lb_logits, v_hgrn_norm_w, v_w_branch_a, v_w_branch_b, v_w_out, v_ln1_w, v_ln1_b, v_w_ffn_gate, v_w_ffn_up, v_w_ffn_down, v_ln2_w, v_ln2_b):
    nbatch, seq, _ = x.shape
    t = nbatch * seq
    ax, ay, ac = lax.axis_index("x"), lax.axis_index("y"), lax.axis_index("c")
    chip = 2 * ax + ay
    dev = 2 * chip + ac
    chip_arr = jnp.reshape(chip, (1,)).astype(jnp.int32)
    core_arr = jnp.reshape(ac, (1,)).astype(jnp.int32)

    shard_w = dict(w_in=w_in[0], w_branch_a=w_branch_a[0], w_branch_b=w_branch_b[0], w_out=w_out[0],
                   w_ffn_gate=w_ffn_gate[0], w_ffn_up=w_ffn_up[0], w_ffn_down=w_ffn_down[0])
    shard_m = dict(w_in=m_w_in[0], w_branch_a=m_w_branch_a[0], w_branch_b=m_w_branch_b[0], w_out=m_w_out[0],
                   w_ffn_gate=m_w_ffn_gate[0], w_ffn_up=m_w_ffn_up[0], w_ffn_down=m_w_ffn_down[0])
    shard_v = dict(w_in=v_w_in[0], w_branch_a=v_w_branch_a[0], w_branch_b=v_w_branch_b[0], w_out=v_w_out[0],
                   w_ffn_gate=v_w_ffn_gate[0], w_ffn_up=v_w_ffn_up[0], w_ffn_down=v_w_ffn_down[0])

    shard16 = {n: shard_w[n].astype(BF16) for n in _BIG}

    def with_mine(gathered, n):
        return lax.dynamic_update_slice(gathered, shard16[n][None], (chip, 0, 0))

    w_p = _permute_cols(_cols_of_chips(with_mine(_gather_weights([shard16["w_in"]])[0], "w_in")))
    late = _BIG[1:]
    late_send, late_recv, late_src, late_land, late_token = _split_start(
        _gather_copies, [shard16[n] for n in late],
        [lax.empty((N_CHIPS,) + shard16[n].shape, BF16) for n in late], "gather_late_start")

    c8 = jnp.concatenate([c, jnp.zeros((8 - nbatch, D), F32)], axis=0)
    c_all = _allgather8(c8, "gather_c")[:, :nbatch, :].reshape(N_DEV * nbatch, D)
    ncol = w_ada.shape[2]
    b_cols = lax.dynamic_slice_in_dim(b_ada, chip * ncol, ncol, axis=1)
    mod_g = _allgather8(_mod_shard(c_all, w_ada[0], b_cols), "gather_mod")
    mod_all = jnp.concatenate([mod_g[2 * k] for k in range(N_CHIPS)], axis=1)
    mod_mine = lax.dynamic_slice_in_dim(mod_all, dev * nbatch, nbatch, axis=0)
    mod8 = jnp.concatenate([mod_mine.reshape(nbatch, 6, D), jnp.zeros((nbatch, 2, D), F32)], axis=1)
    mod8 = mod8 + late_token[0, 0]

    x2 = x.reshape(t, D)
    tgt2 = loss_target.reshape(t, D)
    bias128 = jnp.concatenate([fox_f_bias, jnp.zeros((1, 128 - BH), F32)], axis=1)

    proj, h16 = _proj(x2, mod8, w_p, seq, BF16, "proj")
    projf = _rows_matmul(h16, w_p[:, COL_BF:], "proj_forget")
    ya, ckpt = _hgrn_fwd(proj, lb_logits, hgrn_norm_w, nbatch, seq)
    cum_cols = _fox_cum(projf, bias128, nbatch, seq)
    yb, lse = _fox_fwd(proj, cum_cols, nbatch, seq)
    late_land = _pass_to_sibling(
        _split_wait(_gather_copies, late_send, late_recv, late_src, late_land, yb, "gather_late_wait"))
    full = {n: with_mine(g, n) for n, g in zip(late, late_land)}
    wba, wbb = _cols_of_chips(full["w_branch_a"]), _cols_of_chips(full["w_branch_b"])
    wout = full["w_out"].reshape(D, D)
    wg, wu, wd = full["w_ffn_gate"], full["w_ffn_up"], full["w_ffn_down"]
    merged, u, x1 = _merge_fwd(ya, yb, proj, x2, mod8, wba, wbb, wout, ln1_w, ln1_b, seq)
    a_pre, b_pre, h2, dz2, st2, dm2 = _ffn_fwd(x1, mod8, wg, wu, wd, tgt2, ln2_w, ln2_b, seq)
    loss = lax.psum(st2[2, 0], ("x", "y", "c"))

    da, db, hmid, dffn, du, dxp, st1, dm1 = _ffn_bwd(
        dz2, a_pre, b_pre, jnp.swapaxes(wg, 1, 2), jnp.swapaxes(wu, 1, 2), jnp.swapaxes(wd, 1, 2),
        x1, x2, u, mod8, ln1_w, seq)
    g_st = {}
    g_st["w_ffn_down"] = _tn_matmul(hmid, dffn, "dw_ffn_down", seq)
    g_st["w_ffn_gate"] = _tn_matmul(h2, da, "dw_ffn_gate", seq)
    g_st["w_ffn_up"] = _tn_matmul(h2, db, "dw_ffn_up", seq)
    g_st["w_out"] = _tn_matmul(merged, du, "dw_out", seq).reshape(N_CHIPS, D // N_CHIPS, D)

    def sum_over_cores(names, tag):
        g_list = [g_st[n] for n in names]
        return [_add_my_half(g, o, core_arr, "grad_add_halves_" + n)
                for n, g, o in zip(names, g_list, _swap_halves(g_list, "grad_swap_halves_" + tag))]

    early = ("w_ffn_down", "w_ffn_gate", "w_ffn_up", "w_out")
    e_halves = sum_over_cores(early, "early")
    e_send, e_recv, e_src, e_land, e_token = _split_start(
        _scatter_copies, [h16 for _, h16 in e_halves],
        [lax.empty((3,) + h16.shape[1:], BF16) for _, h16 in e_halves], "grad_scatter_early_start")
    dproj, dpa, dpb, dya, dyb = _merge_bwd(du, ya, yb, proj, wba, wbb, wout, e_token, seq)
    g_st["w_branch_a"] = _tn_matmul(ya, dpa, "dw_branch_a", seq, split=D // N_CHIPS)
    g_st["w_branch_b"] = _tn_matmul(yb, dpb, "dw_branch_b", seq, split=D // N_CHIPS)
    dproj, dq, drs, dcs = _fox_bwd(proj, cum_cols, lse, yb, dyb, dproj, nbatch, seq)
    dproj = _place_cols(dproj, dq, COL_BQ)
    dproj, sm_fox = _fox_dbf(projf, bias128, drs, dcs, dproj, nbatch, seq)
    dproj, sm_hgrn = _hgrn_bwd(proj, dya, ckpt, lb_logits, hgrn_norm_w, dproj, nbatch, seq)
    grad_x2, dm0 = _dh_kernel(dproj, w_p, x2, dxp, mod8, seq)
    dw_in = _unpermute_cols(_tn_matmul(h16, dproj, "dw_in", seq))
    ncin = NIN // N_CHIPS
    g_st["w_in"] = jnp.stack([dw_in[:, k * ncin:(k + 1) * ncin] for k in range(N_CHIPS)])

    e_recv = _split_wait(_scatter_copies, e_send, e_recv, e_src, e_land, dw_in, "grad_scatter_early_wait")
    rest = ("w_in", "w_branch_a", "w_branch_b")
    r_halves = sum_over_cores(rest, "rest")
    r_recv = _scatter_chips([h16 for _, h16 in r_halves], "grad_scatter_rest")
    mine = {n: _add_chips(h32, r, chip_arr, "grad_add_chips_" + n)
            for n, (h32, _), r in zip(early + rest, e_halves + r_halves, list(e_recv) + list(r_recv))}
    g_mine = [mine[n] for n in _BIG]
    g_sib = _join_halves(g_mine)

    dmod = (dm0 + dm1 + dm2)[:, :6, :].reshape(nbatch, 6 * D)
    row2 = jnp.concatenate([st1[0:1], st1[1:2], st2[0:1], st2[1:2], sm_hgrn[1:2], sm_hgrn[0:1], sm_hgrn[0:1],
                            sm_fox[0:1, :BH], jnp.zeros((1, SMALL_W - O_FOX - BH), F32)], axis=1)
    spack = jnp.concatenate([dmod, row2, jnp.zeros((8 - nbatch - 1, SMALL_W), F32)], axis=0)
    gath = _allgather8(spack, "gather_small")
    w8 = _pack_small(b_ada, ln1_w, ln1_b, ln2_w, ln2_b, hgrn_norm_w, lb_logits, fox_f_bias)
    m8 = _pack_small(m_b_ada, m_ln1_w, m_ln1_b, m_ln2_w, m_ln2_b, m_hgrn_norm_w, m_lb_logits, m_fox_f_bias)
    v8 = _pack_small(v_b_ada, v_ln1_w, v_ln1_b, v_ln2_w, v_ln2_b, v_hgrn_norm_w, v_lb_logits, v_fox_f_bias)
    sg, sd, smn, svn = (_unpack_small(p) for p in _small_update(gath, w8, m8, v8))
    dmod_all = gath[:, :nbatch, :].reshape(N_DEV * nbatch, SMALL_W)
    g_ada = _grad_w_ada(c_all, lax.dynamic_slice_in_dim(dmod_all, chip * ncol, ncol, axis=1))

    grads = dict(sg)
    deltas = dict(sd)
    new_m = dict(smn)
    new_v = dict(svn)
    grads["w_ada"] = g_ada
    deltas["w_ada"], new_m["w_ada"], new_v["w_ada"] = _adamw(w_ada[0], g_ada, m_w_ada[0], v_w_ada[0], "adamw_w_ada")
    for n, gm, gs in zip(_BIG, g_mine, g_sib):
        grads[n], deltas[n], new_m[n], new_v[n] = _adamw_halves(shard_w[n], gm, gs, shard_m[n], shard_v[n], core_arr,
                                                                "adamw_" + n)

    names = ["w_ada", "b_ada", "w_in", "fox_f_bias", "lb_logits", "hgrn_norm_w", "w_branch_a", "w_branch_b", "w_out",
             "ln1_w", "ln1_b", "w_ffn_gate", "w_ffn_up", "w_ffn_down", "ln2_w", "ln2_b"]
    shapes = dict(w_ada=w_ada.shape, b_ada=b_ada.shape, w_in=w_in.shape, fox_f_bias=fox_f_bias.shape,
                  lb_logits=lb_logits.shape, hgrn_norm_w=hgrn_norm_w.shape, w_branch_a=w_branch_a.shape,
                  w_branch_b=w_branch_b.shape, w_out=w_out.shape, ln1_w=ln1_w.shape, ln1_b=ln1_b.shape,
                  w_ffn_gate=w_ffn_gate.shape, w_ffn_up=w_ffn_up.shape, w_ffn_down=w_ffn_down.shape,
                  ln2_w=ln2_w.shape, ln2_b=ln2_b.shape)
    outs = [loss, grad_x2.reshape(x.shape)]
    for group in (grads, deltas, new_m, new_v):
        outs += [group[n].reshape(shapes[n]) for n in names]
    return tuple(outs)
```

```python
import functools
import math

import jax
import jax.numpy as jnp
import numpy as np
from jax import lax
from jax.experimental import pallas as pl
from jax.experimental.pallas import tpu as pltpu

F32 = jnp.float32
BF16 = jnp.bfloat16
MESH = pl.DeviceIdType.MESH
HIGHEST = lax.Precision.HIGHEST

D = 1024
AW = 512
AH = 4
ADH = 128
BH = 8
BDH = 64
DFF = 2816
NIN = 5640
NP = 5760
N_CHIPS = 4
N_DEV = 8
HGRN_BLOCK = 256
FFN_TOKENS = 512
COL_GATES = 0
COL_BQ = 2048
COL_KV = 2560
COL_A = 3584
COL_BF = 5632
ALPHA = 2.0 ** 0.25
LN_EPS = 1e-5
RMS_EPS = 1e-6
NEG = -1e30
LOG2E = 1.4426950408889634
LR, B1, B2, EPS, WD, STEP = 0.001, 0.9, 0.999, 1e-08, 0.01, 10
SMALL_W = 6144
O_LN1W, O_LN1B, O_LN2W, O_LN2B, O_NORM, O_LB0, O_LB1, O_FOX = 0, 1024, 2048, 3072, 4096, 4608, 5120, 5632


def _params(sem=None, vmem_mb=None):
    kw = {}
    if sem is not None:
        kw["dimension_semantics"] = sem
    if vmem_mb is not None:
        kw["vmem_limit_bytes"] = vmem_mb << 20
    return pltpu.CompilerParams(**kw)


def _dot(a, b):
    return jnp.dot(a.astype(BF16), b.astype(BF16), preferred_element_type=F32)


def _dot_nt(a, b):
    return lax.dot_general(a.astype(BF16), b.astype(BF16), (((1,), (1,)), ((), ())), preferred_element_type=F32)


def _dot_tn(a, b):
    return lax.dot_general(a.astype(BF16), b.astype(BF16), (((0,), (0,)), ((), ())), preferred_element_type=F32)


def _dot_f32(a, b):
    return jnp.dot(a, b, preferred_element_type=F32, precision=HIGHEST)


def _perm_segments():
    segs = [(3592, 5640), (2048, 2560)]
    for p in range(4):
        segs += [(2560 + 128 * p, 2688 + 128 * p), (3072 + 128 * p, 3200 + 128 * p)]
    for h in range(4):
        segs += [(128 * h + 512 * t, 128 * h + 512 * t + 128) for t in range(4)]
    segs += [(3584, 3592)]
    return segs


def _permute_cols(w):
    parts = [w[:, a:b] for a, b in _perm_segments()]
    parts.append(jnp.zeros((w.shape[0], NP - NIN), w.dtype))
    return jnp.concatenate(parts, axis=1)


def _unpermute_cols(g):
    pos, where = 0, {}
    for a, b in _perm_segments():
        where[a] = (pos, pos + b - a)
        pos += b - a
    parts = [g[:, where[a][0]:where[a][1]] for a in sorted(where)]
    return jnp.concatenate(parts, axis=1)


def _allgather8(v, name):
    rows, cols = v.shape

    def body(x_ref, out_ref, send_sems, recv_sems, local_sem):
        x, y, c = lax.axis_index("x"), lax.axis_index("y"), lax.axis_index("c")
        me, sibling = (x, y, c), (x, y, 1 - c)
        chips = [(1 - x, y), (x, 1 - y), (1 - x, 1 - y)]

        def slot(px, py, pc):
            return out_ref.at[4 * px + 2 * py + pc]

        def copy(k, block, to, src=None):
            return pltpu.make_async_remote_copy(
                src_ref=slot(*block) if src is None else src, dst_ref=slot(*block),
                send_sem=send_sems.at[k], recv_sem=recv_sems.at[k], device_id=to, device_id_type=MESH)

        mine = pltpu.make_async_copy(x_ref, slot(*me), local_sem)
        mine.start()
        first = [copy(0, me, sibling, src=x_ref)]
        first += [copy(1 + j, me, (*chip, c), src=x_ref) for j, chip in enumerate(chips)]
        for cp in first:
            cp.start()
        passed = [copy(4 + j, (*chip, c), sibling) for j, chip in enumerate(chips)]
        for j, chip in enumerate(chips):
            copy(1 + j, (*chip, c), me).wait_recv()
            passed[j].start()
        copy(0, sibling, me).wait_recv()
        for j, chip in enumerate(chips):
            copy(4 + j, (*chip, 1 - c), me).wait_recv()
        for cp in first + passed:
            cp.wait_send()
        mine.wait()

    return pl.pallas_call(
        body, name=name,
        out_shape=jax.ShapeDtypeStruct((N_DEV, rows, cols), v.dtype),
        in_specs=[pl.BlockSpec(memory_space=pltpu.VMEM)],
        out_specs=pl.BlockSpec(memory_space=pltpu.VMEM),
        scratch_shapes=[pltpu.SemaphoreType.DMA((7,)), pltpu.SemaphoreType.DMA((7,)), pltpu.SemaphoreType.DMA],
    )(v)


def _hbm_specs(n):
    return [pl.BlockSpec(memory_space=pl.ANY)] * n


def _gather_weights(shards):
    n = len(shards)

    def body(*refs):
        ins, outs, (send_sems, recv_sems) = refs[:n], refs[n:2 * n], refs[2 * n:]
        x, y, c = lax.axis_index("x"), lax.axis_index("y"), lax.axis_index("c")
        sibling = (x, y, 1 - c)
        chips = [(1 - x, y), (x, 1 - y), (1 - x, 1 - y)]

        def blk(w, px, py, half):
            hr = ins[w].shape[0] // 2
            return outs[w].at[2 * px + py, pl.ds(half * hr, hr), :]

        def copy(w, k, block, to, src=None):
            return pltpu.make_async_remote_copy(
                src_ref=blk(w, *block) if src is None else src, dst_ref=blk(w, *block),
                send_sem=send_sems.at[6 * w + k], recv_sem=recv_sems.at[6 * w + k], device_id=to, device_id_type=MESH)

        first = []
        for w in range(n):
            hr = ins[w].shape[0] // 2
            my_half = ins[w].at[pl.ds(c * hr, hr), :]
            first += [copy(w, j, (x, y, c), (*chip, c), src=my_half) for j, chip in enumerate(chips)]
        for cp in first:
            cp.start()
        passed = []
        for j, chip in enumerate(chips):
            for w in range(n):
                copy(w, j, (*chip, c), (x, y, c)).wait_recv()
                passed.append(copy(w, 3 + j, (*chip, c), sibling))
                passed[-1].start()
        for j, chip in enumerate(chips):
            for w in range(n):
                copy(w, 3 + j, (*chip, 1 - c), (x, y, c)).wait_recv()
        for cp in first + passed:
            cp.wait_send()

    return pl.pallas_call(
        body, name="gather_weights",
        out_shape=[jax.ShapeDtypeStruct((N_CHIPS,) + s.shape, s.dtype) for s in shards],
        in_specs=_hbm_specs(n), out_specs=_hbm_specs(n),
        scratch_shapes=[pltpu.SemaphoreType.DMA((6 * n,)), pltpu.SemaphoreType.DMA((6 * n,))],
    )(*shards)


def _swap_halves(grads, name):
    n = len(grads)

    def body(*refs):
        ins, outs, (send_sems, recv_sems) = refs[:n], refs[n:2 * n], refs[2 * n:]
        x, y, c = lax.axis_index("x"), lax.axis_index("y"), lax.axis_index("c")
        cps = []
        for w in range(n):
            hr = ins[w].shape[1] // 2
            cps.append(pltpu.make_async_remote_copy(
                src_ref=ins[w].at[:, pl.ds((1 - c) * hr, hr), :], dst_ref=outs[w],
                send_sem=send_sems.at[w], recv_sem=recv_sems.at[w], device_id=(x, y, 1 - c), device_id_type=MESH))
        for cp in cps:
            cp.start()
        for cp in cps:
            cp.wait()

    return pl.pallas_call(
        body, name=name,
        out_shape=[jax.ShapeDtypeStruct((N_CHIPS, g.shape[1] // 2, g.shape[2]), g.dtype) for g in grads],
        in_specs=_hbm_specs(n), out_specs=_hbm_specs(n),
        scratch_shapes=[pltpu.SemaphoreType.DMA((n,)), pltpu.SemaphoreType.DMA((n,))],
    )(*grads)


def _scatter_chips(reds, name):
    n = len(reds)

    def body(*refs):
        ins, outs, (send_sems, recv_sems) = refs[:n], refs[n:2 * n], refs[2 * n:]
        x, y, c = lax.axis_index("x"), lax.axis_index("y"), lax.axis_index("c")
        chips = [(1 - x, y), (x, 1 - y), (1 - x, 1 - y)]
        cps = [pltpu.make_async_remote_copy(
            src_ref=ins[w].at[2 * chip[0] + chip[1]], dst_ref=outs[w].at[j],
            send_sem=send_sems.at[3 * w + j], recv_sem=recv_sems.at[3 * w + j],
            device_id=(*chip, c), device_id_type=MESH)
            for j, chip in enumerate(chips) for w in range(n)]
        for cp in cps:
            cp.start()
        for cp in cps:
            cp.wait()

    return pl.pallas_call(
        body, name=name,
        out_shape=[jax.ShapeDtypeStruct((3,) + r.shape[1:], r.dtype) for r in reds],
        in_specs=_hbm_specs(n), out_specs=_hbm_specs(n),
        scratch_shapes=[pltpu.SemaphoreType.DMA((3 * n,)), pltpu.SemaphoreType.DMA((3 * n,))],
    )(*reds)


def _join_halves(halves):
    n = len(halves)

    def body(*refs):
        ins, outs, (send_sems, recv_sems) = refs[:n], refs[n:2 * n], refs[2 * n:]
        x, y, c = lax.axis_index("x"), lax.axis_index("y"), lax.axis_index("c")
        cps = [pltpu.make_async_remote_copy(
            src_ref=ins[w], dst_ref=outs[w], send_sem=send_sems.at[w], recv_sem=recv_sems.at[w],
            device_id=(x, y, 1 - c), device_id_type=MESH) for w in range(n)]
        for cp in cps:
            cp.start()
        for cp in cps:
            cp.wait()

    return pl.pallas_call(
        body, name="grad_join_halves",
        out_shape=[jax.ShapeDtypeStruct(h.shape, h.dtype) for h in halves],
        in_specs=_hbm_specs(n), out_specs=_hbm_specs(n),
        scratch_shapes=[pltpu.SemaphoreType.DMA((n,)), pltpu.SemaphoreType.DMA((n,))],
    )(*halves)


def _in_hbm(v):
    return pltpu.with_memory_space_constraint(v, pltpu.HBM)


_SPLIT_COPY = pltpu.CompilerParams(has_side_effects=pltpu.SideEffectType.DATAFLOW_SIDE_EFFECTING)


def _gather_copies(srcs, lands, send_sems, recv_sems):
    x, y, c = lax.axis_index("x"), lax.axis_index("y"), lax.axis_index("c")
    cps = []
    for w, (src, land) in enumerate(zip(srcs, lands)):
        hr = src.shape[0] // 2
        for j, chip in enumerate([(1 - x, y), (x, 1 - y), (1 - x, 1 - y)]):
            cps.append(pltpu.make_async_remote_copy(
                src_ref=src.at[pl.ds(c * hr, hr), :], dst_ref=land.at[2 * x + y, pl.ds(c * hr, hr), :],
                send_sem=send_sems.at[3 * w + j], recv_sem=recv_sems.at[3 * w + j],
                device_id=(*chip, c), device_id_type=MESH))
    return cps


def _scatter_copies(srcs, lands, send_sems, recv_sems):
    x, y, c = lax.axis_index("x"), lax.axis_index("y"), lax.axis_index("c")
    cps = []
    for w, (src, land) in enumerate(zip(srcs, lands)):
        for j, chip in enumerate([(1 - x, y), (x, 1 - y), (1 - x, 1 - y)]):
            cps.append(pltpu.make_async_remote_copy(
                src_ref=src.at[2 * chip[0] + chip[1]], dst_ref=land.at[j],
                send_sem=send_sems.at[3 * w + j], recv_sem=recv_sems.at[3 * w + j],
                device_id=(*chip, c), device_id_type=MESH))
    return cps


def _split_start(copies, srcs, lands, name):
    n = len(srcs)

    def body(*refs):
        src, lnd, send_sems, recv_sems, token = refs[:n], refs[n:2 * n], refs[2 * n], refs[2 * n + 1], refs[-1]
        for cp in copies(src, lnd, send_sems, recv_sems):
            cp.start()
        token[...] = jnp.zeros_like(token)

    hbm = pl.BlockSpec(memory_space=pltpu.HBM)
    sem = pl.BlockSpec(memory_space=pltpu.SEMAPHORE)
    outs = pl.pallas_call(
        body, name=name,
        out_shape=(pltpu.SemaphoreType.DMA((3 * n,)), pltpu.SemaphoreType.DMA((3 * n,)),
                   *[pltpu.HBM(v.shape, v.dtype) for v in srcs + lands], jax.ShapeDtypeStruct((8, 128), F32)),
        in_specs=[hbm] * (2 * n),
        out_specs=(sem, sem, *([hbm] * (2 * n)), pl.BlockSpec(memory_space=pltpu.VMEM)),
        input_output_aliases={i: 2 + i for i in range(2 * n)},
        compiler_params=_SPLIT_COPY,
    )(*[_in_hbm(v) for v in srcs + lands])
    return outs[0], outs[1], list(outs[2:2 + n]), list(outs[2 + n:2 + 2 * n]), outs[-1]


def _split_wait(copies, send_sems, recv_sems, srcs, lands, after, name):
    n = len(srcs)

    def body(*refs):
        src, lnd, send_sems, recv_sems = refs[:n], refs[n:2 * n], refs[2 * n], refs[2 * n + 1]
        for cp in copies(src, lnd, send_sems, recv_sems):
            cp.wait_send()
            cp.wait_recv()

    hbm = pl.BlockSpec(memory_space=pltpu.HBM)
    sem = pl.BlockSpec(memory_space=pltpu.SEMAPHORE)
    outs = pl.pallas_call(
        body, name=name,
        out_shape=tuple(pltpu.HBM(v.shape, v.dtype) for v in srcs + lands),
        in_specs=[hbm] * (2 * n) + [sem, sem, pl.BlockSpec(memory_space=pl.ANY)],
        out_specs=tuple([hbm] * (2 * n)),
        input_output_aliases={i: i for i in range(2 * n)},
        compiler_params=_SPLIT_COPY,
    )(*srcs, *lands, send_sems, recv_sems, after)
    return list(outs[n:])


def _pass_to_sibling(lands):
    n = len(lands)

    def body(*refs):
        ins, outs, (send_sems, recv_sems) = refs[:n], refs[n:2 * n], refs[2 * n:]
        x, y, c = lax.axis_index("x"), lax.axis_index("y"), lax.axis_index("c")
        cps = []
        for w in range(n):
            hr = ins[w].shape[1] // 2
            for j, chip in enumerate([(1 - x, y), (x, 1 - y), (1 - x, 1 - y)]):
                k = 2 * chip[0] + chip[1]
                cps.append(pltpu.make_async_remote_copy(
                    src_ref=ins[w].at[k, pl.ds(c * hr, hr), :], dst_ref=outs[w].at[k, pl.ds(c * hr, hr), :],
                    send_sem=send_sems.at[3 * w + j], recv_sem=recv_sems.at[3 * w + j],
                    device_id=(x, y, 1 - c), device_id_type=MESH))
        for cp in cps:
            cp.start()
        for cp in cps:
            cp.wait()

    return pl.pallas_call(
        body, name="gather_late_pass",
        out_shape=[jax.ShapeDtypeStruct(v.shape, v.dtype) for v in lands],
        in_specs=_hbm_specs(n), out_specs=_hbm_specs(n),
        input_output_aliases={i: i for i in range(n)},
        scratch_shapes=[pltpu.SemaphoreType.DMA((3 * n,)), pltpu.SemaphoreType.DMA((3 * n,))],
    )(*lands)


def _row_tile(rows):
    for cand in (256, 176, 128, 64, 32, 16):
        if rows % cand == 0:
            return cand
    raise ValueError(rows)


def _add_my_half(g, other, c_idx, name):
    _, k, n = g.shape
    hr = k // 2
    tr = _row_tile(hr)
    nb = hr // tr

    def body(c_ref, g_ref, o_ref, out_ref, out16_ref):
        s = g_ref[...] + o_ref[...]
        out_ref[...] = s
        out16_ref[...] = s.astype(BF16)

    return pl.pallas_call(
        body, name=name,
        grid_spec=pltpu.PrefetchScalarGridSpec(
            num_scalar_prefetch=1, grid=(N_CHIPS, nb),
            in_specs=[pl.BlockSpec((1, tr, n), lambda j, i, c: (j, c[0] * nb + i, 0)),
                      pl.BlockSpec((1, tr, n), lambda j, i, c: (j, i, 0))],
            out_specs=[pl.BlockSpec((1, tr, n), lambda j, i, c: (j, i, 0)),
                       pl.BlockSpec((1, tr, n), lambda j, i, c: (j, i, 0))]),
        out_shape=[jax.ShapeDtypeStruct((N_CHIPS, hr, n), F32), jax.ShapeDtypeStruct((N_CHIPS, hr, n), BF16)],
        compiler_params=_params(("parallel", "parallel")),
    )(c_idx, g, other)


def _add_chips(red, recv, chip_idx, name):
    _, hr, n = red.shape
    tr = _row_tile(hr)

    def body(k_ref, r_ref, v_ref, out_ref):
        out_ref[...] = ((r_ref[0] + v_ref[0].astype(F32)) + v_ref[1].astype(F32)) + v_ref[2].astype(F32)

    return pl.pallas_call(
        body, name=name,
        grid_spec=pltpu.PrefetchScalarGridSpec(
            num_scalar_prefetch=1, grid=(hr // tr,),
            in_specs=[pl.BlockSpec((1, tr, n), lambda i, k: (k[0], i, 0)),
                      pl.BlockSpec((3, tr, n), lambda i, k: (0, i, 0))],
            out_specs=pl.BlockSpec((tr, n), lambda i, k: (i, 0))),
        out_shape=jax.ShapeDtypeStruct((hr, n), F32),
        compiler_params=_params(("parallel",)),
    )(chip_idx, red, recv)


def _mod_shard(c_all, w_ada, b_ada):
    nb, cols = c_all.shape[0], w_ada.shape[1]

    def body(c_ref, w_ref, b_ref, o_ref):
        c = c_ref[...]
        o_ref[...] = _dot(c * jax.nn.sigmoid(c), w_ref[...]) + b_ref[...]

    return pl.pallas_call(
        body, name="mod_shard", out_shape=jax.ShapeDtypeStruct((nb, cols), F32),
        compiler_params=_params(vmem_mb=48),
    )(c_all, w_ada, b_ada)


def _proj(x2, mod8, w, seq, out_dtype, name):
    t = x2.shape[0]
    n = w.shape[1]
    tm, tn = min(2048, seq), min(1152, n)
    tpb = seq // tm

    def body(x_ref, mod_ref, w_ref, o_ref, h_ref):
        @pl.when(pl.program_id(1) == 0)
        def _():
            h_ref[...] = (x_ref[...] * (1.0 + mod_ref[0, 1:2, :]) + mod_ref[0, 0:1, :]).astype(BF16)
        o_ref[...] = jnp.dot(h_ref[...], w_ref[...], preferred_element_type=F32).astype(o_ref.dtype)

    return pl.pallas_call(
        body, name=name, grid=(t // tm, n // tn),
        in_specs=[pl.BlockSpec((tm, D), lambda i, j: (i, 0)),
                  pl.BlockSpec((1, 8, D), lambda i, j: (i // tpb, 0, 0)),
                  pl.BlockSpec((D, tn), lambda i, j: (0, j))],
        out_specs=[pl.BlockSpec((tm, tn), lambda i, j: (i, j)), pl.BlockSpec((tm, D), lambda i, j: (i, 0))],
        out_shape=[jax.ShapeDtypeStruct((t, n), out_dtype), jax.ShapeDtypeStruct((t, D), BF16)],
        compiler_params=_params(("parallel", "arbitrary"), 56),
    )(x2, mod8, w)


def _rows_matmul(a, w, name):
    t, k = a.shape
    n = w.shape[1]
    tm = 1024 if t % 1024 == 0 else t

    def body(a_ref, w_ref, o_ref):
        o_ref[...] = jnp.dot(a_ref[...], w_ref[...], preferred_element_type=F32)

    return pl.pallas_call(
        body, name=name, grid=(t // tm,),
        in_specs=[pl.BlockSpec((tm, k), lambda i: (i, 0)), pl.BlockSpec((k, n), lambda i: (0, 0))],
        out_specs=pl.BlockSpec((tm, n), lambda i: (i, 0)),
        out_shape=jax.ShapeDtypeStruct((t, n), F32),
        compiler_params=_params(("parallel",)),
    )(a, w)


def _tn_matmul(a, b, name, seq, split=None):
    a_st, b_st = a.ndim == 3, b.ndim == 3
    t, ka = a.shape[-2:]
    n = b.shape[-1]
    tt = min(1024, seq)
    nt = t // tt
    if a_st or b_st:
        steps, tn = (a.shape[0] if a_st else b.shape[0]), n
    else:
        tn = split
        if tn is None:
            tn = next(cand for cand in (1152, 1024, 1408, 512, n) if n % cand == 0)
        steps = n // tn
    stacked_out = a_st or b_st or split is not None

    def body(a_ref, b_ref, o_ref):
        part = _dot_tn(a_ref[0] if a_st else a_ref[...], b_ref[0] if b_st else b_ref[...])
        if stacked_out:
            part = part[None]

        @pl.when(pl.program_id(1) == 0)
        def _():
            o_ref[...] = part

        @pl.when(pl.program_id(1) > 0)
        def _():
            o_ref[...] += part

    if a_st:
        in_specs = [pl.BlockSpec((1, tt, ka), lambda j, k: (j, k, 0))]
    else:
        in_specs = [pl.BlockSpec((tt, ka), lambda j, k: (k, 0))]
    if b_st:
        in_specs.append(pl.BlockSpec((1, tt, n), lambda j, k: (j, k, 0)))
    else:
        in_specs.append(pl.BlockSpec((tt, tn), lambda j, k: (k, 0 if a_st else j)))
    if stacked_out:
        out_spec = pl.BlockSpec((1, ka, tn), lambda j, k: (j, 0, 0))
        out_shape = jax.ShapeDtypeStruct((steps, ka, tn), F32)
    else:
        out_spec = pl.BlockSpec((ka, tn), lambda j, k: (0, j))
        out_shape = jax.ShapeDtypeStruct((ka, n), F32)
    return pl.pallas_call(
        body, name=name, grid=(steps, nt), in_specs=in_specs, out_specs=out_spec, out_shape=out_shape,
        compiler_params=_params(("parallel", "arbitrary"), 56),
    )(a, b)


def _dh_kernel(dproj, w_p, x2, dxp, mod8, seq):
    t = x2.shape[0]
    tm, tk = min(1024, seq), 1152
    tpb = seq // tm
    nk = NP // tk
    nbatch = t // seq

    def body(dp_ref, w_ref, x_ref, dxp_ref, mod_ref, gx_ref, dm_ref, acc):
        i, k = pl.program_id(0), pl.program_id(1)

        @pl.when(k == 0)
        def _():
            acc[...] = jnp.zeros_like(acc)

        acc[...] += _dot_nt(dp_ref[...], w_ref[...])

        @pl.when(k == nk - 1)
        def _():
            dh = acc[...]
            gx_ref[...] = dxp_ref[...] + dh * (1.0 + mod_ref[0, 1:2, :])
            upd = jnp.concatenate(
                [jnp.sum(dh, axis=0, keepdims=True), jnp.sum(dh * x_ref[...], axis=0, keepdims=True),
                 jnp.zeros((6, D), F32)], axis=0)

            @pl.when(i % tpb == 0)
            def _():
                dm_ref[0] = upd

            @pl.when(i % tpb != 0)
            def _():
                dm_ref[0] += upd

    return pl.pallas_call(
        body, name="dh", grid=(t // tm, nk),
        in_specs=[pl.BlockSpec((tm, tk), lambda i, k: (i, k)),
                  pl.BlockSpec((D, tk), lambda i, k: (0, k)),
                  pl.BlockSpec((tm, D), lambda i, k: (i, 0)),
                  pl.BlockSpec((tm, D), lambda i, k: (i, 0)),
                  pl.BlockSpec((1, 8, D), lambda i, k: (i // tpb, 0, 0))],
        out_specs=[pl.BlockSpec((tm, D), lambda i, k: (i, 0)),
                   pl.BlockSpec((1, 8, D), lambda i, k: (i // tpb, 0, 0))],
        out_shape=[jax.ShapeDtypeStruct((t, D), F32), jax.ShapeDtypeStruct((nbatch, 8, D), F32)],
        scratch_shapes=[pltpu.VMEM((tm, D), F32)],
        compiler_params=_params(("arbitrary", "arbitrary"), 48),
    )(dproj, w_p, x2, dxp, mod8)


def _tri(n, upper):
    r = lax.broadcasted_iota(jnp.int32, (n, n), 0)
    c = lax.broadcasted_iota(jnp.int32, (n, n), 1)
    return jnp.where((c >= r) if upper else (c <= r), 1.0, 0.0).astype(F32)


@jax.custom_vjp
def _mm_nn(a, b):
    return _dot(a, b)


_mm_nn.defvjp(lambda a, b: (_dot(a, b), (a, b)),
              lambda res, g: (_dot_nt(g, res[1]), _dot_tn(res[0], g)))


@jax.custom_vjp
def _mm_nt(a, b):
    return _dot_nt(a, b)


_mm_nt.defvjp(lambda a, b: (_dot_nt(a, b), (a, b)),
              lambda res, g: (_dot(g, res[1]), _dot_tn(g, res[0])))


@jax.custom_vjp
def _mm_tn(a, b):
    return _dot_tn(a, b)


_mm_tn.defvjp(lambda a, b: (_dot_tn(a, b), (a, b)),
              lambda res, g: (_dot_nt(res[1], g), _dot(res[0], g)))


@jax.custom_vjp
def _cumsum_rows(x):
    return _dot_f32(_tri(x.shape[0], False), x)


_cumsum_rows.defvjp(lambda x: (_cumsum_rows(x), None),
                    lambda _, g: (_dot_f32(_tri(g.shape[0], True), g),))


@functools.partial(jax.custom_vjp, nondiff_argnums=(1,))
def _shift_rows(x, k):
    return pltpu.roll(x, k % x.shape[0], 0)


_shift_rows.defvjp(lambda x, k: (_shift_rows(x, k), None),
                   lambda k, _, g: (pltpu.roll(g, (-k) % g.shape[0], 0),))


def _group_ref(bc, m):
    n = bc.shape[0] // (2 * m)
    b3 = bc.reshape(n, 2 * m, ADH)
    row = lax.broadcasted_iota(jnp.int32, b3.shape, 1)
    ref = jnp.sum(jnp.where(row == m - 1, b3, 0.0), axis=1, keepdims=True)
    return jnp.broadcast_to(ref, b3.shape).reshape(bc.shape)


def _hgrn_block(q, fl, v, g, st, lb, nw):
    n = q.shape[0]
    f = lb + (1.0 - lb) * jax.nn.sigmoid(fl)
    kk = 1.0 - f
    lf = jnp.log(f)
    bc = _cumsum_rows(lf)
    row = lax.broadcasted_iota(jnp.int32, (n, ADH), 0)
    same = jnp.bitwise_xor(lax.broadcasted_iota(jnp.int32, (n, n), 0), lax.broadcasted_iota(jnp.int32, (n, n), 1))
    a = jnp.zeros((n, n), F32)
    m = 1
    while m < n:
        r = jnp.bitwise_and(row, 2 * m - 1)
        up, lo = r >= m, r < m
        if m == 1:
            aq, ak = lf, jnp.zeros_like(lf)
        elif m == 2:
            aq = jnp.where(r == 3, lf + _shift_rows(lf, 1), lf)
            ak = jnp.where(r == 0, _shift_rows(lf, -1), 0.0)
        else:
            ref = _group_ref(bc, m)
            aq, ak = bc - ref, ref - bc
        qt = jnp.where(up, q * jnp.exp(jnp.where(up, aq, 0.0)), 0.0)
        kt = jnp.where(lo, kk * jnp.exp(jnp.where(lo, ak, 0.0)), 0.0)
        a = a + jnp.where(same < 2 * m, _mm_nt(qt, kt), 0.0)
        m *= 2
    last = row == n - 1
    bl = jnp.sum(jnp.where(last, bc, 0.0), axis=0, keepdims=True)
    o = _mm_nn(a, v) + _mm_nt(q * jnp.exp(bc), st) + jnp.sum(q * kk, axis=-1, keepdims=True) * v
    st_new = st * jnp.exp(bl) + _mm_tn(v, kk * jnp.exp(bl - bc))
    rms = lax.rsqrt(jnp.mean(o * o, axis=-1, keepdims=True) + RMS_EPS)
    return o * rms * nw * jax.nn.sigmoid(g), st_new


def _hgrn_fwd(proj, lb_logits, norm_w, nbatch, seq):
    t = proj.shape[0]
    blk = min(HGRN_BLOCK, seq)
    nb = seq // blk

    def body(p_ref, lbl_ref, nw_ref, y_ref, ck_ref, st_s):
        @pl.when(pl.program_id(2) == 0)
        def _():
            st_s[...] = jnp.zeros_like(st_s)

        st = st_s[...]
        ck_ref[0] = st
        lb = jax.nn.sigmoid(lbl_ref[0:1, :] - lbl_ref[1:2, :])
        p = p_ref[...].astype(F32)
        y, st_new = _hgrn_block(p[:, 0:128], p[:, 128:256], p[:, 256:384], p[:, 384:512], st, lb, nw_ref[...])
        st_s[...] = st_new
        y_ref[...] = y.astype(y_ref.dtype)

    return pl.pallas_call(
        body, name="hgrn_fwd", grid=(AH, nbatch, nb),
        in_specs=[pl.BlockSpec((blk, 512), lambda h, b, i: (b * nb + i, COL_A // 512 + h)),
                  pl.BlockSpec((2, 128), lambda h, b, i: (0, h)),
                  pl.BlockSpec((1, 128), lambda h, b, i: (0, h))],
        out_specs=[pl.BlockSpec((blk, 128), lambda h, b, i: (b * nb + i, h)),
                   pl.BlockSpec((1, 128, 128), lambda h, b, i: ((h * nbatch + b) * nb + i, 0, 0))],
        out_shape=[jax.ShapeDtypeStruct((t, AW), BF16), jax.ShapeDtypeStruct((AH * nbatch * nb, 128, 128), F32)],
        scratch_shapes=[pltpu.VMEM((128, 128), F32)],
        compiler_params=_params(("parallel", "parallel", "arbitrary"), 48),
    )(proj, lb_logits, norm_w)


def _hgrn_bwd(proj, dya, ckpt, lb_logits, norm_w, dproj, nbatch, seq):
    t = proj.shape[0]
    blk = min(HGRN_BLOCK, seq)
    nb = seq // blk

    def body(p_ref, dy_ref, ck_ref, lbl_ref, nw_ref, dp_in, dp_ref, sm_ref, dst_s):
        del dp_in
        b_id, i = pl.program_id(1), pl.program_id(2)

        @pl.when(i == 0)
        def _():
            dst_s[...] = jnp.zeros_like(dst_s)

        lb = jax.nn.sigmoid(lbl_ref[0:1, :] - lbl_ref[1:2, :])
        p = p_ref[...].astype(F32)
        _, pullback = jax.vjp(_hgrn_block, p[:, 0:128], p[:, 128:256], p[:, 256:384], p[:, 384:512],
                              ck_ref[0], lb, nw_ref[...])
        dq, dfl, dv, dg, dst, dlb, dnw = pullback((dy_ref[...], dst_s[...]))
        dst_s[...] = dst
        dp_ref[:, 0:128] = dq.astype(dp_ref.dtype)
        dp_ref[:, 128:256] = dfl.astype(dp_ref.dtype)
        dp_ref[:, 256:384] = dv.astype(dp_ref.dtype)
        dp_ref[:, 384:512] = dg.astype(dp_ref.dtype)
        upd = jnp.concatenate([dlb, dnw, jnp.zeros((6, 128), F32)], axis=0)
        first = (b_id == 0) & (i == 0)

        @pl.when(first)
        def _():
            sm_ref[...] = upd

        @pl.when(jnp.logical_not(first))
        def _():
            sm_ref[...] += upd

    def rows(h, b, i):
        return b * nb + (nb - 1 - i)

    return pl.pallas_call(
        body, name="hgrn_bwd", grid=(AH, nbatch, nb),
        in_specs=[pl.BlockSpec((blk, 512), lambda h, b, i: (rows(h, b, i), COL_A // 512 + h)),
                  pl.BlockSpec((blk, 128), lambda h, b, i: (rows(h, b, i), h)),
                  pl.BlockSpec((1, 128, 128), lambda h, b, i: ((h * nbatch + b) * nb + (nb - 1 - i), 0, 0)),
                  pl.BlockSpec((2, 128), lambda h, b, i: (0, h)),
                  pl.BlockSpec((1, 128), lambda h, b, i: (0, h)),
                  pl.BlockSpec(memory_space=pl.ANY)],
        out_specs=[pl.BlockSpec((blk, 512), lambda h, b, i: (rows(h, b, i), COL_A // 512 + h)),
                   pl.BlockSpec((8, 128), lambda h, b, i: (0, h))],
        out_shape=[jax.ShapeDtypeStruct((t, NP), BF16), jax.ShapeDtypeStruct((8, AW), F32)],
        input_output_aliases={5: 0},
        scratch_shapes=[pltpu.VMEM((128, 128), F32)],
        compiler_params=_params(("parallel", "arbitrary", "arbitrary"), 48),
    )(proj, dya, ckpt, lb_logits, norm_w, dproj)


def _log_sigmoid(z):
    return jnp.minimum(z, 0.0) - jnp.log(1.0 + jnp.exp(-jnp.abs(z)))


def _fox_cum(proj, bias128, nbatch, seq):
    t = proj.shape[0]
    ts = min(512, seq)
    nb = seq // ts

    def body(p_ref, b_ref, c_ref, carry):
        @pl.when(pl.program_id(1) == 0)
        def _():
            carry[...] = jnp.zeros_like(carry)
        cum = _dot_f32(_tri(ts, False), _log_sigmoid(p_ref[...] + b_ref[...])) + carry[...]
        carry[...] = cum[ts - 1:ts, :]
        cum2 = cum * LOG2E
        lane = lax.broadcasted_iota(jnp.int32, (ts, 128), 1)
        for p in range(4):
            c_ref[p] = jnp.where(lane < 64, cum2[:, 2 * p:2 * p + 1], cum2[:, 2 * p + 1:2 * p + 2])

    return pl.pallas_call(
        body, name="fox_cum", grid=(nbatch, nb),
        in_specs=[pl.BlockSpec((ts, 128), lambda b, i: (b * nb + i, 0)),
                  pl.BlockSpec((1, 128), lambda b, i: (0, 0))],
        out_specs=pl.BlockSpec((4, ts, 128), lambda b, i: (0, b * nb + i, 0)),
        out_shape=jax.ShapeDtypeStruct((4, t, 128), F32),
        scratch_shapes=[pltpu.VMEM((1, 128), F32)],
        compiler_params=_params(("parallel", "arbitrary")),
    )(proj, bias128)


def _fox_scores_t(q_ref, kv_ref, cc_ref, hh, masked, tq, tk):
    kh = kv_ref[:, 64 * hh:64 * hh + 64].astype(BF16)
    qh = (q_ref[:, 64 * hh:64 * hh + 64] * (LOG2E * BDH ** -0.5)).astype(BF16)
    s = _dot_nt(kh, qh) - cc_ref[0, :, 64 * hh:64 * hh + 1]
    if masked:
        key = lax.broadcasted_iota(jnp.int32, (tk, tq), 0)
        qry = lax.broadcasted_iota(jnp.int32, (tk, tq), 1)
        s = jnp.where(key <= qry, s, NEG)
    return s, kh


def _causal_pairs(nq, key_major):
    if key_major:
        pairs = [(i, j) for j in range(nq) for i in range(j, nq)]
    else:
        pairs = [(i, j) for i in range(nq) for j in range(i + 1)]
    return (jnp.asarray([p[0] for p in pairs], jnp.int32), jnp.asarray([p[1] for p in pairs], jnp.int32))


def _with_ones_lane(x128, hh):
    lane = lax.broadcasted_iota(jnp.int32, x128.shape, 1)
    one = jnp.ones_like(x128)
    zero = jnp.zeros_like(x128)
    if hh == 0:
        return jnp.where(lane < 64, x128, jnp.where(lane == 64, one, zero))
    return jnp.where(lane >= 64, x128, jnp.where(lane == 0, one, zero))


def _fox_fwd(proj, cum_cols, nbatch, seq):
    t = proj.shape[0]
    tq = tk = min(512, seq)
    nq = seq // tq
    qi, kj = _causal_pairs(nq, key_major=False)

    def body(qi_ref, kj_ref, q_ref, kv_ref, cc_ref, o_ref, lse_ref, m_s, acc_s):
        s_id = pl.program_id(2)
        i, j = qi_ref[s_id], kj_ref[s_id]

        @pl.when(j == 0)
        def _():
            m_s[...] = jnp.full_like(m_s, NEG)
            acc_s[...] = jnp.zeros_like(acc_s)

        def step(masked):
            for hh in range(2):
                s, _ = _fox_scores_t(q_ref, kv_ref, cc_ref, hh, masked, tq, tk)
                m_prev = m_s[hh:hh + 1, :]
                m_new = jnp.maximum(m_prev, jnp.max(s, axis=0, keepdims=True))
                alpha = jnp.exp2(m_prev - m_new)
                p = jnp.exp2(s - m_new).astype(BF16)
                v_aug = _with_ones_lane(kv_ref[:, 128:256].astype(BF16), hh)
                acc_s[hh] = acc_s[hh] * alpha + _dot_tn(v_aug, p)
                m_s[hh:hh + 1, :] = m_new

        @pl.when(j < i)
        def _():
            step(False)

        @pl.when(j == i)
        def _():
            step(True)
            a0, a1 = acc_s[0], acc_s[1]
            l0, l1 = a0[64:65, :], a1[0:1, :]
            o_t = jnp.concatenate([a0[0:64, :] / l0, a1[64:128, :] / l1], axis=0)
            o_ref[...] = o_t.T.astype(o_ref.dtype)
            lse_ref[0, 0] = jnp.concatenate(
                [m_s[0:1, :] + jnp.log2(l0), m_s[1:2, :] + jnp.log2(l1), jnp.zeros((6, tq), F32)], axis=0)

    return pl.pallas_call(
        body, name="fox_fwd",
        grid_spec=pltpu.PrefetchScalarGridSpec(
            num_scalar_prefetch=2, grid=(nbatch, 4, qi.shape[0]),
            in_specs=[pl.BlockSpec((tq, 128), lambda b, p, s, qi, kj: (b * nq + qi[s], COL_BQ // 128 + p)),
                      pl.BlockSpec((tk, 256), lambda b, p, s, qi, kj: (b * nq + kj[s], COL_KV // 256 + p)),
                      pl.BlockSpec((1, tk, 128), lambda b, p, s, qi, kj: (p, b * nq + kj[s], 0))],
            out_specs=[pl.BlockSpec((tq, 128), lambda b, p, s, qi, kj: (b * nq + qi[s], p)),
                       pl.BlockSpec((1, 1, 8, tq), lambda b, p, s, qi, kj: (b, p, 0, qi[s]))],
            scratch_shapes=[pltpu.VMEM((8, tq), F32), pltpu.VMEM((2, 128, tq), F32)]),
        out_shape=[jax.ShapeDtypeStruct((t, 512), BF16), jax.ShapeDtypeStruct((nbatch, 4, 8, seq), F32)],
        compiler_params=_params(("parallel", "parallel", "arbitrary"), 48),
    )(qi, kj, proj, proj, cum_cols)


def _fox_bwd(proj, cum_cols, lse, yb, dyb, dproj, nbatch, seq):
    t = proj.shape[0]
    tq = tk = min(512, seq)
    nq = seq // tq
    scale = BDH ** -0.5
    qi, kj = _causal_pairs(nq, key_major=True)
    nsteps = qi.shape[0]

    def body(qi_ref, kj_ref, q_ref, kv_ref, cc_ref, lse_ref, o_ref, do_ref, dp_in,
             dkv_ref, dq_ref, drs_ref, dcs_ref, dk_s, dv_s, dqa_s):
        del dp_in
        hp, s_id = pl.program_id(1), pl.program_id(2)
        i, j = qi_ref[s_id], kj_ref[s_id]

        @pl.when(i == j)
        def _():
            dk_s[...] = jnp.zeros_like(dk_s)
            dv_s[...] = jnp.zeros_like(dv_s)

        @pl.when(s_id == 0)
        def _():
            dqa_s[...] = jnp.zeros_like(dqa_s)

        def step(masked):
            for hh in range(2):
                s, _ = _fox_scores_t(q_ref, kv_ref, cc_ref, hh, masked, tq, tk)
                p = jnp.exp2(s - lse_ref[0, 0, hh:hh + 1, :])
                doh = do_ref[:, 64 * hh:64 * hh + 64]
                dd = lax.dot_general(jnp.ones((8, 64), F32), doh * o_ref[:, 64 * hh:64 * hh + 64].astype(F32),
                                     (((1,), (1,)), ((), ())), preferred_element_type=F32, precision=HIGHEST)[0:1, :]
                doh = doh.astype(BF16)
                dp = _dot_nt(kv_ref[:, 128 + 64 * hh:192 + 64 * hh], doh)
                ds = (p * (dp - dd)).astype(BF16)
                dv_s[:, 64 * hh:64 * hh + 64] += _dot(p, doh)
                q_aug = _with_ones_lane((q_ref[...] * scale).astype(BF16), hh)
                dk_s[hh] += _dot(ds, q_aug)
                k_aug = _with_ones_lane(kv_ref[:, 0:128].astype(BF16), hh)
                dqa_s[i, hh] += _dot_tn(k_aug, ds)

        @pl.when(i == j)
        def _():
            step(True)

        @pl.when(i > j)
        def _():
            step(False)

        @pl.when(i == nq - 1)
        def _():
            lane = lax.broadcasted_iota(jnp.int32, (tk, 128), 1)
            k0, k1 = dk_s[0], dk_s[1]
            dkv_ref[:, 0:128] = jnp.where(lane < 64, k0, k1).astype(dkv_ref.dtype)
            dkv_ref[:, 128:256] = dv_s[...].astype(dkv_ref.dtype)
            dcs_ref[0] = jnp.where(lane == 2 * hp, k0[:, 64:65], jnp.where(lane == 2 * hp + 1, k1[:, 0:1], 0.0))

        @pl.when(s_id == nsteps - 1)
        def _():
            lane = lax.broadcasted_iota(jnp.int32, (tq, 128), 1)
            for blk in range(nq):
                a0 = dqa_s[blk, 0].T
                a1 = dqa_s[blk, 1].T
                rows = pl.ds(blk * tq, tq)
                dq_ref[rows, :] = (jnp.where(lane < 64, a0, a1) * scale).astype(dq_ref.dtype)
                drs_ref[0, rows, :] = jnp.where(lane == 2 * hp, a0[:, 64:65], jnp.where(lane == 2 * hp + 1, a1[:, 0:1], 0.0))

    return pl.pallas_call(
        body, name="fox_bwd",
        grid_spec=pltpu.PrefetchScalarGridSpec(
            num_scalar_prefetch=2, grid=(nbatch, 4, nsteps),
            in_specs=[pl.BlockSpec((tq, 128), lambda b, p, s, qi, kj: (b * nq + qi[s], COL_BQ // 128 + p)),
                      pl.BlockSpec((tk, 256), lambda b, p, s, qi, kj: (b * nq + kj[s], COL_KV // 256 + p)),
                      pl.BlockSpec((1, tk, 128), lambda b, p, s, qi, kj: (p, b * nq + kj[s], 0)),
                      pl.BlockSpec((1, 1, 8, tq), lambda b, p, s, qi, kj: (b, p, 0, qi[s])),
                      pl.BlockSpec((tq, 128), lambda b, p, s, qi, kj: (b * nq + qi[s], p)),
                      pl.BlockSpec((tq, 128), lambda b, p, s, qi, kj: (b * nq + qi[s], p)),
                      pl.BlockSpec(memory_space=pl.ANY)],
            out_specs=[pl.BlockSpec((tk, 256), lambda b, p, s, qi, kj: (b * nq + kj[s], COL_KV // 256 + p)),
                       pl.BlockSpec((seq, 128), lambda b, p, s, qi, kj: (b, p)),
                       pl.BlockSpec((1, seq, 128), lambda b, p, s, qi, kj: (p, b, 0)),
                       pl.BlockSpec((1, tk, 128), lambda b, p, s, qi, kj: (p, b * nq + kj[s], 0))],
            scratch_shapes=[pltpu.VMEM((2, tk, 128), F32), pltpu.VMEM((tk, 128), F32),
                            pltpu.VMEM((nq, 2, 128, tq), F32)]),
        out_shape=[jax.ShapeDtypeStruct((t, NP), BF16), jax.ShapeDtypeStruct((t, 512), BF16),
                   jax.ShapeDtypeStruct((4, t, 128), F32), jax.ShapeDtypeStruct((4, t, 128), F32)],
        input_output_aliases={8: 0},
        compiler_params=_params(("parallel", "parallel", "arbitrary"), 56),
    )(qi, kj, proj, proj, cum_cols, lse, yb, dyb, dproj)


def _place_cols(dproj, src, col):
    t, w = src.shape
    tm = 1024 if t % 1024 == 0 else t

    def body(s_ref, dp_in, o_ref):
        del dp_in
        o_ref[...] = s_ref[...]

    return pl.pallas_call(
        body, name="place_cols", grid=(t // tm,),
        in_specs=[pl.BlockSpec((tm, w), lambda i: (i, 0)), pl.BlockSpec(memory_space=pl.ANY)],
        out_specs=pl.BlockSpec((tm, w), lambda i: (i, col // w)),
        out_shape=jax.ShapeDtypeStruct(dproj.shape, dproj.dtype),
        input_output_aliases={1: 0},
        compiler_params=_params(("parallel",)),
    )(src, dproj)


def _fox_dbf(proj, bias128, drs, dcs, dproj, nbatch, seq):
    t = proj.shape[0]
    ts = min(512, seq)
    nb = seq // ts

    def body(p_ref, b_ref, dr_ref, dc_ref, dp_in, dp_ref, sm_ref, carry):
        del dp_in
        b_id, i = pl.program_id(0), pl.program_id(1)

        @pl.when(i == 0)
        def _():
            carry[...] = jnp.zeros_like(carry)

        dcum = (dr_ref[0] - dc_ref[0]) + (dr_ref[1] - dc_ref[1]) + (dr_ref[2] - dc_ref[2]) + (dr_ref[3] - dc_ref[3])
        rc = _dot_f32(_tri(ts, True), dcum) + carry[...]
        carry[...] = rc[0:1, :]
        z = p_ref[...] + b_ref[...]
        lane = lax.broadcasted_iota(jnp.int32, (ts, 128), 1)
        dz = jnp.where(lane < BH, rc * jax.nn.sigmoid(-z), 0.0)
        dp_ref[...] = dz.astype(dp_ref.dtype)
        upd = jnp.concatenate([jnp.sum(dz, axis=0, keepdims=True), jnp.zeros((7, 128), F32)], axis=0)
        first = (b_id == 0) & (i == 0)

        @pl.when(first)
        def _():
            sm_ref[...] = upd

        @pl.when(jnp.logical_not(first))
        def _():
            sm_ref[...] += upd

    def rows(b, i):
        return b * nb + (nb - 1 - i)

    return pl.pallas_call(
        body, name="fox_dbf", grid=(nbatch, nb),
        in_specs=[pl.BlockSpec((ts, 128), lambda b, i: (rows(b, i), 0)),
                  pl.BlockSpec((1, 128), lambda b, i: (0, 0)),
                  pl.BlockSpec((4, ts, 128), lambda b, i: (0, rows(b, i), 0)),
                  pl.BlockSpec((4, ts, 128), lambda b, i: (0, rows(b, i), 0)),
                  pl.BlockSpec(memory_space=pl.ANY)],
        out_specs=[pl.BlockSpec((ts, 128), lambda b, i: (rows(b, i), COL_BF // 128)),
                   pl.BlockSpec((8, 128), lambda b, i: (0, 0))],
        out_shape=[jax.ShapeDtypeStruct((t, NP), BF16), jax.ShapeDtypeStruct((8, 128), F32)],
        input_output_aliases={4: 0},
        scratch_shapes=[pltpu.VMEM((1, 128), F32)],
        compiler_params=_params(("arbitrary", "arbitrary")),
    )(proj, bias128, drs, dcs, dproj)


def _ln_stats(z):
    mu = jnp.mean(z, axis=-1, keepdims=True)
    zc = z - mu
    rstd = lax.rsqrt(jnp.mean(zc * zc, axis=-1, keepdims=True) + LN_EPS)
    return zc * rstd, rstd


def _ln_bwd(dy, xhat, rstd, w):
    dxh = dy * w
    return rstd * (dxh - jnp.mean(dxh, axis=-1, keepdims=True) - xhat * jnp.mean(dxh * xhat, axis=-1, keepdims=True))


def _merge_fwd(ya, yb, proj, x2, mod8, wba, wbb, wout, ln1w, ln1b, seq):
    t = x2.shape[0]
    tm = min(512, seq)
    tpb = seq // tm

    def body(ya_ref, yb_ref, g_ref, x_ref, mod_ref, wa_ref, wb_ref, wo_ref, lw_ref, lb_ref, mg_ref, u_ref, x1_ref):
        ga = jax.nn.sigmoid(g_ref[:, 0:D].astype(F32))
        gb = jax.nn.sigmoid(g_ref[:, D:2 * D].astype(F32))
        merged = (ga * jnp.dot(ya_ref[...], wa_ref[...], preferred_element_type=F32)
                  + gb * jnp.dot(yb_ref[...], wb_ref[...], preferred_element_type=F32))
        mg = merged.astype(BF16)
        mg_ref[...] = mg
        u = jnp.dot(mg, wo_ref[...], preferred_element_type=F32)
        u_ref[...] = u
        xhat, _ = _ln_stats(ALPHA * x_ref[...] + (1.0 + mod_ref[0, 2:3, :]) * u)
        x1_ref[...] = xhat * lw_ref[...] + lb_ref[...]

    tok = lambda w: pl.BlockSpec((tm, w), lambda i: (i, 0))
    full = lambda a: pl.BlockSpec(a.shape, lambda i: (0,) * a.ndim)
    return pl.pallas_call(
        body, name="merge_fwd", grid=(t // tm,),
        in_specs=[tok(512), tok(512), pl.BlockSpec((tm, 2048), lambda i: (i, COL_GATES // 2048)), tok(D),
                  pl.BlockSpec((1, 8, D), lambda i: (i // tpb, 0, 0)),
                  full(wba), full(wbb), full(wout), full(ln1w), full(ln1b)],
        out_specs=[tok(D), tok(D), tok(D)],
        out_shape=[jax.ShapeDtypeStruct((t, D), BF16), jax.ShapeDtypeStruct((t, D), F32),
                   jax.ShapeDtypeStruct((t, D), F32)],
        compiler_params=_params(("parallel",), 48),
    )(ya, yb, proj, x2, mod8, wba, wbb, wout, ln1w, ln1b)


def _merge_bwd(du, ya, yb, proj, wba, wbb, wout, token, seq):
    t = du.shape[0]
    tm = min(512, seq)

    def body(du_ref, ya_ref, yb_ref, g_ref, wa_ref, wb_ref, wo_ref, token_ref,
             dp_ref, dpa_ref, dpb_ref, dya_ref, dyb_ref):
        del token_ref
        ga = jax.nn.sigmoid(g_ref[:, 0:D].astype(F32))
        gb = jax.nn.sigmoid(g_ref[:, D:2 * D].astype(F32))
        dm = _dot_nt(du_ref[...], wo_ref[...])
        pa = jnp.dot(ya_ref[...], wa_ref[...], preferred_element_type=F32)
        pb = jnp.dot(yb_ref[...], wb_ref[...], preferred_element_type=F32)
        dpa = (dm * ga).astype(BF16)
        dpb = (dm * gb).astype(BF16)
        dpa_ref[...] = dpa
        dpb_ref[...] = dpb
        dp_ref[:, 0:D] = (dm * pa * ga * (1.0 - ga)).astype(BF16)
        dp_ref[:, D:2 * D] = (dm * pb * gb * (1.0 - gb)).astype(BF16)
        dya_ref[...] = _dot_nt(dpa, wa_ref[...])
        dyb_ref[...] = _dot_nt(dpb, wb_ref[...])

    tok = lambda w: pl.BlockSpec((tm, w), lambda i: (i, 0))
    full = lambda a: pl.BlockSpec(a.shape, lambda i: (0,) * a.ndim)
    return pl.pallas_call(
        body, name="merge_bwd", grid=(t // tm,),
        in_specs=[tok(D), tok(512), tok(512), pl.BlockSpec((tm, 2048), lambda i: (i, COL_GATES // 2048)),
                  full(wba), full(wbb), full(wout), full(token)],
        out_specs=[pl.BlockSpec((tm, 2048), lambda i: (i, COL_GATES // 2048)), tok(D), tok(D), tok(512), tok(512)],
        out_shape=[jax.ShapeDtypeStruct((t, NP), BF16), jax.ShapeDtypeStruct((t, D), BF16),
                   jax.ShapeDtypeStruct((t, D), BF16), jax.ShapeDtypeStruct((t, 512), F32),
                   jax.ShapeDtypeStruct((t, 512), F32)],
        compiler_params=_params(("parallel",), 48),
    )(du, ya, yb, proj, wba, wbb, wout, token)


def _ffn_fwd(x1, mod8, wg, wu, wd, target, ln2w, ln2b, seq):
    t = x1.shape[0]
    tm = min(FFN_TOKENS, seq)
    nf, _, tf = wg.shape
    tpb = seq // tm
    nbatch = t // seq

    def body(x_ref, mod_ref, wg_ref, wu_ref, wd_ref, t_ref, lw_ref, lb_ref,
             a_ref, b_ref, h_s, dz_ref, st_ref, dm_ref, acc):
        i, j = pl.program_id(0), pl.program_id(1)

        @pl.when(j == 0)
        def _():
            h_s[...] = (x_ref[...] * (1.0 + mod_ref[0, 4:5, :]) + mod_ref[0, 3:4, :]).astype(BF16)
            acc[...] = jnp.zeros_like(acc)

        a = jnp.dot(h_s[...], wg_ref[0], preferred_element_type=F32)
        b = jnp.dot(h_s[...], wu_ref[0], preferred_element_type=F32)
        a_ref[0] = a.astype(BF16)
        b_ref[0] = b.astype(BF16)
        acc[...] += _dot(a * jax.nn.sigmoid(a) * b, wd_ref[0])

        @pl.when(j == nf - 1)
        def _():
            ffn = acc[...]
            xhat, rstd = _ln_stats(ALPHA * x_ref[...] + (1.0 + mod_ref[0, 5:6, :]) * ffn)
            diff = xhat * lw_ref[...] + lb_ref[...] - t_ref[...]
            loss = 0.5 * jnp.sum(jnp.sum(diff * diff, axis=-1, keepdims=True), axis=0, keepdims=True) / D
            dy = diff * (1.0 / D)
            dz = _ln_bwd(dy, xhat, rstd, lw_ref[...])
            dz_ref[...] = dz
            lane = lax.broadcasted_iota(jnp.int32, (1, D), 1)
            upd = jnp.concatenate(
                [jnp.sum(dy * xhat, axis=0, keepdims=True), jnp.sum(dy, axis=0, keepdims=True),
                 jnp.where(lane == 0, loss, 0.0), jnp.zeros((5, D), F32)], axis=0)
            dmu = jnp.concatenate(
                [jnp.zeros((5, D), F32), jnp.sum(dz * ffn, axis=0, keepdims=True), jnp.zeros((2, D), F32)], axis=0)

            @pl.when(i == 0)
            def _():
                st_ref[...] = upd

            @pl.when(i > 0)
            def _():
                st_ref[...] += upd

            @pl.when(i % tpb == 0)
            def _():
                dm_ref[0] = dmu

            @pl.when(i % tpb != 0)
            def _():
                dm_ref[0] += dmu

    row = lambda: pl.BlockSpec((tm, D), lambda i, j: (i, 0))
    vec = lambda: pl.BlockSpec((1, D), lambda i, j: (0, 0))
    return pl.pallas_call(
        body, name="ffn_fwd", grid=(t // tm, nf),
        in_specs=[row(), pl.BlockSpec((1, 8, D), lambda i, j: (i // tpb, 0, 0)),
                  pl.BlockSpec((1, D, tf), lambda i, j: (j, 0, 0)), pl.BlockSpec((1, D, tf), lambda i, j: (j, 0, 0)),
                  pl.BlockSpec((1, tf, D), lambda i, j: (j, 0, 0)), row(), vec(), vec()],
        out_specs=[pl.BlockSpec((1, tm, tf), lambda i, j: (j, i, 0)), pl.BlockSpec((1, tm, tf), lambda i, j: (j, i, 0)),
                   row(), row(), pl.BlockSpec((8, D), lambda i, j: (0, 0)),
                   pl.BlockSpec((1, 8, D), lambda i, j: (i // tpb, 0, 0))],
        out_shape=[jax.ShapeDtypeStruct((nf, t, tf), BF16), jax.ShapeDtypeStruct((nf, t, tf), BF16),
                   jax.ShapeDtypeStruct((t, D), BF16),
                   jax.ShapeDtypeStruct((t, D), F32), jax.ShapeDtypeStruct((8, D), F32),
                   jax.ShapeDtypeStruct((nbatch, 8, D), F32)],
        scratch_shapes=[pltpu.VMEM((tm, D), F32)],
        compiler_params=_params(("arbitrary", "arbitrary"), 60),
    )(x1, mod8, wg, wu, wd, target, ln2w, ln2b)


def _ffn_bwd(dz2, a, b, wg, wu, wd, x1, x2, u, mod8, ln1w, seq):
    t = x1.shape[0]
    tm = min(512, seq)
    nf, tf, _ = wg.shape
    tpb = seq // tm
    nbatch = t // seq

    def body(dz_ref, a_ref, b_ref, wg_ref, wu_ref, wd_ref, x1_ref, x_ref, u_ref, mod_ref, lw_ref,
             da_ref, db_ref, hm_ref, df_ref, du_ref, dxp_ref, st_ref, dm_ref, acc):
        i, j = pl.program_id(0), pl.program_id(1)

        @pl.when(j == 0)
        def _():
            df_ref[...] = ((1.0 + mod_ref[0, 5:6, :]) * dz_ref[...]).astype(BF16)
            acc[...] = jnp.zeros_like(acc)

        dhm = _dot(df_ref[...], wd_ref[0])
        av = a_ref[0].astype(F32)
        bv = b_ref[0].astype(F32)
        sg = jax.nn.sigmoid(av)
        sl = av * sg
        hm_ref[0] = (sl * bv).astype(BF16)
        da = (dhm * bv * (sg * (1.0 + av * (1.0 - sg)))).astype(BF16)
        db = (dhm * sl).astype(BF16)
        da_ref[0] = da
        db_ref[0] = db
        acc[...] += _dot(da, wg_ref[0]) + _dot(db, wu_ref[0])

        @pl.when(j == nf - 1)
        def _():
            dh2 = acc[...]
            x1v = x1_ref[...]
            uv = u_ref[...]
            dx1 = ALPHA * dz_ref[...] + dh2 * (1.0 + mod_ref[0, 4:5, :])
            xhat, rstd = _ln_stats(ALPHA * x_ref[...] + (1.0 + mod_ref[0, 2:3, :]) * uv)
            dz1 = _ln_bwd(dx1, xhat, rstd, lw_ref[...])
            du_ref[...] = ((1.0 + mod_ref[0, 2:3, :]) * dz1).astype(BF16)
            dxp_ref[...] = ALPHA * dz1
            upd = jnp.concatenate(
                [jnp.sum(dx1 * xhat, axis=0, keepdims=True), jnp.sum(dx1, axis=0, keepdims=True),
                 jnp.zeros((6, D), F32)], axis=0)
            dmu = jnp.concatenate(
                [jnp.zeros((2, D), F32), jnp.sum(dz1 * uv, axis=0, keepdims=True),
                 jnp.sum(dh2, axis=0, keepdims=True), jnp.sum(dh2 * x1v, axis=0, keepdims=True),
                 jnp.zeros((3, D), F32)], axis=0)

            @pl.when(i == 0)
            def _():
                st_ref[...] = upd

            @pl.when(i > 0)
            def _():
                st_ref[...] += upd

            @pl.when(i % tpb == 0)
            def _():
                dm_ref[0] = dmu

            @pl.when(i % tpb != 0)
            def _():
                dm_ref[0] += dmu

    row = lambda: pl.BlockSpec((tm, D), lambda i, j: (i, 0))
    ffb = lambda: pl.BlockSpec((1, tm, tf), lambda i, j: (j, i, 0))
    return pl.pallas_call(
        body, name="ffn_bwd", grid=(t // tm, nf),
        in_specs=[row(), ffb(), ffb(),
                  pl.BlockSpec((1, tf, D), lambda i, j: (j, 0, 0)), pl.BlockSpec((1, tf, D), lambda i, j: (j, 0, 0)),
                  pl.BlockSpec((1, D, tf), lambda i, j: (j, 0, 0)), row(), row(), row(),
                  pl.BlockSpec((1, 8, D), lambda i, j: (i // tpb, 0, 0)), pl.BlockSpec((1, D), lambda i, j: (0, 0))],
        out_specs=[ffb(), ffb(), ffb(), row(), row(), row(), pl.BlockSpec((8, D), lambda i, j: (0, 0)),
                   pl.BlockSpec((1, 8, D), lambda i, j: (i // tpb, 0, 0))],
        out_shape=[jax.ShapeDtypeStruct((nf, t, tf), BF16), jax.ShapeDtypeStruct((nf, t, tf), BF16),
                   jax.ShapeDtypeStruct((nf, t, tf), BF16), jax.ShapeDtypeStruct((t, D), BF16),
                   jax.ShapeDtypeStruct((t, D), BF16), jax.ShapeDtypeStruct((t, D), F32),
                   jax.ShapeDtypeStruct((8, D), F32), jax.ShapeDtypeStruct((nbatch, 8, D), F32)],
        scratch_shapes=[pltpu.VMEM((tm, D), F32)],
        compiler_params=_params(("arbitrary", "arbitrary"), 48),
    )(dz2, a, b, wg, wu, wd, x1, x2, u, mod8, ln1w)


def _adamw_math(w, g, m, v):
    m = B1 * m + (1.0 - B1) * g
    v = B2 * v + (1.0 - B2) * (g * g)
    m_hat = m / (1.0 - B1 ** STEP)
    v_hat = v / (1.0 - B2 ** STEP)
    return -LR * (m_hat / (jnp.sqrt(v_hat) + EPS) + WD * w), m, v


def _adamw(w, g, m, v, name):
    rows, cols = w.shape
    tr = rows
    for cand in (128, 64, 32, 16, 8):
        if rows % cand == 0:
            tr = cand
            break

    def body(w_ref, g_ref, m_ref, v_ref, d_ref, mo_ref, vo_ref):
        d, mn, vn = _adamw_math(w_ref[...], g_ref[...], m_ref[...], v_ref[...])
        d_ref[...] = d
        mo_ref[...] = mn
        vo_ref[...] = vn

    spec = pl.BlockSpec((tr, cols), lambda i: (i, 0))
    return pl.pallas_call(
        body, name=name, grid=(rows // tr,), in_specs=[spec] * 4, out_specs=[spec] * 3,
        out_shape=[jax.ShapeDtypeStruct((rows, cols), F32)] * 3,
        compiler_params=_params(("parallel",), 48),
    )(w, g, m, v)


def _adamw_halves(w, g_mine, g_sib, m, v, c_idx, name):
    rows, cols = w.shape
    hr = rows // 2
    tr = next(cand for cand in (128, 88, 64, 32, 16, 8) if hr % cand == 0)
    tph = hr // tr

    def body(c_ref, w_ref, gm_ref, gs_ref, m_ref, v_ref, g_ref, d_ref, mo_ref, vo_ref):
        g = jnp.where(pl.program_id(0) == c_ref[0], gm_ref[...], gs_ref[...])
        d, mn, vn = _adamw_math(w_ref[...], g, m_ref[...], v_ref[...])
        g_ref[...] = g
        d_ref[...] = d
        mo_ref[...] = mn
        vo_ref[...] = vn

    full = pl.BlockSpec((tr, cols), lambda h, i, c: (h * tph + i, 0))
    half = pl.BlockSpec((tr, cols), lambda h, i, c: (i, 0))
    return pl.pallas_call(
        body, name=name,
        grid_spec=pltpu.PrefetchScalarGridSpec(
            num_scalar_prefetch=1, grid=(2, tph), in_specs=[full, half, half, full, full], out_specs=[full] * 4),
        out_shape=[jax.ShapeDtypeStruct((rows, cols), F32)] * 4,
        compiler_params=_params(("parallel", "parallel"), 48),
    )(c_idx, w, g_mine, g_sib, m, v)


def _grad_w_ada(c_all, dmod_cols):
    def body(c_ref, d_ref, o_ref):
        c = c_ref[...]
        o_ref[...] = lax.dot_general(c * jax.nn.sigmoid(c), d_ref[...], (((0,), (0,)), ((), ())),
                                     preferred_element_type=F32, precision=HIGHEST)

    return pl.pallas_call(
        body, name="grad_w_ada", out_shape=jax.ShapeDtypeStruct((D, dmod_cols.shape[1]), F32),
        compiler_params=_params(vmem_mb=48),
    )(c_all, dmod_cols)


def _small_update(gath, w8, m8, v8):
    def body(g_ref, w_ref, m_ref, v_ref, go_ref, d_ref, mo_ref, vo_ref):
        g0 = g_ref[0, 0:1, :] + g_ref[0, 1:2, :]
        g1 = g_ref[0, 2:3, :]
        for dev in range(1, N_DEV):
            g0 = g0 + (g_ref[dev, 0:1, :] + g_ref[dev, 1:2, :])
            g1 = g1 + g_ref[dev, 2:3, :]
        w = w_ref[...]
        lb = jax.nn.sigmoid(w[1:2, O_LB0:O_LB1] - w[1:2, O_LB1:O_FOX])
        fac = lb * (1.0 - lb)
        g1 = jnp.concatenate([g1[:, :O_LB0], g1[:, O_LB0:O_LB1] * fac, -g1[:, O_LB1:O_FOX] * fac, g1[:, O_FOX:]],
                             axis=1)
        g = jnp.concatenate([g0, g1, jnp.zeros((6, SMALL_W), F32)], axis=0)
        d, mn, vn = _adamw_math(w, g, m_ref[...], v_ref[...])
        go_ref[...] = g
        d_ref[...] = d
        mo_ref[...] = mn
        vo_ref[...] = vn

    return pl.pallas_call(
        body, name="small_update", out_shape=[jax.ShapeDtypeStruct((8, SMALL_W), F32)] * 4,
        compiler_params=_params(vmem_mb=48),
    )(gath, w8, m8, v8)


def _pack_small(b_ada, ln1w, ln1b, ln2w, ln2b, norm_w, lb_logits, fox):
    row1 = jnp.concatenate([ln1w, ln1b, ln2w, ln2b, norm_w, lb_logits[0:1], lb_logits[1:2], fox,
                            jnp.zeros((1, SMALL_W - O_FOX - BH), F32)], axis=1)
    return jnp.concatenate([b_ada, row1, jnp.zeros((6, SMALL_W), F32)], axis=0)


def _unpack_small(p):
    r = p[1:2]
    lb = jnp.concatenate([r[:, O_LB0:O_LB1], r[:, O_LB1:O_FOX]], axis=0)
    return dict(b_ada=p[0:1], ln1_w=r[:, O_LN1W:O_LN1B], ln1_b=r[:, O_LN1B:O_LN2W], ln2_w=r[:, O_LN2W:O_LN2B],
                ln2_b=r[:, O_LN2B:O_NORM], hgrn_norm_w=r[:, O_NORM:O_LB0], lb_logits=lb,
                fox_f_bias=r[:, O_FOX:O_FOX + BH])


_BIG = ("w_in", "w_branch_a", "w_branch_b", "w_out", "w_ffn_gate", "w_ffn_up", "w_ffn_down")


def _cols_of_chips(stacked):
    return jnp.concatenate([stacked[k] for k in range(N_CHIPS)], axis=1)


def kernel(x, c, w_ada, b_ada, w_in, fox_f_bias, lb_logits, hgrn_norm_w, w_branch_a, w_branch_b, w_out, ln1_w, ln1_b, w_ffn_gate, w_ffn_up, w_ffn_down, ln2_w, ln2_b, loss_target, m_w_ada, m_b_ada, m_w_in, m_fox_f_bias, m_lb_logits, m_hgrn_norm_w, m_w_branch_a, m_w_branch_b, m_w_out, m_ln1_w, m_ln1_b, m_w_ffn_gate, m_w_ffn_up, m_w_ffn_down, m_ln2_w, m_ln2_b, v_w_ada, v_b_ada, v_w_in, v_fox_f_bias, v_lb_logits, v_hgrn_norm_w, v_w_branch_a, v_w_branch_b, v_w_out, v_ln1_w, v_ln1_b, v_w_ffn_gate, v_w_ffn_up, v_w_ffn_down, v_ln2_w, v_ln2_b):
    nbatch, seq, _ = x.shape
    t = nbatch * seq
    ax, ay, ac = lax.axis_index("x"), lax.axis_index("y"), lax.axis_index("c")
    chip = 2 * ax + ay
    dev = 2 * chip + ac
    chip_arr = jnp.reshape(chip, (1,)).astype(jnp.int32)
    core_arr = jnp.reshape(ac, (1,)).astype(jnp.int32)

    shard_w = dict(w_in=w_in[0], w_branch_a=w_branch_a[0], w_branch_b=w_branch_b[0], w_out=w_out[0],
                   w_ffn_gate=w_ffn_gate[0], w_ffn_up=w_ffn_up[0], w_ffn_down=w_ffn_down[0])
    shard_m = dict(w_in=m_w_in[0], w_branch_a=m_w_branch_a[0], w_branch_b=m_w_branch_b[0], w_out=m_w_out[0],
                   w_ffn_gate=m_w_ffn_gate[0], w_ffn_up=m_w_ffn_up[0], w_ffn_down=m_w_ffn_down[0])
    shard_v = dict(w_in=v_w_in[0], w_branch_a=v_w_branch_a[0], w_branch_b=v_w_branch_b[0], w_out=v_w_out[0],
                   w_ffn_gate=v_w_ffn_gate[0], w_ffn_up=v_w_ffn_up[0], w_ffn_down=v_w_ffn_down[0])

    shard16 = {n: shard_w[n].astype(BF16) for n in _BIG}

    def with_mine(gathered, n):
        return lax.dynamic_update_slice(gathered, shard16[n][None], (chip, 0, 0))

    w_p = _permute_cols(_cols_of_chips(with_mine(_gather_weights([shard16["w_in"]])[0], "w_in")))
    late = _BIG[1:]
    late_send, late_recv, late_src, late_land, late_token = _split_start(
        _gather_copies, [shard16[n] for n in late],
        [lax.empty((N_CHIPS,) + shard16[n].shape, BF16) for n in late], "gather_late_start")

    c8 = jnp.concatenate([c, jnp.zeros((8 - nbatch, D), F32)], axis=0)
    c_all = _allgather8(c8, "gather_c")[:, :nbatch, :].reshape(N_DEV * nbatch, D)
    ncol = w_ada.shape[2]
    b_cols = lax.dynamic_slice_in_dim(b_ada, chip * ncol, ncol, axis=1)
    mod_g = _allgather8(_mod_shard(c_all, w_ada[0], b_cols), "gather_mod")
    mod_all = jnp.concatenate([mod_g[2 * k] for k in range(N_CHIPS)], axis=1)
    mod_mine = lax.dynamic_slice_in_dim(mod_all, dev * nbatch, nbatch, axis=0)
    mod8 = jnp.concatenate([mod_mine.reshape(nbatch, 6, D), jnp.zeros((nbatch, 2, D), F32)], axis=1)
    mod8 = mod8 + late_token[0, 0]

    x2 = x.reshape(t, D)
    tgt2 = loss_target.reshape(t, D)
    bias128 = jnp.concatenate([fox_f_bias, jnp.zeros((1, 128 - BH), F32)], axis=1)

    proj, h16 = _proj(x2, mod8, w_p, seq, BF16, "proj")
    projf = _rows_matmul(h16, w_p[:, COL_BF:], "proj_forget")
    ya, ckpt = _hgrn_fwd(proj, lb_logits, hgrn_norm_w, nbatch, seq)
    cum_cols = _fox_cum(projf, bias128, nbatch, seq)
    yb, lse = _fox_fwd(proj, cum_cols, nbatch, seq)
    late_land = _pass_to_sibling(
        _split_wait(_gather_copies, late_send, late_recv, late_src, late_land, yb, "gather_late_wait"))
    full = {n: with_mine(g, n) for n, g in zip(late, late_land)}
    wba, wbb = _cols_of_chips(full["w_branch_a"]), _cols_of_chips(full["w_branch_b"])
    wout = full["w_out"].reshape(D, D)
    wg, wu, wd = full["w_ffn_gate"], full["w_ffn_up"], full["w_ffn_down"]
    merged, u, x1 = _merge_fwd(ya, yb, proj, x2, mod8, wba, wbb, wout, ln1_w, ln1_b, seq)
    a_pre, b_pre, h2, dz2, st2, dm2 = _ffn_fwd(x1, mod8, wg, wu, wd, tgt2, ln2_w, ln2_b, seq)
    loss = lax.psum(st2[2, 0], ("x", "y", "c"))

    da, db, hmid, dffn, du, dxp, st1, dm1 = _ffn_bwd(
        dz2, a_pre, b_pre, jnp.swapaxes(wg, 1, 2), jnp.swapaxes(wu, 1, 2), jnp.swapaxes(wd, 1, 2),
        x1, x2, u, mod8, ln1_w, seq)
    g_st = {}
    g_st["w_ffn_down"] = _tn_matmul(hmid, dffn, "dw_ffn_down", seq)
    g_st["w_ffn_gate"] = _tn_matmul(h2, da, "dw_ffn_gate", seq)
    g_st["w_ffn_up"] = _tn_matmul(h2, db, "dw_ffn_up", seq)
    g_st["w_out"] = _tn_matmul(merged, du, "dw_out", seq).reshape(N_CHIPS, D // N_CHIPS, D)

    def sum_over_cores(names, tag):
        g_list = [g_st[n] for n in names]
        return [_add_my_half(g, o, core_arr, "grad_add_halves_" + n)
                for n, g, o in zip(names, g_list, _swap_halves(g_list, "grad_swap_halves_" + tag))]

    early = ("w_ffn_down", "w_ffn_gate", "w_ffn_up", "w_out")
    e_halves = sum_over_cores(early, "early")
    e_send, e_recv, e_src, e_land, e_token = _split_start(
        _scatter_copies, [h16 for _, h16 in e_halves],
        [lax.empty((3,) + h16.shape[1:], BF16) for _, h16 in e_halves], "grad_scatter_early_start")
    dproj, dpa, dpb, dya, dyb = _merge_bwd(du, ya, yb, proj, wba, wbb, wout, e_token, seq)
    g_st["w_branch_a"] = _tn_matmul(ya, dpa, "dw_branch_a", seq, split=D // N_CHIPS)
    g_st["w_branch_b"] = _tn_matmul(yb, dpb, "dw_branch_b", seq, split=D // N_CHIPS)
    dproj, dq, drs, dcs = _fox_bwd(proj, cum_cols, lse, yb, dyb, dproj, nbatch, seq)
    dproj = _place_cols(dproj, dq, COL_BQ)
    dproj, sm_fox = _fox_dbf(projf, bias128, drs, dcs, dproj, nbatch, seq)
    dproj, sm_hgrn = _hgrn_bwd(proj, dya, ckpt, lb_logits, hgrn_norm_w, dproj, nbatch, seq)
    grad_x2, dm0 = _dh_kernel(dproj, w_p, x2, dxp, mod8, seq)
    dw_in = _unpermute_cols(_tn_matmul(h16, dproj, "dw_in", seq))
    ncin = NIN // N_CHIPS
    g_st["w_in"] = jnp.stack([dw_in[:, k * ncin:(k + 1) * ncin] for k in range(N_CHIPS)])

    e_recv = _split_wait(_scatter_copies, e_send, e_recv, e_src, e_land, dw_in, "grad_scatter_early_wait")
    rest = ("w_in", "w_branch_a", "w_branch_b")
    r_halves = sum_over_cores(rest, "rest")
    r_recv = _scatter_chips([h16 for _, h16 in r_halves], "grad_scatter_rest")
    mine = {n: _add_chips(h32, r, chip_arr, "grad_add_chips_" + n)
            for n, (h32, _), r in zip(early + rest, e_halves + r_halves, list(e_recv) + list(r_recv))}
    g_mine = [mine[n] for n in _BIG]
    g_sib = _join_halves(g_mine)

    dmod = (dm0 + dm1 + dm2)[:, :6, :].reshape(nbatch, 6 * D)
    row2 = jnp.concatenate([st1[0:1], st1[1:2], st2[0:1], st2[1:2], sm_hgrn[1:2], sm_hgrn[0:1], sm_hgrn[0:1],
                            sm_fox[0:1, :BH], jnp.zeros((1, SMALL_W - O_FOX - BH), F32)], axis=1)
    spack = jnp.concatenate([dmod, row2, jnp.zeros((8 - nbatch - 1, SMALL_W), F32)], axis=0)
    gath = _allgather8(spack, "gather_small")
    w8 = _pack_small(b_ada, ln1_w, ln1_b, ln2_w, ln2_b, hgrn_norm_w, lb_logits, fox_f_bias)
    m8 = _pack_small(m_b_ada, m_ln1_w, m_ln1_b, m_ln2_w, m_ln2_b, m_hgrn_norm_w, m_lb_logits, m_fox_f_bias)
    v8 = _pack_small(v_b_ada, v_ln1_w, v_ln1_b, v_ln2_w, v_ln2_b, v_hgrn_norm_w, v_lb_logits, v_fox_f_bias)
    sg, sd, smn, svn = (_unpack_small(p) for p in _small_update(gath, w8, m8, v8))
    dmod_all = gath[:, :nbatch, :].reshape(N_DEV * nbatch, SMALL_W)
    g_ada = _grad_w_ada(c_all, lax.dynamic_slice_in_dim(dmod_all, chip * ncol, ncol, axis=1))

    grads = dict(sg)
    deltas = dict(sd)
    new_m = dict(smn)
    new_v = dict(svn)
    grads["w_ada"] = g_ada
    deltas["w_ada"], new_m["w_ada"], new_v["w_ada"] = _adamw(w_ada[0], g_ada, m_w_ada[0], v_w_ada[0], "adamw_w_ada")
    for n, gm, gs in zip(_BIG, g_mine, g_sib):
        grads[n], deltas[n], new_m[n], new_v[n] = _adamw_halves(shard_w[n], gm, gs, shard_m[n], shard_v[n], core_arr,
                                                                "adamw_" + n)

    names = ["w_ada", "b_ada", "w_in", "fox_f_bias", "lb_logits", "hgrn_norm_w", "w_branch_a", "w_branch_b", "w_out",
             "ln1_w", "ln1_b", "w_ffn_gate", "w_ffn_up", "w_ffn_down", "ln2_w", "ln2_b"]
    shapes = dict(w_ada=w_ada.shape, b_ada=b_ada.shape, w_in=w_in.shape, fox_f_bias=fox_f_bias.shape,
                  lb_logits=lb_logits.shape, hgrn_norm_w=hgrn_norm_w.shape, w_branch_a=w_branch_a.shape,
                  w_branch_b=w_branch_b.shape, w_out=w_out.shape, ln1_w=ln1_w.shape, ln1_b=ln1_b.shape,
                  w_ffn_gate=w_ffn_gate.shape, w_ffn_up=w_ffn_up.shape, w_ffn_down=w_ffn_down.shape,
                  ln2_w=ln2_w.shape, ln2_b=ln2_b.shape)
    outs = [loss, grad_x2.reshape(x.shape)]
    for group in (grads, deltas, new_m, new_v):
        outs += [group[n].reshape(shapes[n]) for n in names]
    return tuple(outs)
```

```python
import functools
import math

import jax
import jax.numpy as jnp
import numpy as np
from jax import lax
from jax.experimental import pallas as pl
from jax.experimental.pallas import tpu as pltpu

F32 = jnp.float32
BF16 = jnp.bfloat16
MESH = pl.DeviceIdType.MESH
HIGHEST = lax.Precision.HIGHEST

D = 1024
AW = 512
AH = 4
ADH = 128
BH = 8
BDH = 64
DFF = 2816
NIN = 5640
NP = 5760
N_CHIPS = 4
N_DEV = 8
HGRN_BLOCK = 256
FFN_TOKENS = 512
COL_GATES = 0
COL_BQ = 2048
COL_KV = 2560
COL_A = 3584
COL_BF = 5632
ALPHA = 2.0 ** 0.25
LN_EPS = 1e-5
RMS_EPS = 1e-6
NEG = -1e30
LOG2E = 1.4426950408889634
LR, B1, B2, EPS, WD, STEP = 0.001, 0.9, 0.999, 1e-08, 0.01, 10
SMALL_W = 6144
O_LN1W, O_LN1B, O_LN2W, O_LN2B, O_NORM, O_LB0, O_LB1, O_FOX = 0, 1024, 2048, 3072, 4096, 4608, 5120, 5632


def _params(sem=None, vmem_mb=None):
    kw = {}
    if sem is not None:
        kw["dimension_semantics"] = sem
    if vmem_mb is not None:
        kw["vmem_limit_bytes"] = vmem_mb << 20
    return pltpu.CompilerParams(**kw)


def _dot(a, b):
    return jnp.dot(a.astype(BF16), b.astype(BF16), preferred_element_type=F32)


def _dot_nt(a, b):
    return lax.dot_general(a.astype(BF16), b.astype(BF16), (((1,), (1,)), ((), ())), preferred_element_type=F32)


def _dot_tn(a, b):
    return lax.dot_general(a.astype(BF16), b.astype(BF16), (((0,), (0,)), ((), ())), preferred_element_type=F32)


def _dot_f32(a, b):
    return jnp.dot(a, b, preferred_element_type=F32, precision=HIGHEST)


def _perm_segments():
    segs = [(3592, 5640), (2048, 2560)]
    for p in range(4):
        segs += [(2560 + 128 * p, 2688 + 128 * p), (3072 + 128 * p, 3200 + 128 * p)]
    for h in range(4):
        segs += [(128 * h + 512 * t, 128 * h + 512 * t + 128) for t in range(4)]
    segs += [(3584, 3592)]
    return segs


def _permute_cols(w):
    parts = [w[:, a:b] for a, b in _perm_segments()]
    parts.append(jnp.zeros((w.shape[0], NP - NIN), w.dtype))
    return jnp.concatenate(parts, axis=1)


def _unpermute_cols(g):
    pos, where = 0, {}
    for a, b in _perm_segments():
        where[a] = (pos, pos + b - a)
        pos += b - a
    parts = [g[:, where[a][0]:where[a][1]] for a in sorted(where)]
    return jnp.concatenate(parts, axis=1)


def _allgather8(v, name):
    rows, cols = v.shape

    def body(x_ref, out_ref, send_sems, recv_sems, local_sem):
        x, y, c = lax.axis_index("x"), lax.axis_index("y"), lax.axis_index("c")
        me, sibling = (x, y, c), (x, y, 1 - c)
        chips = [(1 - x, y), (x, 1 - y), (1 - x, 1 - y)]

        def slot(px, py, pc):
            return out_ref.at[4 * px + 2 * py + pc]

        def copy(k, block, to, src=None):
            return pltpu.make_async_remote_copy(
                src_ref=slot(*block) if src is None else src, dst_ref=slot(*block),
                send_sem=send_sems.at[k], recv_sem=recv_sems.at[k], device_id=to, device_id_type=MESH)

        mine = pltpu.make_async_copy(x_ref, slot(*me), local_sem)
        mine.start()
        first = [copy(0, me, sibling, src=x_ref)]
        first += [copy(1 + j, me, (*chip, c), src=x_ref) for j, chip in enumerate(chips)]
        for cp in first:
            cp.start()
        passed = [copy(4 + j, (*chip, c), sibling) for j, chip in enumerate(chips)]
        for j, chip in enumerate(chips):
            copy(1 + j, (*chip, c), me).wait_recv()
            passed[j].start()
        copy(0, sibling, me).wait_recv()
        for j, chip in enumerate(chips):
            copy(4 + j, (*chip, 1 - c), me).wait_recv()
        for cp in first + passed:
            cp.wait_send()
        mine.wait()

    return pl.pallas_call(
        body, name=name,
        out_shape=jax.ShapeDtypeStruct((N_DEV, rows, cols), v.dtype),
        in_specs=[pl.BlockSpec(memory_space=pltpu.VMEM)],
        out_specs=pl.BlockSpec(memory_space=pltpu.VMEM),
        scratch_shapes=[pltpu.SemaphoreType.DMA((7,)), pltpu.SemaphoreType.DMA((7,)), pltpu.SemaphoreType.DMA],
    )(v)


def _hbm_specs(n):
    return [pl.BlockSpec(memory_space=pl.ANY)] * n


def _gather_weights(shards):
    n = len(shards)

    def body(*refs):
        ins, outs, (send_sems, recv_sems) = refs[:n], refs[n:2 * n], refs[2 * n:]
        x, y, c = lax.axis_index("x"), lax.axis_index("y"), lax.axis_index("c")
        sibling = (x, y, 1 - c)
        chips = [(1 - x, y), (x, 1 - y), (1 - x, 1 - y)]

        def blk(w, px, py, half):
            hr = ins[w].shape[0] // 2
            return outs[w].at[2 * px + py, pl.ds(half * hr, hr), :]

        def copy(w, k, block, to, src=None):
            return pltpu.make_async_remote_copy(
                src_ref=blk(w, *block) if src is None else src, dst_ref=blk(w, *block),
                send_sem=send_sems.at[6 * w + k], recv_sem=recv_sems.at[6 * w + k], device_id=to, device_id_type=MESH)

        first = []
        for w in range(n):
            hr = ins[w].shape[0] // 2
            my_half = ins[w].at[pl.ds(c * hr, hr), :]
            first += [copy(w, j, (x, y, c), (*chip, c), src=my_half) for j, chip in enumerate(chips)]
        for cp in first:
            cp.start()
        passed = []
        for j, chip in enumerate(chips):
            for w in range(n):
                copy(w, j, (*chip, c), (x, y, c)).wait_recv()
                passed.append(copy(w, 3 + j, (*chip, c), sibling))
                passed[-1].start()
        for j, chip in enumerate(chips):
            for w in range(n):
                copy(w, 3 + j, (*chip, 1 - c), (x, y, c)).wait_recv()
        for cp in first + passed:
            cp.wait_send()

    return pl.pallas_call(
        body, name="gather_weights",
        out_shape=[jax.ShapeDtypeStruct((N_CHIPS,) + s.shape, s.dtype) for s in shards],
        in_specs=_hbm_specs(n), out_specs=_hbm_specs(n),
        scratch_shapes=[pltpu.SemaphoreType.DMA((6 * n,)), pltpu.SemaphoreType.DMA((6 * n,))],
    )(*shards)


def _swap_halves(grads, name):
    n = len(grads)

    def body(*refs):
        ins, outs, (send_sems, recv_sems) = refs[:n], refs[n:2 * n], refs[2 * n:]
        x, y, c = lax.axis_index("x"), lax.axis_index("y"), lax.axis_index("c")
        cps = []
        for w in range(n):
            hr = ins[w].shape[1] // 2
            cps.append(pltpu.make_async_remote_copy(
                src_ref=ins[w].at[:, pl.ds((1 - c) * hr, hr), :], dst_ref=outs[w],
                send_sem=send_sems.at[w], recv_sem=recv_sems.at[w], device_id=(x, y, 1 - c), device_id_type=MESH))
        for cp in cps:
            cp.start()
        for cp in cps:
            cp.wait()

    return pl.pallas_call(
        body, name=name,
        out_shape=[jax.ShapeDtypeStruct((N_CHIPS, g.shape[1] // 2, g.shape[2]), g.dtype) for g in grads],
        in_specs=_hbm_specs(n), out_specs=_hbm_specs(n),
        scratch_shapes=[pltpu.SemaphoreType.DMA((n,)), pltpu.SemaphoreType.DMA((n,))],
    )(*grads)


def _scatter_chips(reds, name):
    n = len(reds)

    def body(*refs):
        ins, outs, (send_sems, recv_sems) = refs[:n], refs[n:2 * n], refs[2 * n:]
        x, y, c = lax.axis_index("x"), lax.axis_index("y"), lax.axis_index("c")
        chips = [(1 - x, y), (x, 1 - y), (1 - x, 1 - y)]
        cps = [pltpu.make_async_remote_copy(
            src_ref=ins[w].at[2 * chip[0] + chip[1]], dst_ref=outs[w].at[j],
            send_sem=send_sems.at[3 * w + j], recv_sem=recv_sems.at[3 * w + j],
            device_id=(*chip, c), device_id_type=MESH)
            for j, chip in enumerate(chips) for w in range(n)]
        for cp in cps:
            cp.start()
        for cp in cps:
            cp.wait()

    return pl.pallas_call(
        body, name=name,
        out_shape=[jax.ShapeDtypeStruct((3,) + r.shape[1:], r.dtype) for r in reds],
        in_specs=_hbm_specs(n), out_specs=_hbm_specs(n),
        scratch_shapes=[pltpu.SemaphoreType.DMA((3 * n,)), pltpu.SemaphoreType.DMA((3 * n,))],
    )(*reds)


def _join_halves(halves, token, name):
    n = len(halves)

    def body(*refs):
        ins, outs, (send_sems, recv_sems) = refs[:n], refs[n + 1:2 * n + 1], refs[2 * n + 1:]
        x, y, c = lax.axis_index("x"), lax.axis_index("y"), lax.axis_index("c")
        cps = [pltpu.make_async_remote_copy(
            src_ref=ins[w], dst_ref=outs[w], send_sem=send_sems.at[w], recv_sem=recv_sems.at[w],
            device_id=(x, y, 1 - c), device_id_type=MESH) for w in range(n)]
        for cp in cps:
            cp.start()
        for cp in cps:
            cp.wait()

    return pl.pallas_call(
        body, name=name,
        out_shape=[jax.ShapeDtypeStruct(h.shape, h.dtype) for h in halves],
        in_specs=_hbm_specs(n + 1), out_specs=_hbm_specs(n),
        scratch_shapes=[pltpu.SemaphoreType.DMA((n,)), pltpu.SemaphoreType.DMA((n,))],
    )(*halves, token)


def _in_hbm(v):
    return pltpu.with_memory_space_constraint(v, pltpu.HBM)


_SPLIT_COPY = pltpu.CompilerParams(has_side_effects=pltpu.SideEffectType.DATAFLOW_SIDE_EFFECTING)


def _gather_copies(srcs, lands, send_sems, recv_sems):
    x, y, c = lax.axis_index("x"), lax.axis_index("y"), lax.axis_index("c")
    cps = []
    for w, (src, land) in enumerate(zip(srcs, lands)):
        hr = src.shape[0] // 2
        for j, chip in enumerate([(1 - x, y), (x, 1 - y), (1 - x, 1 - y)]):
            cps.append(pltpu.make_async_remote_copy(
                src_ref=src.at[pl.ds(c * hr, hr), :], dst_ref=land.at[2 * x + y, pl.ds(c * hr, hr), :],
                send_sem=send_sems.at[3 * w + j], recv_sem=recv_sems.at[3 * w + j],
                device_id=(*chip, c), device_id_type=MESH))
    return cps


def _scatter_copies(srcs, lands, send_sems, recv_sems):
    x, y, c = lax.axis_index("x"), lax.axis_index("y"), lax.axis_index("c")
    cps = []
    for w, (src, land) in enumerate(zip(srcs, lands)):
        for j, chip in enumerate([(1 - x, y), (x, 1 - y), (1 - x, 1 - y)]):
            cps.append(pltpu.make_async_remote_copy(
                src_ref=src.at[2 * chip[0] + chip[1]], dst_ref=land.at[j],
                send_sem=send_sems.at[3 * w + j], recv_sem=recv_sems.at[3 * w + j],
                device_id=(*chip, c), device_id_type=MESH))
    return cps


def _split_start(copies, srcs, lands, name):
    n = len(srcs)

    def body(*refs):
        src, lnd, send_sems, recv_sems, token = refs[:n], refs[n:2 * n], refs[2 * n], refs[2 * n + 1], refs[-1]
        for cp in copies(src, lnd, send_sems, recv_sems):
            cp.start()
        token[...] = jnp.zeros_like(token)

    hbm = pl.BlockSpec(memory_space=pltpu.HBM)
    sem = pl.BlockSpec(memory_space=pltpu.SEMAPHORE)
    outs = pl.pallas_call(
        body, name=name,
        out_shape=(pltpu.SemaphoreType.DMA((3 * n,)), pltpu.SemaphoreType.DMA((3 * n,)),
                   *[pltpu.HBM(v.shape, v.dtype) for v in srcs + lands], jax.ShapeDtypeStruct((8, 128), F32)),
        in_specs=[hbm] * (2 * n),
        out_specs=(sem, sem, *([hbm] * (2 * n)), pl.BlockSpec(memory_space=pltpu.VMEM)),
        input_output_aliases={i: 2 + i for i in range(2 * n)},
        compiler_params=_SPLIT_COPY,
    )(*[_in_hbm(v) for v in srcs + lands])
    return outs[0], outs[1], list(outs[2:2 + n]), list(outs[2 + n:2 + 2 * n]), outs[-1]


def _split_wait(copies, send_sems, recv_sems, srcs, lands, after, name):
    n = len(srcs)

    def body(*refs):
        src, lnd, send_sems, recv_sems = refs[:n], refs[n:2 * n], refs[2 * n], refs[2 * n + 1]
        for cp in copies(src, lnd, send_sems, recv_sems):
            cp.wait_send()
            cp.wait_recv()

    hbm = pl.BlockSpec(memory_space=pltpu.HBM)
    sem = pl.BlockSpec(memory_space=pltpu.SEMAPHORE)
    outs = pl.pallas_call(
        body, name=name,
        out_shape=tuple(pltpu.HBM(v.shape, v.dtype) for v in srcs + lands),
        in_specs=[hbm] * (2 * n) + [sem, sem, pl.BlockSpec(memory_space=pl.ANY)],
        out_specs=tuple([hbm] * (2 * n)),
        input_output_aliases={i: i for i in range(2 * n)},
        compiler_params=_SPLIT_COPY,
    )(*srcs, *lands, send_sems, recv_sems, after)
    return list(outs[n:])


def _pass_to_sibling(lands):
    n = len(lands)

    def body(*refs):
        ins, outs, (send_sems, recv_sems) = refs[:n], refs[n:2 * n], refs[2 * n:]
        x, y, c = lax.axis_index("x"), lax.axis_index("y"), lax.axis_index("c")
        cps = []
        for w in range(n):
            hr = ins[w].shape[1] // 2
            for j, chip in enumerate([(1 - x, y), (x, 1 - y), (1 - x, 1 - y)]):
                k = 2 * chip[0] + chip[1]
                cps.append(pltpu.make_async_remote_copy(
                    src_ref=ins[w].at[k, pl.ds(c * hr, hr), :], dst_ref=outs[w].at[k, pl.ds(c * hr, hr), :],
                    send_sem=send_sems.at[3 * w + j], recv_sem=recv_sems.at[3 * w + j],
                    device_id=(x, y, 1 - c), device_id_type=MESH))
        for cp in cps:
            cp.start()
        for cp in cps:
            cp.wait()

    return pl.pallas_call(
        body, name="gather_late_pass",
        out_shape=[jax.ShapeDtypeStruct(v.shape, v.dtype) for v in lands],
        in_specs=_hbm_specs(n), out_specs=_hbm_specs(n),
        input_output_aliases={i: i for i in range(n)},
        scratch_shapes=[pltpu.SemaphoreType.DMA((3 * n,)), pltpu.SemaphoreType.DMA((3 * n,))],
    )(*lands)


def _row_tile(rows):
    for cand in (256, 176, 128, 64, 32, 16):
        if rows % cand == 0:
            return cand
    raise ValueError(rows)


def _add_my_half(g, other, c_idx, name):
    _, k, n = g.shape
    hr = k // 2
    tr = _row_tile(hr)
    nb = hr // tr

    def body(c_ref, g_ref, o_ref, out_ref, out16_ref):
        s = g_ref[...] + o_ref[...]
        out_ref[...] = s
        out16_ref[...] = s.astype(BF16)

    return pl.pallas_call(
        body, name=name,
        grid_spec=pltpu.PrefetchScalarGridSpec(
            num_scalar_prefetch=1, grid=(N_CHIPS, nb),
            in_specs=[pl.BlockSpec((1, tr, n), lambda j, i, c: (j, c[0] * nb + i, 0)),
                      pl.BlockSpec((1, tr, n), lambda j, i, c: (j, i, 0))],
            out_specs=[pl.BlockSpec((1, tr, n), lambda j, i, c: (j, i, 0)),
                       pl.BlockSpec((1, tr, n), lambda j, i, c: (j, i, 0))]),
        out_shape=[jax.ShapeDtypeStruct((N_CHIPS, hr, n), F32), jax.ShapeDtypeStruct((N_CHIPS, hr, n), BF16)],
        compiler_params=_params(("parallel", "parallel")),
    )(c_idx, g, other)


def _add_chips(red, recv, chip_idx, name):
    _, hr, n = red.shape
    tr = _row_tile(hr)

    def body(k_ref, r_ref, v_ref, out_ref):
        out_ref[...] = ((r_ref[0] + v_ref[0].astype(F32)) + v_ref[1].astype(F32)) + v_ref[2].astype(F32)

    return pl.pallas_call(
        body, name=name,
        grid_spec=pltpu.PrefetchScalarGridSpec(
            num_scalar_prefetch=1, grid=(hr // tr,),
            in_specs=[pl.BlockSpec((1, tr, n), lambda i, k: (k[0], i, 0)),
                      pl.BlockSpec((3, tr, n), lambda i, k: (0, i, 0))],
            out_specs=pl.BlockSpec((tr, n), lambda i, k: (i, 0))),
        out_shape=jax.ShapeDtypeStruct((hr, n), F32),
        compiler_params=_params(("parallel",)),
    )(chip_idx, red, recv)


def _mod_shard(c_all, w_ada, b_ada):
    nb, cols = c_all.shape[0], w_ada.shape[1]

    def body(c_ref, w_ref, b_ref, o_ref):
        c = c_ref[...]
        o_ref[...] = _dot(c * jax.nn.sigmoid(c), w_ref[...]) + b_ref[...]

    return pl.pallas_call(
        body, name="mod_shard", out_shape=jax.ShapeDtypeStruct((nb, cols), F32),
        compiler_params=_params(vmem_mb=48),
    )(c_all, w_ada, b_ada)


def _proj(x2, mod8, w, seq, out_dtype, name):
    t = x2.shape[0]
    n = w.shape[1]
    tm, tn = min(2048, seq), min(1152, n)
    tpb = seq // tm

    def body(x_ref, mod_ref, w_ref, o_ref, h_ref):
        @pl.when(pl.program_id(1) == 0)
        def _():
            h_ref[...] = (x_ref[...] * (1.0 + mod_ref[0, 1:2, :]) + mod_ref[0, 0:1, :]).astype(BF16)
        o_ref[...] = jnp.dot(h_ref[...], w_ref[...], preferred_element_type=F32).astype(o_ref.dtype)

    return pl.pallas_call(
        body, name=name, grid=(t // tm, n // tn),
        in_specs=[pl.BlockSpec((tm, D), lambda i, j: (i, 0)),
                  pl.BlockSpec((1, 8, D), lambda i, j: (i // tpb, 0, 0)),
                  pl.BlockSpec((D, tn), lambda i, j: (0, j))],
        out_specs=[pl.BlockSpec((tm, tn), lambda i, j: (i, j)), pl.BlockSpec((tm, D), lambda i, j: (i, 0))],
        out_shape=[jax.ShapeDtypeStruct((t, n), out_dtype), jax.ShapeDtypeStruct((t, D), BF16)],
        compiler_params=_params(("parallel", "arbitrary"), 56),
    )(x2, mod8, w)


def _rows_matmul(a, w, name):
    t, k = a.shape
    n = w.shape[1]
    tm = 1024 if t % 1024 == 0 else t

    def body(a_ref, w_ref, o_ref):
        o_ref[...] = jnp.dot(a_ref[...], w_ref[...], preferred_element_type=F32)

    return pl.pallas_call(
        body, name=name, grid=(t // tm,),
        in_specs=[pl.BlockSpec((tm, k), lambda i: (i, 0)), pl.BlockSpec((k, n), lambda i: (0, 0))],
        out_specs=pl.BlockSpec((tm, n), lambda i: (i, 0)),
        out_shape=jax.ShapeDtypeStruct((t, n), F32),
        compiler_params=_params(("parallel",)),
    )(a, w)


def _tn_matmul(a, b, name, seq, split=None):
    a_st, b_st = a.ndim == 3, b.ndim == 3
    t, ka = a.shape[-2:]
    n = b.shape[-1]
    tt = min(1024, seq)
    nt = t // tt
    if a_st or b_st:
        steps, tn = (a.shape[0] if a_st else b.shape[0]), n
    else:
        tn = split
        if tn is None:
            tn = next(cand for cand in (1152, 1024, 1408, 512, n) if n % cand == 0)
        steps = n // tn
    stacked_out = a_st or b_st or split is not None

    def body(a_ref, b_ref, o_ref):
        part = _dot_tn(a_ref[0] if a_st else a_ref[...], b_ref[0] if b_st else b_ref[...])
        if stacked_out:
            part = part[None]

        @pl.when(pl.program_id(1) == 0)
        def _():
            o_ref[...] = part

        @pl.when(pl.program_id(1) > 0)
        def _():
            o_ref[...] += part

    if a_st:
        in_specs = [pl.BlockSpec((1, tt, ka), lambda j, k: (j, k, 0))]
    else:
        in_specs = [pl.BlockSpec((tt, ka), lambda j, k: (k, 0))]
    if b_st:
        in_specs.append(pl.BlockSpec((1, tt, n), lambda j, k: (j, k, 0)))
    else:
        in_specs.append(pl.BlockSpec((tt, tn), lambda j, k: (k, 0 if a_st else j)))
    if stacked_out:
        out_spec = pl.BlockSpec((1, ka, tn), lambda j, k: (j, 0, 0))
        out_shape = jax.ShapeDtypeStruct((steps, ka, tn), F32)
    else:
        out_spec = pl.BlockSpec((ka, tn), lambda j, k: (0, j))
        out_shape = jax.ShapeDtypeStruct((ka, n), F32)
    return pl.pallas_call(
        body, name=name, grid=(steps, nt), in_specs=in_specs, out_specs=out_spec, out_shape=out_shape,
        compiler_params=_params(("parallel", "arbitrary"), 56),
    )(a, b)


def _dh_kernel(dproj, w_p, x2, dxp, mod8, seq):
    t = x2.shape[0]
    tm, tk = min(1024, seq), 1152
    tpb = seq // tm
    nk = NP // tk
    nbatch = t // seq

    def body(dp_ref, w_ref, x_ref, dxp_ref, mod_ref, gx_ref, dm_ref, acc):
        i, k = pl.program_id(0), pl.program_id(1)

        @pl.when(k == 0)
        def _():
            acc[...] = jnp.zeros_like(acc)

        acc[...] += _dot_nt(dp_ref[...], w_ref[...])

        @pl.when(k == nk - 1)
        def _():
            dh = acc[...]
            gx_ref[...] = dxp_ref[...] + dh * (1.0 + mod_ref[0, 1:2, :])
            upd = jnp.concatenate(
                [jnp.sum(dh, axis=0, keepdims=True), jnp.sum(dh * x_ref[...], axis=0, keepdims=True),
                 jnp.zeros((6, D), F32)], axis=0)

            @pl.when(i % tpb == 0)
            def _():
                dm_ref[0] = upd

            @pl.when(i % tpb != 0)
            def _():
                dm_ref[0] += upd

    return pl.pallas_call(
        body, name="dh", grid=(t // tm, nk),
        in_specs=[pl.BlockSpec((tm, tk), lambda i, k: (i, k)),
                  pl.BlockSpec((D, tk), lambda i, k: (0, k)),
                  pl.BlockSpec((tm, D), lambda i, k: (i, 0)),
                  pl.BlockSpec((tm, D), lambda i, k: (i, 0)),
                  pl.BlockSpec((1, 8, D), lambda i, k: (i // tpb, 0, 0))],
        out_specs=[pl.BlockSpec((tm, D), lambda i, k: (i, 0)),
                   pl.BlockSpec((1, 8, D), lambda i, k: (i // tpb, 0, 0))],
        out_shape=[jax.ShapeDtypeStruct((t, D), F32), jax.ShapeDtypeStruct((nbatch, 8, D), F32)],
        scratch_shapes=[pltpu.VMEM((tm, D), F32)],
        compiler_params=_params(("arbitrary", "arbitrary"), 48),
    )(dproj, w_p, x2, dxp, mod8)


def _tri(n, upper):
    r = lax.broadcasted_iota(jnp.int32, (n, n), 0)
    c = lax.broadcasted_iota(jnp.int32, (n, n), 1)
    return jnp.where((c >= r) if upper else (c <= r), 1.0, 0.0).astype(F32)


@jax.custom_vjp
def _mm_nn(a, b):
    return _dot(a, b)


_mm_nn.defvjp(lambda a, b: (_dot(a, b), (a, b)),
              lambda res, g: (_dot_nt(g, res[1]), _dot_tn(res[0], g)))


@jax.custom_vjp
def _mm_nt(a, b):
    return _dot_nt(a, b)


_mm_nt.defvjp(lambda a, b: (_dot_nt(a, b), (a, b)),
              lambda res, g: (_dot(g, res[1]), _dot_tn(g, res[0])))


@jax.custom_vjp
def _mm_tn(a, b):
    return _dot_tn(a, b)


_mm_tn.defvjp(lambda a, b: (_dot_tn(a, b), (a, b)),
              lambda res, g: (_dot_nt(res[1], g), _dot(res[0], g)))


@jax.custom_vjp
def _cumsum_rows(x):
    return _dot_f32(_tri(x.shape[0], False), x)


_cumsum_rows.defvjp(lambda x: (_cumsum_rows(x), None),
                    lambda _, g: (_dot_f32(_tri(g.shape[0], True), g),))


@functools.partial(jax.custom_vjp, nondiff_argnums=(1,))
def _shift_rows(x, k):
    return pltpu.roll(x, k % x.shape[0], 0)


_shift_rows.defvjp(lambda x, k: (_shift_rows(x, k), None),
                   lambda k, _, g: (pltpu.roll(g, (-k) % g.shape[0], 0),))


def _group_ref(bc, m):
    n = bc.shape[0] // (2 * m)
    b3 = bc.reshape(n, 2 * m, ADH)
    row = lax.broadcasted_iota(jnp.int32, b3.shape, 1)
    ref = jnp.sum(jnp.where(row == m - 1, b3, 0.0), axis=1, keepdims=True)
    return jnp.broadcast_to(ref, b3.shape).reshape(bc.shape)


def _hgrn_block(q, fl, v, g, st, lb, nw):
    n = q.shape[0]
    f = lb + (1.0 - lb) * jax.nn.sigmoid(fl)
    kk = 1.0 - f
    lf = jnp.log(f)
    bc = _cumsum_rows(lf)
    row = lax.broadcasted_iota(jnp.int32, (n, ADH), 0)
    same = jnp.bitwise_xor(lax.broadcasted_iota(jnp.int32, (n, n), 0), lax.broadcasted_iota(jnp.int32, (n, n), 1))
    a = jnp.zeros((n, n), F32)
    m = 1
    while m < n:
        r = jnp.bitwise_and(row, 2 * m - 1)
        up, lo = r >= m, r < m
        if m == 1:
            aq, ak = lf, jnp.zeros_like(lf)
        elif m == 2:
            aq = jnp.where(r == 3, lf + _shift_rows(lf, 1), lf)
            ak = jnp.where(r == 0, _shift_rows(lf, -1), 0.0)
        else:
            ref = _group_ref(bc, m)
            aq, ak = bc - ref, ref - bc
        qt = jnp.where(up, q * jnp.exp(jnp.where(up, aq, 0.0)), 0.0)
        kt = jnp.where(lo, kk * jnp.exp(jnp.where(lo, ak, 0.0)), 0.0)
        a = a + jnp.where(same < 2 * m, _mm_nt(qt, kt), 0.0)
        m *= 2
    last = row == n - 1
    bl = jnp.sum(jnp.where(last, bc, 0.0), axis=0, keepdims=True)
    o = _mm_nn(a, v) + _mm_nt(q * jnp.exp(bc), st) + jnp.sum(q * kk, axis=-1, keepdims=True) * v
    st_new = st * jnp.exp(bl) + _mm_tn(v, kk * jnp.exp(bl - bc))
    rms = lax.rsqrt(jnp.mean(o * o, axis=-1, keepdims=True) + RMS_EPS)
    return o * rms * nw * jax.nn.sigmoid(g), st_new


def _hgrn_fwd(proj, lb_logits, norm_w, nbatch, seq):
    t = proj.shape[0]
    blk = min(HGRN_BLOCK, seq)
    nb = seq // blk

    def body(p_ref, lbl_ref, nw_ref, y_ref, ck_ref, st_s):
        @pl.when(pl.program_id(2) == 0)
        def _():
            st_s[...] = jnp.zeros_like(st_s)

        st = st_s[...]
        ck_ref[0] = st
        lb = jax.nn.sigmoid(lbl_ref[0:1, :] - lbl_ref[1:2, :])
        p = p_ref[...].astype(F32)
        y, st_new = _hgrn_block(p[:, 0:128], p[:, 128:256], p[:, 256:384], p[:, 384:512], st, lb, nw_ref[...])
        st_s[...] = st_new
        y_ref[...] = y.astype(y_ref.dtype)

    return pl.pallas_call(
        body, name="hgrn_fwd", grid=(AH, nbatch, nb),
        in_specs=[pl.BlockSpec((blk, 512), lambda h, b, i: (b * nb + i, COL_A // 512 + h)),
                  pl.BlockSpec((2, 128), lambda h, b, i: (0, h)),
                  pl.BlockSpec((1, 128), lambda h, b, i: (0, h))],
        out_specs=[pl.BlockSpec((blk, 128), lambda h, b, i: (b * nb + i, h)),
                   pl.BlockSpec((1, 128, 128), lambda h, b, i: ((h * nbatch + b) * nb + i, 0, 0))],
        out_shape=[jax.ShapeDtypeStruct((t, AW), BF16), jax.ShapeDtypeStruct((AH * nbatch * nb, 128, 128), F32)],
        scratch_shapes=[pltpu.VMEM((128, 128), F32)],
        compiler_params=_params(("parallel", "parallel", "arbitrary"), 48),
    )(proj, lb_logits, norm_w)


def _hgrn_bwd(proj, dya, ckpt, lb_logits, norm_w, dproj, nbatch, seq):
    t = proj.shape[0]
    blk = min(HGRN_BLOCK, seq)
    nb = seq // blk

    def body(p_ref, dy_ref, ck_ref, lbl_ref, nw_ref, dp_in, dp_ref, sm_ref, dst_s):
        del dp_in
        b_id, i = pl.program_id(1), pl.program_id(2)

        @pl.when(i == 0)
        def _():
            dst_s[...] = jnp.zeros_like(dst_s)

        lb = jax.nn.sigmoid(lbl_ref[0:1, :] - lbl_ref[1:2, :])
        p = p_ref[...].astype(F32)
        _, pullback = jax.vjp(_hgrn_block, p[:, 0:128], p[:, 128:256], p[:, 256:384], p[:, 384:512],
                              ck_ref[0], lb, nw_ref[...])
        dq, dfl, dv, dg, dst, dlb, dnw = pullback((dy_ref[...], dst_s[...]))
        dst_s[...] = dst
        dp_ref[:, 0:128] = dq.astype(dp_ref.dtype)
        dp_ref[:, 128:256] = dfl.astype(dp_ref.dtype)
        dp_ref[:, 256:384] = dv.astype(dp_ref.dtype)
        dp_ref[:, 384:512] = dg.astype(dp_ref.dtype)
        upd = jnp.concatenate([dlb, dnw, jnp.zeros((6, 128), F32)], axis=0)
        first = (b_id == 0) & (i == 0)

        @pl.when(first)
        def _():
            sm_ref[...] = upd

        @pl.when(jnp.logical_not(first))
        def _():
            sm_ref[...] += upd

    def rows(h, b, i):
        return b * nb + (nb - 1 - i)

    return pl.pallas_call(
        body, name="hgrn_bwd", grid=(AH, nbatch, nb),
        in_specs=[pl.BlockSpec((blk, 512), lambda h, b, i: (rows(h, b, i), COL_A // 512 + h)),
                  pl.BlockSpec((blk, 128), lambda h, b, i: (rows(h, b, i), h)),
                  pl.BlockSpec((1, 128, 128), lambda h, b, i: ((h * nbatch + b) * nb + (nb - 1 - i), 0, 0)),
                  pl.BlockSpec((2, 128), lambda h, b, i: (0, h)),
                  pl.BlockSpec((1, 128), lambda h, b, i: (0, h)),
                  pl.BlockSpec(memory_space=pl.ANY)],
        out_specs=[pl.BlockSpec((blk, 512), lambda h, b, i: (rows(h, b, i), COL_A // 512 + h)),
                   pl.BlockSpec((8, 128), lambda h, b, i: (0, h))],
        out_shape=[jax.ShapeDtypeStruct((t, NP), BF16), jax.ShapeDtypeStruct((8, AW), F32)],
        input_output_aliases={5: 0},
        scratch_shapes=[pltpu.VMEM((128, 128), F32)],
        compiler_params=_params(("parallel", "arbitrary", "arbitrary"), 48),
    )(proj, dya, ckpt, lb_logits, norm_w, dproj)


def _log_sigmoid(z):
    return jnp.minimum(z, 0.0) - jnp.log(1.0 + jnp.exp(-jnp.abs(z)))


def _fox_cum(proj, bias128, nbatch, seq):
    t = proj.shape[0]
    ts = min(512, seq)
    nb = seq // ts

    def body(p_ref, b_ref, c_ref, carry):
        @pl.when(pl.program_id(1) == 0)
        def _():
            carry[...] = jnp.zeros_like(carry)
        cum = _dot_f32(_tri(ts, False), _log_sigmoid(p_ref[...] + b_ref[...])) + carry[...]
        carry[...] = cum[ts - 1:ts, :]
        cum2 = cum * LOG2E
        lane = lax.broadcasted_iota(jnp.int32, (ts, 128), 1)
        for p in range(4):
            c_ref[p] = jnp.where(lane < 64, cum2[:, 2 * p:2 * p + 1], cum2[:, 2 * p + 1:2 * p + 2])

    return pl.pallas_call(
        body, name="fox_cum", grid=(nbatch, nb),
        in_specs=[pl.BlockSpec((ts, 128), lambda b, i: (b * nb + i, 0)),
                  pl.BlockSpec((1, 128), lambda b, i: (0, 0))],
        out_specs=pl.BlockSpec((4, ts, 128), lambda b, i: (0, b * nb + i, 0)),
        out_shape=jax.ShapeDtypeStruct((4, t, 128), F32),
        scratch_shapes=[pltpu.VMEM((1, 128), F32)],
        compiler_params=_params(("parallel", "arbitrary")),
    )(proj, bias128)


def _fox_scores_t(q_ref, kv_ref, cc_ref, hh, masked, tq, tk):
    kh = kv_ref[:, 64 * hh:64 * hh + 64].astype(BF16)
    qh = (q_ref[:, 64 * hh:64 * hh + 64] * (LOG2E * BDH ** -0.5)).astype(BF16)
    s = _dot_nt(kh, qh) - cc_ref[0, :, 64 * hh:64 * hh + 1]
    if masked:
        key = lax.broadcasted_iota(jnp.int32, (tk, tq), 0)
        qry = lax.broadcasted_iota(jnp.int32, (tk, tq), 1)
        s = jnp.where(key <= qry, s, NEG)
    return s, kh


def _causal_pairs(nq, key_major):
    if key_major:
        pairs = [(i, j) for j in range(nq) for i in range(j, nq)]
    else:
        pairs = [(i, j) for i in range(nq) for j in range(i + 1)]
    return (jnp.asarray([p[0] for p in pairs], jnp.int32), jnp.asarray([p[1] for p in pairs], jnp.int32))


def _with_ones_lane(x128, hh):
    lane = lax.broadcasted_iota(jnp.int32, x128.shape, 1)
    one = jnp.ones_like(x128)
    zero = jnp.zeros_like(x128)
    if hh == 0:
        return jnp.where(lane < 64, x128, jnp.where(lane == 64, one, zero))
    return jnp.where(lane >= 64, x128, jnp.where(lane == 0, one, zero))


def _fox_fwd(proj, cum_cols, nbatch, seq):
    t = proj.shape[0]
    tq = tk = min(512, seq)
    nq = seq // tq
    qi, kj = _causal_pairs(nq, key_major=False)

    def body(qi_ref, kj_ref, q_ref, kv_ref, cc_ref, o_ref, lse_ref, m_s, acc_s):
        s_id = pl.program_id(2)
        i, j = qi_ref[s_id], kj_ref[s_id]

        @pl.when(j == 0)
        def _():
            m_s[...] = jnp.full_like(m_s, NEG)
            acc_s[...] = jnp.zeros_like(acc_s)

        def step(masked):
            for hh in range(2):
                s, _ = _fox_scores_t(q_ref, kv_ref, cc_ref, hh, masked, tq, tk)
                m_prev = m_s[hh:hh + 1, :]
                m_new = jnp.maximum(m_prev, jnp.max(s, axis=0, keepdims=True))
                alpha = jnp.exp2(m_prev - m_new)
                p = jnp.exp2(s - m_new).astype(BF16)
                v_aug = _with_ones_lane(kv_ref[:, 128:256].astype(BF16), hh)
                acc_s[hh] = acc_s[hh] * alpha + _dot_tn(v_aug, p)
                m_s[hh:hh + 1, :] = m_new

        @pl.when(j < i)
        def _():
            step(False)

        @pl.when(j == i)
        def _():
            step(True)
            a0, a1 = acc_s[0], acc_s[1]
            l0, l1 = a0[64:65, :], a1[0:1, :]
            o_t = jnp.concatenate([a0[0:64, :] / l0, a1[64:128, :] / l1], axis=0)
            o_ref[...] = o_t.T.astype(o_ref.dtype)
            lse_ref[0, 0] = jnp.concatenate(
                [m_s[0:1, :] + jnp.log2(l0), m_s[1:2, :] + jnp.log2(l1), jnp.zeros((6, tq), F32)], axis=0)

    return pl.pallas_call(
        body, name="fox_fwd",
        grid_spec=pltpu.PrefetchScalarGridSpec(
            num_scalar_prefetch=2, grid=(nbatch, 4, qi.shape[0]),
            in_specs=[pl.BlockSpec((tq, 128), lambda b, p, s, qi, kj: (b * nq + qi[s], COL_BQ // 128 + p)),
                      pl.BlockSpec((tk, 256), lambda b, p, s, qi, kj: (b * nq + kj[s], COL_KV // 256 + p)),
                      pl.BlockSpec((1, tk, 128), lambda b, p, s, qi, kj: (p, b * nq + kj[s], 0))],
            out_specs=[pl.BlockSpec((tq, 128), lambda b, p, s, qi, kj: (b * nq + qi[s], p)),
                       pl.BlockSpec((1, 1, 8, tq), lambda b, p, s, qi, kj: (b, p, 0, qi[s]))],
            scratch_shapes=[pltpu.VMEM((8, tq), F32), pltpu.VMEM((2, 128, tq), F32)]),
        out_shape=[jax.ShapeDtypeStruct((t, 512), BF16), jax.ShapeDtypeStruct((nbatch, 4, 8, seq), F32)],
        compiler_params=_params(("parallel", "parallel", "arbitrary"), 48),
    )(qi, kj, proj, proj, cum_cols)


def _fox_bwd(proj, cum_cols, lse, yb, dyb, dproj, nbatch, seq):
    t = proj.shape[0]
    tq = tk = min(512, seq)
    nq = seq // tq
    scale = BDH ** -0.5
    qi, kj = _causal_pairs(nq, key_major=True)
    nsteps = qi.shape[0]

    def body(qi_ref, kj_ref, q_ref, kv_ref, cc_ref, lse_ref, o_ref, do_ref, dp_in,
             dkv_ref, dq_ref, drs_ref, dcs_ref, dk_s, dv_s, dqa_s):
        del dp_in
        hp, s_id = pl.program_id(1), pl.program_id(2)
        i, j = qi_ref[s_id], kj_ref[s_id]

        @pl.when(i == j)
        def _():
            dk_s[...] = jnp.zeros_like(dk_s)
            dv_s[...] = jnp.zeros_like(dv_s)

        @pl.when(s_id == 0)
        def _():
            dqa_s[...] = jnp.zeros_like(dqa_s)

        def step(masked):
            for hh in range(2):
                s, _ = _fox_scores_t(q_ref, kv_ref, cc_ref, hh, masked, tq, tk)
                p = jnp.exp2(s - lse_ref[0, 0, hh:hh + 1, :])
                doh = do_ref[:, 64 * hh:64 * hh + 64]
                dd = lax.dot_general(jnp.ones((8, 64), F32), doh * o_ref[:, 64 * hh:64 * hh + 64].astype(F32),
                                     (((1,), (1,)), ((), ())), preferred_element_type=F32, precision=HIGHEST)[0:1, :]
                doh = doh.astype(BF16)
                dp = _dot_nt(kv_ref[:, 128 + 64 * hh:192 + 64 * hh], doh)
                ds = (p * (dp - dd)).astype(BF16)
                dv_s[:, 64 * hh:64 * hh + 64] += _dot(p, doh)
                q_aug = _with_ones_lane((q_ref[...] * scale).astype(BF16), hh)
                dk_s[hh] += _dot(ds, q_aug)
                k_aug = _with_ones_lane(kv_ref[:, 0:128].astype(BF16), hh)
                dqa_s[i, hh] += _dot_tn(k_aug, ds)

        @pl.when(i == j)
        def _():
            step(True)

        @pl.when(i > j)
        def _():
            step(False)

        @pl.when(i == nq - 1)
        def _():
            lane = lax.broadcasted_iota(jnp.int32, (tk, 128), 1)
            k0, k1 = dk_s[0], dk_s[1]
            dkv_ref[:, 0:128] = jnp.where(lane < 64, k0, k1).astype(dkv_ref.dtype)
            dkv_ref[:, 128:256] = dv_s[...].astype(dkv_ref.dtype)
            dcs_ref[0] = jnp.where(lane == 2 * hp, k0[:, 64:65], jnp.where(lane == 2 * hp + 1, k1[:, 0:1], 0.0))

        @pl.when(s_id == nsteps - 1)
        def _():
            lane = lax.broadcasted_iota(jnp.int32, (tq, 128), 1)
            for blk in range(nq):
                a0 = dqa_s[blk, 0].T
                a1 = dqa_s[blk, 1].T
                rows = pl.ds(blk * tq, tq)
                dq_ref[rows, :] = (jnp.where(lane < 64, a0, a1) * scale).astype(dq_ref.dtype)
                drs_ref[0, rows, :] = jnp.where(lane == 2 * hp, a0[:, 64:65], jnp.where(lane == 2 * hp + 1, a1[:, 0:1], 0.0))

    return pl.pallas_call(
        body, name="fox_bwd",
        grid_spec=pltpu.PrefetchScalarGridSpec(
            num_scalar_prefetch=2, grid=(nbatch, 4, nsteps),
            in_specs=[pl.BlockSpec((tq, 128), lambda b, p, s, qi, kj: (b * nq + qi[s], COL_BQ // 128 + p)),
                      pl.BlockSpec((tk, 256), lambda b, p, s, qi, kj: (b * nq + kj[s], COL_KV // 256 + p)),
                      pl.BlockSpec((1, tk, 128), lambda b, p, s, qi, kj: (p, b * nq + kj[s], 0)),
                      pl.BlockSpec((1, 1, 8, tq), lambda b, p, s, qi, kj: (b, p, 0, qi[s])),
                      pl.BlockSpec((tq, 128), lambda b, p, s, qi, kj: (b * nq + qi[s], p)),
                      pl.BlockSpec((tq, 128), lambda b, p, s, qi, kj: (b * nq + qi[s], p)),
                      pl.BlockSpec(memory_space=pl.ANY)],
            out_specs=[pl.BlockSpec((tk, 256), lambda b, p, s, qi, kj: (b * nq + kj[s], COL_KV // 256 + p)),
                       pl.BlockSpec((seq, 128), lambda b, p, s, qi, kj: (b, p)),
                       pl.BlockSpec((1, seq, 128), lambda b, p, s, qi, kj: (p, b, 0)),
                       pl.BlockSpec((1, tk, 128), lambda b, p, s, qi, kj: (p, b * nq + kj[s], 0))],
            scratch_shapes=[pltpu.VMEM((2, tk, 128), F32), pltpu.VMEM((tk, 128), F32),
                            pltpu.VMEM((nq, 2, 128, tq), F32)]),
        out_shape=[jax.ShapeDtypeStruct((t, NP), BF16), jax.ShapeDtypeStruct((t, 512), BF16),
                   jax.ShapeDtypeStruct((4, t, 128), F32), jax.ShapeDtypeStruct((4, t, 128), F32)],
        input_output_aliases={8: 0},
        compiler_params=_params(("parallel", "parallel", "arbitrary"), 56),
    )(qi, kj, proj, proj, cum_cols, lse, yb, dyb, dproj)


def _place_cols(dproj, src, col):
    t, w = src.shape
    tm = 1024 if t % 1024 == 0 else t

    def body(s_ref, dp_in, o_ref):
        del dp_in
        o_ref[...] = s_ref[...]

    return pl.pallas_call(
        body, name="place_cols", grid=(t // tm,),
        in_specs=[pl.BlockSpec((tm, w), lambda i: (i, 0)), pl.BlockSpec(memory_space=pl.ANY)],
        out_specs=pl.BlockSpec((tm, w), lambda i: (i, col // w)),
        out_shape=jax.ShapeDtypeStruct(dproj.shape, dproj.dtype),
        input_output_aliases={1: 0},
        compiler_params=_params(("parallel",)),
    )(src, dproj)


def _fox_dbf(proj, bias128, drs, dcs, dproj, nbatch, seq):
    t = proj.shape[0]
    ts = min(512, seq)
    nb = seq // ts

    def body(p_ref, b_ref, dr_ref, dc_ref, dp_in, dp_ref, sm_ref, carry):
        del dp_in
        b_id, i = pl.program_id(0), pl.program_id(1)

        @pl.when(i == 0)
        def _():
            carry[...] = jnp.zeros_like(carry)

        dcum = (dr_ref[0] - dc_ref[0]) + (dr_ref[1] - dc_ref[1]) + (dr_ref[2] - dc_ref[2]) + (dr_ref[3] - dc_ref[3])
        rc = _dot_f32(_tri(ts, True), dcum) + carry[...]
        carry[...] = rc[0:1, :]
        z = p_ref[...] + b_ref[...]
        lane = lax.broadcasted_iota(jnp.int32, (ts, 128), 1)
        dz = jnp.where(lane < BH, rc * jax.nn.sigmoid(-z), 0.0)
        dp_ref[...] = dz.astype(dp_ref.dtype)
        upd = jnp.concatenate([jnp.sum(dz, axis=0, keepdims=True), jnp.zeros((7, 128), F32)], axis=0)
        first = (b_id == 0) & (i == 0)

        @pl.when(first)
        def _():
            sm_ref[...] = upd

        @pl.when(jnp.logical_not(first))
        def _():
            sm_ref[...] += upd

    def rows(b, i):
        return b * nb + (nb - 1 - i)

    return pl.pallas_call(
        body, name="fox_dbf", grid=(nbatch, nb),
        in_specs=[pl.BlockSpec((ts, 128), lambda b, i: (rows(b, i), 0)),
                  pl.BlockSpec((1, 128), lambda b, i: (0, 0)),
                  pl.BlockSpec((4, ts, 128), lambda b, i: (0, rows(b, i), 0)),
                  pl.BlockSpec((4, ts, 128), lambda b, i: (0, rows(b, i), 0)),
                  pl.BlockSpec(memory_space=pl.ANY)],
        out_specs=[pl.BlockSpec((ts, 128), lambda b, i: (rows(b, i), COL_BF // 128)),
                   pl.BlockSpec((8, 128), lambda b, i: (0, 0))],
        out_shape=[jax.ShapeDtypeStruct((t, NP), BF16), jax.ShapeDtypeStruct((8, 128), F32)],
        input_output_aliases={4: 0},
        scratch_shapes=[pltpu.VMEM((1, 128), F32)],
        compiler_params=_params(("arbitrary", "arbitrary")),
    )(proj, bias128, drs, dcs, dproj)


def _ln_stats(z):
    mu = jnp.mean(z, axis=-1, keepdims=True)
    zc = z - mu
    rstd = lax.rsqrt(jnp.mean(zc * zc, axis=-1, keepdims=True) + LN_EPS)
    return zc * rstd, rstd


def _ln_bwd(dy, xhat, rstd, w):
    dxh = dy * w
    return rstd * (dxh - jnp.mean(dxh, axis=-1, keepdims=True) - xhat * jnp.mean(dxh * xhat, axis=-1, keepdims=True))


def _merge_fwd(ya, yb, proj, x2, mod8, wba, wbb, wout, ln1w, ln1b, seq):
    t = x2.shape[0]
    tm = min(512, seq)
    tpb = seq // tm

    def body(ya_ref, yb_ref, g_ref, x_ref, mod_ref, wa_ref, wb_ref, wo_ref, lw_ref, lb_ref, mg_ref, u_ref, x1_ref):
        ga = jax.nn.sigmoid(g_ref[:, 0:D].astype(F32))
        gb = jax.nn.sigmoid(g_ref[:, D:2 * D].astype(F32))
        merged = (ga * jnp.dot(ya_ref[...], wa_ref[...], preferred_element_type=F32)
                  + gb * jnp.dot(yb_ref[...], wb_ref[...], preferred_element_type=F32))
        mg = merged.astype(BF16)
        mg_ref[...] = mg
        u = jnp.dot(mg, wo_ref[...], preferred_element_type=F32)
        u_ref[...] = u
        xhat, _ = _ln_stats(ALPHA * x_ref[...] + (1.0 + mod_ref[0, 2:3, :]) * u)
        x1_ref[...] = xhat * lw_ref[...] + lb_ref[...]

    tok = lambda w: pl.BlockSpec((tm, w), lambda i: (i, 0))
    full = lambda a: pl.BlockSpec(a.shape, lambda i: (0,) * a.ndim)
    return pl.pallas_call(
        body, name="merge_fwd", grid=(t // tm,),
        in_specs=[tok(512), tok(512), pl.BlockSpec((tm, 2048), lambda i: (i, COL_GATES // 2048)), tok(D),
                  pl.BlockSpec((1, 8, D), lambda i: (i // tpb, 0, 0)),
                  full(wba), full(wbb), full(wout), full(ln1w), full(ln1b)],
        out_specs=[tok(D), tok(D), tok(D)],
        out_shape=[jax.ShapeDtypeStruct((t, D), BF16), jax.ShapeDtypeStruct((t, D), F32),
                   jax.ShapeDtypeStruct((t, D), F32)],
        compiler_params=_params(("parallel",), 48),
    )(ya, yb, proj, x2, mod8, wba, wbb, wout, ln1w, ln1b)


def _merge_bwd(du, ya, yb, proj, wba, wbb, wout, token, seq):
    t = du.shape[0]
    tm = min(512, seq)

    def body(du_ref, ya_ref, yb_ref, g_ref, wa_ref, wb_ref, wo_ref, token_ref,
             dp_ref, dpa_ref, dpb_ref, dya_ref, dyb_ref):
        del token_ref
        ga = jax.nn.sigmoid(g_ref[:, 0:D].astype(F32))
        gb = jax.nn.sigmoid(g_ref[:, D:2 * D].astype(F32))
        dm = _dot_nt(du_ref[...], wo_ref[...])
        pa = jnp.dot(ya_ref[...], wa_ref[...], preferred_element_type=F32)
        pb = jnp.dot(yb_ref[...], wb_ref[...], preferred_element_type=F32)
        dpa = (dm * ga).astype(BF16)
        dpb = (dm * gb).astype(BF16)
        dpa_ref[...] = dpa
        dpb_ref[...] = dpb
        dp_ref[:, 0:D] = (dm * pa * ga * (1.0 - ga)).astype(BF16)
        dp_ref[:, D:2 * D] = (dm * pb * gb * (1.0 - gb)).astype(BF16)
        dya_ref[...] = _dot_nt(dpa, wa_ref[...])
        dyb_ref[...] = _dot_nt(dpb, wb_ref[...])

    tok = lambda w: pl.BlockSpec((tm, w), lambda i: (i, 0))
    full = lambda a: pl.BlockSpec(a.shape, lambda i: (0,) * a.ndim)
    return pl.pallas_call(
        body, name="merge_bwd", grid=(t // tm,),
        in_specs=[tok(D), tok(512), tok(512), pl.BlockSpec((tm, 2048), lambda i: (i, COL_GATES // 2048)),
                  full(wba), full(wbb), full(wout), full(token)],
        out_specs=[pl.BlockSpec((tm, 2048), lambda i: (i, COL_GATES // 2048)), tok(D), tok(D), tok(512), tok(512)],
        out_shape=[jax.ShapeDtypeStruct((t, NP), BF16), jax.ShapeDtypeStruct((t, D), BF16),
                   jax.ShapeDtypeStruct((t, D), BF16), jax.ShapeDtypeStruct((t, 512), F32),
                   jax.ShapeDtypeStruct((t, 512), F32)],
        compiler_params=_params(("parallel",), 48),
    )(du, ya, yb, proj, wba, wbb, wout, token)


def _ffn_fwd(x1, mod8, wg, wu, wd, target, ln2w, ln2b, seq):
    t = x1.shape[0]
    tm = min(FFN_TOKENS, seq)
    nf, _, tf = wg.shape
    tpb = seq // tm
    nbatch = t // seq

    def body(x_ref, mod_ref, wg_ref, wu_ref, wd_ref, t_ref, lw_ref, lb_ref,
             a_ref, b_ref, h_s, dz_ref, st_ref, dm_ref, acc):
        i, j = pl.program_id(0), pl.program_id(1)

        @pl.when(j == 0)
        def _():
            h_s[...] = (x_ref[...] * (1.0 + mod_ref[0, 4:5, :]) + mod_ref[0, 3:4, :]).astype(BF16)
            acc[...] = jnp.zeros_like(acc)

        a = jnp.dot(h_s[...], wg_ref[0], preferred_element_type=F32)
        b = jnp.dot(h_s[...], wu_ref[0], preferred_element_type=F32)
        a_ref[0] = a.astype(BF16)
        b_ref[0] = b.astype(BF16)
        acc[...] += _dot(a * jax.nn.sigmoid(a) * b, wd_ref[0])

        @pl.when(j == nf - 1)
        def _():
            ffn = acc[...]
            xhat, rstd = _ln_stats(ALPHA * x_ref[...] + (1.0 + mod_ref[0, 5:6, :]) * ffn)
            diff = xhat * lw_ref[...] + lb_ref[...] - t_ref[...]
            loss = 0.5 * jnp.sum(jnp.sum(diff * diff, axis=-1, keepdims=True), axis=0, keepdims=True) / D
            dy = diff * (1.0 / D)
            dz = _ln_bwd(dy, xhat, rstd, lw_ref[...])
            dz_ref[...] = dz
            lane = lax.broadcasted_iota(jnp.int32, (1, D), 1)
            upd = jnp.concatenate(
                [jnp.sum(dy * xhat, axis=0, keepdims=True), jnp.sum(dy, axis=0, keepdims=True),
                 jnp.where(lane == 0, loss, 0.0), jnp.zeros((5, D), F32)], axis=0)
            dmu = jnp.concatenate(
                [jnp.zeros((5, D), F32), jnp.sum(dz * ffn, axis=0, keepdims=True), jnp.zeros((2, D), F32)], axis=0)

            @pl.when(i == 0)
            def _():
                st_ref[...] = upd

            @pl.when(i > 0)
            def _():
                st_ref[...] += upd

            @pl.when(i % tpb == 0)
            def _():
                dm_ref[0] = dmu

            @pl.when(i % tpb != 0)
            def _():
                dm_ref[0] += dmu

    row = lambda: pl.BlockSpec((tm, D), lambda i, j: (i, 0))
    vec = lambda: pl.BlockSpec((1, D), lambda i, j: (0, 0))
    return pl.pallas_call(
        body, name="ffn_fwd", grid=(t // tm, nf),
        in_specs=[row(), pl.BlockSpec((1, 8, D), lambda i, j: (i // tpb, 0, 0)),
                  pl.BlockSpec((1, D, tf), lambda i, j: (j, 0, 0)), pl.BlockSpec((1, D, tf), lambda i, j: (j, 0, 0)),
                  pl.BlockSpec((1, tf, D), lambda i, j: (j, 0, 0)), row(), vec(), vec()],
        out_specs=[pl.BlockSpec((1, tm, tf), lambda i, j: (j, i, 0)), pl.BlockSpec((1, tm, tf), lambda i, j: (j, i, 0)),
                   row(), row(), pl.BlockSpec((8, D), lambda i, j: (0, 0)),
                   pl.BlockSpec((1, 8, D), lambda i, j: (i // tpb, 0, 0))],
        out_shape=[jax.ShapeDtypeStruct((nf, t, tf), BF16), jax.ShapeDtypeStruct((nf, t, tf), BF16),
                   jax.ShapeDtypeStruct((t, D), BF16),
                   jax.ShapeDtypeStruct((t, D), F32), jax.ShapeDtypeStruct((8, D), F32),
                   jax.ShapeDtypeStruct((nbatch, 8, D), F32)],
        scratch_shapes=[pltpu.VMEM((tm, D), F32)],
        compiler_params=_params(("arbitrary", "arbitrary"), 60),
    )(x1, mod8, wg, wu, wd, target, ln2w, ln2b)


def _ffn_bwd(dz2, a, b, wg, wu, wd, x1, x2, u, mod8, ln1w, seq):
    t = x1.shape[0]
    tm = min(512, seq)
    nf, tf, _ = wg.shape
    tpb = seq // tm
    nbatch = t // seq

    def body(dz_ref, a_ref, b_ref, wg_ref, wu_ref, wd_ref, x1_ref, x_ref, u_ref, mod_ref, lw_ref,
             da_ref, db_ref, hm_ref, df_ref, du_ref, dxp_ref, st_ref, dm_ref, acc):
        i, j = pl.program_id(0), pl.program_id(1)

        @pl.when(j == 0)
        def _():
            df_ref[...] = ((1.0 + mod_ref[0, 5:6, :]) * dz_ref[...]).astype(BF16)
            acc[...] = jnp.zeros_like(acc)

        dhm = _dot(df_ref[...], wd_ref[0])
        av = a_ref[0].astype(F32)
        bv = b_ref[0].astype(F32)
        sg = jax.nn.sigmoid(av)
        sl = av * sg
        hm_ref[0] = (sl * bv).astype(BF16)
        da = (dhm * bv * (sg * (1.0 + av * (1.0 - sg)))).astype(BF16)
        db = (dhm * sl).astype(BF16)
        da_ref[0] = da
        db_ref[0] = db
        acc[...] += _dot(da, wg_ref[0]) + _dot(db, wu_ref[0])

        @pl.when(j == nf - 1)
        def _():
            dh2 = acc[...]
            x1v = x1_ref[...]
            uv = u_ref[...]
            dx1 = ALPHA * dz_ref[...] + dh2 * (1.0 + mod_ref[0, 4:5, :])
            xhat, rstd = _ln_stats(ALPHA * x_ref[...] + (1.0 + mod_ref[0, 2:3, :]) * uv)
            dz1 = _ln_bwd(dx1, xhat, rstd, lw_ref[...])
            du_ref[...] = ((1.0 + mod_ref[0, 2:3, :]) * dz1).astype(BF16)
            dxp_ref[...] = ALPHA * dz1
            upd = jnp.concatenate(
                [jnp.sum(dx1 * xhat, axis=0, keepdims=True), jnp.sum(dx1, axis=0, keepdims=True),
                 jnp.zeros((6, D), F32)], axis=0)
            dmu = jnp.concatenate(
                [jnp.zeros((2, D), F32), jnp.sum(dz1 * uv, axis=0, keepdims=True),
                 jnp.sum(dh2, axis=0, keepdims=True), jnp.sum(dh2 * x1v, axis=0, keepdims=True),
                 jnp.zeros((3, D), F32)], axis=0)

            @pl.when(i == 0)
            def _():
                st_ref[...] = upd

            @pl.when(i > 0)
            def _():
                st_ref[...] += upd

            @pl.when(i % tpb == 0)
            def _():
                dm_ref[0] = dmu

            @pl.when(i % tpb != 0)
            def _():
                dm_ref[0] += dmu

    row = lambda: pl.BlockSpec((tm, D), lambda i, j: (i, 0))
    ffb = lambda: pl.BlockSpec((1, tm, tf), lambda i, j: (j, i, 0))
    return pl.pallas_call(
        body, name="ffn_bwd", grid=(t // tm, nf),
        in_specs=[row(), ffb(), ffb(),
                  pl.BlockSpec((1, tf, D), lambda i, j: (j, 0, 0)), pl.BlockSpec((1, tf, D), lambda i, j: (j, 0, 0)),
                  pl.BlockSpec((1, D, tf), lambda i, j: (j, 0, 0)), row(), row(), row(),
                  pl.BlockSpec((1, 8, D), lambda i, j: (i // tpb, 0, 0)), pl.BlockSpec((1, D), lambda i, j: (0, 0))],
        out_specs=[ffb(), ffb(), ffb(), row(), row(), row(), pl.BlockSpec((8, D), lambda i, j: (0, 0)),
                   pl.BlockSpec((1, 8, D), lambda i, j: (i // tpb, 0, 0))],
        out_shape=[jax.ShapeDtypeStruct((nf, t, tf), BF16), jax.ShapeDtypeStruct((nf, t, tf), BF16),
                   jax.ShapeDtypeStruct((nf, t, tf), BF16), jax.ShapeDtypeStruct((t, D), BF16),
                   jax.ShapeDtypeStruct((t, D), BF16), jax.ShapeDtypeStruct((t, D), F32),
                   jax.ShapeDtypeStruct((8, D), F32), jax.ShapeDtypeStruct((nbatch, 8, D), F32)],
        scratch_shapes=[pltpu.VMEM((tm, D), F32)],
        compiler_params=_params(("arbitrary", "arbitrary"), 48),
    )(dz2, a, b, wg, wu, wd, x1, x2, u, mod8, ln1w)


def _adamw_math(w, g, m, v):
    m = B1 * m + (1.0 - B1) * g
    v = B2 * v + (1.0 - B2) * (g * g)
    m_hat = m / (1.0 - B1 ** STEP)
    v_hat = v / (1.0 - B2 ** STEP)
    return -LR * (m_hat / (jnp.sqrt(v_hat) + EPS) + WD * w), m, v


def _adamw(w, g, m, v, name):
    rows, cols = w.shape
    tr = rows
    for cand in (128, 64, 32, 16, 8):
        if rows % cand == 0:
            tr = cand
            break

    def body(w_ref, g_ref, m_ref, v_ref, d_ref, mo_ref, vo_ref):
        d, mn, vn = _adamw_math(w_ref[...], g_ref[...], m_ref[...], v_ref[...])
        d_ref[...] = d
        mo_ref[...] = mn
        vo_ref[...] = vn

    spec = pl.BlockSpec((tr, cols), lambda i: (i, 0))
    return pl.pallas_call(
        body, name=name, grid=(rows // tr,), in_specs=[spec] * 4, out_specs=[spec] * 3,
        out_shape=[jax.ShapeDtypeStruct((rows, cols), F32)] * 3,
        compiler_params=_params(("parallel",), 48),
    )(w, g, m, v)


def _adamw_halves(w, g_mine, g_sib, m, v, c_idx, name):
    rows, cols = w.shape
    hr = rows // 2
    tr = next(cand for cand in (128, 88, 64, 32, 16, 8) if hr % cand == 0)
    tph = hr // tr

    def body(c_ref, w_ref, gm_ref, gs_ref, m_ref, v_ref, g_ref, d_ref, mo_ref, vo_ref):
        g = jnp.where(pl.program_id(0) == c_ref[0], gm_ref[...], gs_ref[...])
        d, mn, vn = _adamw_math(w_ref[...], g, m_ref[...], v_ref[...])
        g_ref[...] = g
        d_ref[...] = d
        mo_ref[...] = mn
        vo_ref[...] = vn

    full = pl.BlockSpec((tr, cols), lambda h, i, c: (h * tph + i, 0))
    half = pl.BlockSpec((tr, cols), lambda h, i, c: (i, 0))
    return pl.pallas_call(
        body, name=name,
        grid_spec=pltpu.PrefetchScalarGridSpec(
            num_scalar_prefetch=1, grid=(2, tph), in_specs=[full, half, half, full, full], out_specs=[full] * 4),
        out_shape=[jax.ShapeDtypeStruct((rows, cols), F32)] * 4,
        compiler_params=_params(("parallel", "parallel"), 48),
    )(c_idx, w, g_mine, g_sib, m, v)


def _grad_w_ada(c_all, dmod_cols):
    def body(c_ref, d_ref, o_ref):
        c = c_ref[...]
        o_ref[...] = lax.dot_general(c * jax.nn.sigmoid(c), d_ref[...], (((0,), (0,)), ((), ())),
                                     preferred_element_type=F32, precision=HIGHEST)

    return pl.pallas_call(
        body, name="grad_w_ada", out_shape=jax.ShapeDtypeStruct((D, dmod_cols.shape[1]), F32),
        compiler_params=_params(vmem_mb=48),
    )(c_all, dmod_cols)


def _small_update(gath, w8, m8, v8):
    def body(g_ref, w_ref, m_ref, v_ref, go_ref, d_ref, mo_ref, vo_ref):
        g0 = g_ref[0, 0:1, :] + g_ref[0, 1:2, :]
        g1 = g_ref[0, 2:3, :]
        for dev in range(1, N_DEV):
            g0 = g0 + (g_ref[dev, 0:1, :] + g_ref[dev, 1:2, :])
            g1 = g1 + g_ref[dev, 2:3, :]
        w = w_ref[...]
        lb = jax.nn.sigmoid(w[1:2, O_LB0:O_LB1] - w[1:2, O_LB1:O_FOX])
        fac = lb * (1.0 - lb)
        g1 = jnp.concatenate([g1[:, :O_LB0], g1[:, O_LB0:O_LB1] * fac, -g1[:, O_LB1:O_FOX] * fac, g1[:, O_FOX:]],
                             axis=1)
        g = jnp.concatenate([g0, g1, jnp.zeros((6, SMALL_W), F32)], axis=0)
        d, mn, vn = _adamw_math(w, g, m_ref[...], v_ref[...])
        go_ref[...] = g
        d_ref[...] = d
        mo_ref[...] = mn
        vo_ref[...] = vn

    return pl.pallas_call(
        body, name="small_update", out_shape=[jax.ShapeDtypeStruct((8, SMALL_W), F32)] * 4,
        compiler_params=_params(vmem_mb=48),
    )(gath, w8, m8, v8)


def _pack_small(b_ada, ln1w, ln1b, ln2w, ln2b, norm_w, lb_logits, fox):
    row1 = jnp.concatenate([ln1w, ln1b, ln2w, ln2b, norm_w, lb_logits[0:1], lb_logits[1:2], fox,
                            jnp.zeros((1, SMALL_W - O_FOX - BH), F32)], axis=1)
    return jnp.concatenate([b_ada, row1, jnp.zeros((6, SMALL_W), F32)], axis=0)


def _unpack_small(p):
    r = p[1:2]
    lb = jnp.concatenate([r[:, O_LB0:O_LB1], r[:, O_LB1:O_FOX]], axis=0)
    return dict(b_ada=p[0:1], ln1_w=r[:, O_LN1W:O_LN1B], ln1_b=r[:, O_LN1B:O_LN2W], ln2_w=r[:, O_LN2W:O_LN2B],
                ln2_b=r[:, O_LN2B:O_NORM], hgrn_norm_w=r[:, O_NORM:O_LB0], lb_logits=lb,
                fox_f_bias=r[:, O_FOX:O_FOX + BH])


_BIG = ("w_in", "w_branch_a", "w_branch_b", "w_out", "w_ffn_gate", "w_ffn_up", "w_ffn_down")
_TRANSPOSED = ("w_ffn_gate", "w_ffn_up")


def _cols_of_chips(stacked):
    return jnp.concatenate([stacked[k] for k in range(N_CHIPS)], axis=1)


def kernel(x, c, w_ada, b_ada, w_in, fox_f_bias, lb_logits, hgrn_norm_w, w_branch_a, w_branch_b, w_out, ln1_w, ln1_b, w_ffn_gate, w_ffn_up, w_ffn_down, ln2_w, ln2_b, loss_target, m_w_ada, m_b_ada, m_w_in, m_fox_f_bias, m_lb_logits, m_hgrn_norm_w, m_w_branch_a, m_w_branch_b, m_w_out, m_ln1_w, m_ln1_b, m_w_ffn_gate, m_w_ffn_up, m_w_ffn_down, m_ln2_w, m_ln2_b, v_w_ada, v_b_ada, v_w_in, v_fox_f_bias, v_lb_logits, v_hgrn_norm_w, v_w_branch_a, v_w_branch_b, v_w_out, v_ln1_w, v_ln1_b, v_w_ffn_gate, v_w_ffn_up, v_w_ffn_down, v_ln2_w, v_ln2_b):
    nbatch, seq, _ = x.shape
    t = nbatch * seq
    ax, ay, ac = lax.axis_index("x"), lax.axis_index("y"), lax.axis_index("c")
    chip = 2 * ax + ay
    dev = 2 * chip + ac
    chip_arr = jnp.reshape(chip, (1,)).astype(jnp.int32)
    core_arr = jnp.reshape(ac, (1,)).astype(jnp.int32)

    tr = lambda a: jnp.swapaxes(a[0], 0, 1)
    shard_w = dict(w_in=w_in[0], w_branch_a=w_branch_a[0], w_branch_b=w_branch_b[0], w_out=w_out[0],
                   w_ffn_gate=tr(w_ffn_gate), w_ffn_up=tr(w_ffn_up), w_ffn_down=w_ffn_down[0])
    shard_m = dict(w_in=m_w_in[0], w_branch_a=m_w_branch_a[0], w_branch_b=m_w_branch_b[0], w_out=m_w_out[0],
                   w_ffn_gate=tr(m_w_ffn_gate), w_ffn_up=tr(m_w_ffn_up), w_ffn_down=m_w_ffn_down[0])
    shard_v = dict(w_in=v_w_in[0], w_branch_a=v_w_branch_a[0], w_branch_b=v_w_branch_b[0], w_out=v_w_out[0],
                   w_ffn_gate=tr(v_w_ffn_gate), w_ffn_up=tr(v_w_ffn_up), w_ffn_down=v_w_ffn_down[0])

    shard16 = {n: shard_w[n].astype(BF16) for n in _BIG}

    def with_mine(gathered, n):
        return lax.dynamic_update_slice(gathered, shard16[n][None], (chip, 0, 0))

    w_p = _permute_cols(_cols_of_chips(with_mine(_gather_weights([shard16["w_in"]])[0], "w_in")))
    late = _BIG[1:]
    late_send, late_recv, late_src, late_land, late_token = _split_start(
        _gather_copies, [shard16[n] for n in late],
        [lax.empty((N_CHIPS,) + shard16[n].shape, BF16) for n in late], "gather_late_start")

    c8 = jnp.concatenate([c, jnp.zeros((8 - nbatch, D), F32)], axis=0)
    c_all = _allgather8(c8, "gather_c")[:, :nbatch, :].reshape(N_DEV * nbatch, D)
    ncol = w_ada.shape[2]
    b_cols = lax.dynamic_slice_in_dim(b_ada, chip * ncol, ncol, axis=1)
    mod_g = _allgather8(_mod_shard(c_all, w_ada[0], b_cols), "gather_mod")
    mod_all = jnp.concatenate([mod_g[2 * k] for k in range(N_CHIPS)], axis=1)
    mod_mine = lax.dynamic_slice_in_dim(mod_all, dev * nbatch, nbatch, axis=0)
    mod8 = jnp.concatenate([mod_mine.reshape(nbatch, 6, D), jnp.zeros((nbatch, 2, D), F32)], axis=1)
    mod8 = mod8 + late_token[0, 0]

    x2 = x.reshape(t, D)
    tgt2 = loss_target.reshape(t, D)
    bias128 = jnp.concatenate([fox_f_bias, jnp.zeros((1, 128 - BH), F32)], axis=1)

    proj, h16 = _proj(x2, mod8, w_p, seq, BF16, "proj")
    projf = _rows_matmul(h16, w_p[:, COL_BF:], "proj_forget")
    ya, ckpt = _hgrn_fwd(proj, lb_logits, hgrn_norm_w, nbatch, seq)
    cum_cols = _fox_cum(projf, bias128, nbatch, seq)
    yb, lse = _fox_fwd(proj, cum_cols, nbatch, seq)
    late_land = _pass_to_sibling(
        _split_wait(_gather_copies, late_send, late_recv, late_src, late_land, yb, "gather_late_wait"))
    full = {n: with_mine(g, n) for n, g in zip(late, late_land)}
    wba, wbb = _cols_of_chips(full["w_branch_a"]), _cols_of_chips(full["w_branch_b"])
    wout = full["w_out"].reshape(D, D)
    wg_t, wu_t, wd = full["w_ffn_gate"], full["w_ffn_up"], full["w_ffn_down"]
    wg, wu, wd_t = jnp.swapaxes(wg_t, 1, 2), jnp.swapaxes(wu_t, 1, 2), jnp.swapaxes(wd, 1, 2)
    merged, u, x1 = _merge_fwd(ya, yb, proj, x2, mod8, wba, wbb, wout, ln1_w, ln1_b, seq)
    a_pre, b_pre, h2, dz2, st2, dm2 = _ffn_fwd(x1, mod8, wg, wu, wd, tgt2, ln2_w, ln2_b, seq)
    loss = lax.psum(st2[2, 0], ("x", "y", "c"))

    da, db, hmid, dffn, du, dxp, st1, dm1 = _ffn_bwd(dz2, a_pre, b_pre, wg_t, wu_t, wd_t, x1, x2, u, mod8, ln1_w, seq)
    g_st = {}
    g_st["w_ffn_down"] = _tn_matmul(hmid, dffn, "dw_ffn_down", seq)
    g_st["w_ffn_gate"] = _tn_matmul(da, h2, "dw_ffn_gate", seq)
    g_st["w_ffn_up"] = _tn_matmul(db, h2, "dw_ffn_up", seq)
    g_st["w_out"] = _tn_matmul(merged, du, "dw_out", seq).reshape(N_CHIPS, D // N_CHIPS, D)

    def sum_over_cores(names, tag):
        g_list = [g_st[n] for n in names]
        return [_add_my_half(g, o, core_arr, "grad_add_halves_" + n)
                for n, g, o in zip(names, g_list, _swap_halves(g_list, "grad_swap_halves_" + tag))]

    early = ("w_ffn_down", "w_ffn_gate", "w_ffn_up", "w_out")
    e_halves = sum_over_cores(early, "early")
    e_send, e_recv, e_src, e_land, e_token = _split_start(
        _scatter_copies, [h16 for _, h16 in e_halves],
        [lax.empty((3,) + h16.shape[1:], BF16) for _, h16 in e_halves], "grad_scatter_early_start")
    dproj, dpa, dpb, dya, dyb = _merge_bwd(du, ya, yb, proj, wba, wbb, wout, e_token, seq)
    g_st["w_branch_a"] = _tn_matmul(ya, dpa, "dw_branch_a", seq, split=D // N_CHIPS)
    g_st["w_branch_b"] = _tn_matmul(yb, dpb, "dw_branch_b", seq, split=D // N_CHIPS)
    dproj, dq, drs, dcs = _fox_bwd(proj, cum_cols, lse, yb, dyb, dproj, nbatch, seq)
    dproj = _place_cols(dproj, dq, COL_BQ)
    dproj, sm_fox = _fox_dbf(projf, bias128, drs, dcs, dproj, nbatch, seq)
    dproj, sm_hgrn = _hgrn_bwd(proj, dya, ckpt, lb_logits, hgrn_norm_w, dproj, nbatch, seq)
    grad_x2, dm0 = _dh_kernel(dproj, w_p, x2, dxp, mod8, seq)
    dw_in = _unpermute_cols(_tn_matmul(h16, dproj, "dw_in", seq))
    ncin = NIN // N_CHIPS
    g_st["w_in"] = jnp.stack([dw_in[:, k * ncin:(k + 1) * ncin] for k in range(N_CHIPS)])

    e_recv = _split_wait(_scatter_copies, e_send, e_recv, e_src, e_land, dw_in, "grad_scatter_early_wait")
    rest = ("w_in", "w_branch_a", "w_branch_b")
    r_halves = sum_over_cores(rest, "rest")
    r_send, r_rcv, r_src, r_land, r_token = _split_start(
        _scatter_copies, [h16 for _, h16 in r_halves],
        [lax.empty((3,) + h16.shape[1:], BF16) for _, h16 in r_halves], "grad_scatter_rest_start")

    def finish(names, halves, recv, token, tag):
        g_mine = [_add_chips(h32, r, chip_arr, "grad_add_chips_" + n) for n, (h32, _), r in zip(names, halves, recv)]
        g_sib = _join_halves(g_mine, token, "grad_join_halves_" + tag)
        for n, gm, gs in zip(names, g_mine, g_sib):
            grads[n], deltas[n], new_m[n], new_v[n] = _adamw_halves(
                shard_w[n], gm, gs, shard_m[n], shard_v[n], core_arr, "adamw_" + n)

    grads, deltas, new_m, new_v = {}, {}, {}, {}
    finish(early, e_halves, e_recv, r_token, "early")

    dmod = (dm0 + dm1 + dm2)[:, :6, :].reshape(nbatch, 6 * D)
    row2 = jnp.concatenate([st1[0:1], st1[1:2], st2[0:1], st2[1:2], sm_hgrn[1:2], sm_hgrn[0:1], sm_hgrn[0:1],
                            sm_fox[0:1, :BH], jnp.zeros((1, SMALL_W - O_FOX - BH), F32)], axis=1)
    spack = jnp.concatenate([dmod, row2, jnp.zeros((8 - nbatch - 1, SMALL_W), F32)], axis=0)
    gath = _allgather8(spack, "gather_small")
    w8 = _pack_small(b_ada, ln1_w, ln1_b, ln2_w, ln2_b, hgrn_norm_w, lb_logits, fox_f_bias)
    m8 = _pack_small(m_b_ada, m_ln1_w, m_ln1_b, m_ln2_w, m_ln2_b, m_hgrn_norm_w, m_lb_logits, m_fox_f_bias)
    v8 = _pack_small(v_b_ada, v_ln1_w, v_ln1_b, v_ln2_w, v_ln2_b, v_hgrn_norm_w, v_lb_logits, v_fox_f_bias)
    sg, sd, smn, svn = (_unpack_small(p) for p in _small_update(gath, w8, m8, v8))
    dmod_all = gath[:, :nbatch, :].reshape(N_DEV * nbatch, SMALL_W)
    g_ada = _grad_w_ada(c_all, lax.dynamic_slice_in_dim(dmod_all, chip * ncol, ncol, axis=1))

    for group, small in zip((grads, deltas, new_m, new_v), (sg, sd, smn, svn)):
        group.update(small)
    grads["w_ada"] = g_ada
    deltas["w_ada"], new_m["w_ada"], new_v["w_ada"] = _adamw(w_ada[0], g_ada, m_w_ada[0], v_w_ada[0], "adamw_w_ada")
    r_recv = _split_wait(_scatter_copies, r_send, r_rcv, r_src, r_land, deltas["w_ada"], "grad_scatter_rest_wait")
    finish(rest, r_halves, r_recv, late_token, "rest")

    names = ["w_ada", "b_ada", "w_in", "fox_f_bias", "lb_logits", "hgrn_norm_w", "w_branch_a", "w_branch_b", "w_out",
             "ln1_w", "ln1_b", "w_ffn_gate", "w_ffn_up", "w_ffn_down", "ln2_w", "ln2_b"]
    shapes = dict(w_ada=w_ada.shape, b_ada=b_ada.shape, w_in=w_in.shape, fox_f_bias=fox_f_bias.shape,
                  lb_logits=lb_logits.shape, hgrn_norm_w=hgrn_norm_w.shape, w_branch_a=w_branch_a.shape,
                  w_branch_b=w_branch_b.shape, w_out=w_out.shape, ln1_w=ln1_w.shape, ln1_b=ln1_b.shape,
                  w_ffn_gate=w_ffn_gate.shape, w_ffn_up=w_ffn_up.shape, w_ffn_down=w_ffn_down.shape,
                  ln2_w=ln2_w.shape, ln2_b=ln2_b.shape)
    outs = [loss, grad_x2.reshape(x.shape)]
    for group in (grads, deltas, new_m, new_v):
        outs += [(jnp.swapaxes(group[n], 0, 1) if n in _TRANSPOSED else group[n]).reshape(shapes[n]) for n in names]
    return tuple(outs)
```

```python
import functools
import math

import jax
import jax.numpy as jnp
import numpy as np
from jax import lax
from jax.experimental import pallas as pl
from jax.experimental.pallas import tpu as pltpu

F32 = jnp.float32
BF16 = jnp.bfloat16
MESH = pl.DeviceIdType.MESH
HIGHEST = lax.Precision.HIGHEST

D = 1024
AW = 512
AH = 4
ADH = 128
BH = 8
BDH = 64
DFF = 2816
NIN = 5640
NP = 5760
N_CHIPS = 4
N_DEV = 8
HGRN_BLOCK = 256
FFN_TOKENS = 512
COL_GATES = 0
COL_BQ = 2048
COL_KV = 2560
COL_A = 3584
COL_BF = 5632
ALPHA = 2.0 ** 0.25
LN_EPS = 1e-5
RMS_EPS = 1e-6
NEG = -1e30
LOG2E = 1.4426950408889634
LR, B1, B2, EPS, WD, STEP = 0.001, 0.9, 0.999, 1e-08, 0.01, 10
SMALL_W = 6144
O_LN1W, O_LN1B, O_LN2W, O_LN2B, O_NORM, O_LB0, O_LB1, O_FOX = 0, 1024, 2048, 3072, 4096, 4608, 5120, 5632


def _params(sem=None, vmem_mb=None):
    kw = {}
    if sem is not None:
        kw["dimension_semantics"] = sem
    if vmem_mb is not None:
        kw["vmem_limit_bytes"] = vmem_mb << 20
    return pltpu.CompilerParams(**kw)


def _dot(a, b):
    return jnp.dot(a.astype(BF16), b.astype(BF16), preferred_element_type=F32)


def _dot_nt(a, b):
    return lax.dot_general(a.astype(BF16), b.astype(BF16), (((1,), (1,)), ((), ())), preferred_element_type=F32)


def _dot_tn(a, b):
    return lax.dot_general(a.astype(BF16), b.astype(BF16), (((0,), (0,)), ((), ())), preferred_element_type=F32)


def _dot_f32(a, b):
    return jnp.dot(a, b, preferred_element_type=F32, precision=HIGHEST)


def _perm_segments():
    segs = [(3592, 5640), (2048, 2560)]
    for p in range(4):
        segs += [(2560 + 128 * p, 2688 + 128 * p), (3072 + 128 * p, 3200 + 128 * p)]
    for h in range(4):
        segs += [(128 * h + 512 * t, 128 * h + 512 * t + 128) for t in range(4)]
    segs += [(3584, 3592)]
    return segs


def _permute_cols(w):
    parts = [w[:, a:b] for a, b in _perm_segments()]
    parts.append(jnp.zeros((w.shape[0], NP - NIN), w.dtype))
    return jnp.concatenate(parts, axis=1)


def _unpermute_cols(g):
    pos, where = 0, {}
    for a, b in _perm_segments():
        where[a] = (pos, pos + b - a)
        pos += b - a
    parts = [g[:, where[a][0]:where[a][1]] for a in sorted(where)]
    return jnp.concatenate(parts, axis=1)


def _allgather8(v, name):
    rows, cols = v.shape

    def body(x_ref, out_ref, send_sems, recv_sems, local_sem):
        x, y, c = lax.axis_index("x"), lax.axis_index("y"), lax.axis_index("c")
        me, sibling = (x, y, c), (x, y, 1 - c)
        chips = [(1 - x, y), (x, 1 - y), (1 - x, 1 - y)]

        def slot(px, py, pc):
            return out_ref.at[4 * px + 2 * py + pc]

        def copy(k, block, to, src=None):
            return pltpu.make_async_remote_copy(
                src_ref=slot(*block) if src is None else src, dst_ref=slot(*block),
                send_sem=send_sems.at[k], recv_sem=recv_sems.at[k], device_id=to, device_id_type=MESH)

        mine = pltpu.make_async_copy(x_ref, slot(*me), local_sem)
        mine.start()
        first = [copy(0, me, sibling, src=x_ref)]
        first += [copy(1 + j, me, (*chip, c), src=x_ref) for j, chip in enumerate(chips)]
        for cp in first:
            cp.start()
        passed = [copy(4 + j, (*chip, c), sibling) for j, chip in enumerate(chips)]
        for j, chip in enumerate(chips):
            copy(1 + j, (*chip, c), me).wait_recv()
            passed[j].start()
        copy(0, sibling, me).wait_recv()
        for j, chip in enumerate(chips):
            copy(4 + j, (*chip, 1 - c), me).wait_recv()
        for cp in first + passed:
            cp.wait_send()
        mine.wait()

    return pl.pallas_call(
        body, name=name,
        out_shape=jax.ShapeDtypeStruct((N_DEV, rows, cols), v.dtype),
        in_specs=[pl.BlockSpec(memory_space=pltpu.VMEM)],
        out_specs=pl.BlockSpec(memory_space=pltpu.VMEM),
        scratch_shapes=[pltpu.SemaphoreType.DMA((7,)), pltpu.SemaphoreType.DMA((7,)), pltpu.SemaphoreType.DMA],
    )(v)


def _hbm_specs(n):
    return [pl.BlockSpec(memory_space=pl.ANY)] * n


def _gather_weights(shards):
    n = len(shards)

    def body(*refs):
        ins, outs, (send_sems, recv_sems) = refs[:n], refs[n:2 * n], refs[2 * n:]
        x, y, c = lax.axis_index("x"), lax.axis_index("y"), lax.axis_index("c")
        sibling = (x, y, 1 - c)
        chips = [(1 - x, y), (x, 1 - y), (1 - x, 1 - y)]

        def blk(w, px, py, half):
            hr = ins[w].shape[0] // 2
            return outs[w].at[2 * px + py, pl.ds(half * hr, hr), :]

        def copy(w, k, block, to, src=None):
            return pltpu.make_async_remote_copy(
                src_ref=blk(w, *block) if src is None else src, dst_ref=blk(w, *block),
                send_sem=send_sems.at[6 * w + k], recv_sem=recv_sems.at[6 * w + k], device_id=to, device_id_type=MESH)

        first = []
        for w in range(n):
            hr = ins[w].shape[0] // 2
            my_half = ins[w].at[pl.ds(c * hr, hr), :]
            first += [copy(w, j, (x, y, c), (*chip, c), src=my_half) for j, chip in enumerate(chips)]
        for cp in first:
            cp.start()
        passed = []
        for j, chip in enumerate(chips):
            for w in range(n):
                copy(w, j, (*chip, c), (x, y, c)).wait_recv()
                passed.append(copy(w, 3 + j, (*chip, c), sibling))
                passed[-1].start()
        for j, chip in enumerate(chips):
            for w in range(n):
                copy(w, 3 + j, (*chip, 1 - c), (x, y, c)).wait_recv()
        for cp in first + passed:
            cp.wait_send()

    return pl.pallas_call(
        body, name="gather_weights",
        out_shape=[jax.ShapeDtypeStruct((N_CHIPS,) + s.shape, s.dtype) for s in shards],
        in_specs=_hbm_specs(n), out_specs=_hbm_specs(n),
        scratch_shapes=[pltpu.SemaphoreType.DMA((6 * n,)), pltpu.SemaphoreType.DMA((6 * n,))],
    )(*shards)


def _swap_halves(grads, name):
    n = len(grads)

    def body(*refs):
        ins, outs, (send_sems, recv_sems) = refs[:n], refs[n:2 * n], refs[2 * n:]
        x, y, c = lax.axis_index("x"), lax.axis_index("y"), lax.axis_index("c")
        cps = []
        for w in range(n):
            hr = ins[w].shape[1] // 2
            cps.append(pltpu.make_async_remote_copy(
                src_ref=ins[w].at[:, pl.ds((1 - c) * hr, hr), :], dst_ref=outs[w],
                send_sem=send_sems.at[w], recv_sem=recv_sems.at[w], device_id=(x, y, 1 - c), device_id_type=MESH))
        for cp in cps:
            cp.start()
        for cp in cps:
            cp.wait()

    return pl.pallas_call(
        body, name=name,
        out_shape=[jax.ShapeDtypeStruct((N_CHIPS, g.shape[1] // 2, g.shape[2]), g.dtype) for g in grads],
        in_specs=_hbm_specs(n), out_specs=_hbm_specs(n),
        scratch_shapes=[pltpu.SemaphoreType.DMA((n,)), pltpu.SemaphoreType.DMA((n,))],
    )(*grads)


def _scatter_chips(reds, name):
    n = len(reds)

    def body(*refs):
        ins, outs, (send_sems, recv_sems) = refs[:n], refs[n:2 * n], refs[2 * n:]
        x, y, c = lax.axis_index("x"), lax.axis_index("y"), lax.axis_index("c")
        chips = [(1 - x, y), (x, 1 - y), (1 - x, 1 - y)]
        cps = [pltpu.make_async_remote_copy(
            src_ref=ins[w].at[2 * chip[0] + chip[1]], dst_ref=outs[w].at[j],
            send_sem=send_sems.at[3 * w + j], recv_sem=recv_sems.at[3 * w + j],
            device_id=(*chip, c), device_id_type=MESH)
            for j, chip in enumerate(chips) for w in range(n)]
        for cp in cps:
            cp.start()
        for cp in cps:
            cp.wait()

    return pl.pallas_call(
        body, name=name,
        out_shape=[jax.ShapeDtypeStruct((3,) + r.shape[1:], r.dtype) for r in reds],
        in_specs=_hbm_specs(n), out_specs=_hbm_specs(n),
        scratch_shapes=[pltpu.SemaphoreType.DMA((3 * n,)), pltpu.SemaphoreType.DMA((3 * n,))],
    )(*reds)


def _join_halves(halves, token, name):
    n = len(halves)

    def body(*refs):
        ins, outs, (send_sems, recv_sems) = refs[:n], refs[n + 1:2 * n + 1], refs[2 * n + 1:]
        x, y, c = lax.axis_index("x"), lax.axis_index("y"), lax.axis_index("c")
        cps = [pltpu.make_async_remote_copy(
            src_ref=ins[w], dst_ref=outs[w], send_sem=send_sems.at[w], recv_sem=recv_sems.at[w],
            device_id=(x, y, 1 - c), device_id_type=MESH) for w in range(n)]
        for cp in cps:
            cp.start()
        for cp in cps:
            cp.wait()

    return pl.pallas_call(
        body, name=name,
        out_shape=[jax.ShapeDtypeStruct(h.shape, h.dtype) for h in halves],
        in_specs=_hbm_specs(n + 1), out_specs=_hbm_specs(n),
        scratch_shapes=[pltpu.SemaphoreType.DMA((n,)), pltpu.SemaphoreType.DMA((n,))],
    )(*halves, token)


def _in_hbm(v):
    return pltpu.with_memory_space_constraint(v, pltpu.HBM)


_SPLIT_COPY = pltpu.CompilerParams(has_side_effects=pltpu.SideEffectType.DATAFLOW_SIDE_EFFECTING)


def _gather_copies(srcs, lands, send_sems, recv_sems):
    x, y, c = lax.axis_index("x"), lax.axis_index("y"), lax.axis_index("c")
    cps = []
    for w, (src, land) in enumerate(zip(srcs, lands)):
        hr = src.shape[0] // 2
        for j, chip in enumerate([(1 - x, y), (x, 1 - y), (1 - x, 1 - y)]):
            cps.append(pltpu.make_async_remote_copy(
                src_ref=src.at[pl.ds(c * hr, hr), :], dst_ref=land.at[2 * x + y, pl.ds(c * hr, hr), :],
                send_sem=send_sems.at[3 * w + j], recv_sem=recv_sems.at[3 * w + j],
                device_id=(*chip, c), device_id_type=MESH))
    return cps


def _scatter_copies(srcs, lands, send_sems, recv_sems):
    x, y, c = lax.axis_index("x"), lax.axis_index("y"), lax.axis_index("c")
    cps = []
    for w, (src, land) in enumerate(zip(srcs, lands)):
        for j, chip in enumerate([(1 - x, y), (x, 1 - y), (1 - x, 1 - y)]):
            cps.append(pltpu.make_async_remote_copy(
                src_ref=src.at[2 * chip[0] + chip[1]], dst_ref=land.at[j],
                send_sem=send_sems.at[3 * w + j], recv_sem=recv_sems.at[3 * w + j],
                device_id=(*chip, c), device_id_type=MESH))
    return cps


def _split_start(copies, srcs, lands, name):
    n = len(srcs)

    def body(*refs):
        src, lnd, send_sems, recv_sems, token = refs[:n], refs[n:2 * n], refs[2 * n], refs[2 * n + 1], refs[-1]
        for cp in copies(src, lnd, send_sems, recv_sems):
            cp.start()
        token[...] = jnp.zeros_like(token)

    hbm = pl.BlockSpec(memory_space=pltpu.HBM)
    sem = pl.BlockSpec(memory_space=pltpu.SEMAPHORE)
    outs = pl.pallas_call(
        body, name=name,
        out_shape=(pltpu.SemaphoreType.DMA((3 * n,)), pltpu.SemaphoreType.DMA((3 * n,)),
                   *[pltpu.HBM(v.shape, v.dtype) for v in srcs + lands], jax.ShapeDtypeStruct((8, 128), F32)),
        in_specs=[hbm] * (2 * n),
        out_specs=(sem, sem, *([hbm] * (2 * n)), pl.BlockSpec(memory_space=pltpu.VMEM)),
        input_output_aliases={i: 2 + i for i in range(2 * n)},
        compiler_params=_SPLIT_COPY,
    )(*[_in_hbm(v) for v in srcs + lands])
    return outs[0], outs[1], list(outs[2:2 + n]), list(outs[2 + n:2 + 2 * n]), outs[-1]


def _split_wait(copies, send_sems, recv_sems, srcs, lands, after, name):
    n = len(srcs)

    def body(*refs):
        src, lnd, send_sems, recv_sems = refs[:n], refs[n:2 * n], refs[2 * n], refs[2 * n + 1]
        for cp in copies(src, lnd, send_sems, recv_sems):
            cp.wait_send()
            cp.wait_recv()

    hbm = pl.BlockSpec(memory_space=pltpu.HBM)
    sem = pl.BlockSpec(memory_space=pltpu.SEMAPHORE)
    outs = pl.pallas_call(
        body, name=name,
        out_shape=tuple(pltpu.HBM(v.shape, v.dtype) for v in srcs + lands),
        in_specs=[hbm] * (2 * n) + [sem, sem, pl.BlockSpec(memory_space=pl.ANY)],
        out_specs=tuple([hbm] * (2 * n)),
        input_output_aliases={i: i for i in range(2 * n)},
        compiler_params=_SPLIT_COPY,
    )(*srcs, *lands, send_sems, recv_sems, after)
    return list(outs[n:])


def _pass_to_sibling(lands):
    n = len(lands)

    def body(*refs):
        ins, outs, (send_sems, recv_sems) = refs[:n], refs[n:2 * n], refs[2 * n:]
        x, y, c = lax.axis_index("x"), lax.axis_index("y"), lax.axis_index("c")
        cps = []
        for w in range(n):
            hr = ins[w].shape[1] // 2
            for j, chip in enumerate([(1 - x, y), (x, 1 - y), (1 - x, 1 - y)]):
                k = 2 * chip[0] + chip[1]
                cps.append(pltpu.make_async_remote_copy(
                    src_ref=ins[w].at[k, pl.ds(c * hr, hr), :], dst_ref=outs[w].at[k, pl.ds(c * hr, hr), :],
                    send_sem=send_sems.at[3 * w + j], recv_sem=recv_sems.at[3 * w + j],
                    device_id=(x, y, 1 - c), device_id_type=MESH))
        for cp in cps:
            cp.start()
        for cp in cps:
            cp.wait()

    return pl.pallas_call(
        body, name="gather_late_pass",
        out_shape=[jax.ShapeDtypeStruct(v.shape, v.dtype) for v in lands],
        in_specs=_hbm_specs(n), out_specs=_hbm_specs(n),
        input_output_aliases={i: i for i in range(n)},
        scratch_shapes=[pltpu.SemaphoreType.DMA((3 * n,)), pltpu.SemaphoreType.DMA((3 * n,))],
    )(*lands)


def _row_tile(rows):
    for cand in (256, 176, 128, 64, 32, 16):
        if rows % cand == 0:
            return cand
    raise ValueError(rows)


def _add_my_half(g, other, c_idx, name):
    _, k, n = g.shape
    hr = k // 2
    tr = _row_tile(hr)
    nb = hr // tr

    def body(c_ref, g_ref, o_ref, out_ref, out16_ref):
        s = g_ref[...] + o_ref[...]
        out_ref[...] = s
        out16_ref[...] = s.astype(BF16)

    return pl.pallas_call(
        body, name=name,
        grid_spec=pltpu.PrefetchScalarGridSpec(
            num_scalar_prefetch=1, grid=(N_CHIPS, nb),
            in_specs=[pl.BlockSpec((1, tr, n), lambda j, i, c: (j, c[0] * nb + i, 0)),
                      pl.BlockSpec((1, tr, n), lambda j, i, c: (j, i, 0))],
            out_specs=[pl.BlockSpec((1, tr, n), lambda j, i, c: (j, i, 0)),
                       pl.BlockSpec((1, tr, n), lambda j, i, c: (j, i, 0))]),
        out_shape=[jax.ShapeDtypeStruct((N_CHIPS, hr, n), F32), jax.ShapeDtypeStruct((N_CHIPS, hr, n), BF16)],
        compiler_params=_params(("parallel", "parallel")),
    )(c_idx, g, other)


def _add_chips(red, recv, chip_idx, name):
    _, hr, n = red.shape
    tr = _row_tile(hr)

    def body(k_ref, r_ref, v_ref, out_ref):
        out_ref[...] = ((r_ref[0] + v_ref[0].astype(F32)) + v_ref[1].astype(F32)) + v_ref[2].astype(F32)

    return pl.pallas_call(
        body, name=name,
        grid_spec=pltpu.PrefetchScalarGridSpec(
            num_scalar_prefetch=1, grid=(hr // tr,),
            in_specs=[pl.BlockSpec((1, tr, n), lambda i, k: (k[0], i, 0)),
                      pl.BlockSpec((3, tr, n), lambda i, k: (0, i, 0))],
            out_specs=pl.BlockSpec((tr, n), lambda i, k: (i, 0))),
        out_shape=jax.ShapeDtypeStruct((hr, n), F32),
        compiler_params=_params(("parallel",)),
    )(chip_idx, red, recv)


def _mod_shard(c_all, w_ada, b_ada):
    nb, cols = c_all.shape[0], w_ada.shape[1]

    def body(c_ref, w_ref, b_ref, o_ref):
        c = c_ref[...]
        o_ref[...] = _dot(c * jax.nn.sigmoid(c), w_ref[...]) + b_ref[...]

    return pl.pallas_call(
        body, name="mod_shard", out_shape=jax.ShapeDtypeStruct((nb, cols), F32),
        compiler_params=_params(vmem_mb=48),
    )(c_all, w_ada, b_ada)


def _proj(x2, mod8, w, seq, out_dtype, name):
    t = x2.shape[0]
    n = w.shape[1]
    tm, tn = min(2048, seq), min(1152, n)
    tpb = seq // tm

    def body(x_ref, mod_ref, w_ref, o_ref, h_ref):
        @pl.when(pl.program_id(1) == 0)
        def _():
            h_ref[...] = (x_ref[...] * (1.0 + mod_ref[0, 1:2, :]) + mod_ref[0, 0:1, :]).astype(BF16)
        o_ref[...] = jnp.dot(h_ref[...], w_ref[...], preferred_element_type=F32).astype(o_ref.dtype)

    return pl.pallas_call(
        body, name=name, grid=(t // tm, n // tn),
        in_specs=[pl.BlockSpec((tm, D), lambda i, j: (i, 0)),
                  pl.BlockSpec((1, 8, D), lambda i, j: (i // tpb, 0, 0)),
                  pl.BlockSpec((D, tn), lambda i, j: (0, j))],
        out_specs=[pl.BlockSpec((tm, tn), lambda i, j: (i, j)), pl.BlockSpec((tm, D), lambda i, j: (i, 0))],
        out_shape=[jax.ShapeDtypeStruct((t, n), out_dtype), jax.ShapeDtypeStruct((t, D), BF16)],
        compiler_params=_params(("parallel", "arbitrary"), 56),
    )(x2, mod8, w)


def _rows_matmul(a, w, name):
    t, k = a.shape
    n = w.shape[1]
    tm = 1024 if t % 1024 == 0 else t

    def body(a_ref, w_ref, o_ref):
        o_ref[...] = jnp.dot(a_ref[...], w_ref[...], preferred_element_type=F32)

    return pl.pallas_call(
        body, name=name, grid=(t // tm,),
        in_specs=[pl.BlockSpec((tm, k), lambda i: (i, 0)), pl.BlockSpec((k, n), lambda i: (0, 0))],
        out_specs=pl.BlockSpec((tm, n), lambda i: (i, 0)),
        out_shape=jax.ShapeDtypeStruct((t, n), F32),
        compiler_params=_params(("parallel",)),
    )(a, w)


def _tn_matmul(a, b, name, seq, split=None):
    a_st, b_st = a.ndim == 3, b.ndim == 3
    t, ka = a.shape[-2:]
    n = b.shape[-1]
    tt = min(1024, seq)
    nt = t // tt
    if a_st or b_st:
        steps, tn = (a.shape[0] if a_st else b.shape[0]), n
    else:
        tn = split
        if tn is None:
            tn = next(cand for cand in (1152, 1024, 1408, 512, n) if n % cand == 0)
        steps = n // tn
    stacked_out = a_st or b_st or split is not None

    def body(a_ref, b_ref, o_ref):
        part = _dot_tn(a_ref[0] if a_st else a_ref[...], b_ref[0] if b_st else b_ref[...])
        if stacked_out:
            part = part[None]

        @pl.when(pl.program_id(1) == 0)
        def _():
            o_ref[...] = part

        @pl.when(pl.program_id(1) > 0)
        def _():
            o_ref[...] += part

    if a_st:
        in_specs = [pl.BlockSpec((1, tt, ka), lambda j, k: (j, k, 0))]
    else:
        in_specs = [pl.BlockSpec((tt, ka), lambda j, k: (k, 0))]
    if b_st:
        in_specs.append(pl.BlockSpec((1, tt, n), lambda j, k: (j, k, 0)))
    else:
        in_specs.append(pl.BlockSpec((tt, tn), lambda j, k: (k, 0 if a_st else j)))
    if stacked_out:
        out_spec = pl.BlockSpec((1, ka, tn), lambda j, k: (j, 0, 0))
        out_shape = jax.ShapeDtypeStruct((steps, ka, tn), F32)
    else:
        out_spec = pl.BlockSpec((ka, tn), lambda j, k: (0, j))
        out_shape = jax.ShapeDtypeStruct((ka, n), F32)
    return pl.pallas_call(
        body, name=name, grid=(steps, nt), in_specs=in_specs, out_specs=out_spec, out_shape=out_shape,
        compiler_params=_params(("parallel", "arbitrary"), 56),
    )(a, b)


def _dh_kernel(dproj, w_p, x2, dxp, mod8, seq):
    t = x2.shape[0]
    tm, tk = min(1024, seq), 1152
    tpb = seq // tm
    nk = NP // tk
    nbatch = t // seq

    def body(dp_ref, w_ref, x_ref, dxp_ref, mod_ref, gx_ref, dm_ref, acc):
        i, k = pl.program_id(0), pl.program_id(1)

        @pl.when(k == 0)
        def _():
            acc[...] = jnp.zeros_like(acc)

        acc[...] += _dot_nt(dp_ref[...], w_ref[...])

        @pl.when(k == nk - 1)
        def _():
            dh = acc[...]
            gx_ref[...] = dxp_ref[...] + dh * (1.0 + mod_ref[0, 1:2, :])
            upd = jnp.concatenate(
                [jnp.sum(dh, axis=0, keepdims=True), jnp.sum(dh * x_ref[...], axis=0, keepdims=True),
                 jnp.zeros((6, D), F32)], axis=0)

            @pl.when(i % tpb == 0)
            def _():
                dm_ref[0] = upd

            @pl.when(i % tpb != 0)
            def _():
                dm_ref[0] += upd

    return pl.pallas_call(
        body, name="dh", grid=(t // tm, nk),
        in_specs=[pl.BlockSpec((tm, tk), lambda i, k: (i, k)),
                  pl.BlockSpec((D, tk), lambda i, k: (0, k)),
                  pl.BlockSpec((tm, D), lambda i, k: (i, 0)),
                  pl.BlockSpec((tm, D), lambda i, k: (i, 0)),
                  pl.BlockSpec((1, 8, D), lambda i, k: (i // tpb, 0, 0))],
        out_specs=[pl.BlockSpec((tm, D), lambda i, k: (i, 0)),
                   pl.BlockSpec((1, 8, D), lambda i, k: (i // tpb, 0, 0))],
        out_shape=[jax.ShapeDtypeStruct((t, D), F32), jax.ShapeDtypeStruct((nbatch, 8, D), F32)],
        scratch_shapes=[pltpu.VMEM((tm, D), F32)],
        compiler_params=_params(("arbitrary", "arbitrary"), 48),
    )(dproj, w_p, x2, dxp, mod8)


def _tri(n, upper):
    r = lax.broadcasted_iota(jnp.int32, (n, n), 0)
    c = lax.broadcasted_iota(jnp.int32, (n, n), 1)
    return jnp.where((c >= r) if upper else (c <= r), 1.0, 0.0).astype(F32)


@jax.custom_vjp
def _mm_nn(a, b):
    return _dot(a, b)


_mm_nn.defvjp(lambda a, b: (_dot(a, b), (a, b)),
              lambda res, g: (_dot_nt(g, res[1]), _dot_tn(res[0], g)))


@jax.custom_vjp
def _mm_nt(a, b):
    return _dot_nt(a, b)


_mm_nt.defvjp(lambda a, b: (_dot_nt(a, b), (a, b)),
              lambda res, g: (_dot(g, res[1]), _dot_tn(g, res[0])))


@jax.custom_vjp
def _mm_tn(a, b):
    return _dot_tn(a, b)


_mm_tn.defvjp(lambda a, b: (_dot_tn(a, b), (a, b)),
              lambda res, g: (_dot_nt(res[1], g), _dot(res[0], g)))


@jax.custom_vjp
def _cumsum_rows(x):
    return _dot_f32(_tri(x.shape[0], False), x)


_cumsum_rows.defvjp(lambda x: (_cumsum_rows(x), None),
                    lambda _, g: (_dot_f32(_tri(g.shape[0], True), g),))


@functools.partial(jax.custom_vjp, nondiff_argnums=(1,))
def _shift_rows(x, k):
    return pltpu.roll(x, k % x.shape[0], 0)


_shift_rows.defvjp(lambda x, k: (_shift_rows(x, k), None),
                   lambda k, _, g: (pltpu.roll(g, (-k) % g.shape[0], 0),))


def _group_ref(bc, m):
    n = bc.shape[0] // (2 * m)
    b3 = bc.reshape(n, 2 * m, ADH)
    row = lax.broadcasted_iota(jnp.int32, b3.shape, 1)
    ref = jnp.sum(jnp.where(row == m - 1, b3, 0.0), axis=1, keepdims=True)
    return jnp.broadcast_to(ref, b3.shape).reshape(bc.shape)


def _hgrn_block(q, fl, v, g, st, lb, nw):
    n = q.shape[0]
    f = lb + (1.0 - lb) * jax.nn.sigmoid(fl)
    kk = 1.0 - f
    lf = jnp.log(f)
    bc = _cumsum_rows(lf)
    row = lax.broadcasted_iota(jnp.int32, (n, ADH), 0)
    same = jnp.bitwise_xor(lax.broadcasted_iota(jnp.int32, (n, n), 0), lax.broadcasted_iota(jnp.int32, (n, n), 1))
    a = jnp.zeros((n, n), F32)
    m = 1
    while m < n:
        r = jnp.bitwise_and(row, 2 * m - 1)
        up, lo = r >= m, r < m
        if m == 1:
            aq, ak = lf, jnp.zeros_like(lf)
        elif m == 2:
            aq = jnp.where(r == 3, lf + _shift_rows(lf, 1), lf)
            ak = jnp.where(r == 0, _shift_rows(lf, -1), 0.0)
        else:
            ref = _group_ref(bc, m)
            aq, ak = bc - ref, ref - bc
        qt = jnp.where(up, q * jnp.exp(jnp.where(up, aq, 0.0)), 0.0)
        kt = jnp.where(lo, kk * jnp.exp(jnp.where(lo, ak, 0.0)), 0.0)
        a = a + jnp.where(same < 2 * m, _mm_nt(qt, kt), 0.0)
        m *= 2
    last = row == n - 1
    bl = jnp.sum(jnp.where(last, bc, 0.0), axis=0, keepdims=True)
    o = _mm_nn(a, v) + _mm_nt(q * jnp.exp(bc), st) + jnp.sum(q * kk, axis=-1, keepdims=True) * v
    st_new = st * jnp.exp(bl) + _mm_tn(v, kk * jnp.exp(bl - bc))
    rms = lax.rsqrt(jnp.mean(o * o, axis=-1, keepdims=True) + RMS_EPS)
    return o * rms * nw * jax.nn.sigmoid(g), st_new


def _hgrn_fwd(proj, lb_logits, norm_w, nbatch, seq):
    t = proj.shape[0]
    blk = min(HGRN_BLOCK, seq)
    nb = seq // blk

    def body(p_ref, lbl_ref, nw_ref, y_ref, ck_ref, st_s):
        @pl.when(pl.program_id(2) == 0)
        def _():
            st_s[...] = jnp.zeros_like(st_s)

        st = st_s[...]
        ck_ref[0] = st
        lb = jax.nn.sigmoid(lbl_ref[0:1, :] - lbl_ref[1:2, :])
        p = p_ref[...].astype(F32)
        y, st_new = _hgrn_block(p[:, 0:128], p[:, 128:256], p[:, 256:384], p[:, 384:512], st, lb, nw_ref[...])
        st_s[...] = st_new
        y_ref[...] = y.astype(y_ref.dtype)

    return pl.pallas_call(
        body, name="hgrn_fwd", grid=(AH, nbatch, nb),
        in_specs=[pl.BlockSpec((blk, 512), lambda h, b, i: (b * nb + i, COL_A // 512 + h)),
                  pl.BlockSpec((2, 128), lambda h, b, i: (0, h)),
                  pl.BlockSpec((1, 128), lambda h, b, i: (0, h))],
        out_specs=[pl.BlockSpec((blk, 128), lambda h, b, i: (b * nb + i, h)),
                   pl.BlockSpec((1, 128, 128), lambda h, b, i: ((h * nbatch + b) * nb + i, 0, 0))],
        out_shape=[jax.ShapeDtypeStruct((t, AW), BF16), jax.ShapeDtypeStruct((AH * nbatch * nb, 128, 128), F32)],
        scratch_shapes=[pltpu.VMEM((128, 128), F32)],
        compiler_params=_params(("parallel", "parallel", "arbitrary"), 48),
    )(proj, lb_logits, norm_w)


def _hgrn_bwd(proj, dya, ckpt, lb_logits, norm_w, dproj, nbatch, seq):
    t = proj.shape[0]
    blk = min(HGRN_BLOCK, seq)
    nb = seq // blk

    def body(p_ref, dy_ref, ck_ref, lbl_ref, nw_ref, dp_in, dp_ref, sm_ref, dst_s):
        del dp_in
        b_id, i = pl.program_id(1), pl.program_id(2)

        @pl.when(i == 0)
        def _():
            dst_s[...] = jnp.zeros_like(dst_s)

        lb = jax.nn.sigmoid(lbl_ref[0:1, :] - lbl_ref[1:2, :])
        p = p_ref[...].astype(F32)
        _, pullback = jax.vjp(_hgrn_block, p[:, 0:128], p[:, 128:256], p[:, 256:384], p[:, 384:512],
                              ck_ref[0], lb, nw_ref[...])
        dq, dfl, dv, dg, dst, dlb, dnw = pullback((dy_ref[...], dst_s[...]))
        dst_s[...] = dst
        dp_ref[:, 0:128] = dq.astype(dp_ref.dtype)
        dp_ref[:, 128:256] = dfl.astype(dp_ref.dtype)
        dp_ref[:, 256:384] = dv.astype(dp_ref.dtype)
        dp_ref[:, 384:512] = dg.astype(dp_ref.dtype)
        upd = jnp.concatenate([dlb, dnw, jnp.zeros((6, 128), F32)], axis=0)
        first = (b_id == 0) & (i == 0)

        @pl.when(first)
        def _():
            sm_ref[...] = upd

        @pl.when(jnp.logical_not(first))
        def _():
            sm_ref[...] += upd

    def rows(h, b, i):
        return b * nb + (nb - 1 - i)

    return pl.pallas_call(
        body, name="hgrn_bwd", grid=(AH, nbatch, nb),
        in_specs=[pl.BlockSpec((blk, 512), lambda h, b, i: (rows(h, b, i), COL_A // 512 + h)),
                  pl.BlockSpec((blk, 128), lambda h, b, i: (rows(h, b, i), h)),
                  pl.BlockSpec((1, 128, 128), lambda h, b, i: ((h * nbatch + b) * nb + (nb - 1 - i), 0, 0)),
                  pl.BlockSpec((2, 128), lambda h, b, i: (0, h)),
                  pl.BlockSpec((1, 128), lambda h, b, i: (0, h)),
                  pl.BlockSpec(memory_space=pl.ANY)],
        out_specs=[pl.BlockSpec((blk, 512), lambda h, b, i: (rows(h, b, i), COL_A // 512 + h)),
                   pl.BlockSpec((8, 128), lambda h, b, i: (0, h))],
        out_shape=[jax.ShapeDtypeStruct((t, NP), BF16), jax.ShapeDtypeStruct((8, AW), F32)],
        input_output_aliases={5: 0},
        scratch_shapes=[pltpu.VMEM((128, 128), F32)],
        compiler_params=_params(("parallel", "arbitrary", "arbitrary"), 48),
    )(proj, dya, ckpt, lb_logits, norm_w, dproj)


def _log_sigmoid(z):
    return jnp.minimum(z, 0.0) - jnp.log(1.0 + jnp.exp(-jnp.abs(z)))


def _fox_cum(proj, bias128, nbatch, seq):
    t = proj.shape[0]
    ts = min(512, seq)
    nb = seq // ts

    def body(p_ref, b_ref, c_ref, carry):
        @pl.when(pl.program_id(1) == 0)
        def _():
            carry[...] = jnp.zeros_like(carry)
        cum = _dot_f32(_tri(ts, False), _log_sigmoid(p_ref[...] + b_ref[...])) + carry[...]
        carry[...] = cum[ts - 1:ts, :]
        cum2 = cum * LOG2E
        lane = lax.broadcasted_iota(jnp.int32, (ts, 128), 1)
        for p in range(4):
            c_ref[p] = jnp.where(lane < 64, cum2[:, 2 * p:2 * p + 1], cum2[:, 2 * p + 1:2 * p + 2])

    return pl.pallas_call(
        body, name="fox_cum", grid=(nbatch, nb),
        in_specs=[pl.BlockSpec((ts, 128), lambda b, i: (b * nb + i, 0)),
                  pl.BlockSpec((1, 128), lambda b, i: (0, 0))],
        out_specs=pl.BlockSpec((4, ts, 128), lambda b, i: (0, b * nb + i, 0)),
        out_shape=jax.ShapeDtypeStruct((4, t, 128), F32),
        scratch_shapes=[pltpu.VMEM((1, 128), F32)],
        compiler_params=_params(("parallel", "arbitrary")),
    )(proj, bias128)


def _fox_scores_t(q_ref, kv_ref, cc_ref, hh, masked, tq, tk):
    kh = kv_ref[:, 64 * hh:64 * hh + 64].astype(BF16)
    qh = (q_ref[:, 64 * hh:64 * hh + 64] * (LOG2E * BDH ** -0.5)).astype(BF16)
    s = _dot_nt(kh, qh) - cc_ref[0, :, 64 * hh:64 * hh + 1]
    if masked:
        key = lax.broadcasted_iota(jnp.int32, (tk, tq), 0)
        qry = lax.broadcasted_iota(jnp.int32, (tk, tq), 1)
        s = jnp.where(key <= qry, s, NEG)
    return s, kh


def _causal_pairs(nq, key_major):
    if key_major:
        pairs = [(i, j) for j in range(nq) for i in range(j, nq)]
    else:
        pairs = [(i, j) for i in range(nq) for j in range(i + 1)]
    return (jnp.asarray([p[0] for p in pairs], jnp.int32), jnp.asarray([p[1] for p in pairs], jnp.int32))


def _with_ones_lane(x128, hh):
    lane = lax.broadcasted_iota(jnp.int32, x128.shape, 1)
    one = jnp.ones_like(x128)
    zero = jnp.zeros_like(x128)
    if hh == 0:
        return jnp.where(lane < 64, x128, jnp.where(lane == 64, one, zero))
    return jnp.where(lane >= 64, x128, jnp.where(lane == 0, one, zero))


def _fox_fwd(proj, cum_cols, nbatch, seq):
    t = proj.shape[0]
    tq = tk = min(512, seq)
    nq = seq // tq
    qi, kj = _causal_pairs(nq, key_major=False)

    def body(qi_ref, kj_ref, q_ref, kv_ref, cc_ref, o_ref, lse_ref, m_s, acc_s):
        s_id = pl.program_id(2)
        i, j = qi_ref[s_id], kj_ref[s_id]

        @pl.when(j == 0)
        def _():
            m_s[...] = jnp.full_like(m_s, NEG)
            acc_s[...] = jnp.zeros_like(acc_s)

        def step(masked):
            for hh in range(2):
                s, _ = _fox_scores_t(q_ref, kv_ref, cc_ref, hh, masked, tq, tk)
                m_prev = m_s[hh:hh + 1, :]
                m_new = jnp.maximum(m_prev, jnp.max(s, axis=0, keepdims=True))
                alpha = jnp.exp2(m_prev - m_new)
                p = jnp.exp2(s - m_new).astype(BF16)
                v_aug = _with_ones_lane(kv_ref[:, 128:256].astype(BF16), hh)
                acc_s[hh] = acc_s[hh] * alpha + _dot_tn(v_aug, p)
                m_s[hh:hh + 1, :] = m_new

        @pl.when(j < i)
        def _():
            step(False)

        @pl.when(j == i)
        def _():
            step(True)
            a0, a1 = acc_s[0], acc_s[1]
            l0, l1 = a0[64:65, :], a1[0:1, :]
            o_t = jnp.concatenate([a0[0:64, :] / l0, a1[64:128, :] / l1], axis=0)
            o_ref[...] = o_t.T.astype(o_ref.dtype)
            lse_ref[0, 0] = jnp.concatenate(
                [m_s[0:1, :] + jnp.log2(l0), m_s[1:2, :] + jnp.log2(l1), jnp.zeros((6, tq), F32)], axis=0)

    return pl.pallas_call(
        body, name="fox_fwd",
        grid_spec=pltpu.PrefetchScalarGridSpec(
            num_scalar_prefetch=2, grid=(nbatch, 4, qi.shape[0]),
            in_specs=[pl.BlockSpec((tq, 128), lambda b, p, s, qi, kj: (b * nq + qi[s], COL_BQ // 128 + p)),
                      pl.BlockSpec((tk, 256), lambda b, p, s, qi, kj: (b * nq + kj[s], COL_KV // 256 + p)),
                      pl.BlockSpec((1, tk, 128), lambda b, p, s, qi, kj: (p, b * nq + kj[s], 0))],
            out_specs=[pl.BlockSpec((tq, 128), lambda b, p, s, qi, kj: (b * nq + qi[s], p)),
                       pl.BlockSpec((1, 1, 8, tq), lambda b, p, s, qi, kj: (b, p, 0, qi[s]))],
            scratch_shapes=[pltpu.VMEM((8, tq), F32), pltpu.VMEM((2, 128, tq), F32)]),
        out_shape=[jax.ShapeDtypeStruct((t, 512), BF16), jax.ShapeDtypeStruct((nbatch, 4, 8, seq), F32)],
        compiler_params=_params(("parallel", "parallel", "arbitrary"), 48),
    )(qi, kj, proj, proj, cum_cols)


def _fox_bwd(proj, cum_cols, lse, yb, dyb, dproj, nbatch, seq):
    t = proj.shape[0]
    tq = tk = min(512, seq)
    nq = seq // tq
    scale = BDH ** -0.5
    qi, kj = _causal_pairs(nq, key_major=True)
    nsteps = qi.shape[0]

    def body(qi_ref, kj_ref, q_ref, kv_ref, cc_ref, lse_ref, o_ref, do_ref, dp_in,
             dkv_ref, dq_ref, drs_ref, dcs_ref, dk_s, dv_s, dqa_s):
        del dp_in
        hp, s_id = pl.program_id(1), pl.program_id(2)
        i, j = qi_ref[s_id], kj_ref[s_id]

        @pl.when(i == j)
        def _():
            dk_s[...] = jnp.zeros_like(dk_s)
            dv_s[...] = jnp.zeros_like(dv_s)

        @pl.when(s_id == 0)
        def _():
            dqa_s[...] = jnp.zeros_like(dqa_s)

        def step(masked):
            for hh in range(2):
                s, _ = _fox_scores_t(q_ref, kv_ref, cc_ref, hh, masked, tq, tk)
                p = jnp.exp2(s - lse_ref[0, 0, hh:hh + 1, :])
                doh = do_ref[:, 64 * hh:64 * hh + 64]
                dd = lax.dot_general(jnp.ones((8, 64), F32), doh * o_ref[:, 64 * hh:64 * hh + 64].astype(F32),
                                     (((1,), (1,)), ((), ())), preferred_element_type=F32, precision=HIGHEST)[0:1, :]
                doh = doh.astype(BF16)
                dp = _dot_nt(kv_ref[:, 128 + 64 * hh:192 + 64 * hh], doh)
                ds = (p * (dp - dd)).astype(BF16)
                dv_s[:, 64 * hh:64 * hh + 64] += _dot(p, doh)
                q_aug = _with_ones_lane((q_ref[...] * scale).astype(BF16), hh)
                dk_s[hh] += _dot(ds, q_aug)
                k_aug = _with_ones_lane(kv_ref[:, 0:128].astype(BF16), hh)
                dqa_s[i, hh] += _dot_tn(k_aug, ds)

        @pl.when(i == j)
        def _():
            step(True)

        @pl.when(i > j)
        def _():
            step(False)

        @pl.when(i == nq - 1)
        def _():
            lane = lax.broadcasted_iota(jnp.int32, (tk, 128), 1)
            k0, k1 = dk_s[0], dk_s[1]
            dkv_ref[:, 0:128] = jnp.where(lane < 64, k0, k1).astype(dkv_ref.dtype)
            dkv_ref[:, 128:256] = dv_s[...].astype(dkv_ref.dtype)
            dcs_ref[0] = jnp.where(lane == 2 * hp, k0[:, 64:65], jnp.where(lane == 2 * hp + 1, k1[:, 0:1], 0.0))

        @pl.when(s_id == nsteps - 1)
        def _():
            lane = lax.broadcasted_iota(jnp.int32, (tq, 128), 1)
            for blk in range(nq):
                a0 = dqa_s[blk, 0].T
                a1 = dqa_s[blk, 1].T
                rows = pl.ds(blk * tq, tq)
                dq_ref[rows, :] = (jnp.where(lane < 64, a0, a1) * scale).astype(dq_ref.dtype)
                drs_ref[0, rows, :] = jnp.where(lane == 2 * hp, a0[:, 64:65], jnp.where(lane == 2 * hp + 1, a1[:, 0:1], 0.0))

    return pl.pallas_call(
        body, name="fox_bwd",
        grid_spec=pltpu.PrefetchScalarGridSpec(
            num_scalar_prefetch=2, grid=(nbatch, 4, nsteps),
            in_specs=[pl.BlockSpec((tq, 128), lambda b, p, s, qi, kj: (b * nq + qi[s], COL_BQ // 128 + p)),
                      pl.BlockSpec((tk, 256), lambda b, p, s, qi, kj: (b * nq + kj[s], COL_KV // 256 + p)),
                      pl.BlockSpec((1, tk, 128), lambda b, p, s, qi, kj: (p, b * nq + kj[s], 0)),
                      pl.BlockSpec((1, 1, 8, tq), lambda b, p, s, qi, kj: (b, p, 0, qi[s])),
                      pl.BlockSpec((tq, 128), lambda b, p, s, qi, kj: (b * nq + qi[s], p)),
                      pl.BlockSpec((tq, 128), lambda b, p, s, qi, kj: (b * nq + qi[s], p)),
                      pl.BlockSpec(memory_space=pl.ANY)],
            out_specs=[pl.BlockSpec((tk, 256), lambda b, p, s, qi, kj: (b * nq + kj[s], COL_KV // 256 + p)),
                       pl.BlockSpec((seq, 128), lambda b, p, s, qi, kj: (b, p)),
                       pl.BlockSpec((1, seq, 128), lambda b, p, s, qi, kj: (p, b, 0)),
                       pl.BlockSpec((1, tk, 128), lambda b, p, s, qi, kj: (p, b * nq + kj[s], 0))],
            scratch_shapes=[pltpu.VMEM((2, tk, 128), F32), pltpu.VMEM((tk, 128), F32),
                            pltpu.VMEM((nq, 2, 128, tq), F32)]),
        out_shape=[jax.ShapeDtypeStruct((t, NP), BF16), jax.ShapeDtypeStruct((t, 512), BF16),
                   jax.ShapeDtypeStruct((4, t, 128), F32), jax.ShapeDtypeStruct((4, t, 128), F32)],
        input_output_aliases={8: 0},
        compiler_params=_params(("parallel", "parallel", "arbitrary"), 56),
    )(qi, kj, proj, proj, cum_cols, lse, yb, dyb, dproj)


def _place_cols(dproj, src, col):
    t, w = src.shape
    tm = 1024 if t % 1024 == 0 else t

    def body(s_ref, dp_in, o_ref):
        del dp_in
        o_ref[...] = s_ref[...]

    return pl.pallas_call(
        body, name="place_cols", grid=(t // tm,),
        in_specs=[pl.BlockSpec((tm, w), lambda i: (i, 0)), pl.BlockSpec(memory_space=pl.ANY)],
        out_specs=pl.BlockSpec((tm, w), lambda i: (i, col // w)),
        out_shape=jax.ShapeDtypeStruct(dproj.shape, dproj.dtype),
        input_output_aliases={1: 0},
        compiler_params=_params(("parallel",)),
    )(src, dproj)


def _fox_dbf(proj, bias128, drs, dcs, dproj, nbatch, seq):
    t = proj.shape[0]
    ts = min(512, seq)
    nb = seq // ts

    def body(p_ref, b_ref, dr_ref, dc_ref, dp_in, dp_ref, sm_ref, carry):
        del dp_in
        b_id, i = pl.program_id(0), pl.program_id(1)

        @pl.when(i == 0)
        def _():
            carry[...] = jnp.zeros_like(carry)

        dcum = (dr_ref[0] - dc_ref[0]) + (dr_ref[1] - dc_ref[1]) + (dr_ref[2] - dc_ref[2]) + (dr_ref[3] - dc_ref[3])
        rc = _dot_f32(_tri(ts, True), dcum) + carry[...]
        carry[...] = rc[0:1, :]
        z = p_ref[...] + b_ref[...]
        lane = lax.broadcasted_iota(jnp.int32, (ts, 128), 1)
        dz = jnp.where(lane < BH, rc * jax.nn.sigmoid(-z), 0.0)
        dp_ref[...] = dz.astype(dp_ref.dtype)
        upd = jnp.concatenate([jnp.sum(dz, axis=0, keepdims=True), jnp.zeros((7, 128), F32)], axis=0)
        first = (b_id == 0) & (i == 0)

        @pl.when(first)
        def _():
            sm_ref[...] = upd

        @pl.when(jnp.logical_not(first))
        def _():
            sm_ref[...] += upd

    def rows(b, i):
        return b * nb + (nb - 1 - i)

    return pl.pallas_call(
        body, name="fox_dbf", grid=(nbatch, nb),
        in_specs=[pl.BlockSpec((ts, 128), lambda b, i: (rows(b, i), 0)),
                  pl.BlockSpec((1, 128), lambda b, i: (0, 0)),
                  pl.BlockSpec((4, ts, 128), lambda b, i: (0, rows(b, i), 0)),
                  pl.BlockSpec((4, ts, 128), lambda b, i: (0, rows(b, i), 0)),
                  pl.BlockSpec(memory_space=pl.ANY)],
        out_specs=[pl.BlockSpec((ts, 128), lambda b, i: (rows(b, i), COL_BF // 128)),
                   pl.BlockSpec((8, 128), lambda b, i: (0, 0))],
        out_shape=[jax.ShapeDtypeStruct((t, NP), BF16), jax.ShapeDtypeStruct((8, 128), F32)],
        input_output_aliases={4: 0},
        scratch_shapes=[pltpu.VMEM((1, 128), F32)],
        compiler_params=_params(("arbitrary", "arbitrary")),
    )(proj, bias128, drs, dcs, dproj)


def _ln_stats(z):
    mu = jnp.mean(z, axis=-1, keepdims=True)
    zc = z - mu
    rstd = lax.rsqrt(jnp.mean(zc * zc, axis=-1, keepdims=True) + LN_EPS)
    return zc * rstd, rstd


def _ln_bwd(dy, xhat, rstd, w):
    dxh = dy * w
    return rstd * (dxh - jnp.mean(dxh, axis=-1, keepdims=True) - xhat * jnp.mean(dxh * xhat, axis=-1, keepdims=True))


def _merge_fwd(ya, yb, proj, x2, mod8, wba, wbb, wout, ln1w, ln1b, seq):
    t = x2.shape[0]
    tm = min(512, seq)
    tpb = seq // tm

    def body(ya_ref, yb_ref, g_ref, x_ref, mod_ref, wa_ref, wb_ref, wo_ref, lw_ref, lb_ref, mg_ref, u_ref, x1_ref):
        ga = jax.nn.sigmoid(g_ref[:, 0:D].astype(F32))
        gb = jax.nn.sigmoid(g_ref[:, D:2 * D].astype(F32))
        merged = (ga * jnp.dot(ya_ref[...], wa_ref[...], preferred_element_type=F32)
                  + gb * jnp.dot(yb_ref[...], wb_ref[...], preferred_element_type=F32))
        mg = merged.astype(BF16)
        mg_ref[...] = mg
        u = jnp.dot(mg, wo_ref[...], preferred_element_type=F32)
        u_ref[...] = u
        xhat, _ = _ln_stats(ALPHA * x_ref[...] + (1.0 + mod_ref[0, 2:3, :]) * u)
        x1_ref[...] = xhat * lw_ref[...] + lb_ref[...]

    tok = lambda w: pl.BlockSpec((tm, w), lambda i: (i, 0))
    full = lambda a: pl.BlockSpec(a.shape, lambda i: (0,) * a.ndim)
    return pl.pallas_call(
        body, name="merge_fwd", grid=(t // tm,),
        in_specs=[tok(512), tok(512), pl.BlockSpec((tm, 2048), lambda i: (i, COL_GATES // 2048)), tok(D),
                  pl.BlockSpec((1, 8, D), lambda i: (i // tpb, 0, 0)),
                  full(wba), full(wbb), full(wout), full(ln1w), full(ln1b)],
        out_specs=[tok(D), tok(D), tok(D)],
        out_shape=[jax.ShapeDtypeStruct((t, D), BF16), jax.ShapeDtypeStruct((t, D), F32),
                   jax.ShapeDtypeStruct((t, D), F32)],
        compiler_params=_params(("parallel",), 48),
    )(ya, yb, proj, x2, mod8, wba, wbb, wout, ln1w, ln1b)


def _merge_bwd(du, ya, yb, proj, wba, wbb, wout, token, seq):
    t = du.shape[0]
    tm = min(512, seq)

    def body(du_ref, ya_ref, yb_ref, g_ref, wa_ref, wb_ref, wo_ref, token_ref,
             dp_ref, dpa_ref, dpb_ref, dya_ref, dyb_ref):
        del token_ref
        ga = jax.nn.sigmoid(g_ref[:, 0:D].astype(F32))
        gb = jax.nn.sigmoid(g_ref[:, D:2 * D].astype(F32))
        dm = _dot_nt(du_ref[...], wo_ref[...])
        pa = jnp.dot(ya_ref[...], wa_ref[...], preferred_element_type=F32)
        pb = jnp.dot(yb_ref[...], wb_ref[...], preferred_element_type=F32)
        dpa = (dm * ga).astype(BF16)
        dpb = (dm * gb).astype(BF16)
        dpa_ref[...] = dpa
        dpb_ref[...] = dpb
        dp_ref[:, 0:D] = (dm * pa * ga * (1.0 - ga)).astype(BF16)
        dp_ref[:, D:2 * D] = (dm * pb * gb * (1.0 - gb)).astype(BF16)
        dya_ref[...] = _dot_nt(dpa, wa_ref[...])
        dyb_ref[...] = _dot_nt(dpb, wb_ref[...])

    tok = lambda w: pl.BlockSpec((tm, w), lambda i: (i, 0))
    full = lambda a: pl.BlockSpec(a.shape, lambda i: (0,) * a.ndim)
    return pl.pallas_call(
        body, name="merge_bwd", grid=(t // tm,),
        in_specs=[tok(D), tok(512), tok(512), pl.BlockSpec((tm, 2048), lambda i: (i, COL_GATES // 2048)),
                  full(wba), full(wbb), full(wout), full(token)],
        out_specs=[pl.BlockSpec((tm, 2048), lambda i: (i, COL_GATES // 2048)), tok(D), tok(D), tok(512), tok(512)],
        out_shape=[jax.ShapeDtypeStruct((t, NP), BF16), jax.ShapeDtypeStruct((t, D), BF16),
                   jax.ShapeDtypeStruct((t, D), BF16), jax.ShapeDtypeStruct((t, 512), F32),
                   jax.ShapeDtypeStruct((t, 512), F32)],
        compiler_params=_params(("parallel",), 48),
    )(du, ya, yb, proj, wba, wbb, wout, token)


def _ffn_fwd(x1, mod8, wg, wu, wd, target, ln2w, ln2b, seq):
    t = x1.shape[0]
    tm = min(FFN_TOKENS, seq)
    nf, _, tf = wg.shape
    tpb = seq // tm
    nbatch = t // seq

    def body(x_ref, mod_ref, wg_ref, wu_ref, wd_ref, t_ref, lw_ref, lb_ref,
             a_ref, b_ref, h_s, dz_ref, st_ref, dm_ref, acc):
        i, j = pl.program_id(0), pl.program_id(1)

        @pl.when(j == 0)
        def _():
            h_s[...] = (x_ref[...] * (1.0 + mod_ref[0, 4:5, :]) + mod_ref[0, 3:4, :]).astype(BF16)
            acc[...] = jnp.zeros_like(acc)

        a = jnp.dot(h_s[...], wg_ref[0], preferred_element_type=F32)
        b = jnp.dot(h_s[...], wu_ref[0], preferred_element_type=F32)
        a_ref[0] = a.astype(BF16)
        b_ref[0] = b.astype(BF16)
        acc[...] += _dot(a * jax.nn.sigmoid(a) * b, wd_ref[0])

        @pl.when(j == nf - 1)
        def _():
            ffn = acc[...]
            xhat, rstd = _ln_stats(ALPHA * x_ref[...] + (1.0 + mod_ref[0, 5:6, :]) * ffn)
            diff = xhat * lw_ref[...] + lb_ref[...] - t_ref[...]
            loss = 0.5 * jnp.sum(jnp.sum(diff * diff, axis=-1, keepdims=True), axis=0, keepdims=True) / D
            dy = diff * (1.0 / D)
            dz = _ln_bwd(dy, xhat, rstd, lw_ref[...])
            dz_ref[...] = dz
            lane = lax.broadcasted_iota(jnp.int32, (1, D), 1)
            upd = jnp.concatenate(
                [jnp.sum(dy * xhat, axis=0, keepdims=True), jnp.sum(dy, axis=0, keepdims=True),
                 jnp.where(lane == 0, loss, 0.0), jnp.zeros((5, D), F32)], axis=0)
            dmu = jnp.concatenate(
                [jnp.zeros((5, D), F32), jnp.sum(dz * ffn, axis=0, keepdims=True), jnp.zeros((2, D), F32)], axis=0)

            @pl.when(i == 0)
            def _():
                st_ref[...] = upd

            @pl.when(i > 0)
            def _():
                st_ref[...] += upd

            @pl.when(i % tpb == 0)
            def _():
                dm_ref[0] = dmu

            @pl.when(i % tpb != 0)
            def _():
                dm_ref[0] += dmu

    row = lambda: pl.BlockSpec((tm, D), lambda i, j: (i, 0))
    vec = lambda: pl.BlockSpec((1, D), lambda i, j: (0, 0))
    return pl.pallas_call(
        body, name="ffn_fwd", grid=(t // tm, nf),
        in_specs=[row(), pl.BlockSpec((1, 8, D), lambda i, j: (i // tpb, 0, 0)),
                  pl.BlockSpec((1, D, tf), lambda i, j: (j, 0, 0)), pl.BlockSpec((1, D, tf), lambda i, j: (j, 0, 0)),
                  pl.BlockSpec((1, tf, D), lambda i, j: (j, 0, 0)), row(), vec(), vec()],
        out_specs=[pl.BlockSpec((1, tm, tf), lambda i, j: (j, i, 0)), pl.BlockSpec((1, tm, tf), lambda i, j: (j, i, 0)),
                   row(), row(), pl.BlockSpec((8, D), lambda i, j: (0, 0)),
                   pl.BlockSpec((1, 8, D), lambda i, j: (i // tpb, 0, 0))],
        out_shape=[jax.ShapeDtypeStruct((nf, t, tf), BF16), jax.ShapeDtypeStruct((nf, t, tf), BF16),
                   jax.ShapeDtypeStruct((t, D), BF16),
                   jax.ShapeDtypeStruct((t, D), F32), jax.ShapeDtypeStruct((8, D), F32),
                   jax.ShapeDtypeStruct((nbatch, 8, D), F32)],
        scratch_shapes=[pltpu.VMEM((tm, D), F32)],
        compiler_params=_params(("arbitrary", "arbitrary"), 60),
    )(x1, mod8, wg, wu, wd, target, ln2w, ln2b)


def _ffn_bwd(dz2, a, b, wg, wu, wd, x1, x2, u, mod8, ln1w, seq):
    t = x1.shape[0]
    tm = min(512, seq)
    nf, tf, _ = wg.shape
    tpb = seq // tm
    nbatch = t // seq

    def body(dz_ref, a_ref, b_ref, wg_ref, wu_ref, wd_ref, x1_ref, x_ref, u_ref, mod_ref, lw_ref,
             da_ref, db_ref, hm_ref, df_ref, du_ref, dxp_ref, st_ref, dm_ref, acc):
        i, j = pl.program_id(0), pl.program_id(1)

        @pl.when(j == 0)
        def _():
            df_ref[...] = ((1.0 + mod_ref[0, 5:6, :]) * dz_ref[...]).astype(BF16)
            acc[...] = jnp.zeros_like(acc)

        dhm = _dot(df_ref[...], wd_ref[0])
        av = a_ref[0].astype(F32)
        bv = b_ref[0].astype(F32)
        sg = jax.nn.sigmoid(av)
        sl = av * sg
        hm_ref[0] = (sl * bv).astype(BF16)
        da = (dhm * bv * (sg * (1.0 + av * (1.0 - sg)))).astype(BF16)
        db = (dhm * sl).astype(BF16)
        da_ref[0] = da
        db_ref[0] = db
        acc[...] += _dot(da, wg_ref[0]) + _dot(db, wu_ref[0])

        @pl.when(j == nf - 1)
        def _():
            dh2 = acc[...]
            x1v = x1_ref[...]
            uv = u_ref[...]
            dx1 = ALPHA * dz_ref[...] + dh2 * (1.0 + mod_ref[0, 4:5, :])
            xhat, rstd = _ln_stats(ALPHA * x_ref[...] + (1.0 + mod_ref[0, 2:3, :]) * uv)
            dz1 = _ln_bwd(dx1, xhat, rstd, lw_ref[...])
            du_ref[...] = ((1.0 + mod_ref[0, 2:3, :]) * dz1).astype(BF16)
            dxp_ref[...] = ALPHA * dz1
            upd = jnp.concatenate(
                [jnp.sum(dx1 * xhat, axis=0, keepdims=True), jnp.sum(dx1, axis=0, keepdims=True),
                 jnp.zeros((6, D), F32)], axis=0)
            dmu = jnp.concatenate(
                [jnp.zeros((2, D), F32), jnp.sum(dz1 * uv, axis=0, keepdims=True),
                 jnp.sum(dh2, axis=0, keepdims=True), jnp.sum(dh2 * x1v, axis=0, keepdims=True),
                 jnp.zeros((3, D), F32)], axis=0)

            @pl.when(i == 0)
            def _():
                st_ref[...] = upd

            @pl.when(i > 0)
            def _():
                st_ref[...] += upd

            @pl.when(i % tpb == 0)
            def _():
                dm_ref[0] = dmu

            @pl.when(i % tpb != 0)
            def _():
                dm_ref[0] += dmu

    row = lambda: pl.BlockSpec((tm, D), lambda i, j: (i, 0))
    ffb = lambda: pl.BlockSpec((1, tm, tf), lambda i, j: (j, i, 0))
    return pl.pallas_call(
        body, name="ffn_bwd", grid=(t // tm, nf),
        in_specs=[row(), ffb(), ffb(),
                  pl.BlockSpec((1, tf, D), lambda i, j: (j, 0, 0)), pl.BlockSpec((1, tf, D), lambda i, j: (j, 0, 0)),
                  pl.BlockSpec((1, D, tf), lambda i, j: (j, 0, 0)), row(), row(), row(),
                  pl.BlockSpec((1, 8, D), lambda i, j: (i // tpb, 0, 0)), pl.BlockSpec((1, D), lambda i, j: (0, 0))],
        out_specs=[ffb(), ffb(), ffb(), row(), row(), row(), pl.BlockSpec((8, D), lambda i, j: (0, 0)),
                   pl.BlockSpec((1, 8, D), lambda i, j: (i // tpb, 0, 0))],
        out_shape=[jax.ShapeDtypeStruct((nf, t, tf), BF16), jax.ShapeDtypeStruct((nf, t, tf), BF16),
                   jax.ShapeDtypeStruct((nf, t, tf), BF16), jax.ShapeDtypeStruct((t, D), BF16),
                   jax.ShapeDtypeStruct((t, D), BF16), jax.ShapeDtypeStruct((t, D), F32),
                   jax.ShapeDtypeStruct((8, D), F32), jax.ShapeDtypeStruct((nbatch, 8, D), F32)],
        scratch_shapes=[pltpu.VMEM((tm, D), F32)],
        compiler_params=_params(("arbitrary", "arbitrary"), 48),
    )(dz2, a, b, wg, wu, wd, x1, x2, u, mod8, ln1w)


def _adamw_math(w, g, m, v):
    m = B1 * m + (1.0 - B1) * g
    v = B2 * v + (1.0 - B2) * (g * g)
    m_hat = m / (1.0 - B1 ** STEP)
    v_hat = v / (1.0 - B2 ** STEP)
    return -LR * (m_hat / (jnp.sqrt(v_hat) + EPS) + WD * w), m, v


def _adamw(w, g, m, v, name):
    rows, cols = w.shape
    tr = rows
    for cand in (128, 64, 32, 16, 8):
        if rows % cand == 0:
            tr = cand
            break

    def body(w_ref, g_ref, m_ref, v_ref, d_ref, mo_ref, vo_ref):
        d, mn, vn = _adamw_math(w_ref[...], g_ref[...], m_ref[...], v_ref[...])
        d_ref[...] = d
        mo_ref[...] = mn
        vo_ref[...] = vn

    spec = pl.BlockSpec((tr, cols), lambda i: (i, 0))
    return pl.pallas_call(
        body, name=name, grid=(rows // tr,), in_specs=[spec] * 4, out_specs=[spec] * 3,
        out_shape=[jax.ShapeDtypeStruct((rows, cols), F32)] * 3,
        compiler_params=_params(("parallel",), 48),
    )(w, g, m, v)


def _adamw_halves(w, g_mine, g_sib, m, v, c_idx, name):
    rows, cols = w.shape
    hr = rows // 2
    tr = next(cand for cand in (128, 88, 64, 32, 16, 8) if hr % cand == 0)
    tph = hr // tr

    def body(c_ref, w_ref, gm_ref, gs_ref, m_ref, v_ref, g_ref, d_ref, mo_ref, vo_ref):
        g = jnp.where(pl.program_id(0) == c_ref[0], gm_ref[...], gs_ref[...])
        d, mn, vn = _adamw_math(w_ref[...], g, m_ref[...], v_ref[...])
        g_ref[...] = g
        d_ref[...] = d
        mo_ref[...] = mn
        vo_ref[...] = vn

    full = pl.BlockSpec((tr, cols), lambda h, i, c: (h * tph + i, 0))
    half = pl.BlockSpec((tr, cols), lambda h, i, c: (i, 0))
    return pl.pallas_call(
        body, name=name,
        grid_spec=pltpu.PrefetchScalarGridSpec(
            num_scalar_prefetch=1, grid=(2, tph), in_specs=[full, half, half, full, full], out_specs=[full] * 4),
        out_shape=[jax.ShapeDtypeStruct((rows, cols), F32)] * 4,
        compiler_params=_params(("parallel", "parallel"), 48),
    )(c_idx, w, g_mine, g_sib, m, v)


def _grad_w_ada(c_all, dmod_cols):
    def body(c_ref, d_ref, o_ref):
        c = c_ref[...]
        o_ref[...] = lax.dot_general(c * jax.nn.sigmoid(c), d_ref[...], (((0,), (0,)), ((), ())),
                                     preferred_element_type=F32, precision=HIGHEST)

    return pl.pallas_call(
        body, name="grad_w_ada", out_shape=jax.ShapeDtypeStruct((D, dmod_cols.shape[1]), F32),
        compiler_params=_params(vmem_mb=48),
    )(c_all, dmod_cols)


def _small_update(gath, w8, m8, v8):
    def body(g_ref, w_ref, m_ref, v_ref, go_ref, d_ref, mo_ref, vo_ref):
        g0 = g_ref[0, 0:1, :] + g_ref[0, 1:2, :]
        g1 = g_ref[0, 2:3, :]
        for dev in range(1, N_DEV):
            g0 = g0 + (g_ref[dev, 0:1, :] + g_ref[dev, 1:2, :])
            g1 = g1 + g_ref[dev, 2:3, :]
        w = w_ref[...]
        lb = jax.nn.sigmoid(w[1:2, O_LB0:O_LB1] - w[1:2, O_LB1:O_FOX])
        fac = lb * (1.0 - lb)
        g1 = jnp.concatenate([g1[:, :O_LB0], g1[:, O_LB0:O_LB1] * fac, -g1[:, O_LB1:O_FOX] * fac, g1[:, O_FOX:]],
                             axis=1)
        g = jnp.concatenate([g0, g1, jnp.zeros((6, SMALL_W), F32)], axis=0)
        d, mn, vn = _adamw_math(w, g, m_ref[...], v_ref[...])
        go_ref[...] = g
        d_ref[...] = d
        mo_ref[...] = mn
        vo_ref[...] = vn

    return pl.pallas_call(
        body, name="small_update", out_shape=[jax.ShapeDtypeStruct((8, SMALL_W), F32)] * 4,
        compiler_params=_params(vmem_mb=48),
    )(gath, w8, m8, v8)


def _pack_small(b_ada, ln1w, ln1b, ln2w, ln2b, norm_w, lb_logits, fox):
    row1 = jnp.concatenate([ln1w, ln1b, ln2w, ln2b, norm_w, lb_logits[0:1], lb_logits[1:2], fox,
                            jnp.zeros((1, SMALL_W - O_FOX - BH), F32)], axis=1)
    return jnp.concatenate([b_ada, row1, jnp.zeros((6, SMALL_W), F32)], axis=0)


def _unpack_small(p):
    r = p[1:2]
    lb = jnp.concatenate([r[:, O_LB0:O_LB1], r[:, O_LB1:O_FOX]], axis=0)
    return dict(b_ada=p[0:1], ln1_w=r[:, O_LN1W:O_LN1B], ln1_b=r[:, O_LN1B:O_LN2W], ln2_w=r[:, O_LN2W:O_LN2B],
                ln2_b=r[:, O_LN2B:O_NORM], hgrn_norm_w=r[:, O_NORM:O_LB0], lb_logits=lb,
                fox_f_bias=r[:, O_FOX:O_FOX + BH])


_BIG = ("w_in", "w_branch_a", "w_branch_b", "w_out", "w_ffn_gate", "w_ffn_up", "w_ffn_down")
_TRANSPOSED = ("w_ffn_gate", "w_ffn_up")


def _cols_of_chips(stacked):
    return jnp.concatenate([stacked[k] for k in range(N_CHIPS)], axis=1)


def kernel(x, c, w_ada, b_ada, w_in, fox_f_bias, lb_logits, hgrn_norm_w, w_branch_a, w_branch_b, w_out, ln1_w, ln1_b, w_ffn_gate, w_ffn_up, w_ffn_down, ln2_w, ln2_b, loss_target, m_w_ada, m_b_ada, m_w_in, m_fox_f_bias, m_lb_logits, m_hgrn_norm_w, m_w_branch_a, m_w_branch_b, m_w_out, m_ln1_w, m_ln1_b, m_w_ffn_gate, m_w_ffn_up, m_w_ffn_down, m_ln2_w, m_ln2_b, v_w_ada, v_b_ada, v_w_in, v_fox_f_bias, v_lb_logits, v_hgrn_norm_w, v_w_branch_a, v_w_branch_b, v_w_out, v_ln1_w, v_ln1_b, v_w_ffn_gate, v_w_ffn_up, v_w_ffn_down, v_ln2_w, v_ln2_b):
    nbatch, seq, _ = x.shape
    t = nbatch * seq
    ax, ay, ac = lax.axis_index("x"), lax.axis_index("y"), lax.axis_index("c")
    chip = 2 * ax + ay
    dev = 2 * chip + ac
    chip_arr = jnp.reshape(chip, (1,)).astype(jnp.int32)
    core_arr = jnp.reshape(ac, (1,)).astype(jnp.int32)

    tr = lambda a: jnp.swapaxes(a[0], 0, 1)
    shard_w = dict(w_in=w_in[0], w_branch_a=w_branch_a[0], w_branch_b=w_branch_b[0], w_out=w_out[0],
                   w_ffn_gate=tr(w_ffn_gate), w_ffn_up=tr(w_ffn_up), w_ffn_down=w_ffn_down[0])
    shard_m = dict(w_in=m_w_in[0], w_branch_a=m_w_branch_a[0], w_branch_b=m_w_branch_b[0], w_out=m_w_out[0],
                   w_ffn_gate=tr(m_w_ffn_gate), w_ffn_up=tr(m_w_ffn_up), w_ffn_down=m_w_ffn_down[0])
    shard_v = dict(w_in=v_w_in[0], w_branch_a=v_w_branch_a[0], w_branch_b=v_w_branch_b[0], w_out=v_w_out[0],
                   w_ffn_gate=tr(v_w_ffn_gate), w_ffn_up=tr(v_w_ffn_up), w_ffn_down=v_w_ffn_down[0])

    shard16 = {n: shard_w[n].astype(BF16) for n in _BIG}

    def with_mine(gathered, n):
        return lax.dynamic_update_slice(gathered, shard16[n][None], (chip, 0, 0))

    w_p = _permute_cols(_cols_of_chips(with_mine(_gather_weights([shard16["w_in"]])[0], "w_in")))
    late = _BIG[1:]
    late_send, late_recv, late_src, late_land, late_token = _split_start(
        _gather_copies, [shard16[n] for n in late],
        [lax.empty((N_CHIPS,) + shard16[n].shape, BF16) for n in late], "gather_late_start")

    c8 = jnp.concatenate([c, jnp.zeros((8 - nbatch, D), F32)], axis=0)
    c_all = _allgather8(c8, "gather_c")[:, :nbatch, :].reshape(N_DEV * nbatch, D)
    ncol = w_ada.shape[2]
    b_cols = lax.dynamic_slice_in_dim(b_ada, chip * ncol, ncol, axis=1)
    mod_g = _allgather8(_mod_shard(c_all, w_ada[0], b_cols), "gather_mod")
    mod_all = jnp.concatenate([mod_g[2 * k] for k in range(N_CHIPS)], axis=1)
    mod_mine = lax.dynamic_slice_in_dim(mod_all, dev * nbatch, nbatch, axis=0)
    mod8 = jnp.concatenate([mod_mine.reshape(nbatch, 6, D), jnp.zeros((nbatch, 2, D), F32)], axis=1)
    mod8 = mod8 + late_token[0, 0]

    x2 = x.reshape(t, D)
    tgt2 = loss_target.reshape(t, D)
    bias128 = jnp.concatenate([fox_f_bias, jnp.zeros((1, 128 - BH), F32)], axis=1)

    proj, h16 = _proj(x2, mod8, w_p, seq, BF16, "proj")
    projf = _rows_matmul(h16, w_p[:, COL_BF:], "proj_forget")
    ya, ckpt = _hgrn_fwd(proj, lb_logits, hgrn_norm_w, nbatch, seq)
    cum_cols = _fox_cum(projf, bias128, nbatch, seq)
    yb, lse = _fox_fwd(proj, cum_cols, nbatch, seq)
    late_land = _pass_to_sibling(
        _split_wait(_gather_copies, late_send, late_recv, late_src, late_land, yb, "gather_late_wait"))
    full = {n: with_mine(g, n) for n, g in zip(late, late_land)}
    wba, wbb = _cols_of_chips(full["w_branch_a"]), _cols_of_chips(full["w_branch_b"])
    wout = full["w_out"].reshape(D, D)
    wg_t, wu_t, wd = full["w_ffn_gate"], full["w_ffn_up"], full["w_ffn_down"]
    wg, wu, wd_t = jnp.swapaxes(wg_t, 1, 2), jnp.swapaxes(wu_t, 1, 2), jnp.swapaxes(wd, 1, 2)
    merged, u, x1 = _merge_fwd(ya, yb, proj, x2, mod8, wba, wbb, wout, ln1_w, ln1_b, seq)
    a_pre, b_pre, h2, dz2, st2, dm2 = _ffn_fwd(x1, mod8, wg, wu, wd, tgt2, ln2_w, ln2_b, seq)
    loss = lax.psum(st2[2, 0], ("x", "y", "c"))

    da, db, hmid, dffn, du, dxp, st1, dm1 = _ffn_bwd(dz2, a_pre, b_pre, wg_t, wu_t, wd_t, x1, x2, u, mod8, ln1_w, seq)
    g_st = {}
    g_st["w_ffn_down"] = _tn_matmul(hmid, dffn, "dw_ffn_down", seq)
    g_st["w_ffn_gate"] = _tn_matmul(da, h2, "dw_ffn_gate", seq)
    g_st["w_ffn_up"] = _tn_matmul(db, h2, "dw_ffn_up", seq)
    g_st["w_out"] = _tn_matmul(merged, du, "dw_out", seq).reshape(N_CHIPS, D // N_CHIPS, D)

    def sum_over_cores(names, tag):
        g_list = [g_st[n] for n in names]
        return [_add_my_half(g, o, core_arr, "grad_add_halves_" + n)
                for n, g, o in zip(names, g_list, _swap_halves(g_list, "grad_swap_halves_" + tag))]

    early = ("w_ffn_down", "w_ffn_gate", "w_ffn_up", "w_out")
    e_halves = sum_over_cores(early, "early")
    e_send, e_recv, e_src, e_land, e_token = _split_start(
        _scatter_copies, [h16 for _, h16 in e_halves],
        [lax.empty((3,) + h16.shape[1:], BF16) for _, h16 in e_halves], "grad_scatter_early_start")
    dproj, dpa, dpb, dya, dyb = _merge_bwd(du, ya, yb, proj, wba, wbb, wout, e_token, seq)
    g_st["w_branch_a"] = _tn_matmul(ya, dpa, "dw_branch_a", seq, split=D // N_CHIPS)
    g_st["w_branch_b"] = _tn_matmul(yb, dpb, "dw_branch_b", seq, split=D // N_CHIPS)
    dproj, dq, drs, dcs = _fox_bwd(proj, cum_cols, lse, yb, dyb, dproj, nbatch, seq)
    dproj = _place_cols(dproj, dq, COL_BQ)
    dproj, sm_fox = _fox_dbf(projf, bias128, drs, dcs, dproj, nbatch, seq)
    dproj, sm_hgrn = _hgrn_bwd(proj, dya, ckpt, lb_logits, hgrn_norm_w, dproj, nbatch, seq)
    grad_x2, dm0 = _dh_kernel(dproj, w_p, x2, dxp, mod8, seq)
    dw_in = _unpermute_cols(_tn_matmul(h16, dproj, "dw_in", seq))
    ncin = NIN // N_CHIPS
    g_st["w_in"] = jnp.stack([dw_in[:, k * ncin:(k + 1) * ncin] for k in range(N_CHIPS)])

    e_recv = _split_wait(_scatter_copies, e_send, e_recv, e_src, e_land, dw_in, "grad_scatter_early_wait")
    rest = ("w_in", "w_branch_a", "w_branch_b")
    r_halves = sum_over_cores(rest, "rest")
    r_send, r_rcv, r_src, r_land, r_token = _split_start(
        _scatter_copies, [h16 for _, h16 in r_halves],
        [lax.empty((3,) + h16.shape[1:], BF16) for _, h16 in r_halves], "grad_scatter_rest_start")

    def finish(names, halves, recv, token, tag):
        g_mine = [_add_chips(h32, r, chip_arr, "grad_add_chips_" + n) for n, (h32, _), r in zip(names, halves, recv)]
        g_sib = _join_halves(g_mine, token, "grad_join_halves_" + tag)
        for n, gm, gs in zip(names, g_mine, g_sib):
            grads[n], deltas[n], new_m[n], new_v[n] = _adamw_halves(
                shard_w[n], gm, gs, shard_m[n], shard_v[n], core_arr, "adamw_" + n)

    grads, deltas, new_m, new_v = {}, {}, {}, {}
    finish(early, e_halves, e_recv, r_token, "early")

    dmod = (dm0 + dm1 + dm2)[:, :6, :].reshape(nbatch, 6 * D)
    row2 = jnp.concatenate([st1[0:1], st1[1:2], st2[0:1], st2[1:2], sm_hgrn[1:2], sm_hgrn[0:1], sm_hgrn[0:1],
                            sm_fox[0:1, :BH], jnp.zeros((1, SMALL_W - O_FOX - BH), F32)], axis=1)
    spack = jnp.concatenate([dmod, row2, jnp.zeros((8 - nbatch - 1, SMALL_W), F32)], axis=0)
    spack = spack + r_token[0, 0]
    gath = _allgather8(spack, "gather_small")
    w8 = _pack_small(b_ada, ln1_w, ln1_b, ln2_w, ln2_b, hgrn_norm_w, lb_logits, fox_f_bias)
    m8 = _pack_small(m_b_ada, m_ln1_w, m_ln1_b, m_ln2_w, m_ln2_b, m_hgrn_norm_w, m_lb_logits, m_fox_f_bias)
    v8 = _pack_small(v_b_ada, v_ln1_w, v_ln1_b, v_ln2_w, v_ln2_b, v_hgrn_norm_w, v_lb_logits, v_fox_f_bias)
    sg, sd, smn, svn = (_unpack_small(p) for p in _small_update(gath, w8, m8, v8))
    dmod_all = gath[:, :nbatch, :].reshape(N_DEV * nbatch, SMALL_W)
    g_ada = _grad_w_ada(c_all, lax.dynamic_slice_in_dim(dmod_all, chip * ncol, ncol, axis=1))

    for group, small in zip((grads, deltas, new_m, new_v), (sg, sd, smn, svn)):
        group.update(small)
    grads["w_ada"] = g_ada
    deltas["w_ada"], new_m["w_ada"], new_v["w_ada"] = _adamw(w_ada[0], g_ada, m_w_ada[0], v_w_ada[0], "adamw_w_ada")
    done = sum(new_v[n][0:8, 0:128] for n in early) + new_v["w_ada"][0:8, 0:128]
    r_recv = _split_wait(_scatter_copies, r_send, r_rcv, r_src, r_land, done, "grad_scatter_rest_wait")
    finish(rest, r_halves, r_recv, late_token, "rest")

    names = ["w_ada", "b_ada", "w_in", "fox_f_bias", "lb_logits", "hgrn_norm_w", "w_branch_a", "w_branch_b", "w_out",
             "ln1_w", "ln1_b", "w_ffn_gate", "w_ffn_up", "w_ffn_down", "ln2_w", "ln2_b"]
    shapes = dict(w_ada=w_ada.shape, b_ada=b_ada.shape, w_in=w_in.shape, fox_f_bias=fox_f_bias.shape,
                  lb_logits=lb_logits.shape, hgrn_norm_w=hgrn_norm_w.shape, w_branch_a=w_branch_a.shape,
                  w_branch_b=w_branch_b.shape, w_out=w_out.shape, ln1_w=ln1_w.shape, ln1_b=ln1_b.shape,
                  w_ffn_gate=w_ffn_gate.shape, w_ffn_up=w_ffn_up.shape, w_ffn_down=w_ffn_down.shape,
                  ln2_w=ln2_w.shape, ln2_b=ln2_b.shape)
    outs = [loss, grad_x2.reshape(x.shape)]
    for group in (grads, deltas, new_m, new_v):
        outs += [(jnp.swapaxes(group[n], 0, 1) if n in _TRANSPOSED else group[n]).reshape(shapes[n]) for n in names]
    return tuple(outs)
```

```python
import functools
import math

import jax
import jax.numpy as jnp
import numpy as np
from jax import lax
from jax.experimental import pallas as pl
from jax.experimental.pallas import tpu as pltpu

F32 = jnp.float32
BF16 = jnp.bfloat16
MESH = pl.DeviceIdType.MESH
HIGHEST = lax.Precision.HIGHEST

D = 1024
AW = 512
AH = 4
ADH = 128
BH = 8
BDH = 64
DFF = 2816
NIN = 5640
NP = 5760
N_CHIPS = 4
N_DEV = 8
HGRN_BLOCK = 512
FFN_TOKENS = 512
COL_GATES = 0
COL_BQ = 2048
COL_KV = 2560
COL_A = 3584
COL_BF = 5632
ALPHA = 2.0 ** 0.25
LN_EPS = 1e-5
RMS_EPS = 1e-6
NEG = -1e30
LOG2E = 1.4426950408889634
LR, B1, B2, EPS, WD, STEP = 0.001, 0.9, 0.999, 1e-08, 0.01, 10
SMALL_W = 6144
O_LN1W, O_LN1B, O_LN2W, O_LN2B, O_NORM, O_LB0, O_LB1, O_FOX = 0, 1024, 2048, 3072, 4096, 4608, 5120, 5632


def _params(sem=None, vmem_mb=None):
    kw = {}
    if sem is not None:
        kw["dimension_semantics"] = sem
    if vmem_mb is not None:
        kw["vmem_limit_bytes"] = vmem_mb << 20
    return pltpu.CompilerParams(**kw)


def _dot(a, b):
    return jnp.dot(a.astype(BF16), b.astype(BF16), preferred_element_type=F32)


def _dot_nt(a, b):
    return lax.dot_general(a.astype(BF16), b.astype(BF16), (((1,), (1,)), ((), ())), preferred_element_type=F32)


def _dot_tn(a, b):
    return lax.dot_general(a.astype(BF16), b.astype(BF16), (((0,), (0,)), ((), ())), preferred_element_type=F32)


def _dot_f32(a, b):
    return jnp.dot(a, b, preferred_element_type=F32, precision=HIGHEST)


def _perm_segments():
    segs = [(3592, 5640), (2048, 2560)]
    for p in range(4):
        segs += [(2560 + 128 * p, 2688 + 128 * p), (3072 + 128 * p, 3200 + 128 * p)]
    for h in range(4):
        segs += [(128 * h + 512 * t, 128 * h + 512 * t + 128) for t in range(4)]
    segs += [(3584, 3592)]
    return segs


def _permute_cols(w):
    parts = [w[:, a:b] for a, b in _perm_segments()]
    parts.append(jnp.zeros((w.shape[0], NP - NIN), w.dtype))
    return jnp.concatenate(parts, axis=1)


def _unpermute_cols(g):
    pos, where = 0, {}
    for a, b in _perm_segments():
        where[a] = (pos, pos + b - a)
        pos += b - a
    parts = [g[:, where[a][0]:where[a][1]] for a in sorted(where)]
    return jnp.concatenate(parts, axis=1)


def _allgather8(v, name):
    rows, cols = v.shape

    def body(x_ref, out_ref, send_sems, recv_sems, local_sem):
        x, y, c = lax.axis_index("x"), lax.axis_index("y"), lax.axis_index("c")
        me, sibling = (x, y, c), (x, y, 1 - c)
        chips = [(1 - x, y), (x, 1 - y), (1 - x, 1 - y)]

        def slot(px, py, pc):
            return out_ref.at[4 * px + 2 * py + pc]

        def copy(k, block, to, src=None):
            return pltpu.make_async_remote_copy(
                src_ref=slot(*block) if src is None else src, dst_ref=slot(*block),
                send_sem=send_sems.at[k], recv_sem=recv_sems.at[k], device_id=to, device_id_type=MESH)

        mine = pltpu.make_async_copy(x_ref, slot(*me), local_sem)
        mine.start()
        first = [copy(0, me, sibling, src=x_ref)]
        first += [copy(1 + j, me, (*chip, c), src=x_ref) for j, chip in enumerate(chips)]
        for cp in first:
            cp.start()
        passed = [copy(4 + j, (*chip, c), sibling) for j, chip in enumerate(chips)]
        for j, chip in enumerate(chips):
            copy(1 + j, (*chip, c), me).wait_recv()
            passed[j].start()
        copy(0, sibling, me).wait_recv()
        for j, chip in enumerate(chips):
            copy(4 + j, (*chip, 1 - c), me).wait_recv()
        for cp in first + passed:
            cp.wait_send()
        mine.wait()

    return pl.pallas_call(
        body, name=name,
        out_shape=jax.ShapeDtypeStruct((N_DEV, rows, cols), v.dtype),
        in_specs=[pl.BlockSpec(memory_space=pltpu.VMEM)],
        out_specs=pl.BlockSpec(memory_space=pltpu.VMEM),
        scratch_shapes=[pltpu.SemaphoreType.DMA((7,)), pltpu.SemaphoreType.DMA((7,)), pltpu.SemaphoreType.DMA],
    )(v)


def _hbm_specs(n):
    return [pl.BlockSpec(memory_space=pl.ANY)] * n


def _gather_weights(shards):
    n = len(shards)

    def body(*refs):
        ins, outs, (send_sems, recv_sems) = refs[:n], refs[n:2 * n], refs[2 * n:]
        x, y, c = lax.axis_index("x"), lax.axis_index("y"), lax.axis_index("c")
        sibling = (x, y, 1 - c)
        chips = [(1 - x, y), (x, 1 - y), (1 - x, 1 - y)]

        def blk(w, px, py, half):
            hr = ins[w].shape[0] // 2
            return outs[w].at[2 * px + py, pl.ds(half * hr, hr), :]

        def copy(w, k, block, to, src=None):
            return pltpu.make_async_remote_copy(
                src_ref=blk(w, *block) if src is None else src, dst_ref=blk(w, *block),
                send_sem=send_sems.at[6 * w + k], recv_sem=recv_sems.at[6 * w + k], device_id=to, device_id_type=MESH)

        first = []
        for w in range(n):
            hr = ins[w].shape[0] // 2
            my_half = ins[w].at[pl.ds(c * hr, hr), :]
            first += [copy(w, j, (x, y, c), (*chip, c), src=my_half) for j, chip in enumerate(chips)]
        for cp in first:
            cp.start()
        passed = []
        for j, chip in enumerate(chips):
            for w in range(n):
                copy(w, j, (*chip, c), (x, y, c)).wait_recv()
                passed.append(copy(w, 3 + j, (*chip, c), sibling))
                passed[-1].start()
        for j, chip in enumerate(chips):
            for w in range(n):
                copy(w, 3 + j, (*chip, 1 - c), (x, y, c)).wait_recv()
        for cp in first + passed:
            cp.wait_send()

    return pl.pallas_call(
        body, name="gather_weights",
        out_shape=[jax.ShapeDtypeStruct((N_CHIPS,) + s.shape, s.dtype) for s in shards],
        in_specs=_hbm_specs(n), out_specs=_hbm_specs(n),
        scratch_shapes=[pltpu.SemaphoreType.DMA((6 * n,)), pltpu.SemaphoreType.DMA((6 * n,))],
    )(*shards)


def _swap_halves(grads, name):
    n = len(grads)

    def body(*refs):
        ins, outs, (send_sems, recv_sems) = refs[:n], refs[n:2 * n], refs[2 * n:]
        x, y, c = lax.axis_index("x"), lax.axis_index("y"), lax.axis_index("c")
        cps = []
        for w in range(n):
            hr = ins[w].shape[1] // 2
            cps.append(pltpu.make_async_remote_copy(
                src_ref=ins[w].at[:, pl.ds((1 - c) * hr, hr), :], dst_ref=outs[w],
                send_sem=send_sems.at[w], recv_sem=recv_sems.at[w], device_id=(x, y, 1 - c), device_id_type=MESH))
        for cp in cps:
            cp.start()
        for cp in cps:
            cp.wait()

    return pl.pallas_call(
        body, name=name,
        out_shape=[jax.ShapeDtypeStruct((N_CHIPS, g.shape[1] // 2, g.shape[2]), g.dtype) for g in grads],
        in_specs=_hbm_specs(n), out_specs=_hbm_specs(n),
        scratch_shapes=[pltpu.SemaphoreType.DMA((n,)), pltpu.SemaphoreType.DMA((n,))],
    )(*grads)


def _scatter_chips(reds, name):
    n = len(reds)

    def body(*refs):
        ins, outs, (send_sems, recv_sems) = refs[:n], refs[n:2 * n], refs[2 * n:]
        x, y, c = lax.axis_index("x"), lax.axis_index("y"), lax.axis_index("c")
        chips = [(1 - x, y), (x, 1 - y), (1 - x, 1 - y)]
        cps = [pltpu.make_async_remote_copy(
            src_ref=ins[w].at[2 * chip[0] + chip[1]], dst_ref=outs[w].at[j],
            send_sem=send_sems.at[3 * w + j], recv_sem=recv_sems.at[3 * w + j],
            device_id=(*chip, c), device_id_type=MESH)
            for j, chip in enumerate(chips) for w in range(n)]
        for cp in cps:
            cp.start()
        for cp in cps:
            cp.wait()

    return pl.pallas_call(
        body, name=name,
        out_shape=[jax.ShapeDtypeStruct((3,) + r.shape[1:], r.dtype) for r in reds],
        in_specs=_hbm_specs(n), out_specs=_hbm_specs(n),
        scratch_shapes=[pltpu.SemaphoreType.DMA((3 * n,)), pltpu.SemaphoreType.DMA((3 * n,))],
    )(*reds)


def _join_halves(halves, token, name):
    n = len(halves)

    def body(*refs):
        ins, outs, (send_sems, recv_sems) = refs[:n], refs[n + 1:2 * n + 1], refs[2 * n + 1:]
        x, y, c = lax.axis_index("x"), lax.axis_index("y"), lax.axis_index("c")
        cps = [pltpu.make_async_remote_copy(
            src_ref=ins[w], dst_ref=outs[w], send_sem=send_sems.at[w], recv_sem=recv_sems.at[w],
            device_id=(x, y, 1 - c), device_id_type=MESH) for w in range(n)]
        for cp in cps:
            cp.start()
        for cp in cps:
            cp.wait()

    return pl.pallas_call(
        body, name=name,
        out_shape=[jax.ShapeDtypeStruct(h.shape, h.dtype) for h in halves],
        in_specs=_hbm_specs(n + 1), out_specs=_hbm_specs(n),
        scratch_shapes=[pltpu.SemaphoreType.DMA((n,)), pltpu.SemaphoreType.DMA((n,))],
    )(*halves, token)


def _in_hbm(v):
    return pltpu.with_memory_space_constraint(v, pltpu.HBM)


_SPLIT_COPY = pltpu.CompilerParams(has_side_effects=pltpu.SideEffectType.DATAFLOW_SIDE_EFFECTING)


def _gather_copies(srcs, lands, send_sems, recv_sems):
    x, y, c = lax.axis_index("x"), lax.axis_index("y"), lax.axis_index("c")
    cps = []
    for w, (src, land) in enumerate(zip(srcs, lands)):
        hr = src.shape[0] // 2
        for j, chip in enumerate([(1 - x, y), (x, 1 - y), (1 - x, 1 - y)]):
            cps.append(pltpu.make_async_remote_copy(
                src_ref=src.at[pl.ds(c * hr, hr), :], dst_ref=land.at[2 * x + y, pl.ds(c * hr, hr), :],
                send_sem=send_sems.at[3 * w + j], recv_sem=recv_sems.at[3 * w + j],
                device_id=(*chip, c), device_id_type=MESH))
    return cps


def _scatter_copies(srcs, lands, send_sems, recv_sems):
    x, y, c = lax.axis_index("x"), lax.axis_index("y"), lax.axis_index("c")
    cps = []
    for w, (src, land) in enumerate(zip(srcs, lands)):
        for j, chip in enumerate([(1 - x, y), (x, 1 - y), (1 - x, 1 - y)]):
            cps.append(pltpu.make_async_remote_copy(
                src_ref=src.at[2 * chip[0] + chip[1]], dst_ref=land.at[j],
                send_sem=send_sems.at[3 * w + j], recv_sem=recv_sems.at[3 * w + j],
                device_id=(*chip, c), device_id_type=MESH))
    return cps


def _split_start(copies, srcs, lands, name):
    n = len(srcs)

    def body(*refs):
        src, lnd, send_sems, recv_sems, token = refs[:n], refs[n:2 * n], refs[2 * n], refs[2 * n + 1], refs[-1]
        for cp in copies(src, lnd, send_sems, recv_sems):
            cp.start()
        token[...] = jnp.zeros_like(token)

    hbm = pl.BlockSpec(memory_space=pltpu.HBM)
    sem = pl.BlockSpec(memory_space=pltpu.SEMAPHORE)
    outs = pl.pallas_call(
        body, name=name,
        out_shape=(pltpu.SemaphoreType.DMA((3 * n,)), pltpu.SemaphoreType.DMA((3 * n,)),
                   *[pltpu.HBM(v.shape, v.dtype) for v in srcs + lands], jax.ShapeDtypeStruct((8, 128), F32)),
        in_specs=[hbm] * (2 * n),
        out_specs=(sem, sem, *([hbm] * (2 * n)), pl.BlockSpec(memory_space=pltpu.VMEM)),
        input_output_aliases={i: 2 + i for i in range(2 * n)},
        compiler_params=_SPLIT_COPY,
    )(*[_in_hbm(v) for v in srcs + lands])
    return outs[0], outs[1], list(outs[2:2 + n]), list(outs[2 + n:2 + 2 * n]), outs[-1]


def _split_wait(copies, send_sems, recv_sems, srcs, lands, after, name):
    n = len(srcs)

    def body(*refs):
        src, lnd, send_sems, recv_sems = refs[:n], refs[n:2 * n], refs[2 * n], refs[2 * n + 1]
        for cp in copies(src, lnd, send_sems, recv_sems):
            cp.wait_send()
            cp.wait_recv()

    hbm = pl.BlockSpec(memory_space=pltpu.HBM)
    sem = pl.BlockSpec(memory_space=pltpu.SEMAPHORE)
    outs = pl.pallas_call(
        body, name=name,
        out_shape=tuple(pltpu.HBM(v.shape, v.dtype) for v in srcs + lands),
        in_specs=[hbm] * (2 * n) + [sem, sem, pl.BlockSpec(memory_space=pl.ANY)],
        out_specs=tuple([hbm] * (2 * n)),
        input_output_aliases={i: i for i in range(2 * n)},
        compiler_params=_SPLIT_COPY,
    )(*srcs, *lands, send_sems, recv_sems, after)
    return list(outs[n:])


def _pass_to_sibling(lands):
    n = len(lands)

    def body(*refs):
        ins, outs, (send_sems, recv_sems) = refs[:n], refs[n:2 * n], refs[2 * n:]
        x, y, c = lax.axis_index("x"), lax.axis_index("y"), lax.axis_index("c")
        cps = []
        for w in range(n):
            hr = ins[w].shape[1] // 2
            for j, chip in enumerate([(1 - x, y), (x, 1 - y), (1 - x, 1 - y)]):
                k = 2 * chip[0] + chip[1]
                cps.append(pltpu.make_async_remote_copy(
                    src_ref=ins[w].at[k, pl.ds(c * hr, hr), :], dst_ref=outs[w].at[k, pl.ds(c * hr, hr), :],
                    send_sem=send_sems.at[3 * w + j], recv_sem=recv_sems.at[3 * w + j],
                    device_id=(x, y, 1 - c), device_id_type=MESH))
        for cp in cps:
            cp.start()
        for cp in cps:
            cp.wait()

    return pl.pallas_call(
        body, name="gather_late_pass",
        out_shape=[jax.ShapeDtypeStruct(v.shape, v.dtype) for v in lands],
        in_specs=_hbm_specs(n), out_specs=_hbm_specs(n),
        input_output_aliases={i: i for i in range(n)},
        scratch_shapes=[pltpu.SemaphoreType.DMA((3 * n,)), pltpu.SemaphoreType.DMA((3 * n,))],
    )(*lands)


def _row_tile(rows):
    for cand in (256, 176, 128, 64, 32, 16):
        if rows % cand == 0:
            return cand
    raise ValueError(rows)


def _add_my_half(g, other, c_idx, name):
    _, k, n = g.shape
    hr = k // 2
    tr = _row_tile(hr)
    nb = hr // tr

    def body(c_ref, g_ref, o_ref, out_ref, out16_ref):
        s = g_ref[...] + o_ref[...]
        out_ref[...] = s
        out16_ref[...] = s.astype(BF16)

    return pl.pallas_call(
        body, name=name,
        grid_spec=pltpu.PrefetchScalarGridSpec(
            num_scalar_prefetch=1, grid=(N_CHIPS, nb),
            in_specs=[pl.BlockSpec((1, tr, n), lambda j, i, c: (j, c[0] * nb + i, 0)),
                      pl.BlockSpec((1, tr, n), lambda j, i, c: (j, i, 0))],
            out_specs=[pl.BlockSpec((1, tr, n), lambda j, i, c: (j, i, 0)),
                       pl.BlockSpec((1, tr, n), lambda j, i, c: (j, i, 0))]),
        out_shape=[jax.ShapeDtypeStruct((N_CHIPS, hr, n), F32), jax.ShapeDtypeStruct((N_CHIPS, hr, n), BF16)],
        compiler_params=_params(("parallel", "parallel")),
    )(c_idx, g, other)


def _add_chips(red, recv, chip_idx, name):
    _, hr, n = red.shape
    tr = _row_tile(hr)

    def body(k_ref, r_ref, v_ref, out_ref):
        out_ref[...] = ((r_ref[0] + v_ref[0].astype(F32)) + v_ref[1].astype(F32)) + v_ref[2].astype(F32)

    return pl.pallas_call(
        body, name=name,
        grid_spec=pltpu.PrefetchScalarGridSpec(
            num_scalar_prefetch=1, grid=(hr // tr,),
            in_specs=[pl.BlockSpec((1, tr, n), lambda i, k: (k[0], i, 0)),
                      pl.BlockSpec((3, tr, n), lambda i, k: (0, i, 0))],
            out_specs=pl.BlockSpec((tr, n), lambda i, k: (i, 0))),
        out_shape=jax.ShapeDtypeStruct((hr, n), F32),
        compiler_params=_params(("parallel",)),
    )(chip_idx, red, recv)


def _mod_shard(c_all, w_ada, b_ada):
    nb, cols = c_all.shape[0], w_ada.shape[1]

    def body(c_ref, w_ref, b_ref, o_ref):
        c = c_ref[...]
        o_ref[...] = _dot(c * jax.nn.sigmoid(c), w_ref[...]) + b_ref[...]

    return pl.pallas_call(
        body, name="mod_shard", out_shape=jax.ShapeDtypeStruct((nb, cols), F32),
        compiler_params=_params(vmem_mb=48),
    )(c_all, w_ada, b_ada)


def _proj(x2, mod8, w, seq, out_dtype, name):
    t = x2.shape[0]
    n = w.shape[1]
    tm, tn = min(2048, seq), min(1152, n)
    tpb = seq // tm

    def body(x_ref, mod_ref, w_ref, o_ref, h_ref):
        @pl.when(pl.program_id(1) == 0)
        def _():
            h_ref[...] = (x_ref[...] * (1.0 + mod_ref[0, 1:2, :]) + mod_ref[0, 0:1, :]).astype(BF16)
        o_ref[...] = jnp.dot(h_ref[...], w_ref[...], preferred_element_type=F32).astype(o_ref.dtype)

    return pl.pallas_call(
        body, name=name, grid=(t // tm, n // tn),
        in_specs=[pl.BlockSpec((tm, D), lambda i, j: (i, 0)),
                  pl.BlockSpec((1, 8, D), lambda i, j: (i // tpb, 0, 0)),
                  pl.BlockSpec((D, tn), lambda i, j: (0, j))],
        out_specs=[pl.BlockSpec((tm, tn), lambda i, j: (i, j)), pl.BlockSpec((tm, D), lambda i, j: (i, 0))],
        out_shape=[jax.ShapeDtypeStruct((t, n), out_dtype), jax.ShapeDtypeStruct((t, D), BF16)],
        compiler_params=_params(("parallel", "arbitrary"), 56),
    )(x2, mod8, w)


def _rows_matmul(a, w, name):
    t, k = a.shape
    n = w.shape[1]
    tm = 1024 if t % 1024 == 0 else t

    def body(a_ref, w_ref, o_ref):
        o_ref[...] = jnp.dot(a_ref[...], w_ref[...], preferred_element_type=F32)

    return pl.pallas_call(
        body, name=name, grid=(t // tm,),
        in_specs=[pl.BlockSpec((tm, k), lambda i: (i, 0)), pl.BlockSpec((k, n), lambda i: (0, 0))],
        out_specs=pl.BlockSpec((tm, n), lambda i: (i, 0)),
        out_shape=jax.ShapeDtypeStruct((t, n), F32),
        compiler_params=_params(("parallel",)),
    )(a, w)


def _tn_matmul(a, b, name, seq, split=None):
    a_st, b_st = a.ndim == 3, b.ndim == 3
    t, ka = a.shape[-2:]
    n = b.shape[-1]
    tt = min(1024, seq)
    nt = t // tt
    if a_st or b_st:
        steps, tn = (a.shape[0] if a_st else b.shape[0]), n
    else:
        tn = split
        if tn is None:
            tn = next(cand for cand in (1152, 1024, 1408, 512, n) if n % cand == 0)
        steps = n // tn
    stacked_out = a_st or b_st or split is not None

    def body(a_ref, b_ref, o_ref):
        part = _dot_tn(a_ref[0] if a_st else a_ref[...], b_ref[0] if b_st else b_ref[...])
        if stacked_out:
            part = part[None]

        @pl.when(pl.program_id(1) == 0)
        def _():
            o_ref[...] = part

        @pl.when(pl.program_id(1) > 0)
        def _():
            o_ref[...] += part

    if a_st:
        in_specs = [pl.BlockSpec((1, tt, ka), lambda j, k: (j, k, 0))]
    else:
        in_specs = [pl.BlockSpec((tt, ka), lambda j, k: (k, 0))]
    if b_st:
        in_specs.append(pl.BlockSpec((1, tt, n), lambda j, k: (j, k, 0)))
    else:
        in_specs.append(pl.BlockSpec((tt, tn), lambda j, k: (k, 0 if a_st else j)))
    if stacked_out:
        out_spec = pl.BlockSpec((1, ka, tn), lambda j, k: (j, 0, 0))
        out_shape = jax.ShapeDtypeStruct((steps, ka, tn), F32)
    else:
        out_spec = pl.BlockSpec((ka, tn), lambda j, k: (0, j))
        out_shape = jax.ShapeDtypeStruct((ka, n), F32)
    return pl.pallas_call(
        body, name=name, grid=(steps, nt), in_specs=in_specs, out_specs=out_spec, out_shape=out_shape,
        compiler_params=_params(("parallel", "arbitrary"), 56),
    )(a, b)


def _dh_kernel(dproj, w_p, x2, dxp, mod8, seq):
    t = x2.shape[0]
    tm, tk = min(1024, seq), 1152
    tpb = seq // tm
    nk = NP // tk
    nbatch = t // seq

    def body(dp_ref, w_ref, x_ref, dxp_ref, mod_ref, gx_ref, dm_ref, acc):
        i, k = pl.program_id(0), pl.program_id(1)

        @pl.when(k == 0)
        def _():
            acc[...] = jnp.zeros_like(acc)

        acc[...] += _dot_nt(dp_ref[...], w_ref[...])

        @pl.when(k == nk - 1)
        def _():
            dh = acc[...]
            gx_ref[...] = dxp_ref[...] + dh * (1.0 + mod_ref[0, 1:2, :])
            upd = jnp.concatenate(
                [jnp.sum(dh, axis=0, keepdims=True), jnp.sum(dh * x_ref[...], axis=0, keepdims=True),
                 jnp.zeros((6, D), F32)], axis=0)

            @pl.when(i % tpb == 0)
            def _():
                dm_ref[0] = upd

            @pl.when(i % tpb != 0)
            def _():
                dm_ref[0] += upd

    return pl.pallas_call(
        body, name="dh", grid=(t // tm, nk),
        in_specs=[pl.BlockSpec((tm, tk), lambda i, k: (i, k)),
                  pl.BlockSpec((D, tk), lambda i, k: (0, k)),
                  pl.BlockSpec((tm, D), lambda i, k: (i, 0)),
                  pl.BlockSpec((tm, D), lambda i, k: (i, 0)),
                  pl.BlockSpec((1, 8, D), lambda i, k: (i // tpb, 0, 0))],
        out_specs=[pl.BlockSpec((tm, D), lambda i, k: (i, 0)),
                   pl.BlockSpec((1, 8, D), lambda i, k: (i // tpb, 0, 0))],
        out_shape=[jax.ShapeDtypeStruct((t, D), F32), jax.ShapeDtypeStruct((nbatch, 8, D), F32)],
        scratch_shapes=[pltpu.VMEM((tm, D), F32)],
        compiler_params=_params(("arbitrary", "arbitrary"), 48),
    )(dproj, w_p, x2, dxp, mod8)


def _tri(n, upper):
    r = lax.broadcasted_iota(jnp.int32, (n, n), 0)
    c = lax.broadcasted_iota(jnp.int32, (n, n), 1)
    return jnp.where((c >= r) if upper else (c <= r), 1.0, 0.0).astype(F32)


@jax.custom_vjp
def _mm_nn(a, b):
    return _dot(a, b)


_mm_nn.defvjp(lambda a, b: (_dot(a, b), (a, b)),
              lambda res, g: (_dot_nt(g, res[1]), _dot_tn(res[0], g)))


@jax.custom_vjp
def _mm_nt(a, b):
    return _dot_nt(a, b)


_mm_nt.defvjp(lambda a, b: (_dot_nt(a, b), (a, b)),
              lambda res, g: (_dot(g, res[1]), _dot_tn(g, res[0])))


@jax.custom_vjp
def _mm_tn(a, b):
    return _dot_tn(a, b)


_mm_tn.defvjp(lambda a, b: (_dot_tn(a, b), (a, b)),
              lambda res, g: (_dot_nt(res[1], g), _dot(res[0], g)))


@jax.custom_vjp
def _cumsum_rows(x):
    return _dot_f32(_tri(x.shape[0], False), x)


_cumsum_rows.defvjp(lambda x: (_cumsum_rows(x), None),
                    lambda _, g: (_dot_f32(_tri(g.shape[0], True), g),))


@functools.partial(jax.custom_vjp, nondiff_argnums=(1,))
def _shift_rows(x, k):
    return pltpu.roll(x, k % x.shape[0], 0)


_shift_rows.defvjp(lambda x, k: (_shift_rows(x, k), None),
                   lambda k, _, g: (pltpu.roll(g, (-k) % g.shape[0], 0),))


def _group_ref(bc, m):
    n = bc.shape[0] // (2 * m)
    b3 = bc.reshape(n, 2 * m, ADH)
    row = lax.broadcasted_iota(jnp.int32, b3.shape, 1)
    ref = jnp.sum(jnp.where(row == m - 1, b3, 0.0), axis=1, keepdims=True)
    return jnp.broadcast_to(ref, b3.shape).reshape(bc.shape)


def _hgrn_block(q, fl, v, g, st, lb, nw):
    n = q.shape[0]
    f = lb + (1.0 - lb) * jax.nn.sigmoid(fl)
    kk = 1.0 - f
    lf = jnp.log(f)
    bc = _cumsum_rows(lf)
    row = lax.broadcasted_iota(jnp.int32, (n, ADH), 0)
    same = jnp.bitwise_xor(lax.broadcasted_iota(jnp.int32, (n, n), 0), lax.broadcasted_iota(jnp.int32, (n, n), 1))
    a = jnp.zeros((n, n), F32)
    m = 1
    while m < n:
        r = jnp.bitwise_and(row, 2 * m - 1)
        up, lo = r >= m, r < m
        if m == 1:
            aq, ak = lf, jnp.zeros_like(lf)
        elif m == 2:
            aq = jnp.where(r == 3, lf + _shift_rows(lf, 1), lf)
            ak = jnp.where(r == 0, _shift_rows(lf, -1), 0.0)
        else:
            ref = _group_ref(bc, m)
            aq, ak = bc - ref, ref - bc
        qt = jnp.where(up, q * jnp.exp(jnp.where(up, aq, 0.0)), 0.0)
        kt = jnp.where(lo, kk * jnp.exp(jnp.where(lo, ak, 0.0)), 0.0)
        a = a + jnp.where(same < 2 * m, _mm_nt(qt, kt), 0.0)
        m *= 2
    last = row == n - 1
    bl = jnp.sum(jnp.where(last, bc, 0.0), axis=0, keepdims=True)
    o = _mm_nn(a, v) + _mm_nt(q * jnp.exp(bc), st) + jnp.sum(q * kk, axis=-1, keepdims=True) * v
    st_new = st * jnp.exp(bl) + _mm_tn(v, kk * jnp.exp(bl - bc))
    rms = lax.rsqrt(jnp.mean(o * o, axis=-1, keepdims=True) + RMS_EPS)
    return o * rms * nw * jax.nn.sigmoid(g), st_new


def _hgrn_fwd(proj, lb_logits, norm_w, nbatch, seq):
    t = proj.shape[0]
    blk = min(HGRN_BLOCK, seq)
    nb = seq // blk

    def body(p_ref, lbl_ref, nw_ref, y_ref, ck_ref, st_s):
        @pl.when(pl.program_id(2) == 0)
        def _():
            st_s[...] = jnp.zeros_like(st_s)

        st = st_s[...]
        ck_ref[0] = st
        lb = jax.nn.sigmoid(lbl_ref[0:1, :] - lbl_ref[1:2, :])
        p = p_ref[...].astype(F32)
        y, st_new = _hgrn_block(p[:, 0:128], p[:, 128:256], p[:, 256:384], p[:, 384:512], st, lb, nw_ref[...])
        st_s[...] = st_new
        y_ref[...] = y.astype(y_ref.dtype)

    return pl.pallas_call(
        body, name="hgrn_fwd", grid=(AH, nbatch, nb),
        in_specs=[pl.BlockSpec((blk, 512), lambda h, b, i: (b * nb + i, COL_A // 512 + h)),
                  pl.BlockSpec((2, 128), lambda h, b, i: (0, h)),
                  pl.BlockSpec((1, 128), lambda h, b, i: (0, h))],
        out_specs=[pl.BlockSpec((blk, 128), lambda h, b, i: (b * nb + i, h)),
                   pl.BlockSpec((1, 128, 128), lambda h, b, i: ((h * nbatch + b) * nb + i, 0, 0))],
        out_shape=[jax.ShapeDtypeStruct((t, AW), BF16), jax.ShapeDtypeStruct((AH * nbatch * nb, 128, 128), F32)],
        scratch_shapes=[pltpu.VMEM((128, 128), F32)],
        compiler_params=_params(("parallel", "parallel", "arbitrary"), 48),
    )(proj, lb_logits, norm_w)


def _hgrn_bwd(proj, dya, ckpt, lb_logits, norm_w, dproj, nbatch, seq):
    t = proj.shape[0]
    blk = min(HGRN_BLOCK, seq)
    nb = seq // blk

    def body(p_ref, dy_ref, ck_ref, lbl_ref, nw_ref, dp_in, dp_ref, sm_ref, dst_s):
        del dp_in
        b_id, i = pl.program_id(1), pl.program_id(2)

        @pl.when(i == 0)
        def _():
            dst_s[...] = jnp.zeros_like(dst_s)

        lb = jax.nn.sigmoid(lbl_ref[0:1, :] - lbl_ref[1:2, :])
        p = p_ref[...].astype(F32)
        _, pullback = jax.vjp(_hgrn_block, p[:, 0:128], p[:, 128:256], p[:, 256:384], p[:, 384:512],
                              ck_ref[0], lb, nw_ref[...])
        dq, dfl, dv, dg, dst, dlb, dnw = pullback((dy_ref[...], dst_s[...]))
        dst_s[...] = dst
        dp_ref[:, 0:128] = dq.astype(dp_ref.dtype)
        dp_ref[:, 128:256] = dfl.astype(dp_ref.dtype)
        dp_ref[:, 256:384] = dv.astype(dp_ref.dtype)
        dp_ref[:, 384:512] = dg.astype(dp_ref.dtype)
        upd = jnp.concatenate([dlb, dnw, jnp.zeros((6, 128), F32)], axis=0)
        first = (b_id == 0) & (i == 0)

        @pl.when(first)
        def _():
            sm_ref[...] = upd

        @pl.when(jnp.logical_not(first))
        def _():
            sm_ref[...] += upd

    def rows(h, b, i):
        return b * nb + (nb - 1 - i)

    return pl.pallas_call(
        body, name="hgrn_bwd", grid=(AH, nbatch, nb),
        in_specs=[pl.BlockSpec((blk, 512), lambda h, b, i: (rows(h, b, i), COL_A // 512 + h)),
                  pl.BlockSpec((blk, 128), lambda h, b, i: (rows(h, b, i), h)),
                  pl.BlockSpec((1, 128, 128), lambda h, b, i: ((h * nbatch + b) * nb + (nb - 1 - i), 0, 0)),
                  pl.BlockSpec((2, 128), lambda h, b, i: (0, h)),
                  pl.BlockSpec((1, 128), lambda h, b, i: (0, h)),
                  pl.BlockSpec(memory_space=pl.ANY)],
        out_specs=[pl.BlockSpec((blk, 512), lambda h, b, i: (rows(h, b, i), COL_A // 512 + h)),
                   pl.BlockSpec((8, 128), lambda h, b, i: (0, h))],
        out_shape=[jax.ShapeDtypeStruct((t, NP), BF16), jax.ShapeDtypeStruct((8, AW), F32)],
        input_output_aliases={5: 0},
        scratch_shapes=[pltpu.VMEM((128, 128), F32)],
        compiler_params=_params(("parallel", "arbitrary", "arbitrary"), 48),
    )(proj, dya, ckpt, lb_logits, norm_w, dproj)


def _log_sigmoid(z):
    return jnp.minimum(z, 0.0) - jnp.log(1.0 + jnp.exp(-jnp.abs(z)))


def _fox_cum(proj, bias128, nbatch, seq):
    t = proj.shape[0]
    ts = min(512, seq)
    nb = seq // ts

    def body(p_ref, b_ref, c_ref, carry):
        @pl.when(pl.program_id(1) == 0)
        def _():
            carry[...] = jnp.zeros_like(carry)
        cum = _dot_f32(_tri(ts, False), _log_sigmoid(p_ref[...] + b_ref[...])) + carry[...]
        carry[...] = cum[ts - 1:ts, :]
        cum2 = cum * LOG2E
        for h in range(BH):
            c_ref[h] = jnp.broadcast_to(cum2[:, h:h + 1], (ts, 128))

    return pl.pallas_call(
        body, name="fox_cum", grid=(nbatch, nb),
        in_specs=[pl.BlockSpec((ts, 128), lambda b, i: (b * nb + i, 0)),
                  pl.BlockSpec((1, 128), lambda b, i: (0, 0))],
        out_specs=pl.BlockSpec((BH, ts, 128), lambda b, i: (0, b * nb + i, 0)),
        out_shape=jax.ShapeDtypeStruct((BH, t, 128), F32),
        scratch_shapes=[pltpu.VMEM((1, 128), F32)],
        compiler_params=_params(("parallel", "arbitrary")),
    )(proj, bias128)


def _fox_scores_t(q_ref, kv_ref, cc_ref, hh, masked, tq, tk):
    kh = kv_ref[:, 64 * hh:64 * hh + 64].astype(BF16)
    qh = (q_ref[:, 64 * hh:64 * hh + 64] * (LOG2E * BDH ** -0.5)).astype(BF16)
    s = _dot_nt(kh, qh) - jnp.tile(cc_ref[hh], (1, tq // 128))
    if masked:
        key = lax.broadcasted_iota(jnp.int32, (tk, tq), 0)
        qry = lax.broadcasted_iota(jnp.int32, (tk, tq), 1)
        s = jnp.where(key <= qry, s, NEG)
    return s, kh


def _causal_pairs(nq, key_major):
    if key_major:
        pairs = [(i, j) for j in range(nq) for i in range(j, nq)]
    else:
        pairs = [(i, j) for i in range(nq) for j in range(i + 1)]
    return (jnp.asarray([p[0] for p in pairs], jnp.int32), jnp.asarray([p[1] for p in pairs], jnp.int32))


def _with_ones_lane(x128, hh):
    lane = lax.broadcasted_iota(jnp.int32, x128.shape, 1)
    one = jnp.ones_like(x128)
    zero = jnp.zeros_like(x128)
    if hh == 0:
        return jnp.where(lane < 64, x128, jnp.where(lane == 64, one, zero))
    return jnp.where(lane >= 64, x128, jnp.where(lane == 0, one, zero))


def _fox_fwd(proj, cum_cols, nbatch, seq):
    t = proj.shape[0]
    tq = tk = min(512, seq)
    nq = seq // tq
    qi, kj = _causal_pairs(nq, key_major=False)

    def body(qi_ref, kj_ref, q_ref, kv_ref, cc_ref, o_ref, lse_ref, m_s, acc_s):
        s_id = pl.program_id(2)
        i, j = qi_ref[s_id], kj_ref[s_id]

        @pl.when(j == 0)
        def _():
            m_s[...] = jnp.full_like(m_s, NEG)
            acc_s[...] = jnp.zeros_like(acc_s)

        def step(masked):
            for hh in range(2):
                s, _ = _fox_scores_t(q_ref, kv_ref, cc_ref, hh, masked, tq, tk)
                m_prev = m_s[hh:hh + 1, :]
                m_new = jnp.maximum(m_prev, jnp.max(s, axis=0, keepdims=True))
                alpha = jnp.exp2(m_prev - m_new)
                p = jnp.exp2(s - m_new).astype(BF16)
                v_aug = _with_ones_lane(kv_ref[:, 128:256].astype(BF16), hh)
                acc_s[hh] = acc_s[hh] * alpha + _dot_tn(v_aug, p)
                m_s[hh:hh + 1, :] = m_new

        @pl.when(j < i)
        def _():
            step(False)

        @pl.when(j == i)
        def _():
            step(True)
            a0, a1 = acc_s[0], acc_s[1]
            l0, l1 = a0[64:65, :], a1[0:1, :]
            o_t = jnp.concatenate([a0[0:64, :] / l0, a1[64:128, :] / l1], axis=0)
            o_ref[...] = o_t.T.astype(o_ref.dtype)
            lse_ref[0, 0] = jnp.concatenate(
                [m_s[0:1, :] + jnp.log2(l0), m_s[1:2, :] + jnp.log2(l1), jnp.zeros((6, tq), F32)], axis=0)

    return pl.pallas_call(
        body, name="fox_fwd",
        grid_spec=pltpu.PrefetchScalarGridSpec(
            num_scalar_prefetch=2, grid=(nbatch, 4, qi.shape[0]),
            in_specs=[pl.BlockSpec((tq, 128), lambda b, p, s, qi, kj: (b * nq + qi[s], COL_BQ // 128 + p)),
                      pl.BlockSpec((tk, 256), lambda b, p, s, qi, kj: (b * nq + kj[s], COL_KV // 256 + p)),
                      pl.BlockSpec((2, tk, 128), lambda b, p, s, qi, kj: (p, b * nq + kj[s], 0))],
            out_specs=[pl.BlockSpec((tq, 128), lambda b, p, s, qi, kj: (b * nq + qi[s], p)),
                       pl.BlockSpec((1, 1, 8, tq), lambda b, p, s, qi, kj: (b, p, 0, qi[s]))],
            scratch_shapes=[pltpu.VMEM((8, tq), F32), pltpu.VMEM((2, 128, tq), F32)]),
        out_shape=[jax.ShapeDtypeStruct((t, 512), BF16), jax.ShapeDtypeStruct((nbatch, 4, 8, seq), F32)],
        compiler_params=_params(("parallel", "parallel", "arbitrary"), 48),
    )(qi, kj, proj, proj, cum_cols)


def _fox_bwd(proj, cum_cols, lse, yb, dyb, dproj, nbatch, seq):
    t = proj.shape[0]
    tq = tk = min(512, seq)
    nq = seq // tq
    scale = BDH ** -0.5
    qi, kj = _causal_pairs(nq, key_major=True)
    nsteps = qi.shape[0]

    def body(qi_ref, kj_ref, q_ref, kv_ref, cc_ref, lse_ref, o_ref, do_ref, dp_in,
             dkv_ref, dq_ref, drs_ref, dcs_ref, dk_s, dv_s, dqa_s):
        del dp_in
        hp, s_id = pl.program_id(1), pl.program_id(2)
        i, j = qi_ref[s_id], kj_ref[s_id]

        @pl.when(i == j)
        def _():
            dk_s[...] = jnp.zeros_like(dk_s)
            dv_s[...] = jnp.zeros_like(dv_s)

        @pl.when(s_id == 0)
        def _():
            dqa_s[...] = jnp.zeros_like(dqa_s)

        def step(masked):
            for hh in range(2):
                s, _ = _fox_scores_t(q_ref, kv_ref, cc_ref, hh, masked, tq, tk)
                p = jnp.exp2(s - lse_ref[0, 0, hh:hh + 1, :])
                doh = do_ref[:, 64 * hh:64 * hh + 64]
                dd = lax.dot_general(jnp.ones((8, 64), F32), doh * o_ref[:, 64 * hh:64 * hh + 64].astype(F32),
                                     (((1,), (1,)), ((), ())), preferred_element_type=F32, precision=HIGHEST)[0:1, :]
                doh = doh.astype(BF16)
                dp = _dot_nt(kv_ref[:, 128 + 64 * hh:192 + 64 * hh], doh)
                ds = (p * (dp - dd)).astype(BF16)
                dv_s[:, 64 * hh:64 * hh + 64] += _dot(p, doh)
                q_aug = _with_ones_lane((q_ref[...] * scale).astype(BF16), hh)
                dk_s[hh] += _dot(ds, q_aug)
                k_aug = _with_ones_lane(kv_ref[:, 0:128].astype(BF16), hh)
                dqa_s[i, hh] += _dot_tn(k_aug, ds)

        @pl.when(i == j)
        def _():
            step(True)

        @pl.when(i > j)
        def _():
            step(False)

        @pl.when(i == nq - 1)
        def _():
            lane = lax.broadcasted_iota(jnp.int32, (tk, 128), 1)
            k0, k1 = dk_s[0], dk_s[1]
            dkv_ref[:, 0:128] = jnp.where(lane < 64, k0, k1).astype(dkv_ref.dtype)
            dkv_ref[:, 128:256] = dv_s[...].astype(dkv_ref.dtype)
            dcs_ref[0] = jnp.where(lane == 2 * hp, k0[:, 64:65], jnp.where(lane == 2 * hp + 1, k1[:, 0:1], 0.0))

        @pl.when(s_id == nsteps - 1)
        def _():
            lane = lax.broadcasted_iota(jnp.int32, (tq, 128), 1)
            for blk in range(nq):
                a0 = dqa_s[blk, 0].T
                a1 = dqa_s[blk, 1].T
                rows = pl.ds(blk * tq, tq)
                dq_ref[rows, :] = (jnp.where(lane < 64, a0, a1) * scale).astype(dq_ref.dtype)
                drs_ref[0, rows, :] = jnp.where(lane == 2 * hp, a0[:, 64:65], jnp.where(lane == 2 * hp + 1, a1[:, 0:1], 0.0))

    return pl.pallas_call(
        body, name="fox_bwd",
        grid_spec=pltpu.PrefetchScalarGridSpec(
            num_scalar_prefetch=2, grid=(nbatch, 4, nsteps),
            in_specs=[pl.BlockSpec((tq, 128), lambda b, p, s, qi, kj: (b * nq + qi[s], COL_BQ // 128 + p)),
                      pl.BlockSpec((tk, 256), lambda b, p, s, qi, kj: (b * nq + kj[s], COL_KV // 256 + p)),
                      pl.BlockSpec((2, tk, 128), lambda b, p, s, qi, kj: (p, b * nq + kj[s], 0)),
                      pl.BlockSpec((1, 1, 8, tq), lambda b, p, s, qi, kj: (b, p, 0, qi[s])),
                      pl.BlockSpec((tq, 128), lambda b, p, s, qi, kj: (b * nq + qi[s], p)),
                      pl.BlockSpec((tq, 128), lambda b, p, s, qi, kj: (b * nq + qi[s], p)),
                      pl.BlockSpec(memory_space=pl.ANY)],
            out_specs=[pl.BlockSpec((tk, 256), lambda b, p, s, qi, kj: (b * nq + kj[s], COL_KV // 256 + p)),
                       pl.BlockSpec((seq, 128), lambda b, p, s, qi, kj: (b, p)),
                       pl.BlockSpec((1, seq, 128), lambda b, p, s, qi, kj: (p, b, 0)),
                       pl.BlockSpec((1, tk, 128), lambda b, p, s, qi, kj: (p, b * nq + kj[s], 0))],
            scratch_shapes=[pltpu.VMEM((2, tk, 128), F32), pltpu.VMEM((tk, 128), F32),
                            pltpu.VMEM((nq, 2, 128, tq), F32)]),
        out_shape=[jax.ShapeDtypeStruct((t, NP), BF16), jax.ShapeDtypeStruct((t, 512), BF16),
                   jax.ShapeDtypeStruct((4, t, 128), F32), jax.ShapeDtypeStruct((4, t, 128), F32)],
        input_output_aliases={8: 0},
        compiler_params=_params(("parallel", "parallel", "arbitrary"), 56),
    )(qi, kj, proj, proj, cum_cols, lse, yb, dyb, dproj)


def _place_cols(dproj, src, col):
    t, w = src.shape
    tm = 1024 if t % 1024 == 0 else t

    def body(s_ref, dp_in, o_ref):
        del dp_in
        o_ref[...] = s_ref[...]

    return pl.pallas_call(
        body, name="place_cols", grid=(t // tm,),
        in_specs=[pl.BlockSpec((tm, w), lambda i: (i, 0)), pl.BlockSpec(memory_space=pl.ANY)],
        out_specs=pl.BlockSpec((tm, w), lambda i: (i, col // w)),
        out_shape=jax.ShapeDtypeStruct(dproj.shape, dproj.dtype),
        input_output_aliases={1: 0},
        compiler_params=_params(("parallel",)),
    )(src, dproj)


def _fox_dbf(proj, bias128, drs, dcs, dproj, nbatch, seq):
    t = proj.shape[0]
    ts = min(512, seq)
    nb = seq // ts

    def body(p_ref, b_ref, dr_ref, dc_ref, dp_in, dp_ref, sm_ref, carry):
        del dp_in
        b_id, i = pl.program_id(0), pl.program_id(1)

        @pl.when(i == 0)
        def _():
            carry[...] = jnp.zeros_like(carry)

        dcum = (dr_ref[0] - dc_ref[0]) + (dr_ref[1] - dc_ref[1]) + (dr_ref[2] - dc_ref[2]) + (dr_ref[3] - dc_ref[3])
        rc = _dot_f32(_tri(ts, True), dcum) + carry[...]
        carry[...] = rc[0:1, :]
        z = p_ref[...] + b_ref[...]
        lane = lax.broadcasted_iota(jnp.int32, (ts, 128), 1)
        dz = jnp.where(lane < BH, rc * jax.nn.sigmoid(-z), 0.0)
        dp_ref[...] = dz.astype(dp_ref.dtype)
        upd = jnp.concatenate([jnp.sum(dz, axis=0, keepdims=True), jnp.zeros((7, 128), F32)], axis=0)
        first = (b_id == 0) & (i == 0)

        @pl.when(first)
        def _():
            sm_ref[...] = upd

        @pl.when(jnp.logical_not(first))
        def _():
            sm_ref[...] += upd

    def rows(b, i):
        return b * nb + (nb - 1 - i)

    return pl.pallas_call(
        body, name="fox_dbf", grid=(nbatch, nb),
        in_specs=[pl.BlockSpec((ts, 128), lambda b, i: (rows(b, i), 0)),
                  pl.BlockSpec((1, 128), lambda b, i: (0, 0)),
                  pl.BlockSpec((4, ts, 128), lambda b, i: (0, rows(b, i), 0)),
                  pl.BlockSpec((4, ts, 128), lambda b, i: (0, rows(b, i), 0)),
                  pl.BlockSpec(memory_space=pl.ANY)],
        out_specs=[pl.BlockSpec((ts, 128), lambda b, i: (rows(b, i), COL_BF // 128)),
                   pl.BlockSpec((8, 128), lambda b, i: (0, 0))],
        out_shape=[jax.ShapeDtypeStruct((t, NP), BF16), jax.ShapeDtypeStruct((8, 128), F32)],
        input_output_aliases={4: 0},
        scratch_shapes=[pltpu.VMEM((1, 128), F32)],
        compiler_params=_params(("arbitrary", "arbitrary")),
    )(proj, bias128, drs, dcs, dproj)


def _ln_stats(z):
    mu = jnp.mean(z, axis=-1, keepdims=True)
    zc = z - mu
    rstd = lax.rsqrt(jnp.mean(zc * zc, axis=-1, keepdims=True) + LN_EPS)
    return zc * rstd, rstd


def _ln_bwd(dy, xhat, rstd, w):
    dxh = dy * w
    return rstd * (dxh - jnp.mean(dxh, axis=-1, keepdims=True) - xhat * jnp.mean(dxh * xhat, axis=-1, keepdims=True))


def _merge_fwd(ya, yb, proj, x2, mod8, wba, wbb, wout, ln1w, ln1b, seq):
    t = x2.shape[0]
    tm = min(512, seq)
    tpb = seq // tm

    def body(ya_ref, yb_ref, g_ref, x_ref, mod_ref, wa_ref, wb_ref, wo_ref, lw_ref, lb_ref, mg_ref, u_ref, x1_ref):
        ga = jax.nn.sigmoid(g_ref[:, 0:D].astype(F32))
        gb = jax.nn.sigmoid(g_ref[:, D:2 * D].astype(F32))
        merged = (ga * jnp.dot(ya_ref[...], wa_ref[...], preferred_element_type=F32)
                  + gb * jnp.dot(yb_ref[...], wb_ref[...], preferred_element_type=F32))
        mg = merged.astype(BF16)
        mg_ref[...] = mg
        u = jnp.dot(mg, wo_ref[...], preferred_element_type=F32)
        u_ref[...] = u
        xhat, _ = _ln_stats(ALPHA * x_ref[...] + (1.0 + mod_ref[0, 2:3, :]) * u)
        x1_ref[...] = xhat * lw_ref[...] + lb_ref[...]

    tok = lambda w: pl.BlockSpec((tm, w), lambda i: (i, 0))
    full = lambda a: pl.BlockSpec(a.shape, lambda i: (0,) * a.ndim)
    return pl.pallas_call(
        body, name="merge_fwd", grid=(t // tm,),
        in_specs=[tok(512), tok(512), pl.BlockSpec((tm, 2048), lambda i: (i, COL_GATES // 2048)), tok(D),
                  pl.BlockSpec((1, 8, D), lambda i: (i // tpb, 0, 0)),
                  full(wba), full(wbb), full(wout), full(ln1w), full(ln1b)],
        out_specs=[tok(D), tok(D), tok(D)],
        out_shape=[jax.ShapeDtypeStruct((t, D), BF16), jax.ShapeDtypeStruct((t, D), F32),
                   jax.ShapeDtypeStruct((t, D), F32)],
        compiler_params=_params(("parallel",), 48),
    )(ya, yb, proj, x2, mod8, wba, wbb, wout, ln1w, ln1b)


def _merge_bwd(du, ya, yb, proj, wba, wbb, wout, token, seq):
    t = du.shape[0]
    tm = min(512, seq)

    def body(du_ref, ya_ref, yb_ref, g_ref, wa_ref, wb_ref, wo_ref, token_ref,
             dp_ref, dpa_ref, dpb_ref, dya_ref, dyb_ref):
        del token_ref
        ga = jax.nn.sigmoid(g_ref[:, 0:D].astype(F32))
        gb = jax.nn.sigmoid(g_ref[:, D:2 * D].astype(F32))
        dm = _dot_nt(du_ref[...], wo_ref[...])
        pa = jnp.dot(ya_ref[...], wa_ref[...], preferred_element_type=F32)
        pb = jnp.dot(yb_ref[...], wb_ref[...], preferred_element_type=F32)
        dpa = (dm * ga).astype(BF16)
        dpb = (dm * gb).astype(BF16)
        dpa_ref[...] = dpa
        dpb_ref[...] = dpb
        dp_ref[:, 0:D] = (dm * pa * ga * (1.0 - ga)).astype(BF16)
        dp_ref[:, D:2 * D] = (dm * pb * gb * (1.0 - gb)).astype(BF16)
        dya_ref[...] = _dot_nt(dpa, wa_ref[...])
        dyb_ref[...] = _dot_nt(dpb, wb_ref[...])

    tok = lambda w: pl.BlockSpec((tm, w), lambda i: (i, 0))
    full = lambda a: pl.BlockSpec(a.shape, lambda i: (0,) * a.ndim)
    return pl.pallas_call(
        body, name="merge_bwd", grid=(t // tm,),
        in_specs=[tok(D), tok(512), tok(512), pl.BlockSpec((tm, 2048), lambda i: (i, COL_GATES // 2048)),
                  full(wba), full(wbb), full(wout), full(token)],
        out_specs=[pl.BlockSpec((tm, 2048), lambda i: (i, COL_GATES // 2048)), tok(D), tok(D), tok(512), tok(512)],
        out_shape=[jax.ShapeDtypeStruct((t, NP), BF16), jax.ShapeDtypeStruct((t, D), BF16),
                   jax.ShapeDtypeStruct((t, D), BF16), jax.ShapeDtypeStruct((t, 512), F32),
                   jax.ShapeDtypeStruct((t, 512), F32)],
        compiler_params=_params(("parallel",), 48),
    )(du, ya, yb, proj, wba, wbb, wout, token)


def _ffn_fwd(x1, mod8, wg, wu, wd, target, ln2w, ln2b, seq):
    t = x1.shape[0]
    tm = min(FFN_TOKENS, seq)
    nf, _, tf = wg.shape
    tpb = seq // tm
    nbatch = t // seq

    def body(x_ref, mod_ref, wg_ref, wu_ref, wd_ref, t_ref, lw_ref, lb_ref,
             a_ref, b_ref, h_s, dz_ref, st_ref, dm_ref, acc):
        i, j = pl.program_id(0), pl.program_id(1)

        @pl.when(j == 0)
        def _():
            h_s[...] = (x_ref[...] * (1.0 + mod_ref[0, 4:5, :]) + mod_ref[0, 3:4, :]).astype(BF16)
            acc[...] = jnp.zeros_like(acc)

        a = jnp.dot(h_s[...], wg_ref[0], preferred_element_type=F32)
        b = jnp.dot(h_s[...], wu_ref[0], preferred_element_type=F32)
        a_ref[0] = a.astype(BF16)
        b_ref[0] = b.astype(BF16)
        acc[...] += _dot(a * jax.nn.sigmoid(a) * b, wd_ref[0])

        @pl.when(j == nf - 1)
        def _():
            ffn = acc[...]
            xhat, rstd = _ln_stats(ALPHA * x_ref[...] + (1.0 + mod_ref[0, 5:6, :]) * ffn)
            diff = xhat * lw_ref[...] + lb_ref[...] - t_ref[...]
            loss = 0.5 * jnp.sum(jnp.sum(diff * diff, axis=-1, keepdims=True), axis=0, keepdims=True) / D
            dy = diff * (1.0 / D)
            dz = _ln_bwd(dy, xhat, rstd, lw_ref[...])
            dz_ref[...] = dz
            lane = lax.broadcasted_iota(jnp.int32, (1, D), 1)
            upd = jnp.concatenate(
                [jnp.sum(dy * xhat, axis=0, keepdims=True), jnp.sum(dy, axis=0, keepdims=True),
                 jnp.where(lane == 0, loss, 0.0), jnp.zeros((5, D), F32)], axis=0)
            dmu = jnp.concatenate(
                [jnp.zeros((5, D), F32), jnp.sum(dz * ffn, axis=0, keepdims=True), jnp.zeros((2, D), F32)], axis=0)

            @pl.when(i == 0)
            def _():
                st_ref[...] = upd

            @pl.when(i > 0)
            def _():
                st_ref[...] += upd

            @pl.when(i % tpb == 0)
            def _():
                dm_ref[0] = dmu

            @pl.when(i % tpb != 0)
            def _():
                dm_ref[0] += dmu

    row = lambda: pl.BlockSpec((tm, D), lambda i, j: (i, 0))
    vec = lambda: pl.BlockSpec((1, D), lambda i, j: (0, 0))
    return pl.pallas_call(
        body, name="ffn_fwd", grid=(t // tm, nf),
        in_specs=[row(), pl.BlockSpec((1, 8, D), lambda i, j: (i // tpb, 0, 0)),
                  pl.BlockSpec((1, D, tf), lambda i, j: (j, 0, 0)), pl.BlockSpec((1, D, tf), lambda i, j: (j, 0, 0)),
                  pl.BlockSpec((1, tf, D), lambda i, j: (j, 0, 0)), row(), vec(), vec()],
        out_specs=[pl.BlockSpec((1, tm, tf), lambda i, j: (j, i, 0)), pl.BlockSpec((1, tm, tf), lambda i, j: (j, i, 0)),
                   row(), row(), pl.BlockSpec((8, D), lambda i, j: (0, 0)),
                   pl.BlockSpec((1, 8, D), lambda i, j: (i // tpb, 0, 0))],
        out_shape=[jax.ShapeDtypeStruct((nf, t, tf), BF16), jax.ShapeDtypeStruct((nf, t, tf), BF16),
                   jax.ShapeDtypeStruct((t, D), BF16),
                   jax.ShapeDtypeStruct((t, D), F32), jax.ShapeDtypeStruct((8, D), F32),
                   jax.ShapeDtypeStruct((nbatch, 8, D), F32)],
        scratch_shapes=[pltpu.VMEM((tm, D), F32)],
        compiler_params=_params(("arbitrary", "arbitrary"), 60),
    )(x1, mod8, wg, wu, wd, target, ln2w, ln2b)


def _ffn_bwd(dz2, a, b, wg, wu, wd, x1, x2, u, mod8, ln1w, seq):
    t = x1.shape[0]
    tm = min(512, seq)
    nf, tf, _ = wg.shape
    tpb = seq // tm
    nbatch = t // seq

    def body(dz_ref, a_ref, b_ref, wg_ref, wu_ref, wd_ref, x1_ref, x_ref, u_ref, mod_ref, lw_ref,
             da_ref, db_ref, hm_ref, df_ref, du_ref, dxp_ref, st_ref, dm_ref, acc):
        i, j = pl.program_id(0), pl.program_id(1)

        @pl.when(j == 0)
        def _():
            df_ref[...] = ((1.0 + mod_ref[0, 5:6, :]) * dz_ref[...]).astype(BF16)
            acc[...] = jnp.zeros_like(acc)

        dhm = _dot(df_ref[...], wd_ref[0])
        av = a_ref[0].astype(F32)
        bv = b_ref[0].astype(F32)
        sg = jax.nn.sigmoid(av)
        sl = av * sg
        hm_ref[0] = (sl * bv).astype(BF16)
        da = (dhm * bv * (sg * (1.0 + av * (1.0 - sg)))).astype(BF16)
        db = (dhm * sl).astype(BF16)
        da_ref[0] = da
        db_ref[0] = db
        acc[...] += _dot(da, wg_ref[0]) + _dot(db, wu_ref[0])

        @pl.when(j == nf - 1)
        def _():
            dh2 = acc[...]
            x1v = x1_ref[...]
            uv = u_ref[...]
            dx1 = ALPHA * dz_ref[...] + dh2 * (1.0 + mod_ref[0, 4:5, :])
            xhat, rstd = _ln_stats(ALPHA * x_ref[...] + (1.0 + mod_ref[0, 2:3, :]) * uv)
            dz1 = _ln_bwd(dx1, xhat, rstd, lw_ref[...])
            du_ref[...] = ((1.0 + mod_ref[0, 2:3, :]) * dz1).astype(BF16)
            dxp_ref[...] = ALPHA * dz1
            upd = jnp.concatenate(
                [jnp.sum(dx1 * xhat, axis=0, keepdims=True), jnp.sum(dx1, axis=0, keepdims=True),
                 jnp.zeros((6, D), F32)], axis=0)
            dmu = jnp.concatenate(
                [jnp.zeros((2, D), F32), jnp.sum(dz1 * uv, axis=0, keepdims=True),
                 jnp.sum(dh2, axis=0, keepdims=True), jnp.sum(dh2 * x1v, axis=0, keepdims=True),
                 jnp.zeros((3, D), F32)], axis=0)

            @pl.when(i == 0)
            def _():
                st_ref[...] = upd

            @pl.when(i > 0)
            def _():
                st_ref[...] += upd

            @pl.when(i % tpb == 0)
            def _():
                dm_ref[0] = dmu

            @pl.when(i % tpb != 0)
            def _():
                dm_ref[0] += dmu

    row = lambda: pl.BlockSpec((tm, D), lambda i, j: (i, 0))
    ffb = lambda: pl.BlockSpec((1, tm, tf), lambda i, j: (j, i, 0))
    return pl.pallas_call(
        body, name="ffn_bwd", grid=(t // tm, nf),
        in_specs=[row(), ffb(), ffb(),
                  pl.BlockSpec((1, tf, D), lambda i, j: (j, 0, 0)), pl.BlockSpec((1, tf, D), lambda i, j: (j, 0, 0)),
                  pl.BlockSpec((1, D, tf), lambda i, j: (j, 0, 0)), row(), row(), row(),
                  pl.BlockSpec((1, 8, D), lambda i, j: (i // tpb, 0, 0)), pl.BlockSpec((1, D), lambda i, j: (0, 0))],
        out_specs=[ffb(), ffb(), ffb(), row(), row(), row(), pl.BlockSpec((8, D), lambda i, j: (0, 0)),
                   pl.BlockSpec((1, 8, D), lambda i, j: (i // tpb, 0, 0))],
        out_shape=[jax.ShapeDtypeStruct((nf, t, tf), BF16), jax.ShapeDtypeStruct((nf, t, tf), BF16),
                   jax.ShapeDtypeStruct((nf, t, tf), BF16), jax.ShapeDtypeStruct((t, D), BF16),
                   jax.ShapeDtypeStruct((t, D), BF16), jax.ShapeDtypeStruct((t, D), F32),
                   jax.ShapeDtypeStruct((8, D), F32), jax.ShapeDtypeStruct((nbatch, 8, D), F32)],
        scratch_shapes=[pltpu.VMEM((tm, D), F32)],
        compiler_params=_params(("arbitrary", "arbitrary"), 48),
    )(dz2, a, b, wg, wu, wd, x1, x2, u, mod8, ln1w)


def _adamw_math(w, g, m, v):
    m = B1 * m + (1.0 - B1) * g
    v = B2 * v + (1.0 - B2) * (g * g)
    m_hat = m / (1.0 - B1 ** STEP)
    v_hat = v / (1.0 - B2 ** STEP)
    return -LR * (m_hat / (jnp.sqrt(v_hat) + EPS) + WD * w), m, v


def _adamw(w, g, m, v, name):
    rows, cols = w.shape
    tr = rows
    for cand in (128, 64, 32, 16, 8):
        if rows % cand == 0:
            tr = cand
            break

    def body(w_ref, g_ref, m_ref, v_ref, d_ref, mo_ref, vo_ref):
        d, mn, vn = _adamw_math(w_ref[...], g_ref[...], m_ref[...], v_ref[...])
        d_ref[...] = d
        mo_ref[...] = mn
        vo_ref[...] = vn

    spec = pl.BlockSpec((tr, cols), lambda i: (i, 0))
    return pl.pallas_call(
        body, name=name, grid=(rows // tr,), in_specs=[spec] * 4, out_specs=[spec] * 3,
        out_shape=[jax.ShapeDtypeStruct((rows, cols), F32)] * 3,
        compiler_params=_params(("parallel",), 48),
    )(w, g, m, v)


def _adamw_halves(w, g_mine, g_sib, m, v, c_idx, name):
    rows, cols = w.shape
    hr = rows // 2
    tr = next(cand for cand in (128, 88, 64, 32, 16, 8) if hr % cand == 0)
    tph = hr // tr

    def body(c_ref, w_ref, gm_ref, gs_ref, m_ref, v_ref, g_ref, d_ref, mo_ref, vo_ref):
        g = jnp.where(pl.program_id(0) == c_ref[0], gm_ref[...], gs_ref[...])
        d, mn, vn = _adamw_math(w_ref[...], g, m_ref[...], v_ref[...])
        g_ref[...] = g
        d_ref[...] = d
        mo_ref[...] = mn
        vo_ref[...] = vn

    full = pl.BlockSpec((tr, cols), lambda h, i, c: (h * tph + i, 0))
    half = pl.BlockSpec((tr, cols), lambda h, i, c: (i, 0))
    return pl.pallas_call(
        body, name=name,
        grid_spec=pltpu.PrefetchScalarGridSpec(
            num_scalar_prefetch=1, grid=(2, tph), in_specs=[full, half, half, full, full], out_specs=[full] * 4),
        out_shape=[jax.ShapeDtypeStruct((rows, cols), F32)] * 4,
        compiler_params=_params(("parallel", "parallel"), 48),
    )(c_idx, w, g_mine, g_sib, m, v)


def _grad_w_ada(c_all, dmod_cols):
    def body(c_ref, d_ref, o_ref):
        c = c_ref[...]
        o_ref[...] = lax.dot_general(c * jax.nn.sigmoid(c), d_ref[...], (((0,), (0,)), ((), ())),
                                     preferred_element_type=F32, precision=HIGHEST)

    return pl.pallas_call(
        body, name="grad_w_ada", out_shape=jax.ShapeDtypeStruct((D, dmod_cols.shape[1]), F32),
        compiler_params=_params(vmem_mb=48),
    )(c_all, dmod_cols)


def _small_update(gath, w8, m8, v8):
    def body(g_ref, w_ref, m_ref, v_ref, go_ref, d_ref, mo_ref, vo_ref):
        g0 = g_ref[0, 0:1, :] + g_ref[0, 1:2, :]
        g1 = g_ref[0, 2:3, :]
        for dev in range(1, N_DEV):
            g0 = g0 + (g_ref[dev, 0:1, :] + g_ref[dev, 1:2, :])
            g1 = g1 + g_ref[dev, 2:3, :]
        w = w_ref[...]
        lb = jax.nn.sigmoid(w[1:2, O_LB0:O_LB1] - w[1:2, O_LB1:O_FOX])
        fac = lb * (1.0 - lb)
        g1 = jnp.concatenate([g1[:, :O_LB0], g1[:, O_LB0:O_LB1] * fac, -g1[:, O_LB1:O_FOX] * fac, g1[:, O_FOX:]],
                             axis=1)
        g = jnp.concatenate([g0, g1, jnp.zeros((6, SMALL_W), F32)], axis=0)
        d, mn, vn = _adamw_math(w, g, m_ref[...], v_ref[...])
        go_ref[...] = g
        d_ref[...] = d
        mo_ref[...] = mn
        vo_ref[...] = vn

    return pl.pallas_call(
        body, name="small_update", out_shape=[jax.ShapeDtypeStruct((8, SMALL_W), F32)] * 4,
        compiler_params=_params(vmem_mb=48),
    )(gath, w8, m8, v8)


def _pack_small(b_ada, ln1w, ln1b, ln2w, ln2b, norm_w, lb_logits, fox):
    row1 = jnp.concatenate([ln1w, ln1b, ln2w, ln2b, norm_w, lb_logits[0:1], lb_logits[1:2], fox,
                            jnp.zeros((1, SMALL_W - O_FOX - BH), F32)], axis=1)
    return jnp.concatenate([b_ada, row1, jnp.zeros((6, SMALL_W), F32)], axis=0)


def _unpack_small(p):
    r = p[1:2]
    lb = jnp.concatenate([r[:, O_LB0:O_LB1], r[:, O_LB1:O_FOX]], axis=0)
    return dict(b_ada=p[0:1], ln1_w=r[:, O_LN1W:O_LN1B], ln1_b=r[:, O_LN1B:O_LN2W], ln2_w=r[:, O_LN2W:O_LN2B],
                ln2_b=r[:, O_LN2B:O_NORM], hgrn_norm_w=r[:, O_NORM:O_LB0], lb_logits=lb,
                fox_f_bias=r[:, O_FOX:O_FOX + BH])


_BIG = ("w_in", "w_branch_a", "w_branch_b", "w_out", "w_ffn_gate", "w_ffn_up", "w_ffn_down")
_TRANSPOSED = ("w_ffn_gate", "w_ffn_up")


def _cols_of_chips(stacked):
    return jnp.concatenate([stacked[k] for k in range(N_CHIPS)], axis=1)


def kernel(x, c, w_ada, b_ada, w_in, fox_f_bias, lb_logits, hgrn_norm_w, w_branch_a, w_branch_b, w_out, ln1_w, ln1_b, w_ffn_gate, w_ffn_up, w_ffn_down, ln2_w, ln2_b, loss_target, m_w_ada, m_b_ada, m_w_in, m_fox_f_bias, m_lb_logits, m_hgrn_norm_w, m_w_branch_a, m_w_branch_b, m_w_out, m_ln1_w, m_ln1_b, m_w_ffn_gate, m_w_ffn_up, m_w_ffn_down, m_ln2_w, m_ln2_b, v_w_ada, v_b_ada, v_w_in, v_fox_f_bias, v_lb_logits, v_hgrn_norm_w, v_w_branch_a, v_w_branch_b, v_w_out, v_ln1_w, v_ln1_b, v_w_ffn_gate, v_w_ffn_up, v_w_ffn_down, v_ln2_w, v_ln2_b):
    nbatch, seq, _ = x.shape
    t = nbatch * seq
    ax, ay, ac = lax.axis_index("x"), lax.axis_index("y"), lax.axis_index("c")
    chip = 2 * ax + ay
    dev = 2 * chip + ac
    chip_arr = jnp.reshape(chip, (1,)).astype(jnp.int32)
    core_arr = jnp.reshape(ac, (1,)).astype(jnp.int32)

    tr = lambda a: jnp.swapaxes(a[0], 0, 1)
    shard_w = dict(w_in=w_in[0], w_branch_a=w_branch_a[0], w_branch_b=w_branch_b[0], w_out=w_out[0],
                   w_ffn_gate=tr(w_ffn_gate), w_ffn_up=tr(w_ffn_up), w_ffn_down=w_ffn_down[0])
    shard_m = dict(w_in=m_w_in[0], w_branch_a=m_w_branch_a[0], w_branch_b=m_w_branch_b[0], w_out=m_w_out[0],
                   w_ffn_gate=tr(m_w_ffn_gate), w_ffn_up=tr(m_w_ffn_up), w_ffn_down=m_w_ffn_down[0])
    shard_v = dict(w_in=v_w_in[0], w_branch_a=v_w_branch_a[0], w_branch_b=v_w_branch_b[0], w_out=v_w_out[0],
                   w_ffn_gate=tr(v_w_ffn_gate), w_ffn_up=tr(v_w_ffn_up), w_ffn_down=v_w_ffn_down[0])

    shard16 = {n: shard_w[n].astype(BF16) for n in _BIG}

    def with_mine(gathered, n):
        return lax.dynamic_update_slice(gathered, shard16[n][None], (chip, 0, 0))

    w_p = _permute_cols(_cols_of_chips(with_mine(_gather_weights([shard16["w_in"]])[0], "w_in")))
    late = _BIG[1:]
    late_send, late_recv, late_src, late_land, late_token = _split_start(
        _gather_copies, [shard16[n] for n in late],
        [lax.empty((N_CHIPS,) + shard16[n].shape, BF16) for n in late], "gather_late_start")

    c8 = jnp.concatenate([c, jnp.zeros((8 - nbatch, D), F32)], axis=0)
    c_all = _allgather8(c8, "gather_c")[:, :nbatch, :].reshape(N_DEV * nbatch, D)
    ncol = w_ada.shape[2]
    b_cols = lax.dynamic_slice_in_dim(b_ada, chip * ncol, ncol, axis=1)
    mod_g = _allgather8(_mod_shard(c_all, w_ada[0], b_cols), "gather_mod")
    mod_all = jnp.concatenate([mod_g[2 * k] for k in range(N_CHIPS)], axis=1)
    mod_mine = lax.dynamic_slice_in_dim(mod_all, dev * nbatch, nbatch, axis=0)
    mod8 = jnp.concatenate([mod_mine.reshape(nbatch, 6, D), jnp.zeros((nbatch, 2, D), F32)], axis=1)
    mod8 = mod8 + late_token[0, 0]

    x2 = x.reshape(t, D)
    tgt2 = loss_target.reshape(t, D)
    bias128 = jnp.concatenate([fox_f_bias, jnp.zeros((1, 128 - BH), F32)], axis=1)

    proj, h16 = _proj(x2, mod8, w_p, seq, BF16, "proj")
    projf = _rows_matmul(h16, w_p[:, COL_BF:], "proj_forget")
    ya, ckpt = _hgrn_fwd(proj, lb_logits, hgrn_norm_w, nbatch, seq)
    cum_cols = _fox_cum(projf, bias128, nbatch, seq)
    yb, lse = _fox_fwd(proj, cum_cols, nbatch, seq)
    late_land = _pass_to_sibling(
        _split_wait(_gather_copies, late_send, late_recv, late_src, late_land, yb, "gather_late_wait"))
    full = {n: with_mine(g, n) for n, g in zip(late, late_land)}
    wba, wbb = _cols_of_chips(full["w_branch_a"]), _cols_of_chips(full["w_branch_b"])
    wout = full["w_out"].reshape(D, D)
    wg_t, wu_t, wd = full["w_ffn_gate"], full["w_ffn_up"], full["w_ffn_down"]
    wg, wu, wd_t = jnp.swapaxes(wg_t, 1, 2), jnp.swapaxes(wu_t, 1, 2), jnp.swapaxes(wd, 1, 2)
    merged, u, x1 = _merge_fwd(ya, yb, proj, x2, mod8, wba, wbb, wout, ln1_w, ln1_b, seq)
    a_pre, b_pre, h2, dz2, st2, dm2 = _ffn_fwd(x1, mod8, wg, wu, wd, tgt2, ln2_w, ln2_b, seq)
    loss = lax.psum(st2[2, 0], ("x", "y", "c"))

    da, db, hmid, dffn, du, dxp, st1, dm1 = _ffn_bwd(dz2, a_pre, b_pre, wg_t, wu_t, wd_t, x1, x2, u, mod8, ln1_w, seq)
    g_st = {}
    g_st["w_ffn_down"] = _tn_matmul(hmid, dffn, "dw_ffn_down", seq)
    g_st["w_ffn_gate"] = _tn_matmul(da, h2, "dw_ffn_gate", seq)
    g_st["w_ffn_up"] = _tn_matmul(db, h2, "dw_ffn_up", seq)
    g_st["w_out"] = _tn_matmul(merged, du, "dw_out", seq).reshape(N_CHIPS, D // N_CHIPS, D)

    def sum_over_cores(names, tag):
        g_list = [g_st[n] for n in names]
        return [_add_my_half(g, o, core_arr, "grad_add_halves_" + n)
                for n, g, o in zip(names, g_list, _swap_halves(g_list, "grad_swap_halves_" + tag))]

    early = ("w_ffn_down", "w_ffn_gate", "w_ffn_up", "w_out")
    e_halves = sum_over_cores(early, "early")
    e_send, e_recv, e_src, e_land, e_token = _split_start(
        _scatter_copies, [h16 for _, h16 in e_halves],
        [lax.empty((3,) + h16.shape[1:], BF16) for _, h16 in e_halves], "grad_scatter_early_start")
    dproj, dpa, dpb, dya, dyb = _merge_bwd(du, ya, yb, proj, wba, wbb, wout, e_token, seq)
    g_st["w_branch_a"] = _tn_matmul(ya, dpa, "dw_branch_a", seq, split=D // N_CHIPS)
    g_st["w_branch_b"] = _tn_matmul(yb, dpb, "dw_branch_b", seq, split=D // N_CHIPS)
    dproj, dq, drs, dcs = _fox_bwd(proj, cum_cols, lse, yb, dyb, dproj, nbatch, seq)
    dproj = _place_cols(dproj, dq, COL_BQ)
    dproj, sm_fox = _fox_dbf(projf, bias128, drs, dcs, dproj, nbatch, seq)
    dproj, sm_hgrn = _hgrn_bwd(proj, dya, ckpt, lb_logits, hgrn_norm_w, dproj, nbatch, seq)
    grad_x2, dm0 = _dh_kernel(dproj, w_p, x2, dxp, mod8, seq)
    dw_in = _unpermute_cols(_tn_matmul(h16, dproj, "dw_in", seq))
    ncin = NIN // N_CHIPS
    g_st["w_in"] = jnp.stack([dw_in[:, k * ncin:(k + 1) * ncin] for k in range(N_CHIPS)])

    e_recv = _split_wait(_scatter_copies, e_send, e_recv, e_src, e_land, dw_in, "grad_scatter_early_wait")
    rest = ("w_in", "w_branch_a", "w_branch_b")
    r_halves = sum_over_cores(rest, "rest")
    r_send, r_rcv, r_src, r_land, r_token = _split_start(
        _scatter_copies, [h16 for _, h16 in r_halves],
        [lax.empty((3,) + h16.shape[1:], BF16) for _, h16 in r_halves], "grad_scatter_rest_start")

    def finish(names, halves, recv, token, tag):
        g_mine = [_add_chips(h32, r, chip_arr, "grad_add_chips_" + n) for n, (h32, _), r in zip(names, halves, recv)]
        g_sib = _join_halves(g_mine, token, "grad_join_halves_" + tag)
        for n, gm, gs in zip(names, g_mine, g_sib):
            grads[n], deltas[n], new_m[n], new_v[n] = _adamw_halves(
                shard_w[n], gm, gs, shard_m[n], shard_v[n], core_arr, "adamw_" + n)

    grads, deltas, new_m, new_v = {}, {}, {}, {}
    finish(early, e_halves, e_recv, r_token, "early")

    dmod = (dm0 + dm1 + dm2)[:, :6, :].reshape(nbatch, 6 * D)
    row2 = jnp.concatenate([st1[0:1], st1[1:2], st2[0:1], st2[1:2], sm_hgrn[1:2], sm_hgrn[0:1], sm_hgrn[0:1],
                            sm_fox[0:1, :BH], jnp.zeros((1, SMALL_W - O_FOX - BH), F32)], axis=1)
    spack = jnp.concatenate([dmod, row2, jnp.zeros((8 - nbatch - 1, SMALL_W), F32)], axis=0)
    spack = spack + r_token[0, 0]
    gath = _allgather8(spack, "gather_small")
    w8 = _pack_small(b_ada, ln1_w, ln1_b, ln2_w, ln2_b, hgrn_norm_w, lb_logits, fox_f_bias)
    m8 = _pack_small(m_b_ada, m_ln1_w, m_ln1_b, m_ln2_w, m_ln2_b, m_hgrn_norm_w, m_lb_logits, m_fox_f_bias)
    v8 = _pack_small(v_b_ada, v_ln1_w, v_ln1_b, v_ln2_w, v_ln2_b, v_hgrn_norm_w, v_lb_logits, v_fox_f_bias)
    sg, sd, smn, svn = (_unpack_small(p) for p in _small_update(gath, w8, m8, v8))
    dmod_all = gath[:, :nbatch, :].reshape(N_DEV * nbatch, SMALL_W)
    g_ada = _grad_w_ada(c_all, lax.dynamic_slice_in_dim(dmod_all, chip * ncol, ncol, axis=1))

    for group, small in zip((grads, deltas, new_m, new_v), (sg, sd, smn, svn)):
        group.update(small)
    grads["w_ada"] = g_ada
    deltas["w_ada"], new_m["w_ada"], new_v["w_ada"] = _adamw(w_ada[0], g_ada, m_w_ada[0], v_w_ada[0], "adamw_w_ada")
    done = sum(new_v[n][0:8, 0:128] for n in early) + new_v["w_ada"][0:8, 0:128]
    r_recv = _split_wait(_scatter_copies, r_send, r_rcv, r_src, r_land, done, "grad_scatter_rest_wait")
    finish(rest, r_halves, r_recv, late_token, "rest")

    names = ["w_ada", "b_ada", "w_in", "fox_f_bias", "lb_logits", "hgrn_norm_w", "w_branch_a", "w_branch_b", "w_out",
             "ln1_w", "ln1_b", "w_ffn_gate", "w_ffn_up", "w_ffn_down", "ln2_w", "ln2_b"]
    shapes = dict(w_ada=w_ada.shape, b_ada=b_ada.shape, w_in=w_in.shape, fox_f_bias=fox_f_bias.shape,
                  lb_logits=lb_logits.shape, hgrn_norm_w=hgrn_norm_w.shape, w_branch_a=w_branch_a.shape,
                  w_branch_b=w_branch_b.shape, w_out=w_out.shape, ln1_w=ln1_w.shape, ln1_b=ln1_b.shape,
                  w_ffn_gate=w_ffn_gate.shape, w_ffn_up=w_ffn_up.shape, w_ffn_down=w_ffn_down.shape,
                  ln2_w=ln2_w.shape, ln2_b=ln2_b.shape)
    outs = [loss, grad_x2.reshape(x.shape)]
    for group in (grads, deltas, new_m, new_v):
        outs += [(jnp.swapaxes(group[n], 0, 1) if n in _TRANSPOSED else group[n]).reshape(shapes[n]) for n in names]
    return tuple(outs)
```

```python
import functools
import math

import jax
import jax.numpy as jnp
import numpy as np
from jax import lax
from jax.experimental import pallas as pl
from jax.experimental.pallas import tpu as pltpu

F32 = jnp.float32
BF16 = jnp.bfloat16
MESH = pl.DeviceIdType.MESH
HIGHEST = lax.Precision.HIGHEST

D = 1024
AW = 512
AH = 4
ADH = 128
BH = 8
BDH = 64
DFF = 2816
NIN = 5640
NP = 5760
N_CHIPS = 4
N_DEV = 8
HGRN_BLOCK = 256
FFN_TOKENS = 512
COL_GATES = 0
COL_BQ = 2048
COL_KV = 2560
COL_A = 3584
COL_BF = 5632
ALPHA = 2.0 ** 0.25
LN_EPS = 1e-5
RMS_EPS = 1e-6
NEG = -1e30
LOG2E = 1.4426950408889634
LR, B1, B2, EPS, WD, STEP = 0.001, 0.9, 0.999, 1e-08, 0.01, 10
SMALL_W = 6144
O_LN1W, O_LN1B, O_LN2W, O_LN2B, O_NORM, O_LB0, O_LB1, O_FOX = 0, 1024, 2048, 3072, 4096, 4608, 5120, 5632


def _params(sem=None, vmem_mb=None):
    kw = {}
    if sem is not None:
        kw["dimension_semantics"] = sem
    if vmem_mb is not None:
        kw["vmem_limit_bytes"] = vmem_mb << 20
    return pltpu.CompilerParams(**kw)


def _dot(a, b):
    return jnp.dot(a.astype(BF16), b.astype(BF16), preferred_element_type=F32)


def _dot_nt(a, b):
    return lax.dot_general(a.astype(BF16), b.astype(BF16), (((1,), (1,)), ((), ())), preferred_element_type=F32)


def _dot_tn(a, b):
    return lax.dot_general(a.astype(BF16), b.astype(BF16), (((0,), (0,)), ((), ())), preferred_element_type=F32)


def _dot_f32(a, b):
    return jnp.dot(a, b, preferred_element_type=F32, precision=HIGHEST)


def _perm_segments():
    segs = [(3592, 5640), (2048, 2560)]
    for p in range(4):
        segs += [(2560 + 128 * p, 2688 + 128 * p), (3072 + 128 * p, 3200 + 128 * p)]
    for h in range(4):
        segs += [(128 * h + 512 * t, 128 * h + 512 * t + 128) for t in range(4)]
    segs += [(3584, 3592)]
    return segs


def _permute_cols(w):
    parts = [w[:, a:b] for a, b in _perm_segments()]
    parts.append(jnp.zeros((w.shape[0], NP - NIN), w.dtype))
    return jnp.concatenate(parts, axis=1)


def _unpermute_cols(g):
    pos, where = 0, {}
    for a, b in _perm_segments():
        where[a] = (pos, pos + b - a)
        pos += b - a
    parts = [g[:, where[a][0]:where[a][1]] for a in sorted(where)]
    return jnp.concatenate(parts, axis=1)


def _allgather8(v, name):
    rows, cols = v.shape

    def body(x_ref, out_ref, send_sems, recv_sems, local_sem):
        x, y, c = lax.axis_index("x"), lax.axis_index("y"), lax.axis_index("c")
        me, sibling = (x, y, c), (x, y, 1 - c)
        chips = [(1 - x, y), (x, 1 - y), (1 - x, 1 - y)]

        def slot(px, py, pc):
            return out_ref.at[4 * px + 2 * py + pc]

        def copy(k, block, to, src=None):
            return pltpu.make_async_remote_copy(
                src_ref=slot(*block) if src is None else src, dst_ref=slot(*block),
                send_sem=send_sems.at[k], recv_sem=recv_sems.at[k], device_id=to, device_id_type=MESH)

        mine = pltpu.make_async_copy(x_ref, slot(*me), local_sem)
        mine.start()
        first = [copy(0, me, sibling, src=x_ref)]
        first += [copy(1 + j, me, (*chip, c), src=x_ref) for j, chip in enumerate(chips)]
        for cp in first:
            cp.start()
        passed = [copy(4 + j, (*chip, c), sibling) for j, chip in enumerate(chips)]
        for j, chip in enumerate(chips):
            copy(1 + j, (*chip, c), me).wait_recv()
            passed[j].start()
        copy(0, sibling, me).wait_recv()
        for j, chip in enumerate(chips):
            copy(4 + j, (*chip, 1 - c), me).wait_recv()
        for cp in first + passed:
            cp.wait_send()
        mine.wait()

    return pl.pallas_call(
        body, name=name,
        out_shape=jax.ShapeDtypeStruct((N_DEV, rows, cols), v.dtype),
        in_specs=[pl.BlockSpec(memory_space=pltpu.VMEM)],
        out_specs=pl.BlockSpec(memory_space=pltpu.VMEM),
        scratch_shapes=[pltpu.SemaphoreType.DMA((7,)), pltpu.SemaphoreType.DMA((7,)), pltpu.SemaphoreType.DMA],
    )(v)


def _hbm_specs(n):
    return [pl.BlockSpec(memory_space=pl.ANY)] * n


def _gather_weights(shards):
    n = len(shards)

    def body(*refs):
        ins, outs, (send_sems, recv_sems) = refs[:n], refs[n:2 * n], refs[2 * n:]
        x, y, c = lax.axis_index("x"), lax.axis_index("y"), lax.axis_index("c")
        sibling = (x, y, 1 - c)
        chips = [(1 - x, y), (x, 1 - y), (1 - x, 1 - y)]

        def blk(w, px, py, half):
            hr = ins[w].shape[0] // 2
            return outs[w].at[2 * px + py, pl.ds(half * hr, hr), :]

        def copy(w, k, block, to, src=None):
            return pltpu.make_async_remote_copy(
                src_ref=blk(w, *block) if src is None else src, dst_ref=blk(w, *block),
                send_sem=send_sems.at[6 * w + k], recv_sem=recv_sems.at[6 * w + k], device_id=to, device_id_type=MESH)

        first = []
        for w in range(n):
            hr = ins[w].shape[0] // 2
            my_half = ins[w].at[pl.ds(c * hr, hr), :]
            first += [copy(w, j, (x, y, c), (*chip, c), src=my_half) for j, chip in enumerate(chips)]
        for cp in first:
            cp.start()
        passed = []
        for j, chip in enumerate(chips):
            for w in range(n):
                copy(w, j, (*chip, c), (x, y, c)).wait_recv()
                passed.append(copy(w, 3 + j, (*chip, c), sibling))
                passed[-1].start()
        for j, chip in enumerate(chips):
            for w in range(n):
                copy(w, 3 + j, (*chip, 1 - c), (x, y, c)).wait_recv()
        for cp in first + passed:
            cp.wait_send()

    return pl.pallas_call(
        body, name="gather_weights",
        out_shape=[jax.ShapeDtypeStruct((N_CHIPS,) + s.shape, s.dtype) for s in shards],
        in_specs=_hbm_specs(n), out_specs=_hbm_specs(n),
        scratch_shapes=[pltpu.SemaphoreType.DMA((6 * n,)), pltpu.SemaphoreType.DMA((6 * n,))],
    )(*shards)


def _swap_halves(grads, name):
    n = len(grads)

    def body(*refs):
        ins, outs, (send_sems, recv_sems) = refs[:n], refs[n:2 * n], refs[2 * n:]
        x, y, c = lax.axis_index("x"), lax.axis_index("y"), lax.axis_index("c")
        cps = []
        for w in range(n):
            hr = ins[w].shape[1] // 2
            cps.append(pltpu.make_async_remote_copy(
                src_ref=ins[w].at[:, pl.ds((1 - c) * hr, hr), :], dst_ref=outs[w],
                send_sem=send_sems.at[w], recv_sem=recv_sems.at[w], device_id=(x, y, 1 - c), device_id_type=MESH))
        for cp in cps:
            cp.start()
        for cp in cps:
            cp.wait()

    return pl.pallas_call(
        body, name=name,
        out_shape=[jax.ShapeDtypeStruct((N_CHIPS, g.shape[1] // 2, g.shape[2]), g.dtype) for g in grads],
        in_specs=_hbm_specs(n), out_specs=_hbm_specs(n),
        scratch_shapes=[pltpu.SemaphoreType.DMA((n,)), pltpu.SemaphoreType.DMA((n,))],
    )(*grads)


def _scatter_chips(reds, name):
    n = len(reds)

    def body(*refs):
        ins, outs, (send_sems, recv_sems) = refs[:n], refs[n:2 * n], refs[2 * n:]
        x, y, c = lax.axis_index("x"), lax.axis_index("y"), lax.axis_index("c")
        chips = [(1 - x, y), (x, 1 - y), (1 - x, 1 - y)]
        cps = [pltpu.make_async_remote_copy(
            src_ref=ins[w].at[2 * chip[0] + chip[1]], dst_ref=outs[w].at[j],
            send_sem=send_sems.at[3 * w + j], recv_sem=recv_sems.at[3 * w + j],
            device_id=(*chip, c), device_id_type=MESH)
            for j, chip in enumerate(chips) for w in range(n)]
        for cp in cps:
            cp.start()
        for cp in cps:
            cp.wait()

    return pl.pallas_call(
        body, name=name,
        out_shape=[jax.ShapeDtypeStruct((3,) + r.shape[1:], r.dtype) for r in reds],
        in_specs=_hbm_specs(n), out_specs=_hbm_specs(n),
        scratch_shapes=[pltpu.SemaphoreType.DMA((3 * n,)), pltpu.SemaphoreType.DMA((3 * n,))],
    )(*reds)


def _join_halves(halves, token, name):
    n = len(halves)

    def body(*refs):
        ins, outs, (send_sems, recv_sems) = refs[:n], refs[n + 1:2 * n + 1], refs[2 * n + 1:]
        x, y, c = lax.axis_index("x"), lax.axis_index("y"), lax.axis_index("c")
        cps = [pltpu.make_async_remote_copy(
            src_ref=ins[w], dst_ref=outs[w], send_sem=send_sems.at[w], recv_sem=recv_sems.at[w],
            device_id=(x, y, 1 - c), device_id_type=MESH) for w in range(n)]
        for cp in cps:
            cp.start()
        for cp in cps:
            cp.wait()

    return pl.pallas_call(
        body, name=name,
        out_shape=[jax.ShapeDtypeStruct(h.shape, h.dtype) for h in halves],
        in_specs=_hbm_specs(n + 1), out_specs=_hbm_specs(n),
        scratch_shapes=[pltpu.SemaphoreType.DMA((n,)), pltpu.SemaphoreType.DMA((n,))],
    )(*halves, token)


def _in_hbm(v):
    return pltpu.with_memory_space_constraint(v, pltpu.HBM)


_SPLIT_COPY = pltpu.CompilerParams(has_side_effects=pltpu.SideEffectType.DATAFLOW_SIDE_EFFECTING)


def _gather_copies(srcs, lands, send_sems, recv_sems):
    x, y, c = lax.axis_index("x"), lax.axis_index("y"), lax.axis_index("c")
    cps = []
    for w, (src, land) in enumerate(zip(srcs, lands)):
        hr = src.shape[0] // 2
        for j, chip in enumerate([(1 - x, y), (x, 1 - y), (1 - x, 1 - y)]):
            cps.append(pltpu.make_async_remote_copy(
                src_ref=src.at[pl.ds(c * hr, hr), :], dst_ref=land.at[2 * x + y, pl.ds(c * hr, hr), :],
                send_sem=send_sems.at[3 * w + j], recv_sem=recv_sems.at[3 * w + j],
                device_id=(*chip, c), device_id_type=MESH))
    return cps


def _scatter_copies(srcs, lands, send_sems, recv_sems):
    x, y, c = lax.axis_index("x"), lax.axis_index("y"), lax.axis_index("c")
    cps = []
    for w, (src, land) in enumerate(zip(srcs, lands)):
        for j, chip in enumerate([(1 - x, y), (x, 1 - y), (1 - x, 1 - y)]):
            cps.append(pltpu.make_async_remote_copy(
                src_ref=src.at[2 * chip[0] + chip[1]], dst_ref=land.at[j],
                send_sem=send_sems.at[3 * w + j], recv_sem=recv_sems.at[3 * w + j],
                device_id=(*chip, c), device_id_type=MESH))
    return cps


def _split_start(copies, srcs, lands, name):
    n = len(srcs)

    def body(*refs):
        src, lnd, send_sems, recv_sems, token = refs[:n], refs[n:2 * n], refs[2 * n], refs[2 * n + 1], refs[-1]
        for cp in copies(src, lnd, send_sems, recv_sems):
            cp.start()
        token[...] = jnp.zeros_like(token)

    hbm = pl.BlockSpec(memory_space=pltpu.HBM)
    sem = pl.BlockSpec(memory_space=pltpu.SEMAPHORE)
    outs = pl.pallas_call(
        body, name=name,
        out_shape=(pltpu.SemaphoreType.DMA((3 * n,)), pltpu.SemaphoreType.DMA((3 * n,)),
                   *[pltpu.HBM(v.shape, v.dtype) for v in srcs + lands], jax.ShapeDtypeStruct((8, 128), F32)),
        in_specs=[hbm] * (2 * n),
        out_specs=(sem, sem, *([hbm] * (2 * n)), pl.BlockSpec(memory_space=pltpu.VMEM)),
        input_output_aliases={i: 2 + i for i in range(2 * n)},
        compiler_params=_SPLIT_COPY,
    )(*[_in_hbm(v) for v in srcs + lands])
    return outs[0], outs[1], list(outs[2:2 + n]), list(outs[2 + n:2 + 2 * n]), outs[-1]


def _split_wait(copies, send_sems, recv_sems, srcs, lands, after, name):
    n = len(srcs)

    def body(*refs):
        src, lnd, send_sems, recv_sems = refs[:n], refs[n:2 * n], refs[2 * n], refs[2 * n + 1]
        for cp in copies(src, lnd, send_sems, recv_sems):
            cp.wait_send()
            cp.wait_recv()

    hbm = pl.BlockSpec(memory_space=pltpu.HBM)
    sem = pl.BlockSpec(memory_space=pltpu.SEMAPHORE)
    outs = pl.pallas_call(
        body, name=name,
        out_shape=tuple(pltpu.HBM(v.shape, v.dtype) for v in srcs + lands),
        in_specs=[hbm] * (2 * n) + [sem, sem, pl.BlockSpec(memory_space=pl.ANY)],
        out_specs=tuple([hbm] * (2 * n)),
        input_output_aliases={i: i for i in range(2 * n)},
        compiler_params=_SPLIT_COPY,
    )(*srcs, *lands, send_sems, recv_sems, after)
    return list(outs[n:])


def _pass_to_sibling(lands):
    n = len(lands)

    def body(*refs):
        ins, outs, (send_sems, recv_sems) = refs[:n], refs[n:2 * n], refs[2 * n:]
        x, y, c = lax.axis_index("x"), lax.axis_index("y"), lax.axis_index("c")
        cps = []
        for w in range(n):
            hr = ins[w].shape[1] // 2
            for j, chip in enumerate([(1 - x, y), (x, 1 - y), (1 - x, 1 - y)]):
                k = 2 * chip[0] + chip[1]
                cps.append(pltpu.make_async_remote_copy(
                    src_ref=ins[w].at[k, pl.ds(c * hr, hr), :], dst_ref=outs[w].at[k, pl.ds(c * hr, hr), :],
                    send_sem=send_sems.at[3 * w + j], recv_sem=recv_sems.at[3 * w + j],
                    device_id=(x, y, 1 - c), device_id_type=MESH))
        for cp in cps:
            cp.start()
        for cp in cps:
            cp.wait()

    return pl.pallas_call(
        body, name="gather_late_pass",
        out_shape=[jax.ShapeDtypeStruct(v.shape, v.dtype) for v in lands],
        in_specs=_hbm_specs(n), out_specs=_hbm_specs(n),
        input_output_aliases={i: i for i in range(n)},
        scratch_shapes=[pltpu.SemaphoreType.DMA((3 * n,)), pltpu.SemaphoreType.DMA((3 * n,))],
    )(*lands)


def _row_tile(rows):
    for cand in (256, 176, 128, 64, 32, 16):
        if rows % cand == 0:
            return cand
    raise ValueError(rows)


def _add_my_half(g, other, c_idx, name):
    _, k, n = g.shape
    hr = k // 2
    tr = _row_tile(hr)
    nb = hr // tr

    def body(c_ref, g_ref, o_ref, out_ref, out16_ref):
        s = g_ref[...] + o_ref[...]
        out_ref[...] = s
        out16_ref[...] = s.astype(BF16)

    return pl.pallas_call(
        body, name=name,
        grid_spec=pltpu.PrefetchScalarGridSpec(
            num_scalar_prefetch=1, grid=(N_CHIPS, nb),
            in_specs=[pl.BlockSpec((1, tr, n), lambda j, i, c: (j, c[0] * nb + i, 0)),
                      pl.BlockSpec((1, tr, n), lambda j, i, c: (j, i, 0))],
            out_specs=[pl.BlockSpec((1, tr, n), lambda j, i, c: (j, i, 0)),
                       pl.BlockSpec((1, tr, n), lambda j, i, c: (j, i, 0))]),
        out_shape=[jax.ShapeDtypeStruct((N_CHIPS, hr, n), F32), jax.ShapeDtypeStruct((N_CHIPS, hr, n), BF16)],
        compiler_params=_params(("parallel", "parallel")),
    )(c_idx, g, other)


def _add_chips(red, recv, chip_idx, name):
    _, hr, n = red.shape
    tr = _row_tile(hr)

    def body(k_ref, r_ref, v_ref, out_ref):
        out_ref[...] = ((r_ref[0] + v_ref[0].astype(F32)) + v_ref[1].astype(F32)) + v_ref[2].astype(F32)

    return pl.pallas_call(
        body, name=name,
        grid_spec=pltpu.PrefetchScalarGridSpec(
            num_scalar_prefetch=1, grid=(hr // tr,),
            in_specs=[pl.BlockSpec((1, tr, n), lambda i, k: (k[0], i, 0)),
                      pl.BlockSpec((3, tr, n), lambda i, k: (0, i, 0))],
            out_specs=pl.BlockSpec((tr, n), lambda i, k: (i, 0))),
        out_shape=jax.ShapeDtypeStruct((hr, n), F32),
        compiler_params=_params(("parallel",)),
    )(chip_idx, red, recv)


def _mod_shard(c_all, w_ada, b_ada):
    nb, cols = c_all.shape[0], w_ada.shape[1]

    def body(c_ref, w_ref, b_ref, o_ref):
        c = c_ref[...]
        o_ref[...] = _dot(c * jax.nn.sigmoid(c), w_ref[...]) + b_ref[...]

    return pl.pallas_call(
        body, name="mod_shard", out_shape=jax.ShapeDtypeStruct((nb, cols), F32),
        compiler_params=_params(vmem_mb=48),
    )(c_all, w_ada, b_ada)


def _proj(x2, mod8, w, seq, out_dtype, name):
    t = x2.shape[0]
    n = w.shape[1]
    tm, tn = min(2048, seq), min(1152, n)
    tpb = seq // tm

    def body(x_ref, mod_ref, w_ref, o_ref, h_ref):
        @pl.when(pl.program_id(1) == 0)
        def _():
            h_ref[...] = (x_ref[...] * (1.0 + mod_ref[0, 1:2, :]) + mod_ref[0, 0:1, :]).astype(BF16)
        o_ref[...] = jnp.dot(h_ref[...], w_ref[...], preferred_element_type=F32).astype(o_ref.dtype)

    return pl.pallas_call(
        body, name=name, grid=(t // tm, n // tn),
        in_specs=[pl.BlockSpec((tm, D), lambda i, j: (i, 0)),
                  pl.BlockSpec((1, 8, D), lambda i, j: (i // tpb, 0, 0)),
                  pl.BlockSpec((D, tn), lambda i, j: (0, j))],
        out_specs=[pl.BlockSpec((tm, tn), lambda i, j: (i, j)), pl.BlockSpec((tm, D), lambda i, j: (i, 0))],
        out_shape=[jax.ShapeDtypeStruct((t, n), out_dtype), jax.ShapeDtypeStruct((t, D), BF16)],
        compiler_params=_params(("parallel", "arbitrary"), 56),
    )(x2, mod8, w)


def _rows_matmul(a, w, name):
    t, k = a.shape
    n = w.shape[1]
    tm = 1024 if t % 1024 == 0 else t

    def body(a_ref, w_ref, o_ref):
        o_ref[...] = jnp.dot(a_ref[...], w_ref[...], preferred_element_type=F32)

    return pl.pallas_call(
        body, name=name, grid=(t // tm,),
        in_specs=[pl.BlockSpec((tm, k), lambda i: (i, 0)), pl.BlockSpec((k, n), lambda i: (0, 0))],
        out_specs=pl.BlockSpec((tm, n), lambda i: (i, 0)),
        out_shape=jax.ShapeDtypeStruct((t, n), F32),
        compiler_params=_params(("parallel",)),
    )(a, w)


def _tn_matmul(a, b, name, seq, split=None):
    a_st, b_st = a.ndim == 3, b.ndim == 3
    t, ka = a.shape[-2:]
    n = b.shape[-1]
    tt = min(1024, seq)
    nt = t // tt
    if a_st or b_st:
        steps, tn = (a.shape[0] if a_st else b.shape[0]), n
    else:
        tn = split
        if tn is None:
            tn = next(cand for cand in (1152, 1024, 1408, 512, n) if n % cand == 0)
        steps = n // tn
    stacked_out = a_st or b_st or split is not None

    def body(a_ref, b_ref, o_ref):
        part = _dot_tn(a_ref[0] if a_st else a_ref[...], b_ref[0] if b_st else b_ref[...])
        if stacked_out:
            part = part[None]

        @pl.when(pl.program_id(1) == 0)
        def _():
            o_ref[...] = part

        @pl.when(pl.program_id(1) > 0)
        def _():
            o_ref[...] += part

    if a_st:
        in_specs = [pl.BlockSpec((1, tt, ka), lambda j, k: (j, k, 0))]
    else:
        in_specs = [pl.BlockSpec((tt, ka), lambda j, k: (k, 0))]
    if b_st:
        in_specs.append(pl.BlockSpec((1, tt, n), lambda j, k: (j, k, 0)))
    else:
        in_specs.append(pl.BlockSpec((tt, tn), lambda j, k: (k, 0 if a_st else j)))
    if stacked_out:
        out_spec = pl.BlockSpec((1, ka, tn), lambda j, k: (j, 0, 0))
        out_shape = jax.ShapeDtypeStruct((steps, ka, tn), F32)
    else:
        out_spec = pl.BlockSpec((ka, tn), lambda j, k: (0, j))
        out_shape = jax.ShapeDtypeStruct((ka, n), F32)
    return pl.pallas_call(
        body, name=name, grid=(steps, nt), in_specs=in_specs, out_specs=out_spec, out_shape=out_shape,
        compiler_params=_params(("parallel", "arbitrary"), 56),
    )(a, b)


def _dh_kernel(dproj, w_p, x2, dxp, mod8, seq):
    t = x2.shape[0]
    tm, tk = min(1024, seq), 1152
    tpb = seq // tm
    nk = NP // tk
    nbatch = t // seq

    def body(dp_ref, w_ref, x_ref, dxp_ref, mod_ref, gx_ref, dm_ref, acc):
        i, k = pl.program_id(0), pl.program_id(1)

        @pl.when(k == 0)
        def _():
            acc[...] = jnp.zeros_like(acc)

        acc[...] += _dot_nt(dp_ref[...], w_ref[...])

        @pl.when(k == nk - 1)
        def _():
            dh = acc[...]
            gx_ref[...] = dxp_ref[...] + dh * (1.0 + mod_ref[0, 1:2, :])
            upd = jnp.concatenate(
                [jnp.sum(dh, axis=0, keepdims=True), jnp.sum(dh * x_ref[...], axis=0, keepdims=True),
                 jnp.zeros((6, D), F32)], axis=0)

            @pl.when(i % tpb == 0)
            def _():
                dm_ref[0] = upd

            @pl.when(i % tpb != 0)
            def _():
                dm_ref[0] += upd

    return pl.pallas_call(
        body, name="dh", grid=(t // tm, nk),
        in_specs=[pl.BlockSpec((tm, tk), lambda i, k: (i, k)),
                  pl.BlockSpec((D, tk), lambda i, k: (0, k)),
                  pl.BlockSpec((tm, D), lambda i, k: (i, 0)),
                  pl.BlockSpec((tm, D), lambda i, k: (i, 0)),
                  pl.BlockSpec((1, 8, D), lambda i, k: (i // tpb, 0, 0))],
        out_specs=[pl.BlockSpec((tm, D), lambda i, k: (i, 0)),
                   pl.BlockSpec((1, 8, D), lambda i, k: (i // tpb, 0, 0))],
        out_shape=[jax.ShapeDtypeStruct((t, D), F32), jax.ShapeDtypeStruct((nbatch, 8, D), F32)],
        scratch_shapes=[pltpu.VMEM((tm, D), F32)],
        compiler_params=_params(("arbitrary", "arbitrary"), 48),
    )(dproj, w_p, x2, dxp, mod8)


def _tri(n, upper):
    r = lax.broadcasted_iota(jnp.int32, (n, n), 0)
    c = lax.broadcasted_iota(jnp.int32, (n, n), 1)
    return jnp.where((c >= r) if upper else (c <= r), 1.0, 0.0).astype(F32)


@jax.custom_vjp
def _mm_nn(a, b):
    return _dot(a, b)


_mm_nn.defvjp(lambda a, b: (_dot(a, b), (a, b)),
              lambda res, g: (_dot_nt(g, res[1]), _dot_tn(res[0], g)))


@jax.custom_vjp
def _mm_nt(a, b):
    return _dot_nt(a, b)


_mm_nt.defvjp(lambda a, b: (_dot_nt(a, b), (a, b)),
              lambda res, g: (_dot(g, res[1]), _dot_tn(g, res[0])))


@jax.custom_vjp
def _mm_tn(a, b):
    return _dot_tn(a, b)


_mm_tn.defvjp(lambda a, b: (_dot_tn(a, b), (a, b)),
              lambda res, g: (_dot_nt(res[1], g), _dot(res[0], g)))


@jax.custom_vjp
def _cumsum_rows(x):
    return _dot_f32(_tri(x.shape[0], False), x)


_cumsum_rows.defvjp(lambda x: (_cumsum_rows(x), None),
                    lambda _, g: (_dot_f32(_tri(g.shape[0], True), g),))


@functools.partial(jax.custom_vjp, nondiff_argnums=(1,))
def _shift_rows(x, k):
    return pltpu.roll(x, k % x.shape[0], 0)


_shift_rows.defvjp(lambda x, k: (_shift_rows(x, k), None),
                   lambda k, _, g: (pltpu.roll(g, (-k) % g.shape[0], 0),))


def _group_ref(bc, m):
    n = bc.shape[0] // (2 * m)
    b3 = bc.reshape(n, 2 * m, ADH)
    row = lax.broadcasted_iota(jnp.int32, b3.shape, 1)
    ref = jnp.sum(jnp.where(row == m - 1, b3, 0.0), axis=1, keepdims=True)
    return jnp.broadcast_to(ref, b3.shape).reshape(bc.shape)


def _hgrn_block(q, fl, v, g, st, lb, nw):
    n = q.shape[0]
    f = lb + (1.0 - lb) * jax.nn.sigmoid(fl)
    kk = 1.0 - f
    lf = jnp.log(f)
    bc = _cumsum_rows(lf)
    row = lax.broadcasted_iota(jnp.int32, (n, ADH), 0)
    same = jnp.bitwise_xor(lax.broadcasted_iota(jnp.int32, (n, n), 0), lax.broadcasted_iota(jnp.int32, (n, n), 1))
    a = jnp.zeros((n, n), F32)
    m = 1
    while m < n:
        r = jnp.bitwise_and(row, 2 * m - 1)
        up, lo = r >= m, r < m
        if m == 1:
            aq, ak = lf, jnp.zeros_like(lf)
        elif m == 2:
            aq = jnp.where(r == 3, lf + _shift_rows(lf, 1), lf)
            ak = jnp.where(r == 0, _shift_rows(lf, -1), 0.0)
        else:
            ref = _group_ref(bc, m)
            aq, ak = bc - ref, ref - bc
        qt = jnp.where(up, q * jnp.exp(jnp.where(up, aq, 0.0)), 0.0)
        kt = jnp.where(lo, kk * jnp.exp(jnp.where(lo, ak, 0.0)), 0.0)
        a = a + jnp.where(same < 2 * m, _mm_nt(qt, kt), 0.0)
        m *= 2
    last = row == n - 1
    bl = jnp.sum(jnp.where(last, bc, 0.0), axis=0, keepdims=True)
    o = _mm_nn(a, v) + _mm_nt(q * jnp.exp(bc), st) + jnp.sum(q * kk, axis=-1, keepdims=True) * v
    st_new = st * jnp.exp(bl) + _mm_tn(v, kk * jnp.exp(bl - bc))
    rms = lax.rsqrt(jnp.mean(o * o, axis=-1, keepdims=True) + RMS_EPS)
    return o * rms * nw * jax.nn.sigmoid(g), st_new


def _hgrn_fwd(proj, lb_logits, norm_w, nbatch, seq):
    t = proj.shape[0]
    blk = min(HGRN_BLOCK, seq)
    nb = seq // blk

    def body(p_ref, lbl_ref, nw_ref, y_ref, ck_ref, st_s):
        @pl.when(pl.program_id(2) == 0)
        def _():
            st_s[...] = jnp.zeros_like(st_s)

        st = st_s[...]
        ck_ref[0] = st
        lb = jax.nn.sigmoid(lbl_ref[0:1, :] - lbl_ref[1:2, :])
        p = p_ref[...].astype(F32)
        y, st_new = _hgrn_block(p[:, 0:128], p[:, 128:256], p[:, 256:384], p[:, 384:512], st, lb, nw_ref[...])
        st_s[...] = st_new
        y_ref[...] = y.astype(y_ref.dtype)

    return pl.pallas_call(
        body, name="hgrn_fwd", grid=(AH, nbatch, nb),
        in_specs=[pl.BlockSpec((blk, 512), lambda h, b, i: (b * nb + i, COL_A // 512 + h)),
                  pl.BlockSpec((2, 128), lambda h, b, i: (0, h)),
                  pl.BlockSpec((1, 128), lambda h, b, i: (0, h))],
        out_specs=[pl.BlockSpec((blk, 128), lambda h, b, i: (b * nb + i, h)),
                   pl.BlockSpec((1, 128, 128), lambda h, b, i: ((h * nbatch + b) * nb + i, 0, 0))],
        out_shape=[jax.ShapeDtypeStruct((t, AW), BF16), jax.ShapeDtypeStruct((AH * nbatch * nb, 128, 128), F32)],
        scratch_shapes=[pltpu.VMEM((128, 128), F32)],
        compiler_params=_params(("parallel", "parallel", "arbitrary"), 48),
    )(proj, lb_logits, norm_w)


def _hgrn_bwd(proj, dya, ckpt, lb_logits, norm_w, dproj, nbatch, seq):
    t = proj.shape[0]
    blk = min(HGRN_BLOCK, seq)
    nb = seq // blk

    def body(p_ref, dy_ref, ck_ref, lbl_ref, nw_ref, dp_in, dp_ref, sm_ref, dst_s):
        del dp_in
        b_id, i = pl.program_id(1), pl.program_id(2)

        @pl.when(i == 0)
        def _():
            dst_s[...] = jnp.zeros_like(dst_s)

        lb = jax.nn.sigmoid(lbl_ref[0:1, :] - lbl_ref[1:2, :])
        p = p_ref[...].astype(F32)
        _, pullback = jax.vjp(_hgrn_block, p[:, 0:128], p[:, 128:256], p[:, 256:384], p[:, 384:512],
                              ck_ref[0], lb, nw_ref[...])
        dq, dfl, dv, dg, dst, dlb, dnw = pullback((dy_ref[...], dst_s[...]))
        dst_s[...] = dst
        dp_ref[:, 0:128] = dq.astype(dp_ref.dtype)
        dp_ref[:, 128:256] = dfl.astype(dp_ref.dtype)
        dp_ref[:, 256:384] = dv.astype(dp_ref.dtype)
        dp_ref[:, 384:512] = dg.astype(dp_ref.dtype)
        upd = jnp.concatenate([dlb, dnw, jnp.zeros((6, 128), F32)], axis=0)
        first = (b_id == 0) & (i == 0)

        @pl.when(first)
        def _():
            sm_ref[...] = upd

        @pl.when(jnp.logical_not(first))
        def _():
            sm_ref[...] += upd

    def rows(h, b, i):
        return b * nb + (nb - 1 - i)

    return pl.pallas_call(
        body, name="hgrn_bwd", grid=(AH, nbatch, nb),
        in_specs=[pl.BlockSpec((blk, 512), lambda h, b, i: (rows(h, b, i), COL_A // 512 + h)),
                  pl.BlockSpec((blk, 128), lambda h, b, i: (rows(h, b, i), h)),
                  pl.BlockSpec((1, 128, 128), lambda h, b, i: ((h * nbatch + b) * nb + (nb - 1 - i), 0, 0)),
                  pl.BlockSpec((2, 128), lambda h, b, i: (0, h)),
                  pl.BlockSpec((1, 128), lambda h, b, i: (0, h)),
                  pl.BlockSpec(memory_space=pl.ANY)],
        out_specs=[pl.BlockSpec((blk, 512), lambda h, b, i: (rows(h, b, i), COL_A // 512 + h)),
                   pl.BlockSpec((8, 128), lambda h, b, i: (0, h))],
        out_shape=[jax.ShapeDtypeStruct((t, NP), BF16), jax.ShapeDtypeStruct((8, AW), F32)],
        input_output_aliases={5: 0},
        scratch_shapes=[pltpu.VMEM((128, 128), F32)],
        compiler_params=_params(("parallel", "arbitrary", "arbitrary"), 48),
    )(proj, dya, ckpt, lb_logits, norm_w, dproj)


def _log_sigmoid(z):
    return jnp.minimum(z, 0.0) - jnp.log(1.0 + jnp.exp(-jnp.abs(z)))


def _fox_cum(proj, bias128, nbatch, seq):
    t = proj.shape[0]
    ts = min(512, seq)
    nb = seq // ts

    def body(p_ref, b_ref, c_ref, carry):
        @pl.when(pl.program_id(1) == 0)
        def _():
            carry[...] = jnp.zeros_like(carry)
        cum = _dot_f32(_tri(ts, False), _log_sigmoid(p_ref[...] + b_ref[...])) + carry[...]
        carry[...] = cum[ts - 1:ts, :]
        cum2 = cum * LOG2E
        lane = lax.broadcasted_iota(jnp.int32, (ts, 128), 1)
        for p in range(4):
            c_ref[p] = jnp.where(lane < 64, cum2[:, 2 * p:2 * p + 1], cum2[:, 2 * p + 1:2 * p + 2])

    return pl.pallas_call(
        body, name="fox_cum", grid=(nbatch, nb),
        in_specs=[pl.BlockSpec((ts, 128), lambda b, i: (b * nb + i, 0)),
                  pl.BlockSpec((1, 128), lambda b, i: (0, 0))],
        out_specs=pl.BlockSpec((4, ts, 128), lambda b, i: (0, b * nb + i, 0)),
        out_shape=jax.ShapeDtypeStruct((4, t, 128), F32),
        scratch_shapes=[pltpu.VMEM((1, 128), F32)],
        compiler_params=_params(("parallel", "arbitrary")),
    )(proj, bias128)


def _fox_scores_t(q_ref, kv_ref, cc_ref, hh, masked, tq, tk):
    kh = kv_ref[:, 0:128].astype(BF16)
    qh = _head_lanes((q_ref[...] * (LOG2E * BDH ** -0.5)).astype(BF16), hh)
    s = _dot_nt(kh, qh) - cc_ref[0, :, 64 * hh:64 * hh + 1]
    if masked:
        key = lax.broadcasted_iota(jnp.int32, (tk, tq), 0)
        qry = lax.broadcasted_iota(jnp.int32, (tk, tq), 1)
        s = jnp.where(key <= qry, s, NEG)
    return s, kh


def _causal_pairs(nq, key_major):
    if key_major:
        pairs = [(i, j) for j in range(nq) for i in range(j, nq)]
    else:
        pairs = [(i, j) for i in range(nq) for j in range(i + 1)]
    return (jnp.asarray([p[0] for p in pairs], jnp.int32), jnp.asarray([p[1] for p in pairs], jnp.int32))


def _head_lanes(x128, hh):
    lane = lax.broadcasted_iota(jnp.int32, x128.shape, 1)
    return jnp.where((lane < 64) if hh == 0 else (lane >= 64), x128, jnp.zeros_like(x128))


def _with_ones_lane(x128, hh):
    lane = lax.broadcasted_iota(jnp.int32, x128.shape, 1)
    one = jnp.ones_like(x128)
    zero = jnp.zeros_like(x128)
    if hh == 0:
        return jnp.where(lane < 64, x128, jnp.where(lane == 64, one, zero))
    return jnp.where(lane >= 64, x128, jnp.where(lane == 0, one, zero))


def _fox_fwd(proj, cum_cols, nbatch, seq):
    t = proj.shape[0]
    tq = tk = min(512, seq)
    nq = seq // tq
    qi, kj = _causal_pairs(nq, key_major=False)

    def body(qi_ref, kj_ref, q_ref, kv_ref, cc_ref, o_ref, lse_ref, m_s, acc_s):
        s_id = pl.program_id(2)
        i, j = qi_ref[s_id], kj_ref[s_id]

        @pl.when(j == 0)
        def _():
            m_s[...] = jnp.full_like(m_s, NEG)
            acc_s[...] = jnp.zeros_like(acc_s)

        def step(masked):
            m_prev = m_s[0:2, :]
            acc_prev = [acc_s[0], acc_s[1]]
            v128 = kv_ref[:, 128:256].astype(BF16)
            s = [_fox_scores_t(q_ref, kv_ref, cc_ref, hh, masked, tq, tk)[0] for hh in range(2)]
            m_new = [jnp.maximum(m_prev[hh:hh + 1, :], jnp.max(s[hh], axis=0, keepdims=True)) for hh in range(2)]
            acc_new = []
            for hh in range(2):
                alpha = jnp.exp2(m_prev[hh:hh + 1, :] - m_new[hh])
                p = jnp.exp2(s[hh] - m_new[hh]).astype(BF16)
                acc_new.append(acc_prev[hh] * alpha + _dot_tn(_with_ones_lane(v128, hh), p))
            acc_s[0] = acc_new[0]
            acc_s[1] = acc_new[1]
            m_s[0:2, :] = jnp.concatenate(m_new, axis=0)

        @pl.when(j < i)
        def _():
            step(False)

        @pl.when(j == i)
        def _():
            step(True)
            a0, a1 = acc_s[0], acc_s[1]
            l0, l1 = a0[64:65, :], a1[0:1, :]
            o_t = jnp.concatenate([a0[0:64, :] / l0, a1[64:128, :] / l1], axis=0)
            o_ref[...] = o_t.T.astype(o_ref.dtype)
            lse_ref[0, 0] = jnp.concatenate(
                [m_s[0:1, :] + jnp.log2(l0), m_s[1:2, :] + jnp.log2(l1), jnp.zeros((6, tq), F32)], axis=0)

    return pl.pallas_call(
        body, name="fox_fwd",
        grid_spec=pltpu.PrefetchScalarGridSpec(
            num_scalar_prefetch=2, grid=(nbatch, 4, qi.shape[0]),
            in_specs=[pl.BlockSpec((tq, 128), lambda b, p, s, qi, kj: (b * nq + qi[s], COL_BQ // 128 + p)),
                      pl.BlockSpec((tk, 256), lambda b, p, s, qi, kj: (b * nq + kj[s], COL_KV // 256 + p)),
                      pl.BlockSpec((1, tk, 128), lambda b, p, s, qi, kj: (p, b * nq + kj[s], 0))],
            out_specs=[pl.BlockSpec((tq, 128), lambda b, p, s, qi, kj: (b * nq + qi[s], p)),
                       pl.BlockSpec((1, 1, 8, tq), lambda b, p, s, qi, kj: (b, p, 0, qi[s]))],
            scratch_shapes=[pltpu.VMEM((8, tq), F32), pltpu.VMEM((2, 128, tq), F32)]),
        out_shape=[jax.ShapeDtypeStruct((t, 512), BF16), jax.ShapeDtypeStruct((nbatch, 4, 8, seq), F32)],
        compiler_params=_params(("parallel", "parallel", "arbitrary"), 48),
    )(qi, kj, proj, proj, cum_cols)


def _fox_bwd(proj, cum_cols, lse, yb, dyb, dproj, nbatch, seq):
    t = proj.shape[0]
    tq = tk = min(512, seq)
    nq = seq // tq
    scale = BDH ** -0.5
    qi, kj = _causal_pairs(nq, key_major=True)
    nsteps = qi.shape[0]

    def body(qi_ref, kj_ref, q_ref, kv_ref, cc_ref, lse_ref, o_ref, do_ref, dp_in,
             dkv_ref, dq_ref, drs_ref, dcs_ref, dk_s, dv_s, dqa_s):
        del dp_in
        hp, s_id = pl.program_id(1), pl.program_id(2)
        i, j = qi_ref[s_id], kj_ref[s_id]

        @pl.when(i == j)
        def _():
            dk_s[...] = jnp.zeros_like(dk_s)
            dv_s[...] = jnp.zeros_like(dv_s)

        @pl.when(s_id == 0)
        def _():
            dqa_s[...] = jnp.zeros_like(dqa_s)

        def step(masked):
            dk_prev, dq_prev, dv_prev = [dk_s[0], dk_s[1]], [dqa_s[i, 0], dqa_s[i, 1]], dv_s[...]
            lse2 = lse_ref[0, 0, 0:2, :]
            qs128 = (q_ref[...] * scale).astype(BF16)
            k128 = kv_ref[:, 0:128].astype(BF16)
            v128 = kv_ref[:, 128:256].astype(BF16)
            do128 = do_ref[...]
            doo = do128 * o_ref[...].astype(F32)
            do16 = do128.astype(BF16)
            dk_new, dq_new, dv_new = [], [], dv_prev
            for hh in range(2):
                s, _ = _fox_scores_t(q_ref, kv_ref, cc_ref, hh, masked, tq, tk)
                p = jnp.exp2(s - lse2[hh:hh + 1, :])
                dd = lax.dot_general(jnp.ones((8, 128), F32), _head_lanes(doo, hh),
                                     (((1,), (1,)), ((), ())), preferred_element_type=F32, precision=HIGHEST)[0:1, :]
                doh = _head_lanes(do16, hh)
                dp = _dot_nt(v128, doh)
                ds = (p * (dp - dd)).astype(BF16)
                dv_new = dv_new + _dot(p, doh)
                dk_new.append(dk_prev[hh] + _dot(ds, _with_ones_lane(qs128, hh)))
                dq_new.append(dq_prev[hh] + _dot_tn(_with_ones_lane(k128, hh), ds))
            dv_s[...] = dv_new
            for hh in range(2):
                dk_s[hh] = dk_new[hh]
                dqa_s[i, hh] = dq_new[hh]

        @pl.when(i == j)
        def _():
            step(True)

        @pl.when(i > j)
        def _():
            step(False)

        @pl.when(i == nq - 1)
        def _():
            lane = lax.broadcasted_iota(jnp.int32, (tk, 128), 1)
            k0, k1 = dk_s[0], dk_s[1]
            dkv_ref[:, 0:128] = jnp.where(lane < 64, k0, k1).astype(dkv_ref.dtype)
            dkv_ref[:, 128:256] = dv_s[...].astype(dkv_ref.dtype)
            dcs_ref[0] = jnp.where(lane == 2 * hp, k0[:, 64:65], jnp.where(lane == 2 * hp + 1, k1[:, 0:1], 0.0))

        @pl.when(s_id == nsteps - 1)
        def _():
            lane = lax.broadcasted_iota(jnp.int32, (tq, 128), 1)
            for blk in range(nq):
                a0 = dqa_s[blk, 0].T
                a1 = dqa_s[blk, 1].T
                rows = pl.ds(blk * tq, tq)
                dq_ref[rows, :] = (jnp.where(lane < 64, a0, a1) * scale).astype(dq_ref.dtype)
                drs_ref[0, rows, :] = jnp.where(lane == 2 * hp, a0[:, 64:65], jnp.where(lane == 2 * hp + 1, a1[:, 0:1], 0.0))

    return pl.pallas_call(
        body, name="fox_bwd",
        grid_spec=pltpu.PrefetchScalarGridSpec(
            num_scalar_prefetch=2, grid=(nbatch, 4, nsteps),
            in_specs=[pl.BlockSpec((tq, 128), lambda b, p, s, qi, kj: (b * nq + qi[s], COL_BQ // 128 + p)),
                      pl.BlockSpec((tk, 256), lambda b, p, s, qi, kj: (b * nq + kj[s], COL_KV // 256 + p)),
                      pl.BlockSpec((1, tk, 128), lambda b, p, s, qi, kj: (p, b * nq + kj[s], 0)),
                      pl.BlockSpec((1, 1, 8, tq), lambda b, p, s, qi, kj: (b, p, 0, qi[s])),
                      pl.BlockSpec((tq, 128), lambda b, p, s, qi, kj: (b * nq + qi[s], p)),
                      pl.BlockSpec((tq, 128), lambda b, p, s, qi, kj: (b * nq + qi[s], p)),
                      pl.BlockSpec(memory_space=pl.ANY)],
            out_specs=[pl.BlockSpec((tk, 256), lambda b, p, s, qi, kj: (b * nq + kj[s], COL_KV // 256 + p)),
                       pl.BlockSpec((seq, 128), lambda b, p, s, qi, kj: (b, p)),
                       pl.BlockSpec((1, seq, 128), lambda b, p, s, qi, kj: (p, b, 0)),
                       pl.BlockSpec((1, tk, 128), lambda b, p, s, qi, kj: (p, b * nq + kj[s], 0))],
            scratch_shapes=[pltpu.VMEM((2, tk, 128), F32), pltpu.VMEM((tk, 128), F32),
                            pltpu.VMEM((nq, 2, 128, tq), F32)]),
        out_shape=[jax.ShapeDtypeStruct((t, NP), BF16), jax.ShapeDtypeStruct((t, 512), BF16),
                   jax.ShapeDtypeStruct((4, t, 128), F32), jax.ShapeDtypeStruct((4, t, 128), F32)],
        input_output_aliases={8: 0},
        compiler_params=_params(("parallel", "parallel", "arbitrary"), 56),
    )(qi, kj, proj, proj, cum_cols, lse, yb, dyb, dproj)


def _place_cols(dproj, src, col):
    t, w = src.shape
    tm = 1024 if t % 1024 == 0 else t

    def body(s_ref, dp_in, o_ref):
        del dp_in
        o_ref[...] = s_ref[...]

    return pl.pallas_call(
        body, name="place_cols", grid=(t // tm,),
        in_specs=[pl.BlockSpec((tm, w), lambda i: (i, 0)), pl.BlockSpec(memory_space=pl.ANY)],
        out_specs=pl.BlockSpec((tm, w), lambda i: (i, col // w)),
        out_shape=jax.ShapeDtypeStruct(dproj.shape, dproj.dtype),
        input_output_aliases={1: 0},
        compiler_params=_params(("parallel",)),
    )(src, dproj)


def _fox_dbf(proj, bias128, drs, dcs, dproj, nbatch, seq):
    t = proj.shape[0]
    ts = min(512, seq)
    nb = seq // ts

    def body(p_ref, b_ref, dr_ref, dc_ref, dp_in, dp_ref, sm_ref, carry):
        del dp_in
        b_id, i = pl.program_id(0), pl.program_id(1)

        @pl.when(i == 0)
        def _():
            carry[...] = jnp.zeros_like(carry)

        dcum = (dr_ref[0] - dc_ref[0]) + (dr_ref[1] - dc_ref[1]) + (dr_ref[2] - dc_ref[2]) + (dr_ref[3] - dc_ref[3])
        rc = _dot_f32(_tri(ts, True), dcum) + carry[...]
        carry[...] = rc[0:1, :]
        z = p_ref[...] + b_ref[...]
        lane = lax.broadcasted_iota(jnp.int32, (ts, 128), 1)
        dz = jnp.where(lane < BH, rc * jax.nn.sigmoid(-z), 0.0)
        dp_ref[...] = dz.astype(dp_ref.dtype)
        upd = jnp.concatenate([jnp.sum(dz, axis=0, keepdims=True), jnp.zeros((7, 128), F32)], axis=0)
        first = (b_id == 0) & (i == 0)

        @pl.when(first)
        def _():
            sm_ref[...] = upd

        @pl.when(jnp.logical_not(first))
        def _():
            sm_ref[...] += upd

    def rows(b, i):
        return b * nb + (nb - 1 - i)

    return pl.pallas_call(
        body, name="fox_dbf", grid=(nbatch, nb),
        in_specs=[pl.BlockSpec((ts, 128), lambda b, i: (rows(b, i), 0)),
                  pl.BlockSpec((1, 128), lambda b, i: (0, 0)),
                  pl.BlockSpec((4, ts, 128), lambda b, i: (0, rows(b, i), 0)),
                  pl.BlockSpec((4, ts, 128), lambda b, i: (0, rows(b, i), 0)),
                  pl.BlockSpec(memory_space=pl.ANY)],
        out_specs=[pl.BlockSpec((ts, 128), lambda b, i: (rows(b, i), COL_BF // 128)),
                   pl.BlockSpec((8, 128), lambda b, i: (0, 0))],
        out_shape=[jax.ShapeDtypeStruct((t, NP), BF16), jax.ShapeDtypeStruct((8, 128), F32)],
        input_output_aliases={4: 0},
        scratch_shapes=[pltpu.VMEM((1, 128), F32)],
        compiler_params=_params(("arbitrary", "arbitrary")),
    )(proj, bias128, drs, dcs, dproj)


def _ln_stats(z):
    mu = jnp.mean(z, axis=-1, keepdims=True)
    zc = z - mu
    rstd = lax.rsqrt(jnp.mean(zc * zc, axis=-1, keepdims=True) + LN_EPS)
    return zc * rstd, rstd


def _ln_bwd(dy, xhat, rstd, w):
    dxh = dy * w
    return rstd * (dxh - jnp.mean(dxh, axis=-1, keepdims=True) - xhat * jnp.mean(dxh * xhat, axis=-1, keepdims=True))


def _merge_fwd(ya, yb, proj, x2, mod8, wba, wbb, wout, ln1w, ln1b, seq):
    t = x2.shape[0]
    tm = min(512, seq)
    tpb = seq // tm

    def body(ya_ref, yb_ref, g_ref, x_ref, mod_ref, wa_ref, wb_ref, wo_ref, lw_ref, lb_ref, mg_ref, u_ref, x1_ref):
        ga = jax.nn.sigmoid(g_ref[:, 0:D].astype(F32))
        gb = jax.nn.sigmoid(g_ref[:, D:2 * D].astype(F32))
        merged = (ga * jnp.dot(ya_ref[...], wa_ref[...], preferred_element_type=F32)
                  + gb * jnp.dot(yb_ref[...], wb_ref[...], preferred_element_type=F32))
        mg = merged.astype(BF16)
        mg_ref[...] = mg
        u = jnp.dot(mg, wo_ref[...], preferred_element_type=F32)
        u_ref[...] = u
        xhat, _ = _ln_stats(ALPHA * x_ref[...] + (1.0 + mod_ref[0, 2:3, :]) * u)
        x1_ref[...] = xhat * lw_ref[...] + lb_ref[...]

    tok = lambda w: pl.BlockSpec((tm, w), lambda i: (i, 0))
    full = lambda a: pl.BlockSpec(a.shape, lambda i: (0,) * a.ndim)
    return pl.pallas_call(
        body, name="merge_fwd", grid=(t // tm,),
        in_specs=[tok(512), tok(512), pl.BlockSpec((tm, 2048), lambda i: (i, COL_GATES // 2048)), tok(D),
                  pl.BlockSpec((1, 8, D), lambda i: (i // tpb, 0, 0)),
                  full(wba), full(wbb), full(wout), full(ln1w), full(ln1b)],
        out_specs=[tok(D), tok(D), tok(D)],
        out_shape=[jax.ShapeDtypeStruct((t, D), BF16), jax.ShapeDtypeStruct((t, D), F32),
                   jax.ShapeDtypeStruct((t, D), F32)],
        compiler_params=_params(("parallel",), 48),
    )(ya, yb, proj, x2, mod8, wba, wbb, wout, ln1w, ln1b)


def _merge_bwd(du, ya, yb, proj, wba, wbb, wout, token, seq):
    t = du.shape[0]
    tm = min(512, seq)

    def body(du_ref, ya_ref, yb_ref, g_ref, wa_ref, wb_ref, wo_ref, token_ref,
             dp_ref, dpa_ref, dpb_ref, dya_ref, dyb_ref):
        del token_ref
        ga = jax.nn.sigmoid(g_ref[:, 0:D].astype(F32))
        gb = jax.nn.sigmoid(g_ref[:, D:2 * D].astype(F32))
        dm = _dot_nt(du_ref[...], wo_ref[...])
        pa = jnp.dot(ya_ref[...], wa_ref[...], preferred_element_type=F32)
        pb = jnp.dot(yb_ref[...], wb_ref[...], preferred_element_type=F32)
        dpa = (dm * ga).astype(BF16)
        dpb = (dm * gb).astype(BF16)
        dpa_ref[...] = dpa
        dpb_ref[...] = dpb
        dp_ref[:, 0:D] = (dm * pa * ga * (1.0 - ga)).astype(BF16)
        dp_ref[:, D:2 * D] = (dm * pb * gb * (1.0 - gb)).astype(BF16)
        dya_ref[...] = _dot_nt(dpa, wa_ref[...])
        dyb_ref[...] = _dot_nt(dpb, wb_ref[...])

    tok = lambda w: pl.BlockSpec((tm, w), lambda i: (i, 0))
    full = lambda a: pl.BlockSpec(a.shape, lambda i: (0,) * a.ndim)
    return pl.pallas_call(
        body, name="merge_bwd", grid=(t // tm,),
        in_specs=[tok(D), tok(512), tok(512), pl.BlockSpec((tm, 2048), lambda i: (i, COL_GATES // 2048)),
                  full(wba), full(wbb), full(wout), full(token)],
        out_specs=[pl.BlockSpec((tm, 2048), lambda i: (i, COL_GATES // 2048)), tok(D), tok(D), tok(512), tok(512)],
        out_shape=[jax.ShapeDtypeStruct((t, NP), BF16), jax.ShapeDtypeStruct((t, D), BF16),
                   jax.ShapeDtypeStruct((t, D), BF16), jax.ShapeDtypeStruct((t, 512), F32),
                   jax.ShapeDtypeStruct((t, 512), F32)],
        compiler_params=_params(("parallel",), 48),
    )(du, ya, yb, proj, wba, wbb, wout, token)


def _ffn_fwd(x1, mod8, wg, wu, wd, target, ln2w, ln2b, seq):
    t = x1.shape[0]
    tm = min(FFN_TOKENS, seq)
    nf, _, tf = wg.shape
    tpb = seq // tm
    nbatch = t // seq

    def body(x_ref, mod_ref, wg_ref, wu_ref, wd_ref, t_ref, lw_ref, lb_ref,
             a_ref, b_ref, h_s, dz_ref, st_ref, dm_ref, acc):
        i, j = pl.program_id(0), pl.program_id(1)

        @pl.when(j == 0)
        def _():
            h_s[...] = (x_ref[...] * (1.0 + mod_ref[0, 4:5, :]) + mod_ref[0, 3:4, :]).astype(BF16)
            acc[...] = jnp.zeros_like(acc)

        a = jnp.dot(h_s[...], wg_ref[0], preferred_element_type=F32)
        b = jnp.dot(h_s[...], wu_ref[0], preferred_element_type=F32)
        a_ref[0] = a.astype(BF16)
        b_ref[0] = b.astype(BF16)
        acc[...] += _dot(a * jax.nn.sigmoid(a) * b, wd_ref[0])

        @pl.when(j == nf - 1)
        def _():
            ffn = acc[...]
            xhat, rstd = _ln_stats(ALPHA * x_ref[...] + (1.0 + mod_ref[0, 5:6, :]) * ffn)
            diff = xhat * lw_ref[...] + lb_ref[...] - t_ref[...]
            loss = 0.5 * jnp.sum(jnp.sum(diff * diff, axis=-1, keepdims=True), axis=0, keepdims=True) / D
            dy = diff * (1.0 / D)
            dz = _ln_bwd(dy, xhat, rstd, lw_ref[...])
            dz_ref[...] = dz
            lane = lax.broadcasted_iota(jnp.int32, (1, D), 1)
            upd = jnp.concatenate(
                [jnp.sum(dy * xhat, axis=0, keepdims=True), jnp.sum(dy, axis=0, keepdims=True),
                 jnp.where(lane == 0, loss, 0.0), jnp.zeros((5, D), F32)], axis=0)
            dmu = jnp.concatenate(
                [jnp.zeros((5, D), F32), jnp.sum(dz * ffn, axis=0, keepdims=True), jnp.zeros((2, D), F32)], axis=0)

            @pl.when(i == 0)
            def _():
                st_ref[...] = upd

            @pl.when(i > 0)
            def _():
                st_ref[...] += upd

            @pl.when(i % tpb == 0)
            def _():
                dm_ref[0] = dmu

            @pl.when(i % tpb != 0)
            def _():
                dm_ref[0] += dmu

    row = lambda: pl.BlockSpec((tm, D), lambda i, j: (i, 0))
    vec = lambda: pl.BlockSpec((1, D), lambda i, j: (0, 0))
    return pl.pallas_call(
        body, name="ffn_fwd", grid=(t // tm, nf),
        in_specs=[row(), pl.BlockSpec((1, 8, D), lambda i, j: (i // tpb, 0, 0)),
                  pl.BlockSpec((1, D, tf), lambda i, j: (j, 0, 0)), pl.BlockSpec((1, D, tf), lambda i, j: (j, 0, 0)),
                  pl.BlockSpec((1, tf, D), lambda i, j: (j, 0, 0)), row(), vec(), vec()],
        out_specs=[pl.BlockSpec((1, tm, tf), lambda i, j: (j, i, 0)), pl.BlockSpec((1, tm, tf), lambda i, j: (j, i, 0)),
                   row(), row(), pl.BlockSpec((8, D), lambda i, j: (0, 0)),
                   pl.BlockSpec((1, 8, D), lambda i, j: (i // tpb, 0, 0))],
        out_shape=[jax.ShapeDtypeStruct((nf, t, tf), BF16), jax.ShapeDtypeStruct((nf, t, tf), BF16),
                   jax.ShapeDtypeStruct((t, D), BF16),
                   jax.ShapeDtypeStruct((t, D), F32), jax.ShapeDtypeStruct((8, D), F32),
                   jax.ShapeDtypeStruct((nbatch, 8, D), F32)],
        scratch_shapes=[pltpu.VMEM((tm, D), F32)],
        compiler_params=_params(("arbitrary", "arbitrary"), 60),
    )(x1, mod8, wg, wu, wd, target, ln2w, ln2b)


def _ffn_bwd(dz2, a, b, wg, wu, wd, x1, x2, u, mod8, ln1w, seq):
    t = x1.shape[0]
    tm = min(512, seq)
    nf, tf, _ = wg.shape
    tpb = seq // tm
    nbatch = t // seq

    def body(dz_ref, a_ref, b_ref, wg_ref, wu_ref, wd_ref, x1_ref, x_ref, u_ref, mod_ref, lw_ref,
             da_ref, db_ref, hm_ref, df_ref, du_ref, dxp_ref, st_ref, dm_ref, acc):
        i, j = pl.program_id(0), pl.program_id(1)

        @pl.when(j == 0)
        def _():
            df_ref[...] = ((1.0 + mod_ref[0, 5:6, :]) * dz_ref[...]).astype(BF16)
            acc[...] = jnp.zeros_like(acc)

        dhm = _dot(df_ref[...], wd_ref[0])
        av = a_ref[0].astype(F32)
        bv = b_ref[0].astype(F32)
        sg = jax.nn.sigmoid(av)
        sl = av * sg
        hm_ref[0] = (sl * bv).astype(BF16)
        da = (dhm * bv * (sg * (1.0 + av * (1.0 - sg)))).astype(BF16)
        db = (dhm * sl).astype(BF16)
        da_ref[0] = da
        db_ref[0] = db
        acc[...] += _dot(da, wg_ref[0]) + _dot(db, wu_ref[0])

        @pl.when(j == nf - 1)
        def _():
            dh2 = acc[...]
            x1v = x1_ref[...]
            uv = u_ref[...]
            dx1 = ALPHA * dz_ref[...] + dh2 * (1.0 + mod_ref[0, 4:5, :])
            xhat, rstd = _ln_stats(ALPHA * x_ref[...] + (1.0 + mod_ref[0, 2:3, :]) * uv)
            dz1 = _ln_bwd(dx1, xhat, rstd, lw_ref[...])
            du_ref[...] = ((1.0 + mod_ref[0, 2:3, :]) * dz1).astype(BF16)
            dxp_ref[...] = ALPHA * dz1
            upd = jnp.concatenate(
                [jnp.sum(dx1 * xhat, axis=0, keepdims=True), jnp.sum(dx1, axis=0, keepdims=True),
                 jnp.zeros((6, D), F32)], axis=0)
            dmu = jnp.concatenate(
                [jnp.zeros((2, D), F32), jnp.sum(dz1 * uv, axis=0, keepdims=True),
                 jnp.sum(dh2, axis=0, keepdims=True), jnp.sum(dh2 * x1v, axis=0, keepdims=True),
                 jnp.zeros((3, D), F32)], axis=0)

            @pl.when(i == 0)
            def _():
                st_ref[...] = upd

            @pl.when(i > 0)
            def _():
                st_ref[...] += upd

            @pl.when(i % tpb == 0)
            def _():
                dm_ref[0] = dmu

            @pl.when(i % tpb != 0)
            def _():
                dm_ref[0] += dmu

    row = lambda: pl.BlockSpec((tm, D), lambda i, j: (i, 0))
    ffb = lambda: pl.BlockSpec((1, tm, tf), lambda i, j: (j, i, 0))
    return pl.pallas_call(
        body, name="ffn_bwd", grid=(t // tm, nf),
        in_specs=[row(), ffb(), ffb(),
                  pl.BlockSpec((1, tf, D), lambda i, j: (j, 0, 0)), pl.BlockSpec((1, tf, D), lambda i, j: (j, 0, 0)),
                  pl.BlockSpec((1, D, tf), lambda i, j: (j, 0, 0)), row(), row(), row(),
                  pl.BlockSpec((1, 8, D), lambda i, j: (i // tpb, 0, 0)), pl.BlockSpec((1, D), lambda i, j: (0, 0))],
        out_specs=[ffb(), ffb(), ffb(), row(), row(), row(), pl.BlockSpec((8, D), lambda i, j: (0, 0)),
                   pl.BlockSpec((1, 8, D), lambda i, j: (i // tpb, 0, 0))],
        out_shape=[jax.ShapeDtypeStruct((nf, t, tf), BF16), jax.ShapeDtypeStruct((nf, t, tf), BF16),
                   jax.ShapeDtypeStruct((nf, t, tf), BF16), jax.ShapeDtypeStruct((t, D), BF16),
                   jax.ShapeDtypeStruct((t, D), BF16), jax.ShapeDtypeStruct((t, D), F32),
                   jax.ShapeDtypeStruct((8, D), F32), jax.ShapeDtypeStruct((nbatch, 8, D), F32)],
        scratch_shapes=[pltpu.VMEM((tm, D), F32)],
        compiler_params=_params(("arbitrary", "arbitrary"), 48),
    )(dz2, a, b, wg, wu, wd, x1, x2, u, mod8, ln1w)


def _adamw_math(w, g, m, v):
    m = B1 * m + (1.0 - B1) * g
    v = B2 * v + (1.0 - B2) * (g * g)
    m_hat = m / (1.0 - B1 ** STEP)
    v_hat = v / (1.0 - B2 ** STEP)
    return -LR * (m_hat / (jnp.sqrt(v_hat) + EPS) + WD * w), m, v


def _adamw(w, g, m, v, name):
    rows, cols = w.shape
    tr = rows
    for cand in (128, 64, 32, 16, 8):
        if rows % cand == 0:
            tr = cand
            break

    def body(w_ref, g_ref, m_ref, v_ref, d_ref, mo_ref, vo_ref):
        d, mn, vn = _adamw_math(w_ref[...], g_ref[...], m_ref[...], v_ref[...])
        d_ref[...] = d
        mo_ref[...] = mn
        vo_ref[...] = vn

    spec = pl.BlockSpec((tr, cols), lambda i: (i, 0))
    return pl.pallas_call(
        body, name=name, grid=(rows // tr,), in_specs=[spec] * 4, out_specs=[spec] * 3,
        out_shape=[jax.ShapeDtypeStruct((rows, cols), F32)] * 3,
        compiler_params=_params(("parallel",), 48),
    )(w, g, m, v)


def _adamw_halves(w, g_mine, g_sib, m, v, c_idx, name):
    rows, cols = w.shape
    hr = rows // 2
    tr = next(cand for cand in (128, 88, 64, 32, 16, 8) if hr % cand == 0)
    tph = hr // tr

    def body(c_ref, w_ref, gm_ref, gs_ref, m_ref, v_ref, g_ref, d_ref, mo_ref, vo_ref):
        g = jnp.where(pl.program_id(0) == c_ref[0], gm_ref[...], gs_ref[...])
        d, mn, vn = _adamw_math(w_ref[...], g, m_ref[...], v_ref[...])
        g_ref[...] = g
        d_ref[...] = d
        mo_ref[...] = mn
        vo_ref[...] = vn

    full = pl.BlockSpec((tr, cols), lambda h, i, c: (h * tph + i, 0))
    half = pl.BlockSpec((tr, cols), lambda h, i, c: (i, 0))
    return pl.pallas_call(
        body, name=name,
        grid_spec=pltpu.PrefetchScalarGridSpec(
            num_scalar_prefetch=1, grid=(2, tph), in_specs=[full, half, half, full, full], out_specs=[full] * 4),
        out_shape=[jax.ShapeDtypeStruct((rows, cols), F32)] * 4,
        compiler_params=_params(("parallel", "parallel"), 48),
    )(c_idx, w, g_mine, g_sib, m, v)


def _grad_w_ada(c_all, dmod_cols):
    def body(c_ref, d_ref, o_ref):
        c = c_ref[...]
        o_ref[...] = lax.dot_general(c * jax.nn.sigmoid(c), d_ref[...], (((0,), (0,)), ((), ())),
                                     preferred_element_type=F32, precision=HIGHEST)

    return pl.pallas_call(
        body, name="grad_w_ada", out_shape=jax.ShapeDtypeStruct((D, dmod_cols.shape[1]), F32),
        compiler_params=_params(vmem_mb=48),
    )(c_all, dmod_cols)


def _small_update(gath, w8, m8, v8):
    def body(g_ref, w_ref, m_ref, v_ref, go_ref, d_ref, mo_ref, vo_ref):
        g0 = g_ref[0, 0:1, :] + g_ref[0, 1:2, :]
        g1 = g_ref[0, 2:3, :]
        for dev in range(1, N_DEV):
            g0 = g0 + (g_ref[dev, 0:1, :] + g_ref[dev, 1:2, :])
            g1 = g1 + g_ref[dev, 2:3, :]
        w = w_ref[...]
        lb = jax.nn.sigmoid(w[1:2, O_LB0:O_LB1] - w[1:2, O_LB1:O_FOX])
        fac = lb * (1.0 - lb)
        g1 = jnp.concatenate([g1[:, :O_LB0], g1[:, O_LB0:O_LB1] * fac, -g1[:, O_LB1:O_FOX] * fac, g1[:, O_FOX:]],
                             axis=1)
        g = jnp.concatenate([g0, g1, jnp.zeros((6, SMALL_W), F32)], axis=0)
        d, mn, vn = _adamw_math(w, g, m_ref[...], v_ref[...])
        go_ref[...] = g
        d_ref[...] = d
        mo_ref[...] = mn
        vo_ref[...] = vn

    return pl.pallas_call(
        body, name="small_update", out_shape=[jax.ShapeDtypeStruct((8, SMALL_W), F32)] * 4,
        compiler_params=_params(vmem_mb=48),
    )(gath, w8, m8, v8)


def _pack_small(b_ada, ln1w, ln1b, ln2w, ln2b, norm_w, lb_logits, fox):
    row1 = jnp.concatenate([ln1w, ln1b, ln2w, ln2b, norm_w, lb_logits[0:1], lb_logits[1:2], fox,
                            jnp.zeros((1, SMALL_W - O_FOX - BH), F32)], axis=1)
    return jnp.concatenate([b_ada, row1, jnp.zeros((6, SMALL_W), F32)], axis=0)


def _unpack_small(p):
    r = p[1:2]
    lb = jnp.concatenate([r[:, O_LB0:O_LB1], r[:, O_LB1:O_FOX]], axis=0)
    return dict(b_ada=p[0:1], ln1_w=r[:, O_LN1W:O_LN1B], ln1_b=r[:, O_LN1B:O_LN2W], ln2_w=r[:, O_LN2W:O_LN2B],
                ln2_b=r[:, O_LN2B:O_NORM], hgrn_norm_w=r[:, O_NORM:O_LB0], lb_logits=lb,
                fox_f_bias=r[:, O_FOX:O_FOX + BH])


_BIG = ("w_in", "w_branch_a", "w_branch_b", "w_out", "w_ffn_gate", "w_ffn_up", "w_ffn_down")
_TRANSPOSED = ("w_ffn_gate", "w_ffn_up")


def _cols_of_chips(stacked):
    return jnp.concatenate([stacked[k] for k in range(N_CHIPS)], axis=1)


def kernel(x, c, w_ada, b_ada, w_in, fox_f_bias, lb_logits, hgrn_norm_w, w_branch_a, w_branch_b, w_out, ln1_w, ln1_b, w_ffn_gate, w_ffn_up, w_ffn_down, ln2_w, ln2_b, loss_target, m_w_ada, m_b_ada, m_w_in, m_fox_f_bias, m_lb_logits, m_hgrn_norm_w, m_w_branch_a, m_w_branch_b, m_w_out, m_ln1_w, m_ln1_b, m_w_ffn_gate, m_w_ffn_up, m_w_ffn_down, m_ln2_w, m_ln2_b, v_w_ada, v_b_ada, v_w_in, v_fox_f_bias, v_lb_logits, v_hgrn_norm_w, v_w_branch_a, v_w_branch_b, v_w_out, v_ln1_w, v_ln1_b, v_w_ffn_gate, v_w_ffn_up, v_w_ffn_down, v_ln2_w, v_ln2_b):
    nbatch, seq, _ = x.shape
    t = nbatch * seq
    ax, ay, ac = lax.axis_index("x"), lax.axis_index("y"), lax.axis_index("c")
    chip = 2 * ax + ay
    dev = 2 * chip + ac
    chip_arr = jnp.reshape(chip, (1,)).astype(jnp.int32)
    core_arr = jnp.reshape(ac, (1,)).astype(jnp.int32)

    tr = lambda a: jnp.swapaxes(a[0], 0, 1)
    shard_w = dict(w_in=w_in[0], w_branch_a=w_branch_a[0], w_branch_b=w_branch_b[0], w_out=w_out[0],
                   w_ffn_gate=tr(w_ffn_gate), w_ffn_up=tr(w_ffn_up), w_ffn_down=w_ffn_down[0])
    shard_m = dict(w_in=m_w_in[0], w_branch_a=m_w_branch_a[0], w_branch_b=m_w_branch_b[0], w_out=m_w_out[0],
                   w_ffn_gate=tr(m_w_ffn_gate), w_ffn_up=tr(m_w_ffn_up), w_ffn_down=m_w_ffn_down[0])
    shard_v = dict(w_in=v_w_in[0], w_branch_a=v_w_branch_a[0], w_branch_b=v_w_branch_b[0], w_out=v_w_out[0],
                   w_ffn_gate=tr(v_w_ffn_gate), w_ffn_up=tr(v_w_ffn_up), w_ffn_down=v_w_ffn_down[0])

    shard16 = {n: shard_w[n].astype(BF16) for n in _BIG}

    def with_mine(gathered, n):
        return lax.dynamic_update_slice(gathered, shard16[n][None], (chip, 0, 0))

    w_p = _permute_cols(_cols_of_chips(with_mine(_gather_weights([shard16["w_in"]])[0], "w_in")))
    late = _BIG[1:]
    late_send, late_recv, late_src, late_land, late_token = _split_start(
        _gather_copies, [shard16[n] for n in late],
        [lax.empty((N_CHIPS,) + shard16[n].shape, BF16) for n in late], "gather_late_start")

    c8 = jnp.concatenate([c, jnp.zeros((8 - nbatch, D), F32)], axis=0)
    c_all = _allgather8(c8, "gather_c")[:, :nbatch, :].reshape(N_DEV * nbatch, D)
    ncol = w_ada.shape[2]
    b_cols = lax.dynamic_slice_in_dim(b_ada, chip * ncol, ncol, axis=1)
    mod_g = _allgather8(_mod_shard(c_all, w_ada[0], b_cols), "gather_mod")
    mod_all = jnp.concatenate([mod_g[2 * k] for k in range(N_CHIPS)], axis=1)
    mod_mine = lax.dynamic_slice_in_dim(mod_all, dev * nbatch, nbatch, axis=0)
    mod8 = jnp.concatenate([mod_mine.reshape(nbatch, 6, D), jnp.zeros((nbatch, 2, D), F32)], axis=1)
    mod8 = mod8 + late_token[0, 0]

    x2 = x.reshape(t, D)
    tgt2 = loss_target.reshape(t, D)
    bias128 = jnp.concatenate([fox_f_bias, jnp.zeros((1, 128 - BH), F32)], axis=1)

    proj, h16 = _proj(x2, mod8, w_p, seq, BF16, "proj")
    projf = _rows_matmul(h16, w_p[:, COL_BF:], "proj_forget")
    ya, ckpt = _hgrn_fwd(proj, lb_logits, hgrn_norm_w, nbatch, seq)
    cum_cols = _fox_cum(projf, bias128, nbatch, seq)
    yb, lse = _fox_fwd(proj, cum_cols, nbatch, seq)
    late_land = _pass_to_sibling(
        _split_wait(_gather_copies, late_send, late_recv, late_src, late_land, yb, "gather_late_wait"))
    full = {n: with_mine(g, n) for n, g in zip(late, late_land)}
    wba, wbb = _cols_of_chips(full["w_branch_a"]), _cols_of_chips(full["w_branch_b"])
    wout = full["w_out"].reshape(D, D)
    wg_t, wu_t, wd = full["w_ffn_gate"], full["w_ffn_up"], full["w_ffn_down"]
    wg, wu, wd_t = jnp.swapaxes(wg_t, 1, 2), jnp.swapaxes(wu_t, 1, 2), jnp.swapaxes(wd, 1, 2)
    merged, u, x1 = _merge_fwd(ya, yb, proj, x2, mod8, wba, wbb, wout, ln1_w, ln1_b, seq)
    a_pre, b_pre, h2, dz2, st2, dm2 = _ffn_fwd(x1, mod8, wg, wu, wd, tgt2, ln2_w, ln2_b, seq)
    loss = lax.psum(st2[2, 0], ("x", "y", "c"))

    da, db, hmid, dffn, du, dxp, st1, dm1 = _ffn_bwd(dz2, a_pre, b_pre, wg_t, wu_t, wd_t, x1, x2, u, mod8, ln1_w, seq)
    g_st = {}
    g_st["w_ffn_down"] = _tn_matmul(hmid, dffn, "dw_ffn_down", seq)
    g_st["w_ffn_gate"] = _tn_matmul(da, h2, "dw_ffn_gate", seq)
    g_st["w_ffn_up"] = _tn_matmul(db, h2, "dw_ffn_up", seq)
    g_st["w_out"] = _tn_matmul(merged, du, "dw_out", seq).reshape(N_CHIPS, D // N_CHIPS, D)

    def sum_over_cores(names, tag):
        g_list = [g_st[n] for n in names]
        return [_add_my_half(g, o, core_arr, "grad_add_halves_" + n)
                for n, g, o in zip(names, g_list, _swap_halves(g_list, "grad_swap_halves_" + tag))]

    early = ("w_ffn_down", "w_ffn_gate", "w_ffn_up", "w_out")
    e_halves = sum_over_cores(early, "early")
    e_send, e_recv, e_src, e_land, e_token = _split_start(
        _scatter_copies, [h16 for _, h16 in e_halves],
        [lax.empty((3,) + h16.shape[1:], BF16) for _, h16 in e_halves], "grad_scatter_early_start")
    dproj, dpa, dpb, dya, dyb = _merge_bwd(du, ya, yb, proj, wba, wbb, wout, e_token, seq)
    g_st["w_branch_a"] = _tn_matmul(ya, dpa, "dw_branch_a", seq, split=D // N_CHIPS)
    g_st["w_branch_b"] = _tn_matmul(yb, dpb, "dw_branch_b", seq, split=D // N_CHIPS)
    dproj, dq, drs, dcs = _fox_bwd(proj, cum_cols, lse, yb, dyb, dproj, nbatch, seq)
    dproj = _place_cols(dproj, dq, COL_BQ)
    dproj, sm_fox = _fox_dbf(projf, bias128, drs, dcs, dproj, nbatch, seq)
    dproj, sm_hgrn = _hgrn_bwd(proj, dya, ckpt, lb_logits, hgrn_norm_w, dproj, nbatch, seq)
    grad_x2, dm0 = _dh_kernel(dproj, w_p, x2, dxp, mod8, seq)
    dw_in = _unpermute_cols(_tn_matmul(h16, dproj, "dw_in", seq))
    ncin = NIN // N_CHIPS
    g_st["w_in"] = jnp.stack([dw_in[:, k * ncin:(k + 1) * ncin] for k in range(N_CHIPS)])

    e_recv = _split_wait(_scatter_copies, e_send, e_recv, e_src, e_land, dw_in, "grad_scatter_early_wait")
    rest = ("w_in", "w_branch_a", "w_branch_b")
    r_halves = sum_over_cores(rest, "rest")
    r_send, r_rcv, r_src, r_land, r_token = _split_start(
        _scatter_copies, [h16 for _, h16 in r_halves],
        [lax.empty((3,) + h16.shape[1:], BF16) for _, h16 in r_halves], "grad_scatter_rest_start")

    def finish(names, halves, recv, token, tag):
        g_mine = [_add_chips(h32, r, chip_arr, "grad_add_chips_" + n) for n, (h32, _), r in zip(names, halves, recv)]
        g_sib = _join_halves(g_mine, token, "grad_join_halves_" + tag)
        for n, gm, gs in zip(names, g_mine, g_sib):
            grads[n], deltas[n], new_m[n], new_v[n] = _adamw_halves(
                shard_w[n], gm, gs, shard_m[n], shard_v[n], core_arr, "adamw_" + n)

    grads, deltas, new_m, new_v = {}, {}, {}, {}
    finish(early, e_halves, e_recv, r_token, "early")

    dmod = (dm0 + dm1 + dm2)[:, :6, :].reshape(nbatch, 6 * D)
    row2 = jnp.concatenate([st1[0:1], st1[1:2], st2[0:1], st2[1:2], sm_hgrn[1:2], sm_hgrn[0:1], sm_hgrn[0:1],
                            sm_fox[0:1, :BH], jnp.zeros((1, SMALL_W - O_FOX - BH), F32)], axis=1)
    spack = jnp.concatenate([dmod, row2, jnp.zeros((8 - nbatch - 1, SMALL_W), F32)], axis=0)
    spack = spack + r_token[0, 0]
    gath = _allgather8(spack, "gather_small")
    w8 = _pack_small(b_ada, ln1_w, ln1_b, ln2_w, ln2_b, hgrn_norm_w, lb_logits, fox_f_bias)
    m8 = _pack_small(m_b_ada, m_ln1_w, m_ln1_b, m_ln2_w, m_ln2_b, m_hgrn_norm_w, m_lb_logits, m_fox_f_bias)
    v8 = _pack_small(v_b_ada, v_ln1_w, v_ln1_b, v_ln2_w, v_ln2_b, v_hgrn_norm_w, v_lb_logits, v_fox_f_bias)
    sg, sd, smn, svn = (_unpack_small(p) for p in _small_update(gath, w8, m8, v8))
    dmod_all = gath[:, :nbatch, :].reshape(N_DEV * nbatch, SMALL_W)
    g_ada = _grad_w_ada(c_all, lax.dynamic_slice_in_dim(dmod_all, chip * ncol, ncol, axis=1))

    for group, small in zip((grads, deltas, new_m, new_v), (sg, sd, smn, svn)):
        group.update(small)
    grads["w_ada"] = g_ada
    deltas["w_ada"], new_m["w_ada"], new_v["w_ada"] = _adamw(w_ada[0], g_ada, m_w_ada[0], v_w_ada[0], "adamw_w_ada")
    done = sum(new_v[n][0:8, 0:128] for n in early) + new_v["w_ada"][0:8, 0:128]
    r_recv = _split_wait(_scatter_copies, r_send, r_rcv, r_src, r_land, done, "grad_scatter_rest_wait")
    finish(rest, r_halves, r_recv, late_token, "rest")

    names = ["w_ada", "b_ada", "w_in", "fox_f_bias", "lb_logits", "hgrn_norm_w", "w_branch_a", "w_branch_b", "w_out",
             "ln1_w", "ln1_b", "w_ffn_gate", "w_ffn_up", "w_ffn_down", "ln2_w", "ln2_b"]
    shapes = dict(w_ada=w_ada.shape, b_ada=b_ada.shape, w_in=w_in.shape, fox_f_bias=fox_f_bias.shape,
                  lb_logits=lb_logits.shape, hgrn_norm_w=hgrn_norm_w.shape, w_branch_a=w_branch_a.shape,
                  w_branch_b=w_branch_b.shape, w_out=w_out.shape, ln1_w=ln1_w.shape, ln1_b=ln1_b.shape,
                  w_ffn_gate=w_ffn_gate.shape, w_ffn_up=w_ffn_up.shape, w_ffn_down=w_ffn_down.shape,
                  ln2_w=ln2_w.shape, ln2_b=ln2_b.shape)
    outs = [loss, grad_x2.reshape(x.shape)]
    for group in (grads, deltas, new_m, new_v):
        outs += [(jnp.swapaxes(group[n], 0, 1) if n in _TRANSPOSED else group[n]).reshape(shapes[n]) for n in names]
    return tuple(outs)
```

```python
import functools

import jax
import jax.numpy as jnp
from jax import lax
from jax.experimental import pallas as pl
from jax.experimental.pallas import tpu as pltpu

F32 = jnp.float32
BF16 = jnp.bfloat16
MESH = pl.DeviceIdType.MESH
HIGHEST = lax.Precision.HIGHEST

D = 1024
AW = 512
AH = 4
ADH = 128
BH = 8
BDH = 64
DFF = 2816
NIN = 5640
NP = 5760
N_CHIPS = 4
N_DEV = 8
HGRN_BLOCK = 256
FFN_TOKENS = 512
COL_GATES = 0
COL_A = 2048
COL_BQ = 4096
COL_KV = 4608
COL_BF = 5632
HGRN_HEADS = 4
ALPHA = 2.0 ** 0.25
LN_EPS = 1e-5
RMS_EPS = 1e-6
NEG = -1e30
LOG2E = 1.4426950408889634
LR, B1, B2, EPS, WD, STEP = 0.001, 0.9, 0.999, 1e-08, 0.01, 10
SMALL_W = 6144
O_LN1W, O_LN1B, O_LN2W, O_LN2B, O_NORM, O_LB0, O_LB1, O_FOX = 0, 1024, 2048, 3072, 4096, 4608, 5120, 5632


def _params(sem=None, vmem_mb=None):
    kw = {}
    if sem is not None:
        kw["dimension_semantics"] = sem
    if vmem_mb is not None:
        kw["vmem_limit_bytes"] = vmem_mb << 20
    return pltpu.CompilerParams(**kw)


def _dot(a, b):
    return jnp.dot(a.astype(BF16), b.astype(BF16), preferred_element_type=F32)


def _dot_nt(a, b):
    return lax.dot_general(a.astype(BF16), b.astype(BF16), (((1,), (1,)), ((), ())), preferred_element_type=F32)


def _dot_tn(a, b):
    return lax.dot_general(a.astype(BF16), b.astype(BF16), (((0,), (0,)), ((), ())), preferred_element_type=F32)


def _dot_f32(a, b):
    return jnp.dot(a, b, preferred_element_type=F32, precision=HIGHEST)


def _perm_segments():
    segs = [(3592, 5640)]
    for h in range(4):
        segs += [(128 * h + 512 * t, 128 * h + 512 * t + 128) for t in range(4)]
    segs += [(2048, 2560)]
    for p in range(4):
        segs += [(2560 + 128 * p, 2688 + 128 * p), (3072 + 128 * p, 3200 + 128 * p)]
    segs += [(3584, 3592)]
    return segs


def _permute_cols(w):
    parts = [w[:, a:b] for a, b in _perm_segments()]
    parts.append(jnp.zeros((w.shape[0], NP - NIN), w.dtype))
    return jnp.concatenate(parts, axis=1)


def _unpermute_cols(g):
    pos, where = 0, {}
    for a, b in _perm_segments():
        where[a] = (pos, pos + b - a)
        pos += b - a
    parts = [g[:, where[a][0]:where[a][1]] for a in sorted(where)]
    return jnp.concatenate(parts, axis=1)


def _allgather8(v, name):
    rows, cols = v.shape

    def body(x_ref, out_ref, send_sems, recv_sems, local_sem):
        x, y, c = lax.axis_index("x"), lax.axis_index("y"), lax.axis_index("c")
        me, sibling = (x, y, c), (x, y, 1 - c)
        chips = [(1 - x, y), (x, 1 - y), (1 - x, 1 - y)]

        def slot(px, py, pc):
            return out_ref.at[4 * px + 2 * py + pc]

        def copy(k, block, to, src=None):
            return pltpu.make_async_remote_copy(
                src_ref=slot(*block) if src is None else src, dst_ref=slot(*block),
                send_sem=send_sems.at[k], recv_sem=recv_sems.at[k], device_id=to, device_id_type=MESH)

        mine = pltpu.make_async_copy(x_ref, slot(*me), local_sem)
        mine.start()
        first = [copy(0, me, sibling, src=x_ref)]
        first += [copy(1 + j, me, (*chip, c), src=x_ref) for j, chip in enumerate(chips)]
        for cp in first:
            cp.start()
        passed = [copy(4 + j, (*chip, c), sibling) for j, chip in enumerate(chips)]
        for j, chip in enumerate(chips):
            copy(1 + j, (*chip, c), me).wait_recv()
            passed[j].start()
        copy(0, sibling, me).wait_recv()
        for j, chip in enumerate(chips):
            copy(4 + j, (*chip, 1 - c), me).wait_recv()
        for cp in first + passed:
            cp.wait_send()
        mine.wait()

    return pl.pallas_call(
        body, name=name,
        out_shape=jax.ShapeDtypeStruct((N_DEV, rows, cols), v.dtype),
        in_specs=[pl.BlockSpec(memory_space=pltpu.VMEM)],
        out_specs=pl.BlockSpec(memory_space=pltpu.VMEM),
        scratch_shapes=[pltpu.SemaphoreType.DMA((7,)), pltpu.SemaphoreType.DMA((7,)), pltpu.SemaphoreType.DMA],
    )(v)


def _hbm_specs(n):
    return [pl.BlockSpec(memory_space=pl.ANY)] * n


def _gather_weights(shards):
    n = len(shards)

    def body(*refs):
        ins, outs, (send_sems, recv_sems) = refs[:n], refs[n:2 * n], refs[2 * n:]
        x, y, c = lax.axis_index("x"), lax.axis_index("y"), lax.axis_index("c")
        sibling = (x, y, 1 - c)
        chips = [(1 - x, y), (x, 1 - y), (1 - x, 1 - y)]

        def blk(w, px, py, half):
            hr = ins[w].shape[0] // 2
            return outs[w].at[2 * px + py, pl.ds(half * hr, hr), :]

        def copy(w, k, block, to, src=None):
            return pltpu.make_async_remote_copy(
                src_ref=blk(w, *block) if src is None else src, dst_ref=blk(w, *block),
                send_sem=send_sems.at[6 * w + k], recv_sem=recv_sems.at[6 * w + k], device_id=to, device_id_type=MESH)

        first = []
        for w in range(n):
            hr = ins[w].shape[0] // 2
            my_half = ins[w].at[pl.ds(c * hr, hr), :]
            first += [copy(w, j, (x, y, c), (*chip, c), src=my_half) for j, chip in enumerate(chips)]
        for cp in first:
            cp.start()
        passed = []
        for j, chip in enumerate(chips):
            for w in range(n):
                copy(w, j, (*chip, c), (x, y, c)).wait_recv()
                passed.append(copy(w, 3 + j, (*chip, c), sibling))
                passed[-1].start()
        for j, chip in enumerate(chips):
            for w in range(n):
                copy(w, 3 + j, (*chip, 1 - c), (x, y, c)).wait_recv()
        for cp in first + passed:
            cp.wait_send()

    return pl.pallas_call(
        body, name="gather_weights",
        out_shape=[jax.ShapeDtypeStruct((N_CHIPS,) + s.shape, s.dtype) for s in shards],
        in_specs=_hbm_specs(n), out_specs=_hbm_specs(n),
        scratch_shapes=[pltpu.SemaphoreType.DMA((6 * n,)), pltpu.SemaphoreType.DMA((6 * n,))],
    )(*shards)


def _swap_halves(grads, name):
    n = len(grads)

    def body(*refs):
        ins, outs, (send_sems, recv_sems) = refs[:n], refs[n:2 * n], refs[2 * n:]
        x, y, c = lax.axis_index("x"), lax.axis_index("y"), lax.axis_index("c")
        cps = []
        for w in range(n):
            hr = ins[w].shape[1] // 2
            cps.append(pltpu.make_async_remote_copy(
                src_ref=ins[w].at[:, pl.ds((1 - c) * hr, hr), :], dst_ref=outs[w],
                send_sem=send_sems.at[w], recv_sem=recv_sems.at[w], device_id=(x, y, 1 - c), device_id_type=MESH))
        for cp in cps:
            cp.start()
        for cp in cps:
            cp.wait()

    return pl.pallas_call(
        body, name=name,
        out_shape=[jax.ShapeDtypeStruct((N_CHIPS, g.shape[1] // 2, g.shape[2]), g.dtype) for g in grads],
        in_specs=_hbm_specs(n), out_specs=_hbm_specs(n),
        scratch_shapes=[pltpu.SemaphoreType.DMA((n,)), pltpu.SemaphoreType.DMA((n,))],
    )(*grads)


def _scatter_chips(reds, name):
    n = len(reds)

    def body(*refs):
        ins, outs, (send_sems, recv_sems) = refs[:n], refs[n:2 * n], refs[2 * n:]
        x, y, c = lax.axis_index("x"), lax.axis_index("y"), lax.axis_index("c")
        chips = [(1 - x, y), (x, 1 - y), (1 - x, 1 - y)]
        cps = [pltpu.make_async_remote_copy(
            src_ref=ins[w].at[2 * chip[0] + chip[1]], dst_ref=outs[w].at[j],
            send_sem=send_sems.at[3 * w + j], recv_sem=recv_sems.at[3 * w + j],
            device_id=(*chip, c), device_id_type=MESH)
            for j, chip in enumerate(chips) for w in range(n)]
        for cp in cps:
            cp.start()
        for cp in cps:
            cp.wait()

    return pl.pallas_call(
        body, name=name,
        out_shape=[jax.ShapeDtypeStruct((3,) + r.shape[1:], r.dtype) for r in reds],
        in_specs=_hbm_specs(n), out_specs=_hbm_specs(n),
        scratch_shapes=[pltpu.SemaphoreType.DMA((3 * n,)), pltpu.SemaphoreType.DMA((3 * n,))],
    )(*reds)


def _join_halves(halves, token, name):
    n = len(halves)

    def body(*refs):
        ins, outs, (send_sems, recv_sems) = refs[:n], refs[n + 1:2 * n + 1], refs[2 * n + 1:]
        x, y, c = lax.axis_index("x"), lax.axis_index("y"), lax.axis_index("c")
        cps = [pltpu.make_async_remote_copy(
            src_ref=ins[w], dst_ref=outs[w], send_sem=send_sems.at[w], recv_sem=recv_sems.at[w],
            device_id=(x, y, 1 - c), device_id_type=MESH) for w in range(n)]
        for cp in cps:
            cp.start()
        for cp in cps:
            cp.wait()

    return pl.pallas_call(
        body, name=name,
        out_shape=[jax.ShapeDtypeStruct(h.shape, h.dtype) for h in halves],
        in_specs=_hbm_specs(n + 1), out_specs=_hbm_specs(n),
        scratch_shapes=[pltpu.SemaphoreType.DMA((n,)), pltpu.SemaphoreType.DMA((n,))],
    )(*halves, token)


def _in_hbm(v):
    return pltpu.with_memory_space_constraint(v, pltpu.HBM)


_SPLIT_COPY = pltpu.CompilerParams(has_side_effects=pltpu.SideEffectType.DATAFLOW_SIDE_EFFECTING)


def _gather_copies(srcs, lands, send_sems, recv_sems):
    x, y, c = lax.axis_index("x"), lax.axis_index("y"), lax.axis_index("c")
    cps = []
    for w, (src, land) in enumerate(zip(srcs, lands)):
        hr = src.shape[0] // 2
        for j, chip in enumerate([(1 - x, y), (x, 1 - y), (1 - x, 1 - y)]):
            cps.append(pltpu.make_async_remote_copy(
                src_ref=src.at[pl.ds(c * hr, hr), :], dst_ref=land.at[2 * x + y, pl.ds(c * hr, hr), :],
                send_sem=send_sems.at[3 * w + j], recv_sem=recv_sems.at[3 * w + j],
                device_id=(*chip, c), device_id_type=MESH))
    return cps


def _scatter_copies(srcs, lands, send_sems, recv_sems):
    x, y, c = lax.axis_index("x"), lax.axis_index("y"), lax.axis_index("c")
    cps = []
    for w, (src, land) in enumerate(zip(srcs, lands)):
        for j, chip in enumerate([(1 - x, y), (x, 1 - y), (1 - x, 1 - y)]):
            cps.append(pltpu.make_async_remote_copy(
                src_ref=src.at[2 * chip[0] + chip[1]], dst_ref=land.at[j],
                send_sem=send_sems.at[3 * w + j], recv_sem=recv_sems.at[3 * w + j],
                device_id=(*chip, c), device_id_type=MESH))
    return cps


def _split_start(copies, srcs, lands, name):
    n = len(srcs)

    def body(*refs):
        src, lnd, send_sems, recv_sems, token = refs[:n], refs[n:2 * n], refs[2 * n], refs[2 * n + 1], refs[-1]
        for cp in copies(src, lnd, send_sems, recv_sems):
            cp.start()
        token[...] = jnp.zeros_like(token)

    hbm = pl.BlockSpec(memory_space=pltpu.HBM)
    sem = pl.BlockSpec(memory_space=pltpu.SEMAPHORE)
    outs = pl.pallas_call(
        body, name=name,
        out_shape=(pltpu.SemaphoreType.DMA((3 * n,)), pltpu.SemaphoreType.DMA((3 * n,)),
                   *[pltpu.HBM(v.shape, v.dtype) for v in srcs + lands], jax.ShapeDtypeStruct((8, 128), F32)),
        in_specs=[hbm] * (2 * n),
        out_specs=(sem, sem, *([hbm] * (2 * n)), pl.BlockSpec(memory_space=pltpu.VMEM)),
        input_output_aliases={i: 2 + i for i in range(2 * n)},
        compiler_params=_SPLIT_COPY,
    )(*[_in_hbm(v) for v in srcs + lands])
    return outs[0], outs[1], list(outs[2:2 + n]), list(outs[2 + n:2 + 2 * n]), outs[-1]


def _split_wait(copies, send_sems, recv_sems, srcs, lands, after, name):
    n = len(srcs)

    def body(*refs):
        src, lnd, send_sems, recv_sems = refs[:n], refs[n:2 * n], refs[2 * n], refs[2 * n + 1]
        for cp in copies(src, lnd, send_sems, recv_sems):
            cp.wait_send()
            cp.wait_recv()

    hbm = pl.BlockSpec(memory_space=pltpu.HBM)
    sem = pl.BlockSpec(memory_space=pltpu.SEMAPHORE)
    outs = pl.pallas_call(
        body, name=name,
        out_shape=tuple(pltpu.HBM(v.shape, v.dtype) for v in srcs + lands),
        in_specs=[hbm] * (2 * n) + [sem, sem, pl.BlockSpec(memory_space=pl.ANY)],
        out_specs=tuple([hbm] * (2 * n)),
        input_output_aliases={i: i for i in range(2 * n)},
        compiler_params=_SPLIT_COPY,
    )(*srcs, *lands, send_sems, recv_sems, after)
    return list(outs[n:])


def _pass_to_sibling(lands):
    n = len(lands)

    def body(*refs):
        ins, outs, (send_sems, recv_sems) = refs[:n], refs[n:2 * n], refs[2 * n:]
        x, y, c = lax.axis_index("x"), lax.axis_index("y"), lax.axis_index("c")
        cps = []
        for w in range(n):
            hr = ins[w].shape[1] // 2
            for j, chip in enumerate([(1 - x, y), (x, 1 - y), (1 - x, 1 - y)]):
                k = 2 * chip[0] + chip[1]
                cps.append(pltpu.make_async_remote_copy(
                    src_ref=ins[w].at[k, pl.ds(c * hr, hr), :], dst_ref=outs[w].at[k, pl.ds(c * hr, hr), :],
                    send_sem=send_sems.at[3 * w + j], recv_sem=recv_sems.at[3 * w + j],
                    device_id=(x, y, 1 - c), device_id_type=MESH))
        for cp in cps:
            cp.start()
        for cp in cps:
            cp.wait()

    return pl.pallas_call(
        body, name="gather_late_pass",
        out_shape=[jax.ShapeDtypeStruct(v.shape, v.dtype) for v in lands],
        in_specs=_hbm_specs(n), out_specs=_hbm_specs(n),
        input_output_aliases={i: i for i in range(n)},
        scratch_shapes=[pltpu.SemaphoreType.DMA((3 * n,)), pltpu.SemaphoreType.DMA((3 * n,))],
    )(*lands)


def _row_tile(rows):
    for cand in (256, 176, 128, 64, 32, 16):
        if rows % cand == 0:
            return cand
    raise ValueError(rows)


def _add_my_half(g, other, c_idx, name):
    _, k, n = g.shape
    hr = k // 2
    tr = _row_tile(hr)
    nb = hr // tr

    def body(c_ref, g_ref, o_ref, out_ref, out16_ref):
        s = g_ref[...] + o_ref[...]
        out_ref[...] = s
        out16_ref[...] = s.astype(BF16)

    return pl.pallas_call(
        body, name=name,
        grid_spec=pltpu.PrefetchScalarGridSpec(
            num_scalar_prefetch=1, grid=(N_CHIPS, nb),
            in_specs=[pl.BlockSpec((1, tr, n), lambda j, i, c: (j, c[0] * nb + i, 0)),
                      pl.BlockSpec((1, tr, n), lambda j, i, c: (j, i, 0))],
            out_specs=[pl.BlockSpec((1, tr, n), lambda j, i, c: (j, i, 0)),
                       pl.BlockSpec((1, tr, n), lambda j, i, c: (j, i, 0))]),
        out_shape=[jax.ShapeDtypeStruct((N_CHIPS, hr, n), F32), jax.ShapeDtypeStruct((N_CHIPS, hr, n), BF16)],
        compiler_params=_params(("parallel", "parallel")),
    )(c_idx, g, other)


def _add_chips(red, recv, chip_idx, name):
    _, hr, n = red.shape
    tr = _row_tile(hr)

    def body(k_ref, r_ref, v_ref, out_ref):
        out_ref[...] = ((r_ref[0] + v_ref[0].astype(F32)) + v_ref[1].astype(F32)) + v_ref[2].astype(F32)

    return pl.pallas_call(
        body, name=name,
        grid_spec=pltpu.PrefetchScalarGridSpec(
            num_scalar_prefetch=1, grid=(hr // tr,),
            in_specs=[pl.BlockSpec((1, tr, n), lambda i, k: (k[0], i, 0)),
                      pl.BlockSpec((3, tr, n), lambda i, k: (0, i, 0))],
            out_specs=pl.BlockSpec((tr, n), lambda i, k: (i, 0))),
        out_shape=jax.ShapeDtypeStruct((hr, n), F32),
        compiler_params=_params(("parallel",)),
    )(chip_idx, red, recv)


def _mod_shard(c_all, w_ada, b_ada):
    nb, cols = c_all.shape[0], w_ada.shape[1]

    def body(c_ref, w_ref, b_ref, o_ref):
        c = c_ref[...]
        o_ref[...] = _dot(c * jax.nn.sigmoid(c), w_ref[...]) + b_ref[...]

    return pl.pallas_call(
        body, name="mod_shard", out_shape=jax.ShapeDtypeStruct((nb, cols), F32),
        compiler_params=_params(vmem_mb=48),
    )(c_all, w_ada, b_ada)


def _proj(x2, mod8, w, seq, out_dtype, name):
    t = x2.shape[0]
    n = w.shape[1]
    tm, tn = min(2048, seq), min(1152, n)
    tpb = seq // tm

    def body(x_ref, mod_ref, w_ref, o_ref, h_ref):
        @pl.when(pl.program_id(1) == 0)
        def _():
            h_ref[...] = (x_ref[...] * (1.0 + mod_ref[0, 1:2, :]) + mod_ref[0, 0:1, :]).astype(BF16)
        o_ref[...] = jnp.dot(h_ref[...], w_ref[...], preferred_element_type=F32).astype(o_ref.dtype)

    return pl.pallas_call(
        body, name=name, grid=(t // tm, n // tn),
        in_specs=[pl.BlockSpec((tm, D), lambda i, j: (i, 0)),
                  pl.BlockSpec((1, 8, D), lambda i, j: (i // tpb, 0, 0)),
                  pl.BlockSpec((D, tn), lambda i, j: (0, j))],
        out_specs=[pl.BlockSpec((tm, tn), lambda i, j: (i, j)), pl.BlockSpec((tm, D), lambda i, j: (i, 0))],
        out_shape=[jax.ShapeDtypeStruct((t, n), out_dtype), jax.ShapeDtypeStruct((t, D), BF16)],
        compiler_params=_params(("parallel", "arbitrary"), 56),
    )(x2, mod8, w)


def _rows_matmul(a, w, name):
    t, k = a.shape
    n = w.shape[1]
    tm = 1024 if t % 1024 == 0 else t

    def body(a_ref, w_ref, o_ref):
        o_ref[...] = jnp.dot(a_ref[...], w_ref[...], preferred_element_type=F32)

    return pl.pallas_call(
        body, name=name, grid=(t // tm,),
        in_specs=[pl.BlockSpec((tm, k), lambda i: (i, 0)), pl.BlockSpec((k, n), lambda i: (0, 0))],
        out_specs=pl.BlockSpec((tm, n), lambda i: (i, 0)),
        out_shape=jax.ShapeDtypeStruct((t, n), F32),
        compiler_params=_params(("parallel",)),
    )(a, w)


def _tn_matmul(a, b, name, seq, split=None):
    a_st, b_st = a.ndim == 3, b.ndim == 3
    t, ka = a.shape[-2:]
    n = b.shape[-1]
    tt = min(1024, seq)
    nt = t // tt
    if a_st or b_st:
        steps, tn = (a.shape[0] if a_st else b.shape[0]), n
    else:
        tn = split
        if tn is None:
            tn = next(cand for cand in (1152, 1024, 1408, 512, n) if n % cand == 0)
        steps = n // tn
    stacked_out = a_st or b_st or split is not None

    def body(a_ref, b_ref, o_ref):
        part = _dot_tn(a_ref[0] if a_st else a_ref[...], b_ref[0] if b_st else b_ref[...])
        if stacked_out:
            part = part[None]

        @pl.when(pl.program_id(1) == 0)
        def _():
            o_ref[...] = part

        @pl.when(pl.program_id(1) > 0)
        def _():
            o_ref[...] += part

    if a_st:
        in_specs = [pl.BlockSpec((1, tt, ka), lambda j, k: (j, k, 0))]
    else:
        in_specs = [pl.BlockSpec((tt, ka), lambda j, k: (k, 0))]
    if b_st:
        in_specs.append(pl.BlockSpec((1, tt, n), lambda j, k: (j, k, 0)))
    else:
        in_specs.append(pl.BlockSpec((tt, tn), lambda j, k: (k, 0 if a_st else j)))
    if stacked_out:
        out_spec = pl.BlockSpec((1, ka, tn), lambda j, k: (j, 0, 0))
        out_shape = jax.ShapeDtypeStruct((steps, ka, tn), F32)
    else:
        out_spec = pl.BlockSpec((ka, tn), lambda j, k: (0, j))
        out_shape = jax.ShapeDtypeStruct((ka, n), F32)
    return pl.pallas_call(
        body, name=name, grid=(steps, nt), in_specs=in_specs, out_specs=out_spec, out_shape=out_shape,
        compiler_params=_params(("parallel", "arbitrary"), 56),
    )(a, b)


def _dh_kernel(dproj, w_p, x2, dxp, mod8, seq):
    t = x2.shape[0]
    tm, tk = min(1024, seq), 1152
    tpb = seq // tm
    nk = NP // tk
    nbatch = t // seq

    def body(dp_ref, w_ref, x_ref, dxp_ref, mod_ref, gx_ref, dm_ref, acc):
        i, k = pl.program_id(0), pl.program_id(1)

        @pl.when(k == 0)
        def _():
            acc[...] = jnp.zeros_like(acc)

        acc[...] += _dot_nt(dp_ref[...], w_ref[...])

        @pl.when(k == nk - 1)
        def _():
            dh = acc[...]
            gx_ref[...] = dxp_ref[...] + dh * (1.0 + mod_ref[0, 1:2, :])
            upd = jnp.concatenate(
                [jnp.sum(dh, axis=0, keepdims=True), jnp.sum(dh * x_ref[...], axis=0, keepdims=True),
                 jnp.zeros((6, D), F32)], axis=0)

            @pl.when(i % tpb == 0)
            def _():
                dm_ref[0] = upd

            @pl.when(i % tpb != 0)
            def _():
                dm_ref[0] += upd

    return pl.pallas_call(
        body, name="dh", grid=(t // tm, nk),
        in_specs=[pl.BlockSpec((tm, tk), lambda i, k: (i, k)),
                  pl.BlockSpec((D, tk), lambda i, k: (0, k)),
                  pl.BlockSpec((tm, D), lambda i, k: (i, 0)),
                  pl.BlockSpec((tm, D), lambda i, k: (i, 0)),
                  pl.BlockSpec((1, 8, D), lambda i, k: (i // tpb, 0, 0))],
        out_specs=[pl.BlockSpec((tm, D), lambda i, k: (i, 0)),
                   pl.BlockSpec((1, 8, D), lambda i, k: (i // tpb, 0, 0))],
        out_shape=[jax.ShapeDtypeStruct((t, D), F32), jax.ShapeDtypeStruct((nbatch, 8, D), F32)],
        scratch_shapes=[pltpu.VMEM((tm, D), F32)],
        compiler_params=_params(("arbitrary", "arbitrary"), 48),
    )(dproj, w_p, x2, dxp, mod8)


def _tri(n, upper):
    r = lax.broadcasted_iota(jnp.int32, (n, n), 0)
    c = lax.broadcasted_iota(jnp.int32, (n, n), 1)
    return jnp.where((c >= r) if upper else (c <= r), 1.0, 0.0).astype(F32)


@jax.custom_vjp
def _mm_nn(a, b):
    return _dot(a, b)


_mm_nn.defvjp(lambda a, b: (_dot(a, b), (a, b)),
              lambda res, g: (_dot_nt(g, res[1]), _dot_tn(res[0], g)))


@jax.custom_vjp
def _mm_nt(a, b):
    return _dot_nt(a, b)


_mm_nt.defvjp(lambda a, b: (_dot_nt(a, b), (a, b)),
              lambda res, g: (_dot(g, res[1]), _dot_tn(g, res[0])))


@jax.custom_vjp
def _mm_tn(a, b):
    return _dot_tn(a, b)


_mm_tn.defvjp(lambda a, b: (_dot_tn(a, b), (a, b)),
              lambda res, g: (_dot_nt(res[1], g), _dot(res[0], g)))


@jax.custom_vjp
def _cumsum_rows(x):
    return _dot_f32(_tri(x.shape[0], False), x)


_cumsum_rows.defvjp(lambda x: (_cumsum_rows(x), None),
                    lambda _, g: (_dot_f32(_tri(g.shape[0], True), g),))


@functools.partial(jax.custom_vjp, nondiff_argnums=(1,))
def _shift_rows(x, k):
    return pltpu.roll(x, k % x.shape[0], 0)


_shift_rows.defvjp(lambda x, k: (_shift_rows(x, k), None),
                   lambda k, _, g: (pltpu.roll(g, (-k) % g.shape[0], 0),))


def _group_ref(bc, m):
    n = bc.shape[0] // (2 * m)
    b3 = bc.reshape(n, 2 * m, ADH)
    row = lax.broadcasted_iota(jnp.int32, b3.shape, 1)
    ref = jnp.sum(jnp.where(row == m - 1, b3, 0.0), axis=1, keepdims=True)
    return jnp.broadcast_to(ref, b3.shape).reshape(bc.shape)


def _hgrn_block(q, fl, v, g, st, lb, nw):
    n = q.shape[0]
    f = lb + (1.0 - lb) * jax.nn.sigmoid(fl)
    kk = 1.0 - f
    lf = jnp.log(f)
    bc = _cumsum_rows(lf)
    row = lax.broadcasted_iota(jnp.int32, (n, ADH), 0)
    same = jnp.bitwise_xor(lax.broadcasted_iota(jnp.int32, (n, n), 0), lax.broadcasted_iota(jnp.int32, (n, n), 1))
    a = jnp.zeros((n, n), F32)
    m = 1
    while m < n:
        r = jnp.bitwise_and(row, 2 * m - 1)
        up, lo = r >= m, r < m
        if m == 1:
            aq, ak = lf, jnp.zeros_like(lf)
        elif m == 2:
            aq = jnp.where(r == 3, lf + _shift_rows(lf, 1), lf)
            ak = jnp.where(r == 0, _shift_rows(lf, -1), 0.0)
        else:
            ref = _group_ref(bc, m)
            aq, ak = bc - ref, ref - bc
        qt = jnp.where(up, q * jnp.exp(jnp.where(up, aq, 0.0)), 0.0)
        kt = jnp.where(lo, kk * jnp.exp(jnp.where(lo, ak, 0.0)), 0.0)
        a = a + jnp.where(same < 2 * m, _mm_nt(qt, kt), 0.0)
        m *= 2
    last = row == n - 1
    bl = jnp.sum(jnp.where(last, bc, 0.0), axis=0, keepdims=True)
    o = _mm_nn(a, v) + _mm_nt(q * jnp.exp(bc), st) + jnp.sum(q * kk, axis=-1, keepdims=True) * v
    st_new = st * jnp.exp(bl) + _mm_tn(v, kk * jnp.exp(bl - bc))
    rms = lax.rsqrt(jnp.mean(o * o, axis=-1, keepdims=True) + RMS_EPS)
    return o * rms * nw * jax.nn.sigmoid(g), st_new


def _hgrn_fwd(proj, lb_logits, norm_w, nbatch, seq):
    t = proj.shape[0]
    blk = min(HGRN_BLOCK, seq)
    nb = seq // blk

    nh = HGRN_HEADS
    wp, wy = 512 * nh, ADH * nh

    def body(p_ref, lbl_ref, nw_ref, y_ref, ck_ref, st_s):
        @pl.when(pl.program_id(2) == 0)
        def _():
            st_s[...] = jnp.zeros_like(st_s)

        st = [st_s[h] for h in range(nh)]
        p = p_ref[...].astype(F32)
        lb = jax.nn.sigmoid(lbl_ref[0:1, :] - lbl_ref[1:2, :])
        nw = nw_ref[...]
        res = [_hgrn_block(*(p[:, 512 * h + 128 * k:512 * h + 128 * k + 128] for k in range(4)), st[h],
                           lb[:, 128 * h:128 * h + 128], nw[:, 128 * h:128 * h + 128]) for h in range(nh)]
        for h in range(nh):
            ck_ref[0, h] = st[h]
            st_s[h] = res[h][1]
        y_ref[...] = jnp.concatenate([r[0] for r in res], axis=1).astype(y_ref.dtype)

    return pl.pallas_call(
        body, name="hgrn_fwd", grid=(AH // nh, nbatch, nb),
        in_specs=[pl.BlockSpec((blk, wp), lambda h, b, i: (b * nb + i, COL_A // wp + h)),
                  pl.BlockSpec((2, wy), lambda h, b, i: (0, h)),
                  pl.BlockSpec((1, wy), lambda h, b, i: (0, h))],
        out_specs=[pl.BlockSpec((blk, wy), lambda h, b, i: (b * nb + i, h)),
                   pl.BlockSpec((1, nh, 128, 128), lambda h, b, i: ((h * nbatch + b) * nb + i, 0, 0, 0))],
        out_shape=[jax.ShapeDtypeStruct((t, AW), BF16),
                   jax.ShapeDtypeStruct((AH // nh * nbatch * nb, nh, 128, 128), F32)],
        scratch_shapes=[pltpu.VMEM((nh, 128, 128), F32)],
        compiler_params=_params(("parallel", "parallel", "arbitrary"), 48),
    )(proj, lb_logits, norm_w)


def _hgrn_bwd(proj, dya, ckpt, lb_logits, norm_w, dproj, nbatch, seq):
    t = proj.shape[0]
    blk = min(HGRN_BLOCK, seq)
    nb = seq // blk

    nh = HGRN_HEADS
    wp, wy = 512 * nh, ADH * nh

    def body(p_ref, dy_ref, ck_ref, lbl_ref, nw_ref, dp_in, dp_ref, sm_ref, dst_s):
        del dp_in
        b_id, i = pl.program_id(1), pl.program_id(2)

        @pl.when(i == 0)
        def _():
            dst_s[...] = jnp.zeros_like(dst_s)

        dst = [dst_s[h] for h in range(nh)]
        st = [ck_ref[0, h] for h in range(nh)]
        p = p_ref[...].astype(F32)
        dy = dy_ref[...]
        lb = jax.nn.sigmoid(lbl_ref[0:1, :] - lbl_ref[1:2, :])
        nw = nw_ref[...]
        grads = []
        for h in range(nh):
            _, pullback = jax.vjp(_hgrn_block, *(p[:, 512 * h + 128 * k:512 * h + 128 * k + 128] for k in range(4)),
                                  st[h], lb[:, 128 * h:128 * h + 128], nw[:, 128 * h:128 * h + 128])
            grads.append(pullback((dy[:, 128 * h:128 * h + 128], dst[h])))
        for h in range(nh):
            dst_s[h] = grads[h][4]
        dp_ref[...] = jnp.concatenate([g[k] for g in grads for k in range(4)], axis=1).astype(dp_ref.dtype)
        upd = jnp.concatenate([jnp.concatenate([g[5] for g in grads], axis=1),
                               jnp.concatenate([g[6] for g in grads], axis=1), jnp.zeros((6, wy), F32)], axis=0)
        first = (b_id == 0) & (i == 0)

        @pl.when(first)
        def _():
            sm_ref[...] = upd

        @pl.when(jnp.logical_not(first))
        def _():
            sm_ref[...] += upd

    def rows(h, b, i):
        return b * nb + (nb - 1 - i)

    return pl.pallas_call(
        body, name="hgrn_bwd", grid=(AH // nh, nbatch, nb),
        in_specs=[pl.BlockSpec((blk, wp), lambda h, b, i: (rows(h, b, i), COL_A // wp + h)),
                  pl.BlockSpec((blk, wy), lambda h, b, i: (rows(h, b, i), h)),
                  pl.BlockSpec((1, nh, 128, 128), lambda h, b, i: ((h * nbatch + b) * nb + (nb - 1 - i), 0, 0, 0)),
                  pl.BlockSpec((2, wy), lambda h, b, i: (0, h)),
                  pl.BlockSpec((1, wy), lambda h, b, i: (0, h)),
                  pl.BlockSpec(memory_space=pl.ANY)],
        out_specs=[pl.BlockSpec((blk, wp), lambda h, b, i: (rows(h, b, i), COL_A // wp + h)),
                   pl.BlockSpec((8, wy), lambda h, b, i: (0, h))],
        out_shape=[jax.ShapeDtypeStruct((t, NP), BF16), jax.ShapeDtypeStruct((8, AW), F32)],
        input_output_aliases={5: 0},
        scratch_shapes=[pltpu.VMEM((nh, 128, 128), F32)],
        compiler_params=_params(("parallel", "arbitrary", "arbitrary"), 56),
    )(proj, dya, ckpt, lb_logits, norm_w, dproj)


def _log_sigmoid(z):
    return jnp.minimum(z, 0.0) - jnp.log(1.0 + jnp.exp(-jnp.abs(z)))


def _fox_cum(proj, bias128, nbatch, seq):
    t = proj.shape[0]
    ts = min(512, seq)
    nb = seq // ts

    def body(p_ref, b_ref, c_ref, carry):
        @pl.when(pl.program_id(1) == 0)
        def _():
            carry[...] = jnp.zeros_like(carry)
        cum = _dot_f32(_tri(ts, False), _log_sigmoid(p_ref[...] + b_ref[...])) + carry[...]
        carry[...] = cum[ts - 1:ts, :]
        cum2 = cum * LOG2E
        lane = lax.broadcasted_iota(jnp.int32, (ts, 128), 1)
        for p in range(4):
            c_ref[p] = jnp.where(lane < 64, cum2[:, 2 * p:2 * p + 1], cum2[:, 2 * p + 1:2 * p + 2])

    return pl.pallas_call(
        body, name="fox_cum", grid=(nbatch, nb),
        in_specs=[pl.BlockSpec((ts, 128), lambda b, i: (b * nb + i, 0)),
                  pl.BlockSpec((1, 128), lambda b, i: (0, 0))],
        out_specs=pl.BlockSpec((4, ts, 128), lambda b, i: (0, b * nb + i, 0)),
        out_shape=jax.ShapeDtypeStruct((4, t, 128), F32),
        scratch_shapes=[pltpu.VMEM((1, 128), F32)],
        compiler_params=_params(("parallel", "arbitrary")),
    )(proj, bias128)


def _fox_scores_t(q_ref, kv_ref, cc_ref, hh, masked, tq, tk):
    kh = kv_ref[:, 0:128].astype(BF16)
    qh = _head_lanes((q_ref[...] * (LOG2E * BDH ** -0.5)).astype(BF16), hh)
    s = _dot_nt(kh, qh) - cc_ref[0, :, 64 * hh:64 * hh + 1]
    if masked:
        key = lax.broadcasted_iota(jnp.int32, (tk, tq), 0)
        qry = lax.broadcasted_iota(jnp.int32, (tk, tq), 1)
        s = jnp.where(key <= qry, s, NEG)
    return s, kh


def _causal_pairs(nq, key_major):
    if key_major:
        pairs = [(i, j) for j in range(nq) for i in range(j, nq)]
    else:
        pairs = [(i, j) for i in range(nq) for j in range(i + 1)]
    return (jnp.asarray([p[0] for p in pairs], jnp.int32), jnp.asarray([p[1] for p in pairs], jnp.int32))


def _head_lanes(x128, hh):
    lane = lax.broadcasted_iota(jnp.int32, x128.shape, 1)
    return jnp.where((lane < 64) if hh == 0 else (lane >= 64), x128, jnp.zeros_like(x128))


def _with_ones_lane(x128, hh):
    lane = lax.broadcasted_iota(jnp.int32, x128.shape, 1)
    one = jnp.ones_like(x128)
    zero = jnp.zeros_like(x128)
    if hh == 0:
        return jnp.where(lane < 64, x128, jnp.where(lane == 64, one, zero))
    return jnp.where(lane >= 64, x128, jnp.where(lane == 0, one, zero))


def _fox_fwd(proj, cum_cols, nbatch, seq):
    t = proj.shape[0]
    tq = tk = min(512, seq)
    nq = seq // tq
    qi, kj = _causal_pairs(nq, key_major=False)

    def body(qi_ref, kj_ref, q_ref, kv_ref, cc_ref, o_ref, lse_ref, m_s, acc_s):
        s_id = pl.program_id(2)
        i, j = qi_ref[s_id], kj_ref[s_id]

        @pl.when(j == 0)
        def _():
            m_s[...] = jnp.full_like(m_s, NEG)
            acc_s[...] = jnp.zeros_like(acc_s)

        def step(masked):
            m_prev = m_s[0:2, :]
            acc_prev = [acc_s[0], acc_s[1]]
            v128 = kv_ref[:, 128:256].astype(BF16)
            s = [_fox_scores_t(q_ref, kv_ref, cc_ref, hh, masked, tq, tk)[0] for hh in range(2)]
            m_new = [jnp.maximum(m_prev[hh:hh + 1, :], jnp.max(s[hh], axis=0, keepdims=True)) for hh in range(2)]
            acc_new = []
            for hh in range(2):
                alpha = jnp.exp2(m_prev[hh:hh + 1, :] - m_new[hh])
                p = jnp.exp2(s[hh] - m_new[hh]).astype(BF16)
                acc_new.append(acc_prev[hh] * alpha + _dot_tn(_with_ones_lane(v128, hh), p))
            acc_s[0] = acc_new[0]
            acc_s[1] = acc_new[1]
            m_s[0:2, :] = jnp.concatenate(m_new, axis=0)

        @pl.when(j < i)
        def _():
            step(False)

        @pl.when(j == i)
        def _():
            step(True)
            a0, a1 = acc_s[0], acc_s[1]
            l0, l1 = a0[64:65, :], a1[0:1, :]
            o_t = jnp.concatenate([a0[0:64, :] / l0, a1[64:128, :] / l1], axis=0)
            o_ref[...] = o_t.T.astype(o_ref.dtype)
            lse_ref[0, 0] = jnp.concatenate(
                [m_s[0:1, :] + jnp.log2(l0), m_s[1:2, :] + jnp.log2(l1), jnp.zeros((6, tq), F32)], axis=0)

    return pl.pallas_call(
        body, name="fox_fwd",
        grid_spec=pltpu.PrefetchScalarGridSpec(
            num_scalar_prefetch=2, grid=(nbatch, 4, qi.shape[0]),
            in_specs=[pl.BlockSpec((tq, 128), lambda b, p, s, qi, kj: (b * nq + qi[s], COL_BQ // 128 + p)),
                      pl.BlockSpec((tk, 256), lambda b, p, s, qi, kj: (b * nq + kj[s], COL_KV // 256 + p)),
                      pl.BlockSpec((1, tk, 128), lambda b, p, s, qi, kj: (p, b * nq + kj[s], 0))],
            out_specs=[pl.BlockSpec((tq, 128), lambda b, p, s, qi, kj: (b * nq + qi[s], p)),
                       pl.BlockSpec((1, 1, 8, tq), lambda b, p, s, qi, kj: (b, p, 0, qi[s]))],
            scratch_shapes=[pltpu.VMEM((8, tq), F32), pltpu.VMEM((2, 128, tq), F32)]),
        out_shape=[jax.ShapeDtypeStruct((t, 512), BF16), jax.ShapeDtypeStruct((nbatch, 4, 8, seq), F32)],
        compiler_params=_params(("parallel", "parallel", "arbitrary"), 48),
    )(qi, kj, proj, proj, cum_cols)


def _fox_bwd(proj, cum_cols, lse, yb, dyb, dproj, nbatch, seq):
    t = proj.shape[0]
    tq = tk = min(512, seq)
    nq = seq // tq
    scale = BDH ** -0.5
    qi, kj = _causal_pairs(nq, key_major=True)
    nsteps = qi.shape[0]

    def body(qi_ref, kj_ref, q_ref, kv_ref, cc_ref, lse_ref, o_ref, do_ref, dp_in,
             dkv_ref, dq_ref, drs_ref, dcs_ref, dk_s, dv_s, dqa_s):
        del dp_in
        hp, s_id = pl.program_id(1), pl.program_id(2)
        i, j = qi_ref[s_id], kj_ref[s_id]

        @pl.when(i == j)
        def _():
            dk_s[...] = jnp.zeros_like(dk_s)
            dv_s[...] = jnp.zeros_like(dv_s)

        @pl.when(s_id == 0)
        def _():
            dqa_s[...] = jnp.zeros_like(dqa_s)

        def step(masked):
            dk_prev, dq_prev, dv_prev = [dk_s[0], dk_s[1]], [dqa_s[i, 0], dqa_s[i, 1]], dv_s[...]
            lse2 = lse_ref[0, 0, 0:2, :]
            qs128 = (q_ref[...] * scale).astype(BF16)
            k128 = kv_ref[:, 0:128].astype(BF16)
            v128 = kv_ref[:, 128:256].astype(BF16)
            do128 = do_ref[...]
            doo = do128 * o_ref[...].astype(F32)
            do16 = do128.astype(BF16)
            dk_new, dq_new, dv_new = [], [], dv_prev
            for hh in range(2):
                s, _ = _fox_scores_t(q_ref, kv_ref, cc_ref, hh, masked, tq, tk)
                p = jnp.exp2(s - lse2[hh:hh + 1, :])
                dd = lax.dot_general(jnp.ones((8, 128), F32), _head_lanes(doo, hh),
                                     (((1,), (1,)), ((), ())), preferred_element_type=F32, precision=HIGHEST)[0:1, :]
                doh = _head_lanes(do16, hh)
                dp = _dot_nt(v128, doh)
                ds = (p * (dp - dd)).astype(BF16)
                dv_new = dv_new + _dot(p, doh)
                dk_new.append(dk_prev[hh] + _dot(ds, _with_ones_lane(qs128, hh)))
                dq_new.append(dq_prev[hh] + _dot_tn(_with_ones_lane(k128, hh), ds))
            dv_s[...] = dv_new
            for hh in range(2):
                dk_s[hh] = dk_new[hh]
                dqa_s[i, hh] = dq_new[hh]

        @pl.when(i == j)
        def _():
            step(True)

        @pl.when(i > j)
        def _():
            step(False)

        @pl.when(i == nq - 1)
        def _():
            lane = lax.broadcasted_iota(jnp.int32, (tk, 128), 1)
            k0, k1 = dk_s[0], dk_s[1]
            dkv_ref[:, 0:128] = jnp.where(lane < 64, k0, k1).astype(dkv_ref.dtype)
            dkv_ref[:, 128:256] = dv_s[...].astype(dkv_ref.dtype)
            dcs_ref[0] = jnp.where(lane == 2 * hp, k0[:, 64:65], jnp.where(lane == 2 * hp + 1, k1[:, 0:1], 0.0))

        @pl.when(s_id == nsteps - 1)
        def _():
            lane = lax.broadcasted_iota(jnp.int32, (tq, 128), 1)
            for blk in range(nq):
                a0 = dqa_s[blk, 0].T
                a1 = dqa_s[blk, 1].T
                rows = pl.ds(blk * tq, tq)
                dq_ref[rows, :] = (jnp.where(lane < 64, a0, a1) * scale).astype(dq_ref.dtype)
                drs_ref[0, rows, :] = jnp.where(lane == 2 * hp, a0[:, 64:65], jnp.where(lane == 2 * hp + 1, a1[:, 0:1], 0.0))

    return pl.pallas_call(
        body, name="fox_bwd",
        grid_spec=pltpu.PrefetchScalarGridSpec(
            num_scalar_prefetch=2, grid=(nbatch, 4, nsteps),
            in_specs=[pl.BlockSpec((tq, 128), lambda b, p, s, qi, kj: (b * nq + qi[s], COL_BQ // 128 + p)),
                      pl.BlockSpec((tk, 256), lambda b, p, s, qi, kj: (b * nq + kj[s], COL_KV // 256 + p)),
                      pl.BlockSpec((1, tk, 128), lambda b, p, s, qi, kj: (p, b * nq + kj[s], 0)),
                      pl.BlockSpec((1, 1, 8, tq), lambda b, p, s, qi, kj: (b, p, 0, qi[s])),
                      pl.BlockSpec((tq, 128), lambda b, p, s, qi, kj: (b * nq + qi[s], p)),
                      pl.BlockSpec((tq, 128), lambda b, p, s, qi, kj: (b * nq + qi[s], p)),
                      pl.BlockSpec(memory_space=pl.ANY)],
            out_specs=[pl.BlockSpec((tk, 256), lambda b, p, s, qi, kj: (b * nq + kj[s], COL_KV // 256 + p)),
                       pl.BlockSpec((seq, 128), lambda b, p, s, qi, kj: (b, p)),
                       pl.BlockSpec((1, seq, 128), lambda b, p, s, qi, kj: (p, b, 0)),
                       pl.BlockSpec((1, tk, 128), lambda b, p, s, qi, kj: (p, b * nq + kj[s], 0))],
            scratch_shapes=[pltpu.VMEM((2, tk, 128), F32), pltpu.VMEM((tk, 128), F32),
                            pltpu.VMEM((nq, 2, 128, tq), F32)]),
        out_shape=[jax.ShapeDtypeStruct((t, NP), BF16), jax.ShapeDtypeStruct((t, 512), BF16),
                   jax.ShapeDtypeStruct((4, t, 128), F32), jax.ShapeDtypeStruct((4, t, 128), F32)],
        input_output_aliases={8: 0},
        compiler_params=_params(("parallel", "parallel", "arbitrary"), 56),
    )(qi, kj, proj, proj, cum_cols, lse, yb, dyb, dproj)


def _place_cols(dproj, src, col):
    t, w = src.shape
    tm = 1024 if t % 1024 == 0 else t

    def body(s_ref, dp_in, o_ref):
        del dp_in
        o_ref[...] = s_ref[...]

    return pl.pallas_call(
        body, name="place_cols", grid=(t // tm,),
        in_specs=[pl.BlockSpec((tm, w), lambda i: (i, 0)), pl.BlockSpec(memory_space=pl.ANY)],
        out_specs=pl.BlockSpec((tm, w), lambda i: (i, col // w)),
        out_shape=jax.ShapeDtypeStruct(dproj.shape, dproj.dtype),
        input_output_aliases={1: 0},
        compiler_params=_params(("parallel",)),
    )(src, dproj)


def _fox_dbf(proj, bias128, drs, dcs, dproj, nbatch, seq):
    t = proj.shape[0]
    ts = min(512, seq)
    nb = seq // ts

    def body(p_ref, b_ref, dr_ref, dc_ref, dp_in, dp_ref, sm_ref, carry):
        del dp_in
        b_id, i = pl.program_id(0), pl.program_id(1)

        @pl.when(i == 0)
        def _():
            carry[...] = jnp.zeros_like(carry)

        dcum = (dr_ref[0] - dc_ref[0]) + (dr_ref[1] - dc_ref[1]) + (dr_ref[2] - dc_ref[2]) + (dr_ref[3] - dc_ref[3])
        rc = _dot_f32(_tri(ts, True), dcum) + carry[...]
        carry[...] = rc[0:1, :]
        z = p_ref[...] + b_ref[...]
        lane = lax.broadcasted_iota(jnp.int32, (ts, 128), 1)
        dz = jnp.where(lane < BH, rc * jax.nn.sigmoid(-z), 0.0)
        dp_ref[...] = dz.astype(dp_ref.dtype)
        upd = jnp.concatenate([jnp.sum(dz, axis=0, keepdims=True), jnp.zeros((7, 128), F32)], axis=0)
        first = (b_id == 0) & (i == 0)

        @pl.when(first)
        def _():
            sm_ref[...] = upd

        @pl.when(jnp.logical_not(first))
        def _():
            sm_ref[...] += upd

    def rows(b, i):
        return b * nb + (nb - 1 - i)

    return pl.pallas_call(
        body, name="fox_dbf", grid=(nbatch, nb),
        in_specs=[pl.BlockSpec((ts, 128), lambda b, i: (rows(b, i), 0)),
                  pl.BlockSpec((1, 128), lambda b, i: (0, 0)),
                  pl.BlockSpec((4, ts, 128), lambda b, i: (0, rows(b, i), 0)),
                  pl.BlockSpec((4, ts, 128), lambda b, i: (0, rows(b, i), 0)),
                  pl.BlockSpec(memory_space=pl.ANY)],
        out_specs=[pl.BlockSpec((ts, 128), lambda b, i: (rows(b, i), COL_BF // 128)),
                   pl.BlockSpec((8, 128), lambda b, i: (0, 0))],
        out_shape=[jax.ShapeDtypeStruct((t, NP), BF16), jax.ShapeDtypeStruct((8, 128), F32)],
        input_output_aliases={4: 0},
        scratch_shapes=[pltpu.VMEM((1, 128), F32)],
        compiler_params=_params(("arbitrary", "arbitrary")),
    )(proj, bias128, drs, dcs, dproj)


def _ln_stats(z):
    mu = jnp.mean(z, axis=-1, keepdims=True)
    zc = z - mu
    rstd = lax.rsqrt(jnp.mean(zc * zc, axis=-1, keepdims=True) + LN_EPS)
    return zc * rstd, rstd


def _ln_bwd(dy, xhat, rstd, w):
    dxh = dy * w
    return rstd * (dxh - jnp.mean(dxh, axis=-1, keepdims=True) - xhat * jnp.mean(dxh * xhat, axis=-1, keepdims=True))


def _merge_fwd(ya, yb, proj, x2, mod8, wba, wbb, wout, ln1w, ln1b, seq):
    t = x2.shape[0]
    tm = min(512, seq)
    tpb = seq // tm

    def body(ya_ref, yb_ref, g_ref, x_ref, mod_ref, wa_ref, wb_ref, wo_ref, lw_ref, lb_ref, mg_ref, u_ref, x1_ref):
        ga = jax.nn.sigmoid(g_ref[:, 0:D].astype(F32))
        gb = jax.nn.sigmoid(g_ref[:, D:2 * D].astype(F32))
        merged = (ga * jnp.dot(ya_ref[...], wa_ref[...], preferred_element_type=F32)
                  + gb * jnp.dot(yb_ref[...], wb_ref[...], preferred_element_type=F32))
        mg = merged.astype(BF16)
        mg_ref[...] = mg
        u = jnp.dot(mg, wo_ref[...], preferred_element_type=F32)
        u_ref[...] = u
        xhat, _ = _ln_stats(ALPHA * x_ref[...] + (1.0 + mod_ref[0, 2:3, :]) * u)
        x1_ref[...] = xhat * lw_ref[...] + lb_ref[...]

    tok = lambda w: pl.BlockSpec((tm, w), lambda i: (i, 0))
    full = lambda a: pl.BlockSpec(a.shape, lambda i: (0,) * a.ndim)
    return pl.pallas_call(
        body, name="merge_fwd", grid=(t // tm,),
        in_specs=[tok(512), tok(512), pl.BlockSpec((tm, 2048), lambda i: (i, COL_GATES // 2048)), tok(D),
                  pl.BlockSpec((1, 8, D), lambda i: (i // tpb, 0, 0)),
                  full(wba), full(wbb), full(wout), full(ln1w), full(ln1b)],
        out_specs=[tok(D), tok(D), tok(D)],
        out_shape=[jax.ShapeDtypeStruct((t, D), BF16), jax.ShapeDtypeStruct((t, D), F32),
                   jax.ShapeDtypeStruct((t, D), F32)],
        compiler_params=_params(("parallel",), 48),
    )(ya, yb, proj, x2, mod8, wba, wbb, wout, ln1w, ln1b)


def _merge_bwd(du, ya, yb, proj, wba, wbb, wout, token, seq):
    t = du.shape[0]
    tm = min(512, seq)

    def body(du_ref, ya_ref, yb_ref, g_ref, wa_ref, wb_ref, wo_ref, token_ref,
             dp_ref, dpa_ref, dpb_ref, dya_ref, dyb_ref):
        del token_ref
        ga = jax.nn.sigmoid(g_ref[:, 0:D].astype(F32))
        gb = jax.nn.sigmoid(g_ref[:, D:2 * D].astype(F32))
        dm = _dot_nt(du_ref[...], wo_ref[...])
        pa = jnp.dot(ya_ref[...], wa_ref[...], preferred_element_type=F32)
        pb = jnp.dot(yb_ref[...], wb_ref[...], preferred_element_type=F32)
        dpa = (dm * ga).astype(BF16)
        dpb = (dm * gb).astype(BF16)
        dpa_ref[...] = dpa
        dpb_ref[...] = dpb
        dp_ref[:, 0:D] = (dm * pa * ga * (1.0 - ga)).astype(BF16)
        dp_ref[:, D:2 * D] = (dm * pb * gb * (1.0 - gb)).astype(BF16)
        dya_ref[...] = _dot_nt(dpa, wa_ref[...])
        dyb_ref[...] = _dot_nt(dpb, wb_ref[...])

    tok = lambda w: pl.BlockSpec((tm, w), lambda i: (i, 0))
    full = lambda a: pl.BlockSpec(a.shape, lambda i: (0,) * a.ndim)
    return pl.pallas_call(
        body, name="merge_bwd", grid=(t // tm,),
        in_specs=[tok(D), tok(512), tok(512), pl.BlockSpec((tm, 2048), lambda i: (i, COL_GATES // 2048)),
                  full(wba), full(wbb), full(wout), full(token)],
        out_specs=[pl.BlockSpec((tm, 2048), lambda i: (i, COL_GATES // 2048)), tok(D), tok(D), tok(512), tok(512)],
        out_shape=[jax.ShapeDtypeStruct((t, NP), BF16), jax.ShapeDtypeStruct((t, D), BF16),
                   jax.ShapeDtypeStruct((t, D), BF16), jax.ShapeDtypeStruct((t, 512), F32),
                   jax.ShapeDtypeStruct((t, 512), F32)],
        compiler_params=_params(("parallel",), 48),
    )(du, ya, yb, proj, wba, wbb, wout, token)


def _ffn_fwd(x1, mod8, wg, wu, wd, target, ln2w, ln2b, seq):
    t = x1.shape[0]
    tm = min(FFN_TOKENS, seq)
    nf, _, tf = wg.shape
    tpb = seq // tm
    nbatch = t // seq

    def body(x_ref, mod_ref, wg_ref, wu_ref, wd_ref, t_ref, lw_ref, lb_ref,
             a_ref, b_ref, h_s, dz_ref, st_ref, dm_ref, acc):
        i, j = pl.program_id(0), pl.program_id(1)

        @pl.when(j == 0)
        def _():
            h_s[...] = (x_ref[...] * (1.0 + mod_ref[0, 4:5, :]) + mod_ref[0, 3:4, :]).astype(BF16)
            acc[...] = jnp.zeros_like(acc)

        a = jnp.dot(h_s[...], wg_ref[0], preferred_element_type=F32)
        b = jnp.dot(h_s[...], wu_ref[0], preferred_element_type=F32)
        a_ref[0] = a.astype(BF16)
        b_ref[0] = b.astype(BF16)
        acc[...] += _dot(a * jax.nn.sigmoid(a) * b, wd_ref[0])

        @pl.when(j == nf - 1)
        def _():
            ffn = acc[...]
            xhat, rstd = _ln_stats(ALPHA * x_ref[...] + (1.0 + mod_ref[0, 5:6, :]) * ffn)
            diff = xhat * lw_ref[...] + lb_ref[...] - t_ref[...]
            loss = 0.5 * jnp.sum(jnp.sum(diff * diff, axis=-1, keepdims=True), axis=0, keepdims=True) / D
            dy = diff * (1.0 / D)
            dz = _ln_bwd(dy, xhat, rstd, lw_ref[...])
            dz_ref[...] = dz
            lane = lax.broadcasted_iota(jnp.int32, (1, D), 1)
            upd = jnp.concatenate(
                [jnp.sum(dy * xhat, axis=0, keepdims=True), jnp.sum(dy, axis=0, keepdims=True),
                 jnp.where(lane == 0, loss, 0.0), jnp.zeros((5, D), F32)], axis=0)
            dmu = jnp.concatenate(
                [jnp.zeros((5, D), F32), jnp.sum(dz * ffn, axis=0, keepdims=True), jnp.zeros((2, D), F32)], axis=0)

            @pl.when(i == 0)
            def _():
                st_ref[...] = upd

            @pl.when(i > 0)
            def _():
                st_ref[...] += upd

            @pl.when(i % tpb == 0)
            def _():
                dm_ref[0] = dmu

            @pl.when(i % tpb != 0)
            def _():
                dm_ref[0] += dmu

    row = lambda: pl.BlockSpec((tm, D), lambda i, j: (i, 0))
    vec = lambda: pl.BlockSpec((1, D), lambda i, j: (0, 0))
    return pl.pallas_call(
        body, name="ffn_fwd", grid=(t // tm, nf),
        in_specs=[row(), pl.BlockSpec((1, 8, D), lambda i, j: (i // tpb, 0, 0)),
                  pl.BlockSpec((1, D, tf), lambda i, j: (j, 0, 0)), pl.BlockSpec((1, D, tf), lambda i, j: (j, 0, 0)),
                  pl.BlockSpec((1, tf, D), lambda i, j: (j, 0, 0)), row(), vec(), vec()],
        out_specs=[pl.BlockSpec((1, tm, tf), lambda i, j: (j, i, 0)), pl.BlockSpec((1, tm, tf), lambda i, j: (j, i, 0)),
                   row(), row(), pl.BlockSpec((8, D), lambda i, j: (0, 0)),
                   pl.BlockSpec((1, 8, D), lambda i, j: (i // tpb, 0, 0))],
        out_shape=[jax.ShapeDtypeStruct((nf, t, tf), BF16), jax.ShapeDtypeStruct((nf, t, tf), BF16),
                   jax.ShapeDtypeStruct((t, D), BF16),
                   jax.ShapeDtypeStruct((t, D), F32), jax.ShapeDtypeStruct((8, D), F32),
                   jax.ShapeDtypeStruct((nbatch, 8, D), F32)],
        scratch_shapes=[pltpu.VMEM((tm, D), F32)],
        compiler_params=_params(("arbitrary", "arbitrary"), 60),
    )(x1, mod8, wg, wu, wd, target, ln2w, ln2b)


def _ffn_bwd(dz2, a, b, wg, wu, wd, x1, x2, u, mod8, ln1w, seq):
    t = x1.shape[0]
    tm = min(512, seq)
    nf, tf, _ = wg.shape
    tpb = seq // tm
    nbatch = t // seq

    def body(dz_ref, a_ref, b_ref, wg_ref, wu_ref, wd_ref, x1_ref, x_ref, u_ref, mod_ref, lw_ref,
             da_ref, db_ref, hm_ref, df_ref, du_ref, dxp_ref, st_ref, dm_ref, acc):
        i, j = pl.program_id(0), pl.program_id(1)

        @pl.when(j == 0)
        def _():
            df_ref[...] = ((1.0 + mod_ref[0, 5:6, :]) * dz_ref[...]).astype(BF16)
            acc[...] = jnp.zeros_like(acc)

        dhm = _dot(df_ref[...], wd_ref[0])
        av = a_ref[0].astype(F32)
        bv = b_ref[0].astype(F32)
        sg = jax.nn.sigmoid(av)
        sl = av * sg
        hm_ref[0] = (sl * bv).astype(BF16)
        da = (dhm * bv * (sg * (1.0 + av * (1.0 - sg)))).astype(BF16)
        db = (dhm * sl).astype(BF16)
        da_ref[0] = da
        db_ref[0] = db
        acc[...] += _dot(da, wg_ref[0]) + _dot(db, wu_ref[0])

        @pl.when(j == nf - 1)
        def _():
            dh2 = acc[...]
            x1v = x1_ref[...]
            uv = u_ref[...]
            dx1 = ALPHA * dz_ref[...] + dh2 * (1.0 + mod_ref[0, 4:5, :])
            xhat, rstd = _ln_stats(ALPHA * x_ref[...] + (1.0 + mod_ref[0, 2:3, :]) * uv)
            dz1 = _ln_bwd(dx1, xhat, rstd, lw_ref[...])
            du_ref[...] = ((1.0 + mod_ref[0, 2:3, :]) * dz1).astype(BF16)
            dxp_ref[...] = ALPHA * dz1
            upd = jnp.concatenate(
                [jnp.sum(dx1 * xhat, axis=0, keepdims=True), jnp.sum(dx1, axis=0, keepdims=True),
                 jnp.zeros((6, D), F32)], axis=0)
            dmu = jnp.concatenate(
                [jnp.zeros((2, D), F32), jnp.sum(dz1 * uv, axis=0, keepdims=True),
                 jnp.sum(dh2, axis=0, keepdims=True), jnp.sum(dh2 * x1v, axis=0, keepdims=True),
                 jnp.zeros((3, D), F32)], axis=0)

            @pl.when(i == 0)
            def _():
                st_ref[...] = upd

            @pl.when(i > 0)
            def _():
                st_ref[...] += upd

            @pl.when(i % tpb == 0)
            def _():
                dm_ref[0] = dmu

            @pl.when(i % tpb != 0)
            def _():
                dm_ref[0] += dmu

    row = lambda: pl.BlockSpec((tm, D), lambda i, j: (i, 0))
    ffb = lambda: pl.BlockSpec((1, tm, tf), lambda i, j: (j, i, 0))
    return pl.pallas_call(
        body, name="ffn_bwd", grid=(t // tm, nf),
        in_specs=[row(), ffb(), ffb(),
                  pl.BlockSpec((1, tf, D), lambda i, j: (j, 0, 0)), pl.BlockSpec((1, tf, D), lambda i, j: (j, 0, 0)),
                  pl.BlockSpec((1, D, tf), lambda i, j: (j, 0, 0)), row(), row(), row(),
                  pl.BlockSpec((1, 8, D), lambda i, j: (i // tpb, 0, 0)), pl.BlockSpec((1, D), lambda i, j: (0, 0))],
        out_specs=[ffb(), ffb(), ffb(), row(), row(), row(), pl.BlockSpec((8, D), lambda i, j: (0, 0)),
                   pl.BlockSpec((1, 8, D), lambda i, j: (i // tpb, 0, 0))],
        out_shape=[jax.ShapeDtypeStruct((nf, t, tf), BF16), jax.ShapeDtypeStruct((nf, t, tf), BF16),
                   jax.ShapeDtypeStruct((nf, t, tf), BF16), jax.ShapeDtypeStruct((t, D), BF16),
                   jax.ShapeDtypeStruct((t, D), BF16), jax.ShapeDtypeStruct((t, D), F32),
                   jax.ShapeDtypeStruct((8, D), F32), jax.ShapeDtypeStruct((nbatch, 8, D), F32)],
        scratch_shapes=[pltpu.VMEM((tm, D), F32)],
        compiler_params=_params(("arbitrary", "arbitrary"), 48),
    )(dz2, a, b, wg, wu, wd, x1, x2, u, mod8, ln1w)


def _adamw_math(w, g, m, v):
    m = B1 * m + (1.0 - B1) * g
    v = B2 * v + (1.0 - B2) * (g * g)
    m_hat = m / (1.0 - B1 ** STEP)
    v_hat = v / (1.0 - B2 ** STEP)
    return -LR * (m_hat / (jnp.sqrt(v_hat) + EPS) + WD * w), m, v


def _adamw(w, g, m, v, name):
    rows, cols = w.shape
    tr = rows
    for cand in (128, 64, 32, 16, 8):
        if rows % cand == 0:
            tr = cand
            break

    def body(w_ref, g_ref, m_ref, v_ref, d_ref, mo_ref, vo_ref):
        d, mn, vn = _adamw_math(w_ref[...], g_ref[...], m_ref[...], v_ref[...])
        d_ref[...] = d
        mo_ref[...] = mn
        vo_ref[...] = vn

    spec = pl.BlockSpec((tr, cols), lambda i: (i, 0))
    return pl.pallas_call(
        body, name=name, grid=(rows // tr,), in_specs=[spec] * 4, out_specs=[spec] * 3,
        out_shape=[jax.ShapeDtypeStruct((rows, cols), F32)] * 3,
        compiler_params=_params(("parallel",), 48),
    )(w, g, m, v)


def _adamw_halves(w, g_mine, g_sib, m, v, c_idx, name):
    rows, cols = w.shape
    hr = rows // 2
    tr = next(cand for cand in (128, 88, 64, 32, 16, 8) if hr % cand == 0)
    tph = hr // tr

    def body(c_ref, w_ref, gm_ref, gs_ref, m_ref, v_ref, g_ref, d_ref, mo_ref, vo_ref):
        g = jnp.where(pl.program_id(0) == c_ref[0], gm_ref[...], gs_ref[...])
        d, mn, vn = _adamw_math(w_ref[...], g, m_ref[...], v_ref[...])
        g_ref[...] = g
        d_ref[...] = d
        mo_ref[...] = mn
        vo_ref[...] = vn

    full = pl.BlockSpec((tr, cols), lambda h, i, c: (h * tph + i, 0))
    half = pl.BlockSpec((tr, cols), lambda h, i, c: (i, 0))
    return pl.pallas_call(
        body, name=name,
        grid_spec=pltpu.PrefetchScalarGridSpec(
            num_scalar_prefetch=1, grid=(2, tph), in_specs=[full, half, half, full, full], out_specs=[full] * 4),
        out_shape=[jax.ShapeDtypeStruct((rows, cols), F32)] * 4,
        compiler_params=_params(("parallel", "parallel"), 48),
    )(c_idx, w, g_mine, g_sib, m, v)


def _grad_w_ada(c_all, dmod_cols):
    def body(c_ref, d_ref, o_ref):
        c = c_ref[...]
        o_ref[...] = lax.dot_general(c * jax.nn.sigmoid(c), d_ref[...], (((0,), (0,)), ((), ())),
                                     preferred_element_type=F32, precision=HIGHEST)

    return pl.pallas_call(
        body, name="grad_w_ada", out_shape=jax.ShapeDtypeStruct((D, dmod_cols.shape[1]), F32),
        compiler_params=_params(vmem_mb=48),
    )(c_all, dmod_cols)


def _small_update(gath, w8, m8, v8):
    def body(g_ref, w_ref, m_ref, v_ref, go_ref, d_ref, mo_ref, vo_ref):
        g0 = g_ref[0, 0:1, :] + g_ref[0, 1:2, :]
        g1 = g_ref[0, 2:3, :]
        for dev in range(1, N_DEV):
            g0 = g0 + (g_ref[dev, 0:1, :] + g_ref[dev, 1:2, :])
            g1 = g1 + g_ref[dev, 2:3, :]
        w = w_ref[...]
        lb = jax.nn.sigmoid(w[1:2, O_LB0:O_LB1] - w[1:2, O_LB1:O_FOX])
        fac = lb * (1.0 - lb)
        g1 = jnp.concatenate([g1[:, :O_LB0], g1[:, O_LB0:O_LB1] * fac, -g1[:, O_LB1:O_FOX] * fac, g1[:, O_FOX:]],
                             axis=1)
        g = jnp.concatenate([g0, g1, jnp.zeros((6, SMALL_W), F32)], axis=0)
        d, mn, vn = _adamw_math(w, g, m_ref[...], v_ref[...])
        go_ref[...] = g
        d_ref[...] = d
        mo_ref[...] = mn
        vo_ref[...] = vn

    return pl.pallas_call(
        body, name="small_update", out_shape=[jax.ShapeDtypeStruct((8, SMALL_W), F32)] * 4,
        compiler_params=_params(vmem_mb=48),
    )(gath, w8, m8, v8)


def _pack_small(b_ada, ln1w, ln1b, ln2w, ln2b, norm_w, lb_logits, fox):
    row1 = jnp.concatenate([ln1w, ln1b, ln2w, ln2b, norm_w, lb_logits[0:1], lb_logits[1:2], fox,
                            jnp.zeros((1, SMALL_W - O_FOX - BH), F32)], axis=1)
    return jnp.concatenate([b_ada, row1, jnp.zeros((6, SMALL_W), F32)], axis=0)


def _unpack_small(p):
    r = p[1:2]
    lb = jnp.concatenate([r[:, O_LB0:O_LB1], r[:, O_LB1:O_FOX]], axis=0)
    return dict(b_ada=p[0:1], ln1_w=r[:, O_LN1W:O_LN1B], ln1_b=r[:, O_LN1B:O_LN2W], ln2_w=r[:, O_LN2W:O_LN2B],
                ln2_b=r[:, O_LN2B:O_NORM], hgrn_norm_w=r[:, O_NORM:O_LB0], lb_logits=lb,
                fox_f_bias=r[:, O_FOX:O_FOX + BH])


_BIG = ("w_in", "w_branch_a", "w_branch_b", "w_out", "w_ffn_gate", "w_ffn_up", "w_ffn_down")
_TRANSPOSED = ("w_ffn_gate", "w_ffn_up")


def _cols_of_chips(stacked):
    return jnp.concatenate([stacked[k] for k in range(N_CHIPS)], axis=1)


def kernel(x, c, w_ada, b_ada, w_in, fox_f_bias, lb_logits, hgrn_norm_w, w_branch_a, w_branch_b, w_out, ln1_w, ln1_b, w_ffn_gate, w_ffn_up, w_ffn_down, ln2_w, ln2_b, loss_target, m_w_ada, m_b_ada, m_w_in, m_fox_f_bias, m_lb_logits, m_hgrn_norm_w, m_w_branch_a, m_w_branch_b, m_w_out, m_ln1_w, m_ln1_b, m_w_ffn_gate, m_w_ffn_up, m_w_ffn_down, m_ln2_w, m_ln2_b, v_w_ada, v_b_ada, v_w_in, v_fox_f_bias, v_lb_logits, v_hgrn_norm_w, v_w_branch_a, v_w_branch_b, v_w_out, v_ln1_w, v_ln1_b, v_w_ffn_gate, v_w_ffn_up, v_w_ffn_down, v_ln2_w, v_ln2_b):
    nbatch, seq, _ = x.shape
    t = nbatch * seq
    ax, ay, ac = lax.axis_index("x"), lax.axis_index("y"), lax.axis_index("c")
    chip = 2 * ax + ay
    dev = 2 * chip + ac
    chip_arr = jnp.reshape(chip, (1,)).astype(jnp.int32)
    core_arr = jnp.reshape(ac, (1,)).astype(jnp.int32)

    tr = lambda a: jnp.swapaxes(a[0], 0, 1)
    shard_w = dict(w_in=w_in[0], w_branch_a=w_branch_a[0], w_branch_b=w_branch_b[0], w_out=w_out[0],
                   w_ffn_gate=tr(w_ffn_gate), w_ffn_up=tr(w_ffn_up), w_ffn_down=w_ffn_down[0])
    shard_m = dict(w_in=m_w_in[0], w_branch_a=m_w_branch_a[0], w_branch_b=m_w_branch_b[0], w_out=m_w_out[0],
                   w_ffn_gate=tr(m_w_ffn_gate), w_ffn_up=tr(m_w_ffn_up), w_ffn_down=m_w_ffn_down[0])
    shard_v = dict(w_in=v_w_in[0], w_branch_a=v_w_branch_a[0], w_branch_b=v_w_branch_b[0], w_out=v_w_out[0],
                   w_ffn_gate=tr(v_w_ffn_gate), w_ffn_up=tr(v_w_ffn_up), w_ffn_down=v_w_ffn_down[0])

    shard16 = {n: shard_w[n].astype(BF16) for n in _BIG}

    def with_mine(gathered, n):
        return lax.dynamic_update_slice(gathered, shard16[n][None], (chip, 0, 0))

    w_p = _permute_cols(_cols_of_chips(with_mine(_gather_weights([shard16["w_in"]])[0], "w_in")))
    late = _BIG[1:]
    late_send, late_recv, late_src, late_land, late_token = _split_start(
        _gather_copies, [shard16[n] for n in late],
        [lax.empty((N_CHIPS,) + shard16[n].shape, BF16) for n in late], "gather_late_start")

    c8 = jnp.concatenate([c, jnp.zeros((8 - nbatch, D), F32)], axis=0)
    c_all = _allgather8(c8, "gather_c")[:, :nbatch, :].reshape(N_DEV * nbatch, D)
    ncol = w_ada.shape[2]
    b_cols = lax.dynamic_slice_in_dim(b_ada, chip * ncol, ncol, axis=1)
    mod_g = _allgather8(_mod_shard(c_all, w_ada[0], b_cols), "gather_mod")
    mod_all = jnp.concatenate([mod_g[2 * k] for k in range(N_CHIPS)], axis=1)
    mod_mine = lax.dynamic_slice_in_dim(mod_all, dev * nbatch, nbatch, axis=0)
    mod8 = jnp.concatenate([mod_mine.reshape(nbatch, 6, D), jnp.zeros((nbatch, 2, D), F32)], axis=1)
    mod8 = mod8 + late_token[0, 0]

    x2 = x.reshape(t, D)
    tgt2 = loss_target.reshape(t, D)
    bias128 = jnp.concatenate([fox_f_bias, jnp.zeros((1, 128 - BH), F32)], axis=1)

    proj, h16 = _proj(x2, mod8, w_p, seq, BF16, "proj")
    projf = _rows_matmul(h16, w_p[:, COL_BF:], "proj_forget")
    ya, ckpt = _hgrn_fwd(proj, lb_logits, hgrn_norm_w, nbatch, seq)
    cum_cols = _fox_cum(projf, bias128, nbatch, seq)
    yb, lse = _fox_fwd(proj, cum_cols, nbatch, seq)
    late_land = _pass_to_sibling(
        _split_wait(_gather_copies, late_send, late_recv, late_src, late_land, yb, "gather_late_wait"))
    full = {n: with_mine(g, n) for n, g in zip(late, late_land)}
    wba, wbb = _cols_of_chips(full["w_branch_a"]), _cols_of_chips(full["w_branch_b"])
    wout = full["w_out"].reshape(D, D)
    wg_t, wu_t, wd = full["w_ffn_gate"], full["w_ffn_up"], full["w_ffn_down"]
    wg, wu, wd_t = jnp.swapaxes(wg_t, 1, 2), jnp.swapaxes(wu_t, 1, 2), jnp.swapaxes(wd, 1, 2)
    merged, u, x1 = _merge_fwd(ya, yb, proj, x2, mod8, wba, wbb, wout, ln1_w, ln1_b, seq)
    a_pre, b_pre, h2, dz2, st2, dm2 = _ffn_fwd(x1, mod8, wg, wu, wd, tgt2, ln2_w, ln2_b, seq)
    loss = lax.psum(st2[2, 0], ("x", "y", "c"))

    da, db, hmid, dffn, du, dxp, st1, dm1 = _ffn_bwd(dz2, a_pre, b_pre, wg_t, wu_t, wd_t, x1, x2, u, mod8, ln1_w, seq)
    g_st = {}
    g_st["w_ffn_down"] = _tn_matmul(hmid, dffn, "dw_ffn_down", seq)
    g_st["w_ffn_gate"] = _tn_matmul(da, h2, "dw_ffn_gate", seq)
    g_st["w_ffn_up"] = _tn_matmul(db, h2, "dw_ffn_up", seq)
    g_st["w_out"] = _tn_matmul(merged, du, "dw_out", seq).reshape(N_CHIPS, D // N_CHIPS, D)

    def sum_over_cores(names, tag):
        g_list = [g_st[n] for n in names]
        return [_add_my_half(g, o, core_arr, "grad_add_halves_" + n)
                for n, g, o in zip(names, g_list, _swap_halves(g_list, "grad_swap_halves_" + tag))]

    early = ("w_ffn_down", "w_ffn_gate", "w_ffn_up", "w_out")
    e_halves = sum_over_cores(early, "early")
    e_send, e_recv, e_src, e_land, e_token = _split_start(
        _scatter_copies, [h16 for _, h16 in e_halves],
        [lax.empty((3,) + h16.shape[1:], BF16) for _, h16 in e_halves], "grad_scatter_early_start")
    dproj, dpa, dpb, dya, dyb = _merge_bwd(du, ya, yb, proj, wba, wbb, wout, e_token, seq)
    g_st["w_branch_a"] = _tn_matmul(ya, dpa, "dw_branch_a", seq, split=D // N_CHIPS)
    g_st["w_branch_b"] = _tn_matmul(yb, dpb, "dw_branch_b", seq, split=D // N_CHIPS)
    dproj, dq, drs, dcs = _fox_bwd(proj, cum_cols, lse, yb, dyb, dproj, nbatch, seq)
    dproj = _place_cols(dproj, dq, COL_BQ)
    dproj, sm_fox = _fox_dbf(projf, bias128, drs, dcs, dproj, nbatch, seq)
    dproj, sm_hgrn = _hgrn_bwd(proj, dya, ckpt, lb_logits, hgrn_norm_w, dproj, nbatch, seq)
    grad_x2, dm0 = _dh_kernel(dproj, w_p, x2, dxp, mod8, seq)
    dw_in = _unpermute_cols(_tn_matmul(h16, dproj, "dw_in", seq))
    ncin = NIN // N_CHIPS
    g_st["w_in"] = jnp.stack([dw_in[:, k * ncin:(k + 1) * ncin] for k in range(N_CHIPS)])

    e_recv = _split_wait(_scatter_copies, e_send, e_recv, e_src, e_land, dw_in, "grad_scatter_early_wait")
    rest = ("w_in", "w_branch_a", "w_branch_b")
    r_halves = sum_over_cores(rest, "rest")
    r_send, r_rcv, r_src, r_land, r_token = _split_start(
        _scatter_copies, [h16 for _, h16 in r_halves],
        [lax.empty((3,) + h16.shape[1:], BF16) for _, h16 in r_halves], "grad_scatter_rest_start")

    def finish(names, halves, recv, token, tag):
        g_mine = [_add_chips(h32, r, chip_arr, "grad_add_chips_" + n) for n, (h32, _), r in zip(names, halves, recv)]
        g_sib = _join_halves(g_mine, token, "grad_join_halves_" + tag)
        for n, gm, gs in zip(names, g_mine, g_sib):
            grads[n], deltas[n], new_m[n], new_v[n] = _adamw_halves(
                shard_w[n], gm, gs, shard_m[n], shard_v[n], core_arr, "adamw_" + n)

    grads, deltas, new_m, new_v = {}, {}, {}, {}
    finish(early, e_halves, e_recv, r_token, "early")

    dmod = (dm0 + dm1 + dm2)[:, :6, :].reshape(nbatch, 6 * D)
    row2 = jnp.concatenate([st1[0:1], st1[1:2], st2[0:1], st2[1:2], sm_hgrn[1:2], sm_hgrn[0:1], sm_hgrn[0:1],
                            sm_fox[0:1, :BH], jnp.zeros((1, SMALL_W - O_FOX - BH), F32)], axis=1)
    spack = jnp.concatenate([dmod, row2, jnp.zeros((8 - nbatch - 1, SMALL_W), F32)], axis=0)
    spack = spack + r_token[0, 0]
    gath = _allgather8(spack, "gather_small")
    w8 = _pack_small(b_ada, ln1_w, ln1_b, ln2_w, ln2_b, hgrn_norm_w, lb_logits, fox_f_bias)
    m8 = _pack_small(m_b_ada, m_ln1_w, m_ln1_b, m_ln2_w, m_ln2_b, m_hgrn_norm_w, m_lb_logits, m_fox_f_bias)
    v8 = _pack_small(v_b_ada, v_ln1_w, v_ln1_b, v_ln2_w, v_ln2_b, v_hgrn_norm_w, v_lb_logits, v_fox_f_bias)
    sg, sd, smn, svn = (_unpack_small(p) for p in _small_update(gath, w8, m8, v8))
    dmod_all = gath[:, :nbatch, :].reshape(N_DEV * nbatch, SMALL_W)
    g_ada = _grad_w_ada(c_all, lax.dynamic_slice_in_dim(dmod_all, chip * ncol, ncol, axis=1))

    for group, small in zip((grads, deltas, new_m, new_v), (sg, sd, smn, svn)):
        group.update(small)
    grads["w_ada"] = g_ada
    deltas["w_ada"], new_m["w_ada"], new_v["w_ada"] = _adamw(w_ada[0], g_ada, m_w_ada[0], v_w_ada[0], "adamw_w_ada")
    done = sum(new_v[n][0:8, 0:128] for n in early) + new_v["w_ada"][0:8, 0:128]
    r_recv = _split_wait(_scatter_copies, r_send, r_rcv, r_src, r_land, done, "grad_scatter_rest_wait")
    finish(rest, r_halves, r_recv, late_token, "rest")

    names = ["w_ada", "b_ada", "w_in", "fox_f_bias", "lb_logits", "hgrn_norm_w", "w_branch_a", "w_branch_b", "w_out",
             "ln1_w", "ln1_b", "w_ffn_gate", "w_ffn_up", "w_ffn_down", "ln2_w", "ln2_b"]
    shapes = dict(w_ada=w_ada.shape, b_ada=b_ada.shape, w_in=w_in.shape, fox_f_bias=fox_f_bias.shape,
                  lb_logits=lb_logits.shape, hgrn_norm_w=hgrn_norm_w.shape, w_branch_a=w_branch_a.shape,
                  w_branch_b=w_branch_b.shape, w_out=w_out.shape, ln1_w=ln1_w.shape, ln1_b=ln1_b.shape,
                  w_ffn_gate=w_ffn_gate.shape, w_ffn_up=w_ffn_up.shape, w_ffn_down=w_ffn_down.shape,
                  ln2_w=ln2_w.shape, ln2_b=ln2_b.shape)
    outs = [loss, grad_x2.reshape(x.shape)]
    for group in (grads, deltas, new_m, new_v):
        outs += [(jnp.swapaxes(group[n], 0, 1) if n in _TRANSPOSED else group[n]).reshape(shapes[n]) for n in names]
    return tuple(outs)
```

```python
import functools

import jax
import jax.numpy as jnp
from jax import lax
from jax.experimental import pallas as pl
from jax.experimental.pallas import tpu as pltpu

F32 = jnp.float32
BF16 = jnp.bfloat16
MESH = pl.DeviceIdType.MESH
HIGHEST = lax.Precision.HIGHEST

D = 1024
AW = 512
AH = 4
ADH = 128
BH = 8
BDH = 64
DFF = 2816
NIN = 5640
NP = 5760
N_CHIPS = 4
N_DEV = 8
HGRN_BLOCK = 256
FFN_TOKENS = 512
COL_GATES = 0
COL_A = 2048
COL_BQ = 4096
COL_KV = 4608
COL_BF = 5632
HGRN_HEADS = 4
FOX_PAIRS = 2
ALPHA = 2.0 ** 0.25
LN_EPS = 1e-5
RMS_EPS = 1e-6
NEG = -1e30
LOG2E = 1.4426950408889634
LR, B1, B2, EPS, WD, STEP = 0.001, 0.9, 0.999, 1e-08, 0.01, 10
SMALL_W = 6144
O_LN1W, O_LN1B, O_LN2W, O_LN2B, O_NORM, O_LB0, O_LB1, O_FOX = 0, 1024, 2048, 3072, 4096, 4608, 5120, 5632


def _params(sem=None, vmem_mb=None):
    kw = {}
    if sem is not None:
        kw["dimension_semantics"] = sem
    if vmem_mb is not None:
        kw["vmem_limit_bytes"] = vmem_mb << 20
    return pltpu.CompilerParams(**kw)


def _dot(a, b):
    return jnp.dot(a.astype(BF16), b.astype(BF16), preferred_element_type=F32)


def _dot_nt(a, b):
    return lax.dot_general(a.astype(BF16), b.astype(BF16), (((1,), (1,)), ((), ())), preferred_element_type=F32)


def _dot_tn(a, b):
    return lax.dot_general(a.astype(BF16), b.astype(BF16), (((0,), (0,)), ((), ())), preferred_element_type=F32)


def _dot_f32(a, b):
    return jnp.dot(a, b, preferred_element_type=F32, precision=HIGHEST)


def _perm_segments():
    segs = [(3592, 5640)]
    for h in range(4):
        segs += [(128 * h + 512 * t, 128 * h + 512 * t + 128) for t in range(4)]
    segs += [(2048, 2560)]
    for p in range(4):
        segs += [(2560 + 128 * p, 2688 + 128 * p), (3072 + 128 * p, 3200 + 128 * p)]
    segs += [(3584, 3592)]
    return segs


def _permute_cols(w):
    parts = [w[:, a:b] for a, b in _perm_segments()]
    parts.append(jnp.zeros((w.shape[0], NP - NIN), w.dtype))
    return jnp.concatenate(parts, axis=1)


def _unpermute_cols(g):
    pos, where = 0, {}
    for a, b in _perm_segments():
        where[a] = (pos, pos + b - a)
        pos += b - a
    parts = [g[:, where[a][0]:where[a][1]] for a in sorted(where)]
    return jnp.concatenate(parts, axis=1)


def _allgather8(v, name):
    rows, cols = v.shape

    def body(x_ref, out_ref, send_sems, recv_sems, local_sem):
        x, y, c = lax.axis_index("x"), lax.axis_index("y"), lax.axis_index("c")
        me, sibling = (x, y, c), (x, y, 1 - c)
        chips = [(1 - x, y), (x, 1 - y), (1 - x, 1 - y)]

        def slot(px, py, pc):
            return out_ref.at[4 * px + 2 * py + pc]

        def copy(k, block, to, src=None):
            return pltpu.make_async_remote_copy(
                src_ref=slot(*block) if src is None else src, dst_ref=slot(*block),
                send_sem=send_sems.at[k], recv_sem=recv_sems.at[k], device_id=to, device_id_type=MESH)

        mine = pltpu.make_async_copy(x_ref, slot(*me), local_sem)
        mine.start()
        first = [copy(0, me, sibling, src=x_ref)]
        first += [copy(1 + j, me, (*chip, c), src=x_ref) for j, chip in enumerate(chips)]
        for cp in first:
            cp.start()
        passed = [copy(4 + j, (*chip, c), sibling) for j, chip in enumerate(chips)]
        for j, chip in enumerate(chips):
            copy(1 + j, (*chip, c), me).wait_recv()
            passed[j].start()
        copy(0, sibling, me).wait_recv()
        for j, chip in enumerate(chips):
            copy(4 + j, (*chip, 1 - c), me).wait_recv()
        for cp in first + passed:
            cp.wait_send()
        mine.wait()

    return pl.pallas_call(
        body, name=name,
        out_shape=jax.ShapeDtypeStruct((N_DEV, rows, cols), v.dtype),
        in_specs=[pl.BlockSpec(memory_space=pltpu.VMEM)],
        out_specs=pl.BlockSpec(memory_space=pltpu.VMEM),
        scratch_shapes=[pltpu.SemaphoreType.DMA((7,)), pltpu.SemaphoreType.DMA((7,)), pltpu.SemaphoreType.DMA],
    )(v)


def _hbm_specs(n):
    return [pl.BlockSpec(memory_space=pl.ANY)] * n


def _gather_weights(shards):
    n = len(shards)

    def body(*refs):
        ins, outs, (send_sems, recv_sems) = refs[:n], refs[n:2 * n], refs[2 * n:]
        x, y, c = lax.axis_index("x"), lax.axis_index("y"), lax.axis_index("c")
        sibling = (x, y, 1 - c)
        chips = [(1 - x, y), (x, 1 - y), (1 - x, 1 - y)]

        def blk(w, px, py, half):
            hr = ins[w].shape[0] // 2
            return outs[w].at[2 * px + py, pl.ds(half * hr, hr), :]

        def copy(w, k, block, to, src=None):
            return pltpu.make_async_remote_copy(
                src_ref=blk(w, *block) if src is None else src, dst_ref=blk(w, *block),
                send_sem=send_sems.at[6 * w + k], recv_sem=recv_sems.at[6 * w + k], device_id=to, device_id_type=MESH)

        first = []
        for w in range(n):
            hr = ins[w].shape[0] // 2
            my_half = ins[w].at[pl.ds(c * hr, hr), :]
            first += [copy(w, j, (x, y, c), (*chip, c), src=my_half) for j, chip in enumerate(chips)]
        for cp in first:
            cp.start()
        passed = []
        for j, chip in enumerate(chips):
            for w in range(n):
                copy(w, j, (*chip, c), (x, y, c)).wait_recv()
                passed.append(copy(w, 3 + j, (*chip, c), sibling))
                passed[-1].start()
        for j, chip in enumerate(chips):
            for w in range(n):
                copy(w, 3 + j, (*chip, 1 - c), (x, y, c)).wait_recv()
        for cp in first + passed:
            cp.wait_send()

    return pl.pallas_call(
        body, name="gather_weights",
        out_shape=[jax.ShapeDtypeStruct((N_CHIPS,) + s.shape, s.dtype) for s in shards],
        in_specs=_hbm_specs(n), out_specs=_hbm_specs(n),
        scratch_shapes=[pltpu.SemaphoreType.DMA((6 * n,)), pltpu.SemaphoreType.DMA((6 * n,))],
    )(*shards)


def _swap_halves(grads, name):
    n = len(grads)

    def body(*refs):
        ins, outs, (send_sems, recv_sems) = refs[:n], refs[n:2 * n], refs[2 * n:]
        x, y, c = lax.axis_index("x"), lax.axis_index("y"), lax.axis_index("c")
        cps = []
        for w in range(n):
            hr = ins[w].shape[1] // 2
            cps.append(pltpu.make_async_remote_copy(
                src_ref=ins[w].at[:, pl.ds((1 - c) * hr, hr), :], dst_ref=outs[w],
                send_sem=send_sems.at[w], recv_sem=recv_sems.at[w], device_id=(x, y, 1 - c), device_id_type=MESH))
        for cp in cps:
            cp.start()
        for cp in cps:
            cp.wait()

    return pl.pallas_call(
        body, name=name,
        out_shape=[jax.ShapeDtypeStruct((N_CHIPS, g.shape[1] // 2, g.shape[2]), g.dtype) for g in grads],
        in_specs=_hbm_specs(n), out_specs=_hbm_specs(n),
        scratch_shapes=[pltpu.SemaphoreType.DMA((n,)), pltpu.SemaphoreType.DMA((n,))],
    )(*grads)


def _scatter_chips(reds, name):
    n = len(reds)

    def body(*refs):
        ins, outs, (send_sems, recv_sems) = refs[:n], refs[n:2 * n], refs[2 * n:]
        x, y, c = lax.axis_index("x"), lax.axis_index("y"), lax.axis_index("c")
        chips = [(1 - x, y), (x, 1 - y), (1 - x, 1 - y)]
        cps = [pltpu.make_async_remote_copy(
            src_ref=ins[w].at[2 * chip[0] + chip[1]], dst_ref=outs[w].at[j],
            send_sem=send_sems.at[3 * w + j], recv_sem=recv_sems.at[3 * w + j],
            device_id=(*chip, c), device_id_type=MESH)
            for j, chip in enumerate(chips) for w in range(n)]
        for cp in cps:
            cp.start()
        for cp in cps:
            cp.wait()

    return pl.pallas_call(
        body, name=name,
        out_shape=[jax.ShapeDtypeStruct((3,) + r.shape[1:], r.dtype) for r in reds],
        in_specs=_hbm_specs(n), out_specs=_hbm_specs(n),
        scratch_shapes=[pltpu.SemaphoreType.DMA((3 * n,)), pltpu.SemaphoreType.DMA((3 * n,))],
    )(*reds)


def _join_halves(halves, token, name):
    n = len(halves)

    def body(*refs):
        ins, outs, (send_sems, recv_sems) = refs[:n], refs[n + 1:2 * n + 1], refs[2 * n + 1:]
        x, y, c = lax.axis_index("x"), lax.axis_index("y"), lax.axis_index("c")
        cps = [pltpu.make_async_remote_copy(
            src_ref=ins[w], dst_ref=outs[w], send_sem=send_sems.at[w], recv_sem=recv_sems.at[w],
            device_id=(x, y, 1 - c), device_id_type=MESH) for w in range(n)]
        for cp in cps:
            cp.start()
        for cp in cps:
            cp.wait()

    return pl.pallas_call(
        body, name=name,
        out_shape=[jax.ShapeDtypeStruct(h.shape, h.dtype) for h in halves],
        in_specs=_hbm_specs(n + 1), out_specs=_hbm_specs(n),
        scratch_shapes=[pltpu.SemaphoreType.DMA((n,)), pltpu.SemaphoreType.DMA((n,))],
    )(*halves, token)


def _in_hbm(v):
    return pltpu.with_memory_space_constraint(v, pltpu.HBM)


_SPLIT_COPY = pltpu.CompilerParams(has_side_effects=pltpu.SideEffectType.DATAFLOW_SIDE_EFFECTING)


def _gather_copies(srcs, lands, send_sems, recv_sems):
    x, y, c = lax.axis_index("x"), lax.axis_index("y"), lax.axis_index("c")
    cps = []
    for w, (src, land) in enumerate(zip(srcs, lands)):
        hr = src.shape[0] // 2
        for j, chip in enumerate([(1 - x, y), (x, 1 - y), (1 - x, 1 - y)]):
            cps.append(pltpu.make_async_remote_copy(
                src_ref=src.at[pl.ds(c * hr, hr), :], dst_ref=land.at[2 * x + y, pl.ds(c * hr, hr), :],
                send_sem=send_sems.at[3 * w + j], recv_sem=recv_sems.at[3 * w + j],
                device_id=(*chip, c), device_id_type=MESH))
    return cps


def _scatter_copies(srcs, lands, send_sems, recv_sems):
    x, y, c = lax.axis_index("x"), lax.axis_index("y"), lax.axis_index("c")
    cps = []
    for w, (src, land) in enumerate(zip(srcs, lands)):
        for j, chip in enumerate([(1 - x, y), (x, 1 - y), (1 - x, 1 - y)]):
            cps.append(pltpu.make_async_remote_copy(
                src_ref=src.at[2 * chip[0] + chip[1]], dst_ref=land.at[j],
                send_sem=send_sems.at[3 * w + j], recv_sem=recv_sems.at[3 * w + j],
                device_id=(*chip, c), device_id_type=MESH))
    return cps


def _split_start(copies, srcs, lands, name):
    n = len(srcs)

    def body(*refs):
        src, lnd, send_sems, recv_sems, token = refs[:n], refs[n:2 * n], refs[2 * n], refs[2 * n + 1], refs[-1]
        for cp in copies(src, lnd, send_sems, recv_sems):
            cp.start()
        token[...] = jnp.zeros_like(token)

    hbm = pl.BlockSpec(memory_space=pltpu.HBM)
    sem = pl.BlockSpec(memory_space=pltpu.SEMAPHORE)
    outs = pl.pallas_call(
        body, name=name,
        out_shape=(pltpu.SemaphoreType.DMA((3 * n,)), pltpu.SemaphoreType.DMA((3 * n,)),
                   *[pltpu.HBM(v.shape, v.dtype) for v in srcs + lands], jax.ShapeDtypeStruct((8, 128), F32)),
        in_specs=[hbm] * (2 * n),
        out_specs=(sem, sem, *([hbm] * (2 * n)), pl.BlockSpec(memory_space=pltpu.VMEM)),
        input_output_aliases={i: 2 + i for i in range(2 * n)},
        compiler_params=_SPLIT_COPY,
    )(*[_in_hbm(v) for v in srcs + lands])
    return outs[0], outs[1], list(outs[2:2 + n]), list(outs[2 + n:2 + 2 * n]), outs[-1]


def _split_wait(copies, send_sems, recv_sems, srcs, lands, after, name):
    n = len(srcs)

    def body(*refs):
        src, lnd, send_sems, recv_sems = refs[:n], refs[n:2 * n], refs[2 * n], refs[2 * n + 1]
        for cp in copies(src, lnd, send_sems, recv_sems):
            cp.wait_send()
            cp.wait_recv()

    hbm = pl.BlockSpec(memory_space=pltpu.HBM)
    sem = pl.BlockSpec(memory_space=pltpu.SEMAPHORE)
    outs = pl.pallas_call(
        body, name=name,
        out_shape=tuple(pltpu.HBM(v.shape, v.dtype) for v in srcs + lands),
        in_specs=[hbm] * (2 * n) + [sem, sem, pl.BlockSpec(memory_space=pl.ANY)],
        out_specs=tuple([hbm] * (2 * n)),
        input_output_aliases={i: i for i in range(2 * n)},
        compiler_params=_SPLIT_COPY,
    )(*srcs, *lands, send_sems, recv_sems, after)
    return list(outs[n:])


def _pass_to_sibling(lands):
    n = len(lands)

    def body(*refs):
        ins, outs, (send_sems, recv_sems) = refs[:n], refs[n:2 * n], refs[2 * n:]
        x, y, c = lax.axis_index("x"), lax.axis_index("y"), lax.axis_index("c")
        cps = []
        for w in range(n):
            hr = ins[w].shape[1] // 2
            for j, chip in enumerate([(1 - x, y), (x, 1 - y), (1 - x, 1 - y)]):
                k = 2 * chip[0] + chip[1]
                cps.append(pltpu.make_async_remote_copy(
                    src_ref=ins[w].at[k, pl.ds(c * hr, hr), :], dst_ref=outs[w].at[k, pl.ds(c * hr, hr), :],
                    send_sem=send_sems.at[3 * w + j], recv_sem=recv_sems.at[3 * w + j],
                    device_id=(x, y, 1 - c), device_id_type=MESH))
        for cp in cps:
            cp.start()
        for cp in cps:
            cp.wait()

    return pl.pallas_call(
        body, name="gather_late_pass",
        out_shape=[jax.ShapeDtypeStruct(v.shape, v.dtype) for v in lands],
        in_specs=_hbm_specs(n), out_specs=_hbm_specs(n),
        input_output_aliases={i: i for i in range(n)},
        scratch_shapes=[pltpu.SemaphoreType.DMA((3 * n,)), pltpu.SemaphoreType.DMA((3 * n,))],
    )(*lands)


def _row_tile(rows):
    for cand in (256, 176, 128, 64, 32, 16):
        if rows % cand == 0:
            return cand
    raise ValueError(rows)


def _add_my_half(g, other, c_idx, name):
    _, k, n = g.shape
    hr = k // 2
    tr = _row_tile(hr)
    nb = hr // tr

    def body(c_ref, g_ref, o_ref, out_ref, out16_ref):
        s = g_ref[...] + o_ref[...]
        out_ref[...] = s
        out16_ref[...] = s.astype(BF16)

    return pl.pallas_call(
        body, name=name,
        grid_spec=pltpu.PrefetchScalarGridSpec(
            num_scalar_prefetch=1, grid=(N_CHIPS, nb),
            in_specs=[pl.BlockSpec((1, tr, n), lambda j, i, c: (j, c[0] * nb + i, 0)),
                      pl.BlockSpec((1, tr, n), lambda j, i, c: (j, i, 0))],
            out_specs=[pl.BlockSpec((1, tr, n), lambda j, i, c: (j, i, 0)),
                       pl.BlockSpec((1, tr, n), lambda j, i, c: (j, i, 0))]),
        out_shape=[jax.ShapeDtypeStruct((N_CHIPS, hr, n), F32), jax.ShapeDtypeStruct((N_CHIPS, hr, n), BF16)],
        compiler_params=_params(("parallel", "parallel")),
    )(c_idx, g, other)


def _add_chips(red, recv, chip_idx, name):
    _, hr, n = red.shape
    tr = _row_tile(hr)

    def body(k_ref, r_ref, v_ref, out_ref):
        out_ref[...] = ((r_ref[0] + v_ref[0].astype(F32)) + v_ref[1].astype(F32)) + v_ref[2].astype(F32)

    return pl.pallas_call(
        body, name=name,
        grid_spec=pltpu.PrefetchScalarGridSpec(
            num_scalar_prefetch=1, grid=(hr // tr,),
            in_specs=[pl.BlockSpec((1, tr, n), lambda i, k: (k[0], i, 0)),
                      pl.BlockSpec((3, tr, n), lambda i, k: (0, i, 0))],
            out_specs=pl.BlockSpec((tr, n), lambda i, k: (i, 0))),
        out_shape=jax.ShapeDtypeStruct((hr, n), F32),
        compiler_params=_params(("parallel",)),
    )(chip_idx, red, recv)


def _mod_shard(c_all, w_ada, b_ada):
    nb, cols = c_all.shape[0], w_ada.shape[1]

    def body(c_ref, w_ref, b_ref, o_ref):
        c = c_ref[...]
        o_ref[...] = _dot(c * jax.nn.sigmoid(c), w_ref[...]) + b_ref[...]

    return pl.pallas_call(
        body, name="mod_shard", out_shape=jax.ShapeDtypeStruct((nb, cols), F32),
        compiler_params=_params(vmem_mb=48),
    )(c_all, w_ada, b_ada)


def _proj(x2, mod8, w, seq, out_dtype, name):
    t = x2.shape[0]
    n = w.shape[1]
    tm, tn = min(2048, seq), min(1152, n)
    tpb = seq // tm

    def body(x_ref, mod_ref, w_ref, o_ref, h_ref):
        @pl.when(pl.program_id(1) == 0)
        def _():
            h_ref[...] = (x_ref[...] * (1.0 + mod_ref[0, 1:2, :]) + mod_ref[0, 0:1, :]).astype(BF16)
        o_ref[...] = jnp.dot(h_ref[...], w_ref[...], preferred_element_type=F32).astype(o_ref.dtype)

    return pl.pallas_call(
        body, name=name, grid=(t // tm, n // tn),
        in_specs=[pl.BlockSpec((tm, D), lambda i, j: (i, 0)),
                  pl.BlockSpec((1, 8, D), lambda i, j: (i // tpb, 0, 0)),
                  pl.BlockSpec((D, tn), lambda i, j: (0, j))],
        out_specs=[pl.BlockSpec((tm, tn), lambda i, j: (i, j)), pl.BlockSpec((tm, D), lambda i, j: (i, 0))],
        out_shape=[jax.ShapeDtypeStruct((t, n), out_dtype), jax.ShapeDtypeStruct((t, D), BF16)],
        compiler_params=_params(("parallel", "arbitrary"), 56),
    )(x2, mod8, w)


def _rows_matmul(a, w, name):
    t, k = a.shape
    n = w.shape[1]
    tm = 1024 if t % 1024 == 0 else t

    def body(a_ref, w_ref, o_ref):
        o_ref[...] = jnp.dot(a_ref[...], w_ref[...], preferred_element_type=F32)

    return pl.pallas_call(
        body, name=name, grid=(t // tm,),
        in_specs=[pl.BlockSpec((tm, k), lambda i: (i, 0)), pl.BlockSpec((k, n), lambda i: (0, 0))],
        out_specs=pl.BlockSpec((tm, n), lambda i: (i, 0)),
        out_shape=jax.ShapeDtypeStruct((t, n), F32),
        compiler_params=_params(("parallel",)),
    )(a, w)


def _tn_matmul(a, b, name, seq, split=None):
    a_st, b_st = a.ndim == 3, b.ndim == 3
    t, ka = a.shape[-2:]
    n = b.shape[-1]
    tt = min(1024, seq)
    nt = t // tt
    if a_st or b_st:
        steps, tn = (a.shape[0] if a_st else b.shape[0]), n
    else:
        tn = split
        if tn is None:
            tn = next(cand for cand in (1152, 1024, 1408, 512, n) if n % cand == 0)
        steps = n // tn
    stacked_out = a_st or b_st or split is not None

    def body(a_ref, b_ref, o_ref):
        part = _dot_tn(a_ref[0] if a_st else a_ref[...], b_ref[0] if b_st else b_ref[...])
        if stacked_out:
            part = part[None]

        @pl.when(pl.program_id(1) == 0)
        def _():
            o_ref[...] = part

        @pl.when(pl.program_id(1) > 0)
        def _():
            o_ref[...] += part

    if a_st:
        in_specs = [pl.BlockSpec((1, tt, ka), lambda j, k: (j, k, 0))]
    else:
        in_specs = [pl.BlockSpec((tt, ka), lambda j, k: (k, 0))]
    if b_st:
        in_specs.append(pl.BlockSpec((1, tt, n), lambda j, k: (j, k, 0)))
    else:
        in_specs.append(pl.BlockSpec((tt, tn), lambda j, k: (k, 0 if a_st else j)))
    if stacked_out:
        out_spec = pl.BlockSpec((1, ka, tn), lambda j, k: (j, 0, 0))
        out_shape = jax.ShapeDtypeStruct((steps, ka, tn), F32)
    else:
        out_spec = pl.BlockSpec((ka, tn), lambda j, k: (0, j))
        out_shape = jax.ShapeDtypeStruct((ka, n), F32)
    return pl.pallas_call(
        body, name=name, grid=(steps, nt), in_specs=in_specs, out_specs=out_spec, out_shape=out_shape,
        compiler_params=_params(("parallel", "arbitrary"), 56),
    )(a, b)


def _dh_kernel(dproj, w_p, x2, dxp, mod8, seq):
    t = x2.shape[0]
    tm, tk = min(1024, seq), 1152
    tpb = seq // tm
    nk = NP // tk
    nbatch = t // seq

    def body(dp_ref, w_ref, x_ref, dxp_ref, mod_ref, gx_ref, dm_ref, acc):
        i, k = pl.program_id(0), pl.program_id(1)

        @pl.when(k == 0)
        def _():
            acc[...] = jnp.zeros_like(acc)

        acc[...] += _dot_nt(dp_ref[...], w_ref[...])

        @pl.when(k == nk - 1)
        def _():
            dh = acc[...]
            gx_ref[...] = dxp_ref[...] + dh * (1.0 + mod_ref[0, 1:2, :])
            upd = jnp.concatenate(
                [jnp.sum(dh, axis=0, keepdims=True), jnp.sum(dh * x_ref[...], axis=0, keepdims=True),
                 jnp.zeros((6, D), F32)], axis=0)

            @pl.when(i % tpb == 0)
            def _():
                dm_ref[0] = upd

            @pl.when(i % tpb != 0)
            def _():
                dm_ref[0] += upd

    return pl.pallas_call(
        body, name="dh", grid=(t // tm, nk),
        in_specs=[pl.BlockSpec((tm, tk), lambda i, k: (i, k)),
                  pl.BlockSpec((D, tk), lambda i, k: (0, k)),
                  pl.BlockSpec((tm, D), lambda i, k: (i, 0)),
                  pl.BlockSpec((tm, D), lambda i, k: (i, 0)),
                  pl.BlockSpec((1, 8, D), lambda i, k: (i // tpb, 0, 0))],
        out_specs=[pl.BlockSpec((tm, D), lambda i, k: (i, 0)),
                   pl.BlockSpec((1, 8, D), lambda i, k: (i // tpb, 0, 0))],
        out_shape=[jax.ShapeDtypeStruct((t, D), F32), jax.ShapeDtypeStruct((nbatch, 8, D), F32)],
        scratch_shapes=[pltpu.VMEM((tm, D), F32)],
        compiler_params=_params(("arbitrary", "arbitrary"), 48),
    )(dproj, w_p, x2, dxp, mod8)


def _tri(n, upper):
    r = lax.broadcasted_iota(jnp.int32, (n, n), 0)
    c = lax.broadcasted_iota(jnp.int32, (n, n), 1)
    return jnp.where((c >= r) if upper else (c <= r), 1.0, 0.0).astype(F32)


@jax.custom_vjp
def _mm_nn(a, b):
    return _dot(a, b)


_mm_nn.defvjp(lambda a, b: (_dot(a, b), (a, b)),
              lambda res, g: (_dot_nt(g, res[1]), _dot_tn(res[0], g)))


@jax.custom_vjp
def _mm_nt(a, b):
    return _dot_nt(a, b)


_mm_nt.defvjp(lambda a, b: (_dot_nt(a, b), (a, b)),
              lambda res, g: (_dot(g, res[1]), _dot_tn(g, res[0])))


@jax.custom_vjp
def _mm_tn(a, b):
    return _dot_tn(a, b)


_mm_tn.defvjp(lambda a, b: (_dot_tn(a, b), (a, b)),
              lambda res, g: (_dot_nt(res[1], g), _dot(res[0], g)))


@jax.custom_vjp
def _cumsum_rows(x):
    return _dot_f32(_tri(x.shape[0], False), x)


_cumsum_rows.defvjp(lambda x: (_cumsum_rows(x), None),
                    lambda _, g: (_dot_f32(_tri(g.shape[0], True), g),))


@functools.partial(jax.custom_vjp, nondiff_argnums=(1,))
def _shift_rows(x, k):
    return pltpu.roll(x, k % x.shape[0], 0)


_shift_rows.defvjp(lambda x, k: (_shift_rows(x, k), None),
                   lambda k, _, g: (pltpu.roll(g, (-k) % g.shape[0], 0),))


def _group_ref(bc, m):
    n = bc.shape[0] // (2 * m)
    b3 = bc.reshape(n, 2 * m, ADH)
    row = lax.broadcasted_iota(jnp.int32, b3.shape, 1)
    ref = jnp.sum(jnp.where(row == m - 1, b3, 0.0), axis=1, keepdims=True)
    return jnp.broadcast_to(ref, b3.shape).reshape(bc.shape)


def _hgrn_block(q, fl, v, g, st, lb, nw):
    n = q.shape[0]
    f = lb + (1.0 - lb) * jax.nn.sigmoid(fl)
    kk = 1.0 - f
    lf = jnp.log(f)
    bc = _cumsum_rows(lf)
    row = lax.broadcasted_iota(jnp.int32, (n, ADH), 0)
    same = jnp.bitwise_xor(lax.broadcasted_iota(jnp.int32, (n, n), 0), lax.broadcasted_iota(jnp.int32, (n, n), 1))
    a = jnp.zeros((n, n), F32)
    m = 1
    while m < n:
        r = jnp.bitwise_and(row, 2 * m - 1)
        up, lo = r >= m, r < m
        if m == 1:
            aq, ak = lf, jnp.zeros_like(lf)
        elif m == 2:
            aq = jnp.where(r == 3, lf + _shift_rows(lf, 1), lf)
            ak = jnp.where(r == 0, _shift_rows(lf, -1), 0.0)
        else:
            ref = _group_ref(bc, m)
            aq, ak = bc - ref, ref - bc
        qt = jnp.where(up, q * jnp.exp(jnp.where(up, aq, 0.0)), 0.0)
        kt = jnp.where(lo, kk * jnp.exp(jnp.where(lo, ak, 0.0)), 0.0)
        a = a + jnp.where(same < 2 * m, _mm_nt(qt, kt), 0.0)
        m *= 2
    last = row == n - 1
    bl = jnp.sum(jnp.where(last, bc, 0.0), axis=0, keepdims=True)
    o = _mm_nn(a, v) + _mm_nt(q * jnp.exp(bc), st) + jnp.sum(q * kk, axis=-1, keepdims=True) * v
    st_new = st * jnp.exp(bl) + _mm_tn(v, kk * jnp.exp(bl - bc))
    rms = lax.rsqrt(jnp.mean(o * o, axis=-1, keepdims=True) + RMS_EPS)
    return o * rms * nw * jax.nn.sigmoid(g), st_new


def _hgrn_fwd(proj, lb_logits, norm_w, nbatch, seq):
    t = proj.shape[0]
    blk = min(HGRN_BLOCK, seq)
    nb = seq // blk

    nh = HGRN_HEADS
    wp, wy = 512 * nh, ADH * nh

    def body(p_ref, lbl_ref, nw_ref, y_ref, ck_ref, st_s):
        @pl.when(pl.program_id(2) == 0)
        def _():
            st_s[...] = jnp.zeros_like(st_s)

        st = [st_s[h] for h in range(nh)]
        p = p_ref[...].astype(F32)
        lb = jax.nn.sigmoid(lbl_ref[0:1, :] - lbl_ref[1:2, :])
        nw = nw_ref[...]
        res = [_hgrn_block(*(p[:, 512 * h + 128 * k:512 * h + 128 * k + 128] for k in range(4)), st[h],
                           lb[:, 128 * h:128 * h + 128], nw[:, 128 * h:128 * h + 128]) for h in range(nh)]
        for h in range(nh):
            ck_ref[0, h] = st[h]
            st_s[h] = res[h][1]
        y_ref[...] = jnp.concatenate([r[0] for r in res], axis=1).astype(y_ref.dtype)

    return pl.pallas_call(
        body, name="hgrn_fwd", grid=(AH // nh, nbatch, nb),
        in_specs=[pl.BlockSpec((blk, wp), lambda h, b, i: (b * nb + i, COL_A // wp + h)),
                  pl.BlockSpec((2, wy), lambda h, b, i: (0, h)),
                  pl.BlockSpec((1, wy), lambda h, b, i: (0, h))],
        out_specs=[pl.BlockSpec((blk, wy), lambda h, b, i: (b * nb + i, h)),
                   pl.BlockSpec((1, nh, 128, 128), lambda h, b, i: ((h * nbatch + b) * nb + i, 0, 0, 0))],
        out_shape=[jax.ShapeDtypeStruct((t, AW), BF16),
                   jax.ShapeDtypeStruct((AH // nh * nbatch * nb, nh, 128, 128), F32)],
        scratch_shapes=[pltpu.VMEM((nh, 128, 128), F32)],
        compiler_params=_params(("parallel", "parallel", "arbitrary"), 48),
    )(proj, lb_logits, norm_w)


def _hgrn_bwd(proj, dya, ckpt, lb_logits, norm_w, dproj, nbatch, seq):
    t = proj.shape[0]
    blk = min(HGRN_BLOCK, seq)
    nb = seq // blk

    nh = HGRN_HEADS
    wp, wy = 512 * nh, ADH * nh

    def body(p_ref, dy_ref, ck_ref, lbl_ref, nw_ref, dp_in, dp_ref, sm_ref, dst_s):
        del dp_in
        b_id, i = pl.program_id(1), pl.program_id(2)

        @pl.when(i == 0)
        def _():
            dst_s[...] = jnp.zeros_like(dst_s)

        dst = [dst_s[h] for h in range(nh)]
        st = [ck_ref[0, h] for h in range(nh)]
        p = p_ref[...].astype(F32)
        dy = dy_ref[...]
        lb = jax.nn.sigmoid(lbl_ref[0:1, :] - lbl_ref[1:2, :])
        nw = nw_ref[...]
        grads = []
        for h in range(nh):
            _, pullback = jax.vjp(_hgrn_block, *(p[:, 512 * h + 128 * k:512 * h + 128 * k + 128] for k in range(4)),
                                  st[h], lb[:, 128 * h:128 * h + 128], nw[:, 128 * h:128 * h + 128])
            grads.append(pullback((dy[:, 128 * h:128 * h + 128], dst[h])))
        for h in range(nh):
            dst_s[h] = grads[h][4]
        dp_ref[...] = jnp.concatenate([g[k] for g in grads for k in range(4)], axis=1).astype(dp_ref.dtype)
        upd = jnp.concatenate([jnp.concatenate([g[5] for g in grads], axis=1),
                               jnp.concatenate([g[6] for g in grads], axis=1), jnp.zeros((6, wy), F32)], axis=0)
        first = (b_id == 0) & (i == 0)

        @pl.when(first)
        def _():
            sm_ref[...] = upd

        @pl.when(jnp.logical_not(first))
        def _():
            sm_ref[...] += upd

    def rows(h, b, i):
        return b * nb + (nb - 1 - i)

    return pl.pallas_call(
        body, name="hgrn_bwd", grid=(AH // nh, nbatch, nb),
        in_specs=[pl.BlockSpec((blk, wp), lambda h, b, i: (rows(h, b, i), COL_A // wp + h)),
                  pl.BlockSpec((blk, wy), lambda h, b, i: (rows(h, b, i), h)),
                  pl.BlockSpec((1, nh, 128, 128), lambda h, b, i: ((h * nbatch + b) * nb + (nb - 1 - i), 0, 0, 0)),
                  pl.BlockSpec((2, wy), lambda h, b, i: (0, h)),
                  pl.BlockSpec((1, wy), lambda h, b, i: (0, h)),
                  pl.BlockSpec(memory_space=pl.ANY)],
        out_specs=[pl.BlockSpec((blk, wp), lambda h, b, i: (rows(h, b, i), COL_A // wp + h)),
                   pl.BlockSpec((8, wy), lambda h, b, i: (0, h))],
        out_shape=[jax.ShapeDtypeStruct((t, NP), BF16), jax.ShapeDtypeStruct((8, AW), F32)],
        input_output_aliases={5: 0},
        scratch_shapes=[pltpu.VMEM((nh, 128, 128), F32)],
        compiler_params=_params(("parallel", "arbitrary", "arbitrary"), 56),
    )(proj, dya, ckpt, lb_logits, norm_w, dproj)


def _log_sigmoid(z):
    return jnp.minimum(z, 0.0) - jnp.log(1.0 + jnp.exp(-jnp.abs(z)))


def _fox_cum(proj, bias128, nbatch, seq):
    t = proj.shape[0]
    ts = min(512, seq)
    nb = seq // ts

    def body(p_ref, b_ref, c_ref, carry):
        @pl.when(pl.program_id(1) == 0)
        def _():
            carry[...] = jnp.zeros_like(carry)
        cum = _dot_f32(_tri(ts, False), _log_sigmoid(p_ref[...] + b_ref[...])) + carry[...]
        carry[...] = cum[ts - 1:ts, :]
        cum2 = cum * LOG2E
        lane = lax.broadcasted_iota(jnp.int32, (ts, 128), 1)
        for p in range(4):
            c_ref[p] = jnp.where(lane < 64, cum2[:, 2 * p:2 * p + 1], cum2[:, 2 * p + 1:2 * p + 2])

    return pl.pallas_call(
        body, name="fox_cum", grid=(nbatch, nb),
        in_specs=[pl.BlockSpec((ts, 128), lambda b, i: (b * nb + i, 0)),
                  pl.BlockSpec((1, 128), lambda b, i: (0, 0))],
        out_specs=pl.BlockSpec((4, ts, 128), lambda b, i: (0, b * nb + i, 0)),
        out_shape=jax.ShapeDtypeStruct((4, t, 128), F32),
        scratch_shapes=[pltpu.VMEM((1, 128), F32)],
        compiler_params=_params(("parallel", "arbitrary")),
    )(proj, bias128)


def _fox_scores_t(q128, k128, cc128, hh, masked):
    tq, tk = q128.shape[0], k128.shape[0]
    qh = _head_lanes((q128 * (LOG2E * BDH ** -0.5)).astype(BF16), hh)
    s = _dot_nt(k128, qh) - cc128[:, 64 * hh:64 * hh + 1]
    if masked:
        key = lax.broadcasted_iota(jnp.int32, (tk, tq), 0)
        qry = lax.broadcasted_iota(jnp.int32, (tk, tq), 1)
        s = jnp.where(key <= qry, s, NEG)
    return s


def _causal_pairs(nq, key_major):
    if key_major:
        pairs = [(i, j) for j in range(nq) for i in range(j, nq)]
    else:
        pairs = [(i, j) for i in range(nq) for j in range(i + 1)]
    return (jnp.asarray([p[0] for p in pairs], jnp.int32), jnp.asarray([p[1] for p in pairs], jnp.int32))


def _head_lanes(x128, hh):
    lane = lax.broadcasted_iota(jnp.int32, x128.shape, 1)
    return jnp.where((lane < 64) if hh == 0 else (lane >= 64), x128, jnp.zeros_like(x128))


def _with_ones_lane(x128, hh):
    lane = lax.broadcasted_iota(jnp.int32, x128.shape, 1)
    one = jnp.ones_like(x128)
    zero = jnp.zeros_like(x128)
    if hh == 0:
        return jnp.where(lane < 64, x128, jnp.where(lane == 64, one, zero))
    return jnp.where(lane >= 64, x128, jnp.where(lane == 0, one, zero))


def _fox_fwd(proj, cum_cols, nbatch, seq):
    t = proj.shape[0]
    tq = tk = min(512, seq)
    nq = seq // tq
    npr = FOX_PAIRS
    qi, kj = _causal_pairs(nq, key_major=False)

    def body(qi_ref, kj_ref, q_ref, kv_ref, cc_ref, o_ref, lse_ref, m_s, acc_s):
        s_id = pl.program_id(2)
        i, j = qi_ref[s_id], kj_ref[s_id]

        @pl.when(j == 0)
        def _():
            m_s[...] = jnp.full_like(m_s, NEG)
            acc_s[...] = jnp.zeros_like(acc_s)

        def step(masked):
            heads = [(pr, hh) for pr in range(npr) for hh in range(2)]
            m_prev = m_s[0:2 * npr, :]
            acc_prev = [acc_s[h] for h in range(2 * npr)]
            q128 = [q_ref[:, 128 * pr:128 * pr + 128] for pr in range(npr)]
            k128 = [kv_ref[:, 256 * pr:256 * pr + 128].astype(BF16) for pr in range(npr)]
            v128 = [kv_ref[:, 256 * pr + 128:256 * pr + 256].astype(BF16) for pr in range(npr)]
            s = [_fox_scores_t(q128[pr], k128[pr], cc_ref[pr], hh, masked) for pr, hh in heads]
            m_new = [jnp.maximum(m_prev[h:h + 1, :], jnp.max(s[h], axis=0, keepdims=True)) for h in range(2 * npr)]
            acc_new = []
            for h, (pr, hh) in enumerate(heads):
                alpha = jnp.exp2(m_prev[h:h + 1, :] - m_new[h])
                p = jnp.exp2(s[h] - m_new[h]).astype(BF16)
                acc_new.append(acc_prev[h] * alpha + _dot_tn(_with_ones_lane(v128[pr], hh), p))
            for h in range(2 * npr):
                acc_s[h] = acc_new[h]
            m_s[0:2 * npr, :] = jnp.concatenate(m_new, axis=0)

        @pl.when(j < i)
        def _():
            step(False)

        @pl.when(j == i)
        def _():
            step(True)
            outs = []
            for pr in range(npr):
                a0, a1 = acc_s[2 * pr], acc_s[2 * pr + 1]
                l0, l1 = a0[64:65, :], a1[0:1, :]
                outs.append(jnp.concatenate([a0[0:64, :] / l0, a1[64:128, :] / l1], axis=0).T)
                lse_ref[0, pr] = jnp.concatenate(
                    [m_s[2 * pr:2 * pr + 1, :] + jnp.log2(l0), m_s[2 * pr + 1:2 * pr + 2, :] + jnp.log2(l1),
                     jnp.zeros((6, tq), F32)], axis=0)
            o_ref[...] = jnp.concatenate(outs, axis=1).astype(o_ref.dtype)

    return pl.pallas_call(
        body, name="fox_fwd",
        grid_spec=pltpu.PrefetchScalarGridSpec(
            num_scalar_prefetch=2, grid=(nbatch, 4 // npr, qi.shape[0]),
            in_specs=[pl.BlockSpec((tq, 128 * npr), lambda b, p, s, qi, kj: (b * nq + qi[s], COL_BQ // (128 * npr) + p)),
                      pl.BlockSpec((tk, 256 * npr), lambda b, p, s, qi, kj: (b * nq + kj[s], COL_KV // (256 * npr) + p)),
                      pl.BlockSpec((npr, tk, 128), lambda b, p, s, qi, kj: (p, b * nq + kj[s], 0))],
            out_specs=[pl.BlockSpec((tq, 128 * npr), lambda b, p, s, qi, kj: (b * nq + qi[s], p)),
                       pl.BlockSpec((1, npr, 8, tq), lambda b, p, s, qi, kj: (b, p, 0, qi[s]))],
            scratch_shapes=[pltpu.VMEM((8, tq), F32), pltpu.VMEM((2 * npr, 128, tq), F32)]),
        out_shape=[jax.ShapeDtypeStruct((t, 512), BF16), jax.ShapeDtypeStruct((nbatch, 4, 8, seq), F32)],
        compiler_params=_params(("parallel", "parallel", "arbitrary"), 56),
    )(qi, kj, proj, proj, cum_cols)


def _fox_bwd(proj, cum_cols, lse, yb, dyb, dproj, nbatch, seq):
    t = proj.shape[0]
    tq = tk = min(512, seq)
    nq = seq // tq
    scale = BDH ** -0.5
    qi, kj = _causal_pairs(nq, key_major=True)
    nsteps = qi.shape[0]

    npr = FOX_PAIRS

    def body(qi_ref, kj_ref, q_ref, kv_ref, cc_ref, lse_ref, o_ref, do_ref, dp_in,
             dkv_ref, dq_ref, drs_ref, dcs_ref, dk_s, dv_s, dqa_s):
        del dp_in
        pg, s_id = pl.program_id(1), pl.program_id(2)
        i, j = qi_ref[s_id], kj_ref[s_id]

        @pl.when(i == j)
        def _():
            dk_s[...] = jnp.zeros_like(dk_s)
            dv_s[...] = jnp.zeros_like(dv_s)

        @pl.when(s_id == 0)
        def _():
            dqa_s[...] = jnp.zeros_like(dqa_s)

        def step(masked):
            dk_prev = [dk_s[h] for h in range(2 * npr)]
            dq_prev = [dqa_s[i, h] for h in range(2 * npr)]
            dv_new = [dv_s[pr] for pr in range(npr)]
            dk_new, dq_new = [], []
            for pr in range(npr):
                lanes = slice(128 * pr, 128 * pr + 128)
                q128 = q_ref[:, lanes]
                qs128 = (q128 * scale).astype(BF16)
                k128 = kv_ref[:, 256 * pr:256 * pr + 128].astype(BF16)
                v128 = kv_ref[:, 256 * pr + 128:256 * pr + 256].astype(BF16)
                do128 = do_ref[:, lanes]
                doo = do128 * o_ref[:, lanes].astype(F32)
                do16 = do128.astype(BF16)
                for hh in range(2):
                    s = _fox_scores_t(q128, k128, cc_ref[pr], hh, masked)
                    p = jnp.exp2(s - lse_ref[0, pr, hh:hh + 1, :])
                    dd = lax.dot_general(jnp.ones((8, 128), F32), _head_lanes(doo, hh), (((1,), (1,)), ((), ())),
                                         preferred_element_type=F32, precision=HIGHEST)[0:1, :]
                    doh = _head_lanes(do16, hh)
                    dp = _dot_nt(v128, doh)
                    ds = (p * (dp - dd)).astype(BF16)
                    dv_new[pr] = dv_new[pr] + _dot(p, doh)
                    dk_new.append(dk_prev[2 * pr + hh] + _dot(ds, _with_ones_lane(qs128, hh)))
                    dq_new.append(dq_prev[2 * pr + hh] + _dot_tn(_with_ones_lane(k128, hh), ds))
            for pr in range(npr):
                dv_s[pr] = dv_new[pr]
            for h in range(2 * npr):
                dk_s[h] = dk_new[h]
                dqa_s[i, h] = dq_new[h]

        @pl.when(i == j)
        def _():
            step(True)

        @pl.when(i > j)
        def _():
            step(False)

        def sums_to_lanes(lane, pr, s0, s1):
            hp = npr * pg + pr
            return jnp.where(lane == 2 * hp, s0, jnp.where(lane == 2 * hp + 1, s1, 0.0))

        @pl.when(i == nq - 1)
        def _():
            lane = lax.broadcasted_iota(jnp.int32, (tk, 128), 1)
            for pr in range(npr):
                k0, k1 = dk_s[2 * pr], dk_s[2 * pr + 1]
                dkv_ref[:, 256 * pr:256 * pr + 128] = jnp.where(lane < 64, k0, k1).astype(dkv_ref.dtype)
                dkv_ref[:, 256 * pr + 128:256 * pr + 256] = dv_s[pr].astype(dkv_ref.dtype)
                dcs_ref[pr] = sums_to_lanes(lane, pr, k0[:, 64:65], k1[:, 0:1])

        @pl.when(s_id == nsteps - 1)
        def _():
            lane = lax.broadcasted_iota(jnp.int32, (tq, 128), 1)
            for blk in range(nq):
                rows = pl.ds(blk * tq, tq)
                for pr in range(npr):
                    a0 = dqa_s[blk, 2 * pr].T
                    a1 = dqa_s[blk, 2 * pr + 1].T
                    dq_ref[rows, 128 * pr:128 * pr + 128] = (jnp.where(lane < 64, a0, a1) * scale).astype(dq_ref.dtype)
                    drs_ref[pr, rows, :] = sums_to_lanes(lane, pr, a0[:, 64:65], a1[:, 0:1])

    return pl.pallas_call(
        body, name="fox_bwd",
        grid_spec=pltpu.PrefetchScalarGridSpec(
            num_scalar_prefetch=2, grid=(nbatch, 4 // npr, nsteps),
            in_specs=[pl.BlockSpec((tq, 128 * npr), lambda b, p, s, qi, kj: (b * nq + qi[s], COL_BQ // (128 * npr) + p)),
                      pl.BlockSpec((tk, 256 * npr), lambda b, p, s, qi, kj: (b * nq + kj[s], COL_KV // (256 * npr) + p)),
                      pl.BlockSpec((npr, tk, 128), lambda b, p, s, qi, kj: (p, b * nq + kj[s], 0)),
                      pl.BlockSpec((1, npr, 8, tq), lambda b, p, s, qi, kj: (b, p, 0, qi[s])),
                      pl.BlockSpec((tq, 128 * npr), lambda b, p, s, qi, kj: (b * nq + qi[s], p)),
                      pl.BlockSpec((tq, 128 * npr), lambda b, p, s, qi, kj: (b * nq + qi[s], p)),
                      pl.BlockSpec(memory_space=pl.ANY)],
            out_specs=[pl.BlockSpec((tk, 256 * npr), lambda b, p, s, qi, kj: (b * nq + kj[s], COL_KV // (256 * npr) + p)),
                       pl.BlockSpec((seq, 128 * npr), lambda b, p, s, qi, kj: (b, p)),
                       pl.BlockSpec((npr, seq, 128), lambda b, p, s, qi, kj: (p, b, 0)),
                       pl.BlockSpec((npr, tk, 128), lambda b, p, s, qi, kj: (p, b * nq + kj[s], 0))],
            scratch_shapes=[pltpu.VMEM((2 * npr, tk, 128), F32), pltpu.VMEM((npr, tk, 128), F32),
                            pltpu.VMEM((nq, 2 * npr, 128, tq), F32)]),
        out_shape=[jax.ShapeDtypeStruct((t, NP), BF16), jax.ShapeDtypeStruct((t, 512), BF16),
                   jax.ShapeDtypeStruct((4, t, 128), F32), jax.ShapeDtypeStruct((4, t, 128), F32)],
        input_output_aliases={8: 0},
        compiler_params=_params(("parallel", "parallel", "arbitrary"), 60),
    )(qi, kj, proj, proj, cum_cols, lse, yb, dyb, dproj)


def _place_cols(dproj, src, col):
    t, w = src.shape
    tm = 1024 if t % 1024 == 0 else t

    def body(s_ref, dp_in, o_ref):
        del dp_in
        o_ref[...] = s_ref[...]

    return pl.pallas_call(
        body, name="place_cols", grid=(t // tm,),
        in_specs=[pl.BlockSpec((tm, w), lambda i: (i, 0)), pl.BlockSpec(memory_space=pl.ANY)],
        out_specs=pl.BlockSpec((tm, w), lambda i: (i, col // w)),
        out_shape=jax.ShapeDtypeStruct(dproj.shape, dproj.dtype),
        input_output_aliases={1: 0},
        compiler_params=_params(("parallel",)),
    )(src, dproj)


def _fox_dbf(proj, bias128, drs, dcs, dproj, nbatch, seq):
    t = proj.shape[0]
    ts = min(512, seq)
    nb = seq // ts

    def body(p_ref, b_ref, dr_ref, dc_ref, dp_in, dp_ref, sm_ref, carry):
        del dp_in
        b_id, i = pl.program_id(0), pl.program_id(1)

        @pl.when(i == 0)
        def _():
            carry[...] = jnp.zeros_like(carry)

        dcum = (dr_ref[0] - dc_ref[0]) + (dr_ref[1] - dc_ref[1]) + (dr_ref[2] - dc_ref[2]) + (dr_ref[3] - dc_ref[3])
        rc = _dot_f32(_tri(ts, True), dcum) + carry[...]
        carry[...] = rc[0:1, :]
        z = p_ref[...] + b_ref[...]
        lane = lax.broadcasted_iota(jnp.int32, (ts, 128), 1)
        dz = jnp.where(lane < BH, rc * jax.nn.sigmoid(-z), 0.0)
        dp_ref[...] = dz.astype(dp_ref.dtype)
        upd = jnp.concatenate([jnp.sum(dz, axis=0, keepdims=True), jnp.zeros((7, 128), F32)], axis=0)
        first = (b_id == 0) & (i == 0)

        @pl.when(first)
        def _():
            sm_ref[...] = upd

        @pl.when(jnp.logical_not(first))
        def _():
            sm_ref[...] += upd

    def rows(b, i):
        return b * nb + (nb - 1 - i)

    return pl.pallas_call(
        body, name="fox_dbf", grid=(nbatch, nb),
        in_specs=[pl.BlockSpec((ts, 128), lambda b, i: (rows(b, i), 0)),
                  pl.BlockSpec((1, 128), lambda b, i: (0, 0)),
                  pl.BlockSpec((4, ts, 128), lambda b, i: (0, rows(b, i), 0)),
                  pl.BlockSpec((4, ts, 128), lambda b, i: (0, rows(b, i), 0)),
                  pl.BlockSpec(memory_space=pl.ANY)],
        out_specs=[pl.BlockSpec((ts, 128), lambda b, i: (rows(b, i), COL_BF // 128)),
                   pl.BlockSpec((8, 128), lambda b, i: (0, 0))],
        out_shape=[jax.ShapeDtypeStruct((t, NP), BF16), jax.ShapeDtypeStruct((8, 128), F32)],
        input_output_aliases={4: 0},
        scratch_shapes=[pltpu.VMEM((1, 128), F32)],
        compiler_params=_params(("arbitrary", "arbitrary")),
    )(proj, bias128, drs, dcs, dproj)


def _ln_stats(z):
    mu = jnp.mean(z, axis=-1, keepdims=True)
    zc = z - mu
    rstd = lax.rsqrt(jnp.mean(zc * zc, axis=-1, keepdims=True) + LN_EPS)
    return zc * rstd, rstd


def _ln_bwd(dy, xhat, rstd, w):
    dxh = dy * w
    return rstd * (dxh - jnp.mean(dxh, axis=-1, keepdims=True) - xhat * jnp.mean(dxh * xhat, axis=-1, keepdims=True))


def _merge_fwd(ya, yb, proj, x2, mod8, wba, wbb, wout, ln1w, ln1b, seq):
    t = x2.shape[0]
    tm = min(512, seq)
    tpb = seq // tm

    def body(ya_ref, yb_ref, g_ref, x_ref, mod_ref, wa_ref, wb_ref, wo_ref, lw_ref, lb_ref, mg_ref, u_ref, x1_ref):
        ga = jax.nn.sigmoid(g_ref[:, 0:D].astype(F32))
        gb = jax.nn.sigmoid(g_ref[:, D:2 * D].astype(F32))
        merged = (ga * jnp.dot(ya_ref[...], wa_ref[...], preferred_element_type=F32)
                  + gb * jnp.dot(yb_ref[...], wb_ref[...], preferred_element_type=F32))
        mg = merged.astype(BF16)
        mg_ref[...] = mg
        u = jnp.dot(mg, wo_ref[...], preferred_element_type=F32)
        u_ref[...] = u
        xhat, _ = _ln_stats(ALPHA * x_ref[...] + (1.0 + mod_ref[0, 2:3, :]) * u)
        x1_ref[...] = xhat * lw_ref[...] + lb_ref[...]

    tok = lambda w: pl.BlockSpec((tm, w), lambda i: (i, 0))
    full = lambda a: pl.BlockSpec(a.shape, lambda i: (0,) * a.ndim)
    return pl.pallas_call(
        body, name="merge_fwd", grid=(t // tm,),
        in_specs=[tok(512), tok(512), pl.BlockSpec((tm, 2048), lambda i: (i, COL_GATES // 2048)), tok(D),
                  pl.BlockSpec((1, 8, D), lambda i: (i // tpb, 0, 0)),
                  full(wba), full(wbb), full(wout), full(ln1w), full(ln1b)],
        out_specs=[tok(D), tok(D), tok(D)],
        out_shape=[jax.ShapeDtypeStruct((t, D), BF16), jax.ShapeDtypeStruct((t, D), F32),
                   jax.ShapeDtypeStruct((t, D), F32)],
        compiler_params=_params(("parallel",), 48),
    )(ya, yb, proj, x2, mod8, wba, wbb, wout, ln1w, ln1b)


def _merge_bwd(du, ya, yb, proj, wba, wbb, wout, token, seq):
    t = du.shape[0]
    tm = min(512, seq)

    def body(du_ref, ya_ref, yb_ref, g_ref, wa_ref, wb_ref, wo_ref, token_ref,
             dp_ref, dpa_ref, dpb_ref, dya_ref, dyb_ref):
        del token_ref
        ga = jax.nn.sigmoid(g_ref[:, 0:D].astype(F32))
        gb = jax.nn.sigmoid(g_ref[:, D:2 * D].astype(F32))
        dm = _dot_nt(du_ref[...], wo_ref[...])
        pa = jnp.dot(ya_ref[...], wa_ref[...], preferred_element_type=F32)
        pb = jnp.dot(yb_ref[...], wb_ref[...], preferred_element_type=F32)
        dpa = (dm * ga).astype(BF16)
        dpb = (dm * gb).astype(BF16)
        dpa_ref[...] = dpa
        dpb_ref[...] = dpb
        dp_ref[:, 0:D] = (dm * pa * ga * (1.0 - ga)).astype(BF16)
        dp_ref[:, D:2 * D] = (dm * pb * gb * (1.0 - gb)).astype(BF16)
        dya_ref[...] = _dot_nt(dpa, wa_ref[...])
        dyb_ref[...] = _dot_nt(dpb, wb_ref[...])

    tok = lambda w: pl.BlockSpec((tm, w), lambda i: (i, 0))
    full = lambda a: pl.BlockSpec(a.shape, lambda i: (0,) * a.ndim)
    return pl.pallas_call(
        body, name="merge_bwd", grid=(t // tm,),
        in_specs=[tok(D), tok(512), tok(512), pl.BlockSpec((tm, 2048), lambda i: (i, COL_GATES // 2048)),
                  full(wba), full(wbb), full(wout), full(token)],
        out_specs=[pl.BlockSpec((tm, 2048), lambda i: (i, COL_GATES // 2048)), tok(D), tok(D), tok(512), tok(512)],
        out_shape=[jax.ShapeDtypeStruct((t, NP), BF16), jax.ShapeDtypeStruct((t, D), BF16),
                   jax.ShapeDtypeStruct((t, D), BF16), jax.ShapeDtypeStruct((t, 512), F32),
                   jax.ShapeDtypeStruct((t, 512), F32)],
        compiler_params=_params(("parallel",), 48),
    )(du, ya, yb, proj, wba, wbb, wout, token)


def _ffn_fwd(x1, mod8, wg, wu, wd, target, ln2w, ln2b, seq):
    t = x1.shape[0]
    tm = min(FFN_TOKENS, seq)
    nf, _, tf = wg.shape
    tpb = seq // tm
    nbatch = t // seq

    def body(x_ref, mod_ref, wg_ref, wu_ref, wd_ref, t_ref, lw_ref, lb_ref,
             a_ref, b_ref, h_s, dz_ref, st_ref, dm_ref, acc):
        i, j = pl.program_id(0), pl.program_id(1)

        @pl.when(j == 0)
        def _():
            h_s[...] = (x_ref[...] * (1.0 + mod_ref[0, 4:5, :]) + mod_ref[0, 3:4, :]).astype(BF16)
            acc[...] = jnp.zeros_like(acc)

        a = jnp.dot(h_s[...], wg_ref[0], preferred_element_type=F32)
        b = jnp.dot(h_s[...], wu_ref[0], preferred_element_type=F32)
        a_ref[0] = a.astype(BF16)
        b_ref[0] = b.astype(BF16)
        acc[...] += _dot(a * jax.nn.sigmoid(a) * b, wd_ref[0])

        @pl.when(j == nf - 1)
        def _():
            ffn = acc[...]
            xhat, rstd = _ln_stats(ALPHA * x_ref[...] + (1.0 + mod_ref[0, 5:6, :]) * ffn)
            diff = xhat * lw_ref[...] + lb_ref[...] - t_ref[...]
            loss = 0.5 * jnp.sum(jnp.sum(diff * diff, axis=-1, keepdims=True), axis=0, keepdims=True) / D
            dy = diff * (1.0 / D)
            dz = _ln_bwd(dy, xhat, rstd, lw_ref[...])
            dz_ref[...] = dz
            lane = lax.broadcasted_iota(jnp.int32, (1, D), 1)
            upd = jnp.concatenate(
                [jnp.sum(dy * xhat, axis=0, keepdims=True), jnp.sum(dy, axis=0, keepdims=True),
                 jnp.where(lane == 0, loss, 0.0), jnp.zeros((5, D), F32)], axis=0)
            dmu = jnp.concatenate(
                [jnp.zeros((5, D), F32), jnp.sum(dz * ffn, axis=0, keepdims=True), jnp.zeros((2, D), F32)], axis=0)

            @pl.when(i == 0)
            def _():
                st_ref[...] = upd

            @pl.when(i > 0)
            def _():
                st_ref[...] += upd

            @pl.when(i % tpb == 0)
            def _():
                dm_ref[0] = dmu

            @pl.when(i % tpb != 0)
            def _():
                dm_ref[0] += dmu

    row = lambda: pl.BlockSpec((tm, D), lambda i, j: (i, 0))
    vec = lambda: pl.BlockSpec((1, D), lambda i, j: (0, 0))
    return pl.pallas_call(
        body, name="ffn_fwd", grid=(t // tm, nf),
        in_specs=[row(), pl.BlockSpec((1, 8, D), lambda i, j: (i // tpb, 0, 0)),
                  pl.BlockSpec((1, D, tf), lambda i, j: (j, 0, 0)), pl.BlockSpec((1, D, tf), lambda i, j: (j, 0, 0)),
                  pl.BlockSpec((1, tf, D), lambda i, j: (j, 0, 0)), row(), vec(), vec()],
        out_specs=[pl.BlockSpec((1, tm, tf), lambda i, j: (j, i, 0)), pl.BlockSpec((1, tm, tf), lambda i, j: (j, i, 0)),
                   row(), row(), pl.BlockSpec((8, D), lambda i, j: (0, 0)),
                   pl.BlockSpec((1, 8, D), lambda i, j: (i // tpb, 0, 0))],
        out_shape=[jax.ShapeDtypeStruct((nf, t, tf), BF16), jax.ShapeDtypeStruct((nf, t, tf), BF16),
                   jax.ShapeDtypeStruct((t, D), BF16),
                   jax.ShapeDtypeStruct((t, D), F32), jax.ShapeDtypeStruct((8, D), F32),
                   jax.ShapeDtypeStruct((nbatch, 8, D), F32)],
        scratch_shapes=[pltpu.VMEM((tm, D), F32)],
        compiler_params=_params(("arbitrary", "arbitrary"), 60),
    )(x1, mod8, wg, wu, wd, target, ln2w, ln2b)


def _ffn_bwd(dz2, a, b, wg, wu, wd, x1, x2, u, mod8, ln1w, seq):
    t = x1.shape[0]
    tm = min(512, seq)
    nf, tf, _ = wg.shape
    tpb = seq // tm
    nbatch = t // seq

    def body(dz_ref, a_ref, b_ref, wg_ref, wu_ref, wd_ref, x1_ref, x_ref, u_ref, mod_ref, lw_ref,
             da_ref, db_ref, hm_ref, df_ref, du_ref, dxp_ref, st_ref, dm_ref, acc):
        i, j = pl.program_id(0), pl.program_id(1)

        @pl.when(j == 0)
        def _():
            df_ref[...] = ((1.0 + mod_ref[0, 5:6, :]) * dz_ref[...]).astype(BF16)
            acc[...] = jnp.zeros_like(acc)

        dhm = _dot(df_ref[...], wd_ref[0])
        av = a_ref[0].astype(F32)
        bv = b_ref[0].astype(F32)
        sg = jax.nn.sigmoid(av)
        sl = av * sg
        hm_ref[0] = (sl * bv).astype(BF16)
        da = (dhm * bv * (sg * (1.0 + av * (1.0 - sg)))).astype(BF16)
        db = (dhm * sl).astype(BF16)
        da_ref[0] = da
        db_ref[0] = db
        acc[...] += _dot(da, wg_ref[0]) + _dot(db, wu_ref[0])

        @pl.when(j == nf - 1)
        def _():
            dh2 = acc[...]
            x1v = x1_ref[...]
            uv = u_ref[...]
            dx1 = ALPHA * dz_ref[...] + dh2 * (1.0 + mod_ref[0, 4:5, :])
            xhat, rstd = _ln_stats(ALPHA * x_ref[...] + (1.0 + mod_ref[0, 2:3, :]) * uv)
            dz1 = _ln_bwd(dx1, xhat, rstd, lw_ref[...])
            du_ref[...] = ((1.0 + mod_ref[0, 2:3, :]) * dz1).astype(BF16)
            dxp_ref[...] = ALPHA * dz1
            upd = jnp.concatenate(
                [jnp.sum(dx1 * xhat, axis=0, keepdims=True), jnp.sum(dx1, axis=0, keepdims=True),
                 jnp.zeros((6, D), F32)], axis=0)
            dmu = jnp.concatenate(
                [jnp.zeros((2, D), F32), jnp.sum(dz1 * uv, axis=0, keepdims=True),
                 jnp.sum(dh2, axis=0, keepdims=True), jnp.sum(dh2 * x1v, axis=0, keepdims=True),
                 jnp.zeros((3, D), F32)], axis=0)

            @pl.when(i == 0)
            def _():
                st_ref[...] = upd

            @pl.when(i > 0)
            def _():
                st_ref[...] += upd

            @pl.when(i % tpb == 0)
            def _():
                dm_ref[0] = dmu

            @pl.when(i % tpb != 0)
            def _():
                dm_ref[0] += dmu

    row = lambda: pl.BlockSpec((tm, D), lambda i, j: (i, 0))
    ffb = lambda: pl.BlockSpec((1, tm, tf), lambda i, j: (j, i, 0))
    return pl.pallas_call(
        body, name="ffn_bwd", grid=(t // tm, nf),
        in_specs=[row(), ffb(), ffb(),
                  pl.BlockSpec((1, tf, D), lambda i, j: (j, 0, 0)), pl.BlockSpec((1, tf, D), lambda i, j: (j, 0, 0)),
                  pl.BlockSpec((1, D, tf), lambda i, j: (j, 0, 0)), row(), row(), row(),
                  pl.BlockSpec((1, 8, D), lambda i, j: (i // tpb, 0, 0)), pl.BlockSpec((1, D), lambda i, j: (0, 0))],
        out_specs=[ffb(), ffb(), ffb(), row(), row(), row(), pl.BlockSpec((8, D), lambda i, j: (0, 0)),
                   pl.BlockSpec((1, 8, D), lambda i, j: (i // tpb, 0, 0))],
        out_shape=[jax.ShapeDtypeStruct((nf, t, tf), BF16), jax.ShapeDtypeStruct((nf, t, tf), BF16),
                   jax.ShapeDtypeStruct((nf, t, tf), BF16), jax.ShapeDtypeStruct((t, D), BF16),
                   jax.ShapeDtypeStruct((t, D), BF16), jax.ShapeDtypeStruct((t, D), F32),
                   jax.ShapeDtypeStruct((8, D), F32), jax.ShapeDtypeStruct((nbatch, 8, D), F32)],
        scratch_shapes=[pltpu.VMEM((tm, D), F32)],
        compiler_params=_params(("arbitrary", "arbitrary"), 48),
    )(dz2, a, b, wg, wu, wd, x1, x2, u, mod8, ln1w)


def _adamw_math(w, g, m, v):
    m = B1 * m + (1.0 - B1) * g
    v = B2 * v + (1.0 - B2) * (g * g)
    m_hat = m / (1.0 - B1 ** STEP)
    v_hat = v / (1.0 - B2 ** STEP)
    return -LR * (m_hat / (jnp.sqrt(v_hat) + EPS) + WD * w), m, v


def _adamw(w, g, m, v, name):
    rows, cols = w.shape
    tr = rows
    for cand in (128, 64, 32, 16, 8):
        if rows % cand == 0:
            tr = cand
            break

    def body(w_ref, g_ref, m_ref, v_ref, d_ref, mo_ref, vo_ref):
        d, mn, vn = _adamw_math(w_ref[...], g_ref[...], m_ref[...], v_ref[...])
        d_ref[...] = d
        mo_ref[...] = mn
        vo_ref[...] = vn

    spec = pl.BlockSpec((tr, cols), lambda i: (i, 0))
    return pl.pallas_call(
        body, name=name, grid=(rows // tr,), in_specs=[spec] * 4, out_specs=[spec] * 3,
        out_shape=[jax.ShapeDtypeStruct((rows, cols), F32)] * 3,
        compiler_params=_params(("parallel",), 48),
    )(w, g, m, v)


def _adamw_halves(w, g_mine, g_sib, m, v, c_idx, name):
    rows, cols = w.shape
    hr = rows // 2
    tr = next(cand for cand in (128, 88, 64, 32, 16, 8) if hr % cand == 0)
    tph = hr // tr

    def body(c_ref, w_ref, gm_ref, gs_ref, m_ref, v_ref, g_ref, d_ref, mo_ref, vo_ref):
        g = jnp.where(pl.program_id(0) == c_ref[0], gm_ref[...], gs_ref[...])
        d, mn, vn = _adamw_math(w_ref[...], g, m_ref[...], v_ref[...])
        g_ref[...] = g
        d_ref[...] = d
        mo_ref[...] = mn
        vo_ref[...] = vn

    full = pl.BlockSpec((tr, cols), lambda h, i, c: (h * tph + i, 0))
    half = pl.BlockSpec((tr, cols), lambda h, i, c: (i, 0))
    return pl.pallas_call(
        body, name=name,
        grid_spec=pltpu.PrefetchScalarGridSpec(
            num_scalar_prefetch=1, grid=(2, tph), in_specs=[full, half, half, full, full], out_specs=[full] * 4),
        out_shape=[jax.ShapeDtypeStruct((rows, cols), F32)] * 4,
        compiler_params=_params(("parallel", "parallel"), 48),
    )(c_idx, w, g_mine, g_sib, m, v)


def _grad_w_ada(c_all, dmod_cols):
    def body(c_ref, d_ref, o_ref):
        c = c_ref[...]
        o_ref[...] = lax.dot_general(c * jax.nn.sigmoid(c), d_ref[...], (((0,), (0,)), ((), ())),
                                     preferred_element_type=F32, precision=HIGHEST)

    return pl.pallas_call(
        body, name="grad_w_ada", out_shape=jax.ShapeDtypeStruct((D, dmod_cols.shape[1]), F32),
        compiler_params=_params(vmem_mb=48),
    )(c_all, dmod_cols)


def _small_update(gath, w8, m8, v8):
    def body(g_ref, w_ref, m_ref, v_ref, go_ref, d_ref, mo_ref, vo_ref):
        g0 = g_ref[0, 0:1, :] + g_ref[0, 1:2, :]
        g1 = g_ref[0, 2:3, :]
        for dev in range(1, N_DEV):
            g0 = g0 + (g_ref[dev, 0:1, :] + g_ref[dev, 1:2, :])
            g1 = g1 + g_ref[dev, 2:3, :]
        w = w_ref[...]
        lb = jax.nn.sigmoid(w[1:2, O_LB0:O_LB1] - w[1:2, O_LB1:O_FOX])
        fac = lb * (1.0 - lb)
        g1 = jnp.concatenate([g1[:, :O_LB0], g1[:, O_LB0:O_LB1] * fac, -g1[:, O_LB1:O_FOX] * fac, g1[:, O_FOX:]],
                             axis=1)
        g = jnp.concatenate([g0, g1, jnp.zeros((6, SMALL_W), F32)], axis=0)
        d, mn, vn = _adamw_math(w, g, m_ref[...], v_ref[...])
        go_ref[...] = g
        d_ref[...] = d
        mo_ref[...] = mn
        vo_ref[...] = vn

    return pl.pallas_call(
        body, name="small_update", out_shape=[jax.ShapeDtypeStruct((8, SMALL_W), F32)] * 4,
        compiler_params=_params(vmem_mb=48),
    )(gath, w8, m8, v8)


def _pack_small(b_ada, ln1w, ln1b, ln2w, ln2b, norm_w, lb_logits, fox):
    row1 = jnp.concatenate([ln1w, ln1b, ln2w, ln2b, norm_w, lb_logits[0:1], lb_logits[1:2], fox,
                            jnp.zeros((1, SMALL_W - O_FOX - BH), F32)], axis=1)
    return jnp.concatenate([b_ada, row1, jnp.zeros((6, SMALL_W), F32)], axis=0)


def _unpack_small(p):
    r = p[1:2]
    lb = jnp.concatenate([r[:, O_LB0:O_LB1], r[:, O_LB1:O_FOX]], axis=0)
    return dict(b_ada=p[0:1], ln1_w=r[:, O_LN1W:O_LN1B], ln1_b=r[:, O_LN1B:O_LN2W], ln2_w=r[:, O_LN2W:O_LN2B],
                ln2_b=r[:, O_LN2B:O_NORM], hgrn_norm_w=r[:, O_NORM:O_LB0], lb_logits=lb,
                fox_f_bias=r[:, O_FOX:O_FOX + BH])


_BIG = ("w_in", "w_branch_a", "w_branch_b", "w_out", "w_ffn_gate", "w_ffn_up", "w_ffn_down")
_TRANSPOSED = ("w_ffn_gate", "w_ffn_up")


def _cols_of_chips(stacked):
    return jnp.concatenate([stacked[k] for k in range(N_CHIPS)], axis=1)


def kernel(x, c, w_ada, b_ada, w_in, fox_f_bias, lb_logits, hgrn_norm_w, w_branch_a, w_branch_b, w_out, ln1_w, ln1_b, w_ffn_gate, w_ffn_up, w_ffn_down, ln2_w, ln2_b, loss_target, m_w_ada, m_b_ada, m_w_in, m_fox_f_bias, m_lb_logits, m_hgrn_norm_w, m_w_branch_a, m_w_branch_b, m_w_out, m_ln1_w, m_ln1_b, m_w_ffn_gate, m_w_ffn_up, m_w_ffn_down, m_ln2_w, m_ln2_b, v_w_ada, v_b_ada, v_w_in, v_fox_f_bias, v_lb_logits, v_hgrn_norm_w, v_w_branch_a, v_w_branch_b, v_w_out, v_ln1_w, v_ln1_b, v_w_ffn_gate, v_w_ffn_up, v_w_ffn_down, v_ln2_w, v_ln2_b):
    nbatch, seq, _ = x.shape
    t = nbatch * seq
    ax, ay, ac = lax.axis_index("x"), lax.axis_index("y"), lax.axis_index("c")
    chip = 2 * ax + ay
    dev = 2 * chip + ac
    chip_arr = jnp.reshape(chip, (1,)).astype(jnp.int32)
    core_arr = jnp.reshape(ac, (1,)).astype(jnp.int32)

    tr = lambda a: jnp.swapaxes(a[0], 0, 1)
    shard_w = dict(w_in=w_in[0], w_branch_a=w_branch_a[0], w_branch_b=w_branch_b[0], w_out=w_out[0],
                   w_ffn_gate=tr(w_ffn_gate), w_ffn_up=tr(w_ffn_up), w_ffn_down=w_ffn_down[0])
    shard_m = dict(w_in=m_w_in[0], w_branch_a=m_w_branch_a[0], w_branch_b=m_w_branch_b[0], w_out=m_w_out[0],
                   w_ffn_gate=tr(m_w_ffn_gate), w_ffn_up=tr(m_w_ffn_up), w_ffn_down=m_w_ffn_down[0])
    shard_v = dict(w_in=v_w_in[0], w_branch_a=v_w_branch_a[0], w_branch_b=v_w_branch_b[0], w_out=v_w_out[0],
                   w_ffn_gate=tr(v_w_ffn_gate), w_ffn_up=tr(v_w_ffn_up), w_ffn_down=v_w_ffn_down[0])

    shard16 = {n: shard_w[n].astype(BF16) for n in _BIG}

    def with_mine(gathered, n):
        return lax.dynamic_update_slice(gathered, shard16[n][None], (chip, 0, 0))

    w_p = _permute_cols(_cols_of_chips(with_mine(_gather_weights([shard16["w_in"]])[0], "w_in")))
    late = _BIG[1:]
    late_send, late_recv, late_src, late_land, late_token = _split_start(
        _gather_copies, [shard16[n] for n in late],
        [lax.empty((N_CHIPS,) + shard16[n].shape, BF16) for n in late], "gather_late_start")

    c8 = jnp.concatenate([c, jnp.zeros((8 - nbatch, D), F32)], axis=0)
    c_all = _allgather8(c8, "gather_c")[:, :nbatch, :].reshape(N_DEV * nbatch, D)
    ncol = w_ada.shape[2]
    b_cols = lax.dynamic_slice_in_dim(b_ada, chip * ncol, ncol, axis=1)
    mod_g = _allgather8(_mod_shard(c_all, w_ada[0], b_cols), "gather_mod")
    mod_all = jnp.concatenate([mod_g[2 * k] for k in range(N_CHIPS)], axis=1)
    mod_mine = lax.dynamic_slice_in_dim(mod_all, dev * nbatch, nbatch, axis=0)
    mod8 = jnp.concatenate([mod_mine.reshape(nbatch, 6, D), jnp.zeros((nbatch, 2, D), F32)], axis=1)
    mod8 = mod8 + late_token[0, 0]

    x2 = x.reshape(t, D)
    tgt2 = loss_target.reshape(t, D)
    bias128 = jnp.concatenate([fox_f_bias, jnp.zeros((1, 128 - BH), F32)], axis=1)

    proj, h16 = _proj(x2, mod8, w_p, seq, BF16, "proj")
    projf = _rows_matmul(h16, w_p[:, COL_BF:], "proj_forget")
    ya, ckpt = _hgrn_fwd(proj, lb_logits, hgrn_norm_w, nbatch, seq)
    cum_cols = _fox_cum(projf, bias128, nbatch, seq)
    yb, lse = _fox_fwd(proj, cum_cols, nbatch, seq)
    late_land = _pass_to_sibling(
        _split_wait(_gather_copies, late_send, late_recv, late_src, late_land, yb, "gather_late_wait"))
    full = {n: with_mine(g, n) for n, g in zip(late, late_land)}
    wba, wbb = _cols_of_chips(full["w_branch_a"]), _cols_of_chips(full["w_branch_b"])
    wout = full["w_out"].reshape(D, D)
    wg_t, wu_t, wd = full["w_ffn_gate"], full["w_ffn_up"], full["w_ffn_down"]
    wg, wu, wd_t = jnp.swapaxes(wg_t, 1, 2), jnp.swapaxes(wu_t, 1, 2), jnp.swapaxes(wd, 1, 2)
    merged, u, x1 = _merge_fwd(ya, yb, proj, x2, mod8, wba, wbb, wout, ln1_w, ln1_b, seq)
    a_pre, b_pre, h2, dz2, st2, dm2 = _ffn_fwd(x1, mod8, wg, wu, wd, tgt2, ln2_w, ln2_b, seq)
    loss = lax.psum(st2[2, 0], ("x", "y", "c"))

    da, db, hmid, dffn, du, dxp, st1, dm1 = _ffn_bwd(dz2, a_pre, b_pre, wg_t, wu_t, wd_t, x1, x2, u, mod8, ln1_w, seq)
    g_st = {}
    g_st["w_ffn_down"] = _tn_matmul(hmid, dffn, "dw_ffn_down", seq)
    g_st["w_ffn_gate"] = _tn_matmul(da, h2, "dw_ffn_gate", seq)
    g_st["w_ffn_up"] = _tn_matmul(db, h2, "dw_ffn_up", seq)
    g_st["w_out"] = _tn_matmul(merged, du, "dw_out", seq).reshape(N_CHIPS, D // N_CHIPS, D)

    def sum_over_cores(names, tag):
        g_list = [g_st[n] for n in names]
        return [_add_my_half(g, o, core_arr, "grad_add_halves_" + n)
                for n, g, o in zip(names, g_list, _swap_halves(g_list, "grad_swap_halves_" + tag))]

    early = ("w_ffn_down", "w_ffn_gate", "w_ffn_up", "w_out")
    e_halves = sum_over_cores(early, "early")
    e_send, e_recv, e_src, e_land, e_token = _split_start(
        _scatter_copies, [h16 for _, h16 in e_halves],
        [lax.empty((3,) + h16.shape[1:], BF16) for _, h16 in e_halves], "grad_scatter_early_start")
    dproj, dpa, dpb, dya, dyb = _merge_bwd(du, ya, yb, proj, wba, wbb, wout, e_token, seq)
    g_st["w_branch_a"] = _tn_matmul(ya, dpa, "dw_branch_a", seq, split=D // N_CHIPS)
    g_st["w_branch_b"] = _tn_matmul(yb, dpb, "dw_branch_b", seq, split=D // N_CHIPS)
    dproj, dq, drs, dcs = _fox_bwd(proj, cum_cols, lse, yb, dyb, dproj, nbatch, seq)
    dproj = _place_cols(dproj, dq, COL_BQ)
    dproj, sm_fox = _fox_dbf(projf, bias128, drs, dcs, dproj, nbatch, seq)
    dproj, sm_hgrn = _hgrn_bwd(proj, dya, ckpt, lb_logits, hgrn_norm_w, dproj, nbatch, seq)
    grad_x2, dm0 = _dh_kernel(dproj, w_p, x2, dxp, mod8, seq)
    dw_in = _unpermute_cols(_tn_matmul(h16, dproj, "dw_in", seq))
    ncin = NIN // N_CHIPS
    g_st["w_in"] = jnp.stack([dw_in[:, k * ncin:(k + 1) * ncin] for k in range(N_CHIPS)])

    e_recv = _split_wait(_scatter_copies, e_send, e_recv, e_src, e_land, dw_in, "grad_scatter_early_wait")
    rest = ("w_in", "w_branch_a", "w_branch_b")
    r_halves = sum_over_cores(rest, "rest")
    r_send, r_rcv, r_src, r_land, r_token = _split_start(
        _scatter_copies, [h16 for _, h16 in r_halves],
        [lax.empty((3,) + h16.shape[1:], BF16) for _, h16 in r_halves], "grad_scatter_rest_start")

    def finish(names, halves, recv, token, tag):
        g_mine = [_add_chips(h32, r, chip_arr, "grad_add_chips_" + n) for n, (h32, _), r in zip(names, halves, recv)]
        g_sib = _join_halves(g_mine, token, "grad_join_halves_" + tag)
        for n, gm, gs in zip(names, g_mine, g_sib):
            grads[n], deltas[n], new_m[n], new_v[n] = _adamw_halves(
                shard_w[n], gm, gs, shard_m[n], shard_v[n], core_arr, "adamw_" + n)

    grads, deltas, new_m, new_v = {}, {}, {}, {}
    finish(early, e_halves, e_recv, r_token, "early")

    dmod = (dm0 + dm1 + dm2)[:, :6, :].reshape(nbatch, 6 * D)
    row2 = jnp.concatenate([st1[0:1], st1[1:2], st2[0:1], st2[1:2], sm_hgrn[1:2], sm_hgrn[0:1], sm_hgrn[0:1],
                            sm_fox[0:1, :BH], jnp.zeros((1, SMALL_W - O_FOX - BH), F32)], axis=1)
    spack = jnp.concatenate([dmod, row2, jnp.zeros((8 - nbatch - 1, SMALL_W), F32)], axis=0)
    spack = spack + r_token[0, 0]
    gath = _allgather8(spack, "gather_small")
    w8 = _pack_small(b_ada, ln1_w, ln1_b, ln2_w, ln2_b, hgrn_norm_w, lb_logits, fox_f_bias)
    m8 = _pack_small(m_b_ada, m_ln1_w, m_ln1_b, m_ln2_w, m_ln2_b, m_hgrn_norm_w, m_lb_logits, m_fox_f_bias)
    v8 = _pack_small(v_b_ada, v_ln1_w, v_ln1_b, v_ln2_w, v_ln2_b, v_hgrn_norm_w, v_lb_logits, v_fox_f_bias)
    sg, sd, smn, svn = (_unpack_small(p) for p in _small_update(gath, w8, m8, v8))
    dmod_all = gath[:, :nbatch, :].reshape(N_DEV * nbatch, SMALL_W)
    g_ada = _grad_w_ada(c_all, lax.dynamic_slice_in_dim(dmod_all, chip * ncol, ncol, axis=1))

    for group, small in zip((grads, deltas, new_m, new_v), (sg, sd, smn, svn)):
        group.update(small)
    grads["w_ada"] = g_ada
    deltas["w_ada"], new_m["w_ada"], new_v["w_ada"] = _adamw(w_ada[0], g_ada, m_w_ada[0], v_w_ada[0], "adamw_w_ada")
    done = sum(new_v[n][0:8, 0:128] for n in early) + new_v["w_ada"][0:8, 0:128]
    r_recv = _split_wait(_scatter_copies, r_send, r_rcv, r_src, r_land, done, "grad_scatter_rest_wait")
    finish(rest, r_halves, r_recv, late_token, "rest")

    names = ["w_ada", "b_ada", "w_in", "fox_f_bias", "lb_logits", "hgrn_norm_w", "w_branch_a", "w_branch_b", "w_out",
             "ln1_w", "ln1_b", "w_ffn_gate", "w_ffn_up", "w_ffn_down", "ln2_w", "ln2_b"]
    shapes = dict(w_ada=w_ada.shape, b_ada=b_ada.shape, w_in=w_in.shape, fox_f_bias=fox_f_bias.shape,
                  lb_logits=lb_logits.shape, hgrn_norm_w=hgrn_norm_w.shape, w_branch_a=w_branch_a.shape,
                  w_branch_b=w_branch_b.shape, w_out=w_out.shape, ln1_w=ln1_w.shape, ln1_b=ln1_b.shape,
                  w_ffn_gate=w_ffn_gate.shape, w_ffn_up=w_ffn_up.shape, w_ffn_down=w_ffn_down.shape,
                  ln2_w=ln2_w.shape, ln2_b=ln2_b.shape)
    outs = [loss, grad_x2.reshape(x.shape)]
    for group in (grads, deltas, new_m, new_v):
        outs += [(jnp.swapaxes(group[n], 0, 1) if n in _TRANSPOSED else group[n]).reshape(shapes[n]) for n in names]
    return tuple(outs)
```

```python
import functools

import jax
import jax.numpy as jnp
from jax import lax
from jax.experimental import pallas as pl
from jax.experimental.pallas import tpu as pltpu

F32 = jnp.float32
BF16 = jnp.bfloat16
MESH = pl.DeviceIdType.MESH
HIGHEST = lax.Precision.HIGHEST

D = 1024
AW = 512
AH = 4
ADH = 128
BH = 8
BDH = 64
DFF = 2816
NIN = 5640
NP = 5760
N_CHIPS = 4
N_DEV = 8
HGRN_BLOCK = 256
FFN_TOKENS = 512
COL_GATES = 0
COL_A = 2048
COL_BQ = 4096
COL_KV = 4608
COL_BF = 5632
HGRN_HEADS = 4
FOX_PAIRS = 2
ALPHA = 2.0 ** 0.25
LN_EPS = 1e-5
RMS_EPS = 1e-6
NEG = -1e30
LOG2E = 1.4426950408889634
LR, B1, B2, EPS, WD, STEP = 0.001, 0.9, 0.999, 1e-08, 0.01, 10
SMALL_W = 6144
O_LN1W, O_LN1B, O_LN2W, O_LN2B, O_NORM, O_LB0, O_LB1, O_FOX = 0, 1024, 2048, 3072, 4096, 4608, 5120, 5632


def _params(sem=None, vmem_mb=None):
    kw = {}
    if sem is not None:
        kw["dimension_semantics"] = sem
    if vmem_mb is not None:
        kw["vmem_limit_bytes"] = vmem_mb << 20
    return pltpu.CompilerParams(**kw)


def _dot(a, b):
    return jnp.dot(a.astype(BF16), b.astype(BF16), preferred_element_type=F32)


def _dot_nt(a, b):
    return lax.dot_general(a.astype(BF16), b.astype(BF16), (((1,), (1,)), ((), ())), preferred_element_type=F32)


def _dot_tn(a, b):
    return lax.dot_general(a.astype(BF16), b.astype(BF16), (((0,), (0,)), ((), ())), preferred_element_type=F32)


def _dot_f32(a, b):
    return jnp.dot(a, b, preferred_element_type=F32, precision=HIGHEST)


def _perm_segments():
    segs = [(3592, 5640)]
    for h in range(4):
        segs += [(128 * h + 512 * t, 128 * h + 512 * t + 128) for t in range(4)]
    segs += [(2048, 2560)]
    for p in range(4):
        segs += [(2560 + 128 * p, 2688 + 128 * p), (3072 + 128 * p, 3200 + 128 * p)]
    segs += [(3584, 3592)]
    return segs


def _permute_cols(w):
    parts = [w[:, a:b] for a, b in _perm_segments()]
    parts.append(jnp.zeros((w.shape[0], NP - NIN), w.dtype))
    return jnp.concatenate(parts, axis=1)


def _unpermute_cols(g):
    pos, where = 0, {}
    for a, b in _perm_segments():
        where[a] = (pos, pos + b - a)
        pos += b - a
    parts = [g[:, where[a][0]:where[a][1]] for a in sorted(where)]
    return jnp.concatenate(parts, axis=1)


def _allgather8(v, name):
    rows, cols = v.shape

    def body(x_ref, out_ref, send_sems, recv_sems, local_sem):
        x, y, c = lax.axis_index("x"), lax.axis_index("y"), lax.axis_index("c")
        me, sibling = (x, y, c), (x, y, 1 - c)
        chips = [(1 - x, y), (x, 1 - y), (1 - x, 1 - y)]

        def slot(px, py, pc):
            return out_ref.at[4 * px + 2 * py + pc]

        def copy(k, block, to, src=None):
            return pltpu.make_async_remote_copy(
                src_ref=slot(*block) if src is None else src, dst_ref=slot(*block),
                send_sem=send_sems.at[k], recv_sem=recv_sems.at[k], device_id=to, device_id_type=MESH)

        mine = pltpu.make_async_copy(x_ref, slot(*me), local_sem)
        mine.start()
        first = [copy(0, me, sibling, src=x_ref)]
        first += [copy(1 + j, me, (*chip, c), src=x_ref) for j, chip in enumerate(chips)]
        for cp in first:
            cp.start()
        passed = [copy(4 + j, (*chip, c), sibling) for j, chip in enumerate(chips)]
        for j, chip in enumerate(chips):
            copy(1 + j, (*chip, c), me).wait_recv()
            passed[j].start()
        copy(0, sibling, me).wait_recv()
        for j, chip in enumerate(chips):
            copy(4 + j, (*chip, 1 - c), me).wait_recv()
        for cp in first + passed:
            cp.wait_send()
        mine.wait()

    return pl.pallas_call(
        body, name=name,
        out_shape=jax.ShapeDtypeStruct((N_DEV, rows, cols), v.dtype),
        in_specs=[pl.BlockSpec(memory_space=pltpu.VMEM)],
        out_specs=pl.BlockSpec(memory_space=pltpu.VMEM),
        scratch_shapes=[pltpu.SemaphoreType.DMA((7,)), pltpu.SemaphoreType.DMA((7,)), pltpu.SemaphoreType.DMA],
    )(v)


def _hbm_specs(n):
    return [pl.BlockSpec(memory_space=pl.ANY)] * n


def _gather_weights(shards):
    n = len(shards)

    def body(*refs):
        ins, outs, (send_sems, recv_sems) = refs[:n], refs[n:2 * n], refs[2 * n:]
        x, y, c = lax.axis_index("x"), lax.axis_index("y"), lax.axis_index("c")
        sibling = (x, y, 1 - c)
        chips = [(1 - x, y), (x, 1 - y), (1 - x, 1 - y)]

        def blk(w, px, py, half):
            hr = ins[w].shape[0] // 2
            return outs[w].at[2 * px + py, pl.ds(half * hr, hr), :]

        def copy(w, k, block, to, src=None):
            return pltpu.make_async_remote_copy(
                src_ref=blk(w, *block) if src is None else src, dst_ref=blk(w, *block),
                send_sem=send_sems.at[6 * w + k], recv_sem=recv_sems.at[6 * w + k], device_id=to, device_id_type=MESH)

        first = []
        for w in range(n):
            hr = ins[w].shape[0] // 2
            my_half = ins[w].at[pl.ds(c * hr, hr), :]
            first += [copy(w, j, (x, y, c), (*chip, c), src=my_half) for j, chip in enumerate(chips)]
        for cp in first:
            cp.start()
        passed = []
        for j, chip in enumerate(chips):
            for w in range(n):
                copy(w, j, (*chip, c), (x, y, c)).wait_recv()
                passed.append(copy(w, 3 + j, (*chip, c), sibling))
                passed[-1].start()
        for j, chip in enumerate(chips):
            for w in range(n):
                copy(w, 3 + j, (*chip, 1 - c), (x, y, c)).wait_recv()
        for cp in first + passed:
            cp.wait_send()

    return pl.pallas_call(
        body, name="gather_weights",
        out_shape=[jax.ShapeDtypeStruct((N_CHIPS,) + s.shape, s.dtype) for s in shards],
        in_specs=_hbm_specs(n), out_specs=_hbm_specs(n),
        scratch_shapes=[pltpu.SemaphoreType.DMA((6 * n,)), pltpu.SemaphoreType.DMA((6 * n,))],
    )(*shards)


def _swap_halves(grads, name):
    n = len(grads)

    def body(*refs):
        ins, outs, (send_sems, recv_sems) = refs[:n], refs[n:2 * n], refs[2 * n:]
        x, y, c = lax.axis_index("x"), lax.axis_index("y"), lax.axis_index("c")
        cps = []
        for w in range(n):
            hr = ins[w].shape[1] // 2
            cps.append(pltpu.make_async_remote_copy(
                src_ref=ins[w].at[:, pl.ds((1 - c) * hr, hr), :], dst_ref=outs[w],
                send_sem=send_sems.at[w], recv_sem=recv_sems.at[w], device_id=(x, y, 1 - c), device_id_type=MESH))
        for cp in cps:
            cp.start()
        for cp in cps:
            cp.wait()

    return pl.pallas_call(
        body, name=name,
        out_shape=[jax.ShapeDtypeStruct((N_CHIPS, g.shape[1] // 2, g.shape[2]), g.dtype) for g in grads],
        in_specs=_hbm_specs(n), out_specs=_hbm_specs(n),
        scratch_shapes=[pltpu.SemaphoreType.DMA((n,)), pltpu.SemaphoreType.DMA((n,))],
    )(*grads)


def _scatter_chips(reds, name):
    n = len(reds)

    def body(*refs):
        ins, outs, (send_sems, recv_sems) = refs[:n], refs[n:2 * n], refs[2 * n:]
        x, y, c = lax.axis_index("x"), lax.axis_index("y"), lax.axis_index("c")
        chips = [(1 - x, y), (x, 1 - y), (1 - x, 1 - y)]
        cps = [pltpu.make_async_remote_copy(
            src_ref=ins[w].at[2 * chip[0] + chip[1]], dst_ref=outs[w].at[j],
            send_sem=send_sems.at[3 * w + j], recv_sem=recv_sems.at[3 * w + j],
            device_id=(*chip, c), device_id_type=MESH)
            for j, chip in enumerate(chips) for w in range(n)]
        for cp in cps:
            cp.start()
        for cp in cps:
            cp.wait()

    return pl.pallas_call(
        body, name=name,
        out_shape=[jax.ShapeDtypeStruct((3,) + r.shape[1:], r.dtype) for r in reds],
        in_specs=_hbm_specs(n), out_specs=_hbm_specs(n),
        scratch_shapes=[pltpu.SemaphoreType.DMA((3 * n,)), pltpu.SemaphoreType.DMA((3 * n,))],
    )(*reds)


def _join_halves(halves, token, name):
    n = len(halves)

    def body(*refs):
        ins, outs, (send_sems, recv_sems) = refs[:n], refs[n + 1:2 * n + 1], refs[2 * n + 1:]
        x, y, c = lax.axis_index("x"), lax.axis_index("y"), lax.axis_index("c")
        cps = [pltpu.make_async_remote_copy(
            src_ref=ins[w], dst_ref=outs[w], send_sem=send_sems.at[w], recv_sem=recv_sems.at[w],
            device_id=(x, y, 1 - c), device_id_type=MESH) for w in range(n)]
        for cp in cps:
            cp.start()
        for cp in cps:
            cp.wait()

    return pl.pallas_call(
        body, name=name,
        out_shape=[jax.ShapeDtypeStruct(h.shape, h.dtype) for h in halves],
        in_specs=_hbm_specs(n + 1), out_specs=_hbm_specs(n),
        scratch_shapes=[pltpu.SemaphoreType.DMA((n,)), pltpu.SemaphoreType.DMA((n,))],
    )(*halves, token)


def _in_hbm(v):
    return pltpu.with_memory_space_constraint(v, pltpu.HBM)


_SPLIT_COPY = pltpu.CompilerParams(has_side_effects=pltpu.SideEffectType.DATAFLOW_SIDE_EFFECTING)


def _gather_copies(srcs, lands, send_sems, recv_sems):
    x, y, c = lax.axis_index("x"), lax.axis_index("y"), lax.axis_index("c")
    cps = []
    for w, (src, land) in enumerate(zip(srcs, lands)):
        hr = src.shape[0] // 2
        for j, chip in enumerate([(1 - x, y), (x, 1 - y), (1 - x, 1 - y)]):
            cps.append(pltpu.make_async_remote_copy(
                src_ref=src.at[pl.ds(c * hr, hr), :], dst_ref=land.at[2 * x + y, pl.ds(c * hr, hr), :],
                send_sem=send_sems.at[3 * w + j], recv_sem=recv_sems.at[3 * w + j],
                device_id=(*chip, c), device_id_type=MESH))
    return cps


def _scatter_copies(srcs, lands, send_sems, recv_sems):
    x, y, c = lax.axis_index("x"), lax.axis_index("y"), lax.axis_index("c")
    cps = []
    for w, (src, land) in enumerate(zip(srcs, lands)):
        for j, chip in enumerate([(1 - x, y), (x, 1 - y), (1 - x, 1 - y)]):
            cps.append(pltpu.make_async_remote_copy(
                src_ref=src.at[2 * chip[0] + chip[1]], dst_ref=land.at[j],
                send_sem=send_sems.at[3 * w + j], recv_sem=recv_sems.at[3 * w + j],
                device_id=(*chip, c), device_id_type=MESH))
    return cps


def _split_start(copies, srcs, lands, name):
    n = len(srcs)

    def body(*refs):
        src, lnd, send_sems, recv_sems, token = refs[:n], refs[n:2 * n], refs[2 * n], refs[2 * n + 1], refs[-1]
        for cp in copies(src, lnd, send_sems, recv_sems):
            cp.start()
        token[...] = jnp.zeros_like(token)

    hbm = pl.BlockSpec(memory_space=pltpu.HBM)
    sem = pl.BlockSpec(memory_space=pltpu.SEMAPHORE)
    outs = pl.pallas_call(
        body, name=name,
        out_shape=(pltpu.SemaphoreType.DMA((3 * n,)), pltpu.SemaphoreType.DMA((3 * n,)),
                   *[pltpu.HBM(v.shape, v.dtype) for v in srcs + lands], jax.ShapeDtypeStruct((8, 128), F32)),
        in_specs=[hbm] * (2 * n),
        out_specs=(sem, sem, *([hbm] * (2 * n)), pl.BlockSpec(memory_space=pltpu.VMEM)),
        input_output_aliases={i: 2 + i for i in range(2 * n)},
        compiler_params=_SPLIT_COPY,
    )(*[_in_hbm(v) for v in srcs + lands])
    return outs[0], outs[1], list(outs[2:2 + n]), list(outs[2 + n:2 + 2 * n]), outs[-1]


def _split_wait(copies, send_sems, recv_sems, srcs, lands, after, name):
    n = len(srcs)

    def body(*refs):
        src, lnd, send_sems, recv_sems = refs[:n], refs[n:2 * n], refs[2 * n], refs[2 * n + 1]
        for cp in copies(src, lnd, send_sems, recv_sems):
            cp.wait_send()
            cp.wait_recv()

    hbm = pl.BlockSpec(memory_space=pltpu.HBM)
    sem = pl.BlockSpec(memory_space=pltpu.SEMAPHORE)
    outs = pl.pallas_call(
        body, name=name,
        out_shape=tuple(pltpu.HBM(v.shape, v.dtype) for v in srcs + lands),
        in_specs=[hbm] * (2 * n) + [sem, sem, pl.BlockSpec(memory_space=pl.ANY)],
        out_specs=tuple([hbm] * (2 * n)),
        input_output_aliases={i: i for i in range(2 * n)},
        compiler_params=_SPLIT_COPY,
    )(*srcs, *lands, send_sems, recv_sems, after)
    return list(outs[n:])


def _pass_to_sibling(lands):
    n = len(lands)

    def body(*refs):
        ins, outs, (send_sems, recv_sems) = refs[:n], refs[n:2 * n], refs[2 * n:]
        x, y, c = lax.axis_index("x"), lax.axis_index("y"), lax.axis_index("c")
        cps = []
        for w in range(n):
            hr = ins[w].shape[1] // 2
            for j, chip in enumerate([(1 - x, y), (x, 1 - y), (1 - x, 1 - y)]):
                k = 2 * chip[0] + chip[1]
                cps.append(pltpu.make_async_remote_copy(
                    src_ref=ins[w].at[k, pl.ds(c * hr, hr), :], dst_ref=outs[w].at[k, pl.ds(c * hr, hr), :],
                    send_sem=send_sems.at[3 * w + j], recv_sem=recv_sems.at[3 * w + j],
                    device_id=(x, y, 1 - c), device_id_type=MESH))
        for cp in cps:
            cp.start()
        for cp in cps:
            cp.wait()

    return pl.pallas_call(
        body, name="gather_late_pass",
        out_shape=[jax.ShapeDtypeStruct(v.shape, v.dtype) for v in lands],
        in_specs=_hbm_specs(n), out_specs=_hbm_specs(n),
        input_output_aliases={i: i for i in range(n)},
        scratch_shapes=[pltpu.SemaphoreType.DMA((3 * n,)), pltpu.SemaphoreType.DMA((3 * n,))],
    )(*lands)


def _row_tile(rows):
    for cand in (256, 176, 128, 64, 32, 16):
        if rows % cand == 0:
            return cand
    raise ValueError(rows)


def _add_my_half(g, other, c_idx, name):
    _, k, n = g.shape
    hr = k // 2
    tr = _row_tile(hr)
    nb = hr // tr

    def body(c_ref, g_ref, o_ref, out_ref, out16_ref):
        s = g_ref[...] + o_ref[...]
        out_ref[...] = s
        out16_ref[...] = s.astype(BF16)

    return pl.pallas_call(
        body, name=name,
        grid_spec=pltpu.PrefetchScalarGridSpec(
            num_scalar_prefetch=1, grid=(N_CHIPS, nb),
            in_specs=[pl.BlockSpec((1, tr, n), lambda j, i, c: (j, c[0] * nb + i, 0)),
                      pl.BlockSpec((1, tr, n), lambda j, i, c: (j, i, 0))],
            out_specs=[pl.BlockSpec((1, tr, n), lambda j, i, c: (j, i, 0)),
                       pl.BlockSpec((1, tr, n), lambda j, i, c: (j, i, 0))]),
        out_shape=[jax.ShapeDtypeStruct((N_CHIPS, hr, n), F32), jax.ShapeDtypeStruct((N_CHIPS, hr, n), BF16)],
        compiler_params=_params(("parallel", "parallel")),
    )(c_idx, g, other)


def _add_chips(red, recv, chip_idx, name):
    _, hr, n = red.shape
    tr = _row_tile(hr)

    def body(k_ref, r_ref, v_ref, out_ref):
        out_ref[...] = ((r_ref[0] + v_ref[0].astype(F32)) + v_ref[1].astype(F32)) + v_ref[2].astype(F32)

    return pl.pallas_call(
        body, name=name,
        grid_spec=pltpu.PrefetchScalarGridSpec(
            num_scalar_prefetch=1, grid=(hr // tr,),
            in_specs=[pl.BlockSpec((1, tr, n), lambda i, k: (k[0], i, 0)),
                      pl.BlockSpec((3, tr, n), lambda i, k: (0, i, 0))],
            out_specs=pl.BlockSpec((tr, n), lambda i, k: (i, 0))),
        out_shape=jax.ShapeDtypeStruct((hr, n), F32),
        compiler_params=_params(("parallel",)),
    )(chip_idx, red, recv)


def _mod_shard(c_all, w_ada, b_ada):
    nb, cols = c_all.shape[0], w_ada.shape[1]

    def body(c_ref, w_ref, b_ref, o_ref):
        c = c_ref[...]
        o_ref[...] = _dot(c * jax.nn.sigmoid(c), w_ref[...]) + b_ref[...]

    return pl.pallas_call(
        body, name="mod_shard", out_shape=jax.ShapeDtypeStruct((nb, cols), F32),
        compiler_params=_params(vmem_mb=48),
    )(c_all, w_ada, b_ada)


def _proj(x2, mod8, w, seq, out_dtype, name):
    t = x2.shape[0]
    n = w.shape[1]
    tm, tn = min(2048, seq), min(1152, n)
    tpb = seq // tm

    def body(x_ref, mod_ref, w_ref, o_ref, h_ref):
        @pl.when(pl.program_id(1) == 0)
        def _():
            h_ref[...] = (x_ref[...] * (1.0 + mod_ref[0, 1:2, :]) + mod_ref[0, 0:1, :]).astype(BF16)
        o_ref[...] = jnp.dot(h_ref[...], w_ref[...], preferred_element_type=F32).astype(o_ref.dtype)

    return pl.pallas_call(
        body, name=name, grid=(t // tm, n // tn),
        in_specs=[pl.BlockSpec((tm, D), lambda i, j: (i, 0)),
                  pl.BlockSpec((1, 8, D), lambda i, j: (i // tpb, 0, 0)),
                  pl.BlockSpec((D, tn), lambda i, j: (0, j))],
        out_specs=[pl.BlockSpec((tm, tn), lambda i, j: (i, j)), pl.BlockSpec((tm, D), lambda i, j: (i, 0))],
        out_shape=[jax.ShapeDtypeStruct((t, n), out_dtype), jax.ShapeDtypeStruct((t, D), BF16)],
        compiler_params=_params(("parallel", "arbitrary"), 56),
    )(x2, mod8, w)


def _rows_matmul(a, w, name):
    t, k = a.shape
    n = w.shape[1]
    tm = 1024 if t % 1024 == 0 else t

    def body(a_ref, w_ref, o_ref):
        o_ref[...] = jnp.dot(a_ref[...], w_ref[...], preferred_element_type=F32)

    return pl.pallas_call(
        body, name=name, grid=(t // tm,),
        in_specs=[pl.BlockSpec((tm, k), lambda i: (i, 0)), pl.BlockSpec((k, n), lambda i: (0, 0))],
        out_specs=pl.BlockSpec((tm, n), lambda i: (i, 0)),
        out_shape=jax.ShapeDtypeStruct((t, n), F32),
        compiler_params=_params(("parallel",)),
    )(a, w)


def _tn_matmul(a, b, name, seq, split=None):
    a_st, b_st = a.ndim == 3, b.ndim == 3
    t, ka = a.shape[-2:]
    n = b.shape[-1]
    tt = min(1024, seq)
    nt = t // tt
    if a_st or b_st:
        steps, tn = (a.shape[0] if a_st else b.shape[0]), n
    else:
        tn = split
        if tn is None:
            tn = next(cand for cand in (1152, 1024, 1408, 512, n) if n % cand == 0)
        steps = n // tn
    stacked_out = a_st or b_st or split is not None

    def body(a_ref, b_ref, o_ref):
        part = _dot_tn(a_ref[0] if a_st else a_ref[...], b_ref[0] if b_st else b_ref[...])
        if stacked_out:
            part = part[None]

        @pl.when(pl.program_id(1) == 0)
        def _():
            o_ref[...] = part

        @pl.when(pl.program_id(1) > 0)
        def _():
            o_ref[...] += part

    if a_st:
        in_specs = [pl.BlockSpec((1, tt, ka), lambda j, k: (j, k, 0))]
    else:
        in_specs = [pl.BlockSpec((tt, ka), lambda j, k: (k, 0))]
    if b_st:
        in_specs.append(pl.BlockSpec((1, tt, n), lambda j, k: (j, k, 0)))
    else:
        in_specs.append(pl.BlockSpec((tt, tn), lambda j, k: (k, 0 if a_st else j)))
    if stacked_out:
        out_spec = pl.BlockSpec((1, ka, tn), lambda j, k: (j, 0, 0))
        out_shape = jax.ShapeDtypeStruct((steps, ka, tn), F32)
    else:
        out_spec = pl.BlockSpec((ka, tn), lambda j, k: (0, j))
        out_shape = jax.ShapeDtypeStruct((ka, n), F32)
    return pl.pallas_call(
        body, name=name, grid=(steps, nt), in_specs=in_specs, out_specs=out_spec, out_shape=out_shape,
        compiler_params=_params(("parallel", "arbitrary"), 56),
    )(a, b)


def _dh_kernel(dproj, w_p, x2, dxp, mod8, seq):
    t = x2.shape[0]
    tm, tk = min(1024, seq), 1152
    tpb = seq // tm
    nk = NP // tk
    nbatch = t // seq

    def body(dp_ref, w_ref, x_ref, dxp_ref, mod_ref, gx_ref, dm_ref, acc):
        i, k = pl.program_id(0), pl.program_id(1)

        @pl.when(k == 0)
        def _():
            acc[...] = jnp.zeros_like(acc)

        acc[...] += _dot_nt(dp_ref[...], w_ref[...])

        @pl.when(k == nk - 1)
        def _():
            dh = acc[...]
            gx_ref[...] = dxp_ref[...] + dh * (1.0 + mod_ref[0, 1:2, :])
            upd = jnp.concatenate(
                [jnp.sum(dh, axis=0, keepdims=True), jnp.sum(dh * x_ref[...], axis=0, keepdims=True),
                 jnp.zeros((6, D), F32)], axis=0)

            @pl.when(i % tpb == 0)
            def _():
                dm_ref[0] = upd

            @pl.when(i % tpb != 0)
            def _():
                dm_ref[0] += upd

    return pl.pallas_call(
        body, name="dh", grid=(t // tm, nk),
        in_specs=[pl.BlockSpec((tm, tk), lambda i, k: (i, k)),
                  pl.BlockSpec((D, tk), lambda i, k: (0, k)),
                  pl.BlockSpec((tm, D), lambda i, k: (i, 0)),
                  pl.BlockSpec((tm, D), lambda i, k: (i, 0)),
                  pl.BlockSpec((1, 8, D), lambda i, k: (i // tpb, 0, 0))],
        out_specs=[pl.BlockSpec((tm, D), lambda i, k: (i, 0)),
                   pl.BlockSpec((1, 8, D), lambda i, k: (i // tpb, 0, 0))],
        out_shape=[jax.ShapeDtypeStruct((t, D), F32), jax.ShapeDtypeStruct((nbatch, 8, D), F32)],
        scratch_shapes=[pltpu.VMEM((tm, D), F32)],
        compiler_params=_params(("arbitrary", "arbitrary"), 48),
    )(dproj, w_p, x2, dxp, mod8)


def _tri(n, upper):
    r = lax.broadcasted_iota(jnp.int32, (n, n), 0)
    c = lax.broadcasted_iota(jnp.int32, (n, n), 1)
    return jnp.where((c >= r) if upper else (c <= r), 1.0, 0.0).astype(F32)


@jax.custom_vjp
def _mm_nn(a, b):
    return _dot(a, b)


_mm_nn.defvjp(lambda a, b: (_dot(a, b), (a, b)),
              lambda res, g: (_dot_nt(g, res[1]), _dot_tn(res[0], g)))


@jax.custom_vjp
def _mm_nt(a, b):
    return _dot_nt(a, b)


_mm_nt.defvjp(lambda a, b: (_dot_nt(a, b), (a, b)),
              lambda res, g: (_dot(g, res[1]), _dot_tn(g, res[0])))


@jax.custom_vjp
def _mm_tn(a, b):
    return _dot_tn(a, b)


_mm_tn.defvjp(lambda a, b: (_dot_tn(a, b), (a, b)),
              lambda res, g: (_dot_nt(res[1], g), _dot(res[0], g)))


@jax.custom_vjp
def _cumsum_rows(x):
    return _dot_f32(_tri(x.shape[0], False), x)


_cumsum_rows.defvjp(lambda x: (_cumsum_rows(x), None),
                    lambda _, g: (_dot_f32(_tri(g.shape[0], True), g),))


@functools.partial(jax.custom_vjp, nondiff_argnums=(1,))
def _shift_rows(x, k):
    return pltpu.roll(x, k % x.shape[0], 0)


_shift_rows.defvjp(lambda x, k: (_shift_rows(x, k), None),
                   lambda k, _, g: (pltpu.roll(g, (-k) % g.shape[0], 0),))


def _group_ref(bc, m):
    n = bc.shape[0] // (2 * m)
    b3 = bc.reshape(n, 2 * m, ADH)
    row = lax.broadcasted_iota(jnp.int32, b3.shape, 1)
    ref = jnp.sum(jnp.where(row == m - 1, b3, 0.0), axis=1, keepdims=True)
    return jnp.broadcast_to(ref, b3.shape).reshape(bc.shape)


def _hgrn_block(q, fl, v, g, st, lb, nw):
    n = q.shape[0]
    f = lb + (1.0 - lb) * jax.nn.sigmoid(fl)
    kk = 1.0 - f
    lf = jnp.log(f)
    bc = _cumsum_rows(lf)
    row = lax.broadcasted_iota(jnp.int32, (n, ADH), 0)
    same = jnp.bitwise_xor(lax.broadcasted_iota(jnp.int32, (n, n), 0), lax.broadcasted_iota(jnp.int32, (n, n), 1))
    a = jnp.zeros((n, n), F32)
    m = 1
    while m < n:
        r = jnp.bitwise_and(row, 2 * m - 1)
        up, lo = r >= m, r < m
        if m == 1:
            aq, ak = lf, jnp.zeros_like(lf)
        elif m == 2:
            aq = jnp.where(r == 3, lf + _shift_rows(lf, 1), lf)
            ak = jnp.where(r == 0, _shift_rows(lf, -1), 0.0)
        else:
            ref = _group_ref(bc, m)
            aq, ak = bc - ref, ref - bc
        qt = jnp.where(up, q * jnp.exp(jnp.where(up, aq, 0.0)), 0.0)
        kt = jnp.where(lo, kk * jnp.exp(jnp.where(lo, ak, 0.0)), 0.0)
        a = a + jnp.where(same < 2 * m, _mm_nt(qt, kt), 0.0)
        m *= 2
    last = row == n - 1
    bl = jnp.sum(jnp.where(last, bc, 0.0), axis=0, keepdims=True)
    o = _mm_nn(a, v) + _mm_nt(q * jnp.exp(bc), st) + jnp.sum(q * kk, axis=-1, keepdims=True) * v
    st_new = st * jnp.exp(bl) + _mm_tn(v, kk * jnp.exp(bl - bc))
    rms = lax.rsqrt(jnp.mean(o * o, axis=-1, keepdims=True) + RMS_EPS)
    return o * rms * nw * jax.nn.sigmoid(g), st_new


def _hgrn_fwd(proj, lb_logits, norm_w, nbatch, seq):
    t = proj.shape[0]
    blk = min(HGRN_BLOCK, seq)
    nb = seq // blk

    nh = HGRN_HEADS
    wp, wy = 512 * nh, ADH * nh

    def body(p_ref, lbl_ref, nw_ref, y_ref, ck_ref, st_s):
        @pl.when(pl.program_id(2) == 0)
        def _():
            st_s[...] = jnp.zeros_like(st_s)

        st = [st_s[h] for h in range(nh)]
        p = p_ref[...].astype(F32)
        lb = jax.nn.sigmoid(lbl_ref[0:1, :] - lbl_ref[1:2, :])
        nw = nw_ref[...]
        res = [_hgrn_block(*(p[:, 512 * h + 128 * k:512 * h + 128 * k + 128] for k in range(4)), st[h],
                           lb[:, 128 * h:128 * h + 128], nw[:, 128 * h:128 * h + 128]) for h in range(nh)]
        for h in range(nh):
            ck_ref[0, h] = st[h]
            st_s[h] = res[h][1]
        y_ref[...] = jnp.concatenate([r[0] for r in res], axis=1).astype(y_ref.dtype)

    return pl.pallas_call(
        body, name="hgrn_fwd", grid=(AH // nh, nbatch, nb),
        in_specs=[pl.BlockSpec((blk, wp), lambda h, b, i: (b * nb + i, COL_A // wp + h)),
                  pl.BlockSpec((2, wy), lambda h, b, i: (0, h)),
                  pl.BlockSpec((1, wy), lambda h, b, i: (0, h))],
        out_specs=[pl.BlockSpec((blk, wy), lambda h, b, i: (b * nb + i, h)),
                   pl.BlockSpec((1, nh, 128, 128), lambda h, b, i: ((h * nbatch + b) * nb + i, 0, 0, 0))],
        out_shape=[jax.ShapeDtypeStruct((t, AW), BF16),
                   jax.ShapeDtypeStruct((AH // nh * nbatch * nb, nh, 128, 128), F32)],
        scratch_shapes=[pltpu.VMEM((nh, 128, 128), F32)],
        compiler_params=_params(("parallel", "parallel", "arbitrary"), 48),
    )(proj, lb_logits, norm_w)


def _hgrn_bwd(proj, dya, ckpt, lb_logits, norm_w, dproj, nbatch, seq):
    t = proj.shape[0]
    blk = min(HGRN_BLOCK, seq)
    nb = seq // blk

    nh = HGRN_HEADS
    wp, wy = 512 * nh, ADH * nh

    def body(p_ref, dy_ref, ck_ref, lbl_ref, nw_ref, dp_in, dp_ref, sm_ref, dst_s):
        del dp_in
        b_id, i = pl.program_id(1), pl.program_id(2)

        @pl.when(i == 0)
        def _():
            dst_s[...] = jnp.zeros_like(dst_s)

        dst = [dst_s[h] for h in range(nh)]
        st = [ck_ref[0, h] for h in range(nh)]
        p = p_ref[...].astype(F32)
        dy = dy_ref[...]
        lb = jax.nn.sigmoid(lbl_ref[0:1, :] - lbl_ref[1:2, :])
        nw = nw_ref[...]
        grads = []
        for h in range(nh):
            _, pullback = jax.vjp(_hgrn_block, *(p[:, 512 * h + 128 * k:512 * h + 128 * k + 128] for k in range(4)),
                                  st[h], lb[:, 128 * h:128 * h + 128], nw[:, 128 * h:128 * h + 128])
            grads.append(pullback((dy[:, 128 * h:128 * h + 128], dst[h])))
        for h in range(nh):
            dst_s[h] = grads[h][4]
        dp_ref[...] = jnp.concatenate([g[k] for g in grads for k in range(4)], axis=1).astype(dp_ref.dtype)
        upd = jnp.concatenate([jnp.concatenate([g[5] for g in grads], axis=1),
                               jnp.concatenate([g[6] for g in grads], axis=1), jnp.zeros((6, wy), F32)], axis=0)
        first = (b_id == 0) & (i == 0)

        @pl.when(first)
        def _():
            sm_ref[...] = upd

        @pl.when(jnp.logical_not(first))
        def _():
            sm_ref[...] += upd

    def rows(h, b, i):
        return b * nb + (nb - 1 - i)

    return pl.pallas_call(
        body, name="hgrn_bwd", grid=(AH // nh, nbatch, nb),
        in_specs=[pl.BlockSpec((blk, wp), lambda h, b, i: (rows(h, b, i), COL_A // wp + h)),
                  pl.BlockSpec((blk, wy), lambda h, b, i: (rows(h, b, i), h)),
                  pl.BlockSpec((1, nh, 128, 128), lambda h, b, i: ((h * nbatch + b) * nb + (nb - 1 - i), 0, 0, 0)),
                  pl.BlockSpec((2, wy), lambda h, b, i: (0, h)),
                  pl.BlockSpec((1, wy), lambda h, b, i: (0, h)),
                  pl.BlockSpec(memory_space=pl.ANY)],
        out_specs=[pl.BlockSpec((blk, wp), lambda h, b, i: (rows(h, b, i), COL_A // wp + h)),
                   pl.BlockSpec((8, wy), lambda h, b, i: (0, h))],
        out_shape=[jax.ShapeDtypeStruct((t, NP), BF16), jax.ShapeDtypeStruct((8, AW), F32)],
        input_output_aliases={5: 0},
        scratch_shapes=[pltpu.VMEM((nh, 128, 128), F32)],
        compiler_params=_params(("parallel", "arbitrary", "arbitrary"), 56),
    )(proj, dya, ckpt, lb_logits, norm_w, dproj)


def _log_sigmoid(z):
    return jnp.minimum(z, 0.0) - jnp.log(1.0 + jnp.exp(-jnp.abs(z)))


def _fox_cum(proj, bias128, nbatch, seq):
    t = proj.shape[0]
    ts = min(512, seq)
    nb = seq // ts

    def body(p_ref, b_ref, c_ref, carry):
        @pl.when(pl.program_id(1) == 0)
        def _():
            carry[...] = jnp.zeros_like(carry)
        cum = _dot_f32(_tri(ts, False), _log_sigmoid(p_ref[...] + b_ref[...])) + carry[...]
        carry[...] = cum[ts - 1:ts, :]
        cum2 = cum * LOG2E
        lane = lax.broadcasted_iota(jnp.int32, (ts, 128), 1)
        for p in range(4):
            c_ref[p] = jnp.where(lane < 64, cum2[:, 2 * p:2 * p + 1], cum2[:, 2 * p + 1:2 * p + 2])

    return pl.pallas_call(
        body, name="fox_cum", grid=(nbatch, nb),
        in_specs=[pl.BlockSpec((ts, 128), lambda b, i: (b * nb + i, 0)),
                  pl.BlockSpec((1, 128), lambda b, i: (0, 0))],
        out_specs=pl.BlockSpec((4, ts, 128), lambda b, i: (0, b * nb + i, 0)),
        out_shape=jax.ShapeDtypeStruct((4, t, 128), F32),
        scratch_shapes=[pltpu.VMEM((1, 128), F32)],
        compiler_params=_params(("parallel", "arbitrary")),
    )(proj, bias128)


def _fox_scores_t(q128, k128, cc128, hh, masked):
    tq, tk = q128.shape[0], k128.shape[0]
    qh = _head_lanes((q128 * (LOG2E * BDH ** -0.5)).astype(BF16), hh)
    s = _dot_nt(k128, qh) - cc128[:, 64 * hh:64 * hh + 1]
    if masked:
        key = lax.broadcasted_iota(jnp.int32, (tk, tq), 0)
        qry = lax.broadcasted_iota(jnp.int32, (tk, tq), 1)
        s = jnp.where(key <= qry, s, NEG)
    return s


def _causal_pairs(nq, key_major):
    if key_major:
        pairs = [(i, j) for j in range(nq) for i in range(j, nq)]
    else:
        pairs = [(i, j) for i in range(nq) for j in range(i + 1)]
    return (jnp.asarray([p[0] for p in pairs], jnp.int32), jnp.asarray([p[1] for p in pairs], jnp.int32))


def _head_lanes(x128, hh):
    lane = lax.broadcasted_iota(jnp.int32, x128.shape, 1)
    return jnp.where((lane < 64) if hh == 0 else (lane >= 64), x128, jnp.zeros_like(x128))


def _with_ones_lane(x128, hh):
    lane = lax.broadcasted_iota(jnp.int32, x128.shape, 1)
    one = jnp.ones_like(x128)
    zero = jnp.zeros_like(x128)
    if hh == 0:
        return jnp.where(lane < 64, x128, jnp.where(lane == 64, one, zero))
    return jnp.where(lane >= 64, x128, jnp.where(lane == 0, one, zero))


def _fox_fwd(proj, cum_cols, nbatch, seq):
    t = proj.shape[0]
    tq = tk = min(512, seq)
    nq = seq // tq
    npr = FOX_PAIRS
    qi, kj = _causal_pairs(nq, key_major=False)

    def body(qi_ref, kj_ref, q_ref, kv_ref, cc_ref, o_ref, lse_ref, m_s, acc_s):
        s_id = pl.program_id(2)
        i, j = qi_ref[s_id], kj_ref[s_id]

        @pl.when(j == 0)
        def _():
            m_s[...] = jnp.full_like(m_s, NEG)
            acc_s[...] = jnp.zeros_like(acc_s)

        def step(masked):
            heads = [(pr, hh) for pr in range(npr) for hh in range(2)]
            m_prev = m_s[0:2 * npr, :]
            acc_prev = [acc_s[h] for h in range(2 * npr)]
            q128 = [q_ref[:, 128 * pr:128 * pr + 128] for pr in range(npr)]
            k128 = [kv_ref[:, 256 * pr:256 * pr + 128].astype(BF16) for pr in range(npr)]
            v128 = [kv_ref[:, 256 * pr + 128:256 * pr + 256].astype(BF16) for pr in range(npr)]
            s = [_fox_scores_t(q128[pr], k128[pr], cc_ref[pr], hh, masked) for pr, hh in heads]
            m_new = [jnp.maximum(m_prev[h:h + 1, :], jnp.max(s[h], axis=0, keepdims=True)) for h in range(2 * npr)]
            acc_new = []
            for h, (pr, hh) in enumerate(heads):
                alpha = jnp.exp2(m_prev[h:h + 1, :] - m_new[h])
                p = jnp.exp2(s[h] - m_new[h]).astype(BF16)
                acc_new.append(acc_prev[h] * alpha + _dot_tn(_with_ones_lane(v128[pr], hh), p))
            for h in range(2 * npr):
                acc_s[h] = acc_new[h]
            m_s[0:2 * npr, :] = jnp.concatenate(m_new, axis=0)

        @pl.when(j < i)
        def _():
            step(False)

        @pl.when(j == i)
        def _():
            step(True)
            outs = []
            for pr in range(npr):
                a0, a1 = acc_s[2 * pr], acc_s[2 * pr + 1]
                l0, l1 = a0[64:65, :], a1[0:1, :]
                outs.append(jnp.concatenate([a0[0:64, :] / l0, a1[64:128, :] / l1], axis=0).T)
                lse_ref[0, pr] = jnp.concatenate(
                    [m_s[2 * pr:2 * pr + 1, :] + jnp.log2(l0), m_s[2 * pr + 1:2 * pr + 2, :] + jnp.log2(l1),
                     jnp.zeros((6, tq), F32)], axis=0)
            o_ref[...] = jnp.concatenate(outs, axis=1).astype(o_ref.dtype)

    return pl.pallas_call(
        body, name="fox_fwd",
        grid_spec=pltpu.PrefetchScalarGridSpec(
            num_scalar_prefetch=2, grid=(nbatch, 4 // npr, qi.shape[0]),
            in_specs=[pl.BlockSpec((tq, 128 * npr), lambda b, p, s, qi, kj: (b * nq + qi[s], COL_BQ // (128 * npr) + p)),
                      pl.BlockSpec((tk, 256 * npr), lambda b, p, s, qi, kj: (b * nq + kj[s], COL_KV // (256 * npr) + p)),
                      pl.BlockSpec((npr, tk, 128), lambda b, p, s, qi, kj: (p, b * nq + kj[s], 0))],
            out_specs=[pl.BlockSpec((tq, 128 * npr), lambda b, p, s, qi, kj: (b * nq + qi[s], p)),
                       pl.BlockSpec((1, npr, 8, tq), lambda b, p, s, qi, kj: (b, p, 0, qi[s]))],
            scratch_shapes=[pltpu.VMEM((8, tq), F32), pltpu.VMEM((2 * npr, 128, tq), F32)]),
        out_shape=[jax.ShapeDtypeStruct((t, 512), BF16), jax.ShapeDtypeStruct((nbatch, 4, 8, seq), F32)],
        compiler_params=_params(("parallel", "parallel", "arbitrary"), 56),
    )(qi, kj, proj, proj, cum_cols)


def _fox_bwd(proj, cum_cols, lse, yb, dyb, dproj, nbatch, seq):
    t = proj.shape[0]
    tq = tk = min(512, seq)
    nq = seq // tq
    scale = BDH ** -0.5
    qi, kj = _causal_pairs(nq, key_major=True)
    nsteps = qi.shape[0]

    npr = FOX_PAIRS

    def body(qi_ref, kj_ref, q_ref, kv_ref, cc_ref, lse_ref, o_ref, do_ref, dp_in,
             dkv_ref, dq_ref, drs_ref, dcs_ref, dk_s, dv_s, dqa_s):
        del dp_in
        pg, s_id = pl.program_id(1), pl.program_id(2)
        i, j = qi_ref[s_id], kj_ref[s_id]

        @pl.when(i == j)
        def _():
            dk_s[...] = jnp.zeros_like(dk_s)
            dv_s[...] = jnp.zeros_like(dv_s)

        @pl.when(s_id == 0)
        def _():
            dqa_s[...] = jnp.zeros_like(dqa_s)

        def step(masked):
            dk_prev = [dk_s[h] for h in range(2 * npr)]
            dq_prev = [dqa_s[i, h] for h in range(2 * npr)]
            dv_new = [dv_s[pr] for pr in range(npr)]
            dk_new, dq_new = [], []
            for pr in range(npr):
                lanes = slice(128 * pr, 128 * pr + 128)
                q128 = q_ref[:, lanes]
                qs128 = (q128 * scale).astype(BF16)
                k128 = kv_ref[:, 256 * pr:256 * pr + 128].astype(BF16)
                v128 = kv_ref[:, 256 * pr + 128:256 * pr + 256].astype(BF16)
                do128 = do_ref[:, lanes]
                doo = do128 * o_ref[:, lanes].astype(F32)
                do16 = do128.astype(BF16)
                for hh in range(2):
                    s = _fox_scores_t(q128, k128, cc_ref[pr], hh, masked)
                    p = jnp.exp2(s - lse_ref[0, pr, hh:hh + 1, :])
                    dd = lax.dot_general(jnp.ones((8, 128), F32), _head_lanes(doo, hh), (((1,), (1,)), ((), ())),
                                         preferred_element_type=F32, precision=HIGHEST)[0:1, :]
                    doh = _head_lanes(do16, hh)
                    dp = _dot_nt(v128, doh)
                    ds = (p * (dp - dd)).astype(BF16)
                    dv_new[pr] = dv_new[pr] + _dot(p, doh)
                    dk_new.append(dk_prev[2 * pr + hh] + _dot(ds, _with_ones_lane(qs128, hh)))
                    dq_new.append(dq_prev[2 * pr + hh] + _dot_tn(_with_ones_lane(k128, hh), ds))
            for pr in range(npr):
                dv_s[pr] = dv_new[pr]
            for h in range(2 * npr):
                dk_s[h] = dk_new[h]
                dqa_s[i, h] = dq_new[h]

        @pl.when(i == j)
        def _():
            step(True)

        @pl.when(i > j)
        def _():
            step(False)

        def sums_to_lanes(lane, pr, s0, s1):
            hp = npr * pg + pr
            return jnp.where(lane == 2 * hp, s0, jnp.where(lane == 2 * hp + 1, s1, 0.0))

        @pl.when(i == nq - 1)
        def _():
            lane = lax.broadcasted_iota(jnp.int32, (tk, 128), 1)
            for pr in range(npr):
                k0, k1 = dk_s[2 * pr], dk_s[2 * pr + 1]
                dkv_ref[:, 256 * pr:256 * pr + 128] = jnp.where(lane < 64, k0, k1).astype(dkv_ref.dtype)
                dkv_ref[:, 256 * pr + 128:256 * pr + 256] = dv_s[pr].astype(dkv_ref.dtype)
                dcs_ref[pr] = sums_to_lanes(lane, pr, k0[:, 64:65], k1[:, 0:1])

        @pl.when(s_id == nsteps - 1)
        def _():
            lane = lax.broadcasted_iota(jnp.int32, (tq, 128), 1)
            for blk in range(nq):
                rows = pl.ds(blk * tq, tq)
                for pr in range(npr):
                    a0 = dqa_s[blk, 2 * pr].T
                    a1 = dqa_s[blk, 2 * pr + 1].T
                    dq_ref[rows, 128 * pr:128 * pr + 128] = (jnp.where(lane < 64, a0, a1) * scale).astype(dq_ref.dtype)
                    drs_ref[pr, rows, :] = sums_to_lanes(lane, pr, a0[:, 64:65], a1[:, 0:1])

    return pl.pallas_call(
        body, name="fox_bwd",
        grid_spec=pltpu.PrefetchScalarGridSpec(
            num_scalar_prefetch=2, grid=(nbatch, 4 // npr, nsteps),
            in_specs=[pl.BlockSpec((tq, 128 * npr), lambda b, p, s, qi, kj: (b * nq + qi[s], COL_BQ // (128 * npr) + p)),
                      pl.BlockSpec((tk, 256 * npr), lambda b, p, s, qi, kj: (b * nq + kj[s], COL_KV // (256 * npr) + p)),
                      pl.BlockSpec((npr, tk, 128), lambda b, p, s, qi, kj: (p, b * nq + kj[s], 0)),
                      pl.BlockSpec((1, npr, 8, tq), lambda b, p, s, qi, kj: (b, p, 0, qi[s])),
                      pl.BlockSpec((tq, 128 * npr), lambda b, p, s, qi, kj: (b * nq + qi[s], p)),
                      pl.BlockSpec((tq, 128 * npr), lambda b, p, s, qi, kj: (b * nq + qi[s], p)),
                      pl.BlockSpec(memory_space=pl.ANY)],
            out_specs=[pl.BlockSpec((tk, 256 * npr), lambda b, p, s, qi, kj: (b * nq + kj[s], COL_KV // (256 * npr) + p)),
                       pl.BlockSpec((seq, 128 * npr), lambda b, p, s, qi, kj: (b, p)),
                       pl.BlockSpec((npr, seq, 128), lambda b, p, s, qi, kj: (p, b, 0)),
                       pl.BlockSpec((npr, tk, 128), lambda b, p, s, qi, kj: (p, b * nq + kj[s], 0))],
            scratch_shapes=[pltpu.VMEM((2 * npr, tk, 128), F32), pltpu.VMEM((npr, tk, 128), F32),
                            pltpu.VMEM((nq, 2 * npr, 128, tq), F32)]),
        out_shape=[jax.ShapeDtypeStruct((t, NP), BF16), jax.ShapeDtypeStruct((t, 512), BF16),
                   jax.ShapeDtypeStruct((4, t, 128), F32), jax.ShapeDtypeStruct((4, t, 128), F32)],
        input_output_aliases={8: 0},
        compiler_params=_params(("parallel", "parallel", "arbitrary"), 60),
    )(qi, kj, proj, proj, cum_cols, lse, yb, dyb, dproj)


def _place_cols(dproj, src, col):
    t, w = src.shape
    tm = 1024 if t % 1024 == 0 else t

    def body(s_ref, dp_in, o_ref):
        del dp_in
        o_ref[...] = s_ref[...]

    return pl.pallas_call(
        body, name="place_cols", grid=(t // tm,),
        in_specs=[pl.BlockSpec((tm, w), lambda i: (i, 0)), pl.BlockSpec(memory_space=pl.ANY)],
        out_specs=pl.BlockSpec((tm, w), lambda i: (i, col // w)),
        out_shape=jax.ShapeDtypeStruct(dproj.shape, dproj.dtype),
        input_output_aliases={1: 0},
        compiler_params=_params(("parallel",)),
    )(src, dproj)


def _fox_dbf(proj, bias128, drs, dcs, dproj, nbatch, seq):
    t = proj.shape[0]
    ts = min(512, seq)
    nb = seq // ts

    def body(p_ref, b_ref, dr_ref, dc_ref, dp_in, dp_ref, sm_ref, carry):
        del dp_in
        b_id, i = pl.program_id(0), pl.program_id(1)

        @pl.when(i == 0)
        def _():
            carry[...] = jnp.zeros_like(carry)

        dcum = (dr_ref[0] - dc_ref[0]) + (dr_ref[1] - dc_ref[1]) + (dr_ref[2] - dc_ref[2]) + (dr_ref[3] - dc_ref[3])
        rc = _dot_f32(_tri(ts, True), dcum) + carry[...]
        carry[...] = rc[0:1, :]
        z = p_ref[...] + b_ref[...]
        lane = lax.broadcasted_iota(jnp.int32, (ts, 128), 1)
        dz = jnp.where(lane < BH, rc * jax.nn.sigmoid(-z), 0.0)
        dp_ref[...] = dz.astype(dp_ref.dtype)
        upd = jnp.concatenate([jnp.sum(dz, axis=0, keepdims=True), jnp.zeros((7, 128), F32)], axis=0)
        first = (b_id == 0) & (i == 0)

        @pl.when(first)
        def _():
            sm_ref[...] = upd

        @pl.when(jnp.logical_not(first))
        def _():
            sm_ref[...] += upd

    def rows(b, i):
        return b * nb + (nb - 1 - i)

    return pl.pallas_call(
        body, name="fox_dbf", grid=(nbatch, nb),
        in_specs=[pl.BlockSpec((ts, 128), lambda b, i: (rows(b, i), 0)),
                  pl.BlockSpec((1, 128), lambda b, i: (0, 0)),
                  pl.BlockSpec((4, ts, 128), lambda b, i: (0, rows(b, i), 0)),
                  pl.BlockSpec((4, ts, 128), lambda b, i: (0, rows(b, i), 0)),
                  pl.BlockSpec(memory_space=pl.ANY)],
        out_specs=[pl.BlockSpec((ts, 128), lambda b, i: (rows(b, i), COL_BF // 128)),
                   pl.BlockSpec((8, 128), lambda b, i: (0, 0))],
        out_shape=[jax.ShapeDtypeStruct((t, NP), BF16), jax.ShapeDtypeStruct((8, 128), F32)],
        input_output_aliases={4: 0},
        scratch_shapes=[pltpu.VMEM((1, 128), F32)],
        compiler_params=_params(("arbitrary", "arbitrary")),
    )(proj, bias128, drs, dcs, dproj)


def _ln_stats(z):
    mu = jnp.mean(z, axis=-1, keepdims=True)
    zc = z - mu
    rstd = lax.rsqrt(jnp.mean(zc * zc, axis=-1, keepdims=True) + LN_EPS)
    return zc * rstd, rstd


def _ln_bwd(dy, xhat, rstd, w):
    dxh = dy * w
    return rstd * (dxh - jnp.mean(dxh, axis=-1, keepdims=True) - xhat * jnp.mean(dxh * xhat, axis=-1, keepdims=True))


def _merge_fwd(ya, yb, proj, x2, mod8, wba, wbb, wout, ln1w, ln1b, seq):
    t = x2.shape[0]
    tm = min(512, seq)
    tpb = seq // tm

    def body(ya_ref, yb_ref, g_ref, x_ref, mod_ref, wa_ref, wb_ref, wo_ref, lw_ref, lb_ref, mg_ref, u_ref, x1_ref):
        ga = jax.nn.sigmoid(g_ref[:, 0:D].astype(F32))
        gb = jax.nn.sigmoid(g_ref[:, D:2 * D].astype(F32))
        merged = (ga * jnp.dot(ya_ref[...], wa_ref[...], preferred_element_type=F32)
                  + gb * jnp.dot(yb_ref[...], wb_ref[...], preferred_element_type=F32))
        mg = merged.astype(BF16)
        mg_ref[...] = mg
        u = jnp.dot(mg, wo_ref[...], preferred_element_type=F32)
        u_ref[...] = u
        xhat, _ = _ln_stats(ALPHA * x_ref[...] + (1.0 + mod_ref[0, 2:3, :]) * u)
        x1_ref[...] = xhat * lw_ref[...] + lb_ref[...]

    tok = lambda w: pl.BlockSpec((tm, w), lambda i: (i, 0))
    full = lambda a: pl.BlockSpec(a.shape, lambda i: (0,) * a.ndim)
    return pl.pallas_call(
        body, name="merge_fwd", grid=(t // tm,),
        in_specs=[tok(512), tok(512), pl.BlockSpec((tm, 2048), lambda i: (i, COL_GATES // 2048)), tok(D),
                  pl.BlockSpec((1, 8, D), lambda i: (i // tpb, 0, 0)),
                  full(wba), full(wbb), full(wout), full(ln1w), full(ln1b)],
        out_specs=[tok(D), tok(D), tok(D)],
        out_shape=[jax.ShapeDtypeStruct((t, D), BF16), jax.ShapeDtypeStruct((t, D), F32),
                   jax.ShapeDtypeStruct((t, D), F32)],
        compiler_params=_params(("parallel",), 48),
    )(ya, yb, proj, x2, mod8, wba, wbb, wout, ln1w, ln1b)


def _merge_bwd(du, ya, yb, proj, wba, wbb, wout, token, seq):
    t = du.shape[0]
    tm = min(512, seq)

    def body(du_ref, ya_ref, yb_ref, g_ref, wa_ref, wb_ref, wo_ref, token_ref,
             dp_ref, dpa_ref, dpb_ref, dya_ref, dyb_ref):
        del token_ref
        ga = jax.nn.sigmoid(g_ref[:, 0:D].astype(F32))
        gb = jax.nn.sigmoid(g_ref[:, D:2 * D].astype(F32))
        dm = _dot_nt(du_ref[...], wo_ref[...])
        pa = jnp.dot(ya_ref[...], wa_ref[...], preferred_element_type=F32)
        pb = jnp.dot(yb_ref[...], wb_ref[...], preferred_element_type=F32)
        dpa = (dm * ga).astype(BF16)
        dpb = (dm * gb).astype(BF16)
        dpa_ref[...] = dpa
        dpb_ref[...] = dpb
        dp_ref[:, 0:D] = (dm * pa * ga * (1.0 - ga)).astype(BF16)
        dp_ref[:, D:2 * D] = (dm * pb * gb * (1.0 - gb)).astype(BF16)
        dya_ref[...] = _dot_nt(dpa, wa_ref[...])
        dyb_ref[...] = _dot_nt(dpb, wb_ref[...])

    tok = lambda w: pl.BlockSpec((tm, w), lambda i: (i, 0))
    full = lambda a: pl.BlockSpec(a.shape, lambda i: (0,) * a.ndim)
    return pl.pallas_call(
        body, name="merge_bwd", grid=(t // tm,),
        in_specs=[tok(D), tok(512), tok(512), pl.BlockSpec((tm, 2048), lambda i: (i, COL_GATES // 2048)),
                  full(wba), full(wbb), full(wout), full(token)],
        out_specs=[pl.BlockSpec((tm, 2048), lambda i: (i, COL_GATES // 2048)), tok(D), tok(D), tok(512), tok(512)],
        out_shape=[jax.ShapeDtypeStruct((t, NP), BF16), jax.ShapeDtypeStruct((t, D), BF16),
                   jax.ShapeDtypeStruct((t, D), BF16), jax.ShapeDtypeStruct((t, 512), F32),
                   jax.ShapeDtypeStruct((t, 512), F32)],
        compiler_params=_params(("parallel",), 48),
    )(du, ya, yb, proj, wba, wbb, wout, token)


def _ffn_fwd(x1, mod8, wg, wu, wd, target, ln2w, ln2b, seq):
    t = x1.shape[0]
    tm = min(FFN_TOKENS, seq)
    nf, tf, _ = wg.shape
    tpb = seq // tm
    nbatch = t // seq

    def body(x_ref, mod_ref, wg_ref, wu_ref, wd_ref, t_ref, lw_ref, lb_ref,
             a_ref, b_ref, h_s, dz_ref, st_ref, dm_ref, acc):
        i, j = pl.program_id(0), pl.program_id(1)

        @pl.when(j == 0)
        def _():
            h_s[...] = (x_ref[...] * (1.0 + mod_ref[0, 4:5, :]) + mod_ref[0, 3:4, :]).astype(BF16)
            acc[...] = jnp.zeros_like(acc)

        a = _dot_nt(h_s[...], wg_ref[0])
        b = _dot_nt(h_s[...], wu_ref[0])
        a_ref[0] = a.astype(BF16)
        b_ref[0] = b.astype(BF16)
        acc[...] += _dot(a * jax.nn.sigmoid(a) * b, wd_ref[0])

        @pl.when(j == nf - 1)
        def _():
            ffn = acc[...]
            xhat, rstd = _ln_stats(ALPHA * x_ref[...] + (1.0 + mod_ref[0, 5:6, :]) * ffn)
            diff = xhat * lw_ref[...] + lb_ref[...] - t_ref[...]
            loss = 0.5 * jnp.sum(jnp.sum(diff * diff, axis=-1, keepdims=True), axis=0, keepdims=True) / D
            dy = diff * (1.0 / D)
            dz = _ln_bwd(dy, xhat, rstd, lw_ref[...])
            dz_ref[...] = dz
            lane = lax.broadcasted_iota(jnp.int32, (1, D), 1)
            upd = jnp.concatenate(
                [jnp.sum(dy * xhat, axis=0, keepdims=True), jnp.sum(dy, axis=0, keepdims=True),
                 jnp.where(lane == 0, loss, 0.0), jnp.zeros((5, D), F32)], axis=0)
            dmu = jnp.concatenate(
                [jnp.zeros((5, D), F32), jnp.sum(dz * ffn, axis=0, keepdims=True), jnp.zeros((2, D), F32)], axis=0)

            @pl.when(i == 0)
            def _():
                st_ref[...] = upd

            @pl.when(i > 0)
            def _():
                st_ref[...] += upd

            @pl.when(i % tpb == 0)
            def _():
                dm_ref[0] = dmu

            @pl.when(i % tpb != 0)
            def _():
                dm_ref[0] += dmu

    row = lambda: pl.BlockSpec((tm, D), lambda i, j: (i, 0))
    vec = lambda: pl.BlockSpec((1, D), lambda i, j: (0, 0))
    return pl.pallas_call(
        body, name="ffn_fwd", grid=(t // tm, nf),
        in_specs=[row(), pl.BlockSpec((1, 8, D), lambda i, j: (i // tpb, 0, 0)),
                  pl.BlockSpec((1, tf, D), lambda i, j: (j, 0, 0)), pl.BlockSpec((1, tf, D), lambda i, j: (j, 0, 0)),
                  pl.BlockSpec((1, tf, D), lambda i, j: (j, 0, 0)), row(), vec(), vec()],
        out_specs=[pl.BlockSpec((1, tm, tf), lambda i, j: (j, i, 0)), pl.BlockSpec((1, tm, tf), lambda i, j: (j, i, 0)),
                   row(), row(), pl.BlockSpec((8, D), lambda i, j: (0, 0)),
                   pl.BlockSpec((1, 8, D), lambda i, j: (i // tpb, 0, 0))],
        out_shape=[jax.ShapeDtypeStruct((nf, t, tf), BF16), jax.ShapeDtypeStruct((nf, t, tf), BF16),
                   jax.ShapeDtypeStruct((t, D), BF16),
                   jax.ShapeDtypeStruct((t, D), F32), jax.ShapeDtypeStruct((8, D), F32),
                   jax.ShapeDtypeStruct((nbatch, 8, D), F32)],
        scratch_shapes=[pltpu.VMEM((tm, D), F32)],
        compiler_params=_params(("arbitrary", "arbitrary"), 60),
    )(x1, mod8, wg, wu, wd, target, ln2w, ln2b)


def _ffn_bwd(dz2, a, b, wg, wu, wd, x1, x2, u, mod8, ln1w, seq):
    t = x1.shape[0]
    tm = min(512, seq)
    nf, tf, _ = wg.shape
    tpb = seq // tm
    nbatch = t // seq

    def body(dz_ref, a_ref, b_ref, wg_ref, wu_ref, wd_ref, x1_ref, x_ref, u_ref, mod_ref, lw_ref,
             da_ref, db_ref, hm_ref, df_ref, du_ref, dxp_ref, st_ref, dm_ref, acc):
        i, j = pl.program_id(0), pl.program_id(1)

        @pl.when(j == 0)
        def _():
            df_ref[...] = ((1.0 + mod_ref[0, 5:6, :]) * dz_ref[...]).astype(BF16)
            acc[...] = jnp.zeros_like(acc)

        dhm = _dot_nt(df_ref[...], wd_ref[0])
        av = a_ref[0].astype(F32)
        bv = b_ref[0].astype(F32)
        sg = jax.nn.sigmoid(av)
        sl = av * sg
        hm_ref[0] = (sl * bv).astype(BF16)
        da = (dhm * bv * (sg * (1.0 + av * (1.0 - sg)))).astype(BF16)
        db = (dhm * sl).astype(BF16)
        da_ref[0] = da
        db_ref[0] = db
        acc[...] += _dot(da, wg_ref[0]) + _dot(db, wu_ref[0])

        @pl.when(j == nf - 1)
        def _():
            dh2 = acc[...]
            x1v = x1_ref[...]
            uv = u_ref[...]
            dx1 = ALPHA * dz_ref[...] + dh2 * (1.0 + mod_ref[0, 4:5, :])
            xhat, rstd = _ln_stats(ALPHA * x_ref[...] + (1.0 + mod_ref[0, 2:3, :]) * uv)
            dz1 = _ln_bwd(dx1, xhat, rstd, lw_ref[...])
            du_ref[...] = ((1.0 + mod_ref[0, 2:3, :]) * dz1).astype(BF16)
            dxp_ref[...] = ALPHA * dz1
            upd = jnp.concatenate(
                [jnp.sum(dx1 * xhat, axis=0, keepdims=True), jnp.sum(dx1, axis=0, keepdims=True),
                 jnp.zeros((6, D), F32)], axis=0)
            dmu = jnp.concatenate(
                [jnp.zeros((2, D), F32), jnp.sum(dz1 * uv, axis=0, keepdims=True),
                 jnp.sum(dh2, axis=0, keepdims=True), jnp.sum(dh2 * x1v, axis=0, keepdims=True),
                 jnp.zeros((3, D), F32)], axis=0)

            @pl.when(i == 0)
            def _():
                st_ref[...] = upd

            @pl.when(i > 0)
            def _():
                st_ref[...] += upd

            @pl.when(i % tpb == 0)
            def _():
                dm_ref[0] = dmu

            @pl.when(i % tpb != 0)
            def _():
                dm_ref[0] += dmu

    row = lambda: pl.BlockSpec((tm, D), lambda i, j: (i, 0))
    ffb = lambda: pl.BlockSpec((1, tm, tf), lambda i, j: (j, i, 0))
    return pl.pallas_call(
        body, name="ffn_bwd", grid=(t // tm, nf),
        in_specs=[row(), ffb(), ffb(),
                  pl.BlockSpec((1, tf, D), lambda i, j: (j, 0, 0)), pl.BlockSpec((1, tf, D), lambda i, j: (j, 0, 0)),
                  pl.BlockSpec((1, tf, D), lambda i, j: (j, 0, 0)), row(), row(), row(),
                  pl.BlockSpec((1, 8, D), lambda i, j: (i // tpb, 0, 0)), pl.BlockSpec((1, D), lambda i, j: (0, 0))],
        out_specs=[ffb(), ffb(), ffb(), row(), row(), row(), pl.BlockSpec((8, D), lambda i, j: (0, 0)),
                   pl.BlockSpec((1, 8, D), lambda i, j: (i // tpb, 0, 0))],
        out_shape=[jax.ShapeDtypeStruct((nf, t, tf), BF16), jax.ShapeDtypeStruct((nf, t, tf), BF16),
                   jax.ShapeDtypeStruct((nf, t, tf), BF16), jax.ShapeDtypeStruct((t, D), BF16),
                   jax.ShapeDtypeStruct((t, D), BF16), jax.ShapeDtypeStruct((t, D), F32),
                   jax.ShapeDtypeStruct((8, D), F32), jax.ShapeDtypeStruct((nbatch, 8, D), F32)],
        scratch_shapes=[pltpu.VMEM((tm, D), F32)],
        compiler_params=_params(("arbitrary", "arbitrary"), 60),
    )(dz2, a, b, wg, wu, wd, x1, x2, u, mod8, ln1w)


def _adamw_math(w, g, m, v):
    m = B1 * m + (1.0 - B1) * g
    v = B2 * v + (1.0 - B2) * (g * g)
    m_hat = m / (1.0 - B1 ** STEP)
    v_hat = v / (1.0 - B2 ** STEP)
    return -LR * (m_hat / (jnp.sqrt(v_hat) + EPS) + WD * w), m, v


def _adamw(w, g, m, v, name):
    rows, cols = w.shape
    tr = rows
    for cand in (128, 64, 32, 16, 8):
        if rows % cand == 0:
            tr = cand
            break

    def body(w_ref, g_ref, m_ref, v_ref, d_ref, mo_ref, vo_ref):
        d, mn, vn = _adamw_math(w_ref[...], g_ref[...], m_ref[...], v_ref[...])
        d_ref[...] = d
        mo_ref[...] = mn
        vo_ref[...] = vn

    spec = pl.BlockSpec((tr, cols), lambda i: (i, 0))
    return pl.pallas_call(
        body, name=name, grid=(rows // tr,), in_specs=[spec] * 4, out_specs=[spec] * 3,
        out_shape=[jax.ShapeDtypeStruct((rows, cols), F32)] * 3,
        compiler_params=_params(("parallel",), 48),
    )(w, g, m, v)


def _adamw_halves(w, g_mine, g_sib, m, v, c_idx, name):
    rows, cols = w.shape
    hr = rows // 2
    tr = next(cand for cand in (128, 88, 64, 32, 16, 8) if hr % cand == 0)
    tph = hr // tr

    def body(c_ref, w_ref, gm_ref, gs_ref, m_ref, v_ref, g_ref, d_ref, mo_ref, vo_ref):
        g = jnp.where(pl.program_id(0) == c_ref[0], gm_ref[...], gs_ref[...])
        d, mn, vn = _adamw_math(w_ref[...], g, m_ref[...], v_ref[...])
        g_ref[...] = g
        d_ref[...] = d
        mo_ref[...] = mn
        vo_ref[...] = vn

    full = pl.BlockSpec((tr, cols), lambda h, i, c: (h * tph + i, 0))
    half = pl.BlockSpec((tr, cols), lambda h, i, c: (i, 0))
    return pl.pallas_call(
        body, name=name,
        grid_spec=pltpu.PrefetchScalarGridSpec(
            num_scalar_prefetch=1, grid=(2, tph), in_specs=[full, half, half, full, full], out_specs=[full] * 4),
        out_shape=[jax.ShapeDtypeStruct((rows, cols), F32)] * 4,
        compiler_params=_params(("parallel", "parallel"), 48),
    )(c_idx, w, g_mine, g_sib, m, v)


def _grad_w_ada(c_all, dmod_cols):
    def body(c_ref, d_ref, o_ref):
        c = c_ref[...]
        o_ref[...] = lax.dot_general(c * jax.nn.sigmoid(c), d_ref[...], (((0,), (0,)), ((), ())),
                                     preferred_element_type=F32, precision=HIGHEST)

    return pl.pallas_call(
        body, name="grad_w_ada", out_shape=jax.ShapeDtypeStruct((D, dmod_cols.shape[1]), F32),
        compiler_params=_params(vmem_mb=48),
    )(c_all, dmod_cols)


def _small_update(gath, w8, m8, v8):
    def body(g_ref, w_ref, m_ref, v_ref, go_ref, d_ref, mo_ref, vo_ref):
        g0 = g_ref[0, 0:1, :] + g_ref[0, 1:2, :]
        g1 = g_ref[0, 2:3, :]
        for dev in range(1, N_DEV):
            g0 = g0 + (g_ref[dev, 0:1, :] + g_ref[dev, 1:2, :])
            g1 = g1 + g_ref[dev, 2:3, :]
        w = w_ref[...]
        lb = jax.nn.sigmoid(w[1:2, O_LB0:O_LB1] - w[1:2, O_LB1:O_FOX])
        fac = lb * (1.0 - lb)
        g1 = jnp.concatenate([g1[:, :O_LB0], g1[:, O_LB0:O_LB1] * fac, -g1[:, O_LB1:O_FOX] * fac, g1[:, O_FOX:]],
                             axis=1)
        g = jnp.concatenate([g0, g1, jnp.zeros((6, SMALL_W), F32)], axis=0)
        d, mn, vn = _adamw_math(w, g, m_ref[...], v_ref[...])
        go_ref[...] = g
        d_ref[...] = d
        mo_ref[...] = mn
        vo_ref[...] = vn

    return pl.pallas_call(
        body, name="small_update", out_shape=[jax.ShapeDtypeStruct((8, SMALL_W), F32)] * 4,
        compiler_params=_params(vmem_mb=48),
    )(gath, w8, m8, v8)


def _pack_small(b_ada, ln1w, ln1b, ln2w, ln2b, norm_w, lb_logits, fox):
    row1 = jnp.concatenate([ln1w, ln1b, ln2w, ln2b, norm_w, lb_logits[0:1], lb_logits[1:2], fox,
                            jnp.zeros((1, SMALL_W - O_FOX - BH), F32)], axis=1)
    return jnp.concatenate([b_ada, row1, jnp.zeros((6, SMALL_W), F32)], axis=0)


def _unpack_small(p):
    r = p[1:2]
    lb = jnp.concatenate([r[:, O_LB0:O_LB1], r[:, O_LB1:O_FOX]], axis=0)
    return dict(b_ada=p[0:1], ln1_w=r[:, O_LN1W:O_LN1B], ln1_b=r[:, O_LN1B:O_LN2W], ln2_w=r[:, O_LN2W:O_LN2B],
                ln2_b=r[:, O_LN2B:O_NORM], hgrn_norm_w=r[:, O_NORM:O_LB0], lb_logits=lb,
                fox_f_bias=r[:, O_FOX:O_FOX + BH])


_BIG = ("w_in", "w_branch_a", "w_branch_b", "w_out", "w_ffn_gate", "w_ffn_up", "w_ffn_down")
_TRANSPOSED = ("w_ffn_gate", "w_ffn_up")


def _cols_of_chips(stacked):
    return jnp.concatenate([stacked[k] for k in range(N_CHIPS)], axis=1)


def kernel(x, c, w_ada, b_ada, w_in, fox_f_bias, lb_logits, hgrn_norm_w, w_branch_a, w_branch_b, w_out, ln1_w, ln1_b, w_ffn_gate, w_ffn_up, w_ffn_down, ln2_w, ln2_b, loss_target, m_w_ada, m_b_ada, m_w_in, m_fox_f_bias, m_lb_logits, m_hgrn_norm_w, m_w_branch_a, m_w_branch_b, m_w_out, m_ln1_w, m_ln1_b, m_w_ffn_gate, m_w_ffn_up, m_w_ffn_down, m_ln2_w, m_ln2_b, v_w_ada, v_b_ada, v_w_in, v_fox_f_bias, v_lb_logits, v_hgrn_norm_w, v_w_branch_a, v_w_branch_b, v_w_out, v_ln1_w, v_ln1_b, v_w_ffn_gate, v_w_ffn_up, v_w_ffn_down, v_ln2_w, v_ln2_b):
    nbatch, seq, _ = x.shape
    t = nbatch * seq
    ax, ay, ac = lax.axis_index("x"), lax.axis_index("y"), lax.axis_index("c")
    chip = 2 * ax + ay
    dev = 2 * chip + ac
    chip_arr = jnp.reshape(chip, (1,)).astype(jnp.int32)
    core_arr = jnp.reshape(ac, (1,)).astype(jnp.int32)

    tr = lambda a: jnp.swapaxes(a[0], 0, 1)
    shard_w = dict(w_in=w_in[0], w_branch_a=w_branch_a[0], w_branch_b=w_branch_b[0], w_out=w_out[0],
                   w_ffn_gate=tr(w_ffn_gate), w_ffn_up=tr(w_ffn_up), w_ffn_down=w_ffn_down[0])
    shard_m = dict(w_in=m_w_in[0], w_branch_a=m_w_branch_a[0], w_branch_b=m_w_branch_b[0], w_out=m_w_out[0],
                   w_ffn_gate=tr(m_w_ffn_gate), w_ffn_up=tr(m_w_ffn_up), w_ffn_down=m_w_ffn_down[0])
    shard_v = dict(w_in=v_w_in[0], w_branch_a=v_w_branch_a[0], w_branch_b=v_w_branch_b[0], w_out=v_w_out[0],
                   w_ffn_gate=tr(v_w_ffn_gate), w_ffn_up=tr(v_w_ffn_up), w_ffn_down=v_w_ffn_down[0])

    shard16 = {n: shard_w[n].astype(BF16) for n in _BIG}

    def with_mine(gathered, n):
        return lax.dynamic_update_slice(gathered, shard16[n][None], (chip, 0, 0))

    w_p = _permute_cols(_cols_of_chips(with_mine(_gather_weights([shard16["w_in"]])[0], "w_in")))
    late = _BIG[1:]
    late_send, late_recv, late_src, late_land, late_token = _split_start(
        _gather_copies, [shard16[n] for n in late],
        [lax.empty((N_CHIPS,) + shard16[n].shape, BF16) for n in late], "gather_late_start")

    c8 = jnp.concatenate([c, jnp.zeros((8 - nbatch, D), F32)], axis=0)
    c_all = _allgather8(c8, "gather_c")[:, :nbatch, :].reshape(N_DEV * nbatch, D)
    ncol = w_ada.shape[2]
    b_cols = lax.dynamic_slice_in_dim(b_ada, chip * ncol, ncol, axis=1)
    mod_g = _allgather8(_mod_shard(c_all, w_ada[0], b_cols), "gather_mod")
    mod_all = jnp.concatenate([mod_g[2 * k] for k in range(N_CHIPS)], axis=1)
    mod_mine = lax.dynamic_slice_in_dim(mod_all, dev * nbatch, nbatch, axis=0)
    mod8 = jnp.concatenate([mod_mine.reshape(nbatch, 6, D), jnp.zeros((nbatch, 2, D), F32)], axis=1)
    mod8 = mod8 + late_token[0, 0]

    x2 = x.reshape(t, D)
    tgt2 = loss_target.reshape(t, D)
    bias128 = jnp.concatenate([fox_f_bias, jnp.zeros((1, 128 - BH), F32)], axis=1)

    proj, h16 = _proj(x2, mod8, w_p, seq, BF16, "proj")
    projf = _rows_matmul(h16, w_p[:, COL_BF:], "proj_forget")
    ya, ckpt = _hgrn_fwd(proj, lb_logits, hgrn_norm_w, nbatch, seq)
    cum_cols = _fox_cum(projf, bias128, nbatch, seq)
    yb, lse = _fox_fwd(proj, cum_cols, nbatch, seq)
    late_land = _pass_to_sibling(
        _split_wait(_gather_copies, late_send, late_recv, late_src, late_land, yb, "gather_late_wait"))
    full = {n: with_mine(g, n) for n, g in zip(late, late_land)}
    wba, wbb = _cols_of_chips(full["w_branch_a"]), _cols_of_chips(full["w_branch_b"])
    wout = full["w_out"].reshape(D, D)
    wg_t, wu_t, wd = full["w_ffn_gate"], full["w_ffn_up"], full["w_ffn_down"]
    merged, u, x1 = _merge_fwd(ya, yb, proj, x2, mod8, wba, wbb, wout, ln1_w, ln1_b, seq)
    a_pre, b_pre, h2, dz2, st2, dm2 = _ffn_fwd(x1, mod8, wg_t, wu_t, wd, tgt2, ln2_w, ln2_b, seq)
    loss = lax.psum(st2[2, 0], ("x", "y", "c"))

    da, db, hmid, dffn, du, dxp, st1, dm1 = _ffn_bwd(dz2, a_pre, b_pre, wg_t, wu_t, wd, x1, x2, u, mod8, ln1_w, seq)
    g_st = {}
    g_st["w_ffn_down"] = _tn_matmul(hmid, dffn, "dw_ffn_down", seq)
    g_st["w_ffn_gate"] = _tn_matmul(da, h2, "dw_ffn_gate", seq)
    g_st["w_ffn_up"] = _tn_matmul(db, h2, "dw_ffn_up", seq)
    g_st["w_out"] = _tn_matmul(merged, du, "dw_out", seq).reshape(N_CHIPS, D // N_CHIPS, D)

    def sum_over_cores(names, tag):
        g_list = [g_st[n] for n in names]
        return [_add_my_half(g, o, core_arr, "grad_add_halves_" + n)
                for n, g, o in zip(names, g_list, _swap_halves(g_list, "grad_swap_halves_" + tag))]

    early = ("w_ffn_down", "w_ffn_gate", "w_ffn_up", "w_out")
    e_halves = sum_over_cores(early, "early")
    e_send, e_recv, e_src, e_land, e_token = _split_start(
        _scatter_copies, [h16 for _, h16 in e_halves],
        [lax.empty((3,) + h16.shape[1:], BF16) for _, h16 in e_halves], "grad_scatter_early_start")
    dproj, dpa, dpb, dya, dyb = _merge_bwd(du, ya, yb, proj, wba, wbb, wout, e_token, seq)
    g_st["w_branch_a"] = _tn_matmul(ya, dpa, "dw_branch_a", seq, split=D // N_CHIPS)
    g_st["w_branch_b"] = _tn_matmul(yb, dpb, "dw_branch_b", seq, split=D // N_CHIPS)
    dproj, dq, drs, dcs = _fox_bwd(proj, cum_cols, lse, yb, dyb, dproj, nbatch, seq)
    dproj = _place_cols(dproj, dq, COL_BQ)
    dproj, sm_fox = _fox_dbf(projf, bias128, drs, dcs, dproj, nbatch, seq)
    dproj, sm_hgrn = _hgrn_bwd(proj, dya, ckpt, lb_logits, hgrn_norm_w, dproj, nbatch, seq)
    grad_x2, dm0 = _dh_kernel(dproj, w_p, x2, dxp, mod8, seq)
    dw_in = _unpermute_cols(_tn_matmul(h16, dproj, "dw_in", seq))
    ncin = NIN // N_CHIPS
    g_st["w_in"] = jnp.stack([dw_in[:, k * ncin:(k + 1) * ncin] for k in range(N_CHIPS)])

    e_recv = _split_wait(_scatter_copies, e_send, e_recv, e_src, e_land, dw_in, "grad_scatter_early_wait")
    rest = ("w_in", "w_branch_a", "w_branch_b")
    r_halves = sum_over_cores(rest, "rest")
    r_send, r_rcv, r_src, r_land, r_token = _split_start(
        _scatter_copies, [h16 for _, h16 in r_halves],
        [lax.empty((3,) + h16.shape[1:], BF16) for _, h16 in r_halves], "grad_scatter_rest_start")

    def finish(names, halves, recv, token, tag):
        g_mine = [_add_chips(h32, r, chip_arr, "grad_add_chips_" + n) for n, (h32, _), r in zip(names, halves, recv)]
        g_sib = _join_halves(g_mine, token, "grad_join_halves_" + tag)
        for n, gm, gs in zip(names, g_mine, g_sib):
            grads[n], deltas[n], new_m[n], new_v[n] = _adamw_halves(
                shard_w[n], gm, gs, shard_m[n], shard_v[n], core_arr, "adamw_" + n)

    grads, deltas, new_m, new_v = {}, {}, {}, {}
    finish(early, e_halves, e_recv, r_token, "early")

    dmod = (dm0 + dm1 + dm2)[:, :6, :].reshape(nbatch, 6 * D)
    row2 = jnp.concatenate([st1[0:1], st1[1:2], st2[0:1], st2[1:2], sm_hgrn[1:2], sm_hgrn[0:1], sm_hgrn[0:1],
                            sm_fox[0:1, :BH], jnp.zeros((1, SMALL_W - O_FOX - BH), F32)], axis=1)
    spack = jnp.concatenate([dmod, row2, jnp.zeros((8 - nbatch - 1, SMALL_W), F32)], axis=0)
    spack = spack + r_token[0, 0]
    gath = _allgather8(spack, "gather_small")
    w8 = _pack_small(b_ada, ln1_w, ln1_b, ln2_w, ln2_b, hgrn_norm_w, lb_logits, fox_f_bias)
    m8 = _pack_small(m_b_ada, m_ln1_w, m_ln1_b, m_ln2_w, m_ln2_b, m_hgrn_norm_w, m_lb_logits, m_fox_f_bias)
    v8 = _pack_small(v_b_ada, v_ln1_w, v_ln1_b, v_ln2_w, v_ln2_b, v_hgrn_norm_w, v_lb_logits, v_fox_f_bias)
    sg, sd, smn, svn = (_unpack_small(p) for p in _small_update(gath, w8, m8, v8))
    dmod_all = gath[:, :nbatch, :].reshape(N_DEV * nbatch, SMALL_W)
    g_ada = _grad_w_ada(c_all, lax.dynamic_slice_in_dim(dmod_all, chip * ncol, ncol, axis=1))

    for group, small in zip((grads, deltas, new_m, new_v), (sg, sd, smn, svn)):
        group.update(small)
    grads["w_ada"] = g_ada
    deltas["w_ada"], new_m["w_ada"], new_v["w_ada"] = _adamw(w_ada[0], g_ada, m_w_ada[0], v_w_ada[0], "adamw_w_ada")
    done = sum(new_v[n][0:8, 0:128] for n in early) + new_v["w_ada"][0:8, 0:128]
    r_recv = _split_wait(_scatter_copies, r_send, r_rcv, r_src, r_land, done, "grad_scatter_rest_wait")
    finish(rest, r_halves, r_recv, late_token, "rest")

    names = ["w_ada", "b_ada", "w_in", "fox_f_bias", "lb_logits", "hgrn_norm_w", "w_branch_a", "w_branch_b", "w_out",
             "ln1_w", "ln1_b", "w_ffn_gate", "w_ffn_up", "w_ffn_down", "ln2_w", "ln2_b"]
    shapes = dict(w_ada=w_ada.shape, b_ada=b_ada.shape, w_in=w_in.shape, fox_f_bias=fox_f_bias.shape,
                  lb_logits=lb_logits.shape, hgrn_norm_w=hgrn_norm_w.shape, w_branch_a=w_branch_a.shape,
                  w_branch_b=w_branch_b.shape, w_out=w_out.shape, ln1_w=ln1_w.shape, ln1_b=ln1_b.shape,
                  w_ffn_gate=w_ffn_gate.shape, w_ffn_up=w_ffn_up.shape, w_ffn_down=w_ffn_down.shape,
                  ln2_w=ln2_w.shape, ln2_b=ln2_b.shape)
    outs = [loss, grad_x2.reshape(x.shape)]
    for group in (grads, deltas, new_m, new_v):
        outs += [(jnp.swapaxes(group[n], 0, 1) if n in _TRANSPOSED else group[n]).reshape(shapes[n]) for n in names]
    return tuple(outs)
```

```python
import functools

import jax
import jax.numpy as jnp
from jax import lax
from jax.experimental import pallas as pl
from jax.experimental.pallas import tpu as pltpu

F32 = jnp.float32
BF16 = jnp.bfloat16
MESH = pl.DeviceIdType.MESH
HIGHEST = lax.Precision.HIGHEST

D = 1024
AW = 512
AH = 4
ADH = 128
BH = 8
BDH = 64
DFF = 2816
NIN = 5640
NP = 5760
N_CHIPS = 4
N_DEV = 8
HGRN_BLOCK = 256
FFN_TOKENS = 512
COL_GATES = 0
COL_A = 2048
COL_BQ = 4096
COL_KV = 4608
COL_BF = 5632
HGRN_HEADS = 4
FOX_PAIRS = 2
ALPHA = 2.0 ** 0.25
LN_EPS = 1e-5
RMS_EPS = 1e-6
NEG = -1e30
LOG2E = 1.4426950408889634
LR, B1, B2, EPS, WD, STEP = 0.001, 0.9, 0.999, 1e-08, 0.01, 10
SMALL_W = 6144
O_LN1W, O_LN1B, O_LN2W, O_LN2B, O_NORM, O_LB0, O_LB1, O_FOX = 0, 1024, 2048, 3072, 4096, 4608, 5120, 5632


def _params(sem=None, vmem_mb=None):
    kw = {}
    if sem is not None:
        kw["dimension_semantics"] = sem
    if vmem_mb is not None:
        kw["vmem_limit_bytes"] = vmem_mb << 20
    return pltpu.CompilerParams(**kw)


def _dot(a, b):
    return jnp.dot(a.astype(BF16), b.astype(BF16), preferred_element_type=F32)


def _dot_nt(a, b):
    return lax.dot_general(a.astype(BF16), b.astype(BF16), (((1,), (1,)), ((), ())), preferred_element_type=F32)


def _dot_tn(a, b):
    return lax.dot_general(a.astype(BF16), b.astype(BF16), (((0,), (0,)), ((), ())), preferred_element_type=F32)


def _dot_f32(a, b):
    return jnp.dot(a, b, preferred_element_type=F32, precision=HIGHEST)


def _perm_segments():
    segs = [(3592, 5640)]
    for h in range(4):
        segs += [(128 * h + 512 * t, 128 * h + 512 * t + 128) for t in range(4)]
    segs += [(2048, 2560)]
    for p in range(4):
        segs += [(2560 + 128 * p, 2688 + 128 * p), (3072 + 128 * p, 3200 + 128 * p)]
    segs += [(3584, 3592)]
    return segs


def _permute_cols(w):
    parts = [w[:, a:b] for a, b in _perm_segments()]
    parts.append(jnp.zeros((w.shape[0], NP - NIN), w.dtype))
    return jnp.concatenate(parts, axis=1)


def _unpermute_cols(g):
    pos, where = 0, {}
    for a, b in _perm_segments():
        where[a] = (pos, pos + b - a)
        pos += b - a
    parts = [g[:, where[a][0]:where[a][1]] for a in sorted(where)]
    return jnp.concatenate(parts, axis=1)


def _allgather8(v, name):
    rows, cols = v.shape

    def body(x_ref, out_ref, send_sems, recv_sems, local_sem):
        x, y, c = lax.axis_index("x"), lax.axis_index("y"), lax.axis_index("c")
        me, sibling = (x, y, c), (x, y, 1 - c)
        chips = [(1 - x, y), (x, 1 - y), (1 - x, 1 - y)]

        def slot(px, py, pc):
            return out_ref.at[4 * px + 2 * py + pc]

        def copy(k, block, to, src=None):
            return pltpu.make_async_remote_copy(
                src_ref=slot(*block) if src is None else src, dst_ref=slot(*block),
                send_sem=send_sems.at[k], recv_sem=recv_sems.at[k], device_id=to, device_id_type=MESH)

        mine = pltpu.make_async_copy(x_ref, slot(*me), local_sem)
        mine.start()
        first = [copy(0, me, sibling, src=x_ref)]
        first += [copy(1 + j, me, (*chip, c), src=x_ref) for j, chip in enumerate(chips)]
        for cp in first:
            cp.start()
        passed = [copy(4 + j, (*chip, c), sibling) for j, chip in enumerate(chips)]
        for j, chip in enumerate(chips):
            copy(1 + j, (*chip, c), me).wait_recv()
            passed[j].start()
        copy(0, sibling, me).wait_recv()
        for j, chip in enumerate(chips):
            copy(4 + j, (*chip, 1 - c), me).wait_recv()
        for cp in first + passed:
            cp.wait_send()
        mine.wait()

    return pl.pallas_call(
        body, name=name,
        out_shape=jax.ShapeDtypeStruct((N_DEV, rows, cols), v.dtype),
        in_specs=[pl.BlockSpec(memory_space=pltpu.VMEM)],
        out_specs=pl.BlockSpec(memory_space=pltpu.VMEM),
        scratch_shapes=[pltpu.SemaphoreType.DMA((7,)), pltpu.SemaphoreType.DMA((7,)), pltpu.SemaphoreType.DMA],
    )(v)


def _hbm_specs(n):
    return [pl.BlockSpec(memory_space=pl.ANY)] * n


def _gather_weights(shards):
    n = len(shards)

    def body(*refs):
        ins, outs, (send_sems, recv_sems) = refs[:n], refs[n:2 * n], refs[2 * n:]
        x, y, c = lax.axis_index("x"), lax.axis_index("y"), lax.axis_index("c")
        sibling = (x, y, 1 - c)
        chips = [(1 - x, y), (x, 1 - y), (1 - x, 1 - y)]

        def blk(w, px, py, half):
            hr = ins[w].shape[0] // 2
            return outs[w].at[2 * px + py, pl.ds(half * hr, hr), :]

        def copy(w, k, block, to, src=None):
            return pltpu.make_async_remote_copy(
                src_ref=blk(w, *block) if src is None else src, dst_ref=blk(w, *block),
                send_sem=send_sems.at[6 * w + k], recv_sem=recv_sems.at[6 * w + k], device_id=to, device_id_type=MESH)

        first = []
        for w in range(n):
            hr = ins[w].shape[0] // 2
            my_half = ins[w].at[pl.ds(c * hr, hr), :]
            first += [copy(w, j, (x, y, c), (*chip, c), src=my_half) for j, chip in enumerate(chips)]
        for cp in first:
            cp.start()
        passed = []
        for j, chip in enumerate(chips):
            for w in range(n):
                copy(w, j, (*chip, c), (x, y, c)).wait_recv()
                passed.append(copy(w, 3 + j, (*chip, c), sibling))
                passed[-1].start()
        for j, chip in enumerate(chips):
            for w in range(n):
                copy(w, 3 + j, (*chip, 1 - c), (x, y, c)).wait_recv()
        for cp in first + passed:
            cp.wait_send()

    return pl.pallas_call(
        body, name="gather_weights",
        out_shape=[jax.ShapeDtypeStruct((N_CHIPS,) + s.shape, s.dtype) for s in shards],
        in_specs=_hbm_specs(n), out_specs=_hbm_specs(n),
        scratch_shapes=[pltpu.SemaphoreType.DMA((6 * n,)), pltpu.SemaphoreType.DMA((6 * n,))],
    )(*shards)


def _swap_halves(grads, name):
    n = len(grads)

    def body(*refs):
        ins, outs, (send_sems, recv_sems) = refs[:n], refs[n:2 * n], refs[2 * n:]
        x, y, c = lax.axis_index("x"), lax.axis_index("y"), lax.axis_index("c")
        cps = []
        for w in range(n):
            hr = ins[w].shape[1] // 2
            cps.append(pltpu.make_async_remote_copy(
                src_ref=ins[w].at[:, pl.ds((1 - c) * hr, hr), :], dst_ref=outs[w],
                send_sem=send_sems.at[w], recv_sem=recv_sems.at[w], device_id=(x, y, 1 - c), device_id_type=MESH))
        for cp in cps:
            cp.start()
        for cp in cps:
            cp.wait()

    return pl.pallas_call(
        body, name=name,
        out_shape=[jax.ShapeDtypeStruct((N_CHIPS, g.shape[1] // 2, g.shape[2]), g.dtype) for g in grads],
        in_specs=_hbm_specs(n), out_specs=_hbm_specs(n),
        scratch_shapes=[pltpu.SemaphoreType.DMA((n,)), pltpu.SemaphoreType.DMA((n,))],
    )(*grads)


def _join_halves(halves, token, name):
    n = len(halves)

    def body(*refs):
        ins, outs, (send_sems, recv_sems) = refs[:n], refs[n + 1:2 * n + 1], refs[2 * n + 1:]
        x, y, c = lax.axis_index("x"), lax.axis_index("y"), lax.axis_index("c")
        cps = [pltpu.make_async_remote_copy(
            src_ref=ins[w], dst_ref=outs[w], send_sem=send_sems.at[w], recv_sem=recv_sems.at[w],
            device_id=(x, y, 1 - c), device_id_type=MESH) for w in range(n)]
        for cp in cps:
            cp.start()
        for cp in cps:
            cp.wait()

    return pl.pallas_call(
        body, name=name,
        out_shape=[jax.ShapeDtypeStruct(h.shape, h.dtype) for h in halves],
        in_specs=_hbm_specs(n + 1), out_specs=_hbm_specs(n),
        scratch_shapes=[pltpu.SemaphoreType.DMA((n,)), pltpu.SemaphoreType.DMA((n,))],
    )(*halves, token)


def _in_hbm(v):
    return pltpu.with_memory_space_constraint(v, pltpu.HBM)


_SPLIT_COPY = pltpu.CompilerParams(has_side_effects=pltpu.SideEffectType.DATAFLOW_SIDE_EFFECTING)


def _gather_copies(srcs, lands, send_sems, recv_sems):
    x, y, c = lax.axis_index("x"), lax.axis_index("y"), lax.axis_index("c")
    cps = []
    for w, (src, land) in enumerate(zip(srcs, lands)):
        hr = src.shape[0] // 2
        for j, chip in enumerate([(1 - x, y), (x, 1 - y), (1 - x, 1 - y)]):
            cps.append(pltpu.make_async_remote_copy(
                src_ref=src.at[pl.ds(c * hr, hr), :], dst_ref=land.at[2 * x + y, pl.ds(c * hr, hr), :],
                send_sem=send_sems.at[3 * w + j], recv_sem=recv_sems.at[3 * w + j],
                device_id=(*chip, c), device_id_type=MESH))
    return cps


def _scatter_copies(srcs, lands, send_sems, recv_sems):
    x, y, c = lax.axis_index("x"), lax.axis_index("y"), lax.axis_index("c")
    cps = []
    for w, (src, land) in enumerate(zip(srcs, lands)):
        for j, chip in enumerate([(1 - x, y), (x, 1 - y), (1 - x, 1 - y)]):
            cps.append(pltpu.make_async_remote_copy(
                src_ref=src.at[2 * chip[0] + chip[1]], dst_ref=land.at[j],
                send_sem=send_sems.at[3 * w + j], recv_sem=recv_sems.at[3 * w + j],
                device_id=(*chip, c), device_id_type=MESH))
    return cps


def _split_start(copies, srcs, lands, name):
    n = len(srcs)

    def body(*refs):
        src, lnd, send_sems, recv_sems, token = refs[:n], refs[n:2 * n], refs[2 * n], refs[2 * n + 1], refs[-1]
        for cp in copies(src, lnd, send_sems, recv_sems):
            cp.start()
        token[...] = jnp.zeros_like(token)

    hbm = pl.BlockSpec(memory_space=pltpu.HBM)
    sem = pl.BlockSpec(memory_space=pltpu.SEMAPHORE)
    outs = pl.pallas_call(
        body, name=name,
        out_shape=(pltpu.SemaphoreType.DMA((3 * n,)), pltpu.SemaphoreType.DMA((3 * n,)),
                   *[pltpu.HBM(v.shape, v.dtype) for v in srcs + lands], jax.ShapeDtypeStruct((8, 128), F32)),
        in_specs=[hbm] * (2 * n),
        out_specs=(sem, sem, *([hbm] * (2 * n)), pl.BlockSpec(memory_space=pltpu.VMEM)),
        input_output_aliases={i: 2 + i for i in range(2 * n)},
        compiler_params=_SPLIT_COPY,
    )(*[_in_hbm(v) for v in srcs + lands])
    return outs[0], outs[1], list(outs[2:2 + n]), list(outs[2 + n:2 + 2 * n]), outs[-1]


def _split_wait(copies, send_sems, recv_sems, srcs, lands, after, name):
    n = len(srcs)

    def body(*refs):
        src, lnd, send_sems, recv_sems = refs[:n], refs[n:2 * n], refs[2 * n], refs[2 * n + 1]
        for cp in copies(src, lnd, send_sems, recv_sems):
            cp.wait_send()
            cp.wait_recv()

    hbm = pl.BlockSpec(memory_space=pltpu.HBM)
    sem = pl.BlockSpec(memory_space=pltpu.SEMAPHORE)
    outs = pl.pallas_call(
        body, name=name,
        out_shape=tuple(pltpu.HBM(v.shape, v.dtype) for v in srcs + lands),
        in_specs=[hbm] * (2 * n) + [sem, sem, pl.BlockSpec(memory_space=pl.ANY)],
        out_specs=tuple([hbm] * (2 * n)),
        input_output_aliases={i: i for i in range(2 * n)},
        compiler_params=_SPLIT_COPY,
    )(*srcs, *lands, send_sems, recv_sems, after)
    return list(outs[n:])


def _pass_to_sibling(lands):
    n = len(lands)

    def body(*refs):
        ins, outs, (send_sems, recv_sems) = refs[:n], refs[n:2 * n], refs[2 * n:]
        x, y, c = lax.axis_index("x"), lax.axis_index("y"), lax.axis_index("c")
        cps = []
        for w in range(n):
            hr = ins[w].shape[1] // 2
            for j, chip in enumerate([(1 - x, y), (x, 1 - y), (1 - x, 1 - y)]):
                k = 2 * chip[0] + chip[1]
                cps.append(pltpu.make_async_remote_copy(
                    src_ref=ins[w].at[k, pl.ds(c * hr, hr), :], dst_ref=outs[w].at[k, pl.ds(c * hr, hr), :],
                    send_sem=send_sems.at[3 * w + j], recv_sem=recv_sems.at[3 * w + j],
                    device_id=(x, y, 1 - c), device_id_type=MESH))
        for cp in cps:
            cp.start()
        for cp in cps:
            cp.wait()

    return pl.pallas_call(
        body, name="gather_late_pass",
        out_shape=[jax.ShapeDtypeStruct(v.shape, v.dtype) for v in lands],
        in_specs=_hbm_specs(n), out_specs=_hbm_specs(n),
        input_output_aliases={i: i for i in range(n)},
        scratch_shapes=[pltpu.SemaphoreType.DMA((3 * n,)), pltpu.SemaphoreType.DMA((3 * n,))],
    )(*lands)


def _row_tile(rows):
    for cand in (256, 176, 128, 64, 32, 16):
        if rows % cand == 0:
            return cand
    raise ValueError(rows)


def _add_my_half(g, other, c_idx, name):
    _, k, n = g.shape
    hr = k // 2
    tr = _row_tile(hr)
    nb = hr // tr

    def body(c_ref, g_ref, o_ref, out_ref, out16_ref):
        s = g_ref[...] + o_ref[...]
        out_ref[...] = s
        out16_ref[...] = s.astype(BF16)

    return pl.pallas_call(
        body, name=name,
        grid_spec=pltpu.PrefetchScalarGridSpec(
            num_scalar_prefetch=1, grid=(N_CHIPS, nb),
            in_specs=[pl.BlockSpec((1, tr, n), lambda j, i, c: (j, c[0] * nb + i, 0)),
                      pl.BlockSpec((1, tr, n), lambda j, i, c: (j, i, 0))],
            out_specs=[pl.BlockSpec((1, tr, n), lambda j, i, c: (j, i, 0)),
                       pl.BlockSpec((1, tr, n), lambda j, i, c: (j, i, 0))]),
        out_shape=[jax.ShapeDtypeStruct((N_CHIPS, hr, n), F32), jax.ShapeDtypeStruct((N_CHIPS, hr, n), BF16)],
        compiler_params=_params(("parallel", "parallel")),
    )(c_idx, g, other)


def _add_chips(red, recv, chip_idx, name):
    _, hr, n = red.shape
    tr = _row_tile(hr)

    def body(k_ref, r_ref, v_ref, out_ref):
        out_ref[...] = ((r_ref[0] + v_ref[0].astype(F32)) + v_ref[1].astype(F32)) + v_ref[2].astype(F32)

    return pl.pallas_call(
        body, name=name,
        grid_spec=pltpu.PrefetchScalarGridSpec(
            num_scalar_prefetch=1, grid=(hr // tr,),
            in_specs=[pl.BlockSpec((1, tr, n), lambda i, k: (k[0], i, 0)),
                      pl.BlockSpec((3, tr, n), lambda i, k: (0, i, 0))],
            out_specs=pl.BlockSpec((tr, n), lambda i, k: (i, 0))),
        out_shape=jax.ShapeDtypeStruct((hr, n), F32),
        compiler_params=_params(("parallel",)),
    )(chip_idx, red, recv)


def _mod_shard(c_all, w_ada, b_ada):
    nb, cols = c_all.shape[0], w_ada.shape[1]

    def body(c_ref, w_ref, b_ref, o_ref):
        c = c_ref[...]
        o_ref[...] = _dot(c * jax.nn.sigmoid(c), w_ref[...]) + b_ref[...]

    return pl.pallas_call(
        body, name="mod_shard", out_shape=jax.ShapeDtypeStruct((nb, cols), F32),
        compiler_params=_params(vmem_mb=48),
    )(c_all, w_ada, b_ada)


def _proj(x2, mod8, w, seq, out_dtype, name):
    t = x2.shape[0]
    n = w.shape[1]
    tm, tn = min(2048, seq), min(1152, n)
    tpb = seq // tm

    def body(x_ref, mod_ref, w_ref, o_ref, h_ref):
        @pl.when(pl.program_id(1) == 0)
        def _():
            h_ref[...] = (x_ref[...] * (1.0 + mod_ref[0, 1:2, :]) + mod_ref[0, 0:1, :]).astype(BF16)
        o_ref[...] = jnp.dot(h_ref[...], w_ref[...], preferred_element_type=F32).astype(o_ref.dtype)

    return pl.pallas_call(
        body, name=name, grid=(t // tm, n // tn),
        in_specs=[pl.BlockSpec((tm, D), lambda i, j: (i, 0)),
                  pl.BlockSpec((1, 8, D), lambda i, j: (i // tpb, 0, 0)),
                  pl.BlockSpec((D, tn), lambda i, j: (0, j))],
        out_specs=[pl.BlockSpec((tm, tn), lambda i, j: (i, j)), pl.BlockSpec((tm, D), lambda i, j: (i, 0))],
        out_shape=[jax.ShapeDtypeStruct((t, n), out_dtype), jax.ShapeDtypeStruct((t, D), BF16)],
        compiler_params=_params(("parallel", "arbitrary"), 56),
    )(x2, mod8, w)


def _rows_matmul(a, w, name):
    t, k = a.shape
    n = w.shape[1]
    tm = 1024 if t % 1024 == 0 else t

    def body(a_ref, w_ref, o_ref):
        o_ref[...] = jnp.dot(a_ref[...], w_ref[...], preferred_element_type=F32)

    return pl.pallas_call(
        body, name=name, grid=(t // tm,),
        in_specs=[pl.BlockSpec((tm, k), lambda i: (i, 0)), pl.BlockSpec((k, n), lambda i: (0, 0))],
        out_specs=pl.BlockSpec((tm, n), lambda i: (i, 0)),
        out_shape=jax.ShapeDtypeStruct((t, n), F32),
        compiler_params=_params(("parallel",)),
    )(a, w)


def _tn_matmul(a, b, name, seq, split=None):
    a_st, b_st = a.ndim == 3, b.ndim == 3
    t, ka = a.shape[-2:]
    n = b.shape[-1]
    tt = min(1024, seq)
    nt = t // tt
    if a_st or b_st:
        steps, tn = (a.shape[0] if a_st else b.shape[0]), n
    else:
        tn = split
        if tn is None:
            tn = next(cand for cand in (1152, 1024, 1408, 512, n) if n % cand == 0)
        steps = n // tn
    stacked_out = a_st or b_st or split is not None

    def body(a_ref, b_ref, o_ref):
        part = _dot_tn(a_ref[0] if a_st else a_ref[...], b_ref[0] if b_st else b_ref[...])
        if stacked_out:
            part = part[None]

        @pl.when(pl.program_id(1) == 0)
        def _():
            o_ref[...] = part

        @pl.when(pl.program_id(1) > 0)
        def _():
            o_ref[...] += part

    if a_st:
        in_specs = [pl.BlockSpec((1, tt, ka), lambda j, k: (j, k, 0))]
    else:
        in_specs = [pl.BlockSpec((tt, ka), lambda j, k: (k, 0))]
    if b_st:
        in_specs.append(pl.BlockSpec((1, tt, n), lambda j, k: (j, k, 0)))
    else:
        in_specs.append(pl.BlockSpec((tt, tn), lambda j, k: (k, 0 if a_st else j)))
    if stacked_out:
        out_spec = pl.BlockSpec((1, ka, tn), lambda j, k: (j, 0, 0))
        out_shape = jax.ShapeDtypeStruct((steps, ka, tn), F32)
    else:
        out_spec = pl.BlockSpec((ka, tn), lambda j, k: (0, j))
        out_shape = jax.ShapeDtypeStruct((ka, n), F32)
    return pl.pallas_call(
        body, name=name, grid=(steps, nt), in_specs=in_specs, out_specs=out_spec, out_shape=out_shape,
        compiler_params=_params(("parallel", "arbitrary"), 56),
    )(a, b)


def _dh_kernel(dproj, w_p, x2, dxp, mod8, seq):
    t = x2.shape[0]
    tm, tk = min(1024, seq), 1152
    tpb = seq // tm
    nk = NP // tk
    nbatch = t // seq

    def body(dp_ref, w_ref, x_ref, dxp_ref, mod_ref, gx_ref, dm_ref, acc):
        i, k = pl.program_id(0), pl.program_id(1)

        @pl.when(k == 0)
        def _():
            acc[...] = jnp.zeros_like(acc)

        acc[...] += _dot_nt(dp_ref[...], w_ref[...])

        @pl.when(k == nk - 1)
        def _():
            dh = acc[...]
            gx_ref[...] = dxp_ref[...].astype(F32) + dh * (1.0 + mod_ref[0, 1:2, :])
            upd = jnp.concatenate(
                [jnp.sum(dh, axis=0, keepdims=True), jnp.sum(dh * x_ref[...], axis=0, keepdims=True),
                 jnp.zeros((6, D), F32)], axis=0)

            @pl.when(i % tpb == 0)
            def _():
                dm_ref[0] = upd

            @pl.when(i % tpb != 0)
            def _():
                dm_ref[0] += upd

    return pl.pallas_call(
        body, name="dh", grid=(t // tm, nk),
        in_specs=[pl.BlockSpec((tm, tk), lambda i, k: (i, k)),
                  pl.BlockSpec((D, tk), lambda i, k: (0, k)),
                  pl.BlockSpec((tm, D), lambda i, k: (i, 0)),
                  pl.BlockSpec((tm, D), lambda i, k: (i, 0)),
                  pl.BlockSpec((1, 8, D), lambda i, k: (i // tpb, 0, 0))],
        out_specs=[pl.BlockSpec((tm, D), lambda i, k: (i, 0)),
                   pl.BlockSpec((1, 8, D), lambda i, k: (i // tpb, 0, 0))],
        out_shape=[jax.ShapeDtypeStruct((t, D), F32), jax.ShapeDtypeStruct((nbatch, 8, D), F32)],
        scratch_shapes=[pltpu.VMEM((tm, D), F32)],
        compiler_params=_params(("arbitrary", "arbitrary"), 48),
    )(dproj, w_p, x2, dxp, mod8)


def _tri(n, upper):
    r = lax.broadcasted_iota(jnp.int32, (n, n), 0)
    c = lax.broadcasted_iota(jnp.int32, (n, n), 1)
    return jnp.where((c >= r) if upper else (c <= r), 1.0, 0.0).astype(F32)


@jax.custom_vjp
def _mm_nn(a, b):
    return _dot(a, b)


_mm_nn.defvjp(lambda a, b: (_dot(a, b), (a, b)),
              lambda res, g: (_dot_nt(g, res[1]), _dot_tn(res[0], g)))


@jax.custom_vjp
def _mm_nt(a, b):
    return _dot_nt(a, b)


_mm_nt.defvjp(lambda a, b: (_dot_nt(a, b), (a, b)),
              lambda res, g: (_dot(g, res[1]), _dot_tn(g, res[0])))


@jax.custom_vjp
def _mm_tn(a, b):
    return _dot_tn(a, b)


_mm_tn.defvjp(lambda a, b: (_dot_tn(a, b), (a, b)),
              lambda res, g: (_dot_nt(res[1], g), _dot(res[0], g)))


@jax.custom_vjp
def _cumsum_rows(x):
    return _dot_f32(_tri(x.shape[0], False), x)


_cumsum_rows.defvjp(lambda x: (_cumsum_rows(x), None),
                    lambda _, g: (_dot_f32(_tri(g.shape[0], True), g),))


@functools.partial(jax.custom_vjp, nondiff_argnums=(1,))
def _shift_rows(x, k):
    return pltpu.roll(x, k % x.shape[0], 0)


_shift_rows.defvjp(lambda x, k: (_shift_rows(x, k), None),
                   lambda k, _, g: (pltpu.roll(g, (-k) % g.shape[0], 0),))


def _group_ref(bc, m):
    n = bc.shape[0] // (2 * m)
    b3 = bc.reshape(n, 2 * m, ADH)
    row = lax.broadcasted_iota(jnp.int32, b3.shape, 1)
    ref = jnp.sum(jnp.where(row == m - 1, b3, 0.0), axis=1, keepdims=True)
    return jnp.broadcast_to(ref, b3.shape).reshape(bc.shape)


def _hgrn_block(q, fl, v, g, st, lb, nw):
    n = q.shape[0]
    f = lb + (1.0 - lb) * jax.nn.sigmoid(fl)
    kk = 1.0 - f
    lf = jnp.log(f)
    bc = _cumsum_rows(lf)
    row = lax.broadcasted_iota(jnp.int32, (n, ADH), 0)
    same = jnp.bitwise_xor(lax.broadcasted_iota(jnp.int32, (n, n), 0), lax.broadcasted_iota(jnp.int32, (n, n), 1))
    a = jnp.zeros((n, n), F32)
    m = 1
    while m < n:
        r = jnp.bitwise_and(row, 2 * m - 1)
        up, lo = r >= m, r < m
        if m == 1:
            aq, ak = lf, jnp.zeros_like(lf)
        elif m == 2:
            aq = jnp.where(r == 3, lf + _shift_rows(lf, 1), lf)
            ak = jnp.where(r == 0, _shift_rows(lf, -1), 0.0)
        else:
            ref = _group_ref(bc, m)
            aq, ak = bc - ref, ref - bc
        qt = jnp.where(up, q * jnp.exp(jnp.where(up, aq, 0.0)), 0.0)
        kt = jnp.where(lo, kk * jnp.exp(jnp.where(lo, ak, 0.0)), 0.0)
        a = a + jnp.where(same < 2 * m, _mm_nt(qt, kt), 0.0)
        m *= 2
    last = row == n - 1
    bl = jnp.sum(jnp.where(last, bc, 0.0), axis=0, keepdims=True)
    o = _mm_nn(a, v) + _mm_nt(q * jnp.exp(bc), st) + jnp.sum(q * kk, axis=-1, keepdims=True) * v
    st_new = st * jnp.exp(bl) + _mm_tn(v, kk * jnp.exp(bl - bc))
    rms = lax.rsqrt(jnp.mean(o * o, axis=-1, keepdims=True) + RMS_EPS)
    return o * rms * nw * jax.nn.sigmoid(g), st_new


def _hgrn_fwd(proj, lb_logits, norm_w, nbatch, seq):
    t = proj.shape[0]
    blk = min(HGRN_BLOCK, seq)
    nb = seq // blk

    nh = HGRN_HEADS
    wp, wy = 512 * nh, ADH * nh

    def body(p_ref, lbl_ref, nw_ref, y_ref, ck_ref, st_s):
        @pl.when(pl.program_id(2) == 0)
        def _():
            st_s[...] = jnp.zeros_like(st_s)

        st = [st_s[h] for h in range(nh)]
        p = p_ref[...].astype(F32)
        lb = jax.nn.sigmoid(lbl_ref[0:1, :] - lbl_ref[1:2, :])
        nw = nw_ref[...]
        res = [_hgrn_block(*(p[:, 512 * h + 128 * k:512 * h + 128 * k + 128] for k in range(4)), st[h],
                           lb[:, 128 * h:128 * h + 128], nw[:, 128 * h:128 * h + 128]) for h in range(nh)]
        for h in range(nh):
            ck_ref[0, h] = st[h]
            st_s[h] = res[h][1]
        y_ref[...] = jnp.concatenate([r[0] for r in res], axis=1).astype(y_ref.dtype)

    return pl.pallas_call(
        body, name="hgrn_fwd", grid=(AH // nh, nbatch, nb),
        in_specs=[pl.BlockSpec((blk, wp), lambda h, b, i: (b * nb + i, COL_A // wp + h)),
                  pl.BlockSpec((2, wy), lambda h, b, i: (0, h)),
                  pl.BlockSpec((1, wy), lambda h, b, i: (0, h))],
        out_specs=[pl.BlockSpec((blk, wy), lambda h, b, i: (b * nb + i, h)),
                   pl.BlockSpec((1, nh, 128, 128), lambda h, b, i: ((h * nbatch + b) * nb + i, 0, 0, 0))],
        out_shape=[jax.ShapeDtypeStruct((t, AW), BF16),
                   jax.ShapeDtypeStruct((AH // nh * nbatch * nb, nh, 128, 128), F32)],
        scratch_shapes=[pltpu.VMEM((nh, 128, 128), F32)],
        compiler_params=_params(("parallel", "parallel", "arbitrary"), 48),
    )(proj, lb_logits, norm_w)


def _hgrn_bwd(proj, dya, ckpt, lb_logits, norm_w, dproj, nbatch, seq):
    t = proj.shape[0]
    blk = min(HGRN_BLOCK, seq)
    nb = seq // blk

    nh = HGRN_HEADS
    wp, wy = 512 * nh, ADH * nh

    def body(p_ref, dy_ref, ck_ref, lbl_ref, nw_ref, dp_in, dp_ref, sm_ref, dst_s):
        del dp_in
        b_id, i = pl.program_id(1), pl.program_id(2)

        @pl.when(i == 0)
        def _():
            dst_s[...] = jnp.zeros_like(dst_s)

        dst = [dst_s[h] for h in range(nh)]
        st = [ck_ref[0, h] for h in range(nh)]
        p = p_ref[...].astype(F32)
        dy = dy_ref[...]
        lb = jax.nn.sigmoid(lbl_ref[0:1, :] - lbl_ref[1:2, :])
        nw = nw_ref[...]
        grads = []
        for h in range(nh):
            _, pullback = jax.vjp(_hgrn_block, *(p[:, 512 * h + 128 * k:512 * h + 128 * k + 128] for k in range(4)),
                                  st[h], lb[:, 128 * h:128 * h + 128], nw[:, 128 * h:128 * h + 128])
            grads.append(pullback((dy[:, 128 * h:128 * h + 128], dst[h])))
        for h in range(nh):
            dst_s[h] = grads[h][4]
        dp_ref[...] = jnp.concatenate([g[k] for g in grads for k in range(4)], axis=1).astype(dp_ref.dtype)
        upd = jnp.concatenate([jnp.concatenate([g[5] for g in grads], axis=1),
                               jnp.concatenate([g[6] for g in grads], axis=1), jnp.zeros((6, wy), F32)], axis=0)
        first = (b_id == 0) & (i == 0)

        @pl.when(first)
        def _():
            sm_ref[...] = upd

        @pl.when(jnp.logical_not(first))
        def _():
            sm_ref[...] += upd

    def rows(h, b, i):
        return b * nb + (nb - 1 - i)

    return pl.pallas_call(
        body, name="hgrn_bwd", grid=(AH // nh, nbatch, nb),
        in_specs=[pl.BlockSpec((blk, wp), lambda h, b, i: (rows(h, b, i), COL_A // wp + h)),
                  pl.BlockSpec((blk, wy), lambda h, b, i: (rows(h, b, i), h)),
                  pl.BlockSpec((1, nh, 128, 128), lambda h, b, i: ((h * nbatch + b) * nb + (nb - 1 - i), 0, 0, 0)),
                  pl.BlockSpec((2, wy), lambda h, b, i: (0, h)),
                  pl.BlockSpec((1, wy), lambda h, b, i: (0, h)),
                  pl.BlockSpec(memory_space=pl.ANY)],
        out_specs=[pl.BlockSpec((blk, wp), lambda h, b, i: (rows(h, b, i), COL_A // wp + h)),
                   pl.BlockSpec((8, wy), lambda h, b, i: (0, h))],
        out_shape=[jax.ShapeDtypeStruct((t, NP), BF16), jax.ShapeDtypeStruct((8, AW), F32)],
        input_output_aliases={5: 0},
        scratch_shapes=[pltpu.VMEM((nh, 128, 128), F32)],
        compiler_params=_params(("parallel", "arbitrary", "arbitrary"), 56),
    )(proj, dya, ckpt, lb_logits, norm_w, dproj)


def _log_sigmoid(z):
    return jnp.minimum(z, 0.0) - jnp.log(1.0 + jnp.exp(-jnp.abs(z)))


def _fox_cum(proj, bias128, nbatch, seq):
    t = proj.shape[0]
    ts = min(512, seq)
    nb = seq // ts

    def body(p_ref, b_ref, c_ref, carry):
        @pl.when(pl.program_id(1) == 0)
        def _():
            carry[...] = jnp.zeros_like(carry)
        cum = _dot_f32(_tri(ts, False), _log_sigmoid(p_ref[...] + b_ref[...])) + carry[...]
        carry[...] = cum[ts - 1:ts, :]
        cum2 = cum * LOG2E
        lane = lax.broadcasted_iota(jnp.int32, (ts, 128), 1)
        for p in range(4):
            c_ref[p] = jnp.where(lane < 64, cum2[:, 2 * p:2 * p + 1], cum2[:, 2 * p + 1:2 * p + 2])

    return pl.pallas_call(
        body, name="fox_cum", grid=(nbatch, nb),
        in_specs=[pl.BlockSpec((ts, 128), lambda b, i: (b * nb + i, 0)),
                  pl.BlockSpec((1, 128), lambda b, i: (0, 0))],
        out_specs=pl.BlockSpec((4, ts, 128), lambda b, i: (0, b * nb + i, 0)),
        out_shape=jax.ShapeDtypeStruct((4, t, 128), F32),
        scratch_shapes=[pltpu.VMEM((1, 128), F32)],
        compiler_params=_params(("parallel", "arbitrary")),
    )(proj, bias128)


def _fox_scores_t(q128, k128, cc128, hh, masked):
    tq, tk = q128.shape[0], k128.shape[0]
    qh = _head_lanes((q128 * (LOG2E * BDH ** -0.5)).astype(BF16), hh)
    s = _dot_nt(k128, qh) - cc128[:, 64 * hh:64 * hh + 1]
    if masked:
        key = lax.broadcasted_iota(jnp.int32, (tk, tq), 0)
        qry = lax.broadcasted_iota(jnp.int32, (tk, tq), 1)
        s = jnp.where(key <= qry, s, NEG)
    return s


def _causal_pairs(nq, key_major):
    if key_major:
        pairs = [(i, j) for j in range(nq) for i in range(j, nq)]
    else:
        pairs = [(i, j) for i in range(nq) for j in range(i + 1)]
    return (jnp.asarray([p[0] for p in pairs], jnp.int32), jnp.asarray([p[1] for p in pairs], jnp.int32))


def _head_lanes(x128, hh):
    lane = lax.broadcasted_iota(jnp.int32, x128.shape, 1)
    return jnp.where((lane < 64) if hh == 0 else (lane >= 64), x128, jnp.zeros_like(x128))


def _with_ones_lane(x128, hh):
    lane = lax.broadcasted_iota(jnp.int32, x128.shape, 1)
    one = jnp.ones_like(x128)
    zero = jnp.zeros_like(x128)
    if hh == 0:
        return jnp.where(lane < 64, x128, jnp.where(lane == 64, one, zero))
    return jnp.where(lane >= 64, x128, jnp.where(lane == 0, one, zero))


def _fox_fwd(proj, cum_cols, nbatch, seq):
    t = proj.shape[0]
    tq = tk = min(512, seq)
    nq = seq // tq
    npr = FOX_PAIRS
    qi, kj = _causal_pairs(nq, key_major=False)

    def body(qi_ref, kj_ref, q_ref, kv_ref, cc_ref, o_ref, lse_ref, m_s, acc_s):
        s_id = pl.program_id(2)
        i, j = qi_ref[s_id], kj_ref[s_id]

        @pl.when(j == 0)
        def _():
            m_s[...] = jnp.full_like(m_s, NEG)
            acc_s[...] = jnp.zeros_like(acc_s)

        def step(masked):
            heads = [(pr, hh) for pr in range(npr) for hh in range(2)]
            m_prev = m_s[0:2 * npr, :]
            acc_prev = [acc_s[h] for h in range(2 * npr)]
            q128 = [q_ref[:, 128 * pr:128 * pr + 128] for pr in range(npr)]
            k128 = [kv_ref[:, 256 * pr:256 * pr + 128].astype(BF16) for pr in range(npr)]
            v128 = [kv_ref[:, 256 * pr + 128:256 * pr + 256].astype(BF16) for pr in range(npr)]
            s = [_fox_scores_t(q128[pr], k128[pr], cc_ref[pr], hh, masked) for pr, hh in heads]
            m_new = [jnp.maximum(m_prev[h:h + 1, :], jnp.max(s[h], axis=0, keepdims=True)) for h in range(2 * npr)]
            acc_new = []
            for h, (pr, hh) in enumerate(heads):
                alpha = jnp.exp2(m_prev[h:h + 1, :] - m_new[h])
                p = jnp.exp2(s[h] - m_new[h]).astype(BF16)
                acc_new.append(acc_prev[h] * alpha + _dot_tn(_with_ones_lane(v128[pr], hh), p))
            for h in range(2 * npr):
                acc_s[h] = acc_new[h]
            m_s[0:2 * npr, :] = jnp.concatenate(m_new, axis=0)

        @pl.when(j < i)
        def _():
            step(False)

        @pl.when(j == i)
        def _():
            step(True)
            outs = []
            for pr in range(npr):
                a0, a1 = acc_s[2 * pr], acc_s[2 * pr + 1]
                l0, l1 = a0[64:65, :], a1[0:1, :]
                outs.append(jnp.concatenate([a0[0:64, :] / l0, a1[64:128, :] / l1], axis=0).T)
                lse_ref[0, pr] = jnp.concatenate(
                    [m_s[2 * pr:2 * pr + 1, :] + jnp.log2(l0), m_s[2 * pr + 1:2 * pr + 2, :] + jnp.log2(l1),
                     jnp.zeros((6, tq), F32)], axis=0)
            o_ref[...] = jnp.concatenate(outs, axis=1).astype(o_ref.dtype)

    return pl.pallas_call(
        body, name="fox_fwd",
        grid_spec=pltpu.PrefetchScalarGridSpec(
            num_scalar_prefetch=2, grid=(nbatch, 4 // npr, qi.shape[0]),
            in_specs=[pl.BlockSpec((tq, 128 * npr), lambda b, p, s, qi, kj: (b * nq + qi[s], COL_BQ // (128 * npr) + p)),
                      pl.BlockSpec((tk, 256 * npr), lambda b, p, s, qi, kj: (b * nq + kj[s], COL_KV // (256 * npr) + p)),
                      pl.BlockSpec((npr, tk, 128), lambda b, p, s, qi, kj: (p, b * nq + kj[s], 0))],
            out_specs=[pl.BlockSpec((tq, 128 * npr), lambda b, p, s, qi, kj: (b * nq + qi[s], p)),
                       pl.BlockSpec((1, npr, 8, tq), lambda b, p, s, qi, kj: (b, p, 0, qi[s]))],
            scratch_shapes=[pltpu.VMEM((8, tq), F32), pltpu.VMEM((2 * npr, 128, tq), F32)]),
        out_shape=[jax.ShapeDtypeStruct((t, 512), BF16), jax.ShapeDtypeStruct((nbatch, 4, 8, seq), F32)],
        compiler_params=_params(("parallel", "parallel", "arbitrary"), 56),
    )(qi, kj, proj, proj, cum_cols)


def _fox_bwd(proj, cum_cols, lse, yb, dyb, dproj, nbatch, seq):
    t = proj.shape[0]
    tq = tk = min(512, seq)
    nq = seq // tq
    scale = BDH ** -0.5
    qi, kj = _causal_pairs(nq, key_major=True)
    nsteps = qi.shape[0]

    npr = FOX_PAIRS

    def body(qi_ref, kj_ref, q_ref, kv_ref, cc_ref, lse_ref, o_ref, do_ref, dp_in,
             dkv_ref, dq_ref, drs_ref, dcs_ref, dk_s, dv_s, dqa_s):
        del dp_in
        pg, s_id = pl.program_id(1), pl.program_id(2)
        i, j = qi_ref[s_id], kj_ref[s_id]

        @pl.when(i == j)
        def _():
            dk_s[...] = jnp.zeros_like(dk_s)
            dv_s[...] = jnp.zeros_like(dv_s)

        @pl.when(s_id == 0)
        def _():
            dqa_s[...] = jnp.zeros_like(dqa_s)

        def step(masked):
            dk_prev = [dk_s[h] for h in range(2 * npr)]
            dq_prev = [dqa_s[i, h] for h in range(2 * npr)]
            dv_new = [dv_s[pr] for pr in range(npr)]
            dk_new, dq_new = [], []
            for pr in range(npr):
                lanes = slice(128 * pr, 128 * pr + 128)
                q128 = q_ref[:, lanes]
                qs128 = (q128 * scale).astype(BF16)
                k128 = kv_ref[:, 256 * pr:256 * pr + 128].astype(BF16)
                v128 = kv_ref[:, 256 * pr + 128:256 * pr + 256].astype(BF16)
                do128 = do_ref[:, lanes]
                doo = do128 * o_ref[:, lanes].astype(F32)
                do16 = do128.astype(BF16)
                for hh in range(2):
                    s = _fox_scores_t(q128, k128, cc_ref[pr], hh, masked)
                    p = jnp.exp2(s - lse_ref[0, pr, hh:hh + 1, :])
                    dd = lax.dot_general(jnp.ones((8, 128), F32), _head_lanes(doo, hh), (((1,), (1,)), ((), ())),
                                         preferred_element_type=F32, precision=HIGHEST)[0:1, :]
                    doh = _head_lanes(do16, hh)
                    dp = _dot_nt(v128, doh)
                    ds = (p * (dp - dd)).astype(BF16)
                    dv_new[pr] = dv_new[pr] + _dot(p, doh)
                    dk_new.append(dk_prev[2 * pr + hh] + _dot(ds, _with_ones_lane(qs128, hh)))
                    dq_new.append(dq_prev[2 * pr + hh] + _dot_tn(_with_ones_lane(k128, hh), ds))
            for pr in range(npr):
                dv_s[pr] = dv_new[pr]
            for h in range(2 * npr):
                dk_s[h] = dk_new[h]
                dqa_s[i, h] = dq_new[h]

        @pl.when(i == j)
        def _():
            step(True)

        @pl.when(i > j)
        def _():
            step(False)

        def sums_to_lanes(lane, pr, s0, s1):
            hp = npr * pg + pr
            return jnp.where(lane == 2 * hp, s0, jnp.where(lane == 2 * hp + 1, s1, 0.0))

        @pl.when(i == nq - 1)
        def _():
            lane = lax.broadcasted_iota(jnp.int32, (tk, 128), 1)
            for pr in range(npr):
                k0, k1 = dk_s[2 * pr], dk_s[2 * pr + 1]
                dkv_ref[:, 256 * pr:256 * pr + 128] = jnp.where(lane < 64, k0, k1).astype(dkv_ref.dtype)
                dkv_ref[:, 256 * pr + 128:256 * pr + 256] = dv_s[pr].astype(dkv_ref.dtype)
                dcs_ref[pr] = sums_to_lanes(lane, pr, k0[:, 64:65], k1[:, 0:1])

        @pl.when(s_id == nsteps - 1)
        def _():
            lane = lax.broadcasted_iota(jnp.int32, (tq, 128), 1)
            for blk in range(nq):
                rows = pl.ds(blk * tq, tq)
                for pr in range(npr):
                    a0 = dqa_s[blk, 2 * pr].T
                    a1 = dqa_s[blk, 2 * pr + 1].T
                    dq_ref[rows, 128 * pr:128 * pr + 128] = (jnp.where(lane < 64, a0, a1) * scale).astype(dq_ref.dtype)
                    drs_ref[pr, rows, :] = sums_to_lanes(lane, pr, a0[:, 64:65], a1[:, 0:1])

    return pl.pallas_call(
        body, name="fox_bwd",
        grid_spec=pltpu.PrefetchScalarGridSpec(
            num_scalar_prefetch=2, grid=(nbatch, 4 // npr, nsteps),
            in_specs=[pl.BlockSpec((tq, 128 * npr), lambda b, p, s, qi, kj: (b * nq + qi[s], COL_BQ // (128 * npr) + p)),
                      pl.BlockSpec((tk, 256 * npr), lambda b, p, s, qi, kj: (b * nq + kj[s], COL_KV // (256 * npr) + p)),
                      pl.BlockSpec((npr, tk, 128), lambda b, p, s, qi, kj: (p, b * nq + kj[s], 0)),
                      pl.BlockSpec((1, npr, 8, tq), lambda b, p, s, qi, kj: (b, p, 0, qi[s])),
                      pl.BlockSpec((tq, 128 * npr), lambda b, p, s, qi, kj: (b * nq + qi[s], p)),
                      pl.BlockSpec((tq, 128 * npr), lambda b, p, s, qi, kj: (b * nq + qi[s], p)),
                      pl.BlockSpec(memory_space=pl.ANY)],
            out_specs=[pl.BlockSpec((tk, 256 * npr), lambda b, p, s, qi, kj: (b * nq + kj[s], COL_KV // (256 * npr) + p)),
                       pl.BlockSpec((seq, 128 * npr), lambda b, p, s, qi, kj: (b, p)),
                       pl.BlockSpec((npr, seq, 128), lambda b, p, s, qi, kj: (p, b, 0)),
                       pl.BlockSpec((npr, tk, 128), lambda b, p, s, qi, kj: (p, b * nq + kj[s], 0))],
            scratch_shapes=[pltpu.VMEM((2 * npr, tk, 128), F32), pltpu.VMEM((npr, tk, 128), F32),
                            pltpu.VMEM((nq, 2 * npr, 128, tq), F32)]),
        out_shape=[jax.ShapeDtypeStruct((t, NP), BF16), jax.ShapeDtypeStruct((t, 512), BF16),
                   jax.ShapeDtypeStruct((4, t, 128), F32), jax.ShapeDtypeStruct((4, t, 128), F32)],
        input_output_aliases={8: 0},
        compiler_params=_params(("parallel", "parallel", "arbitrary"), 60),
    )(qi, kj, proj, proj, cum_cols, lse, yb, dyb, dproj)


def _place_cols(dproj, src, col):
    t, w = src.shape
    tm = 1024 if t % 1024 == 0 else t

    def body(s_ref, dp_in, o_ref):
        del dp_in
        o_ref[...] = s_ref[...]

    return pl.pallas_call(
        body, name="place_cols", grid=(t // tm,),
        in_specs=[pl.BlockSpec((tm, w), lambda i: (i, 0)), pl.BlockSpec(memory_space=pl.ANY)],
        out_specs=pl.BlockSpec((tm, w), lambda i: (i, col // w)),
        out_shape=jax.ShapeDtypeStruct(dproj.shape, dproj.dtype),
        input_output_aliases={1: 0},
        compiler_params=_params(("parallel",)),
    )(src, dproj)


def _fox_dbf(proj, bias128, drs, dcs, dproj, nbatch, seq):
    t = proj.shape[0]
    ts = min(512, seq)
    nb = seq // ts

    def body(p_ref, b_ref, dr_ref, dc_ref, dp_in, dp_ref, sm_ref, carry):
        del dp_in
        b_id, i = pl.program_id(0), pl.program_id(1)

        @pl.when(i == 0)
        def _():
            carry[...] = jnp.zeros_like(carry)

        dcum = (dr_ref[0] - dc_ref[0]) + (dr_ref[1] - dc_ref[1]) + (dr_ref[2] - dc_ref[2]) + (dr_ref[3] - dc_ref[3])
        rc = _dot_f32(_tri(ts, True), dcum) + carry[...]
        carry[...] = rc[0:1, :]
        z = p_ref[...] + b_ref[...]
        lane = lax.broadcasted_iota(jnp.int32, (ts, 128), 1)
        dz = jnp.where(lane < BH, rc * jax.nn.sigmoid(-z), 0.0)
        dp_ref[...] = dz.astype(dp_ref.dtype)
        upd = jnp.concatenate([jnp.sum(dz, axis=0, keepdims=True), jnp.zeros((7, 128), F32)], axis=0)
        first = (b_id == 0) & (i == 0)

        @pl.when(first)
        def _():
            sm_ref[...] = upd

        @pl.when(jnp.logical_not(first))
        def _():
            sm_ref[...] += upd

    def rows(b, i):
        return b * nb + (nb - 1 - i)

    return pl.pallas_call(
        body, name="fox_dbf", grid=(nbatch, nb),
        in_specs=[pl.BlockSpec((ts, 128), lambda b, i: (rows(b, i), 0)),
                  pl.BlockSpec((1, 128), lambda b, i: (0, 0)),
                  pl.BlockSpec((4, ts, 128), lambda b, i: (0, rows(b, i), 0)),
                  pl.BlockSpec((4, ts, 128), lambda b, i: (0, rows(b, i), 0)),
                  pl.BlockSpec(memory_space=pl.ANY)],
        out_specs=[pl.BlockSpec((ts, 128), lambda b, i: (rows(b, i), COL_BF // 128)),
                   pl.BlockSpec((8, 128), lambda b, i: (0, 0))],
        out_shape=[jax.ShapeDtypeStruct((t, NP), BF16), jax.ShapeDtypeStruct((8, 128), F32)],
        input_output_aliases={4: 0},
        scratch_shapes=[pltpu.VMEM((1, 128), F32)],
        compiler_params=_params(("arbitrary", "arbitrary")),
    )(proj, bias128, drs, dcs, dproj)


def _ln_stats(z):
    mu = jnp.mean(z, axis=-1, keepdims=True)
    zc = z - mu
    rstd = lax.rsqrt(jnp.mean(zc * zc, axis=-1, keepdims=True) + LN_EPS)
    return zc * rstd, rstd


def _ln_bwd(dy, xhat, rstd, w):
    dxh = dy * w
    return rstd * (dxh - jnp.mean(dxh, axis=-1, keepdims=True) - xhat * jnp.mean(dxh * xhat, axis=-1, keepdims=True))


def _merge_fwd(ya, yb, proj, x2, mod8, wba, wbb, wout, ln1w, ln1b, seq):
    t = x2.shape[0]
    tm = min(512, seq)
    tpb = seq // tm

    def body(ya_ref, yb_ref, g_ref, x_ref, mod_ref, wa_ref, wb_ref, wo_ref, lw_ref, lb_ref, mg_ref, u_ref, x1_ref):
        ga = jax.nn.sigmoid(g_ref[:, 0:D].astype(F32))
        gb = jax.nn.sigmoid(g_ref[:, D:2 * D].astype(F32))
        merged = (ga * jnp.dot(ya_ref[...], wa_ref[...], preferred_element_type=F32)
                  + gb * jnp.dot(yb_ref[...], wb_ref[...], preferred_element_type=F32))
        mg = merged.astype(BF16)
        mg_ref[...] = mg
        u = jnp.dot(mg, wo_ref[...], preferred_element_type=F32)
        u_ref[...] = u.astype(u_ref.dtype)
        xhat, _ = _ln_stats(ALPHA * x_ref[...] + (1.0 + mod_ref[0, 2:3, :]) * u)
        x1_ref[...] = xhat * lw_ref[...] + lb_ref[...]

    tok = lambda w: pl.BlockSpec((tm, w), lambda i: (i, 0))
    full = lambda a: pl.BlockSpec(a.shape, lambda i: (0,) * a.ndim)
    return pl.pallas_call(
        body, name="merge_fwd", grid=(t // tm,),
        in_specs=[tok(512), tok(512), pl.BlockSpec((tm, 2048), lambda i: (i, COL_GATES // 2048)), tok(D),
                  pl.BlockSpec((1, 8, D), lambda i: (i // tpb, 0, 0)),
                  full(wba), full(wbb), full(wout), full(ln1w), full(ln1b)],
        out_specs=[tok(D), tok(D), tok(D)],
        out_shape=[jax.ShapeDtypeStruct((t, D), BF16), jax.ShapeDtypeStruct((t, D), BF16),
                   jax.ShapeDtypeStruct((t, D), F32)],
        compiler_params=_params(("parallel",), 48),
    )(ya, yb, proj, x2, mod8, wba, wbb, wout, ln1w, ln1b)


def _merge_bwd(du, ya, yb, proj, wba, wbb, wout, token, seq):
    t = du.shape[0]
    tm = min(512, seq)

    def body(du_ref, ya_ref, yb_ref, g_ref, wa_ref, wb_ref, wo_ref, token_ref,
             dp_ref, dpa_ref, dpb_ref, dya_ref, dyb_ref):
        del token_ref
        ga = jax.nn.sigmoid(g_ref[:, 0:D].astype(F32))
        gb = jax.nn.sigmoid(g_ref[:, D:2 * D].astype(F32))
        dm = _dot_nt(du_ref[...], wo_ref[...])
        pa = jnp.dot(ya_ref[...], wa_ref[...], preferred_element_type=F32)
        pb = jnp.dot(yb_ref[...], wb_ref[...], preferred_element_type=F32)
        dpa = (dm * ga).astype(BF16)
        dpb = (dm * gb).astype(BF16)
        dpa_ref[...] = dpa
        dpb_ref[...] = dpb
        dp_ref[:, 0:D] = (dm * pa * ga * (1.0 - ga)).astype(BF16)
        dp_ref[:, D:2 * D] = (dm * pb * gb * (1.0 - gb)).astype(BF16)
        dya_ref[...] = _dot_nt(dpa, wa_ref[...])
        dyb_ref[...] = _dot_nt(dpb, wb_ref[...])

    tok = lambda w: pl.BlockSpec((tm, w), lambda i: (i, 0))
    full = lambda a: pl.BlockSpec(a.shape, lambda i: (0,) * a.ndim)
    return pl.pallas_call(
        body, name="merge_bwd", grid=(t // tm,),
        in_specs=[tok(D), tok(512), tok(512), pl.BlockSpec((tm, 2048), lambda i: (i, COL_GATES // 2048)),
                  full(wba), full(wbb), full(wout), full(token)],
        out_specs=[pl.BlockSpec((tm, 2048), lambda i: (i, COL_GATES // 2048)), tok(D), tok(D), tok(512), tok(512)],
        out_shape=[jax.ShapeDtypeStruct((t, NP), BF16), jax.ShapeDtypeStruct((t, D), BF16),
                   jax.ShapeDtypeStruct((t, D), BF16), jax.ShapeDtypeStruct((t, 512), F32),
                   jax.ShapeDtypeStruct((t, 512), F32)],
        compiler_params=_params(("parallel",), 48),
    )(du, ya, yb, proj, wba, wbb, wout, token)


def _ffn_fwd(x1, mod8, wg, wu, wd, target, ln2w, ln2b, seq):
    t = x1.shape[0]
    tm = min(FFN_TOKENS, seq)
    nf, tf, _ = wg.shape
    tpb = seq // tm
    nbatch = t // seq

    def body(x_ref, mod_ref, wg_ref, wu_ref, wd_ref, t_ref, lw_ref, lb_ref,
             a_ref, b_ref, h_s, dz_ref, st_ref, dm_ref, acc):
        i, j = pl.program_id(0), pl.program_id(1)

        @pl.when(j == 0)
        def _():
            h_s[...] = (x_ref[...] * (1.0 + mod_ref[0, 4:5, :]) + mod_ref[0, 3:4, :]).astype(BF16)
            acc[...] = jnp.zeros_like(acc)

        a = _dot_nt(h_s[...], wg_ref[0])
        b = _dot_nt(h_s[...], wu_ref[0])
        a_ref[0] = a.astype(BF16)
        b_ref[0] = b.astype(BF16)
        acc[...] += _dot(a * jax.nn.sigmoid(a) * b, wd_ref[0])

        @pl.when(j == nf - 1)
        def _():
            ffn = acc[...]
            xhat, rstd = _ln_stats(ALPHA * x_ref[...] + (1.0 + mod_ref[0, 5:6, :]) * ffn)
            diff = xhat * lw_ref[...] + lb_ref[...] - t_ref[...]
            loss = 0.5 * jnp.sum(jnp.sum(diff * diff, axis=-1, keepdims=True), axis=0, keepdims=True) / D
            dy = diff * (1.0 / D)
            dz = _ln_bwd(dy, xhat, rstd, lw_ref[...])
            dz_ref[...] = dz
            lane = lax.broadcasted_iota(jnp.int32, (1, D), 1)
            upd = jnp.concatenate(
                [jnp.sum(dy * xhat, axis=0, keepdims=True), jnp.sum(dy, axis=0, keepdims=True),
                 jnp.where(lane == 0, loss, 0.0), jnp.zeros((5, D), F32)], axis=0)
            dmu = jnp.concatenate(
                [jnp.zeros((5, D), F32), jnp.sum(dz * ffn, axis=0, keepdims=True), jnp.zeros((2, D), F32)], axis=0)

            @pl.when(i == 0)
            def _():
                st_ref[...] = upd

            @pl.when(i > 0)
            def _():
                st_ref[...] += upd

            @pl.when(i % tpb == 0)
            def _():
                dm_ref[0] = dmu

            @pl.when(i % tpb != 0)
            def _():
                dm_ref[0] += dmu

    row = lambda: pl.BlockSpec((tm, D), lambda i, j: (i, 0))
    vec = lambda: pl.BlockSpec((1, D), lambda i, j: (0, 0))
    return pl.pallas_call(
        body, name="ffn_fwd", grid=(t // tm, nf),
        in_specs=[row(), pl.BlockSpec((1, 8, D), lambda i, j: (i // tpb, 0, 0)),
                  pl.BlockSpec((1, tf, D), lambda i, j: (j, 0, 0)), pl.BlockSpec((1, tf, D), lambda i, j: (j, 0, 0)),
                  pl.BlockSpec((1, tf, D), lambda i, j: (j, 0, 0)), row(), vec(), vec()],
        out_specs=[pl.BlockSpec((1, tm, tf), lambda i, j: (j, i, 0)), pl.BlockSpec((1, tm, tf), lambda i, j: (j, i, 0)),
                   row(), row(), pl.BlockSpec((8, D), lambda i, j: (0, 0)),
                   pl.BlockSpec((1, 8, D), lambda i, j: (i // tpb, 0, 0))],
        out_shape=[jax.ShapeDtypeStruct((nf, t, tf), BF16), jax.ShapeDtypeStruct((nf, t, tf), BF16),
                   jax.ShapeDtypeStruct((t, D), BF16),
                   jax.ShapeDtypeStruct((t, D), F32), jax.ShapeDtypeStruct((8, D), F32),
                   jax.ShapeDtypeStruct((nbatch, 8, D), F32)],
        scratch_shapes=[pltpu.VMEM((tm, D), F32)],
        compiler_params=_params(("arbitrary", "arbitrary"), 60),
    )(x1, mod8, wg, wu, wd, target, ln2w, ln2b)


def _ffn_bwd(dz2, a, b, wg, wu, wd, x1, x2, u, mod8, ln1w, seq):
    t = x1.shape[0]
    tm = min(512, seq)
    nf, tf, _ = wg.shape
    tpb = seq // tm
    nbatch = t // seq

    def body(dz_ref, a_ref, b_ref, wg_ref, wu_ref, wd_ref, x1_ref, x_ref, u_ref, mod_ref, lw_ref,
             da_ref, db_ref, hm_ref, df_ref, du_ref, dxp_ref, st_ref, dm_ref, acc):
        i, j = pl.program_id(0), pl.program_id(1)

        @pl.when(j == 0)
        def _():
            df_ref[...] = ((1.0 + mod_ref[0, 5:6, :]) * dz_ref[...]).astype(BF16)
            acc[...] = jnp.zeros_like(acc)

        dhm = _dot_nt(df_ref[...], wd_ref[0])
        av = a_ref[0].astype(F32)
        bv = b_ref[0].astype(F32)
        sg = jax.nn.sigmoid(av)
        sl = av * sg
        hm_ref[0] = (sl * bv).astype(BF16)
        da = (dhm * bv * (sg * (1.0 + av * (1.0 - sg)))).astype(BF16)
        db = (dhm * sl).astype(BF16)
        da_ref[0] = da
        db_ref[0] = db
        acc[...] += _dot(da, wg_ref[0]) + _dot(db, wu_ref[0])

        @pl.when(j == nf - 1)
        def _():
            dh2 = acc[...]
            x1v = x1_ref[...]
            uv = u_ref[...].astype(F32)
            dx1 = ALPHA * dz_ref[...] + dh2 * (1.0 + mod_ref[0, 4:5, :])
            xhat, rstd = _ln_stats(ALPHA * x_ref[...] + (1.0 + mod_ref[0, 2:3, :]) * uv)
            dz1 = _ln_bwd(dx1, xhat, rstd, lw_ref[...])
            du_ref[...] = ((1.0 + mod_ref[0, 2:3, :]) * dz1).astype(BF16)
            dxp_ref[...] = (ALPHA * dz1).astype(dxp_ref.dtype)
            upd = jnp.concatenate(
                [jnp.sum(dx1 * xhat, axis=0, keepdims=True), jnp.sum(dx1, axis=0, keepdims=True),
                 jnp.zeros((6, D), F32)], axis=0)
            dmu = jnp.concatenate(
                [jnp.zeros((2, D), F32), jnp.sum(dz1 * uv, axis=0, keepdims=True),
                 jnp.sum(dh2, axis=0, keepdims=True), jnp.sum(dh2 * x1v, axis=0, keepdims=True),
                 jnp.zeros((3, D), F32)], axis=0)

            @pl.when(i == 0)
            def _():
                st_ref[...] = upd

            @pl.when(i > 0)
            def _():
                st_ref[...] += upd

            @pl.when(i % tpb == 0)
            def _():
                dm_ref[0] = dmu

            @pl.when(i % tpb != 0)
            def _():
                dm_ref[0] += dmu

    row = lambda: pl.BlockSpec((tm, D), lambda i, j: (i, 0))
    ffb = lambda: pl.BlockSpec((1, tm, tf), lambda i, j: (j, i, 0))
    return pl.pallas_call(
        body, name="ffn_bwd", grid=(t // tm, nf),
        in_specs=[row(), ffb(), ffb(),
                  pl.BlockSpec((1, tf, D), lambda i, j: (j, 0, 0)), pl.BlockSpec((1, tf, D), lambda i, j: (j, 0, 0)),
                  pl.BlockSpec((1, tf, D), lambda i, j: (j, 0, 0)), row(), row(), row(),
                  pl.BlockSpec((1, 8, D), lambda i, j: (i // tpb, 0, 0)), pl.BlockSpec((1, D), lambda i, j: (0, 0))],
        out_specs=[ffb(), ffb(), ffb(), row(), row(), row(), pl.BlockSpec((8, D), lambda i, j: (0, 0)),
                   pl.BlockSpec((1, 8, D), lambda i, j: (i // tpb, 0, 0))],
        out_shape=[jax.ShapeDtypeStruct((nf, t, tf), BF16), jax.ShapeDtypeStruct((nf, t, tf), BF16),
                   jax.ShapeDtypeStruct((nf, t, tf), BF16), jax.ShapeDtypeStruct((t, D), BF16),
                   jax.ShapeDtypeStruct((t, D), BF16), jax.ShapeDtypeStruct((t, D), BF16),
                   jax.ShapeDtypeStruct((8, D), F32), jax.ShapeDtypeStruct((nbatch, 8, D), F32)],
        scratch_shapes=[pltpu.VMEM((tm, D), F32)],
        compiler_params=_params(("arbitrary", "arbitrary"), 60),
    )(dz2, a, b, wg, wu, wd, x1, x2, u, mod8, ln1w)


def _adamw_math(w, g, m, v):
    m = B1 * m + (1.0 - B1) * g
    v = B2 * v + (1.0 - B2) * (g * g)
    m_hat = m / (1.0 - B1 ** STEP)
    v_hat = v / (1.0 - B2 ** STEP)
    return -LR * (m_hat / (jnp.sqrt(v_hat) + EPS) + WD * w), m, v


def _adamw(w, g, m, v, name):
    rows, cols = w.shape
    tr = rows
    for cand in (128, 64, 32, 16, 8):
        if rows % cand == 0:
            tr = cand
            break

    def body(w_ref, g_ref, m_ref, v_ref, d_ref, mo_ref, vo_ref):
        d, mn, vn = _adamw_math(w_ref[...], g_ref[...], m_ref[...], v_ref[...])
        d_ref[...] = d
        mo_ref[...] = mn
        vo_ref[...] = vn

    spec = pl.BlockSpec((tr, cols), lambda i: (i, 0))
    return pl.pallas_call(
        body, name=name, grid=(rows // tr,), in_specs=[spec] * 4, out_specs=[spec] * 3,
        out_shape=[jax.ShapeDtypeStruct((rows, cols), F32)] * 3,
        compiler_params=_params(("parallel",), 48),
    )(w, g, m, v)


def _adamw_halves(w, g_mine, g_sib, m, v, c_idx, name):
    rows, cols = w.shape
    hr = rows // 2
    tr = next(cand for cand in (128, 88, 64, 32, 16, 8) if hr % cand == 0)
    tph = hr // tr

    def body(c_ref, w_ref, gm_ref, gs_ref, m_ref, v_ref, g_ref, d_ref, mo_ref, vo_ref):
        g = jnp.where(pl.program_id(0) == c_ref[0], gm_ref[...], gs_ref[...])
        d, mn, vn = _adamw_math(w_ref[...], g, m_ref[...], v_ref[...])
        g_ref[...] = g
        d_ref[...] = d
        mo_ref[...] = mn
        vo_ref[...] = vn

    full = pl.BlockSpec((tr, cols), lambda h, i, c: (h * tph + i, 0))
    half = pl.BlockSpec((tr, cols), lambda h, i, c: (i, 0))
    return pl.pallas_call(
        body, name=name,
        grid_spec=pltpu.PrefetchScalarGridSpec(
            num_scalar_prefetch=1, grid=(2, tph), in_specs=[full, half, half, full, full], out_specs=[full] * 4),
        out_shape=[jax.ShapeDtypeStruct((rows, cols), F32)] * 4,
        compiler_params=_params(("parallel", "parallel"), 48),
    )(c_idx, w, g_mine, g_sib, m, v)


def _grad_w_ada(c_all, dmod_cols):
    def body(c_ref, d_ref, o_ref):
        c = c_ref[...]
        o_ref[...] = lax.dot_general(c * jax.nn.sigmoid(c), d_ref[...], (((0,), (0,)), ((), ())),
                                     preferred_element_type=F32, precision=HIGHEST)

    return pl.pallas_call(
        body, name="grad_w_ada", out_shape=jax.ShapeDtypeStruct((D, dmod_cols.shape[1]), F32),
        compiler_params=_params(vmem_mb=48),
    )(c_all, dmod_cols)


def _small_update(gath, w8, m8, v8):
    def body(g_ref, w_ref, m_ref, v_ref, go_ref, d_ref, mo_ref, vo_ref):
        g0 = g_ref[0, 0:1, :] + g_ref[0, 1:2, :]
        g1 = g_ref[0, 2:3, :]
        for dev in range(1, N_DEV):
            g0 = g0 + (g_ref[dev, 0:1, :] + g_ref[dev, 1:2, :])
            g1 = g1 + g_ref[dev, 2:3, :]
        w = w_ref[...]
        lb = jax.nn.sigmoid(w[1:2, O_LB0:O_LB1] - w[1:2, O_LB1:O_FOX])
        fac = lb * (1.0 - lb)
        g1 = jnp.concatenate([g1[:, :O_LB0], g1[:, O_LB0:O_LB1] * fac, -g1[:, O_LB1:O_FOX] * fac, g1[:, O_FOX:]],
                             axis=1)
        g = jnp.concatenate([g0, g1, jnp.zeros((6, SMALL_W), F32)], axis=0)
        d, mn, vn = _adamw_math(w, g, m_ref[...], v_ref[...])
        go_ref[...] = g
        d_ref[...] = d
        mo_ref[...] = mn
        vo_ref[...] = vn

    return pl.pallas_call(
        body, name="small_update", out_shape=[jax.ShapeDtypeStruct((8, SMALL_W), F32)] * 4,
        compiler_params=_params(vmem_mb=48),
    )(gath, w8, m8, v8)


def _pack_small(b_ada, ln1w, ln1b, ln2w, ln2b, norm_w, lb_logits, fox):
    row1 = jnp.concatenate([ln1w, ln1b, ln2w, ln2b, norm_w, lb_logits[0:1], lb_logits[1:2], fox,
                            jnp.zeros((1, SMALL_W - O_FOX - BH), F32)], axis=1)
    return jnp.concatenate([b_ada, row1, jnp.zeros((6, SMALL_W), F32)], axis=0)


def _unpack_small(p):
    r = p[1:2]
    lb = jnp.concatenate([r[:, O_LB0:O_LB1], r[:, O_LB1:O_FOX]], axis=0)
    return dict(b_ada=p[0:1], ln1_w=r[:, O_LN1W:O_LN1B], ln1_b=r[:, O_LN1B:O_LN2W], ln2_w=r[:, O_LN2W:O_LN2B],
                ln2_b=r[:, O_LN2B:O_NORM], hgrn_norm_w=r[:, O_NORM:O_LB0], lb_logits=lb,
                fox_f_bias=r[:, O_FOX:O_FOX + BH])


_BIG = ("w_in", "w_branch_a", "w_branch_b", "w_out", "w_ffn_gate", "w_ffn_up", "w_ffn_down")
_TRANSPOSED = ("w_ffn_gate", "w_ffn_up")


def _cols_of_chips(stacked):
    return jnp.concatenate([stacked[k] for k in range(N_CHIPS)], axis=1)


def kernel(x, c, w_ada, b_ada, w_in, fox_f_bias, lb_logits, hgrn_norm_w, w_branch_a, w_branch_b, w_out, ln1_w, ln1_b, w_ffn_gate, w_ffn_up, w_ffn_down, ln2_w, ln2_b, loss_target, m_w_ada, m_b_ada, m_w_in, m_fox_f_bias, m_lb_logits, m_hgrn_norm_w, m_w_branch_a, m_w_branch_b, m_w_out, m_ln1_w, m_ln1_b, m_w_ffn_gate, m_w_ffn_up, m_w_ffn_down, m_ln2_w, m_ln2_b, v_w_ada, v_b_ada, v_w_in, v_fox_f_bias, v_lb_logits, v_hgrn_norm_w, v_w_branch_a, v_w_branch_b, v_w_out, v_ln1_w, v_ln1_b, v_w_ffn_gate, v_w_ffn_up, v_w_ffn_down, v_ln2_w, v_ln2_b):
    nbatch, seq, _ = x.shape
    t = nbatch * seq
    ax, ay, ac = lax.axis_index("x"), lax.axis_index("y"), lax.axis_index("c")
    chip = 2 * ax + ay
    dev = 2 * chip + ac
    chip_arr = jnp.reshape(chip, (1,)).astype(jnp.int32)
    core_arr = jnp.reshape(ac, (1,)).astype(jnp.int32)

    tr = lambda a: jnp.swapaxes(a[0], 0, 1)
    shard_w = dict(w_in=w_in[0], w_branch_a=w_branch_a[0], w_branch_b=w_branch_b[0], w_out=w_out[0],
                   w_ffn_gate=tr(w_ffn_gate), w_ffn_up=tr(w_ffn_up), w_ffn_down=w_ffn_down[0])
    shard_m = dict(w_in=m_w_in[0], w_branch_a=m_w_branch_a[0], w_branch_b=m_w_branch_b[0], w_out=m_w_out[0],
                   w_ffn_gate=tr(m_w_ffn_gate), w_ffn_up=tr(m_w_ffn_up), w_ffn_down=m_w_ffn_down[0])
    shard_v = dict(w_in=v_w_in[0], w_branch_a=v_w_branch_a[0], w_branch_b=v_w_branch_b[0], w_out=v_w_out[0],
                   w_ffn_gate=tr(v_w_ffn_gate), w_ffn_up=tr(v_w_ffn_up), w_ffn_down=v_w_ffn_down[0])

    shard16 = {n: shard_w[n].astype(BF16) for n in _BIG}

    def with_mine(gathered, n):
        return lax.dynamic_update_slice(gathered, shard16[n][None], (chip, 0, 0))

    w_p = _permute_cols(_cols_of_chips(with_mine(_gather_weights([shard16["w_in"]])[0], "w_in")))
    late = _BIG[1:]
    late_send, late_recv, late_src, late_land, late_token = _split_start(
        _gather_copies, [shard16[n] for n in late],
        [lax.empty((N_CHIPS,) + shard16[n].shape, BF16) for n in late], "gather_late_start")

    c8 = jnp.concatenate([c, jnp.zeros((8 - nbatch, D), F32)], axis=0)
    c_all = _allgather8(c8, "gather_c")[:, :nbatch, :].reshape(N_DEV * nbatch, D)
    ncol = w_ada.shape[2]
    b_cols = lax.dynamic_slice_in_dim(b_ada, chip * ncol, ncol, axis=1)
    mod_g = _allgather8(_mod_shard(c_all, w_ada[0], b_cols), "gather_mod")
    mod_all = jnp.concatenate([mod_g[2 * k] for k in range(N_CHIPS)], axis=1)
    mod_mine = lax.dynamic_slice_in_dim(mod_all, dev * nbatch, nbatch, axis=0)
    mod8 = jnp.concatenate([mod_mine.reshape(nbatch, 6, D), jnp.zeros((nbatch, 2, D), F32)], axis=1)
    mod8 = mod8 + late_token[0, 0]

    x2 = x.reshape(t, D)
    tgt2 = loss_target.reshape(t, D)
    bias128 = jnp.concatenate([fox_f_bias, jnp.zeros((1, 128 - BH), F32)], axis=1)

    proj, h16 = _proj(x2, mod8, w_p, seq, BF16, "proj")
    projf = _rows_matmul(h16, w_p[:, COL_BF:], "proj_forget")
    ya, ckpt = _hgrn_fwd(proj, lb_logits, hgrn_norm_w, nbatch, seq)
    cum_cols = _fox_cum(projf, bias128, nbatch, seq)
    yb, lse = _fox_fwd(proj, cum_cols, nbatch, seq)
    late_land = _pass_to_sibling(
        _split_wait(_gather_copies, late_send, late_recv, late_src, late_land, yb, "gather_late_wait"))
    full = {n: with_mine(g, n) for n, g in zip(late, late_land)}
    wba, wbb = _cols_of_chips(full["w_branch_a"]), _cols_of_chips(full["w_branch_b"])
    wout = full["w_out"].reshape(D, D)
    wg_t, wu_t, wd = full["w_ffn_gate"], full["w_ffn_up"], full["w_ffn_down"]
    merged, u, x1 = _merge_fwd(ya, yb, proj, x2, mod8, wba, wbb, wout, ln1_w, ln1_b, seq)
    a_pre, b_pre, h2, dz2, st2, dm2 = _ffn_fwd(x1, mod8, wg_t, wu_t, wd, tgt2, ln2_w, ln2_b, seq)
    loss = lax.psum(st2[2, 0], ("x", "y", "c"))

    da, db, hmid, dffn, du, dxp, st1, dm1 = _ffn_bwd(dz2, a_pre, b_pre, wg_t, wu_t, wd, x1, x2, u, mod8, ln1_w, seq)
    g_st = {}
    g_st["w_ffn_down"] = _tn_matmul(hmid, dffn, "dw_ffn_down", seq)
    g_st["w_ffn_gate"] = _tn_matmul(da, h2, "dw_ffn_gate", seq)
    g_st["w_ffn_up"] = _tn_matmul(db, h2, "dw_ffn_up", seq)
    g_st["w_out"] = _tn_matmul(merged, du, "dw_out", seq).reshape(N_CHIPS, D // N_CHIPS, D)

    def sum_over_cores(names, tag):
        g_list = [g_st[n] for n in names]
        return [_add_my_half(g, o, core_arr, "grad_add_halves_" + n)
                for n, g, o in zip(names, g_list, _swap_halves(g_list, "grad_swap_halves_" + tag))]

    early = ("w_ffn_down", "w_ffn_gate", "w_ffn_up", "w_out")
    e_halves = sum_over_cores(early, "early")
    e_send, e_recv, e_src, e_land, e_token = _split_start(
        _scatter_copies, [h16 for _, h16 in e_halves],
        [lax.empty((3,) + h16.shape[1:], BF16) for _, h16 in e_halves], "grad_scatter_early_start")
    dproj, dpa, dpb, dya, dyb = _merge_bwd(du, ya, yb, proj, wba, wbb, wout, e_token, seq)
    g_st["w_branch_a"] = _tn_matmul(ya, dpa, "dw_branch_a", seq, split=D // N_CHIPS)
    g_st["w_branch_b"] = _tn_matmul(yb, dpb, "dw_branch_b", seq, split=D // N_CHIPS)
    dproj, dq, drs, dcs = _fox_bwd(proj, cum_cols, lse, yb, dyb, dproj, nbatch, seq)
    dproj = _place_cols(dproj, dq, COL_BQ)
    dproj, sm_fox = _fox_dbf(projf, bias128, drs, dcs, dproj, nbatch, seq)
    dproj, sm_hgrn = _hgrn_bwd(proj, dya, ckpt, lb_logits, hgrn_norm_w, dproj, nbatch, seq)
    grad_x2, dm0 = _dh_kernel(dproj, w_p, x2, dxp, mod8, seq)
    dw_in = _unpermute_cols(_tn_matmul(h16, dproj, "dw_in", seq))
    ncin = NIN // N_CHIPS
    g_st["w_in"] = jnp.stack([dw_in[:, k * ncin:(k + 1) * ncin] for k in range(N_CHIPS)])

    e_recv = _split_wait(_scatter_copies, e_send, e_recv, e_src, e_land, dw_in, "grad_scatter_early_wait")
    rest = ("w_in", "w_branch_a", "w_branch_b")
    r_halves = sum_over_cores(rest, "rest")
    r_send, r_rcv, r_src, r_land, r_token = _split_start(
        _scatter_copies, [h16 for _, h16 in r_halves],
        [lax.empty((3,) + h16.shape[1:], BF16) for _, h16 in r_halves], "grad_scatter_rest_start")

    def finish(names, halves, recv, token, tag):
        g_mine = [_add_chips(h32, r, chip_arr, "grad_add_chips_" + n) for n, (h32, _), r in zip(names, halves, recv)]
        g_sib = _join_halves(g_mine, token, "grad_join_halves_" + tag)
        for n, gm, gs in zip(names, g_mine, g_sib):
            grads[n], deltas[n], new_m[n], new_v[n] = _adamw_halves(
                shard_w[n], gm, gs, shard_m[n], shard_v[n], core_arr, "adamw_" + n)

    grads, deltas, new_m, new_v = {}, {}, {}, {}
    finish(early, e_halves, e_recv, r_token, "early")

    dmod = (dm0 + dm1 + dm2)[:, :6, :].reshape(nbatch, 6 * D)
    row2 = jnp.concatenate([st1[0:1], st1[1:2], st2[0:1], st2[1:2], sm_hgrn[1:2], sm_hgrn[0:1], sm_hgrn[0:1],
                            sm_fox[0:1, :BH], jnp.zeros((1, SMALL_W - O_FOX - BH), F32)], axis=1)
    spack = jnp.concatenate([dmod, row2, jnp.zeros((8 - nbatch - 1, SMALL_W), F32)], axis=0)
    spack = spack + r_token[0, 0]
    gath = _allgather8(spack, "gather_small")
    w8 = _pack_small(b_ada, ln1_w, ln1_b, ln2_w, ln2_b, hgrn_norm_w, lb_logits, fox_f_bias)
    m8 = _pack_small(m_b_ada, m_ln1_w, m_ln1_b, m_ln2_w, m_ln2_b, m_hgrn_norm_w, m_lb_logits, m_fox_f_bias)
    v8 = _pack_small(v_b_ada, v_ln1_w, v_ln1_b, v_ln2_w, v_ln2_b, v_hgrn_norm_w, v_lb_logits, v_fox_f_bias)
    sg, sd, smn, svn = (_unpack_small(p) for p in _small_update(gath, w8, m8, v8))
    dmod_all = gath[:, :nbatch, :].reshape(N_DEV * nbatch, SMALL_W)
    g_ada = _grad_w_ada(c_all, lax.dynamic_slice_in_dim(dmod_all, chip * ncol, ncol, axis=1))

    for group, small in zip((grads, deltas, new_m, new_v), (sg, sd, smn, svn)):
        group.update(small)
    grads["w_ada"] = g_ada
    deltas["w_ada"], new_m["w_ada"], new_v["w_ada"] = _adamw(w_ada[0], g_ada, m_w_ada[0], v_w_ada[0], "adamw_w_ada")
    done = sum(new_v[n][0:8, 0:128] for n in early) + new_v["w_ada"][0:8, 0:128]
    r_recv = _split_wait(_scatter_copies, r_send, r_rcv, r_src, r_land, done, "grad_scatter_rest_wait")
    finish(rest, r_halves, r_recv, late_token, "rest")

    names = ["w_ada", "b_ada", "w_in", "fox_f_bias", "lb_logits", "hgrn_norm_w", "w_branch_a", "w_branch_b", "w_out",
             "ln1_w", "ln1_b", "w_ffn_gate", "w_ffn_up", "w_ffn_down", "ln2_w", "ln2_b"]
    shapes = dict(w_ada=w_ada.shape, b_ada=b_ada.shape, w_in=w_in.shape, fox_f_bias=fox_f_bias.shape,
                  lb_logits=lb_logits.shape, hgrn_norm_w=hgrn_norm_w.shape, w_branch_a=w_branch_a.shape,
                  w_branch_b=w_branch_b.shape, w_out=w_out.shape, ln1_w=ln1_w.shape, ln1_b=ln1_b.shape,
                  w_ffn_gate=w_ffn_gate.shape, w_ffn_up=w_ffn_up.shape, w_ffn_down=w_ffn_down.shape,
                  ln2_w=ln2_w.shape, ln2_b=ln2_b.shape)
    outs = [loss, grad_x2.reshape(x.shape)]
    for group in (grads, deltas, new_m, new_v):
        outs += [(jnp.swapaxes(group[n], 0, 1) if n in _TRANSPOSED else group[n]).reshape(shapes[n]) for n in names]
    return tuple(outs)
```

```python
import functools

import jax
import jax.numpy as jnp
from jax import lax
from jax.experimental import pallas as pl
from jax.experimental.pallas import tpu as pltpu

F32 = jnp.float32
BF16 = jnp.bfloat16
MESH = pl.DeviceIdType.MESH
HIGHEST = lax.Precision.HIGHEST

D = 1024
AW = 512
AH = 4
ADH = 128
BH = 8
BDH = 64
DFF = 2816
NIN = 5640
NP = 5760
N_CHIPS = 4
N_DEV = 8
HGRN_BLOCK = 256
FFN_TOKENS = 512
COL_GATES = 0
COL_A = 2048
COL_BQ = 4096
COL_KV = 4608
COL_BF = 5632
HGRN_HEADS = 4
FOX_PAIRS = 2
ALPHA = 2.0 ** 0.25
LN_EPS = 1e-5
RMS_EPS = 1e-6
NEG = -1e30
LOG2E = 1.4426950408889634
LR, B1, B2, EPS, WD, STEP = 0.001, 0.9, 0.999, 1e-08, 0.01, 10
SMALL_W = 6144
O_LN1W, O_LN1B, O_LN2W, O_LN2B, O_NORM, O_LB0, O_LB1, O_FOX = 0, 1024, 2048, 3072, 4096, 4608, 5120, 5632


def _params(sem=None, vmem_mb=None):
    kw = {}
    if sem is not None:
        kw["dimension_semantics"] = sem
    if vmem_mb is not None:
        kw["vmem_limit_bytes"] = vmem_mb << 20
    return pltpu.CompilerParams(**kw)


def _dot(a, b):
    return jnp.dot(a.astype(BF16), b.astype(BF16), preferred_element_type=F32)


def _dot_nt(a, b):
    return lax.dot_general(a.astype(BF16), b.astype(BF16), (((1,), (1,)), ((), ())), preferred_element_type=F32)


def _dot_tn(a, b):
    return lax.dot_general(a.astype(BF16), b.astype(BF16), (((0,), (0,)), ((), ())), preferred_element_type=F32)


def _dot_f32(a, b):
    return jnp.dot(a, b, preferred_element_type=F32, precision=HIGHEST)


def _perm_segments():
    segs = [(3592, 5640)]
    for h in range(4):
        segs += [(128 * h + 512 * t, 128 * h + 512 * t + 128) for t in range(4)]
    segs += [(2048, 2560)]
    for p in range(4):
        segs += [(2560 + 128 * p, 2688 + 128 * p), (3072 + 128 * p, 3200 + 128 * p)]
    segs += [(3584, 3592)]
    return segs


def _permute_cols(w):
    parts = [w[:, a:b] for a, b in _perm_segments()]
    parts.append(jnp.zeros((w.shape[0], NP - NIN), w.dtype))
    return jnp.concatenate(parts, axis=1)


def _unpermute_cols(g):
    pos, where = 0, {}
    for a, b in _perm_segments():
        where[a] = (pos, pos + b - a)
        pos += b - a
    parts = [g[:, where[a][0]:where[a][1]] for a in sorted(where)]
    return jnp.concatenate(parts, axis=1)


def _allgather8(v, name):
    rows, cols = v.shape

    def body(x_ref, out_ref, send_sems, recv_sems, local_sem):
        x, y, c = lax.axis_index("x"), lax.axis_index("y"), lax.axis_index("c")
        me, sibling = (x, y, c), (x, y, 1 - c)
        chips = [(1 - x, y), (x, 1 - y), (1 - x, 1 - y)]

        def slot(px, py, pc):
            return out_ref.at[4 * px + 2 * py + pc]

        def copy(k, block, to, src=None):
            return pltpu.make_async_remote_copy(
                src_ref=slot(*block) if src is None else src, dst_ref=slot(*block),
                send_sem=send_sems.at[k], recv_sem=recv_sems.at[k], device_id=to, device_id_type=MESH)

        mine = pltpu.make_async_copy(x_ref, slot(*me), local_sem)
        mine.start()
        first = [copy(0, me, sibling, src=x_ref)]
        first += [copy(1 + j, me, (*chip, c), src=x_ref) for j, chip in enumerate(chips)]
        for cp in first:
            cp.start()
        passed = [copy(4 + j, (*chip, c), sibling) for j, chip in enumerate(chips)]
        for j, chip in enumerate(chips):
            copy(1 + j, (*chip, c), me).wait_recv()
            passed[j].start()
        copy(0, sibling, me).wait_recv()
        for j, chip in enumerate(chips):
            copy(4 + j, (*chip, 1 - c), me).wait_recv()
        for cp in first + passed:
            cp.wait_send()
        mine.wait()

    return pl.pallas_call(
        body, name=name,
        out_shape=jax.ShapeDtypeStruct((N_DEV, rows, cols), v.dtype),
        in_specs=[pl.BlockSpec(memory_space=pltpu.VMEM)],
        out_specs=pl.BlockSpec(memory_space=pltpu.VMEM),
        scratch_shapes=[pltpu.SemaphoreType.DMA((7,)), pltpu.SemaphoreType.DMA((7,)), pltpu.SemaphoreType.DMA],
    )(v)


def _hbm_specs(n):
    return [pl.BlockSpec(memory_space=pl.ANY)] * n


def _gather_weights(shards):
    n = len(shards)

    def body(*refs):
        ins, outs, (send_sems, recv_sems) = refs[:n], refs[n:2 * n], refs[2 * n:]
        x, y, c = lax.axis_index("x"), lax.axis_index("y"), lax.axis_index("c")
        sibling = (x, y, 1 - c)
        chips = [(1 - x, y), (x, 1 - y), (1 - x, 1 - y)]

        def blk(w, px, py, half):
            hr = ins[w].shape[0] // 2
            return outs[w].at[2 * px + py, pl.ds(half * hr, hr), :]

        def copy(w, k, block, to, src=None):
            return pltpu.make_async_remote_copy(
                src_ref=blk(w, *block) if src is None else src, dst_ref=blk(w, *block),
                send_sem=send_sems.at[6 * w + k], recv_sem=recv_sems.at[6 * w + k], device_id=to, device_id_type=MESH)

        first = []
        for w in range(n):
            hr = ins[w].shape[0] // 2
            my_half = ins[w].at[pl.ds(c * hr, hr), :]
            first += [copy(w, j, (x, y, c), (*chip, c), src=my_half) for j, chip in enumerate(chips)]
        for cp in first:
            cp.start()
        passed = []
        for j, chip in enumerate(chips):
            for w in range(n):
                copy(w, j, (*chip, c), (x, y, c)).wait_recv()
                passed.append(copy(w, 3 + j, (*chip, c), sibling))
                passed[-1].start()
        for j, chip in enumerate(chips):
            for w in range(n):
                copy(w, 3 + j, (*chip, 1 - c), (x, y, c)).wait_recv()
        for cp in first + passed:
            cp.wait_send()

    return pl.pallas_call(
        body, name="gather_weights",
        out_shape=[jax.ShapeDtypeStruct((N_CHIPS,) + s.shape, s.dtype) for s in shards],
        in_specs=_hbm_specs(n), out_specs=_hbm_specs(n),
        scratch_shapes=[pltpu.SemaphoreType.DMA((6 * n,)), pltpu.SemaphoreType.DMA((6 * n,))],
    )(*shards)


def _swap_halves(grads, name):
    n = len(grads)

    def body(*refs):
        ins, outs, (send_sems, recv_sems) = refs[:n], refs[n:2 * n], refs[2 * n:]
        x, y, c = lax.axis_index("x"), lax.axis_index("y"), lax.axis_index("c")
        cps = []
        for w in range(n):
            hr = ins[w].shape[1] // 2
            cps.append(pltpu.make_async_remote_copy(
                src_ref=ins[w].at[:, pl.ds((1 - c) * hr, hr), :], dst_ref=outs[w],
                send_sem=send_sems.at[w], recv_sem=recv_sems.at[w], device_id=(x, y, 1 - c), device_id_type=MESH))
        for cp in cps:
            cp.start()
        for cp in cps:
            cp.wait()

    return pl.pallas_call(
        body, name=name,
        out_shape=[jax.ShapeDtypeStruct((N_CHIPS, g.shape[1] // 2, g.shape[2]), g.dtype) for g in grads],
        in_specs=_hbm_specs(n), out_specs=_hbm_specs(n),
        scratch_shapes=[pltpu.SemaphoreType.DMA((n,)), pltpu.SemaphoreType.DMA((n,))],
    )(*grads)


def _join_halves(halves, token, name):
    n = len(halves)

    def body(*refs):
        ins, outs, (send_sems, recv_sems) = refs[:n], refs[n + 1:2 * n + 1], refs[2 * n + 1:]
        x, y, c = lax.axis_index("x"), lax.axis_index("y"), lax.axis_index("c")
        cps = [pltpu.make_async_remote_copy(
            src_ref=ins[w], dst_ref=outs[w], send_sem=send_sems.at[w], recv_sem=recv_sems.at[w],
            device_id=(x, y, 1 - c), device_id_type=MESH) for w in range(n)]
        for cp in cps:
            cp.start()
        for cp in cps:
            cp.wait()

    return pl.pallas_call(
        body, name=name,
        out_shape=[jax.ShapeDtypeStruct(h.shape, h.dtype) for h in halves],
        in_specs=_hbm_specs(n + 1), out_specs=_hbm_specs(n),
        scratch_shapes=[pltpu.SemaphoreType.DMA((n,)), pltpu.SemaphoreType.DMA((n,))],
    )(*halves, token)


def _in_hbm(v):
    return pltpu.with_memory_space_constraint(v, pltpu.HBM)


_SPLIT_COPY = pltpu.CompilerParams(has_side_effects=pltpu.SideEffectType.DATAFLOW_SIDE_EFFECTING)


def _gather_copies(srcs, lands, send_sems, recv_sems):
    x, y, c = lax.axis_index("x"), lax.axis_index("y"), lax.axis_index("c")
    cps = []
    for w, (src, land) in enumerate(zip(srcs, lands)):
        hr = src.shape[0] // 2
        for j, chip in enumerate([(1 - x, y), (x, 1 - y), (1 - x, 1 - y)]):
            cps.append(pltpu.make_async_remote_copy(
                src_ref=src.at[pl.ds(c * hr, hr), :], dst_ref=land.at[2 * x + y, pl.ds(c * hr, hr), :],
                send_sem=send_sems.at[3 * w + j], recv_sem=recv_sems.at[3 * w + j],
                device_id=(*chip, c), device_id_type=MESH))
    return cps


def _scatter_copies(srcs, lands, send_sems, recv_sems):
    x, y, c = lax.axis_index("x"), lax.axis_index("y"), lax.axis_index("c")
    cps = []
    for w, (src, land) in enumerate(zip(srcs, lands)):
        for j, chip in enumerate([(1 - x, y), (x, 1 - y), (1 - x, 1 - y)]):
            cps.append(pltpu.make_async_remote_copy(
                src_ref=src.at[2 * chip[0] + chip[1]], dst_ref=land.at[j],
                send_sem=send_sems.at[3 * w + j], recv_sem=recv_sems.at[3 * w + j],
                device_id=(*chip, c), device_id_type=MESH))
    return cps


def _split_start(copies, srcs, lands, name):
    n = len(srcs)

    def body(*refs):
        src, lnd, send_sems, recv_sems, token = refs[:n], refs[n:2 * n], refs[2 * n], refs[2 * n + 1], refs[-1]
        for cp in copies(src, lnd, send_sems, recv_sems):
            cp.start()
        token[...] = jnp.zeros_like(token)

    hbm = pl.BlockSpec(memory_space=pltpu.HBM)
    sem = pl.BlockSpec(memory_space=pltpu.SEMAPHORE)
    outs = pl.pallas_call(
        body, name=name,
        out_shape=(pltpu.SemaphoreType.DMA((3 * n,)), pltpu.SemaphoreType.DMA((3 * n,)),
                   *[pltpu.HBM(v.shape, v.dtype) for v in srcs + lands], jax.ShapeDtypeStruct((8, 128), F32)),
        in_specs=[hbm] * (2 * n),
        out_specs=(sem, sem, *([hbm] * (2 * n)), pl.BlockSpec(memory_space=pltpu.VMEM)),
        input_output_aliases={i: 2 + i for i in range(2 * n)},
        compiler_params=_SPLIT_COPY,
    )(*[_in_hbm(v) for v in srcs + lands])
    return outs[0], outs[1], list(outs[2:2 + n]), list(outs[2 + n:2 + 2 * n]), outs[-1]


def _split_wait(copies, send_sems, recv_sems, srcs, lands, after, name):
    n = len(srcs)

    def body(*refs):
        src, lnd, send_sems, recv_sems = refs[:n], refs[n:2 * n], refs[2 * n], refs[2 * n + 1]
        for cp in copies(src, lnd, send_sems, recv_sems):
            cp.wait_send()
            cp.wait_recv()

    hbm = pl.BlockSpec(memory_space=pltpu.HBM)
    sem = pl.BlockSpec(memory_space=pltpu.SEMAPHORE)
    outs = pl.pallas_call(
        body, name=name,
        out_shape=tuple(pltpu.HBM(v.shape, v.dtype) for v in srcs + lands),
        in_specs=[hbm] * (2 * n) + [sem, sem, pl.BlockSpec(memory_space=pl.ANY)],
        out_specs=tuple([hbm] * (2 * n)),
        input_output_aliases={i: i for i in range(2 * n)},
        compiler_params=_SPLIT_COPY,
    )(*srcs, *lands, send_sems, recv_sems, after)
    return list(outs[n:])


def _pass_to_sibling(lands):
    n = len(lands)

    def body(*refs):
        ins, outs, (send_sems, recv_sems) = refs[:n], refs[n:2 * n], refs[2 * n:]
        x, y, c = lax.axis_index("x"), lax.axis_index("y"), lax.axis_index("c")
        cps = []
        for w in range(n):
            hr = ins[w].shape[1] // 2
            for j, chip in enumerate([(1 - x, y), (x, 1 - y), (1 - x, 1 - y)]):
                k = 2 * chip[0] + chip[1]
                cps.append(pltpu.make_async_remote_copy(
                    src_ref=ins[w].at[k, pl.ds(c * hr, hr), :], dst_ref=outs[w].at[k, pl.ds(c * hr, hr), :],
                    send_sem=send_sems.at[3 * w + j], recv_sem=recv_sems.at[3 * w + j],
                    device_id=(x, y, 1 - c), device_id_type=MESH))
        for cp in cps:
            cp.start()
        for cp in cps:
            cp.wait()

    return pl.pallas_call(
        body, name="gather_late_pass",
        out_shape=[jax.ShapeDtypeStruct(v.shape, v.dtype) for v in lands],
        in_specs=_hbm_specs(n), out_specs=_hbm_specs(n),
        input_output_aliases={i: i for i in range(n)},
        scratch_shapes=[pltpu.SemaphoreType.DMA((3 * n,)), pltpu.SemaphoreType.DMA((3 * n,))],
    )(*lands)


def _row_tile(rows):
    for cand in (256, 176, 128, 64, 32, 16):
        if rows % cand == 0:
            return cand
    raise ValueError(rows)


def _add_my_half(g, other, c_idx, name):
    _, k, n = g.shape
    hr = k // 2
    tr = _row_tile(hr)
    nb = hr // tr

    def body(c_ref, g_ref, o_ref, out_ref, out16_ref):
        s = g_ref[...] + o_ref[...]
        out_ref[...] = s
        out16_ref[...] = s.astype(BF16)

    return pl.pallas_call(
        body, name=name,
        grid_spec=pltpu.PrefetchScalarGridSpec(
            num_scalar_prefetch=1, grid=(N_CHIPS, nb),
            in_specs=[pl.BlockSpec((1, tr, n), lambda j, i, c: (j, c[0] * nb + i, 0)),
                      pl.BlockSpec((1, tr, n), lambda j, i, c: (j, i, 0))],
            out_specs=[pl.BlockSpec((1, tr, n), lambda j, i, c: (j, i, 0)),
                       pl.BlockSpec((1, tr, n), lambda j, i, c: (j, i, 0))]),
        out_shape=[jax.ShapeDtypeStruct((N_CHIPS, hr, n), F32), jax.ShapeDtypeStruct((N_CHIPS, hr, n), BF16)],
        compiler_params=_params(("parallel", "parallel")),
    )(c_idx, g, other)


def _add_chips(red, recv, chip_idx, name):
    _, hr, n = red.shape
    tr = _row_tile(hr)

    def body(k_ref, r_ref, v_ref, out_ref):
        out_ref[...] = ((r_ref[0] + v_ref[0].astype(F32)) + v_ref[1].astype(F32)) + v_ref[2].astype(F32)

    return pl.pallas_call(
        body, name=name,
        grid_spec=pltpu.PrefetchScalarGridSpec(
            num_scalar_prefetch=1, grid=(hr // tr,),
            in_specs=[pl.BlockSpec((1, tr, n), lambda i, k: (k[0], i, 0)),
                      pl.BlockSpec((3, tr, n), lambda i, k: (0, i, 0))],
            out_specs=pl.BlockSpec((tr, n), lambda i, k: (i, 0))),
        out_shape=jax.ShapeDtypeStruct((hr, n), F32),
        compiler_params=_params(("parallel",)),
    )(chip_idx, red, recv)


def _mod_shard(c_all, w_ada, b_ada):
    nb, cols = c_all.shape[0], w_ada.shape[1]

    def body(c_ref, w_ref, b_ref, o_ref):
        c = c_ref[...]
        o_ref[...] = _dot(c * jax.nn.sigmoid(c), w_ref[...]) + b_ref[...]

    return pl.pallas_call(
        body, name="mod_shard", out_shape=jax.ShapeDtypeStruct((nb, cols), F32),
        compiler_params=_params(vmem_mb=48),
    )(c_all, w_ada, b_ada)


def _proj(x2, mod8, w, seq, out_dtype, name):
    t = x2.shape[0]
    n = w.shape[1]
    tm, tn = min(2048, seq), min(1152, n)
    tpb = seq // tm

    def body(x_ref, mod_ref, w_ref, o_ref, h_ref):
        @pl.when(pl.program_id(1) == 0)
        def _():
            h_ref[...] = (x_ref[...] * (1.0 + mod_ref[0, 1:2, :]) + mod_ref[0, 0:1, :]).astype(BF16)
        o_ref[...] = jnp.dot(h_ref[...], w_ref[...], preferred_element_type=F32).astype(o_ref.dtype)

    return pl.pallas_call(
        body, name=name, grid=(t // tm, n // tn),
        in_specs=[pl.BlockSpec((tm, D), lambda i, j: (i, 0)),
                  pl.BlockSpec((1, 8, D), lambda i, j: (i // tpb, 0, 0)),
                  pl.BlockSpec((D, tn), lambda i, j: (0, j))],
        out_specs=[pl.BlockSpec((tm, tn), lambda i, j: (i, j)), pl.BlockSpec((tm, D), lambda i, j: (i, 0))],
        out_shape=[jax.ShapeDtypeStruct((t, n), out_dtype), jax.ShapeDtypeStruct((t, D), BF16)],
        compiler_params=_params(("parallel", "arbitrary"), 56),
    )(x2, mod8, w)


def _rows_matmul(a, w, name):
    t, k = a.shape
    n = w.shape[1]
    tm = 1024 if t % 1024 == 0 else t

    def body(a_ref, w_ref, o_ref):
        o_ref[...] = jnp.dot(a_ref[...], w_ref[...], preferred_element_type=F32)

    return pl.pallas_call(
        body, name=name, grid=(t // tm,),
        in_specs=[pl.BlockSpec((tm, k), lambda i: (i, 0)), pl.BlockSpec((k, n), lambda i: (0, 0))],
        out_specs=pl.BlockSpec((tm, n), lambda i: (i, 0)),
        out_shape=jax.ShapeDtypeStruct((t, n), F32),
        compiler_params=_params(("parallel",)),
    )(a, w)


def _tn_matmul(a, b, name, seq, split=None):
    a_st, b_st = a.ndim == 3, b.ndim == 3
    t, ka = a.shape[-2:]
    n = b.shape[-1]
    tt = min(1024, seq)
    nt = t // tt
    if a_st or b_st:
        steps, tn = (a.shape[0] if a_st else b.shape[0]), n
    else:
        tn = split
        if tn is None:
            tn = next(cand for cand in (1920, 1024, 1408, 512, n) if n % cand == 0)
        steps = n // tn
    stacked_out = a_st or b_st or split is not None

    def body(a_ref, b_ref, o_ref):
        part = _dot_tn(a_ref[0] if a_st else a_ref[...], b_ref[0] if b_st else b_ref[...])
        if stacked_out:
            part = part[None]

        @pl.when(pl.program_id(1) == 0)
        def _():
            o_ref[...] = part

        @pl.when(pl.program_id(1) > 0)
        def _():
            o_ref[...] += part

    if a_st:
        in_specs = [pl.BlockSpec((1, tt, ka), lambda j, k: (j, k, 0))]
    else:
        in_specs = [pl.BlockSpec((tt, ka), lambda j, k: (k, 0))]
    if b_st:
        in_specs.append(pl.BlockSpec((1, tt, n), lambda j, k: (j, k, 0)))
    else:
        in_specs.append(pl.BlockSpec((tt, tn), lambda j, k: (k, 0 if a_st else j)))
    if stacked_out:
        out_spec = pl.BlockSpec((1, ka, tn), lambda j, k: (j, 0, 0))
        out_shape = jax.ShapeDtypeStruct((steps, ka, tn), F32)
    else:
        out_spec = pl.BlockSpec((ka, tn), lambda j, k: (0, j))
        out_shape = jax.ShapeDtypeStruct((ka, n), F32)
    return pl.pallas_call(
        body, name=name, grid=(steps, nt), in_specs=in_specs, out_specs=out_spec, out_shape=out_shape,
        compiler_params=_params(("parallel", "arbitrary"), 56),
    )(a, b)


def _dh_kernel(dproj, w_p, x2, dxp, mod8, seq):
    t = x2.shape[0]
    tm, tk = min(1024, seq), 1920
    tpb = seq // tm
    nk = NP // tk
    nbatch = t // seq

    def body(dp_ref, w_ref, x_ref, dxp_ref, mod_ref, gx_ref, dm_ref, acc):
        i, k = pl.program_id(0), pl.program_id(1)

        @pl.when(k == 0)
        def _():
            acc[...] = jnp.zeros_like(acc)

        acc[...] += _dot_nt(dp_ref[...], w_ref[...])

        @pl.when(k == nk - 1)
        def _():
            dh = acc[...]
            gx_ref[...] = dxp_ref[...].astype(F32) + dh * (1.0 + mod_ref[0, 1:2, :])
            upd = jnp.concatenate(
                [jnp.sum(dh, axis=0, keepdims=True), jnp.sum(dh * x_ref[...], axis=0, keepdims=True),
                 jnp.zeros((6, D), F32)], axis=0)

            @pl.when(i % tpb == 0)
            def _():
                dm_ref[0] = upd

            @pl.when(i % tpb != 0)
            def _():
                dm_ref[0] += upd

    return pl.pallas_call(
        body, name="dh", grid=(t // tm, nk),
        in_specs=[pl.BlockSpec((tm, tk), lambda i, k: (i, k)),
                  pl.BlockSpec((D, tk), lambda i, k: (0, k)),
                  pl.BlockSpec((tm, D), lambda i, k: (i, 0)),
                  pl.BlockSpec((tm, D), lambda i, k: (i, 0)),
                  pl.BlockSpec((1, 8, D), lambda i, k: (i // tpb, 0, 0))],
        out_specs=[pl.BlockSpec((tm, D), lambda i, k: (i, 0)),
                   pl.BlockSpec((1, 8, D), lambda i, k: (i // tpb, 0, 0))],
        out_shape=[jax.ShapeDtypeStruct((t, D), F32), jax.ShapeDtypeStruct((nbatch, 8, D), F32)],
        scratch_shapes=[pltpu.VMEM((tm, D), F32)],
        compiler_params=_params(("arbitrary", "arbitrary"), 56),
    )(dproj, w_p, x2, dxp, mod8)


def _tri(n, upper):
    r = lax.broadcasted_iota(jnp.int32, (n, n), 0)
    c = lax.broadcasted_iota(jnp.int32, (n, n), 1)
    return jnp.where((c >= r) if upper else (c <= r), 1.0, 0.0).astype(F32)


@jax.custom_vjp
def _mm_nn(a, b):
    return _dot(a, b)


_mm_nn.defvjp(lambda a, b: (_dot(a, b), (a, b)),
              lambda res, g: (_dot_nt(g, res[1]), _dot_tn(res[0], g)))


@jax.custom_vjp
def _mm_nt(a, b):
    return _dot_nt(a, b)


_mm_nt.defvjp(lambda a, b: (_dot_nt(a, b), (a, b)),
              lambda res, g: (_dot(g, res[1]), _dot_tn(g, res[0])))


@jax.custom_vjp
def _mm_tn(a, b):
    return _dot_tn(a, b)


_mm_tn.defvjp(lambda a, b: (_dot_tn(a, b), (a, b)),
              lambda res, g: (_dot_nt(res[1], g), _dot(res[0], g)))


@jax.custom_vjp
def _cumsum_rows(x):
    return _dot_f32(_tri(x.shape[0], False), x)


_cumsum_rows.defvjp(lambda x: (_cumsum_rows(x), None),
                    lambda _, g: (_dot_f32(_tri(g.shape[0], True), g),))


@functools.partial(jax.custom_vjp, nondiff_argnums=(1,))
def _shift_rows(x, k):
    return pltpu.roll(x, k % x.shape[0], 0)


_shift_rows.defvjp(lambda x, k: (_shift_rows(x, k), None),
                   lambda k, _, g: (pltpu.roll(g, (-k) % g.shape[0], 0),))


def _group_ref(bc, m):
    n = bc.shape[0] // (2 * m)
    b3 = bc.reshape(n, 2 * m, ADH)
    row = lax.broadcasted_iota(jnp.int32, b3.shape, 1)
    ref = jnp.sum(jnp.where(row == m - 1, b3, 0.0), axis=1, keepdims=True)
    return jnp.broadcast_to(ref, b3.shape).reshape(bc.shape)


def _hgrn_block(q, fl, v, g, st, lb, nw):
    n = q.shape[0]
    f = lb + (1.0 - lb) * jax.nn.sigmoid(fl)
    kk = 1.0 - f
    lf = jnp.log(f)
    bc = _cumsum_rows(lf)
    row = lax.broadcasted_iota(jnp.int32, (n, ADH), 0)
    same = jnp.bitwise_xor(lax.broadcasted_iota(jnp.int32, (n, n), 0), lax.broadcasted_iota(jnp.int32, (n, n), 1))
    a = jnp.zeros((n, n), F32)
    m = 1
    while m < n:
        r = jnp.bitwise_and(row, 2 * m - 1)
        up, lo = r >= m, r < m
        if m == 1:
            aq, ak = lf, jnp.zeros_like(lf)
        elif m == 2:
            aq = jnp.where(r == 3, lf + _shift_rows(lf, 1), lf)
            ak = jnp.where(r == 0, _shift_rows(lf, -1), 0.0)
        else:
            ref = _group_ref(bc, m)
            aq, ak = bc - ref, ref - bc
        qt = jnp.where(up, q * jnp.exp(jnp.where(up, aq, 0.0)), 0.0)
        kt = jnp.where(lo, kk * jnp.exp(jnp.where(lo, ak, 0.0)), 0.0)
        a = a + jnp.where(same < 2 * m, _mm_nt(qt, kt), 0.0)
        m *= 2
    last = row == n - 1
    bl = jnp.sum(jnp.where(last, bc, 0.0), axis=0, keepdims=True)
    o = _mm_nn(a, v) + _mm_nt(q * jnp.exp(bc), st) + jnp.sum(q * kk, axis=-1, keepdims=True) * v
    st_new = st * jnp.exp(bl) + _mm_tn(v, kk * jnp.exp(bl - bc))
    rms = lax.rsqrt(jnp.mean(o * o, axis=-1, keepdims=True) + RMS_EPS)
    return o * rms * nw * jax.nn.sigmoid(g), st_new


def _hgrn_fwd(proj, lb_logits, norm_w, nbatch, seq):
    t = proj.shape[0]
    blk = min(HGRN_BLOCK, seq)
    nb = seq // blk

    nh = HGRN_HEADS
    wp, wy = 512 * nh, ADH * nh

    def body(p_ref, lbl_ref, nw_ref, y_ref, ck_ref, st_s):
        @pl.when(pl.program_id(2) == 0)
        def _():
            st_s[...] = jnp.zeros_like(st_s)

        st = [st_s[h] for h in range(nh)]
        p = p_ref[...].astype(F32)
        lb = jax.nn.sigmoid(lbl_ref[0:1, :] - lbl_ref[1:2, :])
        nw = nw_ref[...]
        res = [_hgrn_block(*(p[:, 512 * h + 128 * k:512 * h + 128 * k + 128] for k in range(4)), st[h],
                           lb[:, 128 * h:128 * h + 128], nw[:, 128 * h:128 * h + 128]) for h in range(nh)]
        for h in range(nh):
            ck_ref[0, h] = st[h]
            st_s[h] = res[h][1]
        y_ref[...] = jnp.concatenate([r[0] for r in res], axis=1).astype(y_ref.dtype)

    return pl.pallas_call(
        body, name="hgrn_fwd", grid=(AH // nh, nbatch, nb),
        in_specs=[pl.BlockSpec((blk, wp), lambda h, b, i: (b * nb + i, COL_A // wp + h)),
                  pl.BlockSpec((2, wy), lambda h, b, i: (0, h)),
                  pl.BlockSpec((1, wy), lambda h, b, i: (0, h))],
        out_specs=[pl.BlockSpec((blk, wy), lambda h, b, i: (b * nb + i, h)),
                   pl.BlockSpec((1, nh, 128, 128), lambda h, b, i: ((h * nbatch + b) * nb + i, 0, 0, 0))],
        out_shape=[jax.ShapeDtypeStruct((t, AW), BF16),
                   jax.ShapeDtypeStruct((AH // nh * nbatch * nb, nh, 128, 128), F32)],
        scratch_shapes=[pltpu.VMEM((nh, 128, 128), F32)],
        compiler_params=_params(("parallel", "parallel", "arbitrary"), 48),
    )(proj, lb_logits, norm_w)


def _hgrn_bwd(proj, dya, ckpt, lb_logits, norm_w, dproj, nbatch, seq):
    t = proj.shape[0]
    blk = min(HGRN_BLOCK, seq)
    nb = seq // blk

    nh = HGRN_HEADS
    wp, wy = 512 * nh, ADH * nh

    def body(p_ref, dy_ref, ck_ref, lbl_ref, nw_ref, dp_in, dp_ref, sm_ref, dst_s):
        del dp_in
        b_id, i = pl.program_id(1), pl.program_id(2)

        @pl.when(i == 0)
        def _():
            dst_s[...] = jnp.zeros_like(dst_s)

        dst = [dst_s[h] for h in range(nh)]
        st = [ck_ref[0, h] for h in range(nh)]
        p = p_ref[...].astype(F32)
        dy = dy_ref[...]
        lb = jax.nn.sigmoid(lbl_ref[0:1, :] - lbl_ref[1:2, :])
        nw = nw_ref[...]
        grads = []
        for h in range(nh):
            _, pullback = jax.vjp(_hgrn_block, *(p[:, 512 * h + 128 * k:512 * h + 128 * k + 128] for k in range(4)),
                                  st[h], lb[:, 128 * h:128 * h + 128], nw[:, 128 * h:128 * h + 128])
            grads.append(pullback((dy[:, 128 * h:128 * h + 128], dst[h])))
        for h in range(nh):
            dst_s[h] = grads[h][4]
        dp_ref[...] = jnp.concatenate([g[k] for g in grads for k in range(4)], axis=1).astype(dp_ref.dtype)
        upd = jnp.concatenate([jnp.concatenate([g[5] for g in grads], axis=1),
                               jnp.concatenate([g[6] for g in grads], axis=1), jnp.zeros((6, wy), F32)], axis=0)
        first = (b_id == 0) & (i == 0)

        @pl.when(first)
        def _():
            sm_ref[...] = upd

        @pl.when(jnp.logical_not(first))
        def _():
            sm_ref[...] += upd

    def rows(h, b, i):
        return b * nb + (nb - 1 - i)

    return pl.pallas_call(
        body, name="hgrn_bwd", grid=(AH // nh, nbatch, nb),
        in_specs=[pl.BlockSpec((blk, wp), lambda h, b, i: (rows(h, b, i), COL_A // wp + h)),
                  pl.BlockSpec((blk, wy), lambda h, b, i: (rows(h, b, i), h)),
                  pl.BlockSpec((1, nh, 128, 128), lambda h, b, i: ((h * nbatch + b) * nb + (nb - 1 - i), 0, 0, 0)),
                  pl.BlockSpec((2, wy), lambda h, b, i: (0, h)),
                  pl.BlockSpec((1, wy), lambda h, b, i: (0, h)),
                  pl.BlockSpec(memory_space=pl.ANY)],
        out_specs=[pl.BlockSpec((blk, wp), lambda h, b, i: (rows(h, b, i), COL_A // wp + h)),
                   pl.BlockSpec((8, wy), lambda h, b, i: (0, h))],
        out_shape=[jax.ShapeDtypeStruct((t, NP), BF16), jax.ShapeDtypeStruct((8, AW), F32)],
        input_output_aliases={5: 0},
        scratch_shapes=[pltpu.VMEM((nh, 128, 128), F32)],
        compiler_params=_params(("parallel", "arbitrary", "arbitrary"), 56),
    )(proj, dya, ckpt, lb_logits, norm_w, dproj)


def _log_sigmoid(z):
    return jnp.minimum(z, 0.0) - jnp.log(1.0 + jnp.exp(-jnp.abs(z)))


def _fox_cum(proj, bias128, nbatch, seq):
    t = proj.shape[0]
    ts = min(512, seq)
    nb = seq // ts

    def body(p_ref, b_ref, c_ref, carry):
        @pl.when(pl.program_id(1) == 0)
        def _():
            carry[...] = jnp.zeros_like(carry)
        cum = _dot_f32(_tri(ts, False), _log_sigmoid(p_ref[...] + b_ref[...])) + carry[...]
        carry[...] = cum[ts - 1:ts, :]
        cum2 = cum * LOG2E
        lane = lax.broadcasted_iota(jnp.int32, (ts, 128), 1)
        for p in range(4):
            c_ref[p] = jnp.where(lane < 64, cum2[:, 2 * p:2 * p + 1], cum2[:, 2 * p + 1:2 * p + 2])

    return pl.pallas_call(
        body, name="fox_cum", grid=(nbatch, nb),
        in_specs=[pl.BlockSpec((ts, 128), lambda b, i: (b * nb + i, 0)),
                  pl.BlockSpec((1, 128), lambda b, i: (0, 0))],
        out_specs=pl.BlockSpec((4, ts, 128), lambda b, i: (0, b * nb + i, 0)),
        out_shape=jax.ShapeDtypeStruct((4, t, 128), F32),
        scratch_shapes=[pltpu.VMEM((1, 128), F32)],
        compiler_params=_params(("parallel", "arbitrary")),
    )(proj, bias128)


def _fox_scores_t(q128, k128, cc128, hh, masked):
    tq, tk = q128.shape[0], k128.shape[0]
    qh = _head_lanes((q128 * (LOG2E * BDH ** -0.5)).astype(BF16), hh)
    s = _dot_nt(k128, qh) - cc128[:, 64 * hh:64 * hh + 1]
    if masked:
        key = lax.broadcasted_iota(jnp.int32, (tk, tq), 0)
        qry = lax.broadcasted_iota(jnp.int32, (tk, tq), 1)
        s = jnp.where(key <= qry, s, NEG)
    return s


def _causal_pairs(nq, key_major):
    if key_major:
        pairs = [(i, j) for j in range(nq) for i in range(j, nq)]
    else:
        pairs = [(i, j) for i in range(nq) for j in range(i + 1)]
    return (jnp.asarray([p[0] for p in pairs], jnp.int32), jnp.asarray([p[1] for p in pairs], jnp.int32))


def _head_lanes(x128, hh):
    lane = lax.broadcasted_iota(jnp.int32, x128.shape, 1)
    return jnp.where((lane < 64) if hh == 0 else (lane >= 64), x128, jnp.zeros_like(x128))


def _with_ones_lane(x128, hh):
    lane = lax.broadcasted_iota(jnp.int32, x128.shape, 1)
    one = jnp.ones_like(x128)
    zero = jnp.zeros_like(x128)
    if hh == 0:
        return jnp.where(lane < 64, x128, jnp.where(lane == 64, one, zero))
    return jnp.where(lane >= 64, x128, jnp.where(lane == 0, one, zero))


def _fox_fwd(proj, cum_cols, nbatch, seq):
    t = proj.shape[0]
    tq = tk = min(512, seq)
    nq = seq // tq
    npr = FOX_PAIRS
    qi, kj = _causal_pairs(nq, key_major=False)

    def body(qi_ref, kj_ref, q_ref, kv_ref, cc_ref, o_ref, lse_ref, m_s, acc_s):
        s_id = pl.program_id(2)
        i, j = qi_ref[s_id], kj_ref[s_id]

        @pl.when(j == 0)
        def _():
            m_s[...] = jnp.full_like(m_s, NEG)
            acc_s[...] = jnp.zeros_like(acc_s)

        def step(masked):
            heads = [(pr, hh) for pr in range(npr) for hh in range(2)]
            m_prev = m_s[0:2 * npr, :]
            acc_prev = [acc_s[h] for h in range(2 * npr)]
            q128 = [q_ref[:, 128 * pr:128 * pr + 128] for pr in range(npr)]
            k128 = [kv_ref[:, 256 * pr:256 * pr + 128].astype(BF16) for pr in range(npr)]
            v128 = [kv_ref[:, 256 * pr + 128:256 * pr + 256].astype(BF16) for pr in range(npr)]
            s = [_fox_scores_t(q128[pr], k128[pr], cc_ref[pr], hh, masked) for pr, hh in heads]
            m_new = [jnp.maximum(m_prev[h:h + 1, :], jnp.max(s[h], axis=0, keepdims=True)) for h in range(2 * npr)]
            acc_new = []
            for h, (pr, hh) in enumerate(heads):
                alpha = jnp.exp2(m_prev[h:h + 1, :] - m_new[h])
                p = jnp.exp2(s[h] - m_new[h]).astype(BF16)
                acc_new.append(acc_prev[h] * alpha + _dot_tn(_with_ones_lane(v128[pr], hh), p))
            for h in range(2 * npr):
                acc_s[h] = acc_new[h]
            m_s[0:2 * npr, :] = jnp.concatenate(m_new, axis=0)

        @pl.when(j < i)
        def _():
            step(False)

        @pl.when(j == i)
        def _():
            step(True)
            outs = []
            for pr in range(npr):
                a0, a1 = acc_s[2 * pr], acc_s[2 * pr + 1]
                l0, l1 = a0[64:65, :], a1[0:1, :]
                outs.append(jnp.concatenate([a0[0:64, :] / l0, a1[64:128, :] / l1], axis=0).T)
                lse_ref[0, pr] = jnp.concatenate(
                    [m_s[2 * pr:2 * pr + 1, :] + jnp.log2(l0), m_s[2 * pr + 1:2 * pr + 2, :] + jnp.log2(l1),
                     jnp.zeros((6, tq), F32)], axis=0)
            o_ref[...] = jnp.concatenate(outs, axis=1).astype(o_ref.dtype)

    return pl.pallas_call(
        body, name="fox_fwd",
        grid_spec=pltpu.PrefetchScalarGridSpec(
            num_scalar_prefetch=2, grid=(nbatch, 4 // npr, qi.shape[0]),
            in_specs=[pl.BlockSpec((tq, 128 * npr), lambda b, p, s, qi, kj: (b * nq + qi[s], COL_BQ // (128 * npr) + p)),
                      pl.BlockSpec((tk, 256 * npr), lambda b, p, s, qi, kj: (b * nq + kj[s], COL_KV // (256 * npr) + p)),
                      pl.BlockSpec((npr, tk, 128), lambda b, p, s, qi, kj: (p, b * nq + kj[s], 0))],
            out_specs=[pl.BlockSpec((tq, 128 * npr), lambda b, p, s, qi, kj: (b * nq + qi[s], p)),
                       pl.BlockSpec((1, npr, 8, tq), lambda b, p, s, qi, kj: (b, p, 0, qi[s]))],
            scratch_shapes=[pltpu.VMEM((8, tq), F32), pltpu.VMEM((2 * npr, 128, tq), F32)]),
        out_shape=[jax.ShapeDtypeStruct((t, 512), BF16), jax.ShapeDtypeStruct((nbatch, 4, 8, seq), F32)],
        compiler_params=_params(("parallel", "parallel", "arbitrary"), 56),
    )(qi, kj, proj, proj, cum_cols)


def _fox_bwd(proj, cum_cols, lse, yb, dyb, dproj, nbatch, seq):
    t = proj.shape[0]
    tq = tk = min(512, seq)
    nq = seq // tq
    scale = BDH ** -0.5
    qi, kj = _causal_pairs(nq, key_major=True)
    nsteps = qi.shape[0]

    npr = FOX_PAIRS

    def body(qi_ref, kj_ref, q_ref, kv_ref, cc_ref, lse_ref, o_ref, do_ref, dp_in,
             dkv_ref, dq_ref, drs_ref, dcs_ref, dk_s, dv_s, dqa_s):
        del dp_in
        pg, s_id = pl.program_id(1), pl.program_id(2)
        i, j = qi_ref[s_id], kj_ref[s_id]

        @pl.when(i == j)
        def _():
            dk_s[...] = jnp.zeros_like(dk_s)
            dv_s[...] = jnp.zeros_like(dv_s)

        @pl.when(s_id == 0)
        def _():
            dqa_s[...] = jnp.zeros_like(dqa_s)

        def step(masked):
            dk_prev = [dk_s[h] for h in range(2 * npr)]
            dq_prev = [dqa_s[i, h] for h in range(2 * npr)]
            dv_new = [dv_s[pr] for pr in range(npr)]
            dk_new, dq_new = [], []
            for pr in range(npr):
                lanes = slice(128 * pr, 128 * pr + 128)
                q128 = q_ref[:, lanes]
                qs128 = (q128 * scale).astype(BF16)
                k128 = kv_ref[:, 256 * pr:256 * pr + 128].astype(BF16)
                v128 = kv_ref[:, 256 * pr + 128:256 * pr + 256].astype(BF16)
                do128 = do_ref[:, lanes]
                doo = do128 * o_ref[:, lanes].astype(F32)
                do16 = do128.astype(BF16)
                for hh in range(2):
                    s = _fox_scores_t(q128, k128, cc_ref[pr], hh, masked)
                    p = jnp.exp2(s - lse_ref[0, pr, hh:hh + 1, :])
                    dd = lax.dot_general(jnp.ones((8, 128), F32), _head_lanes(doo, hh), (((1,), (1,)), ((), ())),
                                         preferred_element_type=F32, precision=HIGHEST)[0:1, :]
                    doh = _head_lanes(do16, hh)
                    dp = _dot_nt(v128, doh)
                    ds = (p * (dp - dd)).astype(BF16)
                    dv_new[pr] = dv_new[pr] + _dot(p, doh)
                    dk_new.append(dk_prev[2 * pr + hh] + _dot(ds, _with_ones_lane(qs128, hh)))
                    dq_new.append(dq_prev[2 * pr + hh] + _dot_tn(_with_ones_lane(k128, hh), ds))
            for pr in range(npr):
                dv_s[pr] = dv_new[pr]
            for h in range(2 * npr):
                dk_s[h] = dk_new[h]
                dqa_s[i, h] = dq_new[h]

        @pl.when(i == j)
        def _():
            step(True)

        @pl.when(i > j)
        def _():
            step(False)

        def sums_to_lanes(lane, pr, s0, s1):
            hp = npr * pg + pr
            return jnp.where(lane == 2 * hp, s0, jnp.where(lane == 2 * hp + 1, s1, 0.0))

        @pl.when(i == nq - 1)
        def _():
            lane = lax.broadcasted_iota(jnp.int32, (tk, 128), 1)
            for pr in range(npr):
                k0, k1 = dk_s[2 * pr], dk_s[2 * pr + 1]
                dkv_ref[:, 256 * pr:256 * pr + 128] = jnp.where(lane < 64, k0, k1).astype(dkv_ref.dtype)
                dkv_ref[:, 256 * pr + 128:256 * pr + 256] = dv_s[pr].astype(dkv_ref.dtype)
                dcs_ref[pr] = sums_to_lanes(lane, pr, k0[:, 64:65], k1[:, 0:1])

        @pl.when(s_id == nsteps - 1)
        def _():
            lane = lax.broadcasted_iota(jnp.int32, (tq, 128), 1)
            for blk in range(nq):
                rows = pl.ds(blk * tq, tq)
                for pr in range(npr):
                    a0 = dqa_s[blk, 2 * pr].T
                    a1 = dqa_s[blk, 2 * pr + 1].T
                    dq_ref[rows, 128 * pr:128 * pr + 128] = (jnp.where(lane < 64, a0, a1) * scale).astype(dq_ref.dtype)
                    drs_ref[pr, rows, :] = sums_to_lanes(lane, pr, a0[:, 64:65], a1[:, 0:1])

    return pl.pallas_call(
        body, name="fox_bwd",
        grid_spec=pltpu.PrefetchScalarGridSpec(
            num_scalar_prefetch=2, grid=(nbatch, 4 // npr, nsteps),
            in_specs=[pl.BlockSpec((tq, 128 * npr), lambda b, p, s, qi, kj: (b * nq + qi[s], COL_BQ // (128 * npr) + p)),
                      pl.BlockSpec((tk, 256 * npr), lambda b, p, s, qi, kj: (b * nq + kj[s], COL_KV // (256 * npr) + p)),
                      pl.BlockSpec((npr, tk, 128), lambda b, p, s, qi, kj: (p, b * nq + kj[s], 0)),
                      pl.BlockSpec((1, npr, 8, tq), lambda b, p, s, qi, kj: (b, p, 0, qi[s])),
                      pl.BlockSpec((tq, 128 * npr), lambda b, p, s, qi, kj: (b * nq + qi[s], p)),
                      pl.BlockSpec((tq, 128 * npr), lambda b, p, s, qi, kj: (b * nq + qi[s], p)),
                      pl.BlockSpec(memory_space=pl.ANY)],
            out_specs=[pl.BlockSpec((tk, 256 * npr), lambda b, p, s, qi, kj: (b * nq + kj[s], COL_KV // (256 * npr) + p)),
                       pl.BlockSpec((seq, 128 * npr), lambda b, p, s, qi, kj: (b, p)),
                       pl.BlockSpec((npr, seq, 128), lambda b, p, s, qi, kj: (p, b, 0)),
                       pl.BlockSpec((npr, tk, 128), lambda b, p, s, qi, kj: (p, b * nq + kj[s], 0))],
            scratch_shapes=[pltpu.VMEM((2 * npr, tk, 128), F32), pltpu.VMEM((npr, tk, 128), F32),
                            pltpu.VMEM((nq, 2 * npr, 128, tq), F32)]),
        out_shape=[jax.ShapeDtypeStruct((t, NP), BF16), jax.ShapeDtypeStruct((t, 512), BF16),
                   jax.ShapeDtypeStruct((4, t, 128), F32), jax.ShapeDtypeStruct((4, t, 128), F32)],
        input_output_aliases={8: 0},
        compiler_params=_params(("parallel", "parallel", "arbitrary"), 60),
    )(qi, kj, proj, proj, cum_cols, lse, yb, dyb, dproj)


def _place_cols(dproj, src, col):
    t, w = src.shape
    tm = 1024 if t % 1024 == 0 else t

    def body(s_ref, dp_in, o_ref):
        del dp_in
        o_ref[...] = s_ref[...]

    return pl.pallas_call(
        body, name="place_cols", grid=(t // tm,),
        in_specs=[pl.BlockSpec((tm, w), lambda i: (i, 0)), pl.BlockSpec(memory_space=pl.ANY)],
        out_specs=pl.BlockSpec((tm, w), lambda i: (i, col // w)),
        out_shape=jax.ShapeDtypeStruct(dproj.shape, dproj.dtype),
        input_output_aliases={1: 0},
        compiler_params=_params(("parallel",)),
    )(src, dproj)


def _fox_dbf(proj, bias128, drs, dcs, dproj, nbatch, seq):
    t = proj.shape[0]
    ts = min(512, seq)
    nb = seq // ts

    def body(p_ref, b_ref, dr_ref, dc_ref, dp_in, dp_ref, sm_ref, carry):
        del dp_in
        b_id, i = pl.program_id(0), pl.program_id(1)

        @pl.when(i == 0)
        def _():
            carry[...] = jnp.zeros_like(carry)

        dcum = (dr_ref[0] - dc_ref[0]) + (dr_ref[1] - dc_ref[1]) + (dr_ref[2] - dc_ref[2]) + (dr_ref[3] - dc_ref[3])
        rc = _dot_f32(_tri(ts, True), dcum) + carry[...]
        carry[...] = rc[0:1, :]
        z = p_ref[...] + b_ref[...]
        lane = lax.broadcasted_iota(jnp.int32, (ts, 128), 1)
        dz = jnp.where(lane < BH, rc * jax.nn.sigmoid(-z), 0.0)
        dp_ref[...] = dz.astype(dp_ref.dtype)
        upd = jnp.concatenate([jnp.sum(dz, axis=0, keepdims=True), jnp.zeros((7, 128), F32)], axis=0)
        first = (b_id == 0) & (i == 0)

        @pl.when(first)
        def _():
            sm_ref[...] = upd

        @pl.when(jnp.logical_not(first))
        def _():
            sm_ref[...] += upd

    def rows(b, i):
        return b * nb + (nb - 1 - i)

    return pl.pallas_call(
        body, name="fox_dbf", grid=(nbatch, nb),
        in_specs=[pl.BlockSpec((ts, 128), lambda b, i: (rows(b, i), 0)),
                  pl.BlockSpec((1, 128), lambda b, i: (0, 0)),
                  pl.BlockSpec((4, ts, 128), lambda b, i: (0, rows(b, i), 0)),
                  pl.BlockSpec((4, ts, 128), lambda b, i: (0, rows(b, i), 0)),
                  pl.BlockSpec(memory_space=pl.ANY)],
        out_specs=[pl.BlockSpec((ts, 128), lambda b, i: (rows(b, i), COL_BF // 128)),
                   pl.BlockSpec((8, 128), lambda b, i: (0, 0))],
        out_shape=[jax.ShapeDtypeStruct((t, NP), BF16), jax.ShapeDtypeStruct((8, 128), F32)],
        input_output_aliases={4: 0},
        scratch_shapes=[pltpu.VMEM((1, 128), F32)],
        compiler_params=_params(("arbitrary", "arbitrary")),
    )(proj, bias128, drs, dcs, dproj)


def _ln_stats(z):
    mu = jnp.mean(z, axis=-1, keepdims=True)
    zc = z - mu
    rstd = lax.rsqrt(jnp.mean(zc * zc, axis=-1, keepdims=True) + LN_EPS)
    return zc * rstd, rstd


def _ln_bwd(dy, xhat, rstd, w):
    dxh = dy * w
    return rstd * (dxh - jnp.mean(dxh, axis=-1, keepdims=True) - xhat * jnp.mean(dxh * xhat, axis=-1, keepdims=True))


def _merge_fwd(ya, yb, proj, x2, mod8, wba, wbb, wout, ln1w, ln1b, seq):
    t = x2.shape[0]
    tm = min(512, seq)
    tpb = seq // tm

    def body(ya_ref, yb_ref, g_ref, x_ref, mod_ref, wa_ref, wb_ref, wo_ref, lw_ref, lb_ref, mg_ref, u_ref, x1_ref):
        ga = jax.nn.sigmoid(g_ref[:, 0:D].astype(F32))
        gb = jax.nn.sigmoid(g_ref[:, D:2 * D].astype(F32))
        merged = (ga * jnp.dot(ya_ref[...], wa_ref[...], preferred_element_type=F32)
                  + gb * jnp.dot(yb_ref[...], wb_ref[...], preferred_element_type=F32))
        mg = merged.astype(BF16)
        mg_ref[...] = mg
        u = jnp.dot(mg, wo_ref[...], preferred_element_type=F32)
        u_ref[...] = u.astype(u_ref.dtype)
        xhat, _ = _ln_stats(ALPHA * x_ref[...] + (1.0 + mod_ref[0, 2:3, :]) * u)
        x1_ref[...] = xhat * lw_ref[...] + lb_ref[...]

    tok = lambda w: pl.BlockSpec((tm, w), lambda i: (i, 0))
    full = lambda a: pl.BlockSpec(a.shape, lambda i: (0,) * a.ndim)
    return pl.pallas_call(
        body, name="merge_fwd", grid=(t // tm,),
        in_specs=[tok(512), tok(512), pl.BlockSpec((tm, 2048), lambda i: (i, COL_GATES // 2048)), tok(D),
                  pl.BlockSpec((1, 8, D), lambda i: (i // tpb, 0, 0)),
                  full(wba), full(wbb), full(wout), full(ln1w), full(ln1b)],
        out_specs=[tok(D), tok(D), tok(D)],
        out_shape=[jax.ShapeDtypeStruct((t, D), BF16), jax.ShapeDtypeStruct((t, D), BF16),
                   jax.ShapeDtypeStruct((t, D), F32)],
        compiler_params=_params(("parallel",), 48),
    )(ya, yb, proj, x2, mod8, wba, wbb, wout, ln1w, ln1b)


def _merge_bwd(du, ya, yb, proj, wba, wbb, wout, token, seq):
    t = du.shape[0]
    tm = min(512, seq)

    def body(du_ref, ya_ref, yb_ref, g_ref, wa_ref, wb_ref, wo_ref, token_ref,
             dp_ref, dpa_ref, dpb_ref, dya_ref, dyb_ref):
        del token_ref
        ga = jax.nn.sigmoid(g_ref[:, 0:D].astype(F32))
        gb = jax.nn.sigmoid(g_ref[:, D:2 * D].astype(F32))
        dm = _dot_nt(du_ref[...], wo_ref[...])
        pa = jnp.dot(ya_ref[...], wa_ref[...], preferred_element_type=F32)
        pb = jnp.dot(yb_ref[...], wb_ref[...], preferred_element_type=F32)
        dpa = (dm * ga).astype(BF16)
        dpb = (dm * gb).astype(BF16)
        dpa_ref[...] = dpa
        dpb_ref[...] = dpb
        dp_ref[:, 0:D] = (dm * pa * ga * (1.0 - ga)).astype(BF16)
        dp_ref[:, D:2 * D] = (dm * pb * gb * (1.0 - gb)).astype(BF16)
        dya_ref[...] = _dot_nt(dpa, wa_ref[...])
        dyb_ref[...] = _dot_nt(dpb, wb_ref[...])

    tok = lambda w: pl.BlockSpec((tm, w), lambda i: (i, 0))
    full = lambda a: pl.BlockSpec(a.shape, lambda i: (0,) * a.ndim)
    return pl.pallas_call(
        body, name="merge_bwd", grid=(t // tm,),
        in_specs=[tok(D), tok(512), tok(512), pl.BlockSpec((tm, 2048), lambda i: (i, COL_GATES // 2048)),
                  full(wba), full(wbb), full(wout), full(token)],
        out_specs=[pl.BlockSpec((tm, 2048), lambda i: (i, COL_GATES // 2048)), tok(D), tok(D), tok(512), tok(512)],
        out_shape=[jax.ShapeDtypeStruct((t, NP), BF16), jax.ShapeDtypeStruct((t, D), BF16),
                   jax.ShapeDtypeStruct((t, D), BF16), jax.ShapeDtypeStruct((t, 512), F32),
                   jax.ShapeDtypeStruct((t, 512), F32)],
        compiler_params=_params(("parallel",), 48),
    )(du, ya, yb, proj, wba, wbb, wout, token)


def _ffn_fwd(x1, mod8, wg, wu, wd, target, ln2w, ln2b, seq):
    t = x1.shape[0]
    tm = min(FFN_TOKENS, seq)
    nf, tf, _ = wg.shape
    tpb = seq // tm
    nbatch = t // seq

    def body(x_ref, mod_ref, wg_ref, wu_ref, wd_ref, t_ref, lw_ref, lb_ref,
             a_ref, b_ref, h_s, dz_ref, st_ref, dm_ref, acc):
        i, j = pl.program_id(0), pl.program_id(1)

        @pl.when(j == 0)
        def _():
            h_s[...] = (x_ref[...] * (1.0 + mod_ref[0, 4:5, :]) + mod_ref[0, 3:4, :]).astype(BF16)
            acc[...] = jnp.zeros_like(acc)

        a = _dot_nt(h_s[...], wg_ref[0])
        b = _dot_nt(h_s[...], wu_ref[0])
        a_ref[0] = a.astype(BF16)
        b_ref[0] = b.astype(BF16)
        acc[...] += _dot(a * jax.nn.sigmoid(a) * b, wd_ref[0])

        @pl.when(j == nf - 1)
        def _():
            ffn = acc[...]
            xhat, rstd = _ln_stats(ALPHA * x_ref[...] + (1.0 + mod_ref[0, 5:6, :]) * ffn)
            diff = xhat * lw_ref[...] + lb_ref[...] - t_ref[...]
            loss = 0.5 * jnp.sum(jnp.sum(diff * diff, axis=-1, keepdims=True), axis=0, keepdims=True) / D
            dy = diff * (1.0 / D)
            dz = _ln_bwd(dy, xhat, rstd, lw_ref[...])
            dz_ref[...] = dz
            lane = lax.broadcasted_iota(jnp.int32, (1, D), 1)
            upd = jnp.concatenate(
                [jnp.sum(dy * xhat, axis=0, keepdims=True), jnp.sum(dy, axis=0, keepdims=True),
                 jnp.where(lane == 0, loss, 0.0), jnp.zeros((5, D), F32)], axis=0)
            dmu = jnp.concatenate(
                [jnp.zeros((5, D), F32), jnp.sum(dz * ffn, axis=0, keepdims=True), jnp.zeros((2, D), F32)], axis=0)

            @pl.when(i == 0)
            def _():
                st_ref[...] = upd

            @pl.when(i > 0)
            def _():
                st_ref[...] += upd

            @pl.when(i % tpb == 0)
            def _():
                dm_ref[0] = dmu

            @pl.when(i % tpb != 0)
            def _():
                dm_ref[0] += dmu

    row = lambda: pl.BlockSpec((tm, D), lambda i, j: (i, 0))
    vec = lambda: pl.BlockSpec((1, D), lambda i, j: (0, 0))
    return pl.pallas_call(
        body, name="ffn_fwd", grid=(t // tm, nf),
        in_specs=[row(), pl.BlockSpec((1, 8, D), lambda i, j: (i // tpb, 0, 0)),
                  pl.BlockSpec((1, tf, D), lambda i, j: (j, 0, 0)), pl.BlockSpec((1, tf, D), lambda i, j: (j, 0, 0)),
                  pl.BlockSpec((1, tf, D), lambda i, j: (j, 0, 0)), row(), vec(), vec()],
        out_specs=[pl.BlockSpec((1, tm, tf), lambda i, j: (j, i, 0)), pl.BlockSpec((1, tm, tf), lambda i, j: (j, i, 0)),
                   row(), row(), pl.BlockSpec((8, D), lambda i, j: (0, 0)),
                   pl.BlockSpec((1, 8, D), lambda i, j: (i // tpb, 0, 0))],
        out_shape=[jax.ShapeDtypeStruct((nf, t, tf), BF16), jax.ShapeDtypeStruct((nf, t, tf), BF16),
                   jax.ShapeDtypeStruct((t, D), BF16),
                   jax.ShapeDtypeStruct((t, D), F32), jax.ShapeDtypeStruct((8, D), F32),
                   jax.ShapeDtypeStruct((nbatch, 8, D), F32)],
        scratch_shapes=[pltpu.VMEM((tm, D), F32)],
        compiler_params=_params(("arbitrary", "arbitrary"), 60),
    )(x1, mod8, wg, wu, wd, target, ln2w, ln2b)


def _ffn_bwd(dz2, a, b, wg, wu, wd, x1, x2, u, mod8, ln1w, seq):
    t = x1.shape[0]
    tm = min(512, seq)
    nf, tf, _ = wg.shape
    tpb = seq // tm
    nbatch = t // seq

    def body(dz_ref, a_ref, b_ref, wg_ref, wu_ref, wd_ref, x1_ref, x_ref, u_ref, mod_ref, lw_ref,
             da_ref, db_ref, hm_ref, df_ref, du_ref, dxp_ref, st_ref, dm_ref, acc):
        i, j = pl.program_id(0), pl.program_id(1)

        @pl.when(j == 0)
        def _():
            df_ref[...] = ((1.0 + mod_ref[0, 5:6, :]) * dz_ref[...]).astype(BF16)
            acc[...] = jnp.zeros_like(acc)

        dhm = _dot_nt(df_ref[...], wd_ref[0])
        av = a_ref[0].astype(F32)
        bv = b_ref[0].astype(F32)
        sg = jax.nn.sigmoid(av)
        sl = av * sg
        hm_ref[0] = (sl * bv).astype(BF16)
        da = (dhm * bv * (sg * (1.0 + av * (1.0 - sg)))).astype(BF16)
        db = (dhm * sl).astype(BF16)
        da_ref[0] = da
        db_ref[0] = db
        acc[...] += _dot(da, wg_ref[0]) + _dot(db, wu_ref[0])

        @pl.when(j == nf - 1)
        def _():
            dh2 = acc[...]
            x1v = x1_ref[...]
            uv = u_ref[...].astype(F32)
            dx1 = ALPHA * dz_ref[...] + dh2 * (1.0 + mod_ref[0, 4:5, :])
            xhat, rstd = _ln_stats(ALPHA * x_ref[...] + (1.0 + mod_ref[0, 2:3, :]) * uv)
            dz1 = _ln_bwd(dx1, xhat, rstd, lw_ref[...])
            du_ref[...] = ((1.0 + mod_ref[0, 2:3, :]) * dz1).astype(BF16)
            dxp_ref[...] = (ALPHA * dz1).astype(dxp_ref.dtype)
            upd = jnp.concatenate(
                [jnp.sum(dx1 * xhat, axis=0, keepdims=True), jnp.sum(dx1, axis=0, keepdims=True),
                 jnp.zeros((6, D), F32)], axis=0)
            dmu = jnp.concatenate(
                [jnp.zeros((2, D), F32), jnp.sum(dz1 * uv, axis=0, keepdims=True),
                 jnp.sum(dh2, axis=0, keepdims=True), jnp.sum(dh2 * x1v, axis=0, keepdims=True),
                 jnp.zeros((3, D), F32)], axis=0)

            @pl.when(i == 0)
            def _():
                st_ref[...] = upd

            @pl.when(i > 0)
            def _():
                st_ref[...] += upd

            @pl.when(i % tpb == 0)
            def _():
                dm_ref[0] = dmu

            @pl.when(i % tpb != 0)
            def _():
                dm_ref[0] += dmu

    row = lambda: pl.BlockSpec((tm, D), lambda i, j: (i, 0))
    ffb = lambda: pl.BlockSpec((1, tm, tf), lambda i, j: (j, i, 0))
    return pl.pallas_call(
        body, name="ffn_bwd", grid=(t // tm, nf),
        in_specs=[row(), ffb(), ffb(),
                  pl.BlockSpec((1, tf, D), lambda i, j: (j, 0, 0)), pl.BlockSpec((1, tf, D), lambda i, j: (j, 0, 0)),
                  pl.BlockSpec((1, tf, D), lambda i, j: (j, 0, 0)), row(), row(), row(),
                  pl.BlockSpec((1, 8, D), lambda i, j: (i // tpb, 0, 0)), pl.BlockSpec((1, D), lambda i, j: (0, 0))],
        out_specs=[ffb(), ffb(), ffb(), row(), row(), row(), pl.BlockSpec((8, D), lambda i, j: (0, 0)),
                   pl.BlockSpec((1, 8, D), lambda i, j: (i // tpb, 0, 0))],
        out_shape=[jax.ShapeDtypeStruct((nf, t, tf), BF16), jax.ShapeDtypeStruct((nf, t, tf), BF16),
                   jax.ShapeDtypeStruct((nf, t, tf), BF16), jax.ShapeDtypeStruct((t, D), BF16),
                   jax.ShapeDtypeStruct((t, D), BF16), jax.ShapeDtypeStruct((t, D), BF16),
                   jax.ShapeDtypeStruct((8, D), F32), jax.ShapeDtypeStruct((nbatch, 8, D), F32)],
        scratch_shapes=[pltpu.VMEM((tm, D), F32)],
        compiler_params=_params(("arbitrary", "arbitrary"), 60),
    )(dz2, a, b, wg, wu, wd, x1, x2, u, mod8, ln1w)


def _adamw_math(w, g, m, v):
    m = B1 * m + (1.0 - B1) * g
    v = B2 * v + (1.0 - B2) * (g * g)
    m_hat = m / (1.0 - B1 ** STEP)
    v_hat = v / (1.0 - B2 ** STEP)
    return -LR * (m_hat / (jnp.sqrt(v_hat) + EPS) + WD * w), m, v


def _adamw(w, g, m, v, name):
    rows, cols = w.shape
    tr = rows
    for cand in (128, 64, 32, 16, 8):
        if rows % cand == 0:
            tr = cand
            break

    def body(w_ref, g_ref, m_ref, v_ref, d_ref, mo_ref, vo_ref):
        d, mn, vn = _adamw_math(w_ref[...], g_ref[...], m_ref[...], v_ref[...])
        d_ref[...] = d
        mo_ref[...] = mn
        vo_ref[...] = vn

    spec = pl.BlockSpec((tr, cols), lambda i: (i, 0))
    return pl.pallas_call(
        body, name=name, grid=(rows // tr,), in_specs=[spec] * 4, out_specs=[spec] * 3,
        out_shape=[jax.ShapeDtypeStruct((rows, cols), F32)] * 3,
        compiler_params=_params(("parallel",), 48),
    )(w, g, m, v)


def _adamw_halves(w, g_mine, g_sib, m, v, c_idx, name):
    rows, cols = w.shape
    hr = rows // 2
    tr = next(cand for cand in (128, 88, 64, 32, 16, 8) if hr % cand == 0)
    tph = hr // tr

    def body(c_ref, w_ref, gm_ref, gs_ref, m_ref, v_ref, g_ref, d_ref, mo_ref, vo_ref):
        g = jnp.where(pl.program_id(0) == c_ref[0], gm_ref[...], gs_ref[...])
        d, mn, vn = _adamw_math(w_ref[...], g, m_ref[...], v_ref[...])
        g_ref[...] = g
        d_ref[...] = d
        mo_ref[...] = mn
        vo_ref[...] = vn

    full = pl.BlockSpec((tr, cols), lambda h, i, c: (h * tph + i, 0))
    half = pl.BlockSpec((tr, cols), lambda h, i, c: (i, 0))
    return pl.pallas_call(
        body, name=name,
        grid_spec=pltpu.PrefetchScalarGridSpec(
            num_scalar_prefetch=1, grid=(2, tph), in_specs=[full, half, half, full, full], out_specs=[full] * 4),
        out_shape=[jax.ShapeDtypeStruct((rows, cols), F32)] * 4,
        compiler_params=_params(("parallel", "parallel"), 48),
    )(c_idx, w, g_mine, g_sib, m, v)


def _grad_w_ada(c_all, dmod_cols):
    def body(c_ref, d_ref, o_ref):
        c = c_ref[...]
        o_ref[...] = lax.dot_general(c * jax.nn.sigmoid(c), d_ref[...], (((0,), (0,)), ((), ())),
                                     preferred_element_type=F32, precision=HIGHEST)

    return pl.pallas_call(
        body, name="grad_w_ada", out_shape=jax.ShapeDtypeStruct((D, dmod_cols.shape[1]), F32),
        compiler_params=_params(vmem_mb=48),
    )(c_all, dmod_cols)


def _small_update(gath, w8, m8, v8):
    def body(g_ref, w_ref, m_ref, v_ref, go_ref, d_ref, mo_ref, vo_ref):
        g0 = g_ref[0, 0:1, :] + g_ref[0, 1:2, :]
        g1 = g_ref[0, 2:3, :]
        for dev in range(1, N_DEV):
            g0 = g0 + (g_ref[dev, 0:1, :] + g_ref[dev, 1:2, :])
            g1 = g1 + g_ref[dev, 2:3, :]
        w = w_ref[...]
        lb = jax.nn.sigmoid(w[1:2, O_LB0:O_LB1] - w[1:2, O_LB1:O_FOX])
        fac = lb * (1.0 - lb)
        g1 = jnp.concatenate([g1[:, :O_LB0], g1[:, O_LB0:O_LB1] * fac, -g1[:, O_LB1:O_FOX] * fac, g1[:, O_FOX:]],
                             axis=1)
        g = jnp.concatenate([g0, g1, jnp.zeros((6, SMALL_W), F32)], axis=0)
        d, mn, vn = _adamw_math(w, g, m_ref[...], v_ref[...])
        go_ref[...] = g
        d_ref[...] = d
        mo_ref[...] = mn
        vo_ref[...] = vn

    return pl.pallas_call(
        body, name="small_update", out_shape=[jax.ShapeDtypeStruct((8, SMALL_W), F32)] * 4,
        compiler_params=_params(vmem_mb=48),
    )(gath, w8, m8, v8)


def _pack_small(b_ada, ln1w, ln1b, ln2w, ln2b, norm_w, lb_logits, fox):
    row1 = jnp.concatenate([ln1w, ln1b, ln2w, ln2b, norm_w, lb_logits[0:1], lb_logits[1:2], fox,
                            jnp.zeros((1, SMALL_W - O_FOX - BH), F32)], axis=1)
    return jnp.concatenate([b_ada, row1, jnp.zeros((6, SMALL_W), F32)], axis=0)


def _unpack_small(p):
    r = p[1:2]
    lb = jnp.concatenate([r[:, O_LB0:O_LB1], r[:, O_LB1:O_FOX]], axis=0)
    return dict(b_ada=p[0:1], ln1_w=r[:, O_LN1W:O_LN1B], ln1_b=r[:, O_LN1B:O_LN2W], ln2_w=r[:, O_LN2W:O_LN2B],
                ln2_b=r[:, O_LN2B:O_NORM], hgrn_norm_w=r[:, O_NORM:O_LB0], lb_logits=lb,
                fox_f_bias=r[:, O_FOX:O_FOX + BH])


_BIG = ("w_in", "w_branch_a", "w_branch_b", "w_out", "w_ffn_gate", "w_ffn_up", "w_ffn_down")
_TRANSPOSED = ("w_ffn_gate", "w_ffn_up")


def _cols_of_chips(stacked):
    return jnp.concatenate([stacked[k] for k in range(N_CHIPS)], axis=1)


def kernel(x, c, w_ada, b_ada, w_in, fox_f_bias, lb_logits, hgrn_norm_w, w_branch_a, w_branch_b, w_out, ln1_w, ln1_b, w_ffn_gate, w_ffn_up, w_ffn_down, ln2_w, ln2_b, loss_target, m_w_ada, m_b_ada, m_w_in, m_fox_f_bias, m_lb_logits, m_hgrn_norm_w, m_w_branch_a, m_w_branch_b, m_w_out, m_ln1_w, m_ln1_b, m_w_ffn_gate, m_w_ffn_up, m_w_ffn_down, m_ln2_w, m_ln2_b, v_w_ada, v_b_ada, v_w_in, v_fox_f_bias, v_lb_logits, v_hgrn_norm_w, v_w_branch_a, v_w_branch_b, v_w_out, v_ln1_w, v_ln1_b, v_w_ffn_gate, v_w_ffn_up, v_w_ffn_down, v_ln2_w, v_ln2_b):
    nbatch, seq, _ = x.shape
    t = nbatch * seq
    ax, ay, ac = lax.axis_index("x"), lax.axis_index("y"), lax.axis_index("c")
    chip = 2 * ax + ay
    dev = 2 * chip + ac
    chip_arr = jnp.reshape(chip, (1,)).astype(jnp.int32)
    core_arr = jnp.reshape(ac, (1,)).astype(jnp.int32)

    tr = lambda a: jnp.swapaxes(a[0], 0, 1)
    shard_w = dict(w_in=w_in[0], w_branch_a=w_branch_a[0], w_branch_b=w_branch_b[0], w_out=w_out[0],
                   w_ffn_gate=tr(w_ffn_gate), w_ffn_up=tr(w_ffn_up), w_ffn_down=w_ffn_down[0])
    shard_m = dict(w_in=m_w_in[0], w_branch_a=m_w_branch_a[0], w_branch_b=m_w_branch_b[0], w_out=m_w_out[0],
                   w_ffn_gate=tr(m_w_ffn_gate), w_ffn_up=tr(m_w_ffn_up), w_ffn_down=m_w_ffn_down[0])
    shard_v = dict(w_in=v_w_in[0], w_branch_a=v_w_branch_a[0], w_branch_b=v_w_branch_b[0], w_out=v_w_out[0],
                   w_ffn_gate=tr(v_w_ffn_gate), w_ffn_up=tr(v_w_ffn_up), w_ffn_down=v_w_ffn_down[0])

    shard16 = {n: shard_w[n].astype(BF16) for n in _BIG}

    def with_mine(gathered, n):
        return lax.dynamic_update_slice(gathered, shard16[n][None], (chip, 0, 0))

    w_p = _permute_cols(_cols_of_chips(with_mine(_gather_weights([shard16["w_in"]])[0], "w_in")))
    late = _BIG[1:]
    late_send, late_recv, late_src, late_land, late_token = _split_start(
        _gather_copies, [shard16[n] for n in late],
        [lax.empty((N_CHIPS,) + shard16[n].shape, BF16) for n in late], "gather_late_start")

    c8 = jnp.concatenate([c, jnp.zeros((8 - nbatch, D), F32)], axis=0)
    c_all = _allgather8(c8, "gather_c")[:, :nbatch, :].reshape(N_DEV * nbatch, D)
    ncol = w_ada.shape[2]
    b_cols = lax.dynamic_slice_in_dim(b_ada, chip * ncol, ncol, axis=1)
    mod_g = _allgather8(_mod_shard(c_all, w_ada[0], b_cols), "gather_mod")
    mod_all = jnp.concatenate([mod_g[2 * k] for k in range(N_CHIPS)], axis=1)
    mod_mine = lax.dynamic_slice_in_dim(mod_all, dev * nbatch, nbatch, axis=0)
    mod8 = jnp.concatenate([mod_mine.reshape(nbatch, 6, D), jnp.zeros((nbatch, 2, D), F32)], axis=1)
    mod8 = mod8 + late_token[0, 0]

    x2 = x.reshape(t, D)
    tgt2 = loss_target.reshape(t, D)
    bias128 = jnp.concatenate([fox_f_bias, jnp.zeros((1, 128 - BH), F32)], axis=1)

    proj, h16 = _proj(x2, mod8, w_p, seq, BF16, "proj")
    projf = _rows_matmul(h16, w_p[:, COL_BF:], "proj_forget")
    ya, ckpt = _hgrn_fwd(proj, lb_logits, hgrn_norm_w, nbatch, seq)
    cum_cols = _fox_cum(projf, bias128, nbatch, seq)
    yb, lse = _fox_fwd(proj, cum_cols, nbatch, seq)
    late_land = _pass_to_sibling(
        _split_wait(_gather_copies, late_send, late_recv, late_src, late_land, yb, "gather_late_wait"))
    full = {n: with_mine(g, n) for n, g in zip(late, late_land)}
    wba, wbb = _cols_of_chips(full["w_branch_a"]), _cols_of_chips(full["w_branch_b"])
    wout = full["w_out"].reshape(D, D)
    wg_t, wu_t, wd = full["w_ffn_gate"], full["w_ffn_up"], full["w_ffn_down"]
    merged, u, x1 = _merge_fwd(ya, yb, proj, x2, mod8, wba, wbb, wout, ln1_w, ln1_b, seq)
    a_pre, b_pre, h2, dz2, st2, dm2 = _ffn_fwd(x1, mod8, wg_t, wu_t, wd, tgt2, ln2_w, ln2_b, seq)
    loss = lax.psum(st2[2, 0], ("x", "y", "c"))

    da, db, hmid, dffn, du, dxp, st1, dm1 = _ffn_bwd(dz2, a_pre, b_pre, wg_t, wu_t, wd, x1, x2, u, mod8, ln1_w, seq)
    g_st = {}
    g_st["w_ffn_down"] = _tn_matmul(hmid, dffn, "dw_ffn_down", seq)
    g_st["w_ffn_gate"] = _tn_matmul(da, h2, "dw_ffn_gate", seq)
    g_st["w_ffn_up"] = _tn_matmul(db, h2, "dw_ffn_up", seq)
    g_st["w_out"] = _tn_matmul(merged, du, "dw_out", seq).reshape(N_CHIPS, D // N_CHIPS, D)

    def sum_over_cores(names, tag):
        g_list = [g_st[n] for n in names]
        return [_add_my_half(g, o, core_arr, "grad_add_halves_" + n)
                for n, g, o in zip(names, g_list, _swap_halves(g_list, "grad_swap_halves_" + tag))]

    early = ("w_ffn_down", "w_ffn_gate", "w_ffn_up", "w_out")
    e_halves = sum_over_cores(early, "early")
    e_send, e_recv, e_src, e_land, e_token = _split_start(
        _scatter_copies, [h16 for _, h16 in e_halves],
        [lax.empty((3,) + h16.shape[1:], BF16) for _, h16 in e_halves], "grad_scatter_early_start")
    dproj, dpa, dpb, dya, dyb = _merge_bwd(du, ya, yb, proj, wba, wbb, wout, e_token, seq)
    g_st["w_branch_a"] = _tn_matmul(ya, dpa, "dw_branch_a", seq, split=D // N_CHIPS)
    g_st["w_branch_b"] = _tn_matmul(yb, dpb, "dw_branch_b", seq, split=D // N_CHIPS)
    dproj, dq, drs, dcs = _fox_bwd(proj, cum_cols, lse, yb, dyb, dproj, nbatch, seq)
    dproj = _place_cols(dproj, dq, COL_BQ)
    dproj, sm_fox = _fox_dbf(projf, bias128, drs, dcs, dproj, nbatch, seq)
    dproj, sm_hgrn = _hgrn_bwd(proj, dya, ckpt, lb_logits, hgrn_norm_w, dproj, nbatch, seq)
    grad_x2, dm0 = _dh_kernel(dproj, w_p, x2, dxp, mod8, seq)
    dw_in = _unpermute_cols(_tn_matmul(h16, dproj, "dw_in", seq))
    ncin = NIN // N_CHIPS
    g_st["w_in"] = jnp.stack([dw_in[:, k * ncin:(k + 1) * ncin] for k in range(N_CHIPS)])

    e_recv = _split_wait(_scatter_copies, e_send, e_recv, e_src, e_land, dw_in, "grad_scatter_early_wait")
    rest = ("w_in", "w_branch_a", "w_branch_b")
    r_halves = sum_over_cores(rest, "rest")
    r_send, r_rcv, r_src, r_land, r_token = _split_start(
        _scatter_copies, [h16 for _, h16 in r_halves],
        [lax.empty((3,) + h16.shape[1:], BF16) for _, h16 in r_halves], "grad_scatter_rest_start")

    def finish(names, halves, recv, token, tag):
        g_mine = [_add_chips(h32, r, chip_arr, "grad_add_chips_" + n) for n, (h32, _), r in zip(names, halves, recv)]
        g_sib = _join_halves(g_mine, token, "grad_join_halves_" + tag)
        for n, gm, gs in zip(names, g_mine, g_sib):
            grads[n], deltas[n], new_m[n], new_v[n] = _adamw_halves(
                shard_w[n], gm, gs, shard_m[n], shard_v[n], core_arr, "adamw_" + n)

    grads, deltas, new_m, new_v = {}, {}, {}, {}
    finish(early, e_halves, e_recv, r_token, "early")

    dmod = (dm0 + dm1 + dm2)[:, :6, :].reshape(nbatch, 6 * D)
    row2 = jnp.concatenate([st1[0:1], st1[1:2], st2[0:1], st2[1:2], sm_hgrn[1:2], sm_hgrn[0:1], sm_hgrn[0:1],
                            sm_fox[0:1, :BH], jnp.zeros((1, SMALL_W - O_FOX - BH), F32)], axis=1)
    spack = jnp.concatenate([dmod, row2, jnp.zeros((8 - nbatch - 1, SMALL_W), F32)], axis=0)
    spack = spack + r_token[0, 0]
    gath = _allgather8(spack, "gather_small")
    w8 = _pack_small(b_ada, ln1_w, ln1_b, ln2_w, ln2_b, hgrn_norm_w, lb_logits, fox_f_bias)
    m8 = _pack_small(m_b_ada, m_ln1_w, m_ln1_b, m_ln2_w, m_ln2_b, m_hgrn_norm_w, m_lb_logits, m_fox_f_bias)
    v8 = _pack_small(v_b_ada, v_ln1_w, v_ln1_b, v_ln2_w, v_ln2_b, v_hgrn_norm_w, v_lb_logits, v_fox_f_bias)
    sg, sd, smn, svn = (_unpack_small(p) for p in _small_update(gath, w8, m8, v8))
    dmod_all = gath[:, :nbatch, :].reshape(N_DEV * nbatch, SMALL_W)
    g_ada = _grad_w_ada(c_all, lax.dynamic_slice_in_dim(dmod_all, chip * ncol, ncol, axis=1))

    for group, small in zip((grads, deltas, new_m, new_v), (sg, sd, smn, svn)):
        group.update(small)
    grads["w_ada"] = g_ada
    deltas["w_ada"], new_m["w_ada"], new_v["w_ada"] = _adamw(w_ada[0], g_ada, m_w_ada[0], v_w_ada[0], "adamw_w_ada")
    done = sum(new_v[n][0:8, 0:128] for n in early) + new_v["w_ada"][0:8, 0:128]
    r_recv = _split_wait(_scatter_copies, r_send, r_rcv, r_src, r_land, done, "grad_scatter_rest_wait")
    finish(rest, r_halves, r_recv, late_token, "rest")

    names = ["w_ada", "b_ada", "w_in", "fox_f_bias", "lb_logits", "hgrn_norm_w", "w_branch_a", "w_branch_b", "w_out",
             "ln1_w", "ln1_b", "w_ffn_gate", "w_ffn_up", "w_ffn_down", "ln2_w", "ln2_b"]
    shapes = dict(w_ada=w_ada.shape, b_ada=b_ada.shape, w_in=w_in.shape, fox_f_bias=fox_f_bias.shape,
                  lb_logits=lb_logits.shape, hgrn_norm_w=hgrn_norm_w.shape, w_branch_a=w_branch_a.shape,
                  w_branch_b=w_branch_b.shape, w_out=w_out.shape, ln1_w=ln1_w.shape, ln1_b=ln1_b.shape,
                  w_ffn_gate=w_ffn_gate.shape, w_ffn_up=w_ffn_up.shape, w_ffn_down=w_ffn_down.shape,
                  ln2_w=ln2_w.shape, ln2_b=ln2_b.shape)
    outs = [loss, grad_x2.reshape(x.shape)]
    for group in (grads, deltas, new_m, new_v):
        outs += [(jnp.swapaxes(group[n], 0, 1) if n in _TRANSPOSED else group[n]).reshape(shapes[n]) for n in names]
    return tuple(outs)
```

```python
import functools

import jax
import jax.numpy as jnp
from jax import lax
from jax.experimental import pallas as pl
from jax.experimental.pallas import tpu as pltpu

F32 = jnp.float32
BF16 = jnp.bfloat16
MESH = pl.DeviceIdType.MESH
HIGHEST = lax.Precision.HIGHEST

D = 1024
AW = 512
AH = 4
ADH = 128
BH = 8
BDH = 64
DFF = 2816
NIN = 5640
NP = 5760
N_CHIPS = 4
N_DEV = 8
HGRN_BLOCK = 256
FFN_TOKENS = 512
COL_GATES = 0
COL_A = 2048
COL_BQ = 4096
COL_KV = 4608
COL_BF = 5632
HGRN_HEADS = 4
FOX_PAIRS = 2
ALPHA = 2.0 ** 0.25
LN_EPS = 1e-5
RMS_EPS = 1e-6
NEG = -1e30
LOG2E = 1.4426950408889634
LR, B1, B2, EPS, WD, STEP = 0.001, 0.9, 0.999, 1e-08, 0.01, 10
SMALL_W = 6144
O_LN1W, O_LN1B, O_LN2W, O_LN2B, O_NORM, O_LB0, O_LB1, O_FOX = 0, 1024, 2048, 3072, 4096, 4608, 5120, 5632


def _params(sem=None, vmem_mb=None):
    kw = {}
    if sem is not None:
        kw["dimension_semantics"] = sem
    if vmem_mb is not None:
        kw["vmem_limit_bytes"] = vmem_mb << 20
    return pltpu.CompilerParams(**kw)


def _dot(a, b):
    return jnp.dot(a.astype(BF16), b.astype(BF16), preferred_element_type=F32)


def _dot_nt(a, b):
    return lax.dot_general(a.astype(BF16), b.astype(BF16), (((1,), (1,)), ((), ())), preferred_element_type=F32)


def _dot_tn(a, b):
    return lax.dot_general(a.astype(BF16), b.astype(BF16), (((0,), (0,)), ((), ())), preferred_element_type=F32)


def _dot_f32(a, b):
    return jnp.dot(a, b, preferred_element_type=F32, precision=HIGHEST)


def _perm_segments():
    segs = [(3592, 5640)]
    for h in range(4):
        segs += [(128 * h + 512 * t, 128 * h + 512 * t + 128) for t in range(4)]
    segs += [(2048, 2560)]
    for p in range(4):
        segs += [(2560 + 128 * p, 2688 + 128 * p), (3072 + 128 * p, 3200 + 128 * p)]
    segs += [(3584, 3592)]
    return segs


def _permute_cols(w):
    parts = [w[:, a:b] for a, b in _perm_segments()]
    parts.append(jnp.zeros((w.shape[0], NP - NIN), w.dtype))
    return jnp.concatenate(parts, axis=1)


def _unpermute_to_chips(g):
    pos, where = 0, []
    for a, b in _perm_segments():
        where.append((a, b, pos))
        pos += b - a
    ncin = NIN // N_CHIPS
    out = []
    for k in range(N_CHIPS):
        lo, hi = k * ncin, (k + 1) * ncin
        parts = [g[:, p + max(a, lo) - a:p + min(b, hi) - a] for a, b, p in sorted(where) if max(a, lo) < min(b, hi)]
        out.append(jnp.concatenate(parts, axis=1))
    return jnp.stack(out)


def _allgather8(v, name):
    rows, cols = v.shape

    def body(x_ref, out_ref, send_sems, recv_sems, local_sem):
        x, y, c = lax.axis_index("x"), lax.axis_index("y"), lax.axis_index("c")
        me, sibling = (x, y, c), (x, y, 1 - c)
        chips = [(1 - x, y), (x, 1 - y), (1 - x, 1 - y)]

        def slot(px, py, pc):
            return out_ref.at[4 * px + 2 * py + pc]

        def copy(k, block, to, src=None):
            return pltpu.make_async_remote_copy(
                src_ref=slot(*block) if src is None else src, dst_ref=slot(*block),
                send_sem=send_sems.at[k], recv_sem=recv_sems.at[k], device_id=to, device_id_type=MESH)

        mine = pltpu.make_async_copy(x_ref, slot(*me), local_sem)
        mine.start()
        first = [copy(0, me, sibling, src=x_ref)]
        first += [copy(1 + j, me, (*chip, c), src=x_ref) for j, chip in enumerate(chips)]
        for cp in first:
            cp.start()
        passed = [copy(4 + j, (*chip, c), sibling) for j, chip in enumerate(chips)]
        for j, chip in enumerate(chips):
            copy(1 + j, (*chip, c), me).wait_recv()
            passed[j].start()
        copy(0, sibling, me).wait_recv()
        for j, chip in enumerate(chips):
            copy(4 + j, (*chip, 1 - c), me).wait_recv()
        for cp in first + passed:
            cp.wait_send()
        mine.wait()

    return pl.pallas_call(
        body, name=name,
        out_shape=jax.ShapeDtypeStruct((N_DEV, rows, cols), v.dtype),
        in_specs=[pl.BlockSpec(memory_space=pltpu.VMEM)],
        out_specs=pl.BlockSpec(memory_space=pltpu.VMEM),
        scratch_shapes=[pltpu.SemaphoreType.DMA((7,)), pltpu.SemaphoreType.DMA((7,)), pltpu.SemaphoreType.DMA],
    )(v)


def _hbm_specs(n):
    return [pl.BlockSpec(memory_space=pl.ANY)] * n


def _swap_halves(grads, name):
    n = len(grads)

    def body(*refs):
        ins, outs, (send_sems, recv_sems) = refs[:n], refs[n:2 * n], refs[2 * n:]
        x, y, c = lax.axis_index("x"), lax.axis_index("y"), lax.axis_index("c")
        cps = []
        for w in range(n):
            hr = ins[w].shape[1] // 2
            cps.append(pltpu.make_async_remote_copy(
                src_ref=ins[w].at[:, pl.ds((1 - c) * hr, hr), :], dst_ref=outs[w],
                send_sem=send_sems.at[w], recv_sem=recv_sems.at[w], device_id=(x, y, 1 - c), device_id_type=MESH))
        for cp in cps:
            cp.start()
        for cp in cps:
            cp.wait()

    return pl.pallas_call(
        body, name=name,
        out_shape=[jax.ShapeDtypeStruct((N_CHIPS, g.shape[1] // 2, g.shape[2]), g.dtype) for g in grads],
        in_specs=_hbm_specs(n), out_specs=_hbm_specs(n),
        scratch_shapes=[pltpu.SemaphoreType.DMA((n,)), pltpu.SemaphoreType.DMA((n,))],
    )(*grads)


def _join_halves(halves, token, name):
    n = len(halves)

    def body(*refs):
        ins, outs, (send_sems, recv_sems) = refs[:n], refs[n + 1:2 * n + 1], refs[2 * n + 1:]
        x, y, c = lax.axis_index("x"), lax.axis_index("y"), lax.axis_index("c")
        cps = [pltpu.make_async_remote_copy(
            src_ref=ins[w], dst_ref=outs[w], send_sem=send_sems.at[w], recv_sem=recv_sems.at[w],
            device_id=(x, y, 1 - c), device_id_type=MESH) for w in range(n)]
        for cp in cps:
            cp.start()
        for cp in cps:
            cp.wait()

    return pl.pallas_call(
        body, name=name,
        out_shape=[jax.ShapeDtypeStruct(h.shape, h.dtype) for h in halves],
        in_specs=_hbm_specs(n + 1), out_specs=_hbm_specs(n),
        scratch_shapes=[pltpu.SemaphoreType.DMA((n,)), pltpu.SemaphoreType.DMA((n,))],
    )(*halves, token)


def _in_hbm(v):
    return pltpu.with_memory_space_constraint(v, pltpu.HBM)


_SPLIT_COPY = pltpu.CompilerParams(has_side_effects=pltpu.SideEffectType.DATAFLOW_SIDE_EFFECTING)


def _gather_copies(srcs, lands, send_sems, recv_sems):
    x, y, c = lax.axis_index("x"), lax.axis_index("y"), lax.axis_index("c")
    cps = []
    for w, (src, land) in enumerate(zip(srcs, lands)):
        hr = src.shape[0] // 2
        for j, chip in enumerate([(1 - x, y), (x, 1 - y), (1 - x, 1 - y)]):
            cps.append(pltpu.make_async_remote_copy(
                src_ref=src.at[pl.ds(c * hr, hr), :], dst_ref=land.at[2 * x + y, pl.ds(c * hr, hr), :],
                send_sem=send_sems.at[3 * w + j], recv_sem=recv_sems.at[3 * w + j],
                device_id=(*chip, c), device_id_type=MESH))
    return cps


def _scatter_copies(srcs, lands, send_sems, recv_sems):
    x, y, c = lax.axis_index("x"), lax.axis_index("y"), lax.axis_index("c")
    cps = []
    for w, (src, land) in enumerate(zip(srcs, lands)):
        for j, chip in enumerate([(1 - x, y), (x, 1 - y), (1 - x, 1 - y)]):
            cps.append(pltpu.make_async_remote_copy(
                src_ref=src.at[2 * chip[0] + chip[1]], dst_ref=land.at[j],
                send_sem=send_sems.at[3 * w + j], recv_sem=recv_sems.at[3 * w + j],
                device_id=(*chip, c), device_id_type=MESH))
    return cps


def _split_start(copies, srcs, lands, name):
    n = len(srcs)

    def body(*refs):
        src, lnd, send_sems, recv_sems, token = refs[:n], refs[n:2 * n], refs[2 * n], refs[2 * n + 1], refs[-1]
        for cp in copies(src, lnd, send_sems, recv_sems):
            cp.start()
        token[...] = jnp.zeros_like(token)

    hbm = pl.BlockSpec(memory_space=pltpu.HBM)
    sem = pl.BlockSpec(memory_space=pltpu.SEMAPHORE)
    outs = pl.pallas_call(
        body, name=name,
        out_shape=(pltpu.SemaphoreType.DMA((3 * n,)), pltpu.SemaphoreType.DMA((3 * n,)),
                   *[pltpu.HBM(v.shape, v.dtype) for v in srcs + lands], jax.ShapeDtypeStruct((8, 128), F32)),
        in_specs=[hbm] * (2 * n),
        out_specs=(sem, sem, *([hbm] * (2 * n)), pl.BlockSpec(memory_space=pltpu.VMEM)),
        input_output_aliases={i: 2 + i for i in range(2 * n)},
        compiler_params=_SPLIT_COPY,
    )(*[_in_hbm(v) for v in srcs + lands])
    return outs[0], outs[1], list(outs[2:2 + n]), list(outs[2 + n:2 + 2 * n]), outs[-1]


def _split_wait(copies, send_sems, recv_sems, srcs, lands, after, name):
    n = len(srcs)

    def body(*refs):
        src, lnd, send_sems, recv_sems = refs[:n], refs[n:2 * n], refs[2 * n], refs[2 * n + 1]
        for cp in copies(src, lnd, send_sems, recv_sems):
            cp.wait_send()
            cp.wait_recv()

    hbm = pl.BlockSpec(memory_space=pltpu.HBM)
    sem = pl.BlockSpec(memory_space=pltpu.SEMAPHORE)
    outs = pl.pallas_call(
        body, name=name,
        out_shape=tuple(pltpu.HBM(v.shape, v.dtype) for v in srcs + lands),
        in_specs=[hbm] * (2 * n) + [sem, sem, pl.BlockSpec(memory_space=pl.ANY)],
        out_specs=tuple([hbm] * (2 * n)),
        input_output_aliases={i: i for i in range(2 * n)},
        compiler_params=_SPLIT_COPY,
    )(*srcs, *lands, send_sems, recv_sems, after)
    return list(outs[n:])


def _pass_to_sibling(lands, name):
    n = len(lands)

    def body(*refs):
        ins, outs, (send_sems, recv_sems) = refs[:n], refs[n:2 * n], refs[2 * n:]
        x, y, c = lax.axis_index("x"), lax.axis_index("y"), lax.axis_index("c")
        cps = []
        for w in range(n):
            hr = ins[w].shape[1] // 2
            for j, chip in enumerate([(1 - x, y), (x, 1 - y), (1 - x, 1 - y)]):
                k = 2 * chip[0] + chip[1]
                cps.append(pltpu.make_async_remote_copy(
                    src_ref=ins[w].at[k, pl.ds(c * hr, hr), :], dst_ref=outs[w].at[k, pl.ds(c * hr, hr), :],
                    send_sem=send_sems.at[3 * w + j], recv_sem=recv_sems.at[3 * w + j],
                    device_id=(x, y, 1 - c), device_id_type=MESH))
        for cp in cps:
            cp.start()
        for cp in cps:
            cp.wait()

    return pl.pallas_call(
        body, name=name,
        out_shape=[jax.ShapeDtypeStruct(v.shape, v.dtype) for v in lands],
        in_specs=_hbm_specs(n), out_specs=_hbm_specs(n),
        input_output_aliases={i: i for i in range(n)},
        scratch_shapes=[pltpu.SemaphoreType.DMA((3 * n,)), pltpu.SemaphoreType.DMA((3 * n,))],
    )(*lands)


def _row_tile(rows):
    for cand in (256, 176, 128, 64, 32, 16):
        if rows % cand == 0:
            return cand
    raise ValueError(rows)


def _add_my_half(g, other, c_idx, name):
    _, k, n = g.shape
    hr = k // 2
    tr = _row_tile(hr)
    nb = hr // tr

    def body(c_ref, g_ref, o_ref, out_ref, out16_ref):
        s = g_ref[...] + o_ref[...]
        out_ref[...] = s
        out16_ref[...] = s.astype(BF16)

    return pl.pallas_call(
        body, name=name,
        grid_spec=pltpu.PrefetchScalarGridSpec(
            num_scalar_prefetch=1, grid=(N_CHIPS, nb),
            in_specs=[pl.BlockSpec((1, tr, n), lambda j, i, c: (j, c[0] * nb + i, 0)),
                      pl.BlockSpec((1, tr, n), lambda j, i, c: (j, i, 0))],
            out_specs=[pl.BlockSpec((1, tr, n), lambda j, i, c: (j, i, 0)),
                       pl.BlockSpec((1, tr, n), lambda j, i, c: (j, i, 0))]),
        out_shape=[jax.ShapeDtypeStruct((N_CHIPS, hr, n), F32), jax.ShapeDtypeStruct((N_CHIPS, hr, n), BF16)],
        compiler_params=_params(("parallel", "parallel")),
    )(c_idx, g, other)


def _add_chips(red, recv, chip_idx, name):
    _, hr, n = red.shape
    tr = _row_tile(hr)

    def body(k_ref, r_ref, v_ref, out_ref):
        out_ref[...] = ((r_ref[0] + v_ref[0].astype(F32)) + v_ref[1].astype(F32)) + v_ref[2].astype(F32)

    return pl.pallas_call(
        body, name=name,
        grid_spec=pltpu.PrefetchScalarGridSpec(
            num_scalar_prefetch=1, grid=(hr // tr,),
            in_specs=[pl.BlockSpec((1, tr, n), lambda i, k: (k[0], i, 0)),
                      pl.BlockSpec((3, tr, n), lambda i, k: (0, i, 0))],
            out_specs=pl.BlockSpec((tr, n), lambda i, k: (i, 0))),
        out_shape=jax.ShapeDtypeStruct((hr, n), F32),
        compiler_params=_params(("parallel",)),
    )(chip_idx, red, recv)


def _mod_shard(c_all, w_ada, b_ada):
    nb, cols = c_all.shape[0], w_ada.shape[1]

    def body(c_ref, w_ref, b_ref, o_ref):
        c = c_ref[...]
        o_ref[...] = _dot(c * jax.nn.sigmoid(c), w_ref[...]) + b_ref[...]

    return pl.pallas_call(
        body, name="mod_shard", out_shape=jax.ShapeDtypeStruct((nb, cols), F32),
        compiler_params=_params(vmem_mb=48),
    )(c_all, w_ada, b_ada)


def _proj(x2, mod8, w, seq, out_dtype, name):
    t = x2.shape[0]
    n = w.shape[1]
    tm, tn = min(2048, seq), min(1152, n)
    tpb = seq // tm

    def body(x_ref, mod_ref, w_ref, o_ref, h_ref):
        @pl.when(pl.program_id(1) == 0)
        def _():
            h_ref[...] = (x_ref[...] * (1.0 + mod_ref[0, 1:2, :]) + mod_ref[0, 0:1, :]).astype(BF16)
        o_ref[...] = jnp.dot(h_ref[...], w_ref[...], preferred_element_type=F32).astype(o_ref.dtype)

    return pl.pallas_call(
        body, name=name, grid=(t // tm, n // tn),
        in_specs=[pl.BlockSpec((tm, D), lambda i, j: (i, 0)),
                  pl.BlockSpec((1, 8, D), lambda i, j: (i // tpb, 0, 0)),
                  pl.BlockSpec((D, tn), lambda i, j: (0, j))],
        out_specs=[pl.BlockSpec((tm, tn), lambda i, j: (i, j)), pl.BlockSpec((tm, D), lambda i, j: (i, 0))],
        out_shape=[jax.ShapeDtypeStruct((t, n), out_dtype), jax.ShapeDtypeStruct((t, D), BF16)],
        compiler_params=_params(("parallel", "arbitrary"), 56),
    )(x2, mod8, w)


def _rows_matmul(a, w, name):
    t, k = a.shape
    n = w.shape[1]
    tm = 1024 if t % 1024 == 0 else t

    def body(a_ref, w_ref, o_ref):
        o_ref[...] = jnp.dot(a_ref[...], w_ref[...], preferred_element_type=F32)

    return pl.pallas_call(
        body, name=name, grid=(t // tm,),
        in_specs=[pl.BlockSpec((tm, k), lambda i: (i, 0)), pl.BlockSpec((k, n), lambda i: (0, 0))],
        out_specs=pl.BlockSpec((tm, n), lambda i: (i, 0)),
        out_shape=jax.ShapeDtypeStruct((t, n), F32),
        compiler_params=_params(("parallel",)),
    )(a, w)


def _tn_matmul(a, b, name, seq, split=None):
    a_st, b_st = a.ndim == 3, b.ndim == 3
    t, ka = a.shape[-2:]
    n = b.shape[-1]
    tt = min(1024, seq)
    nt = t // tt
    if a_st or b_st:
        steps, tn = (a.shape[0] if a_st else b.shape[0]), n
    else:
        tn = split
        if tn is None:
            tn = next(cand for cand in (1920, 1024, 1408, 512, n) if n % cand == 0)
        steps = n // tn
    stacked_out = a_st or b_st or split is not None

    def body(a_ref, b_ref, o_ref):
        part = _dot_tn(a_ref[0] if a_st else a_ref[...], b_ref[0] if b_st else b_ref[...])
        if stacked_out:
            part = part[None]

        @pl.when(pl.program_id(1) == 0)
        def _():
            o_ref[...] = part

        @pl.when(pl.program_id(1) > 0)
        def _():
            o_ref[...] += part

    if a_st:
        in_specs = [pl.BlockSpec((1, tt, ka), lambda j, k: (j, k, 0))]
    else:
        in_specs = [pl.BlockSpec((tt, ka), lambda j, k: (k, 0))]
    if b_st:
        in_specs.append(pl.BlockSpec((1, tt, n), lambda j, k: (j, k, 0)))
    else:
        in_specs.append(pl.BlockSpec((tt, tn), lambda j, k: (k, 0 if a_st else j)))
    if stacked_out:
        out_spec = pl.BlockSpec((1, ka, tn), lambda j, k: (j, 0, 0))
        out_shape = jax.ShapeDtypeStruct((steps, ka, tn), F32)
    else:
        out_spec = pl.BlockSpec((ka, tn), lambda j, k: (0, j))
        out_shape = jax.ShapeDtypeStruct((ka, n), F32)
    return pl.pallas_call(
        body, name=name, grid=(steps, nt), in_specs=in_specs, out_specs=out_spec, out_shape=out_shape,
        compiler_params=_params(("parallel", "arbitrary"), 56),
    )(a, b)


def _dh_kernel(dproj, w_p, x2, dxp, mod8, seq):
    t = x2.shape[0]
    tm, tk = min(1024, seq), 1920
    tpb = seq // tm
    nk = NP // tk
    nbatch = t // seq

    def body(dp_ref, w_ref, x_ref, dxp_ref, mod_ref, gx_ref, dm_ref, acc):
        i, k = pl.program_id(0), pl.program_id(1)

        @pl.when(k == 0)
        def _():
            acc[...] = jnp.zeros_like(acc)

        acc[...] += _dot_nt(dp_ref[...], w_ref[...])

        @pl.when(k == nk - 1)
        def _():
            dh = acc[...]
            gx_ref[...] = dxp_ref[...].astype(F32) + dh * (1.0 + mod_ref[0, 1:2, :])
            upd = jnp.concatenate(
                [jnp.sum(dh, axis=0, keepdims=True), jnp.sum(dh * x_ref[...], axis=0, keepdims=True),
                 jnp.zeros((6, D), F32)], axis=0)

            @pl.when(i % tpb == 0)
            def _():
                dm_ref[0] = upd

            @pl.when(i % tpb != 0)
            def _():
                dm_ref[0] += upd

    return pl.pallas_call(
        body, name="dh", grid=(t // tm, nk),
        in_specs=[pl.BlockSpec((tm, tk), lambda i, k: (i, k)),
                  pl.BlockSpec((D, tk), lambda i, k: (0, k)),
                  pl.BlockSpec((tm, D), lambda i, k: (i, 0)),
                  pl.BlockSpec((tm, D), lambda i, k: (i, 0)),
                  pl.BlockSpec((1, 8, D), lambda i, k: (i // tpb, 0, 0))],
        out_specs=[pl.BlockSpec((tm, D), lambda i, k: (i, 0)),
                   pl.BlockSpec((1, 8, D), lambda i, k: (i // tpb, 0, 0))],
        out_shape=[jax.ShapeDtypeStruct((t, D), F32), jax.ShapeDtypeStruct((nbatch, 8, D), F32)],
        scratch_shapes=[pltpu.VMEM((tm, D), F32)],
        compiler_params=_params(("arbitrary", "arbitrary"), 56),
    )(dproj, w_p, x2, dxp, mod8)


def _tri(n, upper):
    r = lax.broadcasted_iota(jnp.int32, (n, n), 0)
    c = lax.broadcasted_iota(jnp.int32, (n, n), 1)
    return jnp.where((c >= r) if upper else (c <= r), 1.0, 0.0).astype(F32)


@jax.custom_vjp
def _mm_nn(a, b):
    return _dot(a, b)


_mm_nn.defvjp(lambda a, b: (_dot(a, b), (a, b)),
              lambda res, g: (_dot_nt(g, res[1]), _dot_tn(res[0], g)))


@jax.custom_vjp
def _mm_nt(a, b):
    return _dot_nt(a, b)


_mm_nt.defvjp(lambda a, b: (_dot_nt(a, b), (a, b)),
              lambda res, g: (_dot(g, res[1]), _dot_tn(g, res[0])))


@jax.custom_vjp
def _mm_tn(a, b):
    return _dot_tn(a, b)


_mm_tn.defvjp(lambda a, b: (_dot_tn(a, b), (a, b)),
              lambda res, g: (_dot_nt(res[1], g), _dot(res[0], g)))


@jax.custom_vjp
def _cumsum_rows(x):
    return _dot_f32(_tri(x.shape[0], False), x)


_cumsum_rows.defvjp(lambda x: (_cumsum_rows(x), None),
                    lambda _, g: (_dot_f32(_tri(g.shape[0], True), g),))


@functools.partial(jax.custom_vjp, nondiff_argnums=(1,))
def _shift_rows(x, k):
    return pltpu.roll(x, k % x.shape[0], 0)


_shift_rows.defvjp(lambda x, k: (_shift_rows(x, k), None),
                   lambda k, _, g: (pltpu.roll(g, (-k) % g.shape[0], 0),))


def _group_ref(bc, m):
    n = bc.shape[0] // (2 * m)
    b3 = bc.reshape(n, 2 * m, ADH)
    row = lax.broadcasted_iota(jnp.int32, b3.shape, 1)
    ref = jnp.sum(jnp.where(row == m - 1, b3, 0.0), axis=1, keepdims=True)
    return jnp.broadcast_to(ref, b3.shape).reshape(bc.shape)


def _hgrn_block(q, fl, v, g, st, lb, nw):
    n = q.shape[0]
    f = lb + (1.0 - lb) * jax.nn.sigmoid(fl)
    kk = 1.0 - f
    lf = jnp.log(f)
    bc = _cumsum_rows(lf)
    row = lax.broadcasted_iota(jnp.int32, (n, ADH), 0)
    same = jnp.bitwise_xor(lax.broadcasted_iota(jnp.int32, (n, n), 0), lax.broadcasted_iota(jnp.int32, (n, n), 1))
    a = jnp.zeros((n, n), F32)
    m = 1
    while m < n:
        r = jnp.bitwise_and(row, 2 * m - 1)
        up, lo = r >= m, r < m
        if m == 1:
            aq, ak = lf, jnp.zeros_like(lf)
        elif m == 2:
            aq = jnp.where(r == 3, lf + _shift_rows(lf, 1), lf)
            ak = jnp.where(r == 0, _shift_rows(lf, -1), 0.0)
        else:
            ref = _group_ref(bc, m)
            aq, ak = bc - ref, ref - bc
        qt = jnp.where(up, q * jnp.exp(jnp.where(up, aq, 0.0)), 0.0)
        kt = jnp.where(lo, kk * jnp.exp(jnp.where(lo, ak, 0.0)), 0.0)
        a = a + jnp.where(same < 2 * m, _mm_nt(qt, kt), 0.0)
        m *= 2
    last = row == n - 1
    bl = jnp.sum(jnp.where(last, bc, 0.0), axis=0, keepdims=True)
    o = _mm_nn(a, v) + _mm_nt(q * jnp.exp(bc), st) + jnp.sum(q * kk, axis=-1, keepdims=True) * v
    st_new = st * jnp.exp(bl) + _mm_tn(v, kk * jnp.exp(bl - bc))
    rms = lax.rsqrt(jnp.mean(o * o, axis=-1, keepdims=True) + RMS_EPS)
    return o * rms * nw * jax.nn.sigmoid(g), st_new


def _hgrn_fwd(proj, lb_logits, norm_w, nbatch, seq):
    t = proj.shape[0]
    blk = min(HGRN_BLOCK, seq)
    nb = seq // blk

    nh = HGRN_HEADS
    wp, wy = 512 * nh, ADH * nh

    def body(p_ref, lbl_ref, nw_ref, y_ref, ck_ref, st_s):
        @pl.when(pl.program_id(2) == 0)
        def _():
            st_s[...] = jnp.zeros_like(st_s)

        st = [st_s[h] for h in range(nh)]
        p = p_ref[...].astype(F32)
        lb = jax.nn.sigmoid(lbl_ref[0:1, :] - lbl_ref[1:2, :])
        nw = nw_ref[...]
        res = [_hgrn_block(*(p[:, 512 * h + 128 * k:512 * h + 128 * k + 128] for k in range(4)), st[h],
                           lb[:, 128 * h:128 * h + 128], nw[:, 128 * h:128 * h + 128]) for h in range(nh)]
        for h in range(nh):
            ck_ref[0, h] = st[h]
            st_s[h] = res[h][1]
        y_ref[...] = jnp.concatenate([r[0] for r in res], axis=1).astype(y_ref.dtype)

    return pl.pallas_call(
        body, name="hgrn_fwd", grid=(AH // nh, nbatch, nb),
        in_specs=[pl.BlockSpec((blk, wp), lambda h, b, i: (b * nb + i, COL_A // wp + h)),
                  pl.BlockSpec((2, wy), lambda h, b, i: (0, h)),
                  pl.BlockSpec((1, wy), lambda h, b, i: (0, h))],
        out_specs=[pl.BlockSpec((blk, wy), lambda h, b, i: (b * nb + i, h)),
                   pl.BlockSpec((1, nh, 128, 128), lambda h, b, i: ((h * nbatch + b) * nb + i, 0, 0, 0))],
        out_shape=[jax.ShapeDtypeStruct((t, AW), BF16),
                   jax.ShapeDtypeStruct((AH // nh * nbatch * nb, nh, 128, 128), F32)],
        scratch_shapes=[pltpu.VMEM((nh, 128, 128), F32)],
        compiler_params=_params(("parallel", "parallel", "arbitrary"), 48),
    )(proj, lb_logits, norm_w)


def _hgrn_bwd(proj, dya, ckpt, lb_logits, norm_w, dproj, nbatch, seq):
    t = proj.shape[0]
    blk = min(HGRN_BLOCK, seq)
    nb = seq // blk

    nh = HGRN_HEADS
    wp, wy = 512 * nh, ADH * nh

    def body(p_ref, dy_ref, ck_ref, lbl_ref, nw_ref, dp_in, dp_ref, sm_ref, dst_s):
        del dp_in
        b_id, i = pl.program_id(1), pl.program_id(2)

        @pl.when(i == 0)
        def _():
            dst_s[...] = jnp.zeros_like(dst_s)

        dst = [dst_s[h] for h in range(nh)]
        st = [ck_ref[0, h] for h in range(nh)]
        p = p_ref[...].astype(F32)
        dy = dy_ref[...]
        lb = jax.nn.sigmoid(lbl_ref[0:1, :] - lbl_ref[1:2, :])
        nw = nw_ref[...]
        grads = []
        for h in range(nh):
            _, pullback = jax.vjp(_hgrn_block, *(p[:, 512 * h + 128 * k:512 * h + 128 * k + 128] for k in range(4)),
                                  st[h], lb[:, 128 * h:128 * h + 128], nw[:, 128 * h:128 * h + 128])
            grads.append(pullback((dy[:, 128 * h:128 * h + 128], dst[h])))
        for h in range(nh):
            dst_s[h] = grads[h][4]
        dp_ref[...] = jnp.concatenate([g[k] for g in grads for k in range(4)], axis=1).astype(dp_ref.dtype)
        upd = jnp.concatenate([jnp.concatenate([g[5] for g in grads], axis=1),
                               jnp.concatenate([g[6] for g in grads], axis=1), jnp.zeros((6, wy), F32)], axis=0)
        first = (b_id == 0) & (i == 0)

        @pl.when(first)
        def _():
            sm_ref[...] = upd

        @pl.when(jnp.logical_not(first))
        def _():
            sm_ref[...] += upd

    def rows(h, b, i):
        return b * nb + (nb - 1 - i)

    return pl.pallas_call(
        body, name="hgrn_bwd", grid=(AH // nh, nbatch, nb),
        in_specs=[pl.BlockSpec((blk, wp), lambda h, b, i: (rows(h, b, i), COL_A // wp + h)),
                  pl.BlockSpec((blk, wy), lambda h, b, i: (rows(h, b, i), h)),
                  pl.BlockSpec((1, nh, 128, 128), lambda h, b, i: ((h * nbatch + b) * nb + (nb - 1 - i), 0, 0, 0)),
                  pl.BlockSpec((2, wy), lambda h, b, i: (0, h)),
                  pl.BlockSpec((1, wy), lambda h, b, i: (0, h)),
                  pl.BlockSpec(memory_space=pl.ANY)],
        out_specs=[pl.BlockSpec((blk, wp), lambda h, b, i: (rows(h, b, i), COL_A // wp + h)),
                   pl.BlockSpec((8, wy), lambda h, b, i: (0, h))],
        out_shape=[jax.ShapeDtypeStruct((t, NP), BF16), jax.ShapeDtypeStruct((8, AW), F32)],
        input_output_aliases={5: 0},
        scratch_shapes=[pltpu.VMEM((nh, 128, 128), F32)],
        compiler_params=_params(("parallel", "arbitrary", "arbitrary"), 56),
    )(proj, dya, ckpt, lb_logits, norm_w, dproj)


def _log_sigmoid(z):
    return jnp.minimum(z, 0.0) - jnp.log(1.0 + jnp.exp(-jnp.abs(z)))


def _fox_cum(proj, bias128, nbatch, seq):
    t = proj.shape[0]
    ts = min(512, seq)
    nb = seq // ts

    def body(p_ref, b_ref, c_ref, carry):
        @pl.when(pl.program_id(1) == 0)
        def _():
            carry[...] = jnp.zeros_like(carry)
        cum = _dot_f32(_tri(ts, False), _log_sigmoid(p_ref[...] + b_ref[...])) + carry[...]
        carry[...] = cum[ts - 1:ts, :]
        cum2 = cum * LOG2E
        lane = lax.broadcasted_iota(jnp.int32, (ts, 128), 1)
        for p in range(4):
            c_ref[p] = jnp.where(lane < 64, cum2[:, 2 * p:2 * p + 1], cum2[:, 2 * p + 1:2 * p + 2])

    return pl.pallas_call(
        body, name="fox_cum", grid=(nbatch, nb),
        in_specs=[pl.BlockSpec((ts, 128), lambda b, i: (b * nb + i, 0)),
                  pl.BlockSpec((1, 128), lambda b, i: (0, 0))],
        out_specs=pl.BlockSpec((4, ts, 128), lambda b, i: (0, b * nb + i, 0)),
        out_shape=jax.ShapeDtypeStruct((4, t, 128), F32),
        scratch_shapes=[pltpu.VMEM((1, 128), F32)],
        compiler_params=_params(("parallel", "arbitrary")),
    )(proj, bias128)


def _fox_scores_t(q128, k128, cc128, hh, masked):
    tq, tk = q128.shape[0], k128.shape[0]
    qh = _head_lanes((q128 * (LOG2E * BDH ** -0.5)).astype(BF16), hh)
    s = _dot_nt(k128, qh) - cc128[:, 64 * hh:64 * hh + 1]
    if masked:
        key = lax.broadcasted_iota(jnp.int32, (tk, tq), 0)
        qry = lax.broadcasted_iota(jnp.int32, (tk, tq), 1)
        s = jnp.where(key <= qry, s, NEG)
    return s


def _causal_pairs(nq, key_major):
    if key_major:
        pairs = [(i, j) for j in range(nq) for i in range(j, nq)]
    else:
        pairs = [(i, j) for i in range(nq) for j in range(i + 1)]
    return (jnp.asarray([p[0] for p in pairs], jnp.int32), jnp.asarray([p[1] for p in pairs], jnp.int32))


def _head_lanes(x128, hh):
    lane = lax.broadcasted_iota(jnp.int32, x128.shape, 1)
    return jnp.where((lane < 64) if hh == 0 else (lane >= 64), x128, jnp.zeros_like(x128))


def _with_ones_lane(x128, hh):
    lane = lax.broadcasted_iota(jnp.int32, x128.shape, 1)
    one = jnp.ones_like(x128)
    zero = jnp.zeros_like(x128)
    if hh == 0:
        return jnp.where(lane < 64, x128, jnp.where(lane == 64, one, zero))
    return jnp.where(lane >= 64, x128, jnp.where(lane == 0, one, zero))


def _fox_fwd(proj, cum_cols, nbatch, seq):
    t = proj.shape[0]
    tq = tk = min(512, seq)
    nq = seq // tq
    npr = FOX_PAIRS
    qi, kj = _causal_pairs(nq, key_major=False)

    def body(qi_ref, kj_ref, q_ref, kv_ref, cc_ref, o_ref, lse_ref, m_s, acc_s):
        s_id = pl.program_id(2)
        i, j = qi_ref[s_id], kj_ref[s_id]

        @pl.when(j == 0)
        def _():
            m_s[...] = jnp.full_like(m_s, NEG)
            acc_s[...] = jnp.zeros_like(acc_s)

        def step(masked):
            heads = [(pr, hh) for pr in range(npr) for hh in range(2)]
            m_prev = m_s[0:2 * npr, :]
            acc_prev = [acc_s[h] for h in range(2 * npr)]
            q128 = [q_ref[:, 128 * pr:128 * pr + 128] for pr in range(npr)]
            k128 = [kv_ref[:, 256 * pr:256 * pr + 128].astype(BF16) for pr in range(npr)]
            v128 = [kv_ref[:, 256 * pr + 128:256 * pr + 256].astype(BF16) for pr in range(npr)]
            s = [_fox_scores_t(q128[pr], k128[pr], cc_ref[pr], hh, masked) for pr, hh in heads]
            m_new = [jnp.maximum(m_prev[h:h + 1, :], jnp.max(s[h], axis=0, keepdims=True)) for h in range(2 * npr)]
            acc_new = []
            for h, (pr, hh) in enumerate(heads):
                alpha = jnp.exp2(m_prev[h:h + 1, :] - m_new[h])
                p = jnp.exp2(s[h] - m_new[h]).astype(BF16)
                acc_new.append(acc_prev[h] * alpha + _dot_tn(_with_ones_lane(v128[pr], hh), p))
            for h in range(2 * npr):
                acc_s[h] = acc_new[h]
            m_s[0:2 * npr, :] = jnp.concatenate(m_new, axis=0)

        @pl.when(j < i)
        def _():
            step(False)

        @pl.when(j == i)
        def _():
            step(True)
            outs = []
            for pr in range(npr):
                a0, a1 = acc_s[2 * pr], acc_s[2 * pr + 1]
                l0, l1 = a0[64:65, :], a1[0:1, :]
                outs.append(jnp.concatenate([a0[0:64, :] / l0, a1[64:128, :] / l1], axis=0).T)
                lse_ref[0, pr] = jnp.concatenate(
                    [m_s[2 * pr:2 * pr + 1, :] + jnp.log2(l0), m_s[2 * pr + 1:2 * pr + 2, :] + jnp.log2(l1),
                     jnp.zeros((6, tq), F32)], axis=0)
            o_ref[...] = jnp.concatenate(outs, axis=1).astype(o_ref.dtype)

    return pl.pallas_call(
        body, name="fox_fwd",
        grid_spec=pltpu.PrefetchScalarGridSpec(
            num_scalar_prefetch=2, grid=(nbatch, 4 // npr, qi.shape[0]),
            in_specs=[pl.BlockSpec((tq, 128 * npr), lambda b, p, s, qi, kj: (b * nq + qi[s], COL_BQ // (128 * npr) + p)),
                      pl.BlockSpec((tk, 256 * npr), lambda b, p, s, qi, kj: (b * nq + kj[s], COL_KV // (256 * npr) + p)),
                      pl.BlockSpec((npr, tk, 128), lambda b, p, s, qi, kj: (p, b * nq + kj[s], 0))],
            out_specs=[pl.BlockSpec((tq, 128 * npr), lambda b, p, s, qi, kj: (b * nq + qi[s], p)),
                       pl.BlockSpec((1, npr, 8, tq), lambda b, p, s, qi, kj: (b, p, 0, qi[s]))],
            scratch_shapes=[pltpu.VMEM((8, tq), F32), pltpu.VMEM((2 * npr, 128, tq), F32)]),
        out_shape=[jax.ShapeDtypeStruct((t, 512), BF16), jax.ShapeDtypeStruct((nbatch, 4, 8, seq), F32)],
        compiler_params=_params(("parallel", "parallel", "arbitrary"), 56),
    )(qi, kj, proj, proj, cum_cols)


def _fox_bwd(proj, cum_cols, lse, yb, dyb, dproj, nbatch, seq):
    t = proj.shape[0]
    tq = tk = min(512, seq)
    nq = seq // tq
    scale = BDH ** -0.5
    qi, kj = _causal_pairs(nq, key_major=True)
    nsteps = qi.shape[0]

    npr = FOX_PAIRS

    def body(qi_ref, kj_ref, q_ref, kv_ref, cc_ref, lse_ref, o_ref, do_ref, dp_in,
             dkv_ref, dq_ref, drs_ref, dcs_ref, dk_s, dv_s, dqa_s):
        del dp_in
        pg, s_id = pl.program_id(1), pl.program_id(2)
        i, j = qi_ref[s_id], kj_ref[s_id]

        @pl.when(i == j)
        def _():
            dk_s[...] = jnp.zeros_like(dk_s)
            dv_s[...] = jnp.zeros_like(dv_s)

        @pl.when(s_id == 0)
        def _():
            dqa_s[...] = jnp.zeros_like(dqa_s)

        def step(masked):
            dk_prev = [dk_s[h] for h in range(2 * npr)]
            dq_prev = [dqa_s[i, h] for h in range(2 * npr)]
            dv_new = [dv_s[pr] for pr in range(npr)]
            dk_new, dq_new = [], []
            for pr in range(npr):
                lanes = slice(128 * pr, 128 * pr + 128)
                q128 = q_ref[:, lanes]
                qs128 = (q128 * scale).astype(BF16)
                k128 = kv_ref[:, 256 * pr:256 * pr + 128].astype(BF16)
                v128 = kv_ref[:, 256 * pr + 128:256 * pr + 256].astype(BF16)
                do128 = do_ref[:, lanes]
                doo = do128 * o_ref[:, lanes].astype(F32)
                do16 = do128.astype(BF16)
                for hh in range(2):
                    s = _fox_scores_t(q128, k128, cc_ref[pr], hh, masked)
                    p = jnp.exp2(s - lse_ref[0, pr, hh:hh + 1, :])
                    dd = lax.dot_general(jnp.ones((8, 128), F32), _head_lanes(doo, hh), (((1,), (1,)), ((), ())),
                                         preferred_element_type=F32, precision=HIGHEST)[0:1, :]
                    doh = _head_lanes(do16, hh)
                    dp = _dot_nt(v128, doh)
                    ds = (p * (dp - dd)).astype(BF16)
                    dv_new[pr] = dv_new[pr] + _dot(p, doh)
                    dk_new.append(dk_prev[2 * pr + hh] + _dot(ds, _with_ones_lane(qs128, hh)))
                    dq_new.append(dq_prev[2 * pr + hh] + _dot_tn(_with_ones_lane(k128, hh), ds))
            for pr in range(npr):
                dv_s[pr] = dv_new[pr]
            for h in range(2 * npr):
                dk_s[h] = dk_new[h]
                dqa_s[i, h] = dq_new[h]

        @pl.when(i == j)
        def _():
            step(True)

        @pl.when(i > j)
        def _():
            step(False)

        def sums_to_lanes(lane, pr, s0, s1):
            hp = npr * pg + pr
            return jnp.where(lane == 2 * hp, s0, jnp.where(lane == 2 * hp + 1, s1, 0.0))

        @pl.when(i == nq - 1)
        def _():
            lane = lax.broadcasted_iota(jnp.int32, (tk, 128), 1)
            for pr in range(npr):
                k0, k1 = dk_s[2 * pr], dk_s[2 * pr + 1]
                dkv_ref[:, 256 * pr:256 * pr + 128] = jnp.where(lane < 64, k0, k1).astype(dkv_ref.dtype)
                dkv_ref[:, 256 * pr + 128:256 * pr + 256] = dv_s[pr].astype(dkv_ref.dtype)
                dcs_ref[pr] = sums_to_lanes(lane, pr, k0[:, 64:65], k1[:, 0:1])

        @pl.when(s_id == nsteps - 1)
        def _():
            lane = lax.broadcasted_iota(jnp.int32, (tq, 128), 1)
            for blk in range(nq):
                rows = pl.ds(blk * tq, tq)
                for pr in range(npr):
                    a0 = dqa_s[blk, 2 * pr].T
                    a1 = dqa_s[blk, 2 * pr + 1].T
                    dq_ref[rows, 128 * pr:128 * pr + 128] = (jnp.where(lane < 64, a0, a1) * scale).astype(dq_ref.dtype)
                    drs_ref[pr, rows, :] = sums_to_lanes(lane, pr, a0[:, 64:65], a1[:, 0:1])

    return pl.pallas_call(
        body, name="fox_bwd",
        grid_spec=pltpu.PrefetchScalarGridSpec(
            num_scalar_prefetch=2, grid=(nbatch, 4 // npr, nsteps),
            in_specs=[pl.BlockSpec((tq, 128 * npr), lambda b, p, s, qi, kj: (b * nq + qi[s], COL_BQ // (128 * npr) + p)),
                      pl.BlockSpec((tk, 256 * npr), lambda b, p, s, qi, kj: (b * nq + kj[s], COL_KV // (256 * npr) + p)),
                      pl.BlockSpec((npr, tk, 128), lambda b, p, s, qi, kj: (p, b * nq + kj[s], 0)),
                      pl.BlockSpec((1, npr, 8, tq), lambda b, p, s, qi, kj: (b, p, 0, qi[s])),
                      pl.BlockSpec((tq, 128 * npr), lambda b, p, s, qi, kj: (b * nq + qi[s], p)),
                      pl.BlockSpec((tq, 128 * npr), lambda b, p, s, qi, kj: (b * nq + qi[s], p)),
                      pl.BlockSpec(memory_space=pl.ANY)],
            out_specs=[pl.BlockSpec((tk, 256 * npr), lambda b, p, s, qi, kj: (b * nq + kj[s], COL_KV // (256 * npr) + p)),
                       pl.BlockSpec((seq, 128 * npr), lambda b, p, s, qi, kj: (b, p)),
                       pl.BlockSpec((npr, seq, 128), lambda b, p, s, qi, kj: (p, b, 0)),
                       pl.BlockSpec((npr, tk, 128), lambda b, p, s, qi, kj: (p, b * nq + kj[s], 0))],
            scratch_shapes=[pltpu.VMEM((2 * npr, tk, 128), F32), pltpu.VMEM((npr, tk, 128), F32),
                            pltpu.VMEM((nq, 2 * npr, 128, tq), F32)]),
        out_shape=[jax.ShapeDtypeStruct((t, NP), BF16), jax.ShapeDtypeStruct((t, 512), BF16),
                   jax.ShapeDtypeStruct((4, t, 128), F32), jax.ShapeDtypeStruct((4, t, 128), F32)],
        input_output_aliases={8: 0},
        compiler_params=_params(("parallel", "parallel", "arbitrary"), 60),
    )(qi, kj, proj, proj, cum_cols, lse, yb, dyb, dproj)


def _place_cols(dproj, src, col):
    t, w = src.shape
    tm = 1024 if t % 1024 == 0 else t

    def body(s_ref, dp_in, o_ref):
        del dp_in
        o_ref[...] = s_ref[...]

    return pl.pallas_call(
        body, name="place_cols", grid=(t // tm,),
        in_specs=[pl.BlockSpec((tm, w), lambda i: (i, 0)), pl.BlockSpec(memory_space=pl.ANY)],
        out_specs=pl.BlockSpec((tm, w), lambda i: (i, col // w)),
        out_shape=jax.ShapeDtypeStruct(dproj.shape, dproj.dtype),
        input_output_aliases={1: 0},
        compiler_params=_params(("parallel",)),
    )(src, dproj)


def _fox_dbf(proj, bias128, drs, dcs, dproj, nbatch, seq):
    t = proj.shape[0]
    ts = min(512, seq)
    nb = seq // ts

    def body(p_ref, b_ref, dr_ref, dc_ref, dp_in, dp_ref, sm_ref, carry):
        del dp_in
        b_id, i = pl.program_id(0), pl.program_id(1)

        @pl.when(i == 0)
        def _():
            carry[...] = jnp.zeros_like(carry)

        dcum = (dr_ref[0] - dc_ref[0]) + (dr_ref[1] - dc_ref[1]) + (dr_ref[2] - dc_ref[2]) + (dr_ref[3] - dc_ref[3])
        rc = _dot_f32(_tri(ts, True), dcum) + carry[...]
        carry[...] = rc[0:1, :]
        z = p_ref[...] + b_ref[...]
        lane = lax.broadcasted_iota(jnp.int32, (ts, 128), 1)
        dz = jnp.where(lane < BH, rc * jax.nn.sigmoid(-z), 0.0)
        dp_ref[...] = dz.astype(dp_ref.dtype)
        upd = jnp.concatenate([jnp.sum(dz, axis=0, keepdims=True), jnp.zeros((7, 128), F32)], axis=0)
        first = (b_id == 0) & (i == 0)

        @pl.when(first)
        def _():
            sm_ref[...] = upd

        @pl.when(jnp.logical_not(first))
        def _():
            sm_ref[...] += upd

    def rows(b, i):
        return b * nb + (nb - 1 - i)

    return pl.pallas_call(
        body, name="fox_dbf", grid=(nbatch, nb),
        in_specs=[pl.BlockSpec((ts, 128), lambda b, i: (rows(b, i), 0)),
                  pl.BlockSpec((1, 128), lambda b, i: (0, 0)),
                  pl.BlockSpec((4, ts, 128), lambda b, i: (0, rows(b, i), 0)),
                  pl.BlockSpec((4, ts, 128), lambda b, i: (0, rows(b, i), 0)),
                  pl.BlockSpec(memory_space=pl.ANY)],
        out_specs=[pl.BlockSpec((ts, 128), lambda b, i: (rows(b, i), COL_BF // 128)),
                   pl.BlockSpec((8, 128), lambda b, i: (0, 0))],
        out_shape=[jax.ShapeDtypeStruct((t, NP), BF16), jax.ShapeDtypeStruct((8, 128), F32)],
        input_output_aliases={4: 0},
        scratch_shapes=[pltpu.VMEM((1, 128), F32)],
        compiler_params=_params(("arbitrary", "arbitrary")),
    )(proj, bias128, drs, dcs, dproj)


def _ln_stats(z):
    mu = jnp.mean(z, axis=-1, keepdims=True)
    zc = z - mu
    rstd = lax.rsqrt(jnp.mean(zc * zc, axis=-1, keepdims=True) + LN_EPS)
    return zc * rstd, rstd


def _ln_bwd(dy, xhat, rstd, w):
    dxh = dy * w
    return rstd * (dxh - jnp.mean(dxh, axis=-1, keepdims=True) - xhat * jnp.mean(dxh * xhat, axis=-1, keepdims=True))


def _merge_fwd(ya, yb, proj, x2, mod8, wba, wbb, wout, ln1w, ln1b, seq):
    t = x2.shape[0]
    tm = min(512, seq)
    tpb = seq // tm

    def body(ya_ref, yb_ref, g_ref, x_ref, mod_ref, wa_ref, wb_ref, wo_ref, lw_ref, lb_ref, mg_ref, u_ref, x1_ref):
        ga = jax.nn.sigmoid(g_ref[:, 0:D].astype(F32))
        gb = jax.nn.sigmoid(g_ref[:, D:2 * D].astype(F32))
        merged = (ga * jnp.dot(ya_ref[...], wa_ref[...], preferred_element_type=F32)
                  + gb * jnp.dot(yb_ref[...], wb_ref[...], preferred_element_type=F32))
        mg = merged.astype(BF16)
        mg_ref[...] = mg
        u = jnp.dot(mg, wo_ref[...], preferred_element_type=F32)
        u_ref[...] = u.astype(u_ref.dtype)
        xhat, _ = _ln_stats(ALPHA * x_ref[...] + (1.0 + mod_ref[0, 2:3, :]) * u)
        x1_ref[...] = xhat * lw_ref[...] + lb_ref[...]

    tok = lambda w: pl.BlockSpec((tm, w), lambda i: (i, 0))
    full = lambda a: pl.BlockSpec(a.shape, lambda i: (0,) * a.ndim)
    return pl.pallas_call(
        body, name="merge_fwd", grid=(t // tm,),
        in_specs=[tok(512), tok(512), pl.BlockSpec((tm, 2048), lambda i: (i, COL_GATES // 2048)), tok(D),
                  pl.BlockSpec((1, 8, D), lambda i: (i // tpb, 0, 0)),
                  full(wba), full(wbb), full(wout), full(ln1w), full(ln1b)],
        out_specs=[tok(D), tok(D), tok(D)],
        out_shape=[jax.ShapeDtypeStruct((t, D), BF16), jax.ShapeDtypeStruct((t, D), BF16),
                   jax.ShapeDtypeStruct((t, D), F32)],
        compiler_params=_params(("parallel",), 48),
    )(ya, yb, proj, x2, mod8, wba, wbb, wout, ln1w, ln1b)


def _merge_bwd(du, ya, yb, proj, wba, wbb, wout, token, seq):
    t = du.shape[0]
    tm = min(512, seq)

    def body(du_ref, ya_ref, yb_ref, g_ref, wa_ref, wb_ref, wo_ref, token_ref,
             dp_ref, dpa_ref, dpb_ref, dya_ref, dyb_ref):
        del token_ref
        ga = jax.nn.sigmoid(g_ref[:, 0:D].astype(F32))
        gb = jax.nn.sigmoid(g_ref[:, D:2 * D].astype(F32))
        dm = _dot_nt(du_ref[...], wo_ref[...])
        pa = jnp.dot(ya_ref[...], wa_ref[...], preferred_element_type=F32)
        pb = jnp.dot(yb_ref[...], wb_ref[...], preferred_element_type=F32)
        dpa = (dm * ga).astype(BF16)
        dpb = (dm * gb).astype(BF16)
        dpa_ref[...] = dpa
        dpb_ref[...] = dpb
        dp_ref[:, 0:D] = (dm * pa * ga * (1.0 - ga)).astype(BF16)
        dp_ref[:, D:2 * D] = (dm * pb * gb * (1.0 - gb)).astype(BF16)
        dya_ref[...] = _dot_nt(dpa, wa_ref[...])
        dyb_ref[...] = _dot_nt(dpb, wb_ref[...])

    tok = lambda w: pl.BlockSpec((tm, w), lambda i: (i, 0))
    full = lambda a: pl.BlockSpec(a.shape, lambda i: (0,) * a.ndim)
    return pl.pallas_call(
        body, name="merge_bwd", grid=(t // tm,),
        in_specs=[tok(D), tok(512), tok(512), pl.BlockSpec((tm, 2048), lambda i: (i, COL_GATES // 2048)),
                  full(wba), full(wbb), full(wout), full(token)],
        out_specs=[pl.BlockSpec((tm, 2048), lambda i: (i, COL_GATES // 2048)), tok(D), tok(D), tok(512), tok(512)],
        out_shape=[jax.ShapeDtypeStruct((t, NP), BF16), jax.ShapeDtypeStruct((t, D), BF16),
                   jax.ShapeDtypeStruct((t, D), BF16), jax.ShapeDtypeStruct((t, 512), F32),
                   jax.ShapeDtypeStruct((t, 512), F32)],
        compiler_params=_params(("parallel",), 48),
    )(du, ya, yb, proj, wba, wbb, wout, token)


def _ffn_fwd(x1, mod8, wg, wu, wd, target, ln2w, ln2b, seq):
    t = x1.shape[0]
    tm = min(FFN_TOKENS, seq)
    nf, tf, _ = wg.shape
    tpb = seq // tm
    nbatch = t // seq

    def body(x_ref, mod_ref, wg_ref, wu_ref, wd_ref, t_ref, lw_ref, lb_ref,
             a_ref, b_ref, h_s, dz_ref, st_ref, dm_ref, acc):
        i, j = pl.program_id(0), pl.program_id(1)

        @pl.when(j == 0)
        def _():
            h_s[...] = (x_ref[...] * (1.0 + mod_ref[0, 4:5, :]) + mod_ref[0, 3:4, :]).astype(BF16)
            acc[...] = jnp.zeros_like(acc)

        a = _dot_nt(h_s[...], wg_ref[0])
        b = _dot_nt(h_s[...], wu_ref[0])
        a_ref[0] = a.astype(BF16)
        b_ref[0] = b.astype(BF16)
        acc[...] += _dot(a * jax.nn.sigmoid(a) * b, wd_ref[0])

        @pl.when(j == nf - 1)
        def _():
            ffn = acc[...]
            xhat, rstd = _ln_stats(ALPHA * x_ref[...] + (1.0 + mod_ref[0, 5:6, :]) * ffn)
            diff = xhat * lw_ref[...] + lb_ref[...] - t_ref[...]
            loss = 0.5 * jnp.sum(jnp.sum(diff * diff, axis=-1, keepdims=True), axis=0, keepdims=True) / D
            dy = diff * (1.0 / D)
            dz = _ln_bwd(dy, xhat, rstd, lw_ref[...])
            dz_ref[...] = dz
            lane = lax.broadcasted_iota(jnp.int32, (1, D), 1)
            upd = jnp.concatenate(
                [jnp.sum(dy * xhat, axis=0, keepdims=True), jnp.sum(dy, axis=0, keepdims=True),
                 jnp.where(lane == 0, loss, 0.0), jnp.zeros((5, D), F32)], axis=0)
            dmu = jnp.concatenate(
                [jnp.zeros((5, D), F32), jnp.sum(dz * ffn, axis=0, keepdims=True), jnp.zeros((2, D), F32)], axis=0)

            @pl.when(i == 0)
            def _():
                st_ref[...] = upd

            @pl.when(i > 0)
            def _():
                st_ref[...] += upd

            @pl.when(i % tpb == 0)
            def _():
                dm_ref[0] = dmu

            @pl.when(i % tpb != 0)
            def _():
                dm_ref[0] += dmu

    row = lambda: pl.BlockSpec((tm, D), lambda i, j: (i, 0))
    vec = lambda: pl.BlockSpec((1, D), lambda i, j: (0, 0))
    return pl.pallas_call(
        body, name="ffn_fwd", grid=(t // tm, nf),
        in_specs=[row(), pl.BlockSpec((1, 8, D), lambda i, j: (i // tpb, 0, 0)),
                  pl.BlockSpec((1, tf, D), lambda i, j: (j, 0, 0)), pl.BlockSpec((1, tf, D), lambda i, j: (j, 0, 0)),
                  pl.BlockSpec((1, tf, D), lambda i, j: (j, 0, 0)), row(), vec(), vec()],
        out_specs=[pl.BlockSpec((1, tm, tf), lambda i, j: (j, i, 0)), pl.BlockSpec((1, tm, tf), lambda i, j: (j, i, 0)),
                   row(), row(), pl.BlockSpec((8, D), lambda i, j: (0, 0)),
                   pl.BlockSpec((1, 8, D), lambda i, j: (i // tpb, 0, 0))],
        out_shape=[jax.ShapeDtypeStruct((nf, t, tf), BF16), jax.ShapeDtypeStruct((nf, t, tf), BF16),
                   jax.ShapeDtypeStruct((t, D), BF16),
                   jax.ShapeDtypeStruct((t, D), F32), jax.ShapeDtypeStruct((8, D), F32),
                   jax.ShapeDtypeStruct((nbatch, 8, D), F32)],
        scratch_shapes=[pltpu.VMEM((tm, D), F32)],
        compiler_params=_params(("arbitrary", "arbitrary"), 60),
    )(x1, mod8, wg, wu, wd, target, ln2w, ln2b)


def _ffn_bwd(dz2, a, b, wg, wu, wd, x1, x2, u, mod8, ln1w, seq):
    t = x1.shape[0]
    tm = min(512, seq)
    nf, tf, _ = wg.shape
    tpb = seq // tm
    nbatch = t // seq

    def body(dz_ref, a_ref, b_ref, wg_ref, wu_ref, wd_ref, x1_ref, x_ref, u_ref, mod_ref, lw_ref,
             da_ref, db_ref, hm_ref, df_ref, du_ref, dxp_ref, st_ref, dm_ref, acc):
        i, j = pl.program_id(0), pl.program_id(1)

        @pl.when(j == 0)
        def _():
            df_ref[...] = ((1.0 + mod_ref[0, 5:6, :]) * dz_ref[...]).astype(BF16)
            acc[...] = jnp.zeros_like(acc)

        dhm = _dot_nt(df_ref[...], wd_ref[0])
        av = a_ref[0].astype(F32)
        bv = b_ref[0].astype(F32)
        sg = jax.nn.sigmoid(av)
        sl = av * sg
        hm_ref[0] = (sl * bv).astype(BF16)
        da = (dhm * bv * (sg * (1.0 + av * (1.0 - sg)))).astype(BF16)
        db = (dhm * sl).astype(BF16)
        da_ref[0] = da
        db_ref[0] = db
        acc[...] += _dot(da, wg_ref[0]) + _dot(db, wu_ref[0])

        @pl.when(j == nf - 1)
        def _():
            dh2 = acc[...]
            x1v = x1_ref[...]
            uv = u_ref[...].astype(F32)
            dx1 = ALPHA * dz_ref[...] + dh2 * (1.0 + mod_ref[0, 4:5, :])
            xhat, rstd = _ln_stats(ALPHA * x_ref[...] + (1.0 + mod_ref[0, 2:3, :]) * uv)
            dz1 = _ln_bwd(dx1, xhat, rstd, lw_ref[...])
            du_ref[...] = ((1.0 + mod_ref[0, 2:3, :]) * dz1).astype(BF16)
            dxp_ref[...] = (ALPHA * dz1).astype(dxp_ref.dtype)
            upd = jnp.concatenate(
                [jnp.sum(dx1 * xhat, axis=0, keepdims=True), jnp.sum(dx1, axis=0, keepdims=True),
                 jnp.zeros((6, D), F32)], axis=0)
            dmu = jnp.concatenate(
                [jnp.zeros((2, D), F32), jnp.sum(dz1 * uv, axis=0, keepdims=True),
                 jnp.sum(dh2, axis=0, keepdims=True), jnp.sum(dh2 * x1v, axis=0, keepdims=True),
                 jnp.zeros((3, D), F32)], axis=0)

            @pl.when(i == 0)
            def _():
                st_ref[...] = upd

            @pl.when(i > 0)
            def _():
                st_ref[...] += upd

            @pl.when(i % tpb == 0)
            def _():
                dm_ref[0] = dmu

            @pl.when(i % tpb != 0)
            def _():
                dm_ref[0] += dmu

    row = lambda: pl.BlockSpec((tm, D), lambda i, j: (i, 0))
    ffb = lambda: pl.BlockSpec((1, tm, tf), lambda i, j: (j, i, 0))
    return pl.pallas_call(
        body, name="ffn_bwd", grid=(t // tm, nf),
        in_specs=[row(), ffb(), ffb(),
                  pl.BlockSpec((1, tf, D), lambda i, j: (j, 0, 0)), pl.BlockSpec((1, tf, D), lambda i, j: (j, 0, 0)),
                  pl.BlockSpec((1, tf, D), lambda i, j: (j, 0, 0)), row(), row(), row(),
                  pl.BlockSpec((1, 8, D), lambda i, j: (i // tpb, 0, 0)), pl.BlockSpec((1, D), lambda i, j: (0, 0))],
        out_specs=[ffb(), ffb(), ffb(), row(), row(), row(), pl.BlockSpec((8, D), lambda i, j: (0, 0)),
                   pl.BlockSpec((1, 8, D), lambda i, j: (i // tpb, 0, 0))],
        out_shape=[jax.ShapeDtypeStruct((nf, t, tf), BF16), jax.ShapeDtypeStruct((nf, t, tf), BF16),
                   jax.ShapeDtypeStruct((nf, t, tf), BF16), jax.ShapeDtypeStruct((t, D), BF16),
                   jax.ShapeDtypeStruct((t, D), BF16), jax.ShapeDtypeStruct((t, D), BF16),
                   jax.ShapeDtypeStruct((8, D), F32), jax.ShapeDtypeStruct((nbatch, 8, D), F32)],
        scratch_shapes=[pltpu.VMEM((tm, D), F32)],
        compiler_params=_params(("arbitrary", "arbitrary"), 60),
    )(dz2, a, b, wg, wu, wd, x1, x2, u, mod8, ln1w)


def _adamw_math(w, g, m, v):
    m = B1 * m + (1.0 - B1) * g
    v = B2 * v + (1.0 - B2) * (g * g)
    m_hat = m / (1.0 - B1 ** STEP)
    v_hat = v / (1.0 - B2 ** STEP)
    return -LR * (m_hat / (jnp.sqrt(v_hat) + EPS) + WD * w), m, v


def _adamw(w, g, m, v, name):
    rows, cols = w.shape
    tr = rows
    for cand in (128, 64, 32, 16, 8):
        if rows % cand == 0:
            tr = cand
            break

    def body(w_ref, g_ref, m_ref, v_ref, d_ref, mo_ref, vo_ref):
        d, mn, vn = _adamw_math(w_ref[...], g_ref[...], m_ref[...], v_ref[...])
        d_ref[...] = d
        mo_ref[...] = mn
        vo_ref[...] = vn

    spec = pl.BlockSpec((tr, cols), lambda i: (i, 0))
    return pl.pallas_call(
        body, name=name, grid=(rows // tr,), in_specs=[spec] * 4, out_specs=[spec] * 3,
        out_shape=[jax.ShapeDtypeStruct((rows, cols), F32)] * 3,
        compiler_params=_params(("parallel",), 48),
    )(w, g, m, v)


def _adamw_halves(w, g_mine, g_sib, m, v, c_idx, name):
    rows, cols = w.shape
    hr = rows // 2
    tr = next(cand for cand in (128, 88, 64, 32, 16, 8) if hr % cand == 0)
    tph = hr // tr

    def body(c_ref, w_ref, gm_ref, gs_ref, m_ref, v_ref, g_ref, d_ref, mo_ref, vo_ref):
        g = jnp.where(pl.program_id(0) == c_ref[0], gm_ref[...], gs_ref[...])
        d, mn, vn = _adamw_math(w_ref[...], g, m_ref[...], v_ref[...])
        g_ref[...] = g
        d_ref[...] = d
        mo_ref[...] = mn
        vo_ref[...] = vn

    full = pl.BlockSpec((tr, cols), lambda h, i, c: (h * tph + i, 0))
    half = pl.BlockSpec((tr, cols), lambda h, i, c: (i, 0))
    return pl.pallas_call(
        body, name=name,
        grid_spec=pltpu.PrefetchScalarGridSpec(
            num_scalar_prefetch=1, grid=(2, tph), in_specs=[full, half, half, full, full], out_specs=[full] * 4),
        out_shape=[jax.ShapeDtypeStruct((rows, cols), F32)] * 4,
        compiler_params=_params(("parallel", "parallel"), 48),
    )(c_idx, w, g_mine, g_sib, m, v)


def _grad_w_ada(c_all, dmod_cols):
    def body(c_ref, d_ref, o_ref):
        c = c_ref[...]
        o_ref[...] = lax.dot_general(c * jax.nn.sigmoid(c), d_ref[...], (((0,), (0,)), ((), ())),
                                     preferred_element_type=F32, precision=HIGHEST)

    return pl.pallas_call(
        body, name="grad_w_ada", out_shape=jax.ShapeDtypeStruct((D, dmod_cols.shape[1]), F32),
        compiler_params=_params(vmem_mb=48),
    )(c_all, dmod_cols)


def _small_update(gath, w8, m8, v8):
    def body(g_ref, w_ref, m_ref, v_ref, go_ref, d_ref, mo_ref, vo_ref):
        g0 = g_ref[0, 0:1, :] + g_ref[0, 1:2, :]
        g1 = g_ref[0, 2:3, :]
        for dev in range(1, N_DEV):
            g0 = g0 + (g_ref[dev, 0:1, :] + g_ref[dev, 1:2, :])
            g1 = g1 + g_ref[dev, 2:3, :]
        w = w_ref[...]
        lb = jax.nn.sigmoid(w[1:2, O_LB0:O_LB1] - w[1:2, O_LB1:O_FOX])
        fac = lb * (1.0 - lb)
        g1 = jnp.concatenate([g1[:, :O_LB0], g1[:, O_LB0:O_LB1] * fac, -g1[:, O_LB1:O_FOX] * fac, g1[:, O_FOX:]],
                             axis=1)
        g = jnp.concatenate([g0, g1, jnp.zeros((6, SMALL_W), F32)], axis=0)
        d, mn, vn = _adamw_math(w, g, m_ref[...], v_ref[...])
        go_ref[...] = g
        d_ref[...] = d
        mo_ref[...] = mn
        vo_ref[...] = vn

    return pl.pallas_call(
        body, name="small_update", out_shape=[jax.ShapeDtypeStruct((8, SMALL_W), F32)] * 4,
        compiler_params=_params(vmem_mb=48),
    )(gath, w8, m8, v8)


def _pack_small(b_ada, ln1w, ln1b, ln2w, ln2b, norm_w, lb_logits, fox):
    row1 = jnp.concatenate([ln1w, ln1b, ln2w, ln2b, norm_w, lb_logits[0:1], lb_logits[1:2], fox,
                            jnp.zeros((1, SMALL_W - O_FOX - BH), F32)], axis=1)
    return jnp.concatenate([b_ada, row1, jnp.zeros((6, SMALL_W), F32)], axis=0)


def _unpack_small(p):
    r = p[1:2]
    lb = jnp.concatenate([r[:, O_LB0:O_LB1], r[:, O_LB1:O_FOX]], axis=0)
    return dict(b_ada=p[0:1], ln1_w=r[:, O_LN1W:O_LN1B], ln1_b=r[:, O_LN1B:O_LN2W], ln2_w=r[:, O_LN2W:O_LN2B],
                ln2_b=r[:, O_LN2B:O_NORM], hgrn_norm_w=r[:, O_NORM:O_LB0], lb_logits=lb,
                fox_f_bias=r[:, O_FOX:O_FOX + BH])


_BIG = ("w_in", "w_branch_a", "w_branch_b", "w_out", "w_ffn_gate", "w_ffn_up", "w_ffn_down")
_TRANSPOSED = ("w_ffn_gate", "w_ffn_up")


def _cols_of_chips(stacked):
    return jnp.concatenate([stacked[k] for k in range(N_CHIPS)], axis=1)


def kernel(x, c, w_ada, b_ada, w_in, fox_f_bias, lb_logits, hgrn_norm_w, w_branch_a, w_branch_b, w_out, ln1_w, ln1_b, w_ffn_gate, w_ffn_up, w_ffn_down, ln2_w, ln2_b, loss_target, m_w_ada, m_b_ada, m_w_in, m_fox_f_bias, m_lb_logits, m_hgrn_norm_w, m_w_branch_a, m_w_branch_b, m_w_out, m_ln1_w, m_ln1_b, m_w_ffn_gate, m_w_ffn_up, m_w_ffn_down, m_ln2_w, m_ln2_b, v_w_ada, v_b_ada, v_w_in, v_fox_f_bias, v_lb_logits, v_hgrn_norm_w, v_w_branch_a, v_w_branch_b, v_w_out, v_ln1_w, v_ln1_b, v_w_ffn_gate, v_w_ffn_up, v_w_ffn_down, v_ln2_w, v_ln2_b):
    nbatch, seq, _ = x.shape
    t = nbatch * seq
    ax, ay, ac = lax.axis_index("x"), lax.axis_index("y"), lax.axis_index("c")
    chip = 2 * ax + ay
    dev = 2 * chip + ac
    chip_arr = jnp.reshape(chip, (1,)).astype(jnp.int32)
    core_arr = jnp.reshape(ac, (1,)).astype(jnp.int32)

    tr = lambda a: jnp.swapaxes(a[0], 0, 1)
    shard_w = dict(w_in=w_in[0], w_branch_a=w_branch_a[0], w_branch_b=w_branch_b[0], w_out=w_out[0],
                   w_ffn_gate=tr(w_ffn_gate), w_ffn_up=tr(w_ffn_up), w_ffn_down=w_ffn_down[0])
    shard_m = dict(w_in=m_w_in[0], w_branch_a=m_w_branch_a[0], w_branch_b=m_w_branch_b[0], w_out=m_w_out[0],
                   w_ffn_gate=tr(m_w_ffn_gate), w_ffn_up=tr(m_w_ffn_up), w_ffn_down=m_w_ffn_down[0])
    shard_v = dict(w_in=v_w_in[0], w_branch_a=v_w_branch_a[0], w_branch_b=v_w_branch_b[0], w_out=v_w_out[0],
                   w_ffn_gate=tr(v_w_ffn_gate), w_ffn_up=tr(v_w_ffn_up), w_ffn_down=v_w_ffn_down[0])

    shard16 = {n: shard_w[n].astype(BF16) for n in _BIG}

    def with_mine(gathered, n):
        return lax.dynamic_update_slice(gathered, shard16[n][None], (chip, 0, 0))

    def gather_start(names, tag):
        return _split_start(_gather_copies, [shard16[n] for n in names],
                            [lax.empty((N_CHIPS,) + shard16[n].shape, BF16) for n in names], "gather_" + tag + "_start")

    def gather_finish(split, after, tag):
        send, recv, src, land, _ = split
        return _pass_to_sibling(_split_wait(_gather_copies, send, recv, src, land, after, "gather_" + tag + "_wait"),
                                "gather_" + tag + "_pass")

    late = _BIG[1:]
    first_split = gather_start(("w_in",), "first")
    late_split = gather_start(late, "late")
    late_token = first_split[4] + late_split[4]

    c8 = jnp.concatenate([c, jnp.zeros((8 - nbatch, D), F32)], axis=0)
    c_all = _allgather8(c8, "gather_c")[:, :nbatch, :].reshape(N_DEV * nbatch, D)
    ncol = w_ada.shape[2]
    b_cols = lax.dynamic_slice_in_dim(b_ada, chip * ncol, ncol, axis=1)
    mod_g = _allgather8(_mod_shard(c_all, w_ada[0], b_cols), "gather_mod")
    mod_all = jnp.concatenate([mod_g[2 * k] for k in range(N_CHIPS)], axis=1)
    mod_mine = lax.dynamic_slice_in_dim(mod_all, dev * nbatch, nbatch, axis=0)
    mod8 = jnp.concatenate([mod_mine.reshape(nbatch, 6, D), jnp.zeros((nbatch, 2, D), F32)], axis=1)
    mod8 = mod8 + late_token[0, 0]
    w_p = _permute_cols(_cols_of_chips(with_mine(gather_finish(first_split, mod8, "first")[0], "w_in")))

    x2 = x.reshape(t, D)
    tgt2 = loss_target.reshape(t, D)
    bias128 = jnp.concatenate([fox_f_bias, jnp.zeros((1, 128 - BH), F32)], axis=1)

    proj, h16 = _proj(x2, mod8, w_p, seq, BF16, "proj")
    projf = _rows_matmul(h16, w_p[:, COL_BF:], "proj_forget")
    ya, ckpt = _hgrn_fwd(proj, lb_logits, hgrn_norm_w, nbatch, seq)
    cum_cols = _fox_cum(projf, bias128, nbatch, seq)
    yb, lse = _fox_fwd(proj, cum_cols, nbatch, seq)
    full = {n: with_mine(g, n) for n, g in zip(late, gather_finish(late_split, yb, "late"))}
    wba, wbb = _cols_of_chips(full["w_branch_a"]), _cols_of_chips(full["w_branch_b"])
    wout = full["w_out"].reshape(D, D)
    wg_t, wu_t, wd = full["w_ffn_gate"], full["w_ffn_up"], full["w_ffn_down"]
    merged, u, x1 = _merge_fwd(ya, yb, proj, x2, mod8, wba, wbb, wout, ln1_w, ln1_b, seq)
    a_pre, b_pre, h2, dz2, st2, dm2 = _ffn_fwd(x1, mod8, wg_t, wu_t, wd, tgt2, ln2_w, ln2_b, seq)
    loss = lax.psum(st2[2, 0], ("x", "y", "c"))

    da, db, hmid, dffn, du, dxp, st1, dm1 = _ffn_bwd(dz2, a_pre, b_pre, wg_t, wu_t, wd, x1, x2, u, mod8, ln1_w, seq)
    g_st = {}
    g_st["w_ffn_down"] = _tn_matmul(hmid, dffn, "dw_ffn_down", seq)
    g_st["w_ffn_gate"] = _tn_matmul(da, h2, "dw_ffn_gate", seq)
    g_st["w_ffn_up"] = _tn_matmul(db, h2, "dw_ffn_up", seq)
    g_st["w_out"] = _tn_matmul(merged, du, "dw_out", seq).reshape(N_CHIPS, D // N_CHIPS, D)

    def sum_over_cores(names, tag):
        g_list = [g_st[n] for n in names]
        return [_add_my_half(g, o, core_arr, "grad_add_halves_" + n)
                for n, g, o in zip(names, g_list, _swap_halves(g_list, "grad_swap_halves_" + tag))]

    early = ("w_ffn_down", "w_ffn_gate", "w_ffn_up", "w_out")
    e_halves = sum_over_cores(early, "early")
    e_send, e_recv, e_src, e_land, e_token = _split_start(
        _scatter_copies, [h16 for _, h16 in e_halves],
        [lax.empty((3,) + h16.shape[1:], BF16) for _, h16 in e_halves], "grad_scatter_early_start")
    dproj, dpa, dpb, dya, dyb = _merge_bwd(du, ya, yb, proj, wba, wbb, wout, e_token, seq)
    g_st["w_branch_a"] = _tn_matmul(ya, dpa, "dw_branch_a", seq, split=D // N_CHIPS)
    g_st["w_branch_b"] = _tn_matmul(yb, dpb, "dw_branch_b", seq, split=D // N_CHIPS)
    dproj, dq, drs, dcs = _fox_bwd(proj, cum_cols, lse, yb, dyb, dproj, nbatch, seq)
    dproj = _place_cols(dproj, dq, COL_BQ)
    dproj, sm_fox = _fox_dbf(projf, bias128, drs, dcs, dproj, nbatch, seq)
    dproj, sm_hgrn = _hgrn_bwd(proj, dya, ckpt, lb_logits, hgrn_norm_w, dproj, nbatch, seq)
    grad_x2, dm0 = _dh_kernel(dproj, w_p, x2, dxp, mod8, seq)
    dw_in = _tn_matmul(h16, dproj, "dw_in", seq)
    g_st["w_in"] = _unpermute_to_chips(dw_in)

    e_recv = _split_wait(_scatter_copies, e_send, e_recv, e_src, e_land, dw_in, "grad_scatter_early_wait")
    rest = ("w_in", "w_branch_a", "w_branch_b")
    r_halves = sum_over_cores(rest, "rest")
    r_send, r_rcv, r_src, r_land, r_token = _split_start(
        _scatter_copies, [h16 for _, h16 in r_halves],
        [lax.empty((3,) + h16.shape[1:], BF16) for _, h16 in r_halves], "grad_scatter_rest_start")

    def finish(names, halves, recv, token, tag):
        g_mine = [_add_chips(h32, r, chip_arr, "grad_add_chips_" + n) for n, (h32, _), r in zip(names, halves, recv)]
        g_sib = _join_halves(g_mine, token, "grad_join_halves_" + tag)
        for n, gm, gs in zip(names, g_mine, g_sib):
            grads[n], deltas[n], new_m[n], new_v[n] = _adamw_halves(
                shard_w[n], gm, gs, shard_m[n], shard_v[n], core_arr, "adamw_" + n)

    grads, deltas, new_m, new_v = {}, {}, {}, {}
    finish(early, e_halves, e_recv, r_token, "early")

    dmod = (dm0 + dm1 + dm2)[:, :6, :].reshape(nbatch, 6 * D)
    row2 = jnp.concatenate([st1[0:1], st1[1:2], st2[0:1], st2[1:2], sm_hgrn[1:2], sm_hgrn[0:1], sm_hgrn[0:1],
                            sm_fox[0:1, :BH], jnp.zeros((1, SMALL_W - O_FOX - BH), F32)], axis=1)
    spack = jnp.concatenate([dmod, row2, jnp.zeros((8 - nbatch - 1, SMALL_W), F32)], axis=0)
    spack = spack + r_token[0, 0]
    gath = _allgather8(spack, "gather_small")
    w8 = _pack_small(b_ada, ln1_w, ln1_b, ln2_w, ln2_b, hgrn_norm_w, lb_logits, fox_f_bias)
    m8 = _pack_small(m_b_ada, m_ln1_w, m_ln1_b, m_ln2_w, m_ln2_b, m_hgrn_norm_w, m_lb_logits, m_fox_f_bias)
    v8 = _pack_small(v_b_ada, v_ln1_w, v_ln1_b, v_ln2_w, v_ln2_b, v_hgrn_norm_w, v_lb_logits, v_fox_f_bias)
    sg, sd, smn, svn = (_unpack_small(p) for p in _small_update(gath, w8, m8, v8))
    dmod_all = gath[:, :nbatch, :].reshape(N_DEV * nbatch, SMALL_W)
    g_ada = _grad_w_ada(c_all, lax.dynamic_slice_in_dim(dmod_all, chip * ncol, ncol, axis=1))

    for group, small in zip((grads, deltas, new_m, new_v), (sg, sd, smn, svn)):
        group.update(small)
    grads["w_ada"] = g_ada
    deltas["w_ada"], new_m["w_ada"], new_v["w_ada"] = _adamw(w_ada[0], g_ada, m_w_ada[0], v_w_ada[0], "adamw_w_ada")
    done = sum(new_v[n][0:8, 0:128] for n in early) + new_v["w_ada"][0:8, 0:128]
    r_recv = _split_wait(_scatter_copies, r_send, r_rcv, r_src, r_land, done, "grad_scatter_rest_wait")
    finish(rest, r_halves, r_recv, late_token, "rest")

    names = ["w_ada", "b_ada", "w_in", "fox_f_bias", "lb_logits", "hgrn_norm_w", "w_branch_a", "w_branch_b", "w_out",
             "ln1_w", "ln1_b", "w_ffn_gate", "w_ffn_up", "w_ffn_down", "ln2_w", "ln2_b"]
    shapes = dict(w_ada=w_ada.shape, b_ada=b_ada.shape, w_in=w_in.shape, fox_f_bias=fox_f_bias.shape,
                  lb_logits=lb_logits.shape, hgrn_norm_w=hgrn_norm_w.shape, w_branch_a=w_branch_a.shape,
                  w_branch_b=w_branch_b.shape, w_out=w_out.shape, ln1_w=ln1_w.shape, ln1_b=ln1_b.shape,
                  w_ffn_gate=w_ffn_gate.shape, w_ffn_up=w_ffn_up.shape, w_ffn_down=w_ffn_down.shape,
                  ln2_w=ln2_w.shape, ln2_b=ln2_b.shape)
    outs = [loss, grad_x2.reshape(x.shape)]
    for group in (grads, deltas, new_m, new_v):
        outs += [(jnp.swapaxes(group[n], 0, 1) if n in _TRANSPOSED else group[n]).reshape(shapes[n]) for n in names]
    return tuple(outs)
```

```python
import functools

import jax
import jax.numpy as jnp
from jax import lax
from jax.experimental import pallas as pl
from jax.experimental.pallas import tpu as pltpu

F32 = jnp.float32
BF16 = jnp.bfloat16
MESH = pl.DeviceIdType.MESH
HIGHEST = lax.Precision.HIGHEST

D = 1024
AW = 512
AH = 4
ADH = 128
BH = 8
BDH = 64
DFF = 2816
NIN = 5640
NP = 5760
N_CHIPS = 4
N_DEV = 8
HGRN_BLOCK = 256
FFN_TOKENS = 512
COL_GATES = 0
COL_A = 2048
COL_BQ = 4096
COL_KV = 4608
COL_BF = 5632
HGRN_HEADS = 4
FOX_PAIRS = 2
ALPHA = 2.0 ** 0.25
LN_EPS = 1e-5
RMS_EPS = 1e-6
NEG = -1e30
LOG2E = 1.4426950408889634
LR, B1, B2, EPS, WD, STEP = 0.001, 0.9, 0.999, 1e-08, 0.01, 10
SMALL_W = 6144
O_LN1W, O_LN1B, O_LN2W, O_LN2B, O_NORM, O_LB0, O_LB1, O_FOX = 0, 1024, 2048, 3072, 4096, 4608, 5120, 5632


def _params(sem=None, vmem_mb=None):
    kw = {}
    if sem is not None:
        kw["dimension_semantics"] = sem
    if vmem_mb is not None:
        kw["vmem_limit_bytes"] = vmem_mb << 20
    return pltpu.CompilerParams(**kw)


def _dot(a, b):
    return jnp.dot(a.astype(BF16), b.astype(BF16), preferred_element_type=F32)


def _dot_nt(a, b):
    return lax.dot_general(a.astype(BF16), b.astype(BF16), (((1,), (1,)), ((), ())), preferred_element_type=F32)


def _dot_tn(a, b):
    return lax.dot_general(a.astype(BF16), b.astype(BF16), (((0,), (0,)), ((), ())), preferred_element_type=F32)


def _dot_f32(a, b):
    return jnp.dot(a, b, preferred_element_type=F32, precision=HIGHEST)


def _perm_segments():
    segs = [(3592, 5640)]
    for h in range(4):
        segs += [(128 * h + 512 * t, 128 * h + 512 * t + 128) for t in range(4)]
    segs += [(2048, 2560)]
    for p in range(4):
        segs += [(2560 + 128 * p, 2688 + 128 * p), (3072 + 128 * p, 3200 + 128 * p)]
    segs += [(3584, 3592)]
    return segs


def _permute_cols(w):
    parts = [w[:, a:b] for a, b in _perm_segments()]
    parts.append(jnp.zeros((w.shape[0], NP - NIN), w.dtype))
    return jnp.concatenate(parts, axis=1)


def _unpermute_to_chips(g):
    pos, where = 0, []
    for a, b in _perm_segments():
        where.append((a, b, pos))
        pos += b - a
    ncin = NIN // N_CHIPS
    out = []
    for k in range(N_CHIPS):
        lo, hi = k * ncin, (k + 1) * ncin
        parts = [g[:, p + max(a, lo) - a:p + min(b, hi) - a] for a, b, p in sorted(where) if max(a, lo) < min(b, hi)]
        out.append(jnp.concatenate(parts, axis=1))
    return jnp.stack(out)


def _allgather8(v, name):
    rows, cols = v.shape

    def body(x_ref, out_ref, send_sems, recv_sems, local_sem):
        x, y, c = lax.axis_index("x"), lax.axis_index("y"), lax.axis_index("c")
        me, sibling = (x, y, c), (x, y, 1 - c)
        chips = [(1 - x, y), (x, 1 - y), (1 - x, 1 - y)]

        def slot(px, py, pc):
            return out_ref.at[4 * px + 2 * py + pc]

        def copy(k, block, to, src=None):
            return pltpu.make_async_remote_copy(
                src_ref=slot(*block) if src is None else src, dst_ref=slot(*block),
                send_sem=send_sems.at[k], recv_sem=recv_sems.at[k], device_id=to, device_id_type=MESH)

        mine = pltpu.make_async_copy(x_ref, slot(*me), local_sem)
        mine.start()
        first = [copy(0, me, sibling, src=x_ref)]
        first += [copy(1 + j, me, (*chip, c), src=x_ref) for j, chip in enumerate(chips)]
        for cp in first:
            cp.start()
        passed = [copy(4 + j, (*chip, c), sibling) for j, chip in enumerate(chips)]
        for j, chip in enumerate(chips):
            copy(1 + j, (*chip, c), me).wait_recv()
            passed[j].start()
        copy(0, sibling, me).wait_recv()
        for j, chip in enumerate(chips):
            copy(4 + j, (*chip, 1 - c), me).wait_recv()
        for cp in first + passed:
            cp.wait_send()
        mine.wait()

    return pl.pallas_call(
        body, name=name,
        out_shape=jax.ShapeDtypeStruct((N_DEV, rows, cols), v.dtype),
        in_specs=[pl.BlockSpec(memory_space=pltpu.VMEM)],
        out_specs=pl.BlockSpec(memory_space=pltpu.VMEM),
        scratch_shapes=[pltpu.SemaphoreType.DMA((7,)), pltpu.SemaphoreType.DMA((7,)), pltpu.SemaphoreType.DMA],
    )(v)


def _hbm_specs(n):
    return [pl.BlockSpec(memory_space=pl.ANY)] * n


def _swap_halves(grads, name):
    n = len(grads)

    def body(*refs):
        ins, outs, (send_sems, recv_sems) = refs[:n], refs[n:2 * n], refs[2 * n:]
        x, y, c = lax.axis_index("x"), lax.axis_index("y"), lax.axis_index("c")
        cps = []
        for w in range(n):
            hr = ins[w].shape[1] // 2
            cps.append(pltpu.make_async_remote_copy(
                src_ref=ins[w].at[:, pl.ds((1 - c) * hr, hr), :], dst_ref=outs[w],
                send_sem=send_sems.at[w], recv_sem=recv_sems.at[w], device_id=(x, y, 1 - c), device_id_type=MESH))
        for cp in cps:
            cp.start()
        for cp in cps:
            cp.wait()

    return pl.pallas_call(
        body, name=name,
        out_shape=[jax.ShapeDtypeStruct((N_CHIPS, g.shape[1] // 2, g.shape[2]), g.dtype) for g in grads],
        in_specs=_hbm_specs(n), out_specs=_hbm_specs(n),
        scratch_shapes=[pltpu.SemaphoreType.DMA((n,)), pltpu.SemaphoreType.DMA((n,))],
    )(*grads)


def _join_halves(halves, token, name):
    n = len(halves)

    def body(*refs):
        ins, outs, (send_sems, recv_sems) = refs[:n], refs[n + 1:2 * n + 1], refs[2 * n + 1:]
        x, y, c = lax.axis_index("x"), lax.axis_index("y"), lax.axis_index("c")
        cps = [pltpu.make_async_remote_copy(
            src_ref=ins[w], dst_ref=outs[w], send_sem=send_sems.at[w], recv_sem=recv_sems.at[w],
            device_id=(x, y, 1 - c), device_id_type=MESH) for w in range(n)]
        for cp in cps:
            cp.start()
        for cp in cps:
            cp.wait()

    return pl.pallas_call(
        body, name=name,
        out_shape=[jax.ShapeDtypeStruct(h.shape, h.dtype) for h in halves],
        in_specs=_hbm_specs(n + 1), out_specs=_hbm_specs(n),
        scratch_shapes=[pltpu.SemaphoreType.DMA((n,)), pltpu.SemaphoreType.DMA((n,))],
    )(*halves, token)


def _in_hbm(v):
    return pltpu.with_memory_space_constraint(v, pltpu.HBM)


_SPLIT_COPY = pltpu.CompilerParams(has_side_effects=pltpu.SideEffectType.DATAFLOW_SIDE_EFFECTING)


def _gather_copies(srcs, lands, send_sems, recv_sems):
    x, y, c = lax.axis_index("x"), lax.axis_index("y"), lax.axis_index("c")
    cps = []
    for w, (src, land) in enumerate(zip(srcs, lands)):
        hr = src.shape[0] // 2
        for j, chip in enumerate([(1 - x, y), (x, 1 - y), (1 - x, 1 - y)]):
            cps.append(pltpu.make_async_remote_copy(
                src_ref=src.at[pl.ds(c * hr, hr), :], dst_ref=land.at[2 * x + y, pl.ds(c * hr, hr), :],
                send_sem=send_sems.at[3 * w + j], recv_sem=recv_sems.at[3 * w + j],
                device_id=(*chip, c), device_id_type=MESH))
    return cps


def _scatter_copies(srcs, lands, send_sems, recv_sems):
    x, y, c = lax.axis_index("x"), lax.axis_index("y"), lax.axis_index("c")
    cps = []
    for w, (src, land) in enumerate(zip(srcs, lands)):
        for j, chip in enumerate([(1 - x, y), (x, 1 - y), (1 - x, 1 - y)]):
            cps.append(pltpu.make_async_remote_copy(
                src_ref=src.at[2 * chip[0] + chip[1]], dst_ref=land.at[j],
                send_sem=send_sems.at[3 * w + j], recv_sem=recv_sems.at[3 * w + j],
                device_id=(*chip, c), device_id_type=MESH))
    return cps


def _split_start(copies, srcs, lands, name):
    n = len(srcs)

    def body(*refs):
        src, lnd, send_sems, recv_sems, token = refs[:n], refs[n:2 * n], refs[2 * n], refs[2 * n + 1], refs[-1]
        for cp in copies(src, lnd, send_sems, recv_sems):
            cp.start()
        token[...] = jnp.zeros_like(token)

    hbm = pl.BlockSpec(memory_space=pltpu.HBM)
    sem = pl.BlockSpec(memory_space=pltpu.SEMAPHORE)
    outs = pl.pallas_call(
        body, name=name,
        out_shape=(pltpu.SemaphoreType.DMA((3 * n,)), pltpu.SemaphoreType.DMA((3 * n,)),
                   *[pltpu.HBM(v.shape, v.dtype) for v in srcs + lands], jax.ShapeDtypeStruct((8, 128), F32)),
        in_specs=[hbm] * (2 * n),
        out_specs=(sem, sem, *([hbm] * (2 * n)), pl.BlockSpec(memory_space=pltpu.VMEM)),
        input_output_aliases={i: 2 + i for i in range(2 * n)},
        compiler_params=_SPLIT_COPY,
    )(*[_in_hbm(v) for v in srcs + lands])
    return outs[0], outs[1], list(outs[2:2 + n]), list(outs[2 + n:2 + 2 * n]), outs[-1]


def _split_wait(copies, send_sems, recv_sems, srcs, lands, after, name):
    n = len(srcs)

    def body(*refs):
        src, lnd, send_sems, recv_sems = refs[:n], refs[n:2 * n], refs[2 * n], refs[2 * n + 1]
        for cp in copies(src, lnd, send_sems, recv_sems):
            cp.wait_send()
            cp.wait_recv()

    hbm = pl.BlockSpec(memory_space=pltpu.HBM)
    sem = pl.BlockSpec(memory_space=pltpu.SEMAPHORE)
    outs = pl.pallas_call(
        body, name=name,
        out_shape=tuple(pltpu.HBM(v.shape, v.dtype) for v in srcs + lands),
        in_specs=[hbm] * (2 * n) + [sem, sem, pl.BlockSpec(memory_space=pl.ANY)],
        out_specs=tuple([hbm] * (2 * n)),
        input_output_aliases={i: i for i in range(2 * n)},
        compiler_params=_SPLIT_COPY,
    )(*srcs, *lands, send_sems, recv_sems, after)
    return list(outs[n:])


def _pass_to_sibling(lands, name):
    n = len(lands)

    def body(*refs):
        ins, outs, (send_sems, recv_sems) = refs[:n], refs[n:2 * n], refs[2 * n:]
        x, y, c = lax.axis_index("x"), lax.axis_index("y"), lax.axis_index("c")
        cps = []
        for w in range(n):
            hr = ins[w].shape[1] // 2
            for j, chip in enumerate([(1 - x, y), (x, 1 - y), (1 - x, 1 - y)]):
                k = 2 * chip[0] + chip[1]
                cps.append(pltpu.make_async_remote_copy(
                    src_ref=ins[w].at[k, pl.ds(c * hr, hr), :], dst_ref=outs[w].at[k, pl.ds(c * hr, hr), :],
                    send_sem=send_sems.at[3 * w + j], recv_sem=recv_sems.at[3 * w + j],
                    device_id=(x, y, 1 - c), device_id_type=MESH))
        for cp in cps:
            cp.start()
        for cp in cps:
            cp.wait()

    return pl.pallas_call(
        body, name=name,
        out_shape=[jax.ShapeDtypeStruct(v.shape, v.dtype) for v in lands],
        in_specs=_hbm_specs(n), out_specs=_hbm_specs(n),
        input_output_aliases={i: i for i in range(n)},
        scratch_shapes=[pltpu.SemaphoreType.DMA((3 * n,)), pltpu.SemaphoreType.DMA((3 * n,))],
    )(*lands)


def _row_tile(rows):
    for cand in (256, 176, 128, 64, 32, 16):
        if rows % cand == 0:
            return cand
    raise ValueError(rows)


def _add_my_half(g, other, c_idx, name):
    _, k, n = g.shape
    hr = k // 2
    tr = _row_tile(hr)
    nb = hr // tr

    def body(c_ref, g_ref, o_ref, out_ref, out16_ref):
        s = g_ref[...] + o_ref[...]
        out_ref[...] = s
        out16_ref[...] = s.astype(BF16)

    return pl.pallas_call(
        body, name=name,
        grid_spec=pltpu.PrefetchScalarGridSpec(
            num_scalar_prefetch=1, grid=(N_CHIPS, nb),
            in_specs=[pl.BlockSpec((1, tr, n), lambda j, i, c: (j, c[0] * nb + i, 0)),
                      pl.BlockSpec((1, tr, n), lambda j, i, c: (j, i, 0))],
            out_specs=[pl.BlockSpec((1, tr, n), lambda j, i, c: (j, i, 0)),
                       pl.BlockSpec((1, tr, n), lambda j, i, c: (j, i, 0))]),
        out_shape=[jax.ShapeDtypeStruct((N_CHIPS, hr, n), F32), jax.ShapeDtypeStruct((N_CHIPS, hr, n), BF16)],
        compiler_params=_params(("parallel", "parallel")),
    )(c_idx, g, other)


def _add_chips(red, recv, chip_idx, name):
    _, hr, n = red.shape
    tr = _row_tile(hr)

    def body(k_ref, r_ref, v_ref, out_ref):
        out_ref[...] = ((r_ref[0] + v_ref[0].astype(F32)) + v_ref[1].astype(F32)) + v_ref[2].astype(F32)

    return pl.pallas_call(
        body, name=name,
        grid_spec=pltpu.PrefetchScalarGridSpec(
            num_scalar_prefetch=1, grid=(hr // tr,),
            in_specs=[pl.BlockSpec((1, tr, n), lambda i, k: (k[0], i, 0)),
                      pl.BlockSpec((3, tr, n), lambda i, k: (0, i, 0))],
            out_specs=pl.BlockSpec((tr, n), lambda i, k: (i, 0))),
        out_shape=jax.ShapeDtypeStruct((hr, n), F32),
        compiler_params=_params(("parallel",)),
    )(chip_idx, red, recv)


def _mod_shard(c_all, w_ada, b_ada):
    nb, cols = c_all.shape[0], w_ada.shape[1]

    def body(c_ref, w_ref, b_ref, o_ref):
        c = c_ref[...]
        o_ref[...] = _dot(c * jax.nn.sigmoid(c), w_ref[...]) + b_ref[...]

    return pl.pallas_call(
        body, name="mod_shard", out_shape=jax.ShapeDtypeStruct((nb, cols), F32),
        compiler_params=_params(vmem_mb=48),
    )(c_all, w_ada, b_ada)


def _proj(x2, mod8, w, seq, out_dtype, name):
    t = x2.shape[0]
    n = w.shape[1]
    tm, tn = min(2048, seq), min(1152, n)
    tpb = seq // tm

    def body(x_ref, mod_ref, w_ref, o_ref, h_ref):
        @pl.when(pl.program_id(1) == 0)
        def _():
            h_ref[...] = (x_ref[...] * (1.0 + mod_ref[0, 1:2, :]) + mod_ref[0, 0:1, :]).astype(BF16)
        o_ref[...] = jnp.dot(h_ref[...], w_ref[...], preferred_element_type=F32).astype(o_ref.dtype)

    return pl.pallas_call(
        body, name=name, grid=(t // tm, n // tn),
        in_specs=[pl.BlockSpec((tm, D), lambda i, j: (i, 0)),
                  pl.BlockSpec((1, 8, D), lambda i, j: (i // tpb, 0, 0)),
                  pl.BlockSpec((D, tn), lambda i, j: (0, j))],
        out_specs=[pl.BlockSpec((tm, tn), lambda i, j: (i, j)), pl.BlockSpec((tm, D), lambda i, j: (i, 0))],
        out_shape=[jax.ShapeDtypeStruct((t, n), out_dtype), jax.ShapeDtypeStruct((t, D), BF16)],
        compiler_params=_params(("parallel", "arbitrary"), 56),
    )(x2, mod8, w)


def _rows_matmul(a, w, name):
    t, k = a.shape
    n = w.shape[1]
    tm = 1024 if t % 1024 == 0 else t

    def body(a_ref, w_ref, o_ref):
        o_ref[...] = jnp.dot(a_ref[...], w_ref[...], preferred_element_type=F32)

    return pl.pallas_call(
        body, name=name, grid=(t // tm,),
        in_specs=[pl.BlockSpec((tm, k), lambda i: (i, 0)), pl.BlockSpec((k, n), lambda i: (0, 0))],
        out_specs=pl.BlockSpec((tm, n), lambda i: (i, 0)),
        out_shape=jax.ShapeDtypeStruct((t, n), F32),
        compiler_params=_params(("parallel",)),
    )(a, w)


def _tn_matmul(a, b, name, seq, split=None):
    a_st, b_st = a.ndim == 3, b.ndim == 3
    t, ka = a.shape[-2:]
    n = b.shape[-1]
    tt = min(1024, seq)
    nt = t // tt
    if a_st or b_st:
        steps, tn = (a.shape[0] if a_st else b.shape[0]), n
    else:
        tn = split
        if tn is None:
            tn = next(cand for cand in (1920, 1024, 1408, 512, n) if n % cand == 0)
        steps = n // tn
    stacked_out = a_st or b_st or split is not None

    def body(a_ref, b_ref, o_ref):
        part = _dot_tn(a_ref[0] if a_st else a_ref[...], b_ref[0] if b_st else b_ref[...])
        if stacked_out:
            part = part[None]

        @pl.when(pl.program_id(1) == 0)
        def _():
            o_ref[...] = part

        @pl.when(pl.program_id(1) > 0)
        def _():
            o_ref[...] += part

    if a_st:
        in_specs = [pl.BlockSpec((1, tt, ka), lambda j, k: (j, k, 0))]
    else:
        in_specs = [pl.BlockSpec((tt, ka), lambda j, k: (k, 0))]
    if b_st:
        in_specs.append(pl.BlockSpec((1, tt, n), lambda j, k: (j, k, 0)))
    else:
        in_specs.append(pl.BlockSpec((tt, tn), lambda j, k: (k, 0 if a_st else j)))
    if stacked_out:
        out_spec = pl.BlockSpec((1, ka, tn), lambda j, k: (j, 0, 0))
        out_shape = jax.ShapeDtypeStruct((steps, ka, tn), F32)
    else:
        out_spec = pl.BlockSpec((ka, tn), lambda j, k: (0, j))
        out_shape = jax.ShapeDtypeStruct((ka, n), F32)
    return pl.pallas_call(
        body, name=name, grid=(steps, nt), in_specs=in_specs, out_specs=out_spec, out_shape=out_shape,
        compiler_params=_params(("parallel", "arbitrary"), 56),
    )(a, b)


def _dh_kernel(dproj, w_p, x2, dxp, mod8, seq):
    t = x2.shape[0]
    tm, tk = min(1024, seq), 1920
    tpb = seq // tm
    nk = NP // tk
    nbatch = t // seq

    def body(dp_ref, w_ref, x_ref, dxp_ref, mod_ref, gx_ref, dm_ref, acc):
        i, k = pl.program_id(0), pl.program_id(1)

        @pl.when(k == 0)
        def _():
            acc[...] = jnp.zeros_like(acc)

        acc[...] += _dot_nt(dp_ref[...], w_ref[...])

        @pl.when(k == nk - 1)
        def _():
            dh = acc[...]
            gx_ref[...] = dxp_ref[...].astype(F32) + dh * (1.0 + mod_ref[0, 1:2, :])
            upd = jnp.concatenate(
                [jnp.sum(dh, axis=0, keepdims=True), jnp.sum(dh * x_ref[...], axis=0, keepdims=True),
                 jnp.zeros((6, D), F32)], axis=0)

            @pl.when(i % tpb == 0)
            def _():
                dm_ref[0] = upd

            @pl.when(i % tpb != 0)
            def _():
                dm_ref[0] += upd

    return pl.pallas_call(
        body, name="dh", grid=(t // tm, nk),
        in_specs=[pl.BlockSpec((tm, tk), lambda i, k: (i, k)),
                  pl.BlockSpec((D, tk), lambda i, k: (0, k)),
                  pl.BlockSpec((tm, D), lambda i, k: (i, 0)),
                  pl.BlockSpec((tm, D), lambda i, k: (i, 0)),
                  pl.BlockSpec((1, 8, D), lambda i, k: (i // tpb, 0, 0))],
        out_specs=[pl.BlockSpec((tm, D), lambda i, k: (i, 0)),
                   pl.BlockSpec((1, 8, D), lambda i, k: (i // tpb, 0, 0))],
        out_shape=[jax.ShapeDtypeStruct((t, D), F32), jax.ShapeDtypeStruct((nbatch, 8, D), F32)],
        scratch_shapes=[pltpu.VMEM((tm, D), F32)],
        compiler_params=_params(("arbitrary", "arbitrary"), 56),
    )(dproj, w_p, x2, dxp, mod8)


def _tri(n, upper):
    r = lax.broadcasted_iota(jnp.int32, (n, n), 0)
    c = lax.broadcasted_iota(jnp.int32, (n, n), 1)
    return jnp.where((c >= r) if upper else (c <= r), 1.0, 0.0).astype(F32)


@jax.custom_vjp
def _mm_nn(a, b):
    return _dot(a, b)


_mm_nn.defvjp(lambda a, b: (_dot(a, b), (a, b)),
              lambda res, g: (_dot_nt(g, res[1]), _dot_tn(res[0], g)))


@jax.custom_vjp
def _mm_nt(a, b):
    return _dot_nt(a, b)


_mm_nt.defvjp(lambda a, b: (_dot_nt(a, b), (a, b)),
              lambda res, g: (_dot(g, res[1]), _dot_tn(g, res[0])))


@jax.custom_vjp
def _mm_tn(a, b):
    return _dot_tn(a, b)


_mm_tn.defvjp(lambda a, b: (_dot_tn(a, b), (a, b)),
              lambda res, g: (_dot_nt(res[1], g), _dot(res[0], g)))


@jax.custom_vjp
def _cumsum_rows(x):
    return _dot_f32(_tri(x.shape[0], False), x)


_cumsum_rows.defvjp(lambda x: (_cumsum_rows(x), None),
                    lambda _, g: (_dot_f32(_tri(g.shape[0], True), g),))


@functools.partial(jax.custom_vjp, nondiff_argnums=(1,))
def _shift_rows(x, k):
    return pltpu.roll(x, k % x.shape[0], 0)


_shift_rows.defvjp(lambda x, k: (_shift_rows(x, k), None),
                   lambda k, _, g: (pltpu.roll(g, (-k) % g.shape[0], 0),))


def _group_ref(bc, m):
    n = bc.shape[0] // (2 * m)
    b3 = bc.reshape(n, 2 * m, ADH)
    row = lax.broadcasted_iota(jnp.int32, b3.shape, 1)
    ref = jnp.sum(jnp.where(row == m - 1, b3, 0.0), axis=1, keepdims=True)
    return jnp.broadcast_to(ref, b3.shape).reshape(bc.shape)


def _hgrn_block(q, fl, v, g, st, lb, nw):
    n = q.shape[0]
    f = lb + (1.0 - lb) * jax.nn.sigmoid(fl)
    kk = 1.0 - f
    lf = jnp.log(f)
    bc = _cumsum_rows(lf)
    row = lax.broadcasted_iota(jnp.int32, (n, ADH), 0)
    same = jnp.bitwise_xor(lax.broadcasted_iota(jnp.int32, (n, n), 0), lax.broadcasted_iota(jnp.int32, (n, n), 1))
    a = jnp.zeros((n, n), F32)
    m = 1
    while m < n:
        r = jnp.bitwise_and(row, 2 * m - 1)
        up, lo = r >= m, r < m
        if m == 1:
            aq, ak = lf, jnp.zeros_like(lf)
        elif m == 2:
            aq = jnp.where(r == 3, lf + _shift_rows(lf, 1), lf)
            ak = jnp.where(r == 0, _shift_rows(lf, -1), 0.0)
        else:
            ref = _group_ref(bc, m)
            aq, ak = bc - ref, ref - bc
        qt = jnp.where(up, q * jnp.exp(jnp.where(up, aq, 0.0)), 0.0)
        kt = jnp.where(lo, kk * jnp.exp(jnp.where(lo, ak, 0.0)), 0.0)
        a = a + jnp.where(same < 2 * m, _mm_nt(qt, kt), 0.0)
        m *= 2
    last = row == n - 1
    bl = jnp.sum(jnp.where(last, bc, 0.0), axis=0, keepdims=True)
    o = _mm_nn(a, v) + _mm_nt(q * jnp.exp(bc), st) + jnp.sum(q * kk, axis=-1, keepdims=True) * v
    st_new = st * jnp.exp(bl) + _mm_tn(v, kk * jnp.exp(bl - bc))
    rms = lax.rsqrt(jnp.mean(o * o, axis=-1, keepdims=True) + RMS_EPS)
    return o * rms * nw * jax.nn.sigmoid(g), st_new


def _hgrn_fwd(proj, lb_logits, norm_w, nbatch, seq):
    t = proj.shape[0]
    blk = min(HGRN_BLOCK, seq)
    nb = seq // blk

    nh = HGRN_HEADS
    wp, wy = 512 * nh, ADH * nh

    def body(p_ref, lbl_ref, nw_ref, y_ref, ck_ref, st_s):
        @pl.when(pl.program_id(2) == 0)
        def _():
            st_s[...] = jnp.zeros_like(st_s)

        st = [st_s[h] for h in range(nh)]
        p = p_ref[...].astype(F32)
        lb = jax.nn.sigmoid(lbl_ref[0:1, :] - lbl_ref[1:2, :])
        nw = nw_ref[...]
        res = [_hgrn_block(*(p[:, 512 * h + 128 * k:512 * h + 128 * k + 128] for k in range(4)), st[h],
                           lb[:, 128 * h:128 * h + 128], nw[:, 128 * h:128 * h + 128]) for h in range(nh)]
        for h in range(nh):
            ck_ref[0, h] = st[h]
            st_s[h] = res[h][1]
        y_ref[...] = jnp.concatenate([r[0] for r in res], axis=1).astype(y_ref.dtype)

    return pl.pallas_call(
        body, name="hgrn_fwd", grid=(AH // nh, nbatch, nb),
        in_specs=[pl.BlockSpec((blk, wp), lambda h, b, i: (b * nb + i, COL_A // wp + h)),
                  pl.BlockSpec((2, wy), lambda h, b, i: (0, h)),
                  pl.BlockSpec((1, wy), lambda h, b, i: (0, h))],
        out_specs=[pl.BlockSpec((blk, wy), lambda h, b, i: (b * nb + i, h)),
                   pl.BlockSpec((1, nh, 128, 128), lambda h, b, i: ((h * nbatch + b) * nb + i, 0, 0, 0))],
        out_shape=[jax.ShapeDtypeStruct((t, AW), BF16),
                   jax.ShapeDtypeStruct((AH // nh * nbatch * nb, nh, 128, 128), F32)],
        scratch_shapes=[pltpu.VMEM((nh, 128, 128), F32)],
        compiler_params=_params(("parallel", "parallel", "arbitrary"), 48),
    )(proj, lb_logits, norm_w)


def _hgrn_bwd(proj, dya, ckpt, lb_logits, norm_w, dproj, nbatch, seq):
    t = proj.shape[0]
    blk = min(HGRN_BLOCK, seq)
    nb = seq // blk

    nh = HGRN_HEADS
    wp, wy = 512 * nh, ADH * nh

    def body(p_ref, dy_ref, ck_ref, lbl_ref, nw_ref, dp_in, dp_ref, sm_ref, dst_s):
        del dp_in
        b_id, i = pl.program_id(1), pl.program_id(2)

        @pl.when(i == 0)
        def _():
            dst_s[...] = jnp.zeros_like(dst_s)

        dst = [dst_s[h] for h in range(nh)]
        st = [ck_ref[0, h] for h in range(nh)]
        p = p_ref[...].astype(F32)
        dy = dy_ref[...]
        lb = jax.nn.sigmoid(lbl_ref[0:1, :] - lbl_ref[1:2, :])
        nw = nw_ref[...]
        grads = []
        for h in range(nh):
            _, pullback = jax.vjp(_hgrn_block, *(p[:, 512 * h + 128 * k:512 * h + 128 * k + 128] for k in range(4)),
                                  st[h], lb[:, 128 * h:128 * h + 128], nw[:, 128 * h:128 * h + 128])
            grads.append(pullback((dy[:, 128 * h:128 * h + 128], dst[h])))
        for h in range(nh):
            dst_s[h] = grads[h][4]
        dp_ref[...] = jnp.concatenate([g[k] for g in grads for k in range(4)], axis=1).astype(dp_ref.dtype)
        upd = jnp.concatenate([jnp.concatenate([g[5] for g in grads], axis=1),
                               jnp.concatenate([g[6] for g in grads], axis=1), jnp.zeros((6, wy), F32)], axis=0)
        first = (b_id == 0) & (i == 0)

        @pl.when(first)
        def _():
            sm_ref[...] = upd

        @pl.when(jnp.logical_not(first))
        def _():
            sm_ref[...] += upd

    def rows(h, b, i):
        return b * nb + (nb - 1 - i)

    return pl.pallas_call(
        body, name="hgrn_bwd", grid=(AH // nh, nbatch, nb),
        in_specs=[pl.BlockSpec((blk, wp), lambda h, b, i: (rows(h, b, i), COL_A // wp + h)),
                  pl.BlockSpec((blk, wy), lambda h, b, i: (rows(h, b, i), h)),
                  pl.BlockSpec((1, nh, 128, 128), lambda h, b, i: ((h * nbatch + b) * nb + (nb - 1 - i), 0, 0, 0)),
                  pl.BlockSpec((2, wy), lambda h, b, i: (0, h)),
                  pl.BlockSpec((1, wy), lambda h, b, i: (0, h)),
                  pl.BlockSpec(memory_space=pl.ANY)],
        out_specs=[pl.BlockSpec((blk, wp), lambda h, b, i: (rows(h, b, i), COL_A // wp + h)),
                   pl.BlockSpec((8, wy), lambda h, b, i: (0, h))],
        out_shape=[jax.ShapeDtypeStruct((t, NP), BF16), jax.ShapeDtypeStruct((8, AW), F32)],
        input_output_aliases={5: 0},
        scratch_shapes=[pltpu.VMEM((nh, 128, 128), F32)],
        compiler_params=_params(("parallel", "arbitrary", "arbitrary"), 56),
    )(proj, dya, ckpt, lb_logits, norm_w, dproj)


def _log_sigmoid(z):
    return jnp.minimum(z, 0.0) - jnp.log(1.0 + jnp.exp(-jnp.abs(z)))


def _fox_cum(proj, bias128, nbatch, seq):
    t = proj.shape[0]
    ts = min(512, seq)
    nb = seq // ts

    def body(p_ref, b_ref, c_ref, carry):
        @pl.when(pl.program_id(1) == 0)
        def _():
            carry[...] = jnp.zeros_like(carry)
        cum = _dot_f32(_tri(ts, False), _log_sigmoid(p_ref[...] + b_ref[...])) + carry[...]
        carry[...] = cum[ts - 1:ts, :]
        cum2 = cum * LOG2E
        lane = lax.broadcasted_iota(jnp.int32, (ts, 128), 1)
        for p in range(4):
            c_ref[p] = jnp.where(lane < 64, cum2[:, 2 * p:2 * p + 1], cum2[:, 2 * p + 1:2 * p + 2])

    return pl.pallas_call(
        body, name="fox_cum", grid=(nbatch, nb),
        in_specs=[pl.BlockSpec((ts, 128), lambda b, i: (b * nb + i, 0)),
                  pl.BlockSpec((1, 128), lambda b, i: (0, 0))],
        out_specs=pl.BlockSpec((4, ts, 128), lambda b, i: (0, b * nb + i, 0)),
        out_shape=jax.ShapeDtypeStruct((4, t, 128), F32),
        scratch_shapes=[pltpu.VMEM((1, 128), F32)],
        compiler_params=_params(("parallel", "arbitrary")),
    )(proj, bias128)


def _fox_scores_t(q128, k128, cc128, hh, masked):
    tq, tk = q128.shape[0], k128.shape[0]
    qh = _head_lanes((q128 * (LOG2E * BDH ** -0.5)).astype(BF16), hh)
    s = _dot_nt(k128, qh) - cc128[:, 64 * hh:64 * hh + 1]
    if masked:
        key = lax.broadcasted_iota(jnp.int32, (tk, tq), 0)
        qry = lax.broadcasted_iota(jnp.int32, (tk, tq), 1)
        s = jnp.where(key <= qry, s, NEG)
    return s


def _causal_pairs(nq, key_major):
    if key_major:
        pairs = [(i, j) for j in range(nq) for i in range(j, nq)]
    else:
        pairs = [(i, j) for i in range(nq) for j in range(i + 1)]
    return (jnp.asarray([p[0] for p in pairs], jnp.int32), jnp.asarray([p[1] for p in pairs], jnp.int32))


def _head_lanes(x128, hh):
    lane = lax.broadcasted_iota(jnp.int32, x128.shape, 1)
    return jnp.where((lane < 64) if hh == 0 else (lane >= 64), x128, jnp.zeros_like(x128))


def _with_ones_lane(x128, hh):
    lane = lax.broadcasted_iota(jnp.int32, x128.shape, 1)
    one = jnp.ones_like(x128)
    zero = jnp.zeros_like(x128)
    if hh == 0:
        return jnp.where(lane < 64, x128, jnp.where(lane == 64, one, zero))
    return jnp.where(lane >= 64, x128, jnp.where(lane == 0, one, zero))


def _fox_fwd(proj, cum_cols, nbatch, seq):
    t = proj.shape[0]
    tq = tk = min(512, seq)
    nq = seq // tq
    npr = FOX_PAIRS
    qi, kj = _causal_pairs(nq, key_major=False)

    def body(qi_ref, kj_ref, q_ref, kv_ref, cc_ref, o_ref, lse_ref, m_s, acc_s):
        s_id = pl.program_id(2)
        i, j = qi_ref[s_id], kj_ref[s_id]

        @pl.when(j == 0)
        def _():
            m_s[...] = jnp.full_like(m_s, NEG)
            acc_s[...] = jnp.zeros_like(acc_s)

        def step(masked):
            heads = [(pr, hh) for pr in range(npr) for hh in range(2)]
            m_prev = m_s[0:2 * npr, :]
            acc_prev = [acc_s[h] for h in range(2 * npr)]
            q128 = [q_ref[:, 128 * pr:128 * pr + 128] for pr in range(npr)]
            k128 = [kv_ref[:, 256 * pr:256 * pr + 128].astype(BF16) for pr in range(npr)]
            v128 = [kv_ref[:, 256 * pr + 128:256 * pr + 256].astype(BF16) for pr in range(npr)]
            s = [_fox_scores_t(q128[pr], k128[pr], cc_ref[pr], hh, masked) for pr, hh in heads]
            m_new = [jnp.maximum(m_prev[h:h + 1, :], jnp.max(s[h], axis=0, keepdims=True)) for h in range(2 * npr)]
            acc_new = []
            for h, (pr, hh) in enumerate(heads):
                alpha = jnp.exp2(m_prev[h:h + 1, :] - m_new[h])
                p = jnp.exp2(s[h] - m_new[h]).astype(BF16)
                acc_new.append(acc_prev[h] * alpha + _dot_tn(_with_ones_lane(v128[pr], hh), p))
            for h in range(2 * npr):
                acc_s[h] = acc_new[h]
            m_s[0:2 * npr, :] = jnp.concatenate(m_new, axis=0)

        @pl.when(j < i)
        def _():
            step(False)

        @pl.when(j == i)
        def _():
            step(True)
            outs = []
            for pr in range(npr):
                a0, a1 = acc_s[2 * pr], acc_s[2 * pr + 1]
                l0, l1 = a0[64:65, :], a1[0:1, :]
                outs.append(jnp.concatenate([a0[0:64, :] / l0, a1[64:128, :] / l1], axis=0).T)
                lse_ref[0, pr] = jnp.concatenate(
                    [m_s[2 * pr:2 * pr + 1, :] + jnp.log2(l0), m_s[2 * pr + 1:2 * pr + 2, :] + jnp.log2(l1),
                     jnp.zeros((6, tq), F32)], axis=0)
            o_ref[...] = jnp.concatenate(outs, axis=1).astype(o_ref.dtype)

    return pl.pallas_call(
        body, name="fox_fwd",
        grid_spec=pltpu.PrefetchScalarGridSpec(
            num_scalar_prefetch=2, grid=(nbatch, 4 // npr, qi.shape[0]),
            in_specs=[pl.BlockSpec((tq, 128 * npr), lambda b, p, s, qi, kj: (b * nq + qi[s], COL_BQ // (128 * npr) + p)),
                      pl.BlockSpec((tk, 256 * npr), lambda b, p, s, qi, kj: (b * nq + kj[s], COL_KV // (256 * npr) + p)),
                      pl.BlockSpec((npr, tk, 128), lambda b, p, s, qi, kj: (p, b * nq + kj[s], 0))],
            out_specs=[pl.BlockSpec((tq, 128 * npr), lambda b, p, s, qi, kj: (b * nq + qi[s], p)),
                       pl.BlockSpec((1, npr, 8, tq), lambda b, p, s, qi, kj: (b, p, 0, qi[s]))],
            scratch_shapes=[pltpu.VMEM((8, tq), F32), pltpu.VMEM((2 * npr, 128, tq), F32)]),
        out_shape=[jax.ShapeDtypeStruct((t, 512), BF16), jax.ShapeDtypeStruct((nbatch, 4, 8, seq), F32)],
        compiler_params=_params(("parallel", "parallel", "arbitrary"), 56),
    )(qi, kj, proj, proj, cum_cols)


def _fox_bwd(proj, cum_cols, lse, yb, dyb, dproj, nbatch, seq):
    t = proj.shape[0]
    tq = tk = min(512, seq)
    nq = seq // tq
    scale = BDH ** -0.5
    qi, kj = _causal_pairs(nq, key_major=True)
    nsteps = qi.shape[0]

    npr = FOX_PAIRS

    def body(qi_ref, kj_ref, q_ref, kv_ref, cc_ref, lse_ref, o_ref, do_ref, dp_in,
             dkv_ref, dq_ref, drs_ref, dcs_ref, dk_s, dv_s, dqa_s):
        del dp_in
        pg, s_id = pl.program_id(1), pl.program_id(2)
        i, j = qi_ref[s_id], kj_ref[s_id]

        @pl.when(i == j)
        def _():
            dk_s[...] = jnp.zeros_like(dk_s)
            dv_s[...] = jnp.zeros_like(dv_s)

        @pl.when(s_id == 0)
        def _():
            dqa_s[...] = jnp.zeros_like(dqa_s)

        def step(masked):
            dk_prev = [dk_s[h] for h in range(2 * npr)]
            dq_prev = [dqa_s[i, h] for h in range(2 * npr)]
            dv_new = [dv_s[pr] for pr in range(npr)]
            dk_new, dq_new = [], []
            for pr in range(npr):
                lanes = slice(128 * pr, 128 * pr + 128)
                q128 = q_ref[:, lanes]
                qs128 = (q128 * scale).astype(BF16)
                k128 = kv_ref[:, 256 * pr:256 * pr + 128].astype(BF16)
                v128 = kv_ref[:, 256 * pr + 128:256 * pr + 256].astype(BF16)
                do128 = do_ref[:, lanes]
                doo = do128 * o_ref[:, lanes].astype(F32)
                do16 = do128.astype(BF16)
                for hh in range(2):
                    s = _fox_scores_t(q128, k128, cc_ref[pr], hh, masked)
                    p = jnp.exp2(s - lse_ref[0, pr, hh:hh + 1, :])
                    dd = lax.dot_general(jnp.ones((8, 128), F32), _head_lanes(doo, hh), (((1,), (1,)), ((), ())),
                                         preferred_element_type=F32, precision=HIGHEST)[0:1, :]
                    doh = _head_lanes(do16, hh)
                    dp = _dot_nt(v128, doh)
                    ds = (p * (dp - dd)).astype(BF16)
                    dv_new[pr] = dv_new[pr] + _dot(p, doh)
                    dk_new.append(dk_prev[2 * pr + hh] + _dot(ds, _with_ones_lane(qs128, hh)))
                    dq_new.append(dq_prev[2 * pr + hh] + _dot_tn(_with_ones_lane(k128, hh), ds))
            for pr in range(npr):
                dv_s[pr] = dv_new[pr]
            for h in range(2 * npr):
                dk_s[h] = dk_new[h]
                dqa_s[i, h] = dq_new[h]

        @pl.when(i == j)
        def _():
            step(True)

        @pl.when(i > j)
        def _():
            step(False)

        def sums_to_lanes(lane, pr, s0, s1):
            hp = npr * pg + pr
            return jnp.where(lane == 2 * hp, s0, jnp.where(lane == 2 * hp + 1, s1, 0.0))

        @pl.when(i == nq - 1)
        def _():
            lane = lax.broadcasted_iota(jnp.int32, (tk, 128), 1)
            for pr in range(npr):
                k0, k1 = dk_s[2 * pr], dk_s[2 * pr + 1]
                dkv_ref[:, 256 * pr:256 * pr + 128] = jnp.where(lane < 64, k0, k1).astype(dkv_ref.dtype)
                dkv_ref[:, 256 * pr + 128:256 * pr + 256] = dv_s[pr].astype(dkv_ref.dtype)
                dcs_ref[pr] = sums_to_lanes(lane, pr, k0[:, 64:65], k1[:, 0:1])

        @pl.when(s_id == nsteps - 1)
        def _():
            lane = lax.broadcasted_iota(jnp.int32, (tq, 128), 1)
            for blk in range(nq):
                rows = pl.ds(blk * tq, tq)
                for pr in range(npr):
                    a0 = dqa_s[blk, 2 * pr].T
                    a1 = dqa_s[blk, 2 * pr + 1].T
                    dq_ref[rows, 128 * pr:128 * pr + 128] = (jnp.where(lane < 64, a0, a1) * scale).astype(dq_ref.dtype)
                    drs_ref[pr, rows, :] = sums_to_lanes(lane, pr, a0[:, 64:65], a1[:, 0:1])

    return pl.pallas_call(
        body, name="fox_bwd",
        grid_spec=pltpu.PrefetchScalarGridSpec(
            num_scalar_prefetch=2, grid=(nbatch, 4 // npr, nsteps),
            in_specs=[pl.BlockSpec((tq, 128 * npr), lambda b, p, s, qi, kj: (b * nq + qi[s], COL_BQ // (128 * npr) + p)),
                      pl.BlockSpec((tk, 256 * npr), lambda b, p, s, qi, kj: (b * nq + kj[s], COL_KV // (256 * npr) + p)),
                      pl.BlockSpec((npr, tk, 128), lambda b, p, s, qi, kj: (p, b * nq + kj[s], 0)),
                      pl.BlockSpec((1, npr, 8, tq), lambda b, p, s, qi, kj: (b, p, 0, qi[s])),
                      pl.BlockSpec((tq, 128 * npr), lambda b, p, s, qi, kj: (b * nq + qi[s], p)),
                      pl.BlockSpec((tq, 128 * npr), lambda b, p, s, qi, kj: (b * nq + qi[s], p)),
                      pl.BlockSpec(memory_space=pl.ANY)],
            out_specs=[pl.BlockSpec((tk, 256 * npr), lambda b, p, s, qi, kj: (b * nq + kj[s], COL_KV // (256 * npr) + p)),
                       pl.BlockSpec((seq, 128 * npr), lambda b, p, s, qi, kj: (b, p)),
                       pl.BlockSpec((npr, seq, 128), lambda b, p, s, qi, kj: (p, b, 0)),
                       pl.BlockSpec((npr, tk, 128), lambda b, p, s, qi, kj: (p, b * nq + kj[s], 0))],
            scratch_shapes=[pltpu.VMEM((2 * npr, tk, 128), F32), pltpu.VMEM((npr, tk, 128), F32),
                            pltpu.VMEM((nq, 2 * npr, 128, tq), F32)]),
        out_shape=[jax.ShapeDtypeStruct((t, NP), BF16), jax.ShapeDtypeStruct((t, 512), BF16),
                   jax.ShapeDtypeStruct((4, t, 128), F32), jax.ShapeDtypeStruct((4, t, 128), F32)],
        input_output_aliases={8: 0},
        compiler_params=_params(("parallel", "parallel", "arbitrary"), 60),
    )(qi, kj, proj, proj, cum_cols, lse, yb, dyb, dproj)


def _place_cols(dproj, src, col):
    t, w = src.shape
    tm = 1024 if t % 1024 == 0 else t

    def body(s_ref, dp_in, o_ref):
        del dp_in
        o_ref[...] = s_ref[...]

    return pl.pallas_call(
        body, name="place_cols", grid=(t // tm,),
        in_specs=[pl.BlockSpec((tm, w), lambda i: (i, 0)), pl.BlockSpec(memory_space=pl.ANY)],
        out_specs=pl.BlockSpec((tm, w), lambda i: (i, col // w)),
        out_shape=jax.ShapeDtypeStruct(dproj.shape, dproj.dtype),
        input_output_aliases={1: 0},
        compiler_params=_params(("parallel",)),
    )(src, dproj)


def _fox_dbf(proj, bias128, drs, dcs, dproj, nbatch, seq):
    t = proj.shape[0]
    ts = min(512, seq)
    nb = seq // ts

    def body(p_ref, b_ref, dr_ref, dc_ref, dp_in, dp_ref, sm_ref, carry):
        del dp_in
        b_id, i = pl.program_id(0), pl.program_id(1)

        @pl.when(i == 0)
        def _():
            carry[...] = jnp.zeros_like(carry)

        dcum = (dr_ref[0] - dc_ref[0]) + (dr_ref[1] - dc_ref[1]) + (dr_ref[2] - dc_ref[2]) + (dr_ref[3] - dc_ref[3])
        rc = _dot_f32(_tri(ts, True), dcum) + carry[...]
        carry[...] = rc[0:1, :]
        z = p_ref[...] + b_ref[...]
        lane = lax.broadcasted_iota(jnp.int32, (ts, 128), 1)
        dz = jnp.where(lane < BH, rc * jax.nn.sigmoid(-z), 0.0)
        dp_ref[...] = dz.astype(dp_ref.dtype)
        upd = jnp.concatenate([jnp.sum(dz, axis=0, keepdims=True), jnp.zeros((7, 128), F32)], axis=0)
        first = (b_id == 0) & (i == 0)

        @pl.when(first)
        def _():
            sm_ref[...] = upd

        @pl.when(jnp.logical_not(first))
        def _():
            sm_ref[...] += upd

    def rows(b, i):
        return b * nb + (nb - 1 - i)

    return pl.pallas_call(
        body, name="fox_dbf", grid=(nbatch, nb),
        in_specs=[pl.BlockSpec((ts, 128), lambda b, i: (rows(b, i), 0)),
                  pl.BlockSpec((1, 128), lambda b, i: (0, 0)),
                  pl.BlockSpec((4, ts, 128), lambda b, i: (0, rows(b, i), 0)),
                  pl.BlockSpec((4, ts, 128), lambda b, i: (0, rows(b, i), 0)),
                  pl.BlockSpec(memory_space=pl.ANY)],
        out_specs=[pl.BlockSpec((ts, 128), lambda b, i: (rows(b, i), COL_BF // 128)),
                   pl.BlockSpec((8, 128), lambda b, i: (0, 0))],
        out_shape=[jax.ShapeDtypeStruct((t, NP), BF16), jax.ShapeDtypeStruct((8, 128), F32)],
        input_output_aliases={4: 0},
        scratch_shapes=[pltpu.VMEM((1, 128), F32)],
        compiler_params=_params(("arbitrary", "arbitrary")),
    )(proj, bias128, drs, dcs, dproj)


def _ln_stats(z):
    mu = jnp.mean(z, axis=-1, keepdims=True)
    zc = z - mu
    rstd = lax.rsqrt(jnp.mean(zc * zc, axis=-1, keepdims=True) + LN_EPS)
    return zc * rstd, rstd


def _ln_bwd(dy, xhat, rstd, w):
    dxh = dy * w
    return rstd * (dxh - jnp.mean(dxh, axis=-1, keepdims=True) - xhat * jnp.mean(dxh * xhat, axis=-1, keepdims=True))


def _merge_fwd(ya, yb, proj, x2, mod8, wba, wbb, wout, ln1w, ln1b, seq):
    t = x2.shape[0]
    tm = min(512, seq)
    tpb = seq // tm

    def body(ya_ref, yb_ref, g_ref, x_ref, mod_ref, wa_ref, wb_ref, wo_ref, lw_ref, lb_ref, mg_ref, u_ref, x1_ref):
        ga = jax.nn.sigmoid(g_ref[:, 0:D].astype(F32))
        gb = jax.nn.sigmoid(g_ref[:, D:2 * D].astype(F32))
        merged = (ga * jnp.dot(ya_ref[...], wa_ref[...], preferred_element_type=F32)
                  + gb * jnp.dot(yb_ref[...], wb_ref[...], preferred_element_type=F32))
        mg = merged.astype(BF16)
        mg_ref[...] = mg
        u = jnp.dot(mg, wo_ref[...], preferred_element_type=F32)
        u_ref[...] = u.astype(u_ref.dtype)
        xhat, _ = _ln_stats(ALPHA * x_ref[...] + (1.0 + mod_ref[0, 2:3, :]) * u)
        x1_ref[...] = xhat * lw_ref[...] + lb_ref[...]

    tok = lambda w: pl.BlockSpec((tm, w), lambda i: (i, 0))
    full = lambda a: pl.BlockSpec(a.shape, lambda i: (0,) * a.ndim)
    return pl.pallas_call(
        body, name="merge_fwd", grid=(t // tm,),
        in_specs=[tok(512), tok(512), pl.BlockSpec((tm, 2048), lambda i: (i, COL_GATES // 2048)), tok(D),
                  pl.BlockSpec((1, 8, D), lambda i: (i // tpb, 0, 0)),
                  full(wba), full(wbb), full(wout), full(ln1w), full(ln1b)],
        out_specs=[tok(D), tok(D), tok(D)],
        out_shape=[jax.ShapeDtypeStruct((t, D), BF16), jax.ShapeDtypeStruct((t, D), BF16),
                   jax.ShapeDtypeStruct((t, D), F32)],
        compiler_params=_params(("parallel",), 48),
    )(ya, yb, proj, x2, mod8, wba, wbb, wout, ln1w, ln1b)


def _merge_bwd(du, ya, yb, proj, wba, wbb, wout, token, seq):
    t = du.shape[0]
    tm = min(512, seq)

    def body(du_ref, ya_ref, yb_ref, g_ref, wa_ref, wb_ref, wo_ref, token_ref,
             dp_ref, dpa_ref, dpb_ref, dya_ref, dyb_ref):
        del token_ref
        ga = jax.nn.sigmoid(g_ref[:, 0:D].astype(F32))
        gb = jax.nn.sigmoid(g_ref[:, D:2 * D].astype(F32))
        dm = _dot_nt(du_ref[...], wo_ref[...])
        pa = jnp.dot(ya_ref[...], wa_ref[...], preferred_element_type=F32)
        pb = jnp.dot(yb_ref[...], wb_ref[...], preferred_element_type=F32)
        dpa = (dm * ga).astype(BF16)
        dpb = (dm * gb).astype(BF16)
        dpa_ref[...] = dpa
        dpb_ref[...] = dpb
        dp_ref[:, 0:D] = (dm * pa * ga * (1.0 - ga)).astype(BF16)
        dp_ref[:, D:2 * D] = (dm * pb * gb * (1.0 - gb)).astype(BF16)
        dya_ref[...] = _dot_nt(dpa, wa_ref[...])
        dyb_ref[...] = _dot_nt(dpb, wb_ref[...])

    tok = lambda w: pl.BlockSpec((tm, w), lambda i: (i, 0))
    full = lambda a: pl.BlockSpec(a.shape, lambda i: (0,) * a.ndim)
    return pl.pallas_call(
        body, name="merge_bwd", grid=(t // tm,),
        in_specs=[tok(D), tok(512), tok(512), pl.BlockSpec((tm, 2048), lambda i: (i, COL_GATES // 2048)),
                  full(wba), full(wbb), full(wout), full(token)],
        out_specs=[pl.BlockSpec((tm, 2048), lambda i: (i, COL_GATES // 2048)), tok(D), tok(D), tok(512), tok(512)],
        out_shape=[jax.ShapeDtypeStruct((t, NP), BF16), jax.ShapeDtypeStruct((t, D), BF16),
                   jax.ShapeDtypeStruct((t, D), BF16), jax.ShapeDtypeStruct((t, 512), F32),
                   jax.ShapeDtypeStruct((t, 512), F32)],
        compiler_params=_params(("parallel",), 48),
    )(du, ya, yb, proj, wba, wbb, wout, token)


def _ffn_fwd(x1, mod8, wg, wu, wd, target, ln2w, ln2b, seq):
    t = x1.shape[0]
    tm = min(FFN_TOKENS, seq)
    nf, tf, _ = wg.shape
    tpb = seq // tm
    nbatch = t // seq

    def body(x_ref, mod_ref, wg_ref, wu_ref, wd_ref, t_ref, lw_ref, lb_ref,
             a_ref, b_ref, h_s, dz_ref, st_ref, dm_ref, acc):
        i, j = pl.program_id(0), pl.program_id(1)

        @pl.when(j == 0)
        def _():
            h_s[...] = (x_ref[...] * (1.0 + mod_ref[0, 4:5, :]) + mod_ref[0, 3:4, :]).astype(BF16)
            acc[...] = jnp.zeros_like(acc)

        a = _dot_nt(h_s[...], wg_ref[0])
        b = _dot_nt(h_s[...], wu_ref[0])
        a_ref[0] = a.astype(BF16)
        b_ref[0] = b.astype(BF16)
        acc[...] += _dot(a * jax.nn.sigmoid(a) * b, wd_ref[0])

        @pl.when(j == nf - 1)
        def _():
            ffn = acc[...]
            xhat, rstd = _ln_stats(ALPHA * x_ref[...] + (1.0 + mod_ref[0, 5:6, :]) * ffn)
            diff = xhat * lw_ref[...] + lb_ref[...] - t_ref[...]
            loss = 0.5 * jnp.sum(jnp.sum(diff * diff, axis=-1, keepdims=True), axis=0, keepdims=True) / D
            dy = diff * (1.0 / D)
            dz = _ln_bwd(dy, xhat, rstd, lw_ref[...])
            dz_ref[...] = dz
            lane = lax.broadcasted_iota(jnp.int32, (1, D), 1)
            upd = jnp.concatenate(
                [jnp.sum(dy * xhat, axis=0, keepdims=True), jnp.sum(dy, axis=0, keepdims=True),
                 jnp.where(lane == 0, loss, 0.0), jnp.zeros((5, D), F32)], axis=0)
            dmu = jnp.concatenate(
                [jnp.zeros((5, D), F32), jnp.sum(dz * ffn, axis=0, keepdims=True), jnp.zeros((2, D), F32)], axis=0)

            @pl.when(i == 0)
            def _():
                st_ref[...] = upd

            @pl.when(i > 0)
            def _():
                st_ref[...] += upd

            @pl.when(i % tpb == 0)
            def _():
                dm_ref[0] = dmu

            @pl.when(i % tpb != 0)
            def _():
                dm_ref[0] += dmu

    row = lambda: pl.BlockSpec((tm, D), lambda i, j: (i, 0))
    vec = lambda: pl.BlockSpec((1, D), lambda i, j: (0, 0))
    return pl.pallas_call(
        body, name="ffn_fwd", grid=(t // tm, nf),
        in_specs=[row(), pl.BlockSpec((1, 8, D), lambda i, j: (i // tpb, 0, 0)),
                  pl.BlockSpec((1, tf, D), lambda i, j: (j, 0, 0)), pl.BlockSpec((1, tf, D), lambda i, j: (j, 0, 0)),
                  pl.BlockSpec((1, tf, D), lambda i, j: (j, 0, 0)), row(), vec(), vec()],
        out_specs=[pl.BlockSpec((1, tm, tf), lambda i, j: (j, i, 0)), pl.BlockSpec((1, tm, tf), lambda i, j: (j, i, 0)),
                   row(), row(), pl.BlockSpec((8, D), lambda i, j: (0, 0)),
                   pl.BlockSpec((1, 8, D), lambda i, j: (i // tpb, 0, 0))],
        out_shape=[jax.ShapeDtypeStruct((nf, t, tf), BF16), jax.ShapeDtypeStruct((nf, t, tf), BF16),
                   jax.ShapeDtypeStruct((t, D), BF16),
                   jax.ShapeDtypeStruct((t, D), F32), jax.ShapeDtypeStruct((8, D), F32),
                   jax.ShapeDtypeStruct((nbatch, 8, D), F32)],
        scratch_shapes=[pltpu.VMEM((tm, D), F32)],
        compiler_params=_params(("arbitrary", "arbitrary"), 60),
    )(x1, mod8, wg, wu, wd, target, ln2w, ln2b)


def _ffn_bwd(dz2, a, b, wg, wu, wd, x1, x2, u, mod8, ln1w, seq):
    t = x1.shape[0]
    tm = min(512, seq)
    nf, tf, _ = wg.shape
    tpb = seq // tm
    nbatch = t // seq

    def body(dz_ref, a_ref, b_ref, wg_ref, wu_ref, wd_ref, x1_ref, x_ref, u_ref, mod_ref, lw_ref,
             da_ref, db_ref, hm_ref, df_ref, du_ref, dxp_ref, st_ref, dm_ref, acc):
        i, j = pl.program_id(0), pl.program_id(1)

        @pl.when(j == 0)
        def _():
            df_ref[...] = ((1.0 + mod_ref[0, 5:6, :]) * dz_ref[...]).astype(BF16)
            acc[...] = jnp.zeros_like(acc)

        dhm = _dot_nt(df_ref[...], wd_ref[0])
        av = a_ref[0].astype(F32)
        bv = b_ref[0].astype(F32)
        sg = jax.nn.sigmoid(av)
        sl = av * sg
        hm_ref[0] = (sl * bv).astype(BF16)
        da = (dhm * bv * (sg * (1.0 + av * (1.0 - sg)))).astype(BF16)
        db = (dhm * sl).astype(BF16)
        da_ref[0] = da
        db_ref[0] = db
        acc[...] += _dot(da, wg_ref[0]) + _dot(db, wu_ref[0])

        @pl.when(j == nf - 1)
        def _():
            dh2 = acc[...]
            x1v = x1_ref[...]
            uv = u_ref[...].astype(F32)
            dx1 = ALPHA * dz_ref[...] + dh2 * (1.0 + mod_ref[0, 4:5, :])
            xhat, rstd = _ln_stats(ALPHA * x_ref[...] + (1.0 + mod_ref[0, 2:3, :]) * uv)
            dz1 = _ln_bwd(dx1, xhat, rstd, lw_ref[...])
            du_ref[...] = ((1.0 + mod_ref[0, 2:3, :]) * dz1).astype(BF16)
            dxp_ref[...] = (ALPHA * dz1).astype(dxp_ref.dtype)
            upd = jnp.concatenate(
                [jnp.sum(dx1 * xhat, axis=0, keepdims=True), jnp.sum(dx1, axis=0, keepdims=True),
                 jnp.zeros((6, D), F32)], axis=0)
            dmu = jnp.concatenate(
                [jnp.zeros((2, D), F32), jnp.sum(dz1 * uv, axis=0, keepdims=True),
                 jnp.sum(dh2, axis=0, keepdims=True), jnp.sum(dh2 * x1v, axis=0, keepdims=True),
                 jnp.zeros((3, D), F32)], axis=0)

            @pl.when(i == 0)
            def _():
                st_ref[...] = upd

            @pl.when(i > 0)
            def _():
                st_ref[...] += upd

            @pl.when(i % tpb == 0)
            def _():
                dm_ref[0] = dmu

            @pl.when(i % tpb != 0)
            def _():
                dm_ref[0] += dmu

    row = lambda: pl.BlockSpec((tm, D), lambda i, j: (i, 0))
    ffb = lambda: pl.BlockSpec((1, tm, tf), lambda i, j: (j, i, 0))
    return pl.pallas_call(
        body, name="ffn_bwd", grid=(t // tm, nf),
        in_specs=[row(), ffb(), ffb(),
                  pl.BlockSpec((1, tf, D), lambda i, j: (j, 0, 0)), pl.BlockSpec((1, tf, D), lambda i, j: (j, 0, 0)),
                  pl.BlockSpec((1, tf, D), lambda i, j: (j, 0, 0)), row(), row(), row(),
                  pl.BlockSpec((1, 8, D), lambda i, j: (i // tpb, 0, 0)), pl.BlockSpec((1, D), lambda i, j: (0, 0))],
        out_specs=[ffb(), ffb(), ffb(), row(), row(), row(), pl.BlockSpec((8, D), lambda i, j: (0, 0)),
                   pl.BlockSpec((1, 8, D), lambda i, j: (i // tpb, 0, 0))],
        out_shape=[jax.ShapeDtypeStruct((nf, t, tf), BF16), jax.ShapeDtypeStruct((nf, t, tf), BF16),
                   jax.ShapeDtypeStruct((nf, t, tf), BF16), jax.ShapeDtypeStruct((t, D), BF16),
                   jax.ShapeDtypeStruct((t, D), BF16), jax.ShapeDtypeStruct((t, D), BF16),
                   jax.ShapeDtypeStruct((8, D), F32), jax.ShapeDtypeStruct((nbatch, 8, D), F32)],
        scratch_shapes=[pltpu.VMEM((tm, D), F32)],
        compiler_params=_params(("arbitrary", "arbitrary"), 60),
    )(dz2, a, b, wg, wu, wd, x1, x2, u, mod8, ln1w)


def _adamw_math(w, g, m, v):
    m = B1 * m + (1.0 - B1) * g
    v = B2 * v + (1.0 - B2) * (g * g)
    m_hat = m / (1.0 - B1 ** STEP)
    v_hat = v / (1.0 - B2 ** STEP)
    return -LR * (m_hat / (jnp.sqrt(v_hat) + EPS) + WD * w), m, v


def _adamw(w, g, m, v, name):
    rows, cols = w.shape
    tr = rows
    for cand in (128, 64, 32, 16, 8):
        if rows % cand == 0:
            tr = cand
            break

    def body(w_ref, g_ref, m_ref, v_ref, d_ref, mo_ref, vo_ref):
        d, mn, vn = _adamw_math(w_ref[...], g_ref[...], m_ref[...], v_ref[...])
        d_ref[...] = d
        mo_ref[...] = mn
        vo_ref[...] = vn

    spec = pl.BlockSpec((tr, cols), lambda i: (i, 0))
    return pl.pallas_call(
        body, name=name, grid=(rows // tr,), in_specs=[spec] * 4, out_specs=[spec] * 3,
        out_shape=[jax.ShapeDtypeStruct((rows, cols), F32)] * 3,
        compiler_params=_params(("parallel",), 48),
    )(w, g, m, v)


def _adamw_halves(w, g_mine, g_sib, m, v, c_idx, name):
    rows, cols = w.shape
    hr = rows // 2
    tr = next(cand for cand in (128, 88, 64, 32, 16, 8) if hr % cand == 0)
    tph = hr // tr

    def body(c_ref, w_ref, gm_ref, gs_ref, m_ref, v_ref, g_ref, d_ref, mo_ref, vo_ref):
        g = jnp.where(pl.program_id(0) == c_ref[0], gm_ref[...], gs_ref[...])
        d, mn, vn = _adamw_math(w_ref[...], g, m_ref[...], v_ref[...])
        g_ref[...] = g
        d_ref[...] = d
        mo_ref[...] = mn
        vo_ref[...] = vn

    full = pl.BlockSpec((tr, cols), lambda h, i, c: (h * tph + i, 0))
    half = pl.BlockSpec((tr, cols), lambda h, i, c: (i, 0))
    return pl.pallas_call(
        body, name=name,
        grid_spec=pltpu.PrefetchScalarGridSpec(
            num_scalar_prefetch=1, grid=(2, tph), in_specs=[full, half, half, full, full], out_specs=[full] * 4),
        out_shape=[jax.ShapeDtypeStruct((rows, cols), F32)] * 4,
        compiler_params=_params(("parallel", "parallel"), 48),
    )(c_idx, w, g_mine, g_sib, m, v)


def _grad_w_ada(c_all, dmod_cols):
    def body(c_ref, d_ref, o_ref):
        c = c_ref[...]
        o_ref[...] = lax.dot_general(c * jax.nn.sigmoid(c), d_ref[...], (((0,), (0,)), ((), ())),
                                     preferred_element_type=F32, precision=HIGHEST)

    return pl.pallas_call(
        body, name="grad_w_ada", out_shape=jax.ShapeDtypeStruct((D, dmod_cols.shape[1]), F32),
        compiler_params=_params(vmem_mb=48),
    )(c_all, dmod_cols)


def _small_update(gath, w8, m8, v8):
    def body(g_ref, w_ref, m_ref, v_ref, go_ref, d_ref, mo_ref, vo_ref):
        g0 = g_ref[0, 0:1, :] + g_ref[0, 1:2, :]
        g1 = g_ref[0, 2:3, :]
        for dev in range(1, N_DEV):
            g0 = g0 + (g_ref[dev, 0:1, :] + g_ref[dev, 1:2, :])
            g1 = g1 + g_ref[dev, 2:3, :]
        w = w_ref[...]
        lb = jax.nn.sigmoid(w[1:2, O_LB0:O_LB1] - w[1:2, O_LB1:O_FOX])
        fac = lb * (1.0 - lb)
        g1 = jnp.concatenate([g1[:, :O_LB0], g1[:, O_LB0:O_LB1] * fac, -g1[:, O_LB1:O_FOX] * fac, g1[:, O_FOX:]],
                             axis=1)
        g = jnp.concatenate([g0, g1, jnp.zeros((6, SMALL_W), F32)], axis=0)
        d, mn, vn = _adamw_math(w, g, m_ref[...], v_ref[...])
        go_ref[...] = g
        d_ref[...] = d
        mo_ref[...] = mn
        vo_ref[...] = vn

    return pl.pallas_call(
        body, name="small_update", out_shape=[jax.ShapeDtypeStruct((8, SMALL_W), F32)] * 4,
        compiler_params=_params(vmem_mb=48),
    )(gath, w8, m8, v8)


def _pack_small(b_ada, ln1w, ln1b, ln2w, ln2b, norm_w, lb_logits, fox):
    row1 = jnp.concatenate([ln1w, ln1b, ln2w, ln2b, norm_w, lb_logits[0:1], lb_logits[1:2], fox,
                            jnp.zeros((1, SMALL_W - O_FOX - BH), F32)], axis=1)
    return jnp.concatenate([b_ada, row1, jnp.zeros((6, SMALL_W), F32)], axis=0)


def _unpack_small(p):
    r = p[1:2]
    lb = jnp.concatenate([r[:, O_LB0:O_LB1], r[:, O_LB1:O_FOX]], axis=0)
    return dict(b_ada=p[0:1], ln1_w=r[:, O_LN1W:O_LN1B], ln1_b=r[:, O_LN1B:O_LN2W], ln2_w=r[:, O_LN2W:O_LN2B],
                ln2_b=r[:, O_LN2B:O_NORM], hgrn_norm_w=r[:, O_NORM:O_LB0], lb_logits=lb,
                fox_f_bias=r[:, O_FOX:O_FOX + BH])


_BIG = ("w_in", "w_branch_a", "w_branch_b", "w_out", "w_ffn_gate", "w_ffn_up", "w_ffn_down")
_TRANSPOSED = ("w_ffn_gate", "w_ffn_up")


def _cols_of_chips(stacked):
    return jnp.concatenate([stacked[k] for k in range(N_CHIPS)], axis=1)


def kernel(x, c, w_ada, b_ada, w_in, fox_f_bias, lb_logits, hgrn_norm_w, w_branch_a, w_branch_b, w_out, ln1_w, ln1_b, w_ffn_gate, w_ffn_up, w_ffn_down, ln2_w, ln2_b, loss_target, m_w_ada, m_b_ada, m_w_in, m_fox_f_bias, m_lb_logits, m_hgrn_norm_w, m_w_branch_a, m_w_branch_b, m_w_out, m_ln1_w, m_ln1_b, m_w_ffn_gate, m_w_ffn_up, m_w_ffn_down, m_ln2_w, m_ln2_b, v_w_ada, v_b_ada, v_w_in, v_fox_f_bias, v_lb_logits, v_hgrn_norm_w, v_w_branch_a, v_w_branch_b, v_w_out, v_ln1_w, v_ln1_b, v_w_ffn_gate, v_w_ffn_up, v_w_ffn_down, v_ln2_w, v_ln2_b):
    nbatch, seq, _ = x.shape
    t = nbatch * seq
    ax, ay, ac = lax.axis_index("x"), lax.axis_index("y"), lax.axis_index("c")
    chip = 2 * ax + ay
    dev = 2 * chip + ac
    chip_arr = jnp.reshape(chip, (1,)).astype(jnp.int32)
    core_arr = jnp.reshape(ac, (1,)).astype(jnp.int32)

    tr = lambda a: jnp.swapaxes(a[0], 0, 1)
    shard_w = dict(w_in=w_in[0], w_branch_a=w_branch_a[0], w_branch_b=w_branch_b[0], w_out=w_out[0],
                   w_ffn_gate=tr(w_ffn_gate), w_ffn_up=tr(w_ffn_up), w_ffn_down=w_ffn_down[0])
    shard_m = dict(w_in=m_w_in[0], w_branch_a=m_w_branch_a[0], w_branch_b=m_w_branch_b[0], w_out=m_w_out[0],
                   w_ffn_gate=tr(m_w_ffn_gate), w_ffn_up=tr(m_w_ffn_up), w_ffn_down=m_w_ffn_down[0])
    shard_v = dict(w_in=v_w_in[0], w_branch_a=v_w_branch_a[0], w_branch_b=v_w_branch_b[0], w_out=v_w_out[0],
                   w_ffn_gate=tr(v_w_ffn_gate), w_ffn_up=tr(v_w_ffn_up), w_ffn_down=v_w_ffn_down[0])

    shard16 = {n: shard_w[n].astype(BF16) for n in _BIG}

    def with_mine(gathered, n):
        return lax.dynamic_update_slice(gathered, shard16[n][None], (chip, 0, 0))

    def gather_start(names, tag):
        return _split_start(_gather_copies, [shard16[n] for n in names],
                            [lax.empty((N_CHIPS,) + shard16[n].shape, BF16) for n in names], "gather_" + tag + "_start")

    def gather_finish(split, after, tag):
        send, recv, src, land, _ = split
        return _pass_to_sibling(_split_wait(_gather_copies, send, recv, src, land, after, "gather_" + tag + "_wait"),
                                "gather_" + tag + "_pass")

    late = _BIG[1:]
    first_split = gather_start(("w_in",), "first")

    c8 = jnp.concatenate([c, jnp.zeros((8 - nbatch, D), F32)], axis=0)
    c_all = _allgather8(c8, "gather_c")[:, :nbatch, :].reshape(N_DEV * nbatch, D)
    ncol = w_ada.shape[2]
    b_cols = lax.dynamic_slice_in_dim(b_ada, chip * ncol, ncol, axis=1)
    mod_g = _allgather8(_mod_shard(c_all, w_ada[0], b_cols), "gather_mod")
    mod_all = jnp.concatenate([mod_g[2 * k] for k in range(N_CHIPS)], axis=1)
    mod_mine = lax.dynamic_slice_in_dim(mod_all, dev * nbatch, nbatch, axis=0)
    mod8 = jnp.concatenate([mod_mine.reshape(nbatch, 6, D), jnp.zeros((nbatch, 2, D), F32)], axis=1)
    mod8 = mod8 + first_split[4][0, 0]
    w_p = _permute_cols(_cols_of_chips(with_mine(gather_finish(first_split, mod8, "first")[0], "w_in")))
    shard16["w_out"] = shard16["w_out"] + (w_p[0:1, 0:1] * 0).astype(BF16)
    late_split = gather_start(late, "late")
    late_token = late_split[4]
    mod8 = mod8 + late_token[0, 0]

    x2 = x.reshape(t, D)
    tgt2 = loss_target.reshape(t, D)
    bias128 = jnp.concatenate([fox_f_bias, jnp.zeros((1, 128 - BH), F32)], axis=1)

    proj, h16 = _proj(x2, mod8, w_p, seq, BF16, "proj")
    projf = _rows_matmul(h16, w_p[:, COL_BF:], "proj_forget")
    ya, ckpt = _hgrn_fwd(proj, lb_logits, hgrn_norm_w, nbatch, seq)
    cum_cols = _fox_cum(projf, bias128, nbatch, seq)
    yb, lse = _fox_fwd(proj, cum_cols, nbatch, seq)
    full = {n: with_mine(g, n) for n, g in zip(late, gather_finish(late_split, yb, "late"))}
    wba, wbb = _cols_of_chips(full["w_branch_a"]), _cols_of_chips(full["w_branch_b"])
    wout = full["w_out"].reshape(D, D)
    wg_t, wu_t, wd = full["w_ffn_gate"], full["w_ffn_up"], full["w_ffn_down"]
    merged, u, x1 = _merge_fwd(ya, yb, proj, x2, mod8, wba, wbb, wout, ln1_w, ln1_b, seq)
    a_pre, b_pre, h2, dz2, st2, dm2 = _ffn_fwd(x1, mod8, wg_t, wu_t, wd, tgt2, ln2_w, ln2_b, seq)
    loss = lax.psum(st2[2, 0], ("x", "y", "c"))

    da, db, hmid, dffn, du, dxp, st1, dm1 = _ffn_bwd(dz2, a_pre, b_pre, wg_t, wu_t, wd, x1, x2, u, mod8, ln1_w, seq)
    g_st = {}
    g_st["w_ffn_down"] = _tn_matmul(hmid, dffn, "dw_ffn_down", seq)
    g_st["w_ffn_gate"] = _tn_matmul(da, h2, "dw_ffn_gate", seq)
    g_st["w_ffn_up"] = _tn_matmul(db, h2, "dw_ffn_up", seq)
    g_st["w_out"] = _tn_matmul(merged, du, "dw_out", seq).reshape(N_CHIPS, D // N_CHIPS, D)

    def sum_over_cores(names, tag):
        g_list = [g_st[n] for n in names]
        return [_add_my_half(g, o, core_arr, "grad_add_halves_" + n)
                for n, g, o in zip(names, g_list, _swap_halves(g_list, "grad_swap_halves_" + tag))]

    early = ("w_ffn_down", "w_ffn_gate", "w_ffn_up", "w_out")
    e_halves = sum_over_cores(early, "early")
    e_send, e_recv, e_src, e_land, e_token = _split_start(
        _scatter_copies, [h16 for _, h16 in e_halves],
        [lax.empty((3,) + h16.shape[1:], BF16) for _, h16 in e_halves], "grad_scatter_early_start")
    dproj, dpa, dpb, dya, dyb = _merge_bwd(du, ya, yb, proj, wba, wbb, wout, e_token, seq)
    g_st["w_branch_a"] = _tn_matmul(ya, dpa, "dw_branch_a", seq, split=D // N_CHIPS)
    g_st["w_branch_b"] = _tn_matmul(yb, dpb, "dw_branch_b", seq, split=D // N_CHIPS)
    dproj, dq, drs, dcs = _fox_bwd(proj, cum_cols, lse, yb, dyb, dproj, nbatch, seq)
    dproj = _place_cols(dproj, dq, COL_BQ)
    dproj, sm_fox = _fox_dbf(projf, bias128, drs, dcs, dproj, nbatch, seq)
    dproj, sm_hgrn = _hgrn_bwd(proj, dya, ckpt, lb_logits, hgrn_norm_w, dproj, nbatch, seq)
    grad_x2, dm0 = _dh_kernel(dproj, w_p, x2, dxp, mod8, seq)
    dw_in = _tn_matmul(h16, dproj, "dw_in", seq)
    g_st["w_in"] = _unpermute_to_chips(dw_in)

    e_recv = _split_wait(_scatter_copies, e_send, e_recv, e_src, e_land, dw_in, "grad_scatter_early_wait")
    rest = ("w_in", "w_branch_a", "w_branch_b")
    r_halves = sum_over_cores(rest, "rest")
    r_send, r_rcv, r_src, r_land, r_token = _split_start(
        _scatter_copies, [h16 for _, h16 in r_halves],
        [lax.empty((3,) + h16.shape[1:], BF16) for _, h16 in r_halves], "grad_scatter_rest_start")

    def finish(names, halves, recv, token, tag):
        g_mine = [_add_chips(h32, r, chip_arr, "grad_add_chips_" + n) for n, (h32, _), r in zip(names, halves, recv)]
        g_sib = _join_halves(g_mine, token, "grad_join_halves_" + tag)
        for n, gm, gs in zip(names, g_mine, g_sib):
            grads[n], deltas[n], new_m[n], new_v[n] = _adamw_halves(
                shard_w[n], gm, gs, shard_m[n], shard_v[n], core_arr, "adamw_" + n)

    grads, deltas, new_m, new_v = {}, {}, {}, {}
    finish(early, e_halves, e_recv, r_token, "early")

    dmod = (dm0 + dm1 + dm2)[:, :6, :].reshape(nbatch, 6 * D)
    row2 = jnp.concatenate([st1[0:1], st1[1:2], st2[0:1], st2[1:2], sm_hgrn[1:2], sm_hgrn[0:1], sm_hgrn[0:1],
                            sm_fox[0:1, :BH], jnp.zeros((1, SMALL_W - O_FOX - BH), F32)], axis=1)
    spack = jnp.concatenate([dmod, row2, jnp.zeros((8 - nbatch - 1, SMALL_W), F32)], axis=0)
    spack = spack + r_token[0, 0]
    gath = _allgather8(spack, "gather_small")
    w8 = _pack_small(b_ada, ln1_w, ln1_b, ln2_w, ln2_b, hgrn_norm_w, lb_logits, fox_f_bias)
    m8 = _pack_small(m_b_ada, m_ln1_w, m_ln1_b, m_ln2_w, m_ln2_b, m_hgrn_norm_w, m_lb_logits, m_fox_f_bias)
    v8 = _pack_small(v_b_ada, v_ln1_w, v_ln1_b, v_ln2_w, v_ln2_b, v_hgrn_norm_w, v_lb_logits, v_fox_f_bias)
    sg, sd, smn, svn = (_unpack_small(p) for p in _small_update(gath, w8, m8, v8))
    dmod_all = gath[:, :nbatch, :].reshape(N_DEV * nbatch, SMALL_W)
    g_ada = _grad_w_ada(c_all, lax.dynamic_slice_in_dim(dmod_all, chip * ncol, ncol, axis=1))

    for group, small in zip((grads, deltas, new_m, new_v), (sg, sd, smn, svn)):
        group.update(small)
    grads["w_ada"] = g_ada
    deltas["w_ada"], new_m["w_ada"], new_v["w_ada"] = _adamw(w_ada[0], g_ada, m_w_ada[0], v_w_ada[0], "adamw_w_ada")
    done = sum(new_v[n][0:8, 0:128] for n in early) + new_v["w_ada"][0:8, 0:128]
    r_recv = _split_wait(_scatter_copies, r_send, r_rcv, r_src, r_land, done, "grad_scatter_rest_wait")
    finish(rest, r_halves, r_recv, late_token, "rest")

    names = ["w_ada", "b_ada", "w_in", "fox_f_bias", "lb_logits", "hgrn_norm_w", "w_branch_a", "w_branch_b", "w_out",
             "ln1_w", "ln1_b", "w_ffn_gate", "w_ffn_up", "w_ffn_down", "ln2_w", "ln2_b"]
    shapes = dict(w_ada=w_ada.shape, b_ada=b_ada.shape, w_in=w_in.shape, fox_f_bias=fox_f_bias.shape,
                  lb_logits=lb_logits.shape, hgrn_norm_w=hgrn_norm_w.shape, w_branch_a=w_branch_a.shape,
                  w_branch_b=w_branch_b.shape, w_out=w_out.shape, ln1_w=ln1_w.shape, ln1_b=ln1_b.shape,
                  w_ffn_gate=w_ffn_gate.shape, w_ffn_up=w_ffn_up.shape, w_ffn_down=w_ffn_down.shape,
                  ln2_w=ln2_w.shape, ln2_b=ln2_b.shape)
    outs = [loss, grad_x2.reshape(x.shape)]
    for group in (grads, deltas, new_m, new_v):
        outs += [(jnp.swapaxes(group[n], 0, 1) if n in _TRANSPOSED else group[n]).reshape(shapes[n]) for n in names]
    return tuple(outs)
```

```python
import functools

import jax
import jax.numpy as jnp
from jax import lax
from jax.experimental import pallas as pl
from jax.experimental.pallas import tpu as pltpu

F32 = jnp.float32
BF16 = jnp.bfloat16
MESH = pl.DeviceIdType.MESH
HIGHEST = lax.Precision.HIGHEST

D = 1024
AW = 512
AH = 4
ADH = 128
BH = 8
BDH = 64
DFF = 2816
NIN = 5640
NP = 5760
N_CHIPS = 4
N_DEV = 8
HGRN_BLOCK = 256
FFN_TOKENS = 512
COL_GATES = 0
COL_A = 2048
COL_BQ = 4096
COL_KV = 4608
COL_BF = 5632
HGRN_HEADS = 4
FOX_PAIRS = 2
ALPHA = 2.0 ** 0.25
LN_EPS = 1e-5
RMS_EPS = 1e-6
NEG = -1e30
LOG2E = 1.4426950408889634
LR, B1, B2, EPS, WD, STEP = 0.001, 0.9, 0.999, 1e-08, 0.01, 10
SMALL_W = 6144
O_LN1W, O_LN1B, O_LN2W, O_LN2B, O_NORM, O_LB0, O_LB1, O_FOX = 0, 1024, 2048, 3072, 4096, 4608, 5120, 5632


def _params(sem=None, vmem_mb=None):
    kw = {}
    if sem is not None:
        kw["dimension_semantics"] = sem
    if vmem_mb is not None:
        kw["vmem_limit_bytes"] = vmem_mb << 20
    return pltpu.CompilerParams(**kw)


def _dot(a, b):
    return jnp.dot(a.astype(BF16), b.astype(BF16), preferred_element_type=F32)


def _dot_nt(a, b):
    return lax.dot_general(a.astype(BF16), b.astype(BF16), (((1,), (1,)), ((), ())), preferred_element_type=F32)


def _dot_tn(a, b):
    return lax.dot_general(a.astype(BF16), b.astype(BF16), (((0,), (0,)), ((), ())), preferred_element_type=F32)


def _dot_f32(a, b):
    return jnp.dot(a, b, preferred_element_type=F32, precision=HIGHEST)


def _perm_segments():
    segs = [(3592, 5640)]
    for h in range(4):
        segs += [(128 * h + 512 * t, 128 * h + 512 * t + 128) for t in range(4)]
    segs += [(2048, 2560)]
    for p in range(4):
        segs += [(2560 + 128 * p, 2688 + 128 * p), (3072 + 128 * p, 3200 + 128 * p)]
    segs += [(3584, 3592)]
    return segs


def _permute_from_chips(stacked):
    ncin = NIN // N_CHIPS
    parts = []
    for a, b in _perm_segments():
        for k in range(N_CHIPS):
            s, e = max(a, k * ncin), min(b, (k + 1) * ncin)
            if s < e:
                parts.append(stacked[k][:, s - k * ncin:e - k * ncin])
    parts.append(jnp.zeros((stacked.shape[1], NP - NIN), stacked.dtype))
    return jnp.concatenate(parts, axis=1)


def _unpermute_to_chips(g):
    pos, where = 0, []
    for a, b in _perm_segments():
        where.append((a, b, pos))
        pos += b - a
    ncin = NIN // N_CHIPS
    out = []
    for k in range(N_CHIPS):
        lo, hi = k * ncin, (k + 1) * ncin
        parts = [g[:, p + max(a, lo) - a:p + min(b, hi) - a] for a, b, p in sorted(where) if max(a, lo) < min(b, hi)]
        out.append(jnp.concatenate(parts, axis=1))
    return jnp.stack(out)


def _allgather8(v, name):
    rows, cols = v.shape

    def body(x_ref, out_ref, send_sems, recv_sems, local_sem):
        x, y, c = lax.axis_index("x"), lax.axis_index("y"), lax.axis_index("c")
        me, sibling = (x, y, c), (x, y, 1 - c)
        chips = [(1 - x, y), (x, 1 - y), (1 - x, 1 - y)]

        def slot(px, py, pc):
            return out_ref.at[4 * px + 2 * py + pc]

        def copy(k, block, to, src=None):
            return pltpu.make_async_remote_copy(
                src_ref=slot(*block) if src is None else src, dst_ref=slot(*block),
                send_sem=send_sems.at[k], recv_sem=recv_sems.at[k], device_id=to, device_id_type=MESH)

        mine = pltpu.make_async_copy(x_ref, slot(*me), local_sem)
        mine.start()
        first = [copy(0, me, sibling, src=x_ref)]
        first += [copy(1 + j, me, (*chip, c), src=x_ref) for j, chip in enumerate(chips)]
        for cp in first:
            cp.start()
        passed = [copy(4 + j, (*chip, c), sibling) for j, chip in enumerate(chips)]
        for j, chip in enumerate(chips):
            copy(1 + j, (*chip, c), me).wait_recv()
            passed[j].start()
        copy(0, sibling, me).wait_recv()
        for j, chip in enumerate(chips):
            copy(4 + j, (*chip, 1 - c), me).wait_recv()
        for cp in first + passed:
            cp.wait_send()
        mine.wait()

    return pl.pallas_call(
        body, name=name,
        out_shape=jax.ShapeDtypeStruct((N_DEV, rows, cols), v.dtype),
        in_specs=[pl.BlockSpec(memory_space=pltpu.VMEM)],
        out_specs=pl.BlockSpec(memory_space=pltpu.VMEM),
        scratch_shapes=[pltpu.SemaphoreType.DMA((7,)), pltpu.SemaphoreType.DMA((7,)), pltpu.SemaphoreType.DMA],
    )(v)


def _hbm_specs(n):
    return [pl.BlockSpec(memory_space=pl.ANY)] * n


def _swap_halves(grads, name):
    n = len(grads)

    def body(*refs):
        ins, outs, (send_sems, recv_sems) = refs[:n], refs[n:2 * n], refs[2 * n:]
        x, y, c = lax.axis_index("x"), lax.axis_index("y"), lax.axis_index("c")
        cps = []
        for w in range(n):
            hr = ins[w].shape[1] // 2
            cps.append(pltpu.make_async_remote_copy(
                src_ref=ins[w].at[:, pl.ds((1 - c) * hr, hr), :], dst_ref=outs[w],
                send_sem=send_sems.at[w], recv_sem=recv_sems.at[w], device_id=(x, y, 1 - c), device_id_type=MESH))
        for cp in cps:
            cp.start()
        for cp in cps:
            cp.wait()

    return pl.pallas_call(
        body, name=name,
        out_shape=[jax.ShapeDtypeStruct((N_CHIPS, g.shape[1] // 2, g.shape[2]), g.dtype) for g in grads],
        in_specs=_hbm_specs(n), out_specs=_hbm_specs(n),
        scratch_shapes=[pltpu.SemaphoreType.DMA((n,)), pltpu.SemaphoreType.DMA((n,))],
    )(*grads)


def _join_halves(halves, token, name):
    n = len(halves)

    def body(*refs):
        ins, outs, (send_sems, recv_sems) = refs[:n], refs[n + 1:2 * n + 1], refs[2 * n + 1:]
        x, y, c = lax.axis_index("x"), lax.axis_index("y"), lax.axis_index("c")
        cps = [pltpu.make_async_remote_copy(
            src_ref=ins[w], dst_ref=outs[w], send_sem=send_sems.at[w], recv_sem=recv_sems.at[w],
            device_id=(x, y, 1 - c), device_id_type=MESH) for w in range(n)]
        for cp in cps:
            cp.start()
        for cp in cps:
            cp.wait()

    return pl.pallas_call(
        body, name=name,
        out_shape=[jax.ShapeDtypeStruct(h.shape, h.dtype) for h in halves],
        in_specs=_hbm_specs(n + 1), out_specs=_hbm_specs(n),
        scratch_shapes=[pltpu.SemaphoreType.DMA((n,)), pltpu.SemaphoreType.DMA((n,))],
    )(*halves, token)


def _in_hbm(v):
    return pltpu.with_memory_space_constraint(v, pltpu.HBM)


_SPLIT_COPY = pltpu.CompilerParams(has_side_effects=pltpu.SideEffectType.DATAFLOW_SIDE_EFFECTING)


def _gather_copies(srcs, lands, send_sems, recv_sems):
    x, y, c = lax.axis_index("x"), lax.axis_index("y"), lax.axis_index("c")
    cps = []
    for w, (src, land) in enumerate(zip(srcs, lands)):
        hr = src.shape[0] // 2
        for j, chip in enumerate([(1 - x, y), (x, 1 - y), (1 - x, 1 - y)]):
            cps.append(pltpu.make_async_remote_copy(
                src_ref=src.at[pl.ds(c * hr, hr), :], dst_ref=land.at[2 * x + y, pl.ds(c * hr, hr), :],
                send_sem=send_sems.at[3 * w + j], recv_sem=recv_sems.at[3 * w + j],
                device_id=(*chip, c), device_id_type=MESH))
    return cps


def _scatter_copies(srcs, lands, send_sems, recv_sems):
    x, y, c = lax.axis_index("x"), lax.axis_index("y"), lax.axis_index("c")
    cps = []
    for w, (src, land) in enumerate(zip(srcs, lands)):
        for j, chip in enumerate([(1 - x, y), (x, 1 - y), (1 - x, 1 - y)]):
            cps.append(pltpu.make_async_remote_copy(
                src_ref=src.at[2 * chip[0] + chip[1]], dst_ref=land.at[j],
                send_sem=send_sems.at[3 * w + j], recv_sem=recv_sems.at[3 * w + j],
                device_id=(*chip, c), device_id_type=MESH))
    return cps


def _split_start(copies, srcs, lands, name):
    n = len(srcs)

    def body(*refs):
        src, lnd, send_sems, recv_sems, token = refs[:n], refs[n:2 * n], refs[2 * n], refs[2 * n + 1], refs[-1]
        for cp in copies(src, lnd, send_sems, recv_sems):
            cp.start()
        token[...] = jnp.zeros_like(token)

    hbm = pl.BlockSpec(memory_space=pltpu.HBM)
    sem = pl.BlockSpec(memory_space=pltpu.SEMAPHORE)
    outs = pl.pallas_call(
        body, name=name,
        out_shape=(pltpu.SemaphoreType.DMA((3 * n,)), pltpu.SemaphoreType.DMA((3 * n,)),
                   *[pltpu.HBM(v.shape, v.dtype) for v in srcs + lands], jax.ShapeDtypeStruct((8, 128), F32)),
        in_specs=[hbm] * (2 * n),
        out_specs=(sem, sem, *([hbm] * (2 * n)), pl.BlockSpec(memory_space=pltpu.VMEM)),
        input_output_aliases={i: 2 + i for i in range(2 * n)},
        compiler_params=_SPLIT_COPY,
    )(*[_in_hbm(v) for v in srcs + lands])
    return outs[0], outs[1], list(outs[2:2 + n]), list(outs[2 + n:2 + 2 * n]), outs[-1]


def _split_wait(copies, send_sems, recv_sems, srcs, lands, after, name):
    n = len(srcs)

    def body(*refs):
        src, lnd, send_sems, recv_sems = refs[:n], refs[n:2 * n], refs[2 * n], refs[2 * n + 1]
        for cp in copies(src, lnd, send_sems, recv_sems):
            cp.wait_send()
            cp.wait_recv()

    hbm = pl.BlockSpec(memory_space=pltpu.HBM)
    sem = pl.BlockSpec(memory_space=pltpu.SEMAPHORE)
    outs = pl.pallas_call(
        body, name=name,
        out_shape=tuple(pltpu.HBM(v.shape, v.dtype) for v in srcs + lands),
        in_specs=[hbm] * (2 * n) + [sem, sem, pl.BlockSpec(memory_space=pl.ANY)],
        out_specs=tuple([hbm] * (2 * n)),
        input_output_aliases={i: i for i in range(2 * n)},
        compiler_params=_SPLIT_COPY,
    )(*srcs, *lands, send_sems, recv_sems, after)
    return list(outs[n:])


def _pass_to_sibling(lands, name):
    n = len(lands)

    def body(*refs):
        ins, outs, (send_sems, recv_sems) = refs[:n], refs[n:2 * n], refs[2 * n:]
        x, y, c = lax.axis_index("x"), lax.axis_index("y"), lax.axis_index("c")
        cps = []
        for w in range(n):
            hr = ins[w].shape[1] // 2
            for j, chip in enumerate([(1 - x, y), (x, 1 - y), (1 - x, 1 - y)]):
                k = 2 * chip[0] + chip[1]
                cps.append(pltpu.make_async_remote_copy(
                    src_ref=ins[w].at[k, pl.ds(c * hr, hr), :], dst_ref=outs[w].at[k, pl.ds(c * hr, hr), :],
                    send_sem=send_sems.at[3 * w + j], recv_sem=recv_sems.at[3 * w + j],
                    device_id=(x, y, 1 - c), device_id_type=MESH))
        for cp in cps:
            cp.start()
        for cp in cps:
            cp.wait()

    return pl.pallas_call(
        body, name=name,
        out_shape=[jax.ShapeDtypeStruct(v.shape, v.dtype) for v in lands],
        in_specs=_hbm_specs(n), out_specs=_hbm_specs(n),
        input_output_aliases={i: i for i in range(n)},
        scratch_shapes=[pltpu.SemaphoreType.DMA((3 * n,)), pltpu.SemaphoreType.DMA((3 * n,))],
    )(*lands)


def _row_tile(rows):
    for cand in (256, 176, 128, 64, 32, 16):
        if rows % cand == 0:
            return cand
    raise ValueError(rows)


def _add_my_half(g, other, c_idx, name):
    _, k, n = g.shape
    hr = k // 2
    tr = _row_tile(hr)
    nb = hr // tr

    def body(c_ref, g_ref, o_ref, out_ref, out16_ref):
        s = g_ref[...] + o_ref[...]
        out_ref[...] = s
        out16_ref[...] = s.astype(BF16)

    return pl.pallas_call(
        body, name=name,
        grid_spec=pltpu.PrefetchScalarGridSpec(
            num_scalar_prefetch=1, grid=(N_CHIPS, nb),
            in_specs=[pl.BlockSpec((1, tr, n), lambda j, i, c: (j, c[0] * nb + i, 0)),
                      pl.BlockSpec((1, tr, n), lambda j, i, c: (j, i, 0))],
            out_specs=[pl.BlockSpec((1, tr, n), lambda j, i, c: (j, i, 0)),
                       pl.BlockSpec((1, tr, n), lambda j, i, c: (j, i, 0))]),
        out_shape=[jax.ShapeDtypeStruct((N_CHIPS, hr, n), F32), jax.ShapeDtypeStruct((N_CHIPS, hr, n), BF16)],
        compiler_params=_params(("parallel", "parallel")),
    )(c_idx, g, other)


def _add_chips(red, recv, chip_idx, name):
    _, hr, n = red.shape
    tr = _row_tile(hr)

    def body(k_ref, r_ref, v_ref, out_ref):
        out_ref[...] = ((r_ref[0] + v_ref[0].astype(F32)) + v_ref[1].astype(F32)) + v_ref[2].astype(F32)

    return pl.pallas_call(
        body, name=name,
        grid_spec=pltpu.PrefetchScalarGridSpec(
            num_scalar_prefetch=1, grid=(hr // tr,),
            in_specs=[pl.BlockSpec((1, tr, n), lambda i, k: (k[0], i, 0)),
                      pl.BlockSpec((3, tr, n), lambda i, k: (0, i, 0))],
            out_specs=pl.BlockSpec((tr, n), lambda i, k: (i, 0))),
        out_shape=jax.ShapeDtypeStruct((hr, n), F32),
        compiler_params=_params(("parallel",)),
    )(chip_idx, red, recv)


def _mod_shard(c_all, w_ada, b_ada):
    nb, cols = c_all.shape[0], w_ada.shape[1]

    def body(c_ref, w_ref, b_ref, o_ref):
        c = c_ref[...]
        o_ref[...] = _dot(c * jax.nn.sigmoid(c), w_ref[...]) + b_ref[...]

    return pl.pallas_call(
        body, name="mod_shard", out_shape=jax.ShapeDtypeStruct((nb, cols), F32),
        compiler_params=_params(vmem_mb=48),
    )(c_all, w_ada, b_ada)


def _proj(x2, mod8, w, seq, out_dtype, name):
    t = x2.shape[0]
    n = w.shape[1]
    tm, tn = min(2048, seq), min(1152, n)
    tpb = seq // tm

    def body(x_ref, mod_ref, w_ref, o_ref, h_ref):
        @pl.when(pl.program_id(1) == 0)
        def _():
            h_ref[...] = (x_ref[...] * (1.0 + mod_ref[0, 1:2, :]) + mod_ref[0, 0:1, :]).astype(BF16)
        o_ref[...] = jnp.dot(h_ref[...], w_ref[...], preferred_element_type=F32).astype(o_ref.dtype)

    return pl.pallas_call(
        body, name=name, grid=(t // tm, n // tn),
        in_specs=[pl.BlockSpec((tm, D), lambda i, j: (i, 0)),
                  pl.BlockSpec((1, 8, D), lambda i, j: (i // tpb, 0, 0)),
                  pl.BlockSpec((D, tn), lambda i, j: (0, j))],
        out_specs=[pl.BlockSpec((tm, tn), lambda i, j: (i, j)), pl.BlockSpec((tm, D), lambda i, j: (i, 0))],
        out_shape=[jax.ShapeDtypeStruct((t, n), out_dtype), jax.ShapeDtypeStruct((t, D), BF16)],
        compiler_params=_params(("parallel", "arbitrary"), 56),
    )(x2, mod8, w)


def _rows_matmul(a, w, name):
    t, k = a.shape
    n = w.shape[1]
    tm = 1024 if t % 1024 == 0 else t

    def body(a_ref, w_ref, o_ref):
        o_ref[...] = jnp.dot(a_ref[...], w_ref[...], preferred_element_type=F32)

    return pl.pallas_call(
        body, name=name, grid=(t // tm,),
        in_specs=[pl.BlockSpec((tm, k), lambda i: (i, 0)), pl.BlockSpec((k, n), lambda i: (0, 0))],
        out_specs=pl.BlockSpec((tm, n), lambda i: (i, 0)),
        out_shape=jax.ShapeDtypeStruct((t, n), F32),
        compiler_params=_params(("parallel",)),
    )(a, w)


def _tn_matmul(a, b, name, seq, split=None):
    a_st, b_st = a.ndim == 3, b.ndim == 3
    t, ka = a.shape[-2:]
    n = b.shape[-1]
    tt = min(1024, seq)
    nt = t // tt
    if a_st or b_st:
        steps, tn = (a.shape[0] if a_st else b.shape[0]), n
    else:
        tn = split
        if tn is None:
            tn = next(cand for cand in (1920, 1024, 1408, 512, n) if n % cand == 0)
        steps = n // tn
    stacked_out = a_st or b_st or split is not None

    def body(a_ref, b_ref, o_ref):
        part = _dot_tn(a_ref[0] if a_st else a_ref[...], b_ref[0] if b_st else b_ref[...])
        if stacked_out:
            part = part[None]

        @pl.when(pl.program_id(1) == 0)
        def _():
            o_ref[...] = part

        @pl.when(pl.program_id(1) > 0)
        def _():
            o_ref[...] += part

    if a_st:
        in_specs = [pl.BlockSpec((1, tt, ka), lambda j, k: (j, k, 0))]
    else:
        in_specs = [pl.BlockSpec((tt, ka), lambda j, k: (k, 0))]
    if b_st:
        in_specs.append(pl.BlockSpec((1, tt, n), lambda j, k: (j, k, 0)))
    else:
        in_specs.append(pl.BlockSpec((tt, tn), lambda j, k: (k, 0 if a_st else j)))
    if stacked_out:
        out_spec = pl.BlockSpec((1, ka, tn), lambda j, k: (j, 0, 0))
        out_shape = jax.ShapeDtypeStruct((steps, ka, tn), F32)
    else:
        out_spec = pl.BlockSpec((ka, tn), lambda j, k: (0, j))
        out_shape = jax.ShapeDtypeStruct((ka, n), F32)
    return pl.pallas_call(
        body, name=name, grid=(steps, nt), in_specs=in_specs, out_specs=out_spec, out_shape=out_shape,
        compiler_params=_params(("parallel", "arbitrary"), 56),
    )(a, b)


def _dh_kernel(dproj, w_p, x2, dxp, mod8, seq):
    t = x2.shape[0]
    tm, tk = min(1024, seq), 1920
    tpb = seq // tm
    nk = NP // tk
    nbatch = t // seq

    def body(dp_ref, w_ref, x_ref, dxp_ref, mod_ref, gx_ref, dm_ref, acc):
        i, k = pl.program_id(0), pl.program_id(1)

        @pl.when(k == 0)
        def _():
            acc[...] = jnp.zeros_like(acc)

        acc[...] += _dot_nt(dp_ref[...], w_ref[...])

        @pl.when(k == nk - 1)
        def _():
            dh = acc[...]
            gx_ref[...] = dxp_ref[...].astype(F32) + dh * (1.0 + mod_ref[0, 1:2, :])
            upd = jnp.concatenate(
                [jnp.sum(dh, axis=0, keepdims=True), jnp.sum(dh * x_ref[...], axis=0, keepdims=True),
                 jnp.zeros((6, D), F32)], axis=0)

            @pl.when(i % tpb == 0)
            def _():
                dm_ref[0] = upd

            @pl.when(i % tpb != 0)
            def _():
                dm_ref[0] += upd

    return pl.pallas_call(
        body, name="dh", grid=(t // tm, nk),
        in_specs=[pl.BlockSpec((tm, tk), lambda i, k: (i, k)),
                  pl.BlockSpec((D, tk), lambda i, k: (0, k)),
                  pl.BlockSpec((tm, D), lambda i, k: (i, 0)),
                  pl.BlockSpec((tm, D), lambda i, k: (i, 0)),
                  pl.BlockSpec((1, 8, D), lambda i, k: (i // tpb, 0, 0))],
        out_specs=[pl.BlockSpec((tm, D), lambda i, k: (i, 0)),
                   pl.BlockSpec((1, 8, D), lambda i, k: (i // tpb, 0, 0))],
        out_shape=[jax.ShapeDtypeStruct((t, D), F32), jax.ShapeDtypeStruct((nbatch, 8, D), F32)],
        scratch_shapes=[pltpu.VMEM((tm, D), F32)],
        compiler_params=_params(("arbitrary", "arbitrary"), 56),
    )(dproj, w_p, x2, dxp, mod8)


def _tri(n, upper):
    r = lax.broadcasted_iota(jnp.int32, (n, n), 0)
    c = lax.broadcasted_iota(jnp.int32, (n, n), 1)
    return jnp.where((c >= r) if upper else (c <= r), 1.0, 0.0).astype(F32)


@jax.custom_vjp
def _mm_nn(a, b):
    return _dot(a, b)


_mm_nn.defvjp(lambda a, b: (_dot(a, b), (a, b)),
              lambda res, g: (_dot_nt(g, res[1]), _dot_tn(res[0], g)))


@jax.custom_vjp
def _mm_nt(a, b):
    return _dot_nt(a, b)


_mm_nt.defvjp(lambda a, b: (_dot_nt(a, b), (a, b)),
              lambda res, g: (_dot(g, res[1]), _dot_tn(g, res[0])))


@jax.custom_vjp
def _mm_tn(a, b):
    return _dot_tn(a, b)


_mm_tn.defvjp(lambda a, b: (_dot_tn(a, b), (a, b)),
              lambda res, g: (_dot_nt(res[1], g), _dot(res[0], g)))


@jax.custom_vjp
def _cumsum_rows(x):
    return _dot_f32(_tri(x.shape[0], False), x)


_cumsum_rows.defvjp(lambda x: (_cumsum_rows(x), None),
                    lambda _, g: (_dot_f32(_tri(g.shape[0], True), g),))


@functools.partial(jax.custom_vjp, nondiff_argnums=(1,))
def _shift_rows(x, k):
    return pltpu.roll(x, k % x.shape[0], 0)


_shift_rows.defvjp(lambda x, k: (_shift_rows(x, k), None),
                   lambda k, _, g: (pltpu.roll(g, (-k) % g.shape[0], 0),))


def _group_ref(bc, m):
    n = bc.shape[0] // (2 * m)
    b3 = bc.reshape(n, 2 * m, ADH)
    row = lax.broadcasted_iota(jnp.int32, b3.shape, 1)
    ref = jnp.sum(jnp.where(row == m - 1, b3, 0.0), axis=1, keepdims=True)
    return jnp.broadcast_to(ref, b3.shape).reshape(bc.shape)


def _hgrn_block(q, fl, v, g, st, lb, nw):
    n = q.shape[0]
    f = lb + (1.0 - lb) * jax.nn.sigmoid(fl)
    kk = 1.0 - f
    lf = jnp.log(f)
    bc = _cumsum_rows(lf)
    row = lax.broadcasted_iota(jnp.int32, (n, ADH), 0)
    same = jnp.bitwise_xor(lax.broadcasted_iota(jnp.int32, (n, n), 0), lax.broadcasted_iota(jnp.int32, (n, n), 1))
    a = jnp.zeros((n, n), F32)
    m = 1
    while m < n:
        r = jnp.bitwise_and(row, 2 * m - 1)
        up, lo = r >= m, r < m
        if m == 1:
            aq, ak = lf, jnp.zeros_like(lf)
        elif m == 2:
            aq = jnp.where(r == 3, lf + _shift_rows(lf, 1), lf)
            ak = jnp.where(r == 0, _shift_rows(lf, -1), 0.0)
        else:
            ref = _group_ref(bc, m)
            aq, ak = bc - ref, ref - bc
        qt = jnp.where(up, q * jnp.exp(jnp.where(up, aq, 0.0)), 0.0)
        kt = jnp.where(lo, kk * jnp.exp(jnp.where(lo, ak, 0.0)), 0.0)
        a = a + jnp.where(same < 2 * m, _mm_nt(qt, kt), 0.0)
        m *= 2
    last = row == n - 1
    bl = jnp.sum(jnp.where(last, bc, 0.0), axis=0, keepdims=True)
    o = _mm_nn(a, v) + _mm_nt(q * jnp.exp(bc), st) + jnp.sum(q * kk, axis=-1, keepdims=True) * v
    st_new = st * jnp.exp(bl) + _mm_tn(v, kk * jnp.exp(bl - bc))
    rms = lax.rsqrt(jnp.mean(o * o, axis=-1, keepdims=True) + RMS_EPS)
    return o * rms * nw * jax.nn.sigmoid(g), st_new


def _hgrn_fwd(proj, lb_logits, norm_w, nbatch, seq):
    t = proj.shape[0]
    blk = min(HGRN_BLOCK, seq)
    nb = seq // blk

    nh = HGRN_HEADS
    wp, wy = 512 * nh, ADH * nh

    def body(p_ref, lbl_ref, nw_ref, y_ref, ck_ref, st_s):
        @pl.when(pl.program_id(2) == 0)
        def _():
            st_s[...] = jnp.zeros_like(st_s)

        st = [st_s[h] for h in range(nh)]
        p = p_ref[...].astype(F32)
        lb = jax.nn.sigmoid(lbl_ref[0:1, :] - lbl_ref[1:2, :])
        nw = nw_ref[...]
        res = [_hgrn_block(*(p[:, 512 * h + 128 * k:512 * h + 128 * k + 128] for k in range(4)), st[h],
                           lb[:, 128 * h:128 * h + 128], nw[:, 128 * h:128 * h + 128]) for h in range(nh)]
        for h in range(nh):
            ck_ref[0, h] = st[h]
            st_s[h] = res[h][1]
        y_ref[...] = jnp.concatenate([r[0] for r in res], axis=1).astype(y_ref.dtype)

    return pl.pallas_call(
        body, name="hgrn_fwd", grid=(AH // nh, nbatch, nb),
        in_specs=[pl.BlockSpec((blk, wp), lambda h, b, i: (b * nb + i, COL_A // wp + h)),
                  pl.BlockSpec((2, wy), lambda h, b, i: (0, h)),
                  pl.BlockSpec((1, wy), lambda h, b, i: (0, h))],
        out_specs=[pl.BlockSpec((blk, wy), lambda h, b, i: (b * nb + i, h)),
                   pl.BlockSpec((1, nh, 128, 128), lambda h, b, i: ((h * nbatch + b) * nb + i, 0, 0, 0))],
        out_shape=[jax.ShapeDtypeStruct((t, AW), BF16),
                   jax.ShapeDtypeStruct((AH // nh * nbatch * nb, nh, 128, 128), F32)],
        scratch_shapes=[pltpu.VMEM((nh, 128, 128), F32)],
        compiler_params=_params(("parallel", "parallel", "arbitrary"), 48),
    )(proj, lb_logits, norm_w)


def _hgrn_bwd(proj, dya, ckpt, lb_logits, norm_w, dproj, nbatch, seq):
    t = proj.shape[0]
    blk = min(HGRN_BLOCK, seq)
    nb = seq // blk

    nh = HGRN_HEADS
    wp, wy = 512 * nh, ADH * nh

    def body(p_ref, dy_ref, ck_ref, lbl_ref, nw_ref, dp_in, dp_ref, sm_ref, dst_s):
        del dp_in
        b_id, i = pl.program_id(1), pl.program_id(2)

        @pl.when(i == 0)
        def _():
            dst_s[...] = jnp.zeros_like(dst_s)

        dst = [dst_s[h] for h in range(nh)]
        st = [ck_ref[0, h] for h in range(nh)]
        p = p_ref[...].astype(F32)
        dy = dy_ref[...]
        lb = jax.nn.sigmoid(lbl_ref[0:1, :] - lbl_ref[1:2, :])
        nw = nw_ref[...]
        grads = []
        for h in range(nh):
            _, pullback = jax.vjp(_hgrn_block, *(p[:, 512 * h + 128 * k:512 * h + 128 * k + 128] for k in range(4)),
                                  st[h], lb[:, 128 * h:128 * h + 128], nw[:, 128 * h:128 * h + 128])
            grads.append(pullback((dy[:, 128 * h:128 * h + 128], dst[h])))
        for h in range(nh):
            dst_s[h] = grads[h][4]
        dp_ref[...] = jnp.concatenate([g[k] for g in grads for k in range(4)], axis=1).astype(dp_ref.dtype)
        upd = jnp.concatenate([jnp.concatenate([g[5] for g in grads], axis=1),
                               jnp.concatenate([g[6] for g in grads], axis=1), jnp.zeros((6, wy), F32)], axis=0)
        first = (b_id == 0) & (i == 0)

        @pl.when(first)
        def _():
            sm_ref[...] = upd

        @pl.when(jnp.logical_not(first))
        def _():
            sm_ref[...] += upd

    def rows(h, b, i):
        return b * nb + (nb - 1 - i)

    return pl.pallas_call(
        body, name="hgrn_bwd", grid=(AH // nh, nbatch, nb),
        in_specs=[pl.BlockSpec((blk, wp), lambda h, b, i: (rows(h, b, i), COL_A // wp + h)),
                  pl.BlockSpec((blk, wy), lambda h, b, i: (rows(h, b, i), h)),
                  pl.BlockSpec((1, nh, 128, 128), lambda h, b, i: ((h * nbatch + b) * nb + (nb - 1 - i), 0, 0, 0)),
                  pl.BlockSpec((2, wy), lambda h, b, i: (0, h)),
                  pl.BlockSpec((1, wy), lambda h, b, i: (0, h)),
                  pl.BlockSpec(memory_space=pl.ANY)],
        out_specs=[pl.BlockSpec((blk, wp), lambda h, b, i: (rows(h, b, i), COL_A // wp + h)),
                   pl.BlockSpec((8, wy), lambda h, b, i: (0, h))],
        out_shape=[jax.ShapeDtypeStruct((t, NP), BF16), jax.ShapeDtypeStruct((8, AW), F32)],
        input_output_aliases={5: 0},
        scratch_shapes=[pltpu.VMEM((nh, 128, 128), F32)],
        compiler_params=_params(("parallel", "arbitrary", "arbitrary"), 56),
    )(proj, dya, ckpt, lb_logits, norm_w, dproj)


def _log_sigmoid(z):
    return jnp.minimum(z, 0.0) - jnp.log(1.0 + jnp.exp(-jnp.abs(z)))


def _fox_cum(proj, bias128, nbatch, seq):
    t = proj.shape[0]
    ts = min(512, seq)
    nb = seq // ts

    def body(p_ref, b_ref, c_ref, carry):
        @pl.when(pl.program_id(1) == 0)
        def _():
            carry[...] = jnp.zeros_like(carry)
        cum = _dot_f32(_tri(ts, False), _log_sigmoid(p_ref[...] + b_ref[...])) + carry[...]
        carry[...] = cum[ts - 1:ts, :]
        cum2 = cum * LOG2E
        lane = lax.broadcasted_iota(jnp.int32, (ts, 128), 1)
        for p in range(4):
            c_ref[p] = jnp.where(lane < 64, cum2[:, 2 * p:2 * p + 1], cum2[:, 2 * p + 1:2 * p + 2])

    return pl.pallas_call(
        body, name="fox_cum", grid=(nbatch, nb),
        in_specs=[pl.BlockSpec((ts, 128), lambda b, i: (b * nb + i, 0)),
                  pl.BlockSpec((1, 128), lambda b, i: (0, 0))],
        out_specs=pl.BlockSpec((4, ts, 128), lambda b, i: (0, b * nb + i, 0)),
        out_shape=jax.ShapeDtypeStruct((4, t, 128), F32),
        scratch_shapes=[pltpu.VMEM((1, 128), F32)],
        compiler_params=_params(("parallel", "arbitrary")),
    )(proj, bias128)


def _fox_scores_t(q128, k128, cc128, hh, masked):
    tq, tk = q128.shape[0], k128.shape[0]
    qh = _head_lanes((q128 * (LOG2E * BDH ** -0.5)).astype(BF16), hh)
    s = _dot_nt(k128, qh) - cc128[:, 64 * hh:64 * hh + 1]
    if masked:
        key = lax.broadcasted_iota(jnp.int32, (tk, tq), 0)
        qry = lax.broadcasted_iota(jnp.int32, (tk, tq), 1)
        s = jnp.where(key <= qry, s, NEG)
    return s


def _causal_pairs(nq, key_major):
    if key_major:
        pairs = [(i, j) for j in range(nq) for i in range(j, nq)]
    else:
        pairs = [(i, j) for i in range(nq) for j in range(i + 1)]
    return (jnp.asarray([p[0] for p in pairs], jnp.int32), jnp.asarray([p[1] for p in pairs], jnp.int32))


def _head_lanes(x128, hh):
    lane = lax.broadcasted_iota(jnp.int32, x128.shape, 1)
    return jnp.where((lane < 64) if hh == 0 else (lane >= 64), x128, jnp.zeros_like(x128))


def _with_ones_lane(x128, hh):
    lane = lax.broadcasted_iota(jnp.int32, x128.shape, 1)
    one = jnp.ones_like(x128)
    zero = jnp.zeros_like(x128)
    if hh == 0:
        return jnp.where(lane < 64, x128, jnp.where(lane == 64, one, zero))
    return jnp.where(lane >= 64, x128, jnp.where(lane == 0, one, zero))


def _fox_fwd(proj, cum_cols, nbatch, seq):
    t = proj.shape[0]
    tq = tk = min(512, seq)
    nq = seq // tq
    npr = FOX_PAIRS
    qi, kj = _causal_pairs(nq, key_major=False)

    def body(qi_ref, kj_ref, q_ref, kv_ref, cc_ref, o_ref, lse_ref, m_s, acc_s):
        s_id = pl.program_id(2)
        i, j = qi_ref[s_id], kj_ref[s_id]

        @pl.when(j == 0)
        def _():
            m_s[...] = jnp.full_like(m_s, NEG)
            acc_s[...] = jnp.zeros_like(acc_s)

        def step(masked):
            heads = [(pr, hh) for pr in range(npr) for hh in range(2)]
            m_prev = m_s[0:2 * npr, :]
            acc_prev = [acc_s[h] for h in range(2 * npr)]
            q128 = [q_ref[:, 128 * pr:128 * pr + 128] for pr in range(npr)]
            k128 = [kv_ref[:, 256 * pr:256 * pr + 128].astype(BF16) for pr in range(npr)]
            v128 = [kv_ref[:, 256 * pr + 128:256 * pr + 256].astype(BF16) for pr in range(npr)]
            s = [_fox_scores_t(q128[pr], k128[pr], cc_ref[pr], hh, masked) for pr, hh in heads]
            m_new = [jnp.maximum(m_prev[h:h + 1, :], jnp.max(s[h], axis=0, keepdims=True)) for h in range(2 * npr)]
            acc_new = []
            for h, (pr, hh) in enumerate(heads):
                alpha = jnp.exp2(m_prev[h:h + 1, :] - m_new[h])
                p = jnp.exp2(s[h] - m_new[h]).astype(BF16)
                acc_new.append(acc_prev[h] * alpha + _dot_tn(_with_ones_lane(v128[pr], hh), p))
            for h in range(2 * npr):
                acc_s[h] = acc_new[h]
            m_s[0:2 * npr, :] = jnp.concatenate(m_new, axis=0)

        @pl.when(j < i)
        def _():
            step(False)

        @pl.when(j == i)
        def _():
            step(True)
            outs = []
            for pr in range(npr):
                a0, a1 = acc_s[2 * pr], acc_s[2 * pr + 1]
                l0, l1 = a0[64:65, :], a1[0:1, :]
                outs.append(jnp.concatenate([a0[0:64, :] / l0, a1[64:128, :] / l1], axis=0).T)
                lse_ref[0, pr] = jnp.concatenate(
                    [m_s[2 * pr:2 * pr + 1, :] + jnp.log2(l0), m_s[2 * pr + 1:2 * pr + 2, :] + jnp.log2(l1),
                     jnp.zeros((6, tq), F32)], axis=0)
            o_ref[...] = jnp.concatenate(outs, axis=1).astype(o_ref.dtype)

    return pl.pallas_call(
        body, name="fox_fwd",
        grid_spec=pltpu.PrefetchScalarGridSpec(
            num_scalar_prefetch=2, grid=(nbatch, 4 // npr, qi.shape[0]),
            in_specs=[pl.BlockSpec((tq, 128 * npr), lambda b, p, s, qi, kj: (b * nq + qi[s], COL_BQ // (128 * npr) + p)),
                      pl.BlockSpec((tk, 256 * npr), lambda b, p, s, qi, kj: (b * nq + kj[s], COL_KV // (256 * npr) + p)),
                      pl.BlockSpec((npr, tk, 128), lambda b, p, s, qi, kj: (p, b * nq + kj[s], 0))],
            out_specs=[pl.BlockSpec((tq, 128 * npr), lambda b, p, s, qi, kj: (b * nq + qi[s], p)),
                       pl.BlockSpec((1, npr, 8, tq), lambda b, p, s, qi, kj: (b, p, 0, qi[s]))],
            scratch_shapes=[pltpu.VMEM((8, tq), F32), pltpu.VMEM((2 * npr, 128, tq), F32)]),
        out_shape=[jax.ShapeDtypeStruct((t, 512), BF16), jax.ShapeDtypeStruct((nbatch, 4, 8, seq), F32)],
        compiler_params=_params(("parallel", "parallel", "arbitrary"), 56),
    )(qi, kj, proj, proj, cum_cols)


def _fox_bwd(proj, cum_cols, lse, yb, dyb, dproj, nbatch, seq):
    t = proj.shape[0]
    tq = tk = min(512, seq)
    nq = seq // tq
    scale = BDH ** -0.5
    qi, kj = _causal_pairs(nq, key_major=True)
    nsteps = qi.shape[0]

    npr = FOX_PAIRS

    def body(qi_ref, kj_ref, q_ref, kv_ref, cc_ref, lse_ref, o_ref, do_ref, dp_in,
             dkv_ref, dq_ref, drs_ref, dcs_ref, dk_s, dv_s, dqa_s):
        del dp_in
        pg, s_id = pl.program_id(1), pl.program_id(2)
        i, j = qi_ref[s_id], kj_ref[s_id]

        @pl.when(i == j)
        def _():
            dk_s[...] = jnp.zeros_like(dk_s)
            dv_s[...] = jnp.zeros_like(dv_s)

        @pl.when(s_id == 0)
        def _():
            dqa_s[...] = jnp.zeros_like(dqa_s)

        def step(masked):
            dk_prev = [dk_s[h] for h in range(2 * npr)]
            dq_prev = [dqa_s[i, h] for h in range(2 * npr)]
            dv_new = [dv_s[pr] for pr in range(npr)]
            dk_new, dq_new = [], []
            for pr in range(npr):
                lanes = slice(128 * pr, 128 * pr + 128)
                q128 = q_ref[:, lanes]
                qs128 = (q128 * scale).astype(BF16)
                k128 = kv_ref[:, 256 * pr:256 * pr + 128].astype(BF16)
                v128 = kv_ref[:, 256 * pr + 128:256 * pr + 256].astype(BF16)
                do128 = do_ref[:, lanes]
                doo = do128 * o_ref[:, lanes].astype(F32)
                do16 = do128.astype(BF16)
                for hh in range(2):
                    s = _fox_scores_t(q128, k128, cc_ref[pr], hh, masked)
                    p = jnp.exp2(s - lse_ref[0, pr, hh:hh + 1, :])
                    dd = lax.dot_general(jnp.ones((8, 128), F32), _head_lanes(doo, hh), (((1,), (1,)), ((), ())),
                                         preferred_element_type=F32, precision=HIGHEST)[0:1, :]
                    doh = _head_lanes(do16, hh)
                    dp = _dot_nt(v128, doh)
                    ds = (p * (dp - dd)).astype(BF16)
                    dv_new[pr] = dv_new[pr] + _dot(p, doh)
                    dk_new.append(dk_prev[2 * pr + hh] + _dot(ds, _with_ones_lane(qs128, hh)))
                    dq_new.append(dq_prev[2 * pr + hh] + _dot_tn(_with_ones_lane(k128, hh), ds))
            for pr in range(npr):
                dv_s[pr] = dv_new[pr]
            for h in range(2 * npr):
                dk_s[h] = dk_new[h]
                dqa_s[i, h] = dq_new[h]

        @pl.when(i == j)
        def _():
            step(True)

        @pl.when(i > j)
        def _():
            step(False)

        def sums_to_lanes(lane, pr, s0, s1):
            hp = npr * pg + pr
            return jnp.where(lane == 2 * hp, s0, jnp.where(lane == 2 * hp + 1, s1, 0.0))

        @pl.when(i == nq - 1)
        def _():
            lane = lax.broadcasted_iota(jnp.int32, (tk, 128), 1)
            for pr in range(npr):
                k0, k1 = dk_s[2 * pr], dk_s[2 * pr + 1]
                dkv_ref[:, 256 * pr:256 * pr + 128] = jnp.where(lane < 64, k0, k1).astype(dkv_ref.dtype)
                dkv_ref[:, 256 * pr + 128:256 * pr + 256] = dv_s[pr].astype(dkv_ref.dtype)
                dcs_ref[pr] = sums_to_lanes(lane, pr, k0[:, 64:65], k1[:, 0:1])

        @pl.when(s_id == nsteps - 1)
        def _():
            lane = lax.broadcasted_iota(jnp.int32, (tq, 128), 1)
            for blk in range(nq):
                rows = pl.ds(blk * tq, tq)
                for pr in range(npr):
                    a0 = dqa_s[blk, 2 * pr].T
                    a1 = dqa_s[blk, 2 * pr + 1].T
                    dq_ref[rows, 128 * pr:128 * pr + 128] = (jnp.where(lane < 64, a0, a1) * scale).astype(dq_ref.dtype)
                    drs_ref[pr, rows, :] = sums_to_lanes(lane, pr, a0[:, 64:65], a1[:, 0:1])

    return pl.pallas_call(
        body, name="fox_bwd",
        grid_spec=pltpu.PrefetchScalarGridSpec(
            num_scalar_prefetch=2, grid=(nbatch, 4 // npr, nsteps),
            in_specs=[pl.BlockSpec((tq, 128 * npr), lambda b, p, s, qi, kj: (b * nq + qi[s], COL_BQ // (128 * npr) + p)),
                      pl.BlockSpec((tk, 256 * npr), lambda b, p, s, qi, kj: (b * nq + kj[s], COL_KV // (256 * npr) + p)),
                      pl.BlockSpec((npr, tk, 128), lambda b, p, s, qi, kj: (p, b * nq + kj[s], 0)),
                      pl.BlockSpec((1, npr, 8, tq), lambda b, p, s, qi, kj: (b, p, 0, qi[s])),
                      pl.BlockSpec((tq, 128 * npr), lambda b, p, s, qi, kj: (b * nq + qi[s], p)),
                      pl.BlockSpec((tq, 128 * npr), lambda b, p, s, qi, kj: (b * nq + qi[s], p)),
                      pl.BlockSpec(memory_space=pl.ANY)],
            out_specs=[pl.BlockSpec((tk, 256 * npr), lambda b, p, s, qi, kj: (b * nq + kj[s], COL_KV // (256 * npr) + p)),
                       pl.BlockSpec((seq, 128 * npr), lambda b, p, s, qi, kj: (b, p)),
                       pl.BlockSpec((npr, seq, 128), lambda b, p, s, qi, kj: (p, b, 0)),
                       pl.BlockSpec((npr, tk, 128), lambda b, p, s, qi, kj: (p, b * nq + kj[s], 0))],
            scratch_shapes=[pltpu.VMEM((2 * npr, tk, 128), F32), pltpu.VMEM((npr, tk, 128), F32),
                            pltpu.VMEM((nq, 2 * npr, 128, tq), F32)]),
        out_shape=[jax.ShapeDtypeStruct((t, NP), BF16), jax.ShapeDtypeStruct((t, 512), BF16),
                   jax.ShapeDtypeStruct((4, t, 128), F32), jax.ShapeDtypeStruct((4, t, 128), F32)],
        input_output_aliases={8: 0},
        compiler_params=_params(("parallel", "parallel", "arbitrary"), 60),
    )(qi, kj, proj, proj, cum_cols, lse, yb, dyb, dproj)


def _place_cols(dproj, src, col):
    t, w = src.shape
    tm = 1024 if t % 1024 == 0 else t

    def body(s_ref, dp_in, o_ref):
        del dp_in
        o_ref[...] = s_ref[...]

    return pl.pallas_call(
        body, name="place_cols", grid=(t // tm,),
        in_specs=[pl.BlockSpec((tm, w), lambda i: (i, 0)), pl.BlockSpec(memory_space=pl.ANY)],
        out_specs=pl.BlockSpec((tm, w), lambda i: (i, col // w)),
        out_shape=jax.ShapeDtypeStruct(dproj.shape, dproj.dtype),
        input_output_aliases={1: 0},
        compiler_params=_params(("parallel",)),
    )(src, dproj)


def _fox_dbf(proj, bias128, drs, dcs, dproj, nbatch, seq):
    t = proj.shape[0]
    ts = min(512, seq)
    nb = seq // ts

    def body(p_ref, b_ref, dr_ref, dc_ref, dp_in, dp_ref, sm_ref, carry):
        del dp_in
        b_id, i = pl.program_id(0), pl.program_id(1)

        @pl.when(i == 0)
        def _():
            carry[...] = jnp.zeros_like(carry)

        dcum = (dr_ref[0] - dc_ref[0]) + (dr_ref[1] - dc_ref[1]) + (dr_ref[2] - dc_ref[2]) + (dr_ref[3] - dc_ref[3])
        rc = _dot_f32(_tri(ts, True), dcum) + carry[...]
        carry[...] = rc[0:1, :]
        z = p_ref[...] + b_ref[...]
        lane = lax.broadcasted_iota(jnp.int32, (ts, 128), 1)
        dz = jnp.where(lane < BH, rc * jax.nn.sigmoid(-z), 0.0)
        dp_ref[...] = dz.astype(dp_ref.dtype)
        upd = jnp.concatenate([jnp.sum(dz, axis=0, keepdims=True), jnp.zeros((7, 128), F32)], axis=0)
        first = (b_id == 0) & (i == 0)

        @pl.when(first)
        def _():
            sm_ref[...] = upd

        @pl.when(jnp.logical_not(first))
        def _():
            sm_ref[...] += upd

    def rows(b, i):
        return b * nb + (nb - 1 - i)

    return pl.pallas_call(
        body, name="fox_dbf", grid=(nbatch, nb),
        in_specs=[pl.BlockSpec((ts, 128), lambda b, i: (rows(b, i), 0)),
                  pl.BlockSpec((1, 128), lambda b, i: (0, 0)),
                  pl.BlockSpec((4, ts, 128), lambda b, i: (0, rows(b, i), 0)),
                  pl.BlockSpec((4, ts, 128), lambda b, i: (0, rows(b, i), 0)),
                  pl.BlockSpec(memory_space=pl.ANY)],
        out_specs=[pl.BlockSpec((ts, 128), lambda b, i: (rows(b, i), COL_BF // 128)),
                   pl.BlockSpec((8, 128), lambda b, i: (0, 0))],
        out_shape=[jax.ShapeDtypeStruct((t, NP), BF16), jax.ShapeDtypeStruct((8, 128), F32)],
        input_output_aliases={4: 0},
        scratch_shapes=[pltpu.VMEM((1, 128), F32)],
        compiler_params=_params(("arbitrary", "arbitrary")),
    )(proj, bias128, drs, dcs, dproj)


def _ln_stats(z):
    mu = jnp.mean(z, axis=-1, keepdims=True)
    zc = z - mu
    rstd = lax.rsqrt(jnp.mean(zc * zc, axis=-1, keepdims=True) + LN_EPS)
    return zc * rstd, rstd


def _ln_bwd(dy, xhat, rstd, w):
    dxh = dy * w
    return rstd * (dxh - jnp.mean(dxh, axis=-1, keepdims=True) - xhat * jnp.mean(dxh * xhat, axis=-1, keepdims=True))


def _merge_fwd(ya, yb, proj, x2, mod8, wba, wbb, wout, ln1w, ln1b, seq):
    t = x2.shape[0]
    tm = min(512, seq)
    tpb = seq // tm

    def body(ya_ref, yb_ref, g_ref, x_ref, mod_ref, wa_ref, wb_ref, wo_ref, lw_ref, lb_ref, mg_ref, u_ref, x1_ref):
        ga = jax.nn.sigmoid(g_ref[:, 0:D].astype(F32))
        gb = jax.nn.sigmoid(g_ref[:, D:2 * D].astype(F32))
        merged = (ga * jnp.dot(ya_ref[...], wa_ref[...], preferred_element_type=F32)
                  + gb * jnp.dot(yb_ref[...], wb_ref[...], preferred_element_type=F32))
        mg = merged.astype(BF16)
        mg_ref[...] = mg
        u = jnp.dot(mg, wo_ref[...], preferred_element_type=F32)
        u_ref[...] = u.astype(u_ref.dtype)
        xhat, _ = _ln_stats(ALPHA * x_ref[...] + (1.0 + mod_ref[0, 2:3, :]) * u)
        x1_ref[...] = xhat * lw_ref[...] + lb_ref[...]

    tok = lambda w: pl.BlockSpec((tm, w), lambda i: (i, 0))
    full = lambda a: pl.BlockSpec(a.shape, lambda i: (0,) * a.ndim)
    return pl.pallas_call(
        body, name="merge_fwd", grid=(t // tm,),
        in_specs=[tok(512), tok(512), pl.BlockSpec((tm, 2048), lambda i: (i, COL_GATES // 2048)), tok(D),
                  pl.BlockSpec((1, 8, D), lambda i: (i // tpb, 0, 0)),
                  full(wba), full(wbb), full(wout), full(ln1w), full(ln1b)],
        out_specs=[tok(D), tok(D), tok(D)],
        out_shape=[jax.ShapeDtypeStruct((t, D), BF16), jax.ShapeDtypeStruct((t, D), BF16),
                   jax.ShapeDtypeStruct((t, D), F32)],
        compiler_params=_params(("parallel",), 48),
    )(ya, yb, proj, x2, mod8, wba, wbb, wout, ln1w, ln1b)


def _merge_bwd(du, ya, yb, proj, wba, wbb, wout, token, seq):
    t = du.shape[0]
    tm = min(512, seq)

    def body(du_ref, ya_ref, yb_ref, g_ref, wa_ref, wb_ref, wo_ref, token_ref,
             dp_ref, dpa_ref, dpb_ref, dya_ref, dyb_ref):
        del token_ref
        ga = jax.nn.sigmoid(g_ref[:, 0:D].astype(F32))
        gb = jax.nn.sigmoid(g_ref[:, D:2 * D].astype(F32))
        dm = _dot_nt(du_ref[...], wo_ref[...])
        pa = jnp.dot(ya_ref[...], wa_ref[...], preferred_element_type=F32)
        pb = jnp.dot(yb_ref[...], wb_ref[...], preferred_element_type=F32)
        dpa = (dm * ga).astype(BF16)
        dpb = (dm * gb).astype(BF16)
        dpa_ref[...] = dpa
        dpb_ref[...] = dpb
        dp_ref[:, 0:D] = (dm * pa * ga * (1.0 - ga)).astype(BF16)
        dp_ref[:, D:2 * D] = (dm * pb * gb * (1.0 - gb)).astype(BF16)
        dya_ref[...] = _dot_nt(dpa, wa_ref[...])
        dyb_ref[...] = _dot_nt(dpb, wb_ref[...])

    tok = lambda w: pl.BlockSpec((tm, w), lambda i: (i, 0))
    full = lambda a: pl.BlockSpec(a.shape, lambda i: (0,) * a.ndim)
    return pl.pallas_call(
        body, name="merge_bwd", grid=(t // tm,),
        in_specs=[tok(D), tok(512), tok(512), pl.BlockSpec((tm, 2048), lambda i: (i, COL_GATES // 2048)),
                  full(wba), full(wbb), full(wout), full(token)],
        out_specs=[pl.BlockSpec((tm, 2048), lambda i: (i, COL_GATES // 2048)), tok(D), tok(D), tok(512), tok(512)],
        out_shape=[jax.ShapeDtypeStruct((t, NP), BF16), jax.ShapeDtypeStruct((t, D), BF16),
                   jax.ShapeDtypeStruct((t, D), BF16), jax.ShapeDtypeStruct((t, 512), F32),
                   jax.ShapeDtypeStruct((t, 512), F32)],
        compiler_params=_params(("parallel",), 48),
    )(du, ya, yb, proj, wba, wbb, wout, token)


def _ffn_fwd(x1, mod8, wg, wu, wd, target, ln2w, ln2b, seq):
    t = x1.shape[0]
    tm = min(FFN_TOKENS, seq)
    nf, tf, _ = wg.shape
    tpb = seq // tm
    nbatch = t // seq

    def body(x_ref, mod_ref, wg_ref, wu_ref, wd_ref, t_ref, lw_ref, lb_ref,
             a_ref, b_ref, h_s, dz_ref, st_ref, dm_ref, acc):
        i, j = pl.program_id(0), pl.program_id(1)

        @pl.when(j == 0)
        def _():
            h_s[...] = (x_ref[...] * (1.0 + mod_ref[0, 4:5, :]) + mod_ref[0, 3:4, :]).astype(BF16)
            acc[...] = jnp.zeros_like(acc)

        a = _dot_nt(h_s[...], wg_ref[0])
        b = _dot_nt(h_s[...], wu_ref[0])
        a_ref[0] = a.astype(BF16)
        b_ref[0] = b.astype(BF16)
        acc[...] += _dot(a * jax.nn.sigmoid(a) * b, wd_ref[0])

        @pl.when(j == nf - 1)
        def _():
            ffn = acc[...]
            xhat, rstd = _ln_stats(ALPHA * x_ref[...] + (1.0 + mod_ref[0, 5:6, :]) * ffn)
            diff = xhat * lw_ref[...] + lb_ref[...] - t_ref[...]
            loss = 0.5 * jnp.sum(jnp.sum(diff * diff, axis=-1, keepdims=True), axis=0, keepdims=True) / D
            dy = diff * (1.0 / D)
            dz = _ln_bwd(dy, xhat, rstd, lw_ref[...])
            dz_ref[...] = dz
            lane = lax.broadcasted_iota(jnp.int32, (1, D), 1)
            upd = jnp.concatenate(
                [jnp.sum(dy * xhat, axis=0, keepdims=True), jnp.sum(dy, axis=0, keepdims=True),
                 jnp.where(lane == 0, loss, 0.0), jnp.zeros((5, D), F32)], axis=0)
            dmu = jnp.concatenate(
                [jnp.zeros((5, D), F32), jnp.sum(dz * ffn, axis=0, keepdims=True), jnp.zeros((2, D), F32)], axis=0)

            @pl.when(i == 0)
            def _():
                st_ref[...] = upd

            @pl.when(i > 0)
            def _():
                st_ref[...] += upd

            @pl.when(i % tpb == 0)
            def _():
                dm_ref[0] = dmu

            @pl.when(i % tpb != 0)
            def _():
                dm_ref[0] += dmu

    row = lambda: pl.BlockSpec((tm, D), lambda i, j: (i, 0))
    vec = lambda: pl.BlockSpec((1, D), lambda i, j: (0, 0))
    return pl.pallas_call(
        body, name="ffn_fwd", grid=(t // tm, nf),
        in_specs=[row(), pl.BlockSpec((1, 8, D), lambda i, j: (i // tpb, 0, 0)),
                  pl.BlockSpec((1, tf, D), lambda i, j: (j, 0, 0)), pl.BlockSpec((1, tf, D), lambda i, j: (j, 0, 0)),
                  pl.BlockSpec((1, tf, D), lambda i, j: (j, 0, 0)), row(), vec(), vec()],
        out_specs=[pl.BlockSpec((1, tm, tf), lambda i, j: (j, i, 0)), pl.BlockSpec((1, tm, tf), lambda i, j: (j, i, 0)),
                   row(), row(), pl.BlockSpec((8, D), lambda i, j: (0, 0)),
                   pl.BlockSpec((1, 8, D), lambda i, j: (i // tpb, 0, 0))],
        out_shape=[jax.ShapeDtypeStruct((nf, t, tf), BF16), jax.ShapeDtypeStruct((nf, t, tf), BF16),
                   jax.ShapeDtypeStruct((t, D), BF16),
                   jax.ShapeDtypeStruct((t, D), F32), jax.ShapeDtypeStruct((8, D), F32),
                   jax.ShapeDtypeStruct((nbatch, 8, D), F32)],
        scratch_shapes=[pltpu.VMEM((tm, D), F32)],
        compiler_params=_params(("arbitrary", "arbitrary"), 60),
    )(x1, mod8, wg, wu, wd, target, ln2w, ln2b)


def _ffn_bwd(dz2, a, b, wg, wu, wd, x1, x2, u, mod8, ln1w, seq):
    t = x1.shape[0]
    tm = min(512, seq)
    nf, tf, _ = wg.shape
    tpb = seq // tm
    nbatch = t // seq

    def body(dz_ref, a_ref, b_ref, wg_ref, wu_ref, wd_ref, x1_ref, x_ref, u_ref, mod_ref, lw_ref,
             da_ref, db_ref, hm_ref, df_ref, du_ref, dxp_ref, st_ref, dm_ref, acc):
        i, j = pl.program_id(0), pl.program_id(1)

        @pl.when(j == 0)
        def _():
            df_ref[...] = ((1.0 + mod_ref[0, 5:6, :]) * dz_ref[...]).astype(BF16)
            acc[...] = jnp.zeros_like(acc)

        dhm = _dot_nt(df_ref[...], wd_ref[0])
        av = a_ref[0].astype(F32)
        bv = b_ref[0].astype(F32)
        sg = jax.nn.sigmoid(av)
        sl = av * sg
        hm_ref[0] = (sl * bv).astype(BF16)
        da = (dhm * bv * (sg * (1.0 + av * (1.0 - sg)))).astype(BF16)
        db = (dhm * sl).astype(BF16)
        da_ref[0] = da
        db_ref[0] = db
        acc[...] += _dot(da, wg_ref[0]) + _dot(db, wu_ref[0])

        @pl.when(j == nf - 1)
        def _():
            dh2 = acc[...]
            x1v = x1_ref[...]
            uv = u_ref[...].astype(F32)
            dx1 = ALPHA * dz_ref[...] + dh2 * (1.0 + mod_ref[0, 4:5, :])
            xhat, rstd = _ln_stats(ALPHA * x_ref[...] + (1.0 + mod_ref[0, 2:3, :]) * uv)
            dz1 = _ln_bwd(dx1, xhat, rstd, lw_ref[...])
            du_ref[...] = ((1.0 + mod_ref[0, 2:3, :]) * dz1).astype(BF16)
            dxp_ref[...] = (ALPHA * dz1).astype(dxp_ref.dtype)
            upd = jnp.concatenate(
                [jnp.sum(dx1 * xhat, axis=0, keepdims=True), jnp.sum(dx1, axis=0, keepdims=True),
                 jnp.zeros((6, D), F32)], axis=0)
            dmu = jnp.concatenate(
                [jnp.zeros((2, D), F32), jnp.sum(dz1 * uv, axis=0, keepdims=True),
                 jnp.sum(dh2, axis=0, keepdims=True), jnp.sum(dh2 * x1v, axis=0, keepdims=True),
                 jnp.zeros((3, D), F32)], axis=0)

            @pl.when(i == 0)
            def _():
                st_ref[...] = upd

            @pl.when(i > 0)
            def _():
                st_ref[...] += upd

            @pl.when(i % tpb == 0)
            def _():
                dm_ref[0] = dmu

            @pl.when(i % tpb != 0)
            def _():
                dm_ref[0] += dmu

    row = lambda: pl.BlockSpec((tm, D), lambda i, j: (i, 0))
    ffb = lambda: pl.BlockSpec((1, tm, tf), lambda i, j: (j, i, 0))
    return pl.pallas_call(
        body, name="ffn_bwd", grid=(t // tm, nf),
        in_specs=[row(), ffb(), ffb(),
                  pl.BlockSpec((1, tf, D), lambda i, j: (j, 0, 0)), pl.BlockSpec((1, tf, D), lambda i, j: (j, 0, 0)),
                  pl.BlockSpec((1, tf, D), lambda i, j: (j, 0, 0)), row(), row(), row(),
                  pl.BlockSpec((1, 8, D), lambda i, j: (i // tpb, 0, 0)), pl.BlockSpec((1, D), lambda i, j: (0, 0))],
        out_specs=[ffb(), ffb(), ffb(), row(), row(), row(), pl.BlockSpec((8, D), lambda i, j: (0, 0)),
                   pl.BlockSpec((1, 8, D), lambda i, j: (i // tpb, 0, 0))],
        out_shape=[jax.ShapeDtypeStruct((nf, t, tf), BF16), jax.ShapeDtypeStruct((nf, t, tf), BF16),
                   jax.ShapeDtypeStruct((nf, t, tf), BF16), jax.ShapeDtypeStruct((t, D), BF16),
                   jax.ShapeDtypeStruct((t, D), BF16), jax.ShapeDtypeStruct((t, D), BF16),
                   jax.ShapeDtypeStruct((8, D), F32), jax.ShapeDtypeStruct((nbatch, 8, D), F32)],
        scratch_shapes=[pltpu.VMEM((tm, D), F32)],
        compiler_params=_params(("arbitrary", "arbitrary"), 60),
    )(dz2, a, b, wg, wu, wd, x1, x2, u, mod8, ln1w)


def _adamw_math(w, g, m, v):
    m = B1 * m + (1.0 - B1) * g
    v = B2 * v + (1.0 - B2) * (g * g)
    m_hat = m / (1.0 - B1 ** STEP)
    v_hat = v / (1.0 - B2 ** STEP)
    return -LR * (m_hat / (jnp.sqrt(v_hat) + EPS) + WD * w), m, v


def _adamw(w, g, m, v, name):
    rows, cols = w.shape
    tr = rows
    for cand in (128, 64, 32, 16, 8):
        if rows % cand == 0:
            tr = cand
            break

    def body(w_ref, g_ref, m_ref, v_ref, d_ref, mo_ref, vo_ref):
        d, mn, vn = _adamw_math(w_ref[...], g_ref[...], m_ref[...], v_ref[...])
        d_ref[...] = d
        mo_ref[...] = mn
        vo_ref[...] = vn

    spec = pl.BlockSpec((tr, cols), lambda i: (i, 0))
    return pl.pallas_call(
        body, name=name, grid=(rows // tr,), in_specs=[spec] * 4, out_specs=[spec] * 3,
        out_shape=[jax.ShapeDtypeStruct((rows, cols), F32)] * 3,
        compiler_params=_params(("parallel",), 48),
    )(w, g, m, v)


def _adamw_halves(w, g_mine, g_sib, m, v, c_idx, name):
    rows, cols = w.shape
    hr = rows // 2
    tr = next(cand for cand in (128, 88, 64, 32, 16, 8) if hr % cand == 0)
    tph = hr // tr

    def body(c_ref, w_ref, gm_ref, gs_ref, m_ref, v_ref, g_ref, d_ref, mo_ref, vo_ref):
        g = jnp.where(pl.program_id(0) == c_ref[0], gm_ref[...], gs_ref[...])
        d, mn, vn = _adamw_math(w_ref[...], g, m_ref[...], v_ref[...])
        g_ref[...] = g
        d_ref[...] = d
        mo_ref[...] = mn
        vo_ref[...] = vn

    full = pl.BlockSpec((tr, cols), lambda h, i, c: (h * tph + i, 0))
    half = pl.BlockSpec((tr, cols), lambda h, i, c: (i, 0))
    return pl.pallas_call(
        body, name=name,
        grid_spec=pltpu.PrefetchScalarGridSpec(
            num_scalar_prefetch=1, grid=(2, tph), in_specs=[full, half, half, full, full], out_specs=[full] * 4),
        out_shape=[jax.ShapeDtypeStruct((rows, cols), F32)] * 4,
        compiler_params=_params(("parallel", "parallel"), 48),
    )(c_idx, w, g_mine, g_sib, m, v)


def _grad_w_ada(c_all, dmod_cols):
    def body(c_ref, d_ref, o_ref):
        c = c_ref[...]
        o_ref[...] = lax.dot_general(c * jax.nn.sigmoid(c), d_ref[...], (((0,), (0,)), ((), ())),
                                     preferred_element_type=F32, precision=HIGHEST)

    return pl.pallas_call(
        body, name="grad_w_ada", out_shape=jax.ShapeDtypeStruct((D, dmod_cols.shape[1]), F32),
        compiler_params=_params(vmem_mb=48),
    )(c_all, dmod_cols)


def _small_update(gath, w8, m8, v8):
    def body(g_ref, w_ref, m_ref, v_ref, go_ref, d_ref, mo_ref, vo_ref):
        g0 = g_ref[0, 0:1, :] + g_ref[0, 1:2, :]
        g1 = g_ref[0, 2:3, :]
        for dev in range(1, N_DEV):
            g0 = g0 + (g_ref[dev, 0:1, :] + g_ref[dev, 1:2, :])
            g1 = g1 + g_ref[dev, 2:3, :]
        w = w_ref[...]
        lb = jax.nn.sigmoid(w[1:2, O_LB0:O_LB1] - w[1:2, O_LB1:O_FOX])
        fac = lb * (1.0 - lb)
        g1 = jnp.concatenate([g1[:, :O_LB0], g1[:, O_LB0:O_LB1] * fac, -g1[:, O_LB1:O_FOX] * fac, g1[:, O_FOX:]],
                             axis=1)
        g = jnp.concatenate([g0, g1, jnp.zeros((6, SMALL_W), F32)], axis=0)
        d, mn, vn = _adamw_math(w, g, m_ref[...], v_ref[...])
        go_ref[...] = g
        d_ref[...] = d
        mo_ref[...] = mn
        vo_ref[...] = vn

    return pl.pallas_call(
        body, name="small_update", out_shape=[jax.ShapeDtypeStruct((8, SMALL_W), F32)] * 4,
        compiler_params=_params(vmem_mb=48),
    )(gath, w8, m8, v8)


def _pack_small(b_ada, ln1w, ln1b, ln2w, ln2b, norm_w, lb_logits, fox):
    row1 = jnp.concatenate([ln1w, ln1b, ln2w, ln2b, norm_w, lb_logits[0:1], lb_logits[1:2], fox,
                            jnp.zeros((1, SMALL_W - O_FOX - BH), F32)], axis=1)
    return jnp.concatenate([b_ada, row1, jnp.zeros((6, SMALL_W), F32)], axis=0)


def _unpack_small(p):
    r = p[1:2]
    lb = jnp.concatenate([r[:, O_LB0:O_LB1], r[:, O_LB1:O_FOX]], axis=0)
    return dict(b_ada=p[0:1], ln1_w=r[:, O_LN1W:O_LN1B], ln1_b=r[:, O_LN1B:O_LN2W], ln2_w=r[:, O_LN2W:O_LN2B],
                ln2_b=r[:, O_LN2B:O_NORM], hgrn_norm_w=r[:, O_NORM:O_LB0], lb_logits=lb,
                fox_f_bias=r[:, O_FOX:O_FOX + BH])


_BIG = ("w_in", "w_branch_a", "w_branch_b", "w_out", "w_ffn_gate", "w_ffn_up", "w_ffn_down")
_TRANSPOSED = ("w_ffn_gate", "w_ffn_up")


def _cols_of_chips(stacked):
    return jnp.concatenate([stacked[k] for k in range(N_CHIPS)], axis=1)


def kernel(x, c, w_ada, b_ada, w_in, fox_f_bias, lb_logits, hgrn_norm_w, w_branch_a, w_branch_b, w_out, ln1_w, ln1_b, w_ffn_gate, w_ffn_up, w_ffn_down, ln2_w, ln2_b, loss_target, m_w_ada, m_b_ada, m_w_in, m_fox_f_bias, m_lb_logits, m_hgrn_norm_w, m_w_branch_a, m_w_branch_b, m_w_out, m_ln1_w, m_ln1_b, m_w_ffn_gate, m_w_ffn_up, m_w_ffn_down, m_ln2_w, m_ln2_b, v_w_ada, v_b_ada, v_w_in, v_fox_f_bias, v_lb_logits, v_hgrn_norm_w, v_w_branch_a, v_w_branch_b, v_w_out, v_ln1_w, v_ln1_b, v_w_ffn_gate, v_w_ffn_up, v_w_ffn_down, v_ln2_w, v_ln2_b):
    nbatch, seq, _ = x.shape
    t = nbatch * seq
    ax, ay, ac = lax.axis_index("x"), lax.axis_index("y"), lax.axis_index("c")
    chip = 2 * ax + ay
    dev = 2 * chip + ac
    chip_arr = jnp.reshape(chip, (1,)).astype(jnp.int32)
    core_arr = jnp.reshape(ac, (1,)).astype(jnp.int32)

    tr = lambda a: jnp.swapaxes(a[0], 0, 1)
    shard_w = dict(w_in=w_in[0], w_branch_a=w_branch_a[0], w_branch_b=w_branch_b[0], w_out=w_out[0],
                   w_ffn_gate=tr(w_ffn_gate), w_ffn_up=tr(w_ffn_up), w_ffn_down=w_ffn_down[0])
    shard_m = dict(w_in=m_w_in[0], w_branch_a=m_w_branch_a[0], w_branch_b=m_w_branch_b[0], w_out=m_w_out[0],
                   w_ffn_gate=tr(m_w_ffn_gate), w_ffn_up=tr(m_w_ffn_up), w_ffn_down=m_w_ffn_down[0])
    shard_v = dict(w_in=v_w_in[0], w_branch_a=v_w_branch_a[0], w_branch_b=v_w_branch_b[0], w_out=v_w_out[0],
                   w_ffn_gate=tr(v_w_ffn_gate), w_ffn_up=tr(v_w_ffn_up), w_ffn_down=v_w_ffn_down[0])

    shard16 = {n: shard_w[n].astype(BF16) for n in _BIG}

    def with_mine(gathered, n):
        return lax.dynamic_update_slice(gathered, shard16[n][None], (chip, 0, 0))

    def gather_start(names, tag):
        return _split_start(_gather_copies, [shard16[n] for n in names],
                            [lax.empty((N_CHIPS,) + shard16[n].shape, BF16) for n in names], "gather_" + tag + "_start")

    def gather_finish(split, after, tag):
        send, recv, src, land, _ = split
        return _pass_to_sibling(_split_wait(_gather_copies, send, recv, src, land, after, "gather_" + tag + "_wait"),
                                "gather_" + tag + "_pass")

    late = _BIG[1:]
    first_split = gather_start(("w_in",), "first")
    late_split = gather_start(late, "late")
    late_token = first_split[4] + late_split[4]

    c8 = jnp.concatenate([c, jnp.zeros((8 - nbatch, D), F32)], axis=0)
    c_all = _allgather8(c8, "gather_c")[:, :nbatch, :].reshape(N_DEV * nbatch, D)
    ncol = w_ada.shape[2]
    b_cols = lax.dynamic_slice_in_dim(b_ada, chip * ncol, ncol, axis=1)
    mod_g = _allgather8(_mod_shard(c_all, w_ada[0], b_cols), "gather_mod")
    mod_all = jnp.concatenate([mod_g[2 * k] for k in range(N_CHIPS)], axis=1)
    mod_mine = lax.dynamic_slice_in_dim(mod_all, dev * nbatch, nbatch, axis=0)
    mod8 = jnp.concatenate([mod_mine.reshape(nbatch, 6, D), jnp.zeros((nbatch, 2, D), F32)], axis=1)
    mod8 = mod8 + late_token[0, 0]
    w_p = _permute_from_chips(with_mine(gather_finish(first_split, mod8, "first")[0], "w_in"))

    x2 = x.reshape(t, D)
    tgt2 = loss_target.reshape(t, D)
    bias128 = jnp.concatenate([fox_f_bias, jnp.zeros((1, 128 - BH), F32)], axis=1)

    proj, h16 = _proj(x2, mod8, w_p, seq, BF16, "proj")
    projf = _rows_matmul(h16, w_p[:, COL_BF:], "proj_forget")
    ya, ckpt = _hgrn_fwd(proj, lb_logits, hgrn_norm_w, nbatch, seq)
    cum_cols = _fox_cum(projf, bias128, nbatch, seq)
    yb, lse = _fox_fwd(proj, cum_cols, nbatch, seq)
    full = {n: with_mine(g, n) for n, g in zip(late, gather_finish(late_split, yb, "late"))}
    wba, wbb = _cols_of_chips(full["w_branch_a"]), _cols_of_chips(full["w_branch_b"])
    wout = full["w_out"].reshape(D, D)
    wg_t, wu_t, wd = full["w_ffn_gate"], full["w_ffn_up"], full["w_ffn_down"]
    merged, u, x1 = _merge_fwd(ya, yb, proj, x2, mod8, wba, wbb, wout, ln1_w, ln1_b, seq)
    a_pre, b_pre, h2, dz2, st2, dm2 = _ffn_fwd(x1, mod8, wg_t, wu_t, wd, tgt2, ln2_w, ln2_b, seq)
    loss = lax.psum(st2[2, 0], ("x", "y", "c"))

    da, db, hmid, dffn, du, dxp, st1, dm1 = _ffn_bwd(dz2, a_pre, b_pre, wg_t, wu_t, wd, x1, x2, u, mod8, ln1_w, seq)
    g_st = {}
    g_st["w_ffn_down"] = _tn_matmul(hmid, dffn, "dw_ffn_down", seq)
    g_st["w_ffn_gate"] = _tn_matmul(da, h2, "dw_ffn_gate", seq)
    g_st["w_ffn_up"] = _tn_matmul(db, h2, "dw_ffn_up", seq)
    g_st["w_out"] = _tn_matmul(merged, du, "dw_out", seq).reshape(N_CHIPS, D // N_CHIPS, D)

    def sum_over_cores(names, tag):
        g_list = [g_st[n] for n in names]
        return [_add_my_half(g, o, core_arr, "grad_add_halves_" + n)
                for n, g, o in zip(names, g_list, _swap_halves(g_list, "grad_swap_halves_" + tag))]

    early = ("w_ffn_down", "w_ffn_gate", "w_ffn_up", "w_out")
    e_halves = sum_over_cores(early, "early")
    e_send, e_recv, e_src, e_land, e_token = _split_start(
        _scatter_copies, [h16 for _, h16 in e_halves],
        [lax.empty((3,) + h16.shape[1:], BF16) for _, h16 in e_halves], "grad_scatter_early_start")
    dproj, dpa, dpb, dya, dyb = _merge_bwd(du, ya, yb, proj, wba, wbb, wout, e_token, seq)
    g_st["w_branch_a"] = _tn_matmul(ya, dpa, "dw_branch_a", seq, split=D // N_CHIPS)
    g_st["w_branch_b"] = _tn_matmul(yb, dpb, "dw_branch_b", seq, split=D // N_CHIPS)
    dproj, dq, drs, dcs = _fox_bwd(proj, cum_cols, lse, yb, dyb, dproj, nbatch, seq)
    dproj = _place_cols(dproj, dq, COL_BQ)
    dproj, sm_fox = _fox_dbf(projf, bias128, drs, dcs, dproj, nbatch, seq)
    dproj, sm_hgrn = _hgrn_bwd(proj, dya, ckpt, lb_logits, hgrn_norm_w, dproj, nbatch, seq)
    grad_x2, dm0 = _dh_kernel(dproj, w_p, x2, dxp, mod8, seq)
    dw_in = _tn_matmul(h16, dproj, "dw_in", seq)
    g_st["w_in"] = _unpermute_to_chips(dw_in)

    e_recv = _split_wait(_scatter_copies, e_send, e_recv, e_src, e_land, dw_in, "grad_scatter_early_wait")
    rest = ("w_in", "w_branch_a", "w_branch_b")
    r_halves = sum_over_cores(rest, "rest")
    r_send, r_rcv, r_src, r_land, r_token = _split_start(
        _scatter_copies, [h16 for _, h16 in r_halves],
        [lax.empty((3,) + h16.shape[1:], BF16) for _, h16 in r_halves], "grad_scatter_rest_start")

    def finish(names, halves, recv, token, tag):
        g_mine = [_add_chips(h32, r, chip_arr, "grad_add_chips_" + n) for n, (h32, _), r in zip(names, halves, recv)]
        g_sib = _join_halves(g_mine, token, "grad_join_halves_" + tag)
        for n, gm, gs in zip(names, g_mine, g_sib):
            grads[n], deltas[n], new_m[n], new_v[n] = _adamw_halves(
                shard_w[n], gm, gs, shard_m[n], shard_v[n], core_arr, "adamw_" + n)

    grads, deltas, new_m, new_v = {}, {}, {}, {}
    finish(early, e_halves, e_recv, r_token, "early")

    dmod = (dm0 + dm1 + dm2)[:, :6, :].reshape(nbatch, 6 * D)
    row2 = jnp.concatenate([st1[0:1], st1[1:2], st2[0:1], st2[1:2], sm_hgrn[1:2], sm_hgrn[0:1], sm_hgrn[0:1],
                            sm_fox[0:1, :BH], jnp.zeros((1, SMALL_W - O_FOX - BH), F32)], axis=1)
    spack = jnp.concatenate([dmod, row2, jnp.zeros((8 - nbatch - 1, SMALL_W), F32)], axis=0)
    spack = spack + r_token[0, 0]
    gath = _allgather8(spack, "gather_small")
    w8 = _pack_small(b_ada, ln1_w, ln1_b, ln2_w, ln2_b, hgrn_norm_w, lb_logits, fox_f_bias)
    m8 = _pack_small(m_b_ada, m_ln1_w, m_ln1_b, m_ln2_w, m_ln2_b, m_hgrn_norm_w, m_lb_logits, m_fox_f_bias)
    v8 = _pack_small(v_b_ada, v_ln1_w, v_ln1_b, v_ln2_w, v_ln2_b, v_hgrn_norm_w, v_lb_logits, v_fox_f_bias)
    sg, sd, smn, svn = (_unpack_small(p) for p in _small_update(gath, w8, m8, v8))
    dmod_all = gath[:, :nbatch, :].reshape(N_DEV * nbatch, SMALL_W)
    g_ada = _grad_w_ada(c_all, lax.dynamic_slice_in_dim(dmod_all, chip * ncol, ncol, axis=1))

    for group, small in zip((grads, deltas, new_m, new_v), (sg, sd, smn, svn)):
        group.update(small)
    grads["w_ada"] = g_ada
    deltas["w_ada"], new_m["w_ada"], new_v["w_ada"] = _adamw(w_ada[0], g_ada, m_w_ada[0], v_w_ada[0], "adamw_w_ada")
    done = sum(new_v[n][0:8, 0:128] for n in early) + new_v["w_ada"][0:8, 0:128]
    r_recv = _split_wait(_scatter_copies, r_send, r_rcv, r_src, r_land, done, "grad_scatter_rest_wait")
    finish(rest, r_halves, r_recv, late_token, "rest")

    names = ["w_ada", "b_ada", "w_in", "fox_f_bias", "lb_logits", "hgrn_norm_w", "w_branch_a", "w_branch_b", "w_out",
             "ln1_w", "ln1_b", "w_ffn_gate", "w_ffn_up", "w_ffn_down", "ln2_w", "ln2_b"]
    shapes = dict(w_ada=w_ada.shape, b_ada=b_ada.shape, w_in=w_in.shape, fox_f_bias=fox_f_bias.shape,
                  lb_logits=lb_logits.shape, hgrn_norm_w=hgrn_norm_w.shape, w_branch_a=w_branch_a.shape,
                  w_branch_b=w_branch_b.shape, w_out=w_out.shape, ln1_w=ln1_w.shape, ln1_b=ln1_b.shape,
                  w_ffn_gate=w_ffn_gate.shape, w_ffn_up=w_ffn_up.shape, w_ffn_down=w_ffn_down.shape,
                  ln2_w=ln2_w.shape, ln2_b=ln2_b.shape)
    outs = [loss, grad_x2.reshape(x.shape)]
    for group in (grads, deltas, new_m, new_v):
        outs += [(jnp.swapaxes(group[n], 0, 1) if n in _TRANSPOSED else group[n]).reshape(shapes[n]) for n in names]
    return tuple(outs)
```

```python
import functools

import jax
import jax.numpy as jnp
from jax import lax
from jax.experimental import pallas as pl
from jax.experimental.pallas import tpu as pltpu

F32 = jnp.float32
BF16 = jnp.bfloat16
MESH = pl.DeviceIdType.MESH
HIGHEST = lax.Precision.HIGHEST

D = 1024
AW = 512
AH = 4
ADH = 128
BH = 8
BDH = 64
DFF = 2816
NIN = 5640
NP = 5760
N_CHIPS = 4
N_DEV = 8
HGRN_BLOCK = 256
FFN_TOKENS = 512
COL_GATES = 0
COL_A = 2048
COL_BQ = 4096
COL_KV = 4608
COL_BF = 5632
HGRN_HEADS = 4
FOX_PAIRS = 2
ALPHA = 2.0 ** 0.25
LN_EPS = 1e-5
RMS_EPS = 1e-6
NEG = -1e30
LOG2E = 1.4426950408889634
LR, B1, B2, EPS, WD, STEP = 0.001, 0.9, 0.999, 1e-08, 0.01, 10
SMALL_W = 6144
O_LN1W, O_LN1B, O_LN2W, O_LN2B, O_NORM, O_LB0, O_LB1, O_FOX = 0, 1024, 2048, 3072, 4096, 4608, 5120, 5632


def _params(sem=None, vmem_mb=None):
    kw = {}
    if sem is not None:
        kw["dimension_semantics"] = sem
    if vmem_mb is not None:
        kw["vmem_limit_bytes"] = vmem_mb << 20
    return pltpu.CompilerParams(**kw)


def _dot(a, b):
    return jnp.dot(a.astype(BF16), b.astype(BF16), preferred_element_type=F32)


def _dot_nt(a, b):
    return lax.dot_general(a.astype(BF16), b.astype(BF16), (((1,), (1,)), ((), ())), preferred_element_type=F32)


def _dot_tn(a, b):
    return lax.dot_general(a.astype(BF16), b.astype(BF16), (((0,), (0,)), ((), ())), preferred_element_type=F32)


def _dot_f32(a, b):
    return jnp.dot(a, b, preferred_element_type=F32, precision=HIGHEST)


def _perm_segments():
    segs = [(3592, 5640)]
    for h in range(4):
        segs += [(128 * h + 512 * t, 128 * h + 512 * t + 128) for t in range(4)]
    segs += [(2048, 2560)]
    for p in range(4):
        segs += [(2560 + 128 * p, 2688 + 128 * p), (3072 + 128 * p, 3200 + 128 * p)]
    segs += [(3584, 3592)]
    return segs


def _permute_cols(w):
    parts = [w[:, a:b] for a, b in _perm_segments()]
    parts.append(jnp.zeros((w.shape[0], NP - NIN), w.dtype))
    return jnp.concatenate(parts, axis=1)


def _unpermute_to_chips(g):
    pos, where = 0, []
    for a, b in _perm_segments():
        where.append((a, b, pos))
        pos += b - a
    ncin = NIN // N_CHIPS
    out = []
    for k in range(N_CHIPS):
        lo, hi = k * ncin, (k + 1) * ncin
        parts = [g[:, p + max(a, lo) - a:p + min(b, hi) - a] for a, b, p in sorted(where) if max(a, lo) < min(b, hi)]
        out.append(jnp.concatenate(parts, axis=1))
    return jnp.stack(out)


def _allgather8(v, name):
    rows, cols = v.shape

    def body(x_ref, out_ref, send_sems, recv_sems, local_sem):
        x, y, c = lax.axis_index("x"), lax.axis_index("y"), lax.axis_index("c")
        me, sibling = (x, y, c), (x, y, 1 - c)
        chips = [(1 - x, y), (x, 1 - y), (1 - x, 1 - y)]

        def slot(px, py, pc):
            return out_ref.at[4 * px + 2 * py + pc]

        def copy(k, block, to, src=None):
            return pltpu.make_async_remote_copy(
                src_ref=slot(*block) if src is None else src, dst_ref=slot(*block),
                send_sem=send_sems.at[k], recv_sem=recv_sems.at[k], device_id=to, device_id_type=MESH)

        mine = pltpu.make_async_copy(x_ref, slot(*me), local_sem)
        mine.start()
        first = [copy(0, me, sibling, src=x_ref)]
        first += [copy(1 + j, me, (*chip, c), src=x_ref) for j, chip in enumerate(chips)]
        for cp in first:
            cp.start()
        passed = [copy(4 + j, (*chip, c), sibling) for j, chip in enumerate(chips)]
        for j, chip in enumerate(chips):
            copy(1 + j, (*chip, c), me).wait_recv()
            passed[j].start()
        copy(0, sibling, me).wait_recv()
        for j, chip in enumerate(chips):
            copy(4 + j, (*chip, 1 - c), me).wait_recv()
        for cp in first + passed:
            cp.wait_send()
        mine.wait()

    return pl.pallas_call(
        body, name=name,
        out_shape=jax.ShapeDtypeStruct((N_DEV, rows, cols), v.dtype),
        in_specs=[pl.BlockSpec(memory_space=pltpu.VMEM)],
        out_specs=pl.BlockSpec(memory_space=pltpu.VMEM),
        scratch_shapes=[pltpu.SemaphoreType.DMA((7,)), pltpu.SemaphoreType.DMA((7,)), pltpu.SemaphoreType.DMA],
    )(v)


def _hbm_specs(n):
    return [pl.BlockSpec(memory_space=pl.ANY)] * n


def _swap_halves(grads, name):
    n = len(grads)

    def body(*refs):
        ins, outs, (send_sems, recv_sems) = refs[:n], refs[n:2 * n], refs[2 * n:]
        x, y, c = lax.axis_index("x"), lax.axis_index("y"), lax.axis_index("c")
        cps = []
        for w in range(n):
            hr = ins[w].shape[1] // 2
            cps.append(pltpu.make_async_remote_copy(
                src_ref=ins[w].at[:, pl.ds((1 - c) * hr, hr), :], dst_ref=outs[w],
                send_sem=send_sems.at[w], recv_sem=recv_sems.at[w], device_id=(x, y, 1 - c), device_id_type=MESH))
        for cp in cps:
            cp.start()
        for cp in cps:
            cp.wait()

    return pl.pallas_call(
        body, name=name,
        out_shape=[jax.ShapeDtypeStruct((N_CHIPS, g.shape[1] // 2, g.shape[2]), g.dtype) for g in grads],
        in_specs=_hbm_specs(n), out_specs=_hbm_specs(n),
        scratch_shapes=[pltpu.SemaphoreType.DMA((n,)), pltpu.SemaphoreType.DMA((n,))],
    )(*grads)


def _join_halves(halves, token, name):
    n = len(halves)

    def body(*refs):
        ins, outs, (send_sems, recv_sems) = refs[:n], refs[n + 1:2 * n + 1], refs[2 * n + 1:]
        x, y, c = lax.axis_index("x"), lax.axis_index("y"), lax.axis_index("c")
        cps = [pltpu.make_async_remote_copy(
            src_ref=ins[w], dst_ref=outs[w], send_sem=send_sems.at[w], recv_sem=recv_sems.at[w],
            device_id=(x, y, 1 - c), device_id_type=MESH) for w in range(n)]
        for cp in cps:
            cp.start()
        for cp in cps:
            cp.wait()

    return pl.pallas_call(
        body, name=name,
        out_shape=[jax.ShapeDtypeStruct(h.shape, h.dtype) for h in halves],
        in_specs=_hbm_specs(n + 1), out_specs=_hbm_specs(n),
        scratch_shapes=[pltpu.SemaphoreType.DMA((n,)), pltpu.SemaphoreType.DMA((n,))],
    )(*halves, token)


def _in_hbm(v):
    return pltpu.with_memory_space_constraint(v, pltpu.HBM)


_SPLIT_COPY = pltpu.CompilerParams(has_side_effects=pltpu.SideEffectType.DATAFLOW_SIDE_EFFECTING)


def _gather_copies(srcs, lands, send_sems, recv_sems):
    x, y, c = lax.axis_index("x"), lax.axis_index("y"), lax.axis_index("c")
    cps = []
    for w, (src, land) in enumerate(zip(srcs, lands)):
        hr = src.shape[0] // 2
        for j, chip in enumerate([(1 - x, y), (x, 1 - y), (1 - x, 1 - y)]):
            cps.append(pltpu.make_async_remote_copy(
                src_ref=src.at[pl.ds(c * hr, hr), :], dst_ref=land.at[2 * x + y, pl.ds(c * hr, hr), :],
                send_sem=send_sems.at[3 * w + j], recv_sem=recv_sems.at[3 * w + j],
                device_id=(*chip, c), device_id_type=MESH))
    return cps


def _scatter_copies(srcs, lands, send_sems, recv_sems):
    x, y, c = lax.axis_index("x"), lax.axis_index("y"), lax.axis_index("c")
    cps = []
    for w, (src, land) in enumerate(zip(srcs, lands)):
        for j, chip in enumerate([(1 - x, y), (x, 1 - y), (1 - x, 1 - y)]):
            cps.append(pltpu.make_async_remote_copy(
                src_ref=src.at[2 * chip[0] + chip[1]], dst_ref=land.at[j],
                send_sem=send_sems.at[3 * w + j], recv_sem=recv_sems.at[3 * w + j],
                device_id=(*chip, c), device_id_type=MESH))
    return cps


def _split_start(copies, srcs, lands, name):
    n = len(srcs)

    def body(*refs):
        src, lnd, send_sems, recv_sems, token = refs[:n], refs[n:2 * n], refs[2 * n], refs[2 * n + 1], refs[-1]
        for cp in copies(src, lnd, send_sems, recv_sems):
            cp.start()
        token[...] = jnp.zeros_like(token)

    hbm = pl.BlockSpec(memory_space=pltpu.HBM)
    sem = pl.BlockSpec(memory_space=pltpu.SEMAPHORE)
    outs = pl.pallas_call(
        body, name=name,
        out_shape=(pltpu.SemaphoreType.DMA((3 * n,)), pltpu.SemaphoreType.DMA((3 * n,)),
                   *[pltpu.HBM(v.shape, v.dtype) for v in srcs + lands], jax.ShapeDtypeStruct((8, 128), F32)),
        in_specs=[hbm] * (2 * n),
        out_specs=(sem, sem, *([hbm] * (2 * n)), pl.BlockSpec(memory_space=pltpu.VMEM)),
        input_output_aliases={i: 2 + i for i in range(2 * n)},
        compiler_params=_SPLIT_COPY,
    )(*[_in_hbm(v) for v in srcs + lands])
    return outs[0], outs[1], list(outs[2:2 + n]), list(outs[2 + n:2 + 2 * n]), outs[-1]


def _split_wait(copies, send_sems, recv_sems, srcs, lands, after, name):
    n = len(srcs)

    def body(*refs):
        src, lnd, send_sems, recv_sems = refs[:n], refs[n:2 * n], refs[2 * n], refs[2 * n + 1]
        for cp in copies(src, lnd, send_sems, recv_sems):
            cp.wait_send()
            cp.wait_recv()

    hbm = pl.BlockSpec(memory_space=pltpu.HBM)
    sem = pl.BlockSpec(memory_space=pltpu.SEMAPHORE)
    outs = pl.pallas_call(
        body, name=name,
        out_shape=tuple(pltpu.HBM(v.shape, v.dtype) for v in srcs + lands),
        in_specs=[hbm] * (2 * n) + [sem, sem, pl.BlockSpec(memory_space=pl.ANY)],
        out_specs=tuple([hbm] * (2 * n)),
        input_output_aliases={i: i for i in range(2 * n)},
        compiler_params=_SPLIT_COPY,
    )(*srcs, *lands, send_sems, recv_sems, after)
    return list(outs[n:])


def _pass_to_sibling(lands, name):
    n = len(lands)

    def body(*refs):
        ins, outs, (send_sems, recv_sems) = refs[:n], refs[n:2 * n], refs[2 * n:]
        x, y, c = lax.axis_index("x"), lax.axis_index("y"), lax.axis_index("c")
        cps = []
        for w in range(n):
            hr = ins[w].shape[1] // 2
            for j, chip in enumerate([(1 - x, y), (x, 1 - y), (1 - x, 1 - y)]):
                k = 2 * chip[0] + chip[1]
                cps.append(pltpu.make_async_remote_copy(
                    src_ref=ins[w].at[k, pl.ds(c * hr, hr), :], dst_ref=outs[w].at[k, pl.ds(c * hr, hr), :],
                    send_sem=send_sems.at[3 * w + j], recv_sem=recv_sems.at[3 * w + j],
                    device_id=(x, y, 1 - c), device_id_type=MESH))
        for cp in cps:
            cp.start()
        for cp in cps:
            cp.wait()

    return pl.pallas_call(
        body, name=name,
        out_shape=[jax.ShapeDtypeStruct(v.shape, v.dtype) for v in lands],
        in_specs=_hbm_specs(n), out_specs=_hbm_specs(n),
        input_output_aliases={i: i for i in range(n)},
        scratch_shapes=[pltpu.SemaphoreType.DMA((3 * n,)), pltpu.SemaphoreType.DMA((3 * n,))],
    )(*lands)


def _row_tile(rows):
    for cand in (256, 176, 128, 64, 32, 16):
        if rows % cand == 0:
            return cand
    raise ValueError(rows)


def _add_my_half(g, other, c_idx, name):
    _, k, n = g.shape
    hr = k // 2
    tr = _row_tile(hr)
    nb = hr // tr

    def body(c_ref, g_ref, o_ref, out_ref, out16_ref):
        s = g_ref[...] + o_ref[...]
        out_ref[...] = s
        out16_ref[...] = s.astype(BF16)

    return pl.pallas_call(
        body, name=name,
        grid_spec=pltpu.PrefetchScalarGridSpec(
            num_scalar_prefetch=1, grid=(N_CHIPS, nb),
            in_specs=[pl.BlockSpec((1, tr, n), lambda j, i, c: (j, c[0] * nb + i, 0)),
                      pl.BlockSpec((1, tr, n), lambda j, i, c: (j, i, 0))],
            out_specs=[pl.BlockSpec((1, tr, n), lambda j, i, c: (j, i, 0)),
                       pl.BlockSpec((1, tr, n), lambda j, i, c: (j, i, 0))]),
        out_shape=[jax.ShapeDtypeStruct((N_CHIPS, hr, n), F32), jax.ShapeDtypeStruct((N_CHIPS, hr, n), BF16)],
        compiler_params=_params(("parallel", "parallel")),
    )(c_idx, g, other)


def _add_chips(red, recv, chip_idx, name):
    _, hr, n = red.shape
    tr = _row_tile(hr)

    def body(k_ref, r_ref, v_ref, out_ref):
        out_ref[...] = ((r_ref[0] + v_ref[0].astype(F32)) + v_ref[1].astype(F32)) + v_ref[2].astype(F32)

    return pl.pallas_call(
        body, name=name,
        grid_spec=pltpu.PrefetchScalarGridSpec(
            num_scalar_prefetch=1, grid=(hr // tr,),
            in_specs=[pl.BlockSpec((1, tr, n), lambda i, k: (k[0], i, 0)),
                      pl.BlockSpec((3, tr, n), lambda i, k: (0, i, 0))],
            out_specs=pl.BlockSpec((tr, n), lambda i, k: (i, 0))),
        out_shape=jax.ShapeDtypeStruct((hr, n), F32),
        compiler_params=_params(("parallel",)),
    )(chip_idx, red, recv)


def _mod_shard(c_all, w_ada, b_ada):
    nb, cols = c_all.shape[0], w_ada.shape[1]

    def body(c_ref, w_ref, b_ref, o_ref):
        c = c_ref[...]
        o_ref[...] = _dot(c * jax.nn.sigmoid(c), w_ref[...]) + b_ref[...]

    return pl.pallas_call(
        body, name="mod_shard", out_shape=jax.ShapeDtypeStruct((nb, cols), F32),
        compiler_params=_params(vmem_mb=48),
    )(c_all, w_ada, b_ada)


def _proj(x2, mod8, w, seq, out_dtype, name):
    t = x2.shape[0]
    n = w.shape[1]
    tm, tn = min(2048, seq), min(1152, n)
    tpb = seq // tm

    def body(x_ref, mod_ref, w_ref, o_ref, h_ref):
        @pl.when(pl.program_id(1) == 0)
        def _():
            h_ref[...] = (x_ref[...] * (1.0 + mod_ref[0, 1:2, :]) + mod_ref[0, 0:1, :]).astype(BF16)
        o_ref[...] = jnp.dot(h_ref[...], w_ref[...], preferred_element_type=F32).astype(o_ref.dtype)

    return pl.pallas_call(
        body, name=name, grid=(t // tm, n // tn),
        in_specs=[pl.BlockSpec((tm, D), lambda i, j: (i, 0)),
                  pl.BlockSpec((1, 8, D), lambda i, j: (i // tpb, 0, 0)),
                  pl.BlockSpec((D, tn), lambda i, j: (0, j))],
        out_specs=[pl.BlockSpec((tm, tn), lambda i, j: (i, j)), pl.BlockSpec((tm, D), lambda i, j: (i, 0))],
        out_shape=[jax.ShapeDtypeStruct((t, n), out_dtype), jax.ShapeDtypeStruct((t, D), BF16)],
        compiler_params=_params(("parallel", "arbitrary"), 56),
    )(x2, mod8, w)


def _rows_matmul(a, w, name):
    t, k = a.shape
    n = w.shape[1]
    tm = 1024 if t % 1024 == 0 else t

    def body(a_ref, w_ref, o_ref):
        o_ref[...] = jnp.dot(a_ref[...], w_ref[...], preferred_element_type=F32)

    return pl.pallas_call(
        body, name=name, grid=(t // tm,),
        in_specs=[pl.BlockSpec((tm, k), lambda i: (i, 0)), pl.BlockSpec((k, n), lambda i: (0, 0))],
        out_specs=pl.BlockSpec((tm, n), lambda i: (i, 0)),
        out_shape=jax.ShapeDtypeStruct((t, n), F32),
        compiler_params=_params(("parallel",)),
    )(a, w)


def _tn_matmul(a, b, name, seq, split=None):
    a_st, b_st = a.ndim == 3, b.ndim == 3
    t, ka = a.shape[-2:]
    n = b.shape[-1]
    tt = min(2048, seq)
    nt = t // tt
    if a_st or b_st:
        steps, tn = (a.shape[0] if a_st else b.shape[0]), n
    else:
        tn = split
        if tn is None:
            tn = next(cand for cand in (1920, 1024, 1408, 512, n) if n % cand == 0)
        steps = n // tn
    stacked_out = a_st or b_st or split is not None

    def body(a_ref, b_ref, o_ref):
        part = _dot_tn(a_ref[0] if a_st else a_ref[...], b_ref[0] if b_st else b_ref[...])
        if stacked_out:
            part = part[None]

        @pl.when(pl.program_id(1) == 0)
        def _():
            o_ref[...] = part

        @pl.when(pl.program_id(1) > 0)
        def _():
            o_ref[...] += part

    if a_st:
        in_specs = [pl.BlockSpec((1, tt, ka), lambda j, k: (j, k, 0))]
    else:
        in_specs = [pl.BlockSpec((tt, ka), lambda j, k: (k, 0))]
    if b_st:
        in_specs.append(pl.BlockSpec((1, tt, n), lambda j, k: (j, k, 0)))
    else:
        in_specs.append(pl.BlockSpec((tt, tn), lambda j, k: (k, 0 if a_st else j)))
    if stacked_out:
        out_spec = pl.BlockSpec((1, ka, tn), lambda j, k: (j, 0, 0))
        out_shape = jax.ShapeDtypeStruct((steps, ka, tn), F32)
    else:
        out_spec = pl.BlockSpec((ka, tn), lambda j, k: (0, j))
        out_shape = jax.ShapeDtypeStruct((ka, n), F32)
    return pl.pallas_call(
        body, name=name, grid=(steps, nt), in_specs=in_specs, out_specs=out_spec, out_shape=out_shape,
        compiler_params=_params(("parallel", "arbitrary"), 56),
    )(a, b)


def _dh_kernel(dproj, w_p, x2, dxp, mod8, seq):
    t = x2.shape[0]
    tm, tk = min(1024, seq), 1920
    tpb = seq // tm
    nk = NP // tk
    nbatch = t // seq

    def body(dp_ref, w_ref, x_ref, dxp_ref, mod_ref, gx_ref, dm_ref, acc):
        i, k = pl.program_id(0), pl.program_id(1)

        @pl.when(k == 0)
        def _():
            acc[...] = jnp.zeros_like(acc)

        acc[...] += _dot_nt(dp_ref[...], w_ref[...])

        @pl.when(k == nk - 1)
        def _():
            dh = acc[...]
            gx_ref[...] = dxp_ref[...].astype(F32) + dh * (1.0 + mod_ref[0, 1:2, :])
            upd = jnp.concatenate(
                [jnp.sum(dh, axis=0, keepdims=True), jnp.sum(dh * x_ref[...], axis=0, keepdims=True),
                 jnp.zeros((6, D), F32)], axis=0)

            @pl.when(i % tpb == 0)
            def _():
                dm_ref[0] = upd

            @pl.when(i % tpb != 0)
            def _():
                dm_ref[0] += upd

    return pl.pallas_call(
        body, name="dh", grid=(t // tm, nk),
        in_specs=[pl.BlockSpec((tm, tk), lambda i, k: (i, k)),
                  pl.BlockSpec((D, tk), lambda i, k: (0, k)),
                  pl.BlockSpec((tm, D), lambda i, k: (i, 0)),
                  pl.BlockSpec((tm, D), lambda i, k: (i, 0)),
                  pl.BlockSpec((1, 8, D), lambda i, k: (i // tpb, 0, 0))],
        out_specs=[pl.BlockSpec((tm, D), lambda i, k: (i, 0)),
                   pl.BlockSpec((1, 8, D), lambda i, k: (i // tpb, 0, 0))],
        out_shape=[jax.ShapeDtypeStruct((t, D), F32), jax.ShapeDtypeStruct((nbatch, 8, D), F32)],
        scratch_shapes=[pltpu.VMEM((tm, D), F32)],
        compiler_params=_params(("arbitrary", "arbitrary"), 56),
    )(dproj, w_p, x2, dxp, mod8)


def _tri(n, upper):
    r = lax.broadcasted_iota(jnp.int32, (n, n), 0)
    c = lax.broadcasted_iota(jnp.int32, (n, n), 1)
    return jnp.where((c >= r) if upper else (c <= r), 1.0, 0.0).astype(F32)


@jax.custom_vjp
def _mm_nn(a, b):
    return _dot(a, b)


_mm_nn.defvjp(lambda a, b: (_dot(a, b), (a, b)),
              lambda res, g: (_dot_nt(g, res[1]), _dot_tn(res[0], g)))


@jax.custom_vjp
def _mm_nt(a, b):
    return _dot_nt(a, b)


_mm_nt.defvjp(lambda a, b: (_dot_nt(a, b), (a, b)),
              lambda res, g: (_dot(g, res[1]), _dot_tn(g, res[0])))


@jax.custom_vjp
def _mm_tn(a, b):
    return _dot_tn(a, b)


_mm_tn.defvjp(lambda a, b: (_dot_tn(a, b), (a, b)),
              lambda res, g: (_dot_nt(res[1], g), _dot(res[0], g)))


@jax.custom_vjp
def _cumsum_rows(x):
    return _dot_f32(_tri(x.shape[0], False), x)


_cumsum_rows.defvjp(lambda x: (_cumsum_rows(x), None),
                    lambda _, g: (_dot_f32(_tri(g.shape[0], True), g),))


@functools.partial(jax.custom_vjp, nondiff_argnums=(1,))
def _shift_rows(x, k):
    return pltpu.roll(x, k % x.shape[0], 0)


_shift_rows.defvjp(lambda x, k: (_shift_rows(x, k), None),
                   lambda k, _, g: (pltpu.roll(g, (-k) % g.shape[0], 0),))


def _group_ref(bc, m):
    n = bc.shape[0] // (2 * m)
    b3 = bc.reshape(n, 2 * m, ADH)
    row = lax.broadcasted_iota(jnp.int32, b3.shape, 1)
    ref = jnp.sum(jnp.where(row == m - 1, b3, 0.0), axis=1, keepdims=True)
    return jnp.broadcast_to(ref, b3.shape).reshape(bc.shape)


def _hgrn_block(q, fl, v, g, st, lb, nw):
    n = q.shape[0]
    f = lb + (1.0 - lb) * jax.nn.sigmoid(fl)
    kk = 1.0 - f
    lf = jnp.log(f)
    bc = _cumsum_rows(lf)
    row = lax.broadcasted_iota(jnp.int32, (n, ADH), 0)
    same = jnp.bitwise_xor(lax.broadcasted_iota(jnp.int32, (n, n), 0), lax.broadcasted_iota(jnp.int32, (n, n), 1))
    a = jnp.zeros((n, n), F32)
    m = 1
    while m < n:
        r = jnp.bitwise_and(row, 2 * m - 1)
        up, lo = r >= m, r < m
        if m == 1:
            aq, ak = lf, jnp.zeros_like(lf)
        elif m == 2:
            aq = jnp.where(r == 3, lf + _shift_rows(lf, 1), lf)
            ak = jnp.where(r == 0, _shift_rows(lf, -1), 0.0)
        else:
            ref = _group_ref(bc, m)
            aq, ak = bc - ref, ref - bc
        qt = jnp.where(up, q * jnp.exp(jnp.where(up, aq, 0.0)), 0.0)
        kt = jnp.where(lo, kk * jnp.exp(jnp.where(lo, ak, 0.0)), 0.0)
        a = a + jnp.where(same < 2 * m, _mm_nt(qt, kt), 0.0)
        m *= 2
    last = row == n - 1
    bl = jnp.sum(jnp.where(last, bc, 0.0), axis=0, keepdims=True)
    o = _mm_nn(a, v) + _mm_nt(q * jnp.exp(bc), st) + jnp.sum(q * kk, axis=-1, keepdims=True) * v
    st_new = st * jnp.exp(bl) + _mm_tn(v, kk * jnp.exp(bl - bc))
    rms = lax.rsqrt(jnp.mean(o * o, axis=-1, keepdims=True) + RMS_EPS)
    return o * rms * nw * jax.nn.sigmoid(g), st_new


def _hgrn_fwd(proj, lb_logits, norm_w, nbatch, seq):
    t = proj.shape[0]
    blk = min(HGRN_BLOCK, seq)
    nb = seq // blk

    nh = HGRN_HEADS
    wp, wy = 512 * nh, ADH * nh

    def body(p_ref, lbl_ref, nw_ref, y_ref, ck_ref, st_s):
        @pl.when(pl.program_id(2) == 0)
        def _():
            st_s[...] = jnp.zeros_like(st_s)

        st = [st_s[h] for h in range(nh)]
        p = p_ref[...].astype(F32)
        lb = jax.nn.sigmoid(lbl_ref[0:1, :] - lbl_ref[1:2, :])
        nw = nw_ref[...]
        res = [_hgrn_block(*(p[:, 512 * h + 128 * k:512 * h + 128 * k + 128] for k in range(4)), st[h],
                           lb[:, 128 * h:128 * h + 128], nw[:, 128 * h:128 * h + 128]) for h in range(nh)]
        for h in range(nh):
            ck_ref[0, h] = st[h]
            st_s[h] = res[h][1]
        y_ref[...] = jnp.concatenate([r[0] for r in res], axis=1).astype(y_ref.dtype)

    return pl.pallas_call(
        body, name="hgrn_fwd", grid=(AH // nh, nbatch, nb),
        in_specs=[pl.BlockSpec((blk, wp), lambda h, b, i: (b * nb + i, COL_A // wp + h)),
                  pl.BlockSpec((2, wy), lambda h, b, i: (0, h)),
                  pl.BlockSpec((1, wy), lambda h, b, i: (0, h))],
        out_specs=[pl.BlockSpec((blk, wy), lambda h, b, i: (b * nb + i, h)),
                   pl.BlockSpec((1, nh, 128, 128), lambda h, b, i: ((h * nbatch + b) * nb + i, 0, 0, 0))],
        out_shape=[jax.ShapeDtypeStruct((t, AW), BF16),
                   jax.ShapeDtypeStruct((AH // nh * nbatch * nb, nh, 128, 128), F32)],
        scratch_shapes=[pltpu.VMEM((nh, 128, 128), F32)],
        compiler_params=_params(("parallel", "parallel", "arbitrary"), 48),
    )(proj, lb_logits, norm_w)


def _hgrn_bwd(proj, dya, ckpt, lb_logits, norm_w, dproj, nbatch, seq):
    t = proj.shape[0]
    blk = min(HGRN_BLOCK, seq)
    nb = seq // blk

    nh = HGRN_HEADS
    wp, wy = 512 * nh, ADH * nh

    def body(p_ref, dy_ref, ck_ref, lbl_ref, nw_ref, dp_in, dp_ref, sm_ref, dst_s):
        del dp_in
        b_id, i = pl.program_id(1), pl.program_id(2)

        @pl.when(i == 0)
        def _():
            dst_s[...] = jnp.zeros_like(dst_s)

        dst = [dst_s[h] for h in range(nh)]
        st = [ck_ref[0, h] for h in range(nh)]
        p = p_ref[...].astype(F32)
        dy = dy_ref[...]
        lb = jax.nn.sigmoid(lbl_ref[0:1, :] - lbl_ref[1:2, :])
        nw = nw_ref[...]
        grads = []
        for h in range(nh):
            _, pullback = jax.vjp(_hgrn_block, *(p[:, 512 * h + 128 * k:512 * h + 128 * k + 128] for k in range(4)),
                                  st[h], lb[:, 128 * h:128 * h + 128], nw[:, 128 * h:128 * h + 128])
            grads.append(pullback((dy[:, 128 * h:128 * h + 128], dst[h])))
        for h in range(nh):
            dst_s[h] = grads[h][4]
        dp_ref[...] = jnp.concatenate([g[k] for g in grads for k in range(4)], axis=1).astype(dp_ref.dtype)
        upd = jnp.concatenate([jnp.concatenate([g[5] for g in grads], axis=1),
                               jnp.concatenate([g[6] for g in grads], axis=1), jnp.zeros((6, wy), F32)], axis=0)
        first = (b_id == 0) & (i == 0)

        @pl.when(first)
        def _():
            sm_ref[...] = upd

        @pl.when(jnp.logical_not(first))
        def _():
            sm_ref[...] += upd

    def rows(h, b, i):
        return b * nb + (nb - 1 - i)

    return pl.pallas_call(
        body, name="hgrn_bwd", grid=(AH // nh, nbatch, nb),
        in_specs=[pl.BlockSpec((blk, wp), lambda h, b, i: (rows(h, b, i), COL_A // wp + h)),
                  pl.BlockSpec((blk, wy), lambda h, b, i: (rows(h, b, i), h)),
                  pl.BlockSpec((1, nh, 128, 128), lambda h, b, i: ((h * nbatch + b) * nb + (nb - 1 - i), 0, 0, 0)),
                  pl.BlockSpec((2, wy), lambda h, b, i: (0, h)),
                  pl.BlockSpec((1, wy), lambda h, b, i: (0, h)),
                  pl.BlockSpec(memory_space=pl.ANY)],
        out_specs=[pl.BlockSpec((blk, wp), lambda h, b, i: (rows(h, b, i), COL_A // wp + h)),
                   pl.BlockSpec((8, wy), lambda h, b, i: (0, h))],
        out_shape=[jax.ShapeDtypeStruct((t, NP), BF16), jax.ShapeDtypeStruct((8, AW), F32)],
        input_output_aliases={5: 0},
        scratch_shapes=[pltpu.VMEM((nh, 128, 128), F32)],
        compiler_params=_params(("parallel", "arbitrary", "arbitrary"), 56),
    )(proj, dya, ckpt, lb_logits, norm_w, dproj)


def _log_sigmoid(z):
    return jnp.minimum(z, 0.0) - jnp.log(1.0 + jnp.exp(-jnp.abs(z)))


def _fox_cum(proj, bias128, nbatch, seq):
    t = proj.shape[0]
    ts = min(512, seq)
    nb = seq // ts

    def body(p_ref, b_ref, c_ref, carry):
        @pl.when(pl.program_id(1) == 0)
        def _():
            carry[...] = jnp.zeros_like(carry)
        cum = _dot_f32(_tri(ts, False), _log_sigmoid(p_ref[...] + b_ref[...])) + carry[...]
        carry[...] = cum[ts - 1:ts, :]
        cum2 = cum * LOG2E
        lane = lax.broadcasted_iota(jnp.int32, (ts, 128), 1)
        for p in range(4):
            c_ref[p] = jnp.where(lane < 64, cum2[:, 2 * p:2 * p + 1], cum2[:, 2 * p + 1:2 * p + 2])

    return pl.pallas_call(
        body, name="fox_cum", grid=(nbatch, nb),
        in_specs=[pl.BlockSpec((ts, 128), lambda b, i: (b * nb + i, 0)),
                  pl.BlockSpec((1, 128), lambda b, i: (0, 0))],
        out_specs=pl.BlockSpec((4, ts, 128), lambda b, i: (0, b * nb + i, 0)),
        out_shape=jax.ShapeDtypeStruct((4, t, 128), F32),
        scratch_shapes=[pltpu.VMEM((1, 128), F32)],
        compiler_params=_params(("parallel", "arbitrary")),
    )(proj, bias128)


def _fox_scores_t(q128, k128, cc128, hh, masked):
    tq, tk = q128.shape[0], k128.shape[0]
    qh = _head_lanes((q128 * (LOG2E * BDH ** -0.5)).astype(BF16), hh)
    s = _dot_nt(k128, qh) - cc128[:, 64 * hh:64 * hh + 1]
    if masked:
        key = lax.broadcasted_iota(jnp.int32, (tk, tq), 0)
        qry = lax.broadcasted_iota(jnp.int32, (tk, tq), 1)
        s = jnp.where(key <= qry, s, NEG)
    return s


def _causal_pairs(nq, key_major):
    if key_major:
        pairs = [(i, j) for j in range(nq) for i in range(j, nq)]
    else:
        pairs = [(i, j) for i in range(nq) for j in range(i + 1)]
    return (jnp.asarray([p[0] for p in pairs], jnp.int32), jnp.asarray([p[1] for p in pairs], jnp.int32))


def _head_lanes(x128, hh):
    lane = lax.broadcasted_iota(jnp.int32, x128.shape, 1)
    return jnp.where((lane < 64) if hh == 0 else (lane >= 64), x128, jnp.zeros_like(x128))


def _with_ones_lane(x128, hh):
    lane = lax.broadcasted_iota(jnp.int32, x128.shape, 1)
    one = jnp.ones_like(x128)
    zero = jnp.zeros_like(x128)
    if hh == 0:
        return jnp.where(lane < 64, x128, jnp.where(lane == 64, one, zero))
    return jnp.where(lane >= 64, x128, jnp.where(lane == 0, one, zero))


def _fox_fwd(proj, cum_cols, nbatch, seq):
    t = proj.shape[0]
    tq = tk = min(512, seq)
    nq = seq // tq
    npr = FOX_PAIRS
    qi, kj = _causal_pairs(nq, key_major=False)

    def body(qi_ref, kj_ref, q_ref, kv_ref, cc_ref, o_ref, lse_ref, m_s, acc_s):
        s_id = pl.program_id(2)
        i, j = qi_ref[s_id], kj_ref[s_id]

        @pl.when(j == 0)
        def _():
            m_s[...] = jnp.full_like(m_s, NEG)
            acc_s[...] = jnp.zeros_like(acc_s)

        def step(masked):
            heads = [(pr, hh) for pr in range(npr) for hh in range(2)]
            m_prev = m_s[0:2 * npr, :]
            acc_prev = [acc_s[h] for h in range(2 * npr)]
            q128 = [q_ref[:, 128 * pr:128 * pr + 128] for pr in range(npr)]
            k128 = [kv_ref[:, 256 * pr:256 * pr + 128].astype(BF16) for pr in range(npr)]
            v128 = [kv_ref[:, 256 * pr + 128:256 * pr + 256].astype(BF16) for pr in range(npr)]
            s = [_fox_scores_t(q128[pr], k128[pr], cc_ref[pr], hh, masked) for pr, hh in heads]
            m_new = [jnp.maximum(m_prev[h:h + 1, :], jnp.max(s[h], axis=0, keepdims=True)) for h in range(2 * npr)]
            acc_new = []
            for h, (pr, hh) in enumerate(heads):
                alpha = jnp.exp2(m_prev[h:h + 1, :] - m_new[h])
                p = jnp.exp2(s[h] - m_new[h]).astype(BF16)
                acc_new.append(acc_prev[h] * alpha + _dot_tn(_with_ones_lane(v128[pr], hh), p))
            for h in range(2 * npr):
                acc_s[h] = acc_new[h]
            m_s[0:2 * npr, :] = jnp.concatenate(m_new, axis=0)

        @pl.when(j < i)
        def _():
            step(False)

        @pl.when(j == i)
        def _():
            step(True)
            outs = []
            for pr in range(npr):
                a0, a1 = acc_s[2 * pr], acc_s[2 * pr + 1]
                l0, l1 = a0[64:65, :], a1[0:1, :]
                outs.append(jnp.concatenate([a0[0:64, :] / l0, a1[64:128, :] / l1], axis=0).T)
                lse_ref[0, pr] = jnp.concatenate(
                    [m_s[2 * pr:2 * pr + 1, :] + jnp.log2(l0), m_s[2 * pr + 1:2 * pr + 2, :] + jnp.log2(l1),
                     jnp.zeros((6, tq), F32)], axis=0)
            o_ref[...] = jnp.concatenate(outs, axis=1).astype(o_ref.dtype)

    return pl.pallas_call(
        body, name="fox_fwd",
        grid_spec=pltpu.PrefetchScalarGridSpec(
            num_scalar_prefetch=2, grid=(nbatch, 4 // npr, qi.shape[0]),
            in_specs=[pl.BlockSpec((tq, 128 * npr), lambda b, p, s, qi, kj: (b * nq + qi[s], COL_BQ // (128 * npr) + p)),
                      pl.BlockSpec((tk, 256 * npr), lambda b, p, s, qi, kj: (b * nq + kj[s], COL_KV // (256 * npr) + p)),
                      pl.BlockSpec((npr, tk, 128), lambda b, p, s, qi, kj: (p, b * nq + kj[s], 0))],
            out_specs=[pl.BlockSpec((tq, 128 * npr), lambda b, p, s, qi, kj: (b * nq + qi[s], p)),
                       pl.BlockSpec((1, npr, 8, tq), lambda b, p, s, qi, kj: (b, p, 0, qi[s]))],
            scratch_shapes=[pltpu.VMEM((8, tq), F32), pltpu.VMEM((2 * npr, 128, tq), F32)]),
        out_shape=[jax.ShapeDtypeStruct((t, 512), BF16), jax.ShapeDtypeStruct((nbatch, 4, 8, seq), F32)],
        compiler_params=_params(("parallel", "parallel", "arbitrary"), 56),
    )(qi, kj, proj, proj, cum_cols)


def _fox_bwd(proj, cum_cols, lse, yb, dyb, dproj, nbatch, seq):
    t = proj.shape[0]
    tq = tk = min(512, seq)
    nq = seq // tq
    scale = BDH ** -0.5
    qi, kj = _causal_pairs(nq, key_major=True)
    nsteps = qi.shape[0]

    npr = FOX_PAIRS

    def body(qi_ref, kj_ref, q_ref, kv_ref, cc_ref, lse_ref, o_ref, do_ref, dp_in,
             dkv_ref, dq_ref, drs_ref, dcs_ref, dk_s, dv_s, dqa_s):
        del dp_in
        pg, s_id = pl.program_id(1), pl.program_id(2)
        i, j = qi_ref[s_id], kj_ref[s_id]

        @pl.when(i == j)
        def _():
            dk_s[...] = jnp.zeros_like(dk_s)
            dv_s[...] = jnp.zeros_like(dv_s)

        @pl.when(s_id == 0)
        def _():
            dqa_s[...] = jnp.zeros_like(dqa_s)

        def step(masked):
            dk_prev = [dk_s[h] for h in range(2 * npr)]
            dq_prev = [dqa_s[i, h] for h in range(2 * npr)]
            dv_new = [dv_s[pr] for pr in range(npr)]
            dk_new, dq_new = [], []
            for pr in range(npr):
                lanes = slice(128 * pr, 128 * pr + 128)
                q128 = q_ref[:, lanes]
                qs128 = (q128 * scale).astype(BF16)
                k128 = kv_ref[:, 256 * pr:256 * pr + 128].astype(BF16)
                v128 = kv_ref[:, 256 * pr + 128:256 * pr + 256].astype(BF16)
                do128 = do_ref[:, lanes]
                doo = do128 * o_ref[:, lanes].astype(F32)
                do16 = do128.astype(BF16)
                for hh in range(2):
                    s = _fox_scores_t(q128, k128, cc_ref[pr], hh, masked)
                    p = jnp.exp2(s - lse_ref[0, pr, hh:hh + 1, :])
                    dd = lax.dot_general(jnp.ones((8, 128), F32), _head_lanes(doo, hh), (((1,), (1,)), ((), ())),
                                         preferred_element_type=F32, precision=HIGHEST)[0:1, :]
                    doh = _head_lanes(do16, hh)
                    dp = _dot_nt(v128, doh)
                    ds = (p * (dp - dd)).astype(BF16)
                    dv_new[pr] = dv_new[pr] + _dot(p, doh)
                    dk_new.append(dk_prev[2 * pr + hh] + _dot(ds, _with_ones_lane(qs128, hh)))
                    dq_new.append(dq_prev[2 * pr + hh] + _dot_tn(_with_ones_lane(k128, hh), ds))
            for pr in range(npr):
                dv_s[pr] = dv_new[pr]
            for h in range(2 * npr):
                dk_s[h] = dk_new[h]
                dqa_s[i, h] = dq_new[h]

        @pl.when(i == j)
        def _():
            step(True)

        @pl.when(i > j)
        def _():
            step(False)

        def sums_to_lanes(lane, pr, s0, s1):
            hp = npr * pg + pr
            return jnp.where(lane == 2 * hp, s0, jnp.where(lane == 2 * hp + 1, s1, 0.0))

        @pl.when(i == nq - 1)
        def _():
            lane = lax.broadcasted_iota(jnp.int32, (tk, 128), 1)
            for pr in range(npr):
                k0, k1 = dk_s[2 * pr], dk_s[2 * pr + 1]
                dkv_ref[:, 256 * pr:256 * pr + 128] = jnp.where(lane < 64, k0, k1).astype(dkv_ref.dtype)
                dkv_ref[:, 256 * pr + 128:256 * pr + 256] = dv_s[pr].astype(dkv_ref.dtype)
                dcs_ref[pr] = sums_to_lanes(lane, pr, k0[:, 64:65], k1[:, 0:1])

        @pl.when(s_id == nsteps - 1)
        def _():
            lane = lax.broadcasted_iota(jnp.int32, (tq, 128), 1)
            for blk in range(nq):
                rows = pl.ds(blk * tq, tq)
                for pr in range(npr):
                    a0 = dqa_s[blk, 2 * pr].T
                    a1 = dqa_s[blk, 2 * pr + 1].T
                    dq_ref[rows, 128 * pr:128 * pr + 128] = (jnp.where(lane < 64, a0, a1) * scale).astype(dq_ref.dtype)
                    drs_ref[pr, rows, :] = sums_to_lanes(lane, pr, a0[:, 64:65], a1[:, 0:1])

    return pl.pallas_call(
        body, name="fox_bwd",
        grid_spec=pltpu.PrefetchScalarGridSpec(
            num_scalar_prefetch=2, grid=(nbatch, 4 // npr, nsteps),
            in_specs=[pl.BlockSpec((tq, 128 * npr), lambda b, p, s, qi, kj: (b * nq + qi[s], COL_BQ // (128 * npr) + p)),
                      pl.BlockSpec((tk, 256 * npr), lambda b, p, s, qi, kj: (b * nq + kj[s], COL_KV // (256 * npr) + p)),
                      pl.BlockSpec((npr, tk, 128), lambda b, p, s, qi, kj: (p, b * nq + kj[s], 0)),
                      pl.BlockSpec((1, npr, 8, tq), lambda b, p, s, qi, kj: (b, p, 0, qi[s])),
                      pl.BlockSpec((tq, 128 * npr), lambda b, p, s, qi, kj: (b * nq + qi[s], p)),
                      pl.BlockSpec((tq, 128 * npr), lambda b, p, s, qi, kj: (b * nq + qi[s], p)),
                      pl.BlockSpec(memory_space=pl.ANY)],
            out_specs=[pl.BlockSpec((tk, 256 * npr), lambda b, p, s, qi, kj: (b * nq + kj[s], COL_KV // (256 * npr) + p)),
                       pl.BlockSpec((seq, 128 * npr), lambda b, p, s, qi, kj: (b, p)),
                       pl.BlockSpec((npr, seq, 128), lambda b, p, s, qi, kj: (p, b, 0)),
                       pl.BlockSpec((npr, tk, 128), lambda b, p, s, qi, kj: (p, b * nq + kj[s], 0))],
            scratch_shapes=[pltpu.VMEM((2 * npr, tk, 128), F32), pltpu.VMEM((npr, tk, 128), F32),
                            pltpu.VMEM((nq, 2 * npr, 128, tq), F32)]),
        out_shape=[jax.ShapeDtypeStruct((t, NP), BF16), jax.ShapeDtypeStruct((t, 512), BF16),
                   jax.ShapeDtypeStruct((4, t, 128), F32), jax.ShapeDtypeStruct((4, t, 128), F32)],
        input_output_aliases={8: 0},
        compiler_params=_params(("parallel", "parallel", "arbitrary"), 60),
    )(qi, kj, proj, proj, cum_cols, lse, yb, dyb, dproj)


def _place_cols(dproj, src, col):
    t, w = src.shape
    tm = 1024 if t % 1024 == 0 else t

    def body(s_ref, dp_in, o_ref):
        del dp_in
        o_ref[...] = s_ref[...]

    return pl.pallas_call(
        body, name="place_cols", grid=(t // tm,),
        in_specs=[pl.BlockSpec((tm, w), lambda i: (i, 0)), pl.BlockSpec(memory_space=pl.ANY)],
        out_specs=pl.BlockSpec((tm, w), lambda i: (i, col // w)),
        out_shape=jax.ShapeDtypeStruct(dproj.shape, dproj.dtype),
        input_output_aliases={1: 0},
        compiler_params=_params(("parallel",)),
    )(src, dproj)


def _fox_dbf(proj, bias128, drs, dcs, dproj, nbatch, seq):
    t = proj.shape[0]
    ts = min(512, seq)
    nb = seq // ts

    def body(p_ref, b_ref, dr_ref, dc_ref, dp_in, dp_ref, sm_ref, carry):
        del dp_in
        b_id, i = pl.program_id(0), pl.program_id(1)

        @pl.when(i == 0)
        def _():
            carry[...] = jnp.zeros_like(carry)

        dcum = (dr_ref[0] - dc_ref[0]) + (dr_ref[1] - dc_ref[1]) + (dr_ref[2] - dc_ref[2]) + (dr_ref[3] - dc_ref[3])
        rc = _dot_f32(_tri(ts, True), dcum) + carry[...]
        carry[...] = rc[0:1, :]
        z = p_ref[...] + b_ref[...]
        lane = lax.broadcasted_iota(jnp.int32, (ts, 128), 1)
        dz = jnp.where(lane < BH, rc * jax.nn.sigmoid(-z), 0.0)
        dp_ref[...] = dz.astype(dp_ref.dtype)
        upd = jnp.concatenate([jnp.sum(dz, axis=0, keepdims=True), jnp.zeros((7, 128), F32)], axis=0)
        first = (b_id == 0) & (i == 0)

        @pl.when(first)
        def _():
            sm_ref[...] = upd

        @pl.when(jnp.logical_not(first))
        def _():
            sm_ref[...] += upd

    def rows(b, i):
        return b * nb + (nb - 1 - i)

    return pl.pallas_call(
        body, name="fox_dbf", grid=(nbatch, nb),
        in_specs=[pl.BlockSpec((ts, 128), lambda b, i: (rows(b, i), 0)),
                  pl.BlockSpec((1, 128), lambda b, i: (0, 0)),
                  pl.BlockSpec((4, ts, 128), lambda b, i: (0, rows(b, i), 0)),
                  pl.BlockSpec((4, ts, 128), lambda b, i: (0, rows(b, i), 0)),
                  pl.BlockSpec(memory_space=pl.ANY)],
        out_specs=[pl.BlockSpec((ts, 128), lambda b, i: (rows(b, i), COL_BF // 128)),
                   pl.BlockSpec((8, 128), lambda b, i: (0, 0))],
        out_shape=[jax.ShapeDtypeStruct((t, NP), BF16), jax.ShapeDtypeStruct((8, 128), F32)],
        input_output_aliases={4: 0},
        scratch_shapes=[pltpu.VMEM((1, 128), F32)],
        compiler_params=_params(("arbitrary", "arbitrary")),
    )(proj, bias128, drs, dcs, dproj)


def _ln_stats(z):
    mu = jnp.mean(z, axis=-1, keepdims=True)
    zc = z - mu
    rstd = lax.rsqrt(jnp.mean(zc * zc, axis=-1, keepdims=True) + LN_EPS)
    return zc * rstd, rstd


def _ln_bwd(dy, xhat, rstd, w):
    dxh = dy * w
    return rstd * (dxh - jnp.mean(dxh, axis=-1, keepdims=True) - xhat * jnp.mean(dxh * xhat, axis=-1, keepdims=True))


def _merge_fwd(ya, yb, proj, x2, mod8, wba, wbb, wout, ln1w, ln1b, seq):
    t = x2.shape[0]
    tm = min(512, seq)
    tpb = seq // tm

    def body(ya_ref, yb_ref, g_ref, x_ref, mod_ref, wa_ref, wb_ref, wo_ref, lw_ref, lb_ref, mg_ref, u_ref, x1_ref):
        ga = jax.nn.sigmoid(g_ref[:, 0:D].astype(F32))
        gb = jax.nn.sigmoid(g_ref[:, D:2 * D].astype(F32))
        merged = (ga * jnp.dot(ya_ref[...], wa_ref[...], preferred_element_type=F32)
                  + gb * jnp.dot(yb_ref[...], wb_ref[...], preferred_element_type=F32))
        mg = merged.astype(BF16)
        mg_ref[...] = mg
        u = jnp.dot(mg, wo_ref[...], preferred_element_type=F32)
        u_ref[...] = u.astype(u_ref.dtype)
        xhat, _ = _ln_stats(ALPHA * x_ref[...] + (1.0 + mod_ref[0, 2:3, :]) * u)
        x1_ref[...] = xhat * lw_ref[...] + lb_ref[...]

    tok = lambda w: pl.BlockSpec((tm, w), lambda i: (i, 0))
    full = lambda a: pl.BlockSpec(a.shape, lambda i: (0,) * a.ndim)
    return pl.pallas_call(
        body, name="merge_fwd", grid=(t // tm,),
        in_specs=[tok(512), tok(512), pl.BlockSpec((tm, 2048), lambda i: (i, COL_GATES // 2048)), tok(D),
                  pl.BlockSpec((1, 8, D), lambda i: (i // tpb, 0, 0)),
                  full(wba), full(wbb), full(wout), full(ln1w), full(ln1b)],
        out_specs=[tok(D), tok(D), tok(D)],
        out_shape=[jax.ShapeDtypeStruct((t, D), BF16), jax.ShapeDtypeStruct((t, D), BF16),
                   jax.ShapeDtypeStruct((t, D), F32)],
        compiler_params=_params(("parallel",), 48),
    )(ya, yb, proj, x2, mod8, wba, wbb, wout, ln1w, ln1b)


def _merge_bwd(du, ya, yb, proj, wba, wbb, wout, token, seq):
    t = du.shape[0]
    tm = min(512, seq)

    def body(du_ref, ya_ref, yb_ref, g_ref, wa_ref, wb_ref, wo_ref, token_ref,
             dp_ref, dpa_ref, dpb_ref, dya_ref, dyb_ref):
        del token_ref
        ga = jax.nn.sigmoid(g_ref[:, 0:D].astype(F32))
        gb = jax.nn.sigmoid(g_ref[:, D:2 * D].astype(F32))
        dm = _dot_nt(du_ref[...], wo_ref[...])
        pa = jnp.dot(ya_ref[...], wa_ref[...], preferred_element_type=F32)
        pb = jnp.dot(yb_ref[...], wb_ref[...], preferred_element_type=F32)
        dpa = (dm * ga).astype(BF16)
        dpb = (dm * gb).astype(BF16)
        dpa_ref[...] = dpa
        dpb_ref[...] = dpb
        dp_ref[:, 0:D] = (dm * pa * ga * (1.0 - ga)).astype(BF16)
        dp_ref[:, D:2 * D] = (dm * pb * gb * (1.0 - gb)).astype(BF16)
        dya_ref[...] = _dot_nt(dpa, wa_ref[...])
        dyb_ref[...] = _dot_nt(dpb, wb_ref[...])

    tok = lambda w: pl.BlockSpec((tm, w), lambda i: (i, 0))
    full = lambda a: pl.BlockSpec(a.shape, lambda i: (0,) * a.ndim)
    return pl.pallas_call(
        body, name="merge_bwd", grid=(t // tm,),
        in_specs=[tok(D), tok(512), tok(512), pl.BlockSpec((tm, 2048), lambda i: (i, COL_GATES // 2048)),
                  full(wba), full(wbb), full(wout), full(token)],
        out_specs=[pl.BlockSpec((tm, 2048), lambda i: (i, COL_GATES // 2048)), tok(D), tok(D), tok(512), tok(512)],
        out_shape=[jax.ShapeDtypeStruct((t, NP), BF16), jax.ShapeDtypeStruct((t, D), BF16),
                   jax.ShapeDtypeStruct((t, D), BF16), jax.ShapeDtypeStruct((t, 512), F32),
                   jax.ShapeDtypeStruct((t, 512), F32)],
        compiler_params=_params(("parallel",), 48),
    )(du, ya, yb, proj, wba, wbb, wout, token)


def _ffn_fwd(x1, mod8, wg, wu, wd, target, ln2w, ln2b, seq):
    t = x1.shape[0]
    tm = min(FFN_TOKENS, seq)
    nf, tf, _ = wg.shape
    tpb = seq // tm
    nbatch = t // seq

    def body(x_ref, mod_ref, wg_ref, wu_ref, wd_ref, t_ref, lw_ref, lb_ref,
             a_ref, b_ref, h_s, dz_ref, st_ref, dm_ref, acc):
        i, j = pl.program_id(0), pl.program_id(1)

        @pl.when(j == 0)
        def _():
            h_s[...] = (x_ref[...] * (1.0 + mod_ref[0, 4:5, :]) + mod_ref[0, 3:4, :]).astype(BF16)
            acc[...] = jnp.zeros_like(acc)

        a = _dot_nt(h_s[...], wg_ref[0])
        b = _dot_nt(h_s[...], wu_ref[0])
        a_ref[0] = a.astype(BF16)
        b_ref[0] = b.astype(BF16)
        acc[...] += _dot(a * jax.nn.sigmoid(a) * b, wd_ref[0])

        @pl.when(j == nf - 1)
        def _():
            ffn = acc[...]
            xhat, rstd = _ln_stats(ALPHA * x_ref[...] + (1.0 + mod_ref[0, 5:6, :]) * ffn)
            diff = xhat * lw_ref[...] + lb_ref[...] - t_ref[...]
            loss = 0.5 * jnp.sum(jnp.sum(diff * diff, axis=-1, keepdims=True), axis=0, keepdims=True) / D
            dy = diff * (1.0 / D)
            dz = _ln_bwd(dy, xhat, rstd, lw_ref[...])
            dz_ref[...] = dz
            lane = lax.broadcasted_iota(jnp.int32, (1, D), 1)
            upd = jnp.concatenate(
                [jnp.sum(dy * xhat, axis=0, keepdims=True), jnp.sum(dy, axis=0, keepdims=True),
                 jnp.where(lane == 0, loss, 0.0), jnp.zeros((5, D), F32)], axis=0)
            dmu = jnp.concatenate(
                [jnp.zeros((5, D), F32), jnp.sum(dz * ffn, axis=0, keepdims=True), jnp.zeros((2, D), F32)], axis=0)

            @pl.when(i == 0)
            def _():
                st_ref[...] = upd

            @pl.when(i > 0)
            def _():
                st_ref[...] += upd

            @pl.when(i % tpb == 0)
            def _():
                dm_ref[0] = dmu

            @pl.when(i % tpb != 0)
            def _():
                dm_ref[0] += dmu

    row = lambda: pl.BlockSpec((tm, D), lambda i, j: (i, 0))
    vec = lambda: pl.BlockSpec((1, D), lambda i, j: (0, 0))
    return pl.pallas_call(
        body, name="ffn_fwd", grid=(t // tm, nf),
        in_specs=[row(), pl.BlockSpec((1, 8, D), lambda i, j: (i // tpb, 0, 0)),
                  pl.BlockSpec((1, tf, D), lambda i, j: (j, 0, 0)), pl.BlockSpec((1, tf, D), lambda i, j: (j, 0, 0)),
                  pl.BlockSpec((1, tf, D), lambda i, j: (j, 0, 0)), row(), vec(), vec()],
        out_specs=[pl.BlockSpec((1, tm, tf), lambda i, j: (j, i, 0)), pl.BlockSpec((1, tm, tf), lambda i, j: (j, i, 0)),
                   row(), row(), pl.BlockSpec((8, D), lambda i, j: (0, 0)),
                   pl.BlockSpec((1, 8, D), lambda i, j: (i // tpb, 0, 0))],
        out_shape=[jax.ShapeDtypeStruct((nf, t, tf), BF16), jax.ShapeDtypeStruct((nf, t, tf), BF16),
                   jax.ShapeDtypeStruct((t, D), BF16),
                   jax.ShapeDtypeStruct((t, D), F32), jax.ShapeDtypeStruct((8, D), F32),
                   jax.ShapeDtypeStruct((nbatch, 8, D), F32)],
        scratch_shapes=[pltpu.VMEM((tm, D), F32)],
        compiler_params=_params(("arbitrary", "arbitrary"), 60),
    )(x1, mod8, wg, wu, wd, target, ln2w, ln2b)


def _ffn_bwd(dz2, a, b, wg, wu, wd, x1, x2, u, mod8, ln1w, seq):
    t = x1.shape[0]
    tm = min(512, seq)
    nf, tf, _ = wg.shape
    tpb = seq // tm
    nbatch = t // seq

    def body(dz_ref, a_ref, b_ref, wg_ref, wu_ref, wd_ref, x1_ref, x_ref, u_ref, mod_ref, lw_ref,
             da_ref, db_ref, hm_ref, df_ref, du_ref, dxp_ref, st_ref, dm_ref, acc):
        i, j = pl.program_id(0), pl.program_id(1)

        @pl.when(j == 0)
        def _():
            df_ref[...] = ((1.0 + mod_ref[0, 5:6, :]) * dz_ref[...]).astype(BF16)
            acc[...] = jnp.zeros_like(acc)

        dhm = _dot_nt(df_ref[...], wd_ref[0])
        av = a_ref[0].astype(F32)
        bv = b_ref[0].astype(F32)
        sg = jax.nn.sigmoid(av)
        sl = av * sg
        hm_ref[0] = (sl * bv).astype(BF16)
        da = (dhm * bv * (sg * (1.0 + av * (1.0 - sg)))).astype(BF16)
        db = (dhm * sl).astype(BF16)
        da_ref[0] = da
        db_ref[0] = db
        acc[...] += _dot(da, wg_ref[0]) + _dot(db, wu_ref[0])

        @pl.when(j == nf - 1)
        def _():
            dh2 = acc[...]
            x1v = x1_ref[...]
            uv = u_ref[...].astype(F32)
            dx1 = ALPHA * dz_ref[...] + dh2 * (1.0 + mod_ref[0, 4:5, :])
            xhat, rstd = _ln_stats(ALPHA * x_ref[...] + (1.0 + mod_ref[0, 2:3, :]) * uv)
            dz1 = _ln_bwd(dx1, xhat, rstd, lw_ref[...])
            du_ref[...] = ((1.0 + mod_ref[0, 2:3, :]) * dz1).astype(BF16)
            dxp_ref[...] = (ALPHA * dz1).astype(dxp_ref.dtype)
            upd = jnp.concatenate(
                [jnp.sum(dx1 * xhat, axis=0, keepdims=True), jnp.sum(dx1, axis=0, keepdims=True),
                 jnp.zeros((6, D), F32)], axis=0)
            dmu = jnp.concatenate(
                [jnp.zeros((2, D), F32), jnp.sum(dz1 * uv, axis=0, keepdims=True),
                 jnp.sum(dh2, axis=0, keepdims=True), jnp.sum(dh2 * x1v, axis=0, keepdims=True),
                 jnp.zeros((3, D), F32)], axis=0)

            @pl.when(i == 0)
            def _():
                st_ref[...] = upd

            @pl.when(i > 0)
            def _():
                st_ref[...] += upd

            @pl.when(i % tpb == 0)
            def _():
                dm_ref[0] = dmu

            @pl.when(i % tpb != 0)
            def _():
                dm_ref[0] += dmu

    row = lambda: pl.BlockSpec((tm, D), lambda i, j: (i, 0))
    ffb = lambda: pl.BlockSpec((1, tm, tf), lambda i, j: (j, i, 0))
    return pl.pallas_call(
        body, name="ffn_bwd", grid=(t // tm, nf),
        in_specs=[row(), ffb(), ffb(),
                  pl.BlockSpec((1, tf, D), lambda i, j: (j, 0, 0)), pl.BlockSpec((1, tf, D), lambda i, j: (j, 0, 0)),
                  pl.BlockSpec((1, tf, D), lambda i, j: (j, 0, 0)), row(), row(), row(),
                  pl.BlockSpec((1, 8, D), lambda i, j: (i // tpb, 0, 0)), pl.BlockSpec((1, D), lambda i, j: (0, 0))],
        out_specs=[ffb(), ffb(), ffb(), row(), row(), row(), pl.BlockSpec((8, D), lambda i, j: (0, 0)),
                   pl.BlockSpec((1, 8, D), lambda i, j: (i // tpb, 0, 0))],
        out_shape=[jax.ShapeDtypeStruct((nf, t, tf), BF16), jax.ShapeDtypeStruct((nf, t, tf), BF16),
                   jax.ShapeDtypeStruct((nf, t, tf), BF16), jax.ShapeDtypeStruct((t, D), BF16),
                   jax.ShapeDtypeStruct((t, D), BF16), jax.ShapeDtypeStruct((t, D), BF16),
                   jax.ShapeDtypeStruct((8, D), F32), jax.ShapeDtypeStruct((nbatch, 8, D), F32)],
        scratch_shapes=[pltpu.VMEM((tm, D), F32)],
        compiler_params=_params(("arbitrary", "arbitrary"), 60),
    )(dz2, a, b, wg, wu, wd, x1, x2, u, mod8, ln1w)


def _adamw_math(w, g, m, v):
    m = B1 * m + (1.0 - B1) * g
    v = B2 * v + (1.0 - B2) * (g * g)
    m_hat = m / (1.0 - B1 ** STEP)
    v_hat = v / (1.0 - B2 ** STEP)
    return -LR * (m_hat / (jnp.sqrt(v_hat) + EPS) + WD * w), m, v


def _adamw(w, g, m, v, name):
    rows, cols = w.shape
    tr = rows
    for cand in (128, 64, 32, 16, 8):
        if rows % cand == 0:
            tr = cand
            break

    def body(w_ref, g_ref, m_ref, v_ref, d_ref, mo_ref, vo_ref):
        d, mn, vn = _adamw_math(w_ref[...], g_ref[...], m_ref[...], v_ref[...])
        d_ref[...] = d
        mo_ref[...] = mn
        vo_ref[...] = vn

    spec = pl.BlockSpec((tr, cols), lambda i: (i, 0))
    return pl.pallas_call(
        body, name=name, grid=(rows // tr,), in_specs=[spec] * 4, out_specs=[spec] * 3,
        out_shape=[jax.ShapeDtypeStruct((rows, cols), F32)] * 3,
        compiler_params=_params(("parallel",), 48),
    )(w, g, m, v)


def _adamw_halves(w, g_mine, g_sib, m, v, c_idx, name):
    rows, cols = w.shape
    hr = rows // 2
    tr = next(cand for cand in (128, 88, 64, 32, 16, 8) if hr % cand == 0)
    tph = hr // tr

    def body(c_ref, w_ref, gm_ref, gs_ref, m_ref, v_ref, g_ref, d_ref, mo_ref, vo_ref):
        g = jnp.where(pl.program_id(0) == c_ref[0], gm_ref[...], gs_ref[...])
        d, mn, vn = _adamw_math(w_ref[...], g, m_ref[...], v_ref[...])
        g_ref[...] = g
        d_ref[...] = d
        mo_ref[...] = mn
        vo_ref[...] = vn

    full = pl.BlockSpec((tr, cols), lambda h, i, c: (h * tph + i, 0))
    half = pl.BlockSpec((tr, cols), lambda h, i, c: (i, 0))
    return pl.pallas_call(
        body, name=name,
        grid_spec=pltpu.PrefetchScalarGridSpec(
            num_scalar_prefetch=1, grid=(2, tph), in_specs=[full, half, half, full, full], out_specs=[full] * 4),
        out_shape=[jax.ShapeDtypeStruct((rows, cols), F32)] * 4,
        compiler_params=_params(("parallel", "parallel"), 48),
    )(c_idx, w, g_mine, g_sib, m, v)


def _grad_w_ada(c_all, dmod_cols):
    def body(c_ref, d_ref, o_ref):
        c = c_ref[...]
        o_ref[...] = lax.dot_general(c * jax.nn.sigmoid(c), d_ref[...], (((0,), (0,)), ((), ())),
                                     preferred_element_type=F32, precision=HIGHEST)

    return pl.pallas_call(
        body, name="grad_w_ada", out_shape=jax.ShapeDtypeStruct((D, dmod_cols.shape[1]), F32),
        compiler_params=_params(vmem_mb=48),
    )(c_all, dmod_cols)


def _small_update(gath, w8, m8, v8):
    def body(g_ref, w_ref, m_ref, v_ref, go_ref, d_ref, mo_ref, vo_ref):
        g0 = g_ref[0, 0:1, :] + g_ref[0, 1:2, :]
        g1 = g_ref[0, 2:3, :]
        for dev in range(1, N_DEV):
            g0 = g0 + (g_ref[dev, 0:1, :] + g_ref[dev, 1:2, :])
            g1 = g1 + g_ref[dev, 2:3, :]
        w = w_ref[...]
        lb = jax.nn.sigmoid(w[1:2, O_LB0:O_LB1] - w[1:2, O_LB1:O_FOX])
        fac = lb * (1.0 - lb)
        g1 = jnp.concatenate([g1[:, :O_LB0], g1[:, O_LB0:O_LB1] * fac, -g1[:, O_LB1:O_FOX] * fac, g1[:, O_FOX:]],
                             axis=1)
        g = jnp.concatenate([g0, g1, jnp.zeros((6, SMALL_W), F32)], axis=0)
        d, mn, vn = _adamw_math(w, g, m_ref[...], v_ref[...])
        go_ref[...] = g
        d_ref[...] = d
        mo_ref[...] = mn
        vo_ref[...] = vn

    return pl.pallas_call(
        body, name="small_update", out_shape=[jax.ShapeDtypeStruct((8, SMALL_W), F32)] * 4,
        compiler_params=_params(vmem_mb=48),
    )(gath, w8, m8, v8)


def _pack_small(b_ada, ln1w, ln1b, ln2w, ln2b, norm_w, lb_logits, fox):
    row1 = jnp.concatenate([ln1w, ln1b, ln2w, ln2b, norm_w, lb_logits[0:1], lb_logits[1:2], fox,
                            jnp.zeros((1, SMALL_W - O_FOX - BH), F32)], axis=1)
    return jnp.concatenate([b_ada, row1, jnp.zeros((6, SMALL_W), F32)], axis=0)


def _unpack_small(p):
    r = p[1:2]
    lb = jnp.concatenate([r[:, O_LB0:O_LB1], r[:, O_LB1:O_FOX]], axis=0)
    return dict(b_ada=p[0:1], ln1_w=r[:, O_LN1W:O_LN1B], ln1_b=r[:, O_LN1B:O_LN2W], ln2_w=r[:, O_LN2W:O_LN2B],
                ln2_b=r[:, O_LN2B:O_NORM], hgrn_norm_w=r[:, O_NORM:O_LB0], lb_logits=lb,
                fox_f_bias=r[:, O_FOX:O_FOX + BH])


_BIG = ("w_in", "w_branch_a", "w_branch_b", "w_out", "w_ffn_gate", "w_ffn_up", "w_ffn_down")
_TRANSPOSED = ("w_ffn_gate", "w_ffn_up")


def _cols_of_chips(stacked):
    return jnp.concatenate([stacked[k] for k in range(N_CHIPS)], axis=1)


def kernel(x, c, w_ada, b_ada, w_in, fox_f_bias, lb_logits, hgrn_norm_w, w_branch_a, w_branch_b, w_out, ln1_w, ln1_b, w_ffn_gate, w_ffn_up, w_ffn_down, ln2_w, ln2_b, loss_target, m_w_ada, m_b_ada, m_w_in, m_fox_f_bias, m_lb_logits, m_hgrn_norm_w, m_w_branch_a, m_w_branch_b, m_w_out, m_ln1_w, m_ln1_b, m_w_ffn_gate, m_w_ffn_up, m_w_ffn_down, m_ln2_w, m_ln2_b, v_w_ada, v_b_ada, v_w_in, v_fox_f_bias, v_lb_logits, v_hgrn_norm_w, v_w_branch_a, v_w_branch_b, v_w_out, v_ln1_w, v_ln1_b, v_w_ffn_gate, v_w_ffn_up, v_w_ffn_down, v_ln2_w, v_ln2_b):
    nbatch, seq, _ = x.shape
    t = nbatch * seq
    ax, ay, ac = lax.axis_index("x"), lax.axis_index("y"), lax.axis_index("c")
    chip = 2 * ax + ay
    dev = 2 * chip + ac
    chip_arr = jnp.reshape(chip, (1,)).astype(jnp.int32)
    core_arr = jnp.reshape(ac, (1,)).astype(jnp.int32)

    tr = lambda a: jnp.swapaxes(a[0], 0, 1)
    shard_w = dict(w_in=w_in[0], w_branch_a=w_branch_a[0], w_branch_b=w_branch_b[0], w_out=w_out[0],
                   w_ffn_gate=tr(w_ffn_gate), w_ffn_up=tr(w_ffn_up), w_ffn_down=w_ffn_down[0])
    shard_m = dict(w_in=m_w_in[0], w_branch_a=m_w_branch_a[0], w_branch_b=m_w_branch_b[0], w_out=m_w_out[0],
                   w_ffn_gate=tr(m_w_ffn_gate), w_ffn_up=tr(m_w_ffn_up), w_ffn_down=m_w_ffn_down[0])
    shard_v = dict(w_in=v_w_in[0], w_branch_a=v_w_branch_a[0], w_branch_b=v_w_branch_b[0], w_out=v_w_out[0],
                   w_ffn_gate=tr(v_w_ffn_gate), w_ffn_up=tr(v_w_ffn_up), w_ffn_down=v_w_ffn_down[0])

    shard16 = {n: shard_w[n].astype(BF16) for n in _BIG}

    def with_mine(gathered, n):
        return lax.dynamic_update_slice(gathered, shard16[n][None], (chip, 0, 0))

    def gather_start(names, tag):
        return _split_start(_gather_copies, [shard16[n] for n in names],
                            [lax.empty((N_CHIPS,) + shard16[n].shape, BF16) for n in names], "gather_" + tag + "_start")

    def gather_finish(split, after, tag):
        send, recv, src, land, _ = split
        return _pass_to_sibling(_split_wait(_gather_copies, send, recv, src, land, after, "gather_" + tag + "_wait"),
                                "gather_" + tag + "_pass")

    late = _BIG[1:]
    first_split = gather_start(("w_in",), "first")
    late_split = gather_start(late, "late")
    late_token = first_split[4] + late_split[4]

    c8 = jnp.concatenate([c, jnp.zeros((8 - nbatch, D), F32)], axis=0)
    c_all = _allgather8(c8, "gather_c")[:, :nbatch, :].reshape(N_DEV * nbatch, D)
    ncol = w_ada.shape[2]
    b_cols = lax.dynamic_slice_in_dim(b_ada, chip * ncol, ncol, axis=1)
    mod_g = _allgather8(_mod_shard(c_all, w_ada[0], b_cols), "gather_mod")
    mod_all = jnp.concatenate([mod_g[2 * k] for k in range(N_CHIPS)], axis=1)
    mod_mine = lax.dynamic_slice_in_dim(mod_all, dev * nbatch, nbatch, axis=0)
    mod8 = jnp.concatenate([mod_mine.reshape(nbatch, 6, D), jnp.zeros((nbatch, 2, D), F32)], axis=1)
    mod8 = mod8 + late_token[0, 0]
    w_p = _permute_cols(_cols_of_chips(with_mine(gather_finish(first_split, mod8, "first")[0], "w_in")))

    x2 = x.reshape(t, D)
    tgt2 = loss_target.reshape(t, D)
    bias128 = jnp.concatenate([fox_f_bias, jnp.zeros((1, 128 - BH), F32)], axis=1)

    proj, h16 = _proj(x2, mod8, w_p, seq, BF16, "proj")
    projf = _rows_matmul(h16, w_p[:, COL_BF:], "proj_forget")
    ya, ckpt = _hgrn_fwd(proj, lb_logits, hgrn_norm_w, nbatch, seq)
    cum_cols = _fox_cum(projf, bias128, nbatch, seq)
    yb, lse = _fox_fwd(proj, cum_cols, nbatch, seq)
    full = {n: with_mine(g, n) for n, g in zip(late, gather_finish(late_split, yb, "late"))}
    wba, wbb = _cols_of_chips(full["w_branch_a"]), _cols_of_chips(full["w_branch_b"])
    wout = full["w_out"].reshape(D, D)
    wg_t, wu_t, wd = full["w_ffn_gate"], full["w_ffn_up"], full["w_ffn_down"]
    merged, u, x1 = _merge_fwd(ya, yb, proj, x2, mod8, wba, wbb, wout, ln1_w, ln1_b, seq)
    a_pre, b_pre, h2, dz2, st2, dm2 = _ffn_fwd(x1, mod8, wg_t, wu_t, wd, tgt2, ln2_w, ln2_b, seq)
    loss = lax.psum(st2[2, 0], ("x", "y", "c"))

    da, db, hmid, dffn, du, dxp, st1, dm1 = _ffn_bwd(dz2, a_pre, b_pre, wg_t, wu_t, wd, x1, x2, u, mod8, ln1_w, seq)
    g_st = {}
    g_st["w_ffn_down"] = _tn_matmul(hmid, dffn, "dw_ffn_down", seq)
    g_st["w_ffn_gate"] = _tn_matmul(da, h2, "dw_ffn_gate", seq)
    g_st["w_ffn_up"] = _tn_matmul(db, h2, "dw_ffn_up", seq)
    g_st["w_out"] = _tn_matmul(merged, du, "dw_out", seq).reshape(N_CHIPS, D // N_CHIPS, D)

    def sum_over_cores(names, tag):
        g_list = [g_st[n] for n in names]
        return [_add_my_half(g, o, core_arr, "grad_add_halves_" + n)
                for n, g, o in zip(names, g_list, _swap_halves(g_list, "grad_swap_halves_" + tag))]

    early = ("w_ffn_down", "w_ffn_gate", "w_ffn_up", "w_out")
    e_halves = sum_over_cores(early, "early")
    e_send, e_recv, e_src, e_land, e_token = _split_start(
        _scatter_copies, [h16 for _, h16 in e_halves],
        [lax.empty((3,) + h16.shape[1:], BF16) for _, h16 in e_halves], "grad_scatter_early_start")
    dproj, dpa, dpb, dya, dyb = _merge_bwd(du, ya, yb, proj, wba, wbb, wout, e_token, seq)
    g_st["w_branch_a"] = _tn_matmul(ya, dpa, "dw_branch_a", seq, split=D // N_CHIPS)
    g_st["w_branch_b"] = _tn_matmul(yb, dpb, "dw_branch_b", seq, split=D // N_CHIPS)
    dproj, dq, drs, dcs = _fox_bwd(proj, cum_cols, lse, yb, dyb, dproj, nbatch, seq)
    dproj = _place_cols(dproj, dq, COL_BQ)
    dproj, sm_fox = _fox_dbf(projf, bias128, drs, dcs, dproj, nbatch, seq)
    dproj, sm_hgrn = _hgrn_bwd(proj, dya, ckpt, lb_logits, hgrn_norm_w, dproj, nbatch, seq)
    grad_x2, dm0 = _dh_kernel(dproj, w_p, x2, dxp, mod8, seq)
    dw_in = _tn_matmul(h16, dproj, "dw_in", seq)
    g_st["w_in"] = _unpermute_to_chips(dw_in)

    e_recv = _split_wait(_scatter_copies, e_send, e_recv, e_src, e_land, dw_in, "grad_scatter_early_wait")
    rest = ("w_in", "w_branch_a", "w_branch_b")
    r_halves = sum_over_cores(rest, "rest")
    r_send, r_rcv, r_src, r_land, r_token = _split_start(
        _scatter_copies, [h16 for _, h16 in r_halves],
        [lax.empty((3,) + h16.shape[1:], BF16) for _, h16 in r_halves], "grad_scatter_rest_start")

    def finish(names, halves, recv, token, tag):
        g_mine = [_add_chips(h32, r, chip_arr, "grad_add_chips_" + n) for n, (h32, _), r in zip(names, halves, recv)]
        g_sib = _join_halves(g_mine, token, "grad_join_halves_" + tag)
        for n, gm, gs in zip(names, g_mine, g_sib):
            grads[n], deltas[n], new_m[n], new_v[n] = _adamw_halves(
                shard_w[n], gm, gs, shard_m[n], shard_v[n], core_arr, "adamw_" + n)

    grads, deltas, new_m, new_v = {}, {}, {}, {}
    finish(early, e_halves, e_recv, r_token, "early")

    dmod = (dm0 + dm1 + dm2)[:, :6, :].reshape(nbatch, 6 * D)
    row2 = jnp.concatenate([st1[0:1], st1[1:2], st2[0:1], st2[1:2], sm_hgrn[1:2], sm_hgrn[0:1], sm_hgrn[0:1],
                            sm_fox[0:1, :BH], jnp.zeros((1, SMALL_W - O_FOX - BH), F32)], axis=1)
    spack = jnp.concatenate([dmod, row2, jnp.zeros((8 - nbatch - 1, SMALL_W), F32)], axis=0)
    spack = spack + r_token[0, 0]
    gath = _allgather8(spack, "gather_small")
    w8 = _pack_small(b_ada, ln1_w, ln1_b, ln2_w, ln2_b, hgrn_norm_w, lb_logits, fox_f_bias)
    m8 = _pack_small(m_b_ada, m_ln1_w, m_ln1_b, m_ln2_w, m_ln2_b, m_hgrn_norm_w, m_lb_logits, m_fox_f_bias)
    v8 = _pack_small(v_b_ada, v_ln1_w, v_ln1_b, v_ln2_w, v_ln2_b, v_hgrn_norm_w, v_lb_logits, v_fox_f_bias)
    sg, sd, smn, svn = (_unpack_small(p) for p in _small_update(gath, w8, m8, v8))
    dmod_all = gath[:, :nbatch, :].reshape(N_DEV * nbatch, SMALL_W)
    g_ada = _grad_w_ada(c_all, lax.dynamic_slice_in_dim(dmod_all, chip * ncol, ncol, axis=1))

    for group, small in zip((grads, deltas, new_m, new_v), (sg, sd, smn, svn)):
        group.update(small)
    grads["w_ada"] = g_ada
    deltas["w_ada"], new_m["w_ada"], new_v["w_ada"] = _adamw(w_ada[0], g_ada, m_w_ada[0], v_w_ada[0], "adamw_w_ada")
    done = sum(new_v[n][0:8, 0:128] for n in early) + new_v["w_ada"][0:8, 0:128]
    r_recv = _split_wait(_scatter_copies, r_send, r_rcv, r_src, r_land, done, "grad_scatter_rest_wait")
    finish(rest, r_halves, r_recv, late_token, "rest")

    names = ["w_ada", "b_ada", "w_in", "fox_f_bias", "lb_logits", "hgrn_norm_w", "w_branch_a", "w_branch_b", "w_out",
             "ln1_w", "ln1_b", "w_ffn_gate", "w_ffn_up", "w_ffn_down", "ln2_w", "ln2_b"]
    shapes = dict(w_ada=w_ada.shape, b_ada=b_ada.shape, w_in=w_in.shape, fox_f_bias=fox_f_bias.shape,
                  lb_logits=lb_logits.shape, hgrn_norm_w=hgrn_norm_w.shape, w_branch_a=w_branch_a.shape,
                  w_branch_b=w_branch_b.shape, w_out=w_out.shape, ln1_w=ln1_w.shape, ln1_b=ln1_b.shape,
                  w_ffn_gate=w_ffn_gate.shape, w_ffn_up=w_ffn_up.shape, w_ffn_down=w_ffn_down.shape,
                  ln2_w=ln2_w.shape, ln2_b=ln2_b.shape)
    outs = [loss, grad_x2.reshape(x.shape)]
    for group in (grads, deltas, new_m, new_v):
        outs += [(jnp.swapaxes(group[n], 0, 1) if n in _TRANSPOSED else group[n]).reshape(shapes[n]) for n in names]
    return tuple(outs)
```

```python
import functools

import jax
import jax.numpy as jnp
from jax import lax
from jax.experimental import pallas as pl
from jax.experimental.pallas import tpu as pltpu

F32 = jnp.float32
BF16 = jnp.bfloat16
MESH = pl.DeviceIdType.MESH
HIGHEST = lax.Precision.HIGHEST

D = 1024
AW = 512
AH = 4
ADH = 128
BH = 8
BDH = 64
DFF = 2816
NIN = 5640
NP = 5760
N_CHIPS = 4
N_DEV = 8
HGRN_BLOCK = 256
FFN_TOKENS = 512
COL_GATES = 0
COL_A = 2048
COL_BQ = 4096
COL_KV = 4608
COL_BF = 5632
HGRN_HEADS = 4
FOX_PAIRS = 2
ALPHA = 2.0 ** 0.25
LN_EPS = 1e-5
RMS_EPS = 1e-6
NEG = -1e30
LOG2E = 1.4426950408889634
LR, B1, B2, EPS, WD, STEP = 0.001, 0.9, 0.999, 1e-08, 0.01, 10
SMALL_W = 6144
O_LN1W, O_LN1B, O_LN2W, O_LN2B, O_NORM, O_LB0, O_LB1, O_FOX = 0, 1024, 2048, 3072, 4096, 4608, 5120, 5632


def _params(sem=None, vmem_mb=None):
    kw = {}
    if sem is not None:
        kw["dimension_semantics"] = sem
    if vmem_mb is not None:
        kw["vmem_limit_bytes"] = vmem_mb << 20
    return pltpu.CompilerParams(**kw)


def _dot(a, b):
    return jnp.dot(a.astype(BF16), b.astype(BF16), preferred_element_type=F32)


def _dot_nt(a, b):
    return lax.dot_general(a.astype(BF16), b.astype(BF16), (((1,), (1,)), ((), ())), preferred_element_type=F32)


def _dot_tn(a, b):
    return lax.dot_general(a.astype(BF16), b.astype(BF16), (((0,), (0,)), ((), ())), preferred_element_type=F32)


def _dot_f32(a, b):
    return jnp.dot(a, b, preferred_element_type=F32, precision=HIGHEST)


def _perm_segments():
    segs = [(3592, 5640)]
    for h in range(4):
        segs += [(128 * h + 512 * t, 128 * h + 512 * t + 128) for t in range(4)]
    segs += [(2048, 2560)]
    for p in range(4):
        segs += [(2560 + 128 * p, 2688 + 128 * p), (3072 + 128 * p, 3200 + 128 * p)]
    segs += [(3584, 3592)]
    return segs


def _permute_cols(w):
    parts = [w[:, a:b] for a, b in _perm_segments()]
    parts.append(jnp.zeros((w.shape[0], NP - NIN), w.dtype))
    return jnp.concatenate(parts, axis=1)


def _unpermute_to_chips(g):
    pos, where = 0, []
    for a, b in _perm_segments():
        where.append((a, b, pos))
        pos += b - a
    ncin = NIN // N_CHIPS
    out = []
    for k in range(N_CHIPS):
        lo, hi = k * ncin, (k + 1) * ncin
        parts = [g[:, p + max(a, lo) - a:p + min(b, hi) - a] for a, b, p in sorted(where) if max(a, lo) < min(b, hi)]
        out.append(jnp.concatenate(parts, axis=1))
    return jnp.stack(out)


def _allgather8(v, name):
    rows, cols = v.shape

    def body(x_ref, out_ref, send_sems, recv_sems, local_sem):
        x, y, c = lax.axis_index("x"), lax.axis_index("y"), lax.axis_index("c")
        me, sibling = (x, y, c), (x, y, 1 - c)
        chips = [(1 - x, y), (x, 1 - y), (1 - x, 1 - y)]

        def slot(px, py, pc):
            return out_ref.at[4 * px + 2 * py + pc]

        def copy(k, block, to, src=None):
            return pltpu.make_async_remote_copy(
                src_ref=slot(*block) if src is None else src, dst_ref=slot(*block),
                send_sem=send_sems.at[k], recv_sem=recv_sems.at[k], device_id=to, device_id_type=MESH)

        mine = pltpu.make_async_copy(x_ref, slot(*me), local_sem)
        mine.start()
        first = [copy(0, me, sibling, src=x_ref)]
        first += [copy(1 + j, me, (*chip, c), src=x_ref) for j, chip in enumerate(chips)]
        for cp in first:
            cp.start()
        passed = [copy(4 + j, (*chip, c), sibling) for j, chip in enumerate(chips)]
        for j, chip in enumerate(chips):
            copy(1 + j, (*chip, c), me).wait_recv()
            passed[j].start()
        copy(0, sibling, me).wait_recv()
        for j, chip in enumerate(chips):
            copy(4 + j, (*chip, 1 - c), me).wait_recv()
        for cp in first + passed:
            cp.wait_send()
        mine.wait()

    return pl.pallas_call(
        body, name=name,
        out_shape=jax.ShapeDtypeStruct((N_DEV, rows, cols), v.dtype),
        in_specs=[pl.BlockSpec(memory_space=pltpu.VMEM)],
        out_specs=pl.BlockSpec(memory_space=pltpu.VMEM),
        scratch_shapes=[pltpu.SemaphoreType.DMA((7,)), pltpu.SemaphoreType.DMA((7,)), pltpu.SemaphoreType.DMA],
    )(v)


def _hbm_specs(n):
    return [pl.BlockSpec(memory_space=pl.ANY)] * n


def _swap_halves(grads, name):
    n = len(grads)

    def body(*refs):
        ins, outs, (send_sems, recv_sems) = refs[:n], refs[n:2 * n], refs[2 * n:]
        x, y, c = lax.axis_index("x"), lax.axis_index("y"), lax.axis_index("c")
        cps = []
        for w in range(n):
            hr = ins[w].shape[1] // 2
            cps.append(pltpu.make_async_remote_copy(
                src_ref=ins[w].at[:, pl.ds((1 - c) * hr, hr), :], dst_ref=outs[w],
                send_sem=send_sems.at[w], recv_sem=recv_sems.at[w], device_id=(x, y, 1 - c), device_id_type=MESH))
        for cp in cps:
            cp.start()
        for cp in cps:
            cp.wait()

    return pl.pallas_call(
        body, name=name,
        out_shape=[jax.ShapeDtypeStruct((N_CHIPS, g.shape[1] // 2, g.shape[2]), g.dtype) for g in grads],
        in_specs=_hbm_specs(n), out_specs=_hbm_specs(n),
        scratch_shapes=[pltpu.SemaphoreType.DMA((n,)), pltpu.SemaphoreType.DMA((n,))],
    )(*grads)


def _join_halves(halves, token, name):
    n = len(halves)

    def body(*refs):
        ins, outs, (send_sems, recv_sems) = refs[:n], refs[n + 1:2 * n + 1], refs[2 * n + 1:]
        x, y, c = lax.axis_index("x"), lax.axis_index("y"), lax.axis_index("c")
        cps = [pltpu.make_async_remote_copy(
            src_ref=ins[w], dst_ref=outs[w], send_sem=send_sems.at[w], recv_sem=recv_sems.at[w],
            device_id=(x, y, 1 - c), device_id_type=MESH) for w in range(n)]
        for cp in cps:
            cp.start()
        for cp in cps:
            cp.wait()

    return pl.pallas_call(
        body, name=name,
        out_shape=[jax.ShapeDtypeStruct(h.shape, h.dtype) for h in halves],
        in_specs=_hbm_specs(n + 1), out_specs=_hbm_specs(n),
        scratch_shapes=[pltpu.SemaphoreType.DMA((n,)), pltpu.SemaphoreType.DMA((n,))],
    )(*halves, token)


def _in_hbm(v):
    return pltpu.with_memory_space_constraint(v, pltpu.HBM)


_SPLIT_COPY = pltpu.CompilerParams(has_side_effects=pltpu.SideEffectType.DATAFLOW_SIDE_EFFECTING)


def _gather_copies(srcs, lands, send_sems, recv_sems):
    x, y, c = lax.axis_index("x"), lax.axis_index("y"), lax.axis_index("c")
    cps = []
    for w, (src, land) in enumerate(zip(srcs, lands)):
        hr = src.shape[0] // 2
        for j, chip in enumerate([(1 - x, y), (x, 1 - y), (1 - x, 1 - y)]):
            cps.append(pltpu.make_async_remote_copy(
                src_ref=src.at[pl.ds(c * hr, hr), :], dst_ref=land.at[2 * x + y, pl.ds(c * hr, hr), :],
                send_sem=send_sems.at[3 * w + j], recv_sem=recv_sems.at[3 * w + j],
                device_id=(*chip, c), device_id_type=MESH))
    return cps


def _scatter_copies(srcs, lands, send_sems, recv_sems):
    x, y, c = lax.axis_index("x"), lax.axis_index("y"), lax.axis_index("c")
    cps = []
    for w, (src, land) in enumerate(zip(srcs, lands)):
        for j, chip in enumerate([(1 - x, y), (x, 1 - y), (1 - x, 1 - y)]):
            cps.append(pltpu.make_async_remote_copy(
                src_ref=src.at[2 * chip[0] + chip[1]], dst_ref=land.at[j],
                send_sem=send_sems.at[3 * w + j], recv_sem=recv_sems.at[3 * w + j],
                device_id=(*chip, c), device_id_type=MESH))
    return cps


def _split_start(copies, srcs, lands, name):
    n = len(srcs)

    def body(*refs):
        src, lnd, send_sems, recv_sems, token = refs[:n], refs[n:2 * n], refs[2 * n], refs[2 * n + 1], refs[-1]
        for cp in copies(src, lnd, send_sems, recv_sems):
            cp.start()
        token[...] = jnp.zeros_like(token)

    hbm = pl.BlockSpec(memory_space=pltpu.HBM)
    sem = pl.BlockSpec(memory_space=pltpu.SEMAPHORE)
    outs = pl.pallas_call(
        body, name=name,
        out_shape=(pltpu.SemaphoreType.DMA((3 * n,)), pltpu.SemaphoreType.DMA((3 * n,)),
                   *[pltpu.HBM(v.shape, v.dtype) for v in srcs + lands], jax.ShapeDtypeStruct((8, 128), F32)),
        in_specs=[hbm] * (2 * n),
        out_specs=(sem, sem, *([hbm] * (2 * n)), pl.BlockSpec(memory_space=pltpu.VMEM)),
        input_output_aliases={i: 2 + i for i in range(2 * n)},
        compiler_params=_SPLIT_COPY,
    )(*[_in_hbm(v) for v in srcs + lands])
    return outs[0], outs[1], list(outs[2:2 + n]), list(outs[2 + n:2 + 2 * n]), outs[-1]


def _split_wait(copies, send_sems, recv_sems, srcs, lands, after, name):
    n = len(srcs)

    def body(*refs):
        src, lnd, send_sems, recv_sems = refs[:n], refs[n:2 * n], refs[2 * n], refs[2 * n + 1]
        for cp in copies(src, lnd, send_sems, recv_sems):
            cp.wait_send()
            cp.wait_recv()

    hbm = pl.BlockSpec(memory_space=pltpu.HBM)
    sem = pl.BlockSpec(memory_space=pltpu.SEMAPHORE)
    outs = pl.pallas_call(
        body, name=name,
        out_shape=tuple(pltpu.HBM(v.shape, v.dtype) for v in srcs + lands),
        in_specs=[hbm] * (2 * n) + [sem, sem, pl.BlockSpec(memory_space=pl.ANY)],
        out_specs=tuple([hbm] * (2 * n)),
        input_output_aliases={i: i for i in range(2 * n)},
        compiler_params=_SPLIT_COPY,
    )(*srcs, *lands, send_sems, recv_sems, after)
    return list(outs[n:])


def _pass_to_sibling(lands, name):
    n = len(lands)

    def body(*refs):
        ins, outs, (send_sems, recv_sems) = refs[:n], refs[n:2 * n], refs[2 * n:]
        x, y, c = lax.axis_index("x"), lax.axis_index("y"), lax.axis_index("c")
        cps = []
        for w in range(n):
            hr = ins[w].shape[1] // 2
            for j, chip in enumerate([(1 - x, y), (x, 1 - y), (1 - x, 1 - y)]):
                k = 2 * chip[0] + chip[1]
                cps.append(pltpu.make_async_remote_copy(
                    src_ref=ins[w].at[k, pl.ds(c * hr, hr), :], dst_ref=outs[w].at[k, pl.ds(c * hr, hr), :],
                    send_sem=send_sems.at[3 * w + j], recv_sem=recv_sems.at[3 * w + j],
                    device_id=(x, y, 1 - c), device_id_type=MESH))
        for cp in cps:
            cp.start()
        for cp in cps:
            cp.wait()

    return pl.pallas_call(
        body, name=name,
        out_shape=[jax.ShapeDtypeStruct(v.shape, v.dtype) for v in lands],
        in_specs=_hbm_specs(n), out_specs=_hbm_specs(n),
        input_output_aliases={i: i for i in range(n)},
        scratch_shapes=[pltpu.SemaphoreType.DMA((3 * n,)), pltpu.SemaphoreType.DMA((3 * n,))],
    )(*lands)


def _row_tile(rows):
    for cand in (256, 176, 128, 64, 32, 16):
        if rows % cand == 0:
            return cand
    raise ValueError(rows)


def _add_my_half(g, other, c_idx, name):
    _, k, n = g.shape
    hr = k // 2
    tr = _row_tile(hr)
    nb = hr // tr

    def body(c_ref, g_ref, o_ref, out_ref, out16_ref):
        s = g_ref[...] + o_ref[...]
        out_ref[...] = s
        out16_ref[...] = s.astype(BF16)

    return pl.pallas_call(
        body, name=name,
        grid_spec=pltpu.PrefetchScalarGridSpec(
            num_scalar_prefetch=1, grid=(N_CHIPS, nb),
            in_specs=[pl.BlockSpec((1, tr, n), lambda j, i, c: (j, c[0] * nb + i, 0)),
                      pl.BlockSpec((1, tr, n), lambda j, i, c: (j, i, 0))],
            out_specs=[pl.BlockSpec((1, tr, n), lambda j, i, c: (j, i, 0)),
                       pl.BlockSpec((1, tr, n), lambda j, i, c: (j, i, 0))]),
        out_shape=[jax.ShapeDtypeStruct((N_CHIPS, hr, n), F32), jax.ShapeDtypeStruct((N_CHIPS, hr, n), BF16)],
        compiler_params=_params(("parallel", "parallel")),
    )(c_idx, g, other)


def _add_chips(red, recv, chip_idx, name):
    _, hr, n = red.shape
    tr = _row_tile(hr)

    def body(k_ref, r_ref, v_ref, out_ref):
        out_ref[...] = ((r_ref[0] + v_ref[0].astype(F32)) + v_ref[1].astype(F32)) + v_ref[2].astype(F32)

    return pl.pallas_call(
        body, name=name,
        grid_spec=pltpu.PrefetchScalarGridSpec(
            num_scalar_prefetch=1, grid=(hr // tr,),
            in_specs=[pl.BlockSpec((1, tr, n), lambda i, k: (k[0], i, 0)),
                      pl.BlockSpec((3, tr, n), lambda i, k: (0, i, 0))],
            out_specs=pl.BlockSpec((tr, n), lambda i, k: (i, 0))),
        out_shape=jax.ShapeDtypeStruct((hr, n), F32),
        compiler_params=_params(("parallel",)),
    )(chip_idx, red, recv)


def _mod_shard(c_all, w_ada, b_ada):
    nb, cols = c_all.shape[0], w_ada.shape[1]

    def body(c_ref, w_ref, b_ref, o_ref):
        c = c_ref[...]
        o_ref[...] = _dot(c * jax.nn.sigmoid(c), w_ref[...]) + b_ref[...]

    return pl.pallas_call(
        body, name="mod_shard", out_shape=jax.ShapeDtypeStruct((nb, cols), F32),
        compiler_params=_params(vmem_mb=48),
    )(c_all, w_ada, b_ada)


def _proj(x2, mod8, w, seq, out_dtype, name):
    t = x2.shape[0]
    n = w.shape[1]
    tm, tn = min(2048, seq), min(1152, n)
    tpb = seq // tm

    def body(x_ref, mod_ref, w_ref, o_ref, h_ref):
        @pl.when(pl.program_id(1) == 0)
        def _():
            h_ref[...] = (x_ref[...] * (1.0 + mod_ref[0, 1:2, :]) + mod_ref[0, 0:1, :]).astype(BF16)
        o_ref[...] = jnp.dot(h_ref[...], w_ref[...], preferred_element_type=F32).astype(o_ref.dtype)

    return pl.pallas_call(
        body, name=name, grid=(t // tm, n // tn),
        in_specs=[pl.BlockSpec((tm, D), lambda i, j: (i, 0)),
                  pl.BlockSpec((1, 8, D), lambda i, j: (i // tpb, 0, 0)),
                  pl.BlockSpec((D, tn), lambda i, j: (0, j))],
        out_specs=[pl.BlockSpec((tm, tn), lambda i, j: (i, j)), pl.BlockSpec((tm, D), lambda i, j: (i, 0))],
        out_shape=[jax.ShapeDtypeStruct((t, n), out_dtype), jax.ShapeDtypeStruct((t, D), BF16)],
        compiler_params=_params(("parallel", "arbitrary"), 56),
    )(x2, mod8, w)


def _rows_matmul(a, w, name):
    t, k = a.shape
    n = w.shape[1]
    tm = 1024 if t % 1024 == 0 else t

    def body(a_ref, w_ref, o_ref):
        o_ref[...] = jnp.dot(a_ref[...], w_ref[...], preferred_element_type=F32)

    return pl.pallas_call(
        body, name=name, grid=(t // tm,),
        in_specs=[pl.BlockSpec((tm, k), lambda i: (i, 0)), pl.BlockSpec((k, n), lambda i: (0, 0))],
        out_specs=pl.BlockSpec((tm, n), lambda i: (i, 0)),
        out_shape=jax.ShapeDtypeStruct((t, n), F32),
        compiler_params=_params(("parallel",)),
    )(a, w)


def _tn_matmul(a, b, name, seq, split=None):
    a_st, b_st = a.ndim == 3, b.ndim == 3
    t, ka = a.shape[-2:]
    n = b.shape[-1]
    tt = min(2048, seq)
    nt = t // tt
    if a_st or b_st:
        steps, tn = (a.shape[0] if a_st else b.shape[0]), n
    else:
        tn = split
        if tn is None:
            tn = next(cand for cand in (1920, 1024, 1408, 512, n) if n % cand == 0)
        steps = n // tn
    stacked_out = a_st or b_st or split is not None

    def body(a_ref, b_ref, o_ref):
        part = _dot_tn(a_ref[0] if a_st else a_ref[...], b_ref[0] if b_st else b_ref[...])
        if stacked_out:
            part = part[None]

        @pl.when(pl.program_id(1) == 0)
        def _():
            o_ref[...] = part

        @pl.when(pl.program_id(1) > 0)
        def _():
            o_ref[...] += part

    if a_st:
        in_specs = [pl.BlockSpec((1, tt, ka), lambda j, k: (j, k, 0))]
    else:
        in_specs = [pl.BlockSpec((tt, ka), lambda j, k: (k, 0))]
    if b_st:
        in_specs.append(pl.BlockSpec((1, tt, n), lambda j, k: (j, k, 0)))
    else:
        in_specs.append(pl.BlockSpec((tt, tn), lambda j, k: (k, 0 if a_st else j)))
    if stacked_out:
        out_spec = pl.BlockSpec((1, ka, tn), lambda j, k: (j, 0, 0))
        out_shape = jax.ShapeDtypeStruct((steps, ka, tn), F32)
    else:
        out_spec = pl.BlockSpec((ka, tn), lambda j, k: (0, j))
        out_shape = jax.ShapeDtypeStruct((ka, n), F32)
    return pl.pallas_call(
        body, name=name, grid=(steps, nt), in_specs=in_specs, out_specs=out_spec, out_shape=out_shape,
        compiler_params=_params(("parallel", "arbitrary"), 56),
    )(a, b)


def _dh_kernel(dproj, w_p, x2, dxp, mod8, seq):
    t = x2.shape[0]
    tm, tk = min(1024, seq), 1920
    tpb = seq // tm
    nk = NP // tk
    nbatch = t // seq

    def body(dp_ref, w_ref, x_ref, dxp_ref, mod_ref, gx_ref, dm_ref, acc):
        i, k = pl.program_id(0), pl.program_id(1)

        @pl.when(k == 0)
        def _():
            acc[...] = jnp.zeros_like(acc)

        acc[...] += _dot_nt(dp_ref[...], w_ref[...])

        @pl.when(k == nk - 1)
        def _():
            dh = acc[...]
            gx_ref[...] = dxp_ref[...].astype(F32) + dh * (1.0 + mod_ref[0, 1:2, :])
            upd = jnp.concatenate(
                [jnp.sum(dh, axis=0, keepdims=True), jnp.sum(dh * x_ref[...], axis=0, keepdims=True),
                 jnp.zeros((6, D), F32)], axis=0)

            @pl.when(i % tpb == 0)
            def _():
                dm_ref[0] = upd

            @pl.when(i % tpb != 0)
            def _():
                dm_ref[0] += upd

    return pl.pallas_call(
        body, name="dh", grid=(t // tm, nk),
        in_specs=[pl.BlockSpec((tm, tk), lambda i, k: (i, k)),
                  pl.BlockSpec((D, tk), lambda i, k: (0, k)),
                  pl.BlockSpec((tm, D), lambda i, k: (i, 0)),
                  pl.BlockSpec((tm, D), lambda i, k: (i, 0)),
                  pl.BlockSpec((1, 8, D), lambda i, k: (i // tpb, 0, 0))],
        out_specs=[pl.BlockSpec((tm, D), lambda i, k: (i, 0)),
                   pl.BlockSpec((1, 8, D), lambda i, k: (i // tpb, 0, 0))],
        out_shape=[jax.ShapeDtypeStruct((t, D), F32), jax.ShapeDtypeStruct((nbatch, 8, D), F32)],
        scratch_shapes=[pltpu.VMEM((tm, D), F32)],
        compiler_params=_params(("arbitrary", "arbitrary"), 56),
    )(dproj, w_p, x2, dxp, mod8)


def _tri(n, upper):
    r = lax.broadcasted_iota(jnp.int32, (n, n), 0)
    c = lax.broadcasted_iota(jnp.int32, (n, n), 1)
    return jnp.where((c >= r) if upper else (c <= r), 1.0, 0.0).astype(F32)


@jax.custom_vjp
def _mm_nn(a, b):
    return _dot(a, b)


_mm_nn.defvjp(lambda a, b: (_dot(a, b), (a, b)),
              lambda res, g: (_dot_nt(g, res[1]), _dot_tn(res[0], g)))


@jax.custom_vjp
def _mm_nt(a, b):
    return _dot_nt(a, b)


_mm_nt.defvjp(lambda a, b: (_dot_nt(a, b), (a, b)),
              lambda res, g: (_dot(g, res[1]), _dot_tn(g, res[0])))


@jax.custom_vjp
def _mm_tn(a, b):
    return _dot_tn(a, b)


_mm_tn.defvjp(lambda a, b: (_dot_tn(a, b), (a, b)),
              lambda res, g: (_dot_nt(res[1], g), _dot(res[0], g)))


@jax.custom_vjp
def _cumsum_rows(x):
    return _dot_f32(_tri(x.shape[0], False), x)


_cumsum_rows.defvjp(lambda x: (_cumsum_rows(x), None),
                    lambda _, g: (_dot_f32(_tri(g.shape[0], True), g),))


@functools.partial(jax.custom_vjp, nondiff_argnums=(1,))
def _shift_rows(x, k):
    return pltpu.roll(x, k % x.shape[0], 0)


_shift_rows.defvjp(lambda x, k: (_shift_rows(x, k), None),
                   lambda k, _, g: (pltpu.roll(g, (-k) % g.shape[0], 0),))


def _group_ref(bc, m):
    n = bc.shape[0] // (2 * m)
    b3 = bc.reshape(n, 2 * m, ADH)
    row = lax.broadcasted_iota(jnp.int32, b3.shape, 1)
    ref = jnp.sum(jnp.where(row == m - 1, b3, 0.0), axis=1, keepdims=True)
    return jnp.broadcast_to(ref, b3.shape).reshape(bc.shape)


def _hgrn_block(q, fl, v, g, st, lb, nw):
    n = q.shape[0]
    f = lb + (1.0 - lb) * jax.nn.sigmoid(fl)
    kk = 1.0 - f
    lf = jnp.log(f)
    bc = _cumsum_rows(lf)
    row = lax.broadcasted_iota(jnp.int32, (n, ADH), 0)
    same = jnp.bitwise_xor(lax.broadcasted_iota(jnp.int32, (n, n), 0), lax.broadcasted_iota(jnp.int32, (n, n), 1))
    a = jnp.zeros((n, n), F32)
    m = 1
    while m < n:
        r = jnp.bitwise_and(row, 2 * m - 1)
        up, lo = r >= m, r < m
        if m == 1:
            aq, ak = lf, jnp.zeros_like(lf)
        elif m == 2:
            aq = jnp.where(r == 3, lf + _shift_rows(lf, 1), lf)
            ak = jnp.where(r == 0, _shift_rows(lf, -1), 0.0)
        else:
            ref = _group_ref(bc, m)
            aq, ak = bc - ref, ref - bc
        qt = jnp.where(up, q * jnp.exp(jnp.where(up, aq, 0.0)), 0.0)
        kt = jnp.where(lo, kk * jnp.exp(jnp.where(lo, ak, 0.0)), 0.0)
        a = a + jnp.where(same < 2 * m, _mm_nt(qt, kt), 0.0)
        m *= 2
    last = row == n - 1
    bl = jnp.sum(jnp.where(last, bc, 0.0), axis=0, keepdims=True)
    o = _mm_nn(a, v) + _mm_nt(q * jnp.exp(bc), st) + jnp.sum(q * kk, axis=-1, keepdims=True) * v
    st_new = st * jnp.exp(bl) + _mm_tn(v, kk * jnp.exp(bl - bc))
    rms = lax.rsqrt(jnp.mean(o * o, axis=-1, keepdims=True) + RMS_EPS)
    return o * rms * nw * jax.nn.sigmoid(g), st_new


def _hgrn_fwd(proj, lb_logits, norm_w, nbatch, seq):
    t = proj.shape[0]
    blk = min(HGRN_BLOCK, seq)
    nb = seq // blk

    nh = HGRN_HEADS
    wp, wy = 512 * nh, ADH * nh

    def body(p_ref, lbl_ref, nw_ref, y_ref, ck_ref, st_s):
        @pl.when(pl.program_id(2) == 0)
        def _():
            st_s[...] = jnp.zeros_like(st_s)

        st = [st_s[h] for h in range(nh)]
        p = p_ref[...].astype(F32)
        lb = jax.nn.sigmoid(lbl_ref[0:1, :] - lbl_ref[1:2, :])
        nw = nw_ref[...]
        res = [_hgrn_block(*(p[:, 512 * h + 128 * k:512 * h + 128 * k + 128] for k in range(4)), st[h],
                           lb[:, 128 * h:128 * h + 128], nw[:, 128 * h:128 * h + 128]) for h in range(nh)]
        for h in range(nh):
            ck_ref[0, h] = st[h]
            st_s[h] = res[h][1]
        y_ref[...] = jnp.concatenate([r[0] for r in res], axis=1).astype(y_ref.dtype)

    return pl.pallas_call(
        body, name="hgrn_fwd", grid=(AH // nh, nbatch, nb),
        in_specs=[pl.BlockSpec((blk, wp), lambda h, b, i: (b * nb + i, COL_A // wp + h)),
                  pl.BlockSpec((2, wy), lambda h, b, i: (0, h)),
                  pl.BlockSpec((1, wy), lambda h, b, i: (0, h))],
        out_specs=[pl.BlockSpec((blk, wy), lambda h, b, i: (b * nb + i, h)),
                   pl.BlockSpec((1, nh, 128, 128), lambda h, b, i: ((h * nbatch + b) * nb + i, 0, 0, 0))],
        out_shape=[jax.ShapeDtypeStruct((t, AW), BF16),
                   jax.ShapeDtypeStruct((AH // nh * nbatch * nb, nh, 128, 128), F32)],
        scratch_shapes=[pltpu.VMEM((nh, 128, 128), F32)],
        compiler_params=_params(("parallel", "parallel", "arbitrary"), 48),
    )(proj, lb_logits, norm_w)


def _hgrn_bwd(proj, dya, ckpt, lb_logits, norm_w, dproj, nbatch, seq):
    t = proj.shape[0]
    blk = min(HGRN_BLOCK, seq)
    nb = seq // blk

    nh = HGRN_HEADS
    wp, wy = 512 * nh, ADH * nh

    def body(p_ref, dy_ref, ck_ref, lbl_ref, nw_ref, dp_in, dp_ref, sm_ref, dst_s):
        del dp_in
        b_id, i = pl.program_id(1), pl.program_id(2)

        @pl.when(i == 0)
        def _():
            dst_s[...] = jnp.zeros_like(dst_s)

        dst = [dst_s[h] for h in range(nh)]
        st = [ck_ref[0, h] for h in range(nh)]
        p = p_ref[...].astype(F32)
        dy = dy_ref[...]
        lb = jax.nn.sigmoid(lbl_ref[0:1, :] - lbl_ref[1:2, :])
        nw = nw_ref[...]
        grads = []
        for h in range(nh):
            _, pullback = jax.vjp(_hgrn_block, *(p[:, 512 * h + 128 * k:512 * h + 128 * k + 128] for k in range(4)),
                                  st[h], lb[:, 128 * h:128 * h + 128], nw[:, 128 * h:128 * h + 128])
            grads.append(pullback((dy[:, 128 * h:128 * h + 128], dst[h])))
        for h in range(nh):
            dst_s[h] = grads[h][4]
        dp_ref[...] = jnp.concatenate([g[k] for g in grads for k in range(4)], axis=1).astype(dp_ref.dtype)
        upd = jnp.concatenate([jnp.concatenate([g[5] for g in grads], axis=1),
                               jnp.concatenate([g[6] for g in grads], axis=1), jnp.zeros((6, wy), F32)], axis=0)
        first = (b_id == 0) & (i == 0)

        @pl.when(first)
        def _():
            sm_ref[...] = upd

        @pl.when(jnp.logical_not(first))
        def _():
            sm_ref[...] += upd

    def rows(h, b, i):
        return b * nb + (nb - 1 - i)

    return pl.pallas_call(
        body, name="hgrn_bwd", grid=(AH // nh, nbatch, nb),
        in_specs=[pl.BlockSpec((blk, wp), lambda h, b, i: (rows(h, b, i), COL_A // wp + h)),
                  pl.BlockSpec((blk, wy), lambda h, b, i: (rows(h, b, i), h)),
                  pl.BlockSpec((1, nh, 128, 128), lambda h, b, i: ((h * nbatch + b) * nb + (nb - 1 - i), 0, 0, 0)),
                  pl.BlockSpec((2, wy), lambda h, b, i: (0, h)),
                  pl.BlockSpec((1, wy), lambda h, b, i: (0, h)),
                  pl.BlockSpec(memory_space=pl.ANY)],
        out_specs=[pl.BlockSpec((blk, wp), lambda h, b, i: (rows(h, b, i), COL_A // wp + h)),
                   pl.BlockSpec((8, wy), lambda h, b, i: (0, h))],
        out_shape=[jax.ShapeDtypeStruct((t, NP), BF16), jax.ShapeDtypeStruct((8, AW), F32)],
        input_output_aliases={5: 0},
        scratch_shapes=[pltpu.VMEM((nh, 128, 128), F32)],
        compiler_params=_params(("parallel", "arbitrary", "arbitrary"), 56),
    )(proj, dya, ckpt, lb_logits, norm_w, dproj)


def _log_sigmoid(z):
    return jnp.minimum(z, 0.0) - jnp.log(1.0 + jnp.exp(-jnp.abs(z)))


def _fox_cum(proj, bias128, nbatch, seq):
    t = proj.shape[0]
    ts = min(512, seq)
    nb = seq // ts

    def body(p_ref, b_ref, c_ref, carry):
        @pl.when(pl.program_id(1) == 0)
        def _():
            carry[...] = jnp.zeros_like(carry)
        cum = _dot_f32(_tri(ts, False), _log_sigmoid(p_ref[...] + b_ref[...])) + carry[...]
        carry[...] = cum[ts - 1:ts, :]
        cum2 = cum * LOG2E
        lane = lax.broadcasted_iota(jnp.int32, (ts, 128), 1)
        for p in range(4):
            c_ref[p] = jnp.where(lane < 64, cum2[:, 2 * p:2 * p + 1], cum2[:, 2 * p + 1:2 * p + 2])

    return pl.pallas_call(
        body, name="fox_cum", grid=(nbatch, nb),
        in_specs=[pl.BlockSpec((ts, 128), lambda b, i: (b * nb + i, 0)),
                  pl.BlockSpec((1, 128), lambda b, i: (0, 0))],
        out_specs=pl.BlockSpec((4, ts, 128), lambda b, i: (0, b * nb + i, 0)),
        out_shape=jax.ShapeDtypeStruct((4, t, 128), F32),
        scratch_shapes=[pltpu.VMEM((1, 128), F32)],
        compiler_params=_params(("parallel", "arbitrary")),
    )(proj, bias128)


def _fox_scores_t(q128, k128, cc128, hh, masked, q_off=0):
    tq, tk = q128.shape[0], k128.shape[0]
    qh = _head_lanes((q128 * (LOG2E * BDH ** -0.5)).astype(BF16), hh)
    s = _dot_nt(k128, qh) - cc128[:, 64 * hh:64 * hh + 1]
    if masked:
        key = lax.broadcasted_iota(jnp.int32, (tk, tq), 0)
        qry = lax.broadcasted_iota(jnp.int32, (tk, tq), 1) + q_off
        s = jnp.where(key <= qry, s, NEG)
    return s


def _causal_pairs(nq, key_major):
    if key_major:
        pairs = [(i, j) for j in range(nq) for i in range(j, nq)]
    else:
        pairs = [(i, j) for i in range(nq) for j in range(i + 1)]
    return (jnp.asarray([p[0] for p in pairs], jnp.int32), jnp.asarray([p[1] for p in pairs], jnp.int32))


def _head_lanes(x128, hh):
    lane = lax.broadcasted_iota(jnp.int32, x128.shape, 1)
    return jnp.where((lane < 64) if hh == 0 else (lane >= 64), x128, jnp.zeros_like(x128))


def _with_ones_lane(x128, hh):
    lane = lax.broadcasted_iota(jnp.int32, x128.shape, 1)
    one = jnp.ones_like(x128)
    zero = jnp.zeros_like(x128)
    if hh == 0:
        return jnp.where(lane < 64, x128, jnp.where(lane == 64, one, zero))
    return jnp.where(lane >= 64, x128, jnp.where(lane == 0, one, zero))


def _fox_fwd(proj, cum_cols, nbatch, seq):
    t = proj.shape[0]
    tq = tk = min(512, seq)
    nq = seq // tq
    npr = FOX_PAIRS
    qi, kj = _causal_pairs(nq, key_major=False)

    def body(qi_ref, kj_ref, q_ref, kv_ref, cc_ref, o_ref, lse_ref, m_s, acc_s):
        s_id = pl.program_id(2)
        i, j = qi_ref[s_id], kj_ref[s_id]

        @pl.when(j == 0)
        def _():
            m_s[...] = jnp.full_like(m_s, NEG)
            acc_s[...] = jnp.zeros_like(acc_s)

        def step(masked):
            heads = [(pr, hh) for pr in range(npr) for hh in range(2)]
            m_prev = m_s[0:2 * npr, :]
            acc_prev = [acc_s[h] for h in range(2 * npr)]
            q128 = [q_ref[:, 128 * pr:128 * pr + 128] for pr in range(npr)]
            k128 = [kv_ref[:, 256 * pr:256 * pr + 128].astype(BF16) for pr in range(npr)]
            v128 = [kv_ref[:, 256 * pr + 128:256 * pr + 256].astype(BF16) for pr in range(npr)]
            s = [_fox_scores_t(q128[pr], k128[pr], cc_ref[pr], hh, masked) for pr, hh in heads]
            m_new = [jnp.maximum(m_prev[h:h + 1, :], jnp.max(s[h], axis=0, keepdims=True)) for h in range(2 * npr)]
            acc_new = []
            for h, (pr, hh) in enumerate(heads):
                alpha = jnp.exp2(m_prev[h:h + 1, :] - m_new[h])
                p = jnp.exp2(s[h] - m_new[h]).astype(BF16)
                acc_new.append(acc_prev[h] * alpha + _dot_tn(_with_ones_lane(v128[pr], hh), p))
            for h in range(2 * npr):
                acc_s[h] = acc_new[h]
            m_s[0:2 * npr, :] = jnp.concatenate(m_new, axis=0)

        @pl.when(j < i)
        def _():
            step(False)

        @pl.when(j == i)
        def _():
            step(True)
            outs = []
            for pr in range(npr):
                a0, a1 = acc_s[2 * pr], acc_s[2 * pr + 1]
                l0, l1 = a0[64:65, :], a1[0:1, :]
                outs.append(jnp.concatenate([a0[0:64, :] / l0, a1[64:128, :] / l1], axis=0).T)
                lse_ref[0, pr] = jnp.concatenate(
                    [m_s[2 * pr:2 * pr + 1, :] + jnp.log2(l0), m_s[2 * pr + 1:2 * pr + 2, :] + jnp.log2(l1),
                     jnp.zeros((6, tq), F32)], axis=0)
            o_ref[...] = jnp.concatenate(outs, axis=1).astype(o_ref.dtype)

    return pl.pallas_call(
        body, name="fox_fwd",
        grid_spec=pltpu.PrefetchScalarGridSpec(
            num_scalar_prefetch=2, grid=(nbatch, 4 // npr, qi.shape[0]),
            in_specs=[pl.BlockSpec((tq, 128 * npr), lambda b, p, s, qi, kj: (b * nq + qi[s], COL_BQ // (128 * npr) + p)),
                      pl.BlockSpec((tk, 256 * npr), lambda b, p, s, qi, kj: (b * nq + kj[s], COL_KV // (256 * npr) + p)),
                      pl.BlockSpec((npr, tk, 128), lambda b, p, s, qi, kj: (p, b * nq + kj[s], 0))],
            out_specs=[pl.BlockSpec((tq, 128 * npr), lambda b, p, s, qi, kj: (b * nq + qi[s], p)),
                       pl.BlockSpec((1, npr, 8, tq), lambda b, p, s, qi, kj: (b, p, 0, qi[s]))],
            scratch_shapes=[pltpu.VMEM((8, tq), F32), pltpu.VMEM((2 * npr, 128, tq), F32)]),
        out_shape=[jax.ShapeDtypeStruct((t, 512), BF16), jax.ShapeDtypeStruct((nbatch, 4, 8, seq), F32)],
        compiler_params=_params(("parallel", "parallel", "arbitrary"), 56),
    )(qi, kj, proj, proj, cum_cols)


def _fox_bwd(proj, cum_cols, lse, yb, dyb, dproj, nbatch, seq):
    t = proj.shape[0]
    tq = tk = min(512, seq)
    nq = seq // tq
    scale = BDH ** -0.5
    qi, kj = _causal_pairs(nq, key_major=True)
    nsteps = qi.shape[0]

    npr = FOX_PAIRS

    def body(qi_ref, kj_ref, q_ref, kv_ref, cc_ref, lse_ref, o_ref, do_ref, dp_in,
             dkv_ref, dq_ref, drs_ref, dcs_ref, dk_s, dv_s, dqa_s):
        del dp_in
        pg, s_id = pl.program_id(1), pl.program_id(2)
        i, j = qi_ref[s_id], kj_ref[s_id]

        @pl.when(i == j)
        def _():
            dk_s[...] = jnp.zeros_like(dk_s)
            dv_s[...] = jnp.zeros_like(dv_s)

        @pl.when(s_id == 0)
        def _():
            dqa_s[...] = jnp.zeros_like(dqa_s)

        def step(masked):
            dk_prev = [dk_s[h] for h in range(2 * npr)]
            dq_prev = [dqa_s[i, h] for h in range(2 * npr)]
            dv_new = [dv_s[pr] for pr in range(npr)]
            dk_new, dq_new = [], []
            for pr in range(npr):
                lanes = slice(128 * pr, 128 * pr + 128)
                q128 = q_ref[:, lanes]
                qs128 = (q128 * scale).astype(BF16)
                k128 = kv_ref[:, 256 * pr:256 * pr + 128].astype(BF16)
                v128 = kv_ref[:, 256 * pr + 128:256 * pr + 256].astype(BF16)
                do128 = do_ref[:, lanes]
                doo = do128 * o_ref[:, lanes].astype(F32)
                do16 = do128.astype(BF16)
                for hh in range(2):
                    dk_h, dq_parts = dk_prev[2 * pr + hh], []
                    q_aug, k_aug = _with_ones_lane(qs128, hh), _with_ones_lane(k128, hh)
                    for qs in range(0, tq, tq // 2):
                        qr = slice(qs, qs + tq // 2)
                        s = _fox_scores_t(q128[qr], k128, cc_ref[pr], hh, masked, qs)
                        p = jnp.exp2(s - lse_ref[0, pr, hh:hh + 1, qr])
                        dd = lax.dot_general(jnp.ones((8, 128), F32), _head_lanes(doo[qr], hh),
                                             (((1,), (1,)), ((), ())), preferred_element_type=F32,
                                             precision=HIGHEST)[0:1, :]
                        doh = _head_lanes(do16[qr], hh)
                        dp = _dot_nt(v128, doh)
                        ds = (p * (dp - dd)).astype(BF16)
                        dv_new[pr] = dv_new[pr] + _dot(p, doh)
                        dk_h = dk_h + _dot(ds, q_aug[qr])
                        dq_parts.append(_dot_tn(k_aug, ds))
                    dk_new.append(dk_h)
                    dq_new.append(dq_prev[2 * pr + hh] + jnp.concatenate(dq_parts, axis=1))
            for pr in range(npr):
                dv_s[pr] = dv_new[pr]
            for h in range(2 * npr):
                dk_s[h] = dk_new[h]
                dqa_s[i, h] = dq_new[h]

        @pl.when(i == j)
        def _():
            step(True)

        @pl.when(i > j)
        def _():
            step(False)

        def sums_to_lanes(lane, pr, s0, s1):
            hp = npr * pg + pr
            return jnp.where(lane == 2 * hp, s0, jnp.where(lane == 2 * hp + 1, s1, 0.0))

        @pl.when(i == nq - 1)
        def _():
            lane = lax.broadcasted_iota(jnp.int32, (tk, 128), 1)
            for pr in range(npr):
                k0, k1 = dk_s[2 * pr], dk_s[2 * pr + 1]
                dkv_ref[:, 256 * pr:256 * pr + 128] = jnp.where(lane < 64, k0, k1).astype(dkv_ref.dtype)
                dkv_ref[:, 256 * pr + 128:256 * pr + 256] = dv_s[pr].astype(dkv_ref.dtype)
                dcs_ref[pr] = sums_to_lanes(lane, pr, k0[:, 64:65], k1[:, 0:1])

        @pl.when(s_id == nsteps - 1)
        def _():
            lane = lax.broadcasted_iota(jnp.int32, (tq, 128), 1)
            for blk in range(nq):
                rows = pl.ds(blk * tq, tq)
                for pr in range(npr):
                    a0 = dqa_s[blk, 2 * pr].T
                    a1 = dqa_s[blk, 2 * pr + 1].T
                    dq_ref[rows, 128 * pr:128 * pr + 128] = (jnp.where(lane < 64, a0, a1) * scale).astype(dq_ref.dtype)
                    drs_ref[pr, rows, :] = sums_to_lanes(lane, pr, a0[:, 64:65], a1[:, 0:1])

    return pl.pallas_call(
        body, name="fox_bwd",
        grid_spec=pltpu.PrefetchScalarGridSpec(
            num_scalar_prefetch=2, grid=(nbatch, 4 // npr, nsteps),
            in_specs=[pl.BlockSpec((tq, 128 * npr), lambda b, p, s, qi, kj: (b * nq + qi[s], COL_BQ // (128 * npr) + p)),
                      pl.BlockSpec((tk, 256 * npr), lambda b, p, s, qi, kj: (b * nq + kj[s], COL_KV // (256 * npr) + p)),
                      pl.BlockSpec((npr, tk, 128), lambda b, p, s, qi, kj: (p, b * nq + kj[s], 0)),
                      pl.BlockSpec((1, npr, 8, tq), lambda b, p, s, qi, kj: (b, p, 0, qi[s])),
                      pl.BlockSpec((tq, 128 * npr), lambda b, p, s, qi, kj: (b * nq + qi[s], p)),
                      pl.BlockSpec((tq, 128 * npr), lambda b, p, s, qi, kj: (b * nq + qi[s], p)),
                      pl.BlockSpec(memory_space=pl.ANY)],
            out_specs=[pl.BlockSpec((tk, 256 * npr), lambda b, p, s, qi, kj: (b * nq + kj[s], COL_KV // (256 * npr) + p)),
                       pl.BlockSpec((seq, 128 * npr), lambda b, p, s, qi, kj: (b, p)),
                       pl.BlockSpec((npr, seq, 128), lambda b, p, s, qi, kj: (p, b, 0)),
                       pl.BlockSpec((npr, tk, 128), lambda b, p, s, qi, kj: (p, b * nq + kj[s], 0))],
            scratch_shapes=[pltpu.VMEM((2 * npr, tk, 128), F32), pltpu.VMEM((npr, tk, 128), F32),
                            pltpu.VMEM((nq, 2 * npr, 128, tq), F32)]),
        out_shape=[jax.ShapeDtypeStruct((t, NP), BF16), jax.ShapeDtypeStruct((t, 512), BF16),
                   jax.ShapeDtypeStruct((4, t, 128), F32), jax.ShapeDtypeStruct((4, t, 128), F32)],
        input_output_aliases={8: 0},
        compiler_params=_params(("parallel", "parallel", "arbitrary"), 60),
    )(qi, kj, proj, proj, cum_cols, lse, yb, dyb, dproj)


def _place_cols(dproj, src, col):
    t, w = src.shape
    tm = 1024 if t % 1024 == 0 else t

    def body(s_ref, dp_in, o_ref):
        del dp_in
        o_ref[...] = s_ref[...]

    return pl.pallas_call(
        body, name="place_cols", grid=(t // tm,),
        in_specs=[pl.BlockSpec((tm, w), lambda i: (i, 0)), pl.BlockSpec(memory_space=pl.ANY)],
        out_specs=pl.BlockSpec((tm, w), lambda i: (i, col // w)),
        out_shape=jax.ShapeDtypeStruct(dproj.shape, dproj.dtype),
        input_output_aliases={1: 0},
        compiler_params=_params(("parallel",)),
    )(src, dproj)


def _fox_dbf(proj, bias128, drs, dcs, dproj, nbatch, seq):
    t = proj.shape[0]
    ts = min(512, seq)
    nb = seq // ts

    def body(p_ref, b_ref, dr_ref, dc_ref, dp_in, dp_ref, sm_ref, carry):
        del dp_in
        b_id, i = pl.program_id(0), pl.program_id(1)

        @pl.when(i == 0)
        def _():
            carry[...] = jnp.zeros_like(carry)

        dcum = (dr_ref[0] - dc_ref[0]) + (dr_ref[1] - dc_ref[1]) + (dr_ref[2] - dc_ref[2]) + (dr_ref[3] - dc_ref[3])
        rc = _dot_f32(_tri(ts, True), dcum) + carry[...]
        carry[...] = rc[0:1, :]
        z = p_ref[...] + b_ref[...]
        lane = lax.broadcasted_iota(jnp.int32, (ts, 128), 1)
        dz = jnp.where(lane < BH, rc * jax.nn.sigmoid(-z), 0.0)
        dp_ref[...] = dz.astype(dp_ref.dtype)
        upd = jnp.concatenate([jnp.sum(dz, axis=0, keepdims=True), jnp.zeros((7, 128), F32)], axis=0)
        first = (b_id == 0) & (i == 0)

        @pl.when(first)
        def _():
            sm_ref[...] = upd

        @pl.when(jnp.logical_not(first))
        def _():
            sm_ref[...] += upd

    def rows(b, i):
        return b * nb + (nb - 1 - i)

    return pl.pallas_call(
        body, name="fox_dbf", grid=(nbatch, nb),
        in_specs=[pl.BlockSpec((ts, 128), lambda b, i: (rows(b, i), 0)),
                  pl.BlockSpec((1, 128), lambda b, i: (0, 0)),
                  pl.BlockSpec((4, ts, 128), lambda b, i: (0, rows(b, i), 0)),
                  pl.BlockSpec((4, ts, 128), lambda b, i: (0, rows(b, i), 0)),
                  pl.BlockSpec(memory_space=pl.ANY)],
        out_specs=[pl.BlockSpec((ts, 128), lambda b, i: (rows(b, i), COL_BF // 128)),
                   pl.BlockSpec((8, 128), lambda b, i: (0, 0))],
        out_shape=[jax.ShapeDtypeStruct((t, NP), BF16), jax.ShapeDtypeStruct((8, 128), F32)],
        input_output_aliases={4: 0},
        scratch_shapes=[pltpu.VMEM((1, 128), F32)],
        compiler_params=_params(("arbitrary", "arbitrary")),
    )(proj, bias128, drs, dcs, dproj)


def _ln_stats(z):
    mu = jnp.mean(z, axis=-1, keepdims=True)
    zc = z - mu
    rstd = lax.rsqrt(jnp.mean(zc * zc, axis=-1, keepdims=True) + LN_EPS)
    return zc * rstd, rstd


def _ln_bwd(dy, xhat, rstd, w):
    dxh = dy * w
    return rstd * (dxh - jnp.mean(dxh, axis=-1, keepdims=True) - xhat * jnp.mean(dxh * xhat, axis=-1, keepdims=True))


def _merge_fwd(ya, yb, proj, x2, mod8, wba, wbb, wout, ln1w, ln1b, seq):
    t = x2.shape[0]
    tm = min(512, seq)
    tpb = seq // tm

    def body(ya_ref, yb_ref, g_ref, x_ref, mod_ref, wa_ref, wb_ref, wo_ref, lw_ref, lb_ref, mg_ref, u_ref, x1_ref):
        ga = jax.nn.sigmoid(g_ref[:, 0:D].astype(F32))
        gb = jax.nn.sigmoid(g_ref[:, D:2 * D].astype(F32))
        merged = (ga * jnp.dot(ya_ref[...], wa_ref[...], preferred_element_type=F32)
                  + gb * jnp.dot(yb_ref[...], wb_ref[...], preferred_element_type=F32))
        mg = merged.astype(BF16)
        mg_ref[...] = mg
        u = jnp.dot(mg, wo_ref[...], preferred_element_type=F32)
        u_ref[...] = u.astype(u_ref.dtype)
        xhat, _ = _ln_stats(ALPHA * x_ref[...] + (1.0 + mod_ref[0, 2:3, :]) * u)
        x1_ref[...] = xhat * lw_ref[...] + lb_ref[...]

    tok = lambda w: pl.BlockSpec((tm, w), lambda i: (i, 0))
    full = lambda a: pl.BlockSpec(a.shape, lambda i: (0,) * a.ndim)
    return pl.pallas_call(
        body, name="merge_fwd", grid=(t // tm,),
        in_specs=[tok(512), tok(512), pl.BlockSpec((tm, 2048), lambda i: (i, COL_GATES // 2048)), tok(D),
                  pl.BlockSpec((1, 8, D), lambda i: (i // tpb, 0, 0)),
                  full(wba), full(wbb), full(wout), full(ln1w), full(ln1b)],
        out_specs=[tok(D), tok(D), tok(D)],
        out_shape=[jax.ShapeDtypeStruct((t, D), BF16), jax.ShapeDtypeStruct((t, D), BF16),
                   jax.ShapeDtypeStruct((t, D), F32)],
        compiler_params=_params(("parallel",), 48),
    )(ya, yb, proj, x2, mod8, wba, wbb, wout, ln1w, ln1b)


def _merge_bwd(du, ya, yb, proj, wba, wbb, wout, token, seq):
    t = du.shape[0]
    tm = min(512, seq)

    def body(du_ref, ya_ref, yb_ref, g_ref, wa_ref, wb_ref, wo_ref, token_ref,
             dp_ref, dpa_ref, dpb_ref, dya_ref, dyb_ref):
        del token_ref
        ga = jax.nn.sigmoid(g_ref[:, 0:D].astype(F32))
        gb = jax.nn.sigmoid(g_ref[:, D:2 * D].astype(F32))
        dm = _dot_nt(du_ref[...], wo_ref[...])
        pa = jnp.dot(ya_ref[...], wa_ref[...], preferred_element_type=F32)
        pb = jnp.dot(yb_ref[...], wb_ref[...], preferred_element_type=F32)
        dpa = (dm * ga).astype(BF16)
        dpb = (dm * gb).astype(BF16)
        dpa_ref[...] = dpa
        dpb_ref[...] = dpb
        dp_ref[:, 0:D] = (dm * pa * ga * (1.0 - ga)).astype(BF16)
        dp_ref[:, D:2 * D] = (dm * pb * gb * (1.0 - gb)).astype(BF16)
        dya_ref[...] = _dot_nt(dpa, wa_ref[...])
        dyb_ref[...] = _dot_nt(dpb, wb_ref[...])

    tok = lambda w: pl.BlockSpec((tm, w), lambda i: (i, 0))
    full = lambda a: pl.BlockSpec(a.shape, lambda i: (0,) * a.ndim)
    return pl.pallas_call(
        body, name="merge_bwd", grid=(t // tm,),
        in_specs=[tok(D), tok(512), tok(512), pl.BlockSpec((tm, 2048), lambda i: (i, COL_GATES // 2048)),
                  full(wba), full(wbb), full(wout), full(token)],
        out_specs=[pl.BlockSpec((tm, 2048), lambda i: (i, COL_GATES // 2048)), tok(D), tok(D), tok(512), tok(512)],
        out_shape=[jax.ShapeDtypeStruct((t, NP), BF16), jax.ShapeDtypeStruct((t, D), BF16),
                   jax.ShapeDtypeStruct((t, D), BF16), jax.ShapeDtypeStruct((t, 512), F32),
                   jax.ShapeDtypeStruct((t, 512), F32)],
        compiler_params=_params(("parallel",), 48),
    )(du, ya, yb, proj, wba, wbb, wout, token)


def _ffn_fwd(x1, mod8, wg, wu, wd, target, ln2w, ln2b, seq):
    t = x1.shape[0]
    tm = min(FFN_TOKENS, seq)
    nf, tf, _ = wg.shape
    tpb = seq // tm
    nbatch = t // seq

    def body(x_ref, mod_ref, wg_ref, wu_ref, wd_ref, t_ref, lw_ref, lb_ref,
             a_ref, b_ref, h_s, dz_ref, st_ref, dm_ref, acc):
        i, j = pl.program_id(0), pl.program_id(1)

        @pl.when(j == 0)
        def _():
            h_s[...] = (x_ref[...] * (1.0 + mod_ref[0, 4:5, :]) + mod_ref[0, 3:4, :]).astype(BF16)
            acc[...] = jnp.zeros_like(acc)

        a = _dot_nt(h_s[...], wg_ref[0])
        b = _dot_nt(h_s[...], wu_ref[0])
        a_ref[0] = a.astype(BF16)
        b_ref[0] = b.astype(BF16)
        acc[...] += _dot(a * jax.nn.sigmoid(a) * b, wd_ref[0])

        @pl.when(j == nf - 1)
        def _():
            ffn = acc[...]
            xhat, rstd = _ln_stats(ALPHA * x_ref[...] + (1.0 + mod_ref[0, 5:6, :]) * ffn)
            diff = xhat * lw_ref[...] + lb_ref[...] - t_ref[...]
            loss = 0.5 * jnp.sum(jnp.sum(diff * diff, axis=-1, keepdims=True), axis=0, keepdims=True) / D
            dy = diff * (1.0 / D)
            dz = _ln_bwd(dy, xhat, rstd, lw_ref[...])
            dz_ref[...] = dz
            lane = lax.broadcasted_iota(jnp.int32, (1, D), 1)
            upd = jnp.concatenate(
                [jnp.sum(dy * xhat, axis=0, keepdims=True), jnp.sum(dy, axis=0, keepdims=True),
                 jnp.where(lane == 0, loss, 0.0), jnp.zeros((5, D), F32)], axis=0)
            dmu = jnp.concatenate(
                [jnp.zeros((5, D), F32), jnp.sum(dz * ffn, axis=0, keepdims=True), jnp.zeros((2, D), F32)], axis=0)

            @pl.when(i == 0)
            def _():
                st_ref[...] = upd

            @pl.when(i > 0)
            def _():
                st_ref[...] += upd

            @pl.when(i % tpb == 0)
            def _():
                dm_ref[0] = dmu

            @pl.when(i % tpb != 0)
            def _():
                dm_ref[0] += dmu

    row = lambda: pl.BlockSpec((tm, D), lambda i, j: (i, 0))
    vec = lambda: pl.BlockSpec((1, D), lambda i, j: (0, 0))
    return pl.pallas_call(
        body, name="ffn_fwd", grid=(t // tm, nf),
        in_specs=[row(), pl.BlockSpec((1, 8, D), lambda i, j: (i // tpb, 0, 0)),
                  pl.BlockSpec((1, tf, D), lambda i, j: (j, 0, 0)), pl.BlockSpec((1, tf, D), lambda i, j: (j, 0, 0)),
                  pl.BlockSpec((1, tf, D), lambda i, j: (j, 0, 0)), row(), vec(), vec()],
        out_specs=[pl.BlockSpec((1, tm, tf), lambda i, j: (j, i, 0)), pl.BlockSpec((1, tm, tf), lambda i, j: (j, i, 0)),
                   row(), row(), pl.BlockSpec((8, D), lambda i, j: (0, 0)),
                   pl.BlockSpec((1, 8, D), lambda i, j: (i // tpb, 0, 0))],
        out_shape=[jax.ShapeDtypeStruct((nf, t, tf), BF16), jax.ShapeDtypeStruct((nf, t, tf), BF16),
                   jax.ShapeDtypeStruct((t, D), BF16),
                   jax.ShapeDtypeStruct((t, D), F32), jax.ShapeDtypeStruct((8, D), F32),
                   jax.ShapeDtypeStruct((nbatch, 8, D), F32)],
        scratch_shapes=[pltpu.VMEM((tm, D), F32)],
        compiler_params=_params(("arbitrary", "arbitrary"), 60),
    )(x1, mod8, wg, wu, wd, target, ln2w, ln2b)


def _ffn_bwd(dz2, a, b, wg, wu, wd, x1, x2, u, mod8, ln1w, seq):
    t = x1.shape[0]
    tm = min(512, seq)
    nf, tf, _ = wg.shape
    tpb = seq // tm
    nbatch = t // seq

    def body(dz_ref, a_ref, b_ref, wg_ref, wu_ref, wd_ref, x1_ref, x_ref, u_ref, mod_ref, lw_ref,
             da_ref, db_ref, hm_ref, df_ref, du_ref, dxp_ref, st_ref, dm_ref, acc):
        i, j = pl.program_id(0), pl.program_id(1)

        @pl.when(j == 0)
        def _():
            df_ref[...] = ((1.0 + mod_ref[0, 5:6, :]) * dz_ref[...]).astype(BF16)
            acc[...] = jnp.zeros_like(acc)

        dhm = _dot_nt(df_ref[...], wd_ref[0])
        av = a_ref[0].astype(F32)
        bv = b_ref[0].astype(F32)
        sg = jax.nn.sigmoid(av)
        sl = av * sg
        hm_ref[0] = (sl * bv).astype(BF16)
        da = (dhm * bv * (sg * (1.0 + av * (1.0 - sg)))).astype(BF16)
        db = (dhm * sl).astype(BF16)
        da_ref[0] = da
        db_ref[0] = db
        acc[...] += _dot(da, wg_ref[0]) + _dot(db, wu_ref[0])

        @pl.when(j == nf - 1)
        def _():
            dh2 = acc[...]
            x1v = x1_ref[...]
            uv = u_ref[...].astype(F32)
            dx1 = ALPHA * dz_ref[...] + dh2 * (1.0 + mod_ref[0, 4:5, :])
            xhat, rstd = _ln_stats(ALPHA * x_ref[...] + (1.0 + mod_ref[0, 2:3, :]) * uv)
            dz1 = _ln_bwd(dx1, xhat, rstd, lw_ref[...])
            du_ref[...] = ((1.0 + mod_ref[0, 2:3, :]) * dz1).astype(BF16)
            dxp_ref[...] = (ALPHA * dz1).astype(dxp_ref.dtype)
            upd = jnp.concatenate(
                [jnp.sum(dx1 * xhat, axis=0, keepdims=True), jnp.sum(dx1, axis=0, keepdims=True),
                 jnp.zeros((6, D), F32)], axis=0)
            dmu = jnp.concatenate(
                [jnp.zeros((2, D), F32), jnp.sum(dz1 * uv, axis=0, keepdims=True),
                 jnp.sum(dh2, axis=0, keepdims=True), jnp.sum(dh2 * x1v, axis=0, keepdims=True),
                 jnp.zeros((3, D), F32)], axis=0)

            @pl.when(i == 0)
            def _():
                st_ref[...] = upd

            @pl.when(i > 0)
            def _():
                st_ref[...] += upd

            @pl.when(i % tpb == 0)
            def _():
                dm_ref[0] = dmu

            @pl.when(i % tpb != 0)
            def _():
                dm_ref[0] += dmu

    row = lambda: pl.BlockSpec((tm, D), lambda i, j: (i, 0))
    ffb = lambda: pl.BlockSpec((1, tm, tf), lambda i, j: (j, i, 0))
    return pl.pallas_call(
        body, name="ffn_bwd", grid=(t // tm, nf),
        in_specs=[row(), ffb(), ffb(),
                  pl.BlockSpec((1, tf, D), lambda i, j: (j, 0, 0)), pl.BlockSpec((1, tf, D), lambda i, j: (j, 0, 0)),
                  pl.BlockSpec((1, tf, D), lambda i, j: (j, 0, 0)), row(), row(), row(),
                  pl.BlockSpec((1, 8, D), lambda i, j: (i // tpb, 0, 0)), pl.BlockSpec((1, D), lambda i, j: (0, 0))],
        out_specs=[ffb(), ffb(), ffb(), row(), row(), row(), pl.BlockSpec((8, D), lambda i, j: (0, 0)),
                   pl.BlockSpec((1, 8, D), lambda i, j: (i // tpb, 0, 0))],
        out_shape=[jax.ShapeDtypeStruct((nf, t, tf), BF16), jax.ShapeDtypeStruct((nf, t, tf), BF16),
                   jax.ShapeDtypeStruct((nf, t, tf), BF16), jax.ShapeDtypeStruct((t, D), BF16),
                   jax.ShapeDtypeStruct((t, D), BF16), jax.ShapeDtypeStruct((t, D), BF16),
                   jax.ShapeDtypeStruct((8, D), F32), jax.ShapeDtypeStruct((nbatch, 8, D), F32)],
        scratch_shapes=[pltpu.VMEM((tm, D), F32)],
        compiler_params=_params(("arbitrary", "arbitrary"), 60),
    )(dz2, a, b, wg, wu, wd, x1, x2, u, mod8, ln1w)


def _adamw_math(w, g, m, v):
    m = B1 * m + (1.0 - B1) * g
    v = B2 * v + (1.0 - B2) * (g * g)
    m_hat = m / (1.0 - B1 ** STEP)
    v_hat = v / (1.0 - B2 ** STEP)
    return -LR * (m_hat / (jnp.sqrt(v_hat) + EPS) + WD * w), m, v


def _adamw(w, g, m, v, name):
    rows, cols = w.shape
    tr = rows
    for cand in (128, 64, 32, 16, 8):
        if rows % cand == 0:
            tr = cand
            break

    def body(w_ref, g_ref, m_ref, v_ref, d_ref, mo_ref, vo_ref):
        d, mn, vn = _adamw_math(w_ref[...], g_ref[...], m_ref[...], v_ref[...])
        d_ref[...] = d
        mo_ref[...] = mn
        vo_ref[...] = vn

    spec = pl.BlockSpec((tr, cols), lambda i: (i, 0))
    return pl.pallas_call(
        body, name=name, grid=(rows // tr,), in_specs=[spec] * 4, out_specs=[spec] * 3,
        out_shape=[jax.ShapeDtypeStruct((rows, cols), F32)] * 3,
        compiler_params=_params(("parallel",), 48),
    )(w, g, m, v)


def _adamw_halves(w, g_mine, g_sib, m, v, c_idx, name):
    rows, cols = w.shape
    hr = rows // 2
    tr = next(cand for cand in (128, 88, 64, 32, 16, 8) if hr % cand == 0)
    tph = hr // tr

    def body(c_ref, w_ref, gm_ref, gs_ref, m_ref, v_ref, g_ref, d_ref, mo_ref, vo_ref):
        g = jnp.where(pl.program_id(0) == c_ref[0], gm_ref[...], gs_ref[...])
        d, mn, vn = _adamw_math(w_ref[...], g, m_ref[...], v_ref[...])
        g_ref[...] = g
        d_ref[...] = d
        mo_ref[...] = mn
        vo_ref[...] = vn

    full = pl.BlockSpec((tr, cols), lambda h, i, c: (h * tph + i, 0))
    half = pl.BlockSpec((tr, cols), lambda h, i, c: (i, 0))
    return pl.pallas_call(
        body, name=name,
        grid_spec=pltpu.PrefetchScalarGridSpec(
            num_scalar_prefetch=1, grid=(2, tph), in_specs=[full, half, half, full, full], out_specs=[full] * 4),
        out_shape=[jax.ShapeDtypeStruct((rows, cols), F32)] * 4,
        compiler_params=_params(("parallel", "parallel"), 48),
    )(c_idx, w, g_mine, g_sib, m, v)


def _grad_w_ada(c_all, dmod_cols):
    def body(c_ref, d_ref, o_ref):
        c = c_ref[...]
        o_ref[...] = lax.dot_general(c * jax.nn.sigmoid(c), d_ref[...], (((0,), (0,)), ((), ())),
                                     preferred_element_type=F32, precision=HIGHEST)

    return pl.pallas_call(
        body, name="grad_w_ada", out_shape=jax.ShapeDtypeStruct((D, dmod_cols.shape[1]), F32),
        compiler_params=_params(vmem_mb=48),
    )(c_all, dmod_cols)


def _small_update(gath, w8, m8, v8):
    def body(g_ref, w_ref, m_ref, v_ref, go_ref, d_ref, mo_ref, vo_ref):
        g0 = g_ref[0, 0:1, :] + g_ref[0, 1:2, :]
        g1 = g_ref[0, 2:3, :]
        for dev in range(1, N_DEV):
            g0 = g0 + (g_ref[dev, 0:1, :] + g_ref[dev, 1:2, :])
            g1 = g1 + g_ref[dev, 2:3, :]
        w = w_ref[...]
        lb = jax.nn.sigmoid(w[1:2, O_LB0:O_LB1] - w[1:2, O_LB1:O_FOX])
        fac = lb * (1.0 - lb)
        g1 = jnp.concatenate([g1[:, :O_LB0], g1[:, O_LB0:O_LB1] * fac, -g1[:, O_LB1:O_FOX] * fac, g1[:, O_FOX:]],
                             axis=1)
        g = jnp.concatenate([g0, g1, jnp.zeros((6, SMALL_W), F32)], axis=0)
        d, mn, vn = _adamw_math(w, g, m_ref[...], v_ref[...])
        go_ref[...] = g
        d_ref[...] = d
        mo_ref[...] = mn
        vo_ref[...] = vn

    return pl.pallas_call(
        body, name="small_update", out_shape=[jax.ShapeDtypeStruct((8, SMALL_W), F32)] * 4,
        compiler_params=_params(vmem_mb=48),
    )(gath, w8, m8, v8)


def _pack_small(b_ada, ln1w, ln1b, ln2w, ln2b, norm_w, lb_logits, fox):
    row1 = jnp.concatenate([ln1w, ln1b, ln2w, ln2b, norm_w, lb_logits[0:1], lb_logits[1:2], fox,
                            jnp.zeros((1, SMALL_W - O_FOX - BH), F32)], axis=1)
    return jnp.concatenate([b_ada, row1, jnp.zeros((6, SMALL_W), F32)], axis=0)


def _unpack_small(p):
    r = p[1:2]
    lb = jnp.concatenate([r[:, O_LB0:O_LB1], r[:, O_LB1:O_FOX]], axis=0)
    return dict(b_ada=p[0:1], ln1_w=r[:, O_LN1W:O_LN1B], ln1_b=r[:, O_LN1B:O_LN2W], ln2_w=r[:, O_LN2W:O_LN2B],
                ln2_b=r[:, O_LN2B:O_NORM], hgrn_norm_w=r[:, O_NORM:O_LB0], lb_logits=lb,
                fox_f_bias=r[:, O_FOX:O_FOX + BH])


_BIG = ("w_in", "w_branch_a", "w_branch_b", "w_out", "w_ffn_gate", "w_ffn_up", "w_ffn_down")
_TRANSPOSED = ("w_ffn_gate", "w_ffn_up")


def _cols_of_chips(stacked):
    return jnp.concatenate([stacked[k] for k in range(N_CHIPS)], axis=1)


def kernel(x, c, w_ada, b_ada, w_in, fox_f_bias, lb_logits, hgrn_norm_w, w_branch_a, w_branch_b, w_out, ln1_w, ln1_b, w_ffn_gate, w_ffn_up, w_ffn_down, ln2_w, ln2_b, loss_target, m_w_ada, m_b_ada, m_w_in, m_fox_f_bias, m_lb_logits, m_hgrn_norm_w, m_w_branch_a, m_w_branch_b, m_w_out, m_ln1_w, m_ln1_b, m_w_ffn_gate, m_w_ffn_up, m_w_ffn_down, m_ln2_w, m_ln2_b, v_w_ada, v_b_ada, v_w_in, v_fox_f_bias, v_lb_logits, v_hgrn_norm_w, v_w_branch_a, v_w_branch_b, v_w_out, v_ln1_w, v_ln1_b, v_w_ffn_gate, v_w_ffn_up, v_w_ffn_down, v_ln2_w, v_ln2_b):
    nbatch, seq, _ = x.shape
    t = nbatch * seq
    ax, ay, ac = lax.axis_index("x"), lax.axis_index("y"), lax.axis_index("c")
    chip = 2 * ax + ay
    dev = 2 * chip + ac
    chip_arr = jnp.reshape(chip, (1,)).astype(jnp.int32)
    core_arr = jnp.reshape(ac, (1,)).astype(jnp.int32)

    tr = lambda a: jnp.swapaxes(a[0], 0, 1)
    shard_w = dict(w_in=w_in[0], w_branch_a=w_branch_a[0], w_branch_b=w_branch_b[0], w_out=w_out[0],
                   w_ffn_gate=tr(w_ffn_gate), w_ffn_up=tr(w_ffn_up), w_ffn_down=w_ffn_down[0])
    shard_m = dict(w_in=m_w_in[0], w_branch_a=m_w_branch_a[0], w_branch_b=m_w_branch_b[0], w_out=m_w_out[0],
                   w_ffn_gate=tr(m_w_ffn_gate), w_ffn_up=tr(m_w_ffn_up), w_ffn_down=m_w_ffn_down[0])
    shard_v = dict(w_in=v_w_in[0], w_branch_a=v_w_branch_a[0], w_branch_b=v_w_branch_b[0], w_out=v_w_out[0],
                   w_ffn_gate=tr(v_w_ffn_gate), w_ffn_up=tr(v_w_ffn_up), w_ffn_down=v_w_ffn_down[0])

    shard16 = {n: shard_w[n].astype(BF16) for n in _BIG}

    def with_mine(gathered, n):
        return lax.dynamic_update_slice(gathered, shard16[n][None], (chip, 0, 0))

    def gather_start(names, tag):
        return _split_start(_gather_copies, [shard16[n] for n in names],
                            [lax.empty((N_CHIPS,) + shard16[n].shape, BF16) for n in names], "gather_" + tag + "_start")

    def gather_finish(split, after, tag):
        send, recv, src, land, _ = split
        return _pass_to_sibling(_split_wait(_gather_copies, send, recv, src, land, after, "gather_" + tag + "_wait"),
                                "gather_" + tag + "_pass")

    late = _BIG[1:]
    first_split = gather_start(("w_in",), "first")
    late_split = gather_start(late, "late")
    late_token = first_split[4] + late_split[4]

    c8 = jnp.concatenate([c, jnp.zeros((8 - nbatch, D), F32)], axis=0)
    c_all = _allgather8(c8, "gather_c")[:, :nbatch, :].reshape(N_DEV * nbatch, D)
    ncol = w_ada.shape[2]
    b_cols = lax.dynamic_slice_in_dim(b_ada, chip * ncol, ncol, axis=1)
    mod_g = _allgather8(_mod_shard(c_all, w_ada[0], b_cols), "gather_mod")
    mod_all = jnp.concatenate([mod_g[2 * k] for k in range(N_CHIPS)], axis=1)
    mod_mine = lax.dynamic_slice_in_dim(mod_all, dev * nbatch, nbatch, axis=0)
    mod8 = jnp.concatenate([mod_mine.reshape(nbatch, 6, D), jnp.zeros((nbatch, 2, D), F32)], axis=1)
    mod8 = mod8 + late_token[0, 0]
    w_p = _permute_cols(_cols_of_chips(with_mine(gather_finish(first_split, mod8, "first")[0], "w_in")))

    x2 = x.reshape(t, D)
    tgt2 = loss_target.reshape(t, D)
    bias128 = jnp.concatenate([fox_f_bias, jnp.zeros((1, 128 - BH), F32)], axis=1)

    proj, h16 = _proj(x2, mod8, w_p, seq, BF16, "proj")
    projf = _rows_matmul(h16, w_p[:, COL_BF:], "proj_forget")
    ya, ckpt = _hgrn_fwd(proj, lb_logits, hgrn_norm_w, nbatch, seq)
    cum_cols = _fox_cum(projf, bias128, nbatch, seq)
    yb, lse = _fox_fwd(proj, cum_cols, nbatch, seq)
    full = {n: with_mine(g, n) for n, g in zip(late, gather_finish(late_split, yb, "late"))}
    wba, wbb = _cols_of_chips(full["w_branch_a"]), _cols_of_chips(full["w_branch_b"])
    wout = full["w_out"].reshape(D, D)
    wg_t, wu_t, wd = full["w_ffn_gate"], full["w_ffn_up"], full["w_ffn_down"]
    merged, u, x1 = _merge_fwd(ya, yb, proj, x2, mod8, wba, wbb, wout, ln1_w, ln1_b, seq)
    a_pre, b_pre, h2, dz2, st2, dm2 = _ffn_fwd(x1, mod8, wg_t, wu_t, wd, tgt2, ln2_w, ln2_b, seq)
    loss = lax.psum(st2[2, 0], ("x", "y", "c"))

    da, db, hmid, dffn, du, dxp, st1, dm1 = _ffn_bwd(dz2, a_pre, b_pre, wg_t, wu_t, wd, x1, x2, u, mod8, ln1_w, seq)
    g_st = {}
    g_st["w_ffn_down"] = _tn_matmul(hmid, dffn, "dw_ffn_down", seq)
    g_st["w_ffn_gate"] = _tn_matmul(da, h2, "dw_ffn_gate", seq)
    g_st["w_ffn_up"] = _tn_matmul(db, h2, "dw_ffn_up", seq)
    g_st["w_out"] = _tn_matmul(merged, du, "dw_out", seq).reshape(N_CHIPS, D // N_CHIPS, D)

    def sum_over_cores(names, tag):
        g_list = [g_st[n] for n in names]
        return [_add_my_half(g, o, core_arr, "grad_add_halves_" + n)
                for n, g, o in zip(names, g_list, _swap_halves(g_list, "grad_swap_halves_" + tag))]

    early = ("w_ffn_down", "w_ffn_gate", "w_ffn_up", "w_out")
    e_halves = sum_over_cores(early, "early")
    e_send, e_recv, e_src, e_land, e_token = _split_start(
        _scatter_copies, [h16 for _, h16 in e_halves],
        [lax.empty((3,) + h16.shape[1:], BF16) for _, h16 in e_halves], "grad_scatter_early_start")
    dproj, dpa, dpb, dya, dyb = _merge_bwd(du, ya, yb, proj, wba, wbb, wout, e_token, seq)
    g_st["w_branch_a"] = _tn_matmul(ya, dpa, "dw_branch_a", seq, split=D // N_CHIPS)
    g_st["w_branch_b"] = _tn_matmul(yb, dpb, "dw_branch_b", seq, split=D // N_CHIPS)
    dproj, dq, drs, dcs = _fox_bwd(proj, cum_cols, lse, yb, dyb, dproj, nbatch, seq)
    dproj = _place_cols(dproj, dq, COL_BQ)
    dproj, sm_fox = _fox_dbf(projf, bias128, drs, dcs, dproj, nbatch, seq)
    dproj, sm_hgrn = _hgrn_bwd(proj, dya, ckpt, lb_logits, hgrn_norm_w, dproj, nbatch, seq)
    grad_x2, dm0 = _dh_kernel(dproj, w_p, x2, dxp, mod8, seq)
    dw_in = _tn_matmul(h16, dproj, "dw_in", seq)
    g_st["w_in"] = _unpermute_to_chips(dw_in)

    e_recv = _split_wait(_scatter_copies, e_send, e_recv, e_src, e_land, dw_in, "grad_scatter_early_wait")
    rest = ("w_in", "w_branch_a", "w_branch_b")
    r_halves = sum_over_cores(rest, "rest")
    r_send, r_rcv, r_src, r_land, r_token = _split_start(
        _scatter_copies, [h16 for _, h16 in r_halves],
        [lax.empty((3,) + h16.shape[1:], BF16) for _, h16 in r_halves], "grad_scatter_rest_start")

    def finish(names, halves, recv, token, tag):
        g_mine = [_add_chips(h32, r, chip_arr, "grad_add_chips_" + n) for n, (h32, _), r in zip(names, halves, recv)]
        g_sib = _join_halves(g_mine, token, "grad_join_halves_" + tag)
        for n, gm, gs in zip(names, g_mine, g_sib):
            grads[n], deltas[n], new_m[n], new_v[n] = _adamw_halves(
                shard_w[n], gm, gs, shard_m[n], shard_v[n], core_arr, "adamw_" + n)

    grads, deltas, new_m, new_v = {}, {}, {}, {}
    finish(early, e_halves, e_recv, r_token, "early")

    dmod = (dm0 + dm1 + dm2)[:, :6, :].reshape(nbatch, 6 * D)
    row2 = jnp.concatenate([st1[0:1], st1[1:2], st2[0:1], st2[1:2], sm_hgrn[1:2], sm_hgrn[0:1], sm_hgrn[0:1],
                            sm_fox[0:1, :BH], jnp.zeros((1, SMALL_W - O_FOX - BH), F32)], axis=1)
    spack = jnp.concatenate([dmod, row2, jnp.zeros((8 - nbatch - 1, SMALL_W), F32)], axis=0)
    spack = spack + r_token[0, 0]
    gath = _allgather8(spack, "gather_small")
    w8 = _pack_small(b_ada, ln1_w, ln1_b, ln2_w, ln2_b, hgrn_norm_w, lb_logits, fox_f_bias)
    m8 = _pack_small(m_b_ada, m_ln1_w, m_ln1_b, m_ln2_w, m_ln2_b, m_hgrn_norm_w, m_lb_logits, m_fox_f_bias)
    v8 = _pack_small(v_b_ada, v_ln1_w, v_ln1_b, v_ln2_w, v_ln2_b, v_hgrn_norm_w, v_lb_logits, v_fox_f_bias)
    sg, sd, smn, svn = (_unpack_small(p) for p in _small_update(gath, w8, m8, v8))
    dmod_all = gath[:, :nbatch, :].reshape(N_DEV * nbatch, SMALL_W)
    g_ada = _grad_w_ada(c_all, lax.dynamic_slice_in_dim(dmod_all, chip * ncol, ncol, axis=1))

    for group, small in zip((grads, deltas, new_m, new_v), (sg, sd, smn, svn)):
        group.update(small)
    grads["w_ada"] = g_ada
    deltas["w_ada"], new_m["w_ada"], new_v["w_ada"] = _adamw(w_ada[0], g_ada, m_w_ada[0], v_w_ada[0], "adamw_w_ada")
    done = sum(new_v[n][0:8, 0:128] for n in early) + new_v["w_ada"][0:8, 0:128]
    r_recv = _split_wait(_scatter_copies, r_send, r_rcv, r_src, r_land, done, "grad_scatter_rest_wait")
    finish(rest, r_halves, r_recv, late_token, "rest")

    names = ["w_ada", "b_ada", "w_in", "fox_f_bias", "lb_logits", "hgrn_norm_w", "w_branch_a", "w_branch_b", "w_out",
             "ln1_w", "ln1_b", "w_ffn_gate", "w_ffn_up", "w_ffn_down", "ln2_w", "ln2_b"]
    shapes = dict(w_ada=w_ada.shape, b_ada=b_ada.shape, w_in=w_in.shape, fox_f_bias=fox_f_bias.shape,
                  lb_logits=lb_logits.shape, hgrn_norm_w=hgrn_norm_w.shape, w_branch_a=w_branch_a.shape,
                  w_branch_b=w_branch_b.shape, w_out=w_out.shape, ln1_w=ln1_w.shape, ln1_b=ln1_b.shape,
                  w_ffn_gate=w_ffn_gate.shape, w_ffn_up=w_ffn_up.shape, w_ffn_down=w_ffn_down.shape,
                  ln2_w=ln2_w.shape, ln2_b=ln2_b.shape)
    outs = [loss, grad_x2.reshape(x.shape)]
    for group in (grads, deltas, new_m, new_v):
        outs += [(jnp.swapaxes(group[n], 0, 1) if n in _TRANSPOSED else group[n]).reshape(shapes[n]) for n in names]
    return tuple(outs)
```
